```python
import jax, jax.numpy as jnp
from jax import lax
import numpy as np

D_MODEL = 1024
BATCH = 8
SEQ = 2048
DEPTH = 1

MLA_HEADS = 8
Q_LORA_RANK = 384
KV_LORA_RANK = 128
QK_NOPE_DIM = 64
QK_ROPE_DIM = 32
V_HEAD_DIM = 64
QK_HEAD_DIM = QK_NOPE_DIM + QK_ROPE_DIM
MLA_WIDTH = MLA_HEADS * V_HEAD_DIM
Q_BLOCK = 128
ROPE_THETA = 10000.0
SGU_GROUPS = 8
SGU_GROUP_DIM = 64
SGU_WIDTH = SGU_GROUPS * SGU_GROUP_DIM
CHUNK = 128
RMS_EPS = 1e-6
LN_EPS = 1e-5
DN_ALPHA = (2.0 * DEPTH) ** 0.25
DN_BETA = (8.0 * DEPTH) ** -0.25
IN_SPLITS = (Q_LORA_RANK, KV_LORA_RANK, QK_ROPE_DIM, MLA_WIDTH,
             SGU_WIDTH, SGU_WIDTH, SGU_WIDTH, D_MODEL, D_MODEL)
IN_WIDTH = sum(IN_SPLITS)

kernel_name = "hybrid_mla_sgu_gated_deepnorm"


def rms_norm(x, g):
    xf = x.astype(jnp.float32)
    y = xf * lax.rsqrt(jnp.mean(xf * xf, axis=-1, keepdims=True) + RMS_EPS)
    return (y * g.astype(jnp.float32)).astype(x.dtype)


def layer_norm(x, g, b):
    xf = x.astype(jnp.float32)
    mu = jnp.mean(xf, axis=-1, keepdims=True)
    xc = xf - mu
    var = jnp.mean(xc * xc, axis=-1, keepdims=True)
    y = xc * lax.rsqrt(var + LN_EPS) * g.astype(jnp.float32) + b.astype(jnp.float32)
    return y.astype(x.dtype)


def rope_tables(positions):
    inv_freq = ROPE_THETA ** (-jnp.arange(0, QK_ROPE_DIM, 2, dtype=jnp.float32) / QK_ROPE_DIM)
    ang = positions.astype(jnp.float32)[..., None] * inv_freq
    return jnp.cos(ang)[:, :, None, :], jnp.sin(ang)[:, :, None, :]


def apply_rope(x, cos, sin):
    xf = x.astype(jnp.float32)
    half = QK_ROPE_DIM // 2
    x1, x2 = xf[..., :half], xf[..., half:]
    return jnp.concatenate([x1 * cos - x2 * sin, x2 * cos + x1 * sin], axis=-1).astype(x.dtype)


def split_columns(h):
    parts, start = [], 0
    for size in IN_SPLITS:
        parts.append(h[..., start:start + size])
        start += size
    return parts


def mla_attention(c_q, c_kv, k_pe, cos, sin, g_q, w_uq, g_kv, w_ukv):
    q = jnp.einsum('bsr,rhd->bshd', rms_norm(c_q, g_q), w_uq)
    q = jnp.concatenate([q[..., :QK_NOPE_DIM], apply_rope(q[..., QK_NOPE_DIM:], cos, sin)], axis=-1)
    kv = jnp.einsum('bsr,rhd->bshd', rms_norm(c_kv, g_kv), w_ukv)
    k_nope, v = kv[..., :QK_NOPE_DIM], kv[..., QK_NOPE_DIM:]
    k_pe = apply_rope(k_pe[:, :, None, :], cos, sin)
    k = jnp.concatenate([k_nope, jnp.broadcast_to(k_pe, k_nope.shape[:-1] + (QK_ROPE_DIM,))], axis=-1)
    scale = QK_HEAD_DIM ** -0.5
    seq = q.shape[1]
    outs = []
    for start in range(0, seq, Q_BLOCK):
        end = start + Q_BLOCK
        s = jnp.einsum('bqhd,bkhd->bhqk', q[:, start:end], k[:, :end]).astype(jnp.float32) * scale
        causal = jnp.arange(end)[None, :] <= (start + jnp.arange(Q_BLOCK))[:, None]
        p = jax.nn.softmax(jnp.where(causal, s, -jnp.inf), axis=-1)
        outs.append(jnp.einsum('bhqk,bkhd->bqhd', p.astype(v.dtype), v[:, :end]))
    o = jnp.concatenate(outs, axis=1)
    return o.reshape(o.shape[0], seq, MLA_WIDTH)


def spatial_gating(u, v, ln_g, ln_b, w_s, b_s):
    u = jax.nn.gelu(u)
    v = layer_norm(jax.nn.gelu(v), ln_g, ln_b)
    bsz, seq, _ = v.shape
    vc = v.reshape(bsz, seq // CHUNK, CHUNK, SGU_GROUPS, SGU_GROUP_DIM)
    mixed = jnp.einsum('gts,bcsgd->bctgd', jnp.tril(w_s), vc) + b_s.T[:, :, None]
    return u * mixed.reshape(bsz, seq, SGU_WIDTH)


def _fwd_setup_inputs(seed: int = 0) -> dict:
    key = jax.random.key(seed)
    ks = jax.random.split(key, 17)
    nrm = jax.random.normal
    f32 = jnp.float32
    x = nrm(ks[0], (BATCH, SEQ, D_MODEL), f32)
    offset = jax.random.randint(ks[1], (BATCH, 1), 0, 4096, dtype=jnp.int32)
    positions = (offset + jnp.arange(SEQ, dtype=jnp.int32)[None, :]).astype(jnp.int32)
    w_in = nrm(ks[2], (DEPTH, D_MODEL, IN_WIDTH), f32) * D_MODEL ** -0.5
    b_in = 0.02 * nrm(ks[3], (DEPTH, IN_WIDTH), f32)
    g_q = 1.0 + 0.02 * nrm(ks[4], (DEPTH, Q_LORA_RANK), f32)
    w_uq = nrm(ks[5], (DEPTH, Q_LORA_RANK, MLA_HEADS, QK_HEAD_DIM), f32) * Q_LORA_RANK ** -0.5
    g_kv = 1.0 + 0.02 * nrm(ks[6], (DEPTH, KV_LORA_RANK), f32)
    w_ukv = nrm(ks[7], (DEPTH, KV_LORA_RANK, MLA_HEADS, QK_NOPE_DIM + V_HEAD_DIM), f32) * KV_LORA_RANK ** -0.5
    w_oa = nrm(ks[8], (DEPTH, MLA_WIDTH, D_MODEL), f32) * (MLA_WIDTH ** -0.5 * DN_BETA)
    sgu_ln_g = 1.0 + 0.02 * nrm(ks[9], (DEPTH, SGU_WIDTH), f32)
    sgu_ln_b = 0.02 * nrm(ks[10], (DEPTH, SGU_WIDTH), f32)
    w_s = nrm(ks[11], (DEPTH, SGU_GROUPS, CHUNK, CHUNK), f32) * CHUNK ** -0.5
    b_s = 1.0 + 0.02 * nrm(ks[12], (DEPTH, SGU_GROUPS, CHUNK), f32)
    w_ob = nrm(ks[13], (DEPTH, SGU_WIDTH, D_MODEL), f32) * (SGU_WIDTH ** -0.5 * DN_BETA)
    w_out = nrm(ks[14], (DEPTH, D_MODEL, D_MODEL), f32) * (D_MODEL ** -0.5 * DN_BETA)
    ln_g = 1.0 + 0.02 * nrm(ks[15], (DEPTH, D_MODEL), f32)
    ln_b = 0.02 * nrm(ks[16], (DEPTH, D_MODEL), f32)
    return {"x": x, "positions": positions, "w_in": w_in, "b_in": b_in,
            "g_q": g_q, "w_uq": w_uq, "g_kv": g_kv, "w_ukv": w_ukv, "w_oa": w_oa,
            "sgu_ln_g": sgu_ln_g, "sgu_ln_b": sgu_ln_b, "w_s": w_s, "b_s": b_s,
            "w_ob": w_ob, "w_out": w_out, "ln_g": ln_g, "ln_b": ln_b}


def _fwd_reference(x, positions, w_in, b_in, g_q, w_uq, g_kv, w_ukv, w_oa,
              sgu_ln_g, sgu_ln_b, w_s, b_s, w_ob, w_out, ln_g, ln_b):
    cos, sin = rope_tables(positions)
    for l in range(DEPTH):
        h = jnp.einsum('bsd,dn->bsn', x, w_in[l]) + b_in[l]
        c_q, c_kv, k_pe, z_a, u, v, z_b, g_a, g_b = split_columns(h)
        y_a = mla_attention(c_q, c_kv, k_pe, cos, sin, g_q[l], w_uq[l], g_kv[l], w_ukv[l]) * jax.nn.silu(z_a)
        y_b = spatial_gating(u, v, sgu_ln_g[l], sgu_ln_b[l], w_s[l], b_s[l]) * jax.nn.silu(z_b)
        merged = (jax.nn.sigmoid(g_a) * jnp.einsum('bsc,cd->bsd', y_a, w_oa[l])
                  + jax.nn.sigmoid(g_b) * jnp.einsum('bsc,cd->bsd', y_b, w_ob[l]))
        x = layer_norm(DN_ALPHA * x + jnp.einsum('bsd,de->bse', merged, w_out[l]), ln_g[l], ln_b[l])
    return x


import jax as _jax
import jax.numpy as _jnp

TWIN_FORMAT = 'train_step'
FWD_PARAMS = ['x', 'positions', 'w_in', 'b_in', 'g_q', 'w_uq', 'g_kv', 'w_ukv', 'w_oa', 'sgu_ln_g', 'sgu_ln_b', 'w_s', 'b_s', 'w_ob', 'w_out', 'ln_g', 'ln_b']
TWIN_WEIGHTS = ['w_in', 'b_in', 'g_q', 'w_uq', 'g_kv', 'w_ukv', 'w_oa', 'sgu_ln_g', 'sgu_ln_b', 'w_s', 'b_s', 'w_ob', 'w_out', 'ln_g', 'ln_b']
TWIN_DIFF_INPUT = 'x'
TWIN_INPUTS = ['x', 'positions', 'w_in', 'b_in', 'g_q', 'w_uq', 'g_kv', 'w_ukv', 'w_oa', 'sgu_ln_g', 'sgu_ln_b', 'w_s', 'b_s', 'w_ob', 'w_out', 'ln_g', 'ln_b', 'loss_target', 'm_w_in', 'm_b_in', 'm_g_q', 'm_w_uq', 'm_g_kv', 'm_w_ukv', 'm_w_oa', 'm_sgu_ln_g', 'm_sgu_ln_b', 'm_w_s', 'm_b_s', 'm_w_ob', 'm_w_out', 'm_ln_g', 'm_ln_b', 'v_w_in', 'v_b_in', 'v_g_q', 'v_w_uq', 'v_g_kv', 'v_w_ukv', 'v_w_oa', 'v_sgu_ln_g', 'v_sgu_ln_b', 'v_w_s', 'v_b_s', 'v_w_ob', 'v_w_out', 'v_ln_g', 'v_ln_b']
TWIN_OUTPUTS = ['loss', 'grad_x', 'grad_w_in', 'grad_b_in', 'grad_g_q', 'grad_w_uq', 'grad_g_kv', 'grad_w_ukv', 'grad_w_oa', 'grad_sgu_ln_g', 'grad_sgu_ln_b', 'grad_w_s', 'grad_b_s', 'grad_w_ob', 'grad_w_out', 'grad_ln_g', 'grad_ln_b', 'delta_w_in', 'delta_b_in', 'delta_g_q', 'delta_w_uq', 'delta_g_kv', 'delta_w_ukv', 'delta_w_oa', 'delta_sgu_ln_g', 'delta_sgu_ln_b', 'delta_w_s', 'delta_b_s', 'delta_w_ob', 'delta_w_out', 'delta_ln_g', 'delta_ln_b', 'new_m_w_in', 'new_m_b_in', 'new_m_g_q', 'new_m_w_uq', 'new_m_g_kv', 'new_m_w_ukv', 'new_m_w_oa', 'new_m_sgu_ln_g', 'new_m_sgu_ln_b', 'new_m_w_s', 'new_m_b_s', 'new_m_w_ob', 'new_m_w_out', 'new_m_ln_g', 'new_m_ln_b', 'new_v_w_in', 'new_v_b_in', 'new_v_g_q', 'new_v_w_uq', 'new_v_g_kv', 'new_v_w_ukv', 'new_v_w_oa', 'new_v_sgu_ln_g', 'new_v_sgu_ln_b', 'new_v_w_s', 'new_v_b_s', 'new_v_w_ob', 'new_v_w_out', 'new_v_ln_g', 'new_v_ln_b']
TWIN_LEAF_KINDS = {'loss': 'loss', 'grad_x': 'grad_x', 'grad_w_in': 'grad_w', 'grad_b_in': 'grad_w', 'grad_g_q': 'grad_w', 'grad_w_uq': 'grad_w', 'grad_g_kv': 'grad_w', 'grad_w_ukv': 'grad_w', 'grad_w_oa': 'grad_w', 'grad_sgu_ln_g': 'grad_w', 'grad_sgu_ln_b': 'grad_w', 'grad_w_s': 'grad_w', 'grad_b_s': 'grad_w', 'grad_w_ob': 'grad_w', 'grad_w_out': 'grad_w', 'grad_ln_g': 'grad_w', 'grad_ln_b': 'grad_w', 'delta_w_in': 'delta_w', 'delta_b_in': 'delta_w', 'delta_g_q': 'delta_w', 'delta_w_uq': 'delta_w', 'delta_g_kv': 'delta_w', 'delta_w_ukv': 'delta_w', 'delta_w_oa': 'delta_w', 'delta_sgu_ln_g': 'delta_w', 'delta_sgu_ln_b': 'delta_w', 'delta_w_s': 'delta_w', 'delta_b_s': 'delta_w', 'delta_w_ob': 'delta_w', 'delta_w_out': 'delta_w', 'delta_ln_g': 'delta_w', 'delta_ln_b': 'delta_w', 'new_m_w_in': 'new_m', 'new_m_b_in': 'new_m', 'new_m_g_q': 'new_m', 'new_m_w_uq': 'new_m', 'new_m_g_kv': 'new_m', 'new_m_w_ukv': 'new_m', 'new_m_w_oa': 'new_m', 'new_m_sgu_ln_g': 'new_m', 'new_m_sgu_ln_b': 'new_m', 'new_m_w_s': 'new_m', 'new_m_b_s': 'new_m', 'new_m_w_ob': 'new_m', 'new_m_w_out': 'new_m', 'new_m_ln_g': 'new_m', 'new_m_ln_b': 'new_m', 'new_v_w_in': 'new_v', 'new_v_b_in': 'new_v', 'new_v_g_q': 'new_v', 'new_v_w_uq': 'new_v', 'new_v_g_kv': 'new_v', 'new_v_w_ukv': 'new_v', 'new_v_w_oa': 'new_v', 'new_v_sgu_ln_g': 'new_v', 'new_v_sgu_ln_b': 'new_v', 'new_v_w_s': 'new_v', 'new_v_b_s': 'new_v', 'new_v_w_ob': 'new_v', 'new_v_w_out': 'new_v', 'new_v_ln_g': 'new_v', 'new_v_ln_b': 'new_v'}


def _forward(args):
    return _fwd_reference(*[args[k] for k in FWD_PARAMS])


def _output_shape():
    out = _jax.eval_shape(lambda: _forward(_fwd_setup_inputs(0)))
    return out.shape, out.dtype

N_MICROBATCH = 1
ADAM_LR = 0.001
ADAM_B1 = 0.9
ADAM_B2 = 0.999
ADAM_EPS = 1e-08
ADAM_WD = 0.01
ADAM_STEP = 10
PER_EXAMPLE_BATCH_AXIS = {'x': 0, 'positions': 0, 'loss_target': 0}
SHARED_INPUTS = []
_WEIGHT_DTYPES = {'w_in': _jnp.float32, 'b_in': _jnp.float32, 'g_q': _jnp.float32, 'w_uq': _jnp.float32, 'g_kv': _jnp.float32, 'w_ukv': _jnp.float32, 'w_oa': _jnp.float32, 'sgu_ln_g': _jnp.float32, 'sgu_ln_b': _jnp.float32, 'w_s': _jnp.float32, 'b_s': _jnp.float32, 'w_ob': _jnp.float32, 'w_out': _jnp.float32, 'ln_g': _jnp.float32, 'ln_b': _jnp.float32}
MOMENT_SCALE = {'w_in': 7.822950e-03, 'b_in': 1.097803e-02, 'g_q': 3.827587e-03, 'w_uq': 2.636676e-03, 'g_kv': 1.006150e-02, 'w_ukv': 3.329439e-03, 'w_oa': 4.625231e-03, 'sgu_ln_g': 8.109886e-03, 'sgu_ln_b': 7.707967e-03, 'w_s': 5.584947e-03, 'b_s': 7.783704e-03, 'w_ob': 1.646882e-02, 'w_out': 1.689010e-02, 'ln_g': 1.599155e+01, 'ln_b': 3.013666e-01}


def _to_microbatches(a, axis):
    t = _jnp.moveaxis(a, axis, 0)
    t = t.reshape((N_MICROBATCH, t.shape[0] // N_MICROBATCH) + t.shape[1:])
    return _jnp.moveaxis(t, 1, axis + 1)


def setup_inputs(seed: int = 0) -> dict:
    inp = _fwd_setup_inputs(seed)
    key = _jax.random.fold_in(_jax.random.key(seed), 7919)
    shape, _ = _output_shape()
    out = dict(inp)
    out["loss_target"] = _jax.random.normal(_jax.random.fold_in(key, 0), shape, _jnp.float32)
    for i, name in enumerate(TWIN_WEIGHTS):
        w = inp[name].astype(_jnp.float32)
        if MOMENT_SCALE is None:
            s = _jnp.sqrt(_jnp.mean(_jnp.square(w)) + 1e-30)
        else:
            s = MOMENT_SCALE[name]
        km, kv = _jax.random.split(_jax.random.fold_in(key, i + 1))
        out[name] = w
        out["m_" + name] = s * _jax.random.normal(km, w.shape, _jnp.float32)
        out["v_" + name] = (s * s) * _jax.random.uniform(kv, w.shape, _jnp.float32, 0.5, 1.5)
    if N_MICROBATCH > 1:
        for name, axis in PER_EXAMPLE_BATCH_AXIS.items():
            out[name] = _to_microbatches(out[name], axis)
    return {'x': out['x'], 'positions': out['positions'], 'w_in': out['w_in'], 'b_in': out['b_in'], 'g_q': out['g_q'], 'w_uq': out['w_uq'], 'g_kv': out['g_kv'], 'w_ukv': out['w_ukv'], 'w_oa': out['w_oa'], 'sgu_ln_g': out['sgu_ln_g'], 'sgu_ln_b': out['sgu_ln_b'], 'w_s': out['w_s'], 'b_s': out['b_s'], 'w_ob': out['w_ob'], 'w_out': out['w_out'], 'ln_g': out['ln_g'], 'ln_b': out['ln_b'], 'loss_target': out['loss_target'], 'm_w_in': out['m_w_in'], 'm_b_in': out['m_b_in'], 'm_g_q': out['m_g_q'], 'm_w_uq': out['m_w_uq'], 'm_g_kv': out['m_g_kv'], 'm_w_ukv': out['m_w_ukv'], 'm_w_oa': out['m_w_oa'], 'm_sgu_ln_g': out['m_sgu_ln_g'], 'm_sgu_ln_b': out['m_sgu_ln_b'], 'm_w_s': out['m_w_s'], 'm_b_s': out['m_b_s'], 'm_w_ob': out['m_w_ob'], 'm_w_out': out['m_w_out'], 'm_ln_g': out['m_ln_g'], 'm_ln_b': out['m_ln_b'], 'v_w_in': out['v_w_in'], 'v_b_in': out['v_b_in'], 'v_g_q': out['v_g_q'], 'v_w_uq': out['v_w_uq'], 'v_g_kv': out['v_g_kv'], 'v_w_ukv': out['v_w_ukv'], 'v_w_oa': out['v_w_oa'], 'v_sgu_ln_g': out['v_sgu_ln_g'], 'v_sgu_ln_b': out['v_sgu_ln_b'], 'v_w_s': out['v_w_s'], 'v_b_s': out['v_b_s'], 'v_w_ob': out['v_w_ob'], 'v_w_out': out['v_w_out'], 'v_ln_g': out['v_ln_g'], 'v_ln_b': out['v_ln_b']}


def _loss(weights, diff, rest, loss_target):
    with _jax.named_scope("forward"):
        args = {**rest, TWIN_DIFF_INPUT: diff, **{k: w.astype(_WEIGHT_DTYPES[k]) for k, w in weights.items()}}
        y = _forward(args)
    with _jax.named_scope("loss_head"):
        err = _jnp.square(y.astype(_jnp.float32) - loss_target)
        return 0.5 * _jnp.sum(_jnp.mean(err, axis=-1)) if err.ndim else 0.5 * err


def _adamw(w, g, m, v):
    m = ADAM_B1 * m + (1.0 - ADAM_B1) * g
    v = ADAM_B2 * v + (1.0 - ADAM_B2) * _jnp.square(g)
    m_hat = m / (1.0 - ADAM_B1 ** ADAM_STEP)
    v_hat = v / (1.0 - ADAM_B2 ** ADAM_STEP)
    delta = -ADAM_LR * (m_hat / (_jnp.sqrt(v_hat) + ADAM_EPS) + ADAM_WD * w)
    return delta, m, v


def reference(x, positions, w_in, b_in, g_q, w_uq, g_kv, w_ukv, w_oa, sgu_ln_g, sgu_ln_b, w_s, b_s, w_ob, w_out, ln_g, ln_b, loss_target, m_w_in, m_b_in, m_g_q, m_w_uq, m_g_kv, m_w_ukv, m_w_oa, m_sgu_ln_g, m_sgu_ln_b, m_w_s, m_b_s, m_w_ob, m_w_out, m_ln_g, m_ln_b, v_w_in, v_b_in, v_g_q, v_w_uq, v_g_kv, v_w_ukv, v_w_oa, v_sgu_ln_g, v_sgu_ln_b, v_w_s, v_b_s, v_w_ob, v_w_out, v_ln_g, v_ln_b):
    given = dict(x=x, positions=positions, w_in=w_in, b_in=b_in, g_q=g_q, w_uq=w_uq, g_kv=g_kv, w_ukv=w_ukv, w_oa=w_oa, sgu_ln_g=sgu_ln_g, sgu_ln_b=sgu_ln_b, w_s=w_s, b_s=b_s, w_ob=w_ob, w_out=w_out, ln_g=ln_g, ln_b=ln_b, loss_target=loss_target, m_w_in=m_w_in, m_b_in=m_b_in, m_g_q=m_g_q, m_w_uq=m_w_uq, m_g_kv=m_g_kv, m_w_ukv=m_w_ukv, m_w_oa=m_w_oa, m_sgu_ln_g=m_sgu_ln_g, m_sgu_ln_b=m_sgu_ln_b, m_w_s=m_w_s, m_b_s=m_b_s, m_w_ob=m_w_ob, m_w_out=m_w_out, m_ln_g=m_ln_g, m_ln_b=m_ln_b, v_w_in=v_w_in, v_b_in=v_b_in, v_g_q=v_g_q, v_w_uq=v_w_uq, v_g_kv=v_g_kv, v_w_ukv=v_w_ukv, v_w_oa=v_w_oa, v_sgu_ln_g=v_sgu_ln_g, v_sgu_ln_b=v_sgu_ln_b, v_w_s=v_w_s, v_b_s=v_b_s, v_w_ob=v_w_ob, v_w_out=v_w_out, v_ln_g=v_ln_g, v_ln_b=v_ln_b)
    weights = {n: given[n] for n in TWIN_WEIGHTS}
    shared = {n: given[n] for n in SHARED_INPUTS}
    per_example = {n: given[n] for n in ['x', 'positions']}
    grad_fn = _jax.value_and_grad(_loss, argnums=(0, 1))

    def one_microbatch(ex, loss_target):
        ex = dict(ex)
        diff = ex.pop(TWIN_DIFF_INPUT)
        return grad_fn(weights, diff, {**shared, **ex}, loss_target)

    if N_MICROBATCH == 1:
        loss, (grad_w, grad_x) = one_microbatch(per_example, given["loss_target"])
    else:
        def body(carry, xs):
            loss_sum, grad_sum = carry
            l_k, (gw_k, gx_k) = one_microbatch(xs[0], xs[1])
            with _jax.named_scope("update"):
                return (loss_sum + l_k, _jax.tree.map(_jnp.add, grad_sum, gw_k)), gx_k

        init = (_jnp.zeros((), _jnp.float32), _jax.tree.map(_jnp.zeros_like, weights))
        (loss, grad_w), grad_x = _jax.lax.scan(body, init, (per_example, given["loss_target"]))
    with _jax.named_scope("update"):
        delta_w, new_m, new_v = {}, {}, {}
        for n in TWIN_WEIGHTS:
            delta_w[n], new_m[n], new_v[n] = _adamw(weights[n], grad_w[n], given["m_" + n], given["v_" + n])
    return (loss, grad_x, *[grad_w[n] for n in TWIN_WEIGHTS], *[delta_w[n] for n in TWIN_WEIGHTS],
            *[new_m[n] for n in TWIN_WEIGHTS], *[new_v[n] for n in TWIN_WEIGHTS])
```

```python
import functools
import math

import jax
import jax.numpy as jnp
from jax import lax
from jax.experimental import pallas as pl
from jax.experimental.pallas import tpu as pltpu

F32 = jnp.float32
BF16 = jnp.bfloat16

D_MODEL = 1024
SEQ = 2048
N_DEV = 8
MLA_HEADS = 8
Q_LORA_RANK = 384
KV_LORA_RANK = 128
QK_NOPE_DIM = 64
QK_ROPE_DIM = 32
V_HEAD_DIM = 64
QK_HEAD_DIM = QK_NOPE_DIM + QK_ROPE_DIM
MLA_WIDTH = MLA_HEADS * V_HEAD_DIM
ROPE_THETA = 10000.0
SGU_GROUPS = 8
SGU_GROUP_DIM = 64
SGU_WIDTH = SGU_GROUPS * SGU_GROUP_DIM
CHUNK = 128
RMS_EPS = 1e-6
LN_EPS = 1e-5
DN_ALPHA = 2.0 ** 0.25
IN_WIDTH = 4640
ATTN_SCALE = QK_HEAD_DIM ** -0.5

ADAM_LR = 0.001
ADAM_B1 = 0.9
ADAM_B2 = 0.999
ADAM_EPS = 1e-08
ADAM_WD = 0.01
ADAM_STEP = 10

LANES = 128
HEAD_PAD = 128
ROPE_LO = QK_NOPE_DIM
ROPE_MID = ROPE_LO + QK_ROPE_DIM // 2
ROPE_HI = ROPE_LO + QK_ROPE_DIM
CQ_PAD = 512

SEG_A = 2560
SEG_B = 1536
SEG_C = 768

PACK_S_ROWS = 7168
PACK_R_ROWS = 272
VMEM_BIG = 56 * 1024 * 1024
VMEM_MID = 40 * 1024 * 1024


def _sigmoid(x):
    return 1.0 / (1.0 + jnp.exp(-x))


def _gelu_and_grad(x):
    c0 = math.sqrt(2.0 / math.pi)
    x2 = x * x
    t = jnp.tanh(c0 * (x + 0.044715 * x * x2))
    g = 0.5 * x * (1.0 + t)
    dg = 0.5 * (1.0 + t) + 0.5 * x * (1.0 - t * t) * (c0 * (1.0 + 3.0 * 0.044715 * x2))
    return g, dg


def _dot(a, b, dims):
    return lax.dot_general(a, b, (dims, ((), ())), preferred_element_type=F32)


_NN = ((1,), (0,))
_NT = ((1,), (1,))
_TN = ((0,), (0,))


def _mm(a, b, *, ta=False, tb=False, bias=None, add=None, colsum=False, tm, tn, tk, name):
    m, k = (a.shape[1], a.shape[0]) if ta else a.shape
    n = b.shape[0] if tb else b.shape[1]
    assert m % tm == 0 and n % tn == 0 and k % tk == 0
    nk = k // tk
    dims = ((0,) if ta else (1,), (1,) if tb else (0,))

    def body(*refs):
        a_ref, b_ref = refs[0], refs[1]
        pos = 2
        bias_ref = add_ref = cs_ref = None
        if bias is not None:
            bias_ref = refs[pos]; pos += 1
        if add is not None:
            add_ref = refs[pos]; pos += 1
        o_ref = refs[pos]; pos += 1
        if colsum:
            cs_ref = refs[pos]; pos += 1
        acc_ref = refs[pos]
        i, kk = pl.program_id(1), pl.program_id(2)

        @pl.when(kk == 0)
        def _():
            acc_ref[...] = jnp.zeros_like(acc_ref)

        bv = b_ref[...]
        acc_ref[...] += _dot(a_ref[...].astype(BF16), bv.astype(BF16), dims)

        if colsum:
            @pl.when(jnp.logical_and(i == 0, kk == 0))
            def _():
                cs_ref[...] = jnp.zeros_like(cs_ref)

            @pl.when(i == 0)
            def _():
                cs_ref[...] += jnp.sum(bv.astype(F32), axis=0, keepdims=True)

        @pl.when(kk == nk - 1)
        def _():
            r = acc_ref[...]
            if bias is not None:
                r = r + bias_ref[...]
            if add is not None:
                r = r + add_ref[...]
            o_ref[...] = r

    a_spec = pl.BlockSpec((tk, tm), lambda j, i, kk: (kk, i)) if ta else pl.BlockSpec((tm, tk), lambda j, i, kk: (i, kk))
    b_spec = pl.BlockSpec((tn, tk), lambda j, i, kk: (j, kk)) if tb else pl.BlockSpec((tk, tn), lambda j, i, kk: (kk, j))
    in_specs, args = [a_spec, b_spec], [a, b]
    if bias is not None:
        in_specs.append(pl.BlockSpec((1, tn), lambda j, i, kk: (0, j))); args.append(bias)
    if add is not None:
        in_specs.append(pl.BlockSpec((tm, tn), lambda j, i, kk: (i, j))); args.append(add)
    out_shape = [jax.ShapeDtypeStruct((m, n), F32)]
    out_specs = [pl.BlockSpec((tm, tn), lambda j, i, kk: (i, j))]
    if colsum:
        out_shape.append(jax.ShapeDtypeStruct((1, n), F32))
        out_specs.append(pl.BlockSpec((1, tn), lambda j, i, kk: (0, j)))
    res = pl.pallas_call(
        body, name=name, grid=(n // tn, m // tm, nk), in_specs=in_specs, out_specs=out_specs, out_shape=out_shape,
        scratch_shapes=[pltpu.VMEM((tm, tn), F32)],
        compiler_params=pltpu.CompilerParams(dimension_semantics=("arbitrary", "arbitrary", "arbitrary"),
                                             vmem_limit_bytes=VMEM_MID),
    )(*args)
    return res if colsum else res[0]


def _rope_tables(pos_col, invf_lane):
    def body(pos_ref, invf_ref, c_ref, sa_ref, sb_ref):
        ang = pos_ref[...].astype(F32) * invf_ref[...]
        cs, sn = jnp.cos(ang), jnp.sin(ang)
        lane = lax.broadcasted_iota(jnp.int32, ang.shape, 1)
        c_ref[...] = jnp.where(lane < ROPE_LO, 1.0, jnp.where(lane < ROPE_HI, cs, 0.0))
        sa_ref[...] = jnp.where(jnp.logical_and(lane >= ROPE_LO, lane < ROPE_MID), -sn, 0.0)
        sb_ref[...] = jnp.where(jnp.logical_and(lane >= ROPE_MID, lane < ROPE_HI), sn, 0.0)

    shp = jax.ShapeDtypeStruct((SEQ, LANES), F32)
    return pl.pallas_call(body, name="rope_tables", out_shape=[shp, shp, shp])(pos_col, invf_lane)


def _rope(x, c, sa, sb):
    return x * c + pltpu.roll(x, LANES - 16, 1) * sa + pltpu.roll(x, 16, 1) * sb


def _rope_t(dy, c, sa, sb):
    return dy * c + pltpu.roll(dy * sa, 16, 1) + pltpu.roll(dy * sb, LANES - 16, 1)


def _mla_prep(h_c, gq, gkv, wq, wkn, wv, c_t, sa_t, sb_t):
    tm = 256

    def body(cq_ref, ckv_ref, kpe_ref, gq_ref, gkv_ref, wq_ref, wkn_ref, wv_ref, c_ref, sa_ref, sb_ref,
             q_ref, k_ref, v_ref):
        c, sa, sb = c_ref[...], sa_ref[...], sb_ref[...]
        cq = cq_ref[...]
        rq = lax.rsqrt(jnp.sum(cq * cq, axis=1, keepdims=True) * (1.0 / Q_LORA_RANK) + RMS_EPS)
        cqn = ((cq * rq) * gq_ref[...]).astype(BF16)
        qall = _dot(cqn, wq_ref[...], _NN)
        for h in range(MLA_HEADS):
            sl = slice(HEAD_PAD * h, HEAD_PAD * (h + 1))
            q_ref[:, sl] = _rope(qall[:, sl], c, sa, sb).astype(BF16)
        ckv = ckv_ref[...]
        rkv = lax.rsqrt(jnp.sum(ckv * ckv, axis=1, keepdims=True) * (1.0 / KV_LORA_RANK) + RMS_EPS)
        ckvn = ((ckv * rkv) * gkv_ref[...]).astype(BF16)
        knall = _dot(ckvn, wkn_ref[...], _NN)
        kper = _rope(kpe_ref[...], c, sa, sb)
        for h in range(MLA_HEADS):
            sl = slice(HEAD_PAD * h, HEAD_PAD * (h + 1))
            k_ref[:, sl] = (knall[:, sl] + kper).astype(BF16)
        v_ref[...] = _dot(ckvn, wv_ref[...], _NN).astype(BF16)

    full = lambda shape: pl.BlockSpec(shape, lambda i: (0, 0))
    tab = pl.BlockSpec((tm, LANES), lambda i: (i, 0))
    return pl.pallas_call(
        body, name="mla_prep", grid=(SEQ // tm,),
        in_specs=[pl.BlockSpec((tm, CQ_PAD), lambda i: (i, 0)),
                  pl.BlockSpec((tm, LANES), lambda i: (i, CQ_PAD // LANES)),
                  pl.BlockSpec((tm, LANES), lambda i: (i, CQ_PAD // LANES + 1)),
                  full((1, CQ_PAD)), full((1, KV_LORA_RANK)),
                  full((CQ_PAD, MLA_HEADS * HEAD_PAD)), full((KV_LORA_RANK, MLA_HEADS * HEAD_PAD)),
                  full((KV_LORA_RANK, MLA_WIDTH)), tab, tab, tab],
        out_specs=[pl.BlockSpec((tm, MLA_HEADS * HEAD_PAD), lambda i: (i, 0)),
                   pl.BlockSpec((tm, MLA_HEADS * HEAD_PAD), lambda i: (i, 0)),
                   pl.BlockSpec((tm, MLA_WIDTH), lambda i: (i, 0))],
        out_shape=[jax.ShapeDtypeStruct((SEQ, MLA_HEADS * HEAD_PAD), BF16),
                   jax.ShapeDtypeStruct((SEQ, MLA_HEADS * HEAD_PAD), BF16),
                   jax.ShapeDtypeStruct((SEQ, MLA_WIDTH), BF16)],
        compiler_params=pltpu.CompilerParams(dimension_semantics=("arbitrary",), vmem_limit_bytes=VMEM_MID),
    )(h_c, h_c, h_c, gq, gkv, wq, wkn, wv, c_t, sa_t, sb_t)


ATT_T = 256


def _attn_fwd(q, k, v):
    t = ATT_T

    def body(q_ref, k_ref, v_ref, o_ref, l_ref):
        qi = pl.program_id(1)
        row = qi * t + lax.broadcasted_iota(jnp.int32, (t, t), 0)
        col0 = lax.broadcasted_iota(jnp.int32, (t, t), 1)
        lane = lax.broadcasted_iota(jnp.int32, (t, LANES), 1)
        res = []
        for a in range(2):
            sl = slice(HEAD_PAD * a, HEAD_PAD * (a + 1))
            qa = q_ref[:, sl]

            def step(j, carry, sl=sl, qa=qa):
                m, l, acc = carry
                off = pl.multiple_of(j * t, t)
                kj = k_ref[pl.ds(off, t), sl]
                vj = v_ref[pl.ds(off, t), :]
                s = _dot(qa, kj, _NT) * ATTN_SCALE
                s = jnp.where(col0 + off <= row, s, -1e30)
                m_new = jnp.maximum(m, jnp.max(s, axis=1, keepdims=True))
                alpha = jnp.exp(m - m_new)
                p = jnp.exp(s - m_new)
                l = alpha * l + jnp.sum(p, axis=1, keepdims=True)
                acc = alpha * acc + _dot(p.astype(BF16), vj, _NN)
                return m_new, l, acc

            m, l, acc = lax.fori_loop(
                0, qi + 1, step,
                (jnp.full((t, 1), -1e30, F32), jnp.zeros((t, 1), F32), jnp.zeros((t, LANES), F32)))
            res.append((acc / l, m + jnp.log(l)))
        o_ref[...] = jnp.where(lane < V_HEAD_DIM, res[0][0], res[1][0])
        l_ref[...] = jnp.where(lane < V_HEAD_DIM, res[0][1], res[1][1])

    return pl.pallas_call(
        body, name="attn_fwd", grid=(MLA_HEADS // 2, SEQ // t),
        in_specs=[pl.BlockSpec((t, 2 * HEAD_PAD), lambda p, i: (i, p)),
                  pl.BlockSpec((SEQ, 2 * HEAD_PAD), lambda p, i: (0, p)),
                  pl.BlockSpec((SEQ, LANES), lambda p, i: (0, p))],
        out_specs=[pl.BlockSpec((t, LANES), lambda p, i: (i, p)),
                   pl.BlockSpec((t, LANES), lambda p, i: (i, p))],
        out_shape=[jax.ShapeDtypeStruct((SEQ, MLA_WIDTH), F32), jax.ShapeDtypeStruct((SEQ, MLA_WIDTH), F32)],
        compiler_params=pltpu.CompilerParams(dimension_semantics=("arbitrary", "arbitrary"), vmem_limit_bytes=VMEM_MID),
    )(q, k, v)


def _attn_bwd(q, k, v, d_o, o, lse):
    t = ATT_T
    nq = SEQ // t

    def body(q_ref, k_ref, v_ref, do_ref, o_ref, l_ref, dq_ref, dk_ref, dv_ref):
        dk_ref[...] = jnp.zeros_like(dk_ref)
        dv_ref[...] = jnp.zeros_like(dv_ref)
        lane = lax.broadcasted_iota(jnp.int32, (t, LANES), 1)
        col0 = lax.broadcasted_iota(jnp.int32, (t, t), 1)
        row0 = lax.broadcasted_iota(jnp.int32, (t, t), 0)

        def outer(i, carry):
            ioff = pl.multiple_of(i * t, t)
            do_i = do_ref[pl.ds(ioff, t), :]
            o_i = o_ref[pl.ds(ioff, t), :]
            l_i = l_ref[pl.ds(ioff, t), :]
            row = row0 + ioff
            for a in range(2):
                sl = slice(HEAD_PAD * a, HEAD_PAD * (a + 1))
                sel = (lane < V_HEAD_DIM) if a == 0 else (lane >= V_HEAD_DIM)
                doa = jnp.where(sel, do_i, 0.0)
                da = jnp.sum(doa * o_i, axis=1, keepdims=True)
                la = l_i[:, V_HEAD_DIM * a:V_HEAD_DIM * a + 1]
                qa = q_ref[pl.ds(ioff, t), sl]
                doa_bf = doa.astype(BF16)

                def inner(j, dq_acc, sl=sl, qa=qa, doa_bf=doa_bf, da=da, la=la):
                    joff = pl.multiple_of(j * t, t)
                    kj = k_ref[pl.ds(joff, t), sl]
                    vj = v_ref[pl.ds(joff, t), :]
                    s = _dot(qa, kj, _NT) * ATTN_SCALE
                    p = jnp.where(col0 + joff <= row, jnp.exp(s - la), 0.0)
                    dp = _dot(doa_bf, vj, _NT)
                    ds_bf = (p * (dp - da) * ATTN_SCALE).astype(BF16)
                    dk_ref[pl.ds(joff, t), sl] += _dot(ds_bf, qa, _TN)
                    dv_ref[pl.ds(joff, t), :] += _dot(p.astype(BF16), doa_bf, _TN)
                    return dq_acc + _dot(ds_bf, kj, _NN)

                dq_ref[pl.ds(ioff, t), sl] = lax.fori_loop(0, i + 1, inner, jnp.zeros((t, HEAD_PAD), F32))
            return carry

        lax.fori_loop(0, nq, outer, 0)

    wide = pl.BlockSpec((SEQ, 2 * HEAD_PAD), lambda p: (0, p))
    narrow = pl.BlockSpec((SEQ, LANES), lambda p: (0, p))
    return pl.pallas_call(
        body, name="attn_bwd", grid=(MLA_HEADS // 2,),
        in_specs=[wide, wide, narrow, narrow, narrow, narrow],
        out_specs=[wide, wide, narrow],
        out_shape=[jax.ShapeDtypeStruct((SEQ, MLA_HEADS * HEAD_PAD), F32),
                   jax.ShapeDtypeStruct((SEQ, MLA_HEADS * HEAD_PAD), F32),
                   jax.ShapeDtypeStruct((SEQ, MLA_WIDTH), F32)],
        compiler_params=pltpu.CompilerParams(dimension_semantics=("arbitrary",), vmem_limit_bytes=VMEM_BIG),
    )(q, k, v, d_o, o, lse)


def _sgu_math(u, v, zb, lg, lb, ws_ref, bias):
    ug, dug = _gelu_and_grad(u)
    vg, dvg = _gelu_and_grad(v)
    mu = jnp.mean(vg, axis=1, keepdims=True)
    xc = vg - mu
    rstd = lax.rsqrt(jnp.mean(xc * xc, axis=1, keepdims=True) + LN_EPS)
    xh = xc * rstd
    vn_bf = (xh * lg + lb).astype(BF16)
    grp = lax.broadcasted_iota(jnp.int32, (CHUNK, SGU_WIDTH), 1) // SGU_GROUP_DIM
    r_i = lax.broadcasted_iota(jnp.int32, (CHUNK, CHUNK), 0)
    c_i = lax.broadcasted_iota(jnp.int32, (CHUNK, CHUNK), 1)
    tri, tri_t = r_i >= c_i, r_i <= c_i
    mixed = bias
    for g in range(SGU_GROUPS):
        wt = jnp.where(tri, ws_ref[g], 0.0).astype(BF16)
        mixed = mixed + jnp.where(grp == g, _dot(wt, vn_bf, _NN), 0.0)
    sb = _sigmoid(zb)
    return ug, dug, dvg, rstd, xh, vn_bf, grp, tri, tri_t, mixed, sb


def _sgu_fwd(h_b, lg, lb, w_s, bias_full):
    def body(u_ref, v_ref, zb_ref, lg_ref, lb_ref, ws_ref, bias_ref, yb_ref):
        zb = zb_ref[...]
        ug, _, _, _, _, _, _, _, _, mixed, sb = _sgu_math(u_ref[...], v_ref[...], zb, lg_ref[...], lb_ref[...],
                                                       ws_ref, bias_ref[...])
        yb_ref[...] = (ug * mixed) * (zb * sb)

    blk = lambda c: pl.BlockSpec((CHUNK, SGU_WIDTH), lambda i, c=c: (i, c))
    full2 = lambda shape: pl.BlockSpec(shape, lambda i: (0, 0))
    return pl.pallas_call(
        body, name="sgu_fwd", grid=(SEQ // CHUNK,),
        in_specs=[blk(0), blk(1), blk(2), full2((1, SGU_WIDTH)), full2((1, SGU_WIDTH)),
                  pl.BlockSpec((SGU_GROUPS, CHUNK, CHUNK), lambda i: (0, 0, 0)), full2((CHUNK, SGU_WIDTH))],
        out_specs=pl.BlockSpec((CHUNK, SGU_WIDTH), lambda i: (i, 0)),
        out_shape=jax.ShapeDtypeStruct((SEQ, SGU_WIDTH), F32),
        compiler_params=pltpu.CompilerParams(dimension_semantics=("arbitrary",)),
    )(h_b, h_b, h_b, lg, lb, w_s, bias_full)


def _sgu_bwd(h_b, d_yb, lg, lb, w_s, w_st, bias_full):
    nsteps = SEQ // CHUNK

    def body(u_ref, v_ref, zb_ref, dyb_ref, lg_ref, lb_ref, ws_ref, wst_ref, bias_ref,
             dhb_ref, dws_ref, dbs_ref, dlg_ref, dlb_ref, dbias_acc):
        step = pl.program_id(0)

        @pl.when(step == 0)
        def _():
            dws_ref[...] = jnp.zeros_like(dws_ref)
            dlg_ref[...] = jnp.zeros_like(dlg_ref)
            dlb_ref[...] = jnp.zeros_like(dlb_ref)
            dbias_acc[...] = jnp.zeros_like(dbias_acc)

        zb = zb_ref[...]
        lg = lg_ref[...]
        ug, dug, dvg, rstd, xh, vn_bf, grp, tri, tri_t, mixed, sb = _sgu_math(
            u_ref[...], v_ref[...], zb, lg, lb_ref[...], ws_ref, bias_ref[...])
        dyb = dyb_ref[...]
        dsgu = dyb * (zb * sb)
        dzb = dyb * (ug * mixed) * (sb * (1.0 + zb * (1.0 - sb)))
        du = dsgu * mixed * dug
        dmixed = dsgu * ug
        dbias_acc[...] += dmixed
        dvn = jnp.zeros((CHUNK, SGU_WIDTH), F32)
        for g in range(SGU_GROUPS):
            dm_g = jnp.where(grp == g, dmixed, 0.0).astype(BF16)
            wtt = jnp.where(tri_t, wst_ref[g], 0.0).astype(BF16)
            dvn = dvn + _dot(wtt, dm_g, _NN)
            dws_ref[g] += jnp.where(tri, _dot(dm_g, vn_bf, _NT), 0.0)
        dlg_ref[...] += jnp.sum(dvn * xh, axis=0, keepdims=True)
        dlb_ref[...] += jnp.sum(dvn, axis=0, keepdims=True)
        dxh = dvn * lg
        dvgel = rstd * (dxh - jnp.mean(dxh, axis=1, keepdims=True) - xh * jnp.mean(dxh * xh, axis=1, keepdims=True))
        dhb_ref[:, 0:SGU_WIDTH] = du
        dhb_ref[:, SGU_WIDTH:2 * SGU_WIDTH] = dvgel * dvg
        dhb_ref[:, 2 * SGU_WIDTH:3 * SGU_WIDTH] = dzb

        @pl.when(step == nsteps - 1)
        def _():
            acc = dbias_acc[...]
            lane = lax.broadcasted_iota(jnp.int32, (CHUNK, LANES), 1)
            out = jnp.zeros((CHUNK, LANES), F32)
            for g in range(SGU_GROUPS):
                sg = jnp.sum(jnp.where(grp == g, acc, 0.0), axis=1, keepdims=True)
                out = jnp.where(lane == g, sg, out)
            dbs_ref[...] = out

    blk = lambda c: pl.BlockSpec((CHUNK, SGU_WIDTH), lambda i, c=c: (i, c))
    full2 = lambda shape: pl.BlockSpec(shape, lambda i: (0, 0))
    full3 = pl.BlockSpec((SGU_GROUPS, CHUNK, CHUNK), lambda i: (0, 0, 0))
    return pl.pallas_call(
        body, name="sgu_bwd", grid=(nsteps,),
        in_specs=[blk(0), blk(1), blk(2), pl.BlockSpec((CHUNK, SGU_WIDTH), lambda i: (i, 0)),
                  full2((1, SGU_WIDTH)), full2((1, SGU_WIDTH)), full3, full3, full2((CHUNK, SGU_WIDTH))],
        out_specs=[pl.BlockSpec((CHUNK, SEG_B), lambda i: (i, 0)), full3, full2((CHUNK, LANES)),
                   full2((1, SGU_WIDTH)), full2((1, SGU_WIDTH))],
        out_shape=[jax.ShapeDtypeStruct((SEQ, SEG_B), F32),
                   jax.ShapeDtypeStruct((SGU_GROUPS, CHUNK, CHUNK), F32),
                   jax.ShapeDtypeStruct((CHUNK, LANES), F32),
                   jax.ShapeDtypeStruct((1, SGU_WIDTH), F32), jax.ShapeDtypeStruct((1, SGU_WIDTH), F32)],
        scratch_shapes=[pltpu.VMEM((CHUNK, SGU_WIDTH), F32)],
        compiler_params=pltpu.CompilerParams(dimension_semantics=("arbitrary",)),
    )(h_b, h_b, h_b, d_yb, lg, lb, w_s, w_st, bias_full)


def _merge(x, o, h_a, y_b, target, w_oa, w_ob, w_out, ln_g, ln_b):
    tm = 128
    nsteps = SEQ // tm

    def body(x_ref, o_ref, ga_ref, gb_ref, za_ref, yb_ref, tgt_ref, woa_ref, wob_ref, wout_ref, lng_ref, lnb_ref,
             loss_ref, dxr_ref, dha_ref, do_ref, dyb_ref, dwoa_ref, dwob_ref, dwout_ref, dlng_ref, dlnb_ref):
        step = pl.program_id(0)

        @pl.when(step == 0)
        def _():
            for r in (loss_ref, dwoa_ref, dwob_ref, dwout_ref, dlng_ref, dlnb_ref):
                r[...] = jnp.zeros_like(r)

        o = o_ref[...]
        za = za_ref[...]
        sa = _sigmoid(za)
        ya_bf = (o * (za * sa)).astype(BF16)
        yb_bf = yb_ref[...].astype(BF16)
        woa, wob, wout = woa_ref[...], wob_ref[...], wout_ref[...]
        pa = _dot(ya_bf, woa, _NN)
        pb = _dot(yb_bf, wob, _NN)
        sga = _sigmoid(ga_ref[...])
        sgb = _sigmoid(gb_ref[...])
        merged_bf = (sga * pa + sgb * pb).astype(BF16)
        r = DN_ALPHA * x_ref[...] + _dot(merged_bf, wout, _NN)
        mu = jnp.mean(r, axis=1, keepdims=True)
        rc = r - mu
        rstd = lax.rsqrt(jnp.mean(rc * rc, axis=1, keepdims=True) + LN_EPS)
        xh = rc * rstd
        lng = lng_ref[...]
        y = xh * lng + lnb_ref[...]
        e = y - tgt_ref[...]
        loss_ref[...] += 0.5 * jnp.sum(jnp.sum(e * e, axis=1, keepdims=True) * (1.0 / D_MODEL), axis=0, keepdims=True)

        dy = e * (1.0 / D_MODEL)
        dlng_ref[...] += jnp.sum(dy * xh, axis=0, keepdims=True)
        dlnb_ref[...] += jnp.sum(dy, axis=0, keepdims=True)
        dxh = dy * lng
        dr = rstd * (dxh - jnp.mean(dxh, axis=1, keepdims=True) - xh * jnp.mean(dxh * xh, axis=1, keepdims=True))
        dxr_ref[...] = DN_ALPHA * dr
        dr_bf = dr.astype(BF16)
        dwout_ref[...] += _dot(merged_bf, dr_bf, _TN)
        dmerged = _dot(dr_bf, wout, _NT)
        dpa_bf = (dmerged * sga).astype(BF16)
        dpb_bf = (dmerged * sgb).astype(BF16)
        dha_ref[:, 0:D_MODEL] = dmerged * pa * (sga * (1.0 - sga))
        dha_ref[:, D_MODEL:2 * D_MODEL] = dmerged * pb * (sgb * (1.0 - sgb))
        dwoa_ref[...] += _dot(ya_bf, dpa_bf, _TN)
        dwob_ref[...] += _dot(yb_bf, dpb_bf, _TN)
        dya = _dot(dpa_bf, woa, _NT)
        dyb_ref[...] = _dot(dpb_bf, wob, _NT)
        do_ref[...] = dya * (za * sa)
        dha_ref[:, 2 * D_MODEL:SEG_A] = dya * o * (sa * (1.0 + za * (1.0 - sa)))

    row = lambda w, c=0: pl.BlockSpec((tm, w), lambda i, c=c: (i, c))
    full = lambda shape: pl.BlockSpec(shape, lambda i: (0, 0))
    return pl.pallas_call(
        body, name="merge", grid=(nsteps,),
        in_specs=[row(D_MODEL), row(MLA_WIDTH), row(D_MODEL, 0), row(D_MODEL, 1), row(MLA_WIDTH, 4), row(SGU_WIDTH),
                  row(D_MODEL), full((MLA_WIDTH, D_MODEL)), full((SGU_WIDTH, D_MODEL)), full((D_MODEL, D_MODEL)),
                  full((1, D_MODEL)), full((1, D_MODEL))],
        out_specs=[full((1, LANES)), row(D_MODEL), row(SEG_A), row(MLA_WIDTH), row(SGU_WIDTH),
                   full((MLA_WIDTH, D_MODEL)), full((SGU_WIDTH, D_MODEL)), full((D_MODEL, D_MODEL)),
                   full((1, D_MODEL)), full((1, D_MODEL))],
        out_shape=[jax.ShapeDtypeStruct((1, LANES), F32),
                   jax.ShapeDtypeStruct((SEQ, D_MODEL), F32), jax.ShapeDtypeStruct((SEQ, SEG_A), F32),
                   jax.ShapeDtypeStruct((SEQ, MLA_WIDTH), F32), jax.ShapeDtypeStruct((SEQ, SGU_WIDTH), F32),
                   jax.ShapeDtypeStruct((MLA_WIDTH, D_MODEL), F32), jax.ShapeDtypeStruct((SGU_WIDTH, D_MODEL), F32),
                   jax.ShapeDtypeStruct((D_MODEL, D_MODEL), F32),
                   jax.ShapeDtypeStruct((1, D_MODEL), F32), jax.ShapeDtypeStruct((1, D_MODEL), F32)],
        compiler_params=pltpu.CompilerParams(dimension_semantics=("arbitrary",), vmem_limit_bytes=VMEM_BIG),
    )(x, o, h_a, h_a, h_a, y_b, target, w_oa, w_ob, w_out, ln_g, ln_b)


def _mla_bwd(dq, dk, dv, h_c, gq, gkv, wq, wkn, wv, c_t, sa_t, sb_t):
    tm = 256
    hw = MLA_HEADS * HEAD_PAD

    def body(dq_ref, dk_ref, dv_ref, cq_ref, ckv_ref, gq_ref, gkv_ref, wq_ref, wkn_ref, wv_ref, c_ref, sa_ref, sb_ref,
             dhc_ref, dwq_ref, dwkn_ref, dwv_ref, dgq_ref, dgkv_ref, pre_ref):
        @pl.when(pl.program_id(0) == 0)
        def _():
            for r in (dwq_ref, dwkn_ref, dwv_ref, dgq_ref, dgkv_ref):
                r[...] = jnp.zeros_like(r)

        c, sa, sb = c_ref[...], sa_ref[...], sb_ref[...]
        lane = lax.broadcasted_iota(jnp.int32, (tm, LANES), 1)
        rope_lanes = jnp.logical_and(lane >= ROPE_LO, lane < ROPE_HI)

        cq = cq_ref[...]
        gq = gq_ref[...]
        rq = lax.rsqrt(jnp.sum(cq * cq, axis=1, keepdims=True) * (1.0 / Q_LORA_RANK) + RMS_EPS)
        nq = cq * rq
        cqn_bf = (nq * gq).astype(BF16)
        for h in range(MLA_HEADS):
            sl = slice(HEAD_PAD * h, HEAD_PAD * (h + 1))
            pre_ref[:, sl] = _rope_t(dq_ref[:, sl], c, sa, sb).astype(BF16)
        dqpre_bf = pre_ref[...]
        dcqn = _dot(dqpre_bf, wq_ref[...], _NT)
        dwq_ref[...] += _dot(cqn_bf, dqpre_bf, _TN)
        dgq_ref[...] += jnp.sum(dcqn * nq, axis=0, keepdims=True)
        dnq = dcqn * gq
        dhc_ref[:, 0:CQ_PAD] = rq * (dnq - nq * (jnp.sum(dnq * nq, axis=1, keepdims=True) * (1.0 / Q_LORA_RANK)))

        ckv = ckv_ref[...]
        gkv = gkv_ref[...]
        rkv = lax.rsqrt(jnp.sum(ckv * ckv, axis=1, keepdims=True) * (1.0 / KV_LORA_RANK) + RMS_EPS)
        nkv = ckv * rkv
        ckvn_bf = (nkv * gkv).astype(BF16)
        dk = dk_ref[...]
        dk_bf = dk.astype(BF16)
        dv_bf = dv_ref[...].astype(BF16)
        dckvn = _dot(dk_bf, wkn_ref[...], _NT) + _dot(dv_bf, wv_ref[...], _NT)
        dwkn_ref[...] += _dot(ckvn_bf, dk_bf, _TN)
        dwv_ref[...] += _dot(ckvn_bf, dv_bf, _TN)
        dgkv_ref[...] += jnp.sum(dckvn * nkv, axis=0, keepdims=True)
        dnkv = dckvn * gkv
        dhc_ref[:, CQ_PAD:CQ_PAD + LANES] = rkv * (
            dnkv - nkv * (jnp.sum(dnkv * nkv, axis=1, keepdims=True) * (1.0 / KV_LORA_RANK)))
        dkpe = jnp.zeros((tm, LANES), F32)
        for h in range(MLA_HEADS):
            dkpe = dkpe + dk[:, HEAD_PAD * h:HEAD_PAD * (h + 1)]
        dhc_ref[:, CQ_PAD + LANES:SEG_C] = _rope_t(jnp.where(rope_lanes, dkpe, 0.0), c, sa, sb)

    full = lambda shape: pl.BlockSpec(shape, lambda i: (0, 0))
    row = lambda w, c=0: pl.BlockSpec((tm, w), lambda i, c=c: (i, c))
    return pl.pallas_call(
        body, name="mla_bwd", grid=(SEQ // tm,),
        in_specs=[row(hw), row(hw), row(MLA_WIDTH), row(CQ_PAD, 0), row(LANES, CQ_PAD // LANES),
                  full((1, CQ_PAD)), full((1, KV_LORA_RANK)), full((CQ_PAD, hw)), full((KV_LORA_RANK, hw)),
                  full((KV_LORA_RANK, MLA_WIDTH)), row(LANES), row(LANES), row(LANES)],
        out_specs=[row(SEG_C), full((CQ_PAD, hw)), full((KV_LORA_RANK, hw)), full((KV_LORA_RANK, MLA_WIDTH)),
                   full((1, CQ_PAD)), full((1, KV_LORA_RANK))],
        out_shape=[jax.ShapeDtypeStruct((SEQ, SEG_C), F32), jax.ShapeDtypeStruct((CQ_PAD, hw), F32),
                   jax.ShapeDtypeStruct((KV_LORA_RANK, hw), F32), jax.ShapeDtypeStruct((KV_LORA_RANK, MLA_WIDTH), F32),
                   jax.ShapeDtypeStruct((1, CQ_PAD), F32), jax.ShapeDtypeStruct((1, KV_LORA_RANK), F32)],
        scratch_shapes=[pltpu.VMEM((tm, hw), BF16)],
        compiler_params=pltpu.CompilerParams(dimension_semantics=("arbitrary",), vmem_limit_bytes=VMEM_MID),
    )(dq, dk, dv, h_c, h_c, gq, gkv, wq, wkn, wv, c_t, sa_t, sb_t)


def _adamw_all(ws, gs, ms, vs):
    n = len(ws)
    c1 = 1.0 / (1.0 - ADAM_B1 ** ADAM_STEP)
    c2 = 1.0 / (1.0 - ADAM_B2 ** ADAM_STEP)

    def body(*refs):
        for idx in range(n):
            w, g, m, v = (refs[idx][...], refs[n + idx][...], refs[2 * n + idx][...], refs[3 * n + idx][...])
            m_new = ADAM_B1 * m + (1.0 - ADAM_B1) * g
            v_new = ADAM_B2 * v + (1.0 - ADAM_B2) * (g * g)
            delta = -ADAM_LR * ((m_new * c1) / (jnp.sqrt(v_new * c2) + ADAM_EPS) + ADAM_WD * w)
            refs[4 * n + idx][...] = delta
            refs[5 * n + idx][...] = m_new
            refs[6 * n + idx][...] = v_new

    shapes = [jax.ShapeDtypeStruct(w.shape, F32) for w in ws]
    outs = pl.pallas_call(
        body, name="adamw", out_shape=shapes * 3,
        compiler_params=pltpu.CompilerParams(vmem_limit_bytes=VMEM_BIG),
    )(*ws, *gs, *ms, *vs)
    return outs[:n], outs[n:2 * n], outs[2 * n:]


def _all_gather(block, name):
    rows, width = block.shape

    def body(x_ref, out_ref, send_sems, recv_sems, local_sem):
        x, y, c = lax.axis_index("x"), lax.axis_index("y"), lax.axis_index("c")
        me, sibling = (x, y, c), (x, y, 1 - c)
        chips = [(1 - x, y), (x, 1 - y), (1 - x, 1 - y)]

        def slab(px, py, pc):
            return out_ref.at[4 * px + 2 * py + pc]

        def copy(k, blk, to, src=None):
            return pltpu.make_async_remote_copy(
                src_ref=slab(*blk) if src is None else src, dst_ref=slab(*blk),
                send_sem=send_sems.at[k], recv_sem=recv_sems.at[k],
                device_id=to, device_id_type=pl.DeviceIdType.MESH)

        mine = pltpu.make_async_copy(x_ref, slab(*me), local_sem)
        mine.start()
        first = [copy(0, me, sibling, src=x_ref)]
        first += [copy(1 + j, me, (*chip, c), src=x_ref) for j, chip in enumerate(chips)]
        for cp in first:
            cp.start()
        passed = [copy(4 + j, (*chip, c), sibling) for j, chip in enumerate(chips)]
        for j, chip in enumerate(chips):
            copy(1 + j, (*chip, c), me).wait_recv()
            passed[j].start()
        copy(0, sibling, me).wait_recv()
        for j, chip in enumerate(chips):
            copy(4 + j, (*chip, 1 - c), me).wait_recv()
        for cp in first + passed:
            cp.wait_send()
        mine.wait()

    return pl.pallas_call(
        body, name=name,
        out_shape=jax.ShapeDtypeStruct((N_DEV, rows, width), block.dtype),
        in_specs=[pl.BlockSpec(memory_space=pltpu.VMEM)],
        out_specs=pl.BlockSpec(memory_space=pltpu.VMEM),
        scratch_shapes=[pltpu.SemaphoreType.DMA((7,)), pltpu.SemaphoreType.DMA((7,)), pltpu.SemaphoreType.DMA],
        compiler_params=pltpu.CompilerParams(vmem_limit_bytes=VMEM_MID),
    )(block)


def _reduce_scatter(parts, chunk, name):
    _, rows, width = parts.shape
    dt = parts.dtype
    assert rows % chunk == 0
    nch = rows // chunk

    def body(p_ref, out_ref, ra_ref, t_ref, rb_ref, send_sems, recv_sems):
        x, y, c = lax.axis_index("x"), lax.axis_index("y"), lax.axis_index("c")
        sibling = (x, y, 1 - c)

        def rcopy(src, dst, k, to):
            return pltpu.make_async_remote_copy(src_ref=src, dst_ref=dst, send_sem=send_sems.at[k],
                                                recv_sem=recv_sems.at[k], device_id=to,
                                                device_id_type=pl.DeviceIdType.MESH)

        stage1 = []
        for chip in range(4):
            cp = rcopy(p_ref.at[2 * chip + (1 - c)], ra_ref.at[chip], chip, sibling)
            cp.start()
            stage1.append(cp)
        for cp in stage1:
            cp.wait_recv()

        def chip_sum(chip, r0):
            sl = pl.ds(r0, chunk)
            return p_ref[2 * chip + c, sl, :].astype(F32) + ra_ref[chip, sl, :].astype(F32)

        others = [(1 - x, y), (x, 1 - y), (1 - x, 1 - y)]
        stage2 = []
        for k, (cx, cy) in enumerate(others):
            def fill(i, carry, k=k, chip=2 * cx + cy):
                r0 = pl.multiple_of(i * chunk, chunk)
                t_ref[k, pl.ds(r0, chunk), :] = chip_sum(chip, r0).astype(dt)
                return carry
            lax.fori_loop(0, nch, fill, 0)
            cp = rcopy(t_ref.at[k], rb_ref.at[k], 4 + k, (cx, cy, c))
            cp.start()
            stage2.append(cp)
        for cp in stage2:
            cp.wait_recv()

        def final(i, carry):
            r0 = pl.multiple_of(i * chunk, chunk)
            sl = pl.ds(r0, chunk)
            acc = chip_sum(2 * x + y, r0)
            for k in range(3):
                acc = acc + rb_ref[k, sl, :].astype(F32)
            out_ref[sl, :] = acc
            return carry
        lax.fori_loop(0, nch, final, 0)
        for cp in stage1 + stage2:
            cp.wait_send()

    return pl.pallas_call(
        body, name=name,
        out_shape=jax.ShapeDtypeStruct((rows, width), F32),
        in_specs=[pl.BlockSpec(memory_space=pltpu.VMEM)],
        out_specs=pl.BlockSpec(memory_space=pltpu.VMEM),
        scratch_shapes=[pltpu.VMEM((4, rows, width), dt), pltpu.VMEM((3, rows, width), dt),
                        pltpu.VMEM((3, rows, width), dt),
                        pltpu.SemaphoreType.DMA((7,)), pltpu.SemaphoreType.DMA((7,))],
        compiler_params=pltpu.CompilerParams(vmem_limit_bytes=VMEM_BIG),
    )(parts)


_O_CQ, _O_CKV, _O_KPE, _O_ZA, _O_U, _O_V, _O_ZB, _O_GA, _O_GB = 0, 384, 512, 544, 1056, 1568, 2080, 2592, 3616


def _to_segments(w):
    z = lambda n: jnp.zeros(w.shape[:-1] + (n,), w.dtype)
    seg_a = jnp.concatenate([w[..., _O_GA:_O_GB], w[..., _O_GB:IN_WIDTH], w[..., _O_ZA:_O_U]], axis=-1)
    seg_b = jnp.concatenate([w[..., _O_U:_O_V], w[..., _O_V:_O_ZB], w[..., _O_ZB:_O_GA]], axis=-1)
    seg_c = jnp.concatenate([w[..., _O_CQ:_O_CKV], z(CQ_PAD - Q_LORA_RANK), w[..., _O_CKV:_O_KPE],
                             z(ROPE_LO), w[..., _O_KPE:_O_ZA], z(LANES - ROPE_HI)], axis=-1)
    return seg_a, seg_b, seg_c


def _from_segments(seg_a, seg_b, seg_c):
    kpe0 = CQ_PAD + LANES + ROPE_LO
    return jnp.concatenate([
        seg_c[..., 0:Q_LORA_RANK], seg_c[..., CQ_PAD:CQ_PAD + LANES], seg_c[..., kpe0:kpe0 + QK_ROPE_DIM],
        seg_a[..., 2 * D_MODEL:SEG_A], seg_b, seg_a[..., 0:2 * D_MODEL]], axis=-1)


def _rows128(a):
    return a.reshape(-1, LANES)


def kernel(x, positions, w_in, b_in, g_q, w_uq, g_kv, w_ukv, w_oa, sgu_ln_g, sgu_ln_b, w_s, b_s, w_ob, w_out, ln_g, ln_b, loss_target, m_w_in, m_b_in, m_g_q, m_w_uq, m_g_kv, m_w_ukv, m_w_oa, m_sgu_ln_g, m_sgu_ln_b, m_w_s, m_b_s, m_w_ob, m_w_out, m_ln_g, m_ln_b, v_w_in, v_b_in, v_g_q, v_w_uq, v_g_kv, v_w_ukv, v_w_oa, v_sgu_ln_g, v_sgu_ln_b, v_w_s, v_b_s, v_w_ob, v_w_out, v_ln_g, v_ln_b):
    x2 = x[0]
    tgt = loss_target[0]

    sh_local = [w_in[0], w_uq[0].reshape(48, MLA_HEADS * QK_HEAD_DIM), w_oa[0], w_ob[0], w_out[0]]
    sh_rows = [a.size // LANES for a in sh_local]
    used = sum(sh_rows)
    pack = jnp.concatenate([_rows128(a.astype(BF16)) for a in sh_local]
                           + [jnp.zeros((PACK_S_ROWS - used, LANES), BF16)], axis=0)
    gathered = _all_gather(pack, "gather_weights")
    offs = [0]
    for r in sh_rows:
        offs.append(offs[-1] + r)
    seg = lambda i: gathered[:, offs[i]:offs[i + 1], :]
    w_in_f = seg(0).reshape(N_DEV, D_MODEL, IN_WIDTH // N_DEV).transpose(1, 0, 2).reshape(D_MODEL, IN_WIDTH)
    w_uq_f = seg(1).reshape(Q_LORA_RANK, MLA_HEADS, QK_HEAD_DIM)
    w_oa_f = seg(2).reshape(N_DEV, MLA_WIDTH, D_MODEL // N_DEV).transpose(1, 0, 2).reshape(MLA_WIDTH, D_MODEL)
    w_ob_f = seg(3).reshape(N_DEV, SGU_WIDTH, D_MODEL // N_DEV).transpose(1, 0, 2).reshape(SGU_WIDTH, D_MODEL)
    w_out_f = seg(4).reshape(D_MODEL, D_MODEL)

    partials = _local_step(x2, positions, tgt, w_in_f, b_in, g_q, w_uq_f, g_kv, w_ukv, w_oa_f, sgu_ln_g, sgu_ln_b,
                           w_s, b_s, w_ob_f, w_out_f, ln_g, ln_b)
    weights = dict(w_in=w_in, b_in=b_in, g_q=g_q, w_uq=w_uq, g_kv=g_kv, w_ukv=w_ukv, w_oa=w_oa, sgu_ln_g=sgu_ln_g,
                   sgu_ln_b=sgu_ln_b, w_s=w_s, b_s=b_s, w_ob=w_ob, w_out=w_out, ln_g=ln_g, ln_b=ln_b)
    moms = dict(w_in=m_w_in, b_in=m_b_in, g_q=m_g_q, w_uq=m_w_uq, g_kv=m_g_kv, w_ukv=m_w_ukv, w_oa=m_w_oa,
                sgu_ln_g=m_sgu_ln_g, sgu_ln_b=m_sgu_ln_b, w_s=m_w_s, b_s=m_b_s, w_ob=m_w_ob, w_out=m_w_out,
                ln_g=m_ln_g, ln_b=m_ln_b)
    vars_ = dict(w_in=v_w_in, b_in=v_b_in, g_q=v_g_q, w_uq=v_w_uq, g_kv=v_g_kv, w_ukv=v_w_ukv, w_oa=v_w_oa,
                 sgu_ln_g=v_sgu_ln_g, sgu_ln_b=v_sgu_ln_b, w_s=v_w_s, b_s=v_b_s, w_ob=v_w_ob, w_out=v_w_out,
                 ln_g=v_ln_g, ln_b=v_ln_b)
    return _reduce_and_update(partials, sh_local, offs, used, weights, moms, vars_)


def _local_step(x2, positions, tgt, w_in_f, b_in, g_q, w_uq_f, g_kv, w_ukv, w_oa_f, sgu_ln_g, sgu_ln_b, w_s, b_s,
                w_ob_f, w_out_f, ln_g, ln_b):
    wa, wb, wc = _to_segments(w_in_f)
    ba, bb, bc = _to_segments(b_in)
    wq = jnp.pad(w_uq_f, ((0, CQ_PAD - Q_LORA_RANK), (0, 0), (0, HEAD_PAD - QK_HEAD_DIM))).reshape(CQ_PAD, -1)
    w_ukv_bf = w_ukv[0].astype(BF16)
    wkn = jnp.pad(w_ukv_bf[:, :, :QK_NOPE_DIM], ((0, 0), (0, 0), (0, HEAD_PAD - QK_NOPE_DIM))).reshape(KV_LORA_RANK, -1)
    wv = w_ukv_bf[:, :, QK_NOPE_DIM:].reshape(KV_LORA_RANK, MLA_WIDTH)
    gq = jnp.pad(g_q, ((0, 0), (0, CQ_PAD - Q_LORA_RANK)))
    bias_full = jnp.repeat(b_s[0].T, SGU_GROUP_DIM, axis=1)
    w_s3 = w_s[0]
    w_st3 = jnp.swapaxes(w_s3, 1, 2)

    inv_freq = ROPE_THETA ** (-jnp.arange(0, QK_ROPE_DIM, 2, dtype=F32) / QK_ROPE_DIM)
    invf_lane = jnp.concatenate([jnp.zeros((ROPE_LO,), F32), inv_freq, inv_freq,
                                 jnp.zeros((LANES - ROPE_HI,), F32)]).reshape(1, LANES)
    c_t, sa_t, sb_t = _rope_tables(positions.reshape(SEQ, 1), invf_lane)

    h_a = _mm(x2, wa, bias=ba, tm=512, tn=512, tk=D_MODEL, name="in_proj_a")
    h_b = _mm(x2, wb, bias=bb, tm=512, tn=512, tk=D_MODEL, name="in_proj_b")
    h_c = _mm(x2, wc, bias=bc, tm=512, tn=SEG_C, tk=D_MODEL, name="in_proj_c")
    q, k, v = _mla_prep(h_c, gq, g_kv, wq, wkn, wv, c_t, sa_t, sb_t)
    o, lse = _attn_fwd(q, k, v)
    y_b = _sgu_fwd(h_b, sgu_ln_g, sgu_ln_b, w_s3, bias_full)

    (loss_row, dx_res, dh_a, d_o, d_yb, d_woa, d_wob, d_wout, d_lng, d_lnb) = _merge(
        x2, o, h_a, y_b, tgt, w_oa_f, w_ob_f, w_out_f, ln_g, ln_b)
    dh_b, d_ws, d_bs_t, d_slg, d_slb = _sgu_bwd(h_b, d_yb, sgu_ln_g, sgu_ln_b, w_s3, w_st3, bias_full)
    dq, dk, dv = _attn_bwd(q, k, v, d_o, o, lse)
    dh_c, d_wq, d_wkn, d_wv, d_gq, d_gkv = _mla_bwd(dq, dk, dv, h_c, gq, g_kv, wq, wkn, wv, c_t, sa_t, sb_t)

    dx = _mm(dh_a, wa, tb=True, add=dx_res, tm=512, tn=D_MODEL, tk=512, name="dx_a")
    dx = _mm(dh_b, wb, tb=True, add=dx, tm=512, tn=D_MODEL, tk=512, name="dx_b")
    dx = _mm(dh_c, wc, tb=True, add=dx, tm=512, tn=D_MODEL, tk=SEG_C, name="dx_c")
    d_wa, d_ba = _mm(x2, dh_a, ta=True, colsum=True, tm=512, tn=512, tk=512, name="dw_in_a")
    d_wb, d_bb = _mm(x2, dh_b, ta=True, colsum=True, tm=512, tn=512, tk=512, name="dw_in_b")
    d_wc, d_bc = _mm(x2, dh_c, ta=True, colsum=True, tm=512, tn=SEG_C, tk=512, name="dw_in_c")

    p_w_in = _from_segments(d_wa, d_wb, d_wc)
    p_b_in = _from_segments(d_ba, d_bb, d_bc)
    p_w_uq = d_wq[:Q_LORA_RANK].reshape(Q_LORA_RANK, MLA_HEADS, HEAD_PAD)[:, :, :QK_HEAD_DIM]
    p_w_ukv = jnp.concatenate([d_wkn.reshape(KV_LORA_RANK, MLA_HEADS, HEAD_PAD)[:, :, :QK_NOPE_DIM],
                               d_wv.reshape(KV_LORA_RANK, MLA_HEADS, V_HEAD_DIM)], axis=-1)
    p_g_q = d_gq[:, :Q_LORA_RANK]
    p_b_s = d_bs_t[:, :SGU_GROUPS].T
    return (loss_row, dx, p_w_in, p_b_in, p_g_q, p_w_uq, d_gkv, p_w_ukv, d_woa, d_slg, d_slb, d_ws, p_b_s, d_wob,
            d_wout, d_lng, d_lnb)


_NAMES = ["w_in", "b_in", "g_q", "w_uq", "g_kv", "w_ukv", "w_oa", "sgu_ln_g", "sgu_ln_b", "w_s", "b_s", "w_ob",
          "w_out", "ln_g", "ln_b"]
_SHARDED = ["w_in", "w_uq", "w_oa", "w_ob", "w_out"]
_REPLICATED = ["b_in", "g_q", "g_kv", "w_ukv", "sgu_ln_g", "sgu_ln_b", "w_s", "b_s", "ln_g", "ln_b"]


def _reduce_and_update(partials, sh_local, offs, used, weights, moms, vars_):
    (loss_row, dx, p_w_in, p_b_in, p_g_q, p_w_uq, d_gkv, p_w_ukv, d_woa, d_slg, d_slb, d_ws, p_b_s, d_wob, d_wout,
     d_lng, d_lnb) = partials
    sh_parts = jnp.concatenate([
        p_w_in.reshape(D_MODEL, N_DEV, IN_WIDTH // N_DEV).transpose(1, 0, 2).reshape(N_DEV, -1, LANES),
        p_w_uq.reshape(N_DEV, -1, LANES),
        d_woa.reshape(MLA_WIDTH, N_DEV, D_MODEL // N_DEV).transpose(1, 0, 2).reshape(N_DEV, -1, LANES),
        d_wob.reshape(SGU_WIDTH, N_DEV, D_MODEL // N_DEV).transpose(1, 0, 2).reshape(N_DEV, -1, LANES),
        d_wout.reshape(N_DEV, -1, LANES),
        jnp.zeros((N_DEV, PACK_S_ROWS - used, LANES), F32)], axis=1).astype(BF16)
    sh_sum = _reduce_scatter(sh_parts, 1024, "reduce_sharded")
    g_sh = [sh_sum[offs[i]:offs[i + 1]].reshape(a.shape) for i, a in enumerate(sh_local)]

    rep_partial = [p_b_in, p_g_q, d_gkv, p_w_ukv, d_slg, d_slb, d_ws, p_b_s, d_lng, d_lnb]
    rep_flat = jnp.concatenate([a.reshape(-1) for a in rep_partial])
    rep_flat = jnp.pad(rep_flat, (0, N_DEV * PACK_R_ROWS * LANES - rep_flat.size))
    rep_slice = _reduce_scatter(rep_flat.reshape(N_DEV, PACK_R_ROWS, LANES), PACK_R_ROWS, "reduce_replicated")
    rep_sum = _all_gather(rep_slice, "gather_replicated").reshape(-1)
    grads, pos = dict(zip(_SHARDED, g_sh)), 0
    for nm in _REPLICATED:
        grads[nm] = rep_sum[pos:pos + weights[nm].size]
        pos += weights[nm].size
    grads = {nm: grads[nm].reshape(weights[nm].shape) for nm in _NAMES}

    def two_d(a):
        return a.reshape(-1, a.shape[-1]) if a.ndim != 4 else a.reshape(-1, a.shape[-2] * a.shape[-1])

    deltas, new_m, new_v = _adamw_all([two_d(weights[nm]) for nm in _NAMES], [two_d(grads[nm]) for nm in _NAMES],
                                      [two_d(moms[nm]) for nm in _NAMES], [two_d(vars_[nm]) for nm in _NAMES])
    shape_of = lambda lst: [a.reshape(weights[nm].shape) for nm, a in zip(_NAMES, lst)]

    loss = lax.psum(loss_row[0, 0], ("x", "y", "c"))
    return (loss, dx.reshape(1, SEQ, D_MODEL), *[grads[nm] for nm in _NAMES], *shape_of(deltas), *shape_of(new_m),
            *shape_of(new_v))
```

```python
import math

import jax
import jax.numpy as jnp
from jax import lax
from jax.experimental import pallas as pl
from jax.experimental.pallas import tpu as pltpu

F32 = jnp.float32
BF16 = jnp.bfloat16

D_MODEL = 1024
SEQ = 2048
N_DEV = 8
MLA_HEADS = 8
Q_LORA_RANK = 384
KV_LORA_RANK = 128
QK_NOPE_DIM = 64
QK_ROPE_DIM = 32
V_HEAD_DIM = 64
QK_HEAD_DIM = QK_NOPE_DIM + QK_ROPE_DIM
MLA_WIDTH = MLA_HEADS * V_HEAD_DIM
ROPE_THETA = 10000.0
SGU_GROUPS = 8
SGU_GROUP_DIM = 64
SGU_WIDTH = SGU_GROUPS * SGU_GROUP_DIM
CHUNK = 128
RMS_EPS = 1e-6
LN_EPS = 1e-5
DN_ALPHA = 2.0 ** 0.25
IN_WIDTH = 4640
ATTN_SCALE = QK_HEAD_DIM ** -0.5

ADAM_LR = 0.001
ADAM_B1 = 0.9
ADAM_B2 = 0.999
ADAM_EPS = 1e-08
ADAM_WD = 0.01
ADAM_STEP = 10

LANES = 128
HEAD_PAD = 128
ROPE_LO = QK_NOPE_DIM
ROPE_MID = ROPE_LO + QK_ROPE_DIM // 2
ROPE_HI = ROPE_LO + QK_ROPE_DIM
CQ_PAD = 512

SEG_A = 2560
SEG_B = 1536
SEG_C = 768

PACK_R_ROWS = 272
VMEM_BIG = 56 * 1024 * 1024
VMEM_MID = 40 * 1024 * 1024


def _sigmoid(x):
    return 1.0 / (1.0 + jnp.exp(-x))


def _gelu_and_grad(x):
    c0 = math.sqrt(2.0 / math.pi)
    x2 = x * x
    t = jnp.tanh(c0 * (x + 0.044715 * x * x2))
    g = 0.5 * x * (1.0 + t)
    dg = 0.5 * (1.0 + t) + 0.5 * x * (1.0 - t * t) * (c0 * (1.0 + 3.0 * 0.044715 * x2))
    return g, dg


def _dot(a, b, dims):
    return lax.dot_general(a, b, (dims, ((), ())), preferred_element_type=F32)


_NN = ((1,), (0,))
_NT = ((1,), (1,))
_TN = ((0,), (0,))


def _mm(a, b, *, ta=False, tb=False, bias=None, add=None, colsum=False, out_dtype=F32, tm, tn, tk, name):
    m, k = (a.shape[1], a.shape[0]) if ta else a.shape
    n = b.shape[0] if tb else b.shape[1]
    assert m % tm == 0 and n % tn == 0 and k % tk == 0
    nk = k // tk
    dims = ((0,) if ta else (1,), (1,) if tb else (0,))

    def body(*refs):
        a_ref, b_ref = refs[0], refs[1]
        pos = 2
        bias_ref = add_ref = cs_ref = None
        if bias is not None:
            bias_ref = refs[pos]; pos += 1
        if add is not None:
            add_ref = refs[pos]; pos += 1
        o_ref = refs[pos]; pos += 1
        if colsum:
            cs_ref = refs[pos]; pos += 1
        acc_ref = refs[pos]
        i, kk = pl.program_id(1), pl.program_id(2)

        @pl.when(kk == 0)
        def _():
            acc_ref[...] = jnp.zeros_like(acc_ref)

        bv = b_ref[...]
        acc_ref[...] += _dot(a_ref[...].astype(BF16), bv.astype(BF16), dims)

        if colsum:
            @pl.when(jnp.logical_and(i == 0, kk == 0))
            def _():
                cs_ref[...] = jnp.zeros_like(cs_ref)

            @pl.when(i == 0)
            def _():
                cs_ref[...] += jnp.sum(bv.astype(F32), axis=0, keepdims=True)

        @pl.when(kk == nk - 1)
        def _():
            r = acc_ref[...]
            if bias is not None:
                r = r + bias_ref[...]
            if add is not None:
                r = r + add_ref[...]
            o_ref[...] = r.astype(out_dtype)

    a_spec = pl.BlockSpec((tk, tm), lambda j, i, kk: (kk, i)) if ta else pl.BlockSpec((tm, tk), lambda j, i, kk: (i, kk))
    b_spec = pl.BlockSpec((tn, tk), lambda j, i, kk: (j, kk)) if tb else pl.BlockSpec((tk, tn), lambda j, i, kk: (kk, j))
    in_specs, args = [a_spec, b_spec], [a, b]
    if bias is not None:
        in_specs.append(pl.BlockSpec((1, tn), lambda j, i, kk: (0, j))); args.append(bias)
    if add is not None:
        in_specs.append(pl.BlockSpec((tm, tn), lambda j, i, kk: (i, j))); args.append(add)
    out_shape = [jax.ShapeDtypeStruct((m, n), out_dtype)]
    out_specs = [pl.BlockSpec((tm, tn), lambda j, i, kk: (i, j))]
    if colsum:
        out_shape.append(jax.ShapeDtypeStruct((1, n), F32))
        out_specs.append(pl.BlockSpec((1, tn), lambda j, i, kk: (0, j)))
    res = pl.pallas_call(
        body, name=name, grid=(n // tn, m // tm, nk), in_specs=in_specs, out_specs=out_specs, out_shape=out_shape,
        scratch_shapes=[pltpu.VMEM((tm, tn), F32)],
        compiler_params=pltpu.CompilerParams(dimension_semantics=("arbitrary", "arbitrary", "arbitrary"),
                                             vmem_limit_bytes=VMEM_MID),
    )(*args)
    return res if colsum else res[0]


def _rope_tables(pos_col, invf_lane):
    def body(pos_ref, invf_ref, c_ref, sa_ref, sb_ref):
        ang = pos_ref[...].astype(F32) * invf_ref[...]
        cs, sn = jnp.cos(ang), jnp.sin(ang)
        lane = lax.broadcasted_iota(jnp.int32, ang.shape, 1)
        c_ref[...] = jnp.where(lane < ROPE_LO, 1.0, jnp.where(lane < ROPE_HI, cs, 0.0))
        sa_ref[...] = jnp.where(jnp.logical_and(lane >= ROPE_LO, lane < ROPE_MID), -sn, 0.0)
        sb_ref[...] = jnp.where(jnp.logical_and(lane >= ROPE_MID, lane < ROPE_HI), sn, 0.0)

    shp = jax.ShapeDtypeStruct((SEQ, LANES), F32)
    return pl.pallas_call(body, name="rope_tables", out_shape=[shp, shp, shp])(pos_col, invf_lane)


def _rope(x, c, sa, sb):
    return x * c + pltpu.roll(x, LANES - 16, 1) * sa + pltpu.roll(x, 16, 1) * sb


def _rope_t(dy, c, sa, sb):
    return dy * c + pltpu.roll(dy * sa, 16, 1) + pltpu.roll(dy * sb, LANES - 16, 1)


def _mla_prep(h_c, gq, gkv, wq, wkn, wv, c_t, sa_t, sb_t):
    tm = 256

    def body(cq_ref, ckv_ref, kpe_ref, gq_ref, gkv_ref, wq_ref, wkn_ref, wv_ref, c_ref, sa_ref, sb_ref,
             q_ref, k_ref, v_ref):
        c, sa, sb = c_ref[...], sa_ref[...], sb_ref[...]
        cq = cq_ref[...]
        rq = lax.rsqrt(jnp.sum(cq * cq, axis=1, keepdims=True) * (1.0 / Q_LORA_RANK) + RMS_EPS)
        cqn = ((cq * rq) * gq_ref[...]).astype(BF16)
        qall = _dot(cqn, wq_ref[...], _NN)
        for h in range(MLA_HEADS):
            sl = slice(HEAD_PAD * h, HEAD_PAD * (h + 1))
            q_ref[:, sl] = _rope(qall[:, sl], c, sa, sb).astype(BF16)
        ckv = ckv_ref[...]
        rkv = lax.rsqrt(jnp.sum(ckv * ckv, axis=1, keepdims=True) * (1.0 / KV_LORA_RANK) + RMS_EPS)
        ckvn = ((ckv * rkv) * gkv_ref[...]).astype(BF16)
        knall = _dot(ckvn, wkn_ref[...], _NN)
        kper = _rope(kpe_ref[...], c, sa, sb)
        for h in range(MLA_HEADS):
            sl = slice(HEAD_PAD * h, HEAD_PAD * (h + 1))
            k_ref[:, sl] = (knall[:, sl] + kper).astype(BF16)
        v_ref[...] = _dot(ckvn, wv_ref[...], _NN).astype(BF16)

    full = lambda shape: pl.BlockSpec(shape, lambda i: (0, 0))
    tab = pl.BlockSpec((tm, LANES), lambda i: (i, 0))
    return pl.pallas_call(
        body, name="mla_prep", grid=(SEQ // tm,),
        in_specs=[pl.BlockSpec((tm, CQ_PAD), lambda i: (i, 0)),
                  pl.BlockSpec((tm, LANES), lambda i: (i, CQ_PAD // LANES)),
                  pl.BlockSpec((tm, LANES), lambda i: (i, CQ_PAD // LANES + 1)),
                  full((1, CQ_PAD)), full((1, KV_LORA_RANK)),
                  full((CQ_PAD, MLA_HEADS * HEAD_PAD)), full((KV_LORA_RANK, MLA_HEADS * HEAD_PAD)),
                  full((KV_LORA_RANK, MLA_WIDTH)), tab, tab, tab],
        out_specs=[pl.BlockSpec((tm, MLA_HEADS * HEAD_PAD), lambda i: (i, 0)),
                   pl.BlockSpec((tm, MLA_HEADS * HEAD_PAD), lambda i: (i, 0)),
                   pl.BlockSpec((tm, MLA_WIDTH), lambda i: (i, 0))],
        out_shape=[jax.ShapeDtypeStruct((SEQ, MLA_HEADS * HEAD_PAD), BF16),
                   jax.ShapeDtypeStruct((SEQ, MLA_HEADS * HEAD_PAD), BF16),
                   jax.ShapeDtypeStruct((SEQ, MLA_WIDTH), BF16)],
        compiler_params=pltpu.CompilerParams(dimension_semantics=("arbitrary",), vmem_limit_bytes=VMEM_MID),
    )(h_c, h_c, h_c, gq, gkv, wq, wkn, wv, c_t, sa_t, sb_t)


ATT_T = 256


def _attn_fwd(q, k, v):
    t = ATT_T

    def body(q_ref, k_ref, v_ref, o_ref, l_ref):
        qi = pl.program_id(1)
        row = qi * t + lax.broadcasted_iota(jnp.int32, (t, t), 0)
        col0 = lax.broadcasted_iota(jnp.int32, (t, t), 1)
        lane = lax.broadcasted_iota(jnp.int32, (t, LANES), 1)
        res = []
        for a in range(2):
            sl = slice(HEAD_PAD * a, HEAD_PAD * (a + 1))
            qa = q_ref[:, sl]

            def step(j, carry, sl=sl, qa=qa):
                m, l, acc = carry
                off = pl.multiple_of(j * t, t)
                kj = k_ref[pl.ds(off, t), sl]
                vj = v_ref[pl.ds(off, t), :]
                s = _dot(qa, kj, _NT) * ATTN_SCALE
                s = jnp.where(col0 + off <= row, s, -1e30)
                m_new = jnp.maximum(m, jnp.max(s, axis=1, keepdims=True))
                alpha = jnp.exp(m - m_new)
                p = jnp.exp(s - m_new)
                l = alpha * l + jnp.sum(p, axis=1, keepdims=True)
                acc = alpha * acc + _dot(p.astype(BF16), vj, _NN)
                return m_new, l, acc

            m, l, acc = lax.fori_loop(
                0, qi + 1, step,
                (jnp.full((t, 1), -1e30, F32), jnp.zeros((t, 1), F32), jnp.zeros((t, LANES), F32)))
            res.append((acc / l, m + jnp.log(l)))
        o_ref[...] = jnp.where(lane < V_HEAD_DIM, res[0][0], res[1][0])
        l_ref[...] = jnp.where(lane < V_HEAD_DIM, res[0][1], res[1][1])

    return pl.pallas_call(
        body, name="attn_fwd", grid=(MLA_HEADS // 2, SEQ // t),
        in_specs=[pl.BlockSpec((t, 2 * HEAD_PAD), lambda p, i: (i, p)),
                  pl.BlockSpec((SEQ, 2 * HEAD_PAD), lambda p, i: (0, p)),
                  pl.BlockSpec((SEQ, LANES), lambda p, i: (0, p))],
        out_specs=[pl.BlockSpec((t, LANES), lambda p, i: (i, p)),
                   pl.BlockSpec((t, LANES), lambda p, i: (i, p))],
        out_shape=[jax.ShapeDtypeStruct((SEQ, MLA_WIDTH), F32), jax.ShapeDtypeStruct((SEQ, MLA_WIDTH), F32)],
        compiler_params=pltpu.CompilerParams(dimension_semantics=("arbitrary", "arbitrary"), vmem_limit_bytes=VMEM_MID),
    )(q, k, v)


def _attn_bwd(q, k, v, d_o, o, lse):
    t = ATT_T
    nq = SEQ // t

    def body(q_ref, k_ref, v_ref, do_ref, o_ref, l_ref, dq_ref, dk_ref, dv_ref):
        dk_ref[...] = jnp.zeros_like(dk_ref)
        dv_ref[...] = jnp.zeros_like(dv_ref)
        lane = lax.broadcasted_iota(jnp.int32, (t, LANES), 1)
        col0 = lax.broadcasted_iota(jnp.int32, (t, t), 1)
        row0 = lax.broadcasted_iota(jnp.int32, (t, t), 0)

        def outer(i, carry):
            ioff = pl.multiple_of(i * t, t)
            do_i = do_ref[pl.ds(ioff, t), :]
            o_i = o_ref[pl.ds(ioff, t), :]
            l_i = l_ref[pl.ds(ioff, t), :]
            row = row0 + ioff
            for a in range(2):
                sl = slice(HEAD_PAD * a, HEAD_PAD * (a + 1))
                sel = (lane < V_HEAD_DIM) if a == 0 else (lane >= V_HEAD_DIM)
                doa = jnp.where(sel, do_i, 0.0)
                da = jnp.sum(doa * o_i, axis=1, keepdims=True)
                la = l_i[:, V_HEAD_DIM * a:V_HEAD_DIM * a + 1]
                qa = q_ref[pl.ds(ioff, t), sl]
                doa_bf = doa.astype(BF16)

                def inner(j, dq_acc, sl=sl, qa=qa, doa_bf=doa_bf, da=da, la=la):
                    joff = pl.multiple_of(j * t, t)
                    kj = k_ref[pl.ds(joff, t), sl]
                    vj = v_ref[pl.ds(joff, t), :]
                    s = _dot(qa, kj, _NT) * ATTN_SCALE
                    p = jnp.where(col0 + joff <= row, jnp.exp(s - la), 0.0)
                    dp = _dot(doa_bf, vj, _NT)
                    ds_bf = (p * (dp - da) * ATTN_SCALE).astype(BF16)
                    dk_ref[pl.ds(joff, t), sl] += _dot(ds_bf, qa, _TN)
                    dv_ref[pl.ds(joff, t), :] += _dot(p.astype(BF16), doa_bf, _TN)
                    return dq_acc + _dot(ds_bf, kj, _NN)

                dq_ref[pl.ds(ioff, t), sl] = lax.fori_loop(0, i + 1, inner, jnp.zeros((t, HEAD_PAD), F32))
            return carry

        lax.fori_loop(0, nq, outer, 0)

    wide = pl.BlockSpec((SEQ, 2 * HEAD_PAD), lambda p: (0, p))
    narrow = pl.BlockSpec((SEQ, LANES), lambda p: (0, p))
    return pl.pallas_call(
        body, name="attn_bwd", grid=(MLA_HEADS // 2,),
        in_specs=[wide, wide, narrow, narrow, narrow, narrow],
        out_specs=[wide, wide, narrow],
        out_shape=[jax.ShapeDtypeStruct((SEQ, MLA_HEADS * HEAD_PAD), F32),
                   jax.ShapeDtypeStruct((SEQ, MLA_HEADS * HEAD_PAD), F32),
                   jax.ShapeDtypeStruct((SEQ, MLA_WIDTH), F32)],
        compiler_params=pltpu.CompilerParams(dimension_semantics=("arbitrary",), vmem_limit_bytes=VMEM_BIG),
    )(q, k, v, d_o, o, lse)


def _sgu_math(u, v, zb, lg, lb, ws_ref, bias):
    ug, dug = _gelu_and_grad(u)
    vg, dvg = _gelu_and_grad(v)
    mu = jnp.mean(vg, axis=1, keepdims=True)
    xc = vg - mu
    rstd = lax.rsqrt(jnp.mean(xc * xc, axis=1, keepdims=True) + LN_EPS)
    xh = xc * rstd
    vn_bf = (xh * lg + lb).astype(BF16)
    grp = lax.broadcasted_iota(jnp.int32, (CHUNK, SGU_WIDTH), 1) // SGU_GROUP_DIM
    r_i = lax.broadcasted_iota(jnp.int32, (CHUNK, CHUNK), 0)
    c_i = lax.broadcasted_iota(jnp.int32, (CHUNK, CHUNK), 1)
    tri, tri_t = r_i >= c_i, r_i <= c_i
    mixed = bias
    for g in range(SGU_GROUPS):
        wt = jnp.where(tri, ws_ref[g], 0.0).astype(BF16)
        mixed = mixed + jnp.where(grp == g, _dot(wt, vn_bf, _NN), 0.0)
    sb = _sigmoid(zb)
    return ug, dug, dvg, rstd, xh, vn_bf, grp, tri, tri_t, mixed, sb


def _sgu_fwd(h_b, lg, lb, w_s, bias_full):
    def body(u_ref, v_ref, zb_ref, lg_ref, lb_ref, ws_ref, bias_ref, yb_ref):
        zb = zb_ref[...]
        ug, _, _, _, _, _, _, _, _, mixed, sb = _sgu_math(u_ref[...], v_ref[...], zb, lg_ref[...], lb_ref[...],
                                                       ws_ref, bias_ref[...])
        yb_ref[...] = (ug * mixed) * (zb * sb)

    blk = lambda c: pl.BlockSpec((CHUNK, SGU_WIDTH), lambda i, c=c: (i, c))
    full2 = lambda shape: pl.BlockSpec(shape, lambda i: (0, 0))
    return pl.pallas_call(
        body, name="sgu_fwd", grid=(SEQ // CHUNK,),
        in_specs=[blk(0), blk(1), blk(2), full2((1, SGU_WIDTH)), full2((1, SGU_WIDTH)),
                  pl.BlockSpec((SGU_GROUPS, CHUNK, CHUNK), lambda i: (0, 0, 0)), full2((CHUNK, SGU_WIDTH))],
        out_specs=pl.BlockSpec((CHUNK, SGU_WIDTH), lambda i: (i, 0)),
        out_shape=jax.ShapeDtypeStruct((SEQ, SGU_WIDTH), F32),
        compiler_params=pltpu.CompilerParams(dimension_semantics=("arbitrary",)),
    )(h_b, h_b, h_b, lg, lb, w_s, bias_full)


def _sgu_bwd(h_b, d_yb, lg, lb, w_s, w_st, bias_full):
    nsteps = SEQ // CHUNK

    def body(u_ref, v_ref, zb_ref, dyb_ref, lg_ref, lb_ref, ws_ref, wst_ref, bias_ref,
             dhb_ref, dws_ref, dbs_ref, dlg_ref, dlb_ref, dbias_acc):
        step = pl.program_id(0)

        @pl.when(step == 0)
        def _():
            dws_ref[...] = jnp.zeros_like(dws_ref)
            dlg_ref[...] = jnp.zeros_like(dlg_ref)
            dlb_ref[...] = jnp.zeros_like(dlb_ref)
            dbias_acc[...] = jnp.zeros_like(dbias_acc)

        zb = zb_ref[...]
        lg = lg_ref[...]
        ug, dug, dvg, rstd, xh, vn_bf, grp, tri, tri_t, mixed, sb = _sgu_math(
            u_ref[...], v_ref[...], zb, lg, lb_ref[...], ws_ref, bias_ref[...])
        dyb = dyb_ref[...]
        dsgu = dyb * (zb * sb)
        dzb = dyb * (ug * mixed) * (sb * (1.0 + zb * (1.0 - sb)))
        du = dsgu * mixed * dug
        dmixed = dsgu * ug
        dbias_acc[...] += dmixed
        dvn = jnp.zeros((CHUNK, SGU_WIDTH), F32)
        for g in range(SGU_GROUPS):
            dm_g = jnp.where(grp == g, dmixed, 0.0).astype(BF16)
            wtt = jnp.where(tri_t, wst_ref[g], 0.0).astype(BF16)
            dvn = dvn + _dot(wtt, dm_g, _NN)
            dws_ref[g] += jnp.where(tri, _dot(dm_g, vn_bf, _NT), 0.0)
        dlg_ref[...] += jnp.sum(dvn * xh, axis=0, keepdims=True)
        dlb_ref[...] += jnp.sum(dvn, axis=0, keepdims=True)
        dxh = dvn * lg
        dvgel = rstd * (dxh - jnp.mean(dxh, axis=1, keepdims=True) - xh * jnp.mean(dxh * xh, axis=1, keepdims=True))
        dhb_ref[:, 0:SGU_WIDTH] = du
        dhb_ref[:, SGU_WIDTH:2 * SGU_WIDTH] = dvgel * dvg
        dhb_ref[:, 2 * SGU_WIDTH:3 * SGU_WIDTH] = dzb

        @pl.when(step == nsteps - 1)
        def _():
            acc = dbias_acc[...]
            lane = lax.broadcasted_iota(jnp.int32, (CHUNK, LANES), 1)
            out = jnp.zeros((CHUNK, LANES), F32)
            for g in range(SGU_GROUPS):
                sg = jnp.sum(jnp.where(grp == g, acc, 0.0), axis=1, keepdims=True)
                out = jnp.where(lane == g, sg, out)
            dbs_ref[...] = out

    blk = lambda c: pl.BlockSpec((CHUNK, SGU_WIDTH), lambda i, c=c: (i, c))
    full2 = lambda shape: pl.BlockSpec(shape, lambda i: (0, 0))
    full3 = pl.BlockSpec((SGU_GROUPS, CHUNK, CHUNK), lambda i: (0, 0, 0))
    return pl.pallas_call(
        body, name="sgu_bwd", grid=(nsteps,),
        in_specs=[blk(0), blk(1), blk(2), pl.BlockSpec((CHUNK, SGU_WIDTH), lambda i: (i, 0)),
                  full2((1, SGU_WIDTH)), full2((1, SGU_WIDTH)), full3, full3, full2((CHUNK, SGU_WIDTH))],
        out_specs=[pl.BlockSpec((CHUNK, SEG_B), lambda i: (i, 0)), full3, full2((CHUNK, LANES)),
                   full2((1, SGU_WIDTH)), full2((1, SGU_WIDTH))],
        out_shape=[jax.ShapeDtypeStruct((SEQ, SEG_B), F32),
                   jax.ShapeDtypeStruct((SGU_GROUPS, CHUNK, CHUNK), F32),
                   jax.ShapeDtypeStruct((CHUNK, LANES), F32),
                   jax.ShapeDtypeStruct((1, SGU_WIDTH), F32), jax.ShapeDtypeStruct((1, SGU_WIDTH), F32)],
        scratch_shapes=[pltpu.VMEM((CHUNK, SGU_WIDTH), F32)],
        compiler_params=pltpu.CompilerParams(dimension_semantics=("arbitrary",)),
    )(h_b, h_b, h_b, d_yb, lg, lb, w_s, w_st, bias_full)


def _merge(x, o, h_a, y_b, target, w_oa, w_ob, w_out, ln_g, ln_b):
    tm = 128
    nsteps = SEQ // tm

    def body(x_ref, o_ref, ga_ref, gb_ref, za_ref, yb_ref, tgt_ref, woa_ref, wob_ref, wout_ref, lng_ref, lnb_ref,
             loss_ref, dxr_ref, dha_ref, do_ref, dyb_ref, poa_ref, pob_ref, pout_ref, dlng_ref, dlnb_ref,
             dwoa_ref, dwob_ref, dwout_ref):
        step = pl.program_id(0)

        @pl.when(step == 0)
        def _():
            for r in (loss_ref, dwoa_ref, dwob_ref, dwout_ref, dlng_ref, dlnb_ref):
                r[...] = jnp.zeros_like(r)

        o = o_ref[...]
        za = za_ref[...]
        sa = _sigmoid(za)
        ya_bf = (o * (za * sa)).astype(BF16)
        yb_bf = yb_ref[...].astype(BF16)
        woa, wob, wout = woa_ref[...], wob_ref[...], wout_ref[...]
        pa = _dot(ya_bf, woa, _NN)
        pb = _dot(yb_bf, wob, _NN)
        sga = _sigmoid(ga_ref[...])
        sgb = _sigmoid(gb_ref[...])
        merged_bf = (sga * pa + sgb * pb).astype(BF16)
        r = DN_ALPHA * x_ref[...] + _dot(merged_bf, wout, _NN)
        mu = jnp.mean(r, axis=1, keepdims=True)
        rc = r - mu
        rstd = lax.rsqrt(jnp.mean(rc * rc, axis=1, keepdims=True) + LN_EPS)
        xh = rc * rstd
        lng = lng_ref[...]
        y = xh * lng + lnb_ref[...]
        e = y - tgt_ref[...]
        loss_ref[...] += 0.5 * jnp.sum(jnp.sum(e * e, axis=1, keepdims=True) * (1.0 / D_MODEL), axis=0, keepdims=True)

        dy = e * (1.0 / D_MODEL)
        dlng_ref[...] += jnp.sum(dy * xh, axis=0, keepdims=True)
        dlnb_ref[...] += jnp.sum(dy, axis=0, keepdims=True)
        dxh = dy * lng
        dr = rstd * (dxh - jnp.mean(dxh, axis=1, keepdims=True) - xh * jnp.mean(dxh * xh, axis=1, keepdims=True))
        dxr_ref[...] = DN_ALPHA * dr
        dr_bf = dr.astype(BF16)
        dwout_ref[...] += _dot(merged_bf, dr_bf, _TN)
        dmerged = _dot(dr_bf, wout, _NT)
        dpa_bf = (dmerged * sga).astype(BF16)
        dpb_bf = (dmerged * sgb).astype(BF16)
        dha_ref[:, 0:D_MODEL] = dmerged * pa * (sga * (1.0 - sga))
        dha_ref[:, D_MODEL:2 * D_MODEL] = dmerged * pb * (sgb * (1.0 - sgb))
        dwoa_ref[...] += _dot(ya_bf, dpa_bf, _TN)
        dwob_ref[...] += _dot(yb_bf, dpb_bf, _TN)
        dya = _dot(dpa_bf, woa, _NT)
        dyb_ref[...] = _dot(dpb_bf, wob, _NT)
        do_ref[...] = dya * (za * sa)
        dha_ref[:, 2 * D_MODEL:SEG_A] = dya * o * (sa * (1.0 + za * (1.0 - sa)))

        @pl.when(step == nsteps - 1)
        def _():
            cols = D_MODEL // N_DEV
            for j in range(N_DEV):
                poa_ref[j] = dwoa_ref[:, cols * j:cols * (j + 1)].astype(BF16)
                pob_ref[j] = dwob_ref[:, cols * j:cols * (j + 1)].astype(BF16)
                pout_ref[j] = dwout_ref[cols * j:cols * (j + 1), :].astype(BF16)

    row = lambda w, c=0: pl.BlockSpec((tm, w), lambda i, c=c: (i, c))
    full = lambda shape: pl.BlockSpec(shape, lambda i: (0, 0))
    full3 = lambda shape: pl.BlockSpec(shape, lambda i: (0, 0, 0))
    return pl.pallas_call(
        body, name="merge", grid=(nsteps,),
        in_specs=[row(D_MODEL), row(MLA_WIDTH), row(D_MODEL, 0), row(D_MODEL, 1), row(MLA_WIDTH, 4), row(SGU_WIDTH),
                  row(D_MODEL), full((MLA_WIDTH, D_MODEL)), full((SGU_WIDTH, D_MODEL)), full((D_MODEL, D_MODEL)),
                  full((1, D_MODEL)), full((1, D_MODEL))],
        out_specs=[full((1, LANES)), row(D_MODEL), row(SEG_A), row(MLA_WIDTH), row(SGU_WIDTH),
                   full3((N_DEV, MLA_WIDTH, D_MODEL // N_DEV)), full3((N_DEV, SGU_WIDTH, D_MODEL // N_DEV)),
                   full3((N_DEV, D_MODEL // N_DEV, D_MODEL)), full((1, D_MODEL)), full((1, D_MODEL))],
        out_shape=[jax.ShapeDtypeStruct((1, LANES), F32),
                   jax.ShapeDtypeStruct((SEQ, D_MODEL), F32), jax.ShapeDtypeStruct((SEQ, SEG_A), F32),
                   jax.ShapeDtypeStruct((SEQ, MLA_WIDTH), F32), jax.ShapeDtypeStruct((SEQ, SGU_WIDTH), F32),
                   jax.ShapeDtypeStruct((N_DEV, MLA_WIDTH, D_MODEL // N_DEV), BF16),
                   jax.ShapeDtypeStruct((N_DEV, SGU_WIDTH, D_MODEL // N_DEV), BF16),
                   jax.ShapeDtypeStruct((N_DEV, D_MODEL // N_DEV, D_MODEL), BF16),
                   jax.ShapeDtypeStruct((1, D_MODEL), F32), jax.ShapeDtypeStruct((1, D_MODEL), F32)],
        scratch_shapes=[pltpu.VMEM((MLA_WIDTH, D_MODEL), F32), pltpu.VMEM((SGU_WIDTH, D_MODEL), F32),
                        pltpu.VMEM((D_MODEL, D_MODEL), F32)],
        compiler_params=pltpu.CompilerParams(dimension_semantics=("arbitrary",), vmem_limit_bytes=VMEM_BIG),
    )(x, o, h_a, h_a, h_a, y_b, target, w_oa, w_ob, w_out, ln_g, ln_b)


def _mla_bwd(dq, dk, dv, h_c, gq, gkv, wq, wkn, wv, c_t, sa_t, sb_t):
    tm = 256
    hw = MLA_HEADS * HEAD_PAD

    def body(dq_ref, dk_ref, dv_ref, cq_ref, ckv_ref, gq_ref, gkv_ref, wq_ref, wkn_ref, wv_ref, c_ref, sa_ref, sb_ref,
             dhc_ref, puq_ref, dwkn_ref, dwv_ref, dgq_ref, dgkv_ref, pre_ref, dwq_ref):
        @pl.when(pl.program_id(0) == 0)
        def _():
            for r in (dwq_ref, dwkn_ref, dwv_ref, dgq_ref, dgkv_ref):
                r[...] = jnp.zeros_like(r)

        c, sa, sb = c_ref[...], sa_ref[...], sb_ref[...]
        lane = lax.broadcasted_iota(jnp.int32, (tm, LANES), 1)
        rope_lanes = jnp.logical_and(lane >= ROPE_LO, lane < ROPE_HI)

        cq = cq_ref[...]
        gq = gq_ref[...]
        rq = lax.rsqrt(jnp.sum(cq * cq, axis=1, keepdims=True) * (1.0 / Q_LORA_RANK) + RMS_EPS)
        nq = cq * rq
        cqn_bf = (nq * gq).astype(BF16)
        for h in range(MLA_HEADS):
            sl = slice(HEAD_PAD * h, HEAD_PAD * (h + 1))
            pre_ref[:, sl] = _rope_t(dq_ref[:, sl], c, sa, sb).astype(BF16)
        dqpre_bf = pre_ref[...]
        dcqn = _dot(dqpre_bf, wq_ref[...], _NT)
        dwq_ref[...] += _dot(cqn_bf, dqpre_bf, _TN)
        dgq_ref[...] += jnp.sum(dcqn * nq, axis=0, keepdims=True)
        dnq = dcqn * gq
        dhc_ref[:, 0:CQ_PAD] = rq * (dnq - nq * (jnp.sum(dnq * nq, axis=1, keepdims=True) * (1.0 / Q_LORA_RANK)))

        ckv = ckv_ref[...]
        gkv = gkv_ref[...]
        rkv = lax.rsqrt(jnp.sum(ckv * ckv, axis=1, keepdims=True) * (1.0 / KV_LORA_RANK) + RMS_EPS)
        nkv = ckv * rkv
        ckvn_bf = (nkv * gkv).astype(BF16)
        dk = dk_ref[...]
        dk_bf = dk.astype(BF16)
        dv_bf = dv_ref[...].astype(BF16)
        dckvn = _dot(dk_bf, wkn_ref[...], _NT) + _dot(dv_bf, wv_ref[...], _NT)
        dwkn_ref[...] += _dot(ckvn_bf, dk_bf, _TN)
        dwv_ref[...] += _dot(ckvn_bf, dv_bf, _TN)
        dgkv_ref[...] += jnp.sum(dckvn * nkv, axis=0, keepdims=True)
        dnkv = dckvn * gkv
        dhc_ref[:, CQ_PAD:CQ_PAD + LANES] = rkv * (
            dnkv - nkv * (jnp.sum(dnkv * nkv, axis=1, keepdims=True) * (1.0 / KV_LORA_RANK)))
        dkpe = jnp.zeros((tm, LANES), F32)
        for h in range(MLA_HEADS):
            dkpe = dkpe + dk[:, HEAD_PAD * h:HEAD_PAD * (h + 1)]
        dhc_ref[:, CQ_PAD + LANES:SEG_C] = _rope_t(jnp.where(rope_lanes, dkpe, 0.0), c, sa, sb)

        @pl.when(pl.program_id(0) == SEQ // tm - 1)
        def _():
            rows = Q_LORA_RANK // N_DEV
            for j in range(N_DEV):
                for h in range(MLA_HEADS):
                    puq_ref[j, :, QK_HEAD_DIM * h:QK_HEAD_DIM * (h + 1)] = dwq_ref[
                        rows * j:rows * (j + 1), HEAD_PAD * h:HEAD_PAD * h + QK_HEAD_DIM].astype(BF16)

    full = lambda shape: pl.BlockSpec(shape, lambda i: (0, 0))
    row = lambda w, c=0: pl.BlockSpec((tm, w), lambda i, c=c: (i, c))
    return pl.pallas_call(
        body, name="mla_bwd", grid=(SEQ // tm,),
        in_specs=[row(hw), row(hw), row(MLA_WIDTH), row(CQ_PAD, 0), row(LANES, CQ_PAD // LANES),
                  full((1, CQ_PAD)), full((1, KV_LORA_RANK)), full((CQ_PAD, hw)), full((KV_LORA_RANK, hw)),
                  full((KV_LORA_RANK, MLA_WIDTH)), row(LANES), row(LANES), row(LANES)],
        out_specs=[row(SEG_C), pl.BlockSpec((N_DEV, Q_LORA_RANK // N_DEV, MLA_HEADS * QK_HEAD_DIM), lambda i: (0, 0, 0)),
                   full((KV_LORA_RANK, hw)), full((KV_LORA_RANK, MLA_WIDTH)),
                   full((1, CQ_PAD)), full((1, KV_LORA_RANK))],
        out_shape=[jax.ShapeDtypeStruct((SEQ, SEG_C), F32),
                   jax.ShapeDtypeStruct((N_DEV, Q_LORA_RANK // N_DEV, MLA_HEADS * QK_HEAD_DIM), BF16),
                   jax.ShapeDtypeStruct((KV_LORA_RANK, hw), F32), jax.ShapeDtypeStruct((KV_LORA_RANK, MLA_WIDTH), F32),
                   jax.ShapeDtypeStruct((1, CQ_PAD), F32), jax.ShapeDtypeStruct((1, KV_LORA_RANK), F32)],
        scratch_shapes=[pltpu.VMEM((tm, hw), BF16), pltpu.VMEM((CQ_PAD, hw), F32)],
        compiler_params=pltpu.CompilerParams(dimension_semantics=("arbitrary",), vmem_limit_bytes=VMEM_MID),
    )(dq, dk, dv, h_c, h_c, gq, gkv, wq, wkn, wv, c_t, sa_t, sb_t)


def _adamw_all(ws, gs, ms, vs):
    n = len(ws)
    c1 = 1.0 / (1.0 - ADAM_B1 ** ADAM_STEP)
    c2 = 1.0 / (1.0 - ADAM_B2 ** ADAM_STEP)

    def body(*refs):
        for idx in range(n):
            w, g, m, v = (refs[idx][...], refs[n + idx][...], refs[2 * n + idx][...], refs[3 * n + idx][...])
            m_new = ADAM_B1 * m + (1.0 - ADAM_B1) * g
            v_new = ADAM_B2 * v + (1.0 - ADAM_B2) * (g * g)
            delta = -ADAM_LR * ((m_new * c1) / (jnp.sqrt(v_new * c2) + ADAM_EPS) + ADAM_WD * w)
            refs[4 * n + idx][...] = delta
            refs[5 * n + idx][...] = m_new
            refs[6 * n + idx][...] = v_new

    shapes = [jax.ShapeDtypeStruct(w.shape, F32) for w in ws]
    outs = pl.pallas_call(
        body, name="adamw", out_shape=shapes * 3,
        compiler_params=pltpu.CompilerParams(vmem_limit_bytes=VMEM_BIG),
    )(*ws, *gs, *ms, *vs)
    return outs[:n], outs[n:2 * n], outs[2 * n:]


def _all_gather(block, name):
    rows, width = block.shape

    def body(x_ref, out_ref, send_sems, recv_sems, local_sem):
        x, y, c = lax.axis_index("x"), lax.axis_index("y"), lax.axis_index("c")
        me, sibling = (x, y, c), (x, y, 1 - c)
        chips = [(1 - x, y), (x, 1 - y), (1 - x, 1 - y)]

        def slab(px, py, pc):
            return out_ref.at[4 * px + 2 * py + pc]

        def copy(k, blk, to, src=None):
            return pltpu.make_async_remote_copy(
                src_ref=slab(*blk) if src is None else src, dst_ref=slab(*blk),
                send_sem=send_sems.at[k], recv_sem=recv_sems.at[k],
                device_id=to, device_id_type=pl.DeviceIdType.MESH)

        mine = pltpu.make_async_copy(x_ref, slab(*me), local_sem)
        mine.start()
        first = [copy(0, me, sibling, src=x_ref)]
        first += [copy(1 + j, me, (*chip, c), src=x_ref) for j, chip in enumerate(chips)]
        for cp in first:
            cp.start()
        passed = [copy(4 + j, (*chip, c), sibling) for j, chip in enumerate(chips)]
        for j, chip in enumerate(chips):
            copy(1 + j, (*chip, c), me).wait_recv()
            passed[j].start()
        copy(0, sibling, me).wait_recv()
        for j, chip in enumerate(chips):
            copy(4 + j, (*chip, 1 - c), me).wait_recv()
        for cp in first + passed:
            cp.wait_send()
        mine.wait()

    return pl.pallas_call(
        body, name=name,
        out_shape=jax.ShapeDtypeStruct((N_DEV, rows, width), block.dtype),
        in_specs=[pl.BlockSpec(memory_space=pltpu.VMEM)],
        out_specs=pl.BlockSpec(memory_space=pltpu.VMEM),
        scratch_shapes=[pltpu.SemaphoreType.DMA((7,)), pltpu.SemaphoreType.DMA((7,)), pltpu.SemaphoreType.DMA],
        compiler_params=pltpu.CompilerParams(vmem_limit_bytes=VMEM_MID),
    )(block)


SHARD_W = IN_WIDTH // N_DEV

_PIECES = [(0, 384, 2, 0), (384, 512, 2, CQ_PAD), (512, 544, 2, CQ_PAD + LANES + ROPE_LO),
           (544, 1056, 0, 2 * D_MODEL), (1056, 1568, 1, 0), (1568, 2080, 1, SGU_WIDTH),
           (2080, 2592, 1, 2 * SGU_WIDTH), (2592, 3616, 0, 0), (3616, 4640, 0, D_MODEL)]


def _column_runs():
    runs = []
    for n0, n1, seg, d0 in _PIECES:
        for j in range(N_DEV):
            lo, hi = max(n0, j * SHARD_W), min(n1, (j + 1) * SHARD_W)
            if lo < hi:
                runs.append((j, lo - j * SHARD_W, hi - j * SHARD_W, seg, d0 + lo - n0))
    return runs


def _mesh_pos():
    return lax.axis_index("x"), lax.axis_index("y"), lax.axis_index("c")


def _remote(src, dst, send_sems, recv_sems, k, to):
    return pltpu.make_async_remote_copy(src_ref=src, dst_ref=dst, send_sem=send_sems.at[k], recv_sem=recv_sems.at[k],
                                        device_id=to, device_id_type=pl.DeviceIdType.MESH)


def _gather_exchange(gats, send_sems, recv_sems):
    x, y, c = _mesh_pos()
    me, sibling = (x, y, c), (x, y, 1 - c)
    chips = [(1 - x, y), (x, 1 - y), (1 - x, 1 - y)]

    def copy(a, k, blk, to):
        slab = gats[a].at[4 * blk[0] + 2 * blk[1] + blk[2]]
        return _remote(slab, slab, send_sems, recv_sems, 7 * a + k, to)

    arrays = range(len(gats))
    first = [copy(a, 1 + j, me, (*chip, c)) for j, chip in enumerate(chips) for a in arrays]
    first += [copy(a, 0, me, sibling) for a in arrays]
    for cp in first:
        cp.start()
    passed = []
    for j, chip in enumerate(chips):
        for a in arrays:
            copy(a, 1 + j, (*chip, c), me).wait_recv()
            fwd = copy(a, 4 + j, (*chip, c), sibling)
            fwd.start()
            passed.append(fwd)
    for a in arrays:
        copy(a, 0, sibling, me).wait_recv()
    for j, chip in enumerate(chips):
        for a in arrays:
            copy(a, 4 + j, (*chip, 1 - c), me).wait_recv()
    for cp in first + passed:
        cp.wait_send()


def _gather_weights(w_in, w_uq2, w_oa, w_ob, w_out):
    hw = MLA_HEADS * HEAD_PAD
    uq_rows = Q_LORA_RANK // N_DEV
    cols = D_MODEL // N_DEV

    def body(win_ref, wuq_ref, woa_ref, wob_ref, wout_ref, wa_ref, wb_ref, wc_ref, wq_ref, oa_ref, ob_ref, out_ref,
             g_in, g_uq, g_oa, g_ob, g_out, send_sems, recv_sems):
        x, y, c = _mesh_pos()
        me = 4 * x + 2 * y + c
        g_in[me] = win_ref[0].astype(BF16)
        g_uq[me] = wuq_ref[...].astype(BF16)
        g_oa[me] = woa_ref[0].astype(BF16)
        g_ob[me] = wob_ref[0].astype(BF16)
        g_out[me] = wout_ref[0].astype(BF16)
        _gather_exchange([g_in, g_uq, g_oa, g_ob, g_out], send_sems, recv_sems)

        segs = [wa_ref, wb_ref, wc_ref]
        for j, s0, s1, seg, d0 in _column_runs():
            segs[seg][:, d0:d0 + (s1 - s0)] = g_in[j, :, s0:s1]
        zeros = lambda r, w: jnp.zeros((r, w), BF16)
        wc_ref[:, Q_LORA_RANK:CQ_PAD] = zeros(D_MODEL, CQ_PAD - Q_LORA_RANK)
        wc_ref[:, CQ_PAD + LANES:CQ_PAD + LANES + ROPE_LO] = zeros(D_MODEL, ROPE_LO)
        wc_ref[:, CQ_PAD + LANES + ROPE_HI:SEG_C] = zeros(D_MODEL, LANES - ROPE_HI)
        wq_ref[Q_LORA_RANK:CQ_PAD, :] = zeros(CQ_PAD - Q_LORA_RANK, hw)
        for h in range(MLA_HEADS):
            wq_ref[0:Q_LORA_RANK, HEAD_PAD * h + QK_HEAD_DIM:HEAD_PAD * (h + 1)] = zeros(Q_LORA_RANK, HEAD_PAD - QK_HEAD_DIM)
        for j in range(N_DEV):
            for h in range(MLA_HEADS):
                wq_ref[uq_rows * j:uq_rows * (j + 1), HEAD_PAD * h:HEAD_PAD * h + QK_HEAD_DIM] = g_uq[
                    j, :, QK_HEAD_DIM * h:QK_HEAD_DIM * (h + 1)]
            oa_ref[:, cols * j:cols * (j + 1)] = g_oa[j]
            ob_ref[:, cols * j:cols * (j + 1)] = g_ob[j]
            out_ref[cols * j:cols * (j + 1), :] = g_out[j]

    vmem = pl.BlockSpec(memory_space=pltpu.VMEM)
    return pl.pallas_call(
        body, name="gather_weights",
        out_shape=[jax.ShapeDtypeStruct((D_MODEL, SEG_A), BF16), jax.ShapeDtypeStruct((D_MODEL, SEG_B), BF16),
                   jax.ShapeDtypeStruct((D_MODEL, SEG_C), BF16), jax.ShapeDtypeStruct((CQ_PAD, hw), BF16),
                   jax.ShapeDtypeStruct((MLA_WIDTH, D_MODEL), BF16), jax.ShapeDtypeStruct((SGU_WIDTH, D_MODEL), BF16),
                   jax.ShapeDtypeStruct((D_MODEL, D_MODEL), BF16)],
        in_specs=[vmem] * 5, out_specs=[vmem] * 7,
        scratch_shapes=[pltpu.VMEM((N_DEV, D_MODEL, SHARD_W), BF16),
                        pltpu.VMEM((N_DEV, uq_rows, MLA_HEADS * QK_HEAD_DIM), BF16),
                        pltpu.VMEM((N_DEV, MLA_WIDTH, cols), BF16), pltpu.VMEM((N_DEV, SGU_WIDTH, cols), BF16),
                        pltpu.VMEM((N_DEV, cols, D_MODEL), BF16),
                        pltpu.SemaphoreType.DMA((35,)), pltpu.SemaphoreType.DMA((35,))],
        compiler_params=pltpu.CompilerParams(vmem_limit_bytes=VMEM_BIG),
    )(w_in, w_uq2, w_oa, w_ob, w_out)


def _reduce_grads(dwa, dwb, dwc, p_uq, p_oa, p_ob, p_out, p_rep):
    rep_rows = p_rep.shape[1]
    spec = [((D_MODEL, SHARD_W), BF16, 256), (p_uq.shape[1:], BF16, p_uq.shape[1]), (p_oa.shape[1:], BF16, 256),
            (p_ob.shape[1:], BF16, 256), (p_out.shape[1:], BF16, 64), ((rep_rows, LANES), F32, rep_rows)]
    n = len(spec)

    def body(dwa_ref, dwb_ref, dwc_ref, puq_ref, poa_ref, pob_ref, pout_ref, prep_ref,
             gin_ref, guq_ref, goa_ref, gob_ref, gout_ref, grep_ref, pin_ref, *rest):
        ras, tbs, rbs = rest[0:n], rest[n:2 * n], rest[2 * n:3 * n]
        send_sems, recv_sems = rest[3 * n], rest[3 * n + 1]
        x, y, c = _mesh_pos()
        sibling = (x, y, 1 - c)
        parts = [pin_ref, puq_ref, poa_ref, pob_ref, pout_ref, prep_ref]
        outs = [gin_ref.at[0], guq_ref, goa_ref.at[0], gob_ref.at[0], gout_ref.at[0], grep_ref]

        segs = [dwa_ref, dwb_ref, dwc_ref]
        for j, s0, s1, seg, d0 in _column_runs():
            pin_ref[j, :, s0:s1] = segs[seg][:, d0:d0 + (s1 - s0)]

        stage1 = []
        for chip in range(4):
            for a in range(n):
                cp = _remote(parts[a].at[2 * chip + (1 - c)], ras[a].at[chip], send_sems, recv_sems, 7 * a + chip, sibling)
                cp.start()
                stage1.append(cp)
        for cp in stage1:
            cp.wait_recv()

        def rows_loop(a, fn):
            rows, chunk = spec[a][0][0], spec[a][2]
            if rows == chunk:
                fn(pl.ds(0, rows))
            else:
                def step(i, carry):
                    fn(pl.ds(pl.multiple_of(i * chunk, chunk), chunk))
                    return carry
                lax.fori_loop(0, rows // chunk, step, 0)

        def chip_sum(a, chip, sl):
            return parts[a][2 * chip + c, sl, :].astype(F32) + ras[a][chip, sl, :].astype(F32)

        others = [(1 - x, y), (x, 1 - y), (1 - x, 1 - y)]
        stage2 = []
        for k, (cx, cy) in enumerate(others):
            for a in range(n):
                def fill(sl, a=a, k=k, chip=2 * cx + cy):
                    tbs[a][k, sl, :] = chip_sum(a, chip, sl).astype(spec[a][1])
                rows_loop(a, fill)
                cp = _remote(tbs[a].at[k], rbs[a].at[k], send_sems, recv_sems, 7 * a + 4 + k, (cx, cy, c))
                cp.start()
                stage2.append(cp)
        for cp in stage2:
            cp.wait_recv()
        for a in range(n):
            def final(sl, a=a):
                acc = chip_sum(a, 2 * x + y, sl)
                for k in range(3):
                    acc = acc + rbs[a][k, sl, :].astype(F32)
                outs[a][sl, :] = acc
            rows_loop(a, final)
        for cp in stage1 + stage2:
            cp.wait_send()

    vmem = pl.BlockSpec(memory_space=pltpu.VMEM)
    scratch = [pltpu.VMEM((N_DEV, D_MODEL, SHARD_W), BF16)]
    for lead in (4, 3, 3):
        scratch += [pltpu.VMEM((lead,) + tuple(shape), dt) for shape, dt, _ in spec]
    scratch += [pltpu.SemaphoreType.DMA((7 * n,)), pltpu.SemaphoreType.DMA((7 * n,))]
    return pl.pallas_call(
        body, name="reduce_grads",
        out_shape=[jax.ShapeDtypeStruct((1, D_MODEL, SHARD_W), F32), jax.ShapeDtypeStruct(p_uq.shape[1:], F32),
                   jax.ShapeDtypeStruct((1,) + p_oa.shape[1:], F32), jax.ShapeDtypeStruct((1,) + p_ob.shape[1:], F32),
                   jax.ShapeDtypeStruct((1,) + p_out.shape[1:], F32), jax.ShapeDtypeStruct((rep_rows, LANES), F32)],
        in_specs=[vmem] * 8, out_specs=[vmem] * 6, scratch_shapes=scratch,
        compiler_params=pltpu.CompilerParams(vmem_limit_bytes=VMEM_BIG),
    )(dwa, dwb, dwc, p_uq, p_oa, p_ob, p_out, p_rep)


_O_CQ, _O_CKV, _O_KPE, _O_ZA, _O_U, _O_V, _O_ZB, _O_GA, _O_GB = 0, 384, 512, 544, 1056, 1568, 2080, 2592, 3616


def _to_segments(w):
    z = lambda n: jnp.zeros(w.shape[:-1] + (n,), w.dtype)
    seg_a = jnp.concatenate([w[..., _O_GA:_O_GB], w[..., _O_GB:IN_WIDTH], w[..., _O_ZA:_O_U]], axis=-1)
    seg_b = jnp.concatenate([w[..., _O_U:_O_V], w[..., _O_V:_O_ZB], w[..., _O_ZB:_O_GA]], axis=-1)
    seg_c = jnp.concatenate([w[..., _O_CQ:_O_CKV], z(CQ_PAD - Q_LORA_RANK), w[..., _O_CKV:_O_KPE],
                             z(ROPE_LO), w[..., _O_KPE:_O_ZA], z(LANES - ROPE_HI)], axis=-1)
    return seg_a, seg_b, seg_c


def _from_segments(seg_a, seg_b, seg_c):
    kpe0 = CQ_PAD + LANES + ROPE_LO
    return jnp.concatenate([
        seg_c[..., 0:Q_LORA_RANK], seg_c[..., CQ_PAD:CQ_PAD + LANES], seg_c[..., kpe0:kpe0 + QK_ROPE_DIM],
        seg_a[..., 2 * D_MODEL:SEG_A], seg_b, seg_a[..., 0:2 * D_MODEL]], axis=-1)


def kernel(x, positions, w_in, b_in, g_q, w_uq, g_kv, w_ukv, w_oa, sgu_ln_g, sgu_ln_b, w_s, b_s, w_ob, w_out, ln_g, ln_b, loss_target, m_w_in, m_b_in, m_g_q, m_w_uq, m_g_kv, m_w_ukv, m_w_oa, m_sgu_ln_g, m_sgu_ln_b, m_w_s, m_b_s, m_w_ob, m_w_out, m_ln_g, m_ln_b, v_w_in, v_b_in, v_g_q, v_w_uq, v_g_kv, v_w_ukv, v_w_oa, v_sgu_ln_g, v_sgu_ln_b, v_w_s, v_b_s, v_w_ob, v_w_out, v_ln_g, v_ln_b):
    w_uq2 = w_uq[0].reshape(Q_LORA_RANK // N_DEV, MLA_HEADS * QK_HEAD_DIM)
    wa, wb, wc, wq, w_oa_f, w_ob_f, w_out_f = _gather_weights(w_in, w_uq2, w_oa, w_ob, w_out)
    partials = _local_step(x[0], positions, loss_target[0], wa, wb, wc, b_in, g_q, wq, g_kv, w_ukv, w_oa_f, sgu_ln_g,
                           sgu_ln_b, w_s, b_s, w_ob_f, w_out_f, ln_g, ln_b)
    weights = dict(w_in=w_in, b_in=b_in, g_q=g_q, w_uq=w_uq, g_kv=g_kv, w_ukv=w_ukv, w_oa=w_oa, sgu_ln_g=sgu_ln_g,
                   sgu_ln_b=sgu_ln_b, w_s=w_s, b_s=b_s, w_ob=w_ob, w_out=w_out, ln_g=ln_g, ln_b=ln_b)
    moms = dict(w_in=m_w_in, b_in=m_b_in, g_q=m_g_q, w_uq=m_w_uq, g_kv=m_g_kv, w_ukv=m_w_ukv, w_oa=m_w_oa,
                sgu_ln_g=m_sgu_ln_g, sgu_ln_b=m_sgu_ln_b, w_s=m_w_s, b_s=m_b_s, w_ob=m_w_ob, w_out=m_w_out,
                ln_g=m_ln_g, ln_b=m_ln_b)
    vars_ = dict(w_in=v_w_in, b_in=v_b_in, g_q=v_g_q, w_uq=v_w_uq, g_kv=v_g_kv, w_ukv=v_w_ukv, w_oa=v_w_oa,
                 sgu_ln_g=v_sgu_ln_g, sgu_ln_b=v_sgu_ln_b, w_s=v_w_s, b_s=v_b_s, w_ob=v_w_ob, w_out=v_w_out,
                 ln_g=v_ln_g, ln_b=v_ln_b)
    return _reduce_and_update(partials, weights, moms, vars_)


def _local_step(x2, positions, tgt, wa, wb, wc, b_in, g_q, wq, g_kv, w_ukv, w_oa_f, sgu_ln_g, sgu_ln_b, w_s, b_s,
                w_ob_f, w_out_f, ln_g, ln_b):
    ba, bb, bc = _to_segments(b_in)
    w_ukv_bf = w_ukv[0].astype(BF16)
    wkn = jnp.pad(w_ukv_bf[:, :, :QK_NOPE_DIM], ((0, 0), (0, 0), (0, HEAD_PAD - QK_NOPE_DIM))).reshape(KV_LORA_RANK, -1)
    wv = w_ukv_bf[:, :, QK_NOPE_DIM:].reshape(KV_LORA_RANK, MLA_WIDTH)
    gq = jnp.pad(g_q, ((0, 0), (0, CQ_PAD - Q_LORA_RANK)))
    bias_full = jnp.repeat(b_s[0].T, SGU_GROUP_DIM, axis=1)
    w_s3 = w_s[0]
    w_st3 = jnp.swapaxes(w_s3, 1, 2)

    inv_freq = ROPE_THETA ** (-jnp.arange(0, QK_ROPE_DIM, 2, dtype=F32) / QK_ROPE_DIM)
    invf_lane = jnp.concatenate([jnp.zeros((ROPE_LO,), F32), inv_freq, inv_freq,
                                 jnp.zeros((LANES - ROPE_HI,), F32)]).reshape(1, LANES)
    c_t, sa_t, sb_t = _rope_tables(positions.reshape(SEQ, 1), invf_lane)

    h_a = _mm(x2, wa, bias=ba, tm=512, tn=512, tk=D_MODEL, name="in_proj_a")
    h_b = _mm(x2, wb, bias=bb, tm=512, tn=512, tk=D_MODEL, name="in_proj_b")
    h_c = _mm(x2, wc, bias=bc, tm=512, tn=SEG_C, tk=D_MODEL, name="in_proj_c")
    q, k, v = _mla_prep(h_c, gq, g_kv, wq, wkn, wv, c_t, sa_t, sb_t)
    o, lse = _attn_fwd(q, k, v)
    y_b = _sgu_fwd(h_b, sgu_ln_g, sgu_ln_b, w_s3, bias_full)

    (loss_row, dx_res, dh_a, d_o, d_yb, p_oa, p_ob, p_out, d_lng, d_lnb) = _merge(
        x2, o, h_a, y_b, tgt, w_oa_f, w_ob_f, w_out_f, ln_g, ln_b)
    dh_b, d_ws, d_bs_t, d_slg, d_slb = _sgu_bwd(h_b, d_yb, sgu_ln_g, sgu_ln_b, w_s3, w_st3, bias_full)
    dq, dk, dv = _attn_bwd(q, k, v, d_o, o, lse)
    dh_c, p_uq, d_wkn, d_wv, d_gq, d_gkv = _mla_bwd(dq, dk, dv, h_c, gq, g_kv, wq, wkn, wv, c_t, sa_t, sb_t)

    dx = _mm(dh_a, wa, tb=True, add=dx_res, tm=512, tn=D_MODEL, tk=512, name="dx_a")
    dx = _mm(dh_b, wb, tb=True, add=dx, tm=512, tn=D_MODEL, tk=512, name="dx_b")
    dx = _mm(dh_c, wc, tb=True, add=dx, tm=512, tn=D_MODEL, tk=SEG_C, name="dx_c")
    d_wa, d_ba = _mm(x2, dh_a, ta=True, colsum=True, out_dtype=BF16, tm=512, tn=512, tk=512, name="dw_in_a")
    d_wb, d_bb = _mm(x2, dh_b, ta=True, colsum=True, out_dtype=BF16, tm=512, tn=512, tk=512, name="dw_in_b")
    d_wc, d_bc = _mm(x2, dh_c, ta=True, colsum=True, out_dtype=BF16, tm=512, tn=SEG_C, tk=512, name="dw_in_c")

    p_b_in = _from_segments(d_ba, d_bb, d_bc)
    p_w_ukv = jnp.concatenate([d_wkn.reshape(KV_LORA_RANK, MLA_HEADS, HEAD_PAD)[:, :, :QK_NOPE_DIM],
                               d_wv.reshape(KV_LORA_RANK, MLA_HEADS, V_HEAD_DIM)], axis=-1)
    p_g_q = d_gq[:, :Q_LORA_RANK]
    p_b_s = d_bs_t[:, :SGU_GROUPS].T
    sharded = (d_wa, d_wb, d_wc, p_uq, p_oa, p_ob, p_out)
    replicated = [p_b_in, p_g_q, d_gkv, p_w_ukv, d_slg, d_slb, d_ws, p_b_s, d_lng, d_lnb]
    return loss_row, dx, sharded, replicated


_NAMES = ["w_in", "b_in", "g_q", "w_uq", "g_kv", "w_ukv", "w_oa", "sgu_ln_g", "sgu_ln_b", "w_s", "b_s", "w_ob",
          "w_out", "ln_g", "ln_b"]
_REPLICATED = ["b_in", "g_q", "g_kv", "w_ukv", "sgu_ln_g", "sgu_ln_b", "w_s", "b_s", "ln_g", "ln_b"]


def _reduce_and_update(partials, weights, moms, vars_):
    loss_row, dx, sharded, replicated = partials
    rep_flat = jnp.concatenate([a.reshape(-1) for a in replicated] + [loss_row[0, :1]])
    rep_flat = jnp.pad(rep_flat, (0, N_DEV * PACK_R_ROWS * LANES - rep_flat.size))
    g_in, g_uq, g_oa, g_ob, g_out, rep_slice = _reduce_grads(*sharded, rep_flat.reshape(N_DEV, PACK_R_ROWS, LANES))
    rep_sum = _all_gather(rep_slice, "gather_replicated").reshape(-1)
    grads, pos = dict(w_in=g_in, w_uq=g_uq, w_oa=g_oa, w_ob=g_ob, w_out=g_out), 0
    for nm in _REPLICATED:
        grads[nm] = rep_sum[pos:pos + weights[nm].size]
        pos += weights[nm].size
    loss = rep_sum[pos]
    grads = {nm: grads[nm].reshape(weights[nm].shape) for nm in _NAMES}
    deltas, new_m, new_v = _adamw_all([weights[nm] for nm in _NAMES], [grads[nm] for nm in _NAMES],
                                      [moms[nm] for nm in _NAMES], [vars_[nm] for nm in _NAMES])
    return (loss, dx.reshape(1, SEQ, D_MODEL), *[grads[nm] for nm in _NAMES], *deltas, *new_m, *new_v)
```

```python
import math

import jax
import jax.numpy as jnp
from jax import lax
from jax.experimental import pallas as pl
from jax.experimental.pallas import tpu as pltpu

F32 = jnp.float32
BF16 = jnp.bfloat16

D_MODEL = 1024
SEQ = 2048
N_DEV = 8
MLA_HEADS = 8
Q_LORA_RANK = 384
KV_LORA_RANK = 128
QK_NOPE_DIM = 64
QK_ROPE_DIM = 32
V_HEAD_DIM = 64
QK_HEAD_DIM = QK_NOPE_DIM + QK_ROPE_DIM
MLA_WIDTH = MLA_HEADS * V_HEAD_DIM
ROPE_THETA = 10000.0
SGU_GROUPS = 8
SGU_GROUP_DIM = 64
SGU_WIDTH = SGU_GROUPS * SGU_GROUP_DIM
CHUNK = 128
RMS_EPS = 1e-6
LN_EPS = 1e-5
DN_ALPHA = 2.0 ** 0.25
IN_WIDTH = 4640
ATTN_SCALE = QK_HEAD_DIM ** -0.5

ADAM_LR = 0.001
ADAM_B1 = 0.9
ADAM_B2 = 0.999
ADAM_EPS = 1e-08
ADAM_WD = 0.01
ADAM_STEP = 10

LANES = 128
HEAD_PAD = 128
ROPE_LO = QK_NOPE_DIM
ROPE_MID = ROPE_LO + QK_ROPE_DIM // 2
ROPE_HI = ROPE_LO + QK_ROPE_DIM
CQ_PAD = 512

SEG_A = 2560
SEG_B = 1536
SEG_C = 768

PACK_R_ROWS = 272
VMEM_BIG = 56 * 1024 * 1024
VMEM_MID = 40 * 1024 * 1024


def _sigmoid(x):
    return 1.0 / (1.0 + jnp.exp(-x))


def _gelu_and_grad(x):
    c0 = math.sqrt(2.0 / math.pi)
    x2 = x * x
    t = jnp.tanh(c0 * (x + 0.044715 * x * x2))
    g = 0.5 * x * (1.0 + t)
    dg = 0.5 * (1.0 + t) + 0.5 * x * (1.0 - t * t) * (c0 * (1.0 + 3.0 * 0.044715 * x2))
    return g, dg


def _dot(a, b, dims):
    return lax.dot_general(a, b, (dims, ((), ())), preferred_element_type=F32)


_NN = ((1,), (0,))
_NT = ((1,), (1,))
_TN = ((0,), (0,))


def _mm(a, b, *, ta=False, tb=False, bias=None, add=None, colsum=False, out_dtype=F32, tm, tn, tk, name):
    m, k = (a.shape[1], a.shape[0]) if ta else a.shape
    n = b.shape[0] if tb else b.shape[1]
    assert m % tm == 0 and n % tn == 0 and k % tk == 0
    nk = k // tk
    dims = ((0,) if ta else (1,), (1,) if tb else (0,))

    def body(*refs):
        a_ref, b_ref = refs[0], refs[1]
        pos = 2
        bias_ref = add_ref = cs_ref = None
        if bias is not None:
            bias_ref = refs[pos]; pos += 1
        if add is not None:
            add_ref = refs[pos]; pos += 1
        o_ref = refs[pos]; pos += 1
        if colsum:
            cs_ref = refs[pos]; pos += 1
        acc_ref = refs[pos]
        i, kk = pl.program_id(1), pl.program_id(2)

        @pl.when(kk == 0)
        def _():
            acc_ref[...] = jnp.zeros_like(acc_ref)

        bv = b_ref[...]
        acc_ref[...] += _dot(a_ref[...].astype(BF16), bv.astype(BF16), dims)

        if colsum:
            @pl.when(jnp.logical_and(i == 0, kk == 0))
            def _():
                cs_ref[...] = jnp.zeros_like(cs_ref)

            @pl.when(i == 0)
            def _():
                cs_ref[...] += jnp.sum(bv.astype(F32), axis=0, keepdims=True)

        @pl.when(kk == nk - 1)
        def _():
            r = acc_ref[...]
            if bias is not None:
                r = r + bias_ref[...]
            if add is not None:
                r = r + add_ref[...]
            o_ref[...] = r.astype(out_dtype)

    a_spec = pl.BlockSpec((tk, tm), lambda j, i, kk: (kk, i)) if ta else pl.BlockSpec((tm, tk), lambda j, i, kk: (i, kk))
    b_spec = pl.BlockSpec((tn, tk), lambda j, i, kk: (j, kk)) if tb else pl.BlockSpec((tk, tn), lambda j, i, kk: (kk, j))
    in_specs, args = [a_spec, b_spec], [a, b]
    if bias is not None:
        in_specs.append(pl.BlockSpec((1, tn), lambda j, i, kk: (0, j))); args.append(bias)
    if add is not None:
        in_specs.append(pl.BlockSpec((tm, tn), lambda j, i, kk: (i, j))); args.append(add)
    out_shape = [jax.ShapeDtypeStruct((m, n), out_dtype)]
    out_specs = [pl.BlockSpec((tm, tn), lambda j, i, kk: (i, j))]
    if colsum:
        out_shape.append(jax.ShapeDtypeStruct((1, n), F32))
        out_specs.append(pl.BlockSpec((1, tn), lambda j, i, kk: (0, j)))
    res = pl.pallas_call(
        body, name=name, grid=(n // tn, m // tm, nk), in_specs=in_specs, out_specs=out_specs, out_shape=out_shape,
        scratch_shapes=[pltpu.VMEM((tm, tn), F32)],
        compiler_params=pltpu.CompilerParams(dimension_semantics=("arbitrary", "arbitrary", "arbitrary"),
                                             vmem_limit_bytes=VMEM_MID),
    )(*args)
    return res if colsum else res[0]


def _rope_tables(pos_col, invf_lane):
    def body(pos_ref, invf_ref, c_ref, sa_ref, sb_ref):
        ang = pos_ref[...].astype(F32) * invf_ref[...]
        cs, sn = jnp.cos(ang), jnp.sin(ang)
        lane = lax.broadcasted_iota(jnp.int32, ang.shape, 1)
        c_ref[...] = jnp.where(lane < ROPE_LO, 1.0, jnp.where(lane < ROPE_HI, cs, 0.0))
        sa_ref[...] = jnp.where(jnp.logical_and(lane >= ROPE_LO, lane < ROPE_MID), -sn, 0.0)
        sb_ref[...] = jnp.where(jnp.logical_and(lane >= ROPE_MID, lane < ROPE_HI), sn, 0.0)

    shp = jax.ShapeDtypeStruct((SEQ, LANES), F32)
    return pl.pallas_call(body, name="rope_tables", out_shape=[shp, shp, shp])(pos_col, invf_lane)


def _rope(x, c, sa, sb):
    return x * c + pltpu.roll(x, LANES - 16, 1) * sa + pltpu.roll(x, 16, 1) * sb


def _rope_t(dy, c, sa, sb):
    return dy * c + pltpu.roll(dy * sa, 16, 1) + pltpu.roll(dy * sb, LANES - 16, 1)


def _mla_prep(h_c, gq, gkv, wq, wkn, wvx, c_t, sa_t, sb_t):
    tm = 256
    hw = MLA_HEADS * HEAD_PAD

    def body(cq_ref, ckv_ref, kpe_ref, gq_ref, gkv_ref, wq_ref, wkn_ref, wvx_ref, c_ref, sa_ref, sb_ref,
             q_ref, k_ref, kt_ref, vx_ref, vxt_ref):
        c, sa, sb = c_ref[...], sa_ref[...], sb_ref[...]
        cq = cq_ref[...]
        rq = lax.rsqrt(jnp.sum(cq * cq, axis=1, keepdims=True) * (1.0 / Q_LORA_RANK) + RMS_EPS)
        cqn = ((cq * rq) * gq_ref[...]).astype(BF16)
        qall = _dot(cqn, wq_ref[...], _NN)
        for h in range(MLA_HEADS):
            sl = slice(HEAD_PAD * h, HEAD_PAD * (h + 1))
            q_ref[:, sl] = (_rope(qall[:, sl], c, sa, sb) * ATTN_SCALE).astype(BF16)
        ckv = ckv_ref[...]
        rkv = lax.rsqrt(jnp.sum(ckv * ckv, axis=1, keepdims=True) * (1.0 / KV_LORA_RANK) + RMS_EPS)
        ckvn = ((ckv * rkv) * gkv_ref[...]).astype(BF16)
        knall = _dot(ckvn, wkn_ref[...], _NN)
        vall = _dot(ckvn, wvx_ref[...], _NN)
        kper = _rope(kpe_ref[...], c, sa, sb)
        ones_half = (lax.broadcasted_iota(jnp.int32, (tm, HEAD_PAD), 1) >= V_HEAD_DIM).astype(F32)
        for h in range(MLA_HEADS):
            sl = slice(HEAD_PAD * h, HEAD_PAD * (h + 1))
            kh = knall[:, sl] + kper
            vh = vall[:, sl] + ones_half
            k_ref[:, sl] = kh.astype(BF16)
            kt_ref[sl, :] = kh.T.astype(BF16)
            vx_ref[:, sl] = vh.astype(BF16)
            vxt_ref[sl, :] = vh.T.astype(BF16)

    full = lambda shape: pl.BlockSpec(shape, lambda i: (0, 0))
    tab = pl.BlockSpec((tm, LANES), lambda i: (i, 0))
    row = pl.BlockSpec((tm, hw), lambda i: (i, 0))
    col = pl.BlockSpec((hw, tm), lambda i: (0, i))
    return pl.pallas_call(
        body, name="mla_prep", grid=(SEQ // tm,),
        in_specs=[pl.BlockSpec((tm, CQ_PAD), lambda i: (i, 0)),
                  pl.BlockSpec((tm, LANES), lambda i: (i, CQ_PAD // LANES)),
                  pl.BlockSpec((tm, LANES), lambda i: (i, CQ_PAD // LANES + 1)),
                  full((1, CQ_PAD)), full((1, KV_LORA_RANK)),
                  full((CQ_PAD, hw)), full((KV_LORA_RANK, hw)), full((KV_LORA_RANK, hw)), tab, tab, tab],
        out_specs=[row, row, col, row, col],
        out_shape=[jax.ShapeDtypeStruct((SEQ, hw), BF16), jax.ShapeDtypeStruct((SEQ, hw), BF16),
                   jax.ShapeDtypeStruct((hw, SEQ), BF16), jax.ShapeDtypeStruct((SEQ, hw), BF16),
                   jax.ShapeDtypeStruct((hw, SEQ), BF16)],
        compiler_params=pltpu.CompilerParams(dimension_semantics=("arbitrary",), vmem_limit_bytes=VMEM_MID),
    )(h_c, h_c, h_c, gq, gkv, wq, wkn, wvx, c_t, sa_t, sb_t)


ATT_T = 512
ATT_STRIP = 64


def _attn_fwd(q, kt, vx):
    t, rs = ATT_T, ATT_STRIP

    def body(q_ref, kt_ref, vx_ref, o_ref, l_ref, s_scr, p_scr, m_scr, a_scr, acc_scr):
        qi = pl.program_id(1)
        lane = lax.broadcasted_iota(jnp.int32, (t, LANES), 1)
        m_scr[...] = jnp.full((2, t, LANES), -1e30, F32)
        acc_scr[...] = jnp.zeros((2, t, LANES), F32)

        def block(j, masked):
            off = pl.multiple_of(j * t, t)
            for a in range(2):
                sl = slice(HEAD_PAD * a, HEAD_PAD * (a + 1))
                s_scr[a] = _dot(q_ref[:, sl], kt_ref[sl, pl.ds(off, t)], _NN)
                for r in range(t // rs):
                    rows = slice(rs * r, rs * (r + 1))
                    s = s_scr[a, rows, :]
                    if masked:
                        rowi = lax.broadcasted_iota(jnp.int32, (rs, t), 0) + rs * r
                        coli = lax.broadcasted_iota(jnp.int32, (rs, t), 1)
                        s = jnp.where(coli <= rowi, s, -1e30)
                    m_old = m_scr[a, rows, :]
                    m_new = jnp.maximum(m_old, jnp.max(s, axis=1, keepdims=True))
                    p_scr[a, rows, :] = jnp.exp(s - m_new[:, :1]).astype(BF16)
                    a_scr[a, rows, :] = jnp.exp(m_old - m_new)
                    m_scr[a, rows, :] = m_new
                acc_scr[a] = acc_scr[a] * a_scr[a] + _dot(p_scr[a], vx_ref[pl.ds(off, t), sl], _NN)

        def step(j, carry):
            block(j, False)
            return carry
        lax.fori_loop(0, qi, step, 0)
        block(qi, True)
        res = []
        for a in range(2):
            acc = acc_scr[a]
            l = acc[:, V_HEAD_DIM:V_HEAD_DIM + 1]
            res.append((acc / l, m_scr[a] + jnp.log(l)))
        o_ref[...] = jnp.where(lane < V_HEAD_DIM, res[0][0], pltpu.roll(res[1][0], V_HEAD_DIM, 1))
        l_ref[...] = jnp.where(lane < V_HEAD_DIM, res[0][1], res[1][1])

    return pl.pallas_call(
        body, name="attn_fwd", grid=(MLA_HEADS // 2, SEQ // t),
        in_specs=[pl.BlockSpec((t, 2 * HEAD_PAD), lambda p, i: (i, p)),
                  pl.BlockSpec((2 * HEAD_PAD, SEQ), lambda p, i: (p, 0)),
                  pl.BlockSpec((SEQ, 2 * HEAD_PAD), lambda p, i: (0, p))],
        out_specs=[pl.BlockSpec((t, LANES), lambda p, i: (i, p)),
                   pl.BlockSpec((t, LANES), lambda p, i: (i, p))],
        out_shape=[jax.ShapeDtypeStruct((SEQ, MLA_WIDTH), F32), jax.ShapeDtypeStruct((SEQ, MLA_WIDTH), F32)],
        scratch_shapes=[pltpu.VMEM((2, t, t), F32), pltpu.VMEM((2, t, t), BF16), pltpu.VMEM((2, t, LANES), F32),
                        pltpu.VMEM((2, t, LANES), F32), pltpu.VMEM((2, t, LANES), F32)],
        compiler_params=pltpu.CompilerParams(dimension_semantics=("arbitrary", "arbitrary"), vmem_limit_bytes=VMEM_MID),
    )(q, kt, vx)


def _attn_bwd(q, kt, k, vxt, d_o, o, lse):
    t, rs = ATT_T, ATT_STRIP
    nq = SEQ // t

    def body(q_ref, kt_ref, k_ref, vxt_ref, do_ref, o_ref, l_ref, dq_ref, dk_ref, dv_ref,
             s_scr, dp_scr, p_scr, ds_scr, st_scr):
        dk_ref[...] = jnp.zeros_like(dk_ref)
        dv_ref[...] = jnp.zeros_like(dv_ref)
        lane = lax.broadcasted_iota(jnp.int32, (t, LANES), 1)

        def qtile(i, carry):
            ioff = pl.multiple_of(i * t, t)
            do_i = do_ref[pl.ds(ioff, t), :]
            o_i = o_ref[pl.ds(ioff, t), :]
            l_i = l_ref[pl.ds(ioff, t), :]
            for a in range(2):
                sl = slice(HEAD_PAD * a, HEAD_PAD * (a + 1))
                sel = (lane < V_HEAD_DIM) if a == 0 else (lane >= V_HEAD_DIM)
                doa = jnp.where(sel, do_i, 0.0)
                oa = o_i
                if a == 1:
                    doa = pltpu.roll(doa, V_HEAD_DIM, 1)
                    oa = pltpu.roll(o_i, V_HEAD_DIM, 1)
                st_scr[0] = jnp.broadcast_to(jnp.sum(doa * oa, axis=1, keepdims=True), (t, LANES))
                st_scr[1] = jnp.broadcast_to(l_i[:, V_HEAD_DIM * a:V_HEAD_DIM * a + 1], (t, LANES))
                doa_bf = doa.astype(BF16)
                qa = q_ref[pl.ds(ioff, t), sl]

                def block(j, masked, dq_acc, sl=sl, qa=qa, doa_bf=doa_bf):
                    joff = pl.multiple_of(j * t, t)
                    s_scr[...] = _dot(qa, kt_ref[sl, pl.ds(joff, t)], _NN)
                    dp_scr[...] = _dot(doa_bf, vxt_ref[sl, pl.ds(joff, t)], _NN)
                    for r in range(t // rs):
                        rows = slice(rs * r, rs * (r + 1))
                        p = jnp.exp(s_scr[rows, :] - st_scr[1, rows, :1])
                        if masked:
                            rowi = lax.broadcasted_iota(jnp.int32, (rs, t), 0) + rs * r
                            coli = lax.broadcasted_iota(jnp.int32, (rs, t), 1)
                            p = jnp.where(coli <= rowi, p, 0.0)
                        p_scr[rows, :] = p.astype(BF16)
                        ds_scr[rows, :] = (p * (dp_scr[rows, :] - st_scr[0, rows, :1])).astype(BF16)
                    dk_ref[pl.ds(joff, t), sl] += _dot(ds_scr[...], qa, _TN)
                    dv_ref[pl.ds(joff, t), sl] += _dot(p_scr[...], doa_bf, _TN)
                    return dq_acc + _dot(ds_scr[...], k_ref[pl.ds(joff, t), sl], _NN)

                dq_acc = lax.fori_loop(0, i, lambda j, acc: block(j, False, acc), jnp.zeros((t, HEAD_PAD), F32))
                dq_ref[pl.ds(ioff, t), sl] = block(i, True, dq_acc)
            return carry

        lax.fori_loop(0, nq, qtile, 0)

    hw = MLA_HEADS * HEAD_PAD
    wide = pl.BlockSpec((SEQ, 2 * HEAD_PAD), lambda p: (0, p))
    wide_t = pl.BlockSpec((2 * HEAD_PAD, SEQ), lambda p: (p, 0))
    narrow = pl.BlockSpec((SEQ, LANES), lambda p: (0, p))
    return pl.pallas_call(
        body, name="attn_bwd", grid=(MLA_HEADS // 2,),
        in_specs=[wide, wide_t, wide, wide_t, narrow, narrow, narrow],
        out_specs=[wide, wide, wide],
        out_shape=[jax.ShapeDtypeStruct((SEQ, hw), F32)] * 3,
        scratch_shapes=[pltpu.VMEM((t, t), F32), pltpu.VMEM((t, t), F32), pltpu.VMEM((t, t), BF16),
                        pltpu.VMEM((t, t), BF16), pltpu.VMEM((2, t, LANES), F32)],
        compiler_params=pltpu.CompilerParams(dimension_semantics=("arbitrary",), vmem_limit_bytes=VMEM_BIG),
    )(q, kt, k, vxt, d_o, o, lse)


def _sgu_math(u, v, zb, lg, lb, ws_ref, bias):
    ug, dug = _gelu_and_grad(u)
    vg, dvg = _gelu_and_grad(v)
    mu = jnp.mean(vg, axis=1, keepdims=True)
    xc = vg - mu
    rstd = lax.rsqrt(jnp.mean(xc * xc, axis=1, keepdims=True) + LN_EPS)
    xh = xc * rstd
    vn_bf = (xh * lg + lb).astype(BF16)
    grp = lax.broadcasted_iota(jnp.int32, (CHUNK, SGU_WIDTH), 1) // SGU_GROUP_DIM
    r_i = lax.broadcasted_iota(jnp.int32, (CHUNK, CHUNK), 0)
    c_i = lax.broadcasted_iota(jnp.int32, (CHUNK, CHUNK), 1)
    tri, tri_t = r_i >= c_i, r_i <= c_i
    mixed = bias
    for g in range(SGU_GROUPS):
        wt = jnp.where(tri, ws_ref[g], 0.0).astype(BF16)
        mixed = mixed + jnp.where(grp == g, _dot(wt, vn_bf, _NN), 0.0)
    sb = _sigmoid(zb)
    return ug, dug, dvg, rstd, xh, vn_bf, grp, tri, tri_t, mixed, sb


def _sgu_fwd(h_b, lg, lb, w_s, bias_full):
    def body(u_ref, v_ref, zb_ref, lg_ref, lb_ref, ws_ref, bias_ref, yb_ref):
        zb = zb_ref[...]
        ug, _, _, _, _, _, _, _, _, mixed, sb = _sgu_math(u_ref[...], v_ref[...], zb, lg_ref[...], lb_ref[...],
                                                       ws_ref, bias_ref[...])
        yb_ref[...] = (ug * mixed) * (zb * sb)

    blk = lambda c: pl.BlockSpec((CHUNK, SGU_WIDTH), lambda i, c=c: (i, c))
    full2 = lambda shape: pl.BlockSpec(shape, lambda i: (0, 0))
    return pl.pallas_call(
        body, name="sgu_fwd", grid=(SEQ // CHUNK,),
        in_specs=[blk(0), blk(1), blk(2), full2((1, SGU_WIDTH)), full2((1, SGU_WIDTH)),
                  pl.BlockSpec((SGU_GROUPS, CHUNK, CHUNK), lambda i: (0, 0, 0)), full2((CHUNK, SGU_WIDTH))],
        out_specs=pl.BlockSpec((CHUNK, SGU_WIDTH), lambda i: (i, 0)),
        out_shape=jax.ShapeDtypeStruct((SEQ, SGU_WIDTH), F32),
        compiler_params=pltpu.CompilerParams(dimension_semantics=("arbitrary",)),
    )(h_b, h_b, h_b, lg, lb, w_s, bias_full)


def _sgu_bwd(h_b, d_yb, lg, lb, w_s, w_st, bias_full):
    nsteps = SEQ // CHUNK

    def body(u_ref, v_ref, zb_ref, dyb_ref, lg_ref, lb_ref, ws_ref, wst_ref, bias_ref,
             dhb_ref, dws_ref, dbs_ref, dlg_ref, dlb_ref, dbias_acc):
        step = pl.program_id(0)

        @pl.when(step == 0)
        def _():
            dws_ref[...] = jnp.zeros_like(dws_ref)
            dlg_ref[...] = jnp.zeros_like(dlg_ref)
            dlb_ref[...] = jnp.zeros_like(dlb_ref)
            dbias_acc[...] = jnp.zeros_like(dbias_acc)

        zb = zb_ref[...]
        lg = lg_ref[...]
        ug, dug, dvg, rstd, xh, vn_bf, grp, tri, tri_t, mixed, sb = _sgu_math(
            u_ref[...], v_ref[...], zb, lg, lb_ref[...], ws_ref, bias_ref[...])
        dyb = dyb_ref[...]
        dsgu = dyb * (zb * sb)
        dzb = dyb * (ug * mixed) * (sb * (1.0 + zb * (1.0 - sb)))
        du = dsgu * mixed * dug
        dmixed = dsgu * ug
        dbias_acc[...] += dmixed
        dvn = jnp.zeros((CHUNK, SGU_WIDTH), F32)
        for g in range(SGU_GROUPS):
            dm_g = jnp.where(grp == g, dmixed, 0.0).astype(BF16)
            wtt = jnp.where(tri_t, wst_ref[g], 0.0).astype(BF16)
            dvn = dvn + _dot(wtt, dm_g, _NN)
            dws_ref[g] += jnp.where(tri, _dot(dm_g, vn_bf, _NT), 0.0)
        dlg_ref[...] += jnp.sum(dvn * xh, axis=0, keepdims=True)
        dlb_ref[...] += jnp.sum(dvn, axis=0, keepdims=True)
        dxh = dvn * lg
        dvgel = rstd * (dxh - jnp.mean(dxh, axis=1, keepdims=True) - xh * jnp.mean(dxh * xh, axis=1, keepdims=True))
        dhb_ref[:, 0:SGU_WIDTH] = du
        dhb_ref[:, SGU_WIDTH:2 * SGU_WIDTH] = dvgel * dvg
        dhb_ref[:, 2 * SGU_WIDTH:3 * SGU_WIDTH] = dzb

        @pl.when(step == nsteps - 1)
        def _():
            acc = dbias_acc[...]
            lane = lax.broadcasted_iota(jnp.int32, (CHUNK, LANES), 1)
            out = jnp.zeros((CHUNK, LANES), F32)
            for g in range(SGU_GROUPS):
                sg = jnp.sum(jnp.where(grp == g, acc, 0.0), axis=1, keepdims=True)
                out = jnp.where(lane == g, sg, out)
            dbs_ref[...] = out

    blk = lambda c: pl.BlockSpec((CHUNK, SGU_WIDTH), lambda i, c=c: (i, c))
    full2 = lambda shape: pl.BlockSpec(shape, lambda i: (0, 0))
    full3 = pl.BlockSpec((SGU_GROUPS, CHUNK, CHUNK), lambda i: (0, 0, 0))
    return pl.pallas_call(
        body, name="sgu_bwd", grid=(nsteps,),
        in_specs=[blk(0), blk(1), blk(2), pl.BlockSpec((CHUNK, SGU_WIDTH), lambda i: (i, 0)),
                  full2((1, SGU_WIDTH)), full2((1, SGU_WIDTH)), full3, full3, full2((CHUNK, SGU_WIDTH))],
        out_specs=[pl.BlockSpec((CHUNK, SEG_B), lambda i: (i, 0)), full3, full2((CHUNK, LANES)),
                   full2((1, SGU_WIDTH)), full2((1, SGU_WIDTH))],
        out_shape=[jax.ShapeDtypeStruct((SEQ, SEG_B), F32),
                   jax.ShapeDtypeStruct((SGU_GROUPS, CHUNK, CHUNK), F32),
                   jax.ShapeDtypeStruct((CHUNK, LANES), F32),
                   jax.ShapeDtypeStruct((1, SGU_WIDTH), F32), jax.ShapeDtypeStruct((1, SGU_WIDTH), F32)],
        scratch_shapes=[pltpu.VMEM((CHUNK, SGU_WIDTH), F32)],
        compiler_params=pltpu.CompilerParams(dimension_semantics=("arbitrary",)),
    )(h_b, h_b, h_b, d_yb, lg, lb, w_s, w_st, bias_full)


def _merge(x, o, h_a, y_b, target, w_oa, w_ob, w_out, ln_g, ln_b):
    tm = 128
    nsteps = SEQ // tm

    def body(x_ref, o_ref, ga_ref, gb_ref, za_ref, yb_ref, tgt_ref, woa_ref, wob_ref, wout_ref, lng_ref, lnb_ref,
             loss_ref, dxr_ref, dha_ref, do_ref, dyb_ref, poa_ref, pob_ref, pout_ref, dlng_ref, dlnb_ref,
             dwoa_ref, dwob_ref, dwout_ref):
        step = pl.program_id(0)

        @pl.when(step == 0)
        def _():
            for r in (loss_ref, dwoa_ref, dwob_ref, dwout_ref, dlng_ref, dlnb_ref):
                r[...] = jnp.zeros_like(r)

        o = o_ref[...]
        za = za_ref[...]
        sa = _sigmoid(za)
        ya_bf = (o * (za * sa)).astype(BF16)
        yb_bf = yb_ref[...].astype(BF16)
        woa, wob, wout = woa_ref[...], wob_ref[...], wout_ref[...]
        pa = _dot(ya_bf, woa, _NN)
        pb = _dot(yb_bf, wob, _NN)
        sga = _sigmoid(ga_ref[...])
        sgb = _sigmoid(gb_ref[...])
        merged_bf = (sga * pa + sgb * pb).astype(BF16)
        r = DN_ALPHA * x_ref[...] + _dot(merged_bf, wout, _NN)
        mu = jnp.mean(r, axis=1, keepdims=True)
        rc = r - mu
        rstd = lax.rsqrt(jnp.mean(rc * rc, axis=1, keepdims=True) + LN_EPS)
        xh = rc * rstd
        lng = lng_ref[...]
        y = xh * lng + lnb_ref[...]
        e = y - tgt_ref[...]
        loss_ref[...] += 0.5 * jnp.sum(jnp.sum(e * e, axis=1, keepdims=True) * (1.0 / D_MODEL), axis=0, keepdims=True)

        dy = e * (1.0 / D_MODEL)
        dlng_ref[...] += jnp.sum(dy * xh, axis=0, keepdims=True)
        dlnb_ref[...] += jnp.sum(dy, axis=0, keepdims=True)
        dxh = dy * lng
        dr = rstd * (dxh - jnp.mean(dxh, axis=1, keepdims=True) - xh * jnp.mean(dxh * xh, axis=1, keepdims=True))
        dxr_ref[...] = DN_ALPHA * dr
        dr_bf = dr.astype(BF16)
        dwout_ref[...] += _dot(merged_bf, dr_bf, _TN)
        dmerged = _dot(dr_bf, wout, _NT)
        dpa_bf = (dmerged * sga).astype(BF16)
        dpb_bf = (dmerged * sgb).astype(BF16)
        dha_ref[:, 0:D_MODEL] = dmerged * pa * (sga * (1.0 - sga))
        dha_ref[:, D_MODEL:2 * D_MODEL] = dmerged * pb * (sgb * (1.0 - sgb))
        dwoa_ref[...] += _dot(ya_bf, dpa_bf, _TN)
        dwob_ref[...] += _dot(yb_bf, dpb_bf, _TN)
        dya = _dot(dpa_bf, woa, _NT)
        dyb_ref[...] = _dot(dpb_bf, wob, _NT)
        do_ref[...] = dya * (za * sa)
        dha_ref[:, 2 * D_MODEL:SEG_A] = dya * o * (sa * (1.0 + za * (1.0 - sa)))

        @pl.when(step == nsteps - 1)
        def _():
            cols = D_MODEL // N_DEV
            for j in range(N_DEV):
                poa_ref[j] = dwoa_ref[:, cols * j:cols * (j + 1)].astype(BF16)
                pob_ref[j] = dwob_ref[:, cols * j:cols * (j + 1)].astype(BF16)
                pout_ref[j] = dwout_ref[cols * j:cols * (j + 1), :].astype(BF16)

    row = lambda w, c=0: pl.BlockSpec((tm, w), lambda i, c=c: (i, c))
    full = lambda shape: pl.BlockSpec(shape, lambda i: (0, 0))
    full3 = lambda shape: pl.BlockSpec(shape, lambda i: (0, 0, 0))
    return pl.pallas_call(
        body, name="merge", grid=(nsteps,),
        in_specs=[row(D_MODEL), row(MLA_WIDTH), row(D_MODEL, 0), row(D_MODEL, 1), row(MLA_WIDTH, 4), row(SGU_WIDTH),
                  row(D_MODEL), full((MLA_WIDTH, D_MODEL)), full((SGU_WIDTH, D_MODEL)), full((D_MODEL, D_MODEL)),
                  full((1, D_MODEL)), full((1, D_MODEL))],
        out_specs=[full((1, LANES)), row(D_MODEL), row(SEG_A), row(MLA_WIDTH), row(SGU_WIDTH),
                   full3((N_DEV, MLA_WIDTH, D_MODEL // N_DEV)), full3((N_DEV, SGU_WIDTH, D_MODEL // N_DEV)),
                   full3((N_DEV, D_MODEL // N_DEV, D_MODEL)), full((1, D_MODEL)), full((1, D_MODEL))],
        out_shape=[jax.ShapeDtypeStruct((1, LANES), F32),
                   jax.ShapeDtypeStruct((SEQ, D_MODEL), F32), jax.ShapeDtypeStruct((SEQ, SEG_A), F32),
                   jax.ShapeDtypeStruct((SEQ, MLA_WIDTH), F32), jax.ShapeDtypeStruct((SEQ, SGU_WIDTH), F32),
                   jax.ShapeDtypeStruct((N_DEV, MLA_WIDTH, D_MODEL // N_DEV), BF16),
                   jax.ShapeDtypeStruct((N_DEV, SGU_WIDTH, D_MODEL // N_DEV), BF16),
                   jax.ShapeDtypeStruct((N_DEV, D_MODEL // N_DEV, D_MODEL), BF16),
                   jax.ShapeDtypeStruct((1, D_MODEL), F32), jax.ShapeDtypeStruct((1, D_MODEL), F32)],
        scratch_shapes=[pltpu.VMEM((MLA_WIDTH, D_MODEL), F32), pltpu.VMEM((SGU_WIDTH, D_MODEL), F32),
                        pltpu.VMEM((D_MODEL, D_MODEL), F32)],
        compiler_params=pltpu.CompilerParams(dimension_semantics=("arbitrary",), vmem_limit_bytes=VMEM_BIG),
    )(x, o, h_a, h_a, h_a, y_b, target, w_oa, w_ob, w_out, ln_g, ln_b)


def _mla_bwd(dq, dk, dv, h_c, gq, gkv, wq, wkn, wv, c_t, sa_t, sb_t):
    tm = 256
    hw = MLA_HEADS * HEAD_PAD

    def body(dq_ref, dk_ref, dv_ref, cq_ref, ckv_ref, gq_ref, gkv_ref, wq_ref, wkn_ref, wv_ref, c_ref, sa_ref, sb_ref,
             dhc_ref, puq_ref, dwkn_ref, dwv_ref, dgq_ref, dgkv_ref, pre_ref, dwq_ref):
        @pl.when(pl.program_id(0) == 0)
        def _():
            for r in (dwq_ref, dwkn_ref, dwv_ref, dgq_ref, dgkv_ref):
                r[...] = jnp.zeros_like(r)

        c, sa, sb = c_ref[...], sa_ref[...], sb_ref[...]
        lane = lax.broadcasted_iota(jnp.int32, (tm, LANES), 1)
        rope_lanes = jnp.logical_and(lane >= ROPE_LO, lane < ROPE_HI)

        cq = cq_ref[...]
        gq = gq_ref[...]
        rq = lax.rsqrt(jnp.sum(cq * cq, axis=1, keepdims=True) * (1.0 / Q_LORA_RANK) + RMS_EPS)
        nq = cq * rq
        cqn_bf = (nq * gq).astype(BF16)
        for h in range(MLA_HEADS):
            sl = slice(HEAD_PAD * h, HEAD_PAD * (h + 1))
            pre_ref[:, sl] = _rope_t(dq_ref[:, sl] * ATTN_SCALE, c, sa, sb).astype(BF16)
        dqpre_bf = pre_ref[...]
        dcqn = _dot(dqpre_bf, wq_ref[...], _NT)
        dwq_ref[...] += _dot(cqn_bf, dqpre_bf, _TN)
        dgq_ref[...] += jnp.sum(dcqn * nq, axis=0, keepdims=True)
        dnq = dcqn * gq
        dhc_ref[:, 0:CQ_PAD] = rq * (dnq - nq * (jnp.sum(dnq * nq, axis=1, keepdims=True) * (1.0 / Q_LORA_RANK)))

        ckv = ckv_ref[...]
        gkv = gkv_ref[...]
        rkv = lax.rsqrt(jnp.sum(ckv * ckv, axis=1, keepdims=True) * (1.0 / KV_LORA_RANK) + RMS_EPS)
        nkv = ckv * rkv
        ckvn_bf = (nkv * gkv).astype(BF16)
        dk = dk_ref[...]
        dk_bf = dk.astype(BF16)
        dv_bf = dv_ref[...].astype(BF16)
        dckvn = _dot(dk_bf, wkn_ref[...], _NT) + _dot(dv_bf, wv_ref[...], _NT)
        dwkn_ref[...] += _dot(ckvn_bf, dk_bf, _TN)
        dwv_ref[...] += _dot(ckvn_bf, dv_bf, _TN)
        dgkv_ref[...] += jnp.sum(dckvn * nkv, axis=0, keepdims=True)
        dnkv = dckvn * gkv
        dhc_ref[:, CQ_PAD:CQ_PAD + LANES] = rkv * (
            dnkv - nkv * (jnp.sum(dnkv * nkv, axis=1, keepdims=True) * (1.0 / KV_LORA_RANK)))
        dkpe = jnp.zeros((tm, LANES), F32)
        for h in range(MLA_HEADS):
            dkpe = dkpe + dk[:, HEAD_PAD * h:HEAD_PAD * (h + 1)]
        dhc_ref[:, CQ_PAD + LANES:SEG_C] = _rope_t(jnp.where(rope_lanes, dkpe, 0.0), c, sa, sb)

        @pl.when(pl.program_id(0) == SEQ // tm - 1)
        def _():
            rows = Q_LORA_RANK // N_DEV
            for j in range(N_DEV):
                for h in range(MLA_HEADS):
                    puq_ref[j, :, QK_HEAD_DIM * h:QK_HEAD_DIM * (h + 1)] = dwq_ref[
                        rows * j:rows * (j + 1), HEAD_PAD * h:HEAD_PAD * h + QK_HEAD_DIM].astype(BF16)

    full = lambda shape: pl.BlockSpec(shape, lambda i: (0, 0))
    row = lambda w, c=0: pl.BlockSpec((tm, w), lambda i, c=c: (i, c))
    return pl.pallas_call(
        body, name="mla_bwd", grid=(SEQ // tm,),
        in_specs=[row(hw), row(hw), row(hw), row(CQ_PAD, 0), row(LANES, CQ_PAD // LANES),
                  full((1, CQ_PAD)), full((1, KV_LORA_RANK)), full((CQ_PAD, hw)), full((KV_LORA_RANK, hw)),
                  full((KV_LORA_RANK, hw)), row(LANES), row(LANES), row(LANES)],
        out_specs=[row(SEG_C), pl.BlockSpec((N_DEV, Q_LORA_RANK // N_DEV, MLA_HEADS * QK_HEAD_DIM), lambda i: (0, 0, 0)),
                   full((KV_LORA_RANK, hw)), full((KV_LORA_RANK, hw)),
                   full((1, CQ_PAD)), full((1, KV_LORA_RANK))],
        out_shape=[jax.ShapeDtypeStruct((SEQ, SEG_C), F32),
                   jax.ShapeDtypeStruct((N_DEV, Q_LORA_RANK // N_DEV, MLA_HEADS * QK_HEAD_DIM), BF16),
                   jax.ShapeDtypeStruct((KV_LORA_RANK, hw), F32), jax.ShapeDtypeStruct((KV_LORA_RANK, hw), F32),
                   jax.ShapeDtypeStruct((1, CQ_PAD), F32), jax.ShapeDtypeStruct((1, KV_LORA_RANK), F32)],
        scratch_shapes=[pltpu.VMEM((tm, hw), BF16), pltpu.VMEM((CQ_PAD, hw), F32)],
        compiler_params=pltpu.CompilerParams(dimension_semantics=("arbitrary",), vmem_limit_bytes=VMEM_MID),
    )(dq, dk, dv, h_c, h_c, gq, gkv, wq, wkn, wv, c_t, sa_t, sb_t)


def _adamw_all(ws, gs, ms, vs):
    n = len(ws)
    c1 = 1.0 / (1.0 - ADAM_B1 ** ADAM_STEP)
    c2 = 1.0 / (1.0 - ADAM_B2 ** ADAM_STEP)

    def body(*refs):
        for idx in range(n):
            w, g, m, v = (refs[idx][...], refs[n + idx][...], refs[2 * n + idx][...], refs[3 * n + idx][...])
            m_new = ADAM_B1 * m + (1.0 - ADAM_B1) * g
            v_new = ADAM_B2 * v + (1.0 - ADAM_B2) * (g * g)
            delta = -ADAM_LR * ((m_new * c1) / (jnp.sqrt(v_new * c2) + ADAM_EPS) + ADAM_WD * w)
            refs[4 * n + idx][...] = delta
            refs[5 * n + idx][...] = m_new
            refs[6 * n + idx][...] = v_new

    shapes = [jax.ShapeDtypeStruct(w.shape, F32) for w in ws]
    outs = pl.pallas_call(
        body, name="adamw", out_shape=shapes * 3,
        compiler_params=pltpu.CompilerParams(vmem_limit_bytes=VMEM_BIG),
    )(*ws, *gs, *ms, *vs)
    return outs[:n], outs[n:2 * n], outs[2 * n:]


def _all_gather(block, name):
    rows, width = block.shape

    def body(x_ref, out_ref, send_sems, recv_sems, local_sem):
        x, y, c = lax.axis_index("x"), lax.axis_index("y"), lax.axis_index("c")
        me, sibling = (x, y, c), (x, y, 1 - c)
        chips = [(1 - x, y), (x, 1 - y), (1 - x, 1 - y)]

        def slab(px, py, pc):
            return out_ref.at[4 * px + 2 * py + pc]

        def copy(k, blk, to, src=None):
            return pltpu.make_async_remote_copy(
                src_ref=slab(*blk) if src is None else src, dst_ref=slab(*blk),
                send_sem=send_sems.at[k], recv_sem=recv_sems.at[k],
                device_id=to, device_id_type=pl.DeviceIdType.MESH)

        mine = pltpu.make_async_copy(x_ref, slab(*me), local_sem)
        mine.start()
        first = [copy(0, me, sibling, src=x_ref)]
        first += [copy(1 + j, me, (*chip, c), src=x_ref) for j, chip in enumerate(chips)]
        for cp in first:
            cp.start()
        passed = [copy(4 + j, (*chip, c), sibling) for j, chip in enumerate(chips)]
        for j, chip in enumerate(chips):
            copy(1 + j, (*chip, c), me).wait_recv()
            passed[j].start()
        copy(0, sibling, me).wait_recv()
        for j, chip in enumerate(chips):
            copy(4 + j, (*chip, 1 - c), me).wait_recv()
        for cp in first + passed:
            cp.wait_send()
        mine.wait()

    return pl.pallas_call(
        body, name=name,
        out_shape=jax.ShapeDtypeStruct((N_DEV, rows, width), block.dtype),
        in_specs=[pl.BlockSpec(memory_space=pltpu.VMEM)],
        out_specs=pl.BlockSpec(memory_space=pltpu.VMEM),
        scratch_shapes=[pltpu.SemaphoreType.DMA((7,)), pltpu.SemaphoreType.DMA((7,)), pltpu.SemaphoreType.DMA],
        compiler_params=pltpu.CompilerParams(vmem_limit_bytes=VMEM_MID),
    )(block)


SHARD_W = IN_WIDTH // N_DEV

_PIECES = [(0, 384, 2, 0), (384, 512, 2, CQ_PAD), (512, 544, 2, CQ_PAD + LANES + ROPE_LO),
           (544, 1056, 0, 2 * D_MODEL), (1056, 1568, 1, 0), (1568, 2080, 1, SGU_WIDTH),
           (2080, 2592, 1, 2 * SGU_WIDTH), (2592, 3616, 0, 0), (3616, 4640, 0, D_MODEL)]


def _column_runs():
    runs = []
    for n0, n1, seg, d0 in _PIECES:
        for j in range(N_DEV):
            lo, hi = max(n0, j * SHARD_W), min(n1, (j + 1) * SHARD_W)
            if lo < hi:
                runs.append((j, lo - j * SHARD_W, hi - j * SHARD_W, seg, d0 + lo - n0))
    return runs


def _mesh_pos():
    return lax.axis_index("x"), lax.axis_index("y"), lax.axis_index("c")


def _remote(src, dst, send_sems, recv_sems, k, to):
    return pltpu.make_async_remote_copy(src_ref=src, dst_ref=dst, send_sem=send_sems.at[k], recv_sem=recv_sems.at[k],
                                        device_id=to, device_id_type=pl.DeviceIdType.MESH)


def _gather_exchange(gats, send_sems, recv_sems):
    x, y, c = _mesh_pos()
    me, sibling = (x, y, c), (x, y, 1 - c)
    chips = [(1 - x, y), (x, 1 - y), (1 - x, 1 - y)]

    def copy(a, k, blk, to):
        slab = gats[a].at[4 * blk[0] + 2 * blk[1] + blk[2]]
        return _remote(slab, slab, send_sems, recv_sems, 7 * a + k, to)

    arrays = range(len(gats))
    first = [copy(a, 1 + j, me, (*chip, c)) for j, chip in enumerate(chips) for a in arrays]
    first += [copy(a, 0, me, sibling) for a in arrays]
    for cp in first:
        cp.start()
    passed = []
    for j, chip in enumerate(chips):
        for a in arrays:
            copy(a, 1 + j, (*chip, c), me).wait_recv()
            fwd = copy(a, 4 + j, (*chip, c), sibling)
            fwd.start()
            passed.append(fwd)
    for a in arrays:
        copy(a, 0, sibling, me).wait_recv()
    for j, chip in enumerate(chips):
        for a in arrays:
            copy(a, 4 + j, (*chip, 1 - c), me).wait_recv()
    for cp in first + passed:
        cp.wait_send()


def _gather_weights(w_in, w_uq2, w_oa, w_ob, w_out):
    hw = MLA_HEADS * HEAD_PAD
    uq_rows = Q_LORA_RANK // N_DEV
    cols = D_MODEL // N_DEV

    def body(win_ref, wuq_ref, woa_ref, wob_ref, wout_ref, wa_ref, wb_ref, wc_ref, wq_ref, oa_ref, ob_ref, out_ref,
             g_in, g_uq, g_oa, g_ob, g_out, send_sems, recv_sems):
        x, y, c = _mesh_pos()
        me = 4 * x + 2 * y + c
        g_in[me] = win_ref[0].astype(BF16)
        g_uq[me] = wuq_ref[...].astype(BF16)
        g_oa[me] = woa_ref[0].astype(BF16)
        g_ob[me] = wob_ref[0].astype(BF16)
        g_out[me] = wout_ref[0].astype(BF16)
        _gather_exchange([g_in, g_uq, g_oa, g_ob, g_out], send_sems, recv_sems)

        segs = [wa_ref, wb_ref, wc_ref]
        for j, s0, s1, seg, d0 in _column_runs():
            segs[seg][:, d0:d0 + (s1 - s0)] = g_in[j, :, s0:s1]
        zeros = lambda r, w: jnp.zeros((r, w), BF16)
        wc_ref[:, Q_LORA_RANK:CQ_PAD] = zeros(D_MODEL, CQ_PAD - Q_LORA_RANK)
        wc_ref[:, CQ_PAD + LANES:CQ_PAD + LANES + ROPE_LO] = zeros(D_MODEL, ROPE_LO)
        wc_ref[:, CQ_PAD + LANES + ROPE_HI:SEG_C] = zeros(D_MODEL, LANES - ROPE_HI)
        wq_ref[Q_LORA_RANK:CQ_PAD, :] = zeros(CQ_PAD - Q_LORA_RANK, hw)
        for h in range(MLA_HEADS):
            wq_ref[0:Q_LORA_RANK, HEAD_PAD * h + QK_HEAD_DIM:HEAD_PAD * (h + 1)] = zeros(Q_LORA_RANK, HEAD_PAD - QK_HEAD_DIM)
        for j in range(N_DEV):
            for h in range(MLA_HEADS):
                wq_ref[uq_rows * j:uq_rows * (j + 1), HEAD_PAD * h:HEAD_PAD * h + QK_HEAD_DIM] = g_uq[
                    j, :, QK_HEAD_DIM * h:QK_HEAD_DIM * (h + 1)]
            oa_ref[:, cols * j:cols * (j + 1)] = g_oa[j]
            ob_ref[:, cols * j:cols * (j + 1)] = g_ob[j]
            out_ref[cols * j:cols * (j + 1), :] = g_out[j]

    vmem = pl.BlockSpec(memory_space=pltpu.VMEM)
    return pl.pallas_call(
        body, name="gather_weights",
        out_shape=[jax.ShapeDtypeStruct((D_MODEL, SEG_A), BF16), jax.ShapeDtypeStruct((D_MODEL, SEG_B), BF16),
                   jax.ShapeDtypeStruct((D_MODEL, SEG_C), BF16), jax.ShapeDtypeStruct((CQ_PAD, hw), BF16),
                   jax.ShapeDtypeStruct((MLA_WIDTH, D_MODEL), BF16), jax.ShapeDtypeStruct((SGU_WIDTH, D_MODEL), BF16),
                   jax.ShapeDtypeStruct((D_MODEL, D_MODEL), BF16)],
        in_specs=[vmem] * 5, out_specs=[vmem] * 7,
        scratch_shapes=[pltpu.VMEM((N_DEV, D_MODEL, SHARD_W), BF16),
                        pltpu.VMEM((N_DEV, uq_rows, MLA_HEADS * QK_HEAD_DIM), BF16),
                        pltpu.VMEM((N_DEV, MLA_WIDTH, cols), BF16), pltpu.VMEM((N_DEV, SGU_WIDTH, cols), BF16),
                        pltpu.VMEM((N_DEV, cols, D_MODEL), BF16),
                        pltpu.SemaphoreType.DMA((35,)), pltpu.SemaphoreType.DMA((35,))],
        compiler_params=pltpu.CompilerParams(vmem_limit_bytes=VMEM_BIG),
    )(w_in, w_uq2, w_oa, w_ob, w_out)


def _reduce_grads(dwa, dwb, dwc, p_uq, p_oa, p_ob, p_out, p_rep):
    rep_rows = p_rep.shape[1]
    spec = [((D_MODEL, SHARD_W), BF16, 256), (p_uq.shape[1:], BF16, p_uq.shape[1]), (p_oa.shape[1:], BF16, 256),
            (p_ob.shape[1:], BF16, 256), (p_out.shape[1:], BF16, 64), ((rep_rows, LANES), F32, rep_rows)]
    n = len(spec)

    def body(dwa_ref, dwb_ref, dwc_ref, puq_ref, poa_ref, pob_ref, pout_ref, prep_ref,
             gin_ref, guq_ref, goa_ref, gob_ref, gout_ref, grep_ref, pin_ref, *rest):
        ras, tbs, rbs = rest[0:n], rest[n:2 * n], rest[2 * n:3 * n]
        send_sems, recv_sems = rest[3 * n], rest[3 * n + 1]
        x, y, c = _mesh_pos()
        sibling = (x, y, 1 - c)
        parts = [pin_ref, puq_ref, poa_ref, pob_ref, pout_ref, prep_ref]
        outs = [gin_ref.at[0], guq_ref, goa_ref.at[0], gob_ref.at[0], gout_ref.at[0], grep_ref]

        segs = [dwa_ref, dwb_ref, dwc_ref]
        for j, s0, s1, seg, d0 in _column_runs():
            pin_ref[j, :, s0:s1] = segs[seg][:, d0:d0 + (s1 - s0)]

        stage1 = []
        for chip in range(4):
            for a in range(n):
                cp = _remote(parts[a].at[2 * chip + (1 - c)], ras[a].at[chip], send_sems, recv_sems, 7 * a + chip, sibling)
                cp.start()
                stage1.append(cp)
        for cp in stage1:
            cp.wait_recv()

        def rows_loop(a, fn):
            rows, chunk = spec[a][0][0], spec[a][2]
            if rows == chunk:
                fn(pl.ds(0, rows))
            else:
                def step(i, carry):
                    fn(pl.ds(pl.multiple_of(i * chunk, chunk), chunk))
                    return carry
                lax.fori_loop(0, rows // chunk, step, 0)

        def chip_sum(a, chip, sl):
            return parts[a][2 * chip + c, sl, :].astype(F32) + ras[a][chip, sl, :].astype(F32)

        others = [(1 - x, y), (x, 1 - y), (1 - x, 1 - y)]
        stage2 = []
        for k, (cx, cy) in enumerate(others):
            for a in range(n):
                def fill(sl, a=a, k=k, chip=2 * cx + cy):
                    tbs[a][k, sl, :] = chip_sum(a, chip, sl).astype(spec[a][1])
                rows_loop(a, fill)
                cp = _remote(tbs[a].at[k], rbs[a].at[k], send_sems, recv_sems, 7 * a + 4 + k, (cx, cy, c))
                cp.start()
                stage2.append(cp)
        for cp in stage2:
            cp.wait_recv()
        for a in range(n):
            def final(sl, a=a):
                acc = chip_sum(a, 2 * x + y, sl)
                for k in range(3):
                    acc = acc + rbs[a][k, sl, :].astype(F32)
                outs[a][sl, :] = acc
            rows_loop(a, final)
        for cp in stage1 + stage2:
            cp.wait_send()

    vmem = pl.BlockSpec(memory_space=pltpu.VMEM)
    scratch = [pltpu.VMEM((N_DEV, D_MODEL, SHARD_W), BF16)]
    for lead in (4, 3, 3):
        scratch += [pltpu.VMEM((lead,) + tuple(shape), dt) for shape, dt, _ in spec]
    scratch += [pltpu.SemaphoreType.DMA((7 * n,)), pltpu.SemaphoreType.DMA((7 * n,))]
    return pl.pallas_call(
        body, name="reduce_grads",
        out_shape=[jax.ShapeDtypeStruct((1, D_MODEL, SHARD_W), F32), jax.ShapeDtypeStruct(p_uq.shape[1:], F32),
                   jax.ShapeDtypeStruct((1,) + p_oa.shape[1:], F32), jax.ShapeDtypeStruct((1,) + p_ob.shape[1:], F32),
                   jax.ShapeDtypeStruct((1,) + p_out.shape[1:], F32), jax.ShapeDtypeStruct((rep_rows, LANES), F32)],
        in_specs=[vmem] * 8, out_specs=[vmem] * 6, scratch_shapes=scratch,
        compiler_params=pltpu.CompilerParams(vmem_limit_bytes=VMEM_BIG),
    )(dwa, dwb, dwc, p_uq, p_oa, p_ob, p_out, p_rep)


_O_CQ, _O_CKV, _O_KPE, _O_ZA, _O_U, _O_V, _O_ZB, _O_GA, _O_GB = 0, 384, 512, 544, 1056, 1568, 2080, 2592, 3616


def _to_segments(w):
    z = lambda n: jnp.zeros(w.shape[:-1] + (n,), w.dtype)
    seg_a = jnp.concatenate([w[..., _O_GA:_O_GB], w[..., _O_GB:IN_WIDTH], w[..., _O_ZA:_O_U]], axis=-1)
    seg_b = jnp.concatenate([w[..., _O_U:_O_V], w[..., _O_V:_O_ZB], w[..., _O_ZB:_O_GA]], axis=-1)
    seg_c = jnp.concatenate([w[..., _O_CQ:_O_CKV], z(CQ_PAD - Q_LORA_RANK), w[..., _O_CKV:_O_KPE],
                             z(ROPE_LO), w[..., _O_KPE:_O_ZA], z(LANES - ROPE_HI)], axis=-1)
    return seg_a, seg_b, seg_c


def _from_segments(seg_a, seg_b, seg_c):
    kpe0 = CQ_PAD + LANES + ROPE_LO
    return jnp.concatenate([
        seg_c[..., 0:Q_LORA_RANK], seg_c[..., CQ_PAD:CQ_PAD + LANES], seg_c[..., kpe0:kpe0 + QK_ROPE_DIM],
        seg_a[..., 2 * D_MODEL:SEG_A], seg_b, seg_a[..., 0:2 * D_MODEL]], axis=-1)


def kernel(x, positions, w_in, b_in, g_q, w_uq, g_kv, w_ukv, w_oa, sgu_ln_g, sgu_ln_b, w_s, b_s, w_ob, w_out, ln_g, ln_b, loss_target, m_w_in, m_b_in, m_g_q, m_w_uq, m_g_kv, m_w_ukv, m_w_oa, m_sgu_ln_g, m_sgu_ln_b, m_w_s, m_b_s, m_w_ob, m_w_out, m_ln_g, m_ln_b, v_w_in, v_b_in, v_g_q, v_w_uq, v_g_kv, v_w_ukv, v_w_oa, v_sgu_ln_g, v_sgu_ln_b, v_w_s, v_b_s, v_w_ob, v_w_out, v_ln_g, v_ln_b):
    w_uq2 = w_uq[0].reshape(Q_LORA_RANK // N_DEV, MLA_HEADS * QK_HEAD_DIM)
    wa, wb, wc, wq, w_oa_f, w_ob_f, w_out_f = _gather_weights(w_in, w_uq2, w_oa, w_ob, w_out)
    partials = _local_step(x[0], positions, loss_target[0], wa, wb, wc, b_in, g_q, wq, g_kv, w_ukv, w_oa_f, sgu_ln_g,
                           sgu_ln_b, w_s, b_s, w_ob_f, w_out_f, ln_g, ln_b)
    weights = dict(w_in=w_in, b_in=b_in, g_q=g_q, w_uq=w_uq, g_kv=g_kv, w_ukv=w_ukv, w_oa=w_oa, sgu_ln_g=sgu_ln_g,
                   sgu_ln_b=sgu_ln_b, w_s=w_s, b_s=b_s, w_ob=w_ob, w_out=w_out, ln_g=ln_g, ln_b=ln_b)
    moms = dict(w_in=m_w_in, b_in=m_b_in, g_q=m_g_q, w_uq=m_w_uq, g_kv=m_g_kv, w_ukv=m_w_ukv, w_oa=m_w_oa,
                sgu_ln_g=m_sgu_ln_g, sgu_ln_b=m_sgu_ln_b, w_s=m_w_s, b_s=m_b_s, w_ob=m_w_ob, w_out=m_w_out,
                ln_g=m_ln_g, ln_b=m_ln_b)
    vars_ = dict(w_in=v_w_in, b_in=v_b_in, g_q=v_g_q, w_uq=v_w_uq, g_kv=v_g_kv, w_ukv=v_w_ukv, w_oa=v_w_oa,
                 sgu_ln_g=v_sgu_ln_g, sgu_ln_b=v_sgu_ln_b, w_s=v_w_s, b_s=v_b_s, w_ob=v_w_ob, w_out=v_w_out,
                 ln_g=v_ln_g, ln_b=v_ln_b)
    return _reduce_and_update(partials, weights, moms, vars_)


def _local_step(x2, positions, tgt, wa, wb, wc, b_in, g_q, wq, g_kv, w_ukv, w_oa_f, sgu_ln_g, sgu_ln_b, w_s, b_s,
                w_ob_f, w_out_f, ln_g, ln_b):
    ba, bb, bc = _to_segments(b_in)
    w_ukv_bf = w_ukv[0].astype(BF16)
    wkn = jnp.pad(w_ukv_bf[:, :, :QK_NOPE_DIM], ((0, 0), (0, 0), (0, HEAD_PAD - QK_NOPE_DIM))).reshape(KV_LORA_RANK, -1)
    wv = jnp.pad(w_ukv_bf[:, :, QK_NOPE_DIM:], ((0, 0), (0, 0), (0, HEAD_PAD - V_HEAD_DIM))).reshape(KV_LORA_RANK, -1)
    gq = jnp.pad(g_q, ((0, 0), (0, CQ_PAD - Q_LORA_RANK)))
    bias_full = jnp.repeat(b_s[0].T, SGU_GROUP_DIM, axis=1)
    w_s3 = w_s[0]
    w_st3 = jnp.swapaxes(w_s3, 1, 2)

    inv_freq = ROPE_THETA ** (-jnp.arange(0, QK_ROPE_DIM, 2, dtype=F32) / QK_ROPE_DIM)
    invf_lane = jnp.concatenate([jnp.zeros((ROPE_LO,), F32), inv_freq, inv_freq,
                                 jnp.zeros((LANES - ROPE_HI,), F32)]).reshape(1, LANES)
    c_t, sa_t, sb_t = _rope_tables(positions.reshape(SEQ, 1), invf_lane)

    h_a = _mm(x2, wa, bias=ba, tm=512, tn=512, tk=D_MODEL, name="in_proj_a")
    h_b = _mm(x2, wb, bias=bb, tm=512, tn=512, tk=D_MODEL, name="in_proj_b")
    h_c = _mm(x2, wc, bias=bc, tm=512, tn=SEG_C, tk=D_MODEL, name="in_proj_c")
    q, k, kt, vx, vxt = _mla_prep(h_c, gq, g_kv, wq, wkn, wv, c_t, sa_t, sb_t)
    o, lse = _attn_fwd(q, kt, vx)
    y_b = _sgu_fwd(h_b, sgu_ln_g, sgu_ln_b, w_s3, bias_full)

    (loss_row, dx_res, dh_a, d_o, d_yb, p_oa, p_ob, p_out, d_lng, d_lnb) = _merge(
        x2, o, h_a, y_b, tgt, w_oa_f, w_ob_f, w_out_f, ln_g, ln_b)
    dh_b, d_ws, d_bs_t, d_slg, d_slb = _sgu_bwd(h_b, d_yb, sgu_ln_g, sgu_ln_b, w_s3, w_st3, bias_full)
    dq, dk, dv = _attn_bwd(q, kt, k, vxt, d_o, o, lse)
    dh_c, p_uq, d_wkn, d_wv, d_gq, d_gkv = _mla_bwd(dq, dk, dv, h_c, gq, g_kv, wq, wkn, wv, c_t, sa_t, sb_t)

    dx = _mm(dh_a, wa, tb=True, add=dx_res, tm=512, tn=D_MODEL, tk=512, name="dx_a")
    dx = _mm(dh_b, wb, tb=True, add=dx, tm=512, tn=D_MODEL, tk=512, name="dx_b")
    dx = _mm(dh_c, wc, tb=True, add=dx, tm=512, tn=D_MODEL, tk=SEG_C, name="dx_c")
    d_wa, d_ba = _mm(x2, dh_a, ta=True, colsum=True, out_dtype=BF16, tm=512, tn=512, tk=512, name="dw_in_a")
    d_wb, d_bb = _mm(x2, dh_b, ta=True, colsum=True, out_dtype=BF16, tm=512, tn=512, tk=512, name="dw_in_b")
    d_wc, d_bc = _mm(x2, dh_c, ta=True, colsum=True, out_dtype=BF16, tm=512, tn=SEG_C, tk=512, name="dw_in_c")

    p_b_in = _from_segments(d_ba, d_bb, d_bc)
    p_w_ukv = jnp.concatenate([d_wkn.reshape(KV_LORA_RANK, MLA_HEADS, HEAD_PAD)[:, :, :QK_NOPE_DIM],
                               d_wv.reshape(KV_LORA_RANK, MLA_HEADS, HEAD_PAD)[:, :, :V_HEAD_DIM]], axis=-1)
    p_g_q = d_gq[:, :Q_LORA_RANK]
    p_b_s = d_bs_t[:, :SGU_GROUPS].T
    sharded = (d_wa, d_wb, d_wc, p_uq, p_oa, p_ob, p_out)
    replicated = [p_b_in, p_g_q, d_gkv, p_w_ukv, d_slg, d_slb, d_ws, p_b_s, d_lng, d_lnb]
    return loss_row, dx, sharded, replicated


_NAMES = ["w_in", "b_in", "g_q", "w_uq", "g_kv", "w_ukv", "w_oa", "sgu_ln_g", "sgu_ln_b", "w_s", "b_s", "w_ob",
          "w_out", "ln_g", "ln_b"]
_REPLICATED = ["b_in", "g_q", "g_kv", "w_ukv", "sgu_ln_g", "sgu_ln_b", "w_s", "b_s", "ln_g", "ln_b"]


def _reduce_and_update(partials, weights, moms, vars_):
    loss_row, dx, sharded, replicated = partials
    rep_flat = jnp.concatenate([a.reshape(-1) for a in replicated] + [loss_row[0, :1]])
    rep_flat = jnp.pad(rep_flat, (0, N_DEV * PACK_R_ROWS * LANES - rep_flat.size))
    g_in, g_uq, g_oa, g_ob, g_out, rep_slice = _reduce_grads(*sharded, rep_flat.reshape(N_DEV, PACK_R_ROWS, LANES))
    rep_sum = _all_gather(rep_slice, "gather_replicated").reshape(-1)
    grads, pos = dict(w_in=g_in, w_uq=g_uq, w_oa=g_oa, w_ob=g_ob, w_out=g_out), 0
    for nm in _REPLICATED:
        grads[nm] = rep_sum[pos:pos + weights[nm].size]
        pos += weights[nm].size
    loss = rep_sum[pos]
    grads = {nm: grads[nm].reshape(weights[nm].shape) for nm in _NAMES}
    deltas, new_m, new_v = _adamw_all([weights[nm] for nm in _NAMES], [grads[nm] for nm in _NAMES],
                                      [moms[nm] for nm in _NAMES], [vars_[nm] for nm in _NAMES])
    return (loss, dx.reshape(1, SEQ, D_MODEL), *[grads[nm] for nm in _NAMES], *deltas, *new_m, *new_v)
```

```python
import math

import jax
import jax.numpy as jnp
from jax import lax
from jax.experimental import pallas as pl
from jax.experimental.pallas import tpu as pltpu

F32 = jnp.float32
BF16 = jnp.bfloat16

D_MODEL = 1024
SEQ = 2048
N_DEV = 8
MLA_HEADS = 8
Q_LORA_RANK = 384
KV_LORA_RANK = 128
QK_NOPE_DIM = 64
QK_ROPE_DIM = 32
V_HEAD_DIM = 64
QK_HEAD_DIM = QK_NOPE_DIM + QK_ROPE_DIM
MLA_WIDTH = MLA_HEADS * V_HEAD_DIM
ROPE_THETA = 10000.0
SGU_GROUPS = 8
SGU_GROUP_DIM = 64
SGU_WIDTH = SGU_GROUPS * SGU_GROUP_DIM
CHUNK = 128
RMS_EPS = 1e-6
LN_EPS = 1e-5
DN_ALPHA = 2.0 ** 0.25
IN_WIDTH = 4640
ATTN_SCALE = QK_HEAD_DIM ** -0.5

ADAM_LR = 0.001
ADAM_B1 = 0.9
ADAM_B2 = 0.999
ADAM_EPS = 1e-08
ADAM_WD = 0.01
ADAM_STEP = 10

LANES = 128
HEAD_PAD = 128
ROPE_LO = QK_NOPE_DIM
ROPE_MID = ROPE_LO + QK_ROPE_DIM // 2
ROPE_HI = ROPE_LO + QK_ROPE_DIM
CQ_PAD = 512

SEG_A = 2560
SEG_B = 1536
SEG_C = 768

PACK_R_ROWS = 272
VMEM_BIG = 56 * 1024 * 1024
VMEM_MID = 40 * 1024 * 1024


def _sigmoid(x):
    return 1.0 / (1.0 + jnp.exp(-x))


def _gelu_and_grad(x):
    c0 = math.sqrt(2.0 / math.pi)
    x2 = x * x
    t = jnp.tanh(c0 * (x + 0.044715 * x * x2))
    g = 0.5 * x * (1.0 + t)
    dg = 0.5 * (1.0 + t) + 0.5 * x * (1.0 - t * t) * (c0 * (1.0 + 3.0 * 0.044715 * x2))
    return g, dg


def _dot(a, b, dims):
    return lax.dot_general(a, b, (dims, ((), ())), preferred_element_type=F32)


_NN = ((1,), (0,))
_NT = ((1,), (1,))
_TN = ((0,), (0,))


def _store_grad(dh_ref, db_ref, col, val):
    cols = slice(col, col + val.shape[1])
    dh_ref[:, cols] = val.astype(BF16)
    db_ref[:, cols] += jnp.sum(val, axis=0, keepdims=True)


def _mm(a, b, *, tb=False, bias=None, add=None, out_dtype=F32, tm, tn, name):
    m, k = a.shape
    n = b.shape[0] if tb else b.shape[1]
    assert m % tm == 0 and n % tn == 0
    dims = _NT if tb else _NN

    def body(*refs):
        a_ref, b_ref = refs[0], refs[1]
        pos = 2
        r = _dot(a_ref[...], b_ref[...], dims)
        if bias is not None:
            r = r + refs[pos][...]; pos += 1
        if add is not None:
            r = r + refs[pos][...]; pos += 1
        refs[pos][...] = r.astype(out_dtype)

    b_spec = pl.BlockSpec((tn, k), lambda j, i: (j, 0)) if tb else pl.BlockSpec((k, tn), lambda j, i: (0, j))
    in_specs, args = [pl.BlockSpec((tm, k), lambda j, i: (i, 0)), b_spec], [a, b]
    if bias is not None:
        in_specs.append(pl.BlockSpec((1, tn), lambda j, i: (0, j))); args.append(bias)
    if add is not None:
        in_specs.append(pl.BlockSpec((tm, tn), lambda j, i: (i, j))); args.append(add)
    return pl.pallas_call(
        body, name=name, grid=(n // tn, m // tm), in_specs=in_specs,
        out_specs=pl.BlockSpec((tm, tn), lambda j, i: (i, j)), out_shape=jax.ShapeDtypeStruct((m, n), out_dtype),
        compiler_params=pltpu.CompilerParams(dimension_semantics=("arbitrary", "arbitrary"), vmem_limit_bytes=VMEM_BIG),
    )(*args)


def _cast_x(x2):
    tm = 256

    def body(x_ref, xb_ref, xt_ref):
        x = x_ref[...]
        xb_ref[...] = x.astype(BF16)
        xt_ref[...] = x.T.astype(BF16)

    return pl.pallas_call(
        body, name="cast_x", grid=(SEQ // tm,),
        in_specs=[pl.BlockSpec((tm, D_MODEL), lambda i: (i, 0))],
        out_specs=[pl.BlockSpec((tm, D_MODEL), lambda i: (i, 0)), pl.BlockSpec((D_MODEL, tm), lambda i: (0, i))],
        out_shape=[jax.ShapeDtypeStruct((SEQ, D_MODEL), BF16), jax.ShapeDtypeStruct((D_MODEL, SEQ), BF16)],
        compiler_params=pltpu.CompilerParams(dimension_semantics=("arbitrary",)),
    )(x2)


def _rope_tables(pos_col, invf_lane):
    def body(pos_ref, invf_ref, c_ref, sa_ref, sb_ref):
        ang = pos_ref[...].astype(F32) * invf_ref[...]
        cs, sn = jnp.cos(ang), jnp.sin(ang)
        lane = lax.broadcasted_iota(jnp.int32, ang.shape, 1)
        c_ref[...] = jnp.where(lane < ROPE_LO, 1.0, jnp.where(lane < ROPE_HI, cs, 0.0))
        sa_ref[...] = jnp.where(jnp.logical_and(lane >= ROPE_LO, lane < ROPE_MID), -sn, 0.0)
        sb_ref[...] = jnp.where(jnp.logical_and(lane >= ROPE_MID, lane < ROPE_HI), sn, 0.0)

    shp = jax.ShapeDtypeStruct((SEQ, LANES), F32)
    return pl.pallas_call(body, name="rope_tables", out_shape=[shp, shp, shp])(pos_col, invf_lane)


def _rope(x, c, sa, sb):
    return x * c + pltpu.roll(x, LANES - 16, 1) * sa + pltpu.roll(x, 16, 1) * sb


def _rope_t(dy, c, sa, sb):
    return dy * c + pltpu.roll(dy * sa, 16, 1) + pltpu.roll(dy * sb, LANES - 16, 1)


def _mla_prep(h_c, gq, gkv, wq, wkn, wvx, c_t, sa_t, sb_t):
    tm = 256
    hw = MLA_HEADS * HEAD_PAD

    def body(cq_ref, ckv_ref, kpe_ref, gq_ref, gkv_ref, wq_ref, wkn_ref, wvx_ref, c_ref, sa_ref, sb_ref,
             q_ref, k_ref, kt_ref, vx_ref, vxt_ref):
        c, sa, sb = c_ref[...], sa_ref[...], sb_ref[...]
        cq = cq_ref[...]
        rq = lax.rsqrt(jnp.sum(cq * cq, axis=1, keepdims=True) * (1.0 / Q_LORA_RANK) + RMS_EPS)
        cqn = ((cq * rq) * gq_ref[...]).astype(BF16)
        qall = _dot(cqn, wq_ref[...], _NN)
        for h in range(MLA_HEADS):
            sl = slice(HEAD_PAD * h, HEAD_PAD * (h + 1))
            q_ref[:, sl] = (_rope(qall[:, sl], c, sa, sb) * ATTN_SCALE).astype(BF16)
        ckv = ckv_ref[...]
        rkv = lax.rsqrt(jnp.sum(ckv * ckv, axis=1, keepdims=True) * (1.0 / KV_LORA_RANK) + RMS_EPS)
        ckvn = ((ckv * rkv) * gkv_ref[...]).astype(BF16)
        knall = _dot(ckvn, wkn_ref[...], _NN)
        vall = _dot(ckvn, wvx_ref[...], _NN)
        kper = _rope(kpe_ref[...], c, sa, sb)
        ones_half = (lax.broadcasted_iota(jnp.int32, (tm, HEAD_PAD), 1) >= V_HEAD_DIM).astype(F32)
        for h in range(MLA_HEADS):
            sl = slice(HEAD_PAD * h, HEAD_PAD * (h + 1))
            kh = knall[:, sl] + kper
            vh = vall[:, sl] + ones_half
            k_ref[:, sl] = kh.astype(BF16)
            kt_ref[sl, :] = kh.T.astype(BF16)
            vx_ref[:, sl] = vh.astype(BF16)
            vxt_ref[sl, :] = vh.T.astype(BF16)

    full = lambda shape: pl.BlockSpec(shape, lambda i: (0, 0))
    tab = pl.BlockSpec((tm, LANES), lambda i: (i, 0))
    row = pl.BlockSpec((tm, hw), lambda i: (i, 0))
    col = pl.BlockSpec((hw, tm), lambda i: (0, i))
    return pl.pallas_call(
        body, name="mla_prep", grid=(SEQ // tm,),
        in_specs=[pl.BlockSpec((tm, CQ_PAD), lambda i: (i, 0)),
                  pl.BlockSpec((tm, LANES), lambda i: (i, CQ_PAD // LANES)),
                  pl.BlockSpec((tm, LANES), lambda i: (i, CQ_PAD // LANES + 1)),
                  full((1, CQ_PAD)), full((1, KV_LORA_RANK)),
                  full((CQ_PAD, hw)), full((KV_LORA_RANK, hw)), full((KV_LORA_RANK, hw)), tab, tab, tab],
        out_specs=[row, row, col, row, col],
        out_shape=[jax.ShapeDtypeStruct((SEQ, hw), BF16), jax.ShapeDtypeStruct((SEQ, hw), BF16),
                   jax.ShapeDtypeStruct((hw, SEQ), BF16), jax.ShapeDtypeStruct((SEQ, hw), BF16),
                   jax.ShapeDtypeStruct((hw, SEQ), BF16)],
        compiler_params=pltpu.CompilerParams(dimension_semantics=("arbitrary",), vmem_limit_bytes=VMEM_MID),
    )(h_c, h_c, h_c, gq, gkv, wq, wkn, wvx, c_t, sa_t, sb_t)


ATT_T = 512
ATT_STRIP = 64


def _attn_fwd(q, kt, vx):
    t, rs = ATT_T, ATT_STRIP

    def body(q_ref, kt_ref, vx_ref, o_ref, l_ref, s_scr, p_scr, m_scr, a_scr, acc_scr):
        qi = pl.program_id(1)
        lane = lax.broadcasted_iota(jnp.int32, (t, LANES), 1)
        m_scr[...] = jnp.full((2, t, LANES), -1e30, F32)
        acc_scr[...] = jnp.zeros((2, t, LANES), F32)

        def block(j, masked):
            off = pl.multiple_of(j * t, t)
            for a in range(2):
                sl = slice(HEAD_PAD * a, HEAD_PAD * (a + 1))
                s_scr[a] = _dot(q_ref[:, sl], kt_ref[sl, pl.ds(off, t)], _NN)
                for r in range(t // rs):
                    rows = slice(rs * r, rs * (r + 1))
                    s = s_scr[a, rows, :]
                    if masked:
                        rowi = lax.broadcasted_iota(jnp.int32, (rs, t), 0) + rs * r
                        coli = lax.broadcasted_iota(jnp.int32, (rs, t), 1)
                        s = jnp.where(coli <= rowi, s, -1e30)
                    m_old = m_scr[a, rows, :]
                    m_new = jnp.maximum(m_old, jnp.max(s, axis=1, keepdims=True))
                    p_scr[a, rows, :] = jnp.exp(s - m_new[:, :1]).astype(BF16)
                    a_scr[a, rows, :] = jnp.exp(m_old - m_new)
                    m_scr[a, rows, :] = m_new
                acc_scr[a] = acc_scr[a] * a_scr[a] + _dot(p_scr[a], vx_ref[pl.ds(off, t), sl], _NN)

        def step(j, carry):
            block(j, False)
            return carry
        lax.fori_loop(0, qi, step, 0)
        block(qi, True)
        res = []
        for a in range(2):
            acc = acc_scr[a]
            l = acc[:, V_HEAD_DIM:V_HEAD_DIM + 1]
            res.append((acc / l, m_scr[a] + jnp.log(l)))
        o_ref[...] = jnp.where(lane < V_HEAD_DIM, res[0][0], pltpu.roll(res[1][0], V_HEAD_DIM, 1))
        l_ref[...] = jnp.where(lane < V_HEAD_DIM, res[0][1], res[1][1])

    return pl.pallas_call(
        body, name="attn_fwd", grid=(MLA_HEADS // 2, SEQ // t),
        in_specs=[pl.BlockSpec((t, 2 * HEAD_PAD), lambda p, i: (i, p)),
                  pl.BlockSpec((2 * HEAD_PAD, SEQ), lambda p, i: (p, 0)),
                  pl.BlockSpec((SEQ, 2 * HEAD_PAD), lambda p, i: (0, p))],
        out_specs=[pl.BlockSpec((t, LANES), lambda p, i: (i, p)),
                   pl.BlockSpec((t, LANES), lambda p, i: (i, p))],
        out_shape=[jax.ShapeDtypeStruct((SEQ, MLA_WIDTH), F32), jax.ShapeDtypeStruct((SEQ, MLA_WIDTH), F32)],
        scratch_shapes=[pltpu.VMEM((2, t, t), F32), pltpu.VMEM((2, t, t), BF16), pltpu.VMEM((2, t, LANES), F32),
                        pltpu.VMEM((2, t, LANES), F32), pltpu.VMEM((2, t, LANES), F32)],
        compiler_params=pltpu.CompilerParams(dimension_semantics=("arbitrary", "arbitrary"), vmem_limit_bytes=VMEM_MID),
    )(q, kt, vx)


def _attn_bwd(q, kt, k, vxt, d_o, o, lse):
    t, rs = ATT_T, ATT_STRIP
    nq = SEQ // t

    def body(q_ref, kt_ref, k_ref, vxt_ref, do_ref, o_ref, l_ref, dq_ref, dk_ref, dv_ref,
             s_scr, dp_scr, p_scr, ds_scr, st_scr):
        dk_ref[...] = jnp.zeros_like(dk_ref)
        dv_ref[...] = jnp.zeros_like(dv_ref)
        lane = lax.broadcasted_iota(jnp.int32, (t, LANES), 1)

        def qtile(i, carry):
            ioff = pl.multiple_of(i * t, t)
            do_i = do_ref[pl.ds(ioff, t), :]
            o_i = o_ref[pl.ds(ioff, t), :]
            l_i = l_ref[pl.ds(ioff, t), :]
            for a in range(2):
                sl = slice(HEAD_PAD * a, HEAD_PAD * (a + 1))
                sel = (lane < V_HEAD_DIM) if a == 0 else (lane >= V_HEAD_DIM)
                doa = jnp.where(sel, do_i, 0.0)
                oa = o_i
                if a == 1:
                    doa = pltpu.roll(doa, V_HEAD_DIM, 1)
                    oa = pltpu.roll(o_i, V_HEAD_DIM, 1)
                st_scr[0] = jnp.broadcast_to(jnp.sum(doa * oa, axis=1, keepdims=True), (t, LANES))
                st_scr[1] = jnp.broadcast_to(l_i[:, V_HEAD_DIM * a:V_HEAD_DIM * a + 1], (t, LANES))
                doa_bf = doa.astype(BF16)
                qa = q_ref[pl.ds(ioff, t), sl]

                def block(j, masked, dq_acc, sl=sl, qa=qa, doa_bf=doa_bf):
                    joff = pl.multiple_of(j * t, t)
                    s_scr[...] = _dot(qa, kt_ref[sl, pl.ds(joff, t)], _NN)
                    dp_scr[...] = _dot(doa_bf, vxt_ref[sl, pl.ds(joff, t)], _NN)
                    for r in range(t // rs):
                        rows = slice(rs * r, rs * (r + 1))
                        p = jnp.exp(s_scr[rows, :] - st_scr[1, rows, :1])
                        if masked:
                            rowi = lax.broadcasted_iota(jnp.int32, (rs, t), 0) + rs * r
                            coli = lax.broadcasted_iota(jnp.int32, (rs, t), 1)
                            p = jnp.where(coli <= rowi, p, 0.0)
                        p_scr[rows, :] = p.astype(BF16)
                        ds_scr[rows, :] = (p * (dp_scr[rows, :] - st_scr[0, rows, :1])).astype(BF16)
                    dk_ref[pl.ds(joff, t), sl] += _dot(ds_scr[...], qa, _TN)
                    dv_ref[pl.ds(joff, t), sl] += _dot(p_scr[...], doa_bf, _TN)
                    return dq_acc + _dot(ds_scr[...], k_ref[pl.ds(joff, t), sl], _NN)

                dq_acc = lax.fori_loop(0, i, lambda j, acc: block(j, False, acc), jnp.zeros((t, HEAD_PAD), F32))
                dq_ref[pl.ds(ioff, t), sl] = block(i, True, dq_acc)
            return carry

        lax.fori_loop(0, nq, qtile, 0)

    hw = MLA_HEADS * HEAD_PAD
    wide = pl.BlockSpec((SEQ, 2 * HEAD_PAD), lambda p: (0, p))
    wide_t = pl.BlockSpec((2 * HEAD_PAD, SEQ), lambda p: (p, 0))
    narrow = pl.BlockSpec((SEQ, LANES), lambda p: (0, p))
    return pl.pallas_call(
        body, name="attn_bwd", grid=(MLA_HEADS // 2,),
        in_specs=[wide, wide_t, wide, wide_t, narrow, narrow, narrow],
        out_specs=[wide, wide, wide],
        out_shape=[jax.ShapeDtypeStruct((SEQ, hw), F32)] * 3,
        scratch_shapes=[pltpu.VMEM((t, t), F32), pltpu.VMEM((t, t), F32), pltpu.VMEM((t, t), BF16),
                        pltpu.VMEM((t, t), BF16), pltpu.VMEM((2, t, LANES), F32)],
        compiler_params=pltpu.CompilerParams(dimension_semantics=("arbitrary",), vmem_limit_bytes=VMEM_BIG),
    )(q, kt, k, vxt, d_o, o, lse)


def _sgu_math(u, v, zb, lg, lb, ws_ref, bias):
    ug, dug = _gelu_and_grad(u)
    vg, dvg = _gelu_and_grad(v)
    mu = jnp.mean(vg, axis=1, keepdims=True)
    xc = vg - mu
    rstd = lax.rsqrt(jnp.mean(xc * xc, axis=1, keepdims=True) + LN_EPS)
    xh = xc * rstd
    vn_bf = (xh * lg + lb).astype(BF16)
    grp = lax.broadcasted_iota(jnp.int32, (CHUNK, SGU_WIDTH), 1) // SGU_GROUP_DIM
    r_i = lax.broadcasted_iota(jnp.int32, (CHUNK, CHUNK), 0)
    c_i = lax.broadcasted_iota(jnp.int32, (CHUNK, CHUNK), 1)
    tri, tri_t = r_i >= c_i, r_i <= c_i
    mixed = bias
    for g in range(SGU_GROUPS):
        wt = jnp.where(tri, ws_ref[g], 0.0).astype(BF16)
        mixed = mixed + jnp.where(grp == g, _dot(wt, vn_bf, _NN), 0.0)
    sb = _sigmoid(zb)
    return ug, dug, dvg, rstd, xh, vn_bf, grp, tri, tri_t, mixed, sb


def _sgu_fwd(h_b, lg, lb, w_s, bias_full):
    def body(u_ref, v_ref, zb_ref, lg_ref, lb_ref, ws_ref, bias_ref, yb_ref):
        zb = zb_ref[...]
        ug, _, _, _, _, _, _, _, _, mixed, sb = _sgu_math(u_ref[...], v_ref[...], zb, lg_ref[...], lb_ref[...],
                                                       ws_ref, bias_ref[...])
        yb_ref[...] = (ug * mixed) * (zb * sb)

    blk = lambda c: pl.BlockSpec((CHUNK, SGU_WIDTH), lambda i, c=c: (i, c))
    full2 = lambda shape: pl.BlockSpec(shape, lambda i: (0, 0))
    return pl.pallas_call(
        body, name="sgu_fwd", grid=(SEQ // CHUNK,),
        in_specs=[blk(0), blk(1), blk(2), full2((1, SGU_WIDTH)), full2((1, SGU_WIDTH)),
                  pl.BlockSpec((SGU_GROUPS, CHUNK, CHUNK), lambda i: (0, 0, 0)), full2((CHUNK, SGU_WIDTH))],
        out_specs=pl.BlockSpec((CHUNK, SGU_WIDTH), lambda i: (i, 0)),
        out_shape=jax.ShapeDtypeStruct((SEQ, SGU_WIDTH), F32),
        compiler_params=pltpu.CompilerParams(dimension_semantics=("arbitrary",)),
    )(h_b, h_b, h_b, lg, lb, w_s, bias_full)


def _sgu_bwd(h_b, d_yb, lg, lb, w_s, w_st, bias_full):
    nsteps = SEQ // CHUNK

    def body(u_ref, v_ref, zb_ref, dyb_ref, lg_ref, lb_ref, ws_ref, wst_ref, bias_ref,
             dhb_ref, dws_ref, dbs_ref, dlg_ref, dlb_ref, dbb_ref, dbias_acc):
        step = pl.program_id(0)

        @pl.when(step == 0)
        def _():
            dbb_ref[...] = jnp.zeros_like(dbb_ref)
            dws_ref[...] = jnp.zeros_like(dws_ref)
            dlg_ref[...] = jnp.zeros_like(dlg_ref)
            dlb_ref[...] = jnp.zeros_like(dlb_ref)
            dbias_acc[...] = jnp.zeros_like(dbias_acc)

        zb = zb_ref[...]
        lg = lg_ref[...]
        ug, dug, dvg, rstd, xh, vn_bf, grp, tri, tri_t, mixed, sb = _sgu_math(
            u_ref[...], v_ref[...], zb, lg, lb_ref[...], ws_ref, bias_ref[...])
        dyb = dyb_ref[...]
        dsgu = dyb * (zb * sb)
        dzb = dyb * (ug * mixed) * (sb * (1.0 + zb * (1.0 - sb)))
        du = dsgu * mixed * dug
        dmixed = dsgu * ug
        dbias_acc[...] += dmixed
        dvn = jnp.zeros((CHUNK, SGU_WIDTH), F32)
        for g in range(SGU_GROUPS):
            dm_g = jnp.where(grp == g, dmixed, 0.0).astype(BF16)
            wtt = jnp.where(tri_t, wst_ref[g], 0.0).astype(BF16)
            dvn = dvn + _dot(wtt, dm_g, _NN)
            dws_ref[g] += jnp.where(tri, _dot(dm_g, vn_bf, _NT), 0.0)
        dlg_ref[...] += jnp.sum(dvn * xh, axis=0, keepdims=True)
        dlb_ref[...] += jnp.sum(dvn, axis=0, keepdims=True)
        dxh = dvn * lg
        dvgel = rstd * (dxh - jnp.mean(dxh, axis=1, keepdims=True) - xh * jnp.mean(dxh * xh, axis=1, keepdims=True))
        _store_grad(dhb_ref, dbb_ref, 0, du)
        _store_grad(dhb_ref, dbb_ref, SGU_WIDTH, dvgel * dvg)
        _store_grad(dhb_ref, dbb_ref, 2 * SGU_WIDTH, dzb)

        @pl.when(step == nsteps - 1)
        def _():
            acc = dbias_acc[...]
            lane = lax.broadcasted_iota(jnp.int32, (CHUNK, LANES), 1)
            out = jnp.zeros((CHUNK, LANES), F32)
            for g in range(SGU_GROUPS):
                sg = jnp.sum(jnp.where(grp == g, acc, 0.0), axis=1, keepdims=True)
                out = jnp.where(lane == g, sg, out)
            dbs_ref[...] = out

    blk = lambda c: pl.BlockSpec((CHUNK, SGU_WIDTH), lambda i, c=c: (i, c))
    full2 = lambda shape: pl.BlockSpec(shape, lambda i: (0, 0))
    full3 = pl.BlockSpec((SGU_GROUPS, CHUNK, CHUNK), lambda i: (0, 0, 0))
    return pl.pallas_call(
        body, name="sgu_bwd", grid=(nsteps,),
        in_specs=[blk(0), blk(1), blk(2), pl.BlockSpec((CHUNK, SGU_WIDTH), lambda i: (i, 0)),
                  full2((1, SGU_WIDTH)), full2((1, SGU_WIDTH)), full3, full3, full2((CHUNK, SGU_WIDTH))],
        out_specs=[pl.BlockSpec((CHUNK, SEG_B), lambda i: (i, 0)), full3, full2((CHUNK, LANES)),
                   full2((1, SGU_WIDTH)), full2((1, SGU_WIDTH)), full2((1, SEG_B))],
        out_shape=[jax.ShapeDtypeStruct((SEQ, SEG_B), BF16),
                   jax.ShapeDtypeStruct((SGU_GROUPS, CHUNK, CHUNK), F32),
                   jax.ShapeDtypeStruct((CHUNK, LANES), F32),
                   jax.ShapeDtypeStruct((1, SGU_WIDTH), F32), jax.ShapeDtypeStruct((1, SGU_WIDTH), F32),
                   jax.ShapeDtypeStruct((1, SEG_B), F32)],
        scratch_shapes=[pltpu.VMEM((CHUNK, SGU_WIDTH), F32)],
        compiler_params=pltpu.CompilerParams(dimension_semantics=("arbitrary",)),
    )(h_b, h_b, h_b, d_yb, lg, lb, w_s, w_st, bias_full)


def _merge(x, o, h_a, y_b, target, w_oa, w_ob, w_out, ln_g, ln_b):
    tm = 256
    nsteps = SEQ // tm

    def body(x_ref, o_ref, ga_ref, gb_ref, za_ref, yb_ref, tgt_ref, woa_ref, wob_ref, wout_ref, lng_ref, lnb_ref,
             loss_ref, dxr_ref, dha_ref, do_ref, dyb_ref, poa_ref, pob_ref, pout_ref, dlng_ref, dlnb_ref, dba_ref,
             dwoa_ref, dwob_ref, dwout_ref):
        step = pl.program_id(0)

        @pl.when(step == 0)
        def _():
            for r in (loss_ref, dwoa_ref, dwob_ref, dwout_ref, dlng_ref, dlnb_ref, dba_ref):
                r[...] = jnp.zeros_like(r)

        o = o_ref[...]
        za = za_ref[...]
        sa = _sigmoid(za)
        ya_bf = (o * (za * sa)).astype(BF16)
        yb_bf = yb_ref[...].astype(BF16)
        woa, wob, wout = woa_ref[...], wob_ref[...], wout_ref[...]
        pa = _dot(ya_bf, woa, _NN)
        pb = _dot(yb_bf, wob, _NN)
        sga = _sigmoid(ga_ref[...])
        sgb = _sigmoid(gb_ref[...])
        merged_bf = (sga * pa + sgb * pb).astype(BF16)
        r = DN_ALPHA * x_ref[...] + _dot(merged_bf, wout, _NN)
        mu = jnp.mean(r, axis=1, keepdims=True)
        rc = r - mu
        rstd = lax.rsqrt(jnp.mean(rc * rc, axis=1, keepdims=True) + LN_EPS)
        xh = rc * rstd
        lng = lng_ref[...]
        y = xh * lng + lnb_ref[...]
        e = y - tgt_ref[...]
        loss_ref[...] += 0.5 * jnp.sum(jnp.sum(e * e, axis=1, keepdims=True) * (1.0 / D_MODEL), axis=0, keepdims=True)

        dy = e * (1.0 / D_MODEL)
        dlng_ref[...] += jnp.sum(dy * xh, axis=0, keepdims=True)
        dlnb_ref[...] += jnp.sum(dy, axis=0, keepdims=True)
        dxh = dy * lng
        dr = rstd * (dxh - jnp.mean(dxh, axis=1, keepdims=True) - xh * jnp.mean(dxh * xh, axis=1, keepdims=True))
        dxr_ref[...] = DN_ALPHA * dr
        dr_bf = dr.astype(BF16)
        dwout_ref[...] += _dot(merged_bf, dr_bf, _TN)
        dmerged = _dot(dr_bf, wout, _NT)
        dpa_bf = (dmerged * sga).astype(BF16)
        dpb_bf = (dmerged * sgb).astype(BF16)
        _store_grad(dha_ref, dba_ref, 0, dmerged * pa * (sga * (1.0 - sga)))
        _store_grad(dha_ref, dba_ref, D_MODEL, dmerged * pb * (sgb * (1.0 - sgb)))
        dwoa_ref[...] += _dot(ya_bf, dpa_bf, _TN)
        dwob_ref[...] += _dot(yb_bf, dpb_bf, _TN)
        dya = _dot(dpa_bf, woa, _NT)
        dyb_ref[...] = _dot(dpb_bf, wob, _NT)
        do_ref[...] = dya * (za * sa)
        _store_grad(dha_ref, dba_ref, 2 * D_MODEL, dya * o * (sa * (1.0 + za * (1.0 - sa))))

        @pl.when(step == nsteps - 1)
        def _():
            cols = D_MODEL // N_DEV
            for j in range(N_DEV):
                poa_ref[j] = dwoa_ref[:, cols * j:cols * (j + 1)].astype(BF16)
                pob_ref[j] = dwob_ref[:, cols * j:cols * (j + 1)].astype(BF16)
                pout_ref[j] = dwout_ref[cols * j:cols * (j + 1), :].astype(BF16)

    row = lambda w, c=0: pl.BlockSpec((tm, w), lambda i, c=c: (i, c))
    full = lambda shape: pl.BlockSpec(shape, lambda i: (0, 0))
    full3 = lambda shape: pl.BlockSpec(shape, lambda i: (0, 0, 0))
    return pl.pallas_call(
        body, name="merge", grid=(nsteps,),
        in_specs=[row(D_MODEL), row(MLA_WIDTH), row(D_MODEL, 0), row(D_MODEL, 1), row(MLA_WIDTH, 4), row(SGU_WIDTH),
                  row(D_MODEL), full((MLA_WIDTH, D_MODEL)), full((SGU_WIDTH, D_MODEL)), full((D_MODEL, D_MODEL)),
                  full((1, D_MODEL)), full((1, D_MODEL))],
        out_specs=[full((1, LANES)), row(D_MODEL), row(SEG_A), row(MLA_WIDTH), row(SGU_WIDTH),
                   full3((N_DEV, MLA_WIDTH, D_MODEL // N_DEV)), full3((N_DEV, SGU_WIDTH, D_MODEL // N_DEV)),
                   full3((N_DEV, D_MODEL // N_DEV, D_MODEL)), full((1, D_MODEL)), full((1, D_MODEL)), full((1, SEG_A))],
        out_shape=[jax.ShapeDtypeStruct((1, LANES), F32),
                   jax.ShapeDtypeStruct((SEQ, D_MODEL), F32), jax.ShapeDtypeStruct((SEQ, SEG_A), BF16),
                   jax.ShapeDtypeStruct((SEQ, MLA_WIDTH), F32), jax.ShapeDtypeStruct((SEQ, SGU_WIDTH), F32),
                   jax.ShapeDtypeStruct((N_DEV, MLA_WIDTH, D_MODEL // N_DEV), BF16),
                   jax.ShapeDtypeStruct((N_DEV, SGU_WIDTH, D_MODEL // N_DEV), BF16),
                   jax.ShapeDtypeStruct((N_DEV, D_MODEL // N_DEV, D_MODEL), BF16),
                   jax.ShapeDtypeStruct((1, D_MODEL), F32), jax.ShapeDtypeStruct((1, D_MODEL), F32),
                   jax.ShapeDtypeStruct((1, SEG_A), F32)],
        scratch_shapes=[pltpu.VMEM((MLA_WIDTH, D_MODEL), F32), pltpu.VMEM((SGU_WIDTH, D_MODEL), F32),
                        pltpu.VMEM((D_MODEL, D_MODEL), F32)],
        compiler_params=pltpu.CompilerParams(dimension_semantics=("arbitrary",), vmem_limit_bytes=VMEM_BIG),
    )(x, o, h_a, h_a, h_a, y_b, target, w_oa, w_ob, w_out, ln_g, ln_b)


def _mla_bwd(dq, dk, dv, h_c, gq, gkv, wq, wkn, wv, c_t, sa_t, sb_t):
    tm = 256
    hw = MLA_HEADS * HEAD_PAD

    def body(dq_ref, dk_ref, dv_ref, cq_ref, ckv_ref, gq_ref, gkv_ref, wq_ref, wkn_ref, wv_ref, c_ref, sa_ref, sb_ref,
             dhc_ref, puq_ref, dwkn_ref, dwv_ref, dgq_ref, dgkv_ref, dbc_ref, pre_ref, dwq_ref):
        @pl.when(pl.program_id(0) == 0)
        def _():
            for r in (dwq_ref, dwkn_ref, dwv_ref, dgq_ref, dgkv_ref, dbc_ref):
                r[...] = jnp.zeros_like(r)

        c, sa, sb = c_ref[...], sa_ref[...], sb_ref[...]
        lane = lax.broadcasted_iota(jnp.int32, (tm, LANES), 1)
        rope_lanes = jnp.logical_and(lane >= ROPE_LO, lane < ROPE_HI)

        cq = cq_ref[...]
        gq = gq_ref[...]
        rq = lax.rsqrt(jnp.sum(cq * cq, axis=1, keepdims=True) * (1.0 / Q_LORA_RANK) + RMS_EPS)
        nq = cq * rq
        cqn_bf = (nq * gq).astype(BF16)
        for h in range(MLA_HEADS):
            sl = slice(HEAD_PAD * h, HEAD_PAD * (h + 1))
            pre_ref[:, sl] = _rope_t(dq_ref[:, sl] * ATTN_SCALE, c, sa, sb).astype(BF16)
        dqpre_bf = pre_ref[...]
        dcqn = _dot(dqpre_bf, wq_ref[...], _NT)
        dwq_ref[...] += _dot(cqn_bf, dqpre_bf, _TN)
        dgq_ref[...] += jnp.sum(dcqn * nq, axis=0, keepdims=True)
        dnq = dcqn * gq
        _store_grad(dhc_ref, dbc_ref, 0,
                    rq * (dnq - nq * (jnp.sum(dnq * nq, axis=1, keepdims=True) * (1.0 / Q_LORA_RANK))))

        ckv = ckv_ref[...]
        gkv = gkv_ref[...]
        rkv = lax.rsqrt(jnp.sum(ckv * ckv, axis=1, keepdims=True) * (1.0 / KV_LORA_RANK) + RMS_EPS)
        nkv = ckv * rkv
        ckvn_bf = (nkv * gkv).astype(BF16)
        dk = dk_ref[...]
        dk_bf = dk.astype(BF16)
        dv_bf = dv_ref[...].astype(BF16)
        dckvn = _dot(dk_bf, wkn_ref[...], _NT) + _dot(dv_bf, wv_ref[...], _NT)
        dwkn_ref[...] += _dot(ckvn_bf, dk_bf, _TN)
        dwv_ref[...] += _dot(ckvn_bf, dv_bf, _TN)
        dgkv_ref[...] += jnp.sum(dckvn * nkv, axis=0, keepdims=True)
        dnkv = dckvn * gkv
        _store_grad(dhc_ref, dbc_ref, CQ_PAD, rkv * (
            dnkv - nkv * (jnp.sum(dnkv * nkv, axis=1, keepdims=True) * (1.0 / KV_LORA_RANK))))
        dkpe = jnp.zeros((tm, LANES), F32)
        for h in range(MLA_HEADS):
            dkpe = dkpe + dk[:, HEAD_PAD * h:HEAD_PAD * (h + 1)]
        _store_grad(dhc_ref, dbc_ref, CQ_PAD + LANES, _rope_t(jnp.where(rope_lanes, dkpe, 0.0), c, sa, sb))

        @pl.when(pl.program_id(0) == SEQ // tm - 1)
        def _():
            rows = Q_LORA_RANK // N_DEV
            for j in range(N_DEV):
                for h in range(MLA_HEADS):
                    puq_ref[j, :, QK_HEAD_DIM * h:QK_HEAD_DIM * (h + 1)] = dwq_ref[
                        rows * j:rows * (j + 1), HEAD_PAD * h:HEAD_PAD * h + QK_HEAD_DIM].astype(BF16)

    full = lambda shape: pl.BlockSpec(shape, lambda i: (0, 0))
    row = lambda w, c=0: pl.BlockSpec((tm, w), lambda i, c=c: (i, c))
    return pl.pallas_call(
        body, name="mla_bwd", grid=(SEQ // tm,),
        in_specs=[row(hw), row(hw), row(hw), row(CQ_PAD, 0), row(LANES, CQ_PAD // LANES),
                  full((1, CQ_PAD)), full((1, KV_LORA_RANK)), full((CQ_PAD, hw)), full((KV_LORA_RANK, hw)),
                  full((KV_LORA_RANK, hw)), row(LANES), row(LANES), row(LANES)],
        out_specs=[row(SEG_C), pl.BlockSpec((N_DEV, Q_LORA_RANK // N_DEV, MLA_HEADS * QK_HEAD_DIM), lambda i: (0, 0, 0)),
                   full((KV_LORA_RANK, hw)), full((KV_LORA_RANK, hw)),
                   full((1, CQ_PAD)), full((1, KV_LORA_RANK)), full((1, SEG_C))],
        out_shape=[jax.ShapeDtypeStruct((SEQ, SEG_C), BF16),
                   jax.ShapeDtypeStruct((N_DEV, Q_LORA_RANK // N_DEV, MLA_HEADS * QK_HEAD_DIM), BF16),
                   jax.ShapeDtypeStruct((KV_LORA_RANK, hw), F32), jax.ShapeDtypeStruct((KV_LORA_RANK, hw), F32),
                   jax.ShapeDtypeStruct((1, CQ_PAD), F32), jax.ShapeDtypeStruct((1, KV_LORA_RANK), F32),
                   jax.ShapeDtypeStruct((1, SEG_C), F32)],
        scratch_shapes=[pltpu.VMEM((tm, hw), BF16), pltpu.VMEM((CQ_PAD, hw), F32)],
        compiler_params=pltpu.CompilerParams(dimension_semantics=("arbitrary",), vmem_limit_bytes=VMEM_MID),
    )(dq, dk, dv, h_c, h_c, gq, gkv, wq, wkn, wv, c_t, sa_t, sb_t)


def _adamw_all(ws, gs, ms, vs):
    n = len(ws)
    c1 = 1.0 / (1.0 - ADAM_B1 ** ADAM_STEP)
    c2 = 1.0 / (1.0 - ADAM_B2 ** ADAM_STEP)

    def body(*refs):
        for idx in range(n):
            w, g, m, v = (refs[idx][...], refs[n + idx][...], refs[2 * n + idx][...], refs[3 * n + idx][...])
            m_new = ADAM_B1 * m + (1.0 - ADAM_B1) * g
            v_new = ADAM_B2 * v + (1.0 - ADAM_B2) * (g * g)
            delta = -ADAM_LR * ((m_new * c1) / (jnp.sqrt(v_new * c2) + ADAM_EPS) + ADAM_WD * w)
            refs[4 * n + idx][...] = delta
            refs[5 * n + idx][...] = m_new
            refs[6 * n + idx][...] = v_new

    shapes = [jax.ShapeDtypeStruct(w.shape, F32) for w in ws]
    outs = pl.pallas_call(
        body, name="adamw", out_shape=shapes * 3,
        compiler_params=pltpu.CompilerParams(vmem_limit_bytes=VMEM_BIG),
    )(*ws, *gs, *ms, *vs)
    return outs[:n], outs[n:2 * n], outs[2 * n:]


def _all_gather(block, name):
    rows, width = block.shape

    def body(x_ref, out_ref, send_sems, recv_sems, local_sem):
        x, y, c = lax.axis_index("x"), lax.axis_index("y"), lax.axis_index("c")
        me, sibling = (x, y, c), (x, y, 1 - c)
        chips = [(1 - x, y), (x, 1 - y), (1 - x, 1 - y)]

        def slab(px, py, pc):
            return out_ref.at[4 * px + 2 * py + pc]

        def copy(k, blk, to, src=None):
            return pltpu.make_async_remote_copy(
                src_ref=slab(*blk) if src is None else src, dst_ref=slab(*blk),
                send_sem=send_sems.at[k], recv_sem=recv_sems.at[k],
                device_id=to, device_id_type=pl.DeviceIdType.MESH)

        mine = pltpu.make_async_copy(x_ref, slab(*me), local_sem)
        mine.start()
        first = [copy(0, me, sibling, src=x_ref)]
        first += [copy(1 + j, me, (*chip, c), src=x_ref) for j, chip in enumerate(chips)]
        for cp in first:
            cp.start()
        passed = [copy(4 + j, (*chip, c), sibling) for j, chip in enumerate(chips)]
        for j, chip in enumerate(chips):
            copy(1 + j, (*chip, c), me).wait_recv()
            passed[j].start()
        copy(0, sibling, me).wait_recv()
        for j, chip in enumerate(chips):
            copy(4 + j, (*chip, 1 - c), me).wait_recv()
        for cp in first + passed:
            cp.wait_send()
        mine.wait()

    return pl.pallas_call(
        body, name=name,
        out_shape=jax.ShapeDtypeStruct((N_DEV, rows, width), block.dtype),
        in_specs=[pl.BlockSpec(memory_space=pltpu.VMEM)],
        out_specs=pl.BlockSpec(memory_space=pltpu.VMEM),
        scratch_shapes=[pltpu.SemaphoreType.DMA((7,)), pltpu.SemaphoreType.DMA((7,)), pltpu.SemaphoreType.DMA],
        compiler_params=pltpu.CompilerParams(vmem_limit_bytes=VMEM_MID),
    )(block)


SHARD_W = IN_WIDTH // N_DEV

_PIECES = [(0, 384, 2, 0), (384, 512, 2, CQ_PAD), (512, 544, 2, CQ_PAD + LANES + ROPE_LO),
           (544, 1056, 0, 2 * D_MODEL), (1056, 1568, 1, 0), (1568, 2080, 1, SGU_WIDTH),
           (2080, 2592, 1, 2 * SGU_WIDTH), (2592, 3616, 0, 0), (3616, 4640, 0, D_MODEL)]


def _column_runs():
    runs = []
    for n0, n1, seg, d0 in _PIECES:
        for j in range(N_DEV):
            lo, hi = max(n0, j * SHARD_W), min(n1, (j + 1) * SHARD_W)
            if lo < hi:
                runs.append((j, lo - j * SHARD_W, hi - j * SHARD_W, seg, d0 + lo - n0))
    return runs


def _mesh_pos():
    return lax.axis_index("x"), lax.axis_index("y"), lax.axis_index("c")


def _remote(src, dst, send_sems, recv_sems, k, to):
    return pltpu.make_async_remote_copy(src_ref=src, dst_ref=dst, send_sem=send_sems.at[k], recv_sem=recv_sems.at[k],
                                        device_id=to, device_id_type=pl.DeviceIdType.MESH)


def _gather_exchange(gats, send_sems, recv_sems):
    x, y, c = _mesh_pos()
    me, sibling = (x, y, c), (x, y, 1 - c)
    chips = [(1 - x, y), (x, 1 - y), (1 - x, 1 - y)]

    def copy(a, k, blk, to):
        slab = gats[a].at[4 * blk[0] + 2 * blk[1] + blk[2]]
        return _remote(slab, slab, send_sems, recv_sems, 7 * a + k, to)

    arrays = range(len(gats))
    first = [copy(a, 1 + j, me, (*chip, c)) for j, chip in enumerate(chips) for a in arrays]
    first += [copy(a, 0, me, sibling) for a in arrays]
    for cp in first:
        cp.start()
    passed = []
    for j, chip in enumerate(chips):
        for a in arrays:
            copy(a, 1 + j, (*chip, c), me).wait_recv()
            fwd = copy(a, 4 + j, (*chip, c), sibling)
            fwd.start()
            passed.append(fwd)
    for a in arrays:
        copy(a, 0, sibling, me).wait_recv()
    for j, chip in enumerate(chips):
        for a in arrays:
            copy(a, 4 + j, (*chip, 1 - c), me).wait_recv()
    for cp in first + passed:
        cp.wait_send()


def _gather_weights(w_in, w_uq2, w_oa, w_ob, w_out):
    hw = MLA_HEADS * HEAD_PAD
    uq_rows = Q_LORA_RANK // N_DEV
    cols = D_MODEL // N_DEV

    def body(win_ref, wuq_ref, woa_ref, wob_ref, wout_ref, wa_ref, wb_ref, wc_ref, wq_ref, oa_ref, ob_ref, out_ref,
             g_in, g_uq, g_oa, g_ob, g_out, send_sems, recv_sems):
        x, y, c = _mesh_pos()
        me = 4 * x + 2 * y + c
        g_in[me] = win_ref[0].astype(BF16)
        g_uq[me] = wuq_ref[...].astype(BF16)
        g_oa[me] = woa_ref[0].astype(BF16)
        g_ob[me] = wob_ref[0].astype(BF16)
        g_out[me] = wout_ref[0].astype(BF16)
        _gather_exchange([g_in, g_uq, g_oa, g_ob, g_out], send_sems, recv_sems)

        segs = [wa_ref, wb_ref, wc_ref]
        for j, s0, s1, seg, d0 in _column_runs():
            segs[seg][:, d0:d0 + (s1 - s0)] = g_in[j, :, s0:s1]
        zeros = lambda r, w: jnp.zeros((r, w), BF16)
        wc_ref[:, Q_LORA_RANK:CQ_PAD] = zeros(D_MODEL, CQ_PAD - Q_LORA_RANK)
        wc_ref[:, CQ_PAD + LANES:CQ_PAD + LANES + ROPE_LO] = zeros(D_MODEL, ROPE_LO)
        wc_ref[:, CQ_PAD + LANES + ROPE_HI:SEG_C] = zeros(D_MODEL, LANES - ROPE_HI)
        wq_ref[Q_LORA_RANK:CQ_PAD, :] = zeros(CQ_PAD - Q_LORA_RANK, hw)
        for h in range(MLA_HEADS):
            wq_ref[0:Q_LORA_RANK, HEAD_PAD * h + QK_HEAD_DIM:HEAD_PAD * (h + 1)] = zeros(Q_LORA_RANK, HEAD_PAD - QK_HEAD_DIM)
        for j in range(N_DEV):
            for h in range(MLA_HEADS):
                wq_ref[uq_rows * j:uq_rows * (j + 1), HEAD_PAD * h:HEAD_PAD * h + QK_HEAD_DIM] = g_uq[
                    j, :, QK_HEAD_DIM * h:QK_HEAD_DIM * (h + 1)]
            oa_ref[:, cols * j:cols * (j + 1)] = g_oa[j]
            ob_ref[:, cols * j:cols * (j + 1)] = g_ob[j]
            out_ref[cols * j:cols * (j + 1), :] = g_out[j]

    vmem = pl.BlockSpec(memory_space=pltpu.VMEM)
    return pl.pallas_call(
        body, name="gather_weights",
        out_shape=[jax.ShapeDtypeStruct((D_MODEL, SEG_A), BF16), jax.ShapeDtypeStruct((D_MODEL, SEG_B), BF16),
                   jax.ShapeDtypeStruct((D_MODEL, SEG_C), BF16), jax.ShapeDtypeStruct((CQ_PAD, hw), BF16),
                   jax.ShapeDtypeStruct((MLA_WIDTH, D_MODEL), BF16), jax.ShapeDtypeStruct((SGU_WIDTH, D_MODEL), BF16),
                   jax.ShapeDtypeStruct((D_MODEL, D_MODEL), BF16)],
        in_specs=[vmem] * 5, out_specs=[vmem] * 7,
        scratch_shapes=[pltpu.VMEM((N_DEV, D_MODEL, SHARD_W), BF16),
                        pltpu.VMEM((N_DEV, uq_rows, MLA_HEADS * QK_HEAD_DIM), BF16),
                        pltpu.VMEM((N_DEV, MLA_WIDTH, cols), BF16), pltpu.VMEM((N_DEV, SGU_WIDTH, cols), BF16),
                        pltpu.VMEM((N_DEV, cols, D_MODEL), BF16),
                        pltpu.SemaphoreType.DMA((35,)), pltpu.SemaphoreType.DMA((35,))],
        compiler_params=pltpu.CompilerParams(vmem_limit_bytes=VMEM_BIG),
    )(w_in, w_uq2, w_oa, w_ob, w_out)


def _reduce_grads(dwa, dwb, dwc, p_uq, p_oa, p_ob, p_out, p_rep):
    rep_rows = p_rep.shape[1]
    spec = [((D_MODEL, SHARD_W), BF16, 256), (p_uq.shape[1:], BF16, p_uq.shape[1]), (p_oa.shape[1:], BF16, 256),
            (p_ob.shape[1:], BF16, 256), (p_out.shape[1:], BF16, 64), ((rep_rows, LANES), F32, rep_rows)]
    n = len(spec)

    def body(dwa_ref, dwb_ref, dwc_ref, puq_ref, poa_ref, pob_ref, pout_ref, prep_ref,
             gin_ref, guq_ref, goa_ref, gob_ref, gout_ref, grep_ref, pin_ref, *rest):
        ras, tbs, rbs = rest[0:n], rest[n:2 * n], rest[2 * n:3 * n]
        send_sems, recv_sems = rest[3 * n], rest[3 * n + 1]
        x, y, c = _mesh_pos()
        sibling = (x, y, 1 - c)
        parts = [pin_ref, puq_ref, poa_ref, pob_ref, pout_ref, prep_ref]
        outs = [gin_ref.at[0], guq_ref, goa_ref.at[0], gob_ref.at[0], gout_ref.at[0], grep_ref]

        segs = [dwa_ref, dwb_ref, dwc_ref]
        for j, s0, s1, seg, d0 in _column_runs():
            pin_ref[j, :, s0:s1] = segs[seg][:, d0:d0 + (s1 - s0)]

        stage1 = []
        for chip in range(4):
            for a in range(n):
                cp = _remote(parts[a].at[2 * chip + (1 - c)], ras[a].at[chip], send_sems, recv_sems, 7 * a + chip, sibling)
                cp.start()
                stage1.append(cp)
        for cp in stage1:
            cp.wait_recv()

        def rows_loop(a, fn):
            rows, chunk = spec[a][0][0], spec[a][2]
            if rows == chunk:
                fn(pl.ds(0, rows))
            else:
                def step(i, carry):
                    fn(pl.ds(pl.multiple_of(i * chunk, chunk), chunk))
                    return carry
                lax.fori_loop(0, rows // chunk, step, 0)

        def chip_sum(a, chip, sl):
            return parts[a][2 * chip + c, sl, :].astype(F32) + ras[a][chip, sl, :].astype(F32)

        others = [(1 - x, y), (x, 1 - y), (1 - x, 1 - y)]
        stage2 = []
        for k, (cx, cy) in enumerate(others):
            for a in range(n):
                def fill(sl, a=a, k=k, chip=2 * cx + cy):
                    tbs[a][k, sl, :] = chip_sum(a, chip, sl).astype(spec[a][1])
                rows_loop(a, fill)
                cp = _remote(tbs[a].at[k], rbs[a].at[k], send_sems, recv_sems, 7 * a + 4 + k, (cx, cy, c))
                cp.start()
                stage2.append(cp)
        for cp in stage2:
            cp.wait_recv()
        for a in range(n):
            def final(sl, a=a):
                acc = chip_sum(a, 2 * x + y, sl)
                for k in range(3):
                    acc = acc + rbs[a][k, sl, :].astype(F32)
                outs[a][sl, :] = acc
            rows_loop(a, final)
        for cp in stage1 + stage2:
            cp.wait_send()

    vmem = pl.BlockSpec(memory_space=pltpu.VMEM)
    scratch = [pltpu.VMEM((N_DEV, D_MODEL, SHARD_W), BF16)]
    for lead in (4, 3, 3):
        scratch += [pltpu.VMEM((lead,) + tuple(shape), dt) for shape, dt, _ in spec]
    scratch += [pltpu.SemaphoreType.DMA((7 * n,)), pltpu.SemaphoreType.DMA((7 * n,))]
    return pl.pallas_call(
        body, name="reduce_grads",
        out_shape=[jax.ShapeDtypeStruct((1, D_MODEL, SHARD_W), F32), jax.ShapeDtypeStruct(p_uq.shape[1:], F32),
                   jax.ShapeDtypeStruct((1,) + p_oa.shape[1:], F32), jax.ShapeDtypeStruct((1,) + p_ob.shape[1:], F32),
                   jax.ShapeDtypeStruct((1,) + p_out.shape[1:], F32), jax.ShapeDtypeStruct((rep_rows, LANES), F32)],
        in_specs=[vmem] * 8, out_specs=[vmem] * 6, scratch_shapes=scratch,
        compiler_params=pltpu.CompilerParams(vmem_limit_bytes=VMEM_BIG),
    )(dwa, dwb, dwc, p_uq, p_oa, p_ob, p_out, p_rep)


_O_CQ, _O_CKV, _O_KPE, _O_ZA, _O_U, _O_V, _O_ZB, _O_GA, _O_GB = 0, 384, 512, 544, 1056, 1568, 2080, 2592, 3616


def _to_segments(w):
    z = lambda n: jnp.zeros(w.shape[:-1] + (n,), w.dtype)
    seg_a = jnp.concatenate([w[..., _O_GA:_O_GB], w[..., _O_GB:IN_WIDTH], w[..., _O_ZA:_O_U]], axis=-1)
    seg_b = jnp.concatenate([w[..., _O_U:_O_V], w[..., _O_V:_O_ZB], w[..., _O_ZB:_O_GA]], axis=-1)
    seg_c = jnp.concatenate([w[..., _O_CQ:_O_CKV], z(CQ_PAD - Q_LORA_RANK), w[..., _O_CKV:_O_KPE],
                             z(ROPE_LO), w[..., _O_KPE:_O_ZA], z(LANES - ROPE_HI)], axis=-1)
    return seg_a, seg_b, seg_c


def _from_segments(seg_a, seg_b, seg_c):
    kpe0 = CQ_PAD + LANES + ROPE_LO
    return jnp.concatenate([
        seg_c[..., 0:Q_LORA_RANK], seg_c[..., CQ_PAD:CQ_PAD + LANES], seg_c[..., kpe0:kpe0 + QK_ROPE_DIM],
        seg_a[..., 2 * D_MODEL:SEG_A], seg_b, seg_a[..., 0:2 * D_MODEL]], axis=-1)


def kernel(x, positions, w_in, b_in, g_q, w_uq, g_kv, w_ukv, w_oa, sgu_ln_g, sgu_ln_b, w_s, b_s, w_ob, w_out, ln_g, ln_b, loss_target, m_w_in, m_b_in, m_g_q, m_w_uq, m_g_kv, m_w_ukv, m_w_oa, m_sgu_ln_g, m_sgu_ln_b, m_w_s, m_b_s, m_w_ob, m_w_out, m_ln_g, m_ln_b, v_w_in, v_b_in, v_g_q, v_w_uq, v_g_kv, v_w_ukv, v_w_oa, v_sgu_ln_g, v_sgu_ln_b, v_w_s, v_b_s, v_w_ob, v_w_out, v_ln_g, v_ln_b):
    w_uq2 = w_uq[0].reshape(Q_LORA_RANK // N_DEV, MLA_HEADS * QK_HEAD_DIM)
    wa, wb, wc, wq, w_oa_f, w_ob_f, w_out_f = _gather_weights(w_in, w_uq2, w_oa, w_ob, w_out)
    partials = _local_step(x[0], positions, loss_target[0], wa, wb, wc, b_in, g_q, wq, g_kv, w_ukv, w_oa_f, sgu_ln_g,
                           sgu_ln_b, w_s, b_s, w_ob_f, w_out_f, ln_g, ln_b)
    weights = dict(w_in=w_in, b_in=b_in, g_q=g_q, w_uq=w_uq, g_kv=g_kv, w_ukv=w_ukv, w_oa=w_oa, sgu_ln_g=sgu_ln_g,
                   sgu_ln_b=sgu_ln_b, w_s=w_s, b_s=b_s, w_ob=w_ob, w_out=w_out, ln_g=ln_g, ln_b=ln_b)
    moms = dict(w_in=m_w_in, b_in=m_b_in, g_q=m_g_q, w_uq=m_w_uq, g_kv=m_g_kv, w_ukv=m_w_ukv, w_oa=m_w_oa,
                sgu_ln_g=m_sgu_ln_g, sgu_ln_b=m_sgu_ln_b, w_s=m_w_s, b_s=m_b_s, w_ob=m_w_ob, w_out=m_w_out,
                ln_g=m_ln_g, ln_b=m_ln_b)
    vars_ = dict(w_in=v_w_in, b_in=v_b_in, g_q=v_g_q, w_uq=v_w_uq, g_kv=v_g_kv, w_ukv=v_w_ukv, w_oa=v_w_oa,
                 sgu_ln_g=v_sgu_ln_g, sgu_ln_b=v_sgu_ln_b, w_s=v_w_s, b_s=v_b_s, w_ob=v_w_ob, w_out=v_w_out,
                 ln_g=v_ln_g, ln_b=v_ln_b)
    return _reduce_and_update(partials, weights, moms, vars_)


def _local_step(x2, positions, tgt, wa, wb, wc, b_in, g_q, wq, g_kv, w_ukv, w_oa_f, sgu_ln_g, sgu_ln_b, w_s, b_s,
                w_ob_f, w_out_f, ln_g, ln_b):
    ba, bb, bc = _to_segments(b_in)
    w_ukv_bf = w_ukv[0].astype(BF16)
    wkn = jnp.pad(w_ukv_bf[:, :, :QK_NOPE_DIM], ((0, 0), (0, 0), (0, HEAD_PAD - QK_NOPE_DIM))).reshape(KV_LORA_RANK, -1)
    wv = jnp.pad(w_ukv_bf[:, :, QK_NOPE_DIM:], ((0, 0), (0, 0), (0, HEAD_PAD - V_HEAD_DIM))).reshape(KV_LORA_RANK, -1)
    gq = jnp.pad(g_q, ((0, 0), (0, CQ_PAD - Q_LORA_RANK)))
    bias_full = jnp.repeat(b_s[0].T, SGU_GROUP_DIM, axis=1)
    w_s3 = w_s[0]
    w_st3 = jnp.swapaxes(w_s3, 1, 2)

    inv_freq = ROPE_THETA ** (-jnp.arange(0, QK_ROPE_DIM, 2, dtype=F32) / QK_ROPE_DIM)
    invf_lane = jnp.concatenate([jnp.zeros((ROPE_LO,), F32), inv_freq, inv_freq,
                                 jnp.zeros((LANES - ROPE_HI,), F32)]).reshape(1, LANES)
    c_t, sa_t, sb_t = _rope_tables(positions.reshape(SEQ, 1), invf_lane)

    x_bf, xt_bf = _cast_x(x2)
    h_a = _mm(x_bf, wa, bias=ba, tm=512, tn=SEG_A // 2, name="in_proj_a")
    h_b = _mm(x_bf, wb, bias=bb, tm=512, tn=SEG_B // 2, name="in_proj_b")
    h_c = _mm(x_bf, wc, bias=bc, tm=512, tn=SEG_C, name="in_proj_c")
    q, k, kt, vx, vxt = _mla_prep(h_c, gq, g_kv, wq, wkn, wv, c_t, sa_t, sb_t)
    o, lse = _attn_fwd(q, kt, vx)
    y_b = _sgu_fwd(h_b, sgu_ln_g, sgu_ln_b, w_s3, bias_full)

    (loss_row, dx_res, dh_a, d_o, d_yb, p_oa, p_ob, p_out, d_lng, d_lnb, d_ba) = _merge(
        x2, o, h_a, y_b, tgt, w_oa_f, w_ob_f, w_out_f, ln_g, ln_b)
    dh_b, d_ws, d_bs_t, d_slg, d_slb, d_bb = _sgu_bwd(h_b, d_yb, sgu_ln_g, sgu_ln_b, w_s3, w_st3, bias_full)
    dq, dk, dv = _attn_bwd(q, kt, k, vxt, d_o, o, lse)
    dh_c, p_uq, d_wkn, d_wv, d_gq, d_gkv, d_bc = _mla_bwd(dq, dk, dv, h_c, gq, g_kv, wq, wkn, wv, c_t, sa_t, sb_t)

    dx = _mm(dh_a, wa, tb=True, add=dx_res, tm=512, tn=D_MODEL, name="dx_a")
    dx = _mm(dh_b, wb, tb=True, add=dx, tm=512, tn=D_MODEL, name="dx_b")
    dx = _mm(dh_c, wc, tb=True, add=dx, tm=512, tn=D_MODEL, name="dx_c")
    d_wa = _mm(xt_bf, dh_a, out_dtype=BF16, tm=512, tn=512, name="dw_in_a")
    d_wb = _mm(xt_bf, dh_b, out_dtype=BF16, tm=512, tn=512, name="dw_in_b")
    d_wc = _mm(xt_bf, dh_c, out_dtype=BF16, tm=512, tn=SEG_C, name="dw_in_c")

    p_b_in = _from_segments(d_ba, d_bb, d_bc)
    p_w_ukv = jnp.concatenate([d_wkn.reshape(KV_LORA_RANK, MLA_HEADS, HEAD_PAD)[:, :, :QK_NOPE_DIM],
                               d_wv.reshape(KV_LORA_RANK, MLA_HEADS, HEAD_PAD)[:, :, :V_HEAD_DIM]], axis=-1)
    p_g_q = d_gq[:, :Q_LORA_RANK]
    p_b_s = d_bs_t[:, :SGU_GROUPS].T
    sharded = (d_wa, d_wb, d_wc, p_uq, p_oa, p_ob, p_out)
    replicated = [p_b_in, p_g_q, d_gkv, p_w_ukv, d_slg, d_slb, d_ws, p_b_s, d_lng, d_lnb]
    return loss_row, dx, sharded, replicated


_NAMES = ["w_in", "b_in", "g_q", "w_uq", "g_kv", "w_ukv", "w_oa", "sgu_ln_g", "sgu_ln_b", "w_s", "b_s", "w_ob",
          "w_out", "ln_g", "ln_b"]
_REPLICATED = ["b_in", "g_q", "g_kv", "w_ukv", "sgu_ln_g", "sgu_ln_b", "w_s", "b_s", "ln_g", "ln_b"]


def _reduce_and_update(partials, weights, moms, vars_):
    loss_row, dx, sharded, replicated = partials
    rep_flat = jnp.concatenate([a.reshape(-1) for a in replicated] + [loss_row[0, :1]])
    rep_flat = jnp.pad(rep_flat, (0, N_DEV * PACK_R_ROWS * LANES - rep_flat.size))
    g_in, g_uq, g_oa, g_ob, g_out, rep_slice = _reduce_grads(*sharded, rep_flat.reshape(N_DEV, PACK_R_ROWS, LANES))
    rep_sum = _all_gather(rep_slice, "gather_replicated").reshape(-1)
    grads, pos = dict(w_in=g_in, w_uq=g_uq, w_oa=g_oa, w_ob=g_ob, w_out=g_out), 0
    for nm in _REPLICATED:
        grads[nm] = rep_sum[pos:pos + weights[nm].size]
        pos += weights[nm].size
    loss = rep_sum[pos]
    grads = {nm: grads[nm].reshape(weights[nm].shape) for nm in _NAMES}
    deltas, new_m, new_v = _adamw_all([weights[nm] for nm in _NAMES], [grads[nm] for nm in _NAMES],
                                      [moms[nm] for nm in _NAMES], [vars_[nm] for nm in _NAMES])
    return (loss, dx.reshape(1, SEQ, D_MODEL), *[grads[nm] for nm in _NAMES], *deltas, *new_m, *new_v)
```

```python
import math

import jax
import jax.numpy as jnp
from jax import lax
from jax.experimental import pallas as pl
from jax.experimental.pallas import tpu as pltpu

F32 = jnp.float32
BF16 = jnp.bfloat16

D_MODEL = 1024
SEQ = 2048
N_DEV = 8
MLA_HEADS = 8
Q_LORA_RANK = 384
KV_LORA_RANK = 128
QK_NOPE_DIM = 64
QK_ROPE_DIM = 32
V_HEAD_DIM = 64
QK_HEAD_DIM = QK_NOPE_DIM + QK_ROPE_DIM
MLA_WIDTH = MLA_HEADS * V_HEAD_DIM
ROPE_THETA = 10000.0
SGU_GROUPS = 8
SGU_GROUP_DIM = 64
SGU_WIDTH = SGU_GROUPS * SGU_GROUP_DIM
CHUNK = 128
RMS_EPS = 1e-6
LN_EPS = 1e-5
DN_ALPHA = 2.0 ** 0.25
IN_WIDTH = 4640
ATTN_SCALE = QK_HEAD_DIM ** -0.5

ADAM_LR = 0.001
ADAM_B1 = 0.9
ADAM_B2 = 0.999
ADAM_EPS = 1e-08
ADAM_WD = 0.01
ADAM_STEP = 10

LANES = 128
HEAD_PAD = 128
ROPE_LO = QK_NOPE_DIM
ROPE_MID = ROPE_LO + QK_ROPE_DIM // 2
ROPE_HI = ROPE_LO + QK_ROPE_DIM
CQ_PAD = 512

SEG_A = 2560
SEG_B = 1536
SEG_C = 768

PACK_R_ROWS = 272
VMEM_BIG = 56 * 1024 * 1024
VMEM_MID = 40 * 1024 * 1024


def _sigmoid(x):
    return 1.0 / (1.0 + jnp.exp(-x))


def _gelu_and_grad(x):
    c0 = math.sqrt(2.0 / math.pi)
    x2 = x * x
    t = jnp.tanh(c0 * (x + 0.044715 * x * x2))
    g = 0.5 * x * (1.0 + t)
    dg = 0.5 * (1.0 + t) + 0.5 * x * (1.0 - t * t) * (c0 * (1.0 + 3.0 * 0.044715 * x2))
    return g, dg


def _dot(a, b, dims):
    return lax.dot_general(a, b, (dims, ((), ())), preferred_element_type=F32)


_NN = ((1,), (0,))
_NT = ((1,), (1,))
_TN = ((0,), (0,))


def _store_grad(dh_ref, db_ref, col, val):
    cols = slice(col, col + val.shape[1])
    dh_ref[:, cols] = val.astype(BF16)
    db_ref[:, cols] += jnp.sum(val, axis=0, keepdims=True)


def _mm(a, b, *, tb=False, bias=None, add=None, out_dtype=F32, tm, tn, name):
    m, k = a.shape
    n = b.shape[0] if tb else b.shape[1]
    assert m % tm == 0 and n % tn == 0
    dims = _NT if tb else _NN

    def body(*refs):
        a_ref, b_ref = refs[0], refs[1]
        pos = 2
        r = _dot(a_ref[...], b_ref[...], dims)
        if bias is not None:
            r = r + refs[pos][...]; pos += 1
        if add is not None:
            r = r + refs[pos][...]; pos += 1
        refs[pos][...] = r.astype(out_dtype)

    b_spec = pl.BlockSpec((tn, k), lambda j, i: (j, 0)) if tb else pl.BlockSpec((k, tn), lambda j, i: (0, j))
    in_specs, args = [pl.BlockSpec((tm, k), lambda j, i: (i, 0)), b_spec], [a, b]
    if bias is not None:
        in_specs.append(pl.BlockSpec((1, tn), lambda j, i: (0, j))); args.append(bias)
    if add is not None:
        in_specs.append(pl.BlockSpec((tm, tn), lambda j, i: (i, j))); args.append(add)
    return pl.pallas_call(
        body, name=name, grid=(n // tn, m // tm), in_specs=in_specs,
        out_specs=pl.BlockSpec((tm, tn), lambda j, i: (i, j)), out_shape=jax.ShapeDtypeStruct((m, n), out_dtype),
        compiler_params=pltpu.CompilerParams(dimension_semantics=("arbitrary", "arbitrary"), vmem_limit_bytes=VMEM_BIG),
    )(*args)


def _cast_x(x2):
    tm = 256

    def body(x_ref, xb_ref, xt_ref):
        x = x_ref[...]
        xb_ref[...] = x.astype(BF16)
        xt_ref[...] = x.T.astype(BF16)

    return pl.pallas_call(
        body, name="cast_x", grid=(SEQ // tm,),
        in_specs=[pl.BlockSpec((tm, D_MODEL), lambda i: (i, 0))],
        out_specs=[pl.BlockSpec((tm, D_MODEL), lambda i: (i, 0)), pl.BlockSpec((D_MODEL, tm), lambda i: (0, i))],
        out_shape=[jax.ShapeDtypeStruct((SEQ, D_MODEL), BF16), jax.ShapeDtypeStruct((D_MODEL, SEQ), BF16)],
        compiler_params=pltpu.CompilerParams(dimension_semantics=("arbitrary",)),
    )(x2)


def _rope_tables(pos_col, invf_lane):
    def body(pos_ref, invf_ref, c_ref, sa_ref, sb_ref):
        ang = pos_ref[...].astype(F32) * invf_ref[...]
        cs, sn = jnp.cos(ang), jnp.sin(ang)
        lane = lax.broadcasted_iota(jnp.int32, ang.shape, 1)
        c_ref[...] = jnp.where(lane < ROPE_LO, 1.0, jnp.where(lane < ROPE_HI, cs, 0.0))
        sa_ref[...] = jnp.where(jnp.logical_and(lane >= ROPE_LO, lane < ROPE_MID), -sn, 0.0)
        sb_ref[...] = jnp.where(jnp.logical_and(lane >= ROPE_MID, lane < ROPE_HI), sn, 0.0)

    shp = jax.ShapeDtypeStruct((SEQ, LANES), F32)
    return pl.pallas_call(body, name="rope_tables", out_shape=[shp, shp, shp])(pos_col, invf_lane)


def _rope(x, c, sa, sb):
    return x * c + pltpu.roll(x, LANES - 16, 1) * sa + pltpu.roll(x, 16, 1) * sb


def _rope_t(dy, c, sa, sb):
    return dy * c + pltpu.roll(dy * sa, 16, 1) + pltpu.roll(dy * sb, LANES - 16, 1)


def _mla_prep(h_c, gq, gkv, wq, wkn, wvx, c_t, sa_t, sb_t):
    tm = 256
    hw = MLA_HEADS * HEAD_PAD

    def body(cq_ref, ckv_ref, kpe_ref, gq_ref, gkv_ref, wq_ref, wkn_ref, wvx_ref, c_ref, sa_ref, sb_ref,
             q_ref, k_ref, kt_ref, vx_ref, vxt_ref):
        c, sa, sb = c_ref[...], sa_ref[...], sb_ref[...]
        cq = cq_ref[...]
        rq = lax.rsqrt(jnp.sum(cq * cq, axis=1, keepdims=True) * (1.0 / Q_LORA_RANK) + RMS_EPS)
        cqn = ((cq * rq) * gq_ref[...]).astype(BF16)
        qall = _dot(cqn, wq_ref[...], _NN)
        for h in range(MLA_HEADS):
            sl = slice(HEAD_PAD * h, HEAD_PAD * (h + 1))
            q_ref[:, sl] = (_rope(qall[:, sl], c, sa, sb) * ATTN_SCALE).astype(BF16)
        ckv = ckv_ref[...]
        rkv = lax.rsqrt(jnp.sum(ckv * ckv, axis=1, keepdims=True) * (1.0 / KV_LORA_RANK) + RMS_EPS)
        ckvn = ((ckv * rkv) * gkv_ref[...]).astype(BF16)
        knall = _dot(ckvn, wkn_ref[...], _NN)
        vall = _dot(ckvn, wvx_ref[...], _NN)
        kper = _rope(kpe_ref[...], c, sa, sb)
        ones_half = (lax.broadcasted_iota(jnp.int32, (tm, HEAD_PAD), 1) >= V_HEAD_DIM).astype(F32)
        for h in range(MLA_HEADS):
            sl = slice(HEAD_PAD * h, HEAD_PAD * (h + 1))
            kh = knall[:, sl] + kper
            vh = vall[:, sl] + ones_half
            k_ref[:, sl] = kh.astype(BF16)
            kt_ref[sl, :] = kh.T.astype(BF16)
            vx_ref[:, sl] = vh.astype(BF16)
            vxt_ref[sl, :] = vh.T.astype(BF16)

    full = lambda shape: pl.BlockSpec(shape, lambda i: (0, 0))
    tab = pl.BlockSpec((tm, LANES), lambda i: (i, 0))
    row = pl.BlockSpec((tm, hw), lambda i: (i, 0))
    col = pl.BlockSpec((hw, tm), lambda i: (0, i))
    return pl.pallas_call(
        body, name="mla_prep", grid=(SEQ // tm,),
        in_specs=[pl.BlockSpec((tm, CQ_PAD), lambda i: (i, 0)),
                  pl.BlockSpec((tm, LANES), lambda i: (i, CQ_PAD // LANES)),
                  pl.BlockSpec((tm, LANES), lambda i: (i, CQ_PAD // LANES + 1)),
                  full((1, CQ_PAD)), full((1, KV_LORA_RANK)),
                  full((CQ_PAD, hw)), full((KV_LORA_RANK, hw)), full((KV_LORA_RANK, hw)), tab, tab, tab],
        out_specs=[row, row, col, row, col],
        out_shape=[jax.ShapeDtypeStruct((SEQ, hw), BF16), jax.ShapeDtypeStruct((SEQ, hw), BF16),
                   jax.ShapeDtypeStruct((hw, SEQ), BF16), jax.ShapeDtypeStruct((SEQ, hw), BF16),
                   jax.ShapeDtypeStruct((hw, SEQ), BF16)],
        compiler_params=pltpu.CompilerParams(dimension_semantics=("arbitrary",), vmem_limit_bytes=VMEM_MID),
    )(h_c, h_c, h_c, gq, gkv, wq, wkn, wvx, c_t, sa_t, sb_t)


ATT_T = 512
ATT_STRIP = 64


def _attn_fwd(q, kt, vx):
    t, rs = ATT_T, ATT_STRIP

    def body(q_ref, kt_ref, vx_ref, o_ref, l_ref, s_scr, p_scr, m_scr, a_scr, acc_scr):
        qi = pl.program_id(1)
        lane = lax.broadcasted_iota(jnp.int32, (t, LANES), 1)
        m_scr[...] = jnp.full((2, t, LANES), -1e30, F32)
        acc_scr[...] = jnp.zeros((2, t, LANES), F32)

        def block(j, masked):
            off = pl.multiple_of(j * t, t)
            for a in range(2):
                sl = slice(HEAD_PAD * a, HEAD_PAD * (a + 1))
                s_scr[a] = _dot(q_ref[:, sl], kt_ref[sl, pl.ds(off, t)], _NN)
                for r in range(t // rs):
                    rows = slice(rs * r, rs * (r + 1))
                    s = s_scr[a, rows, :]
                    if masked:
                        rowi = lax.broadcasted_iota(jnp.int32, (rs, t), 0) + rs * r
                        coli = lax.broadcasted_iota(jnp.int32, (rs, t), 1)
                        s = jnp.where(coli <= rowi, s, -1e30)
                    m_old = m_scr[a, rows, :]
                    m_new = jnp.maximum(m_old, jnp.max(s, axis=1, keepdims=True))
                    p_scr[a, rows, :] = jnp.exp(s - m_new[:, :1]).astype(BF16)
                    a_scr[a, rows, :] = jnp.exp(m_old - m_new)
                    m_scr[a, rows, :] = m_new
                acc_scr[a] = acc_scr[a] * a_scr[a] + _dot(p_scr[a], vx_ref[pl.ds(off, t), sl], _NN)

        def step(j, carry):
            block(j, False)
            return carry
        lax.fori_loop(0, qi, step, 0)
        block(qi, True)
        res = []
        for a in range(2):
            acc = acc_scr[a]
            l = acc[:, V_HEAD_DIM:V_HEAD_DIM + 1]
            res.append((acc / l, m_scr[a] + jnp.log(l)))
        o_ref[...] = jnp.where(lane < V_HEAD_DIM, res[0][0], pltpu.roll(res[1][0], V_HEAD_DIM, 1))
        l_ref[...] = jnp.where(lane < V_HEAD_DIM, res[0][1], res[1][1])

    return pl.pallas_call(
        body, name="attn_fwd", grid=(MLA_HEADS // 2, SEQ // t),
        in_specs=[pl.BlockSpec((t, 2 * HEAD_PAD), lambda p, i: (i, p)),
                  pl.BlockSpec((2 * HEAD_PAD, SEQ), lambda p, i: (p, 0)),
                  pl.BlockSpec((SEQ, 2 * HEAD_PAD), lambda p, i: (0, p))],
        out_specs=[pl.BlockSpec((t, LANES), lambda p, i: (i, p)),
                   pl.BlockSpec((t, LANES), lambda p, i: (i, p))],
        out_shape=[jax.ShapeDtypeStruct((SEQ, MLA_WIDTH), F32), jax.ShapeDtypeStruct((SEQ, MLA_WIDTH), F32)],
        scratch_shapes=[pltpu.VMEM((2, t, t), F32), pltpu.VMEM((2, t, t), BF16), pltpu.VMEM((2, t, LANES), F32),
                        pltpu.VMEM((2, t, LANES), F32), pltpu.VMEM((2, t, LANES), F32)],
        compiler_params=pltpu.CompilerParams(dimension_semantics=("arbitrary", "arbitrary"), vmem_limit_bytes=VMEM_MID),
    )(q, kt, vx)


def _exchange_parts(parts, lands, send_sems, recv_sems, local_sems):
    x, y, c = _mesh_pos()
    me = 4 * x + 2 * y + c
    peers = [(x, y, 1 - c), (1 - x, y, c), (x, 1 - y, c), (1 - x, 1 - y, c),
             (1 - x, y, 1 - c), (x, 1 - y, 1 - c), (1 - x, 1 - y, 1 - c)]
    remote, local = [], []
    for a, (part, land) in enumerate(zip(parts, lands)):
        for k, peer in enumerate(peers):
            t = 4 * peer[0] + 2 * peer[1] + peer[2]
            remote.append(_remote(part.at[t], land.at[me], send_sems, recv_sems, 7 * a + k, peer))
        local.append(pltpu.make_async_copy(part.at[me], land.at[me], local_sems.at[a]))
    return remote, local


def _attn_bwd(q, kt, k, vxt, d_o, o, lse, parts):
    t, rs = ATT_T, ATT_STRIP
    nq = SEQ // t
    npart = len(parts)
    nsteps = MLA_HEADS // 2

    def body(q_ref, kt_ref, k_ref, vxt_ref, do_ref, o_ref, l_ref, *rest):
        part_refs, rest = rest[:npart], rest[npart:]
        dq_ref, dk_ref, dv_ref = rest[:3]
        land_refs, rest = rest[3:3 + npart], rest[3 + npart:]
        s_scr, dp_scr, p_scr, ds_scr, st_scr, send_sems, recv_sems, local_sems = rest
        remote, local = _exchange_parts(part_refs, land_refs, send_sems, recv_sems, local_sems)

        @pl.when(pl.program_id(0) == 0)
        def _():
            for cp in remote + local:
                cp.start()

        dk_ref[...] = jnp.zeros_like(dk_ref)
        dv_ref[...] = jnp.zeros_like(dv_ref)
        lane = lax.broadcasted_iota(jnp.int32, (t, LANES), 1)

        def qtile(i, carry):
            ioff = pl.multiple_of(i * t, t)
            do_i = do_ref[pl.ds(ioff, t), :]
            o_i = o_ref[pl.ds(ioff, t), :]
            l_i = l_ref[pl.ds(ioff, t), :]
            for a in range(2):
                sl = slice(HEAD_PAD * a, HEAD_PAD * (a + 1))
                sel = (lane < V_HEAD_DIM) if a == 0 else (lane >= V_HEAD_DIM)
                doa = jnp.where(sel, do_i, 0.0)
                oa = o_i
                if a == 1:
                    doa = pltpu.roll(doa, V_HEAD_DIM, 1)
                    oa = pltpu.roll(o_i, V_HEAD_DIM, 1)
                st_scr[0] = jnp.broadcast_to(jnp.sum(doa * oa, axis=1, keepdims=True), (t, LANES))
                st_scr[1] = jnp.broadcast_to(l_i[:, V_HEAD_DIM * a:V_HEAD_DIM * a + 1], (t, LANES))
                doa_bf = doa.astype(BF16)
                qa = q_ref[pl.ds(ioff, t), sl]

                def block(j, masked, dq_acc, sl=sl, qa=qa, doa_bf=doa_bf):
                    joff = pl.multiple_of(j * t, t)
                    s_scr[...] = _dot(qa, kt_ref[sl, pl.ds(joff, t)], _NN)
                    dp_scr[...] = _dot(doa_bf, vxt_ref[sl, pl.ds(joff, t)], _NN)
                    for r in range(t // rs):
                        rows = slice(rs * r, rs * (r + 1))
                        p = jnp.exp(s_scr[rows, :] - st_scr[1, rows, :1])
                        if masked:
                            rowi = lax.broadcasted_iota(jnp.int32, (rs, t), 0) + rs * r
                            coli = lax.broadcasted_iota(jnp.int32, (rs, t), 1)
                            p = jnp.where(coli <= rowi, p, 0.0)
                        p_scr[rows, :] = p.astype(BF16)
                        ds_scr[rows, :] = (p * (dp_scr[rows, :] - st_scr[0, rows, :1])).astype(BF16)
                    dk_ref[pl.ds(joff, t), sl] += _dot(ds_scr[...], qa, _TN)
                    dv_ref[pl.ds(joff, t), sl] += _dot(p_scr[...], doa_bf, _TN)
                    return dq_acc + _dot(ds_scr[...], k_ref[pl.ds(joff, t), sl], _NN)

                dq_acc = lax.fori_loop(0, i, lambda j, acc: block(j, False, acc), jnp.zeros((t, HEAD_PAD), F32))
                dq_ref[pl.ds(ioff, t), sl] = block(i, True, dq_acc)
            return carry

        lax.fori_loop(0, nq, qtile, 0)

        @pl.when(pl.program_id(0) == nsteps - 1)
        def _():
            for cp in remote:
                cp.wait_recv()
            for cp in remote:
                cp.wait_send()
            for cp in local:
                cp.wait()

    hw = MLA_HEADS * HEAD_PAD
    wide = pl.BlockSpec((SEQ, 2 * HEAD_PAD), lambda p: (0, p))
    wide_t = pl.BlockSpec((2 * HEAD_PAD, SEQ), lambda p: (p, 0))
    narrow = pl.BlockSpec((SEQ, LANES), lambda p: (0, p))
    hbm = pl.BlockSpec(memory_space=pl.ANY)
    res = pl.pallas_call(
        body, name="attn_bwd", grid=(nsteps,),
        in_specs=[wide, wide_t, wide, wide_t, narrow, narrow, narrow] + [hbm] * npart,
        out_specs=[wide, wide, wide] + [hbm] * npart,
        out_shape=[jax.ShapeDtypeStruct((SEQ, hw), F32)] * 3 + [jax.ShapeDtypeStruct(p.shape, p.dtype) for p in parts],
        scratch_shapes=[pltpu.VMEM((t, t), F32), pltpu.VMEM((t, t), F32), pltpu.VMEM((t, t), BF16),
                        pltpu.VMEM((t, t), BF16), pltpu.VMEM((2, t, LANES), F32),
                        pltpu.SemaphoreType.DMA((7 * npart,)), pltpu.SemaphoreType.DMA((7 * npart,)),
                        pltpu.SemaphoreType.DMA((npart,))],
        compiler_params=pltpu.CompilerParams(dimension_semantics=("arbitrary",), vmem_limit_bytes=VMEM_BIG),
    )(q, kt, k, vxt, d_o, o, lse, *parts)
    return res[0], res[1], res[2], res[3:]


def _sgu_math(u, v, zb, lg, lb, ws_ref, bias):
    ug, dug = _gelu_and_grad(u)
    vg, dvg = _gelu_and_grad(v)
    mu = jnp.mean(vg, axis=1, keepdims=True)
    xc = vg - mu
    rstd = lax.rsqrt(jnp.mean(xc * xc, axis=1, keepdims=True) + LN_EPS)
    xh = xc * rstd
    vn_bf = (xh * lg + lb).astype(BF16)
    grp = lax.broadcasted_iota(jnp.int32, (CHUNK, SGU_WIDTH), 1) // SGU_GROUP_DIM
    r_i = lax.broadcasted_iota(jnp.int32, (CHUNK, CHUNK), 0)
    c_i = lax.broadcasted_iota(jnp.int32, (CHUNK, CHUNK), 1)
    tri, tri_t = r_i >= c_i, r_i <= c_i
    mixed = bias
    for g in range(SGU_GROUPS):
        wt = jnp.where(tri, ws_ref[g], 0.0).astype(BF16)
        mixed = mixed + jnp.where(grp == g, _dot(wt, vn_bf, _NN), 0.0)
    sb = _sigmoid(zb)
    return ug, dug, dvg, rstd, xh, vn_bf, grp, tri, tri_t, mixed, sb


def _sgu_fwd(h_b, lg, lb, w_s, bias_full):
    def body(u_ref, v_ref, zb_ref, lg_ref, lb_ref, ws_ref, bias_ref, yb_ref):
        zb = zb_ref[...]
        ug, _, _, _, _, _, _, _, _, mixed, sb = _sgu_math(u_ref[...], v_ref[...], zb, lg_ref[...], lb_ref[...],
                                                       ws_ref, bias_ref[...])
        yb_ref[...] = (ug * mixed) * (zb * sb)

    blk = lambda c: pl.BlockSpec((CHUNK, SGU_WIDTH), lambda i, c=c: (i, c))
    full2 = lambda shape: pl.BlockSpec(shape, lambda i: (0, 0))
    return pl.pallas_call(
        body, name="sgu_fwd", grid=(SEQ // CHUNK,),
        in_specs=[blk(0), blk(1), blk(2), full2((1, SGU_WIDTH)), full2((1, SGU_WIDTH)),
                  pl.BlockSpec((SGU_GROUPS, CHUNK, CHUNK), lambda i: (0, 0, 0)), full2((CHUNK, SGU_WIDTH))],
        out_specs=pl.BlockSpec((CHUNK, SGU_WIDTH), lambda i: (i, 0)),
        out_shape=jax.ShapeDtypeStruct((SEQ, SGU_WIDTH), F32),
        compiler_params=pltpu.CompilerParams(dimension_semantics=("arbitrary",)),
    )(h_b, h_b, h_b, lg, lb, w_s, bias_full)


def _sgu_bwd(h_b, d_yb, lg, lb, w_s, w_st, bias_full):
    nsteps = SEQ // CHUNK

    def body(u_ref, v_ref, zb_ref, dyb_ref, lg_ref, lb_ref, ws_ref, wst_ref, bias_ref,
             dhb_ref, dws_ref, dbs_ref, dlg_ref, dlb_ref, dbb_ref, dbias_acc):
        step = pl.program_id(0)

        @pl.when(step == 0)
        def _():
            dbb_ref[...] = jnp.zeros_like(dbb_ref)
            dws_ref[...] = jnp.zeros_like(dws_ref)
            dlg_ref[...] = jnp.zeros_like(dlg_ref)
            dlb_ref[...] = jnp.zeros_like(dlb_ref)
            dbias_acc[...] = jnp.zeros_like(dbias_acc)

        zb = zb_ref[...]
        lg = lg_ref[...]
        ug, dug, dvg, rstd, xh, vn_bf, grp, tri, tri_t, mixed, sb = _sgu_math(
            u_ref[...], v_ref[...], zb, lg, lb_ref[...], ws_ref, bias_ref[...])
        dyb = dyb_ref[...]
        dsgu = dyb * (zb * sb)
        dzb = dyb * (ug * mixed) * (sb * (1.0 + zb * (1.0 - sb)))
        du = dsgu * mixed * dug
        dmixed = dsgu * ug
        dbias_acc[...] += dmixed
        dvn = jnp.zeros((CHUNK, SGU_WIDTH), F32)
        for g in range(SGU_GROUPS):
            dm_g = jnp.where(grp == g, dmixed, 0.0).astype(BF16)
            wtt = jnp.where(tri_t, wst_ref[g], 0.0).astype(BF16)
            dvn = dvn + _dot(wtt, dm_g, _NN)
            dws_ref[g] += jnp.where(tri, _dot(dm_g, vn_bf, _NT), 0.0)
        dlg_ref[...] += jnp.sum(dvn * xh, axis=0, keepdims=True)
        dlb_ref[...] += jnp.sum(dvn, axis=0, keepdims=True)
        dxh = dvn * lg
        dvgel = rstd * (dxh - jnp.mean(dxh, axis=1, keepdims=True) - xh * jnp.mean(dxh * xh, axis=1, keepdims=True))
        _store_grad(dhb_ref, dbb_ref, 0, du)
        _store_grad(dhb_ref, dbb_ref, SGU_WIDTH, dvgel * dvg)
        _store_grad(dhb_ref, dbb_ref, 2 * SGU_WIDTH, dzb)

        @pl.when(step == nsteps - 1)
        def _():
            acc = dbias_acc[...]
            lane = lax.broadcasted_iota(jnp.int32, (CHUNK, LANES), 1)
            out = jnp.zeros((CHUNK, LANES), F32)
            for g in range(SGU_GROUPS):
                sg = jnp.sum(jnp.where(grp == g, acc, 0.0), axis=1, keepdims=True)
                out = jnp.where(lane == g, sg, out)
            dbs_ref[...] = out

    blk = lambda c: pl.BlockSpec((CHUNK, SGU_WIDTH), lambda i, c=c: (i, c))
    full2 = lambda shape: pl.BlockSpec(shape, lambda i: (0, 0))
    full3 = pl.BlockSpec((SGU_GROUPS, CHUNK, CHUNK), lambda i: (0, 0, 0))
    return pl.pallas_call(
        body, name="sgu_bwd", grid=(nsteps,),
        in_specs=[blk(0), blk(1), blk(2), pl.BlockSpec((CHUNK, SGU_WIDTH), lambda i: (i, 0)),
                  full2((1, SGU_WIDTH)), full2((1, SGU_WIDTH)), full3, full3, full2((CHUNK, SGU_WIDTH))],
        out_specs=[pl.BlockSpec((CHUNK, SEG_B), lambda i: (i, 0)), full3, full2((CHUNK, LANES)),
                   full2((1, SGU_WIDTH)), full2((1, SGU_WIDTH)), full2((1, SEG_B))],
        out_shape=[jax.ShapeDtypeStruct((SEQ, SEG_B), BF16),
                   jax.ShapeDtypeStruct((SGU_GROUPS, CHUNK, CHUNK), F32),
                   jax.ShapeDtypeStruct((CHUNK, LANES), F32),
                   jax.ShapeDtypeStruct((1, SGU_WIDTH), F32), jax.ShapeDtypeStruct((1, SGU_WIDTH), F32),
                   jax.ShapeDtypeStruct((1, SEG_B), F32)],
        scratch_shapes=[pltpu.VMEM((CHUNK, SGU_WIDTH), F32)],
        compiler_params=pltpu.CompilerParams(dimension_semantics=("arbitrary",)),
    )(h_b, h_b, h_b, d_yb, lg, lb, w_s, w_st, bias_full)


def _merge(x, o, h_a, y_b, target, w_oa, w_ob, w_out, ln_g, ln_b):
    tm = 256
    nsteps = SEQ // tm

    def body(x_ref, o_ref, ga_ref, gb_ref, za_ref, yb_ref, tgt_ref, woa_ref, wob_ref, wout_ref, lng_ref, lnb_ref,
             loss_ref, dxr_ref, dha_ref, do_ref, dyb_ref, poa_ref, pob_ref, pout_ref, dlng_ref, dlnb_ref, dba_ref,
             dwoa_ref, dwob_ref, dwout_ref):
        step = pl.program_id(0)

        @pl.when(step == 0)
        def _():
            for r in (loss_ref, dwoa_ref, dwob_ref, dwout_ref, dlng_ref, dlnb_ref, dba_ref):
                r[...] = jnp.zeros_like(r)

        o = o_ref[...]
        za = za_ref[...]
        sa = _sigmoid(za)
        ya_bf = (o * (za * sa)).astype(BF16)
        yb_bf = yb_ref[...].astype(BF16)
        woa, wob, wout = woa_ref[...], wob_ref[...], wout_ref[...]
        pa = _dot(ya_bf, woa, _NN)
        pb = _dot(yb_bf, wob, _NN)
        sga = _sigmoid(ga_ref[...])
        sgb = _sigmoid(gb_ref[...])
        merged_bf = (sga * pa + sgb * pb).astype(BF16)
        r = DN_ALPHA * x_ref[...] + _dot(merged_bf, wout, _NN)
        mu = jnp.mean(r, axis=1, keepdims=True)
        rc = r - mu
        rstd = lax.rsqrt(jnp.mean(rc * rc, axis=1, keepdims=True) + LN_EPS)
        xh = rc * rstd
        lng = lng_ref[...]
        y = xh * lng + lnb_ref[...]
        e = y - tgt_ref[...]
        loss_ref[...] += 0.5 * jnp.sum(jnp.sum(e * e, axis=1, keepdims=True) * (1.0 / D_MODEL), axis=0, keepdims=True)

        dy = e * (1.0 / D_MODEL)
        dlng_ref[...] += jnp.sum(dy * xh, axis=0, keepdims=True)
        dlnb_ref[...] += jnp.sum(dy, axis=0, keepdims=True)
        dxh = dy * lng
        dr = rstd * (dxh - jnp.mean(dxh, axis=1, keepdims=True) - xh * jnp.mean(dxh * xh, axis=1, keepdims=True))
        dxr_ref[...] = DN_ALPHA * dr
        dr_bf = dr.astype(BF16)
        dwout_ref[...] += _dot(merged_bf, dr_bf, _TN)
        dmerged = _dot(dr_bf, wout, _NT)
        dpa_bf = (dmerged * sga).astype(BF16)
        dpb_bf = (dmerged * sgb).astype(BF16)
        _store_grad(dha_ref, dba_ref, 0, dmerged * pa * (sga * (1.0 - sga)))
        _store_grad(dha_ref, dba_ref, D_MODEL, dmerged * pb * (sgb * (1.0 - sgb)))
        dwoa_ref[...] += _dot(ya_bf, dpa_bf, _TN)
        dwob_ref[...] += _dot(yb_bf, dpb_bf, _TN)
        dya = _dot(dpa_bf, woa, _NT)
        dyb_ref[...] = _dot(dpb_bf, wob, _NT)
        do_ref[...] = dya * (za * sa)
        _store_grad(dha_ref, dba_ref, 2 * D_MODEL, dya * o * (sa * (1.0 + za * (1.0 - sa))))

        @pl.when(step == nsteps - 1)
        def _():
            cols = D_MODEL // N_DEV
            for j in range(N_DEV):
                poa_ref[j] = dwoa_ref[:, cols * j:cols * (j + 1)].astype(BF16)
                pob_ref[j] = dwob_ref[:, cols * j:cols * (j + 1)].astype(BF16)
                pout_ref[j] = dwout_ref[cols * j:cols * (j + 1), :].astype(BF16)

    row = lambda w, c=0: pl.BlockSpec((tm, w), lambda i, c=c: (i, c))
    full = lambda shape: pl.BlockSpec(shape, lambda i: (0, 0))
    full3 = lambda shape: pl.BlockSpec(shape, lambda i: (0, 0, 0))
    return pl.pallas_call(
        body, name="merge", grid=(nsteps,),
        in_specs=[row(D_MODEL), row(MLA_WIDTH), row(D_MODEL, 0), row(D_MODEL, 1), row(MLA_WIDTH, 4), row(SGU_WIDTH),
                  row(D_MODEL), full((MLA_WIDTH, D_MODEL)), full((SGU_WIDTH, D_MODEL)), full((D_MODEL, D_MODEL)),
                  full((1, D_MODEL)), full((1, D_MODEL))],
        out_specs=[full((1, LANES)), row(D_MODEL), row(SEG_A), row(MLA_WIDTH), row(SGU_WIDTH),
                   full3((N_DEV, MLA_WIDTH, D_MODEL // N_DEV)), full3((N_DEV, SGU_WIDTH, D_MODEL // N_DEV)),
                   full3((N_DEV, D_MODEL // N_DEV, D_MODEL)), full((1, D_MODEL)), full((1, D_MODEL)), full((1, SEG_A))],
        out_shape=[jax.ShapeDtypeStruct((1, LANES), F32),
                   jax.ShapeDtypeStruct((SEQ, D_MODEL), F32), jax.ShapeDtypeStruct((SEQ, SEG_A), BF16),
                   jax.ShapeDtypeStruct((SEQ, MLA_WIDTH), F32), jax.ShapeDtypeStruct((SEQ, SGU_WIDTH), F32),
                   jax.ShapeDtypeStruct((N_DEV, MLA_WIDTH, D_MODEL // N_DEV), BF16),
                   jax.ShapeDtypeStruct((N_DEV, SGU_WIDTH, D_MODEL // N_DEV), BF16),
                   jax.ShapeDtypeStruct((N_DEV, D_MODEL // N_DEV, D_MODEL), BF16),
                   jax.ShapeDtypeStruct((1, D_MODEL), F32), jax.ShapeDtypeStruct((1, D_MODEL), F32),
                   jax.ShapeDtypeStruct((1, SEG_A), F32)],
        scratch_shapes=[pltpu.VMEM((MLA_WIDTH, D_MODEL), F32), pltpu.VMEM((SGU_WIDTH, D_MODEL), F32),
                        pltpu.VMEM((D_MODEL, D_MODEL), F32)],
        compiler_params=pltpu.CompilerParams(dimension_semantics=("arbitrary",), vmem_limit_bytes=VMEM_BIG),
    )(x, o, h_a, h_a, h_a, y_b, target, w_oa, w_ob, w_out, ln_g, ln_b)


def _mla_bwd(dq, dk, dv, h_c, gq, gkv, wq, wkn, wv, c_t, sa_t, sb_t):
    tm = 256
    hw = MLA_HEADS * HEAD_PAD

    def body(dq_ref, dk_ref, dv_ref, cq_ref, ckv_ref, gq_ref, gkv_ref, wq_ref, wkn_ref, wv_ref, c_ref, sa_ref, sb_ref,
             dhc_ref, puq_ref, dwkn_ref, dwv_ref, dgq_ref, dgkv_ref, dbc_ref, pre_ref, dwq_ref):
        @pl.when(pl.program_id(0) == 0)
        def _():
            for r in (dwq_ref, dwkn_ref, dwv_ref, dgq_ref, dgkv_ref, dbc_ref):
                r[...] = jnp.zeros_like(r)

        c, sa, sb = c_ref[...], sa_ref[...], sb_ref[...]
        lane = lax.broadcasted_iota(jnp.int32, (tm, LANES), 1)
        rope_lanes = jnp.logical_and(lane >= ROPE_LO, lane < ROPE_HI)

        cq = cq_ref[...]
        gq = gq_ref[...]
        rq = lax.rsqrt(jnp.sum(cq * cq, axis=1, keepdims=True) * (1.0 / Q_LORA_RANK) + RMS_EPS)
        nq = cq * rq
        cqn_bf = (nq * gq).astype(BF16)
        for h in range(MLA_HEADS):
            sl = slice(HEAD_PAD * h, HEAD_PAD * (h + 1))
            pre_ref[:, sl] = _rope_t(dq_ref[:, sl] * ATTN_SCALE, c, sa, sb).astype(BF16)
        dqpre_bf = pre_ref[...]
        dcqn = _dot(dqpre_bf, wq_ref[...], _NT)
        dwq_ref[...] += _dot(cqn_bf, dqpre_bf, _TN)
        dgq_ref[...] += jnp.sum(dcqn * nq, axis=0, keepdims=True)
        dnq = dcqn * gq
        _store_grad(dhc_ref, dbc_ref, 0,
                    rq * (dnq - nq * (jnp.sum(dnq * nq, axis=1, keepdims=True) * (1.0 / Q_LORA_RANK))))

        ckv = ckv_ref[...]
        gkv = gkv_ref[...]
        rkv = lax.rsqrt(jnp.sum(ckv * ckv, axis=1, keepdims=True) * (1.0 / KV_LORA_RANK) + RMS_EPS)
        nkv = ckv * rkv
        ckvn_bf = (nkv * gkv).astype(BF16)
        dk = dk_ref[...]
        dk_bf = dk.astype(BF16)
        dv_bf = dv_ref[...].astype(BF16)
        dckvn = _dot(dk_bf, wkn_ref[...], _NT) + _dot(dv_bf, wv_ref[...], _NT)
        dwkn_ref[...] += _dot(ckvn_bf, dk_bf, _TN)
        dwv_ref[...] += _dot(ckvn_bf, dv_bf, _TN)
        dgkv_ref[...] += jnp.sum(dckvn * nkv, axis=0, keepdims=True)
        dnkv = dckvn * gkv
        _store_grad(dhc_ref, dbc_ref, CQ_PAD, rkv * (
            dnkv - nkv * (jnp.sum(dnkv * nkv, axis=1, keepdims=True) * (1.0 / KV_LORA_RANK))))
        dkpe = jnp.zeros((tm, LANES), F32)
        for h in range(MLA_HEADS):
            dkpe = dkpe + dk[:, HEAD_PAD * h:HEAD_PAD * (h + 1)]
        _store_grad(dhc_ref, dbc_ref, CQ_PAD + LANES, _rope_t(jnp.where(rope_lanes, dkpe, 0.0), c, sa, sb))

        @pl.when(pl.program_id(0) == SEQ // tm - 1)
        def _():
            rows = Q_LORA_RANK // N_DEV
            for j in range(N_DEV):
                for h in range(MLA_HEADS):
                    puq_ref[j, :, QK_HEAD_DIM * h:QK_HEAD_DIM * (h + 1)] = dwq_ref[
                        rows * j:rows * (j + 1), HEAD_PAD * h:HEAD_PAD * h + QK_HEAD_DIM].astype(BF16)

    full = lambda shape: pl.BlockSpec(shape, lambda i: (0, 0))
    row = lambda w, c=0: pl.BlockSpec((tm, w), lambda i, c=c: (i, c))
    return pl.pallas_call(
        body, name="mla_bwd", grid=(SEQ // tm,),
        in_specs=[row(hw), row(hw), row(hw), row(CQ_PAD, 0), row(LANES, CQ_PAD // LANES),
                  full((1, CQ_PAD)), full((1, KV_LORA_RANK)), full((CQ_PAD, hw)), full((KV_LORA_RANK, hw)),
                  full((KV_LORA_RANK, hw)), row(LANES), row(LANES), row(LANES)],
        out_specs=[row(SEG_C), pl.BlockSpec((N_DEV, Q_LORA_RANK // N_DEV, MLA_HEADS * QK_HEAD_DIM), lambda i: (0, 0, 0)),
                   full((KV_LORA_RANK, hw)), full((KV_LORA_RANK, hw)),
                   full((1, CQ_PAD)), full((1, KV_LORA_RANK)), full((1, SEG_C))],
        out_shape=[jax.ShapeDtypeStruct((SEQ, SEG_C), BF16),
                   jax.ShapeDtypeStruct((N_DEV, Q_LORA_RANK // N_DEV, MLA_HEADS * QK_HEAD_DIM), BF16),
                   jax.ShapeDtypeStruct((KV_LORA_RANK, hw), F32), jax.ShapeDtypeStruct((KV_LORA_RANK, hw), F32),
                   jax.ShapeDtypeStruct((1, CQ_PAD), F32), jax.ShapeDtypeStruct((1, KV_LORA_RANK), F32),
                   jax.ShapeDtypeStruct((1, SEG_C), F32)],
        scratch_shapes=[pltpu.VMEM((tm, hw), BF16), pltpu.VMEM((CQ_PAD, hw), F32)],
        compiler_params=pltpu.CompilerParams(dimension_semantics=("arbitrary",), vmem_limit_bytes=VMEM_MID),
    )(dq, dk, dv, h_c, h_c, gq, gkv, wq, wkn, wv, c_t, sa_t, sb_t)


def _adamw_all(ws, gs, ms, vs):
    n = len(ws)
    c1 = 1.0 / (1.0 - ADAM_B1 ** ADAM_STEP)
    c2 = 1.0 / (1.0 - ADAM_B2 ** ADAM_STEP)

    def body(*refs):
        for idx in range(n):
            w, g, m, v = (refs[idx][...], refs[n + idx][...], refs[2 * n + idx][...], refs[3 * n + idx][...])
            m_new = ADAM_B1 * m + (1.0 - ADAM_B1) * g
            v_new = ADAM_B2 * v + (1.0 - ADAM_B2) * (g * g)
            delta = -ADAM_LR * ((m_new * c1) / (jnp.sqrt(v_new * c2) + ADAM_EPS) + ADAM_WD * w)
            refs[4 * n + idx][...] = delta
            refs[5 * n + idx][...] = m_new
            refs[6 * n + idx][...] = v_new

    shapes = [jax.ShapeDtypeStruct(w.shape, F32) for w in ws]
    outs = pl.pallas_call(
        body, name="adamw", out_shape=shapes * 3,
        compiler_params=pltpu.CompilerParams(vmem_limit_bytes=VMEM_BIG),
    )(*ws, *gs, *ms, *vs)
    return outs[:n], outs[n:2 * n], outs[2 * n:]


SHARD_W = IN_WIDTH // N_DEV

_PIECES = [(0, 384, 2, 0), (384, 512, 2, CQ_PAD), (512, 544, 2, CQ_PAD + LANES + ROPE_LO),
           (544, 1056, 0, 2 * D_MODEL), (1056, 1568, 1, 0), (1568, 2080, 1, SGU_WIDTH),
           (2080, 2592, 1, 2 * SGU_WIDTH), (2592, 3616, 0, 0), (3616, 4640, 0, D_MODEL)]


def _column_runs():
    runs = []
    for n0, n1, seg, d0 in _PIECES:
        for j in range(N_DEV):
            lo, hi = max(n0, j * SHARD_W), min(n1, (j + 1) * SHARD_W)
            if lo < hi:
                runs.append((j, lo - j * SHARD_W, hi - j * SHARD_W, seg, d0 + lo - n0))
    return runs


def _mesh_pos():
    return lax.axis_index("x"), lax.axis_index("y"), lax.axis_index("c")


def _remote(src, dst, send_sems, recv_sems, k, to):
    return pltpu.make_async_remote_copy(src_ref=src, dst_ref=dst, send_sem=send_sems.at[k], recv_sem=recv_sems.at[k],
                                        device_id=to, device_id_type=pl.DeviceIdType.MESH)


def _gather_exchange(gats, send_sems, recv_sems):
    x, y, c = _mesh_pos()
    me, sibling = (x, y, c), (x, y, 1 - c)
    chips = [(1 - x, y), (x, 1 - y), (1 - x, 1 - y)]

    def copy(a, k, blk, to):
        slab = gats[a].at[4 * blk[0] + 2 * blk[1] + blk[2]]
        return _remote(slab, slab, send_sems, recv_sems, 7 * a + k, to)

    arrays = range(len(gats))
    first = [copy(a, 1 + j, me, (*chip, c)) for j, chip in enumerate(chips) for a in arrays]
    first += [copy(a, 0, me, sibling) for a in arrays]
    for cp in first:
        cp.start()
    passed = []
    for j, chip in enumerate(chips):
        for a in arrays:
            copy(a, 1 + j, (*chip, c), me).wait_recv()
            fwd = copy(a, 4 + j, (*chip, c), sibling)
            fwd.start()
            passed.append(fwd)
    for a in arrays:
        copy(a, 0, sibling, me).wait_recv()
    for j, chip in enumerate(chips):
        for a in arrays:
            copy(a, 4 + j, (*chip, 1 - c), me).wait_recv()
    for cp in first + passed:
        cp.wait_send()


def _gather_weights(w_in, w_uq2, w_oa, w_ob, w_out):
    hw = MLA_HEADS * HEAD_PAD
    uq_rows = Q_LORA_RANK // N_DEV
    cols = D_MODEL // N_DEV

    def body(win_ref, wuq_ref, woa_ref, wob_ref, wout_ref, wa_ref, wb_ref, wc_ref, wq_ref, oa_ref, ob_ref, out_ref,
             g_in, g_uq, g_oa, g_ob, g_out, send_sems, recv_sems):
        x, y, c = _mesh_pos()
        me = 4 * x + 2 * y + c
        g_in[me] = win_ref[0].astype(BF16)
        g_uq[me] = wuq_ref[...].astype(BF16)
        g_oa[me] = woa_ref[0].astype(BF16)
        g_ob[me] = wob_ref[0].astype(BF16)
        g_out[me] = wout_ref[0].astype(BF16)
        _gather_exchange([g_in, g_uq, g_oa, g_ob, g_out], send_sems, recv_sems)

        segs = [wa_ref, wb_ref, wc_ref]
        for j, s0, s1, seg, d0 in _column_runs():
            segs[seg][:, d0:d0 + (s1 - s0)] = g_in[j, :, s0:s1]
        zeros = lambda r, w: jnp.zeros((r, w), BF16)
        wc_ref[:, Q_LORA_RANK:CQ_PAD] = zeros(D_MODEL, CQ_PAD - Q_LORA_RANK)
        wc_ref[:, CQ_PAD + LANES:CQ_PAD + LANES + ROPE_LO] = zeros(D_MODEL, ROPE_LO)
        wc_ref[:, CQ_PAD + LANES + ROPE_HI:SEG_C] = zeros(D_MODEL, LANES - ROPE_HI)
        wq_ref[Q_LORA_RANK:CQ_PAD, :] = zeros(CQ_PAD - Q_LORA_RANK, hw)
        for h in range(MLA_HEADS):
            wq_ref[0:Q_LORA_RANK, HEAD_PAD * h + QK_HEAD_DIM:HEAD_PAD * (h + 1)] = zeros(Q_LORA_RANK, HEAD_PAD - QK_HEAD_DIM)
        for j in range(N_DEV):
            for h in range(MLA_HEADS):
                wq_ref[uq_rows * j:uq_rows * (j + 1), HEAD_PAD * h:HEAD_PAD * h + QK_HEAD_DIM] = g_uq[
                    j, :, QK_HEAD_DIM * h:QK_HEAD_DIM * (h + 1)]
            oa_ref[:, cols * j:cols * (j + 1)] = g_oa[j]
            ob_ref[:, cols * j:cols * (j + 1)] = g_ob[j]
            out_ref[cols * j:cols * (j + 1), :] = g_out[j]

    vmem = pl.BlockSpec(memory_space=pltpu.VMEM)
    return pl.pallas_call(
        body, name="gather_weights",
        out_shape=[jax.ShapeDtypeStruct((D_MODEL, SEG_A), BF16), jax.ShapeDtypeStruct((D_MODEL, SEG_B), BF16),
                   jax.ShapeDtypeStruct((D_MODEL, SEG_C), BF16), jax.ShapeDtypeStruct((CQ_PAD, hw), BF16),
                   jax.ShapeDtypeStruct((MLA_WIDTH, D_MODEL), BF16), jax.ShapeDtypeStruct((SGU_WIDTH, D_MODEL), BF16),
                   jax.ShapeDtypeStruct((D_MODEL, D_MODEL), BF16)],
        in_specs=[vmem] * 5, out_specs=[vmem] * 7,
        scratch_shapes=[pltpu.VMEM((N_DEV, D_MODEL, SHARD_W), BF16),
                        pltpu.VMEM((N_DEV, uq_rows, MLA_HEADS * QK_HEAD_DIM), BF16),
                        pltpu.VMEM((N_DEV, MLA_WIDTH, cols), BF16), pltpu.VMEM((N_DEV, SGU_WIDTH, cols), BF16),
                        pltpu.VMEM((N_DEV, cols, D_MODEL), BF16),
                        pltpu.SemaphoreType.DMA((35,)), pltpu.SemaphoreType.DMA((35,))],
        compiler_params=pltpu.CompilerParams(vmem_limit_bytes=VMEM_BIG),
    )(w_in, w_uq2, w_oa, w_ob, w_out)


C_NAT = 544


def _to_parts(dwa, dwb):
    def body(dwa_ref, dwb_ref, pin_ref):
        pin_ref[0, :, 0:C_NAT] = jnp.zeros((D_MODEL, C_NAT), BF16)
        segs = [dwa_ref, dwb_ref]
        for j, s0, s1, seg, d0 in _column_runs():
            if seg < 2:
                pin_ref[j, :, s0:s1] = segs[seg][:, d0:d0 + (s1 - s0)]

    return pl.pallas_call(body, name="to_parts", out_shape=jax.ShapeDtypeStruct((N_DEV, D_MODEL, SHARD_W), BF16),
                          compiler_params=pltpu.CompilerParams(vmem_limit_bytes=VMEM_MID))(dwa, dwb)


def _sum_landed(lands, c_all, is_dev0):
    r_in, r_oa, r_ob, r_out = lands
    rows = D_MODEL // N_DEV

    def body(rin_ref, roa_ref, rob_ref, rout_ref, call_ref, flag_ref, gin_ref, goa_ref, gob_ref, gout_ref):
        def total(ref, sl):
            acc = ref[0, sl, :].astype(F32)
            for s in range(1, N_DEV):
                acc = acc + ref[s, sl, :].astype(F32)
            return acc

        flag = flag_ref[...]
        for j in range(N_DEV):
            sl = slice(rows * j, rows * (j + 1))
            tot = total(rin_ref, sl)
            gin_ref[0, sl, C_NAT:SHARD_W] = tot[:, C_NAT:SHARD_W]
            gin_ref[0, sl, 0:C_NAT] = tot[:, 0:C_NAT] + flag * call_ref[j]
        goa_ref[0] = total(roa_ref, slice(None))
        gob_ref[0] = total(rob_ref, slice(None))
        gout_ref[0] = total(rout_ref, slice(None))

    return pl.pallas_call(
        body, name="sum_landed",
        out_shape=[jax.ShapeDtypeStruct((1,) + r.shape[1:], F32) for r in lands],
        compiler_params=pltpu.CompilerParams(vmem_limit_bytes=VMEM_MID),
    )(r_in, r_oa, r_ob, r_out, c_all, is_dev0)


def _reduce_grads(dwc, p_uq, p_rep):
    rep_rows = p_rep.shape[1]
    c_rows = D_MODEL // N_DEV
    spec = [((c_rows, C_NAT), BF16, c_rows), (p_uq.shape[1:], BF16, p_uq.shape[1]), ((rep_rows, LANES), F32, rep_rows)]
    n = len(spec)

    def body(dwc_ref, puq_ref, prep_ref, gc_ref, guq_ref, grep_ref, pc_ref, *rest):
        ras, tbs, rbs = rest[0:n], rest[n:2 * n], rest[2 * n:3 * n]
        send_sems, recv_sems = rest[3 * n], rest[3 * n + 1]
        x, y, c = _mesh_pos()
        sibling = (x, y, 1 - c)
        parts = [pc_ref, puq_ref, prep_ref]
        outs = [gc_ref, guq_ref, grep_ref]

        for j, s0, s1, seg, d0 in _column_runs():
            if seg == 2:
                for r in range(N_DEV):
                    pc_ref[r, :, s0:s1] = dwc_ref[c_rows * r:c_rows * (r + 1), d0:d0 + (s1 - s0)]

        stage1 = []
        for chip in range(4):
            for a in range(n):
                cp = _remote(parts[a].at[2 * chip + (1 - c)], ras[a].at[chip], send_sems, recv_sems, 7 * a + chip, sibling)
                cp.start()
                stage1.append(cp)
        for cp in stage1:
            cp.wait_recv()

        def rows_loop(a, fn):
            rows, chunk = spec[a][0][0], spec[a][2]
            if rows == chunk:
                fn(pl.ds(0, rows))
            else:
                def step(i, carry):
                    fn(pl.ds(pl.multiple_of(i * chunk, chunk), chunk))
                    return carry
                lax.fori_loop(0, rows // chunk, step, 0)

        def chip_sum(a, chip, sl):
            return parts[a][2 * chip + c, sl, :].astype(F32) + ras[a][chip, sl, :].astype(F32)

        others = [(1 - x, y), (x, 1 - y), (1 - x, 1 - y)]
        stage2 = []
        for k, (cx, cy) in enumerate(others):
            for a in range(n):
                def fill(sl, a=a, k=k, chip=2 * cx + cy):
                    tbs[a][k, sl, :] = chip_sum(a, chip, sl).astype(spec[a][1])
                rows_loop(a, fill)
                cp = _remote(tbs[a].at[k], rbs[a].at[k], send_sems, recv_sems, 7 * a + 4 + k, (cx, cy, c))
                cp.start()
                stage2.append(cp)
        for cp in stage2:
            cp.wait_recv()
        for a in range(n):
            def final(sl, a=a):
                acc = chip_sum(a, 2 * x + y, sl)
                for k in range(3):
                    acc = acc + rbs[a][k, sl, :].astype(F32)
                outs[a][sl, :] = acc
            rows_loop(a, final)
        for cp in stage1 + stage2:
            cp.wait_send()

    vmem = pl.BlockSpec(memory_space=pltpu.VMEM)
    scratch = [pltpu.VMEM((N_DEV, c_rows, C_NAT), BF16)]
    for lead in (4, 3, 3):
        scratch += [pltpu.VMEM((lead,) + tuple(shape), dt) for shape, dt, _ in spec]
    scratch += [pltpu.SemaphoreType.DMA((7 * n,)), pltpu.SemaphoreType.DMA((7 * n,))]
    return pl.pallas_call(
        body, name="reduce_grads",
        out_shape=[jax.ShapeDtypeStruct(tuple(shape), F32) for shape, _, _ in spec],
        in_specs=[vmem] * 3, out_specs=[vmem] * 3, scratch_shapes=scratch,
        compiler_params=pltpu.CompilerParams(vmem_limit_bytes=VMEM_MID),
    )(dwc, p_uq, p_rep)


def _gather_small(rep_slice, c_slice):
    def body(rep_ref, c_ref, rep_all, c_all, send_sems, recv_sems):
        x, y, c = _mesh_pos()
        me = 4 * x + 2 * y + c
        rep_all[me] = rep_ref[...]
        c_all[me] = c_ref[...]
        _gather_exchange([rep_all, c_all], send_sems, recv_sems)

    vmem = pl.BlockSpec(memory_space=pltpu.VMEM)
    return pl.pallas_call(
        body, name="gather_small",
        out_shape=[jax.ShapeDtypeStruct((N_DEV,) + rep_slice.shape, F32), jax.ShapeDtypeStruct((N_DEV,) + c_slice.shape, F32)],
        in_specs=[vmem] * 2, out_specs=[vmem] * 2,
        scratch_shapes=[pltpu.SemaphoreType.DMA((14,)), pltpu.SemaphoreType.DMA((14,))],
    )(rep_slice, c_slice)


_O_CQ, _O_CKV, _O_KPE, _O_ZA, _O_U, _O_V, _O_ZB, _O_GA, _O_GB = 0, 384, 512, 544, 1056, 1568, 2080, 2592, 3616


def _to_segments(w):
    z = lambda n: jnp.zeros(w.shape[:-1] + (n,), w.dtype)
    seg_a = jnp.concatenate([w[..., _O_GA:_O_GB], w[..., _O_GB:IN_WIDTH], w[..., _O_ZA:_O_U]], axis=-1)
    seg_b = jnp.concatenate([w[..., _O_U:_O_V], w[..., _O_V:_O_ZB], w[..., _O_ZB:_O_GA]], axis=-1)
    seg_c = jnp.concatenate([w[..., _O_CQ:_O_CKV], z(CQ_PAD - Q_LORA_RANK), w[..., _O_CKV:_O_KPE],
                             z(ROPE_LO), w[..., _O_KPE:_O_ZA], z(LANES - ROPE_HI)], axis=-1)
    return seg_a, seg_b, seg_c


def _from_segments(seg_a, seg_b, seg_c):
    kpe0 = CQ_PAD + LANES + ROPE_LO
    return jnp.concatenate([
        seg_c[..., 0:Q_LORA_RANK], seg_c[..., CQ_PAD:CQ_PAD + LANES], seg_c[..., kpe0:kpe0 + QK_ROPE_DIM],
        seg_a[..., 2 * D_MODEL:SEG_A], seg_b, seg_a[..., 0:2 * D_MODEL]], axis=-1)


def kernel(x, positions, w_in, b_in, g_q, w_uq, g_kv, w_ukv, w_oa, sgu_ln_g, sgu_ln_b, w_s, b_s, w_ob, w_out, ln_g, ln_b, loss_target, m_w_in, m_b_in, m_g_q, m_w_uq, m_g_kv, m_w_ukv, m_w_oa, m_sgu_ln_g, m_sgu_ln_b, m_w_s, m_b_s, m_w_ob, m_w_out, m_ln_g, m_ln_b, v_w_in, v_b_in, v_g_q, v_w_uq, v_g_kv, v_w_ukv, v_w_oa, v_sgu_ln_g, v_sgu_ln_b, v_w_s, v_b_s, v_w_ob, v_w_out, v_ln_g, v_ln_b):
    w_uq2 = w_uq[0].reshape(Q_LORA_RANK // N_DEV, MLA_HEADS * QK_HEAD_DIM)
    wa, wb, wc, wq, w_oa_f, w_ob_f, w_out_f = _gather_weights(w_in, w_uq2, w_oa, w_ob, w_out)
    partials = _local_step(x[0], positions, loss_target[0], wa, wb, wc, b_in, g_q, wq, g_kv, w_ukv, w_oa_f, sgu_ln_g,
                           sgu_ln_b, w_s, b_s, w_ob_f, w_out_f, ln_g, ln_b)
    weights = dict(w_in=w_in, b_in=b_in, g_q=g_q, w_uq=w_uq, g_kv=g_kv, w_ukv=w_ukv, w_oa=w_oa, sgu_ln_g=sgu_ln_g,
                   sgu_ln_b=sgu_ln_b, w_s=w_s, b_s=b_s, w_ob=w_ob, w_out=w_out, ln_g=ln_g, ln_b=ln_b)
    moms = dict(w_in=m_w_in, b_in=m_b_in, g_q=m_g_q, w_uq=m_w_uq, g_kv=m_g_kv, w_ukv=m_w_ukv, w_oa=m_w_oa,
                sgu_ln_g=m_sgu_ln_g, sgu_ln_b=m_sgu_ln_b, w_s=m_w_s, b_s=m_b_s, w_ob=m_w_ob, w_out=m_w_out,
                ln_g=m_ln_g, ln_b=m_ln_b)
    vars_ = dict(w_in=v_w_in, b_in=v_b_in, g_q=v_g_q, w_uq=v_w_uq, g_kv=v_g_kv, w_ukv=v_w_ukv, w_oa=v_w_oa,
                 sgu_ln_g=v_sgu_ln_g, sgu_ln_b=v_sgu_ln_b, w_s=v_w_s, b_s=v_b_s, w_ob=v_w_ob, w_out=v_w_out,
                 ln_g=v_ln_g, ln_b=v_ln_b)
    return _reduce_and_update(partials, weights, moms, vars_)


def _local_step(x2, positions, tgt, wa, wb, wc, b_in, g_q, wq, g_kv, w_ukv, w_oa_f, sgu_ln_g, sgu_ln_b, w_s, b_s,
                w_ob_f, w_out_f, ln_g, ln_b):
    ba, bb, bc = _to_segments(b_in)
    w_ukv_bf = w_ukv[0].astype(BF16)
    wkn = jnp.pad(w_ukv_bf[:, :, :QK_NOPE_DIM], ((0, 0), (0, 0), (0, HEAD_PAD - QK_NOPE_DIM))).reshape(KV_LORA_RANK, -1)
    wv = jnp.pad(w_ukv_bf[:, :, QK_NOPE_DIM:], ((0, 0), (0, 0), (0, HEAD_PAD - V_HEAD_DIM))).reshape(KV_LORA_RANK, -1)
    gq = jnp.pad(g_q, ((0, 0), (0, CQ_PAD - Q_LORA_RANK)))
    bias_full = jnp.repeat(b_s[0].T, SGU_GROUP_DIM, axis=1)
    w_s3 = w_s[0]
    w_st3 = jnp.swapaxes(w_s3, 1, 2)

    inv_freq = ROPE_THETA ** (-jnp.arange(0, QK_ROPE_DIM, 2, dtype=F32) / QK_ROPE_DIM)
    invf_lane = jnp.concatenate([jnp.zeros((ROPE_LO,), F32), inv_freq, inv_freq,
                                 jnp.zeros((LANES - ROPE_HI,), F32)]).reshape(1, LANES)
    c_t, sa_t, sb_t = _rope_tables(positions.reshape(SEQ, 1), invf_lane)

    x_bf, xt_bf = _cast_x(x2)
    h_a = _mm(x_bf, wa, bias=ba, tm=512, tn=SEG_A // 2, name="in_proj_a")
    h_b = _mm(x_bf, wb, bias=bb, tm=512, tn=SEG_B // 2, name="in_proj_b")
    h_c = _mm(x_bf, wc, bias=bc, tm=512, tn=SEG_C, name="in_proj_c")
    q, k, kt, vx, vxt = _mla_prep(h_c, gq, g_kv, wq, wkn, wv, c_t, sa_t, sb_t)
    o, lse = _attn_fwd(q, kt, vx)
    y_b = _sgu_fwd(h_b, sgu_ln_g, sgu_ln_b, w_s3, bias_full)

    (loss_row, dx_res, dh_a, d_o, d_yb, p_oa, p_ob, p_out, d_lng, d_lnb, d_ba) = _merge(
        x2, o, h_a, y_b, tgt, w_oa_f, w_ob_f, w_out_f, ln_g, ln_b)
    dh_b, d_ws, d_bs_t, d_slg, d_slb, d_bb = _sgu_bwd(h_b, d_yb, sgu_ln_g, sgu_ln_b, w_s3, w_st3, bias_full)
    d_wa = _mm(xt_bf, dh_a, out_dtype=BF16, tm=512, tn=512, name="dw_in_a")
    d_wb = _mm(xt_bf, dh_b, out_dtype=BF16, tm=512, tn=512, name="dw_in_b")
    dq, dk, dv, landed = _attn_bwd(q, kt, k, vxt, d_o, o, lse, (_to_parts(d_wa, d_wb), p_oa, p_ob, p_out))
    dh_c, p_uq, d_wkn, d_wv, d_gq, d_gkv, d_bc = _mla_bwd(dq, dk, dv, h_c, gq, g_kv, wq, wkn, wv, c_t, sa_t, sb_t)
    d_wc = _mm(xt_bf, dh_c, out_dtype=BF16, tm=512, tn=SEG_C, name="dw_in_c")

    dx = _mm(dh_a, wa, tb=True, add=dx_res, tm=512, tn=D_MODEL, name="dx_a")
    dx = _mm(dh_b, wb, tb=True, add=dx, tm=512, tn=D_MODEL, name="dx_b")
    dx = _mm(dh_c, wc, tb=True, add=dx, tm=512, tn=D_MODEL, name="dx_c")

    p_b_in = _from_segments(d_ba, d_bb, d_bc)
    p_w_ukv = jnp.concatenate([d_wkn.reshape(KV_LORA_RANK, MLA_HEADS, HEAD_PAD)[:, :, :QK_NOPE_DIM],
                               d_wv.reshape(KV_LORA_RANK, MLA_HEADS, HEAD_PAD)[:, :, :V_HEAD_DIM]], axis=-1)
    p_g_q = d_gq[:, :Q_LORA_RANK]
    p_b_s = d_bs_t[:, :SGU_GROUPS].T
    replicated = [p_b_in, p_g_q, d_gkv, p_w_ukv, d_slg, d_slb, d_ws, p_b_s, d_lng, d_lnb]
    return loss_row, dx, landed, d_wc, p_uq, replicated


_NAMES = ["w_in", "b_in", "g_q", "w_uq", "g_kv", "w_ukv", "w_oa", "sgu_ln_g", "sgu_ln_b", "w_s", "b_s", "w_ob",
          "w_out", "ln_g", "ln_b"]
_REPLICATED = ["b_in", "g_q", "g_kv", "w_ukv", "sgu_ln_g", "sgu_ln_b", "w_s", "b_s", "ln_g", "ln_b"]


def _reduce_and_update(partials, weights, moms, vars_):
    loss_row, dx, landed, d_wc, p_uq, replicated = partials
    rep_flat = jnp.concatenate([a.reshape(-1) for a in replicated] + [loss_row[0, :1]])
    rep_flat = jnp.pad(rep_flat, (0, N_DEV * PACK_R_ROWS * LANES - rep_flat.size))
    c_slice, g_uq, rep_slice = _reduce_grads(d_wc, p_uq, rep_flat.reshape(N_DEV, PACK_R_ROWS, LANES))
    rep_all, c_all = _gather_small(rep_slice, c_slice)
    rep_sum = rep_all.reshape(-1)
    device = 4 * lax.axis_index("x") + 2 * lax.axis_index("y") + lax.axis_index("c")
    g_in, g_oa, g_ob, g_out = _sum_landed(landed, c_all, (device == 0).astype(F32).reshape(1, 1))
    grads, pos = dict(w_in=g_in, w_uq=g_uq, w_oa=g_oa, w_ob=g_ob, w_out=g_out), 0
    for nm in _REPLICATED:
        grads[nm] = rep_sum[pos:pos + weights[nm].size]
        pos += weights[nm].size
    loss = rep_sum[pos]
    grads = {nm: grads[nm].reshape(weights[nm].shape) for nm in _NAMES}
    deltas, new_m, new_v = _adamw_all([weights[nm] for nm in _NAMES], [grads[nm] for nm in _NAMES],
                                      [moms[nm] for nm in _NAMES], [vars_[nm] for nm in _NAMES])
    return (loss, dx.reshape(1, SEQ, D_MODEL), *[grads[nm] for nm in _NAMES], *deltas, *new_m, *new_v)
```

```python
import math

import jax
import jax.numpy as jnp
from jax import lax
from jax.experimental import pallas as pl
from jax.experimental.pallas import tpu as pltpu

F32 = jnp.float32
BF16 = jnp.bfloat16

D_MODEL = 1024
SEQ = 2048
N_DEV = 8
MLA_HEADS = 8
Q_LORA_RANK = 384
KV_LORA_RANK = 128
QK_NOPE_DIM = 64
QK_ROPE_DIM = 32
V_HEAD_DIM = 64
QK_HEAD_DIM = QK_NOPE_DIM + QK_ROPE_DIM
MLA_WIDTH = MLA_HEADS * V_HEAD_DIM
ROPE_THETA = 10000.0
SGU_GROUPS = 8
SGU_GROUP_DIM = 64
SGU_WIDTH = SGU_GROUPS * SGU_GROUP_DIM
CHUNK = 128
RMS_EPS = 1e-6
LN_EPS = 1e-5
DN_ALPHA = 2.0 ** 0.25
IN_WIDTH = 4640
ATTN_SCALE = QK_HEAD_DIM ** -0.5

ADAM_LR = 0.001
ADAM_B1 = 0.9
ADAM_B2 = 0.999
ADAM_EPS = 1e-08
ADAM_WD = 0.01
ADAM_STEP = 10

LANES = 128
HEAD_PAD = 128
ROPE_LO = QK_NOPE_DIM
ROPE_MID = ROPE_LO + QK_ROPE_DIM // 2
ROPE_HI = ROPE_LO + QK_ROPE_DIM
CQ_PAD = 512

SEG_A = 2560
SEG_B = 1536
SEG_C = 768

PACK_R_ROWS = 272
VMEM_BIG = 56 * 1024 * 1024
VMEM_MID = 40 * 1024 * 1024


def _sigmoid(x):
    return 1.0 / (1.0 + jnp.exp(-x))


def _gelu_and_grad(x):
    c0 = math.sqrt(2.0 / math.pi)
    x2 = x * x
    t = jnp.tanh(c0 * (x + 0.044715 * x * x2))
    g = 0.5 * x * (1.0 + t)
    dg = 0.5 * (1.0 + t) + 0.5 * x * (1.0 - t * t) * (c0 * (1.0 + 3.0 * 0.044715 * x2))
    return g, dg


def _dot(a, b, dims):
    return lax.dot_general(a, b, (dims, ((), ())), preferred_element_type=F32)


_NN = ((1,), (0,))
_NT = ((1,), (1,))
_TN = ((0,), (0,))


def _store_grad(dh_ref, db_ref, col, val):
    cols = slice(col, col + val.shape[1])
    dh_ref[:, cols] = val.astype(BF16)
    db_ref[:, cols] += jnp.sum(val, axis=0, keepdims=True)


def _mm(a, b, *, tb=False, bias=None, add=None, out_dtype=F32, tm, tn, name):
    m, k = a.shape
    n = b.shape[0] if tb else b.shape[1]
    assert m % tm == 0 and n % tn == 0
    dims = _NT if tb else _NN

    def body(*refs):
        a_ref, b_ref = refs[0], refs[1]
        pos = 2
        r = _dot(a_ref[...], b_ref[...], dims)
        if bias is not None:
            r = r + refs[pos][...]; pos += 1
        if add is not None:
            r = r + refs[pos][...]; pos += 1
        refs[pos][...] = r.astype(out_dtype)

    b_spec = pl.BlockSpec((tn, k), lambda j, i: (j, 0)) if tb else pl.BlockSpec((k, tn), lambda j, i: (0, j))
    in_specs, args = [pl.BlockSpec((tm, k), lambda j, i: (i, 0)), b_spec], [a, b]
    if bias is not None:
        in_specs.append(pl.BlockSpec((1, tn), lambda j, i: (0, j))); args.append(bias)
    if add is not None:
        in_specs.append(pl.BlockSpec((tm, tn), lambda j, i: (i, j))); args.append(add)
    return pl.pallas_call(
        body, name=name, grid=(n // tn, m // tm), in_specs=in_specs,
        out_specs=pl.BlockSpec((tm, tn), lambda j, i: (i, j)), out_shape=jax.ShapeDtypeStruct((m, n), out_dtype),
        compiler_params=pltpu.CompilerParams(dimension_semantics=("arbitrary", "arbitrary"), vmem_limit_bytes=VMEM_BIG),
    )(*args)


def _cast_x(x2):
    tm = 256

    def body(x_ref, xb_ref, xt_ref):
        x = x_ref[...]
        xb_ref[...] = x.astype(BF16)
        xt_ref[...] = x.T.astype(BF16)

    return pl.pallas_call(
        body, name="cast_x", grid=(SEQ // tm,),
        in_specs=[pl.BlockSpec((tm, D_MODEL), lambda i: (i, 0))],
        out_specs=[pl.BlockSpec((tm, D_MODEL), lambda i: (i, 0)), pl.BlockSpec((D_MODEL, tm), lambda i: (0, i))],
        out_shape=[jax.ShapeDtypeStruct((SEQ, D_MODEL), BF16), jax.ShapeDtypeStruct((D_MODEL, SEQ), BF16)],
        compiler_params=pltpu.CompilerParams(dimension_semantics=("arbitrary",)),
    )(x2)


def _rope_tables(pos_col, invf_lane):
    def body(pos_ref, invf_ref, c_ref, sa_ref, sb_ref):
        ang = pos_ref[...].astype(F32) * invf_ref[...]
        cs, sn = jnp.cos(ang), jnp.sin(ang)
        lane = lax.broadcasted_iota(jnp.int32, ang.shape, 1)
        c_ref[...] = jnp.where(lane < ROPE_LO, 1.0, jnp.where(lane < ROPE_HI, cs, 0.0))
        sa_ref[...] = jnp.where(jnp.logical_and(lane >= ROPE_LO, lane < ROPE_MID), -sn, 0.0)
        sb_ref[...] = jnp.where(jnp.logical_and(lane >= ROPE_MID, lane < ROPE_HI), sn, 0.0)

    shp = jax.ShapeDtypeStruct((SEQ, LANES), F32)
    return pl.pallas_call(body, name="rope_tables", out_shape=[shp, shp, shp])(pos_col, invf_lane)


def _rope(x, c, sa, sb):
    return x * c + pltpu.roll(x, LANES - 16, 1) * sa + pltpu.roll(x, 16, 1) * sb


def _rope_t(dy, c, sa, sb):
    return dy * c + pltpu.roll(dy * sa, 16, 1) + pltpu.roll(dy * sb, LANES - 16, 1)


def _mla_prep(h_c, gq, gkv, wq, wkn, wvx, c_t, sa_t, sb_t):
    tm = 256
    hw = MLA_HEADS * HEAD_PAD

    def body(cq_ref, ckv_ref, kpe_ref, gq_ref, gkv_ref, wq_ref, wkn_ref, wvx_ref, c_ref, sa_ref, sb_ref,
             q_ref, k_ref, kt_ref, vx_ref, vxt_ref):
        c, sa, sb = c_ref[...], sa_ref[...], sb_ref[...]
        cq = cq_ref[...]
        rq = lax.rsqrt(jnp.sum(cq * cq, axis=1, keepdims=True) * (1.0 / Q_LORA_RANK) + RMS_EPS)
        cqn = ((cq * rq) * gq_ref[...]).astype(BF16)
        qall = _dot(cqn, wq_ref[...], _NN)
        for h in range(MLA_HEADS):
            sl = slice(HEAD_PAD * h, HEAD_PAD * (h + 1))
            q_ref[:, sl] = (_rope(qall[:, sl], c, sa, sb) * ATTN_SCALE).astype(BF16)
        ckv = ckv_ref[...]
        rkv = lax.rsqrt(jnp.sum(ckv * ckv, axis=1, keepdims=True) * (1.0 / KV_LORA_RANK) + RMS_EPS)
        ckvn = ((ckv * rkv) * gkv_ref[...]).astype(BF16)
        knall = _dot(ckvn, wkn_ref[...], _NN)
        vall = _dot(ckvn, wvx_ref[...], _NN)
        kper = _rope(kpe_ref[...], c, sa, sb)
        ones_half = (lax.broadcasted_iota(jnp.int32, (tm, HEAD_PAD), 1) >= V_HEAD_DIM).astype(F32)
        for h in range(MLA_HEADS):
            sl = slice(HEAD_PAD * h, HEAD_PAD * (h + 1))
            kh = knall[:, sl] + kper
            vh = vall[:, sl] + ones_half
            k_ref[:, sl] = kh.astype(BF16)
            kt_ref[sl, :] = kh.T.astype(BF16)
            vx_ref[:, sl] = vh.astype(BF16)
            vxt_ref[sl, :] = vh.T.astype(BF16)

    full = lambda shape: pl.BlockSpec(shape, lambda i: (0, 0))
    tab = pl.BlockSpec((tm, LANES), lambda i: (i, 0))
    row = pl.BlockSpec((tm, hw), lambda i: (i, 0))
    col = pl.BlockSpec((hw, tm), lambda i: (0, i))
    return pl.pallas_call(
        body, name="mla_prep", grid=(SEQ // tm,),
        in_specs=[pl.BlockSpec((tm, CQ_PAD), lambda i: (i, 0)),
                  pl.BlockSpec((tm, LANES), lambda i: (i, CQ_PAD // LANES)),
                  pl.BlockSpec((tm, LANES), lambda i: (i, CQ_PAD // LANES + 1)),
                  full((1, CQ_PAD)), full((1, KV_LORA_RANK)),
                  full((CQ_PAD, hw)), full((KV_LORA_RANK, hw)), full((KV_LORA_RANK, hw)), tab, tab, tab],
        out_specs=[row, row, col, row, col],
        out_shape=[jax.ShapeDtypeStruct((SEQ, hw), BF16), jax.ShapeDtypeStruct((SEQ, hw), BF16),
                   jax.ShapeDtypeStruct((hw, SEQ), BF16), jax.ShapeDtypeStruct((SEQ, hw), BF16),
                   jax.ShapeDtypeStruct((hw, SEQ), BF16)],
        compiler_params=pltpu.CompilerParams(dimension_semantics=("arbitrary",), vmem_limit_bytes=VMEM_MID),
    )(h_c, h_c, h_c, gq, gkv, wq, wkn, wvx, c_t, sa_t, sb_t)


ATT_T = 512
ATT_STRIP = 64


def _attn_fwd(q, kt, vx):
    t, rs = ATT_T, ATT_STRIP

    def body(q_ref, kt_ref, vx_ref, o_ref, l_ref, s_scr, p_scr, m_scr, a_scr, acc_scr):
        qi = pl.program_id(1)
        lane = lax.broadcasted_iota(jnp.int32, (t, LANES), 1)
        m_scr[...] = jnp.full((2, t, LANES), -1e30, F32)
        acc_scr[...] = jnp.zeros((2, t, LANES), F32)

        def block(j, masked):
            off = pl.multiple_of(j * t, t)
            for a in range(2):
                sl = slice(HEAD_PAD * a, HEAD_PAD * (a + 1))
                s_scr[a] = _dot(q_ref[:, sl], kt_ref[sl, pl.ds(off, t)], _NN)
                for r in range(t // rs):
                    rows = slice(rs * r, rs * (r + 1))
                    s = s_scr[a, rows, :]
                    if masked:
                        rowi = lax.broadcasted_iota(jnp.int32, (rs, t), 0) + rs * r
                        coli = lax.broadcasted_iota(jnp.int32, (rs, t), 1)
                        s = jnp.where(coli <= rowi, s, -1e30)
                    m_old = m_scr[a, rows, :]
                    m_new = jnp.maximum(m_old, jnp.max(s, axis=1, keepdims=True))
                    p_scr[a, rows, :] = jnp.exp(s - m_new[:, :1]).astype(BF16)
                    a_scr[a, rows, :] = jnp.exp(m_old - m_new)
                    m_scr[a, rows, :] = m_new
                acc_scr[a] = acc_scr[a] * a_scr[a] + _dot(p_scr[a], vx_ref[pl.ds(off, t), sl], _NN)

        def step(j, carry):
            block(j, False)
            return carry
        lax.fori_loop(0, qi, step, 0)
        block(qi, True)
        res = []
        for a in range(2):
            acc = acc_scr[a]
            l = acc[:, V_HEAD_DIM:V_HEAD_DIM + 1]
            res.append((acc / l, m_scr[a] + jnp.log(l)))
        o_ref[...] = jnp.where(lane < V_HEAD_DIM, res[0][0], pltpu.roll(res[1][0], V_HEAD_DIM, 1))
        l_ref[...] = jnp.where(lane < V_HEAD_DIM, res[0][1], res[1][1])

    return pl.pallas_call(
        body, name="attn_fwd", grid=(MLA_HEADS // 2, SEQ // t),
        in_specs=[pl.BlockSpec((t, 2 * HEAD_PAD), lambda p, i: (i, p)),
                  pl.BlockSpec((2 * HEAD_PAD, SEQ), lambda p, i: (p, 0)),
                  pl.BlockSpec((SEQ, 2 * HEAD_PAD), lambda p, i: (0, p))],
        out_specs=[pl.BlockSpec((t, LANES), lambda p, i: (i, p)),
                   pl.BlockSpec((t, LANES), lambda p, i: (i, p))],
        out_shape=[jax.ShapeDtypeStruct((SEQ, MLA_WIDTH), F32), jax.ShapeDtypeStruct((SEQ, MLA_WIDTH), F32)],
        scratch_shapes=[pltpu.VMEM((2, t, t), F32), pltpu.VMEM((2, t, t), BF16), pltpu.VMEM((2, t, LANES), F32),
                        pltpu.VMEM((2, t, LANES), F32), pltpu.VMEM((2, t, LANES), F32)],
        compiler_params=pltpu.CompilerParams(dimension_semantics=("arbitrary", "arbitrary"), vmem_limit_bytes=VMEM_MID),
    )(q, kt, vx)


def _exchange_parts(parts, lands, send_sems, recv_sems, local_sems):
    x, y, c = _mesh_pos()
    me = 4 * x + 2 * y + c
    peers = [(x, y, 1 - c), (1 - x, y, c), (x, 1 - y, c), (1 - x, 1 - y, c),
             (1 - x, y, 1 - c), (x, 1 - y, 1 - c), (1 - x, 1 - y, 1 - c)]
    remote, local = [], []
    for a, (part, land) in enumerate(zip(parts, lands)):
        for k, peer in enumerate(peers):
            t = 4 * peer[0] + 2 * peer[1] + peer[2]
            remote.append(_remote(part.at[t], land.at[me], send_sems, recv_sems, 7 * a + k, peer))
        local.append(pltpu.make_async_copy(part.at[me], land.at[me], local_sems.at[a]))
    return remote, local


def _exchange_start(first_step, exchange):
    remote, local = exchange

    @pl.when(first_step)
    def _():
        for cp in remote + local:
            cp.start()


def _exchange_finish(last_step, exchange):
    remote, local = exchange

    @pl.when(last_step)
    def _():
        for cp in remote:
            cp.wait_recv()
        for cp in remote:
            cp.wait_send()
        for cp in local:
            cp.wait()


def _exchange_sems(npart):
    return [pltpu.SemaphoreType.DMA((7 * npart,)), pltpu.SemaphoreType.DMA((7 * npart,)),
            pltpu.SemaphoreType.DMA((npart,))]


def _attn_bwd(q, kt, k, vxt, d_o, o, lse, parts):
    t, rs = ATT_T, ATT_STRIP
    nq = SEQ // t
    npart = len(parts)
    nsteps = MLA_HEADS // 2

    def body(q_ref, kt_ref, k_ref, vxt_ref, do_ref, o_ref, l_ref, *rest):
        part_refs, rest = rest[:npart], rest[npart:]
        dq_ref, dk_ref, dv_ref = rest[:3]
        land_refs, rest = rest[3:3 + npart], rest[3 + npart:]
        s_scr, dp_scr, p_scr, ds_scr, st_scr, send_sems, recv_sems, local_sems = rest
        exchange = _exchange_parts(part_refs, land_refs, send_sems, recv_sems, local_sems)
        _exchange_start(pl.program_id(0) == 0, exchange)
        dk_ref[...] = jnp.zeros_like(dk_ref)
        dv_ref[...] = jnp.zeros_like(dv_ref)
        lane = lax.broadcasted_iota(jnp.int32, (t, LANES), 1)

        def qtile(i, carry):
            ioff = pl.multiple_of(i * t, t)
            do_i = do_ref[pl.ds(ioff, t), :]
            o_i = o_ref[pl.ds(ioff, t), :]
            l_i = l_ref[pl.ds(ioff, t), :]
            for a in range(2):
                sl = slice(HEAD_PAD * a, HEAD_PAD * (a + 1))
                sel = (lane < V_HEAD_DIM) if a == 0 else (lane >= V_HEAD_DIM)
                doa = jnp.where(sel, do_i, 0.0)
                oa = o_i
                if a == 1:
                    doa = pltpu.roll(doa, V_HEAD_DIM, 1)
                    oa = pltpu.roll(o_i, V_HEAD_DIM, 1)
                st_scr[0] = jnp.broadcast_to(jnp.sum(doa * oa, axis=1, keepdims=True), (t, LANES))
                st_scr[1] = jnp.broadcast_to(l_i[:, V_HEAD_DIM * a:V_HEAD_DIM * a + 1], (t, LANES))
                doa_bf = doa.astype(BF16)
                qa = q_ref[pl.ds(ioff, t), sl]

                def block(j, masked, dq_acc, sl=sl, qa=qa, doa_bf=doa_bf):
                    joff = pl.multiple_of(j * t, t)
                    s_scr[...] = _dot(qa, kt_ref[sl, pl.ds(joff, t)], _NN)
                    dp_scr[...] = _dot(doa_bf, vxt_ref[sl, pl.ds(joff, t)], _NN)
                    for r in range(t // rs):
                        rows = slice(rs * r, rs * (r + 1))
                        p = jnp.exp(s_scr[rows, :] - st_scr[1, rows, :1])
                        if masked:
                            rowi = lax.broadcasted_iota(jnp.int32, (rs, t), 0) + rs * r
                            coli = lax.broadcasted_iota(jnp.int32, (rs, t), 1)
                            p = jnp.where(coli <= rowi, p, 0.0)
                        p_scr[rows, :] = p.astype(BF16)
                        ds_scr[rows, :] = (p * (dp_scr[rows, :] - st_scr[0, rows, :1])).astype(BF16)
                    dk_ref[pl.ds(joff, t), sl] += _dot(ds_scr[...], qa, _TN)
                    dv_ref[pl.ds(joff, t), sl] += _dot(p_scr[...], doa_bf, _TN)
                    return dq_acc + _dot(ds_scr[...], k_ref[pl.ds(joff, t), sl], _NN)

                dq_acc = lax.fori_loop(0, i, lambda j, acc: block(j, False, acc), jnp.zeros((t, HEAD_PAD), F32))
                dq_ref[pl.ds(ioff, t), sl] = block(i, True, dq_acc)
            return carry

        lax.fori_loop(0, nq, qtile, 0)
        _exchange_finish(pl.program_id(0) == nsteps - 1, exchange)

    hw = MLA_HEADS * HEAD_PAD
    wide = pl.BlockSpec((SEQ, 2 * HEAD_PAD), lambda p: (0, p))
    wide_t = pl.BlockSpec((2 * HEAD_PAD, SEQ), lambda p: (p, 0))
    narrow = pl.BlockSpec((SEQ, LANES), lambda p: (0, p))
    hbm = pl.BlockSpec(memory_space=pl.ANY)
    res = pl.pallas_call(
        body, name="attn_bwd", grid=(nsteps,),
        in_specs=[wide, wide_t, wide, wide_t, narrow, narrow, narrow] + [hbm] * npart,
        out_specs=[wide, wide, wide] + [hbm] * npart,
        out_shape=[jax.ShapeDtypeStruct((SEQ, hw), F32)] * 3 + [jax.ShapeDtypeStruct(p.shape, p.dtype) for p in parts],
        scratch_shapes=[pltpu.VMEM((t, t), F32), pltpu.VMEM((t, t), F32), pltpu.VMEM((t, t), BF16),
                        pltpu.VMEM((t, t), BF16), pltpu.VMEM((2, t, LANES), F32)] + _exchange_sems(npart),
        compiler_params=pltpu.CompilerParams(dimension_semantics=("arbitrary",), vmem_limit_bytes=VMEM_BIG),
    )(q, kt, k, vxt, d_o, o, lse, *parts)
    return res[0], res[1], res[2], res[3:]


def _sgu_math(u, v, zb, lg, lb, ws_ref, bias):
    ug, dug = _gelu_and_grad(u)
    vg, dvg = _gelu_and_grad(v)
    mu = jnp.mean(vg, axis=1, keepdims=True)
    xc = vg - mu
    rstd = lax.rsqrt(jnp.mean(xc * xc, axis=1, keepdims=True) + LN_EPS)
    xh = xc * rstd
    vn_bf = (xh * lg + lb).astype(BF16)
    grp = lax.broadcasted_iota(jnp.int32, (CHUNK, SGU_WIDTH), 1) // SGU_GROUP_DIM
    r_i = lax.broadcasted_iota(jnp.int32, (CHUNK, CHUNK), 0)
    c_i = lax.broadcasted_iota(jnp.int32, (CHUNK, CHUNK), 1)
    tri, tri_t = r_i >= c_i, r_i <= c_i
    mixed = bias
    for g in range(SGU_GROUPS):
        wt = jnp.where(tri, ws_ref[g], 0.0).astype(BF16)
        mixed = mixed + jnp.where(grp == g, _dot(wt, vn_bf, _NN), 0.0)
    sb = _sigmoid(zb)
    return ug, dug, dvg, rstd, xh, vn_bf, grp, tri, tri_t, mixed, sb


def _sgu_fwd(h_b, lg, lb, w_s, bias_full):
    def body(u_ref, v_ref, zb_ref, lg_ref, lb_ref, ws_ref, bias_ref, yb_ref):
        zb = zb_ref[...]
        ug, _, _, _, _, _, _, _, _, mixed, sb = _sgu_math(u_ref[...], v_ref[...], zb, lg_ref[...], lb_ref[...],
                                                       ws_ref, bias_ref[...])
        yb_ref[...] = (ug * mixed) * (zb * sb)

    blk = lambda c: pl.BlockSpec((CHUNK, SGU_WIDTH), lambda i, c=c: (i, c))
    full2 = lambda shape: pl.BlockSpec(shape, lambda i: (0, 0))
    return pl.pallas_call(
        body, name="sgu_fwd", grid=(SEQ // CHUNK,),
        in_specs=[blk(0), blk(1), blk(2), full2((1, SGU_WIDTH)), full2((1, SGU_WIDTH)),
                  pl.BlockSpec((SGU_GROUPS, CHUNK, CHUNK), lambda i: (0, 0, 0)), full2((CHUNK, SGU_WIDTH))],
        out_specs=pl.BlockSpec((CHUNK, SGU_WIDTH), lambda i: (i, 0)),
        out_shape=jax.ShapeDtypeStruct((SEQ, SGU_WIDTH), F32),
        compiler_params=pltpu.CompilerParams(dimension_semantics=("arbitrary",)),
    )(h_b, h_b, h_b, lg, lb, w_s, bias_full)


def _sgu_bwd(h_b, d_yb, lg, lb, w_s, w_st, bias_full, parts):
    nsteps = SEQ // CHUNK
    npart = len(parts)

    def body(u_ref, v_ref, zb_ref, dyb_ref, lg_ref, lb_ref, ws_ref, wst_ref, bias_ref, *rest):
        part_refs, rest = rest[:npart], rest[npart:]
        dhb_ref, dws_ref, dbs_ref, dlg_ref, dlb_ref, dbb_ref = rest[:6]
        land_refs, (dbias_acc, send_sems, recv_sems, local_sems) = rest[6:6 + npart], rest[6 + npart:]
        step = pl.program_id(0)
        exchange = _exchange_parts(part_refs, land_refs, send_sems, recv_sems, local_sems)
        _exchange_start(step == 0, exchange)

        @pl.when(step == 0)
        def _():
            dbb_ref[...] = jnp.zeros_like(dbb_ref)
            dws_ref[...] = jnp.zeros_like(dws_ref)
            dlg_ref[...] = jnp.zeros_like(dlg_ref)
            dlb_ref[...] = jnp.zeros_like(dlb_ref)
            dbias_acc[...] = jnp.zeros_like(dbias_acc)

        zb = zb_ref[...]
        lg = lg_ref[...]
        ug, dug, dvg, rstd, xh, vn_bf, grp, tri, tri_t, mixed, sb = _sgu_math(
            u_ref[...], v_ref[...], zb, lg, lb_ref[...], ws_ref, bias_ref[...])
        dyb = dyb_ref[...]
        dsgu = dyb * (zb * sb)
        dzb = dyb * (ug * mixed) * (sb * (1.0 + zb * (1.0 - sb)))
        du = dsgu * mixed * dug
        dmixed = dsgu * ug
        dbias_acc[...] += dmixed
        dvn = jnp.zeros((CHUNK, SGU_WIDTH), F32)
        for g in range(SGU_GROUPS):
            dm_g = jnp.where(grp == g, dmixed, 0.0).astype(BF16)
            wtt = jnp.where(tri_t, wst_ref[g], 0.0).astype(BF16)
            dvn = dvn + _dot(wtt, dm_g, _NN)
            dws_ref[g] += jnp.where(tri, _dot(dm_g, vn_bf, _NT), 0.0)
        dlg_ref[...] += jnp.sum(dvn * xh, axis=0, keepdims=True)
        dlb_ref[...] += jnp.sum(dvn, axis=0, keepdims=True)
        dxh = dvn * lg
        dvgel = rstd * (dxh - jnp.mean(dxh, axis=1, keepdims=True) - xh * jnp.mean(dxh * xh, axis=1, keepdims=True))
        _store_grad(dhb_ref, dbb_ref, 0, du)
        _store_grad(dhb_ref, dbb_ref, SGU_WIDTH, dvgel * dvg)
        _store_grad(dhb_ref, dbb_ref, 2 * SGU_WIDTH, dzb)

        @pl.when(step == nsteps - 1)
        def _():
            acc = dbias_acc[...]
            lane = lax.broadcasted_iota(jnp.int32, (CHUNK, LANES), 1)
            out = jnp.zeros((CHUNK, LANES), F32)
            for g in range(SGU_GROUPS):
                sg = jnp.sum(jnp.where(grp == g, acc, 0.0), axis=1, keepdims=True)
                out = jnp.where(lane == g, sg, out)
            dbs_ref[...] = out

        _exchange_finish(step == nsteps - 1, exchange)

    blk = lambda c: pl.BlockSpec((CHUNK, SGU_WIDTH), lambda i, c=c: (i, c))
    full2 = lambda shape: pl.BlockSpec(shape, lambda i: (0, 0))
    full3 = pl.BlockSpec((SGU_GROUPS, CHUNK, CHUNK), lambda i: (0, 0, 0))
    hbm = pl.BlockSpec(memory_space=pl.ANY)
    res = pl.pallas_call(
        body, name="sgu_bwd", grid=(nsteps,),
        in_specs=[blk(0), blk(1), blk(2), pl.BlockSpec((CHUNK, SGU_WIDTH), lambda i: (i, 0)),
                  full2((1, SGU_WIDTH)), full2((1, SGU_WIDTH)), full3, full3, full2((CHUNK, SGU_WIDTH))] + [hbm] * npart,
        out_specs=[pl.BlockSpec((CHUNK, SEG_B), lambda i: (i, 0)), full3, full2((CHUNK, LANES)),
                   full2((1, SGU_WIDTH)), full2((1, SGU_WIDTH)), full2((1, SEG_B))] + [hbm] * npart,
        out_shape=[jax.ShapeDtypeStruct((SEQ, SEG_B), BF16),
                   jax.ShapeDtypeStruct((SGU_GROUPS, CHUNK, CHUNK), F32),
                   jax.ShapeDtypeStruct((CHUNK, LANES), F32),
                   jax.ShapeDtypeStruct((1, SGU_WIDTH), F32), jax.ShapeDtypeStruct((1, SGU_WIDTH), F32),
                   jax.ShapeDtypeStruct((1, SEG_B), F32)] + [jax.ShapeDtypeStruct(p.shape, p.dtype) for p in parts],
        scratch_shapes=[pltpu.VMEM((CHUNK, SGU_WIDTH), F32)] + _exchange_sems(npart),
        compiler_params=pltpu.CompilerParams(dimension_semantics=("arbitrary",)),
    )(h_b, h_b, h_b, d_yb, lg, lb, w_s, w_st, bias_full, *parts)
    return res[:6], res[6:]


def _merge(x, o, h_a, y_b, target, w_oa, w_ob, w_out, ln_g, ln_b):
    tm = 256
    nsteps = SEQ // tm

    def body(x_ref, o_ref, ga_ref, gb_ref, za_ref, yb_ref, tgt_ref, woa_ref, wob_ref, wout_ref, lng_ref, lnb_ref,
             loss_ref, dxr_ref, dha_ref, do_ref, dyb_ref, poa_ref, pob_ref, pout_ref, dlng_ref, dlnb_ref, dba_ref,
             dwoa_ref, dwob_ref, dwout_ref):
        step = pl.program_id(0)

        @pl.when(step == 0)
        def _():
            for r in (loss_ref, dwoa_ref, dwob_ref, dwout_ref, dlng_ref, dlnb_ref, dba_ref):
                r[...] = jnp.zeros_like(r)

        o = o_ref[...]
        za = za_ref[...]
        sa = _sigmoid(za)
        ya_bf = (o * (za * sa)).astype(BF16)
        yb_bf = yb_ref[...].astype(BF16)
        woa, wob, wout = woa_ref[...], wob_ref[...], wout_ref[...]
        pa = _dot(ya_bf, woa, _NN)
        pb = _dot(yb_bf, wob, _NN)
        sga = _sigmoid(ga_ref[...])
        sgb = _sigmoid(gb_ref[...])
        merged_bf = (sga * pa + sgb * pb).astype(BF16)
        r = DN_ALPHA * x_ref[...] + _dot(merged_bf, wout, _NN)
        mu = jnp.mean(r, axis=1, keepdims=True)
        rc = r - mu
        rstd = lax.rsqrt(jnp.mean(rc * rc, axis=1, keepdims=True) + LN_EPS)
        xh = rc * rstd
        lng = lng_ref[...]
        y = xh * lng + lnb_ref[...]
        e = y - tgt_ref[...]
        loss_ref[...] += 0.5 * jnp.sum(jnp.sum(e * e, axis=1, keepdims=True) * (1.0 / D_MODEL), axis=0, keepdims=True)

        dy = e * (1.0 / D_MODEL)
        dlng_ref[...] += jnp.sum(dy * xh, axis=0, keepdims=True)
        dlnb_ref[...] += jnp.sum(dy, axis=0, keepdims=True)
        dxh = dy * lng
        dr = rstd * (dxh - jnp.mean(dxh, axis=1, keepdims=True) - xh * jnp.mean(dxh * xh, axis=1, keepdims=True))
        dxr_ref[...] = DN_ALPHA * dr
        dr_bf = dr.astype(BF16)
        dwout_ref[...] += _dot(merged_bf, dr_bf, _TN)
        dmerged = _dot(dr_bf, wout, _NT)
        dpa_bf = (dmerged * sga).astype(BF16)
        dpb_bf = (dmerged * sgb).astype(BF16)
        _store_grad(dha_ref, dba_ref, 0, dmerged * pa * (sga * (1.0 - sga)))
        _store_grad(dha_ref, dba_ref, D_MODEL, dmerged * pb * (sgb * (1.0 - sgb)))
        dwoa_ref[...] += _dot(ya_bf, dpa_bf, _TN)
        dwob_ref[...] += _dot(yb_bf, dpb_bf, _TN)
        dya = _dot(dpa_bf, woa, _NT)
        dyb_ref[...] = _dot(dpb_bf, wob, _NT)
        do_ref[...] = dya * (za * sa)
        _store_grad(dha_ref, dba_ref, 2 * D_MODEL, dya * o * (sa * (1.0 + za * (1.0 - sa))))

        @pl.when(step == nsteps - 1)
        def _():
            cols = D_MODEL // N_DEV
            for j in range(N_DEV):
                poa_ref[j] = dwoa_ref[:, cols * j:cols * (j + 1)].astype(BF16)
                pob_ref[j] = dwob_ref[:, cols * j:cols * (j + 1)].astype(BF16)
                pout_ref[j] = dwout_ref[cols * j:cols * (j + 1), :].astype(BF16)

    row = lambda w, c=0: pl.BlockSpec((tm, w), lambda i, c=c: (i, c))
    full = lambda shape: pl.BlockSpec(shape, lambda i: (0, 0))
    full3 = lambda shape: pl.BlockSpec(shape, lambda i: (0, 0, 0))
    return pl.pallas_call(
        body, name="merge", grid=(nsteps,),
        in_specs=[row(D_MODEL), row(MLA_WIDTH), row(D_MODEL, 0), row(D_MODEL, 1), row(MLA_WIDTH, 4), row(SGU_WIDTH),
                  row(D_MODEL), full((MLA_WIDTH, D_MODEL)), full((SGU_WIDTH, D_MODEL)), full((D_MODEL, D_MODEL)),
                  full((1, D_MODEL)), full((1, D_MODEL))],
        out_specs=[full((1, LANES)), row(D_MODEL), row(SEG_A), row(MLA_WIDTH), row(SGU_WIDTH),
                   full3((N_DEV, MLA_WIDTH, D_MODEL // N_DEV)), full3((N_DEV, SGU_WIDTH, D_MODEL // N_DEV)),
                   full3((N_DEV, D_MODEL // N_DEV, D_MODEL)), full((1, D_MODEL)), full((1, D_MODEL)), full((1, SEG_A))],
        out_shape=[jax.ShapeDtypeStruct((1, LANES), F32),
                   jax.ShapeDtypeStruct((SEQ, D_MODEL), F32), jax.ShapeDtypeStruct((SEQ, SEG_A), BF16),
                   jax.ShapeDtypeStruct((SEQ, MLA_WIDTH), F32), jax.ShapeDtypeStruct((SEQ, SGU_WIDTH), F32),
                   jax.ShapeDtypeStruct((N_DEV, MLA_WIDTH, D_MODEL // N_DEV), BF16),
                   jax.ShapeDtypeStruct((N_DEV, SGU_WIDTH, D_MODEL // N_DEV), BF16),
                   jax.ShapeDtypeStruct((N_DEV, D_MODEL // N_DEV, D_MODEL), BF16),
                   jax.ShapeDtypeStruct((1, D_MODEL), F32), jax.ShapeDtypeStruct((1, D_MODEL), F32),
                   jax.ShapeDtypeStruct((1, SEG_A), F32)],
        scratch_shapes=[pltpu.VMEM((MLA_WIDTH, D_MODEL), F32), pltpu.VMEM((SGU_WIDTH, D_MODEL), F32),
                        pltpu.VMEM((D_MODEL, D_MODEL), F32)],
        compiler_params=pltpu.CompilerParams(dimension_semantics=("arbitrary",), vmem_limit_bytes=VMEM_BIG),
    )(x, o, h_a, h_a, h_a, y_b, target, w_oa, w_ob, w_out, ln_g, ln_b)


def _mla_bwd(dq, dk, dv, h_c, gq, gkv, wq, wkn, wv, c_t, sa_t, sb_t):
    tm = 256
    hw = MLA_HEADS * HEAD_PAD

    def body(dq_ref, dk_ref, dv_ref, cq_ref, ckv_ref, gq_ref, gkv_ref, wq_ref, wkn_ref, wv_ref, c_ref, sa_ref, sb_ref,
             dhc_ref, puq_ref, dwkn_ref, dwv_ref, dgq_ref, dgkv_ref, dbc_ref, pre_ref, dwq_ref):
        @pl.when(pl.program_id(0) == 0)
        def _():
            for r in (dwq_ref, dwkn_ref, dwv_ref, dgq_ref, dgkv_ref, dbc_ref):
                r[...] = jnp.zeros_like(r)

        c, sa, sb = c_ref[...], sa_ref[...], sb_ref[...]
        lane = lax.broadcasted_iota(jnp.int32, (tm, LANES), 1)
        rope_lanes = jnp.logical_and(lane >= ROPE_LO, lane < ROPE_HI)

        cq = cq_ref[...]
        gq = gq_ref[...]
        rq = lax.rsqrt(jnp.sum(cq * cq, axis=1, keepdims=True) * (1.0 / Q_LORA_RANK) + RMS_EPS)
        nq = cq * rq
        cqn_bf = (nq * gq).astype(BF16)
        for h in range(MLA_HEADS):
            sl = slice(HEAD_PAD * h, HEAD_PAD * (h + 1))
            pre_ref[:, sl] = _rope_t(dq_ref[:, sl] * ATTN_SCALE, c, sa, sb).astype(BF16)
        dqpre_bf = pre_ref[...]
        dcqn = _dot(dqpre_bf, wq_ref[...], _NT)
        dwq_ref[...] += _dot(cqn_bf, dqpre_bf, _TN)
        dgq_ref[...] += jnp.sum(dcqn * nq, axis=0, keepdims=True)
        dnq = dcqn * gq
        _store_grad(dhc_ref, dbc_ref, 0,
                    rq * (dnq - nq * (jnp.sum(dnq * nq, axis=1, keepdims=True) * (1.0 / Q_LORA_RANK))))

        ckv = ckv_ref[...]
        gkv = gkv_ref[...]
        rkv = lax.rsqrt(jnp.sum(ckv * ckv, axis=1, keepdims=True) * (1.0 / KV_LORA_RANK) + RMS_EPS)
        nkv = ckv * rkv
        ckvn_bf = (nkv * gkv).astype(BF16)
        dk = dk_ref[...]
        dk_bf = dk.astype(BF16)
        dv_bf = dv_ref[...].astype(BF16)
        dckvn = _dot(dk_bf, wkn_ref[...], _NT) + _dot(dv_bf, wv_ref[...], _NT)
        dwkn_ref[...] += _dot(ckvn_bf, dk_bf, _TN)
        dwv_ref[...] += _dot(ckvn_bf, dv_bf, _TN)
        dgkv_ref[...] += jnp.sum(dckvn * nkv, axis=0, keepdims=True)
        dnkv = dckvn * gkv
        _store_grad(dhc_ref, dbc_ref, CQ_PAD, rkv * (
            dnkv - nkv * (jnp.sum(dnkv * nkv, axis=1, keepdims=True) * (1.0 / KV_LORA_RANK))))
        dkpe = jnp.zeros((tm, LANES), F32)
        for h in range(MLA_HEADS):
            dkpe = dkpe + dk[:, HEAD_PAD * h:HEAD_PAD * (h + 1)]
        _store_grad(dhc_ref, dbc_ref, CQ_PAD + LANES, _rope_t(jnp.where(rope_lanes, dkpe, 0.0), c, sa, sb))

        @pl.when(pl.program_id(0) == SEQ // tm - 1)
        def _():
            rows = Q_LORA_RANK // N_DEV
            for j in range(N_DEV):
                for h in range(MLA_HEADS):
                    puq_ref[j, :, QK_HEAD_DIM * h:QK_HEAD_DIM * (h + 1)] = dwq_ref[
                        rows * j:rows * (j + 1), HEAD_PAD * h:HEAD_PAD * h + QK_HEAD_DIM].astype(BF16)

    full = lambda shape: pl.BlockSpec(shape, lambda i: (0, 0))
    row = lambda w, c=0: pl.BlockSpec((tm, w), lambda i, c=c: (i, c))
    return pl.pallas_call(
        body, name="mla_bwd", grid=(SEQ // tm,),
        in_specs=[row(hw), row(hw), row(hw), row(CQ_PAD, 0), row(LANES, CQ_PAD // LANES),
                  full((1, CQ_PAD)), full((1, KV_LORA_RANK)), full((CQ_PAD, hw)), full((KV_LORA_RANK, hw)),
                  full((KV_LORA_RANK, hw)), row(LANES), row(LANES), row(LANES)],
        out_specs=[row(SEG_C), pl.BlockSpec((N_DEV, Q_LORA_RANK // N_DEV, MLA_HEADS * QK_HEAD_DIM), lambda i: (0, 0, 0)),
                   full((KV_LORA_RANK, hw)), full((KV_LORA_RANK, hw)),
                   full((1, CQ_PAD)), full((1, KV_LORA_RANK)), full((1, SEG_C))],
        out_shape=[jax.ShapeDtypeStruct((SEQ, SEG_C), BF16),
                   jax.ShapeDtypeStruct((N_DEV, Q_LORA_RANK // N_DEV, MLA_HEADS * QK_HEAD_DIM), BF16),
                   jax.ShapeDtypeStruct((KV_LORA_RANK, hw), F32), jax.ShapeDtypeStruct((KV_LORA_RANK, hw), F32),
                   jax.ShapeDtypeStruct((1, CQ_PAD), F32), jax.ShapeDtypeStruct((1, KV_LORA_RANK), F32),
                   jax.ShapeDtypeStruct((1, SEG_C), F32)],
        scratch_shapes=[pltpu.VMEM((tm, hw), BF16), pltpu.VMEM((CQ_PAD, hw), F32)],
        compiler_params=pltpu.CompilerParams(dimension_semantics=("arbitrary",), vmem_limit_bytes=VMEM_MID),
    )(dq, dk, dv, h_c, h_c, gq, gkv, wq, wkn, wv, c_t, sa_t, sb_t)


def _adamw_all(ws, gs, ms, vs):
    n = len(ws)
    c1 = 1.0 / (1.0 - ADAM_B1 ** ADAM_STEP)
    c2 = 1.0 / (1.0 - ADAM_B2 ** ADAM_STEP)

    def body(*refs):
        for idx in range(n):
            w, g, m, v = (refs[idx][...], refs[n + idx][...], refs[2 * n + idx][...], refs[3 * n + idx][...])
            m_new = ADAM_B1 * m + (1.0 - ADAM_B1) * g
            v_new = ADAM_B2 * v + (1.0 - ADAM_B2) * (g * g)
            delta = -ADAM_LR * ((m_new * c1) / (jnp.sqrt(v_new * c2) + ADAM_EPS) + ADAM_WD * w)
            refs[4 * n + idx][...] = delta
            refs[5 * n + idx][...] = m_new
            refs[6 * n + idx][...] = v_new

    shapes = [jax.ShapeDtypeStruct(w.shape, F32) for w in ws]
    outs = pl.pallas_call(
        body, name="adamw", out_shape=shapes * 3,
        compiler_params=pltpu.CompilerParams(vmem_limit_bytes=VMEM_BIG),
    )(*ws, *gs, *ms, *vs)
    return outs[:n], outs[n:2 * n], outs[2 * n:]


SHARD_W = IN_WIDTH // N_DEV

_PIECES = [(0, 384, 2, 0), (384, 512, 2, CQ_PAD), (512, 544, 2, CQ_PAD + LANES + ROPE_LO),
           (544, 1056, 0, 2 * D_MODEL), (1056, 1568, 1, 0), (1568, 2080, 1, SGU_WIDTH),
           (2080, 2592, 1, 2 * SGU_WIDTH), (2592, 3616, 0, 0), (3616, 4640, 0, D_MODEL)]


def _column_runs():
    runs = []
    for n0, n1, seg, d0 in _PIECES:
        for j in range(N_DEV):
            lo, hi = max(n0, j * SHARD_W), min(n1, (j + 1) * SHARD_W)
            if lo < hi:
                runs.append((j, lo - j * SHARD_W, hi - j * SHARD_W, seg, d0 + lo - n0))
    return runs


def _mesh_pos():
    return lax.axis_index("x"), lax.axis_index("y"), lax.axis_index("c")


def _remote(src, dst, send_sems, recv_sems, k, to):
    return pltpu.make_async_remote_copy(src_ref=src, dst_ref=dst, send_sem=send_sems.at[k], recv_sem=recv_sems.at[k],
                                        device_id=to, device_id_type=pl.DeviceIdType.MESH)


def _gather_exchange(gats, send_sems, recv_sems):
    x, y, c = _mesh_pos()
    me, sibling = (x, y, c), (x, y, 1 - c)
    chips = [(1 - x, y), (x, 1 - y), (1 - x, 1 - y)]

    def copy(a, k, blk, to):
        slab = gats[a].at[4 * blk[0] + 2 * blk[1] + blk[2]]
        return _remote(slab, slab, send_sems, recv_sems, 7 * a + k, to)

    arrays = range(len(gats))
    first = [copy(a, 1 + j, me, (*chip, c)) for j, chip in enumerate(chips) for a in arrays]
    first += [copy(a, 0, me, sibling) for a in arrays]
    for cp in first:
        cp.start()
    passed = []
    for j, chip in enumerate(chips):
        for a in arrays:
            copy(a, 1 + j, (*chip, c), me).wait_recv()
            fwd = copy(a, 4 + j, (*chip, c), sibling)
            fwd.start()
            passed.append(fwd)
    for a in arrays:
        copy(a, 0, sibling, me).wait_recv()
    for j, chip in enumerate(chips):
        for a in arrays:
            copy(a, 4 + j, (*chip, 1 - c), me).wait_recv()
    for cp in first + passed:
        cp.wait_send()


def _gather_weights(w_in, w_uq2, w_oa, w_ob, w_out):
    hw = MLA_HEADS * HEAD_PAD
    uq_rows = Q_LORA_RANK // N_DEV
    cols = D_MODEL // N_DEV

    def body(win_ref, wuq_ref, woa_ref, wob_ref, wout_ref, wa_ref, wb_ref, wc_ref, wq_ref, oa_ref, ob_ref, out_ref,
             g_in, g_uq, g_oa, g_ob, g_out, send_sems, recv_sems):
        x, y, c = _mesh_pos()
        me = 4 * x + 2 * y + c
        g_in[me] = win_ref[0].astype(BF16)
        g_uq[me] = wuq_ref[...].astype(BF16)
        g_oa[me] = woa_ref[0].astype(BF16)
        g_ob[me] = wob_ref[0].astype(BF16)
        g_out[me] = wout_ref[0].astype(BF16)
        _gather_exchange([g_in, g_uq, g_oa, g_ob, g_out], send_sems, recv_sems)

        segs = [wa_ref, wb_ref, wc_ref]
        for j, s0, s1, seg, d0 in _column_runs():
            segs[seg][:, d0:d0 + (s1 - s0)] = g_in[j, :, s0:s1]
        zeros = lambda r, w: jnp.zeros((r, w), BF16)
        wc_ref[:, Q_LORA_RANK:CQ_PAD] = zeros(D_MODEL, CQ_PAD - Q_LORA_RANK)
        wc_ref[:, CQ_PAD + LANES:CQ_PAD + LANES + ROPE_LO] = zeros(D_MODEL, ROPE_LO)
        wc_ref[:, CQ_PAD + LANES + ROPE_HI:SEG_C] = zeros(D_MODEL, LANES - ROPE_HI)
        wq_ref[Q_LORA_RANK:CQ_PAD, :] = zeros(CQ_PAD - Q_LORA_RANK, hw)
        for h in range(MLA_HEADS):
            wq_ref[0:Q_LORA_RANK, HEAD_PAD * h + QK_HEAD_DIM:HEAD_PAD * (h + 1)] = zeros(Q_LORA_RANK, HEAD_PAD - QK_HEAD_DIM)
        for j in range(N_DEV):
            for h in range(MLA_HEADS):
                wq_ref[uq_rows * j:uq_rows * (j + 1), HEAD_PAD * h:HEAD_PAD * h + QK_HEAD_DIM] = g_uq[
                    j, :, QK_HEAD_DIM * h:QK_HEAD_DIM * (h + 1)]
            oa_ref[:, cols * j:cols * (j + 1)] = g_oa[j]
            ob_ref[:, cols * j:cols * (j + 1)] = g_ob[j]
            out_ref[cols * j:cols * (j + 1), :] = g_out[j]

    vmem = pl.BlockSpec(memory_space=pltpu.VMEM)
    return pl.pallas_call(
        body, name="gather_weights",
        out_shape=[jax.ShapeDtypeStruct((D_MODEL, SEG_A), BF16), jax.ShapeDtypeStruct((D_MODEL, SEG_B), BF16),
                   jax.ShapeDtypeStruct((D_MODEL, SEG_C), BF16), jax.ShapeDtypeStruct((CQ_PAD, hw), BF16),
                   jax.ShapeDtypeStruct((MLA_WIDTH, D_MODEL), BF16), jax.ShapeDtypeStruct((SGU_WIDTH, D_MODEL), BF16),
                   jax.ShapeDtypeStruct((D_MODEL, D_MODEL), BF16)],
        in_specs=[vmem] * 5, out_specs=[vmem] * 7,
        scratch_shapes=[pltpu.VMEM((N_DEV, D_MODEL, SHARD_W), BF16),
                        pltpu.VMEM((N_DEV, uq_rows, MLA_HEADS * QK_HEAD_DIM), BF16),
                        pltpu.VMEM((N_DEV, MLA_WIDTH, cols), BF16), pltpu.VMEM((N_DEV, SGU_WIDTH, cols), BF16),
                        pltpu.VMEM((N_DEV, cols, D_MODEL), BF16),
                        pltpu.SemaphoreType.DMA((35,)), pltpu.SemaphoreType.DMA((35,))],
        compiler_params=pltpu.CompilerParams(vmem_limit_bytes=VMEM_BIG),
    )(w_in, w_uq2, w_oa, w_ob, w_out)


C_NAT = 544


def _to_parts(dwa, dwb):
    def body(dwa_ref, dwb_ref, pin_ref):
        pin_ref[0, :, 0:C_NAT] = jnp.zeros((D_MODEL, C_NAT), BF16)
        segs = [dwa_ref, dwb_ref]
        for j, s0, s1, seg, d0 in _column_runs():
            if seg < 2:
                pin_ref[j, :, s0:s1] = segs[seg][:, d0:d0 + (s1 - s0)]

    return pl.pallas_call(body, name="to_parts", out_shape=jax.ShapeDtypeStruct((N_DEV, D_MODEL, SHARD_W), BF16),
                          compiler_params=pltpu.CompilerParams(vmem_limit_bytes=VMEM_MID))(dwa, dwb)


def _sum_landed(lands, c_all, is_dev0):
    r_in, r_oa, r_ob, r_out = lands
    rows = D_MODEL // N_DEV

    def body(rin_ref, roa_ref, rob_ref, rout_ref, call_ref, flag_ref, gin_ref, goa_ref, gob_ref, gout_ref):
        def total(ref, sl):
            acc = ref[0, sl, :].astype(F32)
            for s in range(1, N_DEV):
                acc = acc + ref[s, sl, :].astype(F32)
            return acc

        flag = flag_ref[...]
        for j in range(N_DEV):
            sl = slice(rows * j, rows * (j + 1))
            tot = total(rin_ref, sl)
            gin_ref[0, sl, C_NAT:SHARD_W] = tot[:, C_NAT:SHARD_W]
            gin_ref[0, sl, 0:C_NAT] = tot[:, 0:C_NAT] + flag * call_ref[j].astype(F32)
        goa_ref[0] = total(roa_ref, slice(None))
        gob_ref[0] = total(rob_ref, slice(None))
        gout_ref[0] = total(rout_ref, slice(None))

    return pl.pallas_call(
        body, name="sum_landed",
        out_shape=[jax.ShapeDtypeStruct((1,) + r.shape[1:], F32) for r in lands],
        compiler_params=pltpu.CompilerParams(vmem_limit_bytes=VMEM_MID),
    )(r_in, r_oa, r_ob, r_out, c_all, is_dev0)


def _reduce_grads(dwc, p_uq, p_rep):
    rep_rows = p_rep.shape[1]
    c_rows = D_MODEL // N_DEV
    spec = [((c_rows, C_NAT), BF16, c_rows), (p_uq.shape[1:], BF16, p_uq.shape[1]), ((rep_rows, LANES), F32, rep_rows)]
    n = len(spec)

    def body(dwc_ref, puq_ref, prep_ref, gc_ref, guq_ref, grep_ref, pc_ref, *rest):
        ras, tbs, rbs = rest[0:n], rest[n:2 * n], rest[2 * n:3 * n]
        send_sems, recv_sems = rest[3 * n], rest[3 * n + 1]
        x, y, c = _mesh_pos()
        sibling = (x, y, 1 - c)
        parts = [pc_ref, puq_ref, prep_ref]
        outs = [gc_ref, guq_ref, grep_ref]

        for j, s0, s1, seg, d0 in _column_runs():
            if seg == 2:
                for r in range(N_DEV):
                    pc_ref[r, :, s0:s1] = dwc_ref[c_rows * r:c_rows * (r + 1), d0:d0 + (s1 - s0)]

        stage1 = []
        for chip in range(4):
            for a in range(n):
                cp = _remote(parts[a].at[2 * chip + (1 - c)], ras[a].at[chip], send_sems, recv_sems, 7 * a + chip, sibling)
                cp.start()
                stage1.append(cp)
        for cp in stage1:
            cp.wait_recv()

        def rows_loop(a, fn):
            rows, chunk = spec[a][0][0], spec[a][2]
            if rows == chunk:
                fn(pl.ds(0, rows))
            else:
                def step(i, carry):
                    fn(pl.ds(pl.multiple_of(i * chunk, chunk), chunk))
                    return carry
                lax.fori_loop(0, rows // chunk, step, 0)

        def chip_sum(a, chip, sl):
            return parts[a][2 * chip + c, sl, :].astype(F32) + ras[a][chip, sl, :].astype(F32)

        others = [(1 - x, y), (x, 1 - y), (1 - x, 1 - y)]
        stage2 = []
        for k, (cx, cy) in enumerate(others):
            for a in range(n):
                def fill(sl, a=a, k=k, chip=2 * cx + cy):
                    tbs[a][k, sl, :] = chip_sum(a, chip, sl).astype(spec[a][1])
                rows_loop(a, fill)
                cp = _remote(tbs[a].at[k], rbs[a].at[k], send_sems, recv_sems, 7 * a + 4 + k, (cx, cy, c))
                cp.start()
                stage2.append(cp)
        for cp in stage2:
            cp.wait_recv()
        for a in range(n):
            def final(sl, a=a):
                acc = chip_sum(a, 2 * x + y, sl)
                for k in range(3):
                    acc = acc + rbs[a][k, sl, :].astype(F32)
                outs[a][sl, :] = acc
            rows_loop(a, final)
        for cp in stage1 + stage2:
            cp.wait_send()

    vmem = pl.BlockSpec(memory_space=pltpu.VMEM)
    scratch = [pltpu.VMEM((N_DEV, c_rows, C_NAT), BF16)]
    for lead in (4, 3, 3):
        scratch += [pltpu.VMEM((lead,) + tuple(shape), dt) for shape, dt, _ in spec]
    scratch += [pltpu.SemaphoreType.DMA((7 * n,)), pltpu.SemaphoreType.DMA((7 * n,))]
    return pl.pallas_call(
        body, name="reduce_grads",
        out_shape=[jax.ShapeDtypeStruct(tuple(shape), F32) for shape, _, _ in spec],
        in_specs=[vmem] * 3, out_specs=[vmem] * 3, scratch_shapes=scratch,
        compiler_params=pltpu.CompilerParams(vmem_limit_bytes=VMEM_MID),
    )(dwc, p_uq, p_rep)


def _gather_small(rep_slice, c_slice):
    def body(rep_ref, c_ref, rep_all, c_all, send_sems, recv_sems):
        x, y, c = _mesh_pos()
        me = 4 * x + 2 * y + c
        rep_all[me] = rep_ref[...]
        c_all[me] = c_ref[...].astype(BF16)
        _gather_exchange([rep_all, c_all], send_sems, recv_sems)

    vmem = pl.BlockSpec(memory_space=pltpu.VMEM)
    return pl.pallas_call(
        body, name="gather_small",
        out_shape=[jax.ShapeDtypeStruct((N_DEV,) + rep_slice.shape, F32), jax.ShapeDtypeStruct((N_DEV,) + c_slice.shape, BF16)],
        in_specs=[vmem] * 2, out_specs=[vmem] * 2,
        scratch_shapes=[pltpu.SemaphoreType.DMA((14,)), pltpu.SemaphoreType.DMA((14,))],
    )(rep_slice, c_slice)


_O_CQ, _O_CKV, _O_KPE, _O_ZA, _O_U, _O_V, _O_ZB, _O_GA, _O_GB = 0, 384, 512, 544, 1056, 1568, 2080, 2592, 3616


def _to_segments(w):
    z = lambda n: jnp.zeros(w.shape[:-1] + (n,), w.dtype)
    seg_a = jnp.concatenate([w[..., _O_GA:_O_GB], w[..., _O_GB:IN_WIDTH], w[..., _O_ZA:_O_U]], axis=-1)
    seg_b = jnp.concatenate([w[..., _O_U:_O_V], w[..., _O_V:_O_ZB], w[..., _O_ZB:_O_GA]], axis=-1)
    seg_c = jnp.concatenate([w[..., _O_CQ:_O_CKV], z(CQ_PAD - Q_LORA_RANK), w[..., _O_CKV:_O_KPE],
                             z(ROPE_LO), w[..., _O_KPE:_O_ZA], z(LANES - ROPE_HI)], axis=-1)
    return seg_a, seg_b, seg_c


def _from_segments(seg_a, seg_b, seg_c):
    kpe0 = CQ_PAD + LANES + ROPE_LO
    return jnp.concatenate([
        seg_c[..., 0:Q_LORA_RANK], seg_c[..., CQ_PAD:CQ_PAD + LANES], seg_c[..., kpe0:kpe0 + QK_ROPE_DIM],
        seg_a[..., 2 * D_MODEL:SEG_A], seg_b, seg_a[..., 0:2 * D_MODEL]], axis=-1)


def kernel(x, positions, w_in, b_in, g_q, w_uq, g_kv, w_ukv, w_oa, sgu_ln_g, sgu_ln_b, w_s, b_s, w_ob, w_out, ln_g, ln_b, loss_target, m_w_in, m_b_in, m_g_q, m_w_uq, m_g_kv, m_w_ukv, m_w_oa, m_sgu_ln_g, m_sgu_ln_b, m_w_s, m_b_s, m_w_ob, m_w_out, m_ln_g, m_ln_b, v_w_in, v_b_in, v_g_q, v_w_uq, v_g_kv, v_w_ukv, v_w_oa, v_sgu_ln_g, v_sgu_ln_b, v_w_s, v_b_s, v_w_ob, v_w_out, v_ln_g, v_ln_b):
    w_uq2 = w_uq[0].reshape(Q_LORA_RANK // N_DEV, MLA_HEADS * QK_HEAD_DIM)
    wa, wb, wc, wq, w_oa_f, w_ob_f, w_out_f = _gather_weights(w_in, w_uq2, w_oa, w_ob, w_out)
    partials = _local_step(x[0], positions, loss_target[0], wa, wb, wc, b_in, g_q, wq, g_kv, w_ukv, w_oa_f, sgu_ln_g,
                           sgu_ln_b, w_s, b_s, w_ob_f, w_out_f, ln_g, ln_b)
    weights = dict(w_in=w_in, b_in=b_in, g_q=g_q, w_uq=w_uq, g_kv=g_kv, w_ukv=w_ukv, w_oa=w_oa, sgu_ln_g=sgu_ln_g,
                   sgu_ln_b=sgu_ln_b, w_s=w_s, b_s=b_s, w_ob=w_ob, w_out=w_out, ln_g=ln_g, ln_b=ln_b)
    moms = dict(w_in=m_w_in, b_in=m_b_in, g_q=m_g_q, w_uq=m_w_uq, g_kv=m_g_kv, w_ukv=m_w_ukv, w_oa=m_w_oa,
                sgu_ln_g=m_sgu_ln_g, sgu_ln_b=m_sgu_ln_b, w_s=m_w_s, b_s=m_b_s, w_ob=m_w_ob, w_out=m_w_out,
                ln_g=m_ln_g, ln_b=m_ln_b)
    vars_ = dict(w_in=v_w_in, b_in=v_b_in, g_q=v_g_q, w_uq=v_w_uq, g_kv=v_g_kv, w_ukv=v_w_ukv, w_oa=v_w_oa,
                 sgu_ln_g=v_sgu_ln_g, sgu_ln_b=v_sgu_ln_b, w_s=v_w_s, b_s=v_b_s, w_ob=v_w_ob, w_out=v_w_out,
                 ln_g=v_ln_g, ln_b=v_ln_b)
    return _reduce_and_update(partials, weights, moms, vars_)


def _local_step(x2, positions, tgt, wa, wb, wc, b_in, g_q, wq, g_kv, w_ukv, w_oa_f, sgu_ln_g, sgu_ln_b, w_s, b_s,
                w_ob_f, w_out_f, ln_g, ln_b):
    ba, bb, bc = _to_segments(b_in)
    w_ukv_bf = w_ukv[0].astype(BF16)
    wkn = jnp.pad(w_ukv_bf[:, :, :QK_NOPE_DIM], ((0, 0), (0, 0), (0, HEAD_PAD - QK_NOPE_DIM))).reshape(KV_LORA_RANK, -1)
    wv = jnp.pad(w_ukv_bf[:, :, QK_NOPE_DIM:], ((0, 0), (0, 0), (0, HEAD_PAD - V_HEAD_DIM))).reshape(KV_LORA_RANK, -1)
    gq = jnp.pad(g_q, ((0, 0), (0, CQ_PAD - Q_LORA_RANK)))
    bias_full = jnp.repeat(b_s[0].T, SGU_GROUP_DIM, axis=1)
    w_s3 = w_s[0]
    w_st3 = jnp.swapaxes(w_s3, 1, 2)

    inv_freq = ROPE_THETA ** (-jnp.arange(0, QK_ROPE_DIM, 2, dtype=F32) / QK_ROPE_DIM)
    invf_lane = jnp.concatenate([jnp.zeros((ROPE_LO,), F32), inv_freq, inv_freq,
                                 jnp.zeros((LANES - ROPE_HI,), F32)]).reshape(1, LANES)
    c_t, sa_t, sb_t = _rope_tables(positions.reshape(SEQ, 1), invf_lane)

    x_bf, xt_bf = _cast_x(x2)
    h_a = _mm(x_bf, wa, bias=ba, tm=512, tn=SEG_A // 2, name="in_proj_a")
    h_b = _mm(x_bf, wb, bias=bb, tm=512, tn=SEG_B // 2, name="in_proj_b")
    h_c = _mm(x_bf, wc, bias=bc, tm=512, tn=SEG_C, name="in_proj_c")
    q, k, kt, vx, vxt = _mla_prep(h_c, gq, g_kv, wq, wkn, wv, c_t, sa_t, sb_t)
    o, lse = _attn_fwd(q, kt, vx)
    y_b = _sgu_fwd(h_b, sgu_ln_g, sgu_ln_b, w_s3, bias_full)

    (loss_row, dx_res, dh_a, d_o, d_yb, p_oa, p_ob, p_out, d_lng, d_lnb, d_ba) = _merge(
        x2, o, h_a, y_b, tgt, w_oa_f, w_ob_f, w_out_f, ln_g, ln_b)
    (dh_b, d_ws, d_bs_t, d_slg, d_slb, d_bb), landed_o = _sgu_bwd(h_b, d_yb, sgu_ln_g, sgu_ln_b, w_s3, w_st3, bias_full,
                                                                 (p_oa, p_ob, p_out))
    d_wa = _mm(xt_bf, dh_a, out_dtype=BF16, tm=512, tn=512, name="dw_in_a")
    d_wb = _mm(xt_bf, dh_b, out_dtype=BF16, tm=512, tn=512, name="dw_in_b")
    dq, dk, dv, landed_in = _attn_bwd(q, kt, k, vxt, d_o, o, lse, (_to_parts(d_wa, d_wb),))
    landed = (*landed_in, *landed_o)
    dh_c, p_uq, d_wkn, d_wv, d_gq, d_gkv, d_bc = _mla_bwd(dq, dk, dv, h_c, gq, g_kv, wq, wkn, wv, c_t, sa_t, sb_t)
    d_wc = _mm(xt_bf, dh_c, out_dtype=BF16, tm=512, tn=SEG_C, name="dw_in_c")

    dx = _mm(dh_a, wa, tb=True, add=dx_res, tm=512, tn=D_MODEL, name="dx_a")
    dx = _mm(dh_b, wb, tb=True, add=dx, tm=512, tn=D_MODEL, name="dx_b")
    dx = _mm(dh_c, wc, tb=True, add=dx, tm=512, tn=D_MODEL, name="dx_c")

    p_b_in = _from_segments(d_ba, d_bb, d_bc)
    p_w_ukv = jnp.concatenate([d_wkn.reshape(KV_LORA_RANK, MLA_HEADS, HEAD_PAD)[:, :, :QK_NOPE_DIM],
                               d_wv.reshape(KV_LORA_RANK, MLA_HEADS, HEAD_PAD)[:, :, :V_HEAD_DIM]], axis=-1)
    p_g_q = d_gq[:, :Q_LORA_RANK]
    p_b_s = d_bs_t[:, :SGU_GROUPS].T
    replicated = [p_b_in, p_g_q, d_gkv, p_w_ukv, d_slg, d_slb, d_ws, p_b_s, d_lng, d_lnb]
    return loss_row, dx, landed, d_wc, p_uq, replicated


_NAMES = ["w_in", "b_in", "g_q", "w_uq", "g_kv", "w_ukv", "w_oa", "sgu_ln_g", "sgu_ln_b", "w_s", "b_s", "w_ob",
          "w_out", "ln_g", "ln_b"]
_REPLICATED = ["b_in", "g_q", "g_kv", "w_ukv", "sgu_ln_g", "sgu_ln_b", "w_s", "b_s", "ln_g", "ln_b"]


def _reduce_and_update(partials, weights, moms, vars_):
    loss_row, dx, landed, d_wc, p_uq, replicated = partials
    rep_flat = jnp.concatenate([a.reshape(-1) for a in replicated] + [loss_row[0, :1]])
    rep_flat = jnp.pad(rep_flat, (0, N_DEV * PACK_R_ROWS * LANES - rep_flat.size))
    c_slice, g_uq, rep_slice = _reduce_grads(d_wc, p_uq, rep_flat.reshape(N_DEV, PACK_R_ROWS, LANES))
    rep_all, c_all = _gather_small(rep_slice, c_slice)
    rep_sum = rep_all.reshape(-1)
    device = 4 * lax.axis_index("x") + 2 * lax.axis_index("y") + lax.axis_index("c")
    g_in, g_oa, g_ob, g_out = _sum_landed(landed, c_all, (device == 0).astype(F32).reshape(1, 1))
    grads, pos = dict(w_in=g_in, w_uq=g_uq, w_oa=g_oa, w_ob=g_ob, w_out=g_out), 0
    for nm in _REPLICATED:
        grads[nm] = rep_sum[pos:pos + weights[nm].size]
        pos += weights[nm].size
    loss = rep_sum[pos]
    grads = {nm: grads[nm].reshape(weights[nm].shape) for nm in _NAMES}
    deltas, new_m, new_v = _adamw_all([weights[nm] for nm in _NAMES], [grads[nm] for nm in _NAMES],
                                      [moms[nm] for nm in _NAMES], [vars_[nm] for nm in _NAMES])
    return (loss, dx.reshape(1, SEQ, D_MODEL), *[grads[nm] for nm in _NAMES], *deltas, *new_m, *new_v)
```

```python
import math

import jax
import jax.numpy as jnp
from jax import lax
from jax.experimental import pallas as pl
from jax.experimental.pallas import tpu as pltpu

F32 = jnp.float32
BF16 = jnp.bfloat16

D_MODEL = 1024
SEQ = 2048
N_DEV = 8
MLA_HEADS = 8
Q_LORA_RANK = 384
KV_LORA_RANK = 128
QK_NOPE_DIM = 64
QK_ROPE_DIM = 32
V_HEAD_DIM = 64
QK_HEAD_DIM = QK_NOPE_DIM + QK_ROPE_DIM
MLA_WIDTH = MLA_HEADS * V_HEAD_DIM
ROPE_THETA = 10000.0
SGU_GROUPS = 8
SGU_GROUP_DIM = 64
SGU_WIDTH = SGU_GROUPS * SGU_GROUP_DIM
CHUNK = 128
RMS_EPS = 1e-6
LN_EPS = 1e-5
DN_ALPHA = 2.0 ** 0.25
IN_WIDTH = 4640
ATTN_SCALE = QK_HEAD_DIM ** -0.5

ADAM_LR = 0.001
ADAM_B1 = 0.9
ADAM_B2 = 0.999
ADAM_EPS = 1e-08
ADAM_WD = 0.01
ADAM_STEP = 10

LANES = 128
HEAD_PAD = 128
ROPE_LO = QK_NOPE_DIM
ROPE_MID = ROPE_LO + QK_ROPE_DIM // 2
ROPE_HI = ROPE_LO + QK_ROPE_DIM
CQ_PAD = 512

SEG_A = 2560
SEG_B = 1536
SEG_C = 768

PACK_R_ROWS = 272
VMEM_BIG = 56 * 1024 * 1024
VMEM_MID = 40 * 1024 * 1024


def _sigmoid(x):
    return 1.0 / (1.0 + jnp.exp(-x))


def _gelu_and_grad(x):
    c0 = math.sqrt(2.0 / math.pi)
    x2 = x * x
    t = jnp.tanh(c0 * (x + 0.044715 * x * x2))
    g = 0.5 * x * (1.0 + t)
    dg = 0.5 * (1.0 + t) + 0.5 * x * (1.0 - t * t) * (c0 * (1.0 + 3.0 * 0.044715 * x2))
    return g, dg


def _dot(a, b, dims):
    return lax.dot_general(a, b, (dims, ((), ())), preferred_element_type=F32)


_NN = ((1,), (0,))
_NT = ((1,), (1,))
_TN = ((0,), (0,))


def _store_grad(dh_ref, db_ref, col, val):
    cols = slice(col, col + val.shape[1])
    dh_ref[:, cols] = val.astype(BF16)
    db_ref[:, cols] += jnp.sum(val, axis=0, keepdims=True)


def _mm(a, b, *, tb=False, bias=None, add=None, out_dtype=F32, own=(), tm, tn, name):
    m, k = a.shape
    n = b.shape[0] if tb else b.shape[1]
    assert m % tm == 0 and n % tn == 0
    dims = _NT if tb else _NN
    nown = len(own)
    nm = m // tm
    nsteps = (n // tn) * nm

    def body(*refs):
        a_ref, b_ref = refs[0], refs[1]
        pos = 2
        r = _dot(a_ref[...], b_ref[...], dims)
        if bias is not None:
            r = r + refs[pos][...]; pos += 1
        if add is not None:
            r = r + refs[pos][...]; pos += 1
        own_refs = refs[pos:pos + nown]; pos += nown
        refs[pos][...] = r.astype(out_dtype)
        if nown:
            gat_refs = refs[pos + 1:pos + 1 + nown]
            send_sems, recv_sems, local_sems = refs[pos + 1 + nown:]
            _gather_behind(own_refs, gat_refs, send_sems, recv_sems, local_sems,
                           pl.program_id(0) * nm + pl.program_id(1), nsteps - 2, nsteps - 1)

    b_spec = pl.BlockSpec((tn, k), lambda j, i: (j, 0)) if tb else pl.BlockSpec((k, tn), lambda j, i: (0, j))
    in_specs, args = [pl.BlockSpec((tm, k), lambda j, i: (i, 0)), b_spec], [a, b]
    if bias is not None:
        in_specs.append(pl.BlockSpec((1, tn), lambda j, i: (0, j))); args.append(bias)
    if add is not None:
        in_specs.append(pl.BlockSpec((tm, tn), lambda j, i: (i, j))); args.append(add)
    hbm = pl.BlockSpec(memory_space=pl.ANY)
    res = pl.pallas_call(
        body, name=name, grid=(n // tn, nm), in_specs=in_specs + [hbm] * nown,
        out_specs=[pl.BlockSpec((tm, tn), lambda j, i: (i, j))] + [hbm] * nown,
        out_shape=[jax.ShapeDtypeStruct((m, n), out_dtype)]
        + [jax.ShapeDtypeStruct((N_DEV,) + o.shape, o.dtype) for o in own],
        scratch_shapes=_exchange_sems(nown) if nown else [],
        compiler_params=pltpu.CompilerParams(dimension_semantics=("arbitrary", "arbitrary"), vmem_limit_bytes=VMEM_BIG),
    )(*args, *own)
    return (res[0], res[1:]) if nown else res[0]


def _cast_x(x2):
    tm = 256

    def body(x_ref, xb_ref, xt_ref):
        x = x_ref[...]
        xb_ref[...] = x.astype(BF16)
        xt_ref[...] = x.T.astype(BF16)

    return pl.pallas_call(
        body, name="cast_x", grid=(SEQ // tm,),
        in_specs=[pl.BlockSpec((tm, D_MODEL), lambda i: (i, 0))],
        out_specs=[pl.BlockSpec((tm, D_MODEL), lambda i: (i, 0)), pl.BlockSpec((D_MODEL, tm), lambda i: (0, i))],
        out_shape=[jax.ShapeDtypeStruct((SEQ, D_MODEL), BF16), jax.ShapeDtypeStruct((D_MODEL, SEQ), BF16)],
        compiler_params=pltpu.CompilerParams(dimension_semantics=("arbitrary",)),
    )(x2)


def _rope_tables(pos_col, invf_lane):
    def body(pos_ref, invf_ref, c_ref, sa_ref, sb_ref):
        ang = pos_ref[...].astype(F32) * invf_ref[...]
        cs, sn = jnp.cos(ang), jnp.sin(ang)
        lane = lax.broadcasted_iota(jnp.int32, ang.shape, 1)
        c_ref[...] = jnp.where(lane < ROPE_LO, 1.0, jnp.where(lane < ROPE_HI, cs, 0.0))
        sa_ref[...] = jnp.where(jnp.logical_and(lane >= ROPE_LO, lane < ROPE_MID), -sn, 0.0)
        sb_ref[...] = jnp.where(jnp.logical_and(lane >= ROPE_MID, lane < ROPE_HI), sn, 0.0)

    shp = jax.ShapeDtypeStruct((SEQ, LANES), F32)
    return pl.pallas_call(body, name="rope_tables", out_shape=[shp, shp, shp])(pos_col, invf_lane)


def _rope(x, c, sa, sb):
    return x * c + pltpu.roll(x, LANES - 16, 1) * sa + pltpu.roll(x, 16, 1) * sb


def _rope_t(dy, c, sa, sb):
    return dy * c + pltpu.roll(dy * sa, 16, 1) + pltpu.roll(dy * sb, LANES - 16, 1)


def _mla_prep(h_c, gq, gkv, wq, wkn, wvx, c_t, sa_t, sb_t):
    tm = 256
    hw = MLA_HEADS * HEAD_PAD

    def body(cq_ref, ckv_ref, kpe_ref, gq_ref, gkv_ref, wq_ref, wkn_ref, wvx_ref, c_ref, sa_ref, sb_ref,
             q_ref, k_ref, kt_ref, vx_ref, vxt_ref):
        c, sa, sb = c_ref[...], sa_ref[...], sb_ref[...]
        cq = cq_ref[...]
        rq = lax.rsqrt(jnp.sum(cq * cq, axis=1, keepdims=True) * (1.0 / Q_LORA_RANK) + RMS_EPS)
        cqn = ((cq * rq) * gq_ref[...]).astype(BF16)
        qall = _dot(cqn, wq_ref[...], _NN)
        for h in range(MLA_HEADS):
            sl = slice(HEAD_PAD * h, HEAD_PAD * (h + 1))
            q_ref[:, sl] = (_rope(qall[:, sl], c, sa, sb) * ATTN_SCALE).astype(BF16)
        ckv = ckv_ref[...]
        rkv = lax.rsqrt(jnp.sum(ckv * ckv, axis=1, keepdims=True) * (1.0 / KV_LORA_RANK) + RMS_EPS)
        ckvn = ((ckv * rkv) * gkv_ref[...]).astype(BF16)
        knall = _dot(ckvn, wkn_ref[...], _NN)
        vall = _dot(ckvn, wvx_ref[...], _NN)
        kper = _rope(kpe_ref[...], c, sa, sb)
        ones_half = (lax.broadcasted_iota(jnp.int32, (tm, HEAD_PAD), 1) >= V_HEAD_DIM).astype(F32)
        for h in range(MLA_HEADS):
            sl = slice(HEAD_PAD * h, HEAD_PAD * (h + 1))
            kh = knall[:, sl] + kper
            vh = vall[:, sl] + ones_half
            k_ref[:, sl] = kh.astype(BF16)
            kt_ref[sl, :] = kh.T.astype(BF16)
            vx_ref[:, sl] = vh.astype(BF16)
            vxt_ref[sl, :] = vh.T.astype(BF16)

    full = lambda shape: pl.BlockSpec(shape, lambda i: (0, 0))
    tab = pl.BlockSpec((tm, LANES), lambda i: (i, 0))
    row = pl.BlockSpec((tm, hw), lambda i: (i, 0))
    col = pl.BlockSpec((hw, tm), lambda i: (0, i))
    return pl.pallas_call(
        body, name="mla_prep", grid=(SEQ // tm,),
        in_specs=[pl.BlockSpec((tm, CQ_PAD), lambda i: (i, 0)),
                  pl.BlockSpec((tm, LANES), lambda i: (i, CQ_PAD // LANES)),
                  pl.BlockSpec((tm, LANES), lambda i: (i, CQ_PAD // LANES + 1)),
                  full((1, CQ_PAD)), full((1, KV_LORA_RANK)),
                  full((CQ_PAD, hw)), full((KV_LORA_RANK, hw)), full((KV_LORA_RANK, hw)), tab, tab, tab],
        out_specs=[row, row, col, row, col],
        out_shape=[jax.ShapeDtypeStruct((SEQ, hw), BF16), jax.ShapeDtypeStruct((SEQ, hw), BF16),
                   jax.ShapeDtypeStruct((hw, SEQ), BF16), jax.ShapeDtypeStruct((SEQ, hw), BF16),
                   jax.ShapeDtypeStruct((hw, SEQ), BF16)],
        compiler_params=pltpu.CompilerParams(dimension_semantics=("arbitrary",), vmem_limit_bytes=VMEM_MID),
    )(h_c, h_c, h_c, gq, gkv, wq, wkn, wvx, c_t, sa_t, sb_t)


ATT_T = 512
ATT_STRIP = 64


def _attn_fwd(q, kt, vx, own):
    t, rs = ATT_T, ATT_STRIP
    nown = len(own)
    nq = SEQ // t
    nsteps = (MLA_HEADS // 2) * nq

    def body(q_ref, kt_ref, vx_ref, *rest):
        own_refs, (o_ref, l_ref), gat_refs = rest[:nown], rest[nown:nown + 2], rest[nown + 2:2 * nown + 2]
        s_scr, p_scr, m_scr, a_scr, acc_scr, send_sems, recv_sems, local_sems = rest[2 * nown + 2:]
        qi = pl.program_id(1)
        _gather_behind(own_refs, gat_refs, send_sems, recv_sems, local_sems, pl.program_id(0) * nq + qi,
                       nsteps - 2, nsteps - 1)
        lane = lax.broadcasted_iota(jnp.int32, (t, LANES), 1)
        m_scr[...] = jnp.full((2, t, LANES), -1e30, F32)
        acc_scr[...] = jnp.zeros((2, t, LANES), F32)

        def block(j, masked):
            off = pl.multiple_of(j * t, t)
            for a in range(2):
                sl = slice(HEAD_PAD * a, HEAD_PAD * (a + 1))
                s_scr[a] = _dot(q_ref[:, sl], kt_ref[sl, pl.ds(off, t)], _NN)
                for r in range(t // rs):
                    rows = slice(rs * r, rs * (r + 1))
                    s = s_scr[a, rows, :]
                    if masked:
                        rowi = lax.broadcasted_iota(jnp.int32, (rs, t), 0) + rs * r
                        coli = lax.broadcasted_iota(jnp.int32, (rs, t), 1)
                        s = jnp.where(coli <= rowi, s, -1e30)
                    m_old = m_scr[a, rows, :]
                    m_new = jnp.maximum(m_old, jnp.max(s, axis=1, keepdims=True))
                    p_scr[a, rows, :] = jnp.exp(s - m_new[:, :1]).astype(BF16)
                    a_scr[a, rows, :] = jnp.exp(m_old - m_new)
                    m_scr[a, rows, :] = m_new
                acc_scr[a] = acc_scr[a] * a_scr[a] + _dot(p_scr[a], vx_ref[pl.ds(off, t), sl], _NN)

        def step(j, carry):
            block(j, False)
            return carry
        lax.fori_loop(0, qi, step, 0)
        block(qi, True)
        res = []
        for a in range(2):
            acc = acc_scr[a]
            l = acc[:, V_HEAD_DIM:V_HEAD_DIM + 1]
            res.append((acc / l, m_scr[a] + jnp.log(l)))
        o_ref[...] = jnp.where(lane < V_HEAD_DIM, res[0][0], pltpu.roll(res[1][0], V_HEAD_DIM, 1))
        l_ref[...] = jnp.where(lane < V_HEAD_DIM, res[0][1], res[1][1])

    hbm = pl.BlockSpec(memory_space=pl.ANY)
    res = pl.pallas_call(
        body, name="attn_fwd", grid=(MLA_HEADS // 2, nq),
        in_specs=[pl.BlockSpec((t, 2 * HEAD_PAD), lambda p, i: (i, p)),
                  pl.BlockSpec((2 * HEAD_PAD, SEQ), lambda p, i: (p, 0)),
                  pl.BlockSpec((SEQ, 2 * HEAD_PAD), lambda p, i: (0, p))] + [hbm] * nown,
        out_specs=[pl.BlockSpec((t, LANES), lambda p, i: (i, p)),
                   pl.BlockSpec((t, LANES), lambda p, i: (i, p))] + [hbm] * nown,
        out_shape=[jax.ShapeDtypeStruct((SEQ, MLA_WIDTH), F32), jax.ShapeDtypeStruct((SEQ, MLA_WIDTH), F32)]
        + [jax.ShapeDtypeStruct((N_DEV,) + a.shape, a.dtype) for a in own],
        scratch_shapes=[pltpu.VMEM((2, t, t), F32), pltpu.VMEM((2, t, t), BF16), pltpu.VMEM((2, t, LANES), F32),
                        pltpu.VMEM((2, t, LANES), F32), pltpu.VMEM((2, t, LANES), F32)] + _exchange_sems(nown),
        compiler_params=pltpu.CompilerParams(dimension_semantics=("arbitrary", "arbitrary"), vmem_limit_bytes=VMEM_MID),
    )(q, kt, vx, *own)
    return res[0], res[1], res[2:]


def _exchange_parts(parts, lands, send_sems, recv_sems, local_sems):
    x, y, c = _mesh_pos()
    me = 4 * x + 2 * y + c
    peers = [(x, y, 1 - c), (1 - x, y, c), (x, 1 - y, c), (1 - x, 1 - y, c),
             (1 - x, y, 1 - c), (x, 1 - y, 1 - c), (1 - x, 1 - y, 1 - c)]
    remote, local = [], []
    for a, (part, land) in enumerate(zip(parts, lands)):
        for k, peer in enumerate(peers):
            t = 4 * peer[0] + 2 * peer[1] + peer[2]
            remote.append(_remote(part.at[t], land.at[me], send_sems, recv_sems, 7 * a + k, peer))
        local.append(pltpu.make_async_copy(part.at[me], land.at[me], local_sems.at[a]))
    return remote, local


def _exchange_start(first_step, exchange):
    remote, local = exchange

    @pl.when(first_step)
    def _():
        for cp in remote + local:
            cp.start()


def _exchange_finish(last_step, exchange):
    remote, local = exchange

    @pl.when(last_step)
    def _():
        for cp in remote:
            cp.wait_recv()
        for cp in remote:
            cp.wait_send()
        for cp in local:
            cp.wait()


def _exchange_sems(npart):
    return [pltpu.SemaphoreType.DMA((7 * npart,)), pltpu.SemaphoreType.DMA((7 * npart,)),
            pltpu.SemaphoreType.DMA((npart,))]


def _attn_bwd(q, kt, k, vxt, d_o, o, lse, parts):
    t, rs = ATT_T, ATT_STRIP
    nq = SEQ // t
    npart = len(parts)
    nsteps = MLA_HEADS // 2

    def body(q_ref, kt_ref, k_ref, vxt_ref, do_ref, o_ref, l_ref, *rest):
        part_refs, rest = rest[:npart], rest[npart:]
        dq_ref, dk_ref, dv_ref = rest[:3]
        land_refs, rest = rest[3:3 + npart], rest[3 + npart:]
        s_scr, dp_scr, p_scr, ds_scr, st_scr, send_sems, recv_sems, local_sems = rest
        exchange = _exchange_parts(part_refs, land_refs, send_sems, recv_sems, local_sems)
        _exchange_start(pl.program_id(0) == 0, exchange)
        dk_ref[...] = jnp.zeros_like(dk_ref)
        dv_ref[...] = jnp.zeros_like(dv_ref)
        lane = lax.broadcasted_iota(jnp.int32, (t, LANES), 1)

        def qtile(i, carry):
            ioff = pl.multiple_of(i * t, t)
            do_i = do_ref[pl.ds(ioff, t), :]
            o_i = o_ref[pl.ds(ioff, t), :]
            l_i = l_ref[pl.ds(ioff, t), :]
            for a in range(2):
                sl = slice(HEAD_PAD * a, HEAD_PAD * (a + 1))
                sel = (lane < V_HEAD_DIM) if a == 0 else (lane >= V_HEAD_DIM)
                doa = jnp.where(sel, do_i, 0.0)
                oa = o_i
                if a == 1:
                    doa = pltpu.roll(doa, V_HEAD_DIM, 1)
                    oa = pltpu.roll(o_i, V_HEAD_DIM, 1)
                st_scr[0] = jnp.broadcast_to(jnp.sum(doa * oa, axis=1, keepdims=True), (t, LANES))
                st_scr[1] = jnp.broadcast_to(l_i[:, V_HEAD_DIM * a:V_HEAD_DIM * a + 1], (t, LANES))
                doa_bf = doa.astype(BF16)
                qa = q_ref[pl.ds(ioff, t), sl]

                def block(j, masked, dq_acc, sl=sl, qa=qa, doa_bf=doa_bf):
                    joff = pl.multiple_of(j * t, t)
                    s_scr[...] = _dot(qa, kt_ref[sl, pl.ds(joff, t)], _NN)
                    dp_scr[...] = _dot(doa_bf, vxt_ref[sl, pl.ds(joff, t)], _NN)
                    for r in range(t // rs):
                        rows = slice(rs * r, rs * (r + 1))
                        p = jnp.exp(s_scr[rows, :] - st_scr[1, rows, :1])
                        if masked:
                            rowi = lax.broadcasted_iota(jnp.int32, (rs, t), 0) + rs * r
                            coli = lax.broadcasted_iota(jnp.int32, (rs, t), 1)
                            p = jnp.where(coli <= rowi, p, 0.0)
                        p_scr[rows, :] = p.astype(BF16)
                        ds_scr[rows, :] = (p * (dp_scr[rows, :] - st_scr[0, rows, :1])).astype(BF16)
                    dk_ref[pl.ds(joff, t), sl] += _dot(ds_scr[...], qa, _TN)
                    dv_ref[pl.ds(joff, t), sl] += _dot(p_scr[...], doa_bf, _TN)
                    return dq_acc + _dot(ds_scr[...], k_ref[pl.ds(joff, t), sl], _NN)

                dq_acc = lax.fori_loop(0, i, lambda j, acc: block(j, False, acc), jnp.zeros((t, HEAD_PAD), F32))
                dq_ref[pl.ds(ioff, t), sl] = block(i, True, dq_acc)
            return carry

        lax.fori_loop(0, nq, qtile, 0)
        _exchange_finish(pl.program_id(0) == nsteps - 1, exchange)

    hw = MLA_HEADS * HEAD_PAD
    wide = pl.BlockSpec((SEQ, 2 * HEAD_PAD), lambda p: (0, p))
    wide_t = pl.BlockSpec((2 * HEAD_PAD, SEQ), lambda p: (p, 0))
    narrow = pl.BlockSpec((SEQ, LANES), lambda p: (0, p))
    hbm = pl.BlockSpec(memory_space=pl.ANY)
    res = pl.pallas_call(
        body, name="attn_bwd", grid=(nsteps,),
        in_specs=[wide, wide_t, wide, wide_t, narrow, narrow, narrow] + [hbm] * npart,
        out_specs=[wide, wide, wide] + [hbm] * npart,
        out_shape=[jax.ShapeDtypeStruct((SEQ, hw), F32)] * 3 + [jax.ShapeDtypeStruct(p.shape, p.dtype) for p in parts],
        scratch_shapes=[pltpu.VMEM((t, t), F32), pltpu.VMEM((t, t), F32), pltpu.VMEM((t, t), BF16),
                        pltpu.VMEM((t, t), BF16), pltpu.VMEM((2, t, LANES), F32)] + _exchange_sems(npart),
        compiler_params=pltpu.CompilerParams(dimension_semantics=("arbitrary",), vmem_limit_bytes=VMEM_BIG),
    )(q, kt, k, vxt, d_o, o, lse, *parts)
    return res[0], res[1], res[2], res[3:]


def _sgu_math(u, v, zb, lg, lb, ws_ref, bias):
    ug, dug = _gelu_and_grad(u)
    vg, dvg = _gelu_and_grad(v)
    mu = jnp.mean(vg, axis=1, keepdims=True)
    xc = vg - mu
    rstd = lax.rsqrt(jnp.mean(xc * xc, axis=1, keepdims=True) + LN_EPS)
    xh = xc * rstd
    vn_bf = (xh * lg + lb).astype(BF16)
    grp = lax.broadcasted_iota(jnp.int32, (CHUNK, SGU_WIDTH), 1) // SGU_GROUP_DIM
    r_i = lax.broadcasted_iota(jnp.int32, (CHUNK, CHUNK), 0)
    c_i = lax.broadcasted_iota(jnp.int32, (CHUNK, CHUNK), 1)
    tri, tri_t = r_i >= c_i, r_i <= c_i
    mixed = bias
    for g in range(SGU_GROUPS):
        wt = jnp.where(tri, ws_ref[g], 0.0).astype(BF16)
        mixed = mixed + jnp.where(grp == g, _dot(wt, vn_bf, _NN), 0.0)
    sb = _sigmoid(zb)
    return ug, dug, dvg, rstd, xh, vn_bf, grp, tri, tri_t, mixed, sb


def _sgu_fwd(h_b, lg, lb, w_s, bias_full):
    def body(u_ref, v_ref, zb_ref, lg_ref, lb_ref, ws_ref, bias_ref, yb_ref):
        zb = zb_ref[...]
        ug, _, _, _, _, _, _, _, _, mixed, sb = _sgu_math(u_ref[...], v_ref[...], zb, lg_ref[...], lb_ref[...],
                                                       ws_ref, bias_ref[...])
        yb_ref[...] = (ug * mixed) * (zb * sb)

    blk = lambda c: pl.BlockSpec((CHUNK, SGU_WIDTH), lambda i, c=c: (i, c))
    full2 = lambda shape: pl.BlockSpec(shape, lambda i: (0, 0))
    return pl.pallas_call(
        body, name="sgu_fwd", grid=(SEQ // CHUNK,),
        in_specs=[blk(0), blk(1), blk(2), full2((1, SGU_WIDTH)), full2((1, SGU_WIDTH)),
                  pl.BlockSpec((SGU_GROUPS, CHUNK, CHUNK), lambda i: (0, 0, 0)), full2((CHUNK, SGU_WIDTH))],
        out_specs=pl.BlockSpec((CHUNK, SGU_WIDTH), lambda i: (i, 0)),
        out_shape=jax.ShapeDtypeStruct((SEQ, SGU_WIDTH), F32),
        compiler_params=pltpu.CompilerParams(dimension_semantics=("arbitrary",)),
    )(h_b, h_b, h_b, lg, lb, w_s, bias_full)


def _sgu_bwd(h_b, d_yb, lg, lb, w_s, w_st, bias_full, parts):
    nsteps = SEQ // CHUNK
    npart = len(parts)

    def body(u_ref, v_ref, zb_ref, dyb_ref, lg_ref, lb_ref, ws_ref, wst_ref, bias_ref, *rest):
        part_refs, rest = rest[:npart], rest[npart:]
        dhb_ref, dws_ref, dbs_ref, dlg_ref, dlb_ref, dbb_ref = rest[:6]
        land_refs, (dbias_acc, send_sems, recv_sems, local_sems) = rest[6:6 + npart], rest[6 + npart:]
        step = pl.program_id(0)
        exchange = _exchange_parts(part_refs, land_refs, send_sems, recv_sems, local_sems)
        _exchange_start(step == 0, exchange)

        @pl.when(step == 0)
        def _():
            dbb_ref[...] = jnp.zeros_like(dbb_ref)
            dws_ref[...] = jnp.zeros_like(dws_ref)
            dlg_ref[...] = jnp.zeros_like(dlg_ref)
            dlb_ref[...] = jnp.zeros_like(dlb_ref)
            dbias_acc[...] = jnp.zeros_like(dbias_acc)

        zb = zb_ref[...]
        lg = lg_ref[...]
        ug, dug, dvg, rstd, xh, vn_bf, grp, tri, tri_t, mixed, sb = _sgu_math(
            u_ref[...], v_ref[...], zb, lg, lb_ref[...], ws_ref, bias_ref[...])
        dyb = dyb_ref[...]
        dsgu = dyb * (zb * sb)
        dzb = dyb * (ug * mixed) * (sb * (1.0 + zb * (1.0 - sb)))
        du = dsgu * mixed * dug
        dmixed = dsgu * ug
        dbias_acc[...] += dmixed
        dvn = jnp.zeros((CHUNK, SGU_WIDTH), F32)
        for g in range(SGU_GROUPS):
            dm_g = jnp.where(grp == g, dmixed, 0.0).astype(BF16)
            wtt = jnp.where(tri_t, wst_ref[g], 0.0).astype(BF16)
            dvn = dvn + _dot(wtt, dm_g, _NN)
            dws_ref[g] += jnp.where(tri, _dot(dm_g, vn_bf, _NT), 0.0)
        dlg_ref[...] += jnp.sum(dvn * xh, axis=0, keepdims=True)
        dlb_ref[...] += jnp.sum(dvn, axis=0, keepdims=True)
        dxh = dvn * lg
        dvgel = rstd * (dxh - jnp.mean(dxh, axis=1, keepdims=True) - xh * jnp.mean(dxh * xh, axis=1, keepdims=True))
        _store_grad(dhb_ref, dbb_ref, 0, du)
        _store_grad(dhb_ref, dbb_ref, SGU_WIDTH, dvgel * dvg)
        _store_grad(dhb_ref, dbb_ref, 2 * SGU_WIDTH, dzb)

        @pl.when(step == nsteps - 1)
        def _():
            acc = dbias_acc[...]
            lane = lax.broadcasted_iota(jnp.int32, (CHUNK, LANES), 1)
            out = jnp.zeros((CHUNK, LANES), F32)
            for g in range(SGU_GROUPS):
                sg = jnp.sum(jnp.where(grp == g, acc, 0.0), axis=1, keepdims=True)
                out = jnp.where(lane == g, sg, out)
            dbs_ref[...] = out

        _exchange_finish(step == nsteps - 1, exchange)

    blk = lambda c: pl.BlockSpec((CHUNK, SGU_WIDTH), lambda i, c=c: (i, c))
    full2 = lambda shape: pl.BlockSpec(shape, lambda i: (0, 0))
    full3 = pl.BlockSpec((SGU_GROUPS, CHUNK, CHUNK), lambda i: (0, 0, 0))
    hbm = pl.BlockSpec(memory_space=pl.ANY)
    res = pl.pallas_call(
        body, name="sgu_bwd", grid=(nsteps,),
        in_specs=[blk(0), blk(1), blk(2), pl.BlockSpec((CHUNK, SGU_WIDTH), lambda i: (i, 0)),
                  full2((1, SGU_WIDTH)), full2((1, SGU_WIDTH)), full3, full3, full2((CHUNK, SGU_WIDTH))] + [hbm] * npart,
        out_specs=[pl.BlockSpec((CHUNK, SEG_B), lambda i: (i, 0)), full3, full2((CHUNK, LANES)),
                   full2((1, SGU_WIDTH)), full2((1, SGU_WIDTH)), full2((1, SEG_B))] + [hbm] * npart,
        out_shape=[jax.ShapeDtypeStruct((SEQ, SEG_B), BF16),
                   jax.ShapeDtypeStruct((SGU_GROUPS, CHUNK, CHUNK), F32),
                   jax.ShapeDtypeStruct((CHUNK, LANES), F32),
                   jax.ShapeDtypeStruct((1, SGU_WIDTH), F32), jax.ShapeDtypeStruct((1, SGU_WIDTH), F32),
                   jax.ShapeDtypeStruct((1, SEG_B), F32)] + [jax.ShapeDtypeStruct(p.shape, p.dtype) for p in parts],
        scratch_shapes=[pltpu.VMEM((CHUNK, SGU_WIDTH), F32)] + _exchange_sems(npart),
        compiler_params=pltpu.CompilerParams(dimension_semantics=("arbitrary",)),
    )(h_b, h_b, h_b, d_yb, lg, lb, w_s, w_st, bias_full, *parts)
    return res[:6], res[6:]


def _merge(x, o, h_a, y_b, target, w_oa, w_ob, w_out, ln_g, ln_b):
    tm = 256
    nsteps = SEQ // tm

    def body(x_ref, o_ref, ga_ref, gb_ref, za_ref, yb_ref, tgt_ref, woa_ref, wob_ref, wout_ref, lng_ref, lnb_ref,
             loss_ref, dxr_ref, dha_ref, do_ref, dyb_ref, poa_ref, pob_ref, pout_ref, dlng_ref, dlnb_ref, dba_ref,
             dwoa_ref, dwob_ref, dwout_ref):
        step = pl.program_id(0)

        @pl.when(step == 0)
        def _():
            for r in (loss_ref, dwoa_ref, dwob_ref, dwout_ref, dlng_ref, dlnb_ref, dba_ref):
                r[...] = jnp.zeros_like(r)

        o = o_ref[...]
        za = za_ref[...]
        sa = _sigmoid(za)
        ya_bf = (o * (za * sa)).astype(BF16)
        yb_bf = yb_ref[...].astype(BF16)
        woa, wob, wout = woa_ref[...], wob_ref[...], wout_ref[...]
        pa = _dot(ya_bf, woa, _NN)
        pb = _dot(yb_bf, wob, _NN)
        sga = _sigmoid(ga_ref[...])
        sgb = _sigmoid(gb_ref[...])
        merged_bf = (sga * pa + sgb * pb).astype(BF16)
        r = DN_ALPHA * x_ref[...] + _dot(merged_bf, wout, _NN)
        mu = jnp.mean(r, axis=1, keepdims=True)
        rc = r - mu
        rstd = lax.rsqrt(jnp.mean(rc * rc, axis=1, keepdims=True) + LN_EPS)
        xh = rc * rstd
        lng = lng_ref[...]
        y = xh * lng + lnb_ref[...]
        e = y - tgt_ref[...]
        loss_ref[...] += 0.5 * jnp.sum(jnp.sum(e * e, axis=1, keepdims=True) * (1.0 / D_MODEL), axis=0, keepdims=True)

        dy = e * (1.0 / D_MODEL)
        dlng_ref[...] += jnp.sum(dy * xh, axis=0, keepdims=True)
        dlnb_ref[...] += jnp.sum(dy, axis=0, keepdims=True)
        dxh = dy * lng
        dr = rstd * (dxh - jnp.mean(dxh, axis=1, keepdims=True) - xh * jnp.mean(dxh * xh, axis=1, keepdims=True))
        dxr_ref[...] = DN_ALPHA * dr
        dr_bf = dr.astype(BF16)
        dwout_ref[...] += _dot(merged_bf, dr_bf, _TN)
        dmerged = _dot(dr_bf, wout, _NT)
        dpa_bf = (dmerged * sga).astype(BF16)
        dpb_bf = (dmerged * sgb).astype(BF16)
        _store_grad(dha_ref, dba_ref, 0, dmerged * pa * (sga * (1.0 - sga)))
        _store_grad(dha_ref, dba_ref, D_MODEL, dmerged * pb * (sgb * (1.0 - sgb)))
        dwoa_ref[...] += _dot(ya_bf, dpa_bf, _TN)
        dwob_ref[...] += _dot(yb_bf, dpb_bf, _TN)
        dya = _dot(dpa_bf, woa, _NT)
        dyb_ref[...] = _dot(dpb_bf, wob, _NT)
        do_ref[...] = dya * (za * sa)
        _store_grad(dha_ref, dba_ref, 2 * D_MODEL, dya * o * (sa * (1.0 + za * (1.0 - sa))))

        @pl.when(step == nsteps - 1)
        def _():
            cols = D_MODEL // N_DEV
            for j in range(N_DEV):
                poa_ref[j] = dwoa_ref[:, cols * j:cols * (j + 1)].astype(BF16)
                pob_ref[j] = dwob_ref[:, cols * j:cols * (j + 1)].astype(BF16)
                pout_ref[j] = dwout_ref[cols * j:cols * (j + 1), :].astype(BF16)

    row = lambda w, c=0: pl.BlockSpec((tm, w), lambda i, c=c: (i, c))
    full = lambda shape: pl.BlockSpec(shape, lambda i: (0, 0))
    full3 = lambda shape: pl.BlockSpec(shape, lambda i: (0, 0, 0))
    return pl.pallas_call(
        body, name="merge", grid=(nsteps,),
        in_specs=[row(D_MODEL), row(MLA_WIDTH), row(D_MODEL, 0), row(D_MODEL, 1), row(MLA_WIDTH, 4), row(SGU_WIDTH),
                  row(D_MODEL), full((MLA_WIDTH, D_MODEL)), full((SGU_WIDTH, D_MODEL)), full((D_MODEL, D_MODEL)),
                  full((1, D_MODEL)), full((1, D_MODEL))],
        out_specs=[full((1, LANES)), row(D_MODEL), row(SEG_A), row(MLA_WIDTH), row(SGU_WIDTH),
                   full3((N_DEV, MLA_WIDTH, D_MODEL // N_DEV)), full3((N_DEV, SGU_WIDTH, D_MODEL // N_DEV)),
                   full3((N_DEV, D_MODEL // N_DEV, D_MODEL)), full((1, D_MODEL)), full((1, D_MODEL)), full((1, SEG_A))],
        out_shape=[jax.ShapeDtypeStruct((1, LANES), F32),
                   jax.ShapeDtypeStruct((SEQ, D_MODEL), F32), jax.ShapeDtypeStruct((SEQ, SEG_A), BF16),
                   jax.ShapeDtypeStruct((SEQ, MLA_WIDTH), F32), jax.ShapeDtypeStruct((SEQ, SGU_WIDTH), F32),
                   jax.ShapeDtypeStruct((N_DEV, MLA_WIDTH, D_MODEL // N_DEV), BF16),
                   jax.ShapeDtypeStruct((N_DEV, SGU_WIDTH, D_MODEL // N_DEV), BF16),
                   jax.ShapeDtypeStruct((N_DEV, D_MODEL // N_DEV, D_MODEL), BF16),
                   jax.ShapeDtypeStruct((1, D_MODEL), F32), jax.ShapeDtypeStruct((1, D_MODEL), F32),
                   jax.ShapeDtypeStruct((1, SEG_A), F32)],
        scratch_shapes=[pltpu.VMEM((MLA_WIDTH, D_MODEL), F32), pltpu.VMEM((SGU_WIDTH, D_MODEL), F32),
                        pltpu.VMEM((D_MODEL, D_MODEL), F32)],
        compiler_params=pltpu.CompilerParams(dimension_semantics=("arbitrary",), vmem_limit_bytes=VMEM_BIG),
    )(x, o, h_a, h_a, h_a, y_b, target, w_oa, w_ob, w_out, ln_g, ln_b)


def _mla_bwd(dq, dk, dv, h_c, gq, gkv, wq, wkn, wv, c_t, sa_t, sb_t):
    tm = 256
    hw = MLA_HEADS * HEAD_PAD

    def body(dq_ref, dk_ref, dv_ref, cq_ref, ckv_ref, gq_ref, gkv_ref, wq_ref, wkn_ref, wv_ref, c_ref, sa_ref, sb_ref,
             dhc_ref, puq_ref, dwkn_ref, dwv_ref, dgq_ref, dgkv_ref, dbc_ref, pre_ref, dwq_ref):
        @pl.when(pl.program_id(0) == 0)
        def _():
            for r in (dwq_ref, dwkn_ref, dwv_ref, dgq_ref, dgkv_ref, dbc_ref):
                r[...] = jnp.zeros_like(r)

        c, sa, sb = c_ref[...], sa_ref[...], sb_ref[...]
        lane = lax.broadcasted_iota(jnp.int32, (tm, LANES), 1)
        rope_lanes = jnp.logical_and(lane >= ROPE_LO, lane < ROPE_HI)

        cq = cq_ref[...]
        gq = gq_ref[...]
        rq = lax.rsqrt(jnp.sum(cq * cq, axis=1, keepdims=True) * (1.0 / Q_LORA_RANK) + RMS_EPS)
        nq = cq * rq
        cqn_bf = (nq * gq).astype(BF16)
        for h in range(MLA_HEADS):
            sl = slice(HEAD_PAD * h, HEAD_PAD * (h + 1))
            pre_ref[:, sl] = _rope_t(dq_ref[:, sl] * ATTN_SCALE, c, sa, sb).astype(BF16)
        dqpre_bf = pre_ref[...]
        dcqn = _dot(dqpre_bf, wq_ref[...], _NT)
        dwq_ref[...] += _dot(cqn_bf, dqpre_bf, _TN)
        dgq_ref[...] += jnp.sum(dcqn * nq, axis=0, keepdims=True)
        dnq = dcqn * gq
        _store_grad(dhc_ref, dbc_ref, 0,
                    rq * (dnq - nq * (jnp.sum(dnq * nq, axis=1, keepdims=True) * (1.0 / Q_LORA_RANK))))

        ckv = ckv_ref[...]
        gkv = gkv_ref[...]
        rkv = lax.rsqrt(jnp.sum(ckv * ckv, axis=1, keepdims=True) * (1.0 / KV_LORA_RANK) + RMS_EPS)
        nkv = ckv * rkv
        ckvn_bf = (nkv * gkv).astype(BF16)
        dk = dk_ref[...]
        dk_bf = dk.astype(BF16)
        dv_bf = dv_ref[...].astype(BF16)
        dckvn = _dot(dk_bf, wkn_ref[...], _NT) + _dot(dv_bf, wv_ref[...], _NT)
        dwkn_ref[...] += _dot(ckvn_bf, dk_bf, _TN)
        dwv_ref[...] += _dot(ckvn_bf, dv_bf, _TN)
        dgkv_ref[...] += jnp.sum(dckvn * nkv, axis=0, keepdims=True)
        dnkv = dckvn * gkv
        _store_grad(dhc_ref, dbc_ref, CQ_PAD, rkv * (
            dnkv - nkv * (jnp.sum(dnkv * nkv, axis=1, keepdims=True) * (1.0 / KV_LORA_RANK))))
        dkpe = jnp.zeros((tm, LANES), F32)
        for h in range(MLA_HEADS):
            dkpe = dkpe + dk[:, HEAD_PAD * h:HEAD_PAD * (h + 1)]
        _store_grad(dhc_ref, dbc_ref, CQ_PAD + LANES, _rope_t(jnp.where(rope_lanes, dkpe, 0.0), c, sa, sb))

        @pl.when(pl.program_id(0) == SEQ // tm - 1)
        def _():
            rows = Q_LORA_RANK // N_DEV
            for j in range(N_DEV):
                for h in range(MLA_HEADS):
                    puq_ref[j, :, QK_HEAD_DIM * h:QK_HEAD_DIM * (h + 1)] = dwq_ref[
                        rows * j:rows * (j + 1), HEAD_PAD * h:HEAD_PAD * h + QK_HEAD_DIM].astype(BF16)

    full = lambda shape: pl.BlockSpec(shape, lambda i: (0, 0))
    row = lambda w, c=0: pl.BlockSpec((tm, w), lambda i, c=c: (i, c))
    return pl.pallas_call(
        body, name="mla_bwd", grid=(SEQ // tm,),
        in_specs=[row(hw), row(hw), row(hw), row(CQ_PAD, 0), row(LANES, CQ_PAD // LANES),
                  full((1, CQ_PAD)), full((1, KV_LORA_RANK)), full((CQ_PAD, hw)), full((KV_LORA_RANK, hw)),
                  full((KV_LORA_RANK, hw)), row(LANES), row(LANES), row(LANES)],
        out_specs=[row(SEG_C), pl.BlockSpec((N_DEV, Q_LORA_RANK // N_DEV, MLA_HEADS * QK_HEAD_DIM), lambda i: (0, 0, 0)),
                   full((KV_LORA_RANK, hw)), full((KV_LORA_RANK, hw)),
                   full((1, CQ_PAD)), full((1, KV_LORA_RANK)), full((1, SEG_C))],
        out_shape=[jax.ShapeDtypeStruct((SEQ, SEG_C), BF16),
                   jax.ShapeDtypeStruct((N_DEV, Q_LORA_RANK // N_DEV, MLA_HEADS * QK_HEAD_DIM), BF16),
                   jax.ShapeDtypeStruct((KV_LORA_RANK, hw), F32), jax.ShapeDtypeStruct((KV_LORA_RANK, hw), F32),
                   jax.ShapeDtypeStruct((1, CQ_PAD), F32), jax.ShapeDtypeStruct((1, KV_LORA_RANK), F32),
                   jax.ShapeDtypeStruct((1, SEG_C), F32)],
        scratch_shapes=[pltpu.VMEM((tm, hw), BF16), pltpu.VMEM((CQ_PAD, hw), F32)],
        compiler_params=pltpu.CompilerParams(dimension_semantics=("arbitrary",), vmem_limit_bytes=VMEM_MID),
    )(dq, dk, dv, h_c, h_c, gq, gkv, wq, wkn, wv, c_t, sa_t, sb_t)


def _adamw_all(ws, gs, ms, vs):
    n = len(ws)
    c1 = 1.0 / (1.0 - ADAM_B1 ** ADAM_STEP)
    c2 = 1.0 / (1.0 - ADAM_B2 ** ADAM_STEP)

    def body(*refs):
        for idx in range(n):
            w, g, m, v = (refs[idx][...], refs[n + idx][...], refs[2 * n + idx][...], refs[3 * n + idx][...])
            m_new = ADAM_B1 * m + (1.0 - ADAM_B1) * g
            v_new = ADAM_B2 * v + (1.0 - ADAM_B2) * (g * g)
            delta = -ADAM_LR * ((m_new * c1) / (jnp.sqrt(v_new * c2) + ADAM_EPS) + ADAM_WD * w)
            refs[4 * n + idx][...] = delta
            refs[5 * n + idx][...] = m_new
            refs[6 * n + idx][...] = v_new

    shapes = [jax.ShapeDtypeStruct(w.shape, F32) for w in ws]
    outs = pl.pallas_call(
        body, name="adamw", out_shape=shapes * 3,
        compiler_params=pltpu.CompilerParams(vmem_limit_bytes=VMEM_BIG),
    )(*ws, *gs, *ms, *vs)
    return outs[:n], outs[n:2 * n], outs[2 * n:]


SHARD_W = IN_WIDTH // N_DEV

_PIECES = [(0, 384, 2, 0), (384, 512, 2, CQ_PAD), (512, 544, 2, CQ_PAD + LANES + ROPE_LO),
           (544, 1056, 0, 2 * D_MODEL), (1056, 1568, 1, 0), (1568, 2080, 1, SGU_WIDTH),
           (2080, 2592, 1, 2 * SGU_WIDTH), (2592, 3616, 0, 0), (3616, 4640, 0, D_MODEL)]


def _column_runs():
    runs = []
    for n0, n1, seg, d0 in _PIECES:
        for j in range(N_DEV):
            lo, hi = max(n0, j * SHARD_W), min(n1, (j + 1) * SHARD_W)
            if lo < hi:
                runs.append((j, lo - j * SHARD_W, hi - j * SHARD_W, seg, d0 + lo - n0))
    return runs


def _mesh_pos():
    return lax.axis_index("x"), lax.axis_index("y"), lax.axis_index("c")


def _remote(src, dst, send_sems, recv_sems, k, to):
    return pltpu.make_async_remote_copy(src_ref=src, dst_ref=dst, send_sem=send_sems.at[k], recv_sem=recv_sems.at[k],
                                        device_id=to, device_id_type=pl.DeviceIdType.MESH)


def _gather_exchange(gats, send_sems, recv_sems):
    x, y, c = _mesh_pos()
    me, sibling = (x, y, c), (x, y, 1 - c)
    chips = [(1 - x, y), (x, 1 - y), (1 - x, 1 - y)]

    def copy(a, k, blk, to):
        slab = gats[a].at[4 * blk[0] + 2 * blk[1] + blk[2]]
        return _remote(slab, slab, send_sems, recv_sems, 7 * a + k, to)

    arrays = range(len(gats))
    first = [copy(a, 1 + j, me, (*chip, c)) for j, chip in enumerate(chips) for a in arrays]
    first += [copy(a, 0, me, sibling) for a in arrays]
    for cp in first:
        cp.start()
    passed = []
    for j, chip in enumerate(chips):
        for a in arrays:
            copy(a, 1 + j, (*chip, c), me).wait_recv()
            fwd = copy(a, 4 + j, (*chip, c), sibling)
            fwd.start()
            passed.append(fwd)
    for a in arrays:
        copy(a, 0, sibling, me).wait_recv()
    for j, chip in enumerate(chips):
        for a in arrays:
            copy(a, 4 + j, (*chip, 1 - c), me).wait_recv()
    for cp in first + passed:
        cp.wait_send()


def _gather_behind(own, gats, send_sems, recv_sems, local_sems, step, mid, last):
    x, y, c = _mesh_pos()
    me, sibling = (x, y, c), (x, y, 1 - c)
    chips = [(1 - x, y), (x, 1 - y), (1 - x, 1 - y)]
    arrays = range(len(gats))

    def copy(a, k, blk, to, src=None):
        slab = gats[a].at[4 * blk[0] + 2 * blk[1] + blk[2]]
        return _remote(slab if src is None else src, slab, send_sems, recv_sems, 7 * a + k, to)

    first = [copy(a, 1 + j, me, (*chip, c), src=own[a]) for j, chip in enumerate(chips) for a in arrays]
    first += [copy(a, 0, me, sibling, src=own[a]) for a in arrays]
    local = [pltpu.make_async_copy(own[a], gats[a].at[4 * x + 2 * y + c], local_sems.at[a]) for a in arrays]
    passed = [copy(a, 4 + j, (*chip, c), sibling) for j, chip in enumerate(chips) for a in arrays]

    @pl.when(step == 0)
    def _():
        for cp in first + local:
            cp.start()

    @pl.when(step == mid)
    def _():
        for j, chip in enumerate(chips):
            for a in arrays:
                copy(a, 1 + j, (*chip, c), me).wait_recv()
        for cp in passed:
            cp.start()

    @pl.when(step == last)
    def _():
        for a in arrays:
            copy(a, 0, sibling, me).wait_recv()
        for j, chip in enumerate(chips):
            for a in arrays:
                copy(a, 4 + j, (*chip, 1 - c), me).wait_recv()
        for cp in first + passed:
            cp.wait_send()
        for cp in local:
            cp.wait()


def _gather_first(w_in, w_uq2, w_oa, w_ob, w_out):
    hw = MLA_HEADS * HEAD_PAD
    uq_rows = Q_LORA_RANK // N_DEV

    def body(win_ref, wuq_ref, woa_ref, wob_ref, wout_ref, wc_ref, wq_ref, winb_ref, oab_ref, obb_ref, outb_ref,
             g_uq, blk0, send_sems, recv_sems):
        x, y, c = _mesh_pos()
        me = (x, y, c)
        winb_ref[...] = win_ref[0].astype(BF16)
        oab_ref[...] = woa_ref[0].astype(BF16)
        obb_ref[...] = wob_ref[0].astype(BF16)
        outb_ref[...] = wout_ref[0].astype(BF16)
        g_uq[4 * x + 2 * y + c] = wuq_ref[...].astype(BF16)

        chip0 = jnp.logical_and(x == 0, y == 0)
        dev0 = jnp.logical_and(chip0, c == 0)
        north = c == 1
        targets = [(0, 0, 1), (1, 0, 0), (0, 1, 0), (1, 1, 0)]

        def bcopy(k, to):
            return _remote(blk0, blk0, send_sems, recv_sems, 7 + k, to)

        @pl.when(dev0)
        def _():
            blk0[...] = winb_ref[...]
            for k, to in enumerate(targets):
                bcopy(k, to).start()

        _gather_exchange([g_uq], send_sems, recv_sems)

        for k, (cx, cy, _) in enumerate(targets[1:], start=1):
            @pl.when(jnp.logical_and(jnp.logical_and(x == cx, y == cy), c == 0))
            def _(k=k, cx=cx, cy=cy):
                bcopy(k, me).wait_recv()
                onward = bcopy(4, (cx, cy, 1))
                onward.start()
                onward.wait_send()

        @pl.when(jnp.logical_and(chip0, north))
        def _():
            bcopy(0, me).wait_recv()

        @pl.when(jnp.logical_and(jnp.logical_not(chip0), north))
        def _():
            bcopy(4, me).wait_recv()

        @pl.when(dev0)
        def _():
            for k, to in enumerate(targets):
                bcopy(k, to).wait_send()

        for j, s0, s1, seg, d0 in _column_runs():
            if seg == 2:
                wc_ref[:, d0:d0 + (s1 - s0)] = blk0[:, s0:s1]
        zeros = lambda r, w: jnp.zeros((r, w), BF16)
        wc_ref[:, Q_LORA_RANK:CQ_PAD] = zeros(D_MODEL, CQ_PAD - Q_LORA_RANK)
        wc_ref[:, CQ_PAD + LANES:CQ_PAD + LANES + ROPE_LO] = zeros(D_MODEL, ROPE_LO)
        wc_ref[:, CQ_PAD + LANES + ROPE_HI:SEG_C] = zeros(D_MODEL, LANES - ROPE_HI)
        wq_ref[Q_LORA_RANK:CQ_PAD, :] = zeros(CQ_PAD - Q_LORA_RANK, hw)
        for h in range(MLA_HEADS):
            wq_ref[0:Q_LORA_RANK, HEAD_PAD * h + QK_HEAD_DIM:HEAD_PAD * (h + 1)] = zeros(Q_LORA_RANK, HEAD_PAD - QK_HEAD_DIM)
        for j in range(N_DEV):
            for h in range(MLA_HEADS):
                wq_ref[uq_rows * j:uq_rows * (j + 1), HEAD_PAD * h:HEAD_PAD * h + QK_HEAD_DIM] = g_uq[
                    j, :, QK_HEAD_DIM * h:QK_HEAD_DIM * (h + 1)]

    vmem = pl.BlockSpec(memory_space=pltpu.VMEM)
    return pl.pallas_call(
        body, name="gather_first",
        out_shape=[jax.ShapeDtypeStruct((D_MODEL, SEG_C), BF16), jax.ShapeDtypeStruct((CQ_PAD, hw), BF16),
                   jax.ShapeDtypeStruct(w_in.shape[1:], BF16), jax.ShapeDtypeStruct(w_oa.shape[1:], BF16),
                   jax.ShapeDtypeStruct(w_ob.shape[1:], BF16), jax.ShapeDtypeStruct(w_out.shape[1:], BF16)],
        in_specs=[vmem] * 5, out_specs=[vmem] * 6,
        scratch_shapes=[pltpu.VMEM((N_DEV, uq_rows, MLA_HEADS * QK_HEAD_DIM), BF16), pltpu.VMEM((D_MODEL, SHARD_W), BF16),
                        pltpu.SemaphoreType.DMA((12,)), pltpu.SemaphoreType.DMA((12,))],
        compiler_params=pltpu.CompilerParams(vmem_limit_bytes=VMEM_MID),
    )(w_in, w_uq2, w_oa, w_ob, w_out)


def _assemble_in(g_in):
    def body(g_ref, wa_ref, wb_ref):
        segs = [wa_ref, wb_ref]
        for j, s0, s1, seg, d0 in _column_runs():
            if seg < 2:
                segs[seg][:, d0:d0 + (s1 - s0)] = g_ref[j, :, s0:s1]

    return pl.pallas_call(
        body, name="assemble_in",
        out_shape=[jax.ShapeDtypeStruct((D_MODEL, SEG_A), BF16), jax.ShapeDtypeStruct((D_MODEL, SEG_B), BF16)],
        compiler_params=pltpu.CompilerParams(vmem_limit_bytes=VMEM_MID),
    )(g_in)


def _assemble_out(g_oa, g_ob, g_out):
    cols = D_MODEL // N_DEV

    def body(goa_ref, gob_ref, gout_ref, oa_ref, ob_ref, out_ref):
        for j in range(N_DEV):
            oa_ref[:, cols * j:cols * (j + 1)] = goa_ref[j]
            ob_ref[:, cols * j:cols * (j + 1)] = gob_ref[j]
            out_ref[cols * j:cols * (j + 1), :] = gout_ref[j]

    return pl.pallas_call(
        body, name="assemble_out",
        out_shape=[jax.ShapeDtypeStruct((MLA_WIDTH, D_MODEL), BF16), jax.ShapeDtypeStruct((SGU_WIDTH, D_MODEL), BF16),
                   jax.ShapeDtypeStruct((D_MODEL, D_MODEL), BF16)],
    )(g_oa, g_ob, g_out)


C_NAT = 544


def _to_parts(dwa, dwb):
    def body(dwa_ref, dwb_ref, pin_ref):
        pin_ref[0, :, 0:C_NAT] = jnp.zeros((D_MODEL, C_NAT), BF16)
        segs = [dwa_ref, dwb_ref]
        for j, s0, s1, seg, d0 in _column_runs():
            if seg < 2:
                pin_ref[j, :, s0:s1] = segs[seg][:, d0:d0 + (s1 - s0)]

    return pl.pallas_call(body, name="to_parts", out_shape=jax.ShapeDtypeStruct((N_DEV, D_MODEL, SHARD_W), BF16),
                          compiler_params=pltpu.CompilerParams(vmem_limit_bytes=VMEM_MID))(dwa, dwb)


def _sum_landed(lands, c_all, is_dev0):
    r_in, r_oa, r_ob, r_out = lands
    rows = D_MODEL // N_DEV

    def body(rin_ref, roa_ref, rob_ref, rout_ref, call_ref, flag_ref, gin_ref, goa_ref, gob_ref, gout_ref):
        def total(ref, sl):
            acc = ref[0, sl, :].astype(F32)
            for s in range(1, N_DEV):
                acc = acc + ref[s, sl, :].astype(F32)
            return acc

        flag = flag_ref[...]
        for j in range(N_DEV):
            sl = slice(rows * j, rows * (j + 1))
            tot = total(rin_ref, sl)
            gin_ref[0, sl, C_NAT:SHARD_W] = tot[:, C_NAT:SHARD_W]
            gin_ref[0, sl, 0:C_NAT] = tot[:, 0:C_NAT] + flag * call_ref[j].astype(F32)
        goa_ref[0] = total(roa_ref, slice(None))
        gob_ref[0] = total(rob_ref, slice(None))
        gout_ref[0] = total(rout_ref, slice(None))

    return pl.pallas_call(
        body, name="sum_landed",
        out_shape=[jax.ShapeDtypeStruct((1,) + r.shape[1:], F32) for r in lands],
        compiler_params=pltpu.CompilerParams(vmem_limit_bytes=VMEM_MID),
    )(r_in, r_oa, r_ob, r_out, c_all, is_dev0)


def _reduce_grads(dwc, p_uq, p_rep):
    rep_rows = p_rep.shape[1]
    c_rows = D_MODEL // N_DEV
    spec = [((c_rows, C_NAT), BF16, c_rows), (p_uq.shape[1:], BF16, p_uq.shape[1]), ((rep_rows, LANES), F32, rep_rows)]
    n = len(spec)

    def body(dwc_ref, puq_ref, prep_ref, gc_ref, guq_ref, grep_ref, pc_ref, *rest):
        ras, tbs, rbs = rest[0:n], rest[n:2 * n], rest[2 * n:3 * n]
        send_sems, recv_sems = rest[3 * n], rest[3 * n + 1]
        x, y, c = _mesh_pos()
        sibling = (x, y, 1 - c)
        parts = [pc_ref, puq_ref, prep_ref]
        outs = [gc_ref, guq_ref, grep_ref]

        for j, s0, s1, seg, d0 in _column_runs():
            if seg == 2:
                for r in range(N_DEV):
                    pc_ref[r, :, s0:s1] = dwc_ref[c_rows * r:c_rows * (r + 1), d0:d0 + (s1 - s0)]

        stage1 = []
        for chip in range(4):
            for a in range(n):
                cp = _remote(parts[a].at[2 * chip + (1 - c)], ras[a].at[chip], send_sems, recv_sems, 7 * a + chip, sibling)
                cp.start()
                stage1.append(cp)
        for cp in stage1:
            cp.wait_recv()

        def rows_loop(a, fn):
            rows, chunk = spec[a][0][0], spec[a][2]
            if rows == chunk:
                fn(pl.ds(0, rows))
            else:
                def step(i, carry):
                    fn(pl.ds(pl.multiple_of(i * chunk, chunk), chunk))
                    return carry
                lax.fori_loop(0, rows // chunk, step, 0)

        def chip_sum(a, chip, sl):
            return parts[a][2 * chip + c, sl, :].astype(F32) + ras[a][chip, sl, :].astype(F32)

        others = [(1 - x, y), (x, 1 - y), (1 - x, 1 - y)]
        stage2 = []
        for k, (cx, cy) in enumerate(others):
            for a in range(n):
                def fill(sl, a=a, k=k, chip=2 * cx + cy):
                    tbs[a][k, sl, :] = chip_sum(a, chip, sl).astype(spec[a][1])
                rows_loop(a, fill)
                cp = _remote(tbs[a].at[k], rbs[a].at[k], send_sems, recv_sems, 7 * a + 4 + k, (cx, cy, c))
                cp.start()
                stage2.append(cp)
        for cp in stage2:
            cp.wait_recv()
        for a in range(n):
            def final(sl, a=a):
                acc = chip_sum(a, 2 * x + y, sl)
                for k in range(3):
                    acc = acc + rbs[a][k, sl, :].astype(F32)
                outs[a][sl, :] = acc
            rows_loop(a, final)
        for cp in stage1 + stage2:
            cp.wait_send()

    vmem = pl.BlockSpec(memory_space=pltpu.VMEM)
    scratch = [pltpu.VMEM((N_DEV, c_rows, C_NAT), BF16)]
    for lead in (4, 3, 3):
        scratch += [pltpu.VMEM((lead,) + tuple(shape), dt) for shape, dt, _ in spec]
    scratch += [pltpu.SemaphoreType.DMA((7 * n,)), pltpu.SemaphoreType.DMA((7 * n,))]
    return pl.pallas_call(
        body, name="reduce_grads",
        out_shape=[jax.ShapeDtypeStruct(tuple(shape), F32) for shape, _, _ in spec],
        in_specs=[vmem] * 3, out_specs=[vmem] * 3, scratch_shapes=scratch,
        compiler_params=pltpu.CompilerParams(vmem_limit_bytes=VMEM_MID),
    )(dwc, p_uq, p_rep)


def _gather_small(rep_slice, c_slice):
    def body(rep_ref, c_ref, rep_all, c_all, send_sems, recv_sems):
        x, y, c = _mesh_pos()
        me = 4 * x + 2 * y + c
        rep_all[me] = rep_ref[...]
        c_all[me] = c_ref[...].astype(BF16)
        _gather_exchange([rep_all, c_all], send_sems, recv_sems)

    vmem = pl.BlockSpec(memory_space=pltpu.VMEM)
    return pl.pallas_call(
        body, name="gather_small",
        out_shape=[jax.ShapeDtypeStruct((N_DEV,) + rep_slice.shape, F32), jax.ShapeDtypeStruct((N_DEV,) + c_slice.shape, BF16)],
        in_specs=[vmem] * 2, out_specs=[vmem] * 2,
        scratch_shapes=[pltpu.SemaphoreType.DMA((14,)), pltpu.SemaphoreType.DMA((14,))],
    )(rep_slice, c_slice)


_O_CQ, _O_CKV, _O_KPE, _O_ZA, _O_U, _O_V, _O_ZB, _O_GA, _O_GB = 0, 384, 512, 544, 1056, 1568, 2080, 2592, 3616


def _to_segments(w):
    z = lambda n: jnp.zeros(w.shape[:-1] + (n,), w.dtype)
    seg_a = jnp.concatenate([w[..., _O_GA:_O_GB], w[..., _O_GB:IN_WIDTH], w[..., _O_ZA:_O_U]], axis=-1)
    seg_b = jnp.concatenate([w[..., _O_U:_O_V], w[..., _O_V:_O_ZB], w[..., _O_ZB:_O_GA]], axis=-1)
    seg_c = jnp.concatenate([w[..., _O_CQ:_O_CKV], z(CQ_PAD - Q_LORA_RANK), w[..., _O_CKV:_O_KPE],
                             z(ROPE_LO), w[..., _O_KPE:_O_ZA], z(LANES - ROPE_HI)], axis=-1)
    return seg_a, seg_b, seg_c


def _from_segments(seg_a, seg_b, seg_c):
    kpe0 = CQ_PAD + LANES + ROPE_LO
    return jnp.concatenate([
        seg_c[..., 0:Q_LORA_RANK], seg_c[..., CQ_PAD:CQ_PAD + LANES], seg_c[..., kpe0:kpe0 + QK_ROPE_DIM],
        seg_a[..., 2 * D_MODEL:SEG_A], seg_b, seg_a[..., 0:2 * D_MODEL]], axis=-1)


def kernel(x, positions, w_in, b_in, g_q, w_uq, g_kv, w_ukv, w_oa, sgu_ln_g, sgu_ln_b, w_s, b_s, w_ob, w_out, ln_g, ln_b, loss_target, m_w_in, m_b_in, m_g_q, m_w_uq, m_g_kv, m_w_ukv, m_w_oa, m_sgu_ln_g, m_sgu_ln_b, m_w_s, m_b_s, m_w_ob, m_w_out, m_ln_g, m_ln_b, v_w_in, v_b_in, v_g_q, v_w_uq, v_g_kv, v_w_ukv, v_w_oa, v_sgu_ln_g, v_sgu_ln_b, v_w_s, v_b_s, v_w_ob, v_w_out, v_ln_g, v_ln_b):
    w_uq2 = w_uq[0].reshape(Q_LORA_RANK // N_DEV, MLA_HEADS * QK_HEAD_DIM)
    wc, wq, *own_shards = _gather_first(w_in, w_uq2, w_oa, w_ob, w_out)
    partials = _local_step(x[0], positions, loss_target[0], wc, wq, own_shards, b_in, g_q, g_kv, w_ukv, sgu_ln_g,
                           sgu_ln_b, w_s, b_s, ln_g, ln_b)
    weights = dict(w_in=w_in, b_in=b_in, g_q=g_q, w_uq=w_uq, g_kv=g_kv, w_ukv=w_ukv, w_oa=w_oa, sgu_ln_g=sgu_ln_g,
                   sgu_ln_b=sgu_ln_b, w_s=w_s, b_s=b_s, w_ob=w_ob, w_out=w_out, ln_g=ln_g, ln_b=ln_b)
    moms = dict(w_in=m_w_in, b_in=m_b_in, g_q=m_g_q, w_uq=m_w_uq, g_kv=m_g_kv, w_ukv=m_w_ukv, w_oa=m_w_oa,
                sgu_ln_g=m_sgu_ln_g, sgu_ln_b=m_sgu_ln_b, w_s=m_w_s, b_s=m_b_s, w_ob=m_w_ob, w_out=m_w_out,
                ln_g=m_ln_g, ln_b=m_ln_b)
    vars_ = dict(w_in=v_w_in, b_in=v_b_in, g_q=v_g_q, w_uq=v_w_uq, g_kv=v_g_kv, w_ukv=v_w_ukv, w_oa=v_w_oa,
                 sgu_ln_g=v_sgu_ln_g, sgu_ln_b=v_sgu_ln_b, w_s=v_w_s, b_s=v_b_s, w_ob=v_w_ob, w_out=v_w_out,
                 ln_g=v_ln_g, ln_b=v_ln_b)
    return _reduce_and_update(partials, weights, moms, vars_)


def _local_step(x2, positions, tgt, wc, wq, own_shards, b_in, g_q, g_kv, w_ukv, sgu_ln_g, sgu_ln_b, w_s, b_s, ln_g, ln_b):
    win_b, oa_b, ob_b, out_b = own_shards
    ba, bb, bc = _to_segments(b_in)
    w_ukv_bf = w_ukv[0].astype(BF16)
    wkn = jnp.pad(w_ukv_bf[:, :, :QK_NOPE_DIM], ((0, 0), (0, 0), (0, HEAD_PAD - QK_NOPE_DIM))).reshape(KV_LORA_RANK, -1)
    wv = jnp.pad(w_ukv_bf[:, :, QK_NOPE_DIM:], ((0, 0), (0, 0), (0, HEAD_PAD - V_HEAD_DIM))).reshape(KV_LORA_RANK, -1)
    gq = jnp.pad(g_q, ((0, 0), (0, CQ_PAD - Q_LORA_RANK)))
    bias_full = jnp.repeat(b_s[0].T, SGU_GROUP_DIM, axis=1)
    w_s3 = w_s[0]
    w_st3 = jnp.swapaxes(w_s3, 1, 2)

    inv_freq = ROPE_THETA ** (-jnp.arange(0, QK_ROPE_DIM, 2, dtype=F32) / QK_ROPE_DIM)
    invf_lane = jnp.concatenate([jnp.zeros((ROPE_LO,), F32), inv_freq, inv_freq,
                                 jnp.zeros((LANES - ROPE_HI,), F32)]).reshape(1, LANES)
    c_t, sa_t, sb_t = _rope_tables(positions.reshape(SEQ, 1), invf_lane)

    x_bf, xt_bf = _cast_x(x2)
    h_c = _mm(x_bf, wc, bias=bc, tm=512, tn=SEG_C, name="in_proj_c")
    q, k, kt, vx, vxt = _mla_prep(h_c, gq, g_kv, wq, wkn, wv, c_t, sa_t, sb_t)
    o, lse, (g_in,) = _attn_fwd(q, kt, vx, (win_b,))
    wa, wb = _assemble_in(g_in)
    h_a, gathered = _mm(x_bf, wa, bias=ba, own=(oa_b, ob_b, out_b), tm=512, tn=SEG_A // 2, name="in_proj_a")
    h_b = _mm(x_bf, wb, bias=bb, tm=512, tn=SEG_B // 2, name="in_proj_b")
    y_b = _sgu_fwd(h_b, sgu_ln_g, sgu_ln_b, w_s3, bias_full)
    w_oa_f, w_ob_f, w_out_f = _assemble_out(*gathered)

    (loss_row, dx_res, dh_a, d_o, d_yb, p_oa, p_ob, p_out, d_lng, d_lnb, d_ba) = _merge(
        x2, o, h_a, y_b, tgt, w_oa_f, w_ob_f, w_out_f, ln_g, ln_b)
    (dh_b, d_ws, d_bs_t, d_slg, d_slb, d_bb), landed_o = _sgu_bwd(h_b, d_yb, sgu_ln_g, sgu_ln_b, w_s3, w_st3, bias_full,
                                                                 (p_oa, p_ob, p_out))
    d_wa = _mm(xt_bf, dh_a, out_dtype=BF16, tm=512, tn=512, name="dw_in_a")
    d_wb = _mm(xt_bf, dh_b, out_dtype=BF16, tm=512, tn=512, name="dw_in_b")
    dq, dk, dv, landed_in = _attn_bwd(q, kt, k, vxt, d_o, o, lse, (_to_parts(d_wa, d_wb),))
    landed = (*landed_in, *landed_o)
    dh_c, p_uq, d_wkn, d_wv, d_gq, d_gkv, d_bc = _mla_bwd(dq, dk, dv, h_c, gq, g_kv, wq, wkn, wv, c_t, sa_t, sb_t)
    d_wc = _mm(xt_bf, dh_c, out_dtype=BF16, tm=512, tn=SEG_C, name="dw_in_c")

    dx = _mm(dh_a, wa, tb=True, add=dx_res, tm=512, tn=D_MODEL, name="dx_a")
    dx = _mm(dh_b, wb, tb=True, add=dx, tm=512, tn=D_MODEL, name="dx_b")
    dx = _mm(dh_c, wc, tb=True, add=dx, tm=512, tn=D_MODEL, name="dx_c")

    p_b_in = _from_segments(d_ba, d_bb, d_bc)
    p_w_ukv = jnp.concatenate([d_wkn.reshape(KV_LORA_RANK, MLA_HEADS, HEAD_PAD)[:, :, :QK_NOPE_DIM],
                               d_wv.reshape(KV_LORA_RANK, MLA_HEADS, HEAD_PAD)[:, :, :V_HEAD_DIM]], axis=-1)
    p_g_q = d_gq[:, :Q_LORA_RANK]
    p_b_s = d_bs_t[:, :SGU_GROUPS].T
    replicated = [p_b_in, p_g_q, d_gkv, p_w_ukv, d_slg, d_slb, d_ws, p_b_s, d_lng, d_lnb]
    return loss_row, dx, landed, d_wc, p_uq, replicated


_NAMES = ["w_in", "b_in", "g_q", "w_uq", "g_kv", "w_ukv", "w_oa", "sgu_ln_g", "sgu_ln_b", "w_s", "b_s", "w_ob",
          "w_out", "ln_g", "ln_b"]
_REPLICATED = ["b_in", "g_q", "g_kv", "w_ukv", "sgu_ln_g", "sgu_ln_b", "w_s", "b_s", "ln_g", "ln_b"]


def _reduce_and_update(partials, weights, moms, vars_):
    loss_row, dx, landed, d_wc, p_uq, replicated = partials
    rep_flat = jnp.concatenate([a.reshape(-1) for a in replicated] + [loss_row[0, :1]])
    rep_flat = jnp.pad(rep_flat, (0, N_DEV * PACK_R_ROWS * LANES - rep_flat.size))
    c_slice, g_uq, rep_slice = _reduce_grads(d_wc, p_uq, rep_flat.reshape(N_DEV, PACK_R_ROWS, LANES))
    rep_all, c_all = _gather_small(rep_slice, c_slice)
    rep_sum = rep_all.reshape(-1)
    device = 4 * lax.axis_index("x") + 2 * lax.axis_index("y") + lax.axis_index("c")
    g_in, g_oa, g_ob, g_out = _sum_landed(landed, c_all, (device == 0).astype(F32).reshape(1, 1))
    grads, pos = dict(w_in=g_in, w_uq=g_uq, w_oa=g_oa, w_ob=g_ob, w_out=g_out), 0
    for nm in _REPLICATED:
        grads[nm] = rep_sum[pos:pos + weights[nm].size]
        pos += weights[nm].size
    loss = rep_sum[pos]
    grads = {nm: grads[nm].reshape(weights[nm].shape) for nm in _NAMES}
    deltas, new_m, new_v = _adamw_all([weights[nm] for nm in _NAMES], [grads[nm] for nm in _NAMES],
                                      [moms[nm] for nm in _NAMES], [vars_[nm] for nm in _NAMES])
    return (loss, dx.reshape(1, SEQ, D_MODEL), *[grads[nm] for nm in _NAMES], *deltas, *new_m, *new_v)
```

```python
import math

import jax
import jax.numpy as jnp
from jax import lax
from jax.experimental import pallas as pl
from jax.experimental.pallas import tpu as pltpu

F32 = jnp.float32
BF16 = jnp.bfloat16

D_MODEL = 1024
SEQ = 2048
N_DEV = 8
MLA_HEADS = 8
Q_LORA_RANK = 384
KV_LORA_RANK = 128
QK_NOPE_DIM = 64
QK_ROPE_DIM = 32
V_HEAD_DIM = 64
QK_HEAD_DIM = QK_NOPE_DIM + QK_ROPE_DIM
MLA_WIDTH = MLA_HEADS * V_HEAD_DIM
ROPE_THETA = 10000.0
SGU_GROUPS = 8
SGU_GROUP_DIM = 64
SGU_WIDTH = SGU_GROUPS * SGU_GROUP_DIM
CHUNK = 128
RMS_EPS = 1e-6
LN_EPS = 1e-5
DN_ALPHA = 2.0 ** 0.25
IN_WIDTH = 4640
ATTN_SCALE = QK_HEAD_DIM ** -0.5

ADAM_LR = 0.001
ADAM_B1 = 0.9
ADAM_B2 = 0.999
ADAM_EPS = 1e-08
ADAM_WD = 0.01
ADAM_STEP = 10

LANES = 128
HEAD_PAD = 128
ROPE_LO = QK_NOPE_DIM
ROPE_MID = ROPE_LO + QK_ROPE_DIM // 2
ROPE_HI = ROPE_LO + QK_ROPE_DIM
CQ_PAD = 512

SEG_A = 2560
SEG_B = 1536
SEG_C = 768

PACK_R_ROWS = 272
VMEM_BIG = 56 * 1024 * 1024
VMEM_MID = 40 * 1024 * 1024


def _sigmoid(x):
    return 1.0 / (1.0 + jnp.exp(-x))


def _gelu_and_grad(x):
    c0 = math.sqrt(2.0 / math.pi)
    x2 = x * x
    t = jnp.tanh(c0 * (x + 0.044715 * x * x2))
    g = 0.5 * x * (1.0 + t)
    dg = 0.5 * (1.0 + t) + 0.5 * x * (1.0 - t * t) * (c0 * (1.0 + 3.0 * 0.044715 * x2))
    return g, dg


def _dot(a, b, dims):
    return lax.dot_general(a, b, (dims, ((), ())), preferred_element_type=F32)


_NN = ((1,), (0,))
_NT = ((1,), (1,))
_TN = ((0,), (0,))


def _hbm(*operands):
    return [pltpu.with_memory_space_constraint(a, pltpu.HBM) for a in operands]


def _store_grad(dh_ref, db_ref, col, val):
    cols = slice(col, col + val.shape[1])
    dh_ref[:, cols] = val.astype(BF16)
    db_ref[:, cols] += jnp.sum(val, axis=0, keepdims=True)


def _mm(a, b, *, tb=False, bias=None, add=None, out_dtype=F32, own=(), tm, tn, name):
    m, k = a.shape
    n = b.shape[0] if tb else b.shape[1]
    assert m % tm == 0 and n % tn == 0
    dims = _NT if tb else _NN
    nown = len(own)
    nm = m // tm
    nsteps = (n // tn) * nm

    def body(*refs):
        a_ref, b_ref = refs[0], refs[1]
        pos = 2
        r = _dot(a_ref[...], b_ref[...], dims)
        if bias is not None:
            r = r + refs[pos][...]; pos += 1
        if add is not None:
            r = r + refs[pos][...]; pos += 1
        own_refs = refs[pos:pos + nown]; pos += nown
        refs[pos][...] = r.astype(out_dtype)
        if nown:
            gat_refs = refs[pos + 1:pos + 1 + nown]
            send_sems, recv_sems, local_sems = refs[pos + 1 + nown:]
            _gather_behind(own_refs, gat_refs, send_sems, recv_sems, local_sems,
                           pl.program_id(0) * nm + pl.program_id(1), nsteps - 2, nsteps - 1)

    b_spec = pl.BlockSpec((tn, k), lambda j, i: (j, 0)) if tb else pl.BlockSpec((k, tn), lambda j, i: (0, j))
    in_specs, args = [pl.BlockSpec((tm, k), lambda j, i: (i, 0)), b_spec], [a, b]
    if bias is not None:
        in_specs.append(pl.BlockSpec((1, tn), lambda j, i: (0, j))); args.append(bias)
    if add is not None:
        in_specs.append(pl.BlockSpec((tm, tn), lambda j, i: (i, j))); args.append(add)
    hbm = pl.BlockSpec(memory_space=pl.ANY)
    res = pl.pallas_call(
        body, name=name, grid=(n // tn, nm), in_specs=in_specs + [hbm] * nown,
        out_specs=[pl.BlockSpec((tm, tn), lambda j, i: (i, j))] + [hbm] * nown,
        out_shape=[jax.ShapeDtypeStruct((m, n), out_dtype)]
        + [jax.ShapeDtypeStruct((N_DEV,) + o.shape, o.dtype) for o in own],
        scratch_shapes=_exchange_sems(nown) if nown else [],
        compiler_params=pltpu.CompilerParams(dimension_semantics=("arbitrary", "arbitrary"), vmem_limit_bytes=VMEM_BIG),
    )(*_hbm(*args, *own))
    return (res[0], res[1:]) if nown else res[0]


def _cast_x(x2):
    tm = 256

    def body(x_ref, xb_ref, xt_ref):
        x = x_ref[...]
        xb_ref[...] = x.astype(BF16)
        xt_ref[...] = x.T.astype(BF16)

    return pl.pallas_call(
        body, name="cast_x", grid=(SEQ // tm,),
        in_specs=[pl.BlockSpec((tm, D_MODEL), lambda i: (i, 0))],
        out_specs=[pl.BlockSpec((tm, D_MODEL), lambda i: (i, 0)), pl.BlockSpec((D_MODEL, tm), lambda i: (0, i))],
        out_shape=[jax.ShapeDtypeStruct((SEQ, D_MODEL), BF16), jax.ShapeDtypeStruct((D_MODEL, SEQ), BF16)],
        compiler_params=pltpu.CompilerParams(dimension_semantics=("arbitrary",)),
    )(*_hbm(x2))


def _rope_tables(pos_col, invf_lane):
    def body(pos_ref, invf_ref, c_ref, sa_ref, sb_ref):
        ang = pos_ref[...].astype(F32) * invf_ref[...]
        cs, sn = jnp.cos(ang), jnp.sin(ang)
        lane = lax.broadcasted_iota(jnp.int32, ang.shape, 1)
        c_ref[...] = jnp.where(lane < ROPE_LO, 1.0, jnp.where(lane < ROPE_HI, cs, 0.0))
        sa_ref[...] = jnp.where(jnp.logical_and(lane >= ROPE_LO, lane < ROPE_MID), -sn, 0.0)
        sb_ref[...] = jnp.where(jnp.logical_and(lane >= ROPE_MID, lane < ROPE_HI), sn, 0.0)

    shp = jax.ShapeDtypeStruct((SEQ, LANES), F32)
    return pl.pallas_call(body, name="rope_tables", out_shape=[shp, shp, shp])(pos_col, invf_lane)


def _rope(x, c, sa, sb):
    return x * c + pltpu.roll(x, LANES - 16, 1) * sa + pltpu.roll(x, 16, 1) * sb


def _rope_t(dy, c, sa, sb):
    return dy * c + pltpu.roll(dy * sa, 16, 1) + pltpu.roll(dy * sb, LANES - 16, 1)


def _mla_prep(h_c, gq, gkv, wq, wkn, wvx, c_t, sa_t, sb_t):
    tm = 256
    hw = MLA_HEADS * HEAD_PAD

    def body(cq_ref, ckv_ref, kpe_ref, gq_ref, gkv_ref, wq_ref, wkn_ref, wvx_ref, c_ref, sa_ref, sb_ref,
             q_ref, k_ref, kt_ref, vx_ref, vxt_ref):
        c, sa, sb = c_ref[...], sa_ref[...], sb_ref[...]
        cq = cq_ref[...]
        rq = lax.rsqrt(jnp.sum(cq * cq, axis=1, keepdims=True) * (1.0 / Q_LORA_RANK) + RMS_EPS)
        cqn = ((cq * rq) * gq_ref[...]).astype(BF16)
        qall = _dot(cqn, wq_ref[...], _NN)
        for h in range(MLA_HEADS):
            sl = slice(HEAD_PAD * h, HEAD_PAD * (h + 1))
            q_ref[:, sl] = (_rope(qall[:, sl], c, sa, sb) * ATTN_SCALE).astype(BF16)
        ckv = ckv_ref[...]
        rkv = lax.rsqrt(jnp.sum(ckv * ckv, axis=1, keepdims=True) * (1.0 / KV_LORA_RANK) + RMS_EPS)
        ckvn = ((ckv * rkv) * gkv_ref[...]).astype(BF16)
        knall = _dot(ckvn, wkn_ref[...], _NN)
        vall = _dot(ckvn, wvx_ref[...], _NN)
        kper = _rope(kpe_ref[...], c, sa, sb)
        ones_half = (lax.broadcasted_iota(jnp.int32, (tm, HEAD_PAD), 1) >= V_HEAD_DIM).astype(F32)
        for h in range(MLA_HEADS):
            sl = slice(HEAD_PAD * h, HEAD_PAD * (h + 1))
            kh = knall[:, sl] + kper
            vh = vall[:, sl] + ones_half
            k_ref[:, sl] = kh.astype(BF16)
            kt_ref[sl, :] = kh.T.astype(BF16)
            vx_ref[:, sl] = vh.astype(BF16)
            vxt_ref[sl, :] = vh.T.astype(BF16)

    full = lambda shape: pl.BlockSpec(shape, lambda i: (0, 0))
    tab = pl.BlockSpec((tm, LANES), lambda i: (i, 0))
    row = pl.BlockSpec((tm, hw), lambda i: (i, 0))
    col = pl.BlockSpec((hw, tm), lambda i: (0, i))
    return pl.pallas_call(
        body, name="mla_prep", grid=(SEQ // tm,),
        in_specs=[pl.BlockSpec((tm, CQ_PAD), lambda i: (i, 0)),
                  pl.BlockSpec((tm, LANES), lambda i: (i, CQ_PAD // LANES)),
                  pl.BlockSpec((tm, LANES), lambda i: (i, CQ_PAD // LANES + 1)),
                  full((1, CQ_PAD)), full((1, KV_LORA_RANK)),
                  full((CQ_PAD, hw)), full((KV_LORA_RANK, hw)), full((KV_LORA_RANK, hw)), tab, tab, tab],
        out_specs=[row, row, col, row, col],
        out_shape=[jax.ShapeDtypeStruct((SEQ, hw), BF16), jax.ShapeDtypeStruct((SEQ, hw), BF16),
                   jax.ShapeDtypeStruct((hw, SEQ), BF16), jax.ShapeDtypeStruct((SEQ, hw), BF16),
                   jax.ShapeDtypeStruct((hw, SEQ), BF16)],
        compiler_params=pltpu.CompilerParams(dimension_semantics=("arbitrary",), vmem_limit_bytes=VMEM_MID),
    )(*_hbm(h_c, h_c, h_c, gq, gkv, wq, wkn, wvx, c_t, sa_t, sb_t))


ATT_T = 512
ATT_STRIP = 64


def _attn_fwd(q, kt, vx, own):
    t, rs = ATT_T, ATT_STRIP
    nown = len(own)
    nq = SEQ // t
    nsteps = (MLA_HEADS // 2) * nq

    def body(q_ref, kt_ref, vx_ref, *rest):
        own_refs, (o_ref, l_ref), gat_refs = rest[:nown], rest[nown:nown + 2], rest[nown + 2:2 * nown + 2]
        s_scr, p_scr, m_scr, a_scr, acc_scr, send_sems, recv_sems, local_sems = rest[2 * nown + 2:]
        qi = pl.program_id(1)
        _gather_behind(own_refs, gat_refs, send_sems, recv_sems, local_sems, pl.program_id(0) * nq + qi,
                       nsteps - 2, nsteps - 1)
        lane = lax.broadcasted_iota(jnp.int32, (t, LANES), 1)
        m_scr[...] = jnp.full((2, t, LANES), -1e30, F32)
        acc_scr[...] = jnp.zeros((2, t, LANES), F32)

        def block(j, masked):
            off = pl.multiple_of(j * t, t)
            for a in range(2):
                sl = slice(HEAD_PAD * a, HEAD_PAD * (a + 1))
                s_scr[a] = _dot(q_ref[:, sl], kt_ref[sl, pl.ds(off, t)], _NN)
                for r in range(t // rs):
                    rows = slice(rs * r, rs * (r + 1))
                    s = s_scr[a, rows, :]
                    if masked:
                        rowi = lax.broadcasted_iota(jnp.int32, (rs, t), 0) + rs * r
                        coli = lax.broadcasted_iota(jnp.int32, (rs, t), 1)
                        s = jnp.where(coli <= rowi, s, -1e30)
                    m_old = m_scr[a, rows, :]
                    m_new = jnp.maximum(m_old, jnp.max(s, axis=1, keepdims=True))
                    p_scr[a, rows, :] = jnp.exp(s - m_new[:, :1]).astype(BF16)
                    a_scr[a, rows, :] = jnp.exp(m_old - m_new)
                    m_scr[a, rows, :] = m_new
                acc_scr[a] = acc_scr[a] * a_scr[a] + _dot(p_scr[a], vx_ref[pl.ds(off, t), sl], _NN)

        def step(j, carry):
            block(j, False)
            return carry
        lax.fori_loop(0, qi, step, 0)
        block(qi, True)
        res = []
        for a in range(2):
            acc = acc_scr[a]
            l = acc[:, V_HEAD_DIM:V_HEAD_DIM + 1]
            res.append((acc / l, m_scr[a] + jnp.log(l)))
        o_ref[...] = jnp.where(lane < V_HEAD_DIM, res[0][0], pltpu.roll(res[1][0], V_HEAD_DIM, 1))
        l_ref[...] = jnp.where(lane < V_HEAD_DIM, res[0][1], res[1][1])

    hbm = pl.BlockSpec(memory_space=pl.ANY)
    res = pl.pallas_call(
        body, name="attn_fwd", grid=(MLA_HEADS // 2, nq),
        in_specs=[pl.BlockSpec((t, 2 * HEAD_PAD), lambda p, i: (i, p)),
                  pl.BlockSpec((2 * HEAD_PAD, SEQ), lambda p, i: (p, 0)),
                  pl.BlockSpec((SEQ, 2 * HEAD_PAD), lambda p, i: (0, p))] + [hbm] * nown,
        out_specs=[pl.BlockSpec((t, LANES), lambda p, i: (i, p)),
                   pl.BlockSpec((t, LANES), lambda p, i: (i, p))] + [hbm] * nown,
        out_shape=[jax.ShapeDtypeStruct((SEQ, MLA_WIDTH), F32), jax.ShapeDtypeStruct((SEQ, MLA_WIDTH), F32)]
        + [jax.ShapeDtypeStruct((N_DEV,) + a.shape, a.dtype) for a in own],
        scratch_shapes=[pltpu.VMEM((2, t, t), F32), pltpu.VMEM((2, t, t), BF16), pltpu.VMEM((2, t, LANES), F32),
                        pltpu.VMEM((2, t, LANES), F32), pltpu.VMEM((2, t, LANES), F32)] + _exchange_sems(nown),
        compiler_params=pltpu.CompilerParams(dimension_semantics=("arbitrary", "arbitrary"), vmem_limit_bytes=VMEM_MID),
    )(*_hbm(q, kt, vx, *own))
    return res[0], res[1], res[2:]


def _exchange_parts(parts, lands, send_sems, recv_sems, local_sems):
    x, y, c = _mesh_pos()
    me = 4 * x + 2 * y + c
    peers = [(x, y, 1 - c), (1 - x, y, c), (x, 1 - y, c), (1 - x, 1 - y, c),
             (1 - x, y, 1 - c), (x, 1 - y, 1 - c), (1 - x, 1 - y, 1 - c)]
    remote, local = [], []
    for a, (part, land) in enumerate(zip(parts, lands)):
        for k, peer in enumerate(peers):
            t = 4 * peer[0] + 2 * peer[1] + peer[2]
            remote.append(_remote(part.at[t], land.at[me], send_sems, recv_sems, 7 * a + k, peer))
        local.append(pltpu.make_async_copy(part.at[me], land.at[me], local_sems.at[a]))
    return remote, local


def _exchange_start(first_step, exchange):
    remote, local = exchange

    @pl.when(first_step)
    def _():
        for cp in remote + local:
            cp.start()


def _exchange_finish(last_step, exchange):
    remote, local = exchange

    @pl.when(last_step)
    def _():
        for cp in remote:
            cp.wait_recv()
        for cp in remote:
            cp.wait_send()
        for cp in local:
            cp.wait()


def _exchange_sems(npart):
    return [pltpu.SemaphoreType.DMA((7 * npart,)), pltpu.SemaphoreType.DMA((7 * npart,)),
            pltpu.SemaphoreType.DMA((npart,))]


def _attn_bwd(q, kt, k, vxt, d_o, o, lse, parts):
    t, rs = ATT_T, ATT_STRIP
    nq = SEQ // t
    npart = len(parts)
    nsteps = MLA_HEADS // 2

    def body(q_ref, kt_ref, k_ref, vxt_ref, do_ref, o_ref, l_ref, *rest):
        part_refs, rest = rest[:npart], rest[npart:]
        dq_ref, dk_ref, dv_ref = rest[:3]
        land_refs, rest = rest[3:3 + npart], rest[3 + npart:]
        s_scr, dp_scr, p_scr, ds_scr, st_scr, send_sems, recv_sems, local_sems = rest
        exchange = _exchange_parts(part_refs, land_refs, send_sems, recv_sems, local_sems)
        _exchange_start(pl.program_id(0) == 0, exchange)
        dk_ref[...] = jnp.zeros_like(dk_ref)
        dv_ref[...] = jnp.zeros_like(dv_ref)
        lane = lax.broadcasted_iota(jnp.int32, (t, LANES), 1)

        def qtile(i, carry):
            ioff = pl.multiple_of(i * t, t)
            do_i = do_ref[pl.ds(ioff, t), :]
            o_i = o_ref[pl.ds(ioff, t), :]
            l_i = l_ref[pl.ds(ioff, t), :]
            for a in range(2):
                sl = slice(HEAD_PAD * a, HEAD_PAD * (a + 1))
                sel = (lane < V_HEAD_DIM) if a == 0 else (lane >= V_HEAD_DIM)
                doa = jnp.where(sel, do_i, 0.0)
                oa = o_i
                if a == 1:
                    doa = pltpu.roll(doa, V_HEAD_DIM, 1)
                    oa = pltpu.roll(o_i, V_HEAD_DIM, 1)
                st_scr[0] = jnp.broadcast_to(jnp.sum(doa * oa, axis=1, keepdims=True), (t, LANES))
                st_scr[1] = jnp.broadcast_to(l_i[:, V_HEAD_DIM * a:V_HEAD_DIM * a + 1], (t, LANES))
                doa_bf = doa.astype(BF16)
                qa = q_ref[pl.ds(ioff, t), sl]

                def block(j, masked, dq_acc, sl=sl, qa=qa, doa_bf=doa_bf):
                    joff = pl.multiple_of(j * t, t)
                    s_scr[...] = _dot(qa, kt_ref[sl, pl.ds(joff, t)], _NN)
                    dp_scr[...] = _dot(doa_bf, vxt_ref[sl, pl.ds(joff, t)], _NN)
                    for r in range(t // rs):
                        rows = slice(rs * r, rs * (r + 1))
                        p = jnp.exp(s_scr[rows, :] - st_scr[1, rows, :1])
                        if masked:
                            rowi = lax.broadcasted_iota(jnp.int32, (rs, t), 0) + rs * r
                            coli = lax.broadcasted_iota(jnp.int32, (rs, t), 1)
                            p = jnp.where(coli <= rowi, p, 0.0)
                        p_scr[rows, :] = p.astype(BF16)
                        ds_scr[rows, :] = (p * (dp_scr[rows, :] - st_scr[0, rows, :1])).astype(BF16)
                    dk_ref[pl.ds(joff, t), sl] += _dot(ds_scr[...], qa, _TN)
                    dv_ref[pl.ds(joff, t), sl] += _dot(p_scr[...], doa_bf, _TN)
                    return dq_acc + _dot(ds_scr[...], k_ref[pl.ds(joff, t), sl], _NN)

                dq_acc = lax.fori_loop(0, i, lambda j, acc: block(j, False, acc), jnp.zeros((t, HEAD_PAD), F32))
                dq_ref[pl.ds(ioff, t), sl] = block(i, True, dq_acc)
            return carry

        lax.fori_loop(0, nq, qtile, 0)
        _exchange_finish(pl.program_id(0) == nsteps - 1, exchange)

    hw = MLA_HEADS * HEAD_PAD
    wide = pl.BlockSpec((SEQ, 2 * HEAD_PAD), lambda p: (0, p))
    wide_t = pl.BlockSpec((2 * HEAD_PAD, SEQ), lambda p: (p, 0))
    narrow = pl.BlockSpec((SEQ, LANES), lambda p: (0, p))
    hbm = pl.BlockSpec(memory_space=pl.ANY)
    res = pl.pallas_call(
        body, name="attn_bwd", grid=(nsteps,),
        in_specs=[wide, wide_t, wide, wide_t, narrow, narrow, narrow] + [hbm] * npart,
        out_specs=[wide, wide, wide] + [hbm] * npart,
        out_shape=[jax.ShapeDtypeStruct((SEQ, hw), F32)] * 3 + [jax.ShapeDtypeStruct(p.shape, p.dtype) for p in parts],
        scratch_shapes=[pltpu.VMEM((t, t), F32), pltpu.VMEM((t, t), F32), pltpu.VMEM((t, t), BF16),
                        pltpu.VMEM((t, t), BF16), pltpu.VMEM((2, t, LANES), F32)] + _exchange_sems(npart),
        compiler_params=pltpu.CompilerParams(dimension_semantics=("arbitrary",), vmem_limit_bytes=VMEM_BIG),
    )(*_hbm(q, kt, k, vxt, d_o, o, lse, *parts))
    return res[0], res[1], res[2], res[3:]


def _sgu_math(u, v, zb, lg, lb, ws_ref, bias):
    ug, dug = _gelu_and_grad(u)
    vg, dvg = _gelu_and_grad(v)
    mu = jnp.mean(vg, axis=1, keepdims=True)
    xc = vg - mu
    rstd = lax.rsqrt(jnp.mean(xc * xc, axis=1, keepdims=True) + LN_EPS)
    xh = xc * rstd
    vn_bf = (xh * lg + lb).astype(BF16)
    grp = lax.broadcasted_iota(jnp.int32, (CHUNK, SGU_WIDTH), 1) // SGU_GROUP_DIM
    r_i = lax.broadcasted_iota(jnp.int32, (CHUNK, CHUNK), 0)
    c_i = lax.broadcasted_iota(jnp.int32, (CHUNK, CHUNK), 1)
    tri, tri_t = r_i >= c_i, r_i <= c_i
    mixed = bias
    for g in range(SGU_GROUPS):
        wt = jnp.where(tri, ws_ref[g], 0.0).astype(BF16)
        mixed = mixed + jnp.where(grp == g, _dot(wt, vn_bf, _NN), 0.0)
    sb = _sigmoid(zb)
    return ug, dug, dvg, rstd, xh, vn_bf, grp, tri, tri_t, mixed, sb


def _sgu_fwd(h_b, lg, lb, w_s, bias_full):
    def body(u_ref, v_ref, zb_ref, lg_ref, lb_ref, ws_ref, bias_ref, yb_ref):
        zb = zb_ref[...]
        ug, _, _, _, _, _, _, _, _, mixed, sb = _sgu_math(u_ref[...], v_ref[...], zb, lg_ref[...], lb_ref[...],
                                                       ws_ref, bias_ref[...])
        yb_ref[...] = (ug * mixed) * (zb * sb)

    blk = lambda c: pl.BlockSpec((CHUNK, SGU_WIDTH), lambda i, c=c: (i, c))
    full2 = lambda shape: pl.BlockSpec(shape, lambda i: (0, 0))
    return pl.pallas_call(
        body, name="sgu_fwd", grid=(SEQ // CHUNK,),
        in_specs=[blk(0), blk(1), blk(2), full2((1, SGU_WIDTH)), full2((1, SGU_WIDTH)),
                  pl.BlockSpec((SGU_GROUPS, CHUNK, CHUNK), lambda i: (0, 0, 0)), full2((CHUNK, SGU_WIDTH))],
        out_specs=pl.BlockSpec((CHUNK, SGU_WIDTH), lambda i: (i, 0)),
        out_shape=jax.ShapeDtypeStruct((SEQ, SGU_WIDTH), F32),
        compiler_params=pltpu.CompilerParams(dimension_semantics=("arbitrary",)),
    )(*_hbm(h_b, h_b, h_b, lg, lb, w_s, bias_full))


def _sgu_bwd(h_b, d_yb, lg, lb, w_s, w_st, bias_full, parts):
    nsteps = SEQ // CHUNK
    npart = len(parts)

    def body(u_ref, v_ref, zb_ref, dyb_ref, lg_ref, lb_ref, ws_ref, wst_ref, bias_ref, *rest):
        part_refs, rest = rest[:npart], rest[npart:]
        dhb_ref, dws_ref, dbs_ref, dlg_ref, dlb_ref, dbb_ref = rest[:6]
        land_refs, (dbias_acc, send_sems, recv_sems, local_sems) = rest[6:6 + npart], rest[6 + npart:]
        step = pl.program_id(0)
        exchange = _exchange_parts(part_refs, land_refs, send_sems, recv_sems, local_sems)
        _exchange_start(step == 0, exchange)

        @pl.when(step == 0)
        def _():
            dbb_ref[...] = jnp.zeros_like(dbb_ref)
            dws_ref[...] = jnp.zeros_like(dws_ref)
            dlg_ref[...] = jnp.zeros_like(dlg_ref)
            dlb_ref[...] = jnp.zeros_like(dlb_ref)
            dbias_acc[...] = jnp.zeros_like(dbias_acc)

        zb = zb_ref[...]
        lg = lg_ref[...]
        ug, dug, dvg, rstd, xh, vn_bf, grp, tri, tri_t, mixed, sb = _sgu_math(
            u_ref[...], v_ref[...], zb, lg, lb_ref[...], ws_ref, bias_ref[...])
        dyb = dyb_ref[...]
        dsgu = dyb * (zb * sb)
        dzb = dyb * (ug * mixed) * (sb * (1.0 + zb * (1.0 - sb)))
        du = dsgu * mixed * dug
        dmixed = dsgu * ug
        dbias_acc[...] += dmixed
        dvn = jnp.zeros((CHUNK, SGU_WIDTH), F32)
        for g in range(SGU_GROUPS):
            dm_g = jnp.where(grp == g, dmixed, 0.0).astype(BF16)
            wtt = jnp.where(tri_t, wst_ref[g], 0.0).astype(BF16)
            dvn = dvn + _dot(wtt, dm_g, _NN)
            dws_ref[g] += jnp.where(tri, _dot(dm_g, vn_bf, _NT), 0.0)
        dlg_ref[...] += jnp.sum(dvn * xh, axis=0, keepdims=True)
        dlb_ref[...] += jnp.sum(dvn, axis=0, keepdims=True)
        dxh = dvn * lg
        dvgel = rstd * (dxh - jnp.mean(dxh, axis=1, keepdims=True) - xh * jnp.mean(dxh * xh, axis=1, keepdims=True))
        _store_grad(dhb_ref, dbb_ref, 0, du)
        _store_grad(dhb_ref, dbb_ref, SGU_WIDTH, dvgel * dvg)
        _store_grad(dhb_ref, dbb_ref, 2 * SGU_WIDTH, dzb)

        @pl.when(step == nsteps - 1)
        def _():
            acc = dbias_acc[...]
            lane = lax.broadcasted_iota(jnp.int32, (CHUNK, LANES), 1)
            out = jnp.zeros((CHUNK, LANES), F32)
            for g in range(SGU_GROUPS):
                sg = jnp.sum(jnp.where(grp == g, acc, 0.0), axis=1, keepdims=True)
                out = jnp.where(lane == g, sg, out)
            dbs_ref[...] = out

        _exchange_finish(step == nsteps - 1, exchange)

    blk = lambda c: pl.BlockSpec((CHUNK, SGU_WIDTH), lambda i, c=c: (i, c))
    full2 = lambda shape: pl.BlockSpec(shape, lambda i: (0, 0))
    full3 = pl.BlockSpec((SGU_GROUPS, CHUNK, CHUNK), lambda i: (0, 0, 0))
    hbm = pl.BlockSpec(memory_space=pl.ANY)
    res = pl.pallas_call(
        body, name="sgu_bwd", grid=(nsteps,),
        in_specs=[blk(0), blk(1), blk(2), pl.BlockSpec((CHUNK, SGU_WIDTH), lambda i: (i, 0)),
                  full2((1, SGU_WIDTH)), full2((1, SGU_WIDTH)), full3, full3, full2((CHUNK, SGU_WIDTH))] + [hbm] * npart,
        out_specs=[pl.BlockSpec((CHUNK, SEG_B), lambda i: (i, 0)), full3, full2((CHUNK, LANES)),
                   full2((1, SGU_WIDTH)), full2((1, SGU_WIDTH)), full2((1, SEG_B))] + [hbm] * npart,
        out_shape=[jax.ShapeDtypeStruct((SEQ, SEG_B), BF16),
                   jax.ShapeDtypeStruct((SGU_GROUPS, CHUNK, CHUNK), F32),
                   jax.ShapeDtypeStruct((CHUNK, LANES), F32),
                   jax.ShapeDtypeStruct((1, SGU_WIDTH), F32), jax.ShapeDtypeStruct((1, SGU_WIDTH), F32),
                   jax.ShapeDtypeStruct((1, SEG_B), F32)] + [jax.ShapeDtypeStruct(p.shape, p.dtype) for p in parts],
        scratch_shapes=[pltpu.VMEM((CHUNK, SGU_WIDTH), F32)] + _exchange_sems(npart),
        compiler_params=pltpu.CompilerParams(dimension_semantics=("arbitrary",)),
    )(*_hbm(h_b, h_b, h_b, d_yb, lg, lb, w_s, w_st, bias_full, *parts))
    return res[:6], res[6:]


def _merge(x, o, h_a, y_b, target, w_oa, w_ob, w_out, ln_g, ln_b):
    tm = 256
    nsteps = SEQ // tm

    def body(x_ref, o_ref, ga_ref, gb_ref, za_ref, yb_ref, tgt_ref, woa_ref, wob_ref, wout_ref, lng_ref, lnb_ref,
             loss_ref, dxr_ref, dha_ref, do_ref, dyb_ref, poa_ref, pob_ref, pout_ref, dlng_ref, dlnb_ref, dba_ref,
             dwoa_ref, dwob_ref, dwout_ref):
        step = pl.program_id(0)

        @pl.when(step == 0)
        def _():
            for r in (loss_ref, dwoa_ref, dwob_ref, dwout_ref, dlng_ref, dlnb_ref, dba_ref):
                r[...] = jnp.zeros_like(r)

        o = o_ref[...]
        za = za_ref[...]
        sa = _sigmoid(za)
        ya_bf = (o * (za * sa)).astype(BF16)
        yb_bf = yb_ref[...].astype(BF16)
        woa, wob, wout = woa_ref[...], wob_ref[...], wout_ref[...]
        pa = _dot(ya_bf, woa, _NN)
        pb = _dot(yb_bf, wob, _NN)
        sga = _sigmoid(ga_ref[...])
        sgb = _sigmoid(gb_ref[...])
        merged_bf = (sga * pa + sgb * pb).astype(BF16)
        r = DN_ALPHA * x_ref[...] + _dot(merged_bf, wout, _NN)
        mu = jnp.mean(r, axis=1, keepdims=True)
        rc = r - mu
        rstd = lax.rsqrt(jnp.mean(rc * rc, axis=1, keepdims=True) + LN_EPS)
        xh = rc * rstd
        lng = lng_ref[...]
        y = xh * lng + lnb_ref[...]
        e = y - tgt_ref[...]
        loss_ref[...] += 0.5 * jnp.sum(jnp.sum(e * e, axis=1, keepdims=True) * (1.0 / D_MODEL), axis=0, keepdims=True)

        dy = e * (1.0 / D_MODEL)
        dlng_ref[...] += jnp.sum(dy * xh, axis=0, keepdims=True)
        dlnb_ref[...] += jnp.sum(dy, axis=0, keepdims=True)
        dxh = dy * lng
        dr = rstd * (dxh - jnp.mean(dxh, axis=1, keepdims=True) - xh * jnp.mean(dxh * xh, axis=1, keepdims=True))
        dxr_ref[...] = DN_ALPHA * dr
        dr_bf = dr.astype(BF16)
        dwout_ref[...] += _dot(merged_bf, dr_bf, _TN)
        dmerged = _dot(dr_bf, wout, _NT)
        dpa_bf = (dmerged * sga).astype(BF16)
        dpb_bf = (dmerged * sgb).astype(BF16)
        _store_grad(dha_ref, dba_ref, 0, dmerged * pa * (sga * (1.0 - sga)))
        _store_grad(dha_ref, dba_ref, D_MODEL, dmerged * pb * (sgb * (1.0 - sgb)))
        dwoa_ref[...] += _dot(ya_bf, dpa_bf, _TN)
        dwob_ref[...] += _dot(yb_bf, dpb_bf, _TN)
        dya = _dot(dpa_bf, woa, _NT)
        dyb_ref[...] = _dot(dpb_bf, wob, _NT)
        do_ref[...] = dya * (za * sa)
        _store_grad(dha_ref, dba_ref, 2 * D_MODEL, dya * o * (sa * (1.0 + za * (1.0 - sa))))

        @pl.when(step == nsteps - 1)
        def _():
            cols = D_MODEL // N_DEV
            for j in range(N_DEV):
                poa_ref[j] = dwoa_ref[:, cols * j:cols * (j + 1)].astype(BF16)
                pob_ref[j] = dwob_ref[:, cols * j:cols * (j + 1)].astype(BF16)
                pout_ref[j] = dwout_ref[cols * j:cols * (j + 1), :].astype(BF16)

    row = lambda w, c=0: pl.BlockSpec((tm, w), lambda i, c=c: (i, c))
    full = lambda shape: pl.BlockSpec(shape, lambda i: (0, 0))
    full3 = lambda shape: pl.BlockSpec(shape, lambda i: (0, 0, 0))
    return pl.pallas_call(
        body, name="merge", grid=(nsteps,),
        in_specs=[row(D_MODEL), row(MLA_WIDTH), row(D_MODEL, 0), row(D_MODEL, 1), row(MLA_WIDTH, 4), row(SGU_WIDTH),
                  row(D_MODEL), full((MLA_WIDTH, D_MODEL)), full((SGU_WIDTH, D_MODEL)), full((D_MODEL, D_MODEL)),
                  full((1, D_MODEL)), full((1, D_MODEL))],
        out_specs=[full((1, LANES)), row(D_MODEL), row(SEG_A), row(MLA_WIDTH), row(SGU_WIDTH),
                   full3((N_DEV, MLA_WIDTH, D_MODEL // N_DEV)), full3((N_DEV, SGU_WIDTH, D_MODEL // N_DEV)),
                   full3((N_DEV, D_MODEL // N_DEV, D_MODEL)), full((1, D_MODEL)), full((1, D_MODEL)), full((1, SEG_A))],
        out_shape=[jax.ShapeDtypeStruct((1, LANES), F32),
                   jax.ShapeDtypeStruct((SEQ, D_MODEL), F32), jax.ShapeDtypeStruct((SEQ, SEG_A), BF16),
                   jax.ShapeDtypeStruct((SEQ, MLA_WIDTH), F32), jax.ShapeDtypeStruct((SEQ, SGU_WIDTH), F32),
                   jax.ShapeDtypeStruct((N_DEV, MLA_WIDTH, D_MODEL // N_DEV), BF16),
                   jax.ShapeDtypeStruct((N_DEV, SGU_WIDTH, D_MODEL // N_DEV), BF16),
                   jax.ShapeDtypeStruct((N_DEV, D_MODEL // N_DEV, D_MODEL), BF16),
                   jax.ShapeDtypeStruct((1, D_MODEL), F32), jax.ShapeDtypeStruct((1, D_MODEL), F32),
                   jax.ShapeDtypeStruct((1, SEG_A), F32)],
        scratch_shapes=[pltpu.VMEM((MLA_WIDTH, D_MODEL), F32), pltpu.VMEM((SGU_WIDTH, D_MODEL), F32),
                        pltpu.VMEM((D_MODEL, D_MODEL), F32)],
        compiler_params=pltpu.CompilerParams(dimension_semantics=("arbitrary",), vmem_limit_bytes=VMEM_BIG),
    )(*_hbm(x, o, h_a, h_a, h_a, y_b, target, w_oa, w_ob, w_out, ln_g, ln_b))


def _mla_bwd(dq, dk, dv, h_c, gq, gkv, wq, wkn, wv, c_t, sa_t, sb_t):
    tm = 256
    hw = MLA_HEADS * HEAD_PAD

    def body(dq_ref, dk_ref, dv_ref, cq_ref, ckv_ref, gq_ref, gkv_ref, wq_ref, wkn_ref, wv_ref, c_ref, sa_ref, sb_ref,
             dhc_ref, puq_ref, dwkn_ref, dwv_ref, dgq_ref, dgkv_ref, dbc_ref, pre_ref, dwq_ref):
        @pl.when(pl.program_id(0) == 0)
        def _():
            for r in (dwq_ref, dwkn_ref, dwv_ref, dgq_ref, dgkv_ref, dbc_ref):
                r[...] = jnp.zeros_like(r)

        c, sa, sb = c_ref[...], sa_ref[...], sb_ref[...]
        lane = lax.broadcasted_iota(jnp.int32, (tm, LANES), 1)
        rope_lanes = jnp.logical_and(lane >= ROPE_LO, lane < ROPE_HI)

        cq = cq_ref[...]
        gq = gq_ref[...]
        rq = lax.rsqrt(jnp.sum(cq * cq, axis=1, keepdims=True) * (1.0 / Q_LORA_RANK) + RMS_EPS)
        nq = cq * rq
        cqn_bf = (nq * gq).astype(BF16)
        for h in range(MLA_HEADS):
            sl = slice(HEAD_PAD * h, HEAD_PAD * (h + 1))
            pre_ref[:, sl] = _rope_t(dq_ref[:, sl] * ATTN_SCALE, c, sa, sb).astype(BF16)
        dqpre_bf = pre_ref[...]
        dcqn = _dot(dqpre_bf, wq_ref[...], _NT)
        dwq_ref[...] += _dot(cqn_bf, dqpre_bf, _TN)
        dgq_ref[...] += jnp.sum(dcqn * nq, axis=0, keepdims=True)
        dnq = dcqn * gq
        _store_grad(dhc_ref, dbc_ref, 0,
                    rq * (dnq - nq * (jnp.sum(dnq * nq, axis=1, keepdims=True) * (1.0 / Q_LORA_RANK))))

        ckv = ckv_ref[...]
        gkv = gkv_ref[...]
        rkv = lax.rsqrt(jnp.sum(ckv * ckv, axis=1, keepdims=True) * (1.0 / KV_LORA_RANK) + RMS_EPS)
        nkv = ckv * rkv
        ckvn_bf = (nkv * gkv).astype(BF16)
        dk = dk_ref[...]
        dk_bf = dk.astype(BF16)
        dv_bf = dv_ref[...].astype(BF16)
        dckvn = _dot(dk_bf, wkn_ref[...], _NT) + _dot(dv_bf, wv_ref[...], _NT)
        dwkn_ref[...] += _dot(ckvn_bf, dk_bf, _TN)
        dwv_ref[...] += _dot(ckvn_bf, dv_bf, _TN)
        dgkv_ref[...] += jnp.sum(dckvn * nkv, axis=0, keepdims=True)
        dnkv = dckvn * gkv
        _store_grad(dhc_ref, dbc_ref, CQ_PAD, rkv * (
            dnkv - nkv * (jnp.sum(dnkv * nkv, axis=1, keepdims=True) * (1.0 / KV_LORA_RANK))))
        dkpe = jnp.zeros((tm, LANES), F32)
        for h in range(MLA_HEADS):
            dkpe = dkpe + dk[:, HEAD_PAD * h:HEAD_PAD * (h + 1)]
        _store_grad(dhc_ref, dbc_ref, CQ_PAD + LANES, _rope_t(jnp.where(rope_lanes, dkpe, 0.0), c, sa, sb))

        @pl.when(pl.program_id(0) == SEQ // tm - 1)
        def _():
            rows = Q_LORA_RANK // N_DEV
            for j in range(N_DEV):
                for h in range(MLA_HEADS):
                    puq_ref[j, :, QK_HEAD_DIM * h:QK_HEAD_DIM * (h + 1)] = dwq_ref[
                        rows * j:rows * (j + 1), HEAD_PAD * h:HEAD_PAD * h + QK_HEAD_DIM].astype(BF16)

    full = lambda shape: pl.BlockSpec(shape, lambda i: (0, 0))
    row = lambda w, c=0: pl.BlockSpec((tm, w), lambda i, c=c: (i, c))
    return pl.pallas_call(
        body, name="mla_bwd", grid=(SEQ // tm,),
        in_specs=[row(hw), row(hw), row(hw), row(CQ_PAD, 0), row(LANES, CQ_PAD // LANES),
                  full((1, CQ_PAD)), full((1, KV_LORA_RANK)), full((CQ_PAD, hw)), full((KV_LORA_RANK, hw)),
                  full((KV_LORA_RANK, hw)), row(LANES), row(LANES), row(LANES)],
        out_specs=[row(SEG_C), pl.BlockSpec((N_DEV, Q_LORA_RANK // N_DEV, MLA_HEADS * QK_HEAD_DIM), lambda i: (0, 0, 0)),
                   full((KV_LORA_RANK, hw)), full((KV_LORA_RANK, hw)),
                   full((1, CQ_PAD)), full((1, KV_LORA_RANK)), full((1, SEG_C))],
        out_shape=[jax.ShapeDtypeStruct((SEQ, SEG_C), BF16),
                   jax.ShapeDtypeStruct((N_DEV, Q_LORA_RANK // N_DEV, MLA_HEADS * QK_HEAD_DIM), BF16),
                   jax.ShapeDtypeStruct((KV_LORA_RANK, hw), F32), jax.ShapeDtypeStruct((KV_LORA_RANK, hw), F32),
                   jax.ShapeDtypeStruct((1, CQ_PAD), F32), jax.ShapeDtypeStruct((1, KV_LORA_RANK), F32),
                   jax.ShapeDtypeStruct((1, SEG_C), F32)],
        scratch_shapes=[pltpu.VMEM((tm, hw), BF16), pltpu.VMEM((CQ_PAD, hw), F32)],
        compiler_params=pltpu.CompilerParams(dimension_semantics=("arbitrary",), vmem_limit_bytes=VMEM_MID),
    )(*_hbm(dq, dk, dv, h_c, h_c, gq, gkv, wq, wkn, wv, c_t, sa_t, sb_t))


def _adamw_all(ws, gs, ms, vs):
    n = len(ws)
    c1 = 1.0 / (1.0 - ADAM_B1 ** ADAM_STEP)
    c2 = 1.0 / (1.0 - ADAM_B2 ** ADAM_STEP)

    def body(*refs):
        for idx in range(n):
            w, g, m, v = (refs[idx][...], refs[n + idx][...], refs[2 * n + idx][...], refs[3 * n + idx][...])
            m_new = ADAM_B1 * m + (1.0 - ADAM_B1) * g
            v_new = ADAM_B2 * v + (1.0 - ADAM_B2) * (g * g)
            delta = -ADAM_LR * ((m_new * c1) / (jnp.sqrt(v_new * c2) + ADAM_EPS) + ADAM_WD * w)
            refs[4 * n + idx][...] = delta
            refs[5 * n + idx][...] = m_new
            refs[6 * n + idx][...] = v_new

    shapes = [jax.ShapeDtypeStruct(w.shape, F32) for w in ws]
    outs = pl.pallas_call(
        body, name="adamw", out_shape=shapes * 3,
        compiler_params=pltpu.CompilerParams(vmem_limit_bytes=VMEM_BIG),
    )(*ws, *gs, *ms, *vs)
    return outs[:n], outs[n:2 * n], outs[2 * n:]


SHARD_W = IN_WIDTH // N_DEV

_PIECES = [(0, 384, 2, 0), (384, 512, 2, CQ_PAD), (512, 544, 2, CQ_PAD + LANES + ROPE_LO),
           (544, 1056, 0, 2 * D_MODEL), (1056, 1568, 1, 0), (1568, 2080, 1, SGU_WIDTH),
           (2080, 2592, 1, 2 * SGU_WIDTH), (2592, 3616, 0, 0), (3616, 4640, 0, D_MODEL)]


def _column_runs():
    runs = []
    for n0, n1, seg, d0 in _PIECES:
        for j in range(N_DEV):
            lo, hi = max(n0, j * SHARD_W), min(n1, (j + 1) * SHARD_W)
            if lo < hi:
                runs.append((j, lo - j * SHARD_W, hi - j * SHARD_W, seg, d0 + lo - n0))
    return runs


def _mesh_pos():
    return lax.axis_index("x"), lax.axis_index("y"), lax.axis_index("c")


def _remote(src, dst, send_sems, recv_sems, k, to):
    return pltpu.make_async_remote_copy(src_ref=src, dst_ref=dst, send_sem=send_sems.at[k], recv_sem=recv_sems.at[k],
                                        device_id=to, device_id_type=pl.DeviceIdType.MESH)


def _gather_exchange(gats, send_sems, recv_sems):
    x, y, c = _mesh_pos()
    me, sibling = (x, y, c), (x, y, 1 - c)
    chips = [(1 - x, y), (x, 1 - y), (1 - x, 1 - y)]

    def copy(a, k, blk, to):
        slab = gats[a].at[4 * blk[0] + 2 * blk[1] + blk[2]]
        return _remote(slab, slab, send_sems, recv_sems, 7 * a + k, to)

    arrays = range(len(gats))
    first = [copy(a, 1 + j, me, (*chip, c)) for j, chip in enumerate(chips) for a in arrays]
    first += [copy(a, 0, me, sibling) for a in arrays]
    for cp in first:
        cp.start()
    passed = []
    for j, chip in enumerate(chips):
        for a in arrays:
            copy(a, 1 + j, (*chip, c), me).wait_recv()
            fwd = copy(a, 4 + j, (*chip, c), sibling)
            fwd.start()
            passed.append(fwd)
    for a in arrays:
        copy(a, 0, sibling, me).wait_recv()
    for j, chip in enumerate(chips):
        for a in arrays:
            copy(a, 4 + j, (*chip, 1 - c), me).wait_recv()
    for cp in first + passed:
        cp.wait_send()


def _gather_behind(own, gats, send_sems, recv_sems, local_sems, step, mid, last):
    x, y, c = _mesh_pos()
    me, sibling = (x, y, c), (x, y, 1 - c)
    chips = [(1 - x, y), (x, 1 - y), (1 - x, 1 - y)]
    arrays = range(len(gats))

    def copy(a, k, blk, to, src=None):
        slab = gats[a].at[4 * blk[0] + 2 * blk[1] + blk[2]]
        return _remote(slab if src is None else src, slab, send_sems, recv_sems, 7 * a + k, to)

    first = [copy(a, 1 + j, me, (*chip, c), src=own[a]) for j, chip in enumerate(chips) for a in arrays]
    first += [copy(a, 0, me, sibling, src=own[a]) for a in arrays]
    local = [pltpu.make_async_copy(own[a], gats[a].at[4 * x + 2 * y + c], local_sems.at[a]) for a in arrays]
    passed = [copy(a, 4 + j, (*chip, c), sibling) for j, chip in enumerate(chips) for a in arrays]

    @pl.when(step == 0)
    def _():
        for cp in first + local:
            cp.start()

    @pl.when(step == mid)
    def _():
        for j, chip in enumerate(chips):
            for a in arrays:
                copy(a, 1 + j, (*chip, c), me).wait_recv()
        for cp in passed:
            cp.start()

    @pl.when(step == last)
    def _():
        for a in arrays:
            copy(a, 0, sibling, me).wait_recv()
        for j, chip in enumerate(chips):
            for a in arrays:
                copy(a, 4 + j, (*chip, 1 - c), me).wait_recv()
        for cp in first + passed:
            cp.wait_send()
        for cp in local:
            cp.wait()


def _gather_first(w_in, w_uq2, w_oa, w_ob, w_out):
    hw = MLA_HEADS * HEAD_PAD
    uq_rows = Q_LORA_RANK // N_DEV

    def body(win_ref, wuq_ref, woa_ref, wob_ref, wout_ref, wc_ref, wq_ref, winb_ref, oab_ref, obb_ref, outb_ref,
             g_uq, blk0, send_sems, recv_sems):
        x, y, c = _mesh_pos()
        me = (x, y, c)
        winb_ref[...] = win_ref[0].astype(BF16)
        oab_ref[...] = woa_ref[0].astype(BF16)
        obb_ref[...] = wob_ref[0].astype(BF16)
        outb_ref[...] = wout_ref[0].astype(BF16)
        g_uq[4 * x + 2 * y + c] = wuq_ref[...].astype(BF16)

        chip0 = jnp.logical_and(x == 0, y == 0)
        dev0 = jnp.logical_and(chip0, c == 0)
        north = c == 1
        targets = [(0, 0, 1), (1, 0, 0), (0, 1, 0), (1, 1, 0)]

        def bcopy(k, to):
            return _remote(blk0, blk0, send_sems, recv_sems, 7 + k, to)

        @pl.when(dev0)
        def _():
            blk0[...] = winb_ref[...]
            for k, to in enumerate(targets):
                bcopy(k, to).start()

        _gather_exchange([g_uq], send_sems, recv_sems)

        for k, (cx, cy, _) in enumerate(targets[1:], start=1):
            @pl.when(jnp.logical_and(jnp.logical_and(x == cx, y == cy), c == 0))
            def _(k=k, cx=cx, cy=cy):
                bcopy(k, me).wait_recv()
                onward = bcopy(4, (cx, cy, 1))
                onward.start()
                onward.wait_send()

        @pl.when(jnp.logical_and(chip0, north))
        def _():
            bcopy(0, me).wait_recv()

        @pl.when(jnp.logical_and(jnp.logical_not(chip0), north))
        def _():
            bcopy(4, me).wait_recv()

        @pl.when(dev0)
        def _():
            for k, to in enumerate(targets):
                bcopy(k, to).wait_send()

        for j, s0, s1, seg, d0 in _column_runs():
            if seg == 2:
                wc_ref[:, d0:d0 + (s1 - s0)] = blk0[:, s0:s1]
        zeros = lambda r, w: jnp.zeros((r, w), BF16)
        wc_ref[:, Q_LORA_RANK:CQ_PAD] = zeros(D_MODEL, CQ_PAD - Q_LORA_RANK)
        wc_ref[:, CQ_PAD + LANES:CQ_PAD + LANES + ROPE_LO] = zeros(D_MODEL, ROPE_LO)
        wc_ref[:, CQ_PAD + LANES + ROPE_HI:SEG_C] = zeros(D_MODEL, LANES - ROPE_HI)
        wq_ref[Q_LORA_RANK:CQ_PAD, :] = zeros(CQ_PAD - Q_LORA_RANK, hw)
        for h in range(MLA_HEADS):
            wq_ref[0:Q_LORA_RANK, HEAD_PAD * h + QK_HEAD_DIM:HEAD_PAD * (h + 1)] = zeros(Q_LORA_RANK, HEAD_PAD - QK_HEAD_DIM)
        for j in range(N_DEV):
            for h in range(MLA_HEADS):
                wq_ref[uq_rows * j:uq_rows * (j + 1), HEAD_PAD * h:HEAD_PAD * h + QK_HEAD_DIM] = g_uq[
                    j, :, QK_HEAD_DIM * h:QK_HEAD_DIM * (h + 1)]

    vmem = pl.BlockSpec(memory_space=pltpu.VMEM)
    return pl.pallas_call(
        body, name="gather_first",
        out_shape=[jax.ShapeDtypeStruct((D_MODEL, SEG_C), BF16), jax.ShapeDtypeStruct((CQ_PAD, hw), BF16),
                   jax.ShapeDtypeStruct(w_in.shape[1:], BF16), jax.ShapeDtypeStruct(w_oa.shape[1:], BF16),
                   jax.ShapeDtypeStruct(w_ob.shape[1:], BF16), jax.ShapeDtypeStruct(w_out.shape[1:], BF16)],
        in_specs=[vmem] * 5, out_specs=[vmem] * 6,
        scratch_shapes=[pltpu.VMEM((N_DEV, uq_rows, MLA_HEADS * QK_HEAD_DIM), BF16), pltpu.VMEM((D_MODEL, SHARD_W), BF16),
                        pltpu.SemaphoreType.DMA((12,)), pltpu.SemaphoreType.DMA((12,))],
        compiler_params=pltpu.CompilerParams(vmem_limit_bytes=VMEM_MID),
    )(w_in, w_uq2, w_oa, w_ob, w_out)


def _assemble_in(g_in):
    def body(g_ref, wa_ref, wb_ref):
        segs = [wa_ref, wb_ref]
        for j, s0, s1, seg, d0 in _column_runs():
            if seg < 2:
                segs[seg][:, d0:d0 + (s1 - s0)] = g_ref[j, :, s0:s1]

    return pl.pallas_call(
        body, name="assemble_in",
        out_shape=[jax.ShapeDtypeStruct((D_MODEL, SEG_A), BF16), jax.ShapeDtypeStruct((D_MODEL, SEG_B), BF16)],
        compiler_params=pltpu.CompilerParams(vmem_limit_bytes=VMEM_MID),
    )(g_in)


def _assemble_out(g_oa, g_ob, g_out):
    cols = D_MODEL // N_DEV

    def body(goa_ref, gob_ref, gout_ref, oa_ref, ob_ref, out_ref):
        for j in range(N_DEV):
            oa_ref[:, cols * j:cols * (j + 1)] = goa_ref[j]
            ob_ref[:, cols * j:cols * (j + 1)] = gob_ref[j]
            out_ref[cols * j:cols * (j + 1), :] = gout_ref[j]

    return pl.pallas_call(
        body, name="assemble_out",
        out_shape=[jax.ShapeDtypeStruct((MLA_WIDTH, D_MODEL), BF16), jax.ShapeDtypeStruct((SGU_WIDTH, D_MODEL), BF16),
                   jax.ShapeDtypeStruct((D_MODEL, D_MODEL), BF16)],
    )(g_oa, g_ob, g_out)


C_NAT = 544


def _to_parts(dwa, dwb):
    def body(dwa_ref, dwb_ref, pin_ref):
        pin_ref[0, :, 0:C_NAT] = jnp.zeros((D_MODEL, C_NAT), BF16)
        segs = [dwa_ref, dwb_ref]
        for j, s0, s1, seg, d0 in _column_runs():
            if seg < 2:
                pin_ref[j, :, s0:s1] = segs[seg][:, d0:d0 + (s1 - s0)]

    return pl.pallas_call(body, name="to_parts", out_shape=jax.ShapeDtypeStruct((N_DEV, D_MODEL, SHARD_W), BF16),
                          compiler_params=pltpu.CompilerParams(vmem_limit_bytes=VMEM_MID))(dwa, dwb)


def _finish_grads(dwc, p_uq, p_rep, landed):
    rep_rows = p_rep.shape[1]
    c_rows = D_MODEL // N_DEV
    spec = [((c_rows, C_NAT), BF16, c_rows), (p_uq.shape[1:], BF16, p_uq.shape[1]), ((rep_rows, LANES), F32, rep_rows)]
    n = len(spec)

    def body(dwc_ref, puq_ref, prep_ref, rin_ref, roa_ref, rob_ref, rout_ref,
             gin_ref, guq_ref, goa_ref, gob_ref, gout_ref, rep_all, pc_ref, c_red, c_all, *rest):
        ras, tbs, rbs = rest[0:n], rest[n:2 * n], rest[2 * n:3 * n]
        send_sems, recv_sems = rest[3 * n], rest[3 * n + 1]
        x, y, c = _mesh_pos()
        me = 4 * x + 2 * y + c
        sibling = (x, y, 1 - c)
        parts = [pc_ref, puq_ref, prep_ref]
        outs = [c_red, guq_ref, rep_all.at[me]]

        for j, s0, s1, seg, d0 in _column_runs():
            if seg == 2:
                for r in range(N_DEV):
                    pc_ref[r, :, s0:s1] = dwc_ref[c_rows * r:c_rows * (r + 1), d0:d0 + (s1 - s0)]

        stage1 = []
        for chip in range(4):
            for a in range(n):
                cp = _remote(parts[a].at[2 * chip + (1 - c)], ras[a].at[chip], send_sems, recv_sems, 7 * a + chip, sibling)
                cp.start()
                stage1.append(cp)
        for cp in stage1:
            cp.wait_recv()

        def rows_loop(a, fn):
            rows, chunk = spec[a][0][0], spec[a][2]
            if rows == chunk:
                fn(pl.ds(0, rows))
            else:
                def step(i, carry):
                    fn(pl.ds(pl.multiple_of(i * chunk, chunk), chunk))
                    return carry
                lax.fori_loop(0, rows // chunk, step, 0)

        def chip_sum(a, chip, sl):
            return parts[a][2 * chip + c, sl, :].astype(F32) + ras[a][chip, sl, :].astype(F32)

        others = [(1 - x, y), (x, 1 - y), (1 - x, 1 - y)]
        stage2 = []
        for k, (cx, cy) in enumerate(others):
            for a in range(n):
                def fill(sl, a=a, k=k, chip=2 * cx + cy):
                    tbs[a][k, sl, :] = chip_sum(a, chip, sl).astype(spec[a][1])
                rows_loop(a, fill)
                cp = _remote(tbs[a].at[k], rbs[a].at[k], send_sems, recv_sems, 7 * a + 4 + k, (cx, cy, c))
                cp.start()
                stage2.append(cp)
        for cp in stage2:
            cp.wait_recv()
        for a in range(n):
            def final(sl, a=a):
                acc = chip_sum(a, 2 * x + y, sl)
                for k in range(3):
                    acc = acc + rbs[a][k, sl, :].astype(F32)
                outs[a][sl, :] = acc
            rows_loop(a, final)
        for cp in stage1 + stage2:
            cp.wait_send()

        c_all[me] = c_red[...].astype(BF16)
        _gather_exchange([rep_all, c_all], rest[3 * n + 2], rest[3 * n + 3])

        def total(ref, sl):
            acc = ref[0, sl, :].astype(F32)
            for s in range(1, N_DEV):
                acc = acc + ref[s, sl, :].astype(F32)
            return acc

        dev0 = jnp.where(me == 0, 1.0, 0.0)
        for j in range(N_DEV):
            sl = slice(c_rows * j, c_rows * (j + 1))
            tot = total(rin_ref, sl)
            gin_ref[0, sl, C_NAT:SHARD_W] = tot[:, C_NAT:SHARD_W]
            gin_ref[0, sl, 0:C_NAT] = tot[:, 0:C_NAT] + dev0 * c_all[j].astype(F32)
        goa_ref[0] = total(roa_ref, slice(None))
        gob_ref[0] = total(rob_ref, slice(None))
        gout_ref[0] = total(rout_ref, slice(None))

    vmem = pl.BlockSpec(memory_space=pltpu.VMEM)
    scratch = [pltpu.VMEM((N_DEV, c_rows, C_NAT), BF16), pltpu.VMEM((c_rows, C_NAT), F32),
               pltpu.VMEM((N_DEV, c_rows, C_NAT), BF16)]
    for lead in (4, 3, 3):
        scratch += [pltpu.VMEM((lead,) + tuple(shape), dt) for shape, dt, _ in spec]
    scratch += [pltpu.SemaphoreType.DMA((7 * n,)), pltpu.SemaphoreType.DMA((7 * n,)),
                pltpu.SemaphoreType.DMA((14,)), pltpu.SemaphoreType.DMA((14,))]
    return pl.pallas_call(
        body, name="finish_grads",
        out_shape=[jax.ShapeDtypeStruct((1,) + landed[0].shape[1:], F32), jax.ShapeDtypeStruct(p_uq.shape[1:], F32)]
        + [jax.ShapeDtypeStruct((1,) + r.shape[1:], F32) for r in landed[1:]]
        + [jax.ShapeDtypeStruct((N_DEV, rep_rows, LANES), F32)],
        in_specs=[vmem] * 7, out_specs=[vmem] * 6, scratch_shapes=scratch,
        compiler_params=pltpu.CompilerParams(vmem_limit_bytes=VMEM_BIG),
    )(dwc, p_uq, p_rep, *landed)


_O_CQ, _O_CKV, _O_KPE, _O_ZA, _O_U, _O_V, _O_ZB, _O_GA, _O_GB = 0, 384, 512, 544, 1056, 1568, 2080, 2592, 3616


def _to_segments(w):
    z = lambda n: jnp.zeros(w.shape[:-1] + (n,), w.dtype)
    seg_a = jnp.concatenate([w[..., _O_GA:_O_GB], w[..., _O_GB:IN_WIDTH], w[..., _O_ZA:_O_U]], axis=-1)
    seg_b = jnp.concatenate([w[..., _O_U:_O_V], w[..., _O_V:_O_ZB], w[..., _O_ZB:_O_GA]], axis=-1)
    seg_c = jnp.concatenate([w[..., _O_CQ:_O_CKV], z(CQ_PAD - Q_LORA_RANK), w[..., _O_CKV:_O_KPE],
                             z(ROPE_LO), w[..., _O_KPE:_O_ZA], z(LANES - ROPE_HI)], axis=-1)
    return seg_a, seg_b, seg_c


def _from_segments(seg_a, seg_b, seg_c):
    kpe0 = CQ_PAD + LANES + ROPE_LO
    return jnp.concatenate([
        seg_c[..., 0:Q_LORA_RANK], seg_c[..., CQ_PAD:CQ_PAD + LANES], seg_c[..., kpe0:kpe0 + QK_ROPE_DIM],
        seg_a[..., 2 * D_MODEL:SEG_A], seg_b, seg_a[..., 0:2 * D_MODEL]], axis=-1)


def kernel(x, positions, w_in, b_in, g_q, w_uq, g_kv, w_ukv, w_oa, sgu_ln_g, sgu_ln_b, w_s, b_s, w_ob, w_out, ln_g, ln_b, loss_target, m_w_in, m_b_in, m_g_q, m_w_uq, m_g_kv, m_w_ukv, m_w_oa, m_sgu_ln_g, m_sgu_ln_b, m_w_s, m_b_s, m_w_ob, m_w_out, m_ln_g, m_ln_b, v_w_in, v_b_in, v_g_q, v_w_uq, v_g_kv, v_w_ukv, v_w_oa, v_sgu_ln_g, v_sgu_ln_b, v_w_s, v_b_s, v_w_ob, v_w_out, v_ln_g, v_ln_b):
    w_uq2 = w_uq[0].reshape(Q_LORA_RANK // N_DEV, MLA_HEADS * QK_HEAD_DIM)
    wc, wq, *own_shards = _gather_first(w_in, w_uq2, w_oa, w_ob, w_out)
    partials = _local_step(x[0], positions, loss_target[0], wc, wq, own_shards, b_in, g_q, g_kv, w_ukv, sgu_ln_g,
                           sgu_ln_b, w_s, b_s, ln_g, ln_b)
    weights = dict(w_in=w_in, b_in=b_in, g_q=g_q, w_uq=w_uq, g_kv=g_kv, w_ukv=w_ukv, w_oa=w_oa, sgu_ln_g=sgu_ln_g,
                   sgu_ln_b=sgu_ln_b, w_s=w_s, b_s=b_s, w_ob=w_ob, w_out=w_out, ln_g=ln_g, ln_b=ln_b)
    moms = dict(w_in=m_w_in, b_in=m_b_in, g_q=m_g_q, w_uq=m_w_uq, g_kv=m_g_kv, w_ukv=m_w_ukv, w_oa=m_w_oa,
                sgu_ln_g=m_sgu_ln_g, sgu_ln_b=m_sgu_ln_b, w_s=m_w_s, b_s=m_b_s, w_ob=m_w_ob, w_out=m_w_out,
                ln_g=m_ln_g, ln_b=m_ln_b)
    vars_ = dict(w_in=v_w_in, b_in=v_b_in, g_q=v_g_q, w_uq=v_w_uq, g_kv=v_g_kv, w_ukv=v_w_ukv, w_oa=v_w_oa,
                 sgu_ln_g=v_sgu_ln_g, sgu_ln_b=v_sgu_ln_b, w_s=v_w_s, b_s=v_b_s, w_ob=v_w_ob, w_out=v_w_out,
                 ln_g=v_ln_g, ln_b=v_ln_b)
    return _reduce_and_update(partials, weights, moms, vars_)


def _local_step(x2, positions, tgt, wc, wq, own_shards, b_in, g_q, g_kv, w_ukv, sgu_ln_g, sgu_ln_b, w_s, b_s, ln_g, ln_b):
    win_b, oa_b, ob_b, out_b = own_shards
    ba, bb, bc = _to_segments(b_in)
    w_ukv_bf = w_ukv[0].astype(BF16)
    wkn = jnp.pad(w_ukv_bf[:, :, :QK_NOPE_DIM], ((0, 0), (0, 0), (0, HEAD_PAD - QK_NOPE_DIM))).reshape(KV_LORA_RANK, -1)
    wv = jnp.pad(w_ukv_bf[:, :, QK_NOPE_DIM:], ((0, 0), (0, 0), (0, HEAD_PAD - V_HEAD_DIM))).reshape(KV_LORA_RANK, -1)
    gq = jnp.pad(g_q, ((0, 0), (0, CQ_PAD - Q_LORA_RANK)))
    bias_full = jnp.repeat(b_s[0].T, SGU_GROUP_DIM, axis=1)
    w_s3 = w_s[0]
    w_st3 = jnp.swapaxes(w_s3, 1, 2)

    inv_freq = ROPE_THETA ** (-jnp.arange(0, QK_ROPE_DIM, 2, dtype=F32) / QK_ROPE_DIM)
    invf_lane = jnp.concatenate([jnp.zeros((ROPE_LO,), F32), inv_freq, inv_freq,
                                 jnp.zeros((LANES - ROPE_HI,), F32)]).reshape(1, LANES)
    c_t, sa_t, sb_t = _rope_tables(positions.reshape(SEQ, 1), invf_lane)

    x_bf, xt_bf = _cast_x(x2)
    h_c = _mm(x_bf, wc, bias=bc, tm=512, tn=SEG_C, name="in_proj_c")
    q, k, kt, vx, vxt = _mla_prep(h_c, gq, g_kv, wq, wkn, wv, c_t, sa_t, sb_t)
    o, lse, (g_in,) = _attn_fwd(q, kt, vx, (win_b,))
    wa, wb = _assemble_in(g_in)
    h_a, gathered = _mm(x_bf, wa, bias=ba, own=(oa_b, ob_b, out_b), tm=512, tn=SEG_A // 2, name="in_proj_a")
    h_b = _mm(x_bf, wb, bias=bb, tm=512, tn=SEG_B // 2, name="in_proj_b")
    y_b = _sgu_fwd(h_b, sgu_ln_g, sgu_ln_b, w_s3, bias_full)
    w_oa_f, w_ob_f, w_out_f = _assemble_out(*gathered)

    (loss_row, dx_res, dh_a, d_o, d_yb, p_oa, p_ob, p_out, d_lng, d_lnb, d_ba) = _merge(
        x2, o, h_a, y_b, tgt, w_oa_f, w_ob_f, w_out_f, ln_g, ln_b)
    (dh_b, d_ws, d_bs_t, d_slg, d_slb, d_bb), landed_o = _sgu_bwd(h_b, d_yb, sgu_ln_g, sgu_ln_b, w_s3, w_st3, bias_full,
                                                                 (p_oa, p_ob, p_out))
    d_wa = _mm(xt_bf, dh_a, out_dtype=BF16, tm=512, tn=512, name="dw_in_a")
    d_wb = _mm(xt_bf, dh_b, out_dtype=BF16, tm=512, tn=512, name="dw_in_b")
    dq, dk, dv, landed_in = _attn_bwd(q, kt, k, vxt, d_o, o, lse, (_to_parts(d_wa, d_wb),))
    landed = (*landed_in, *landed_o)
    dh_c, p_uq, d_wkn, d_wv, d_gq, d_gkv, d_bc = _mla_bwd(dq, dk, dv, h_c, gq, g_kv, wq, wkn, wv, c_t, sa_t, sb_t)
    d_wc = _mm(xt_bf, dh_c, out_dtype=BF16, tm=512, tn=SEG_C, name="dw_in_c")

    dx = _mm(dh_a, wa, tb=True, add=dx_res, tm=512, tn=D_MODEL, name="dx_a")
    dx = _mm(dh_b, wb, tb=True, add=dx, tm=512, tn=D_MODEL, name="dx_b")
    dx = _mm(dh_c, wc, tb=True, add=dx, tm=512, tn=D_MODEL, name="dx_c")

    p_b_in = _from_segments(d_ba, d_bb, d_bc)
    p_w_ukv = jnp.concatenate([d_wkn.reshape(KV_LORA_RANK, MLA_HEADS, HEAD_PAD)[:, :, :QK_NOPE_DIM],
                               d_wv.reshape(KV_LORA_RANK, MLA_HEADS, HEAD_PAD)[:, :, :V_HEAD_DIM]], axis=-1)
    p_g_q = d_gq[:, :Q_LORA_RANK]
    p_b_s = d_bs_t[:, :SGU_GROUPS].T
    replicated = [p_b_in, p_g_q, d_gkv, p_w_ukv, d_slg, d_slb, d_ws, p_b_s, d_lng, d_lnb]
    return loss_row, dx, landed, d_wc, p_uq, replicated


_NAMES = ["w_in", "b_in", "g_q", "w_uq", "g_kv", "w_ukv", "w_oa", "sgu_ln_g", "sgu_ln_b", "w_s", "b_s", "w_ob",
          "w_out", "ln_g", "ln_b"]
_REPLICATED = ["b_in", "g_q", "g_kv", "w_ukv", "sgu_ln_g", "sgu_ln_b", "w_s", "b_s", "ln_g", "ln_b"]


def _reduce_and_update(partials, weights, moms, vars_):
    loss_row, dx, landed, d_wc, p_uq, replicated = partials
    rep_flat = jnp.concatenate([a.reshape(-1) for a in replicated] + [loss_row[0, :1]])
    rep_flat = jnp.pad(rep_flat, (0, N_DEV * PACK_R_ROWS * LANES - rep_flat.size))
    g_in, g_uq, g_oa, g_ob, g_out, rep_all = _finish_grads(d_wc, p_uq, rep_flat.reshape(N_DEV, PACK_R_ROWS, LANES), landed)
    rep_sum = rep_all.reshape(-1)
    grads, pos = dict(w_in=g_in, w_uq=g_uq, w_oa=g_oa, w_ob=g_ob, w_out=g_out), 0
    for nm in _REPLICATED:
        grads[nm] = rep_sum[pos:pos + weights[nm].size]
        pos += weights[nm].size
    loss = rep_sum[pos]
    grads = {nm: grads[nm].reshape(weights[nm].shape) for nm in _NAMES}
    deltas, new_m, new_v = _adamw_all([weights[nm] for nm in _NAMES], [grads[nm] for nm in _NAMES],
                                      [moms[nm] for nm in _NAMES], [vars_[nm] for nm in _NAMES])
    return (loss, dx.reshape(1, SEQ, D_MODEL), *[grads[nm] for nm in _NAMES], *deltas, *new_m, *new_v)
```

```python
import math

import jax
import jax.numpy as jnp
from jax import lax
from jax.experimental import pallas as pl
from jax.experimental.pallas import tpu as pltpu

F32 = jnp.float32
BF16 = jnp.bfloat16

D_MODEL = 1024
SEQ = 2048
N_DEV = 8
MLA_HEADS = 8
Q_LORA_RANK = 384
KV_LORA_RANK = 128
QK_NOPE_DIM = 64
QK_ROPE_DIM = 32
V_HEAD_DIM = 64
QK_HEAD_DIM = QK_NOPE_DIM + QK_ROPE_DIM
MLA_WIDTH = MLA_HEADS * V_HEAD_DIM
ROPE_THETA = 10000.0
SGU_GROUPS = 8
SGU_GROUP_DIM = 64
SGU_WIDTH = SGU_GROUPS * SGU_GROUP_DIM
CHUNK = 128
RMS_EPS = 1e-6
LN_EPS = 1e-5
DN_ALPHA = 2.0 ** 0.25
IN_WIDTH = 4640
ATTN_SCALE = QK_HEAD_DIM ** -0.5

ADAM_LR = 0.001
ADAM_B1 = 0.9
ADAM_B2 = 0.999
ADAM_EPS = 1e-08
ADAM_WD = 0.01
ADAM_STEP = 10

LANES = 128
HEAD_PAD = 128
ROPE_LO = QK_NOPE_DIM
ROPE_MID = ROPE_LO + QK_ROPE_DIM // 2
ROPE_HI = ROPE_LO + QK_ROPE_DIM
CQ_PAD = 512

SEG_A = 2560
SEG_B = 1536
SEG_C = 768

PACK_R_ROWS = 272
VMEM_BIG = 56 * 1024 * 1024
VMEM_MID = 40 * 1024 * 1024


def _sigmoid(x):
    return 1.0 / (1.0 + jnp.exp(-x))


def _gelu_and_grad(x):
    c0 = math.sqrt(2.0 / math.pi)
    x2 = x * x
    t = jnp.tanh(c0 * (x + 0.044715 * x * x2))
    g = 0.5 * x * (1.0 + t)
    dg = 0.5 * (1.0 + t) + 0.5 * x * (1.0 - t * t) * (c0 * (1.0 + 3.0 * 0.044715 * x2))
    return g, dg


def _dot(a, b, dims):
    return lax.dot_general(a, b, (dims, ((), ())), preferred_element_type=F32)


_NN = ((1,), (0,))
_NT = ((1,), (1,))
_TN = ((0,), (0,))


def _store_grad(dh_ref, db_ref, col, val):
    cols = slice(col, col + val.shape[1])
    dh_ref[:, cols] = val.astype(BF16)
    db_ref[:, cols] += jnp.sum(val, axis=0, keepdims=True)


def _mm(a, b, *, tb=False, bias=None, add=None, out_dtype=F32, own=(), tm, tn, name):
    m, k = a.shape
    n = b.shape[0] if tb else b.shape[1]
    assert m % tm == 0 and n % tn == 0
    dims = _NT if tb else _NN
    nown = len(own)
    nm = m // tm
    nsteps = (n // tn) * nm

    def body(*refs):
        a_ref, b_ref = refs[0], refs[1]
        pos = 2
        r = _dot(a_ref[...], b_ref[...], dims)
        if bias is not None:
            r = r + refs[pos][...]; pos += 1
        if add is not None:
            r = r + refs[pos][...]; pos += 1
        own_refs = refs[pos:pos + nown]; pos += nown
        refs[pos][...] = r.astype(out_dtype)
        if nown:
            gat_refs = refs[pos + 1:pos + 1 + nown]
            send_sems, recv_sems, local_sems = refs[pos + 1 + nown:]
            _gather_behind(own_refs, gat_refs, send_sems, recv_sems, local_sems,
                           pl.program_id(0) * nm + pl.program_id(1), nsteps - 2, nsteps - 1)

    b_spec = pl.BlockSpec((tn, k), lambda j, i: (j, 0)) if tb else pl.BlockSpec((k, tn), lambda j, i: (0, j))
    in_specs, args = [pl.BlockSpec((tm, k), lambda j, i: (i, 0)), b_spec], [a, b]
    if bias is not None:
        in_specs.append(pl.BlockSpec((1, tn), lambda j, i: (0, j))); args.append(bias)
    if add is not None:
        in_specs.append(pl.BlockSpec((tm, tn), lambda j, i: (i, j))); args.append(add)
    hbm = pl.BlockSpec(memory_space=pl.ANY)
    res = pl.pallas_call(
        body, name=name, grid=(n // tn, nm), in_specs=in_specs + [hbm] * nown,
        out_specs=[pl.BlockSpec((tm, tn), lambda j, i: (i, j))] + [hbm] * nown,
        out_shape=[jax.ShapeDtypeStruct((m, n), out_dtype)]
        + [jax.ShapeDtypeStruct((N_DEV,) + o.shape, o.dtype) for o in own],
        scratch_shapes=_exchange_sems(nown) if nown else [],
        compiler_params=pltpu.CompilerParams(dimension_semantics=("arbitrary", "arbitrary"), vmem_limit_bytes=VMEM_BIG),
    )(*args, *own)
    return (res[0], res[1:]) if nown else res[0]


def _rope(x, c, sa, sb):
    return x * c + pltpu.roll(x, LANES - 16, 1) * sa + pltpu.roll(x, 16, 1) * sb


def _rope_t(dy, c, sa, sb):
    return dy * c + pltpu.roll(dy * sa, 16, 1) + pltpu.roll(dy * sb, LANES - 16, 1)


def _mla_prep(h_c, gq, gkv, wq, wkn, wvx, c_t, sa_t, sb_t):
    tm = 256
    hw = MLA_HEADS * HEAD_PAD

    def body(cq_ref, ckv_ref, kpe_ref, gq_ref, gkv_ref, wq_ref, wkn_ref, wvx_ref, c_ref, sa_ref, sb_ref,
             q_ref, k_ref, kt_ref, vx_ref, vxt_ref):
        c, sa, sb = c_ref[...], sa_ref[...], sb_ref[...]
        cq = cq_ref[...]
        rq = lax.rsqrt(jnp.sum(cq * cq, axis=1, keepdims=True) * (1.0 / Q_LORA_RANK) + RMS_EPS)
        cqn = ((cq * rq) * gq_ref[...]).astype(BF16)
        qall = _dot(cqn, wq_ref[...], _NN)
        for h in range(MLA_HEADS):
            sl = slice(HEAD_PAD * h, HEAD_PAD * (h + 1))
            q_ref[:, sl] = (_rope(qall[:, sl], c, sa, sb) * ATTN_SCALE).astype(BF16)
        ckv = ckv_ref[...]
        rkv = lax.rsqrt(jnp.sum(ckv * ckv, axis=1, keepdims=True) * (1.0 / KV_LORA_RANK) + RMS_EPS)
        ckvn = ((ckv * rkv) * gkv_ref[...]).astype(BF16)
        knall = _dot(ckvn, wkn_ref[...], _NN)
        vall = _dot(ckvn, wvx_ref[...], _NN)
        kper = _rope(kpe_ref[...], c, sa, sb)
        ones_half = (lax.broadcasted_iota(jnp.int32, (tm, HEAD_PAD), 1) >= V_HEAD_DIM).astype(F32)
        for h in range(MLA_HEADS):
            sl = slice(HEAD_PAD * h, HEAD_PAD * (h + 1))
            kh = knall[:, sl] + kper
            vh = vall[:, sl] + ones_half
            k_ref[:, sl] = kh.astype(BF16)
            kt_ref[sl, :] = kh.T.astype(BF16)
            vx_ref[:, sl] = vh.astype(BF16)
            vxt_ref[sl, :] = vh.T.astype(BF16)

    full = lambda shape: pl.BlockSpec(shape, lambda i: (0, 0))
    tab = pl.BlockSpec((tm, LANES), lambda i: (i, 0))
    row = pl.BlockSpec((tm, hw), lambda i: (i, 0))
    col = pl.BlockSpec((hw, tm), lambda i: (0, i))
    return pl.pallas_call(
        body, name="mla_prep", grid=(SEQ // tm,),
        in_specs=[pl.BlockSpec((tm, CQ_PAD), lambda i: (i, 0)),
                  pl.BlockSpec((tm, LANES), lambda i: (i, CQ_PAD // LANES)),
                  pl.BlockSpec((tm, LANES), lambda i: (i, CQ_PAD // LANES + 1)),
                  full((1, CQ_PAD)), full((1, KV_LORA_RANK)),
                  full((CQ_PAD, hw)), full((KV_LORA_RANK, hw)), full((KV_LORA_RANK, hw)), tab, tab, tab],
        out_specs=[row, row, col, row, col],
        out_shape=[jax.ShapeDtypeStruct((SEQ, hw), BF16), jax.ShapeDtypeStruct((SEQ, hw), BF16),
                   jax.ShapeDtypeStruct((hw, SEQ), BF16), jax.ShapeDtypeStruct((SEQ, hw), BF16),
                   jax.ShapeDtypeStruct((hw, SEQ), BF16)],
        compiler_params=pltpu.CompilerParams(dimension_semantics=("arbitrary",), vmem_limit_bytes=VMEM_MID),
    )(h_c, h_c, h_c, gq, gkv, wq, wkn, wvx, c_t, sa_t, sb_t)


ATT_T = 512
ATT_STRIP = 64


def _attn_fwd(q, kt, vx, own):
    t, rs = ATT_T, ATT_STRIP
    nown = len(own)
    nq = SEQ // t
    nsteps = (MLA_HEADS // 2) * nq

    def body(q_ref, kt_ref, vx_ref, *rest):
        own_refs, (o_ref, l_ref), gat_refs = rest[:nown], rest[nown:nown + 2], rest[nown + 2:2 * nown + 2]
        s_scr, p_scr, m_scr, a_scr, acc_scr, send_sems, recv_sems, local_sems = rest[2 * nown + 2:]
        qi = pl.program_id(1)
        _gather_behind(own_refs, gat_refs, send_sems, recv_sems, local_sems, pl.program_id(0) * nq + qi,
                       nsteps - 2, nsteps - 1)
        lane = lax.broadcasted_iota(jnp.int32, (t, LANES), 1)
        m_scr[...] = jnp.full((2, t, LANES), -1e30, F32)
        acc_scr[...] = jnp.zeros((2, t, LANES), F32)

        def block(j, masked):
            off = pl.multiple_of(j * t, t)
            for a in range(2):
                sl = slice(HEAD_PAD * a, HEAD_PAD * (a + 1))
                s_scr[a] = _dot(q_ref[:, sl], kt_ref[sl, pl.ds(off, t)], _NN)
                for r in range(t // rs):
                    rows = slice(rs * r, rs * (r + 1))
                    s = s_scr[a, rows, :]
                    if masked:
                        rowi = lax.broadcasted_iota(jnp.int32, (rs, t), 0) + rs * r
                        coli = lax.broadcasted_iota(jnp.int32, (rs, t), 1)
                        s = jnp.where(coli <= rowi, s, -1e30)
                    m_old = m_scr[a, rows, :]
                    m_new = jnp.maximum(m_old, jnp.max(s, axis=1, keepdims=True))
                    p_scr[a, rows, :] = jnp.exp(s - m_new[:, :1]).astype(BF16)
                    a_scr[a, rows, :] = jnp.exp(m_old - m_new)
                    m_scr[a, rows, :] = m_new
                acc_scr[a] = acc_scr[a] * a_scr[a] + _dot(p_scr[a], vx_ref[pl.ds(off, t), sl], _NN)

        def step(j, carry):
            block(j, False)
            return carry
        lax.fori_loop(0, qi, step, 0)
        block(qi, True)
        res = []
        for a in range(2):
            acc = acc_scr[a]
            l = acc[:, V_HEAD_DIM:V_HEAD_DIM + 1]
            res.append((acc / l, m_scr[a] + jnp.log(l)))
        o_ref[...] = jnp.where(lane < V_HEAD_DIM, res[0][0], pltpu.roll(res[1][0], V_HEAD_DIM, 1))
        l_ref[...] = jnp.where(lane < V_HEAD_DIM, res[0][1], res[1][1])

    hbm = pl.BlockSpec(memory_space=pl.ANY)
    res = pl.pallas_call(
        body, name="attn_fwd", grid=(MLA_HEADS // 2, nq),
        in_specs=[pl.BlockSpec((t, 2 * HEAD_PAD), lambda p, i: (i, p)),
                  pl.BlockSpec((2 * HEAD_PAD, SEQ), lambda p, i: (p, 0)),
                  pl.BlockSpec((SEQ, 2 * HEAD_PAD), lambda p, i: (0, p))] + [hbm] * nown,
        out_specs=[pl.BlockSpec((t, LANES), lambda p, i: (i, p)),
                   pl.BlockSpec((t, LANES), lambda p, i: (i, p))] + [hbm] * nown,
        out_shape=[jax.ShapeDtypeStruct((SEQ, MLA_WIDTH), F32), jax.ShapeDtypeStruct((SEQ, MLA_WIDTH), F32)]
        + [jax.ShapeDtypeStruct((N_DEV,) + a.shape, a.dtype) for a in own],
        scratch_shapes=[pltpu.VMEM((2, t, t), F32), pltpu.VMEM((2, t, t), BF16), pltpu.VMEM((2, t, LANES), F32),
                        pltpu.VMEM((2, t, LANES), F32), pltpu.VMEM((2, t, LANES), F32)] + _exchange_sems(nown),
        compiler_params=pltpu.CompilerParams(dimension_semantics=("arbitrary", "arbitrary"), vmem_limit_bytes=VMEM_MID),
    )(q, kt, vx, *own)
    return res[0], res[1], res[2:]


def _exchange_parts(parts, lands, send_sems, recv_sems, local_sems):
    x, y, c = _mesh_pos()
    me = 4 * x + 2 * y + c
    peers = [(x, y, 1 - c), (1 - x, y, c), (x, 1 - y, c), (1 - x, 1 - y, c),
             (1 - x, y, 1 - c), (x, 1 - y, 1 - c), (1 - x, 1 - y, 1 - c)]
    remote, local = [], []
    for a, (part, land) in enumerate(zip(parts, lands)):
        for k, peer in enumerate(peers):
            t = 4 * peer[0] + 2 * peer[1] + peer[2]
            remote.append(_remote(part.at[t], land.at[me], send_sems, recv_sems, 7 * a + k, peer))
        local.append(pltpu.make_async_copy(part.at[me], land.at[me], local_sems.at[a]))
    return remote, local


def _exchange_start(first_step, exchange):
    remote, local = exchange

    @pl.when(first_step)
    def _():
        for cp in remote + local:
            cp.start()


def _exchange_finish(last_step, exchange):
    remote, local = exchange

    @pl.when(last_step)
    def _():
        for cp in remote:
            cp.wait_recv()
        for cp in remote:
            cp.wait_send()
        for cp in local:
            cp.wait()


def _exchange_sems(npart):
    return [pltpu.SemaphoreType.DMA((7 * npart,)), pltpu.SemaphoreType.DMA((7 * npart,)),
            pltpu.SemaphoreType.DMA((npart,))]


def _attn_bwd(q, kt, k, vxt, d_o, o, lse, parts):
    t, rs = ATT_T, ATT_STRIP
    nq = SEQ // t
    npart = len(parts)
    nsteps = MLA_HEADS // 2

    def body(q_ref, kt_ref, k_ref, vxt_ref, do_ref, o_ref, l_ref, *rest):
        part_refs, rest = rest[:npart], rest[npart:]
        dq_ref, dk_ref, dv_ref = rest[:3]
        land_refs, rest = rest[3:3 + npart], rest[3 + npart:]
        s_scr, dp_scr, p_scr, ds_scr, st_scr, send_sems, recv_sems, local_sems = rest
        exchange = _exchange_parts(part_refs, land_refs, send_sems, recv_sems, local_sems)
        _exchange_start(pl.program_id(0) == 0, exchange)
        dk_ref[...] = jnp.zeros_like(dk_ref)
        dv_ref[...] = jnp.zeros_like(dv_ref)
        lane = lax.broadcasted_iota(jnp.int32, (t, LANES), 1)

        def qtile(i, carry):
            ioff = pl.multiple_of(i * t, t)
            do_i = do_ref[pl.ds(ioff, t), :]
            o_i = o_ref[pl.ds(ioff, t), :]
            l_i = l_ref[pl.ds(ioff, t), :]
            for a in range(2):
                sl = slice(HEAD_PAD * a, HEAD_PAD * (a + 1))
                sel = (lane < V_HEAD_DIM) if a == 0 else (lane >= V_HEAD_DIM)
                doa = jnp.where(sel, do_i, 0.0)
                oa = o_i
                if a == 1:
                    doa = pltpu.roll(doa, V_HEAD_DIM, 1)
                    oa = pltpu.roll(o_i, V_HEAD_DIM, 1)
                st_scr[0] = jnp.broadcast_to(jnp.sum(doa * oa, axis=1, keepdims=True), (t, LANES))
                st_scr[1] = jnp.broadcast_to(l_i[:, V_HEAD_DIM * a:V_HEAD_DIM * a + 1], (t, LANES))
                doa_bf = doa.astype(BF16)
                qa = q_ref[pl.ds(ioff, t), sl]

                def block(j, masked, dq_acc, sl=sl, qa=qa, doa_bf=doa_bf):
                    joff = pl.multiple_of(j * t, t)
                    s_scr[...] = _dot(qa, kt_ref[sl, pl.ds(joff, t)], _NN)
                    dp_scr[...] = _dot(doa_bf, vxt_ref[sl, pl.ds(joff, t)], _NN)
                    for r in range(t // rs):
                        rows = slice(rs * r, rs * (r + 1))
                        p = jnp.exp(s_scr[rows, :] - st_scr[1, rows, :1])
                        if masked:
                            rowi = lax.broadcasted_iota(jnp.int32, (rs, t), 0) + rs * r
                            coli = lax.broadcasted_iota(jnp.int32, (rs, t), 1)
                            p = jnp.where(coli <= rowi, p, 0.0)
                        p_scr[rows, :] = p.astype(BF16)
                        ds_scr[rows, :] = (p * (dp_scr[rows, :] - st_scr[0, rows, :1])).astype(BF16)
                    dk_ref[pl.ds(joff, t), sl] += _dot(ds_scr[...], qa, _TN)
                    dv_ref[pl.ds(joff, t), sl] += _dot(p_scr[...], doa_bf, _TN)
                    return dq_acc + _dot(ds_scr[...], k_ref[pl.ds(joff, t), sl], _NN)

                dq_acc = lax.fori_loop(0, i, lambda j, acc: block(j, False, acc), jnp.zeros((t, HEAD_PAD), F32))
                dq_ref[pl.ds(ioff, t), sl] = block(i, True, dq_acc)
            return carry

        lax.fori_loop(0, nq, qtile, 0)
        _exchange_finish(pl.program_id(0) == nsteps - 1, exchange)

    hw = MLA_HEADS * HEAD_PAD
    wide = pl.BlockSpec((SEQ, 2 * HEAD_PAD), lambda p: (0, p))
    wide_t = pl.BlockSpec((2 * HEAD_PAD, SEQ), lambda p: (p, 0))
    narrow = pl.BlockSpec((SEQ, LANES), lambda p: (0, p))
    hbm = pl.BlockSpec(memory_space=pl.ANY)
    res = pl.pallas_call(
        body, name="attn_bwd", grid=(nsteps,),
        in_specs=[wide, wide_t, wide, wide_t, narrow, narrow, narrow] + [hbm] * npart,
        out_specs=[wide, wide, wide] + [hbm] * npart,
        out_shape=[jax.ShapeDtypeStruct((SEQ, hw), F32)] * 3 + [jax.ShapeDtypeStruct(p.shape, p.dtype) for p in parts],
        scratch_shapes=[pltpu.VMEM((t, t), F32), pltpu.VMEM((t, t), F32), pltpu.VMEM((t, t), BF16),
                        pltpu.VMEM((t, t), BF16), pltpu.VMEM((2, t, LANES), F32)] + _exchange_sems(npart),
        compiler_params=pltpu.CompilerParams(dimension_semantics=("arbitrary",), vmem_limit_bytes=VMEM_BIG),
    )(q, kt, k, vxt, d_o, o, lse, *parts)
    return res[0], res[1], res[2], res[3:]


def _sgu_math(u, v, zb, lg, lb, ws_ref, bias):
    ug, dug = _gelu_and_grad(u)
    vg, dvg = _gelu_and_grad(v)
    mu = jnp.mean(vg, axis=1, keepdims=True)
    xc = vg - mu
    rstd = lax.rsqrt(jnp.mean(xc * xc, axis=1, keepdims=True) + LN_EPS)
    xh = xc * rstd
    vn_bf = (xh * lg + lb).astype(BF16)
    grp = lax.broadcasted_iota(jnp.int32, (CHUNK, SGU_WIDTH), 1) // SGU_GROUP_DIM
    r_i = lax.broadcasted_iota(jnp.int32, (CHUNK, CHUNK), 0)
    c_i = lax.broadcasted_iota(jnp.int32, (CHUNK, CHUNK), 1)
    tri, tri_t = r_i >= c_i, r_i <= c_i
    mixed = bias
    for g in range(SGU_GROUPS):
        wt = jnp.where(tri, ws_ref[g], 0.0).astype(BF16)
        mixed = mixed + jnp.where(grp == g, _dot(wt, vn_bf, _NN), 0.0)
    sb = _sigmoid(zb)
    return ug, dug, dvg, rstd, xh, vn_bf, grp, tri, tri_t, mixed, sb


def _sgu_fwd(h_b, lg, lb, w_s, bias_full):
    def body(u_ref, v_ref, zb_ref, lg_ref, lb_ref, ws_ref, bias_ref, yb_ref):
        zb = zb_ref[...]
        ug, _, _, _, _, _, _, _, _, mixed, sb = _sgu_math(u_ref[...], v_ref[...], zb, lg_ref[...], lb_ref[...],
                                                       ws_ref, bias_ref[...])
        yb_ref[...] = (ug * mixed) * (zb * sb)

    blk = lambda c: pl.BlockSpec((CHUNK, SGU_WIDTH), lambda i, c=c: (i, c))
    full2 = lambda shape: pl.BlockSpec(shape, lambda i: (0, 0))
    return pl.pallas_call(
        body, name="sgu_fwd", grid=(SEQ // CHUNK,),
        in_specs=[blk(0), blk(1), blk(2), full2((1, SGU_WIDTH)), full2((1, SGU_WIDTH)),
                  pl.BlockSpec((SGU_GROUPS, CHUNK, CHUNK), lambda i: (0, 0, 0)), full2((CHUNK, SGU_WIDTH))],
        out_specs=pl.BlockSpec((CHUNK, SGU_WIDTH), lambda i: (i, 0)),
        out_shape=jax.ShapeDtypeStruct((SEQ, SGU_WIDTH), F32),
        compiler_params=pltpu.CompilerParams(dimension_semantics=("arbitrary",)),
    )(h_b, h_b, h_b, lg, lb, w_s, bias_full)


def _sgu_bwd(h_b, d_yb, lg, lb, w_s, w_st, bias_full, parts):
    nsteps = SEQ // CHUNK
    npart = len(parts)

    def body(u_ref, v_ref, zb_ref, dyb_ref, lg_ref, lb_ref, ws_ref, wst_ref, bias_ref, *rest):
        part_refs, rest = rest[:npart], rest[npart:]
        dhb_ref, dws_ref, dbs_ref, dlg_ref, dlb_ref, dbb_ref = rest[:6]
        land_refs, (dbias_acc, send_sems, recv_sems, local_sems) = rest[6:6 + npart], rest[6 + npart:]
        step = pl.program_id(0)
        exchange = _exchange_parts(part_refs, land_refs, send_sems, recv_sems, local_sems)
        _exchange_start(step == 0, exchange)

        @pl.when(step == 0)
        def _():
            dbb_ref[...] = jnp.zeros_like(dbb_ref)
            dws_ref[...] = jnp.zeros_like(dws_ref)
            dlg_ref[...] = jnp.zeros_like(dlg_ref)
            dlb_ref[...] = jnp.zeros_like(dlb_ref)
            dbias_acc[...] = jnp.zeros_like(dbias_acc)

        zb = zb_ref[...]
        lg = lg_ref[...]
        ug, dug, dvg, rstd, xh, vn_bf, grp, tri, tri_t, mixed, sb = _sgu_math(
            u_ref[...], v_ref[...], zb, lg, lb_ref[...], ws_ref, bias_ref[...])
        dyb = dyb_ref[...]
        dsgu = dyb * (zb * sb)
        dzb = dyb * (ug * mixed) * (sb * (1.0 + zb * (1.0 - sb)))
        du = dsgu * mixed * dug
        dmixed = dsgu * ug
        dbias_acc[...] += dmixed
        dvn = jnp.zeros((CHUNK, SGU_WIDTH), F32)
        for g in range(SGU_GROUPS):
            dm_g = jnp.where(grp == g, dmixed, 0.0).astype(BF16)
            wtt = jnp.where(tri_t, wst_ref[g], 0.0).astype(BF16)
            dvn = dvn + _dot(wtt, dm_g, _NN)
            dws_ref[g] += jnp.where(tri, _dot(dm_g, vn_bf, _NT), 0.0)
        dlg_ref[...] += jnp.sum(dvn * xh, axis=0, keepdims=True)
        dlb_ref[...] += jnp.sum(dvn, axis=0, keepdims=True)
        dxh = dvn * lg
        dvgel = rstd * (dxh - jnp.mean(dxh, axis=1, keepdims=True) - xh * jnp.mean(dxh * xh, axis=1, keepdims=True))
        _store_grad(dhb_ref, dbb_ref, 0, du)
        _store_grad(dhb_ref, dbb_ref, SGU_WIDTH, dvgel * dvg)
        _store_grad(dhb_ref, dbb_ref, 2 * SGU_WIDTH, dzb)

        @pl.when(step == nsteps - 1)
        def _():
            acc = dbias_acc[...]
            lane = lax.broadcasted_iota(jnp.int32, (CHUNK, LANES), 1)
            out = jnp.zeros((CHUNK, LANES), F32)
            for g in range(SGU_GROUPS):
                sg = jnp.sum(jnp.where(grp == g, acc, 0.0), axis=1, keepdims=True)
                out = jnp.where(lane == g, sg, out)
            dbs_ref[...] = out

        _exchange_finish(step == nsteps - 1, exchange)

    blk = lambda c: pl.BlockSpec((CHUNK, SGU_WIDTH), lambda i, c=c: (i, c))
    full2 = lambda shape: pl.BlockSpec(shape, lambda i: (0, 0))
    full3 = pl.BlockSpec((SGU_GROUPS, CHUNK, CHUNK), lambda i: (0, 0, 0))
    hbm = pl.BlockSpec(memory_space=pl.ANY)
    res = pl.pallas_call(
        body, name="sgu_bwd", grid=(nsteps,),
        in_specs=[blk(0), blk(1), blk(2), pl.BlockSpec((CHUNK, SGU_WIDTH), lambda i: (i, 0)),
                  full2((1, SGU_WIDTH)), full2((1, SGU_WIDTH)), full3, full3, full2((CHUNK, SGU_WIDTH))] + [hbm] * npart,
        out_specs=[pl.BlockSpec((CHUNK, SEG_B), lambda i: (i, 0)), full3, full2((CHUNK, LANES)),
                   full2((1, SGU_WIDTH)), full2((1, SGU_WIDTH)), full2((1, SEG_B))] + [hbm] * npart,
        out_shape=[jax.ShapeDtypeStruct((SEQ, SEG_B), BF16),
                   jax.ShapeDtypeStruct((SGU_GROUPS, CHUNK, CHUNK), F32),
                   jax.ShapeDtypeStruct((CHUNK, LANES), F32),
                   jax.ShapeDtypeStruct((1, SGU_WIDTH), F32), jax.ShapeDtypeStruct((1, SGU_WIDTH), F32),
                   jax.ShapeDtypeStruct((1, SEG_B), F32)] + [jax.ShapeDtypeStruct(p.shape, p.dtype) for p in parts],
        scratch_shapes=[pltpu.VMEM((CHUNK, SGU_WIDTH), F32)] + _exchange_sems(npart),
        compiler_params=pltpu.CompilerParams(dimension_semantics=("arbitrary",)),
    )(h_b, h_b, h_b, d_yb, lg, lb, w_s, w_st, bias_full, *parts)
    return res[:6], res[6:]


def _merge(x, o, h_a, y_b, target, w_oa, w_ob, w_out, ln_g, ln_b):
    tm = 256
    nsteps = SEQ // tm

    def body(x_ref, o_ref, ga_ref, gb_ref, za_ref, yb_ref, tgt_ref, woa_ref, wob_ref, wout_ref, lng_ref, lnb_ref,
             loss_ref, dxr_ref, dha_ref, do_ref, dyb_ref, poa_ref, pob_ref, pout_ref, dlng_ref, dlnb_ref, dba_ref,
             dwoa_ref, dwob_ref, dwout_ref):
        step = pl.program_id(0)

        @pl.when(step == 0)
        def _():
            for r in (loss_ref, dwoa_ref, dwob_ref, dwout_ref, dlng_ref, dlnb_ref, dba_ref):
                r[...] = jnp.zeros_like(r)

        o = o_ref[...]
        za = za_ref[...]
        sa = _sigmoid(za)
        ya_bf = (o * (za * sa)).astype(BF16)
        yb_bf = yb_ref[...].astype(BF16)
        woa, wob, wout = woa_ref[...], wob_ref[...], wout_ref[...]
        pa = _dot(ya_bf, woa, _NN)
        pb = _dot(yb_bf, wob, _NN)
        sga = _sigmoid(ga_ref[...])
        sgb = _sigmoid(gb_ref[...])
        merged_bf = (sga * pa + sgb * pb).astype(BF16)
        r = DN_ALPHA * x_ref[...] + _dot(merged_bf, wout, _NN)
        mu = jnp.mean(r, axis=1, keepdims=True)
        rc = r - mu
        rstd = lax.rsqrt(jnp.mean(rc * rc, axis=1, keepdims=True) + LN_EPS)
        xh = rc * rstd
        lng = lng_ref[...]
        y = xh * lng + lnb_ref[...]
        e = y - tgt_ref[...]
        loss_ref[...] += 0.5 * jnp.sum(jnp.sum(e * e, axis=1, keepdims=True) * (1.0 / D_MODEL), axis=0, keepdims=True)

        dy = e * (1.0 / D_MODEL)
        dlng_ref[...] += jnp.sum(dy * xh, axis=0, keepdims=True)
        dlnb_ref[...] += jnp.sum(dy, axis=0, keepdims=True)
        dxh = dy * lng
        dr = rstd * (dxh - jnp.mean(dxh, axis=1, keepdims=True) - xh * jnp.mean(dxh * xh, axis=1, keepdims=True))
        dxr_ref[...] = DN_ALPHA * dr
        dr_bf = dr.astype(BF16)
        dwout_ref[...] += _dot(merged_bf, dr_bf, _TN)
        dmerged = _dot(dr_bf, wout, _NT)
        dpa_bf = (dmerged * sga).astype(BF16)
        dpb_bf = (dmerged * sgb).astype(BF16)
        _store_grad(dha_ref, dba_ref, 0, dmerged * pa * (sga * (1.0 - sga)))
        _store_grad(dha_ref, dba_ref, D_MODEL, dmerged * pb * (sgb * (1.0 - sgb)))
        dwoa_ref[...] += _dot(ya_bf, dpa_bf, _TN)
        dwob_ref[...] += _dot(yb_bf, dpb_bf, _TN)
        dya = _dot(dpa_bf, woa, _NT)
        dyb_ref[...] = _dot(dpb_bf, wob, _NT)
        do_ref[...] = dya * (za * sa)
        _store_grad(dha_ref, dba_ref, 2 * D_MODEL, dya * o * (sa * (1.0 + za * (1.0 - sa))))

        @pl.when(step == nsteps - 1)
        def _():
            cols = D_MODEL // N_DEV
            for j in range(N_DEV):
                poa_ref[j] = dwoa_ref[:, cols * j:cols * (j + 1)].astype(BF16)
                pob_ref[j] = dwob_ref[:, cols * j:cols * (j + 1)].astype(BF16)
                pout_ref[j] = dwout_ref[cols * j:cols * (j + 1), :].astype(BF16)

    row = lambda w, c=0: pl.BlockSpec((tm, w), lambda i, c=c: (i, c))
    full = lambda shape: pl.BlockSpec(shape, lambda i: (0, 0))
    full3 = lambda shape: pl.BlockSpec(shape, lambda i: (0, 0, 0))
    return pl.pallas_call(
        body, name="merge", grid=(nsteps,),
        in_specs=[row(D_MODEL), row(MLA_WIDTH), row(D_MODEL, 0), row(D_MODEL, 1), row(MLA_WIDTH, 4), row(SGU_WIDTH),
                  row(D_MODEL), full((MLA_WIDTH, D_MODEL)), full((SGU_WIDTH, D_MODEL)), full((D_MODEL, D_MODEL)),
                  full((1, D_MODEL)), full((1, D_MODEL))],
        out_specs=[full((1, LANES)), row(D_MODEL), row(SEG_A), row(MLA_WIDTH), row(SGU_WIDTH),
                   full3((N_DEV, MLA_WIDTH, D_MODEL // N_DEV)), full3((N_DEV, SGU_WIDTH, D_MODEL // N_DEV)),
                   full3((N_DEV, D_MODEL // N_DEV, D_MODEL)), full((1, D_MODEL)), full((1, D_MODEL)), full((1, SEG_A))],
        out_shape=[jax.ShapeDtypeStruct((1, LANES), F32),
                   jax.ShapeDtypeStruct((SEQ, D_MODEL), F32), jax.ShapeDtypeStruct((SEQ, SEG_A), BF16),
                   jax.ShapeDtypeStruct((SEQ, MLA_WIDTH), F32), jax.ShapeDtypeStruct((SEQ, SGU_WIDTH), F32),
                   jax.ShapeDtypeStruct((N_DEV, MLA_WIDTH, D_MODEL // N_DEV), BF16),
                   jax.ShapeDtypeStruct((N_DEV, SGU_WIDTH, D_MODEL // N_DEV), BF16),
                   jax.ShapeDtypeStruct((N_DEV, D_MODEL // N_DEV, D_MODEL), BF16),
                   jax.ShapeDtypeStruct((1, D_MODEL), F32), jax.ShapeDtypeStruct((1, D_MODEL), F32),
                   jax.ShapeDtypeStruct((1, SEG_A), F32)],
        scratch_shapes=[pltpu.VMEM((MLA_WIDTH, D_MODEL), F32), pltpu.VMEM((SGU_WIDTH, D_MODEL), F32),
                        pltpu.VMEM((D_MODEL, D_MODEL), F32)],
        compiler_params=pltpu.CompilerParams(dimension_semantics=("arbitrary",), vmem_limit_bytes=VMEM_BIG),
    )(x, o, h_a, h_a, h_a, y_b, target, w_oa, w_ob, w_out, ln_g, ln_b)


def _mla_bwd(dq, dk, dv, h_c, gq, gkv, wq, wkn, wv, c_t, sa_t, sb_t):
    tm = 256
    hw = MLA_HEADS * HEAD_PAD

    def body(dq_ref, dk_ref, dv_ref, cq_ref, ckv_ref, gq_ref, gkv_ref, wq_ref, wkn_ref, wv_ref, c_ref, sa_ref, sb_ref,
             dhc_ref, puq_ref, dwkn_ref, dwv_ref, dgq_ref, dgkv_ref, dbc_ref, pre_ref, dwq_ref):
        @pl.when(pl.program_id(0) == 0)
        def _():
            for r in (dwq_ref, dwkn_ref, dwv_ref, dgq_ref, dgkv_ref, dbc_ref):
                r[...] = jnp.zeros_like(r)

        c, sa, sb = c_ref[...], sa_ref[...], sb_ref[...]
        lane = lax.broadcasted_iota(jnp.int32, (tm, LANES), 1)
        rope_lanes = jnp.logical_and(lane >= ROPE_LO, lane < ROPE_HI)

        cq = cq_ref[...]
        gq = gq_ref[...]
        rq = lax.rsqrt(jnp.sum(cq * cq, axis=1, keepdims=True) * (1.0 / Q_LORA_RANK) + RMS_EPS)
        nq = cq * rq
        cqn_bf = (nq * gq).astype(BF16)
        for h in range(MLA_HEADS):
            sl = slice(HEAD_PAD * h, HEAD_PAD * (h + 1))
            pre_ref[:, sl] = _rope_t(dq_ref[:, sl] * ATTN_SCALE, c, sa, sb).astype(BF16)
        dqpre_bf = pre_ref[...]
        dcqn = _dot(dqpre_bf, wq_ref[...], _NT)
        dwq_ref[...] += _dot(cqn_bf, dqpre_bf, _TN)
        dgq_ref[...] += jnp.sum(dcqn * nq, axis=0, keepdims=True)
        dnq = dcqn * gq
        _store_grad(dhc_ref, dbc_ref, 0,
                    rq * (dnq - nq * (jnp.sum(dnq * nq, axis=1, keepdims=True) * (1.0 / Q_LORA_RANK))))

        ckv = ckv_ref[...]
        gkv = gkv_ref[...]
        rkv = lax.rsqrt(jnp.sum(ckv * ckv, axis=1, keepdims=True) * (1.0 / KV_LORA_RANK) + RMS_EPS)
        nkv = ckv * rkv
        ckvn_bf = (nkv * gkv).astype(BF16)
        dk = dk_ref[...]
        dk_bf = dk.astype(BF16)
        dv_bf = dv_ref[...].astype(BF16)
        dckvn = _dot(dk_bf, wkn_ref[...], _NT) + _dot(dv_bf, wv_ref[...], _NT)
        dwkn_ref[...] += _dot(ckvn_bf, dk_bf, _TN)
        dwv_ref[...] += _dot(ckvn_bf, dv_bf, _TN)
        dgkv_ref[...] += jnp.sum(dckvn * nkv, axis=0, keepdims=True)
        dnkv = dckvn * gkv
        _store_grad(dhc_ref, dbc_ref, CQ_PAD, rkv * (
            dnkv - nkv * (jnp.sum(dnkv * nkv, axis=1, keepdims=True) * (1.0 / KV_LORA_RANK))))
        dkpe = jnp.zeros((tm, LANES), F32)
        for h in range(MLA_HEADS):
            dkpe = dkpe + dk[:, HEAD_PAD * h:HEAD_PAD * (h + 1)]
        _store_grad(dhc_ref, dbc_ref, CQ_PAD + LANES, _rope_t(jnp.where(rope_lanes, dkpe, 0.0), c, sa, sb))

        @pl.when(pl.program_id(0) == SEQ // tm - 1)
        def _():
            rows = Q_LORA_RANK // N_DEV
            for j in range(N_DEV):
                for h in range(MLA_HEADS):
                    puq_ref[j, :, QK_HEAD_DIM * h:QK_HEAD_DIM * (h + 1)] = dwq_ref[
                        rows * j:rows * (j + 1), HEAD_PAD * h:HEAD_PAD * h + QK_HEAD_DIM].astype(BF16)

    full = lambda shape: pl.BlockSpec(shape, lambda i: (0, 0))
    row = lambda w, c=0: pl.BlockSpec((tm, w), lambda i, c=c: (i, c))
    return pl.pallas_call(
        body, name="mla_bwd", grid=(SEQ // tm,),
        in_specs=[row(hw), row(hw), row(hw), row(CQ_PAD, 0), row(LANES, CQ_PAD // LANES),
                  full((1, CQ_PAD)), full((1, KV_LORA_RANK)), full((CQ_PAD, hw)), full((KV_LORA_RANK, hw)),
                  full((KV_LORA_RANK, hw)), row(LANES), row(LANES), row(LANES)],
        out_specs=[row(SEG_C), pl.BlockSpec((N_DEV, Q_LORA_RANK // N_DEV, MLA_HEADS * QK_HEAD_DIM), lambda i: (0, 0, 0)),
                   full((KV_LORA_RANK, hw)), full((KV_LORA_RANK, hw)),
                   full((1, CQ_PAD)), full((1, KV_LORA_RANK)), full((1, SEG_C))],
        out_shape=[jax.ShapeDtypeStruct((SEQ, SEG_C), BF16),
                   jax.ShapeDtypeStruct((N_DEV, Q_LORA_RANK // N_DEV, MLA_HEADS * QK_HEAD_DIM), BF16),
                   jax.ShapeDtypeStruct((KV_LORA_RANK, hw), F32), jax.ShapeDtypeStruct((KV_LORA_RANK, hw), F32),
                   jax.ShapeDtypeStruct((1, CQ_PAD), F32), jax.ShapeDtypeStruct((1, KV_LORA_RANK), F32),
                   jax.ShapeDtypeStruct((1, SEG_C), F32)],
        scratch_shapes=[pltpu.VMEM((tm, hw), BF16), pltpu.VMEM((CQ_PAD, hw), F32)],
        compiler_params=pltpu.CompilerParams(dimension_semantics=("arbitrary",), vmem_limit_bytes=VMEM_MID),
    )(dq, dk, dv, h_c, h_c, gq, gkv, wq, wkn, wv, c_t, sa_t, sb_t)


def _adamw_all(ws, gs, ms, vs):
    n = len(ws)
    c1 = 1.0 / (1.0 - ADAM_B1 ** ADAM_STEP)
    c2 = 1.0 / (1.0 - ADAM_B2 ** ADAM_STEP)

    def body(*refs):
        for idx in range(n):
            w, g, m, v = (refs[idx][...], refs[n + idx][...], refs[2 * n + idx][...], refs[3 * n + idx][...])
            m_new = ADAM_B1 * m + (1.0 - ADAM_B1) * g
            v_new = ADAM_B2 * v + (1.0 - ADAM_B2) * (g * g)
            delta = -ADAM_LR * ((m_new * c1) / (jnp.sqrt(v_new * c2) + ADAM_EPS) + ADAM_WD * w)
            refs[4 * n + idx][...] = delta
            refs[5 * n + idx][...] = m_new
            refs[6 * n + idx][...] = v_new

    shapes = [jax.ShapeDtypeStruct(w.shape, F32) for w in ws]
    outs = pl.pallas_call(
        body, name="adamw", out_shape=shapes * 3,
        compiler_params=pltpu.CompilerParams(vmem_limit_bytes=VMEM_BIG),
    )(*ws, *gs, *ms, *vs)
    return outs[:n], outs[n:2 * n], outs[2 * n:]


SHARD_W = IN_WIDTH // N_DEV

_PIECES = [(0, 384, 2, 0), (384, 512, 2, CQ_PAD), (512, 544, 2, CQ_PAD + LANES + ROPE_LO),
           (544, 1056, 0, 2 * D_MODEL), (1056, 1568, 1, 0), (1568, 2080, 1, SGU_WIDTH),
           (2080, 2592, 1, 2 * SGU_WIDTH), (2592, 3616, 0, 0), (3616, 4640, 0, D_MODEL)]


def _column_runs():
    runs = []
    for n0, n1, seg, d0 in _PIECES:
        for j in range(N_DEV):
            lo, hi = max(n0, j * SHARD_W), min(n1, (j + 1) * SHARD_W)
            if lo < hi:
                runs.append((j, lo - j * SHARD_W, hi - j * SHARD_W, seg, d0 + lo - n0))
    return runs


def _mesh_pos():
    return lax.axis_index("x"), lax.axis_index("y"), lax.axis_index("c")


def _remote(src, dst, send_sems, recv_sems, k, to):
    return pltpu.make_async_remote_copy(src_ref=src, dst_ref=dst, send_sem=send_sems.at[k], recv_sem=recv_sems.at[k],
                                        device_id=to, device_id_type=pl.DeviceIdType.MESH)


def _gather_exchange(gats, send_sems, recv_sems, meanwhile=None):
    x, y, c = _mesh_pos()
    me, sibling = (x, y, c), (x, y, 1 - c)
    chips = [(1 - x, y), (x, 1 - y), (1 - x, 1 - y)]

    def copy(a, k, blk, to):
        slab = gats[a].at[4 * blk[0] + 2 * blk[1] + blk[2]]
        return _remote(slab, slab, send_sems, recv_sems, 7 * a + k, to)

    arrays = range(len(gats))
    first = [copy(a, 1 + j, me, (*chip, c)) for j, chip in enumerate(chips) for a in arrays]
    first += [copy(a, 0, me, sibling) for a in arrays]
    for cp in first:
        cp.start()
    if meanwhile is not None:
        meanwhile()
    passed = []
    for j, chip in enumerate(chips):
        for a in arrays:
            copy(a, 1 + j, (*chip, c), me).wait_recv()
            fwd = copy(a, 4 + j, (*chip, c), sibling)
            fwd.start()
            passed.append(fwd)
    for a in arrays:
        copy(a, 0, sibling, me).wait_recv()
    for j, chip in enumerate(chips):
        for a in arrays:
            copy(a, 4 + j, (*chip, 1 - c), me).wait_recv()
    for cp in first + passed:
        cp.wait_send()


def _gather_behind(own, gats, send_sems, recv_sems, local_sems, step, mid, last):
    x, y, c = _mesh_pos()
    me, sibling = (x, y, c), (x, y, 1 - c)
    chips = [(1 - x, y), (x, 1 - y), (1 - x, 1 - y)]
    arrays = range(len(gats))

    def copy(a, k, blk, to, src=None):
        slab = gats[a].at[4 * blk[0] + 2 * blk[1] + blk[2]]
        return _remote(slab if src is None else src, slab, send_sems, recv_sems, 7 * a + k, to)

    first = [copy(a, 1 + j, me, (*chip, c), src=own[a]) for j, chip in enumerate(chips) for a in arrays]
    first += [copy(a, 0, me, sibling, src=own[a]) for a in arrays]
    local = [pltpu.make_async_copy(own[a], gats[a].at[4 * x + 2 * y + c], local_sems.at[a]) for a in arrays]
    passed = [copy(a, 4 + j, (*chip, c), sibling) for j, chip in enumerate(chips) for a in arrays]

    @pl.when(step == 0)
    def _():
        for cp in first + local:
            cp.start()

    @pl.when(step == mid)
    def _():
        for j, chip in enumerate(chips):
            for a in arrays:
                copy(a, 1 + j, (*chip, c), me).wait_recv()
        for cp in passed:
            cp.start()

    @pl.when(step == last)
    def _():
        for a in arrays:
            copy(a, 0, sibling, me).wait_recv()
        for j, chip in enumerate(chips):
            for a in arrays:
                copy(a, 4 + j, (*chip, 1 - c), me).wait_recv()
        for cp in first + passed:
            cp.wait_send()
        for cp in local:
            cp.wait()


def _gather_first(w_in, w_uq2, w_oa, w_ob, w_out, x2, pos_col, invf_lane):
    hw = MLA_HEADS * HEAD_PAD
    uq_rows = Q_LORA_RANK // N_DEV
    rows = 256

    def body(win_ref, wuq_ref, woa_ref, wob_ref, wout_ref, x_ref, pos_ref, invf_ref,
             wc_ref, wq_ref, winb_ref, oab_ref, obb_ref, outb_ref, xb_ref, xt_ref, c_ref, sa_ref, sb_ref,
             g_uq, blk0, send_sems, recv_sems):
        def local_work():
            for i in range(SEQ // rows):
                xi = x_ref[rows * i:rows * (i + 1), :]
                xb_ref[rows * i:rows * (i + 1), :] = xi.astype(BF16)
                xt_ref[:, rows * i:rows * (i + 1)] = xi.T.astype(BF16)
            ang = pos_ref[...].astype(F32) * invf_ref[...]
            cs, sn = jnp.cos(ang), jnp.sin(ang)
            lane = lax.broadcasted_iota(jnp.int32, ang.shape, 1)
            c_ref[...] = jnp.where(lane < ROPE_LO, 1.0, jnp.where(lane < ROPE_HI, cs, 0.0))
            sa_ref[...] = jnp.where(jnp.logical_and(lane >= ROPE_LO, lane < ROPE_MID), -sn, 0.0)
            sb_ref[...] = jnp.where(jnp.logical_and(lane >= ROPE_MID, lane < ROPE_HI), sn, 0.0)

        x, y, c = _mesh_pos()
        me = (x, y, c)
        winb_ref[...] = win_ref[0].astype(BF16)
        oab_ref[...] = woa_ref[0].astype(BF16)
        obb_ref[...] = wob_ref[0].astype(BF16)
        outb_ref[...] = wout_ref[0].astype(BF16)
        g_uq[4 * x + 2 * y + c] = wuq_ref[...].astype(BF16)

        chip0 = jnp.logical_and(x == 0, y == 0)
        dev0 = jnp.logical_and(chip0, c == 0)
        north = c == 1
        targets = [(0, 0, 1), (1, 0, 0), (0, 1, 0), (1, 1, 0)]

        def bcopy(k, to):
            return _remote(blk0, blk0, send_sems, recv_sems, 7 + k, to)

        @pl.when(dev0)
        def _():
            blk0[...] = winb_ref[...]
            for k, to in enumerate(targets):
                bcopy(k, to).start()

        _gather_exchange([g_uq], send_sems, recv_sems, meanwhile=local_work)

        for k, (cx, cy, _) in enumerate(targets[1:], start=1):
            @pl.when(jnp.logical_and(jnp.logical_and(x == cx, y == cy), c == 0))
            def _(k=k, cx=cx, cy=cy):
                bcopy(k, me).wait_recv()
                onward = bcopy(4, (cx, cy, 1))
                onward.start()
                onward.wait_send()

        @pl.when(jnp.logical_and(chip0, north))
        def _():
            bcopy(0, me).wait_recv()

        @pl.when(jnp.logical_and(jnp.logical_not(chip0), north))
        def _():
            bcopy(4, me).wait_recv()

        @pl.when(dev0)
        def _():
            for k, to in enumerate(targets):
                bcopy(k, to).wait_send()

        for j, s0, s1, seg, d0 in _column_runs():
            if seg == 2:
                wc_ref[:, d0:d0 + (s1 - s0)] = blk0[:, s0:s1]
        zeros = lambda r, w: jnp.zeros((r, w), BF16)
        wc_ref[:, Q_LORA_RANK:CQ_PAD] = zeros(D_MODEL, CQ_PAD - Q_LORA_RANK)
        wc_ref[:, CQ_PAD + LANES:CQ_PAD + LANES + ROPE_LO] = zeros(D_MODEL, ROPE_LO)
        wc_ref[:, CQ_PAD + LANES + ROPE_HI:SEG_C] = zeros(D_MODEL, LANES - ROPE_HI)
        wq_ref[Q_LORA_RANK:CQ_PAD, :] = zeros(CQ_PAD - Q_LORA_RANK, hw)
        for h in range(MLA_HEADS):
            wq_ref[0:Q_LORA_RANK, HEAD_PAD * h + QK_HEAD_DIM:HEAD_PAD * (h + 1)] = zeros(Q_LORA_RANK, HEAD_PAD - QK_HEAD_DIM)
        for j in range(N_DEV):
            for h in range(MLA_HEADS):
                wq_ref[uq_rows * j:uq_rows * (j + 1), HEAD_PAD * h:HEAD_PAD * h + QK_HEAD_DIM] = g_uq[
                    j, :, QK_HEAD_DIM * h:QK_HEAD_DIM * (h + 1)]

    vmem = pl.BlockSpec(memory_space=pltpu.VMEM)
    return pl.pallas_call(
        body, name="gather_first",
        out_shape=[jax.ShapeDtypeStruct((D_MODEL, SEG_C), BF16), jax.ShapeDtypeStruct((CQ_PAD, hw), BF16),
                   jax.ShapeDtypeStruct(w_in.shape[1:], BF16), jax.ShapeDtypeStruct(w_oa.shape[1:], BF16),
                   jax.ShapeDtypeStruct(w_ob.shape[1:], BF16), jax.ShapeDtypeStruct(w_out.shape[1:], BF16),
                   jax.ShapeDtypeStruct((SEQ, D_MODEL), BF16), jax.ShapeDtypeStruct((D_MODEL, SEQ), BF16)]
        + [jax.ShapeDtypeStruct((SEQ, LANES), F32)] * 3,
        in_specs=[vmem] * 8, out_specs=[vmem] * 11,
        scratch_shapes=[pltpu.VMEM((N_DEV, uq_rows, MLA_HEADS * QK_HEAD_DIM), BF16), pltpu.VMEM((D_MODEL, SHARD_W), BF16),
                        pltpu.SemaphoreType.DMA((12,)), pltpu.SemaphoreType.DMA((12,))],
        compiler_params=pltpu.CompilerParams(vmem_limit_bytes=VMEM_BIG),
    )(w_in, w_uq2, w_oa, w_ob, w_out, x2, pos_col, invf_lane)


def _assemble_in(g_in):
    def body(g_ref, wa_ref, wb_ref):
        segs = [wa_ref, wb_ref]
        for j, s0, s1, seg, d0 in _column_runs():
            if seg < 2:
                segs[seg][:, d0:d0 + (s1 - s0)] = g_ref[j, :, s0:s1]

    return pl.pallas_call(
        body, name="assemble_in",
        out_shape=[jax.ShapeDtypeStruct((D_MODEL, SEG_A), BF16), jax.ShapeDtypeStruct((D_MODEL, SEG_B), BF16)],
        compiler_params=pltpu.CompilerParams(vmem_limit_bytes=VMEM_MID),
    )(g_in)


def _assemble_out(g_oa, g_ob, g_out):
    cols = D_MODEL // N_DEV

    def body(goa_ref, gob_ref, gout_ref, oa_ref, ob_ref, out_ref):
        for j in range(N_DEV):
            oa_ref[:, cols * j:cols * (j + 1)] = goa_ref[j]
            ob_ref[:, cols * j:cols * (j + 1)] = gob_ref[j]
            out_ref[cols * j:cols * (j + 1), :] = gout_ref[j]

    return pl.pallas_call(
        body, name="assemble_out",
        out_shape=[jax.ShapeDtypeStruct((MLA_WIDTH, D_MODEL), BF16), jax.ShapeDtypeStruct((SGU_WIDTH, D_MODEL), BF16),
                   jax.ShapeDtypeStruct((D_MODEL, D_MODEL), BF16)],
    )(g_oa, g_ob, g_out)


C_NAT = 544


def _to_parts(dwa, dwb):
    def body(dwa_ref, dwb_ref, pin_ref):
        pin_ref[0, :, 0:C_NAT] = jnp.zeros((D_MODEL, C_NAT), BF16)
        segs = [dwa_ref, dwb_ref]
        for j, s0, s1, seg, d0 in _column_runs():
            if seg < 2:
                pin_ref[j, :, s0:s1] = segs[seg][:, d0:d0 + (s1 - s0)]

    return pl.pallas_call(body, name="to_parts", out_shape=jax.ShapeDtypeStruct((N_DEV, D_MODEL, SHARD_W), BF16),
                          compiler_params=pltpu.CompilerParams(vmem_limit_bytes=VMEM_MID))(dwa, dwb)


def _finish_grads(dwc, p_uq, p_rep, landed):
    rep_rows = p_rep.shape[1]
    c_rows = D_MODEL // N_DEV
    spec = [((c_rows, C_NAT), BF16, c_rows), (p_uq.shape[1:], BF16, p_uq.shape[1]), ((rep_rows, LANES), F32, rep_rows)]
    n = len(spec)

    def body(dwc_ref, puq_ref, prep_ref, rin_ref, roa_ref, rob_ref, rout_ref,
             gin_ref, guq_ref, goa_ref, gob_ref, gout_ref, rep_all, pc_ref, c_red, c_all, *rest):
        ras, tbs, rbs = rest[0:n], rest[n:2 * n], rest[2 * n:3 * n]
        send_sems, recv_sems = rest[3 * n], rest[3 * n + 1]
        x, y, c = _mesh_pos()
        me = 4 * x + 2 * y + c
        sibling = (x, y, 1 - c)
        parts = [pc_ref, puq_ref, prep_ref]
        outs = [c_red, guq_ref, rep_all.at[me]]

        for j, s0, s1, seg, d0 in _column_runs():
            if seg == 2:
                for r in range(N_DEV):
                    pc_ref[r, :, s0:s1] = dwc_ref[c_rows * r:c_rows * (r + 1), d0:d0 + (s1 - s0)]

        stage1 = []
        for chip in range(4):
            for a in range(n):
                cp = _remote(parts[a].at[2 * chip + (1 - c)], ras[a].at[chip], send_sems, recv_sems, 7 * a + chip, sibling)
                cp.start()
                stage1.append(cp)

        def total(ref, sl):
            acc = ref[0, sl, :].astype(F32)
            for s in range(1, N_DEV):
                acc = acc + ref[s, sl, :].astype(F32)
            return acc

        for j in range(N_DEV):
            sl = slice(c_rows * j, c_rows * (j + 1))
            gin_ref[0, sl, :] = total(rin_ref, sl)
        goa_ref[0] = total(roa_ref, slice(None))
        gob_ref[0] = total(rob_ref, slice(None))
        gout_ref[0] = total(rout_ref, slice(None))

        for cp in stage1:
            cp.wait_recv()

        def rows_loop(a, fn):
            rows, chunk = spec[a][0][0], spec[a][2]
            if rows == chunk:
                fn(pl.ds(0, rows))
            else:
                def step(i, carry):
                    fn(pl.ds(pl.multiple_of(i * chunk, chunk), chunk))
                    return carry
                lax.fori_loop(0, rows // chunk, step, 0)

        def chip_sum(a, chip, sl):
            return parts[a][2 * chip + c, sl, :].astype(F32) + ras[a][chip, sl, :].astype(F32)

        others = [(1 - x, y), (x, 1 - y), (1 - x, 1 - y)]
        stage2 = []
        for k, (cx, cy) in enumerate(others):
            for a in range(n):
                def fill(sl, a=a, k=k, chip=2 * cx + cy):
                    tbs[a][k, sl, :] = chip_sum(a, chip, sl).astype(spec[a][1])
                rows_loop(a, fill)
                cp = _remote(tbs[a].at[k], rbs[a].at[k], send_sems, recv_sems, 7 * a + 4 + k, (cx, cy, c))
                cp.start()
                stage2.append(cp)
        for cp in stage2:
            cp.wait_recv()
        for a in range(n):
            def final(sl, a=a):
                acc = chip_sum(a, 2 * x + y, sl)
                for k in range(3):
                    acc = acc + rbs[a][k, sl, :].astype(F32)
                outs[a][sl, :] = acc
            rows_loop(a, final)
        for cp in stage1 + stage2:
            cp.wait_send()

        c_all[me] = c_red[...].astype(BF16)
        _gather_exchange([rep_all, c_all], rest[3 * n + 2], rest[3 * n + 3])

        dev0 = jnp.where(me == 0, 1.0, 0.0)
        for j in range(N_DEV):
            sl = slice(c_rows * j, c_rows * (j + 1))
            gin_ref[0, sl, 0:C_NAT] += dev0 * c_all[j].astype(F32)

    vmem = pl.BlockSpec(memory_space=pltpu.VMEM)
    scratch = [pltpu.VMEM((N_DEV, c_rows, C_NAT), BF16), pltpu.VMEM((c_rows, C_NAT), F32),
               pltpu.VMEM((N_DEV, c_rows, C_NAT), BF16)]
    for lead in (4, 3, 3):
        scratch += [pltpu.VMEM((lead,) + tuple(shape), dt) for shape, dt, _ in spec]
    scratch += [pltpu.SemaphoreType.DMA((7 * n,)), pltpu.SemaphoreType.DMA((7 * n,)),
                pltpu.SemaphoreType.DMA((14,)), pltpu.SemaphoreType.DMA((14,))]
    return pl.pallas_call(
        body, name="finish_grads",
        out_shape=[jax.ShapeDtypeStruct((1,) + landed[0].shape[1:], F32), jax.ShapeDtypeStruct(p_uq.shape[1:], F32)]
        + [jax.ShapeDtypeStruct((1,) + r.shape[1:], F32) for r in landed[1:]]
        + [jax.ShapeDtypeStruct((N_DEV, rep_rows, LANES), F32)],
        in_specs=[vmem] * 7, out_specs=[vmem] * 6, scratch_shapes=scratch,
        compiler_params=pltpu.CompilerParams(vmem_limit_bytes=VMEM_BIG),
    )(dwc, p_uq, p_rep, *landed)


_O_CQ, _O_CKV, _O_KPE, _O_ZA, _O_U, _O_V, _O_ZB, _O_GA, _O_GB = 0, 384, 512, 544, 1056, 1568, 2080, 2592, 3616


def _to_segments(w):
    z = lambda n: jnp.zeros(w.shape[:-1] + (n,), w.dtype)
    seg_a = jnp.concatenate([w[..., _O_GA:_O_GB], w[..., _O_GB:IN_WIDTH], w[..., _O_ZA:_O_U]], axis=-1)
    seg_b = jnp.concatenate([w[..., _O_U:_O_V], w[..., _O_V:_O_ZB], w[..., _O_ZB:_O_GA]], axis=-1)
    seg_c = jnp.concatenate([w[..., _O_CQ:_O_CKV], z(CQ_PAD - Q_LORA_RANK), w[..., _O_CKV:_O_KPE],
                             z(ROPE_LO), w[..., _O_KPE:_O_ZA], z(LANES - ROPE_HI)], axis=-1)
    return seg_a, seg_b, seg_c


def _from_segments(seg_a, seg_b, seg_c):
    kpe0 = CQ_PAD + LANES + ROPE_LO
    return jnp.concatenate([
        seg_c[..., 0:Q_LORA_RANK], seg_c[..., CQ_PAD:CQ_PAD + LANES], seg_c[..., kpe0:kpe0 + QK_ROPE_DIM],
        seg_a[..., 2 * D_MODEL:SEG_A], seg_b, seg_a[..., 0:2 * D_MODEL]], axis=-1)


def kernel(x, positions, w_in, b_in, g_q, w_uq, g_kv, w_ukv, w_oa, sgu_ln_g, sgu_ln_b, w_s, b_s, w_ob, w_out, ln_g, ln_b, loss_target, m_w_in, m_b_in, m_g_q, m_w_uq, m_g_kv, m_w_ukv, m_w_oa, m_sgu_ln_g, m_sgu_ln_b, m_w_s, m_b_s, m_w_ob, m_w_out, m_ln_g, m_ln_b, v_w_in, v_b_in, v_g_q, v_w_uq, v_g_kv, v_w_ukv, v_w_oa, v_sgu_ln_g, v_sgu_ln_b, v_w_s, v_b_s, v_w_ob, v_w_out, v_ln_g, v_ln_b):
    w_uq2 = w_uq[0].reshape(Q_LORA_RANK // N_DEV, MLA_HEADS * QK_HEAD_DIM)
    inv_freq = ROPE_THETA ** (-jnp.arange(0, QK_ROPE_DIM, 2, dtype=F32) / QK_ROPE_DIM)
    invf_lane = jnp.concatenate([jnp.zeros((ROPE_LO,), F32), inv_freq, inv_freq,
                                 jnp.zeros((LANES - ROPE_HI,), F32)]).reshape(1, LANES)
    first = _gather_first(w_in, w_uq2, w_oa, w_ob, w_out, x[0], positions.reshape(SEQ, 1), invf_lane)
    partials = _local_step(x[0], loss_target[0], first, b_in, g_q, g_kv, w_ukv, sgu_ln_g, sgu_ln_b, w_s, b_s, ln_g, ln_b)
    weights = dict(w_in=w_in, b_in=b_in, g_q=g_q, w_uq=w_uq, g_kv=g_kv, w_ukv=w_ukv, w_oa=w_oa, sgu_ln_g=sgu_ln_g,
                   sgu_ln_b=sgu_ln_b, w_s=w_s, b_s=b_s, w_ob=w_ob, w_out=w_out, ln_g=ln_g, ln_b=ln_b)
    moms = dict(w_in=m_w_in, b_in=m_b_in, g_q=m_g_q, w_uq=m_w_uq, g_kv=m_g_kv, w_ukv=m_w_ukv, w_oa=m_w_oa,
                sgu_ln_g=m_sgu_ln_g, sgu_ln_b=m_sgu_ln_b, w_s=m_w_s, b_s=m_b_s, w_ob=m_w_ob, w_out=m_w_out,
                ln_g=m_ln_g, ln_b=m_ln_b)
    vars_ = dict(w_in=v_w_in, b_in=v_b_in, g_q=v_g_q, w_uq=v_w_uq, g_kv=v_g_kv, w_ukv=v_w_ukv, w_oa=v_w_oa,
                 sgu_ln_g=v_sgu_ln_g, sgu_ln_b=v_sgu_ln_b, w_s=v_w_s, b_s=v_b_s, w_ob=v_w_ob, w_out=v_w_out,
                 ln_g=v_ln_g, ln_b=v_ln_b)
    return _reduce_and_update(partials, weights, moms, vars_)


def _local_step(x2, tgt, first, b_in, g_q, g_kv, w_ukv, sgu_ln_g, sgu_ln_b, w_s, b_s, ln_g, ln_b):
    wc, wq, win_b, oa_b, ob_b, out_b, x_bf, xt_bf, c_t, sa_t, sb_t = first
    ba, bb, bc = _to_segments(b_in)
    w_ukv_bf = w_ukv[0].astype(BF16)
    wkn = jnp.pad(w_ukv_bf[:, :, :QK_NOPE_DIM], ((0, 0), (0, 0), (0, HEAD_PAD - QK_NOPE_DIM))).reshape(KV_LORA_RANK, -1)
    wv = jnp.pad(w_ukv_bf[:, :, QK_NOPE_DIM:], ((0, 0), (0, 0), (0, HEAD_PAD - V_HEAD_DIM))).reshape(KV_LORA_RANK, -1)
    gq = jnp.pad(g_q, ((0, 0), (0, CQ_PAD - Q_LORA_RANK)))
    bias_full = jnp.repeat(b_s[0].T, SGU_GROUP_DIM, axis=1)
    w_s3 = w_s[0]
    w_st3 = jnp.swapaxes(w_s3, 1, 2)

    h_c = _mm(x_bf, wc, bias=bc, tm=512, tn=SEG_C, name="in_proj_c")
    q, k, kt, vx, vxt = _mla_prep(h_c, gq, g_kv, wq, wkn, wv, c_t, sa_t, sb_t)
    o, lse, (g_in,) = _attn_fwd(q, kt, vx, (win_b,))
    wa, wb = _assemble_in(g_in)
    h_a, gathered = _mm(x_bf, wa, bias=ba, own=(oa_b, ob_b, out_b), tm=512, tn=SEG_A // 2, name="in_proj_a")
    h_b = _mm(x_bf, wb, bias=bb, tm=512, tn=SEG_B // 2, name="in_proj_b")
    y_b = _sgu_fwd(h_b, sgu_ln_g, sgu_ln_b, w_s3, bias_full)
    w_oa_f, w_ob_f, w_out_f = _assemble_out(*gathered)

    (loss_row, dx_res, dh_a, d_o, d_yb, p_oa, p_ob, p_out, d_lng, d_lnb, d_ba) = _merge(
        x2, o, h_a, y_b, tgt, w_oa_f, w_ob_f, w_out_f, ln_g, ln_b)
    (dh_b, d_ws, d_bs_t, d_slg, d_slb, d_bb), landed_o = _sgu_bwd(h_b, d_yb, sgu_ln_g, sgu_ln_b, w_s3, w_st3, bias_full,
                                                                 (p_oa, p_ob, p_out))
    d_wa = _mm(xt_bf, dh_a, out_dtype=BF16, tm=512, tn=512, name="dw_in_a")
    d_wb = _mm(xt_bf, dh_b, out_dtype=BF16, tm=512, tn=512, name="dw_in_b")
    dq, dk, dv, landed_in = _attn_bwd(q, kt, k, vxt, d_o, o, lse, (_to_parts(d_wa, d_wb),))
    landed = (*landed_in, *landed_o)
    dh_c, p_uq, d_wkn, d_wv, d_gq, d_gkv, d_bc = _mla_bwd(dq, dk, dv, h_c, gq, g_kv, wq, wkn, wv, c_t, sa_t, sb_t)
    d_wc = _mm(xt_bf, dh_c, out_dtype=BF16, tm=512, tn=SEG_C, name="dw_in_c")

    dx = _mm(dh_a, wa, tb=True, add=dx_res, tm=512, tn=D_MODEL, name="dx_a")
    dx = _mm(dh_b, wb, tb=True, add=dx, tm=512, tn=D_MODEL, name="dx_b")
    dx = _mm(dh_c, wc, tb=True, add=dx, tm=512, tn=D_MODEL, name="dx_c")

    p_b_in = _from_segments(d_ba, d_bb, d_bc)
    p_w_ukv = jnp.concatenate([d_wkn.reshape(KV_LORA_RANK, MLA_HEADS, HEAD_PAD)[:, :, :QK_NOPE_DIM],
                               d_wv.reshape(KV_LORA_RANK, MLA_HEADS, HEAD_PAD)[:, :, :V_HEAD_DIM]], axis=-1)
    p_g_q = d_gq[:, :Q_LORA_RANK]
    p_b_s = d_bs_t[:, :SGU_GROUPS].T
    replicated = [p_b_in, p_g_q, d_gkv, p_w_ukv, d_slg, d_slb, d_ws, p_b_s, d_lng, d_lnb]
    return loss_row, dx, landed, d_wc, p_uq, replicated


_NAMES = ["w_in", "b_in", "g_q", "w_uq", "g_kv", "w_ukv", "w_oa", "sgu_ln_g", "sgu_ln_b", "w_s", "b_s", "w_ob",
          "w_out", "ln_g", "ln_b"]
_REPLICATED = ["b_in", "g_q", "g_kv", "w_ukv", "sgu_ln_g", "sgu_ln_b", "w_s", "b_s", "ln_g", "ln_b"]


def _reduce_and_update(partials, weights, moms, vars_):
    loss_row, dx, landed, d_wc, p_uq, replicated = partials
    rep_flat = jnp.concatenate([a.reshape(-1) for a in replicated] + [loss_row[0, :1]])
    rep_flat = jnp.pad(rep_flat, (0, N_DEV * PACK_R_ROWS * LANES - rep_flat.size))
    g_in, g_uq, g_oa, g_ob, g_out, rep_all = _finish_grads(d_wc, p_uq, rep_flat.reshape(N_DEV, PACK_R_ROWS, LANES), landed)
    rep_sum = rep_all.reshape(-1)
    grads, pos = dict(w_in=g_in, w_uq=g_uq, w_oa=g_oa, w_ob=g_ob, w_out=g_out), 0
    for nm in _REPLICATED:
        grads[nm] = rep_sum[pos:pos + weights[nm].size]
        pos += weights[nm].size
    loss = rep_sum[pos]
    grads = {nm: grads[nm].reshape(weights[nm].shape) for nm in _NAMES}
    deltas, new_m, new_v = _adamw_all([weights[nm] for nm in _NAMES], [grads[nm] for nm in _NAMES],
                                      [moms[nm] for nm in _NAMES], [vars_[nm] for nm in _NAMES])
    return (loss, dx.reshape(1, SEQ, D_MODEL), *[grads[nm] for nm in _NAMES], *deltas, *new_m, *new_v)
```

```python
import math

import jax
import jax.numpy as jnp
from jax import lax
from jax.experimental import pallas as pl
from jax.experimental.pallas import tpu as pltpu

F32 = jnp.float32
BF16 = jnp.bfloat16

D_MODEL = 1024
SEQ = 2048
N_DEV = 8
MLA_HEADS = 8
Q_LORA_RANK = 384
KV_LORA_RANK = 128
QK_NOPE_DIM = 64
QK_ROPE_DIM = 32
V_HEAD_DIM = 64
QK_HEAD_DIM = QK_NOPE_DIM + QK_ROPE_DIM
MLA_WIDTH = MLA_HEADS * V_HEAD_DIM
ROPE_THETA = 10000.0
SGU_GROUPS = 8
SGU_GROUP_DIM = 64
SGU_WIDTH = SGU_GROUPS * SGU_GROUP_DIM
CHUNK = 128
RMS_EPS = 1e-6
LN_EPS = 1e-5
DN_ALPHA = 2.0 ** 0.25
IN_WIDTH = 4640
ATTN_SCALE = QK_HEAD_DIM ** -0.5

ADAM_LR = 0.001
ADAM_B1 = 0.9
ADAM_B2 = 0.999
ADAM_EPS = 1e-08
ADAM_WD = 0.01
ADAM_STEP = 10

LANES = 128
HEAD_PAD = 128
ROPE_LO = QK_NOPE_DIM
ROPE_MID = ROPE_LO + QK_ROPE_DIM // 2
ROPE_HI = ROPE_LO + QK_ROPE_DIM
CQ_PAD = 512

SEG_A = 2560
SEG_B = 1536
SEG_C = 768

PACK_R_ROWS = 272
VMEM_BIG = 56 * 1024 * 1024
VMEM_MID = 40 * 1024 * 1024


def _sigmoid(x):
    return 1.0 / (1.0 + jnp.exp(-x))


def _gelu_and_grad(x):
    c0 = math.sqrt(2.0 / math.pi)
    x2 = x * x
    t = jnp.tanh(c0 * (x + 0.044715 * x * x2))
    g = 0.5 * x * (1.0 + t)
    dg = 0.5 * (1.0 + t) + 0.5 * x * (1.0 - t * t) * (c0 * (1.0 + 3.0 * 0.044715 * x2))
    return g, dg


def _dot(a, b, dims):
    return lax.dot_general(a, b, (dims, ((), ())), preferred_element_type=F32)


_NN = ((1,), (0,))
_NT = ((1,), (1,))
_TN = ((0,), (0,))


def _store_grad(dh_ref, db_ref, col, val):
    cols = slice(col, col + val.shape[1])
    dh_ref[:, cols] = val.astype(BF16)
    db_ref[:, cols] += jnp.sum(val, axis=0, keepdims=True)


def _mm(a, b, *, tb=False, bias=None, add=None, out_dtype=F32, own=(), parts=(), tm, tn, name):
    m, k = a.shape
    n = b.shape[0] if tb else b.shape[1]
    assert m % tm == 0 and n % tn == 0 and not (own and parts)
    dims = _NT if tb else _NN
    nown = len(own) + len(parts)
    nm = m // tm
    nsteps = (n // tn) * nm

    def body(*refs):
        a_ref, b_ref = refs[0], refs[1]
        pos = 2
        r = _dot(a_ref[...], b_ref[...], dims)
        if bias is not None:
            r = r + refs[pos][...]; pos += 1
        if add is not None:
            r = r + refs[pos][...]; pos += 1
        own_refs = refs[pos:pos + nown]; pos += nown
        refs[pos][...] = r.astype(out_dtype)
        if nown:
            gat_refs = refs[pos + 1:pos + 1 + nown]
            send_sems, recv_sems, local_sems = refs[pos + 1 + nown:]
            step = pl.program_id(0) * nm + pl.program_id(1)
            if own:
                _gather_behind(own_refs, gat_refs, send_sems, recv_sems, local_sems, step, nsteps - 2, nsteps - 1)
            else:
                exchange = _exchange_parts(own_refs, gat_refs, send_sems, recv_sems, local_sems)
                _exchange_start(step == 0, exchange)
                _exchange_finish(step == nsteps - 1, exchange)

    b_spec = pl.BlockSpec((tn, k), lambda j, i: (j, 0)) if tb else pl.BlockSpec((k, tn), lambda j, i: (0, j))
    in_specs, args = [pl.BlockSpec((tm, k), lambda j, i: (i, 0)), b_spec], [a, b]
    if bias is not None:
        in_specs.append(pl.BlockSpec((1, tn), lambda j, i: (0, j))); args.append(bias)
    if add is not None:
        in_specs.append(pl.BlockSpec((tm, tn), lambda j, i: (i, j))); args.append(add)
    hbm = pl.BlockSpec(memory_space=pl.ANY)
    res = pl.pallas_call(
        body, name=name, grid=(n // tn, nm), in_specs=in_specs + [hbm] * nown,
        out_specs=[pl.BlockSpec((tm, tn), lambda j, i: (i, j))] + [hbm] * nown,
        out_shape=[jax.ShapeDtypeStruct((m, n), out_dtype)]
        + [jax.ShapeDtypeStruct((N_DEV,) + o.shape, o.dtype) for o in own]
        + [jax.ShapeDtypeStruct(p.shape, p.dtype) for p in parts],
        scratch_shapes=_exchange_sems(nown) if nown else [],
        compiler_params=pltpu.CompilerParams(dimension_semantics=("arbitrary", "arbitrary"), vmem_limit_bytes=VMEM_BIG),
    )(*args, *own, *parts)
    return (res[0], res[1:]) if nown else res[0]


def _rope(x, c, sa, sb):
    return x * c + pltpu.roll(x, LANES - 16, 1) * sa + pltpu.roll(x, 16, 1) * sb


def _rope_t(dy, c, sa, sb):
    return dy * c + pltpu.roll(dy * sa, 16, 1) + pltpu.roll(dy * sb, LANES - 16, 1)


def _mla_prep(h_c, gq, gkv, wq, wkn, wvx, c_t, sa_t, sb_t):
    tm = 256
    hw = MLA_HEADS * HEAD_PAD

    def body(cq_ref, ckv_ref, kpe_ref, gq_ref, gkv_ref, wq_ref, wkn_ref, wvx_ref, c_ref, sa_ref, sb_ref,
             q_ref, k_ref, kt_ref, vx_ref, vxt_ref):
        c, sa, sb = c_ref[...], sa_ref[...], sb_ref[...]
        cq = cq_ref[...]
        rq = lax.rsqrt(jnp.sum(cq * cq, axis=1, keepdims=True) * (1.0 / Q_LORA_RANK) + RMS_EPS)
        cqn = ((cq * rq) * gq_ref[...]).astype(BF16)
        qall = _dot(cqn, wq_ref[...], _NN)
        for h in range(MLA_HEADS):
            sl = slice(HEAD_PAD * h, HEAD_PAD * (h + 1))
            q_ref[:, sl] = (_rope(qall[:, sl], c, sa, sb) * ATTN_SCALE).astype(BF16)
        ckv = ckv_ref[...]
        rkv = lax.rsqrt(jnp.sum(ckv * ckv, axis=1, keepdims=True) * (1.0 / KV_LORA_RANK) + RMS_EPS)
        ckvn = ((ckv * rkv) * gkv_ref[...]).astype(BF16)
        knall = _dot(ckvn, wkn_ref[...], _NN)
        vall = _dot(ckvn, wvx_ref[...], _NN)
        kper = _rope(kpe_ref[...], c, sa, sb)
        ones_half = (lax.broadcasted_iota(jnp.int32, (tm, HEAD_PAD), 1) >= V_HEAD_DIM).astype(F32)
        for h in range(MLA_HEADS):
            sl = slice(HEAD_PAD * h, HEAD_PAD * (h + 1))
            kh = knall[:, sl] + kper
            vh = vall[:, sl] + ones_half
            k_ref[:, sl] = kh.astype(BF16)
            kt_ref[sl, :] = kh.T.astype(BF16)
            vx_ref[:, sl] = vh.astype(BF16)
            vxt_ref[sl, :] = vh.T.astype(BF16)

    full = lambda shape: pl.BlockSpec(shape, lambda i: (0, 0))
    tab = pl.BlockSpec((tm, LANES), lambda i: (i, 0))
    row = pl.BlockSpec((tm, hw), lambda i: (i, 0))
    col = pl.BlockSpec((hw, tm), lambda i: (0, i))
    return pl.pallas_call(
        body, name="mla_prep", grid=(SEQ // tm,),
        in_specs=[pl.BlockSpec((tm, CQ_PAD), lambda i: (i, 0)),
                  pl.BlockSpec((tm, LANES), lambda i: (i, CQ_PAD // LANES)),
                  pl.BlockSpec((tm, LANES), lambda i: (i, CQ_PAD // LANES + 1)),
                  full((1, CQ_PAD)), full((1, KV_LORA_RANK)),
                  full((CQ_PAD, hw)), full((KV_LORA_RANK, hw)), full((KV_LORA_RANK, hw)), tab, tab, tab],
        out_specs=[row, row, col, row, col],
        out_shape=[jax.ShapeDtypeStruct((SEQ, hw), BF16), jax.ShapeDtypeStruct((SEQ, hw), BF16),
                   jax.ShapeDtypeStruct((hw, SEQ), BF16), jax.ShapeDtypeStruct((SEQ, hw), BF16),
                   jax.ShapeDtypeStruct((hw, SEQ), BF16)],
        compiler_params=pltpu.CompilerParams(dimension_semantics=("arbitrary",), vmem_limit_bytes=VMEM_MID),
    )(h_c, h_c, h_c, gq, gkv, wq, wkn, wvx, c_t, sa_t, sb_t)


ATT_T = 512
ATT_STRIP = 64


def _attn_fwd(q, kt, vx, own):
    t, rs = ATT_T, ATT_STRIP
    nown = len(own)
    nq = SEQ // t
    nsteps = (MLA_HEADS // 2) * nq

    def body(q_ref, kt_ref, vx_ref, *rest):
        own_refs, (o_ref, l_ref), gat_refs = rest[:nown], rest[nown:nown + 2], rest[nown + 2:2 * nown + 2]
        s_scr, p_scr, m_scr, a_scr, acc_scr, send_sems, recv_sems, local_sems = rest[2 * nown + 2:]
        qi = pl.program_id(1)
        _gather_behind(own_refs, gat_refs, send_sems, recv_sems, local_sems, pl.program_id(0) * nq + qi,
                       nsteps - 2, nsteps - 1)
        lane = lax.broadcasted_iota(jnp.int32, (t, LANES), 1)
        m_scr[...] = jnp.full((2, t, LANES), -1e30, F32)
        acc_scr[...] = jnp.zeros((2, t, LANES), F32)

        def block(j, masked):
            off = pl.multiple_of(j * t, t)
            for a in range(2):
                sl = slice(HEAD_PAD * a, HEAD_PAD * (a + 1))
                s_scr[a] = _dot(q_ref[:, sl], kt_ref[sl, pl.ds(off, t)], _NN)
                for r in range(t // rs):
                    rows = slice(rs * r, rs * (r + 1))
                    s = s_scr[a, rows, :]
                    if masked:
                        rowi = lax.broadcasted_iota(jnp.int32, (rs, t), 0) + rs * r
                        coli = lax.broadcasted_iota(jnp.int32, (rs, t), 1)
                        s = jnp.where(coli <= rowi, s, -1e30)
                    m_old = m_scr[a, rows, :]
                    m_new = jnp.maximum(m_old, jnp.max(s, axis=1, keepdims=True))
                    p_scr[a, rows, :] = jnp.exp(s - m_new[:, :1]).astype(BF16)
                    a_scr[a, rows, :] = jnp.exp(m_old - m_new)
                    m_scr[a, rows, :] = m_new
                acc_scr[a] = acc_scr[a] * a_scr[a] + _dot(p_scr[a], vx_ref[pl.ds(off, t), sl], _NN)

        def step(j, carry):
            block(j, False)
            return carry
        lax.fori_loop(0, qi, step, 0)
        block(qi, True)
        res = []
        for a in range(2):
            acc = acc_scr[a]
            l = acc[:, V_HEAD_DIM:V_HEAD_DIM + 1]
            res.append((acc / l, m_scr[a] + jnp.log(l)))
        o_ref[...] = jnp.where(lane < V_HEAD_DIM, res[0][0], pltpu.roll(res[1][0], V_HEAD_DIM, 1))
        l_ref[...] = jnp.where(lane < V_HEAD_DIM, res[0][1], res[1][1])

    hbm = pl.BlockSpec(memory_space=pl.ANY)
    res = pl.pallas_call(
        body, name="attn_fwd", grid=(MLA_HEADS // 2, nq),
        in_specs=[pl.BlockSpec((t, 2 * HEAD_PAD), lambda p, i: (i, p)),
                  pl.BlockSpec((2 * HEAD_PAD, SEQ), lambda p, i: (p, 0)),
                  pl.BlockSpec((SEQ, 2 * HEAD_PAD), lambda p, i: (0, p))] + [hbm] * nown,
        out_specs=[pl.BlockSpec((t, LANES), lambda p, i: (i, p)),
                   pl.BlockSpec((t, LANES), lambda p, i: (i, p))] + [hbm] * nown,
        out_shape=[jax.ShapeDtypeStruct((SEQ, MLA_WIDTH), F32), jax.ShapeDtypeStruct((SEQ, MLA_WIDTH), F32)]
        + [jax.ShapeDtypeStruct((N_DEV,) + a.shape, a.dtype) for a in own],
        scratch_shapes=[pltpu.VMEM((2, t, t), F32), pltpu.VMEM((2, t, t), BF16), pltpu.VMEM((2, t, LANES), F32),
                        pltpu.VMEM((2, t, LANES), F32), pltpu.VMEM((2, t, LANES), F32)] + _exchange_sems(nown),
        compiler_params=pltpu.CompilerParams(dimension_semantics=("arbitrary", "arbitrary"), vmem_limit_bytes=VMEM_MID),
    )(q, kt, vx, *own)
    return res[0], res[1], res[2:]


def _exchange_parts(parts, lands, send_sems, recv_sems, local_sems):
    x, y, c = _mesh_pos()
    me = 4 * x + 2 * y + c
    peers = [(x, y, 1 - c), (1 - x, y, c), (x, 1 - y, c), (1 - x, 1 - y, c),
             (1 - x, y, 1 - c), (x, 1 - y, 1 - c), (1 - x, 1 - y, 1 - c)]
    remote, local = [], []
    for a, (part, land) in enumerate(zip(parts, lands)):
        for k, peer in enumerate(peers):
            t = 4 * peer[0] + 2 * peer[1] + peer[2]
            remote.append(_remote(part.at[t], land.at[me], send_sems, recv_sems, 7 * a + k, peer))
        local.append(pltpu.make_async_copy(part.at[me], land.at[me], local_sems.at[a]))
    return remote, local


def _exchange_start(first_step, exchange):
    remote, local = exchange

    @pl.when(first_step)
    def _():
        for cp in remote + local:
            cp.start()


def _exchange_finish(last_step, exchange):
    remote, local = exchange

    @pl.when(last_step)
    def _():
        for cp in remote:
            cp.wait_recv()
        for cp in remote:
            cp.wait_send()
        for cp in local:
            cp.wait()


def _exchange_sems(npart):
    return [pltpu.SemaphoreType.DMA((7 * npart,)), pltpu.SemaphoreType.DMA((7 * npart,)),
            pltpu.SemaphoreType.DMA((npart,))]


def _attn_bwd(q, kt, k, vxt, d_o, o, lse, parts):
    t, rs = ATT_T, ATT_STRIP
    nq = SEQ // t
    npart = len(parts)
    nsteps = MLA_HEADS // 2

    def body(q_ref, kt_ref, k_ref, vxt_ref, do_ref, o_ref, l_ref, *rest):
        part_refs, rest = rest[:npart], rest[npart:]
        dq_ref, dk_ref, dv_ref = rest[:3]
        land_refs, rest = rest[3:3 + npart], rest[3 + npart:]
        s_scr, dp_scr, p_scr, ds_scr, st_scr, send_sems, recv_sems, local_sems = rest
        exchange = _exchange_parts(part_refs, land_refs, send_sems, recv_sems, local_sems)
        _exchange_start(pl.program_id(0) == 0, exchange)
        dk_ref[...] = jnp.zeros_like(dk_ref)
        dv_ref[...] = jnp.zeros_like(dv_ref)
        lane = lax.broadcasted_iota(jnp.int32, (t, LANES), 1)

        def qtile(i, carry):
            ioff = pl.multiple_of(i * t, t)
            do_i = do_ref[pl.ds(ioff, t), :]
            o_i = o_ref[pl.ds(ioff, t), :]
            l_i = l_ref[pl.ds(ioff, t), :]
            for a in range(2):
                sl = slice(HEAD_PAD * a, HEAD_PAD * (a + 1))
                sel = (lane < V_HEAD_DIM) if a == 0 else (lane >= V_HEAD_DIM)
                doa = jnp.where(sel, do_i, 0.0)
                oa = o_i
                if a == 1:
                    doa = pltpu.roll(doa, V_HEAD_DIM, 1)
                    oa = pltpu.roll(o_i, V_HEAD_DIM, 1)
                st_scr[0] = jnp.broadcast_to(jnp.sum(doa * oa, axis=1, keepdims=True), (t, LANES))
                st_scr[1] = jnp.broadcast_to(l_i[:, V_HEAD_DIM * a:V_HEAD_DIM * a + 1], (t, LANES))
                doa_bf = doa.astype(BF16)
                qa = q_ref[pl.ds(ioff, t), sl]

                def block(j, masked, dq_acc, sl=sl, qa=qa, doa_bf=doa_bf):
                    joff = pl.multiple_of(j * t, t)
                    s_scr[...] = _dot(qa, kt_ref[sl, pl.ds(joff, t)], _NN)
                    dp_scr[...] = _dot(doa_bf, vxt_ref[sl, pl.ds(joff, t)], _NN)
                    for r in range(t // rs):
                        rows = slice(rs * r, rs * (r + 1))
                        p = jnp.exp(s_scr[rows, :] - st_scr[1, rows, :1])
                        if masked:
                            rowi = lax.broadcasted_iota(jnp.int32, (rs, t), 0) + rs * r
                            coli = lax.broadcasted_iota(jnp.int32, (rs, t), 1)
                            p = jnp.where(coli <= rowi, p, 0.0)
                        p_scr[rows, :] = p.astype(BF16)
                        ds_scr[rows, :] = (p * (dp_scr[rows, :] - st_scr[0, rows, :1])).astype(BF16)
                    dk_ref[pl.ds(joff, t), sl] += _dot(ds_scr[...], qa, _TN)
                    dv_ref[pl.ds(joff, t), sl] += _dot(p_scr[...], doa_bf, _TN)
                    return dq_acc + _dot(ds_scr[...], k_ref[pl.ds(joff, t), sl], _NN)

                dq_acc = lax.fori_loop(0, i, lambda j, acc: block(j, False, acc), jnp.zeros((t, HEAD_PAD), F32))
                dq_ref[pl.ds(ioff, t), sl] = block(i, True, dq_acc)
            return carry

        lax.fori_loop(0, nq, qtile, 0)
        _exchange_finish(pl.program_id(0) == nsteps - 1, exchange)

    hw = MLA_HEADS * HEAD_PAD
    wide = pl.BlockSpec((SEQ, 2 * HEAD_PAD), lambda p: (0, p))
    wide_t = pl.BlockSpec((2 * HEAD_PAD, SEQ), lambda p: (p, 0))
    narrow = pl.BlockSpec((SEQ, LANES), lambda p: (0, p))
    hbm = pl.BlockSpec(memory_space=pl.ANY)
    res = pl.pallas_call(
        body, name="attn_bwd", grid=(nsteps,),
        in_specs=[wide, wide_t, wide, wide_t, narrow, narrow, narrow] + [hbm] * npart,
        out_specs=[wide, wide, wide] + [hbm] * npart,
        out_shape=[jax.ShapeDtypeStruct((SEQ, hw), F32)] * 3 + [jax.ShapeDtypeStruct(p.shape, p.dtype) for p in parts],
        scratch_shapes=[pltpu.VMEM((t, t), F32), pltpu.VMEM((t, t), F32), pltpu.VMEM((t, t), BF16),
                        pltpu.VMEM((t, t), BF16), pltpu.VMEM((2, t, LANES), F32)] + _exchange_sems(npart),
        compiler_params=pltpu.CompilerParams(dimension_semantics=("arbitrary",), vmem_limit_bytes=VMEM_BIG),
    )(q, kt, k, vxt, d_o, o, lse, *parts)
    return res[0], res[1], res[2], res[3:]


def _sgu_math(u, v, zb, lg, lb, ws_ref, bias):
    ug, dug = _gelu_and_grad(u)
    vg, dvg = _gelu_and_grad(v)
    mu = jnp.mean(vg, axis=1, keepdims=True)
    xc = vg - mu
    rstd = lax.rsqrt(jnp.mean(xc * xc, axis=1, keepdims=True) + LN_EPS)
    xh = xc * rstd
    vn_bf = (xh * lg + lb).astype(BF16)
    grp = lax.broadcasted_iota(jnp.int32, (CHUNK, SGU_WIDTH), 1) // SGU_GROUP_DIM
    r_i = lax.broadcasted_iota(jnp.int32, (CHUNK, CHUNK), 0)
    c_i = lax.broadcasted_iota(jnp.int32, (CHUNK, CHUNK), 1)
    tri, tri_t = r_i >= c_i, r_i <= c_i
    mixed = bias
    for g in range(SGU_GROUPS):
        wt = jnp.where(tri, ws_ref[g], 0.0).astype(BF16)
        mixed = mixed + jnp.where(grp == g, _dot(wt, vn_bf, _NN), 0.0)
    sb = _sigmoid(zb)
    return ug, dug, dvg, rstd, xh, vn_bf, grp, tri, tri_t, mixed, sb


def _sgu_fwd(h_b, lg, lb, w_s, bias_full):
    def body(u_ref, v_ref, zb_ref, lg_ref, lb_ref, ws_ref, bias_ref, yb_ref):
        zb = zb_ref[...]
        ug, _, _, _, _, _, _, _, _, mixed, sb = _sgu_math(u_ref[...], v_ref[...], zb, lg_ref[...], lb_ref[...],
                                                       ws_ref, bias_ref[...])
        yb_ref[...] = (ug * mixed) * (zb * sb)

    blk = lambda c: pl.BlockSpec((CHUNK, SGU_WIDTH), lambda i, c=c: (i, c))
    full2 = lambda shape: pl.BlockSpec(shape, lambda i: (0, 0))
    return pl.pallas_call(
        body, name="sgu_fwd", grid=(SEQ // CHUNK,),
        in_specs=[blk(0), blk(1), blk(2), full2((1, SGU_WIDTH)), full2((1, SGU_WIDTH)),
                  pl.BlockSpec((SGU_GROUPS, CHUNK, CHUNK), lambda i: (0, 0, 0)), full2((CHUNK, SGU_WIDTH))],
        out_specs=pl.BlockSpec((CHUNK, SGU_WIDTH), lambda i: (i, 0)),
        out_shape=jax.ShapeDtypeStruct((SEQ, SGU_WIDTH), F32),
        compiler_params=pltpu.CompilerParams(dimension_semantics=("arbitrary",)),
    )(h_b, h_b, h_b, lg, lb, w_s, bias_full)


def _sgu_bwd(h_b, d_yb, lg, lb, w_s, w_st, bias_full, parts):
    nsteps = SEQ // CHUNK
    npart = len(parts)

    def body(u_ref, v_ref, zb_ref, dyb_ref, lg_ref, lb_ref, ws_ref, wst_ref, bias_ref, *rest):
        part_refs, rest = rest[:npart], rest[npart:]
        dhb_ref, dws_ref, dbs_ref, dlg_ref, dlb_ref, dbb_ref = rest[:6]
        land_refs, (dbias_acc, send_sems, recv_sems, local_sems) = rest[6:6 + npart], rest[6 + npart:]
        step = pl.program_id(0)
        exchange = _exchange_parts(part_refs, land_refs, send_sems, recv_sems, local_sems)
        _exchange_start(step == 0, exchange)

        @pl.when(step == 0)
        def _():
            dbb_ref[...] = jnp.zeros_like(dbb_ref)
            dws_ref[...] = jnp.zeros_like(dws_ref)
            dlg_ref[...] = jnp.zeros_like(dlg_ref)
            dlb_ref[...] = jnp.zeros_like(dlb_ref)
            dbias_acc[...] = jnp.zeros_like(dbias_acc)

        zb = zb_ref[...]
        lg = lg_ref[...]
        ug, dug, dvg, rstd, xh, vn_bf, grp, tri, tri_t, mixed, sb = _sgu_math(
            u_ref[...], v_ref[...], zb, lg, lb_ref[...], ws_ref, bias_ref[...])
        dyb = dyb_ref[...]
        dsgu = dyb * (zb * sb)
        dzb = dyb * (ug * mixed) * (sb * (1.0 + zb * (1.0 - sb)))
        du = dsgu * mixed * dug
        dmixed = dsgu * ug
        dbias_acc[...] += dmixed
        dvn = jnp.zeros((CHUNK, SGU_WIDTH), F32)
        for g in range(SGU_GROUPS):
            dm_g = jnp.where(grp == g, dmixed, 0.0).astype(BF16)
            wtt = jnp.where(tri_t, wst_ref[g], 0.0).astype(BF16)
            dvn = dvn + _dot(wtt, dm_g, _NN)
            dws_ref[g] += jnp.where(tri, _dot(dm_g, vn_bf, _NT), 0.0)
        dlg_ref[...] += jnp.sum(dvn * xh, axis=0, keepdims=True)
        dlb_ref[...] += jnp.sum(dvn, axis=0, keepdims=True)
        dxh = dvn * lg
        dvgel = rstd * (dxh - jnp.mean(dxh, axis=1, keepdims=True) - xh * jnp.mean(dxh * xh, axis=1, keepdims=True))
        _store_grad(dhb_ref, dbb_ref, 0, du)
        _store_grad(dhb_ref, dbb_ref, SGU_WIDTH, dvgel * dvg)
        _store_grad(dhb_ref, dbb_ref, 2 * SGU_WIDTH, dzb)

        @pl.when(step == nsteps - 1)
        def _():
            acc = dbias_acc[...]
            lane = lax.broadcasted_iota(jnp.int32, (CHUNK, LANES), 1)
            out = jnp.zeros((CHUNK, LANES), F32)
            for g in range(SGU_GROUPS):
                sg = jnp.sum(jnp.where(grp == g, acc, 0.0), axis=1, keepdims=True)
                out = jnp.where(lane == g, sg, out)
            dbs_ref[...] = out

        _exchange_finish(step == nsteps - 1, exchange)

    blk = lambda c: pl.BlockSpec((CHUNK, SGU_WIDTH), lambda i, c=c: (i, c))
    full2 = lambda shape: pl.BlockSpec(shape, lambda i: (0, 0))
    full3 = pl.BlockSpec((SGU_GROUPS, CHUNK, CHUNK), lambda i: (0, 0, 0))
    hbm = pl.BlockSpec(memory_space=pl.ANY)
    res = pl.pallas_call(
        body, name="sgu_bwd", grid=(nsteps,),
        in_specs=[blk(0), blk(1), blk(2), pl.BlockSpec((CHUNK, SGU_WIDTH), lambda i: (i, 0)),
                  full2((1, SGU_WIDTH)), full2((1, SGU_WIDTH)), full3, full3, full2((CHUNK, SGU_WIDTH))] + [hbm] * npart,
        out_specs=[pl.BlockSpec((CHUNK, SEG_B), lambda i: (i, 0)), full3, full2((CHUNK, LANES)),
                   full2((1, SGU_WIDTH)), full2((1, SGU_WIDTH)), full2((1, SEG_B))] + [hbm] * npart,
        out_shape=[jax.ShapeDtypeStruct((SEQ, SEG_B), BF16),
                   jax.ShapeDtypeStruct((SGU_GROUPS, CHUNK, CHUNK), F32),
                   jax.ShapeDtypeStruct((CHUNK, LANES), F32),
                   jax.ShapeDtypeStruct((1, SGU_WIDTH), F32), jax.ShapeDtypeStruct((1, SGU_WIDTH), F32),
                   jax.ShapeDtypeStruct((1, SEG_B), F32)] + [jax.ShapeDtypeStruct(p.shape, p.dtype) for p in parts],
        scratch_shapes=[pltpu.VMEM((CHUNK, SGU_WIDTH), F32)] + _exchange_sems(npart),
        compiler_params=pltpu.CompilerParams(dimension_semantics=("arbitrary",)),
    )(h_b, h_b, h_b, d_yb, lg, lb, w_s, w_st, bias_full, *parts)
    return res[:6], res[6:]


def _merge(x, o, h_a, y_b, target, w_oa, w_ob, w_out, ln_g, ln_b):
    tm = 256
    nsteps = SEQ // tm

    def body(x_ref, o_ref, ga_ref, gb_ref, za_ref, yb_ref, tgt_ref, woa_ref, wob_ref, wout_ref, lng_ref, lnb_ref,
             loss_ref, dxr_ref, dha_ref, do_ref, dyb_ref, poa_ref, pob_ref, pout_ref, dlng_ref, dlnb_ref, dba_ref,
             dwoa_ref, dwob_ref, dwout_ref):
        step = pl.program_id(0)

        @pl.when(step == 0)
        def _():
            for r in (loss_ref, dwoa_ref, dwob_ref, dwout_ref, dlng_ref, dlnb_ref, dba_ref):
                r[...] = jnp.zeros_like(r)

        o = o_ref[...]
        za = za_ref[...]
        sa = _sigmoid(za)
        ya_bf = (o * (za * sa)).astype(BF16)
        yb_bf = yb_ref[...].astype(BF16)
        woa, wob, wout = woa_ref[...], wob_ref[...], wout_ref[...]
        pa = _dot(ya_bf, woa, _NN)
        pb = _dot(yb_bf, wob, _NN)
        sga = _sigmoid(ga_ref[...])
        sgb = _sigmoid(gb_ref[...])
        merged_bf = (sga * pa + sgb * pb).astype(BF16)
        r = DN_ALPHA * x_ref[...] + _dot(merged_bf, wout, _NN)
        mu = jnp.mean(r, axis=1, keepdims=True)
        rc = r - mu
        rstd = lax.rsqrt(jnp.mean(rc * rc, axis=1, keepdims=True) + LN_EPS)
        xh = rc * rstd
        lng = lng_ref[...]
        y = xh * lng + lnb_ref[...]
        e = y - tgt_ref[...]
        loss_ref[...] += 0.5 * jnp.sum(jnp.sum(e * e, axis=1, keepdims=True) * (1.0 / D_MODEL), axis=0, keepdims=True)

        dy = e * (1.0 / D_MODEL)
        dlng_ref[...] += jnp.sum(dy * xh, axis=0, keepdims=True)
        dlnb_ref[...] += jnp.sum(dy, axis=0, keepdims=True)
        dxh = dy * lng
        dr = rstd * (dxh - jnp.mean(dxh, axis=1, keepdims=True) - xh * jnp.mean(dxh * xh, axis=1, keepdims=True))
        dxr_ref[...] = DN_ALPHA * dr
        dr_bf = dr.astype(BF16)
        dwout_ref[...] += _dot(merged_bf, dr_bf, _TN)
        dmerged = _dot(dr_bf, wout, _NT)
        dpa_bf = (dmerged * sga).astype(BF16)
        dpb_bf = (dmerged * sgb).astype(BF16)
        _store_grad(dha_ref, dba_ref, 0, dmerged * pa * (sga * (1.0 - sga)))
        _store_grad(dha_ref, dba_ref, D_MODEL, dmerged * pb * (sgb * (1.0 - sgb)))
        dwoa_ref[...] += _dot(ya_bf, dpa_bf, _TN)
        dwob_ref[...] += _dot(yb_bf, dpb_bf, _TN)
        dya = _dot(dpa_bf, woa, _NT)
        dyb_ref[...] = _dot(dpb_bf, wob, _NT)
        do_ref[...] = dya * (za * sa)
        _store_grad(dha_ref, dba_ref, 2 * D_MODEL, dya * o * (sa * (1.0 + za * (1.0 - sa))))

        @pl.when(step == nsteps - 1)
        def _():
            cols = D_MODEL // N_DEV
            for j in range(N_DEV):
                poa_ref[j] = dwoa_ref[:, cols * j:cols * (j + 1)].astype(BF16)
                pob_ref[j] = dwob_ref[:, cols * j:cols * (j + 1)].astype(BF16)
                pout_ref[j] = dwout_ref[cols * j:cols * (j + 1), :].astype(BF16)

    row = lambda w, c=0: pl.BlockSpec((tm, w), lambda i, c=c: (i, c))
    full = lambda shape: pl.BlockSpec(shape, lambda i: (0, 0))
    full3 = lambda shape: pl.BlockSpec(shape, lambda i: (0, 0, 0))
    return pl.pallas_call(
        body, name="merge", grid=(nsteps,),
        in_specs=[row(D_MODEL), row(MLA_WIDTH), row(D_MODEL, 0), row(D_MODEL, 1), row(MLA_WIDTH, 4), row(SGU_WIDTH),
                  row(D_MODEL), full((MLA_WIDTH, D_MODEL)), full((SGU_WIDTH, D_MODEL)), full((D_MODEL, D_MODEL)),
                  full((1, D_MODEL)), full((1, D_MODEL))],
        out_specs=[full((1, LANES)), row(D_MODEL), row(SEG_A), row(MLA_WIDTH), row(SGU_WIDTH),
                   full3((N_DEV, MLA_WIDTH, D_MODEL // N_DEV)), full3((N_DEV, SGU_WIDTH, D_MODEL // N_DEV)),
                   full3((N_DEV, D_MODEL // N_DEV, D_MODEL)), full((1, D_MODEL)), full((1, D_MODEL)), full((1, SEG_A))],
        out_shape=[jax.ShapeDtypeStruct((1, LANES), F32),
                   jax.ShapeDtypeStruct((SEQ, D_MODEL), F32), jax.ShapeDtypeStruct((SEQ, SEG_A), BF16),
                   jax.ShapeDtypeStruct((SEQ, MLA_WIDTH), F32), jax.ShapeDtypeStruct((SEQ, SGU_WIDTH), F32),
                   jax.ShapeDtypeStruct((N_DEV, MLA_WIDTH, D_MODEL // N_DEV), BF16),
                   jax.ShapeDtypeStruct((N_DEV, SGU_WIDTH, D_MODEL // N_DEV), BF16),
                   jax.ShapeDtypeStruct((N_DEV, D_MODEL // N_DEV, D_MODEL), BF16),
                   jax.ShapeDtypeStruct((1, D_MODEL), F32), jax.ShapeDtypeStruct((1, D_MODEL), F32),
                   jax.ShapeDtypeStruct((1, SEG_A), F32)],
        scratch_shapes=[pltpu.VMEM((MLA_WIDTH, D_MODEL), F32), pltpu.VMEM((SGU_WIDTH, D_MODEL), F32),
                        pltpu.VMEM((D_MODEL, D_MODEL), F32)],
        compiler_params=pltpu.CompilerParams(dimension_semantics=("arbitrary",), vmem_limit_bytes=VMEM_BIG),
    )(x, o, h_a, h_a, h_a, y_b, target, w_oa, w_ob, w_out, ln_g, ln_b)


def _mla_bwd(dq, dk, dv, h_c, gq, gkv, wq, wkn, wv, c_t, sa_t, sb_t):
    tm = 256
    hw = MLA_HEADS * HEAD_PAD

    def body(dq_ref, dk_ref, dv_ref, cq_ref, ckv_ref, gq_ref, gkv_ref, wq_ref, wkn_ref, wv_ref, c_ref, sa_ref, sb_ref,
             dhc_ref, puq_ref, dwkn_ref, dwv_ref, dgq_ref, dgkv_ref, dbc_ref, pre_ref, dwq_ref):
        @pl.when(pl.program_id(0) == 0)
        def _():
            for r in (dwq_ref, dwkn_ref, dwv_ref, dgq_ref, dgkv_ref, dbc_ref):
                r[...] = jnp.zeros_like(r)

        c, sa, sb = c_ref[...], sa_ref[...], sb_ref[...]
        lane = lax.broadcasted_iota(jnp.int32, (tm, LANES), 1)
        rope_lanes = jnp.logical_and(lane >= ROPE_LO, lane < ROPE_HI)

        cq = cq_ref[...]
        gq = gq_ref[...]
        rq = lax.rsqrt(jnp.sum(cq * cq, axis=1, keepdims=True) * (1.0 / Q_LORA_RANK) + RMS_EPS)
        nq = cq * rq
        cqn_bf = (nq * gq).astype(BF16)
        for h in range(MLA_HEADS):
            sl = slice(HEAD_PAD * h, HEAD_PAD * (h + 1))
            pre_ref[:, sl] = _rope_t(dq_ref[:, sl] * ATTN_SCALE, c, sa, sb).astype(BF16)
        dqpre_bf = pre_ref[...]
        dcqn = _dot(dqpre_bf, wq_ref[...], _NT)
        dwq_ref[...] += _dot(cqn_bf, dqpre_bf, _TN)
        dgq_ref[...] += jnp.sum(dcqn * nq, axis=0, keepdims=True)
        dnq = dcqn * gq
        _store_grad(dhc_ref, dbc_ref, 0,
                    rq * (dnq - nq * (jnp.sum(dnq * nq, axis=1, keepdims=True) * (1.0 / Q_LORA_RANK))))

        ckv = ckv_ref[...]
        gkv = gkv_ref[...]
        rkv = lax.rsqrt(jnp.sum(ckv * ckv, axis=1, keepdims=True) * (1.0 / KV_LORA_RANK) + RMS_EPS)
        nkv = ckv * rkv
        ckvn_bf = (nkv * gkv).astype(BF16)
        dk = dk_ref[...]
        dk_bf = dk.astype(BF16)
        dv_bf = dv_ref[...].astype(BF16)
        dckvn = _dot(dk_bf, wkn_ref[...], _NT) + _dot(dv_bf, wv_ref[...], _NT)
        dwkn_ref[...] += _dot(ckvn_bf, dk_bf, _TN)
        dwv_ref[...] += _dot(ckvn_bf, dv_bf, _TN)
        dgkv_ref[...] += jnp.sum(dckvn * nkv, axis=0, keepdims=True)
        dnkv = dckvn * gkv
        _store_grad(dhc_ref, dbc_ref, CQ_PAD, rkv * (
            dnkv - nkv * (jnp.sum(dnkv * nkv, axis=1, keepdims=True) * (1.0 / KV_LORA_RANK))))
        dkpe = jnp.zeros((tm, LANES), F32)
        for h in range(MLA_HEADS):
            dkpe = dkpe + dk[:, HEAD_PAD * h:HEAD_PAD * (h + 1)]
        _store_grad(dhc_ref, dbc_ref, CQ_PAD + LANES, _rope_t(jnp.where(rope_lanes, dkpe, 0.0), c, sa, sb))

        @pl.when(pl.program_id(0) == SEQ // tm - 1)
        def _():
            rows = Q_LORA_RANK // N_DEV
            for j in range(N_DEV):
                for h in range(MLA_HEADS):
                    puq_ref[j, :, QK_HEAD_DIM * h:QK_HEAD_DIM * (h + 1)] = dwq_ref[
                        rows * j:rows * (j + 1), HEAD_PAD * h:HEAD_PAD * h + QK_HEAD_DIM].astype(BF16)

    full = lambda shape: pl.BlockSpec(shape, lambda i: (0, 0))
    row = lambda w, c=0: pl.BlockSpec((tm, w), lambda i, c=c: (i, c))
    return pl.pallas_call(
        body, name="mla_bwd", grid=(SEQ // tm,),
        in_specs=[row(hw), row(hw), row(hw), row(CQ_PAD, 0), row(LANES, CQ_PAD // LANES),
                  full((1, CQ_PAD)), full((1, KV_LORA_RANK)), full((CQ_PAD, hw)), full((KV_LORA_RANK, hw)),
                  full((KV_LORA_RANK, hw)), row(LANES), row(LANES), row(LANES)],
        out_specs=[row(SEG_C), pl.BlockSpec((N_DEV, Q_LORA_RANK // N_DEV, MLA_HEADS * QK_HEAD_DIM), lambda i: (0, 0, 0)),
                   full((KV_LORA_RANK, hw)), full((KV_LORA_RANK, hw)),
                   full((1, CQ_PAD)), full((1, KV_LORA_RANK)), full((1, SEG_C))],
        out_shape=[jax.ShapeDtypeStruct((SEQ, SEG_C), BF16),
                   jax.ShapeDtypeStruct((N_DEV, Q_LORA_RANK // N_DEV, MLA_HEADS * QK_HEAD_DIM), BF16),
                   jax.ShapeDtypeStruct((KV_LORA_RANK, hw), F32), jax.ShapeDtypeStruct((KV_LORA_RANK, hw), F32),
                   jax.ShapeDtypeStruct((1, CQ_PAD), F32), jax.ShapeDtypeStruct((1, KV_LORA_RANK), F32),
                   jax.ShapeDtypeStruct((1, SEG_C), F32)],
        scratch_shapes=[pltpu.VMEM((tm, hw), BF16), pltpu.VMEM((CQ_PAD, hw), F32)],
        compiler_params=pltpu.CompilerParams(dimension_semantics=("arbitrary",), vmem_limit_bytes=VMEM_MID),
    )(dq, dk, dv, h_c, h_c, gq, gkv, wq, wkn, wv, c_t, sa_t, sb_t)


def _adamw_all(ws, gs, ms, vs):
    n = len(ws)
    c1 = 1.0 / (1.0 - ADAM_B1 ** ADAM_STEP)
    c2 = 1.0 / (1.0 - ADAM_B2 ** ADAM_STEP)

    def body(*refs):
        for idx in range(n):
            w, g, m, v = (refs[idx][...], refs[n + idx][...], refs[2 * n + idx][...], refs[3 * n + idx][...])
            m_new = ADAM_B1 * m + (1.0 - ADAM_B1) * g
            v_new = ADAM_B2 * v + (1.0 - ADAM_B2) * (g * g)
            delta = -ADAM_LR * ((m_new * c1) / (jnp.sqrt(v_new * c2) + ADAM_EPS) + ADAM_WD * w)
            refs[4 * n + idx][...] = delta
            refs[5 * n + idx][...] = m_new
            refs[6 * n + idx][...] = v_new

    shapes = [jax.ShapeDtypeStruct(w.shape, F32) for w in ws]
    outs = pl.pallas_call(
        body, name="adamw", out_shape=shapes * 3,
        compiler_params=pltpu.CompilerParams(vmem_limit_bytes=VMEM_BIG),
    )(*ws, *gs, *ms, *vs)
    return outs[:n], outs[n:2 * n], outs[2 * n:]


SHARD_W = IN_WIDTH // N_DEV

_PIECES = [(0, 384, 2, 0), (384, 512, 2, CQ_PAD), (512, 544, 2, CQ_PAD + LANES + ROPE_LO),
           (544, 1056, 0, 2 * D_MODEL), (1056, 1568, 1, 0), (1568, 2080, 1, SGU_WIDTH),
           (2080, 2592, 1, 2 * SGU_WIDTH), (2592, 3616, 0, 0), (3616, 4640, 0, D_MODEL)]


def _column_runs():
    runs = []
    for n0, n1, seg, d0 in _PIECES:
        for j in range(N_DEV):
            lo, hi = max(n0, j * SHARD_W), min(n1, (j + 1) * SHARD_W)
            if lo < hi:
                runs.append((j, lo - j * SHARD_W, hi - j * SHARD_W, seg, d0 + lo - n0))
    return runs


def _mesh_pos():
    return lax.axis_index("x"), lax.axis_index("y"), lax.axis_index("c")


def _remote(src, dst, send_sems, recv_sems, k, to):
    return pltpu.make_async_remote_copy(src_ref=src, dst_ref=dst, send_sem=send_sems.at[k], recv_sem=recv_sems.at[k],
                                        device_id=to, device_id_type=pl.DeviceIdType.MESH)


def _gather_exchange(gats, send_sems, recv_sems, meanwhile=None):
    x, y, c = _mesh_pos()
    me, sibling = (x, y, c), (x, y, 1 - c)
    chips = [(1 - x, y), (x, 1 - y), (1 - x, 1 - y)]

    def copy(a, k, blk, to):
        slab = gats[a].at[4 * blk[0] + 2 * blk[1] + blk[2]]
        return _remote(slab, slab, send_sems, recv_sems, 7 * a + k, to)

    arrays = range(len(gats))
    first = [copy(a, 1 + j, me, (*chip, c)) for j, chip in enumerate(chips) for a in arrays]
    first += [copy(a, 0, me, sibling) for a in arrays]
    for cp in first:
        cp.start()
    if meanwhile is not None:
        meanwhile()
    passed = []
    for j, chip in enumerate(chips):
        for a in arrays:
            copy(a, 1 + j, (*chip, c), me).wait_recv()
            fwd = copy(a, 4 + j, (*chip, c), sibling)
            fwd.start()
            passed.append(fwd)
    for a in arrays:
        copy(a, 0, sibling, me).wait_recv()
    for j, chip in enumerate(chips):
        for a in arrays:
            copy(a, 4 + j, (*chip, 1 - c), me).wait_recv()
    for cp in first + passed:
        cp.wait_send()


def _gather_behind(own, gats, send_sems, recv_sems, local_sems, step, mid, last):
    x, y, c = _mesh_pos()
    me, sibling = (x, y, c), (x, y, 1 - c)
    chips = [(1 - x, y), (x, 1 - y), (1 - x, 1 - y)]
    arrays = range(len(gats))

    def copy(a, k, blk, to, src=None):
        slab = gats[a].at[4 * blk[0] + 2 * blk[1] + blk[2]]
        return _remote(slab if src is None else src, slab, send_sems, recv_sems, 7 * a + k, to)

    first = [copy(a, 1 + j, me, (*chip, c), src=own[a]) for j, chip in enumerate(chips) for a in arrays]
    first += [copy(a, 0, me, sibling, src=own[a]) for a in arrays]
    local = [pltpu.make_async_copy(own[a], gats[a].at[4 * x + 2 * y + c], local_sems.at[a]) for a in arrays]
    passed = [copy(a, 4 + j, (*chip, c), sibling) for j, chip in enumerate(chips) for a in arrays]

    @pl.when(step == 0)
    def _():
        for cp in first + local:
            cp.start()

    @pl.when(step == mid)
    def _():
        for j, chip in enumerate(chips):
            for a in arrays:
                copy(a, 1 + j, (*chip, c), me).wait_recv()
        for cp in passed:
            cp.start()

    @pl.when(step == last)
    def _():
        for a in arrays:
            copy(a, 0, sibling, me).wait_recv()
        for j, chip in enumerate(chips):
            for a in arrays:
                copy(a, 4 + j, (*chip, 1 - c), me).wait_recv()
        for cp in first + passed:
            cp.wait_send()
        for cp in local:
            cp.wait()


def _gather_first(w_in, w_uq2, w_oa, w_ob, w_out, x2, pos_col, invf_lane):
    hw = MLA_HEADS * HEAD_PAD
    uq_rows = Q_LORA_RANK // N_DEV
    rows = 256

    def body(win_ref, wuq_ref, woa_ref, wob_ref, wout_ref, x_ref, pos_ref, invf_ref,
             wc_ref, wq_ref, winb_ref, oab_ref, obb_ref, outb_ref, xb_ref, xt_ref, c_ref, sa_ref, sb_ref,
             g_uq, blk0, send_sems, recv_sems):
        def local_work():
            for i in range(SEQ // rows):
                xi = x_ref[rows * i:rows * (i + 1), :]
                xb_ref[rows * i:rows * (i + 1), :] = xi.astype(BF16)
                xt_ref[:, rows * i:rows * (i + 1)] = xi.T.astype(BF16)
            ang = pos_ref[...].astype(F32) * invf_ref[...]
            cs, sn = jnp.cos(ang), jnp.sin(ang)
            lane = lax.broadcasted_iota(jnp.int32, ang.shape, 1)
            c_ref[...] = jnp.where(lane < ROPE_LO, 1.0, jnp.where(lane < ROPE_HI, cs, 0.0))
            sa_ref[...] = jnp.where(jnp.logical_and(lane >= ROPE_LO, lane < ROPE_MID), -sn, 0.0)
            sb_ref[...] = jnp.where(jnp.logical_and(lane >= ROPE_MID, lane < ROPE_HI), sn, 0.0)

        x, y, c = _mesh_pos()
        me = (x, y, c)
        winb_ref[...] = win_ref[0].astype(BF16)
        oab_ref[...] = woa_ref[0].astype(BF16)
        obb_ref[...] = wob_ref[0].astype(BF16)
        outb_ref[...] = wout_ref[0].astype(BF16)
        g_uq[4 * x + 2 * y + c] = wuq_ref[...].astype(BF16)

        chip0 = jnp.logical_and(x == 0, y == 0)
        dev0 = jnp.logical_and(chip0, c == 0)
        north = c == 1
        targets = [(0, 0, 1), (1, 0, 0), (0, 1, 0), (1, 1, 0)]

        def bcopy(k, to):
            return _remote(blk0, blk0, send_sems, recv_sems, 7 + k, to)

        @pl.when(dev0)
        def _():
            blk0[...] = winb_ref[...]
            for k, to in enumerate(targets):
                bcopy(k, to).start()

        _gather_exchange([g_uq], send_sems, recv_sems, meanwhile=local_work)

        for k, (cx, cy, _) in enumerate(targets[1:], start=1):
            @pl.when(jnp.logical_and(jnp.logical_and(x == cx, y == cy), c == 0))
            def _(k=k, cx=cx, cy=cy):
                bcopy(k, me).wait_recv()
                onward = bcopy(4, (cx, cy, 1))
                onward.start()
                onward.wait_send()

        @pl.when(jnp.logical_and(chip0, north))
        def _():
            bcopy(0, me).wait_recv()

        @pl.when(jnp.logical_and(jnp.logical_not(chip0), north))
        def _():
            bcopy(4, me).wait_recv()

        @pl.when(dev0)
        def _():
            for k, to in enumerate(targets):
                bcopy(k, to).wait_send()

        for j, s0, s1, seg, d0 in _column_runs():
            if seg == 2:
                wc_ref[:, d0:d0 + (s1 - s0)] = blk0[:, s0:s1]
        zeros = lambda r, w: jnp.zeros((r, w), BF16)
        wc_ref[:, Q_LORA_RANK:CQ_PAD] = zeros(D_MODEL, CQ_PAD - Q_LORA_RANK)
        wc_ref[:, CQ_PAD + LANES:CQ_PAD + LANES + ROPE_LO] = zeros(D_MODEL, ROPE_LO)
        wc_ref[:, CQ_PAD + LANES + ROPE_HI:SEG_C] = zeros(D_MODEL, LANES - ROPE_HI)
        wq_ref[Q_LORA_RANK:CQ_PAD, :] = zeros(CQ_PAD - Q_LORA_RANK, hw)
        for h in range(MLA_HEADS):
            wq_ref[0:Q_LORA_RANK, HEAD_PAD * h + QK_HEAD_DIM:HEAD_PAD * (h + 1)] = zeros(Q_LORA_RANK, HEAD_PAD - QK_HEAD_DIM)
        for j in range(N_DEV):
            for h in range(MLA_HEADS):
                wq_ref[uq_rows * j:uq_rows * (j + 1), HEAD_PAD * h:HEAD_PAD * h + QK_HEAD_DIM] = g_uq[
                    j, :, QK_HEAD_DIM * h:QK_HEAD_DIM * (h + 1)]

    vmem = pl.BlockSpec(memory_space=pltpu.VMEM)
    return pl.pallas_call(
        body, name="gather_first",
        out_shape=[jax.ShapeDtypeStruct((D_MODEL, SEG_C), BF16), jax.ShapeDtypeStruct((CQ_PAD, hw), BF16),
                   jax.ShapeDtypeStruct(w_in.shape[1:], BF16), jax.ShapeDtypeStruct(w_oa.shape[1:], BF16),
                   jax.ShapeDtypeStruct(w_ob.shape[1:], BF16), jax.ShapeDtypeStruct(w_out.shape[1:], BF16),
                   jax.ShapeDtypeStruct((SEQ, D_MODEL), BF16), jax.ShapeDtypeStruct((D_MODEL, SEQ), BF16)]
        + [jax.ShapeDtypeStruct((SEQ, LANES), F32)] * 3,
        in_specs=[vmem] * 8, out_specs=[vmem] * 11,
        scratch_shapes=[pltpu.VMEM((N_DEV, uq_rows, MLA_HEADS * QK_HEAD_DIM), BF16), pltpu.VMEM((D_MODEL, SHARD_W), BF16),
                        pltpu.SemaphoreType.DMA((12,)), pltpu.SemaphoreType.DMA((12,))],
        compiler_params=pltpu.CompilerParams(vmem_limit_bytes=VMEM_BIG),
    )(w_in, w_uq2, w_oa, w_ob, w_out, x2, pos_col, invf_lane)


def _assemble_in(g_in):
    def body(g_ref, wa_ref, wb_ref):
        segs = [wa_ref, wb_ref]
        for j, s0, s1, seg, d0 in _column_runs():
            if seg < 2:
                segs[seg][:, d0:d0 + (s1 - s0)] = g_ref[j, :, s0:s1]

    return pl.pallas_call(
        body, name="assemble_in",
        out_shape=[jax.ShapeDtypeStruct((D_MODEL, SEG_A), BF16), jax.ShapeDtypeStruct((D_MODEL, SEG_B), BF16)],
        compiler_params=pltpu.CompilerParams(vmem_limit_bytes=VMEM_MID),
    )(g_in)


def _assemble_out(g_oa, g_ob, g_out):
    cols = D_MODEL // N_DEV

    def body(goa_ref, gob_ref, gout_ref, oa_ref, ob_ref, out_ref):
        for j in range(N_DEV):
            oa_ref[:, cols * j:cols * (j + 1)] = goa_ref[j]
            ob_ref[:, cols * j:cols * (j + 1)] = gob_ref[j]
            out_ref[cols * j:cols * (j + 1), :] = gout_ref[j]

    return pl.pallas_call(
        body, name="assemble_out",
        out_shape=[jax.ShapeDtypeStruct((MLA_WIDTH, D_MODEL), BF16), jax.ShapeDtypeStruct((SGU_WIDTH, D_MODEL), BF16),
                   jax.ShapeDtypeStruct((D_MODEL, D_MODEL), BF16)],
    )(g_oa, g_ob, g_out)


C_NAT = 544


def _to_parts(dwa, dwb):
    def body(dwa_ref, dwb_ref, pin_ref):
        pin_ref[0, :, 0:C_NAT] = jnp.zeros((D_MODEL, C_NAT), BF16)
        segs = [dwa_ref, dwb_ref]
        for j, s0, s1, seg, d0 in _column_runs():
            if seg < 2:
                pin_ref[j, :, s0:s1] = segs[seg][:, d0:d0 + (s1 - s0)]

    return pl.pallas_call(body, name="to_parts", out_shape=jax.ShapeDtypeStruct((N_DEV, D_MODEL, SHARD_W), BF16),
                          compiler_params=pltpu.CompilerParams(vmem_limit_bytes=VMEM_MID))(dwa, dwb)


def _finish_grads(dwc, p_uq, p_rep, landed):
    rep_rows = p_rep.shape[1]
    c_rows = D_MODEL // N_DEV
    spec = [((c_rows, C_NAT), BF16, c_rows), (p_uq.shape[1:], BF16, p_uq.shape[1]), ((rep_rows, LANES), F32, rep_rows)]
    n = len(spec)

    def body(dwc_ref, puq_ref, prep_ref, rin_ref, roa_ref, rob_ref, rout_ref,
             gin_ref, guq_ref, goa_ref, gob_ref, gout_ref, rep_all, pc_ref, c_red, c_all, *rest):
        ras, tbs, rbs = rest[0:n], rest[n:2 * n], rest[2 * n:3 * n]
        send_sems, recv_sems = rest[3 * n], rest[3 * n + 1]
        x, y, c = _mesh_pos()
        me = 4 * x + 2 * y + c
        sibling = (x, y, 1 - c)
        parts = [pc_ref, puq_ref, prep_ref]
        outs = [c_red, guq_ref, rep_all.at[me]]

        for j, s0, s1, seg, d0 in _column_runs():
            if seg == 2:
                for r in range(N_DEV):
                    pc_ref[r, :, s0:s1] = dwc_ref[c_rows * r:c_rows * (r + 1), d0:d0 + (s1 - s0)]

        stage1 = []
        for chip in range(4):
            for a in range(n):
                cp = _remote(parts[a].at[2 * chip + (1 - c)], ras[a].at[chip], send_sems, recv_sems, 7 * a + chip, sibling)
                cp.start()
                stage1.append(cp)

        def total(ref, sl):
            acc = ref[0, sl, :].astype(F32)
            for s in range(1, N_DEV):
                acc = acc + ref[s, sl, :].astype(F32)
            return acc

        for j in range(N_DEV):
            sl = slice(c_rows * j, c_rows * (j + 1))
            gin_ref[0, sl, :] = total(rin_ref, sl)
        goa_ref[0] = total(roa_ref, slice(None))
        gob_ref[0] = total(rob_ref, slice(None))
        gout_ref[0] = total(rout_ref, slice(None))

        for cp in stage1:
            cp.wait_recv()

        def rows_loop(a, fn):
            rows, chunk = spec[a][0][0], spec[a][2]
            if rows == chunk:
                fn(pl.ds(0, rows))
            else:
                def step(i, carry):
                    fn(pl.ds(pl.multiple_of(i * chunk, chunk), chunk))
                    return carry
                lax.fori_loop(0, rows // chunk, step, 0)

        def chip_sum(a, chip, sl):
            return parts[a][2 * chip + c, sl, :].astype(F32) + ras[a][chip, sl, :].astype(F32)

        others = [(1 - x, y), (x, 1 - y), (1 - x, 1 - y)]
        stage2 = []
        for k, (cx, cy) in enumerate(others):
            for a in range(n):
                def fill(sl, a=a, k=k, chip=2 * cx + cy):
                    tbs[a][k, sl, :] = chip_sum(a, chip, sl).astype(spec[a][1])
                rows_loop(a, fill)
                cp = _remote(tbs[a].at[k], rbs[a].at[k], send_sems, recv_sems, 7 * a + 4 + k, (cx, cy, c))
                cp.start()
                stage2.append(cp)
        for cp in stage2:
            cp.wait_recv()
        for a in range(n):
            def final(sl, a=a):
                acc = chip_sum(a, 2 * x + y, sl)
                for k in range(3):
                    acc = acc + rbs[a][k, sl, :].astype(F32)
                outs[a][sl, :] = acc
            rows_loop(a, final)
        for cp in stage1 + stage2:
            cp.wait_send()

        c_all[me] = c_red[...].astype(BF16)
        _gather_exchange([rep_all, c_all], rest[3 * n + 2], rest[3 * n + 3])

        dev0 = jnp.where(me == 0, 1.0, 0.0)
        for j in range(N_DEV):
            sl = slice(c_rows * j, c_rows * (j + 1))
            gin_ref[0, sl, 0:C_NAT] += dev0 * c_all[j].astype(F32)

    vmem = pl.BlockSpec(memory_space=pltpu.VMEM)
    scratch = [pltpu.VMEM((N_DEV, c_rows, C_NAT), BF16), pltpu.VMEM((c_rows, C_NAT), F32),
               pltpu.VMEM((N_DEV, c_rows, C_NAT), BF16)]
    for lead in (4, 3, 3):
        scratch += [pltpu.VMEM((lead,) + tuple(shape), dt) for shape, dt, _ in spec]
    scratch += [pltpu.SemaphoreType.DMA((7 * n,)), pltpu.SemaphoreType.DMA((7 * n,)),
                pltpu.SemaphoreType.DMA((14,)), pltpu.SemaphoreType.DMA((14,))]
    return pl.pallas_call(
        body, name="finish_grads",
        out_shape=[jax.ShapeDtypeStruct((1,) + landed[0].shape[1:], F32), jax.ShapeDtypeStruct(p_uq.shape[1:], F32)]
        + [jax.ShapeDtypeStruct((1,) + r.shape[1:], F32) for r in landed[1:]]
        + [jax.ShapeDtypeStruct((N_DEV, rep_rows, LANES), F32)],
        in_specs=[vmem] * 7, out_specs=[vmem] * 6, scratch_shapes=scratch,
        compiler_params=pltpu.CompilerParams(vmem_limit_bytes=VMEM_BIG),
    )(dwc, p_uq, p_rep, *landed)


_O_CQ, _O_CKV, _O_KPE, _O_ZA, _O_U, _O_V, _O_ZB, _O_GA, _O_GB = 0, 384, 512, 544, 1056, 1568, 2080, 2592, 3616


def _to_segments(w):
    z = lambda n: jnp.zeros(w.shape[:-1] + (n,), w.dtype)
    seg_a = jnp.concatenate([w[..., _O_GA:_O_GB], w[..., _O_GB:IN_WIDTH], w[..., _O_ZA:_O_U]], axis=-1)
    seg_b = jnp.concatenate([w[..., _O_U:_O_V], w[..., _O_V:_O_ZB], w[..., _O_ZB:_O_GA]], axis=-1)
    seg_c = jnp.concatenate([w[..., _O_CQ:_O_CKV], z(CQ_PAD - Q_LORA_RANK), w[..., _O_CKV:_O_KPE],
                             z(ROPE_LO), w[..., _O_KPE:_O_ZA], z(LANES - ROPE_HI)], axis=-1)
    return seg_a, seg_b, seg_c


def _from_segments(seg_a, seg_b, seg_c):
    kpe0 = CQ_PAD + LANES + ROPE_LO
    return jnp.concatenate([
        seg_c[..., 0:Q_LORA_RANK], seg_c[..., CQ_PAD:CQ_PAD + LANES], seg_c[..., kpe0:kpe0 + QK_ROPE_DIM],
        seg_a[..., 2 * D_MODEL:SEG_A], seg_b, seg_a[..., 0:2 * D_MODEL]], axis=-1)


def kernel(x, positions, w_in, b_in, g_q, w_uq, g_kv, w_ukv, w_oa, sgu_ln_g, sgu_ln_b, w_s, b_s, w_ob, w_out, ln_g, ln_b, loss_target, m_w_in, m_b_in, m_g_q, m_w_uq, m_g_kv, m_w_ukv, m_w_oa, m_sgu_ln_g, m_sgu_ln_b, m_w_s, m_b_s, m_w_ob, m_w_out, m_ln_g, m_ln_b, v_w_in, v_b_in, v_g_q, v_w_uq, v_g_kv, v_w_ukv, v_w_oa, v_sgu_ln_g, v_sgu_ln_b, v_w_s, v_b_s, v_w_ob, v_w_out, v_ln_g, v_ln_b):
    w_uq2 = w_uq[0].reshape(Q_LORA_RANK // N_DEV, MLA_HEADS * QK_HEAD_DIM)
    inv_freq = ROPE_THETA ** (-jnp.arange(0, QK_ROPE_DIM, 2, dtype=F32) / QK_ROPE_DIM)
    invf_lane = jnp.concatenate([jnp.zeros((ROPE_LO,), F32), inv_freq, inv_freq,
                                 jnp.zeros((LANES - ROPE_HI,), F32)]).reshape(1, LANES)
    first = _gather_first(w_in, w_uq2, w_oa, w_ob, w_out, x[0], positions.reshape(SEQ, 1), invf_lane)
    partials = _local_step(x[0], loss_target[0], first, b_in, g_q, g_kv, w_ukv, sgu_ln_g, sgu_ln_b, w_s, b_s, ln_g, ln_b)
    weights = dict(w_in=w_in, b_in=b_in, g_q=g_q, w_uq=w_uq, g_kv=g_kv, w_ukv=w_ukv, w_oa=w_oa, sgu_ln_g=sgu_ln_g,
                   sgu_ln_b=sgu_ln_b, w_s=w_s, b_s=b_s, w_ob=w_ob, w_out=w_out, ln_g=ln_g, ln_b=ln_b)
    moms = dict(w_in=m_w_in, b_in=m_b_in, g_q=m_g_q, w_uq=m_w_uq, g_kv=m_g_kv, w_ukv=m_w_ukv, w_oa=m_w_oa,
                sgu_ln_g=m_sgu_ln_g, sgu_ln_b=m_sgu_ln_b, w_s=m_w_s, b_s=m_b_s, w_ob=m_w_ob, w_out=m_w_out,
                ln_g=m_ln_g, ln_b=m_ln_b)
    vars_ = dict(w_in=v_w_in, b_in=v_b_in, g_q=v_g_q, w_uq=v_w_uq, g_kv=v_g_kv, w_ukv=v_w_ukv, w_oa=v_w_oa,
                 sgu_ln_g=v_sgu_ln_g, sgu_ln_b=v_sgu_ln_b, w_s=v_w_s, b_s=v_b_s, w_ob=v_w_ob, w_out=v_w_out,
                 ln_g=v_ln_g, ln_b=v_ln_b)
    return _reduce_and_update(partials, weights, moms, vars_)


def _local_step(x2, tgt, first, b_in, g_q, g_kv, w_ukv, sgu_ln_g, sgu_ln_b, w_s, b_s, ln_g, ln_b):
    wc, wq, win_b, oa_b, ob_b, out_b, x_bf, xt_bf, c_t, sa_t, sb_t = first
    ba, bb, bc = _to_segments(b_in)
    w_ukv_bf = w_ukv[0].astype(BF16)
    wkn = jnp.pad(w_ukv_bf[:, :, :QK_NOPE_DIM], ((0, 0), (0, 0), (0, HEAD_PAD - QK_NOPE_DIM))).reshape(KV_LORA_RANK, -1)
    wv = jnp.pad(w_ukv_bf[:, :, QK_NOPE_DIM:], ((0, 0), (0, 0), (0, HEAD_PAD - V_HEAD_DIM))).reshape(KV_LORA_RANK, -1)
    gq = jnp.pad(g_q, ((0, 0), (0, CQ_PAD - Q_LORA_RANK)))
    bias_full = jnp.repeat(b_s[0].T, SGU_GROUP_DIM, axis=1)
    w_s3 = w_s[0]
    w_st3 = jnp.swapaxes(w_s3, 1, 2)

    h_c = _mm(x_bf, wc, bias=bc, tm=512, tn=SEG_C, name="in_proj_c")
    q, k, kt, vx, vxt = _mla_prep(h_c, gq, g_kv, wq, wkn, wv, c_t, sa_t, sb_t)
    o, lse, (g_in,) = _attn_fwd(q, kt, vx, (win_b,))
    wa, wb = _assemble_in(g_in)
    h_a, (g_out,) = _mm(x_bf, wa, bias=ba, own=(out_b,), tm=512, tn=SEG_A // 2, name="in_proj_a")
    h_b, (g_oa, g_ob) = _mm(x_bf, wb, bias=bb, own=(oa_b, ob_b), tm=512, tn=SEG_B // 2, name="in_proj_b")
    y_b = _sgu_fwd(h_b, sgu_ln_g, sgu_ln_b, w_s3, bias_full)
    w_oa_f, w_ob_f, w_out_f = _assemble_out(g_oa, g_ob, g_out)

    (loss_row, dx_res, dh_a, d_o, d_yb, p_oa, p_ob, p_out, d_lng, d_lnb, d_ba) = _merge(
        x2, o, h_a, y_b, tgt, w_oa_f, w_ob_f, w_out_f, ln_g, ln_b)
    (dh_b, d_ws, d_bs_t, d_slg, d_slb, d_bb), (r_out,) = _sgu_bwd(h_b, d_yb, sgu_ln_g, sgu_ln_b, w_s3, w_st3, bias_full,
                                                                 (p_out,))
    d_wa, (r_oa, r_ob) = _mm(xt_bf, dh_a, out_dtype=BF16, parts=(p_oa, p_ob), tm=512, tn=512, name="dw_in_a")
    d_wb = _mm(xt_bf, dh_b, out_dtype=BF16, tm=512, tn=512, name="dw_in_b")
    dq, dk, dv, landed_in = _attn_bwd(q, kt, k, vxt, d_o, o, lse, (_to_parts(d_wa, d_wb),))
    landed = (*landed_in, r_oa, r_ob, r_out)
    dh_c, p_uq, d_wkn, d_wv, d_gq, d_gkv, d_bc = _mla_bwd(dq, dk, dv, h_c, gq, g_kv, wq, wkn, wv, c_t, sa_t, sb_t)
    d_wc = _mm(xt_bf, dh_c, out_dtype=BF16, tm=512, tn=SEG_C, name="dw_in_c")

    dx = _mm(dh_a, wa, tb=True, add=dx_res, tm=512, tn=D_MODEL, name="dx_a")
    dx = _mm(dh_b, wb, tb=True, add=dx, tm=512, tn=D_MODEL, name="dx_b")
    dx = _mm(dh_c, wc, tb=True, add=dx, tm=512, tn=D_MODEL, name="dx_c")

    p_b_in = _from_segments(d_ba, d_bb, d_bc)
    p_w_ukv = jnp.concatenate([d_wkn.reshape(KV_LORA_RANK, MLA_HEADS, HEAD_PAD)[:, :, :QK_NOPE_DIM],
                               d_wv.reshape(KV_LORA_RANK, MLA_HEADS, HEAD_PAD)[:, :, :V_HEAD_DIM]], axis=-1)
    p_g_q = d_gq[:, :Q_LORA_RANK]
    p_b_s = d_bs_t[:, :SGU_GROUPS].T
    replicated = [p_b_in, p_g_q, d_gkv, p_w_ukv, d_slg, d_slb, d_ws, p_b_s, d_lng, d_lnb]
    return loss_row, dx, landed, d_wc, p_uq, replicated


_NAMES = ["w_in", "b_in", "g_q", "w_uq", "g_kv", "w_ukv", "w_oa", "sgu_ln_g", "sgu_ln_b", "w_s", "b_s", "w_ob",
          "w_out", "ln_g", "ln_b"]
_REPLICATED = ["b_in", "g_q", "g_kv", "w_ukv", "sgu_ln_g", "sgu_ln_b", "w_s", "b_s", "ln_g", "ln_b"]


def _reduce_and_update(partials, weights, moms, vars_):
    loss_row, dx, landed, d_wc, p_uq, replicated = partials
    rep_flat = jnp.concatenate([a.reshape(-1) for a in replicated] + [loss_row[0, :1]])
    rep_flat = jnp.pad(rep_flat, (0, N_DEV * PACK_R_ROWS * LANES - rep_flat.size))
    g_in, g_uq, g_oa, g_ob, g_out, rep_all = _finish_grads(d_wc, p_uq, rep_flat.reshape(N_DEV, PACK_R_ROWS, LANES), landed)
    rep_sum = rep_all.reshape(-1)
    grads, pos = dict(w_in=g_in, w_uq=g_uq, w_oa=g_oa, w_ob=g_ob, w_out=g_out), 0
    for nm in _REPLICATED:
        grads[nm] = rep_sum[pos:pos + weights[nm].size]
        pos += weights[nm].size
    loss = rep_sum[pos]
    grads = {nm: grads[nm].reshape(weights[nm].shape) for nm in _NAMES}
    deltas, new_m, new_v = _adamw_all([weights[nm] for nm in _NAMES], [grads[nm] for nm in _NAMES],
                                      [moms[nm] for nm in _NAMES], [vars_[nm] for nm in _NAMES])
    return (loss, dx.reshape(1, SEQ, D_MODEL), *[grads[nm] for nm in _NAMES], *deltas, *new_m, *new_v)
```

```python
import math

import jax
import jax.numpy as jnp
from jax import lax
from jax.experimental import pallas as pl
from jax.experimental.pallas import tpu as pltpu

F32 = jnp.float32
BF16 = jnp.bfloat16

D_MODEL = 1024
SEQ = 2048
N_DEV = 8
MLA_HEADS = 8
Q_LORA_RANK = 384
KV_LORA_RANK = 128
QK_NOPE_DIM = 64
QK_ROPE_DIM = 32
V_HEAD_DIM = 64
QK_HEAD_DIM = QK_NOPE_DIM + QK_ROPE_DIM
MLA_WIDTH = MLA_HEADS * V_HEAD_DIM
ROPE_THETA = 10000.0
SGU_GROUPS = 8
SGU_GROUP_DIM = 64
SGU_WIDTH = SGU_GROUPS * SGU_GROUP_DIM
CHUNK = 128
RMS_EPS = 1e-6
LN_EPS = 1e-5
DN_ALPHA = 2.0 ** 0.25
IN_WIDTH = 4640
ATTN_SCALE = QK_HEAD_DIM ** -0.5

ADAM_LR = 0.001
ADAM_B1 = 0.9
ADAM_B2 = 0.999
ADAM_EPS = 1e-08
ADAM_WD = 0.01
ADAM_STEP = 10

LANES = 128
HEAD_PAD = 128
ROPE_LO = QK_NOPE_DIM
ROPE_MID = ROPE_LO + QK_ROPE_DIM // 2
ROPE_HI = ROPE_LO + QK_ROPE_DIM
CQ_PAD = 512

SEG_A = 2560
SEG_B = 1536
SEG_C = 768

PACK_R_ROWS = 272
VMEM_BIG = 56 * 1024 * 1024
VMEM_MID = 40 * 1024 * 1024


def _sigmoid(x):
    return 1.0 / (1.0 + jnp.exp(-x))


def _gelu_and_grad(x):
    c0 = math.sqrt(2.0 / math.pi)
    x2 = x * x
    t = jnp.tanh(c0 * (x + 0.044715 * x * x2))
    g = 0.5 * x * (1.0 + t)
    dg = 0.5 * (1.0 + t) + 0.5 * x * (1.0 - t * t) * (c0 * (1.0 + 3.0 * 0.044715 * x2))
    return g, dg


def _dot(a, b, dims):
    return lax.dot_general(a, b, (dims, ((), ())), preferred_element_type=F32)


_NN = ((1,), (0,))
_NT = ((1,), (1,))
_TN = ((0,), (0,))


def _store_grad(dh_ref, db_ref, col, val):
    cols = slice(col, col + val.shape[1])
    dh_ref[:, cols] = val.astype(BF16)
    db_ref[:, cols] += jnp.sum(val, axis=0, keepdims=True)


def _mm(a, b, *, tb=False, bias=None, add=None, out_dtype=F32, own=(), parts=(), tm, tn, name):
    m, k = a.shape
    n = b.shape[0] if tb else b.shape[1]
    assert m % tm == 0 and n % tn == 0 and not (own and parts)
    dims = _NT if tb else _NN
    nown = len(own) + len(parts)
    nm = m // tm
    nsteps = (n // tn) * nm

    def body(*refs):
        a_ref, b_ref = refs[0], refs[1]
        pos = 2
        r = _dot(a_ref[...], b_ref[...], dims)
        if bias is not None:
            r = r + refs[pos][...]; pos += 1
        if add is not None:
            r = r + refs[pos][...]; pos += 1
        own_refs = refs[pos:pos + nown]; pos += nown
        refs[pos][...] = r.astype(out_dtype)
        if nown:
            gat_refs = refs[pos + 1:pos + 1 + nown]
            send_sems, recv_sems, local_sems = refs[pos + 1 + nown:]
            step = pl.program_id(0) * nm + pl.program_id(1)
            if own:
                _gather_behind(own_refs, gat_refs, send_sems, recv_sems, local_sems, step, nsteps - 2, nsteps - 1)
            else:
                exchange = _exchange_parts(own_refs, gat_refs, send_sems, recv_sems, local_sems)
                _exchange_start(step == 0, exchange)
                _exchange_finish(step == nsteps - 1, exchange)

    b_spec = pl.BlockSpec((tn, k), lambda j, i: (j, 0)) if tb else pl.BlockSpec((k, tn), lambda j, i: (0, j))
    in_specs, args = [pl.BlockSpec((tm, k), lambda j, i: (i, 0)), b_spec], [a, b]
    if bias is not None:
        in_specs.append(pl.BlockSpec((1, tn), lambda j, i: (0, j))); args.append(bias)
    if add is not None:
        in_specs.append(pl.BlockSpec((tm, tn), lambda j, i: (i, j))); args.append(add)
    hbm = pl.BlockSpec(memory_space=pl.ANY)
    res = pl.pallas_call(
        body, name=name, grid=(n // tn, nm), in_specs=in_specs + [hbm] * nown,
        out_specs=[pl.BlockSpec((tm, tn), lambda j, i: (i, j))] + [hbm] * nown,
        out_shape=[jax.ShapeDtypeStruct((m, n), out_dtype)]
        + [jax.ShapeDtypeStruct((N_DEV,) + o.shape, o.dtype) for o in own]
        + [jax.ShapeDtypeStruct(p.shape, p.dtype) for p in parts],
        scratch_shapes=_exchange_sems(nown) if nown else [],
        compiler_params=pltpu.CompilerParams(dimension_semantics=("arbitrary", "arbitrary"), vmem_limit_bytes=VMEM_BIG),
    )(*args, *own, *parts)
    return (res[0], res[1:]) if nown else res[0]


def _rope(x, c, sa, sb):
    return x * c + pltpu.roll(x, LANES - 16, 1) * sa + pltpu.roll(x, 16, 1) * sb


def _rope_t(dy, c, sa, sb):
    return dy * c + pltpu.roll(dy * sa, 16, 1) + pltpu.roll(dy * sb, LANES - 16, 1)


def _mla_prep(h_c, gq, gkv, wq, wkn, wvx, c_t, sa_t, sb_t):
    tm = 256
    hw = MLA_HEADS * HEAD_PAD

    def body(cq_ref, ckv_ref, kpe_ref, gq_ref, gkv_ref, wq_ref, wkn_ref, wvx_ref, c_ref, sa_ref, sb_ref,
             q_ref, k_ref, kt_ref, vx_ref, vxt_ref):
        c, sa, sb = c_ref[...], sa_ref[...], sb_ref[...]
        cq = cq_ref[...]
        rq = lax.rsqrt(jnp.sum(cq * cq, axis=1, keepdims=True) * (1.0 / Q_LORA_RANK) + RMS_EPS)
        cqn = ((cq * rq) * gq_ref[...]).astype(BF16)
        qall = _dot(cqn, wq_ref[...], _NN)
        for h in range(MLA_HEADS):
            sl = slice(HEAD_PAD * h, HEAD_PAD * (h + 1))
            q_ref[:, sl] = (_rope(qall[:, sl], c, sa, sb) * ATTN_SCALE).astype(BF16)
        ckv = ckv_ref[...]
        rkv = lax.rsqrt(jnp.sum(ckv * ckv, axis=1, keepdims=True) * (1.0 / KV_LORA_RANK) + RMS_EPS)
        ckvn = ((ckv * rkv) * gkv_ref[...]).astype(BF16)
        knall = _dot(ckvn, wkn_ref[...], _NN)
        vall = _dot(ckvn, wvx_ref[...], _NN)
        kper = _rope(kpe_ref[...], c, sa, sb)
        ones_half = (lax.broadcasted_iota(jnp.int32, (tm, HEAD_PAD), 1) >= V_HEAD_DIM).astype(F32)
        for h in range(MLA_HEADS):
            sl = slice(HEAD_PAD * h, HEAD_PAD * (h + 1))
            kh = knall[:, sl] + kper
            vh = vall[:, sl] + ones_half
            k_ref[:, sl] = kh.astype(BF16)
            kt_ref[sl, :] = kh.T.astype(BF16)
            vx_ref[:, sl] = vh.astype(BF16)
            vxt_ref[sl, :] = vh.T.astype(BF16)

    full = lambda shape: pl.BlockSpec(shape, lambda i: (0, 0))
    tab = pl.BlockSpec((tm, LANES), lambda i: (i, 0))
    row = pl.BlockSpec((tm, hw), lambda i: (i, 0))
    col = pl.BlockSpec((hw, tm), lambda i: (0, i))
    return pl.pallas_call(
        body, name="mla_prep", grid=(SEQ // tm,),
        in_specs=[pl.BlockSpec((tm, CQ_PAD), lambda i: (i, 0)),
                  pl.BlockSpec((tm, LANES), lambda i: (i, CQ_PAD // LANES)),
                  pl.BlockSpec((tm, LANES), lambda i: (i, CQ_PAD // LANES + 1)),
                  full((1, CQ_PAD)), full((1, KV_LORA_RANK)),
                  full((CQ_PAD, hw)), full((KV_LORA_RANK, hw)), full((KV_LORA_RANK, hw)), tab, tab, tab],
        out_specs=[row, row, col, row, col],
        out_shape=[jax.ShapeDtypeStruct((SEQ, hw), BF16), jax.ShapeDtypeStruct((SEQ, hw), BF16),
                   jax.ShapeDtypeStruct((hw, SEQ), BF16), jax.ShapeDtypeStruct((SEQ, hw), BF16),
                   jax.ShapeDtypeStruct((hw, SEQ), BF16)],
        compiler_params=pltpu.CompilerParams(dimension_semantics=("arbitrary",), vmem_limit_bytes=VMEM_MID),
    )(h_c, h_c, h_c, gq, gkv, wq, wkn, wvx, c_t, sa_t, sb_t)


ATT_T = 512
ATT_STRIP = 64


def _attn_fwd(q, kt, vx, own):
    t, rs = ATT_T, ATT_STRIP
    nown = len(own)
    nq = SEQ // t
    nsteps = (MLA_HEADS // 2) * nq

    def body(q_ref, kt_ref, vx_ref, *rest):
        own_refs, (o_ref, l_ref), gat_refs = rest[:nown], rest[nown:nown + 2], rest[nown + 2:2 * nown + 2]
        s_scr, p_scr, m_scr, a_scr, acc_scr, send_sems, recv_sems, local_sems = rest[2 * nown + 2:]
        qi = pl.program_id(1)
        _gather_behind(own_refs, gat_refs, send_sems, recv_sems, local_sems, pl.program_id(0) * nq + qi,
                       nsteps - 2, nsteps - 1)
        lane = lax.broadcasted_iota(jnp.int32, (t, LANES), 1)
        m_scr[...] = jnp.full((2, t, LANES), -1e30, F32)
        acc_scr[...] = jnp.zeros((2, t, LANES), F32)

        def block(j, masked):
            off = pl.multiple_of(j * t, t)
            for a in range(2):
                sl = slice(HEAD_PAD * a, HEAD_PAD * (a + 1))
                s_scr[a] = _dot(q_ref[:, sl], kt_ref[sl, pl.ds(off, t)], _NN)
                for r in range(t // rs):
                    rows = slice(rs * r, rs * (r + 1))
                    s = s_scr[a, rows, :]
                    if masked:
                        rowi = lax.broadcasted_iota(jnp.int32, (rs, t), 0) + rs * r
                        coli = lax.broadcasted_iota(jnp.int32, (rs, t), 1)
                        s = jnp.where(coli <= rowi, s, -1e30)
                    m_old = m_scr[a, rows, :]
                    m_new = jnp.maximum(m_old, jnp.max(s, axis=1, keepdims=True))
                    p_scr[a, rows, :] = jnp.exp(s - m_new[:, :1]).astype(BF16)
                    a_scr[a, rows, :] = jnp.exp(m_old - m_new)
                    m_scr[a, rows, :] = m_new
                acc_scr[a] = acc_scr[a] * a_scr[a] + _dot(p_scr[a], vx_ref[pl.ds(off, t), sl], _NN)

        def step(j, carry):
            block(j, False)
            return carry
        lax.fori_loop(0, qi, step, 0)
        block(qi, True)
        res = []
        for a in range(2):
            acc = acc_scr[a]
            l = acc[:, V_HEAD_DIM:V_HEAD_DIM + 1]
            res.append((acc / l, m_scr[a] + jnp.log(l)))
        o_ref[...] = jnp.where(lane < V_HEAD_DIM, res[0][0], pltpu.roll(res[1][0], V_HEAD_DIM, 1))
        l_ref[...] = jnp.where(lane < V_HEAD_DIM, res[0][1], res[1][1])

    hbm = pl.BlockSpec(memory_space=pl.ANY)
    res = pl.pallas_call(
        body, name="attn_fwd", grid=(MLA_HEADS // 2, nq),
        in_specs=[pl.BlockSpec((t, 2 * HEAD_PAD), lambda p, i: (i, p)),
                  pl.BlockSpec((2 * HEAD_PAD, SEQ), lambda p, i: (p, 0)),
                  pl.BlockSpec((SEQ, 2 * HEAD_PAD), lambda p, i: (0, p))] + [hbm] * nown,
        out_specs=[pl.BlockSpec((t, LANES), lambda p, i: (i, p)),
                   pl.BlockSpec((t, LANES), lambda p, i: (i, p))] + [hbm] * nown,
        out_shape=[jax.ShapeDtypeStruct((SEQ, MLA_WIDTH), F32), jax.ShapeDtypeStruct((SEQ, MLA_WIDTH), F32)]
        + [jax.ShapeDtypeStruct((N_DEV,) + a.shape, a.dtype) for a in own],
        scratch_shapes=[pltpu.VMEM((2, t, t), F32), pltpu.VMEM((2, t, t), BF16), pltpu.VMEM((2, t, LANES), F32),
                        pltpu.VMEM((2, t, LANES), F32), pltpu.VMEM((2, t, LANES), F32)] + _exchange_sems(nown),
        compiler_params=pltpu.CompilerParams(dimension_semantics=("arbitrary", "arbitrary"), vmem_limit_bytes=VMEM_MID),
    )(q, kt, vx, *own)
    return res[0], res[1], res[2:]


def _exchange_parts(parts, lands, send_sems, recv_sems, local_sems):
    x, y, c = _mesh_pos()
    me = 4 * x + 2 * y + c
    peers = [(x, y, 1 - c), (1 - x, y, c), (x, 1 - y, c), (1 - x, 1 - y, c),
             (1 - x, y, 1 - c), (x, 1 - y, 1 - c), (1 - x, 1 - y, 1 - c)]
    remote, local = [], []
    for a, (part, land) in enumerate(zip(parts, lands)):
        for k, peer in enumerate(peers):
            t = 4 * peer[0] + 2 * peer[1] + peer[2]
            remote.append(_remote(part.at[t], land.at[me], send_sems, recv_sems, 7 * a + k, peer))
        local.append(pltpu.make_async_copy(part.at[me], land.at[me], local_sems.at[a]))
    return remote, local


def _exchange_start(first_step, exchange):
    remote, local = exchange

    @pl.when(first_step)
    def _():
        for cp in remote + local:
            cp.start()


def _exchange_finish(last_step, exchange):
    remote, local = exchange

    @pl.when(last_step)
    def _():
        for cp in remote:
            cp.wait_recv()
        for cp in remote:
            cp.wait_send()
        for cp in local:
            cp.wait()


def _exchange_sems(npart):
    return [pltpu.SemaphoreType.DMA((7 * npart,)), pltpu.SemaphoreType.DMA((7 * npart,)),
            pltpu.SemaphoreType.DMA((npart,))]


def _attn_bwd(q, kt, k, vxt, d_o, o, lse, parts):
    t, rs = ATT_T, ATT_STRIP
    nq = SEQ // t
    npart = len(parts)
    nsteps = MLA_HEADS // 2

    def body(q_ref, kt_ref, k_ref, vxt_ref, do_ref, o_ref, l_ref, *rest):
        part_refs, rest = rest[:npart], rest[npart:]
        dq_ref, dk_ref, dv_ref = rest[:3]
        land_refs, rest = rest[3:3 + npart], rest[3 + npart:]
        s_scr, dp_scr, p_scr, ds_scr, st_scr, send_sems, recv_sems, local_sems = rest
        exchange = _exchange_parts(part_refs, land_refs, send_sems, recv_sems, local_sems)
        _exchange_start(pl.program_id(0) == 0, exchange)
        dk_ref[...] = jnp.zeros_like(dk_ref)
        dv_ref[...] = jnp.zeros_like(dv_ref)
        lane = lax.broadcasted_iota(jnp.int32, (t, LANES), 1)

        def qtile(i, carry):
            ioff = pl.multiple_of(i * t, t)
            do_i = do_ref[pl.ds(ioff, t), :]
            o_i = o_ref[pl.ds(ioff, t), :]
            l_i = l_ref[pl.ds(ioff, t), :]
            for a in range(2):
                sl = slice(HEAD_PAD * a, HEAD_PAD * (a + 1))
                sel = (lane < V_HEAD_DIM) if a == 0 else (lane >= V_HEAD_DIM)
                doa = jnp.where(sel, do_i, 0.0)
                oa = o_i
                if a == 1:
                    doa = pltpu.roll(doa, V_HEAD_DIM, 1)
                    oa = pltpu.roll(o_i, V_HEAD_DIM, 1)
                st_scr[0] = jnp.broadcast_to(jnp.sum(doa * oa, axis=1, keepdims=True), (t, LANES))
                st_scr[1] = jnp.broadcast_to(l_i[:, V_HEAD_DIM * a:V_HEAD_DIM * a + 1], (t, LANES))
                doa_bf = doa.astype(BF16)
                qa = q_ref[pl.ds(ioff, t), sl]

                def block(j, masked, dq_acc, sl=sl, qa=qa, doa_bf=doa_bf):
                    joff = pl.multiple_of(j * t, t)
                    s_scr[...] = _dot(qa, kt_ref[sl, pl.ds(joff, t)], _NN)
                    dp_scr[...] = _dot(doa_bf, vxt_ref[sl, pl.ds(joff, t)], _NN)
                    for r in range(t // rs):
                        rows = slice(rs * r, rs * (r + 1))
                        p = jnp.exp(s_scr[rows, :] - st_scr[1, rows, :1])
                        if masked:
                            rowi = lax.broadcasted_iota(jnp.int32, (rs, t), 0) + rs * r
                            coli = lax.broadcasted_iota(jnp.int32, (rs, t), 1)
                            p = jnp.where(coli <= rowi, p, 0.0)
                        p_scr[rows, :] = p.astype(BF16)
                        ds_scr[rows, :] = (p * (dp_scr[rows, :] - st_scr[0, rows, :1])).astype(BF16)
                    dk_ref[pl.ds(joff, t), sl] += _dot(ds_scr[...], qa, _TN)
                    dv_ref[pl.ds(joff, t), sl] += _dot(p_scr[...], doa_bf, _TN)
                    return dq_acc + _dot(ds_scr[...], k_ref[pl.ds(joff, t), sl], _NN)

                dq_acc = lax.fori_loop(0, i, lambda j, acc: block(j, False, acc), jnp.zeros((t, HEAD_PAD), F32))
                dq_ref[pl.ds(ioff, t), sl] = block(i, True, dq_acc)
            return carry

        lax.fori_loop(0, nq, qtile, 0)
        _exchange_finish(pl.program_id(0) == nsteps - 1, exchange)

    hw = MLA_HEADS * HEAD_PAD
    wide = pl.BlockSpec((SEQ, 2 * HEAD_PAD), lambda p: (0, p))
    wide_t = pl.BlockSpec((2 * HEAD_PAD, SEQ), lambda p: (p, 0))
    narrow = pl.BlockSpec((SEQ, LANES), lambda p: (0, p))
    hbm = pl.BlockSpec(memory_space=pl.ANY)
    res = pl.pallas_call(
        body, name="attn_bwd", grid=(nsteps,),
        in_specs=[wide, wide_t, wide, wide_t, narrow, narrow, narrow] + [hbm] * npart,
        out_specs=[wide, wide, wide] + [hbm] * npart,
        out_shape=[jax.ShapeDtypeStruct((SEQ, hw), F32)] * 3 + [jax.ShapeDtypeStruct(p.shape, p.dtype) for p in parts],
        scratch_shapes=[pltpu.VMEM((t, t), F32), pltpu.VMEM((t, t), F32), pltpu.VMEM((t, t), BF16),
                        pltpu.VMEM((t, t), BF16), pltpu.VMEM((2, t, LANES), F32)] + _exchange_sems(npart),
        compiler_params=pltpu.CompilerParams(dimension_semantics=("arbitrary",), vmem_limit_bytes=VMEM_BIG),
    )(q, kt, k, vxt, d_o, o, lse, *parts)
    return res[0], res[1], res[2], res[3:]


def _sgu_math(u, v, zb, lg, lb, ws_ref, bias):
    ug, dug = _gelu_and_grad(u)
    vg, dvg = _gelu_and_grad(v)
    mu = jnp.mean(vg, axis=1, keepdims=True)
    xc = vg - mu
    rstd = lax.rsqrt(jnp.mean(xc * xc, axis=1, keepdims=True) + LN_EPS)
    xh = xc * rstd
    vn_bf = (xh * lg + lb).astype(BF16)
    grp = lax.broadcasted_iota(jnp.int32, (CHUNK, SGU_WIDTH), 1) // SGU_GROUP_DIM
    r_i = lax.broadcasted_iota(jnp.int32, (CHUNK, CHUNK), 0)
    c_i = lax.broadcasted_iota(jnp.int32, (CHUNK, CHUNK), 1)
    tri, tri_t = r_i >= c_i, r_i <= c_i
    mixed = bias
    for g in range(SGU_GROUPS):
        wt = jnp.where(tri, ws_ref[g], 0.0).astype(BF16)
        mixed = mixed + jnp.where(grp == g, _dot(wt, vn_bf, _NN), 0.0)
    sb = _sigmoid(zb)
    return ug, dug, dvg, rstd, xh, vn_bf, grp, tri, tri_t, mixed, sb


def _sgu_fwd(h_b, lg, lb, w_s, bias_full):
    def body(u_ref, v_ref, zb_ref, lg_ref, lb_ref, ws_ref, bias_ref, yb_ref):
        zb = zb_ref[...]
        ug, _, _, _, _, _, _, _, _, mixed, sb = _sgu_math(u_ref[...], v_ref[...], zb, lg_ref[...], lb_ref[...],
                                                       ws_ref, bias_ref[...])
        yb_ref[...] = (ug * mixed) * (zb * sb)

    blk = lambda c: pl.BlockSpec((CHUNK, SGU_WIDTH), lambda i, c=c: (i, c))
    full2 = lambda shape: pl.BlockSpec(shape, lambda i: (0, 0))
    return pl.pallas_call(
        body, name="sgu_fwd", grid=(SEQ // CHUNK,),
        in_specs=[blk(0), blk(1), blk(2), full2((1, SGU_WIDTH)), full2((1, SGU_WIDTH)),
                  pl.BlockSpec((SGU_GROUPS, CHUNK, CHUNK), lambda i: (0, 0, 0)), full2((CHUNK, SGU_WIDTH))],
        out_specs=pl.BlockSpec((CHUNK, SGU_WIDTH), lambda i: (i, 0)),
        out_shape=jax.ShapeDtypeStruct((SEQ, SGU_WIDTH), F32),
        compiler_params=pltpu.CompilerParams(dimension_semantics=("arbitrary",)),
    )(h_b, h_b, h_b, lg, lb, w_s, bias_full)


def _sgu_bwd(h_b, d_yb, lg, lb, w_s, w_st, bias_full, parts):
    nsteps = SEQ // CHUNK
    npart = len(parts)

    def body(u_ref, v_ref, zb_ref, dyb_ref, lg_ref, lb_ref, ws_ref, wst_ref, bias_ref, *rest):
        part_refs, rest = rest[:npart], rest[npart:]
        dhb_ref, dws_ref, dbs_ref, dlg_ref, dlb_ref, dbb_ref = rest[:6]
        land_refs, (dbias_acc, send_sems, recv_sems, local_sems) = rest[6:6 + npart], rest[6 + npart:]
        step = pl.program_id(0)
        exchange = _exchange_parts(part_refs, land_refs, send_sems, recv_sems, local_sems)
        _exchange_start(step == 0, exchange)

        @pl.when(step == 0)
        def _():
            dbb_ref[...] = jnp.zeros_like(dbb_ref)
            dws_ref[...] = jnp.zeros_like(dws_ref)
            dlg_ref[...] = jnp.zeros_like(dlg_ref)
            dlb_ref[...] = jnp.zeros_like(dlb_ref)
            dbias_acc[...] = jnp.zeros_like(dbias_acc)

        zb = zb_ref[...]
        lg = lg_ref[...]
        ug, dug, dvg, rstd, xh, vn_bf, grp, tri, tri_t, mixed, sb = _sgu_math(
            u_ref[...], v_ref[...], zb, lg, lb_ref[...], ws_ref, bias_ref[...])
        dyb = dyb_ref[...]
        dsgu = dyb * (zb * sb)
        dzb = dyb * (ug * mixed) * (sb * (1.0 + zb * (1.0 - sb)))
        du = dsgu * mixed * dug
        dmixed = dsgu * ug
        dbias_acc[...] += dmixed
        dvn = jnp.zeros((CHUNK, SGU_WIDTH), F32)
        for g in range(SGU_GROUPS):
            dm_g = jnp.where(grp == g, dmixed, 0.0).astype(BF16)
            wtt = jnp.where(tri_t, wst_ref[g], 0.0).astype(BF16)
            dvn = dvn + _dot(wtt, dm_g, _NN)
            dws_ref[g] += jnp.where(tri, _dot(dm_g, vn_bf, _NT), 0.0)
        dlg_ref[...] += jnp.sum(dvn * xh, axis=0, keepdims=True)
        dlb_ref[...] += jnp.sum(dvn, axis=0, keepdims=True)
        dxh = dvn * lg
        dvgel = rstd * (dxh - jnp.mean(dxh, axis=1, keepdims=True) - xh * jnp.mean(dxh * xh, axis=1, keepdims=True))
        _store_grad(dhb_ref, dbb_ref, 0, du)
        _store_grad(dhb_ref, dbb_ref, SGU_WIDTH, dvgel * dvg)
        _store_grad(dhb_ref, dbb_ref, 2 * SGU_WIDTH, dzb)

        @pl.when(step == nsteps - 1)
        def _():
            acc = dbias_acc[...]
            lane = lax.broadcasted_iota(jnp.int32, (CHUNK, LANES), 1)
            out = jnp.zeros((CHUNK, LANES), F32)
            for g in range(SGU_GROUPS):
                sg = jnp.sum(jnp.where(grp == g, acc, 0.0), axis=1, keepdims=True)
                out = jnp.where(lane == g, sg, out)
            dbs_ref[...] = out

        _exchange_finish(step == nsteps - 1, exchange)

    blk = lambda c: pl.BlockSpec((CHUNK, SGU_WIDTH), lambda i, c=c: (i, c))
    full2 = lambda shape: pl.BlockSpec(shape, lambda i: (0, 0))
    full3 = pl.BlockSpec((SGU_GROUPS, CHUNK, CHUNK), lambda i: (0, 0, 0))
    hbm = pl.BlockSpec(memory_space=pl.ANY)
    res = pl.pallas_call(
        body, name="sgu_bwd", grid=(nsteps,),
        in_specs=[blk(0), blk(1), blk(2), pl.BlockSpec((CHUNK, SGU_WIDTH), lambda i: (i, 0)),
                  full2((1, SGU_WIDTH)), full2((1, SGU_WIDTH)), full3, full3, full2((CHUNK, SGU_WIDTH))] + [hbm] * npart,
        out_specs=[pl.BlockSpec((CHUNK, SEG_B), lambda i: (i, 0)), full3, full2((CHUNK, LANES)),
                   full2((1, SGU_WIDTH)), full2((1, SGU_WIDTH)), full2((1, SEG_B))] + [hbm] * npart,
        out_shape=[jax.ShapeDtypeStruct((SEQ, SEG_B), BF16),
                   jax.ShapeDtypeStruct((SGU_GROUPS, CHUNK, CHUNK), F32),
                   jax.ShapeDtypeStruct((CHUNK, LANES), F32),
                   jax.ShapeDtypeStruct((1, SGU_WIDTH), F32), jax.ShapeDtypeStruct((1, SGU_WIDTH), F32),
                   jax.ShapeDtypeStruct((1, SEG_B), F32)] + [jax.ShapeDtypeStruct(p.shape, p.dtype) for p in parts],
        scratch_shapes=[pltpu.VMEM((CHUNK, SGU_WIDTH), F32)] + _exchange_sems(npart),
        compiler_params=pltpu.CompilerParams(dimension_semantics=("arbitrary",)),
    )(h_b, h_b, h_b, d_yb, lg, lb, w_s, w_st, bias_full, *parts)
    return res[:6], res[6:]


def _merge(x, o, h_a, y_b, target, w_oa, w_ob, w_out, ln_g, ln_b):
    tm = 256
    nsteps = SEQ // tm

    def body(x_ref, o_ref, ga_ref, gb_ref, za_ref, yb_ref, tgt_ref, woa_ref, wob_ref, wout_ref, lng_ref, lnb_ref,
             loss_ref, dxr_ref, dha_ref, do_ref, dyb_ref, poa_ref, pob_ref, pout_ref, dlng_ref, dlnb_ref, dba_ref,
             dwoa_ref, dwob_ref, dwout_ref):
        step = pl.program_id(0)

        @pl.when(step == 0)
        def _():
            for r in (loss_ref, dwoa_ref, dwob_ref, dwout_ref, dlng_ref, dlnb_ref, dba_ref):
                r[...] = jnp.zeros_like(r)

        o = o_ref[...]
        za = za_ref[...]
        sa = _sigmoid(za)
        ya_bf = (o * (za * sa)).astype(BF16)
        yb_bf = yb_ref[...].astype(BF16)
        woa, wob, wout = woa_ref[...], wob_ref[...], wout_ref[...]
        pa = _dot(ya_bf, woa, _NN)
        pb = _dot(yb_bf, wob, _NN)
        sga = _sigmoid(ga_ref[...])
        sgb = _sigmoid(gb_ref[...])
        merged_bf = (sga * pa + sgb * pb).astype(BF16)
        r = DN_ALPHA * x_ref[...] + _dot(merged_bf, wout, _NN)
        mu = jnp.mean(r, axis=1, keepdims=True)
        rc = r - mu
        rstd = lax.rsqrt(jnp.mean(rc * rc, axis=1, keepdims=True) + LN_EPS)
        xh = rc * rstd
        lng = lng_ref[...]
        y = xh * lng + lnb_ref[...]
        e = y - tgt_ref[...]
        loss_ref[...] += 0.5 * jnp.sum(jnp.sum(e * e, axis=1, keepdims=True) * (1.0 / D_MODEL), axis=0, keepdims=True)

        dy = e * (1.0 / D_MODEL)
        dlng_ref[...] += jnp.sum(dy * xh, axis=0, keepdims=True)
        dlnb_ref[...] += jnp.sum(dy, axis=0, keepdims=True)
        dxh = dy * lng
        dr = rstd * (dxh - jnp.mean(dxh, axis=1, keepdims=True) - xh * jnp.mean(dxh * xh, axis=1, keepdims=True))
        dxr_ref[...] = DN_ALPHA * dr
        dr_bf = dr.astype(BF16)
        dwout_ref[...] += _dot(merged_bf, dr_bf, _TN)
        dmerged = _dot(dr_bf, wout, _NT)
        dpa_bf = (dmerged * sga).astype(BF16)
        dpb_bf = (dmerged * sgb).astype(BF16)
        _store_grad(dha_ref, dba_ref, 0, dmerged * pa * (sga * (1.0 - sga)))
        _store_grad(dha_ref, dba_ref, D_MODEL, dmerged * pb * (sgb * (1.0 - sgb)))
        dwoa_ref[...] += _dot(ya_bf, dpa_bf, _TN)
        dwob_ref[...] += _dot(yb_bf, dpb_bf, _TN)
        dya = _dot(dpa_bf, woa, _NT)
        dyb_ref[...] = _dot(dpb_bf, wob, _NT)
        do_ref[...] = dya * (za * sa)
        _store_grad(dha_ref, dba_ref, 2 * D_MODEL, dya * o * (sa * (1.0 + za * (1.0 - sa))))

        @pl.when(step == nsteps - 1)
        def _():
            cols = D_MODEL // N_DEV
            for j in range(N_DEV):
                poa_ref[j] = dwoa_ref[:, cols * j:cols * (j + 1)].astype(BF16)
                pob_ref[j] = dwob_ref[:, cols * j:cols * (j + 1)].astype(BF16)
                pout_ref[j] = dwout_ref[cols * j:cols * (j + 1), :].astype(BF16)

    row = lambda w, c=0: pl.BlockSpec((tm, w), lambda i, c=c: (i, c))
    full = lambda shape: pl.BlockSpec(shape, lambda i: (0, 0))
    full3 = lambda shape: pl.BlockSpec(shape, lambda i: (0, 0, 0))
    return pl.pallas_call(
        body, name="merge", grid=(nsteps,),
        in_specs=[row(D_MODEL), row(MLA_WIDTH), row(D_MODEL, 0), row(D_MODEL, 1), row(MLA_WIDTH, 4), row(SGU_WIDTH),
                  row(D_MODEL), full((MLA_WIDTH, D_MODEL)), full((SGU_WIDTH, D_MODEL)), full((D_MODEL, D_MODEL)),
                  full((1, D_MODEL)), full((1, D_MODEL))],
        out_specs=[full((1, LANES)), row(D_MODEL), row(SEG_A), row(MLA_WIDTH), row(SGU_WIDTH),
                   full3((N_DEV, MLA_WIDTH, D_MODEL // N_DEV)), full3((N_DEV, SGU_WIDTH, D_MODEL // N_DEV)),
                   full3((N_DEV, D_MODEL // N_DEV, D_MODEL)), full((1, D_MODEL)), full((1, D_MODEL)), full((1, SEG_A))],
        out_shape=[jax.ShapeDtypeStruct((1, LANES), F32),
                   jax.ShapeDtypeStruct((SEQ, D_MODEL), F32), jax.ShapeDtypeStruct((SEQ, SEG_A), BF16),
                   jax.ShapeDtypeStruct((SEQ, MLA_WIDTH), F32), jax.ShapeDtypeStruct((SEQ, SGU_WIDTH), F32),
                   jax.ShapeDtypeStruct((N_DEV, MLA_WIDTH, D_MODEL // N_DEV), BF16),
                   jax.ShapeDtypeStruct((N_DEV, SGU_WIDTH, D_MODEL // N_DEV), BF16),
                   jax.ShapeDtypeStruct((N_DEV, D_MODEL // N_DEV, D_MODEL), BF16),
                   jax.ShapeDtypeStruct((1, D_MODEL), F32), jax.ShapeDtypeStruct((1, D_MODEL), F32),
                   jax.ShapeDtypeStruct((1, SEG_A), F32)],
        scratch_shapes=[pltpu.VMEM((MLA_WIDTH, D_MODEL), F32), pltpu.VMEM((SGU_WIDTH, D_MODEL), F32),
                        pltpu.VMEM((D_MODEL, D_MODEL), F32)],
        compiler_params=pltpu.CompilerParams(dimension_semantics=("arbitrary",), vmem_limit_bytes=VMEM_BIG),
    )(x, o, h_a, h_a, h_a, y_b, target, w_oa, w_ob, w_out, ln_g, ln_b)


def _mla_bwd(dq, dk, dv, h_c, gq, gkv, wq, wkn, wv, c_t, sa_t, sb_t):
    tm = 256
    hw = MLA_HEADS * HEAD_PAD

    def body(dq_ref, dk_ref, dv_ref, cq_ref, ckv_ref, gq_ref, gkv_ref, wq_ref, wkn_ref, wv_ref, c_ref, sa_ref, sb_ref,
             dhc_ref, puq_ref, dwkn_ref, dwv_ref, dgq_ref, dgkv_ref, dbc_ref, pre_ref, dwq_ref):
        @pl.when(pl.program_id(0) == 0)
        def _():
            for r in (dwq_ref, dwkn_ref, dwv_ref, dgq_ref, dgkv_ref, dbc_ref):
                r[...] = jnp.zeros_like(r)

        c, sa, sb = c_ref[...], sa_ref[...], sb_ref[...]
        lane = lax.broadcasted_iota(jnp.int32, (tm, LANES), 1)
        rope_lanes = jnp.logical_and(lane >= ROPE_LO, lane < ROPE_HI)

        cq = cq_ref[...]
        gq = gq_ref[...]
        rq = lax.rsqrt(jnp.sum(cq * cq, axis=1, keepdims=True) * (1.0 / Q_LORA_RANK) + RMS_EPS)
        nq = cq * rq
        cqn_bf = (nq * gq).astype(BF16)
        for h in range(MLA_HEADS):
            sl = slice(HEAD_PAD * h, HEAD_PAD * (h + 1))
            pre_ref[:, sl] = _rope_t(dq_ref[:, sl] * ATTN_SCALE, c, sa, sb).astype(BF16)
        dqpre_bf = pre_ref[...]
        dcqn = _dot(dqpre_bf, wq_ref[...], _NT)
        dwq_ref[...] += _dot(cqn_bf, dqpre_bf, _TN)
        dgq_ref[...] += jnp.sum(dcqn * nq, axis=0, keepdims=True)
        dnq = dcqn * gq
        _store_grad(dhc_ref, dbc_ref, 0,
                    rq * (dnq - nq * (jnp.sum(dnq * nq, axis=1, keepdims=True) * (1.0 / Q_LORA_RANK))))

        ckv = ckv_ref[...]
        gkv = gkv_ref[...]
        rkv = lax.rsqrt(jnp.sum(ckv * ckv, axis=1, keepdims=True) * (1.0 / KV_LORA_RANK) + RMS_EPS)
        nkv = ckv * rkv
        ckvn_bf = (nkv * gkv).astype(BF16)
        dk = dk_ref[...]
        dk_bf = dk.astype(BF16)
        dv_bf = dv_ref[...].astype(BF16)
        dckvn = _dot(dk_bf, wkn_ref[...], _NT) + _dot(dv_bf, wv_ref[...], _NT)
        dwkn_ref[...] += _dot(ckvn_bf, dk_bf, _TN)
        dwv_ref[...] += _dot(ckvn_bf, dv_bf, _TN)
        dgkv_ref[...] += jnp.sum(dckvn * nkv, axis=0, keepdims=True)
        dnkv = dckvn * gkv
        _store_grad(dhc_ref, dbc_ref, CQ_PAD, rkv * (
            dnkv - nkv * (jnp.sum(dnkv * nkv, axis=1, keepdims=True) * (1.0 / KV_LORA_RANK))))
        dkpe = jnp.zeros((tm, LANES), F32)
        for h in range(MLA_HEADS):
            dkpe = dkpe + dk[:, HEAD_PAD * h:HEAD_PAD * (h + 1)]
        _store_grad(dhc_ref, dbc_ref, CQ_PAD + LANES, _rope_t(jnp.where(rope_lanes, dkpe, 0.0), c, sa, sb))

        @pl.when(pl.program_id(0) == SEQ // tm - 1)
        def _():
            rows = Q_LORA_RANK // N_DEV
            for j in range(N_DEV):
                for h in range(MLA_HEADS):
                    puq_ref[j, :, QK_HEAD_DIM * h:QK_HEAD_DIM * (h + 1)] = dwq_ref[
                        rows * j:rows * (j + 1), HEAD_PAD * h:HEAD_PAD * h + QK_HEAD_DIM].astype(BF16)

    full = lambda shape: pl.BlockSpec(shape, lambda i: (0, 0))
    row = lambda w, c=0: pl.BlockSpec((tm, w), lambda i, c=c: (i, c))
    return pl.pallas_call(
        body, name="mla_bwd", grid=(SEQ // tm,),
        in_specs=[row(hw), row(hw), row(hw), row(CQ_PAD, 0), row(LANES, CQ_PAD // LANES),
                  full((1, CQ_PAD)), full((1, KV_LORA_RANK)), full((CQ_PAD, hw)), full((KV_LORA_RANK, hw)),
                  full((KV_LORA_RANK, hw)), row(LANES), row(LANES), row(LANES)],
        out_specs=[row(SEG_C), pl.BlockSpec((N_DEV, Q_LORA_RANK // N_DEV, MLA_HEADS * QK_HEAD_DIM), lambda i: (0, 0, 0)),
                   full((KV_LORA_RANK, hw)), full((KV_LORA_RANK, hw)),
                   full((1, CQ_PAD)), full((1, KV_LORA_RANK)), full((1, SEG_C))],
        out_shape=[jax.ShapeDtypeStruct((SEQ, SEG_C), BF16),
                   jax.ShapeDtypeStruct((N_DEV, Q_LORA_RANK // N_DEV, MLA_HEADS * QK_HEAD_DIM), BF16),
                   jax.ShapeDtypeStruct((KV_LORA_RANK, hw), F32), jax.ShapeDtypeStruct((KV_LORA_RANK, hw), F32),
                   jax.ShapeDtypeStruct((1, CQ_PAD), F32), jax.ShapeDtypeStruct((1, KV_LORA_RANK), F32),
                   jax.ShapeDtypeStruct((1, SEG_C), F32)],
        scratch_shapes=[pltpu.VMEM((tm, hw), BF16), pltpu.VMEM((CQ_PAD, hw), F32)],
        compiler_params=pltpu.CompilerParams(dimension_semantics=("arbitrary",), vmem_limit_bytes=VMEM_MID),
    )(dq, dk, dv, h_c, h_c, gq, gkv, wq, wkn, wv, c_t, sa_t, sb_t)


def _adamw_all(ws, gs, ms, vs):
    n = len(ws)
    c1 = 1.0 / (1.0 - ADAM_B1 ** ADAM_STEP)
    c2 = 1.0 / (1.0 - ADAM_B2 ** ADAM_STEP)

    def body(*refs):
        for idx in range(n):
            w, g, m, v = (refs[idx][...], refs[n + idx][...], refs[2 * n + idx][...], refs[3 * n + idx][...])
            m_new = ADAM_B1 * m + (1.0 - ADAM_B1) * g
            v_new = ADAM_B2 * v + (1.0 - ADAM_B2) * (g * g)
            delta = -ADAM_LR * ((m_new * c1) / (jnp.sqrt(v_new * c2) + ADAM_EPS) + ADAM_WD * w)
            refs[4 * n + idx][...] = delta
            refs[5 * n + idx][...] = m_new
            refs[6 * n + idx][...] = v_new

    shapes = [jax.ShapeDtypeStruct(w.shape, F32) for w in ws]
    outs = pl.pallas_call(
        body, name="adamw", out_shape=shapes * 3,
        compiler_params=pltpu.CompilerParams(vmem_limit_bytes=VMEM_BIG),
    )(*ws, *gs, *ms, *vs)
    return outs[:n], outs[n:2 * n], outs[2 * n:]


SHARD_W = IN_WIDTH // N_DEV

_PIECES = [(0, 384, 2, 0), (384, 512, 2, CQ_PAD), (512, 544, 2, CQ_PAD + LANES + ROPE_LO),
           (544, 1056, 0, 2 * D_MODEL), (1056, 1568, 1, 0), (1568, 2080, 1, SGU_WIDTH),
           (2080, 2592, 1, 2 * SGU_WIDTH), (2592, 3616, 0, 0), (3616, 4640, 0, D_MODEL)]


def _column_runs():
    runs = []
    for n0, n1, seg, d0 in _PIECES:
        for j in range(N_DEV):
            lo, hi = max(n0, j * SHARD_W), min(n1, (j + 1) * SHARD_W)
            if lo < hi:
                runs.append((j, lo - j * SHARD_W, hi - j * SHARD_W, seg, d0 + lo - n0))
    return runs


def _mesh_pos():
    return lax.axis_index("x"), lax.axis_index("y"), lax.axis_index("c")


def _remote(src, dst, send_sems, recv_sems, k, to):
    return pltpu.make_async_remote_copy(src_ref=src, dst_ref=dst, send_sem=send_sems.at[k], recv_sem=recv_sems.at[k],
                                        device_id=to, device_id_type=pl.DeviceIdType.MESH)


def _gather_exchange(gats, send_sems, recv_sems, meanwhile=None):
    x, y, c = _mesh_pos()
    me, sibling = (x, y, c), (x, y, 1 - c)
    chips = [(1 - x, y), (x, 1 - y), (1 - x, 1 - y)]

    def copy(a, k, blk, to):
        slab = gats[a].at[4 * blk[0] + 2 * blk[1] + blk[2]]
        return _remote(slab, slab, send_sems, recv_sems, 7 * a + k, to)

    arrays = range(len(gats))
    first = [copy(a, 1 + j, me, (*chip, c)) for j, chip in enumerate(chips) for a in arrays]
    first += [copy(a, 0, me, sibling) for a in arrays]
    for cp in first:
        cp.start()
    if meanwhile is not None:
        meanwhile()
    passed = []
    for j, chip in enumerate(chips):
        for a in arrays:
            copy(a, 1 + j, (*chip, c), me).wait_recv()
            fwd = copy(a, 4 + j, (*chip, c), sibling)
            fwd.start()
            passed.append(fwd)
    for a in arrays:
        copy(a, 0, sibling, me).wait_recv()
    for j, chip in enumerate(chips):
        for a in arrays:
            copy(a, 4 + j, (*chip, 1 - c), me).wait_recv()
    for cp in first + passed:
        cp.wait_send()


def _gather_behind(own, gats, send_sems, recv_sems, local_sems, step, mid, last):
    x, y, c = _mesh_pos()
    me, sibling = (x, y, c), (x, y, 1 - c)
    chips = [(1 - x, y), (x, 1 - y), (1 - x, 1 - y)]
    arrays = range(len(gats))

    def copy(a, k, blk, to, src=None):
        slab = gats[a].at[4 * blk[0] + 2 * blk[1] + blk[2]]
        return _remote(slab if src is None else src, slab, send_sems, recv_sems, 7 * a + k, to)

    first = [copy(a, 1 + j, me, (*chip, c), src=own[a]) for j, chip in enumerate(chips) for a in arrays]
    first += [copy(a, 0, me, sibling, src=own[a]) for a in arrays]
    local = [pltpu.make_async_copy(own[a], gats[a].at[4 * x + 2 * y + c], local_sems.at[a]) for a in arrays]
    passed = [copy(a, 4 + j, (*chip, c), sibling) for j, chip in enumerate(chips) for a in arrays]

    @pl.when(step == 0)
    def _():
        for cp in first + local:
            cp.start()

    @pl.when(step == mid)
    def _():
        for j, chip in enumerate(chips):
            for a in arrays:
                copy(a, 1 + j, (*chip, c), me).wait_recv()
        for cp in passed:
            cp.start()

    @pl.when(step == last)
    def _():
        for a in arrays:
            copy(a, 0, sibling, me).wait_recv()
        for j, chip in enumerate(chips):
            for a in arrays:
                copy(a, 4 + j, (*chip, 1 - c), me).wait_recv()
        for cp in first + passed:
            cp.wait_send()
        for cp in local:
            cp.wait()


def _gather_first(w_in, w_uq2, w_oa, w_ob, w_out, x2, pos_col, invf_lane):
    hw = MLA_HEADS * HEAD_PAD
    uq_rows = Q_LORA_RANK // N_DEV
    rows = 256

    def body(win_ref, wuq_ref, woa_ref, wob_ref, wout_ref, x_ref, pos_ref, invf_ref,
             wc_ref, wq_ref, winb_ref, oab_ref, obb_ref, outb_ref, xb_ref, xt_ref, c_ref, sa_ref, sb_ref,
             g_uq, blk0, send_sems, recv_sems):
        def local_work():
            for i in range(SEQ // rows):
                xi = x_ref[rows * i:rows * (i + 1), :]
                xb_ref[rows * i:rows * (i + 1), :] = xi.astype(BF16)
                xt_ref[:, rows * i:rows * (i + 1)] = xi.T.astype(BF16)
            ang = pos_ref[...].astype(F32) * invf_ref[...]
            cs, sn = jnp.cos(ang), jnp.sin(ang)
            lane = lax.broadcasted_iota(jnp.int32, ang.shape, 1)
            c_ref[...] = jnp.where(lane < ROPE_LO, 1.0, jnp.where(lane < ROPE_HI, cs, 0.0))
            sa_ref[...] = jnp.where(jnp.logical_and(lane >= ROPE_LO, lane < ROPE_MID), -sn, 0.0)
            sb_ref[...] = jnp.where(jnp.logical_and(lane >= ROPE_MID, lane < ROPE_HI), sn, 0.0)

        x, y, c = _mesh_pos()
        me = (x, y, c)
        winb_ref[...] = win_ref[0].astype(BF16)
        oab_ref[...] = woa_ref[0].astype(BF16)
        obb_ref[...] = wob_ref[0].astype(BF16)
        outb_ref[...] = wout_ref[0].astype(BF16)
        g_uq[4 * x + 2 * y + c] = wuq_ref[...].astype(BF16)

        chip0 = jnp.logical_and(x == 0, y == 0)
        dev0 = jnp.logical_and(chip0, c == 0)
        north = c == 1
        targets = [(0, 0, 1), (1, 0, 0), (0, 1, 0), (1, 1, 0)]

        def bcopy(k, to):
            return _remote(blk0, blk0, send_sems, recv_sems, 7 + k, to)

        @pl.when(dev0)
        def _():
            blk0[...] = winb_ref[...]
            for k, to in enumerate(targets):
                bcopy(k, to).start()

        _gather_exchange([g_uq], send_sems, recv_sems, meanwhile=local_work)

        for k, (cx, cy, _) in enumerate(targets[1:], start=1):
            @pl.when(jnp.logical_and(jnp.logical_and(x == cx, y == cy), c == 0))
            def _(k=k, cx=cx, cy=cy):
                bcopy(k, me).wait_recv()
                onward = bcopy(4, (cx, cy, 1))
                onward.start()
                onward.wait_send()

        @pl.when(jnp.logical_and(chip0, north))
        def _():
            bcopy(0, me).wait_recv()

        @pl.when(jnp.logical_and(jnp.logical_not(chip0), north))
        def _():
            bcopy(4, me).wait_recv()

        @pl.when(dev0)
        def _():
            for k, to in enumerate(targets):
                bcopy(k, to).wait_send()

        for j, s0, s1, seg, d0 in _column_runs():
            if seg == 2:
                wc_ref[:, d0:d0 + (s1 - s0)] = blk0[:, s0:s1]
        zeros = lambda r, w: jnp.zeros((r, w), BF16)
        wc_ref[:, Q_LORA_RANK:CQ_PAD] = zeros(D_MODEL, CQ_PAD - Q_LORA_RANK)
        wc_ref[:, CQ_PAD + LANES:CQ_PAD + LANES + ROPE_LO] = zeros(D_MODEL, ROPE_LO)
        wc_ref[:, CQ_PAD + LANES + ROPE_HI:SEG_C] = zeros(D_MODEL, LANES - ROPE_HI)
        wq_ref[Q_LORA_RANK:CQ_PAD, :] = zeros(CQ_PAD - Q_LORA_RANK, hw)
        for h in range(MLA_HEADS):
            wq_ref[0:Q_LORA_RANK, HEAD_PAD * h + QK_HEAD_DIM:HEAD_PAD * (h + 1)] = zeros(Q_LORA_RANK, HEAD_PAD - QK_HEAD_DIM)
        for j in range(N_DEV):
            for h in range(MLA_HEADS):
                wq_ref[uq_rows * j:uq_rows * (j + 1), HEAD_PAD * h:HEAD_PAD * h + QK_HEAD_DIM] = g_uq[
                    j, :, QK_HEAD_DIM * h:QK_HEAD_DIM * (h + 1)]

    vmem = pl.BlockSpec(memory_space=pltpu.VMEM)
    return pl.pallas_call(
        body, name="gather_first",
        out_shape=[jax.ShapeDtypeStruct((D_MODEL, SEG_C), BF16), jax.ShapeDtypeStruct((CQ_PAD, hw), BF16),
                   jax.ShapeDtypeStruct(w_in.shape[1:], BF16), jax.ShapeDtypeStruct(w_oa.shape[1:], BF16),
                   jax.ShapeDtypeStruct(w_ob.shape[1:], BF16), jax.ShapeDtypeStruct(w_out.shape[1:], BF16),
                   jax.ShapeDtypeStruct((SEQ, D_MODEL), BF16), jax.ShapeDtypeStruct((D_MODEL, SEQ), BF16)]
        + [jax.ShapeDtypeStruct((SEQ, LANES), F32)] * 3,
        in_specs=[vmem] * 8, out_specs=[vmem] * 11,
        scratch_shapes=[pltpu.VMEM((N_DEV, uq_rows, MLA_HEADS * QK_HEAD_DIM), BF16), pltpu.VMEM((D_MODEL, SHARD_W), BF16),
                        pltpu.SemaphoreType.DMA((12,)), pltpu.SemaphoreType.DMA((12,))],
        compiler_params=pltpu.CompilerParams(vmem_limit_bytes=VMEM_BIG),
    )(w_in, w_uq2, w_oa, w_ob, w_out, x2, pos_col, invf_lane)


def _assemble_in(g_in):
    def body(g_ref, wa_ref, wb_ref):
        segs = [wa_ref, wb_ref]
        for j, s0, s1, seg, d0 in _column_runs():
            if seg < 2:
                segs[seg][:, d0:d0 + (s1 - s0)] = g_ref[j, :, s0:s1]

    return pl.pallas_call(
        body, name="assemble_in",
        out_shape=[jax.ShapeDtypeStruct((D_MODEL, SEG_A), BF16), jax.ShapeDtypeStruct((D_MODEL, SEG_B), BF16)],
        compiler_params=pltpu.CompilerParams(vmem_limit_bytes=VMEM_MID),
    )(g_in)


def _assemble_out(g_oa, g_ob, g_out):
    cols = D_MODEL // N_DEV

    def body(goa_ref, gob_ref, gout_ref, oa_ref, ob_ref, out_ref):
        for j in range(N_DEV):
            oa_ref[:, cols * j:cols * (j + 1)] = goa_ref[j]
            ob_ref[:, cols * j:cols * (j + 1)] = gob_ref[j]
            out_ref[cols * j:cols * (j + 1), :] = gout_ref[j]

    return pl.pallas_call(
        body, name="assemble_out",
        out_shape=[jax.ShapeDtypeStruct((MLA_WIDTH, D_MODEL), BF16), jax.ShapeDtypeStruct((SGU_WIDTH, D_MODEL), BF16),
                   jax.ShapeDtypeStruct((D_MODEL, D_MODEL), BF16)],
    )(g_oa, g_ob, g_out)


C_NAT = 544


def _to_parts(dwa, dwb):
    def body(dwa_ref, dwb_ref, pin_ref):
        pin_ref[0, :, 0:C_NAT] = jnp.zeros((D_MODEL, C_NAT), BF16)
        segs = [dwa_ref, dwb_ref]
        for j, s0, s1, seg, d0 in _column_runs():
            if seg < 2:
                pin_ref[j, :, s0:s1] = segs[seg][:, d0:d0 + (s1 - s0)]

    return pl.pallas_call(body, name="to_parts", out_shape=jax.ShapeDtypeStruct((N_DEV, D_MODEL, SHARD_W), BF16),
                          compiler_params=pltpu.CompilerParams(vmem_limit_bytes=VMEM_MID))(dwa, dwb)


def _dx_tail(dhs, ws, dx_res, dwc, p_uq, p_rep):
    tm = SEQ // 4
    rep_rows = p_rep.shape[1]
    c_rows = D_MODEL // N_DEV
    spec = [((c_rows, C_NAT), BF16), (p_uq.shape[1:], BF16), ((rep_rows, LANES), F32)]
    n = len(spec)

    def body(dha_ref, dhb_ref, dhc_ref, wa_ref, wb_ref, wc_ref, dxr_ref, dwc_ref, puq_ref, prep_ref,
             dx_ref, call_ref, guq_ref, repall_ref, pc_ref, c_all, rep_all, *rest):
        ras, tbs, rbs = rest[0:n], rest[n:2 * n], rest[2 * n:3 * n]
        send_sems, recv_sems, gsend, grecv = rest[3 * n:]
        step = pl.program_id(0)
        x, y, c = _mesh_pos()
        me_idx = 4 * x + 2 * y + c
        me, sibling = (x, y, c), (x, y, 1 - c)
        others = [(1 - x, y), (x, 1 - y), (1 - x, 1 - y)]
        parts = [pc_ref, puq_ref, prep_ref]
        gats = [rep_all, c_all]

        def stage1(chip, a):
            return _remote(parts[a].at[2 * chip + (1 - c)], ras[a].at[chip], send_sems, recv_sems, 7 * a + chip, sibling)

        def stage2(k, a):
            cx, cy = others[k]
            return _remote(tbs[a].at[k], rbs[a].at[k], send_sems, recv_sems, 7 * a + 4 + k, (cx, cy, c))

        def gcopy(a, k, blk, to):
            slab = gats[a].at[4 * blk[0] + 2 * blk[1] + blk[2]]
            return _remote(slab, slab, gsend, grecv, 7 * a + k, to)

        def chip_sum(a, chip):
            return parts[a][2 * chip + c].astype(F32) + ras[a][chip].astype(F32)

        @pl.when(step == 0)
        def _():
            for j, s0, s1, seg, d0 in _column_runs():
                if seg == 2:
                    for r in range(N_DEV):
                        pc_ref[r, :, s0:s1] = dwc_ref[c_rows * r:c_rows * (r + 1), d0:d0 + (s1 - s0)]
            for chip in range(4):
                for a in range(n):
                    stage1(chip, a).start()

        @pl.when(step == 1)
        def _():
            for chip in range(4):
                for a in range(n):
                    stage1(chip, a).wait_recv()
            for k, (cx, cy) in enumerate(others):
                for a in range(n):
                    tbs[a][k] = chip_sum(a, 2 * cx + cy).astype(spec[a][1])
                    stage2(k, a).start()

        @pl.when(step == 2)
        def _():
            for k in range(3):
                for a in range(n):
                    stage2(k, a).wait_recv()
            sums = []
            for a in range(n):
                acc = chip_sum(a, 2 * x + y)
                for k in range(3):
                    acc = acc + rbs[a][k].astype(F32)
                sums.append(acc)
            c_all[me_idx] = sums[0].astype(BF16)
            guq_ref[...] = sums[1]
            rep_all[me_idx] = sums[2]
            for a in range(2):
                for j, chip in enumerate(others):
                    gcopy(a, 1 + j, me, (*chip, c)).start()
                gcopy(a, 0, me, sibling).start()

        @pl.when(step == 3)
        def _():
            for j, chip in enumerate(others):
                for a in range(2):
                    gcopy(a, 1 + j, (*chip, c), me).wait_recv()
                    gcopy(a, 4 + j, (*chip, c), sibling).start()
            for a in range(2):
                gcopy(a, 0, sibling, me).wait_recv()
                for j, chip in enumerate(others):
                    gcopy(a, 4 + j, (*chip, 1 - c), me).wait_recv()
            for a in range(2):
                gcopy(a, 0, me, sibling).wait_send()
                for j, chip in enumerate(others):
                    gcopy(a, 1 + j, me, (*chip, c)).wait_send()
                    gcopy(a, 4 + j, (*chip, c), sibling).wait_send()
            for a in range(n):
                for chip in range(4):
                    stage1(chip, a).wait_send()
                for k in range(3):
                    stage2(k, a).wait_send()
            call_ref[...] = c_all[...]
            repall_ref[...] = rep_all[...]

        dx_ref[...] = (dxr_ref[...] + _dot(dha_ref[...], wa_ref[...], _NT) + _dot(dhb_ref[...], wb_ref[...], _NT)
                       + _dot(dhc_ref[...], wc_ref[...], _NT))

    row = lambda w: pl.BlockSpec((tm, w), lambda i: (i, 0))
    full = lambda shape: pl.BlockSpec(shape, lambda i: (0,) * len(shape))
    scratch = [pltpu.VMEM((N_DEV, c_rows, C_NAT), BF16), pltpu.VMEM((N_DEV, c_rows, C_NAT), BF16),
               pltpu.VMEM((N_DEV, rep_rows, LANES), F32)]
    for lead in (4, 3, 3):
        scratch += [pltpu.VMEM((lead,) + tuple(shape), dt) for shape, dt in spec]
    scratch += [pltpu.SemaphoreType.DMA((7 * n,)), pltpu.SemaphoreType.DMA((7 * n,)),
                pltpu.SemaphoreType.DMA((14,)), pltpu.SemaphoreType.DMA((14,))]
    return pl.pallas_call(
        body, name="dx_tail", grid=(SEQ // tm,),
        in_specs=[row(SEG_A), row(SEG_B), row(SEG_C), full(ws[0].shape), full(ws[1].shape), full(ws[2].shape),
                  row(D_MODEL), full(dwc.shape), full(p_uq.shape), full(p_rep.shape)],
        out_specs=[row(D_MODEL), full((N_DEV, c_rows, C_NAT)), full(p_uq.shape[1:]), full((N_DEV, rep_rows, LANES))],
        out_shape=[jax.ShapeDtypeStruct((SEQ, D_MODEL), F32), jax.ShapeDtypeStruct((N_DEV, c_rows, C_NAT), BF16),
                   jax.ShapeDtypeStruct(p_uq.shape[1:], F32), jax.ShapeDtypeStruct((N_DEV, rep_rows, LANES), F32)],
        scratch_shapes=scratch,
        compiler_params=pltpu.CompilerParams(dimension_semantics=("arbitrary",), vmem_limit_bytes=VMEM_BIG),
    )(*dhs, *ws, dx_res, dwc, p_uq, p_rep)


def _sum_landed(landed, c_all):
    c_rows = D_MODEL // N_DEV

    def body(rin_ref, roa_ref, rob_ref, rout_ref, call_ref, gin_ref, goa_ref, gob_ref, gout_ref):
        def total(ref, sl):
            acc = ref[0, sl, :].astype(F32)
            for s in range(1, N_DEV):
                acc = acc + ref[s, sl, :].astype(F32)
            return acc

        x, y, c = _mesh_pos()
        dev0 = jnp.where(4 * x + 2 * y + c == 0, 1.0, 0.0)
        for j in range(N_DEV):
            sl = slice(c_rows * j, c_rows * (j + 1))
            tot = total(rin_ref, sl)
            gin_ref[0, sl, C_NAT:SHARD_W] = tot[:, C_NAT:SHARD_W]
            gin_ref[0, sl, 0:C_NAT] = tot[:, 0:C_NAT] + dev0 * call_ref[j].astype(F32)
        goa_ref[0] = total(roa_ref, slice(None))
        gob_ref[0] = total(rob_ref, slice(None))
        gout_ref[0] = total(rout_ref, slice(None))

    return pl.pallas_call(
        body, name="sum_landed",
        out_shape=[jax.ShapeDtypeStruct((1,) + r.shape[1:], F32) for r in landed],
        compiler_params=pltpu.CompilerParams(vmem_limit_bytes=VMEM_MID),
    )(*landed, c_all)


_O_CQ, _O_CKV, _O_KPE, _O_ZA, _O_U, _O_V, _O_ZB, _O_GA, _O_GB = 0, 384, 512, 544, 1056, 1568, 2080, 2592, 3616


def _to_segments(w):
    z = lambda n: jnp.zeros(w.shape[:-1] + (n,), w.dtype)
    seg_a = jnp.concatenate([w[..., _O_GA:_O_GB], w[..., _O_GB:IN_WIDTH], w[..., _O_ZA:_O_U]], axis=-1)
    seg_b = jnp.concatenate([w[..., _O_U:_O_V], w[..., _O_V:_O_ZB], w[..., _O_ZB:_O_GA]], axis=-1)
    seg_c = jnp.concatenate([w[..., _O_CQ:_O_CKV], z(CQ_PAD - Q_LORA_RANK), w[..., _O_CKV:_O_KPE],
                             z(ROPE_LO), w[..., _O_KPE:_O_ZA], z(LANES - ROPE_HI)], axis=-1)
    return seg_a, seg_b, seg_c


def _from_segments(seg_a, seg_b, seg_c):
    kpe0 = CQ_PAD + LANES + ROPE_LO
    return jnp.concatenate([
        seg_c[..., 0:Q_LORA_RANK], seg_c[..., CQ_PAD:CQ_PAD + LANES], seg_c[..., kpe0:kpe0 + QK_ROPE_DIM],
        seg_a[..., 2 * D_MODEL:SEG_A], seg_b, seg_a[..., 0:2 * D_MODEL]], axis=-1)


def kernel(x, positions, w_in, b_in, g_q, w_uq, g_kv, w_ukv, w_oa, sgu_ln_g, sgu_ln_b, w_s, b_s, w_ob, w_out, ln_g, ln_b, loss_target, m_w_in, m_b_in, m_g_q, m_w_uq, m_g_kv, m_w_ukv, m_w_oa, m_sgu_ln_g, m_sgu_ln_b, m_w_s, m_b_s, m_w_ob, m_w_out, m_ln_g, m_ln_b, v_w_in, v_b_in, v_g_q, v_w_uq, v_g_kv, v_w_ukv, v_w_oa, v_sgu_ln_g, v_sgu_ln_b, v_w_s, v_b_s, v_w_ob, v_w_out, v_ln_g, v_ln_b):
    w_uq2 = w_uq[0].reshape(Q_LORA_RANK // N_DEV, MLA_HEADS * QK_HEAD_DIM)
    inv_freq = ROPE_THETA ** (-jnp.arange(0, QK_ROPE_DIM, 2, dtype=F32) / QK_ROPE_DIM)
    invf_lane = jnp.concatenate([jnp.zeros((ROPE_LO,), F32), inv_freq, inv_freq,
                                 jnp.zeros((LANES - ROPE_HI,), F32)]).reshape(1, LANES)
    first = _gather_first(w_in, w_uq2, w_oa, w_ob, w_out, x[0], positions.reshape(SEQ, 1), invf_lane)
    partials = _local_step(x[0], loss_target[0], first, b_in, g_q, g_kv, w_ukv, sgu_ln_g, sgu_ln_b, w_s, b_s, ln_g, ln_b)
    weights = dict(w_in=w_in, b_in=b_in, g_q=g_q, w_uq=w_uq, g_kv=g_kv, w_ukv=w_ukv, w_oa=w_oa, sgu_ln_g=sgu_ln_g,
                   sgu_ln_b=sgu_ln_b, w_s=w_s, b_s=b_s, w_ob=w_ob, w_out=w_out, ln_g=ln_g, ln_b=ln_b)
    moms = dict(w_in=m_w_in, b_in=m_b_in, g_q=m_g_q, w_uq=m_w_uq, g_kv=m_g_kv, w_ukv=m_w_ukv, w_oa=m_w_oa,
                sgu_ln_g=m_sgu_ln_g, sgu_ln_b=m_sgu_ln_b, w_s=m_w_s, b_s=m_b_s, w_ob=m_w_ob, w_out=m_w_out,
                ln_g=m_ln_g, ln_b=m_ln_b)
    vars_ = dict(w_in=v_w_in, b_in=v_b_in, g_q=v_g_q, w_uq=v_w_uq, g_kv=v_g_kv, w_ukv=v_w_ukv, w_oa=v_w_oa,
                 sgu_ln_g=v_sgu_ln_g, sgu_ln_b=v_sgu_ln_b, w_s=v_w_s, b_s=v_b_s, w_ob=v_w_ob, w_out=v_w_out,
                 ln_g=v_ln_g, ln_b=v_ln_b)
    return _reduce_and_update(partials, weights, moms, vars_)


def _local_step(x2, tgt, first, b_in, g_q, g_kv, w_ukv, sgu_ln_g, sgu_ln_b, w_s, b_s, ln_g, ln_b):
    wc, wq, win_b, oa_b, ob_b, out_b, x_bf, xt_bf, c_t, sa_t, sb_t = first
    ba, bb, bc = _to_segments(b_in)
    w_ukv_bf = w_ukv[0].astype(BF16)
    wkn = jnp.pad(w_ukv_bf[:, :, :QK_NOPE_DIM], ((0, 0), (0, 0), (0, HEAD_PAD - QK_NOPE_DIM))).reshape(KV_LORA_RANK, -1)
    wv = jnp.pad(w_ukv_bf[:, :, QK_NOPE_DIM:], ((0, 0), (0, 0), (0, HEAD_PAD - V_HEAD_DIM))).reshape(KV_LORA_RANK, -1)
    gq = jnp.pad(g_q, ((0, 0), (0, CQ_PAD - Q_LORA_RANK)))
    bias_full = jnp.repeat(b_s[0].T, SGU_GROUP_DIM, axis=1)
    w_s3 = w_s[0]
    w_st3 = jnp.swapaxes(w_s3, 1, 2)

    h_c = _mm(x_bf, wc, bias=bc, tm=512, tn=SEG_C, name="in_proj_c")
    q, k, kt, vx, vxt = _mla_prep(h_c, gq, g_kv, wq, wkn, wv, c_t, sa_t, sb_t)
    o, lse, (g_in,) = _attn_fwd(q, kt, vx, (win_b,))
    wa, wb = _assemble_in(g_in)
    h_a, (g_out,) = _mm(x_bf, wa, bias=ba, own=(out_b,), tm=512, tn=SEG_A // 2, name="in_proj_a")
    h_b, (g_oa, g_ob) = _mm(x_bf, wb, bias=bb, own=(oa_b, ob_b), tm=512, tn=SEG_B // 2, name="in_proj_b")
    y_b = _sgu_fwd(h_b, sgu_ln_g, sgu_ln_b, w_s3, bias_full)
    w_oa_f, w_ob_f, w_out_f = _assemble_out(g_oa, g_ob, g_out)

    (loss_row, dx_res, dh_a, d_o, d_yb, p_oa, p_ob, p_out, d_lng, d_lnb, d_ba) = _merge(
        x2, o, h_a, y_b, tgt, w_oa_f, w_ob_f, w_out_f, ln_g, ln_b)
    (dh_b, d_ws, d_bs_t, d_slg, d_slb, d_bb), (r_out,) = _sgu_bwd(h_b, d_yb, sgu_ln_g, sgu_ln_b, w_s3, w_st3, bias_full,
                                                                 (p_out,))
    d_wa, (r_oa, r_ob) = _mm(xt_bf, dh_a, out_dtype=BF16, parts=(p_oa, p_ob), tm=512, tn=512, name="dw_in_a")
    d_wb = _mm(xt_bf, dh_b, out_dtype=BF16, tm=512, tn=512, name="dw_in_b")
    dq, dk, dv, landed_in = _attn_bwd(q, kt, k, vxt, d_o, o, lse, (_to_parts(d_wa, d_wb),))
    landed = (*landed_in, r_oa, r_ob, r_out)
    dh_c, p_uq, d_wkn, d_wv, d_gq, d_gkv, d_bc = _mla_bwd(dq, dk, dv, h_c, gq, g_kv, wq, wkn, wv, c_t, sa_t, sb_t)
    d_wc = _mm(xt_bf, dh_c, out_dtype=BF16, tm=512, tn=SEG_C, name="dw_in_c")


    p_b_in = _from_segments(d_ba, d_bb, d_bc)
    p_w_ukv = jnp.concatenate([d_wkn.reshape(KV_LORA_RANK, MLA_HEADS, HEAD_PAD)[:, :, :QK_NOPE_DIM],
                               d_wv.reshape(KV_LORA_RANK, MLA_HEADS, HEAD_PAD)[:, :, :V_HEAD_DIM]], axis=-1)
    p_g_q = d_gq[:, :Q_LORA_RANK]
    p_b_s = d_bs_t[:, :SGU_GROUPS].T
    replicated = [p_b_in, p_g_q, d_gkv, p_w_ukv, d_slg, d_slb, d_ws, p_b_s, d_lng, d_lnb]
    return loss_row, ((dh_a, dh_b, dh_c), (wa, wb, wc), dx_res), landed, d_wc, p_uq, replicated


_NAMES = ["w_in", "b_in", "g_q", "w_uq", "g_kv", "w_ukv", "w_oa", "sgu_ln_g", "sgu_ln_b", "w_s", "b_s", "w_ob",
          "w_out", "ln_g", "ln_b"]
_REPLICATED = ["b_in", "g_q", "g_kv", "w_ukv", "sgu_ln_g", "sgu_ln_b", "w_s", "b_s", "ln_g", "ln_b"]


def _reduce_and_update(partials, weights, moms, vars_):
    loss_row, (dhs, ws, dx_res), landed, d_wc, p_uq, replicated = partials
    rep_flat = jnp.concatenate([a.reshape(-1) for a in replicated] + [loss_row[0, :1]])
    rep_flat = jnp.pad(rep_flat, (0, N_DEV * PACK_R_ROWS * LANES - rep_flat.size))
    dx, c_all, g_uq, rep_all = _dx_tail(dhs, ws, dx_res, d_wc, p_uq, rep_flat.reshape(N_DEV, PACK_R_ROWS, LANES))
    g_in, g_oa, g_ob, g_out = _sum_landed(landed, c_all)
    rep_sum = rep_all.reshape(-1)
    grads, pos = dict(w_in=g_in, w_uq=g_uq, w_oa=g_oa, w_ob=g_ob, w_out=g_out), 0
    for nm in _REPLICATED:
        grads[nm] = rep_sum[pos:pos + weights[nm].size]
        pos += weights[nm].size
    loss = rep_sum[pos]
    grads = {nm: grads[nm].reshape(weights[nm].shape) for nm in _NAMES}
    deltas, new_m, new_v = _adamw_all([weights[nm] for nm in _NAMES], [grads[nm] for nm in _NAMES],
                                      [moms[nm] for nm in _NAMES], [vars_[nm] for nm in _NAMES])
    return (loss, dx.reshape(1, SEQ, D_MODEL), *[grads[nm] for nm in _NAMES], *deltas, *new_m, *new_v)
```

```python
import math

import jax
import jax.numpy as jnp
from jax import lax
from jax.experimental import pallas as pl
from jax.experimental.pallas import tpu as pltpu

F32 = jnp.float32
BF16 = jnp.bfloat16

D_MODEL = 1024
SEQ = 2048
N_DEV = 8
MLA_HEADS = 8
Q_LORA_RANK = 384
KV_LORA_RANK = 128
QK_NOPE_DIM = 64
QK_ROPE_DIM = 32
V_HEAD_DIM = 64
QK_HEAD_DIM = QK_NOPE_DIM + QK_ROPE_DIM
MLA_WIDTH = MLA_HEADS * V_HEAD_DIM
ROPE_THETA = 10000.0
SGU_GROUPS = 8
SGU_GROUP_DIM = 64
SGU_WIDTH = SGU_GROUPS * SGU_GROUP_DIM
CHUNK = 128
RMS_EPS = 1e-6
LN_EPS = 1e-5
DN_ALPHA = 2.0 ** 0.25
IN_WIDTH = 4640
ATTN_SCALE = QK_HEAD_DIM ** -0.5

ADAM_LR = 0.001
ADAM_B1 = 0.9
ADAM_B2 = 0.999
ADAM_EPS = 1e-08
ADAM_WD = 0.01
ADAM_STEP = 10

LANES = 128
HEAD_PAD = 128
ROPE_LO = QK_NOPE_DIM
ROPE_MID = ROPE_LO + QK_ROPE_DIM // 2
ROPE_HI = ROPE_LO + QK_ROPE_DIM
CQ_PAD = 512

SEG_A = 2560
SEG_B = 1536
SEG_C = 768

PACK_R_ROWS = 272
VMEM_BIG = 56 * 1024 * 1024
VMEM_MID = 40 * 1024 * 1024


def _sigmoid(x):
    return 1.0 / (1.0 + jnp.exp(-x))


def _gelu_and_grad(x):
    c0 = math.sqrt(2.0 / math.pi)
    x2 = x * x
    t = jnp.tanh(c0 * (x + 0.044715 * x * x2))
    g = 0.5 * x * (1.0 + t)
    dg = 0.5 * (1.0 + t) + 0.5 * x * (1.0 - t * t) * (c0 * (1.0 + 3.0 * 0.044715 * x2))
    return g, dg


def _dot(a, b, dims):
    return lax.dot_general(a, b, (dims, ((), ())), preferred_element_type=F32)


_NN = ((1,), (0,))
_NT = ((1,), (1,))
_TN = ((0,), (0,))


def _store_grad(dh_ref, db_ref, col, val):
    cols = slice(col, col + val.shape[1])
    dh_ref[:, cols] = val.astype(BF16)
    db_ref[:, cols] += jnp.sum(val, axis=0, keepdims=True)


def _mm(a, b, *, tb=False, bias=None, add=None, out_dtype=F32, own=(), parts=(), tm, tn, name):
    m, k = a.shape
    n = b.shape[0] if tb else b.shape[1]
    assert m % tm == 0 and n % tn == 0 and not (own and parts)
    dims = _NT if tb else _NN
    nown = len(own) + len(parts)
    nm = m // tm
    nsteps = (n // tn) * nm

    def body(*refs):
        a_ref, b_ref = refs[0], refs[1]
        pos = 2
        r = _dot(a_ref[...], b_ref[...], dims)
        if bias is not None:
            r = r + refs[pos][...]; pos += 1
        if add is not None:
            r = r + refs[pos][...]; pos += 1
        own_refs = refs[pos:pos + nown]; pos += nown
        refs[pos][...] = r.astype(out_dtype)
        if nown:
            gat_refs = refs[pos + 1:pos + 1 + nown]
            send_sems, recv_sems, local_sems = refs[pos + 1 + nown:]
            step = pl.program_id(0) * nm + pl.program_id(1)
            if own:
                _gather_behind(own_refs, gat_refs, send_sems, recv_sems, local_sems, step, nsteps - 2, nsteps - 1)
            else:
                exchange = _exchange_parts(own_refs, gat_refs, send_sems, recv_sems, local_sems)
                _exchange_start(step == 0, exchange)
                _exchange_finish(step == nsteps - 1, exchange)

    b_spec = pl.BlockSpec((tn, k), lambda j, i: (j, 0)) if tb else pl.BlockSpec((k, tn), lambda j, i: (0, j))
    in_specs, args = [pl.BlockSpec((tm, k), lambda j, i: (i, 0)), b_spec], [a, b]
    if bias is not None:
        in_specs.append(pl.BlockSpec((1, tn), lambda j, i: (0, j))); args.append(bias)
    if add is not None:
        in_specs.append(pl.BlockSpec((tm, tn), lambda j, i: (i, j))); args.append(add)
    hbm = pl.BlockSpec(memory_space=pl.ANY)
    res = pl.pallas_call(
        body, name=name, grid=(n // tn, nm), in_specs=in_specs + [hbm] * nown,
        out_specs=[pl.BlockSpec((tm, tn), lambda j, i: (i, j))] + [hbm] * nown,
        out_shape=[jax.ShapeDtypeStruct((m, n), out_dtype)]
        + [jax.ShapeDtypeStruct((N_DEV,) + o.shape, o.dtype) for o in own]
        + [jax.ShapeDtypeStruct(p.shape, p.dtype) for p in parts],
        scratch_shapes=_exchange_sems(nown) if nown else [],
        compiler_params=pltpu.CompilerParams(dimension_semantics=("arbitrary", "arbitrary"), vmem_limit_bytes=VMEM_BIG),
    )(*args, *own, *parts)
    return (res[0], res[1:]) if nown else res[0]


def _rope(x, c, sa, sb):
    return x * c + pltpu.roll(x, LANES - 16, 1) * sa + pltpu.roll(x, 16, 1) * sb


def _rope_t(dy, c, sa, sb):
    return dy * c + pltpu.roll(dy * sa, 16, 1) + pltpu.roll(dy * sb, LANES - 16, 1)


def _mla_prep(h_c, gq, gkv, wq, wkn, wvx, c_t, sa_t, sb_t):
    tm = 256
    hw = MLA_HEADS * HEAD_PAD

    def body(cq_ref, ckv_ref, kpe_ref, gq_ref, gkv_ref, wq_ref, wkn_ref, wvx_ref, c_ref, sa_ref, sb_ref,
             q_ref, k_ref, kt_ref, vx_ref, vxt_ref):
        c, sa, sb = c_ref[...], sa_ref[...], sb_ref[...]
        cq = cq_ref[...]
        rq = lax.rsqrt(jnp.sum(cq * cq, axis=1, keepdims=True) * (1.0 / Q_LORA_RANK) + RMS_EPS)
        cqn = ((cq * rq) * gq_ref[...]).astype(BF16)
        qall = _dot(cqn, wq_ref[...], _NN)
        for h in range(MLA_HEADS):
            sl = slice(HEAD_PAD * h, HEAD_PAD * (h + 1))
            q_ref[:, sl] = (_rope(qall[:, sl], c, sa, sb) * ATTN_SCALE).astype(BF16)
        ckv = ckv_ref[...]
        rkv = lax.rsqrt(jnp.sum(ckv * ckv, axis=1, keepdims=True) * (1.0 / KV_LORA_RANK) + RMS_EPS)
        ckvn = ((ckv * rkv) * gkv_ref[...]).astype(BF16)
        knall = _dot(ckvn, wkn_ref[...], _NN)
        vall = _dot(ckvn, wvx_ref[...], _NN)
        kper = _rope(kpe_ref[...], c, sa, sb)
        ones_half = (lax.broadcasted_iota(jnp.int32, (tm, HEAD_PAD), 1) >= V_HEAD_DIM).astype(F32)
        for h in range(MLA_HEADS):
            sl = slice(HEAD_PAD * h, HEAD_PAD * (h + 1))
            kh = knall[:, sl] + kper
            vh = vall[:, sl] + ones_half
            k_ref[:, sl] = kh.astype(BF16)
            kt_ref[sl, :] = kh.T.astype(BF16)
            vx_ref[:, sl] = vh.astype(BF16)
            vxt_ref[sl, :] = vh.T.astype(BF16)

    full = lambda shape: pl.BlockSpec(shape, lambda i: (0, 0))
    tab = pl.BlockSpec((tm, LANES), lambda i: (i, 0))
    row = pl.BlockSpec((tm, hw), lambda i: (i, 0))
    col = pl.BlockSpec((hw, tm), lambda i: (0, i))
    return pl.pallas_call(
        body, name="mla_prep", grid=(SEQ // tm,),
        in_specs=[pl.BlockSpec((tm, CQ_PAD), lambda i: (i, 0)),
                  pl.BlockSpec((tm, LANES), lambda i: (i, CQ_PAD // LANES)),
                  pl.BlockSpec((tm, LANES), lambda i: (i, CQ_PAD // LANES + 1)),
                  full((1, CQ_PAD)), full((1, KV_LORA_RANK)),
                  full((CQ_PAD, hw)), full((KV_LORA_RANK, hw)), full((KV_LORA_RANK, hw)), tab, tab, tab],
        out_specs=[row, row, col, row, col],
        out_shape=[jax.ShapeDtypeStruct((SEQ, hw), BF16), jax.ShapeDtypeStruct((SEQ, hw), BF16),
                   jax.ShapeDtypeStruct((hw, SEQ), BF16), jax.ShapeDtypeStruct((SEQ, hw), BF16),
                   jax.ShapeDtypeStruct((hw, SEQ), BF16)],
        compiler_params=pltpu.CompilerParams(dimension_semantics=("arbitrary",), vmem_limit_bytes=VMEM_MID),
    )(h_c, h_c, h_c, gq, gkv, wq, wkn, wvx, c_t, sa_t, sb_t)


ATT_T = 512
ATT_STRIP = 64


def _attn_fwd(q, kt, vx, own):
    t, rs = ATT_T, ATT_STRIP
    nown = len(own)
    nq = SEQ // t
    nsteps = (MLA_HEADS // 2) * nq

    def body(q_ref, kt_ref, vx_ref, *rest):
        own_refs, (o_ref, l_ref), gat_refs = rest[:nown], rest[nown:nown + 2], rest[nown + 2:2 * nown + 2]
        s_scr, p_scr, m_scr, a_scr, acc_scr, send_sems, recv_sems, local_sems = rest[2 * nown + 2:]
        qi = pl.program_id(1)
        _gather_behind(own_refs, gat_refs, send_sems, recv_sems, local_sems, pl.program_id(0) * nq + qi,
                       nsteps - 2, nsteps - 1)
        lane = lax.broadcasted_iota(jnp.int32, (t, LANES), 1)
        m_scr[...] = jnp.full((2, t, LANES), -1e30, F32)
        acc_scr[...] = jnp.zeros((2, t, LANES), F32)

        def block(j, masked):
            off = pl.multiple_of(j * t, t)
            for a in range(2):
                sl = slice(HEAD_PAD * a, HEAD_PAD * (a + 1))
                s_scr[a] = _dot(q_ref[:, sl], kt_ref[sl, pl.ds(off, t)], _NN)
                for r in range(t // rs):
                    rows = slice(rs * r, rs * (r + 1))
                    s = s_scr[a, rows, :]
                    if masked:
                        rowi = lax.broadcasted_iota(jnp.int32, (rs, t), 0) + rs * r
                        coli = lax.broadcasted_iota(jnp.int32, (rs, t), 1)
                        s = jnp.where(coli <= rowi, s, -1e30)
                    m_old = m_scr[a, rows, :]
                    m_new = jnp.maximum(m_old, jnp.max(s, axis=1, keepdims=True))
                    p_scr[a, rows, :] = jnp.exp(s - m_new[:, :1]).astype(BF16)
                    a_scr[a, rows, :] = jnp.exp(m_old - m_new)
                    m_scr[a, rows, :] = m_new
                acc_scr[a] = acc_scr[a] * a_scr[a] + _dot(p_scr[a], vx_ref[pl.ds(off, t), sl], _NN)

        def step(j, carry):
            block(j, False)
            return carry
        lax.fori_loop(0, qi, step, 0)
        block(qi, True)
        res = []
        for a in range(2):
            acc = acc_scr[a]
            l = acc[:, V_HEAD_DIM:V_HEAD_DIM + 1]
            res.append((acc / l, m_scr[a] + jnp.log(l)))
        o_ref[...] = jnp.where(lane < V_HEAD_DIM, res[0][0], pltpu.roll(res[1][0], V_HEAD_DIM, 1))
        l_ref[...] = jnp.where(lane < V_HEAD_DIM, res[0][1], res[1][1])

    hbm = pl.BlockSpec(memory_space=pl.ANY)
    res = pl.pallas_call(
        body, name="attn_fwd", grid=(MLA_HEADS // 2, nq),
        in_specs=[pl.BlockSpec((t, 2 * HEAD_PAD), lambda p, i: (i, p)),
                  pl.BlockSpec((2 * HEAD_PAD, SEQ), lambda p, i: (p, 0)),
                  pl.BlockSpec((SEQ, 2 * HEAD_PAD), lambda p, i: (0, p))] + [hbm] * nown,
        out_specs=[pl.BlockSpec((t, LANES), lambda p, i: (i, p)),
                   pl.BlockSpec((t, LANES), lambda p, i: (i, p))] + [hbm] * nown,
        out_shape=[jax.ShapeDtypeStruct((SEQ, MLA_WIDTH), F32), jax.ShapeDtypeStruct((SEQ, MLA_WIDTH), F32)]
        + [jax.ShapeDtypeStruct((N_DEV,) + a.shape, a.dtype) for a in own],
        scratch_shapes=[pltpu.VMEM((2, t, t), F32), pltpu.VMEM((2, t, t), BF16), pltpu.VMEM((2, t, LANES), F32),
                        pltpu.VMEM((2, t, LANES), F32), pltpu.VMEM((2, t, LANES), F32)] + _exchange_sems(nown),
        compiler_params=pltpu.CompilerParams(dimension_semantics=("arbitrary", "arbitrary"), vmem_limit_bytes=VMEM_MID),
    )(q, kt, vx, *own)
    return res[0], res[1], res[2:]


def _exchange_parts(parts, lands, send_sems, recv_sems, local_sems):
    x, y, c = _mesh_pos()
    me = 4 * x + 2 * y + c
    peers = [(x, y, 1 - c), (1 - x, y, c), (x, 1 - y, c), (1 - x, 1 - y, c),
             (1 - x, y, 1 - c), (x, 1 - y, 1 - c), (1 - x, 1 - y, 1 - c)]
    remote, local = [], []
    for a, (part, land) in enumerate(zip(parts, lands)):
        for k, peer in enumerate(peers):
            t = 4 * peer[0] + 2 * peer[1] + peer[2]
            remote.append(_remote(part.at[t], land.at[me], send_sems, recv_sems, 7 * a + k, peer))
        local.append(pltpu.make_async_copy(part.at[me], land.at[me], local_sems.at[a]))
    return remote, local


def _exchange_start(first_step, exchange):
    remote, local = exchange

    @pl.when(first_step)
    def _():
        for cp in remote + local:
            cp.start()


def _exchange_finish(last_step, exchange):
    remote, local = exchange

    @pl.when(last_step)
    def _():
        for cp in remote:
            cp.wait_recv()
        for cp in remote:
            cp.wait_send()
        for cp in local:
            cp.wait()


def _exchange_sems(npart):
    return [pltpu.SemaphoreType.DMA((7 * npart,)), pltpu.SemaphoreType.DMA((7 * npart,)),
            pltpu.SemaphoreType.DMA((npart,))]


def _attn_bwd(q, kt, k, vxt, d_o, o, lse, parts):
    t, rs = ATT_T, ATT_STRIP
    nq = SEQ // t
    npart = len(parts)
    nsteps = MLA_HEADS // 2

    def body(q_ref, kt_ref, k_ref, vxt_ref, do_ref, o_ref, l_ref, *rest):
        part_refs, rest = rest[:npart], rest[npart:]
        dq_ref, dk_ref, dv_ref = rest[:3]
        land_refs, rest = rest[3:3 + npart], rest[3 + npart:]
        s_scr, dp_scr, p_scr, ds_scr, st_scr, send_sems, recv_sems, local_sems = rest
        exchange = _exchange_parts(part_refs, land_refs, send_sems, recv_sems, local_sems)
        _exchange_start(pl.program_id(0) == 0, exchange)
        dk_ref[...] = jnp.zeros_like(dk_ref)
        dv_ref[...] = jnp.zeros_like(dv_ref)
        lane = lax.broadcasted_iota(jnp.int32, (t, LANES), 1)

        def qtile(i, carry):
            ioff = pl.multiple_of(i * t, t)
            do_i = do_ref[pl.ds(ioff, t), :]
            o_i = o_ref[pl.ds(ioff, t), :]
            l_i = l_ref[pl.ds(ioff, t), :]
            for a in range(2):
                sl = slice(HEAD_PAD * a, HEAD_PAD * (a + 1))
                sel = (lane < V_HEAD_DIM) if a == 0 else (lane >= V_HEAD_DIM)
                doa = jnp.where(sel, do_i, 0.0)
                oa = o_i
                if a == 1:
                    doa = pltpu.roll(doa, V_HEAD_DIM, 1)
                    oa = pltpu.roll(o_i, V_HEAD_DIM, 1)
                st_scr[0] = jnp.broadcast_to(jnp.sum(doa * oa, axis=1, keepdims=True), (t, LANES))
                st_scr[1] = jnp.broadcast_to(l_i[:, V_HEAD_DIM * a:V_HEAD_DIM * a + 1], (t, LANES))
                doa_bf = doa.astype(BF16)
                qa = q_ref[pl.ds(ioff, t), sl]

                def block(j, masked, dq_acc, sl=sl, qa=qa, doa_bf=doa_bf):
                    joff = pl.multiple_of(j * t, t)
                    s_scr[...] = _dot(qa, kt_ref[sl, pl.ds(joff, t)], _NN)
                    dp_scr[...] = _dot(doa_bf, vxt_ref[sl, pl.ds(joff, t)], _NN)
                    for r in range(t // rs):
                        rows = slice(rs * r, rs * (r + 1))
                        p = jnp.exp(s_scr[rows, :] - st_scr[1, rows, :1])
                        if masked:
                            rowi = lax.broadcasted_iota(jnp.int32, (rs, t), 0) + rs * r
                            coli = lax.broadcasted_iota(jnp.int32, (rs, t), 1)
                            p = jnp.where(coli <= rowi, p, 0.0)
                        p_scr[rows, :] = p.astype(BF16)
                        ds_scr[rows, :] = (p * (dp_scr[rows, :] - st_scr[0, rows, :1])).astype(BF16)
                    dk_ref[pl.ds(joff, t), sl] += _dot(ds_scr[...], qa, _TN)
                    dv_ref[pl.ds(joff, t), sl] += _dot(p_scr[...], doa_bf, _TN)
                    return dq_acc + _dot(ds_scr[...], k_ref[pl.ds(joff, t), sl], _NN)

                dq_acc = lax.fori_loop(0, i, lambda j, acc: block(j, False, acc), jnp.zeros((t, HEAD_PAD), F32))
                dq_ref[pl.ds(ioff, t), sl] = block(i, True, dq_acc)
            return carry

        lax.fori_loop(0, nq, qtile, 0)
        _exchange_finish(pl.program_id(0) == nsteps - 1, exchange)

    hw = MLA_HEADS * HEAD_PAD
    wide = pl.BlockSpec((SEQ, 2 * HEAD_PAD), lambda p: (0, p))
    wide_t = pl.BlockSpec((2 * HEAD_PAD, SEQ), lambda p: (p, 0))
    narrow = pl.BlockSpec((SEQ, LANES), lambda p: (0, p))
    hbm = pl.BlockSpec(memory_space=pl.ANY)
    res = pl.pallas_call(
        body, name="attn_bwd", grid=(nsteps,),
        in_specs=[wide, wide_t, wide, wide_t, narrow, narrow, narrow] + [hbm] * npart,
        out_specs=[wide, wide, wide] + [hbm] * npart,
        out_shape=[jax.ShapeDtypeStruct((SEQ, hw), F32)] * 3 + [jax.ShapeDtypeStruct(p.shape, p.dtype) for p in parts],
        scratch_shapes=[pltpu.VMEM((t, t), F32), pltpu.VMEM((t, t), F32), pltpu.VMEM((t, t), BF16),
                        pltpu.VMEM((t, t), BF16), pltpu.VMEM((2, t, LANES), F32)] + _exchange_sems(npart),
        compiler_params=pltpu.CompilerParams(dimension_semantics=("arbitrary",), vmem_limit_bytes=VMEM_BIG),
    )(q, kt, k, vxt, d_o, o, lse, *parts)
    return res[0], res[1], res[2], res[3:]


def _sgu_math(u, v, zb, lg, lb, ws_ref, bias):
    ug, dug = _gelu_and_grad(u)
    vg, dvg = _gelu_and_grad(v)
    mu = jnp.mean(vg, axis=1, keepdims=True)
    xc = vg - mu
    rstd = lax.rsqrt(jnp.mean(xc * xc, axis=1, keepdims=True) + LN_EPS)
    xh = xc * rstd
    vn_bf = (xh * lg + lb).astype(BF16)
    grp = lax.broadcasted_iota(jnp.int32, (CHUNK, SGU_WIDTH), 1) // SGU_GROUP_DIM
    r_i = lax.broadcasted_iota(jnp.int32, (CHUNK, CHUNK), 0)
    c_i = lax.broadcasted_iota(jnp.int32, (CHUNK, CHUNK), 1)
    tri, tri_t = r_i >= c_i, r_i <= c_i
    mixed = bias
    for g in range(SGU_GROUPS):
        wt = jnp.where(tri, ws_ref[g], 0.0).astype(BF16)
        mixed = mixed + jnp.where(grp == g, _dot(wt, vn_bf, _NN), 0.0)
    sb = _sigmoid(zb)
    return ug, dug, dvg, rstd, xh, vn_bf, grp, tri, tri_t, mixed, sb


def _sgu_fwd(h_b, lg, lb, w_s, bias_full):
    def body(u_ref, v_ref, zb_ref, lg_ref, lb_ref, ws_ref, bias_ref, yb_ref):
        zb = zb_ref[...]
        ug, _, _, _, _, _, _, _, _, mixed, sb = _sgu_math(u_ref[...], v_ref[...], zb, lg_ref[...], lb_ref[...],
                                                       ws_ref, bias_ref[...])
        yb_ref[...] = (ug * mixed) * (zb * sb)

    blk = lambda c: pl.BlockSpec((CHUNK, SGU_WIDTH), lambda i, c=c: (i, c))
    full2 = lambda shape: pl.BlockSpec(shape, lambda i: (0, 0))
    return pl.pallas_call(
        body, name="sgu_fwd", grid=(SEQ // CHUNK,),
        in_specs=[blk(0), blk(1), blk(2), full2((1, SGU_WIDTH)), full2((1, SGU_WIDTH)),
                  pl.BlockSpec((SGU_GROUPS, CHUNK, CHUNK), lambda i: (0, 0, 0)), full2((CHUNK, SGU_WIDTH))],
        out_specs=pl.BlockSpec((CHUNK, SGU_WIDTH), lambda i: (i, 0)),
        out_shape=jax.ShapeDtypeStruct((SEQ, SGU_WIDTH), F32),
        compiler_params=pltpu.CompilerParams(dimension_semantics=("arbitrary",)),
    )(h_b, h_b, h_b, lg, lb, w_s, bias_full)


def _sgu_bwd(h_b, d_yb, lg, lb, w_s, w_st, bias_full, parts):
    nsteps = SEQ // CHUNK
    npart = len(parts)

    def body(u_ref, v_ref, zb_ref, dyb_ref, lg_ref, lb_ref, ws_ref, wst_ref, bias_ref, *rest):
        part_refs, rest = rest[:npart], rest[npart:]
        dhb_ref, dws_ref, dbs_ref, dlg_ref, dlb_ref, dbb_ref = rest[:6]
        land_refs, (dbias_acc, send_sems, recv_sems, local_sems) = rest[6:6 + npart], rest[6 + npart:]
        step = pl.program_id(0)
        exchange = _exchange_parts(part_refs, land_refs, send_sems, recv_sems, local_sems)
        _exchange_start(step == 0, exchange)

        @pl.when(step == 0)
        def _():
            dbb_ref[...] = jnp.zeros_like(dbb_ref)
            dws_ref[...] = jnp.zeros_like(dws_ref)
            dlg_ref[...] = jnp.zeros_like(dlg_ref)
            dlb_ref[...] = jnp.zeros_like(dlb_ref)
            dbias_acc[...] = jnp.zeros_like(dbias_acc)

        zb = zb_ref[...]
        lg = lg_ref[...]
        ug, dug, dvg, rstd, xh, vn_bf, grp, tri, tri_t, mixed, sb = _sgu_math(
            u_ref[...], v_ref[...], zb, lg, lb_ref[...], ws_ref, bias_ref[...])
        dyb = dyb_ref[...]
        dsgu = dyb * (zb * sb)
        dzb = dyb * (ug * mixed) * (sb * (1.0 + zb * (1.0 - sb)))
        du = dsgu * mixed * dug
        dmixed = dsgu * ug
        dbias_acc[...] += dmixed
        dvn = jnp.zeros((CHUNK, SGU_WIDTH), F32)
        for g in range(SGU_GROUPS):
            dm_g = jnp.where(grp == g, dmixed, 0.0).astype(BF16)
            wtt = jnp.where(tri_t, wst_ref[g], 0.0).astype(BF16)
            dvn = dvn + _dot(wtt, dm_g, _NN)
            dws_ref[g] += jnp.where(tri, _dot(dm_g, vn_bf, _NT), 0.0)
        dlg_ref[...] += jnp.sum(dvn * xh, axis=0, keepdims=True)
        dlb_ref[...] += jnp.sum(dvn, axis=0, keepdims=True)
        dxh = dvn * lg
        dvgel = rstd * (dxh - jnp.mean(dxh, axis=1, keepdims=True) - xh * jnp.mean(dxh * xh, axis=1, keepdims=True))
        _store_grad(dhb_ref, dbb_ref, 0, du)
        _store_grad(dhb_ref, dbb_ref, SGU_WIDTH, dvgel * dvg)
        _store_grad(dhb_ref, dbb_ref, 2 * SGU_WIDTH, dzb)

        @pl.when(step == nsteps - 1)
        def _():
            acc = dbias_acc[...]
            lane = lax.broadcasted_iota(jnp.int32, (CHUNK, LANES), 1)
            out = jnp.zeros((CHUNK, LANES), F32)
            for g in range(SGU_GROUPS):
                sg = jnp.sum(jnp.where(grp == g, acc, 0.0), axis=1, keepdims=True)
                out = jnp.where(lane == g, sg, out)
            dbs_ref[...] = out

        _exchange_finish(step == nsteps - 1, exchange)

    blk = lambda c: pl.BlockSpec((CHUNK, SGU_WIDTH), lambda i, c=c: (i, c))
    full2 = lambda shape: pl.BlockSpec(shape, lambda i: (0, 0))
    full3 = pl.BlockSpec((SGU_GROUPS, CHUNK, CHUNK), lambda i: (0, 0, 0))
    hbm = pl.BlockSpec(memory_space=pl.ANY)
    res = pl.pallas_call(
        body, name="sgu_bwd", grid=(nsteps,),
        in_specs=[blk(0), blk(1), blk(2), pl.BlockSpec((CHUNK, SGU_WIDTH), lambda i: (i, 0)),
                  full2((1, SGU_WIDTH)), full2((1, SGU_WIDTH)), full3, full3, full2((CHUNK, SGU_WIDTH))] + [hbm] * npart,
        out_specs=[pl.BlockSpec((CHUNK, SEG_B), lambda i: (i, 0)), full3, full2((CHUNK, LANES)),
                   full2((1, SGU_WIDTH)), full2((1, SGU_WIDTH)), full2((1, SEG_B))] + [hbm] * npart,
        out_shape=[jax.ShapeDtypeStruct((SEQ, SEG_B), BF16),
                   jax.ShapeDtypeStruct((SGU_GROUPS, CHUNK, CHUNK), F32),
                   jax.ShapeDtypeStruct((CHUNK, LANES), F32),
                   jax.ShapeDtypeStruct((1, SGU_WIDTH), F32), jax.ShapeDtypeStruct((1, SGU_WIDTH), F32),
                   jax.ShapeDtypeStruct((1, SEG_B), F32)] + [jax.ShapeDtypeStruct(p.shape, p.dtype) for p in parts],
        scratch_shapes=[pltpu.VMEM((CHUNK, SGU_WIDTH), F32)] + _exchange_sems(npart),
        compiler_params=pltpu.CompilerParams(dimension_semantics=("arbitrary",)),
    )(h_b, h_b, h_b, d_yb, lg, lb, w_s, w_st, bias_full, *parts)
    return res[:6], res[6:]


def _merge(x, o, h_a, y_b, target, w_oa, w_ob, w_out, ln_g, ln_b):
    tm = 256
    nsteps = SEQ // tm

    def body(x_ref, o_ref, ga_ref, gb_ref, za_ref, yb_ref, tgt_ref, woa_ref, wob_ref, wout_ref, lng_ref, lnb_ref,
             loss_ref, dxr_ref, dha_ref, do_ref, dyb_ref, poa_ref, pob_ref, pout_ref, dlng_ref, dlnb_ref, dba_ref,
             dwoa_ref, dwob_ref, dwout_ref):
        step = pl.program_id(0)

        @pl.when(step == 0)
        def _():
            for r in (loss_ref, dwoa_ref, dwob_ref, dwout_ref, dlng_ref, dlnb_ref, dba_ref):
                r[...] = jnp.zeros_like(r)

        o = o_ref[...]
        za = za_ref[...]
        sa = _sigmoid(za)
        ya_bf = (o * (za * sa)).astype(BF16)
        yb_bf = yb_ref[...].astype(BF16)
        woa, wob, wout = woa_ref[...], wob_ref[...], wout_ref[...]
        pa = _dot(ya_bf, woa, _NN)
        pb = _dot(yb_bf, wob, _NN)
        sga = _sigmoid(ga_ref[...])
        sgb = _sigmoid(gb_ref[...])
        merged_bf = (sga * pa + sgb * pb).astype(BF16)
        r = DN_ALPHA * x_ref[...] + _dot(merged_bf, wout, _NN)
        mu = jnp.mean(r, axis=1, keepdims=True)
        rc = r - mu
        rstd = lax.rsqrt(jnp.mean(rc * rc, axis=1, keepdims=True) + LN_EPS)
        xh = rc * rstd
        lng = lng_ref[...]
        y = xh * lng + lnb_ref[...]
        e = y - tgt_ref[...]
        loss_ref[...] += 0.5 * jnp.sum(jnp.sum(e * e, axis=1, keepdims=True) * (1.0 / D_MODEL), axis=0, keepdims=True)

        dy = e * (1.0 / D_MODEL)
        dlng_ref[...] += jnp.sum(dy * xh, axis=0, keepdims=True)
        dlnb_ref[...] += jnp.sum(dy, axis=0, keepdims=True)
        dxh = dy * lng
        dr = rstd * (dxh - jnp.mean(dxh, axis=1, keepdims=True) - xh * jnp.mean(dxh * xh, axis=1, keepdims=True))
        dxr_ref[...] = DN_ALPHA * dr
        dr_bf = dr.astype(BF16)
        dwout_ref[...] += _dot(merged_bf, dr_bf, _TN)
        dmerged = _dot(dr_bf, wout, _NT)
        dpa_bf = (dmerged * sga).astype(BF16)
        dpb_bf = (dmerged * sgb).astype(BF16)
        _store_grad(dha_ref, dba_ref, 0, dmerged * pa * (sga * (1.0 - sga)))
        _store_grad(dha_ref, dba_ref, D_MODEL, dmerged * pb * (sgb * (1.0 - sgb)))
        dwoa_ref[...] += _dot(ya_bf, dpa_bf, _TN)
        dwob_ref[...] += _dot(yb_bf, dpb_bf, _TN)
        dya = _dot(dpa_bf, woa, _NT)
        dyb_ref[...] = _dot(dpb_bf, wob, _NT)
        do_ref[...] = dya * (za * sa)
        _store_grad(dha_ref, dba_ref, 2 * D_MODEL, dya * o * (sa * (1.0 + za * (1.0 - sa))))

        @pl.when(step == nsteps - 1)
        def _():
            cols = D_MODEL // N_DEV
            for j in range(N_DEV):
                poa_ref[j] = dwoa_ref[:, cols * j:cols * (j + 1)].astype(BF16)
                pob_ref[j] = dwob_ref[:, cols * j:cols * (j + 1)].astype(BF16)
                pout_ref[j] = dwout_ref[cols * j:cols * (j + 1), :].astype(BF16)

    row = lambda w, c=0: pl.BlockSpec((tm, w), lambda i, c=c: (i, c))
    full = lambda shape: pl.BlockSpec(shape, lambda i: (0, 0))
    full3 = lambda shape: pl.BlockSpec(shape, lambda i: (0, 0, 0))
    return pl.pallas_call(
        body, name="merge", grid=(nsteps,),
        in_specs=[row(D_MODEL), row(MLA_WIDTH), row(D_MODEL, 0), row(D_MODEL, 1), row(MLA_WIDTH, 4), row(SGU_WIDTH),
                  row(D_MODEL), full((MLA_WIDTH, D_MODEL)), full((SGU_WIDTH, D_MODEL)), full((D_MODEL, D_MODEL)),
                  full((1, D_MODEL)), full((1, D_MODEL))],
        out_specs=[full((1, LANES)), row(D_MODEL), row(SEG_A), row(MLA_WIDTH), row(SGU_WIDTH),
                   full3((N_DEV, MLA_WIDTH, D_MODEL // N_DEV)), full3((N_DEV, SGU_WIDTH, D_MODEL // N_DEV)),
                   full3((N_DEV, D_MODEL // N_DEV, D_MODEL)), full((1, D_MODEL)), full((1, D_MODEL)), full((1, SEG_A))],
        out_shape=[jax.ShapeDtypeStruct((1, LANES), F32),
                   jax.ShapeDtypeStruct((SEQ, D_MODEL), F32), jax.ShapeDtypeStruct((SEQ, SEG_A), BF16),
                   jax.ShapeDtypeStruct((SEQ, MLA_WIDTH), F32), jax.ShapeDtypeStruct((SEQ, SGU_WIDTH), F32),
                   jax.ShapeDtypeStruct((N_DEV, MLA_WIDTH, D_MODEL // N_DEV), BF16),
                   jax.ShapeDtypeStruct((N_DEV, SGU_WIDTH, D_MODEL // N_DEV), BF16),
                   jax.ShapeDtypeStruct((N_DEV, D_MODEL // N_DEV, D_MODEL), BF16),
                   jax.ShapeDtypeStruct((1, D_MODEL), F32), jax.ShapeDtypeStruct((1, D_MODEL), F32),
                   jax.ShapeDtypeStruct((1, SEG_A), F32)],
        scratch_shapes=[pltpu.VMEM((MLA_WIDTH, D_MODEL), F32), pltpu.VMEM((SGU_WIDTH, D_MODEL), F32),
                        pltpu.VMEM((D_MODEL, D_MODEL), F32)],
        compiler_params=pltpu.CompilerParams(dimension_semantics=("arbitrary",), vmem_limit_bytes=VMEM_BIG),
    )(x, o, h_a, h_a, h_a, y_b, target, w_oa, w_ob, w_out, ln_g, ln_b)


def _mla_bwd(dq, dk, dv, h_c, gq, gkv, wq, wkn, wv, c_t, sa_t, sb_t):
    tm = 256
    hw = MLA_HEADS * HEAD_PAD

    def body(dq_ref, dk_ref, dv_ref, cq_ref, ckv_ref, gq_ref, gkv_ref, wq_ref, wkn_ref, wv_ref, c_ref, sa_ref, sb_ref,
             dhc_ref, puq_ref, dwkn_ref, dwv_ref, dgq_ref, dgkv_ref, dbc_ref, pre_ref, dwq_ref):
        @pl.when(pl.program_id(0) == 0)
        def _():
            for r in (dwq_ref, dwkn_ref, dwv_ref, dgq_ref, dgkv_ref, dbc_ref):
                r[...] = jnp.zeros_like(r)

        c, sa, sb = c_ref[...], sa_ref[...], sb_ref[...]
        lane = lax.broadcasted_iota(jnp.int32, (tm, LANES), 1)
        rope_lanes = jnp.logical_and(lane >= ROPE_LO, lane < ROPE_HI)

        cq = cq_ref[...]
        gq = gq_ref[...]
        rq = lax.rsqrt(jnp.sum(cq * cq, axis=1, keepdims=True) * (1.0 / Q_LORA_RANK) + RMS_EPS)
        nq = cq * rq
        cqn_bf = (nq * gq).astype(BF16)
        for h in range(MLA_HEADS):
            sl = slice(HEAD_PAD * h, HEAD_PAD * (h + 1))
            pre_ref[:, sl] = _rope_t(dq_ref[:, sl] * ATTN_SCALE, c, sa, sb).astype(BF16)
        dqpre_bf = pre_ref[...]
        dcqn = _dot(dqpre_bf, wq_ref[...], _NT)
        dwq_ref[...] += _dot(cqn_bf, dqpre_bf, _TN)
        dgq_ref[...] += jnp.sum(dcqn * nq, axis=0, keepdims=True)
        dnq = dcqn * gq
        _store_grad(dhc_ref, dbc_ref, 0,
                    rq * (dnq - nq * (jnp.sum(dnq * nq, axis=1, keepdims=True) * (1.0 / Q_LORA_RANK))))

        ckv = ckv_ref[...]
        gkv = gkv_ref[...]
        rkv = lax.rsqrt(jnp.sum(ckv * ckv, axis=1, keepdims=True) * (1.0 / KV_LORA_RANK) + RMS_EPS)
        nkv = ckv * rkv
        ckvn_bf = (nkv * gkv).astype(BF16)
        dk = dk_ref[...]
        dk_bf = dk.astype(BF16)
        dv_bf = dv_ref[...].astype(BF16)
        dckvn = _dot(dk_bf, wkn_ref[...], _NT) + _dot(dv_bf, wv_ref[...], _NT)
        dwkn_ref[...] += _dot(ckvn_bf, dk_bf, _TN)
        dwv_ref[...] += _dot(ckvn_bf, dv_bf, _TN)
        dgkv_ref[...] += jnp.sum(dckvn * nkv, axis=0, keepdims=True)
        dnkv = dckvn * gkv
        _store_grad(dhc_ref, dbc_ref, CQ_PAD, rkv * (
            dnkv - nkv * (jnp.sum(dnkv * nkv, axis=1, keepdims=True) * (1.0 / KV_LORA_RANK))))
        dkpe = jnp.zeros((tm, LANES), F32)
        for h in range(MLA_HEADS):
            dkpe = dkpe + dk[:, HEAD_PAD * h:HEAD_PAD * (h + 1)]
        _store_grad(dhc_ref, dbc_ref, CQ_PAD + LANES, _rope_t(jnp.where(rope_lanes, dkpe, 0.0), c, sa, sb))

        @pl.when(pl.program_id(0) == SEQ // tm - 1)
        def _():
            rows = Q_LORA_RANK // N_DEV
            for j in range(N_DEV):
                for h in range(MLA_HEADS):
                    puq_ref[j, :, QK_HEAD_DIM * h:QK_HEAD_DIM * (h + 1)] = dwq_ref[
                        rows * j:rows * (j + 1), HEAD_PAD * h:HEAD_PAD * h + QK_HEAD_DIM].astype(BF16)

    full = lambda shape: pl.BlockSpec(shape, lambda i: (0, 0))
    row = lambda w, c=0: pl.BlockSpec((tm, w), lambda i, c=c: (i, c))
    return pl.pallas_call(
        body, name="mla_bwd", grid=(SEQ // tm,),
        in_specs=[row(hw), row(hw), row(hw), row(CQ_PAD, 0), row(LANES, CQ_PAD // LANES),
                  full((1, CQ_PAD)), full((1, KV_LORA_RANK)), full((CQ_PAD, hw)), full((KV_LORA_RANK, hw)),
                  full((KV_LORA_RANK, hw)), row(LANES), row(LANES), row(LANES)],
        out_specs=[row(SEG_C), pl.BlockSpec((N_DEV, Q_LORA_RANK // N_DEV, MLA_HEADS * QK_HEAD_DIM), lambda i: (0, 0, 0)),
                   full((KV_LORA_RANK, hw)), full((KV_LORA_RANK, hw)),
                   full((1, CQ_PAD)), full((1, KV_LORA_RANK)), full((1, SEG_C))],
        out_shape=[jax.ShapeDtypeStruct((SEQ, SEG_C), BF16),
                   jax.ShapeDtypeStruct((N_DEV, Q_LORA_RANK // N_DEV, MLA_HEADS * QK_HEAD_DIM), BF16),
                   jax.ShapeDtypeStruct((KV_LORA_RANK, hw), F32), jax.ShapeDtypeStruct((KV_LORA_RANK, hw), F32),
                   jax.ShapeDtypeStruct((1, CQ_PAD), F32), jax.ShapeDtypeStruct((1, KV_LORA_RANK), F32),
                   jax.ShapeDtypeStruct((1, SEG_C), F32)],
        scratch_shapes=[pltpu.VMEM((tm, hw), BF16), pltpu.VMEM((CQ_PAD, hw), F32)],
        compiler_params=pltpu.CompilerParams(dimension_semantics=("arbitrary",), vmem_limit_bytes=VMEM_MID),
    )(dq, dk, dv, h_c, h_c, gq, gkv, wq, wkn, wv, c_t, sa_t, sb_t)


def _adamw_all(ws, gs, ms, vs):
    n = len(ws)
    c1 = 1.0 / (1.0 - ADAM_B1 ** ADAM_STEP)
    c2 = 1.0 / (1.0 - ADAM_B2 ** ADAM_STEP)

    def body(*refs):
        for idx in range(n):
            w, g, m, v = (refs[idx][...], refs[n + idx][...], refs[2 * n + idx][...], refs[3 * n + idx][...])
            m_new = ADAM_B1 * m + (1.0 - ADAM_B1) * g
            v_new = ADAM_B2 * v + (1.0 - ADAM_B2) * (g * g)
            delta = -ADAM_LR * ((m_new * c1) / (jnp.sqrt(v_new * c2) + ADAM_EPS) + ADAM_WD * w)
            refs[4 * n + idx][...] = delta
            refs[5 * n + idx][...] = m_new
            refs[6 * n + idx][...] = v_new

    shapes = [jax.ShapeDtypeStruct(w.shape, F32) for w in ws]
    outs = pl.pallas_call(
        body, name="adamw", out_shape=shapes * 3,
        compiler_params=pltpu.CompilerParams(vmem_limit_bytes=VMEM_BIG),
    )(*ws, *gs, *ms, *vs)
    return outs[:n], outs[n:2 * n], outs[2 * n:]


SHARD_W = IN_WIDTH // N_DEV

_PIECES = [(0, 384, 2, 0), (384, 512, 2, CQ_PAD), (512, 544, 2, CQ_PAD + LANES + ROPE_LO),
           (544, 1056, 0, 2 * D_MODEL), (1056, 1568, 1, 0), (1568, 2080, 1, SGU_WIDTH),
           (2080, 2592, 1, 2 * SGU_WIDTH), (2592, 3616, 0, 0), (3616, 4640, 0, D_MODEL)]


def _column_runs():
    runs = []
    for n0, n1, seg, d0 in _PIECES:
        for j in range(N_DEV):
            lo, hi = max(n0, j * SHARD_W), min(n1, (j + 1) * SHARD_W)
            if lo < hi:
                runs.append((j, lo - j * SHARD_W, hi - j * SHARD_W, seg, d0 + lo - n0))
    return runs


def _mesh_pos():
    return lax.axis_index("x"), lax.axis_index("y"), lax.axis_index("c")


def _remote(src, dst, send_sems, recv_sems, k, to):
    return pltpu.make_async_remote_copy(src_ref=src, dst_ref=dst, send_sem=send_sems.at[k], recv_sem=recv_sems.at[k],
                                        device_id=to, device_id_type=pl.DeviceIdType.MESH)


def _gather_exchange(gats, send_sems, recv_sems, meanwhile=None):
    x, y, c = _mesh_pos()
    me, sibling = (x, y, c), (x, y, 1 - c)
    chips = [(1 - x, y), (x, 1 - y), (1 - x, 1 - y)]

    def copy(a, k, blk, to):
        slab = gats[a].at[4 * blk[0] + 2 * blk[1] + blk[2]]
        return _remote(slab, slab, send_sems, recv_sems, 7 * a + k, to)

    arrays = range(len(gats))
    first = [copy(a, 1 + j, me, (*chip, c)) for j, chip in enumerate(chips) for a in arrays]
    first += [copy(a, 0, me, sibling) for a in arrays]
    for cp in first:
        cp.start()
    if meanwhile is not None:
        meanwhile()
    passed = []
    for j, chip in enumerate(chips):
        for a in arrays:
            copy(a, 1 + j, (*chip, c), me).wait_recv()
            fwd = copy(a, 4 + j, (*chip, c), sibling)
            fwd.start()
            passed.append(fwd)
    for a in arrays:
        copy(a, 0, sibling, me).wait_recv()
    for j, chip in enumerate(chips):
        for a in arrays:
            copy(a, 4 + j, (*chip, 1 - c), me).wait_recv()
    for cp in first + passed:
        cp.wait_send()


def _gather_behind(own, gats, send_sems, recv_sems, local_sems, step, mid, last):
    x, y, c = _mesh_pos()
    me, sibling = (x, y, c), (x, y, 1 - c)
    chips = [(1 - x, y), (x, 1 - y), (1 - x, 1 - y)]
    arrays = range(len(gats))

    def copy(a, k, blk, to, src=None):
        slab = gats[a].at[4 * blk[0] + 2 * blk[1] + blk[2]]
        return _remote(slab if src is None else src, slab, send_sems, recv_sems, 7 * a + k, to)

    first = [copy(a, 1 + j, me, (*chip, c), src=own[a]) for j, chip in enumerate(chips) for a in arrays]
    first += [copy(a, 0, me, sibling, src=own[a]) for a in arrays]
    local = [pltpu.make_async_copy(own[a], gats[a].at[4 * x + 2 * y + c], local_sems.at[a]) for a in arrays]
    passed = [copy(a, 4 + j, (*chip, c), sibling) for j, chip in enumerate(chips) for a in arrays]

    @pl.when(step == 0)
    def _():
        for cp in first + local:
            cp.start()

    @pl.when(step == mid)
    def _():
        for j, chip in enumerate(chips):
            for a in arrays:
                copy(a, 1 + j, (*chip, c), me).wait_recv()
        for cp in passed:
            cp.start()

    @pl.when(step == last)
    def _():
        for a in arrays:
            copy(a, 0, sibling, me).wait_recv()
        for j, chip in enumerate(chips):
            for a in arrays:
                copy(a, 4 + j, (*chip, 1 - c), me).wait_recv()
        for cp in first + passed:
            cp.wait_send()
        for cp in local:
            cp.wait()


def _gather_first(w_in, w_uq2, w_oa, w_ob, w_out, x2, pos_col, invf_lane):
    hw = MLA_HEADS * HEAD_PAD
    uq_rows = Q_LORA_RANK // N_DEV
    rows = 256

    def body(win_ref, wuq_ref, woa_ref, wob_ref, wout_ref, x_ref, pos_ref, invf_ref,
             wc_ref, wq_ref, winb_ref, oab_ref, obb_ref, outb_ref, xb_ref, xt_ref, c_ref, sa_ref, sb_ref,
             g_uq, blk0, send_sems, recv_sems):
        def local_work():
            for i in range(SEQ // rows):
                xi = x_ref[rows * i:rows * (i + 1), :]
                xb_ref[rows * i:rows * (i + 1), :] = xi.astype(BF16)
                xt_ref[:, rows * i:rows * (i + 1)] = xi.T.astype(BF16)
            ang = pos_ref[...].astype(F32) * invf_ref[...]
            cs, sn = jnp.cos(ang), jnp.sin(ang)
            lane = lax.broadcasted_iota(jnp.int32, ang.shape, 1)
            c_ref[...] = jnp.where(lane < ROPE_LO, 1.0, jnp.where(lane < ROPE_HI, cs, 0.0))
            sa_ref[...] = jnp.where(jnp.logical_and(lane >= ROPE_LO, lane < ROPE_MID), -sn, 0.0)
            sb_ref[...] = jnp.where(jnp.logical_and(lane >= ROPE_MID, lane < ROPE_HI), sn, 0.0)

        x, y, c = _mesh_pos()
        me = (x, y, c)
        winb_ref[...] = win_ref[0].astype(BF16)
        oab_ref[...] = woa_ref[0].astype(BF16)
        obb_ref[...] = wob_ref[0].astype(BF16)
        outb_ref[...] = wout_ref[0].astype(BF16)
        g_uq[4 * x + 2 * y + c] = wuq_ref[...].astype(BF16)

        chip0 = jnp.logical_and(x == 0, y == 0)
        south = c == 0
        half = D_MODEL // 2
        halves = [blk0.at[pl.ds(0, half)], blk0.at[pl.ds(half, half)]]

        def bcopy(k, to, part=None):
            ref = blk0 if part is None else halves[part]
            return _remote(ref, ref, send_sems, recv_sems, 7 + k, to)

        sends0 = [(0, (0, 0, 1), None), (1, (1, 0, 0), 0), (2, (0, 1, 0), 1), (3, (1, 0, 0), 1), (4, (0, 1, 0), 0)]

        @pl.when(jnp.logical_and(chip0, south))
        def _():
            blk0[...] = winb_ref[...]
            for k, to, part in sends0:
                bcopy(k, to, part).start()

        _gather_exchange([g_uq], send_sems, recv_sems, meanwhile=local_work)

        for (cx, cy), first_k, first_half, second_k in (((1, 0), 1, 0, 3), ((0, 1), 2, 1, 4)):
            @pl.when(jnp.logical_and(jnp.logical_and(x == cx, y == cy), south))
            def _(cx=cx, cy=cy, first_k=first_k, first_half=first_half, second_k=second_k):
                bcopy(first_k, me, first_half).wait_recv()
                onward = bcopy(5 + first_half, (1, 1, 0), first_half)
                onward.start()
                bcopy(second_k, me, 1 - first_half).wait_recv()
                north = bcopy(7, (cx, cy, 1))
                north.start()
                onward.wait_send()
                north.wait_send()

        @pl.when(jnp.logical_and(jnp.logical_and(x == 1, y == 1), south))
        def _():
            bcopy(5, me, 0).wait_recv()
            bcopy(6, me, 1).wait_recv()
            north = bcopy(7, (1, 1, 1))
            north.start()
            north.wait_send()

        @pl.when(jnp.logical_and(chip0, c == 1))
        def _():
            bcopy(0, me).wait_recv()

        @pl.when(jnp.logical_and(jnp.logical_not(chip0), c == 1))
        def _():
            bcopy(7, me).wait_recv()

        @pl.when(jnp.logical_and(chip0, south))
        def _():
            for k, to, part in sends0:
                bcopy(k, to, part).wait_send()

        for j, s0, s1, seg, d0 in _column_runs():
            if seg == 2:
                wc_ref[:, d0:d0 + (s1 - s0)] = blk0[:, s0:s1]
        zeros = lambda r, w: jnp.zeros((r, w), BF16)
        wc_ref[:, Q_LORA_RANK:CQ_PAD] = zeros(D_MODEL, CQ_PAD - Q_LORA_RANK)
        wc_ref[:, CQ_PAD + LANES:CQ_PAD + LANES + ROPE_LO] = zeros(D_MODEL, ROPE_LO)
        wc_ref[:, CQ_PAD + LANES + ROPE_HI:SEG_C] = zeros(D_MODEL, LANES - ROPE_HI)
        wq_ref[Q_LORA_RANK:CQ_PAD, :] = zeros(CQ_PAD - Q_LORA_RANK, hw)
        for h in range(MLA_HEADS):
            wq_ref[0:Q_LORA_RANK, HEAD_PAD * h + QK_HEAD_DIM:HEAD_PAD * (h + 1)] = zeros(Q_LORA_RANK, HEAD_PAD - QK_HEAD_DIM)
        for j in range(N_DEV):
            for h in range(MLA_HEADS):
                wq_ref[uq_rows * j:uq_rows * (j + 1), HEAD_PAD * h:HEAD_PAD * h + QK_HEAD_DIM] = g_uq[
                    j, :, QK_HEAD_DIM * h:QK_HEAD_DIM * (h + 1)]

    vmem = pl.BlockSpec(memory_space=pltpu.VMEM)
    return pl.pallas_call(
        body, name="gather_first",
        out_shape=[jax.ShapeDtypeStruct((D_MODEL, SEG_C), BF16), jax.ShapeDtypeStruct((CQ_PAD, hw), BF16),
                   jax.ShapeDtypeStruct(w_in.shape[1:], BF16), jax.ShapeDtypeStruct(w_oa.shape[1:], BF16),
                   jax.ShapeDtypeStruct(w_ob.shape[1:], BF16), jax.ShapeDtypeStruct(w_out.shape[1:], BF16),
                   jax.ShapeDtypeStruct((SEQ, D_MODEL), BF16), jax.ShapeDtypeStruct((D_MODEL, SEQ), BF16)]
        + [jax.ShapeDtypeStruct((SEQ, LANES), F32)] * 3,
        in_specs=[vmem] * 8, out_specs=[vmem] * 11,
        scratch_shapes=[pltpu.VMEM((N_DEV, uq_rows, MLA_HEADS * QK_HEAD_DIM), BF16), pltpu.VMEM((D_MODEL, SHARD_W), BF16),
                        pltpu.SemaphoreType.DMA((15,)), pltpu.SemaphoreType.DMA((15,))],
        compiler_params=pltpu.CompilerParams(vmem_limit_bytes=VMEM_BIG),
    )(w_in, w_uq2, w_oa, w_ob, w_out, x2, pos_col, invf_lane)


def _assemble_in(g_in):
    def body(g_ref, wa_ref, wb_ref):
        segs = [wa_ref, wb_ref]
        for j, s0, s1, seg, d0 in _column_runs():
            if seg < 2:
                segs[seg][:, d0:d0 + (s1 - s0)] = g_ref[j, :, s0:s1]

    return pl.pallas_call(
        body, name="assemble_in",
        out_shape=[jax.ShapeDtypeStruct((D_MODEL, SEG_A), BF16), jax.ShapeDtypeStruct((D_MODEL, SEG_B), BF16)],
        compiler_params=pltpu.CompilerParams(vmem_limit_bytes=VMEM_MID),
    )(g_in)


def _assemble_out(g_oa, g_ob, g_out):
    cols = D_MODEL // N_DEV

    def body(goa_ref, gob_ref, gout_ref, oa_ref, ob_ref, out_ref):
        for j in range(N_DEV):
            oa_ref[:, cols * j:cols * (j + 1)] = goa_ref[j]
            ob_ref[:, cols * j:cols * (j + 1)] = gob_ref[j]
            out_ref[cols * j:cols * (j + 1), :] = gout_ref[j]

    return pl.pallas_call(
        body, name="assemble_out",
        out_shape=[jax.ShapeDtypeStruct((MLA_WIDTH, D_MODEL), BF16), jax.ShapeDtypeStruct((SGU_WIDTH, D_MODEL), BF16),
                   jax.ShapeDtypeStruct((D_MODEL, D_MODEL), BF16)],
    )(g_oa, g_ob, g_out)


C_NAT = 544


def _to_parts(dwa, dwb):
    def body(dwa_ref, dwb_ref, pin_ref):
        pin_ref[0, :, 0:C_NAT] = jnp.zeros((D_MODEL, C_NAT), BF16)
        segs = [dwa_ref, dwb_ref]
        for j, s0, s1, seg, d0 in _column_runs():
            if seg < 2:
                pin_ref[j, :, s0:s1] = segs[seg][:, d0:d0 + (s1 - s0)]

    return pl.pallas_call(body, name="to_parts", out_shape=jax.ShapeDtypeStruct((N_DEV, D_MODEL, SHARD_W), BF16),
                          compiler_params=pltpu.CompilerParams(vmem_limit_bytes=VMEM_MID))(dwa, dwb)


def _dx_tail(dhs, ws, dx_res, dwc, p_uq, p_rep):
    tm = SEQ // 4
    rep_rows = p_rep.shape[1]
    c_rows = D_MODEL // N_DEV
    spec = [((c_rows, C_NAT), BF16), (p_uq.shape[1:], BF16), ((rep_rows, LANES), F32)]
    n = len(spec)

    nseg = len(dhs)

    def body(*refs):
        dh_refs, w_refs = refs[:nseg], refs[nseg:2 * nseg]
        dxr_ref, dwc_ref, puq_ref, prep_ref, dx_ref, call_ref, guq_ref, repall_ref, pc_ref, c_all, rep_all = refs[
            2 * nseg:2 * nseg + 11]
        rest = refs[2 * nseg + 11:]
        ras, tbs, rbs = rest[0:n], rest[n:2 * n], rest[2 * n:3 * n]
        send_sems, recv_sems, gsend, grecv = rest[3 * n:]
        step = pl.program_id(0)
        x, y, c = _mesh_pos()
        me_idx = 4 * x + 2 * y + c
        me, sibling = (x, y, c), (x, y, 1 - c)
        others = [(1 - x, y), (x, 1 - y), (1 - x, 1 - y)]
        parts = [pc_ref, puq_ref, prep_ref]
        gats = [rep_all, c_all]

        def stage1(chip, a):
            return _remote(parts[a].at[2 * chip + (1 - c)], ras[a].at[chip], send_sems, recv_sems, 7 * a + chip, sibling)

        def stage2(k, a):
            cx, cy = others[k]
            return _remote(tbs[a].at[k], rbs[a].at[k], send_sems, recv_sems, 7 * a + 4 + k, (cx, cy, c))

        def gcopy(a, k, blk, to):
            slab = gats[a].at[4 * blk[0] + 2 * blk[1] + blk[2]]
            return _remote(slab, slab, gsend, grecv, 7 * a + k, to)

        def chip_sum(a, chip):
            return parts[a][2 * chip + c].astype(F32) + ras[a][chip].astype(F32)

        @pl.when(step == 0)
        def _():
            for j, s0, s1, seg, d0 in _column_runs():
                if seg == 2:
                    for r in range(N_DEV):
                        pc_ref[r, :, s0:s1] = dwc_ref[c_rows * r:c_rows * (r + 1), d0:d0 + (s1 - s0)]
            for chip in range(4):
                for a in range(n):
                    stage1(chip, a).start()

        @pl.when(step == 1)
        def _():
            for chip in range(4):
                for a in range(n):
                    stage1(chip, a).wait_recv()
            for k, (cx, cy) in enumerate(others):
                for a in range(n):
                    tbs[a][k] = chip_sum(a, 2 * cx + cy).astype(spec[a][1])
                    stage2(k, a).start()

        @pl.when(step == 2)
        def _():
            for k in range(3):
                for a in range(n):
                    stage2(k, a).wait_recv()
            sums = []
            for a in range(n):
                acc = chip_sum(a, 2 * x + y)
                for k in range(3):
                    acc = acc + rbs[a][k].astype(F32)
                sums.append(acc)
            c_all[me_idx] = sums[0].astype(BF16)
            guq_ref[...] = sums[1]
            rep_all[me_idx] = sums[2]
            for a in range(2):
                for j, chip in enumerate(others):
                    gcopy(a, 1 + j, me, (*chip, c)).start()
                gcopy(a, 0, me, sibling).start()

        @pl.when(step == 3)
        def _():
            for j, chip in enumerate(others):
                for a in range(2):
                    gcopy(a, 1 + j, (*chip, c), me).wait_recv()
                    gcopy(a, 4 + j, (*chip, c), sibling).start()
            for a in range(2):
                gcopy(a, 0, sibling, me).wait_recv()
                for j, chip in enumerate(others):
                    gcopy(a, 4 + j, (*chip, 1 - c), me).wait_recv()
            for a in range(2):
                gcopy(a, 0, me, sibling).wait_send()
                for j, chip in enumerate(others):
                    gcopy(a, 1 + j, me, (*chip, c)).wait_send()
                    gcopy(a, 4 + j, (*chip, c), sibling).wait_send()
            for a in range(n):
                for chip in range(4):
                    stage1(chip, a).wait_send()
                for k in range(3):
                    stage2(k, a).wait_send()
            call_ref[...] = c_all[...]
            repall_ref[...] = rep_all[...]

        acc = dxr_ref[...]
        for dh_ref, w_ref in zip(dh_refs, w_refs):
            acc = acc + _dot(dh_ref[...], w_ref[...], _NT)
        dx_ref[...] = acc

    row = lambda w: pl.BlockSpec((tm, w), lambda i: (i, 0))
    full = lambda shape: pl.BlockSpec(shape, lambda i: (0,) * len(shape))
    scratch = [pltpu.VMEM((N_DEV, c_rows, C_NAT), BF16), pltpu.VMEM((N_DEV, c_rows, C_NAT), BF16),
               pltpu.VMEM((N_DEV, rep_rows, LANES), F32)]
    for lead in (4, 3, 3):
        scratch += [pltpu.VMEM((lead,) + tuple(shape), dt) for shape, dt in spec]
    scratch += [pltpu.SemaphoreType.DMA((7 * n,)), pltpu.SemaphoreType.DMA((7 * n,)),
                pltpu.SemaphoreType.DMA((14,)), pltpu.SemaphoreType.DMA((14,))]
    return pl.pallas_call(
        body, name="dx_tail", grid=(SEQ // tm,),
        in_specs=[row(dh.shape[1]) for dh in dhs] + [full(w.shape) for w in ws]
        + [row(D_MODEL), full(dwc.shape), full(p_uq.shape), full(p_rep.shape)],
        out_specs=[row(D_MODEL), full((N_DEV, c_rows, C_NAT)), full(p_uq.shape[1:]), full((N_DEV, rep_rows, LANES))],
        out_shape=[jax.ShapeDtypeStruct((SEQ, D_MODEL), F32), jax.ShapeDtypeStruct((N_DEV, c_rows, C_NAT), BF16),
                   jax.ShapeDtypeStruct(p_uq.shape[1:], F32), jax.ShapeDtypeStruct((N_DEV, rep_rows, LANES), F32)],
        scratch_shapes=scratch,
        compiler_params=pltpu.CompilerParams(dimension_semantics=("arbitrary",), vmem_limit_bytes=VMEM_BIG),
    )(*dhs, *ws, dx_res, dwc, p_uq, p_rep)


def _sum_landed(landed, c_all):
    c_rows = D_MODEL // N_DEV

    def body(rin_ref, roa_ref, rob_ref, rout_ref, call_ref, gin_ref, goa_ref, gob_ref, gout_ref):
        def total(ref, sl):
            acc = ref[0, sl, :].astype(F32)
            for s in range(1, N_DEV):
                acc = acc + ref[s, sl, :].astype(F32)
            return acc

        x, y, c = _mesh_pos()
        dev0 = jnp.where(4 * x + 2 * y + c == 0, 1.0, 0.0)
        for j in range(N_DEV):
            sl = slice(c_rows * j, c_rows * (j + 1))
            tot = total(rin_ref, sl)
            gin_ref[0, sl, C_NAT:SHARD_W] = tot[:, C_NAT:SHARD_W]
            gin_ref[0, sl, 0:C_NAT] = tot[:, 0:C_NAT] + dev0 * call_ref[j].astype(F32)
        goa_ref[0] = total(roa_ref, slice(None))
        gob_ref[0] = total(rob_ref, slice(None))
        gout_ref[0] = total(rout_ref, slice(None))

    return pl.pallas_call(
        body, name="sum_landed",
        out_shape=[jax.ShapeDtypeStruct((1,) + r.shape[1:], F32) for r in landed],
        compiler_params=pltpu.CompilerParams(vmem_limit_bytes=VMEM_MID),
    )(*landed, c_all)


_O_CQ, _O_CKV, _O_KPE, _O_ZA, _O_U, _O_V, _O_ZB, _O_GA, _O_GB = 0, 384, 512, 544, 1056, 1568, 2080, 2592, 3616


def _to_segments(w):
    z = lambda n: jnp.zeros(w.shape[:-1] + (n,), w.dtype)
    seg_a = jnp.concatenate([w[..., _O_GA:_O_GB], w[..., _O_GB:IN_WIDTH], w[..., _O_ZA:_O_U]], axis=-1)
    seg_b = jnp.concatenate([w[..., _O_U:_O_V], w[..., _O_V:_O_ZB], w[..., _O_ZB:_O_GA]], axis=-1)
    seg_c = jnp.concatenate([w[..., _O_CQ:_O_CKV], z(CQ_PAD - Q_LORA_RANK), w[..., _O_CKV:_O_KPE],
                             z(ROPE_LO), w[..., _O_KPE:_O_ZA], z(LANES - ROPE_HI)], axis=-1)
    return seg_a, seg_b, seg_c


def _from_segments(seg_a, seg_b, seg_c):
    kpe0 = CQ_PAD + LANES + ROPE_LO
    return jnp.concatenate([
        seg_c[..., 0:Q_LORA_RANK], seg_c[..., CQ_PAD:CQ_PAD + LANES], seg_c[..., kpe0:kpe0 + QK_ROPE_DIM],
        seg_a[..., 2 * D_MODEL:SEG_A], seg_b, seg_a[..., 0:2 * D_MODEL]], axis=-1)


def kernel(x, positions, w_in, b_in, g_q, w_uq, g_kv, w_ukv, w_oa, sgu_ln_g, sgu_ln_b, w_s, b_s, w_ob, w_out, ln_g, ln_b, loss_target, m_w_in, m_b_in, m_g_q, m_w_uq, m_g_kv, m_w_ukv, m_w_oa, m_sgu_ln_g, m_sgu_ln_b, m_w_s, m_b_s, m_w_ob, m_w_out, m_ln_g, m_ln_b, v_w_in, v_b_in, v_g_q, v_w_uq, v_g_kv, v_w_ukv, v_w_oa, v_sgu_ln_g, v_sgu_ln_b, v_w_s, v_b_s, v_w_ob, v_w_out, v_ln_g, v_ln_b):
    w_uq2 = w_uq[0].reshape(Q_LORA_RANK // N_DEV, MLA_HEADS * QK_HEAD_DIM)
    inv_freq = ROPE_THETA ** (-jnp.arange(0, QK_ROPE_DIM, 2, dtype=F32) / QK_ROPE_DIM)
    invf_lane = jnp.concatenate([jnp.zeros((ROPE_LO,), F32), inv_freq, inv_freq,
                                 jnp.zeros((LANES - ROPE_HI,), F32)]).reshape(1, LANES)
    first = _gather_first(w_in, w_uq2, w_oa, w_ob, w_out, x[0], positions.reshape(SEQ, 1), invf_lane)
    partials = _local_step(x[0], loss_target[0], first, b_in, g_q, g_kv, w_ukv, sgu_ln_g, sgu_ln_b, w_s, b_s, ln_g, ln_b)
    weights = dict(w_in=w_in, b_in=b_in, g_q=g_q, w_uq=w_uq, g_kv=g_kv, w_ukv=w_ukv, w_oa=w_oa, sgu_ln_g=sgu_ln_g,
                   sgu_ln_b=sgu_ln_b, w_s=w_s, b_s=b_s, w_ob=w_ob, w_out=w_out, ln_g=ln_g, ln_b=ln_b)
    moms = dict(w_in=m_w_in, b_in=m_b_in, g_q=m_g_q, w_uq=m_w_uq, g_kv=m_g_kv, w_ukv=m_w_ukv, w_oa=m_w_oa,
                sgu_ln_g=m_sgu_ln_g, sgu_ln_b=m_sgu_ln_b, w_s=m_w_s, b_s=m_b_s, w_ob=m_w_ob, w_out=m_w_out,
                ln_g=m_ln_g, ln_b=m_ln_b)
    vars_ = dict(w_in=v_w_in, b_in=v_b_in, g_q=v_g_q, w_uq=v_w_uq, g_kv=v_g_kv, w_ukv=v_w_ukv, w_oa=v_w_oa,
                 sgu_ln_g=v_sgu_ln_g, sgu_ln_b=v_sgu_ln_b, w_s=v_w_s, b_s=v_b_s, w_ob=v_w_ob, w_out=v_w_out,
                 ln_g=v_ln_g, ln_b=v_ln_b)
    return _reduce_and_update(partials, weights, moms, vars_)


def _local_step(x2, tgt, first, b_in, g_q, g_kv, w_ukv, sgu_ln_g, sgu_ln_b, w_s, b_s, ln_g, ln_b):
    wc, wq, win_b, oa_b, ob_b, out_b, x_bf, xt_bf, c_t, sa_t, sb_t = first
    ba, bb, bc = _to_segments(b_in)
    w_ukv_bf = w_ukv[0].astype(BF16)
    wkn = jnp.pad(w_ukv_bf[:, :, :QK_NOPE_DIM], ((0, 0), (0, 0), (0, HEAD_PAD - QK_NOPE_DIM))).reshape(KV_LORA_RANK, -1)
    wv = jnp.pad(w_ukv_bf[:, :, QK_NOPE_DIM:], ((0, 0), (0, 0), (0, HEAD_PAD - V_HEAD_DIM))).reshape(KV_LORA_RANK, -1)
    gq = jnp.pad(g_q, ((0, 0), (0, CQ_PAD - Q_LORA_RANK)))
    bias_full = jnp.repeat(b_s[0].T, SGU_GROUP_DIM, axis=1)
    w_s3 = w_s[0]
    w_st3 = jnp.swapaxes(w_s3, 1, 2)

    h_c = _mm(x_bf, wc, bias=bc, tm=512, tn=SEG_C, name="in_proj_c")
    q, k, kt, vx, vxt = _mla_prep(h_c, gq, g_kv, wq, wkn, wv, c_t, sa_t, sb_t)
    o, lse, (g_in,) = _attn_fwd(q, kt, vx, (win_b,))
    wa, wb = _assemble_in(g_in)
    h_a, (g_out,) = _mm(x_bf, wa, bias=ba, own=(out_b,), tm=512, tn=SEG_A // 2, name="in_proj_a")
    h_b, (g_oa, g_ob) = _mm(x_bf, wb, bias=bb, own=(oa_b, ob_b), tm=512, tn=SEG_B // 2, name="in_proj_b")
    y_b = _sgu_fwd(h_b, sgu_ln_g, sgu_ln_b, w_s3, bias_full)
    w_oa_f, w_ob_f, w_out_f = _assemble_out(g_oa, g_ob, g_out)

    (loss_row, dx_res, dh_a, d_o, d_yb, p_oa, p_ob, p_out, d_lng, d_lnb, d_ba) = _merge(
        x2, o, h_a, y_b, tgt, w_oa_f, w_ob_f, w_out_f, ln_g, ln_b)
    (dh_b, d_ws, d_bs_t, d_slg, d_slb, d_bb), (r_out,) = _sgu_bwd(h_b, d_yb, sgu_ln_g, sgu_ln_b, w_s3, w_st3, bias_full,
                                                                 (p_out,))
    d_wa, (r_oa,) = _mm(xt_bf, dh_a, out_dtype=BF16, parts=(p_oa,), tm=512, tn=512, name="dw_in_a")
    d_wb, (r_ob,) = _mm(xt_bf, dh_b, out_dtype=BF16, parts=(p_ob,), tm=512, tn=512, name="dw_in_b")
    dq, dk, dv, landed_in = _attn_bwd(q, kt, k, vxt, d_o, o, lse, (_to_parts(d_wa, d_wb),))
    landed = (*landed_in, r_oa, r_ob, r_out)
    dh_c, p_uq, d_wkn, d_wv, d_gq, d_gkv, d_bc = _mla_bwd(dq, dk, dv, h_c, gq, g_kv, wq, wkn, wv, c_t, sa_t, sb_t)
    d_wc = _mm(xt_bf, dh_c, out_dtype=BF16, tm=512, tn=SEG_C, name="dw_in_c")


    p_b_in = _from_segments(d_ba, d_bb, d_bc)
    p_w_ukv = jnp.concatenate([d_wkn.reshape(KV_LORA_RANK, MLA_HEADS, HEAD_PAD)[:, :, :QK_NOPE_DIM],
                               d_wv.reshape(KV_LORA_RANK, MLA_HEADS, HEAD_PAD)[:, :, :V_HEAD_DIM]], axis=-1)
    p_g_q = d_gq[:, :Q_LORA_RANK]
    p_b_s = d_bs_t[:, :SGU_GROUPS].T
    replicated = [p_b_in, p_g_q, d_gkv, p_w_ukv, d_slg, d_slb, d_ws, p_b_s, d_lng, d_lnb]
    return loss_row, ((dh_a, dh_b, dh_c), (wa, wb, wc), dx_res), landed, d_wc, p_uq, replicated


_NAMES = ["w_in", "b_in", "g_q", "w_uq", "g_kv", "w_ukv", "w_oa", "sgu_ln_g", "sgu_ln_b", "w_s", "b_s", "w_ob",
          "w_out", "ln_g", "ln_b"]
_REPLICATED = ["b_in", "g_q", "g_kv", "w_ukv", "sgu_ln_g", "sgu_ln_b", "w_s", "b_s", "ln_g", "ln_b"]


def _reduce_and_update(partials, weights, moms, vars_):
    loss_row, (dhs, ws, dx_res), landed, d_wc, p_uq, replicated = partials
    rep_flat = jnp.concatenate([a.reshape(-1) for a in replicated] + [loss_row[0, :1]])
    rep_flat = jnp.pad(rep_flat, (0, N_DEV * PACK_R_ROWS * LANES - rep_flat.size))
    dx_ab, c_all, g_uq, rep_all = _dx_tail(dhs[:2], ws[:2], dx_res, d_wc, p_uq,
                                           rep_flat.reshape(N_DEV, PACK_R_ROWS, LANES))
    dx = _mm(dhs[2], ws[2], tb=True, add=dx_ab, tm=512, tn=D_MODEL, name="dx_c")
    g_in, g_oa, g_ob, g_out = _sum_landed(landed, c_all)
    rep_sum = rep_all.reshape(-1)
    grads, pos = dict(w_in=g_in, w_uq=g_uq, w_oa=g_oa, w_ob=g_ob, w_out=g_out), 0
    for nm in _REPLICATED:
        grads[nm] = rep_sum[pos:pos + weights[nm].size]
        pos += weights[nm].size
    loss = rep_sum[pos]
    grads = {nm: grads[nm].reshape(weights[nm].shape) for nm in _NAMES}
    deltas, new_m, new_v = _adamw_all([weights[nm] for nm in _NAMES], [grads[nm] for nm in _NAMES],
                                      [moms[nm] for nm in _NAMES], [vars_[nm] for nm in _NAMES])
    return (loss, dx.reshape(1, SEQ, D_MODEL), *[grads[nm] for nm in _NAMES], *deltas, *new_m, *new_v)
```

```python
import math

import jax
import jax.numpy as jnp
from jax import lax
from jax.experimental import pallas as pl
from jax.experimental.pallas import tpu as pltpu

F32 = jnp.float32
BF16 = jnp.bfloat16

D_MODEL = 1024
SEQ = 2048
N_DEV = 8
MLA_HEADS = 8
Q_LORA_RANK = 384
KV_LORA_RANK = 128
QK_NOPE_DIM = 64
QK_ROPE_DIM = 32
V_HEAD_DIM = 64
QK_HEAD_DIM = QK_NOPE_DIM + QK_ROPE_DIM
MLA_WIDTH = MLA_HEADS * V_HEAD_DIM
ROPE_THETA = 10000.0
SGU_GROUPS = 8
SGU_GROUP_DIM = 64
SGU_WIDTH = SGU_GROUPS * SGU_GROUP_DIM
CHUNK = 128
RMS_EPS = 1e-6
LN_EPS = 1e-5
DN_ALPHA = 2.0 ** 0.25
IN_WIDTH = 4640
ATTN_SCALE = QK_HEAD_DIM ** -0.5

ADAM_LR = 0.001
ADAM_B1 = 0.9
ADAM_B2 = 0.999
ADAM_EPS = 1e-08
ADAM_WD = 0.01
ADAM_STEP = 10

LANES = 128
HEAD_PAD = 128
ROPE_LO = QK_NOPE_DIM
ROPE_MID = ROPE_LO + QK_ROPE_DIM // 2
ROPE_HI = ROPE_LO + QK_ROPE_DIM
CQ_PAD = 512

SEG_A = 2560
SEG_B = 1536
SEG_C = 768

PACK_R_ROWS = 272
VMEM_BIG = 56 * 1024 * 1024
VMEM_MID = 40 * 1024 * 1024


def _sigmoid(x):
    return 1.0 / (1.0 + jnp.exp(-x))


def _gelu_and_grad(x):
    c0 = math.sqrt(2.0 / math.pi)
    x2 = x * x
    t = jnp.tanh(c0 * (x + 0.044715 * x * x2))
    g = 0.5 * x * (1.0 + t)
    dg = 0.5 * (1.0 + t) + 0.5 * x * (1.0 - t * t) * (c0 * (1.0 + 3.0 * 0.044715 * x2))
    return g, dg


def _dot(a, b, dims):
    return lax.dot_general(a, b, (dims, ((), ())), preferred_element_type=F32)


_NN = ((1,), (0,))
_NT = ((1,), (1,))
_TN = ((0,), (0,))


def _store_grad(dh_ref, db_ref, col, val):
    cols = slice(col, col + val.shape[1])
    dh_ref[:, cols] = val.astype(BF16)
    db_ref[:, cols] += jnp.sum(val, axis=0, keepdims=True)


def _mm(a, b, *, tb=False, bias=None, add=None, out_dtype=F32, own=(), parts=(), tm, tn, name):
    m, k = a.shape
    n = b.shape[0] if tb else b.shape[1]
    assert m % tm == 0 and n % tn == 0 and not (own and parts)
    dims = _NT if tb else _NN
    nown = len(own) + len(parts)
    nm = m // tm
    nsteps = (n // tn) * nm

    def body(*refs):
        a_ref, b_ref = refs[0], refs[1]
        pos = 2
        r = _dot(a_ref[...], b_ref[...], dims)
        if bias is not None:
            r = r + refs[pos][...]; pos += 1
        if add is not None:
            r = r + refs[pos][...]; pos += 1
        own_refs = refs[pos:pos + nown]; pos += nown
        refs[pos][...] = r.astype(out_dtype)
        if nown:
            gat_refs = refs[pos + 1:pos + 1 + nown]
            send_sems, recv_sems, local_sems = refs[pos + 1 + nown:]
            step = pl.program_id(0) * nm + pl.program_id(1)
            if own:
                _gather_behind(own_refs, gat_refs, send_sems, recv_sems, local_sems, step, nsteps - 2, nsteps - 1)
            else:
                exchange = _exchange_parts(own_refs, gat_refs, send_sems, recv_sems, local_sems)
                _exchange_start(step == 0, exchange)
                _exchange_finish(step == nsteps - 1, exchange)

    b_spec = pl.BlockSpec((tn, k), lambda j, i: (j, 0)) if tb else pl.BlockSpec((k, tn), lambda j, i: (0, j))
    in_specs, args = [pl.BlockSpec((tm, k), lambda j, i: (i, 0)), b_spec], [a, b]
    if bias is not None:
        in_specs.append(pl.BlockSpec((1, tn), lambda j, i: (0, j))); args.append(bias)
    if add is not None:
        in_specs.append(pl.BlockSpec((tm, tn), lambda j, i: (i, j))); args.append(add)
    hbm = pl.BlockSpec(memory_space=pl.ANY)
    res = pl.pallas_call(
        body, name=name, grid=(n // tn, nm), in_specs=in_specs + [hbm] * nown,
        out_specs=[pl.BlockSpec((tm, tn), lambda j, i: (i, j))] + [hbm] * nown,
        out_shape=[jax.ShapeDtypeStruct((m, n), out_dtype)]
        + [jax.ShapeDtypeStruct((N_DEV,) + o.shape, o.dtype) for o in own]
        + [jax.ShapeDtypeStruct(p.shape, p.dtype) for p in parts],
        scratch_shapes=_exchange_sems(nown) if nown else [],
        compiler_params=pltpu.CompilerParams(dimension_semantics=("arbitrary", "arbitrary"), vmem_limit_bytes=VMEM_BIG),
    )(*args, *own, *parts)
    return (res[0], res[1:]) if nown else res[0]


def _rope(x, c, sa, sb):
    return x * c + pltpu.roll(x, LANES - 16, 1) * sa + pltpu.roll(x, 16, 1) * sb


def _rope_t(dy, c, sa, sb):
    return dy * c + pltpu.roll(dy * sa, 16, 1) + pltpu.roll(dy * sb, LANES - 16, 1)


def _mla_prep(h_c, gq, gkv, wq, wkn, wvx, c_t, sa_t, sb_t, own):
    tm = 256
    hw = MLA_HEADS * HEAD_PAD
    nown = len(own)
    nsteps = SEQ // tm

    def body(cq_ref, ckv_ref, kpe_ref, gq_ref, gkv_ref, wq_ref, wkn_ref, wvx_ref, c_ref, sa_ref, sb_ref, *rest):
        own_refs, (q_ref, k_ref, kt_ref, vx_ref, vxt_ref) = rest[:nown], rest[nown:nown + 5]
        gat_refs, (send_sems, recv_sems, local_sems) = rest[nown + 5:2 * nown + 5], rest[2 * nown + 5:]
        _gather_behind(own_refs, gat_refs, send_sems, recv_sems, local_sems, pl.program_id(0), nsteps - 2, nsteps - 1)
        c, sa, sb = c_ref[...], sa_ref[...], sb_ref[...]
        cq = cq_ref[...]
        rq = lax.rsqrt(jnp.sum(cq * cq, axis=1, keepdims=True) * (1.0 / Q_LORA_RANK) + RMS_EPS)
        cqn = ((cq * rq) * gq_ref[...]).astype(BF16)
        qall = _dot(cqn, wq_ref[...], _NN)
        for h in range(MLA_HEADS):
            sl = slice(HEAD_PAD * h, HEAD_PAD * (h + 1))
            q_ref[:, sl] = (_rope(qall[:, sl], c, sa, sb) * ATTN_SCALE).astype(BF16)
        ckv = ckv_ref[...]
        rkv = lax.rsqrt(jnp.sum(ckv * ckv, axis=1, keepdims=True) * (1.0 / KV_LORA_RANK) + RMS_EPS)
        ckvn = ((ckv * rkv) * gkv_ref[...]).astype(BF16)
        knall = _dot(ckvn, wkn_ref[...], _NN)
        vall = _dot(ckvn, wvx_ref[...], _NN)
        kper = _rope(kpe_ref[...], c, sa, sb)
        ones_half = (lax.broadcasted_iota(jnp.int32, (tm, HEAD_PAD), 1) >= V_HEAD_DIM).astype(F32)
        for h in range(MLA_HEADS):
            sl = slice(HEAD_PAD * h, HEAD_PAD * (h + 1))
            kh = knall[:, sl] + kper
            vh = vall[:, sl] + ones_half
            k_ref[:, sl] = kh.astype(BF16)
            kt_ref[sl, :] = kh.T.astype(BF16)
            vx_ref[:, sl] = vh.astype(BF16)
            vxt_ref[sl, :] = vh.T.astype(BF16)

    full = lambda shape: pl.BlockSpec(shape, lambda i: (0, 0))
    tab = pl.BlockSpec((tm, LANES), lambda i: (i, 0))
    row = pl.BlockSpec((tm, hw), lambda i: (i, 0))
    col = pl.BlockSpec((hw, tm), lambda i: (0, i))
    hbm = pl.BlockSpec(memory_space=pl.ANY)
    res = pl.pallas_call(
        body, name="mla_prep", grid=(nsteps,),
        in_specs=[pl.BlockSpec((tm, CQ_PAD), lambda i: (i, 0)),
                  pl.BlockSpec((tm, LANES), lambda i: (i, CQ_PAD // LANES)),
                  pl.BlockSpec((tm, LANES), lambda i: (i, CQ_PAD // LANES + 1)),
                  full((1, CQ_PAD)), full((1, KV_LORA_RANK)),
                  full((CQ_PAD, hw)), full((KV_LORA_RANK, hw)), full((KV_LORA_RANK, hw)), tab, tab, tab] + [hbm] * nown,
        out_specs=[row, row, col, row, col] + [hbm] * nown,
        out_shape=[jax.ShapeDtypeStruct((SEQ, hw), BF16), jax.ShapeDtypeStruct((SEQ, hw), BF16),
                   jax.ShapeDtypeStruct((hw, SEQ), BF16), jax.ShapeDtypeStruct((SEQ, hw), BF16),
                   jax.ShapeDtypeStruct((hw, SEQ), BF16)]
        + [jax.ShapeDtypeStruct((N_DEV,) + a.shape, a.dtype) for a in own],
        scratch_shapes=_exchange_sems(nown),
        compiler_params=pltpu.CompilerParams(dimension_semantics=("arbitrary",), vmem_limit_bytes=VMEM_MID),
    )(h_c, h_c, h_c, gq, gkv, wq, wkn, wvx, c_t, sa_t, sb_t, *own)
    return res[:5], res[5:]


ATT_T = 512
ATT_STRIP = 64


def _attn_fwd(q, kt, vx, own):
    t, rs = ATT_T, ATT_STRIP
    nown = len(own)
    nq = SEQ // t
    nsteps = (MLA_HEADS // 2) * nq

    def body(q_ref, kt_ref, vx_ref, *rest):
        own_refs, (o_ref, l_ref), gat_refs = rest[:nown], rest[nown:nown + 2], rest[nown + 2:2 * nown + 2]
        s_scr, p_scr, m_scr, a_scr, acc_scr, send_sems, recv_sems, local_sems = rest[2 * nown + 2:]
        qi = pl.program_id(1)
        _gather_behind(own_refs, gat_refs, send_sems, recv_sems, local_sems, pl.program_id(0) * nq + qi,
                       nsteps - 2, nsteps - 1)
        lane = lax.broadcasted_iota(jnp.int32, (t, LANES), 1)
        m_scr[...] = jnp.full((2, t, LANES), -1e30, F32)
        acc_scr[...] = jnp.zeros((2, t, LANES), F32)

        def block(j, masked):
            off = pl.multiple_of(j * t, t)
            for a in range(2):
                sl = slice(HEAD_PAD * a, HEAD_PAD * (a + 1))
                s_scr[a] = _dot(q_ref[:, sl], kt_ref[sl, pl.ds(off, t)], _NN)
                for r in range(t // rs):
                    rows = slice(rs * r, rs * (r + 1))
                    s = s_scr[a, rows, :]
                    if masked:
                        rowi = lax.broadcasted_iota(jnp.int32, (rs, t), 0) + rs * r
                        coli = lax.broadcasted_iota(jnp.int32, (rs, t), 1)
                        s = jnp.where(coli <= rowi, s, -1e30)
                    m_old = m_scr[a, rows, :]
                    m_new = jnp.maximum(m_old, jnp.max(s, axis=1, keepdims=True))
                    p_scr[a, rows, :] = jnp.exp(s - m_new[:, :1]).astype(BF16)
                    a_scr[a, rows, :] = jnp.exp(m_old - m_new)
                    m_scr[a, rows, :] = m_new
                acc_scr[a] = acc_scr[a] * a_scr[a] + _dot(p_scr[a], vx_ref[pl.ds(off, t), sl], _NN)

        def step(j, carry):
            block(j, False)
            return carry
        lax.fori_loop(0, qi, step, 0)
        block(qi, True)
        res = []
        for a in range(2):
            acc = acc_scr[a]
            l = acc[:, V_HEAD_DIM:V_HEAD_DIM + 1]
            res.append((acc / l, m_scr[a] + jnp.log(l)))
        o_ref[...] = jnp.where(lane < V_HEAD_DIM, res[0][0], pltpu.roll(res[1][0], V_HEAD_DIM, 1))
        l_ref[...] = jnp.where(lane < V_HEAD_DIM, res[0][1], res[1][1])

    hbm = pl.BlockSpec(memory_space=pl.ANY)
    res = pl.pallas_call(
        body, name="attn_fwd", grid=(MLA_HEADS // 2, nq),
        in_specs=[pl.BlockSpec((t, 2 * HEAD_PAD), lambda p, i: (i, p)),
                  pl.BlockSpec((2 * HEAD_PAD, SEQ), lambda p, i: (p, 0)),
                  pl.BlockSpec((SEQ, 2 * HEAD_PAD), lambda p, i: (0, p))] + [hbm] * nown,
        out_specs=[pl.BlockSpec((t, LANES), lambda p, i: (i, p)),
                   pl.BlockSpec((t, LANES), lambda p, i: (i, p))] + [hbm] * nown,
        out_shape=[jax.ShapeDtypeStruct((SEQ, MLA_WIDTH), F32), jax.ShapeDtypeStruct((SEQ, MLA_WIDTH), F32)]
        + [jax.ShapeDtypeStruct((N_DEV,) + a.shape, a.dtype) for a in own],
        scratch_shapes=[pltpu.VMEM((2, t, t), F32), pltpu.VMEM((2, t, t), BF16), pltpu.VMEM((2, t, LANES), F32),
                        pltpu.VMEM((2, t, LANES), F32), pltpu.VMEM((2, t, LANES), F32)] + _exchange_sems(nown),
        compiler_params=pltpu.CompilerParams(dimension_semantics=("arbitrary", "arbitrary"), vmem_limit_bytes=VMEM_MID),
    )(q, kt, vx, *own)
    return res[0], res[1], res[2:]


def _exchange_parts(parts, lands, send_sems, recv_sems, local_sems):
    x, y, c = _mesh_pos()
    me = 4 * x + 2 * y + c
    peers = [(x, y, 1 - c), (1 - x, y, c), (x, 1 - y, c), (1 - x, 1 - y, c),
             (1 - x, y, 1 - c), (x, 1 - y, 1 - c), (1 - x, 1 - y, 1 - c)]
    remote, local = [], []
    for a, (part, land) in enumerate(zip(parts, lands)):
        for k, peer in enumerate(peers):
            t = 4 * peer[0] + 2 * peer[1] + peer[2]
            remote.append(_remote(part.at[t], land.at[me], send_sems, recv_sems, 7 * a + k, peer))
        local.append(pltpu.make_async_copy(part.at[me], land.at[me], local_sems.at[a]))
    return remote, local


def _exchange_start(first_step, exchange):
    remote, local = exchange

    @pl.when(first_step)
    def _():
        for cp in remote + local:
            cp.start()


def _exchange_finish(last_step, exchange):
    remote, local = exchange

    @pl.when(last_step)
    def _():
        for cp in remote:
            cp.wait_recv()
        for cp in remote:
            cp.wait_send()
        for cp in local:
            cp.wait()


def _exchange_sems(npart):
    return [pltpu.SemaphoreType.DMA((7 * npart,)), pltpu.SemaphoreType.DMA((7 * npart,)),
            pltpu.SemaphoreType.DMA((npart,))]


def _attn_bwd(q, kt, k, vxt, d_o, o, lse, parts):
    t, rs = ATT_T, ATT_STRIP
    nq = SEQ // t
    npart = len(parts)
    nsteps = MLA_HEADS // 2

    def body(q_ref, kt_ref, k_ref, vxt_ref, do_ref, o_ref, l_ref, *rest):
        part_refs, rest = rest[:npart], rest[npart:]
        dq_ref, dk_ref, dv_ref = rest[:3]
        land_refs, rest = rest[3:3 + npart], rest[3 + npart:]
        s_scr, dp_scr, p_scr, ds_scr, st_scr, send_sems, recv_sems, local_sems = rest
        exchange = _exchange_parts(part_refs, land_refs, send_sems, recv_sems, local_sems)
        _exchange_start(pl.program_id(0) == 0, exchange)
        dk_ref[...] = jnp.zeros_like(dk_ref)
        dv_ref[...] = jnp.zeros_like(dv_ref)
        lane = lax.broadcasted_iota(jnp.int32, (t, LANES), 1)

        def qtile(i, carry):
            ioff = pl.multiple_of(i * t, t)
            do_i = do_ref[pl.ds(ioff, t), :]
            o_i = o_ref[pl.ds(ioff, t), :]
            l_i = l_ref[pl.ds(ioff, t), :]
            for a in range(2):
                sl = slice(HEAD_PAD * a, HEAD_PAD * (a + 1))
                sel = (lane < V_HEAD_DIM) if a == 0 else (lane >= V_HEAD_DIM)
                doa = jnp.where(sel, do_i, 0.0)
                oa = o_i
                if a == 1:
                    doa = pltpu.roll(doa, V_HEAD_DIM, 1)
                    oa = pltpu.roll(o_i, V_HEAD_DIM, 1)
                st_scr[0] = jnp.broadcast_to(jnp.sum(doa * oa, axis=1, keepdims=True), (t, LANES))
                st_scr[1] = jnp.broadcast_to(l_i[:, V_HEAD_DIM * a:V_HEAD_DIM * a + 1], (t, LANES))
                doa_bf = doa.astype(BF16)
                qa = q_ref[pl.ds(ioff, t), sl]

                def block(j, masked, dq_acc, sl=sl, qa=qa, doa_bf=doa_bf):
                    joff = pl.multiple_of(j * t, t)
                    s_scr[...] = _dot(qa, kt_ref[sl, pl.ds(joff, t)], _NN)
                    dp_scr[...] = _dot(doa_bf, vxt_ref[sl, pl.ds(joff, t)], _NN)
                    for r in range(t // rs):
                        rows = slice(rs * r, rs * (r + 1))
                        p = jnp.exp(s_scr[rows, :] - st_scr[1, rows, :1])
                        if masked:
                            rowi = lax.broadcasted_iota(jnp.int32, (rs, t), 0) + rs * r
                            coli = lax.broadcasted_iota(jnp.int32, (rs, t), 1)
                            p = jnp.where(coli <= rowi, p, 0.0)
                        p_scr[rows, :] = p.astype(BF16)
                        ds_scr[rows, :] = (p * (dp_scr[rows, :] - st_scr[0, rows, :1])).astype(BF16)
                    dk_ref[pl.ds(joff, t), sl] += _dot(ds_scr[...], qa, _TN)
                    dv_ref[pl.ds(joff, t), sl] += _dot(p_scr[...], doa_bf, _TN)
                    return dq_acc + _dot(ds_scr[...], k_ref[pl.ds(joff, t), sl], _NN)

                dq_acc = lax.fori_loop(0, i, lambda j, acc: block(j, False, acc), jnp.zeros((t, HEAD_PAD), F32))
                dq_ref[pl.ds(ioff, t), sl] = block(i, True, dq_acc)
            return carry

        lax.fori_loop(0, nq, qtile, 0)
        _exchange_finish(pl.program_id(0) == nsteps - 1, exchange)

    hw = MLA_HEADS * HEAD_PAD
    wide = pl.BlockSpec((SEQ, 2 * HEAD_PAD), lambda p: (0, p))
    wide_t = pl.BlockSpec((2 * HEAD_PAD, SEQ), lambda p: (p, 0))
    narrow = pl.BlockSpec((SEQ, LANES), lambda p: (0, p))
    hbm = pl.BlockSpec(memory_space=pl.ANY)
    res = pl.pallas_call(
        body, name="attn_bwd", grid=(nsteps,),
        in_specs=[wide, wide_t, wide, wide_t, narrow, narrow, narrow] + [hbm] * npart,
        out_specs=[wide, wide, wide] + [hbm] * npart,
        out_shape=[jax.ShapeDtypeStruct((SEQ, hw), F32)] * 3 + [jax.ShapeDtypeStruct(p.shape, p.dtype) for p in parts],
        scratch_shapes=[pltpu.VMEM((t, t), F32), pltpu.VMEM((t, t), F32), pltpu.VMEM((t, t), BF16),
                        pltpu.VMEM((t, t), BF16), pltpu.VMEM((2, t, LANES), F32)] + _exchange_sems(npart),
        compiler_params=pltpu.CompilerParams(dimension_semantics=("arbitrary",), vmem_limit_bytes=VMEM_BIG),
    )(q, kt, k, vxt, d_o, o, lse, *parts)
    return res[0], res[1], res[2], res[3:]


def _sgu_math(u, v, zb, lg, lb, ws_ref, bias):
    ug, dug = _gelu_and_grad(u)
    vg, dvg = _gelu_and_grad(v)
    mu = jnp.mean(vg, axis=1, keepdims=True)
    xc = vg - mu
    rstd = lax.rsqrt(jnp.mean(xc * xc, axis=1, keepdims=True) + LN_EPS)
    xh = xc * rstd
    vn_bf = (xh * lg + lb).astype(BF16)
    grp = lax.broadcasted_iota(jnp.int32, (CHUNK, SGU_WIDTH), 1) // SGU_GROUP_DIM
    r_i = lax.broadcasted_iota(jnp.int32, (CHUNK, CHUNK), 0)
    c_i = lax.broadcasted_iota(jnp.int32, (CHUNK, CHUNK), 1)
    tri, tri_t = r_i >= c_i, r_i <= c_i
    mixed = bias
    for g in range(SGU_GROUPS):
        wt = jnp.where(tri, ws_ref[g], 0.0).astype(BF16)
        mixed = mixed + jnp.where(grp == g, _dot(wt, vn_bf, _NN), 0.0)
    sb = _sigmoid(zb)
    return ug, dug, dvg, rstd, xh, vn_bf, grp, tri, tri_t, mixed, sb


def _sgu_fwd(h_b, lg, lb, w_s, bias_full):
    def body(u_ref, v_ref, zb_ref, lg_ref, lb_ref, ws_ref, bias_ref, yb_ref):
        zb = zb_ref[...]
        ug, _, _, _, _, _, _, _, _, mixed, sb = _sgu_math(u_ref[...], v_ref[...], zb, lg_ref[...], lb_ref[...],
                                                       ws_ref, bias_ref[...])
        yb_ref[...] = (ug * mixed) * (zb * sb)

    blk = lambda c: pl.BlockSpec((CHUNK, SGU_WIDTH), lambda i, c=c: (i, c))
    full2 = lambda shape: pl.BlockSpec(shape, lambda i: (0, 0))
    return pl.pallas_call(
        body, name="sgu_fwd", grid=(SEQ // CHUNK,),
        in_specs=[blk(0), blk(1), blk(2), full2((1, SGU_WIDTH)), full2((1, SGU_WIDTH)),
                  pl.BlockSpec((SGU_GROUPS, CHUNK, CHUNK), lambda i: (0, 0, 0)), full2((CHUNK, SGU_WIDTH))],
        out_specs=pl.BlockSpec((CHUNK, SGU_WIDTH), lambda i: (i, 0)),
        out_shape=jax.ShapeDtypeStruct((SEQ, SGU_WIDTH), F32),
        compiler_params=pltpu.CompilerParams(dimension_semantics=("arbitrary",)),
    )(h_b, h_b, h_b, lg, lb, w_s, bias_full)


def _sgu_bwd(h_b, d_yb, lg, lb, w_s, w_st, bias_full, parts):
    nsteps = SEQ // CHUNK
    npart = len(parts)

    def body(u_ref, v_ref, zb_ref, dyb_ref, lg_ref, lb_ref, ws_ref, wst_ref, bias_ref, *rest):
        part_refs, rest = rest[:npart], rest[npart:]
        dhb_ref, dws_ref, dbs_ref, dlg_ref, dlb_ref, dbb_ref = rest[:6]
        land_refs, (dbias_acc, send_sems, recv_sems, local_sems) = rest[6:6 + npart], rest[6 + npart:]
        step = pl.program_id(0)
        exchange = _exchange_parts(part_refs, land_refs, send_sems, recv_sems, local_sems)
        _exchange_start(step == 0, exchange)

        @pl.when(step == 0)
        def _():
            dbb_ref[...] = jnp.zeros_like(dbb_ref)
            dws_ref[...] = jnp.zeros_like(dws_ref)
            dlg_ref[...] = jnp.zeros_like(dlg_ref)
            dlb_ref[...] = jnp.zeros_like(dlb_ref)
            dbias_acc[...] = jnp.zeros_like(dbias_acc)

        zb = zb_ref[...]
        lg = lg_ref[...]
        ug, dug, dvg, rstd, xh, vn_bf, grp, tri, tri_t, mixed, sb = _sgu_math(
            u_ref[...], v_ref[...], zb, lg, lb_ref[...], ws_ref, bias_ref[...])
        dyb = dyb_ref[...]
        dsgu = dyb * (zb * sb)
        dzb = dyb * (ug * mixed) * (sb * (1.0 + zb * (1.0 - sb)))
        du = dsgu * mixed * dug
        dmixed = dsgu * ug
        dbias_acc[...] += dmixed
        dvn = jnp.zeros((CHUNK, SGU_WIDTH), F32)
        for g in range(SGU_GROUPS):
            dm_g = jnp.where(grp == g, dmixed, 0.0).astype(BF16)
            wtt = jnp.where(tri_t, wst_ref[g], 0.0).astype(BF16)
            dvn = dvn + _dot(wtt, dm_g, _NN)
            dws_ref[g] += jnp.where(tri, _dot(dm_g, vn_bf, _NT), 0.0)
        dlg_ref[...] += jnp.sum(dvn * xh, axis=0, keepdims=True)
        dlb_ref[...] += jnp.sum(dvn, axis=0, keepdims=True)
        dxh = dvn * lg
        dvgel = rstd * (dxh - jnp.mean(dxh, axis=1, keepdims=True) - xh * jnp.mean(dxh * xh, axis=1, keepdims=True))
        _store_grad(dhb_ref, dbb_ref, 0, du)
        _store_grad(dhb_ref, dbb_ref, SGU_WIDTH, dvgel * dvg)
        _store_grad(dhb_ref, dbb_ref, 2 * SGU_WIDTH, dzb)

        @pl.when(step == nsteps - 1)
        def _():
            acc = dbias_acc[...]
            lane = lax.broadcasted_iota(jnp.int32, (CHUNK, LANES), 1)
            out = jnp.zeros((CHUNK, LANES), F32)
            for g in range(SGU_GROUPS):
                sg = jnp.sum(jnp.where(grp == g, acc, 0.0), axis=1, keepdims=True)
                out = jnp.where(lane == g, sg, out)
            dbs_ref[...] = out

        _exchange_finish(step == nsteps - 1, exchange)

    blk = lambda c: pl.BlockSpec((CHUNK, SGU_WIDTH), lambda i, c=c: (i, c))
    full2 = lambda shape: pl.BlockSpec(shape, lambda i: (0, 0))
    full3 = pl.BlockSpec((SGU_GROUPS, CHUNK, CHUNK), lambda i: (0, 0, 0))
    hbm = pl.BlockSpec(memory_space=pl.ANY)
    res = pl.pallas_call(
        body, name="sgu_bwd", grid=(nsteps,),
        in_specs=[blk(0), blk(1), blk(2), pl.BlockSpec((CHUNK, SGU_WIDTH), lambda i: (i, 0)),
                  full2((1, SGU_WIDTH)), full2((1, SGU_WIDTH)), full3, full3, full2((CHUNK, SGU_WIDTH))] + [hbm] * npart,
        out_specs=[pl.BlockSpec((CHUNK, SEG_B), lambda i: (i, 0)), full3, full2((CHUNK, LANES)),
                   full2((1, SGU_WIDTH)), full2((1, SGU_WIDTH)), full2((1, SEG_B))] + [hbm] * npart,
        out_shape=[jax.ShapeDtypeStruct((SEQ, SEG_B), BF16),
                   jax.ShapeDtypeStruct((SGU_GROUPS, CHUNK, CHUNK), F32),
                   jax.ShapeDtypeStruct((CHUNK, LANES), F32),
                   jax.ShapeDtypeStruct((1, SGU_WIDTH), F32), jax.ShapeDtypeStruct((1, SGU_WIDTH), F32),
                   jax.ShapeDtypeStruct((1, SEG_B), F32)] + [jax.ShapeDtypeStruct(p.shape, p.dtype) for p in parts],
        scratch_shapes=[pltpu.VMEM((CHUNK, SGU_WIDTH), F32)] + _exchange_sems(npart),
        compiler_params=pltpu.CompilerParams(dimension_semantics=("arbitrary",)),
    )(h_b, h_b, h_b, d_yb, lg, lb, w_s, w_st, bias_full, *parts)
    return res[:6], res[6:]


def _merge(x, o, h_a, y_b, target, w_oa, w_ob, w_out, ln_g, ln_b):
    tm = 256
    nsteps = SEQ // tm

    def body(x_ref, o_ref, ga_ref, gb_ref, za_ref, yb_ref, tgt_ref, woa_ref, wob_ref, wout_ref, lng_ref, lnb_ref,
             loss_ref, dxr_ref, dha_ref, do_ref, dyb_ref, poa_ref, pob_ref, pout_ref, dlng_ref, dlnb_ref, dba_ref,
             dwoa_ref, dwob_ref, dwout_ref):
        step = pl.program_id(0)

        @pl.when(step == 0)
        def _():
            for r in (loss_ref, dwoa_ref, dwob_ref, dwout_ref, dlng_ref, dlnb_ref, dba_ref):
                r[...] = jnp.zeros_like(r)

        o = o_ref[...]
        za = za_ref[...]
        sa = _sigmoid(za)
        ya_bf = (o * (za * sa)).astype(BF16)
        yb_bf = yb_ref[...].astype(BF16)
        woa, wob, wout = woa_ref[...], wob_ref[...], wout_ref[...]
        pa = _dot(ya_bf, woa, _NN)
        pb = _dot(yb_bf, wob, _NN)
        sga = _sigmoid(ga_ref[...])
        sgb = _sigmoid(gb_ref[...])
        merged_bf = (sga * pa + sgb * pb).astype(BF16)
        r = DN_ALPHA * x_ref[...] + _dot(merged_bf, wout, _NN)
        mu = jnp.mean(r, axis=1, keepdims=True)
        rc = r - mu
        rstd = lax.rsqrt(jnp.mean(rc * rc, axis=1, keepdims=True) + LN_EPS)
        xh = rc * rstd
        lng = lng_ref[...]
        y = xh * lng + lnb_ref[...]
        e = y - tgt_ref[...]
        loss_ref[...] += 0.5 * jnp.sum(jnp.sum(e * e, axis=1, keepdims=True) * (1.0 / D_MODEL), axis=0, keepdims=True)

        dy = e * (1.0 / D_MODEL)
        dlng_ref[...] += jnp.sum(dy * xh, axis=0, keepdims=True)
        dlnb_ref[...] += jnp.sum(dy, axis=0, keepdims=True)
        dxh = dy * lng
        dr = rstd * (dxh - jnp.mean(dxh, axis=1, keepdims=True) - xh * jnp.mean(dxh * xh, axis=1, keepdims=True))
        dxr_ref[...] = DN_ALPHA * dr
        dr_bf = dr.astype(BF16)
        dwout_ref[...] += _dot(merged_bf, dr_bf, _TN)
        dmerged = _dot(dr_bf, wout, _NT)
        dpa_bf = (dmerged * sga).astype(BF16)
        dpb_bf = (dmerged * sgb).astype(BF16)
        _store_grad(dha_ref, dba_ref, 0, dmerged * pa * (sga * (1.0 - sga)))
        _store_grad(dha_ref, dba_ref, D_MODEL, dmerged * pb * (sgb * (1.0 - sgb)))
        dwoa_ref[...] += _dot(ya_bf, dpa_bf, _TN)
        dwob_ref[...] += _dot(yb_bf, dpb_bf, _TN)
        dya = _dot(dpa_bf, woa, _NT)
        dyb_ref[...] = _dot(dpb_bf, wob, _NT)
        do_ref[...] = dya * (za * sa)
        _store_grad(dha_ref, dba_ref, 2 * D_MODEL, dya * o * (sa * (1.0 + za * (1.0 - sa))))

        @pl.when(step == nsteps - 1)
        def _():
            cols = D_MODEL // N_DEV
            for j in range(N_DEV):
                poa_ref[j] = dwoa_ref[:, cols * j:cols * (j + 1)].astype(BF16)
                pob_ref[j] = dwob_ref[:, cols * j:cols * (j + 1)].astype(BF16)
                pout_ref[j] = dwout_ref[cols * j:cols * (j + 1), :].astype(BF16)

    row = lambda w, c=0: pl.BlockSpec((tm, w), lambda i, c=c: (i, c))
    full = lambda shape: pl.BlockSpec(shape, lambda i: (0, 0))
    full3 = lambda shape: pl.BlockSpec(shape, lambda i: (0, 0, 0))
    return pl.pallas_call(
        body, name="merge", grid=(nsteps,),
        in_specs=[row(D_MODEL), row(MLA_WIDTH), row(D_MODEL, 0), row(D_MODEL, 1), row(MLA_WIDTH, 4), row(SGU_WIDTH),
                  row(D_MODEL), full((MLA_WIDTH, D_MODEL)), full((SGU_WIDTH, D_MODEL)), full((D_MODEL, D_MODEL)),
                  full((1, D_MODEL)), full((1, D_MODEL))],
        out_specs=[full((1, LANES)), row(D_MODEL), row(SEG_A), row(MLA_WIDTH), row(SGU_WIDTH),
                   full3((N_DEV, MLA_WIDTH, D_MODEL // N_DEV)), full3((N_DEV, SGU_WIDTH, D_MODEL // N_DEV)),
                   full3((N_DEV, D_MODEL // N_DEV, D_MODEL)), full((1, D_MODEL)), full((1, D_MODEL)), full((1, SEG_A))],
        out_shape=[jax.ShapeDtypeStruct((1, LANES), F32),
                   jax.ShapeDtypeStruct((SEQ, D_MODEL), F32), jax.ShapeDtypeStruct((SEQ, SEG_A), BF16),
                   jax.ShapeDtypeStruct((SEQ, MLA_WIDTH), F32), jax.ShapeDtypeStruct((SEQ, SGU_WIDTH), F32),
                   jax.ShapeDtypeStruct((N_DEV, MLA_WIDTH, D_MODEL // N_DEV), BF16),
                   jax.ShapeDtypeStruct((N_DEV, SGU_WIDTH, D_MODEL // N_DEV), BF16),
                   jax.ShapeDtypeStruct((N_DEV, D_MODEL // N_DEV, D_MODEL), BF16),
                   jax.ShapeDtypeStruct((1, D_MODEL), F32), jax.ShapeDtypeStruct((1, D_MODEL), F32),
                   jax.ShapeDtypeStruct((1, SEG_A), F32)],
        scratch_shapes=[pltpu.VMEM((MLA_WIDTH, D_MODEL), F32), pltpu.VMEM((SGU_WIDTH, D_MODEL), F32),
                        pltpu.VMEM((D_MODEL, D_MODEL), F32)],
        compiler_params=pltpu.CompilerParams(dimension_semantics=("arbitrary",), vmem_limit_bytes=VMEM_BIG),
    )(x, o, h_a, h_a, h_a, y_b, target, w_oa, w_ob, w_out, ln_g, ln_b)


def _mla_bwd(dq, dk, dv, h_c, gq, gkv, wq, wkn, wv, c_t, sa_t, sb_t):
    tm = 256
    hw = MLA_HEADS * HEAD_PAD

    def body(dq_ref, dk_ref, dv_ref, cq_ref, ckv_ref, gq_ref, gkv_ref, wq_ref, wkn_ref, wv_ref, c_ref, sa_ref, sb_ref,
             dhc_ref, puq_ref, dwkn_ref, dwv_ref, dgq_ref, dgkv_ref, dbc_ref, pre_ref, dwq_ref):
        @pl.when(pl.program_id(0) == 0)
        def _():
            for r in (dwq_ref, dwkn_ref, dwv_ref, dgq_ref, dgkv_ref, dbc_ref):
                r[...] = jnp.zeros_like(r)

        c, sa, sb = c_ref[...], sa_ref[...], sb_ref[...]
        lane = lax.broadcasted_iota(jnp.int32, (tm, LANES), 1)
        rope_lanes = jnp.logical_and(lane >= ROPE_LO, lane < ROPE_HI)

        cq = cq_ref[...]
        gq = gq_ref[...]
        rq = lax.rsqrt(jnp.sum(cq * cq, axis=1, keepdims=True) * (1.0 / Q_LORA_RANK) + RMS_EPS)
        nq = cq * rq
        cqn_bf = (nq * gq).astype(BF16)
        for h in range(MLA_HEADS):
            sl = slice(HEAD_PAD * h, HEAD_PAD * (h + 1))
            pre_ref[:, sl] = _rope_t(dq_ref[:, sl] * ATTN_SCALE, c, sa, sb).astype(BF16)
        dqpre_bf = pre_ref[...]
        dcqn = _dot(dqpre_bf, wq_ref[...], _NT)
        dwq_ref[...] += _dot(cqn_bf, dqpre_bf, _TN)
        dgq_ref[...] += jnp.sum(dcqn * nq, axis=0, keepdims=True)
        dnq = dcqn * gq
        _store_grad(dhc_ref, dbc_ref, 0,
                    rq * (dnq - nq * (jnp.sum(dnq * nq, axis=1, keepdims=True) * (1.0 / Q_LORA_RANK))))

        ckv = ckv_ref[...]
        gkv = gkv_ref[...]
        rkv = lax.rsqrt(jnp.sum(ckv * ckv, axis=1, keepdims=True) * (1.0 / KV_LORA_RANK) + RMS_EPS)
        nkv = ckv * rkv
        ckvn_bf = (nkv * gkv).astype(BF16)
        dk = dk_ref[...]
        dk_bf = dk.astype(BF16)
        dv_bf = dv_ref[...].astype(BF16)
        dckvn = _dot(dk_bf, wkn_ref[...], _NT) + _dot(dv_bf, wv_ref[...], _NT)
        dwkn_ref[...] += _dot(ckvn_bf, dk_bf, _TN)
        dwv_ref[...] += _dot(ckvn_bf, dv_bf, _TN)
        dgkv_ref[...] += jnp.sum(dckvn * nkv, axis=0, keepdims=True)
        dnkv = dckvn * gkv
        _store_grad(dhc_ref, dbc_ref, CQ_PAD, rkv * (
            dnkv - nkv * (jnp.sum(dnkv * nkv, axis=1, keepdims=True) * (1.0 / KV_LORA_RANK))))
        dkpe = jnp.zeros((tm, LANES), F32)
        for h in range(MLA_HEADS):
            dkpe = dkpe + dk[:, HEAD_PAD * h:HEAD_PAD * (h + 1)]
        _store_grad(dhc_ref, dbc_ref, CQ_PAD + LANES, _rope_t(jnp.where(rope_lanes, dkpe, 0.0), c, sa, sb))

        @pl.when(pl.program_id(0) == SEQ // tm - 1)
        def _():
            rows = Q_LORA_RANK // N_DEV
            for j in range(N_DEV):
                for h in range(MLA_HEADS):
                    puq_ref[j, :, QK_HEAD_DIM * h:QK_HEAD_DIM * (h + 1)] = dwq_ref[
                        rows * j:rows * (j + 1), HEAD_PAD * h:HEAD_PAD * h + QK_HEAD_DIM].astype(BF16)

    full = lambda shape: pl.BlockSpec(shape, lambda i: (0, 0))
    row = lambda w, c=0: pl.BlockSpec((tm, w), lambda i, c=c: (i, c))
    return pl.pallas_call(
        body, name="mla_bwd", grid=(SEQ // tm,),
        in_specs=[row(hw), row(hw), row(hw), row(CQ_PAD, 0), row(LANES, CQ_PAD // LANES),
                  full((1, CQ_PAD)), full((1, KV_LORA_RANK)), full((CQ_PAD, hw)), full((KV_LORA_RANK, hw)),
                  full((KV_LORA_RANK, hw)), row(LANES), row(LANES), row(LANES)],
        out_specs=[row(SEG_C), pl.BlockSpec((N_DEV, Q_LORA_RANK // N_DEV, MLA_HEADS * QK_HEAD_DIM), lambda i: (0, 0, 0)),
                   full((KV_LORA_RANK, hw)), full((KV_LORA_RANK, hw)),
                   full((1, CQ_PAD)), full((1, KV_LORA_RANK)), full((1, SEG_C))],
        out_shape=[jax.ShapeDtypeStruct((SEQ, SEG_C), BF16),
                   jax.ShapeDtypeStruct((N_DEV, Q_LORA_RANK // N_DEV, MLA_HEADS * QK_HEAD_DIM), BF16),
                   jax.ShapeDtypeStruct((KV_LORA_RANK, hw), F32), jax.ShapeDtypeStruct((KV_LORA_RANK, hw), F32),
                   jax.ShapeDtypeStruct((1, CQ_PAD), F32), jax.ShapeDtypeStruct((1, KV_LORA_RANK), F32),
                   jax.ShapeDtypeStruct((1, SEG_C), F32)],
        scratch_shapes=[pltpu.VMEM((tm, hw), BF16), pltpu.VMEM((CQ_PAD, hw), F32)],
        compiler_params=pltpu.CompilerParams(dimension_semantics=("arbitrary",), vmem_limit_bytes=VMEM_MID),
    )(dq, dk, dv, h_c, h_c, gq, gkv, wq, wkn, wv, c_t, sa_t, sb_t)


def _adamw_all(ws, gs, ms, vs):
    n = len(ws)
    c1 = 1.0 / (1.0 - ADAM_B1 ** ADAM_STEP)
    c2 = 1.0 / (1.0 - ADAM_B2 ** ADAM_STEP)

    def body(*refs):
        for idx in range(n):
            w, g, m, v = (refs[idx][...], refs[n + idx][...], refs[2 * n + idx][...], refs[3 * n + idx][...])
            m_new = ADAM_B1 * m + (1.0 - ADAM_B1) * g
            v_new = ADAM_B2 * v + (1.0 - ADAM_B2) * (g * g)
            delta = -ADAM_LR * ((m_new * c1) / (jnp.sqrt(v_new * c2) + ADAM_EPS) + ADAM_WD * w)
            refs[4 * n + idx][...] = delta
            refs[5 * n + idx][...] = m_new
            refs[6 * n + idx][...] = v_new

    shapes = [jax.ShapeDtypeStruct(w.shape, F32) for w in ws]
    outs = pl.pallas_call(
        body, name="adamw", out_shape=shapes * 3,
        compiler_params=pltpu.CompilerParams(vmem_limit_bytes=VMEM_BIG),
    )(*ws, *gs, *ms, *vs)
    return outs[:n], outs[n:2 * n], outs[2 * n:]


SHARD_W = IN_WIDTH // N_DEV
W_IN_LO = 256

_PIECES = [(0, 384, 2, 0), (384, 512, 2, CQ_PAD), (512, 544, 2, CQ_PAD + LANES + ROPE_LO),
           (544, 1056, 0, 2 * D_MODEL), (1056, 1568, 1, 0), (1568, 2080, 1, SGU_WIDTH),
           (2080, 2592, 1, 2 * SGU_WIDTH), (2592, 3616, 0, 0), (3616, 4640, 0, D_MODEL)]


def _column_runs():
    runs = []
    for n0, n1, seg, d0 in _PIECES:
        for j in range(N_DEV):
            lo, hi = max(n0, j * SHARD_W), min(n1, (j + 1) * SHARD_W)
            if lo < hi:
                runs.append((j, lo - j * SHARD_W, hi - j * SHARD_W, seg, d0 + lo - n0))
    return runs


def _mesh_pos():
    return lax.axis_index("x"), lax.axis_index("y"), lax.axis_index("c")


def _remote(src, dst, send_sems, recv_sems, k, to):
    return pltpu.make_async_remote_copy(src_ref=src, dst_ref=dst, send_sem=send_sems.at[k], recv_sem=recv_sems.at[k],
                                        device_id=to, device_id_type=pl.DeviceIdType.MESH)


def _gather_exchange(gats, send_sems, recv_sems, meanwhile=None):
    x, y, c = _mesh_pos()
    me, sibling = (x, y, c), (x, y, 1 - c)
    chips = [(1 - x, y), (x, 1 - y), (1 - x, 1 - y)]

    def copy(a, k, blk, to):
        slab = gats[a].at[4 * blk[0] + 2 * blk[1] + blk[2]]
        return _remote(slab, slab, send_sems, recv_sems, 7 * a + k, to)

    arrays = range(len(gats))
    first = [copy(a, 1 + j, me, (*chip, c)) for j, chip in enumerate(chips) for a in arrays]
    first += [copy(a, 0, me, sibling) for a in arrays]
    for cp in first:
        cp.start()
    if meanwhile is not None:
        meanwhile()
    passed = []
    for j, chip in enumerate(chips):
        for a in arrays:
            copy(a, 1 + j, (*chip, c), me).wait_recv()
            fwd = copy(a, 4 + j, (*chip, c), sibling)
            fwd.start()
            passed.append(fwd)
    for a in arrays:
        copy(a, 0, sibling, me).wait_recv()
    for j, chip in enumerate(chips):
        for a in arrays:
            copy(a, 4 + j, (*chip, 1 - c), me).wait_recv()
    for cp in first + passed:
        cp.wait_send()


def _gather_behind(own, gats, send_sems, recv_sems, local_sems, step, mid, last):
    x, y, c = _mesh_pos()
    me, sibling = (x, y, c), (x, y, 1 - c)
    chips = [(1 - x, y), (x, 1 - y), (1 - x, 1 - y)]
    arrays = range(len(gats))

    def copy(a, k, blk, to, src=None):
        slab = gats[a].at[4 * blk[0] + 2 * blk[1] + blk[2]]
        return _remote(slab if src is None else src, slab, send_sems, recv_sems, 7 * a + k, to)

    first = [copy(a, 1 + j, me, (*chip, c), src=own[a]) for j, chip in enumerate(chips) for a in arrays]
    first += [copy(a, 0, me, sibling, src=own[a]) for a in arrays]
    local = [pltpu.make_async_copy(own[a], gats[a].at[4 * x + 2 * y + c], local_sems.at[a]) for a in arrays]
    passed = [copy(a, 4 + j, (*chip, c), sibling) for j, chip in enumerate(chips) for a in arrays]

    @pl.when(step == 0)
    def _():
        for cp in first + local:
            cp.start()

    @pl.when(step == mid)
    def _():
        for j, chip in enumerate(chips):
            for a in arrays:
                copy(a, 1 + j, (*chip, c), me).wait_recv()
        for cp in passed:
            cp.start()

    @pl.when(step == last)
    def _():
        for a in arrays:
            copy(a, 0, sibling, me).wait_recv()
        for j, chip in enumerate(chips):
            for a in arrays:
                copy(a, 4 + j, (*chip, 1 - c), me).wait_recv()
        for cp in first + passed:
            cp.wait_send()
        for cp in local:
            cp.wait()


def _gather_first(w_in, w_uq2, w_oa, w_ob, w_out, x2, pos_col, invf_lane):
    hw = MLA_HEADS * HEAD_PAD
    uq_rows = Q_LORA_RANK // N_DEV
    rows = 256

    def body(win_ref, wuq_ref, woa_ref, wob_ref, wout_ref, x_ref, pos_ref, invf_ref,
             wc_ref, wq_ref, winlo_ref, winhi_ref, oab_ref, obb_ref, outb_ref, xb_ref, xt_ref, c_ref, sa_ref, sb_ref,
             g_uq, blk0, send_sems, recv_sems):
        def local_work():
            for i in range(SEQ // rows):
                xi = x_ref[rows * i:rows * (i + 1), :]
                xb_ref[rows * i:rows * (i + 1), :] = xi.astype(BF16)
                xt_ref[:, rows * i:rows * (i + 1)] = xi.T.astype(BF16)
            ang = pos_ref[...].astype(F32) * invf_ref[...]
            cs, sn = jnp.cos(ang), jnp.sin(ang)
            lane = lax.broadcasted_iota(jnp.int32, ang.shape, 1)
            c_ref[...] = jnp.where(lane < ROPE_LO, 1.0, jnp.where(lane < ROPE_HI, cs, 0.0))
            sa_ref[...] = jnp.where(jnp.logical_and(lane >= ROPE_LO, lane < ROPE_MID), -sn, 0.0)
            sb_ref[...] = jnp.where(jnp.logical_and(lane >= ROPE_MID, lane < ROPE_HI), sn, 0.0)

        x, y, c = _mesh_pos()
        me = (x, y, c)
        winlo_ref[...] = win_ref[0, 0:W_IN_LO, :].astype(BF16)
        winhi_ref[...] = win_ref[0, W_IN_LO:D_MODEL, :].astype(BF16)
        oab_ref[...] = woa_ref[0].astype(BF16)
        obb_ref[...] = wob_ref[0].astype(BF16)
        outb_ref[...] = wout_ref[0].astype(BF16)
        g_uq[4 * x + 2 * y + c] = wuq_ref[...].astype(BF16)

        chip0 = jnp.logical_and(x == 0, y == 0)
        south = c == 0
        half = D_MODEL // 2
        halves = [blk0.at[pl.ds(0, half)], blk0.at[pl.ds(half, half)]]

        def bcopy(k, to, part=None):
            ref = blk0 if part is None else halves[part]
            return _remote(ref, ref, send_sems, recv_sems, 7 + k, to)

        sends0 = [(0, (0, 0, 1), None), (1, (1, 0, 0), 0), (2, (0, 1, 0), 1), (3, (1, 0, 0), 1), (4, (0, 1, 0), 0)]

        @pl.when(jnp.logical_and(chip0, south))
        def _():
            blk0[0:W_IN_LO, :] = winlo_ref[...]
            blk0[W_IN_LO:D_MODEL, :] = winhi_ref[...]
            for k, to, part in sends0:
                bcopy(k, to, part).start()

        _gather_exchange([g_uq], send_sems, recv_sems, meanwhile=local_work)

        for (cx, cy), first_k, first_half, second_k in (((1, 0), 1, 0, 3), ((0, 1), 2, 1, 4)):
            @pl.when(jnp.logical_and(jnp.logical_and(x == cx, y == cy), south))
            def _(cx=cx, cy=cy, first_k=first_k, first_half=first_half, second_k=second_k):
                bcopy(first_k, me, first_half).wait_recv()
                onward = bcopy(5 + first_half, (1, 1, 0), first_half)
                onward.start()
                bcopy(second_k, me, 1 - first_half).wait_recv()
                north = bcopy(7, (cx, cy, 1))
                north.start()
                onward.wait_send()
                north.wait_send()

        @pl.when(jnp.logical_and(jnp.logical_and(x == 1, y == 1), south))
        def _():
            bcopy(5, me, 0).wait_recv()
            bcopy(6, me, 1).wait_recv()
            north = bcopy(7, (1, 1, 1))
            north.start()
            north.wait_send()

        @pl.when(jnp.logical_and(chip0, c == 1))
        def _():
            bcopy(0, me).wait_recv()

        @pl.when(jnp.logical_and(jnp.logical_not(chip0), c == 1))
        def _():
            bcopy(7, me).wait_recv()

        @pl.when(jnp.logical_and(chip0, south))
        def _():
            for k, to, part in sends0:
                bcopy(k, to, part).wait_send()

        for j, s0, s1, seg, d0 in _column_runs():
            if seg == 2:
                wc_ref[:, d0:d0 + (s1 - s0)] = blk0[:, s0:s1]
        zeros = lambda r, w: jnp.zeros((r, w), BF16)
        wc_ref[:, Q_LORA_RANK:CQ_PAD] = zeros(D_MODEL, CQ_PAD - Q_LORA_RANK)
        wc_ref[:, CQ_PAD + LANES:CQ_PAD + LANES + ROPE_LO] = zeros(D_MODEL, ROPE_LO)
        wc_ref[:, CQ_PAD + LANES + ROPE_HI:SEG_C] = zeros(D_MODEL, LANES - ROPE_HI)
        wq_ref[Q_LORA_RANK:CQ_PAD, :] = zeros(CQ_PAD - Q_LORA_RANK, hw)
        for h in range(MLA_HEADS):
            wq_ref[0:Q_LORA_RANK, HEAD_PAD * h + QK_HEAD_DIM:HEAD_PAD * (h + 1)] = zeros(Q_LORA_RANK, HEAD_PAD - QK_HEAD_DIM)
        for j in range(N_DEV):
            for h in range(MLA_HEADS):
                wq_ref[uq_rows * j:uq_rows * (j + 1), HEAD_PAD * h:HEAD_PAD * h + QK_HEAD_DIM] = g_uq[
                    j, :, QK_HEAD_DIM * h:QK_HEAD_DIM * (h + 1)]

    vmem = pl.BlockSpec(memory_space=pltpu.VMEM)
    return pl.pallas_call(
        body, name="gather_first",
        out_shape=[jax.ShapeDtypeStruct((D_MODEL, SEG_C), BF16), jax.ShapeDtypeStruct((CQ_PAD, hw), BF16),
                   jax.ShapeDtypeStruct((W_IN_LO, SHARD_W), BF16), jax.ShapeDtypeStruct((D_MODEL - W_IN_LO, SHARD_W), BF16),
                   jax.ShapeDtypeStruct(w_oa.shape[1:], BF16),
                   jax.ShapeDtypeStruct(w_ob.shape[1:], BF16), jax.ShapeDtypeStruct(w_out.shape[1:], BF16),
                   jax.ShapeDtypeStruct((SEQ, D_MODEL), BF16), jax.ShapeDtypeStruct((D_MODEL, SEQ), BF16)]
        + [jax.ShapeDtypeStruct((SEQ, LANES), F32)] * 3,
        in_specs=[vmem] * 8, out_specs=[vmem] * 12,
        scratch_shapes=[pltpu.VMEM((N_DEV, uq_rows, MLA_HEADS * QK_HEAD_DIM), BF16), pltpu.VMEM((D_MODEL, SHARD_W), BF16),
                        pltpu.SemaphoreType.DMA((15,)), pltpu.SemaphoreType.DMA((15,))],
        compiler_params=pltpu.CompilerParams(vmem_limit_bytes=VMEM_BIG),
    )(w_in, w_uq2, w_oa, w_ob, w_out, x2, pos_col, invf_lane)


def _assemble_in(g_lo, g_hi):
    def body(glo_ref, ghi_ref, wa_ref, wb_ref):
        segs = [wa_ref, wb_ref]
        for j, s0, s1, seg, d0 in _column_runs():
            if seg < 2:
                segs[seg][0:W_IN_LO, d0:d0 + (s1 - s0)] = glo_ref[j, :, s0:s1]
                segs[seg][W_IN_LO:D_MODEL, d0:d0 + (s1 - s0)] = ghi_ref[j, :, s0:s1]

    return pl.pallas_call(
        body, name="assemble_in",
        out_shape=[jax.ShapeDtypeStruct((D_MODEL, SEG_A), BF16), jax.ShapeDtypeStruct((D_MODEL, SEG_B), BF16)],
        compiler_params=pltpu.CompilerParams(vmem_limit_bytes=VMEM_MID),
    )(g_lo, g_hi)


def _assemble_out(g_oa, g_ob, g_out):
    cols = D_MODEL // N_DEV

    def body(goa_ref, gob_ref, gout_ref, oa_ref, ob_ref, out_ref):
        for j in range(N_DEV):
            oa_ref[:, cols * j:cols * (j + 1)] = goa_ref[j]
            ob_ref[:, cols * j:cols * (j + 1)] = gob_ref[j]
            out_ref[cols * j:cols * (j + 1), :] = gout_ref[j]

    return pl.pallas_call(
        body, name="assemble_out",
        out_shape=[jax.ShapeDtypeStruct((MLA_WIDTH, D_MODEL), BF16), jax.ShapeDtypeStruct((SGU_WIDTH, D_MODEL), BF16),
                   jax.ShapeDtypeStruct((D_MODEL, D_MODEL), BF16)],
    )(g_oa, g_ob, g_out)


C_NAT = 544


def _to_parts(dwa, dwb):
    def body(dwa_ref, dwb_ref, pin_ref):
        pin_ref[0, :, 0:C_NAT] = jnp.zeros((D_MODEL, C_NAT), BF16)
        segs = [dwa_ref, dwb_ref]
        for j, s0, s1, seg, d0 in _column_runs():
            if seg < 2:
                pin_ref[j, :, s0:s1] = segs[seg][:, d0:d0 + (s1 - s0)]

    return pl.pallas_call(body, name="to_parts", out_shape=jax.ShapeDtypeStruct((N_DEV, D_MODEL, SHARD_W), BF16),
                          compiler_params=pltpu.CompilerParams(vmem_limit_bytes=VMEM_MID))(dwa, dwb)


def _dx_tail(dhs, ws, dx_res, dwc, p_uq, p_rep):
    tm = SEQ // 4
    rep_rows = p_rep.shape[1]
    c_rows = D_MODEL // N_DEV
    spec = [((c_rows, C_NAT), BF16), (p_uq.shape[1:], BF16), ((rep_rows, LANES), F32)]
    n = len(spec)

    nseg = len(dhs)

    def body(*refs):
        dh_refs, w_refs = refs[:nseg], refs[nseg:2 * nseg]
        dxr_ref, dwc_ref, puq_ref, prep_ref, dx_ref, call_ref, guq_ref, repall_ref, pc_ref, c_all, rep_all = refs[
            2 * nseg:2 * nseg + 11]
        rest = refs[2 * nseg + 11:]
        ras, tbs, rbs = rest[0:n], rest[n:2 * n], rest[2 * n:3 * n]
        send_sems, recv_sems, gsend, grecv = rest[3 * n:]
        step = pl.program_id(0)
        x, y, c = _mesh_pos()
        me_idx = 4 * x + 2 * y + c
        me, sibling = (x, y, c), (x, y, 1 - c)
        others = [(1 - x, y), (x, 1 - y), (1 - x, 1 - y)]
        parts = [pc_ref, puq_ref, prep_ref]
        gats = [rep_all, c_all]

        def stage1(chip, a):
            return _remote(parts[a].at[2 * chip + (1 - c)], ras[a].at[chip], send_sems, recv_sems, 7 * a + chip, sibling)

        def stage2(k, a):
            cx, cy = others[k]
            return _remote(tbs[a].at[k], rbs[a].at[k], send_sems, recv_sems, 7 * a + 4 + k, (cx, cy, c))

        def gcopy(a, k, blk, to):
            slab = gats[a].at[4 * blk[0] + 2 * blk[1] + blk[2]]
            return _remote(slab, slab, gsend, grecv, 7 * a + k, to)

        def chip_sum(a, chip):
            return parts[a][2 * chip + c].astype(F32) + ras[a][chip].astype(F32)

        @pl.when(step == 0)
        def _():
            for j, s0, s1, seg, d0 in _column_runs():
                if seg == 2:
                    for r in range(N_DEV):
                        pc_ref[r, :, s0:s1] = dwc_ref[c_rows * r:c_rows * (r + 1), d0:d0 + (s1 - s0)]
            for chip in range(4):
                for a in range(n):
                    stage1(chip, a).start()

        @pl.when(step == 1)
        def _():
            for chip in range(4):
                for a in range(n):
                    stage1(chip, a).wait_recv()
            for k, (cx, cy) in enumerate(others):
                for a in range(n):
                    tbs[a][k] = chip_sum(a, 2 * cx + cy).astype(spec[a][1])
                    stage2(k, a).start()

        @pl.when(step == 2)
        def _():
            for k in range(3):
                for a in range(n):
                    stage2(k, a).wait_recv()
            sums = []
            for a in range(n):
                acc = chip_sum(a, 2 * x + y)
                for k in range(3):
                    acc = acc + rbs[a][k].astype(F32)
                sums.append(acc)
            c_all[me_idx] = sums[0].astype(BF16)
            guq_ref[...] = sums[1]
            rep_all[me_idx] = sums[2]
            for a in range(2):
                for j, chip in enumerate(others):
                    gcopy(a, 1 + j, me, (*chip, c)).start()
                gcopy(a, 0, me, sibling).start()

        @pl.when(step == 3)
        def _():
            for j, chip in enumerate(others):
                for a in range(2):
                    gcopy(a, 1 + j, (*chip, c), me).wait_recv()
                    gcopy(a, 4 + j, (*chip, c), sibling).start()
            for a in range(2):
                gcopy(a, 0, sibling, me).wait_recv()
                for j, chip in enumerate(others):
                    gcopy(a, 4 + j, (*chip, 1 - c), me).wait_recv()
            for a in range(2):
                gcopy(a, 0, me, sibling).wait_send()
                for j, chip in enumerate(others):
                    gcopy(a, 1 + j, me, (*chip, c)).wait_send()
                    gcopy(a, 4 + j, (*chip, c), sibling).wait_send()
            for a in range(n):
                for chip in range(4):
                    stage1(chip, a).wait_send()
                for k in range(3):
                    stage2(k, a).wait_send()
            call_ref[...] = c_all[...]
            repall_ref[...] = rep_all[...]

        acc = dxr_ref[...]
        for dh_ref, w_ref in zip(dh_refs, w_refs):
            acc = acc + _dot(dh_ref[...], w_ref[...], _NT)
        dx_ref[...] = acc

    row = lambda w: pl.BlockSpec((tm, w), lambda i: (i, 0))
    full = lambda shape: pl.BlockSpec(shape, lambda i: (0,) * len(shape))
    scratch = [pltpu.VMEM((N_DEV, c_rows, C_NAT), BF16), pltpu.VMEM((N_DEV, c_rows, C_NAT), BF16),
               pltpu.VMEM((N_DEV, rep_rows, LANES), F32)]
    for lead in (4, 3, 3):
        scratch += [pltpu.VMEM((lead,) + tuple(shape), dt) for shape, dt in spec]
    scratch += [pltpu.SemaphoreType.DMA((7 * n,)), pltpu.SemaphoreType.DMA((7 * n,)),
                pltpu.SemaphoreType.DMA((14,)), pltpu.SemaphoreType.DMA((14,))]
    return pl.pallas_call(
        body, name="dx_tail", grid=(SEQ // tm,),
        in_specs=[row(dh.shape[1]) for dh in dhs] + [full(w.shape) for w in ws]
        + [row(D_MODEL), full(dwc.shape), full(p_uq.shape), full(p_rep.shape)],
        out_specs=[row(D_MODEL), full((N_DEV, c_rows, C_NAT)), full(p_uq.shape[1:]), full((N_DEV, rep_rows, LANES))],
        out_shape=[jax.ShapeDtypeStruct((SEQ, D_MODEL), F32), jax.ShapeDtypeStruct((N_DEV, c_rows, C_NAT), BF16),
                   jax.ShapeDtypeStruct(p_uq.shape[1:], F32), jax.ShapeDtypeStruct((N_DEV, rep_rows, LANES), F32)],
        scratch_shapes=scratch,
        compiler_params=pltpu.CompilerParams(dimension_semantics=("arbitrary",), vmem_limit_bytes=VMEM_BIG),
    )(*dhs, *ws, dx_res, dwc, p_uq, p_rep)


def _sum_landed(landed, c_all):
    c_rows = D_MODEL // N_DEV

    def body(rin_ref, roa_ref, rob_ref, rout_ref, call_ref, gin_ref, goa_ref, gob_ref, gout_ref):
        def total(ref, sl):
            acc = ref[0, sl, :].astype(F32)
            for s in range(1, N_DEV):
                acc = acc + ref[s, sl, :].astype(F32)
            return acc

        x, y, c = _mesh_pos()
        dev0 = jnp.where(4 * x + 2 * y + c == 0, 1.0, 0.0)
        for j in range(N_DEV):
            sl = slice(c_rows * j, c_rows * (j + 1))
            tot = total(rin_ref, sl)
            gin_ref[0, sl, C_NAT:SHARD_W] = tot[:, C_NAT:SHARD_W]
            gin_ref[0, sl, 0:C_NAT] = tot[:, 0:C_NAT] + dev0 * call_ref[j].astype(F32)
        goa_ref[0] = total(roa_ref, slice(None))
        gob_ref[0] = total(rob_ref, slice(None))
        gout_ref[0] = total(rout_ref, slice(None))

    return pl.pallas_call(
        body, name="sum_landed",
        out_shape=[jax.ShapeDtypeStruct((1,) + r.shape[1:], F32) for r in landed],
        compiler_params=pltpu.CompilerParams(vmem_limit_bytes=VMEM_MID),
    )(*landed, c_all)


_O_CQ, _O_CKV, _O_KPE, _O_ZA, _O_U, _O_V, _O_ZB, _O_GA, _O_GB = 0, 384, 512, 544, 1056, 1568, 2080, 2592, 3616


def _to_segments(w):
    z = lambda n: jnp.zeros(w.shape[:-1] + (n,), w.dtype)
    seg_a = jnp.concatenate([w[..., _O_GA:_O_GB], w[..., _O_GB:IN_WIDTH], w[..., _O_ZA:_O_U]], axis=-1)
    seg_b = jnp.concatenate([w[..., _O_U:_O_V], w[..., _O_V:_O_ZB], w[..., _O_ZB:_O_GA]], axis=-1)
    seg_c = jnp.concatenate([w[..., _O_CQ:_O_CKV], z(CQ_PAD - Q_LORA_RANK), w[..., _O_CKV:_O_KPE],
                             z(ROPE_LO), w[..., _O_KPE:_O_ZA], z(LANES - ROPE_HI)], axis=-1)
    return seg_a, seg_b, seg_c


def _from_segments(seg_a, seg_b, seg_c):
    kpe0 = CQ_PAD + LANES + ROPE_LO
    return jnp.concatenate([
        seg_c[..., 0:Q_LORA_RANK], seg_c[..., CQ_PAD:CQ_PAD + LANES], seg_c[..., kpe0:kpe0 + QK_ROPE_DIM],
        seg_a[..., 2 * D_MODEL:SEG_A], seg_b, seg_a[..., 0:2 * D_MODEL]], axis=-1)


def kernel(x, positions, w_in, b_in, g_q, w_uq, g_kv, w_ukv, w_oa, sgu_ln_g, sgu_ln_b, w_s, b_s, w_ob, w_out, ln_g, ln_b, loss_target, m_w_in, m_b_in, m_g_q, m_w_uq, m_g_kv, m_w_ukv, m_w_oa, m_sgu_ln_g, m_sgu_ln_b, m_w_s, m_b_s, m_w_ob, m_w_out, m_ln_g, m_ln_b, v_w_in, v_b_in, v_g_q, v_w_uq, v_g_kv, v_w_ukv, v_w_oa, v_sgu_ln_g, v_sgu_ln_b, v_w_s, v_b_s, v_w_ob, v_w_out, v_ln_g, v_ln_b):
    w_uq2 = w_uq[0].reshape(Q_LORA_RANK // N_DEV, MLA_HEADS * QK_HEAD_DIM)
    inv_freq = ROPE_THETA ** (-jnp.arange(0, QK_ROPE_DIM, 2, dtype=F32) / QK_ROPE_DIM)
    invf_lane = jnp.concatenate([jnp.zeros((ROPE_LO,), F32), inv_freq, inv_freq,
                                 jnp.zeros((LANES - ROPE_HI,), F32)]).reshape(1, LANES)
    first = _gather_first(w_in, w_uq2, w_oa, w_ob, w_out, x[0], positions.reshape(SEQ, 1), invf_lane)
    partials = _local_step(x[0], loss_target[0], first, b_in, g_q, g_kv, w_ukv, sgu_ln_g, sgu_ln_b, w_s, b_s, ln_g, ln_b)
    weights = dict(w_in=w_in, b_in=b_in, g_q=g_q, w_uq=w_uq, g_kv=g_kv, w_ukv=w_ukv, w_oa=w_oa, sgu_ln_g=sgu_ln_g,
                   sgu_ln_b=sgu_ln_b, w_s=w_s, b_s=b_s, w_ob=w_ob, w_out=w_out, ln_g=ln_g, ln_b=ln_b)
    moms = dict(w_in=m_w_in, b_in=m_b_in, g_q=m_g_q, w_uq=m_w_uq, g_kv=m_g_kv, w_ukv=m_w_ukv, w_oa=m_w_oa,
                sgu_ln_g=m_sgu_ln_g, sgu_ln_b=m_sgu_ln_b, w_s=m_w_s, b_s=m_b_s, w_ob=m_w_ob, w_out=m_w_out,
                ln_g=m_ln_g, ln_b=m_ln_b)
    vars_ = dict(w_in=v_w_in, b_in=v_b_in, g_q=v_g_q, w_uq=v_w_uq, g_kv=v_g_kv, w_ukv=v_w_ukv, w_oa=v_w_oa,
                 sgu_ln_g=v_sgu_ln_g, sgu_ln_b=v_sgu_ln_b, w_s=v_w_s, b_s=v_b_s, w_ob=v_w_ob, w_out=v_w_out,
                 ln_g=v_ln_g, ln_b=v_ln_b)
    return _reduce_and_update(partials, weights, moms, vars_)


def _local_step(x2, tgt, first, b_in, g_q, g_kv, w_ukv, sgu_ln_g, sgu_ln_b, w_s, b_s, ln_g, ln_b):
    wc, wq, win_lo, win_hi, oa_b, ob_b, out_b, x_bf, xt_bf, c_t, sa_t, sb_t = first
    ba, bb, bc = _to_segments(b_in)
    w_ukv_bf = w_ukv[0].astype(BF16)
    wkn = jnp.pad(w_ukv_bf[:, :, :QK_NOPE_DIM], ((0, 0), (0, 0), (0, HEAD_PAD - QK_NOPE_DIM))).reshape(KV_LORA_RANK, -1)
    wv = jnp.pad(w_ukv_bf[:, :, QK_NOPE_DIM:], ((0, 0), (0, 0), (0, HEAD_PAD - V_HEAD_DIM))).reshape(KV_LORA_RANK, -1)
    gq = jnp.pad(g_q, ((0, 0), (0, CQ_PAD - Q_LORA_RANK)))
    bias_full = jnp.repeat(b_s[0].T, SGU_GROUP_DIM, axis=1)
    w_s3 = w_s[0]
    w_st3 = jnp.swapaxes(w_s3, 1, 2)

    h_c = _mm(x_bf, wc, bias=bc, tm=512, tn=SEG_C, name="in_proj_c")
    (q, k, kt, vx, vxt), (g_lo,) = _mla_prep(h_c, gq, g_kv, wq, wkn, wv, c_t, sa_t, sb_t, (win_lo,))
    o, lse, (g_hi,) = _attn_fwd(q, kt, vx, (win_hi,))
    wa, wb = _assemble_in(g_lo, g_hi)
    h_a, (g_out,) = _mm(x_bf, wa, bias=ba, own=(out_b,), tm=512, tn=SEG_A // 2, name="in_proj_a")
    h_b, (g_oa, g_ob) = _mm(x_bf, wb, bias=bb, own=(oa_b, ob_b), tm=512, tn=SEG_B // 2, name="in_proj_b")
    y_b = _sgu_fwd(h_b, sgu_ln_g, sgu_ln_b, w_s3, bias_full)
    w_oa_f, w_ob_f, w_out_f = _assemble_out(g_oa, g_ob, g_out)

    (loss_row, dx_res, dh_a, d_o, d_yb, p_oa, p_ob, p_out, d_lng, d_lnb, d_ba) = _merge(
        x2, o, h_a, y_b, tgt, w_oa_f, w_ob_f, w_out_f, ln_g, ln_b)
    (dh_b, d_ws, d_bs_t, d_slg, d_slb, d_bb), (r_out,) = _sgu_bwd(h_b, d_yb, sgu_ln_g, sgu_ln_b, w_s3, w_st3, bias_full,
                                                                 (p_out,))
    d_wa, (r_oa,) = _mm(xt_bf, dh_a, out_dtype=BF16, parts=(p_oa,), tm=512, tn=512, name="dw_in_a")
    d_wb, (r_ob,) = _mm(xt_bf, dh_b, out_dtype=BF16, parts=(p_ob,), tm=512, tn=512, name="dw_in_b")
    dq, dk, dv, landed_in = _attn_bwd(q, kt, k, vxt, d_o, o, lse, (_to_parts(d_wa, d_wb),))
    landed = (*landed_in, r_oa, r_ob, r_out)
    dh_c, p_uq, d_wkn, d_wv, d_gq, d_gkv, d_bc = _mla_bwd(dq, dk, dv, h_c, gq, g_kv, wq, wkn, wv, c_t, sa_t, sb_t)
    d_wc = _mm(xt_bf, dh_c, out_dtype=BF16, tm=512, tn=SEG_C, name="dw_in_c")


    p_b_in = _from_segments(d_ba, d_bb, d_bc)
    p_w_ukv = jnp.concatenate([d_wkn.reshape(KV_LORA_RANK, MLA_HEADS, HEAD_PAD)[:, :, :QK_NOPE_DIM],
                               d_wv.reshape(KV_LORA_RANK, MLA_HEADS, HEAD_PAD)[:, :, :V_HEAD_DIM]], axis=-1)
    p_g_q = d_gq[:, :Q_LORA_RANK]
    p_b_s = d_bs_t[:, :SGU_GROUPS].T
    replicated = [p_b_in, p_g_q, d_gkv, p_w_ukv, d_slg, d_slb, d_ws, p_b_s, d_lng, d_lnb]
    return loss_row, ((dh_a, dh_b, dh_c), (wa, wb, wc), dx_res), landed, d_wc, p_uq, replicated


_NAMES = ["w_in", "b_in", "g_q", "w_uq", "g_kv", "w_ukv", "w_oa", "sgu_ln_g", "sgu_ln_b", "w_s", "b_s", "w_ob",
          "w_out", "ln_g", "ln_b"]
_REPLICATED = ["b_in", "g_q", "g_kv", "w_ukv", "sgu_ln_g", "sgu_ln_b", "w_s", "b_s", "ln_g", "ln_b"]


def _reduce_and_update(partials, weights, moms, vars_):
    loss_row, (dhs, ws, dx_res), landed, d_wc, p_uq, replicated = partials
    rep_flat = jnp.concatenate([a.reshape(-1) for a in replicated] + [loss_row[0, :1]])
    rep_flat = jnp.pad(rep_flat, (0, N_DEV * PACK_R_ROWS * LANES - rep_flat.size))
    dx_ab, c_all, g_uq, rep_all = _dx_tail(dhs[:2], ws[:2], dx_res, d_wc, p_uq,
                                           rep_flat.reshape(N_DEV, PACK_R_ROWS, LANES))
    dx = _mm(dhs[2], ws[2], tb=True, add=dx_ab, tm=512, tn=D_MODEL, name="dx_c")
    g_in, g_oa, g_ob, g_out = _sum_landed(landed, c_all)
    rep_sum = rep_all.reshape(-1)
    grads, pos = dict(w_in=g_in, w_uq=g_uq, w_oa=g_oa, w_ob=g_ob, w_out=g_out), 0
    for nm in _REPLICATED:
        grads[nm] = rep_sum[pos:pos + weights[nm].size]
        pos += weights[nm].size
    loss = rep_sum[pos]
    grads = {nm: grads[nm].reshape(weights[nm].shape) for nm in _NAMES}
    deltas, new_m, new_v = _adamw_all([weights[nm] for nm in _NAMES], [grads[nm] for nm in _NAMES],
                                      [moms[nm] for nm in _NAMES], [vars_[nm] for nm in _NAMES])
    return (loss, dx.reshape(1, SEQ, D_MODEL), *[grads[nm] for nm in _NAMES], *deltas, *new_m, *new_v)
```

```python
import math

import jax
import jax.numpy as jnp
from jax import lax
from jax.experimental import pallas as pl
from jax.experimental.pallas import tpu as pltpu

F32 = jnp.float32
BF16 = jnp.bfloat16

D_MODEL = 1024
SEQ = 2048
N_DEV = 8
MLA_HEADS = 8
Q_LORA_RANK = 384
KV_LORA_RANK = 128
QK_NOPE_DIM = 64
QK_ROPE_DIM = 32
V_HEAD_DIM = 64
QK_HEAD_DIM = QK_NOPE_DIM + QK_ROPE_DIM
MLA_WIDTH = MLA_HEADS * V_HEAD_DIM
ROPE_THETA = 10000.0
SGU_GROUPS = 8
SGU_GROUP_DIM = 64
SGU_WIDTH = SGU_GROUPS * SGU_GROUP_DIM
CHUNK = 128
RMS_EPS = 1e-6
LN_EPS = 1e-5
DN_ALPHA = 2.0 ** 0.25
IN_WIDTH = 4640
ATTN_SCALE = QK_HEAD_DIM ** -0.5

ADAM_LR = 0.001
ADAM_B1 = 0.9
ADAM_B2 = 0.999
ADAM_EPS = 1e-08
ADAM_WD = 0.01
ADAM_STEP = 10

LANES = 128
HEAD_PAD = 128
ROPE_LO = QK_NOPE_DIM
ROPE_MID = ROPE_LO + QK_ROPE_DIM // 2
ROPE_HI = ROPE_LO + QK_ROPE_DIM
CQ_PAD = 512

SEG_A = 2560
SEG_B = 1536
SEG_C = 768

PACK_R_ROWS = 272
VMEM_BIG = 56 * 1024 * 1024
VMEM_MID = 40 * 1024 * 1024


def _sigmoid(x):
    return 1.0 / (1.0 + jnp.exp(-x))


def _gelu_and_grad(x):
    c0 = math.sqrt(2.0 / math.pi)
    x2 = x * x
    t = jnp.tanh(c0 * (x + 0.044715 * x * x2))
    g = 0.5 * x * (1.0 + t)
    dg = 0.5 * (1.0 + t) + 0.5 * x * (1.0 - t * t) * (c0 * (1.0 + 3.0 * 0.044715 * x2))
    return g, dg


def _dot(a, b, dims):
    return lax.dot_general(a, b, (dims, ((), ())), preferred_element_type=F32)


_NN = ((1,), (0,))
_NT = ((1,), (1,))
_TN = ((0,), (0,))


def _store_grad(dh_ref, db_ref, col, val):
    cols = slice(col, col + val.shape[1])
    dh_ref[:, cols] = val.astype(BF16)
    db_ref[:, cols] += jnp.sum(val, axis=0, keepdims=True)


def _mm(a, b, *, tb=False, bias=None, add=None, out_dtype=F32, own=(), parts=(), tm, tn, name):
    m, k = a.shape
    n = b.shape[0] if tb else b.shape[1]
    assert m % tm == 0 and n % tn == 0 and not (own and parts)
    dims = _NT if tb else _NN
    nown = len(own) + len(parts)
    nm = m // tm
    nsteps = (n // tn) * nm

    def body(*refs):
        a_ref, b_ref = refs[0], refs[1]
        pos = 2
        r = _dot(a_ref[...], b_ref[...], dims)
        if bias is not None:
            r = r + refs[pos][...]; pos += 1
        if add is not None:
            r = r + refs[pos][...]; pos += 1
        own_refs = refs[pos:pos + nown]; pos += nown
        refs[pos][...] = r.astype(out_dtype)
        if nown:
            gat_refs = refs[pos + 1:pos + 1 + nown]
            send_sems, recv_sems, local_sems = refs[pos + 1 + nown:]
            step = pl.program_id(0) * nm + pl.program_id(1)
            if own:
                _gather_behind(own_refs, gat_refs, send_sems, recv_sems, local_sems, step, nsteps - 2, nsteps - 1)
            else:
                exchange = _exchange_parts(own_refs, gat_refs, send_sems, recv_sems, local_sems)
                _exchange_start(step == 0, exchange)
                _exchange_finish(step == nsteps - 1, exchange)

    b_spec = pl.BlockSpec((tn, k), lambda j, i: (j, 0)) if tb else pl.BlockSpec((k, tn), lambda j, i: (0, j))
    in_specs, args = [pl.BlockSpec((tm, k), lambda j, i: (i, 0)), b_spec], [a, b]
    if bias is not None:
        in_specs.append(pl.BlockSpec((1, tn), lambda j, i: (0, j))); args.append(bias)
    if add is not None:
        in_specs.append(pl.BlockSpec((tm, tn), lambda j, i: (i, j))); args.append(add)
    hbm = pl.BlockSpec(memory_space=pl.ANY)
    res = pl.pallas_call(
        body, name=name, grid=(n // tn, nm), in_specs=in_specs + [hbm] * nown,
        out_specs=[pl.BlockSpec((tm, tn), lambda j, i: (i, j))] + [hbm] * nown,
        out_shape=[jax.ShapeDtypeStruct((m, n), out_dtype)]
        + [jax.ShapeDtypeStruct((N_DEV,) + o.shape, o.dtype) for o in own]
        + [jax.ShapeDtypeStruct(p.shape, p.dtype) for p in parts],
        scratch_shapes=_exchange_sems(nown) if nown else [],
        compiler_params=pltpu.CompilerParams(dimension_semantics=("arbitrary", "arbitrary"), vmem_limit_bytes=VMEM_BIG),
    )(*args, *own, *parts)
    return (res[0], res[1:]) if nown else res[0]


def _rope(x, c, sa, sb):
    return x * c + pltpu.roll(x, LANES - 16, 1) * sa + pltpu.roll(x, 16, 1) * sb


def _rope_t(dy, c, sa, sb):
    return dy * c + pltpu.roll(dy * sa, 16, 1) + pltpu.roll(dy * sb, LANES - 16, 1)


def _mla_prep(h_c, gq, gkv, wq, wkn, wvx, c_t, sa_t, sb_t, own):
    tm = 256
    hw = MLA_HEADS * HEAD_PAD
    nown = len(own)
    nsteps = SEQ // tm

    def body(cq_ref, ckv_ref, kpe_ref, gq_ref, gkv_ref, wq_ref, wkn_ref, wvx_ref, c_ref, sa_ref, sb_ref, *rest):
        own_refs, (q_ref, k_ref, kt_ref, vx_ref, vxt_ref) = rest[:nown], rest[nown:nown + 5]
        gat_refs, (send_sems, recv_sems, local_sems) = rest[nown + 5:2 * nown + 5], rest[2 * nown + 5:]
        _gather_behind(own_refs, gat_refs, send_sems, recv_sems, local_sems, pl.program_id(0), nsteps - 2, nsteps - 1)
        c, sa, sb = c_ref[...], sa_ref[...], sb_ref[...]
        cq = cq_ref[...]
        rq = lax.rsqrt(jnp.sum(cq * cq, axis=1, keepdims=True) * (1.0 / Q_LORA_RANK) + RMS_EPS)
        cqn = ((cq * rq) * gq_ref[...]).astype(BF16)
        qall = _dot(cqn, wq_ref[...], _NN)
        for h in range(MLA_HEADS):
            sl = slice(HEAD_PAD * h, HEAD_PAD * (h + 1))
            q_ref[:, sl] = (_rope(qall[:, sl], c, sa, sb) * ATTN_SCALE).astype(BF16)
        ckv = ckv_ref[...]
        rkv = lax.rsqrt(jnp.sum(ckv * ckv, axis=1, keepdims=True) * (1.0 / KV_LORA_RANK) + RMS_EPS)
        ckvn = ((ckv * rkv) * gkv_ref[...]).astype(BF16)
        knall = _dot(ckvn, wkn_ref[...], _NN)
        vall = _dot(ckvn, wvx_ref[...], _NN)
        kper = _rope(kpe_ref[...], c, sa, sb)
        ones_half = (lax.broadcasted_iota(jnp.int32, (tm, HEAD_PAD), 1) >= V_HEAD_DIM).astype(F32)
        for h in range(MLA_HEADS):
            sl = slice(HEAD_PAD * h, HEAD_PAD * (h + 1))
            kh = knall[:, sl] + kper
            vh = vall[:, sl] + ones_half
            k_ref[:, sl] = kh.astype(BF16)
            kt_ref[sl, :] = kh.T.astype(BF16)
            vx_ref[:, sl] = vh.astype(BF16)
            vxt_ref[sl, :] = vh.T.astype(BF16)

    full = lambda shape: pl.BlockSpec(shape, lambda i: (0, 0))
    tab = pl.BlockSpec((tm, LANES), lambda i: (i, 0))
    row = pl.BlockSpec((tm, hw), lambda i: (i, 0))
    col = pl.BlockSpec((hw, tm), lambda i: (0, i))
    hbm = pl.BlockSpec(memory_space=pl.ANY)
    res = pl.pallas_call(
        body, name="mla_prep", grid=(nsteps,),
        in_specs=[pl.BlockSpec((tm, CQ_PAD), lambda i: (i, 0)),
                  pl.BlockSpec((tm, LANES), lambda i: (i, CQ_PAD // LANES)),
                  pl.BlockSpec((tm, LANES), lambda i: (i, CQ_PAD // LANES + 1)),
                  full((1, CQ_PAD)), full((1, KV_LORA_RANK)),
                  full((CQ_PAD, hw)), full((KV_LORA_RANK, hw)), full((KV_LORA_RANK, hw)), tab, tab, tab] + [hbm] * nown,
        out_specs=[row, row, col, row, col] + [hbm] * nown,
        out_shape=[jax.ShapeDtypeStruct((SEQ, hw), BF16), jax.ShapeDtypeStruct((SEQ, hw), BF16),
                   jax.ShapeDtypeStruct((hw, SEQ), BF16), jax.ShapeDtypeStruct((SEQ, hw), BF16),
                   jax.ShapeDtypeStruct((hw, SEQ), BF16)]
        + [jax.ShapeDtypeStruct((N_DEV,) + a.shape, a.dtype) for a in own],
        scratch_shapes=_exchange_sems(nown),
        compiler_params=pltpu.CompilerParams(dimension_semantics=("arbitrary",), vmem_limit_bytes=VMEM_MID),
    )(h_c, h_c, h_c, gq, gkv, wq, wkn, wvx, c_t, sa_t, sb_t, *own)
    return res[:5], res[5:]


ATT_T = 512
ATT_STRIP = 64


def _attn_fwd(q, kt, vx, own):
    t, rs = ATT_T, ATT_STRIP
    nown = len(own)
    nq = SEQ // t
    nsteps = (MLA_HEADS // 2) * nq

    def body(q_ref, kt_ref, vx_ref, *rest):
        own_refs, (o_ref, l_ref), gat_refs = rest[:nown], rest[nown:nown + 2], rest[nown + 2:2 * nown + 2]
        s_scr, p_scr, m_scr, a_scr, acc_scr, send_sems, recv_sems, local_sems = rest[2 * nown + 2:]
        qi = pl.program_id(1)
        _gather_behind(own_refs, gat_refs, send_sems, recv_sems, local_sems, pl.program_id(0) * nq + qi,
                       nsteps - 2, nsteps - 1)
        lane = lax.broadcasted_iota(jnp.int32, (t, LANES), 1)
        m_scr[...] = jnp.full((2, t, LANES), -1e30, F32)
        acc_scr[...] = jnp.zeros((2, t, LANES), F32)

        def block(j, masked):
            off = pl.multiple_of(j * t, t)
            for a in range(2):
                sl = slice(HEAD_PAD * a, HEAD_PAD * (a + 1))
                s_scr[a] = _dot(q_ref[:, sl], kt_ref[sl, pl.ds(off, t)], _NN)
                for r in range(t // rs):
                    rows = slice(rs * r, rs * (r + 1))
                    s = s_scr[a, rows, :]
                    if masked:
                        rowi = lax.broadcasted_iota(jnp.int32, (rs, t), 0) + rs * r
                        coli = lax.broadcasted_iota(jnp.int32, (rs, t), 1)
                        s = jnp.where(coli <= rowi, s, -1e30)
                    m_old = m_scr[a, rows, :]
                    m_new = jnp.maximum(m_old, jnp.max(s, axis=1, keepdims=True))
                    p_scr[a, rows, :] = jnp.exp(s - m_new[:, :1]).astype(BF16)
                    a_scr[a, rows, :] = jnp.exp(m_old - m_new)
                    m_scr[a, rows, :] = m_new
                acc_scr[a] = acc_scr[a] * a_scr[a] + _dot(p_scr[a], vx_ref[pl.ds(off, t), sl], _NN)

        def step(j, carry):
            block(j, False)
            return carry
        lax.fori_loop(0, qi, step, 0)
        block(qi, True)
        res = []
        for a in range(2):
            acc = acc_scr[a]
            l = acc[:, V_HEAD_DIM:V_HEAD_DIM + 1]
            res.append((acc / l, m_scr[a] + jnp.log(l)))
        o_ref[...] = jnp.where(lane < V_HEAD_DIM, res[0][0], pltpu.roll(res[1][0], V_HEAD_DIM, 1))
        l_ref[...] = jnp.where(lane < V_HEAD_DIM, res[0][1], res[1][1])

    hbm = pl.BlockSpec(memory_space=pl.ANY)
    res = pl.pallas_call(
        body, name="attn_fwd", grid=(MLA_HEADS // 2, nq),
        in_specs=[pl.BlockSpec((t, 2 * HEAD_PAD), lambda p, i: (i, p)),
                  pl.BlockSpec((2 * HEAD_PAD, SEQ), lambda p, i: (p, 0)),
                  pl.BlockSpec((SEQ, 2 * HEAD_PAD), lambda p, i: (0, p))] + [hbm] * nown,
        out_specs=[pl.BlockSpec((t, LANES), lambda p, i: (i, p)),
                   pl.BlockSpec((t, LANES), lambda p, i: (i, p))] + [hbm] * nown,
        out_shape=[jax.ShapeDtypeStruct((SEQ, MLA_WIDTH), F32), jax.ShapeDtypeStruct((SEQ, MLA_WIDTH), F32)]
        + [jax.ShapeDtypeStruct((N_DEV,) + a.shape, a.dtype) for a in own],
        scratch_shapes=[pltpu.VMEM((2, t, t), F32), pltpu.VMEM((2, t, t), BF16), pltpu.VMEM((2, t, LANES), F32),
                        pltpu.VMEM((2, t, LANES), F32), pltpu.VMEM((2, t, LANES), F32)] + _exchange_sems(nown),
        compiler_params=pltpu.CompilerParams(dimension_semantics=("arbitrary", "arbitrary"), vmem_limit_bytes=VMEM_MID),
    )(q, kt, vx, *own)
    return res[0], res[1], res[2:]


def _exchange_parts(parts, lands, send_sems, recv_sems, local_sems):
    x, y, c = _mesh_pos()
    me = 4 * x + 2 * y + c
    peers = [(x, y, 1 - c), (1 - x, y, c), (x, 1 - y, c), (1 - x, 1 - y, c),
             (1 - x, y, 1 - c), (x, 1 - y, 1 - c), (1 - x, 1 - y, 1 - c)]
    remote, local = [], []
    for a, (part, land) in enumerate(zip(parts, lands)):
        for k, peer in enumerate(peers):
            t = 4 * peer[0] + 2 * peer[1] + peer[2]
            remote.append(_remote(part.at[t], land.at[me], send_sems, recv_sems, 7 * a + k, peer))
        local.append(pltpu.make_async_copy(part.at[me], land.at[me], local_sems.at[a]))
    return remote, local


def _exchange_start(first_step, exchange):
    remote, local = exchange

    @pl.when(first_step)
    def _():
        for cp in remote + local:
            cp.start()


def _exchange_finish(last_step, exchange):
    remote, local = exchange

    @pl.when(last_step)
    def _():
        for cp in remote:
            cp.wait_recv()
        for cp in remote:
            cp.wait_send()
        for cp in local:
            cp.wait()


def _exchange_sems(npart):
    return [pltpu.SemaphoreType.DMA((7 * npart,)), pltpu.SemaphoreType.DMA((7 * npart,)),
            pltpu.SemaphoreType.DMA((npart,))]


def _attn_bwd(q, kt, k, vxt, d_o, o, lse, parts):
    t, rs = ATT_T, ATT_STRIP
    nq = SEQ // t
    npart = len(parts)
    nsteps = MLA_HEADS // 2

    def body(q_ref, kt_ref, k_ref, vxt_ref, do_ref, o_ref, l_ref, *rest):
        part_refs, rest = rest[:npart], rest[npart:]
        dq_ref, dk_ref, dv_ref = rest[:3]
        land_refs, rest = rest[3:3 + npart], rest[3 + npart:]
        s_scr, dp_scr, p_scr, ds_scr, st_scr, send_sems, recv_sems, local_sems = rest
        exchange = _exchange_parts(part_refs, land_refs, send_sems, recv_sems, local_sems)
        _exchange_start(pl.program_id(0) == 0, exchange)
        dk_ref[...] = jnp.zeros_like(dk_ref)
        dv_ref[...] = jnp.zeros_like(dv_ref)
        lane = lax.broadcasted_iota(jnp.int32, (t, LANES), 1)

        def qtile(i, carry):
            ioff = pl.multiple_of(i * t, t)
            do_i = do_ref[pl.ds(ioff, t), :]
            o_i = o_ref[pl.ds(ioff, t), :]
            l_i = l_ref[pl.ds(ioff, t), :]
            for a in range(2):
                sl = slice(HEAD_PAD * a, HEAD_PAD * (a + 1))
                sel = (lane < V_HEAD_DIM) if a == 0 else (lane >= V_HEAD_DIM)
                doa = jnp.where(sel, do_i, 0.0)
                oa = o_i
                if a == 1:
                    doa = pltpu.roll(doa, V_HEAD_DIM, 1)
                    oa = pltpu.roll(o_i, V_HEAD_DIM, 1)
                st_scr[0] = jnp.broadcast_to(jnp.sum(doa * oa, axis=1, keepdims=True), (t, LANES))
                st_scr[1] = jnp.broadcast_to(l_i[:, V_HEAD_DIM * a:V_HEAD_DIM * a + 1], (t, LANES))
                doa_bf = doa.astype(BF16)
                qa = q_ref[pl.ds(ioff, t), sl]

                def block(j, masked, dq_acc, sl=sl, qa=qa, doa_bf=doa_bf):
                    joff = pl.multiple_of(j * t, t)
                    s_scr[...] = _dot(qa, kt_ref[sl, pl.ds(joff, t)], _NN)
                    dp_scr[...] = _dot(doa_bf, vxt_ref[sl, pl.ds(joff, t)], _NN)
                    for r in range(t // rs):
                        rows = slice(rs * r, rs * (r + 1))
                        p = jnp.exp(s_scr[rows, :] - st_scr[1, rows, :1])
                        if masked:
                            rowi = lax.broadcasted_iota(jnp.int32, (rs, t), 0) + rs * r
                            coli = lax.broadcasted_iota(jnp.int32, (rs, t), 1)
                            p = jnp.where(coli <= rowi, p, 0.0)
                        p_scr[rows, :] = p.astype(BF16)
                        ds_scr[rows, :] = (p * (dp_scr[rows, :] - st_scr[0, rows, :1])).astype(BF16)
                    dk_ref[pl.ds(joff, t), sl] += _dot(ds_scr[...], qa, _TN)
                    dv_ref[pl.ds(joff, t), sl] += _dot(p_scr[...], doa_bf, _TN)
                    return dq_acc + _dot(ds_scr[...], k_ref[pl.ds(joff, t), sl], _NN)

                dq_acc = lax.fori_loop(0, i, lambda j, acc: block(j, False, acc), jnp.zeros((t, HEAD_PAD), F32))
                dq_ref[pl.ds(ioff, t), sl] = block(i, True, dq_acc)
            return carry

        lax.fori_loop(0, nq, qtile, 0)
        _exchange_finish(pl.program_id(0) == nsteps - 1, exchange)

    hw = MLA_HEADS * HEAD_PAD
    wide = pl.BlockSpec((SEQ, 2 * HEAD_PAD), lambda p: (0, p))
    wide_t = pl.BlockSpec((2 * HEAD_PAD, SEQ), lambda p: (p, 0))
    narrow = pl.BlockSpec((SEQ, LANES), lambda p: (0, p))
    hbm = pl.BlockSpec(memory_space=pl.ANY)
    res = pl.pallas_call(
        body, name="attn_bwd", grid=(nsteps,),
        in_specs=[wide, wide_t, wide, wide_t, narrow, narrow, narrow] + [hbm] * npart,
        out_specs=[wide, wide, wide] + [hbm] * npart,
        out_shape=[jax.ShapeDtypeStruct((SEQ, hw), F32)] * 3 + [jax.ShapeDtypeStruct(p.shape, p.dtype) for p in parts],
        scratch_shapes=[pltpu.VMEM((t, t), F32), pltpu.VMEM((t, t), F32), pltpu.VMEM((t, t), BF16),
                        pltpu.VMEM((t, t), BF16), pltpu.VMEM((2, t, LANES), F32)] + _exchange_sems(npart),
        compiler_params=pltpu.CompilerParams(dimension_semantics=("arbitrary",), vmem_limit_bytes=VMEM_BIG),
    )(q, kt, k, vxt, d_o, o, lse, *parts)
    return res[0], res[1], res[2], res[3:]


def _sgu_math(u, v, zb, lg, lb, ws_ref, bias):
    ug, dug = _gelu_and_grad(u)
    vg, dvg = _gelu_and_grad(v)
    mu = jnp.mean(vg, axis=1, keepdims=True)
    xc = vg - mu
    rstd = lax.rsqrt(jnp.mean(xc * xc, axis=1, keepdims=True) + LN_EPS)
    xh = xc * rstd
    vn_bf = (xh * lg + lb).astype(BF16)
    grp = lax.broadcasted_iota(jnp.int32, (CHUNK, SGU_WIDTH), 1) // SGU_GROUP_DIM
    r_i = lax.broadcasted_iota(jnp.int32, (CHUNK, CHUNK), 0)
    c_i = lax.broadcasted_iota(jnp.int32, (CHUNK, CHUNK), 1)
    tri, tri_t = r_i >= c_i, r_i <= c_i
    mixed = bias
    for g in range(SGU_GROUPS):
        wt = jnp.where(tri, ws_ref[g], 0.0).astype(BF16)
        mixed = mixed + jnp.where(grp == g, _dot(wt, vn_bf, _NN), 0.0)
    sb = _sigmoid(zb)
    return ug, dug, dvg, rstd, xh, vn_bf, grp, tri, tri_t, mixed, sb


def _sgu_fwd(h_b, lg, lb, w_s, bias_full):
    def body(u_ref, v_ref, zb_ref, lg_ref, lb_ref, ws_ref, bias_ref, yb_ref):
        zb = zb_ref[...]
        ug, _, _, _, _, _, _, _, _, mixed, sb = _sgu_math(u_ref[...], v_ref[...], zb, lg_ref[...], lb_ref[...],
                                                       ws_ref, bias_ref[...])
        yb_ref[...] = (ug * mixed) * (zb * sb)

    blk = lambda c: pl.BlockSpec((CHUNK, SGU_WIDTH), lambda i, c=c: (i, c))
    full2 = lambda shape: pl.BlockSpec(shape, lambda i: (0, 0))
    return pl.pallas_call(
        body, name="sgu_fwd", grid=(SEQ // CHUNK,),
        in_specs=[blk(0), blk(1), blk(2), full2((1, SGU_WIDTH)), full2((1, SGU_WIDTH)),
                  pl.BlockSpec((SGU_GROUPS, CHUNK, CHUNK), lambda i: (0, 0, 0)), full2((CHUNK, SGU_WIDTH))],
        out_specs=pl.BlockSpec((CHUNK, SGU_WIDTH), lambda i: (i, 0)),
        out_shape=jax.ShapeDtypeStruct((SEQ, SGU_WIDTH), F32),
        compiler_params=pltpu.CompilerParams(dimension_semantics=("arbitrary",)),
    )(h_b, h_b, h_b, lg, lb, w_s, bias_full)


def _sgu_bwd(h_b, d_yb, lg, lb, w_s, w_st, bias_full, parts):
    nsteps = SEQ // CHUNK
    npart = len(parts)

    def body(u_ref, v_ref, zb_ref, dyb_ref, lg_ref, lb_ref, ws_ref, wst_ref, bias_ref, *rest):
        part_refs, rest = rest[:npart], rest[npart:]
        dhb_ref, dws_ref, dbs_ref, dlg_ref, dlb_ref, dbb_ref = rest[:6]
        land_refs, (dbias_acc, send_sems, recv_sems, local_sems) = rest[6:6 + npart], rest[6 + npart:]
        step = pl.program_id(0)
        exchange = _exchange_parts(part_refs, land_refs, send_sems, recv_sems, local_sems)
        _exchange_start(step == 0, exchange)

        @pl.when(step == 0)
        def _():
            dbb_ref[...] = jnp.zeros_like(dbb_ref)
            dws_ref[...] = jnp.zeros_like(dws_ref)
            dlg_ref[...] = jnp.zeros_like(dlg_ref)
            dlb_ref[...] = jnp.zeros_like(dlb_ref)
            dbias_acc[...] = jnp.zeros_like(dbias_acc)

        zb = zb_ref[...]
        lg = lg_ref[...]
        ug, dug, dvg, rstd, xh, vn_bf, grp, tri, tri_t, mixed, sb = _sgu_math(
            u_ref[...], v_ref[...], zb, lg, lb_ref[...], ws_ref, bias_ref[...])
        dyb = dyb_ref[...]
        dsgu = dyb * (zb * sb)
        dzb = dyb * (ug * mixed) * (sb * (1.0 + zb * (1.0 - sb)))
        du = dsgu * mixed * dug
        dmixed = dsgu * ug
        dbias_acc[...] += dmixed
        dvn = jnp.zeros((CHUNK, SGU_WIDTH), F32)
        for g in range(SGU_GROUPS):
            dm_g = jnp.where(grp == g, dmixed, 0.0).astype(BF16)
            wtt = jnp.where(tri_t, wst_ref[g], 0.0).astype(BF16)
            dvn = dvn + _dot(wtt, dm_g, _NN)
            dws_ref[g] += jnp.where(tri, _dot(dm_g, vn_bf, _NT), 0.0)
        dlg_ref[...] += jnp.sum(dvn * xh, axis=0, keepdims=True)
        dlb_ref[...] += jnp.sum(dvn, axis=0, keepdims=True)
        dxh = dvn * lg
        dvgel = rstd * (dxh - jnp.mean(dxh, axis=1, keepdims=True) - xh * jnp.mean(dxh * xh, axis=1, keepdims=True))
        _store_grad(dhb_ref, dbb_ref, 0, du)
        _store_grad(dhb_ref, dbb_ref, SGU_WIDTH, dvgel * dvg)
        _store_grad(dhb_ref, dbb_ref, 2 * SGU_WIDTH, dzb)

        @pl.when(step == nsteps - 1)
        def _():
            acc = dbias_acc[...]
            lane = lax.broadcasted_iota(jnp.int32, (CHUNK, LANES), 1)
            out = jnp.zeros((CHUNK, LANES), F32)
            for g in range(SGU_GROUPS):
                sg = jnp.sum(jnp.where(grp == g, acc, 0.0), axis=1, keepdims=True)
                out = jnp.where(lane == g, sg, out)
            dbs_ref[...] = out

        _exchange_finish(step == nsteps - 1, exchange)

    blk = lambda c: pl.BlockSpec((CHUNK, SGU_WIDTH), lambda i, c=c: (i, c))
    full2 = lambda shape: pl.BlockSpec(shape, lambda i: (0, 0))
    full3 = pl.BlockSpec((SGU_GROUPS, CHUNK, CHUNK), lambda i: (0, 0, 0))
    hbm = pl.BlockSpec(memory_space=pl.ANY)
    res = pl.pallas_call(
        body, name="sgu_bwd", grid=(nsteps,),
        in_specs=[blk(0), blk(1), blk(2), pl.BlockSpec((CHUNK, SGU_WIDTH), lambda i: (i, 0)),
                  full2((1, SGU_WIDTH)), full2((1, SGU_WIDTH)), full3, full3, full2((CHUNK, SGU_WIDTH))] + [hbm] * npart,
        out_specs=[pl.BlockSpec((CHUNK, SEG_B), lambda i: (i, 0)), full3, full2((CHUNK, LANES)),
                   full2((1, SGU_WIDTH)), full2((1, SGU_WIDTH)), full2((1, SEG_B))] + [hbm] * npart,
        out_shape=[jax.ShapeDtypeStruct((SEQ, SEG_B), BF16),
                   jax.ShapeDtypeStruct((SGU_GROUPS, CHUNK, CHUNK), F32),
                   jax.ShapeDtypeStruct((CHUNK, LANES), F32),
                   jax.ShapeDtypeStruct((1, SGU_WIDTH), F32), jax.ShapeDtypeStruct((1, SGU_WIDTH), F32),
                   jax.ShapeDtypeStruct((1, SEG_B), F32)] + [jax.ShapeDtypeStruct(p.shape, p.dtype) for p in parts],
        scratch_shapes=[pltpu.VMEM((CHUNK, SGU_WIDTH), F32)] + _exchange_sems(npart),
        compiler_params=pltpu.CompilerParams(dimension_semantics=("arbitrary",)),
    )(h_b, h_b, h_b, d_yb, lg, lb, w_s, w_st, bias_full, *parts)
    return res[:6], res[6:]


def _merge(x, o, h_a, y_b, target, w_oa, w_ob, w_out, ln_g, ln_b):
    tm = 256
    nsteps = SEQ // tm

    def body(x_ref, o_ref, ga_ref, gb_ref, za_ref, yb_ref, tgt_ref, woa_ref, wob_ref, wout_ref, lng_ref, lnb_ref,
             loss_ref, dxr_ref, dha_ref, do_ref, dyb_ref, poa_ref, pob_ref, pout_ref, dlng_ref, dlnb_ref, dba_ref,
             dwoa_ref, dwob_ref, dwout_ref):
        step = pl.program_id(0)

        @pl.when(step == 0)
        def _():
            for r in (loss_ref, dwoa_ref, dwob_ref, dwout_ref, dlng_ref, dlnb_ref, dba_ref):
                r[...] = jnp.zeros_like(r)

        o = o_ref[...]
        za = za_ref[...]
        sa = _sigmoid(za)
        ya_bf = (o * (za * sa)).astype(BF16)
        yb_bf = yb_ref[...].astype(BF16)
        woa, wob, wout = woa_ref[...], wob_ref[...], wout_ref[...]
        pa = _dot(ya_bf, woa, _NN)
        pb = _dot(yb_bf, wob, _NN)
        sga = _sigmoid(ga_ref[...])
        sgb = _sigmoid(gb_ref[...])
        merged_bf = (sga * pa + sgb * pb).astype(BF16)
        r = DN_ALPHA * x_ref[...] + _dot(merged_bf, wout, _NN)
        mu = jnp.mean(r, axis=1, keepdims=True)
        rc = r - mu
        rstd = lax.rsqrt(jnp.mean(rc * rc, axis=1, keepdims=True) + LN_EPS)
        xh = rc * rstd
        lng = lng_ref[...]
        y = xh * lng + lnb_ref[...]
        e = y - tgt_ref[...]
        loss_ref[...] += 0.5 * jnp.sum(jnp.sum(e * e, axis=1, keepdims=True) * (1.0 / D_MODEL), axis=0, keepdims=True)

        dy = e * (1.0 / D_MODEL)
        dlng_ref[...] += jnp.sum(dy * xh, axis=0, keepdims=True)
        dlnb_ref[...] += jnp.sum(dy, axis=0, keepdims=True)
        dxh = dy * lng
        dr = rstd * (dxh - jnp.mean(dxh, axis=1, keepdims=True) - xh * jnp.mean(dxh * xh, axis=1, keepdims=True))
        dxr_ref[...] = DN_ALPHA * dr
        dr_bf = dr.astype(BF16)
        dwout_ref[...] += _dot(merged_bf, dr_bf, _TN)
        dmerged = _dot(dr_bf, wout, _NT)
        dpa_bf = (dmerged * sga).astype(BF16)
        dpb_bf = (dmerged * sgb).astype(BF16)
        _store_grad(dha_ref, dba_ref, 0, dmerged * pa * (sga * (1.0 - sga)))
        _store_grad(dha_ref, dba_ref, D_MODEL, dmerged * pb * (sgb * (1.0 - sgb)))
        dwoa_ref[...] += _dot(ya_bf, dpa_bf, _TN)
        dwob_ref[...] += _dot(yb_bf, dpb_bf, _TN)
        dya = _dot(dpa_bf, woa, _NT)
        dyb_ref[...] = _dot(dpb_bf, wob, _NT)
        do_ref[...] = dya * (za * sa)
        _store_grad(dha_ref, dba_ref, 2 * D_MODEL, dya * o * (sa * (1.0 + za * (1.0 - sa))))

        @pl.when(step == nsteps - 1)
        def _():
            cols = D_MODEL // N_DEV
            for j in range(N_DEV):
                poa_ref[j] = dwoa_ref[:, cols * j:cols * (j + 1)].astype(BF16)
                pob_ref[j] = dwob_ref[:, cols * j:cols * (j + 1)].astype(BF16)
                pout_ref[j] = dwout_ref[cols * j:cols * (j + 1), :].astype(BF16)

    row = lambda w, c=0: pl.BlockSpec((tm, w), lambda i, c=c: (i, c))
    full = lambda shape: pl.BlockSpec(shape, lambda i: (0, 0))
    full3 = lambda shape: pl.BlockSpec(shape, lambda i: (0, 0, 0))
    return pl.pallas_call(
        body, name="merge", grid=(nsteps,),
        in_specs=[row(D_MODEL), row(MLA_WIDTH), row(D_MODEL, 0), row(D_MODEL, 1), row(MLA_WIDTH, 4), row(SGU_WIDTH),
                  row(D_MODEL), full((MLA_WIDTH, D_MODEL)), full((SGU_WIDTH, D_MODEL)), full((D_MODEL, D_MODEL)),
                  full((1, D_MODEL)), full((1, D_MODEL))],
        out_specs=[full((1, LANES)), row(D_MODEL), row(SEG_A), row(MLA_WIDTH), row(SGU_WIDTH),
                   full3((N_DEV, MLA_WIDTH, D_MODEL // N_DEV)), full3((N_DEV, SGU_WIDTH, D_MODEL // N_DEV)),
                   full3((N_DEV, D_MODEL // N_DEV, D_MODEL)), full((1, D_MODEL)), full((1, D_MODEL)), full((1, SEG_A))],
        out_shape=[jax.ShapeDtypeStruct((1, LANES), F32),
                   jax.ShapeDtypeStruct((SEQ, D_MODEL), F32), jax.ShapeDtypeStruct((SEQ, SEG_A), BF16),
                   jax.ShapeDtypeStruct((SEQ, MLA_WIDTH), F32), jax.ShapeDtypeStruct((SEQ, SGU_WIDTH), F32),
                   jax.ShapeDtypeStruct((N_DEV, MLA_WIDTH, D_MODEL // N_DEV), BF16),
                   jax.ShapeDtypeStruct((N_DEV, SGU_WIDTH, D_MODEL // N_DEV), BF16),
                   jax.ShapeDtypeStruct((N_DEV, D_MODEL // N_DEV, D_MODEL), BF16),
                   jax.ShapeDtypeStruct((1, D_MODEL), F32), jax.ShapeDtypeStruct((1, D_MODEL), F32),
                   jax.ShapeDtypeStruct((1, SEG_A), F32)],
        scratch_shapes=[pltpu.VMEM((MLA_WIDTH, D_MODEL), F32), pltpu.VMEM((SGU_WIDTH, D_MODEL), F32),
                        pltpu.VMEM((D_MODEL, D_MODEL), F32)],
        compiler_params=pltpu.CompilerParams(dimension_semantics=("arbitrary",), vmem_limit_bytes=VMEM_BIG),
    )(x, o, h_a, h_a, h_a, y_b, target, w_oa, w_ob, w_out, ln_g, ln_b)


def _mla_bwd(dq, dk, dv, h_c, gq, gkv, wq, wkn, wv, c_t, sa_t, sb_t):
    tm = 256
    hw = MLA_HEADS * HEAD_PAD

    def body(dq_ref, dk_ref, dv_ref, cq_ref, ckv_ref, gq_ref, gkv_ref, wq_ref, wkn_ref, wv_ref, c_ref, sa_ref, sb_ref,
             dhc_ref, puq_ref, dwkn_ref, dwv_ref, dgq_ref, dgkv_ref, dbc_ref, pre_ref, dwq_ref):
        @pl.when(pl.program_id(0) == 0)
        def _():
            for r in (dwq_ref, dwkn_ref, dwv_ref, dgq_ref, dgkv_ref, dbc_ref):
                r[...] = jnp.zeros_like(r)

        c, sa, sb = c_ref[...], sa_ref[...], sb_ref[...]
        lane = lax.broadcasted_iota(jnp.int32, (tm, LANES), 1)
        rope_lanes = jnp.logical_and(lane >= ROPE_LO, lane < ROPE_HI)

        cq = cq_ref[...]
        gq = gq_ref[...]
        rq = lax.rsqrt(jnp.sum(cq * cq, axis=1, keepdims=True) * (1.0 / Q_LORA_RANK) + RMS_EPS)
        nq = cq * rq
        cqn_bf = (nq * gq).astype(BF16)
        for h in range(MLA_HEADS):
            sl = slice(HEAD_PAD * h, HEAD_PAD * (h + 1))
            pre_ref[:, sl] = _rope_t(dq_ref[:, sl] * ATTN_SCALE, c, sa, sb).astype(BF16)
        dqpre_bf = pre_ref[...]
        dcqn = _dot(dqpre_bf, wq_ref[...], _NT)
        dwq_ref[...] += _dot(cqn_bf, dqpre_bf, _TN)
        dgq_ref[...] += jnp.sum(dcqn * nq, axis=0, keepdims=True)
        dnq = dcqn * gq
        _store_grad(dhc_ref, dbc_ref, 0,
                    rq * (dnq - nq * (jnp.sum(dnq * nq, axis=1, keepdims=True) * (1.0 / Q_LORA_RANK))))

        ckv = ckv_ref[...]
        gkv = gkv_ref[...]
        rkv = lax.rsqrt(jnp.sum(ckv * ckv, axis=1, keepdims=True) * (1.0 / KV_LORA_RANK) + RMS_EPS)
        nkv = ckv * rkv
        ckvn_bf = (nkv * gkv).astype(BF16)
        dk = dk_ref[...]
        dk_bf = dk.astype(BF16)
        dv_bf = dv_ref[...].astype(BF16)
        dckvn = _dot(dk_bf, wkn_ref[...], _NT) + _dot(dv_bf, wv_ref[...], _NT)
        dwkn_ref[...] += _dot(ckvn_bf, dk_bf, _TN)
        dwv_ref[...] += _dot(ckvn_bf, dv_bf, _TN)
        dgkv_ref[...] += jnp.sum(dckvn * nkv, axis=0, keepdims=True)
        dnkv = dckvn * gkv
        _store_grad(dhc_ref, dbc_ref, CQ_PAD, rkv * (
            dnkv - nkv * (jnp.sum(dnkv * nkv, axis=1, keepdims=True) * (1.0 / KV_LORA_RANK))))
        dkpe = jnp.zeros((tm, LANES), F32)
        for h in range(MLA_HEADS):
            dkpe = dkpe + dk[:, HEAD_PAD * h:HEAD_PAD * (h + 1)]
        _store_grad(dhc_ref, dbc_ref, CQ_PAD + LANES, _rope_t(jnp.where(rope_lanes, dkpe, 0.0), c, sa, sb))

        @pl.when(pl.program_id(0) == SEQ // tm - 1)
        def _():
            rows = Q_LORA_RANK // N_DEV
            for j in range(N_DEV):
                for h in range(MLA_HEADS):
                    puq_ref[j, :, QK_HEAD_DIM * h:QK_HEAD_DIM * (h + 1)] = dwq_ref[
                        rows * j:rows * (j + 1), HEAD_PAD * h:HEAD_PAD * h + QK_HEAD_DIM].astype(BF16)

    full = lambda shape: pl.BlockSpec(shape, lambda i: (0, 0))
    row = lambda w, c=0: pl.BlockSpec((tm, w), lambda i, c=c: (i, c))
    return pl.pallas_call(
        body, name="mla_bwd", grid=(SEQ // tm,),
        in_specs=[row(hw), row(hw), row(hw), row(CQ_PAD, 0), row(LANES, CQ_PAD // LANES),
                  full((1, CQ_PAD)), full((1, KV_LORA_RANK)), full((CQ_PAD, hw)), full((KV_LORA_RANK, hw)),
                  full((KV_LORA_RANK, hw)), row(LANES), row(LANES), row(LANES)],
        out_specs=[row(SEG_C), pl.BlockSpec((N_DEV, Q_LORA_RANK // N_DEV, MLA_HEADS * QK_HEAD_DIM), lambda i: (0, 0, 0)),
                   full((KV_LORA_RANK, hw)), full((KV_LORA_RANK, hw)),
                   full((1, CQ_PAD)), full((1, KV_LORA_RANK)), full((1, SEG_C))],
        out_shape=[jax.ShapeDtypeStruct((SEQ, SEG_C), BF16),
                   jax.ShapeDtypeStruct((N_DEV, Q_LORA_RANK // N_DEV, MLA_HEADS * QK_HEAD_DIM), BF16),
                   jax.ShapeDtypeStruct((KV_LORA_RANK, hw), F32), jax.ShapeDtypeStruct((KV_LORA_RANK, hw), F32),
                   jax.ShapeDtypeStruct((1, CQ_PAD), F32), jax.ShapeDtypeStruct((1, KV_LORA_RANK), F32),
                   jax.ShapeDtypeStruct((1, SEG_C), F32)],
        scratch_shapes=[pltpu.VMEM((tm, hw), BF16), pltpu.VMEM((CQ_PAD, hw), F32)],
        compiler_params=pltpu.CompilerParams(dimension_semantics=("arbitrary",), vmem_limit_bytes=VMEM_MID),
    )(dq, dk, dv, h_c, h_c, gq, gkv, wq, wkn, wv, c_t, sa_t, sb_t)


def _adamw_all(ws, gs, ms, vs):
    n = len(ws)
    c1 = 1.0 / (1.0 - ADAM_B1 ** ADAM_STEP)
    c2 = 1.0 / (1.0 - ADAM_B2 ** ADAM_STEP)

    def body(*refs):
        for idx in range(n):
            w, g, m, v = (refs[idx][...], refs[n + idx][...], refs[2 * n + idx][...], refs[3 * n + idx][...])
            m_new = ADAM_B1 * m + (1.0 - ADAM_B1) * g
            v_new = ADAM_B2 * v + (1.0 - ADAM_B2) * (g * g)
            delta = -ADAM_LR * ((m_new * c1) / (jnp.sqrt(v_new * c2) + ADAM_EPS) + ADAM_WD * w)
            refs[4 * n + idx][...] = delta
            refs[5 * n + idx][...] = m_new
            refs[6 * n + idx][...] = v_new

    shapes = [jax.ShapeDtypeStruct(w.shape, F32) for w in ws]
    outs = pl.pallas_call(
        body, name="adamw", out_shape=shapes * 3,
        compiler_params=pltpu.CompilerParams(vmem_limit_bytes=VMEM_BIG),
    )(*ws, *gs, *ms, *vs)
    return outs[:n], outs[n:2 * n], outs[2 * n:]


SHARD_W = IN_WIDTH // N_DEV
W_IN_LO = 128

_PIECES = [(0, 384, 2, 0), (384, 512, 2, CQ_PAD), (512, 544, 2, CQ_PAD + LANES + ROPE_LO),
           (544, 1056, 0, 2 * D_MODEL), (1056, 1568, 1, 0), (1568, 2080, 1, SGU_WIDTH),
           (2080, 2592, 1, 2 * SGU_WIDTH), (2592, 3616, 0, 0), (3616, 4640, 0, D_MODEL)]


def _column_runs():
    runs = []
    for n0, n1, seg, d0 in _PIECES:
        for j in range(N_DEV):
            lo, hi = max(n0, j * SHARD_W), min(n1, (j + 1) * SHARD_W)
            if lo < hi:
                runs.append((j, lo - j * SHARD_W, hi - j * SHARD_W, seg, d0 + lo - n0))
    return runs


def _mesh_pos():
    return lax.axis_index("x"), lax.axis_index("y"), lax.axis_index("c")


def _remote(src, dst, send_sems, recv_sems, k, to):
    return pltpu.make_async_remote_copy(src_ref=src, dst_ref=dst, send_sem=send_sems.at[k], recv_sem=recv_sems.at[k],
                                        device_id=to, device_id_type=pl.DeviceIdType.MESH)


def _gather_exchange(gats, send_sems, recv_sems, meanwhile=None):
    x, y, c = _mesh_pos()
    me, sibling = (x, y, c), (x, y, 1 - c)
    chips = [(1 - x, y), (x, 1 - y), (1 - x, 1 - y)]

    def copy(a, k, blk, to):
        slab = gats[a].at[4 * blk[0] + 2 * blk[1] + blk[2]]
        return _remote(slab, slab, send_sems, recv_sems, 7 * a + k, to)

    arrays = range(len(gats))
    first = [copy(a, 1 + j, me, (*chip, c)) for j, chip in enumerate(chips) for a in arrays]
    first += [copy(a, 0, me, sibling) for a in arrays]
    for cp in first:
        cp.start()
    if meanwhile is not None:
        meanwhile()
    passed = []
    for j, chip in enumerate(chips):
        for a in arrays:
            copy(a, 1 + j, (*chip, c), me).wait_recv()
            fwd = copy(a, 4 + j, (*chip, c), sibling)
            fwd.start()
            passed.append(fwd)
    for a in arrays:
        copy(a, 0, sibling, me).wait_recv()
    for j, chip in enumerate(chips):
        for a in arrays:
            copy(a, 4 + j, (*chip, 1 - c), me).wait_recv()
    for cp in first + passed:
        cp.wait_send()


def _gather_behind(own, gats, send_sems, recv_sems, local_sems, step, mid, last):
    x, y, c = _mesh_pos()
    me, sibling = (x, y, c), (x, y, 1 - c)
    chips = [(1 - x, y), (x, 1 - y), (1 - x, 1 - y)]
    arrays = range(len(gats))

    def copy(a, k, blk, to, src=None):
        slab = gats[a].at[4 * blk[0] + 2 * blk[1] + blk[2]]
        return _remote(slab if src is None else src, slab, send_sems, recv_sems, 7 * a + k, to)

    first = [copy(a, 1 + j, me, (*chip, c), src=own[a]) for j, chip in enumerate(chips) for a in arrays]
    first += [copy(a, 0, me, sibling, src=own[a]) for a in arrays]
    local = [pltpu.make_async_copy(own[a], gats[a].at[4 * x + 2 * y + c], local_sems.at[a]) for a in arrays]
    passed = [copy(a, 4 + j, (*chip, c), sibling) for j, chip in enumerate(chips) for a in arrays]

    @pl.when(step == 0)
    def _():
        for cp in first + local:
            cp.start()

    @pl.when(step == mid)
    def _():
        for j, chip in enumerate(chips):
            for a in arrays:
                copy(a, 1 + j, (*chip, c), me).wait_recv()
        for cp in passed:
            cp.start()

    @pl.when(step == last)
    def _():
        for a in arrays:
            copy(a, 0, sibling, me).wait_recv()
        for j, chip in enumerate(chips):
            for a in arrays:
                copy(a, 4 + j, (*chip, 1 - c), me).wait_recv()
        for cp in first + passed:
            cp.wait_send()
        for cp in local:
            cp.wait()


def _gather_first(w_in, w_uq2, w_oa, w_ob, w_out, x2, pos_col, invf_lane):
    hw = MLA_HEADS * HEAD_PAD
    uq_rows = Q_LORA_RANK // N_DEV
    rows = 256

    def body(win_ref, wuq_ref, woa_ref, wob_ref, wout_ref, x_ref, pos_ref, invf_ref,
             wc_ref, wq_ref, winlo_ref, winhi_ref, oab_ref, obb_ref, outb_ref, xb_ref, xt_ref, c_ref, sa_ref, sb_ref,
             g_uq, blk0, send_sems, recv_sems):
        def local_work():
            for i in range(SEQ // rows):
                xi = x_ref[rows * i:rows * (i + 1), :]
                xb_ref[rows * i:rows * (i + 1), :] = xi.astype(BF16)
                xt_ref[:, rows * i:rows * (i + 1)] = xi.T.astype(BF16)
            ang = pos_ref[...].astype(F32) * invf_ref[...]
            cs, sn = jnp.cos(ang), jnp.sin(ang)
            lane = lax.broadcasted_iota(jnp.int32, ang.shape, 1)
            c_ref[...] = jnp.where(lane < ROPE_LO, 1.0, jnp.where(lane < ROPE_HI, cs, 0.0))
            sa_ref[...] = jnp.where(jnp.logical_and(lane >= ROPE_LO, lane < ROPE_MID), -sn, 0.0)
            sb_ref[...] = jnp.where(jnp.logical_and(lane >= ROPE_MID, lane < ROPE_HI), sn, 0.0)

        x, y, c = _mesh_pos()
        me = (x, y, c)
        winlo_ref[...] = win_ref[0, 0:W_IN_LO, :].astype(BF16)
        winhi_ref[...] = win_ref[0, W_IN_LO:D_MODEL, :].astype(BF16)
        oab_ref[...] = woa_ref[0].astype(BF16)
        obb_ref[...] = wob_ref[0].astype(BF16)
        outb_ref[...] = wout_ref[0].astype(BF16)
        g_uq[4 * x + 2 * y + c] = wuq_ref[...].astype(BF16)

        chip0 = jnp.logical_and(x == 0, y == 0)
        south = c == 0
        half = D_MODEL // 2
        halves = [blk0.at[pl.ds(0, half)], blk0.at[pl.ds(half, half)]]

        def bcopy(k, to, part=None):
            ref = blk0 if part is None else halves[part]
            return _remote(ref, ref, send_sems, recv_sems, 7 + k, to)

        sends0 = [(0, (0, 0, 1), None), (1, (1, 0, 0), 0), (2, (0, 1, 0), 1), (3, (1, 0, 0), 1), (4, (0, 1, 0), 0)]

        @pl.when(jnp.logical_and(chip0, south))
        def _():
            blk0[0:W_IN_LO, :] = winlo_ref[...]
            blk0[W_IN_LO:D_MODEL, :] = winhi_ref[...]
            for k, to, part in sends0:
                bcopy(k, to, part).start()

        _gather_exchange([g_uq], send_sems, recv_sems, meanwhile=local_work)

        for (cx, cy), first_k, first_half, second_k in (((1, 0), 1, 0, 3), ((0, 1), 2, 1, 4)):
            @pl.when(jnp.logical_and(jnp.logical_and(x == cx, y == cy), south))
            def _(cx=cx, cy=cy, first_k=first_k, first_half=first_half, second_k=second_k):
                bcopy(first_k, me, first_half).wait_recv()
                onward = bcopy(5 + first_half, (1, 1, 0), first_half)
                onward.start()
                bcopy(second_k, me, 1 - first_half).wait_recv()
                north = bcopy(7, (cx, cy, 1))
                north.start()
                onward.wait_send()
                north.wait_send()

        @pl.when(jnp.logical_and(jnp.logical_and(x == 1, y == 1), south))
        def _():
            bcopy(5, me, 0).wait_recv()
            bcopy(6, me, 1).wait_recv()
            north = bcopy(7, (1, 1, 1))
            north.start()
            north.wait_send()

        @pl.when(jnp.logical_and(chip0, c == 1))
        def _():
            bcopy(0, me).wait_recv()

        @pl.when(jnp.logical_and(jnp.logical_not(chip0), c == 1))
        def _():
            bcopy(7, me).wait_recv()

        @pl.when(jnp.logical_and(chip0, south))
        def _():
            for k, to, part in sends0:
                bcopy(k, to, part).wait_send()

        for j, s0, s1, seg, d0 in _column_runs():
            if seg == 2:
                wc_ref[:, d0:d0 + (s1 - s0)] = blk0[:, s0:s1]
        zeros = lambda r, w: jnp.zeros((r, w), BF16)
        wc_ref[:, Q_LORA_RANK:CQ_PAD] = zeros(D_MODEL, CQ_PAD - Q_LORA_RANK)
        wc_ref[:, CQ_PAD + LANES:CQ_PAD + LANES + ROPE_LO] = zeros(D_MODEL, ROPE_LO)
        wc_ref[:, CQ_PAD + LANES + ROPE_HI:SEG_C] = zeros(D_MODEL, LANES - ROPE_HI)
        wq_ref[Q_LORA_RANK:CQ_PAD, :] = zeros(CQ_PAD - Q_LORA_RANK, hw)
        for h in range(MLA_HEADS):
            wq_ref[0:Q_LORA_RANK, HEAD_PAD * h + QK_HEAD_DIM:HEAD_PAD * (h + 1)] = zeros(Q_LORA_RANK, HEAD_PAD - QK_HEAD_DIM)
        for j in range(N_DEV):
            for h in range(MLA_HEADS):
                wq_ref[uq_rows * j:uq_rows * (j + 1), HEAD_PAD * h:HEAD_PAD * h + QK_HEAD_DIM] = g_uq[
                    j, :, QK_HEAD_DIM * h:QK_HEAD_DIM * (h + 1)]

    vmem = pl.BlockSpec(memory_space=pltpu.VMEM)
    return pl.pallas_call(
        body, name="gather_first",
        out_shape=[jax.ShapeDtypeStruct((D_MODEL, SEG_C), BF16), jax.ShapeDtypeStruct((CQ_PAD, hw), BF16),
                   jax.ShapeDtypeStruct((W_IN_LO, SHARD_W), BF16), jax.ShapeDtypeStruct((D_MODEL - W_IN_LO, SHARD_W), BF16),
                   jax.ShapeDtypeStruct(w_oa.shape[1:], BF16),
                   jax.ShapeDtypeStruct(w_ob.shape[1:], BF16), jax.ShapeDtypeStruct(w_out.shape[1:], BF16),
                   jax.ShapeDtypeStruct((SEQ, D_MODEL), BF16), jax.ShapeDtypeStruct((D_MODEL, SEQ), BF16)]
        + [jax.ShapeDtypeStruct((SEQ, LANES), F32)] * 3,
        in_specs=[vmem] * 8, out_specs=[vmem] * 12,
        scratch_shapes=[pltpu.VMEM((N_DEV, uq_rows, MLA_HEADS * QK_HEAD_DIM), BF16), pltpu.VMEM((D_MODEL, SHARD_W), BF16),
                        pltpu.SemaphoreType.DMA((15,)), pltpu.SemaphoreType.DMA((15,))],
        compiler_params=pltpu.CompilerParams(vmem_limit_bytes=VMEM_BIG),
    )(w_in, w_uq2, w_oa, w_ob, w_out, x2, pos_col, invf_lane)


def _assemble_in(g_lo, g_hi):
    def body(glo_ref, ghi_ref, wa_ref, wb_ref):
        segs = [wa_ref, wb_ref]
        for j, s0, s1, seg, d0 in _column_runs():
            if seg < 2:
                segs[seg][0:W_IN_LO, d0:d0 + (s1 - s0)] = glo_ref[j, :, s0:s1]
                segs[seg][W_IN_LO:D_MODEL, d0:d0 + (s1 - s0)] = ghi_ref[j, :, s0:s1]

    return pl.pallas_call(
        body, name="assemble_in",
        out_shape=[jax.ShapeDtypeStruct((D_MODEL, SEG_A), BF16), jax.ShapeDtypeStruct((D_MODEL, SEG_B), BF16)],
        compiler_params=pltpu.CompilerParams(vmem_limit_bytes=VMEM_MID),
    )(g_lo, g_hi)


def _assemble_out(g_oa, g_ob, g_out):
    cols = D_MODEL // N_DEV

    def body(goa_ref, gob_ref, gout_ref, oa_ref, ob_ref, out_ref):
        for j in range(N_DEV):
            oa_ref[:, cols * j:cols * (j + 1)] = goa_ref[j]
            ob_ref[:, cols * j:cols * (j + 1)] = gob_ref[j]
            out_ref[cols * j:cols * (j + 1), :] = gout_ref[j]

    return pl.pallas_call(
        body, name="assemble_out",
        out_shape=[jax.ShapeDtypeStruct((MLA_WIDTH, D_MODEL), BF16), jax.ShapeDtypeStruct((SGU_WIDTH, D_MODEL), BF16),
                   jax.ShapeDtypeStruct((D_MODEL, D_MODEL), BF16)],
    )(g_oa, g_ob, g_out)


C_NAT = 544


def _to_parts(dwa, dwb):
    def body(dwa_ref, dwb_ref, pin_ref):
        pin_ref[0, :, 0:C_NAT] = jnp.zeros((D_MODEL, C_NAT), BF16)
        segs = [dwa_ref, dwb_ref]
        for j, s0, s1, seg, d0 in _column_runs():
            if seg < 2:
                pin_ref[j, :, s0:s1] = segs[seg][:, d0:d0 + (s1 - s0)]

    return pl.pallas_call(body, name="to_parts", out_shape=jax.ShapeDtypeStruct((N_DEV, D_MODEL, SHARD_W), BF16),
                          compiler_params=pltpu.CompilerParams(vmem_limit_bytes=VMEM_MID))(dwa, dwb)


def _dx_tail(dhs, ws, dx_res, dwc, p_uq, p_rep):
    tm = SEQ // 4
    rep_rows = p_rep.shape[1]
    c_rows = D_MODEL // N_DEV
    spec = [((c_rows, C_NAT), BF16), (p_uq.shape[1:], BF16), ((rep_rows, LANES), F32)]
    n = len(spec)

    nseg = len(dhs)

    def body(*refs):
        dh_refs, w_refs = refs[:nseg], refs[nseg:2 * nseg]
        dxr_ref, dwc_ref, puq_ref, prep_ref, dx_ref, call_ref, guq_ref, repall_ref, pc_ref, c_all, rep_all = refs[
            2 * nseg:2 * nseg + 11]
        rest = refs[2 * nseg + 11:]
        ras, tbs, rbs = rest[0:n], rest[n:2 * n], rest[2 * n:3 * n]
        send_sems, recv_sems, gsend, grecv = rest[3 * n:]
        step = pl.program_id(0)
        x, y, c = _mesh_pos()
        me_idx = 4 * x + 2 * y + c
        me, sibling = (x, y, c), (x, y, 1 - c)
        others = [(1 - x, y), (x, 1 - y), (1 - x, 1 - y)]
        parts = [pc_ref, puq_ref, prep_ref]
        gats = [rep_all, c_all]

        def stage1(chip, a):
            return _remote(parts[a].at[2 * chip + (1 - c)], ras[a].at[chip], send_sems, recv_sems, 7 * a + chip, sibling)

        def stage2(k, a):
            cx, cy = others[k]
            return _remote(tbs[a].at[k], rbs[a].at[k], send_sems, recv_sems, 7 * a + 4 + k, (cx, cy, c))

        def gcopy(a, k, blk, to):
            slab = gats[a].at[4 * blk[0] + 2 * blk[1] + blk[2]]
            return _remote(slab, slab, gsend, grecv, 7 * a + k, to)

        def chip_sum(a, chip):
            return parts[a][2 * chip + c].astype(F32) + ras[a][chip].astype(F32)

        @pl.when(step == 0)
        def _():
            for j, s0, s1, seg, d0 in _column_runs():
                if seg == 2:
                    for r in range(N_DEV):
                        pc_ref[r, :, s0:s1] = dwc_ref[c_rows * r:c_rows * (r + 1), d0:d0 + (s1 - s0)]
            for chip in range(4):
                for a in range(n):
                    stage1(chip, a).start()

        @pl.when(step == 1)
        def _():
            for chip in range(4):
                for a in range(n):
                    stage1(chip, a).wait_recv()
            for k, (cx, cy) in enumerate(others):
                for a in range(n):
                    tbs[a][k] = chip_sum(a, 2 * cx + cy).astype(spec[a][1])
                    stage2(k, a).start()

        @pl.when(step == 2)
        def _():
            for k in range(3):
                for a in range(n):
                    stage2(k, a).wait_recv()
            sums = []
            for a in range(n):
                acc = chip_sum(a, 2 * x + y)
                for k in range(3):
                    acc = acc + rbs[a][k].astype(F32)
                sums.append(acc)
            c_all[me_idx] = sums[0].astype(BF16)
            guq_ref[...] = sums[1]
            rep_all[me_idx] = sums[2]
            for a in range(2):
                for j, chip in enumerate(others):
                    gcopy(a, 1 + j, me, (*chip, c)).start()
                gcopy(a, 0, me, sibling).start()

        @pl.when(step == 3)
        def _():
            for j, chip in enumerate(others):
                for a in range(2):
                    gcopy(a, 1 + j, (*chip, c), me).wait_recv()
                    gcopy(a, 4 + j, (*chip, c), sibling).start()
            for a in range(2):
                gcopy(a, 0, sibling, me).wait_recv()
                for j, chip in enumerate(others):
                    gcopy(a, 4 + j, (*chip, 1 - c), me).wait_recv()
            for a in range(2):
                gcopy(a, 0, me, sibling).wait_send()
                for j, chip in enumerate(others):
                    gcopy(a, 1 + j, me, (*chip, c)).wait_send()
                    gcopy(a, 4 + j, (*chip, c), sibling).wait_send()
            for a in range(n):
                for chip in range(4):
                    stage1(chip, a).wait_send()
                for k in range(3):
                    stage2(k, a).wait_send()
            call_ref[...] = c_all[...]
            repall_ref[...] = rep_all[...]

        acc = dxr_ref[...]
        for dh_ref, w_ref in zip(dh_refs, w_refs):
            acc = acc + _dot(dh_ref[...], w_ref[...], _NT)
        dx_ref[...] = acc

    row = lambda w: pl.BlockSpec((tm, w), lambda i: (i, 0))
    full = lambda shape: pl.BlockSpec(shape, lambda i: (0,) * len(shape))
    scratch = [pltpu.VMEM((N_DEV, c_rows, C_NAT), BF16), pltpu.VMEM((N_DEV, c_rows, C_NAT), BF16),
               pltpu.VMEM((N_DEV, rep_rows, LANES), F32)]
    for lead in (4, 3, 3):
        scratch += [pltpu.VMEM((lead,) + tuple(shape), dt) for shape, dt in spec]
    scratch += [pltpu.SemaphoreType.DMA((7 * n,)), pltpu.SemaphoreType.DMA((7 * n,)),
                pltpu.SemaphoreType.DMA((14,)), pltpu.SemaphoreType.DMA((14,))]
    return pl.pallas_call(
        body, name="dx_tail", grid=(SEQ // tm,),
        in_specs=[row(dh.shape[1]) for dh in dhs] + [full(w.shape) for w in ws]
        + [row(D_MODEL), full(dwc.shape), full(p_uq.shape), full(p_rep.shape)],
        out_specs=[row(D_MODEL), full((N_DEV, c_rows, C_NAT)), full(p_uq.shape[1:]), full((N_DEV, rep_rows, LANES))],
        out_shape=[jax.ShapeDtypeStruct((SEQ, D_MODEL), F32), jax.ShapeDtypeStruct((N_DEV, c_rows, C_NAT), BF16),
                   jax.ShapeDtypeStruct(p_uq.shape[1:], F32), jax.ShapeDtypeStruct((N_DEV, rep_rows, LANES), F32)],
        scratch_shapes=scratch,
        compiler_params=pltpu.CompilerParams(dimension_semantics=("arbitrary",), vmem_limit_bytes=VMEM_BIG),
    )(*dhs, *ws, dx_res, dwc, p_uq, p_rep)


def _sum_landed(landed, c_all):
    c_rows = D_MODEL // N_DEV

    def body(rin_ref, roa_ref, rob_ref, rout_ref, call_ref, gin_ref, goa_ref, gob_ref, gout_ref):
        def total(ref, sl):
            acc = ref[0, sl, :].astype(F32)
            for s in range(1, N_DEV):
                acc = acc + ref[s, sl, :].astype(F32)
            return acc

        x, y, c = _mesh_pos()
        dev0 = jnp.where(4 * x + 2 * y + c == 0, 1.0, 0.0)
        for j in range(N_DEV):
            sl = slice(c_rows * j, c_rows * (j + 1))
            tot = total(rin_ref, sl)
            gin_ref[0, sl, C_NAT:SHARD_W] = tot[:, C_NAT:SHARD_W]
            gin_ref[0, sl, 0:C_NAT] = tot[:, 0:C_NAT] + dev0 * call_ref[j].astype(F32)
        goa_ref[0] = total(roa_ref, slice(None))
        gob_ref[0] = total(rob_ref, slice(None))
        gout_ref[0] = total(rout_ref, slice(None))

    return pl.pallas_call(
        body, name="sum_landed",
        out_shape=[jax.ShapeDtypeStruct((1,) + r.shape[1:], F32) for r in landed],
        compiler_params=pltpu.CompilerParams(vmem_limit_bytes=VMEM_MID),
    )(*landed, c_all)


_O_CQ, _O_CKV, _O_KPE, _O_ZA, _O_U, _O_V, _O_ZB, _O_GA, _O_GB = 0, 384, 512, 544, 1056, 1568, 2080, 2592, 3616


def _to_segments(w):
    z = lambda n: jnp.zeros(w.shape[:-1] + (n,), w.dtype)
    seg_a = jnp.concatenate([w[..., _O_GA:_O_GB], w[..., _O_GB:IN_WIDTH], w[..., _O_ZA:_O_U]], axis=-1)
    seg_b = jnp.concatenate([w[..., _O_U:_O_V], w[..., _O_V:_O_ZB], w[..., _O_ZB:_O_GA]], axis=-1)
    seg_c = jnp.concatenate([w[..., _O_CQ:_O_CKV], z(CQ_PAD - Q_LORA_RANK), w[..., _O_CKV:_O_KPE],
                             z(ROPE_LO), w[..., _O_KPE:_O_ZA], z(LANES - ROPE_HI)], axis=-1)
    return seg_a, seg_b, seg_c


def _from_segments(seg_a, seg_b, seg_c):
    kpe0 = CQ_PAD + LANES + ROPE_LO
    return jnp.concatenate([
        seg_c[..., 0:Q_LORA_RANK], seg_c[..., CQ_PAD:CQ_PAD + LANES], seg_c[..., kpe0:kpe0 + QK_ROPE_DIM],
        seg_a[..., 2 * D_MODEL:SEG_A], seg_b, seg_a[..., 0:2 * D_MODEL]], axis=-1)


def kernel(x, positions, w_in, b_in, g_q, w_uq, g_kv, w_ukv, w_oa, sgu_ln_g, sgu_ln_b, w_s, b_s, w_ob, w_out, ln_g, ln_b, loss_target, m_w_in, m_b_in, m_g_q, m_w_uq, m_g_kv, m_w_ukv, m_w_oa, m_sgu_ln_g, m_sgu_ln_b, m_w_s, m_b_s, m_w_ob, m_w_out, m_ln_g, m_ln_b, v_w_in, v_b_in, v_g_q, v_w_uq, v_g_kv, v_w_ukv, v_w_oa, v_sgu_ln_g, v_sgu_ln_b, v_w_s, v_b_s, v_w_ob, v_w_out, v_ln_g, v_ln_b):
    w_uq2 = w_uq[0].reshape(Q_LORA_RANK // N_DEV, MLA_HEADS * QK_HEAD_DIM)
    inv_freq = ROPE_THETA ** (-jnp.arange(0, QK_ROPE_DIM, 2, dtype=F32) / QK_ROPE_DIM)
    invf_lane = jnp.concatenate([jnp.zeros((ROPE_LO,), F32), inv_freq, inv_freq,
                                 jnp.zeros((LANES - ROPE_HI,), F32)]).reshape(1, LANES)
    first = _gather_first(w_in, w_uq2, w_oa, w_ob, w_out, x[0], positions.reshape(SEQ, 1), invf_lane)
    partials = _local_step(x[0], loss_target[0], first, b_in, g_q, g_kv, w_ukv, sgu_ln_g, sgu_ln_b, w_s, b_s, ln_g, ln_b)
    weights = dict(w_in=w_in, b_in=b_in, g_q=g_q, w_uq=w_uq, g_kv=g_kv, w_ukv=w_ukv, w_oa=w_oa, sgu_ln_g=sgu_ln_g,
                   sgu_ln_b=sgu_ln_b, w_s=w_s, b_s=b_s, w_ob=w_ob, w_out=w_out, ln_g=ln_g, ln_b=ln_b)
    moms = dict(w_in=m_w_in, b_in=m_b_in, g_q=m_g_q, w_uq=m_w_uq, g_kv=m_g_kv, w_ukv=m_w_ukv, w_oa=m_w_oa,
                sgu_ln_g=m_sgu_ln_g, sgu_ln_b=m_sgu_ln_b, w_s=m_w_s, b_s=m_b_s, w_ob=m_w_ob, w_out=m_w_out,
                ln_g=m_ln_g, ln_b=m_ln_b)
    vars_ = dict(w_in=v_w_in, b_in=v_b_in, g_q=v_g_q, w_uq=v_w_uq, g_kv=v_g_kv, w_ukv=v_w_ukv, w_oa=v_w_oa,
                 sgu_ln_g=v_sgu_ln_g, sgu_ln_b=v_sgu_ln_b, w_s=v_w_s, b_s=v_b_s, w_ob=v_w_ob, w_out=v_w_out,
                 ln_g=v_ln_g, ln_b=v_ln_b)
    return _reduce_and_update(partials, weights, moms, vars_)


def _local_step(x2, tgt, first, b_in, g_q, g_kv, w_ukv, sgu_ln_g, sgu_ln_b, w_s, b_s, ln_g, ln_b):
    wc, wq, win_lo, win_hi, oa_b, ob_b, out_b, x_bf, xt_bf, c_t, sa_t, sb_t = first
    ba, bb, bc = _to_segments(b_in)
    w_ukv_bf = w_ukv[0].astype(BF16)
    wkn = jnp.pad(w_ukv_bf[:, :, :QK_NOPE_DIM], ((0, 0), (0, 0), (0, HEAD_PAD - QK_NOPE_DIM))).reshape(KV_LORA_RANK, -1)
    wv = jnp.pad(w_ukv_bf[:, :, QK_NOPE_DIM:], ((0, 0), (0, 0), (0, HEAD_PAD - V_HEAD_DIM))).reshape(KV_LORA_RANK, -1)
    gq = jnp.pad(g_q, ((0, 0), (0, CQ_PAD - Q_LORA_RANK)))
    bias_full = jnp.repeat(b_s[0].T, SGU_GROUP_DIM, axis=1)
    w_s3 = w_s[0]
    w_st3 = jnp.swapaxes(w_s3, 1, 2)

    h_c = _mm(x_bf, wc, bias=bc, tm=512, tn=SEG_C, name="in_proj_c")
    (q, k, kt, vx, vxt), (g_lo,) = _mla_prep(h_c, gq, g_kv, wq, wkn, wv, c_t, sa_t, sb_t, (win_lo,))
    o, lse, (g_hi,) = _attn_fwd(q, kt, vx, (win_hi,))
    wa, wb = _assemble_in(g_lo, g_hi)
    h_a, (g_out,) = _mm(x_bf, wa, bias=ba, own=(out_b,), tm=512, tn=SEG_A // 2, name="in_proj_a")
    h_b, (g_oa, g_ob) = _mm(x_bf, wb, bias=bb, own=(oa_b, ob_b), tm=512, tn=SEG_B // 2, name="in_proj_b")
    y_b = _sgu_fwd(h_b, sgu_ln_g, sgu_ln_b, w_s3, bias_full)
    w_oa_f, w_ob_f, w_out_f = _assemble_out(g_oa, g_ob, g_out)

    (loss_row, dx_res, dh_a, d_o, d_yb, p_oa, p_ob, p_out, d_lng, d_lnb, d_ba) = _merge(
        x2, o, h_a, y_b, tgt, w_oa_f, w_ob_f, w_out_f, ln_g, ln_b)
    (dh_b, d_ws, d_bs_t, d_slg, d_slb, d_bb), (r_out,) = _sgu_bwd(h_b, d_yb, sgu_ln_g, sgu_ln_b, w_s3, w_st3, bias_full,
                                                                 (p_out,))
    d_wa, (r_oa,) = _mm(xt_bf, dh_a, out_dtype=BF16, parts=(p_oa,), tm=512, tn=512, name="dw_in_a")
    d_wb, (r_ob,) = _mm(xt_bf, dh_b, out_dtype=BF16, parts=(p_ob,), tm=512, tn=512, name="dw_in_b")
    dq, dk, dv, landed_in = _attn_bwd(q, kt, k, vxt, d_o, o, lse, (_to_parts(d_wa, d_wb),))
    landed = (*landed_in, r_oa, r_ob, r_out)
    dh_c, p_uq, d_wkn, d_wv, d_gq, d_gkv, d_bc = _mla_bwd(dq, dk, dv, h_c, gq, g_kv, wq, wkn, wv, c_t, sa_t, sb_t)
    d_wc = _mm(xt_bf, dh_c, out_dtype=BF16, tm=512, tn=SEG_C, name="dw_in_c")


    p_b_in = _from_segments(d_ba, d_bb, d_bc)
    p_w_ukv = jnp.concatenate([d_wkn.reshape(KV_LORA_RANK, MLA_HEADS, HEAD_PAD)[:, :, :QK_NOPE_DIM],
                               d_wv.reshape(KV_LORA_RANK, MLA_HEADS, HEAD_PAD)[:, :, :V_HEAD_DIM]], axis=-1)
    p_g_q = d_gq[:, :Q_LORA_RANK]
    p_b_s = d_bs_t[:, :SGU_GROUPS].T
    replicated = [p_b_in, p_g_q, d_gkv, p_w_ukv, d_slg, d_slb, d_ws, p_b_s, d_lng, d_lnb]
    return loss_row, ((dh_a, dh_b, dh_c), (wa, wb, wc), dx_res), landed, d_wc, p_uq, replicated


_NAMES = ["w_in", "b_in", "g_q", "w_uq", "g_kv", "w_ukv", "w_oa", "sgu_ln_g", "sgu_ln_b", "w_s", "b_s", "w_ob",
          "w_out", "ln_g", "ln_b"]
_REPLICATED = ["b_in", "g_q", "g_kv", "w_ukv", "sgu_ln_g", "sgu_ln_b", "w_s", "b_s", "ln_g", "ln_b"]


def _reduce_and_update(partials, weights, moms, vars_):
    loss_row, (dhs, ws, dx_res), landed, d_wc, p_uq, replicated = partials
    rep_flat = jnp.concatenate([a.reshape(-1) for a in replicated] + [loss_row[0, :1]])
    rep_flat = jnp.pad(rep_flat, (0, N_DEV * PACK_R_ROWS * LANES - rep_flat.size))
    dx_ab, c_all, g_uq, rep_all = _dx_tail(dhs[:2], ws[:2], dx_res, d_wc, p_uq,
                                           rep_flat.reshape(N_DEV, PACK_R_ROWS, LANES))
    dx = _mm(dhs[2], ws[2], tb=True, add=dx_ab, tm=512, tn=D_MODEL, name="dx_c")
    g_in, g_oa, g_ob, g_out = _sum_landed(landed, c_all)
    rep_sum = rep_all.reshape(-1)
    grads, pos = dict(w_in=g_in, w_uq=g_uq, w_oa=g_oa, w_ob=g_ob, w_out=g_out), 0
    for nm in _REPLICATED:
        grads[nm] = rep_sum[pos:pos + weights[nm].size]
        pos += weights[nm].size
    loss = rep_sum[pos]
    grads = {nm: grads[nm].reshape(weights[nm].shape) for nm in _NAMES}
    deltas, new_m, new_v = _adamw_all([weights[nm] for nm in _NAMES], [grads[nm] for nm in _NAMES],
                                      [moms[nm] for nm in _NAMES], [vars_[nm] for nm in _NAMES])
    return (loss, dx.reshape(1, SEQ, D_MODEL), *[grads[nm] for nm in _NAMES], *deltas, *new_m, *new_v)
```

```python
import math

import jax
import jax.numpy as jnp
from jax import lax
from jax.experimental import pallas as pl
from jax.experimental.pallas import tpu as pltpu

F32 = jnp.float32
BF16 = jnp.bfloat16

D_MODEL = 1024
SEQ = 2048
N_DEV = 8
MLA_HEADS = 8
Q_LORA_RANK = 384
KV_LORA_RANK = 128
QK_NOPE_DIM = 64
QK_ROPE_DIM = 32
V_HEAD_DIM = 64
QK_HEAD_DIM = QK_NOPE_DIM + QK_ROPE_DIM
MLA_WIDTH = MLA_HEADS * V_HEAD_DIM
ROPE_THETA = 10000.0
SGU_GROUPS = 8
SGU_GROUP_DIM = 64
SGU_WIDTH = SGU_GROUPS * SGU_GROUP_DIM
CHUNK = 128
RMS_EPS = 1e-6
LN_EPS = 1e-5
DN_ALPHA = 2.0 ** 0.25
IN_WIDTH = 4640
ATTN_SCALE = QK_HEAD_DIM ** -0.5

ADAM_LR = 0.001
ADAM_B1 = 0.9
ADAM_B2 = 0.999
ADAM_EPS = 1e-08
ADAM_WD = 0.01
ADAM_STEP = 10

LANES = 128
HEAD_PAD = 128
ROPE_LO = QK_NOPE_DIM
ROPE_MID = ROPE_LO + QK_ROPE_DIM // 2
ROPE_HI = ROPE_LO + QK_ROPE_DIM
CQ_PAD = 512

SEG_A = 2560
SEG_B = 1536
SEG_C = 768

PACK_R_ROWS = 272
VMEM_BIG = 56 * 1024 * 1024
VMEM_MID = 40 * 1024 * 1024


def _sigmoid(x):
    return 1.0 / (1.0 + jnp.exp(-x))


def _gelu_and_grad(x):
    c0 = math.sqrt(2.0 / math.pi)
    x2 = x * x
    t = jnp.tanh(c0 * (x + 0.044715 * x * x2))
    g = 0.5 * x * (1.0 + t)
    dg = 0.5 * (1.0 + t) + 0.5 * x * (1.0 - t * t) * (c0 * (1.0 + 3.0 * 0.044715 * x2))
    return g, dg


def _dot(a, b, dims):
    return lax.dot_general(a, b, (dims, ((), ())), preferred_element_type=F32)


_NN = ((1,), (0,))
_NT = ((1,), (1,))
_TN = ((0,), (0,))


def _store_grad(dh_ref, db_ref, col, val):
    cols = slice(col, col + val.shape[1])
    dh_ref[:, cols] = val.astype(BF16)
    db_ref[:, cols] += jnp.sum(val, axis=0, keepdims=True)


def _mm(a, b, *, tb=False, bias=None, add=None, out_dtype=F32, own=(), parts=(), tm, tn, name):
    m, k = a.shape
    n = b.shape[0] if tb else b.shape[1]
    assert m % tm == 0 and n % tn == 0 and not (own and parts)
    dims = _NT if tb else _NN
    nown = len(own) + len(parts)
    nm = m // tm
    nsteps = (n // tn) * nm

    def body(*refs):
        a_ref, b_ref = refs[0], refs[1]
        pos = 2
        r = _dot(a_ref[...], b_ref[...], dims)
        if bias is not None:
            r = r + refs[pos][...]; pos += 1
        if add is not None:
            r = r + refs[pos][...]; pos += 1
        own_refs = refs[pos:pos + nown]; pos += nown
        refs[pos][...] = r.astype(out_dtype)
        if nown:
            gat_refs = refs[pos + 1:pos + 1 + nown]
            send_sems, recv_sems, local_sems = refs[pos + 1 + nown:]
            step = pl.program_id(0) * nm + pl.program_id(1)
            if own:
                _gather_behind(own_refs, gat_refs, send_sems, recv_sems, local_sems, step, nsteps - 2, nsteps - 1)
            else:
                exchange = _exchange_parts(own_refs, gat_refs, send_sems, recv_sems, local_sems)
                _exchange_start(step == 0, exchange)
                _exchange_finish(step == nsteps - 1, exchange)

    b_spec = pl.BlockSpec((tn, k), lambda j, i: (j, 0)) if tb else pl.BlockSpec((k, tn), lambda j, i: (0, j))
    in_specs, args = [pl.BlockSpec((tm, k), lambda j, i: (i, 0)), b_spec], [a, b]
    if bias is not None:
        in_specs.append(pl.BlockSpec((1, tn), lambda j, i: (0, j))); args.append(bias)
    if add is not None:
        in_specs.append(pl.BlockSpec((tm, tn), lambda j, i: (i, j))); args.append(add)
    hbm = pl.BlockSpec(memory_space=pl.ANY)
    res = pl.pallas_call(
        body, name=name, grid=(n // tn, nm), in_specs=in_specs + [hbm] * nown,
        out_specs=[pl.BlockSpec((tm, tn), lambda j, i: (i, j))] + [hbm] * nown,
        out_shape=[jax.ShapeDtypeStruct((m, n), out_dtype)]
        + [jax.ShapeDtypeStruct((N_DEV,) + o.shape, o.dtype) for o in own]
        + [jax.ShapeDtypeStruct(p.shape, p.dtype) for p in parts],
        scratch_shapes=_exchange_sems(nown) if nown else [],
        compiler_params=pltpu.CompilerParams(dimension_semantics=("arbitrary", "arbitrary"), vmem_limit_bytes=VMEM_BIG),
    )(*args, *own, *parts)
    return (res[0], res[1:]) if nown else res[0]


def _rope(x, c, sa, sb):
    return x * c + pltpu.roll(x, LANES - 16, 1) * sa + pltpu.roll(x, 16, 1) * sb


def _rope_t(dy, c, sa, sb):
    return dy * c + pltpu.roll(dy * sa, 16, 1) + pltpu.roll(dy * sb, LANES - 16, 1)


def _mla_prep(h_c, gq, gkv, wq, wkn, wvx, c_t, sa_t, sb_t):
    tm = 256
    hw = MLA_HEADS * HEAD_PAD

    def body(cq_ref, ckv_ref, kpe_ref, gq_ref, gkv_ref, wq_ref, wkn_ref, wvx_ref, c_ref, sa_ref, sb_ref,
             q_ref, k_ref, kt_ref, vx_ref, vxt_ref):
        c, sa, sb = c_ref[...], sa_ref[...], sb_ref[...]
        cq = cq_ref[...]
        rq = lax.rsqrt(jnp.sum(cq * cq, axis=1, keepdims=True) * (1.0 / Q_LORA_RANK) + RMS_EPS)
        cqn = ((cq * rq) * gq_ref[...]).astype(BF16)
        qall = _dot(cqn, wq_ref[...], _NN)
        for h in range(MLA_HEADS):
            sl = slice(HEAD_PAD * h, HEAD_PAD * (h + 1))
            q_ref[:, sl] = (_rope(qall[:, sl], c, sa, sb) * ATTN_SCALE).astype(BF16)
        ckv = ckv_ref[...]
        rkv = lax.rsqrt(jnp.sum(ckv * ckv, axis=1, keepdims=True) * (1.0 / KV_LORA_RANK) + RMS_EPS)
        ckvn = ((ckv * rkv) * gkv_ref[...]).astype(BF16)
        knall = _dot(ckvn, wkn_ref[...], _NN)
        vall = _dot(ckvn, wvx_ref[...], _NN)
        kper = _rope(kpe_ref[...], c, sa, sb)
        ones_half = (lax.broadcasted_iota(jnp.int32, (tm, HEAD_PAD), 1) >= V_HEAD_DIM).astype(F32)
        for h in range(MLA_HEADS):
            sl = slice(HEAD_PAD * h, HEAD_PAD * (h + 1))
            kh = knall[:, sl] + kper
            vh = vall[:, sl] + ones_half
            k_ref[:, sl] = kh.astype(BF16)
            kt_ref[sl, :] = kh.T.astype(BF16)
            vx_ref[:, sl] = vh.astype(BF16)
            vxt_ref[sl, :] = vh.T.astype(BF16)

    full = lambda shape: pl.BlockSpec(shape, lambda i: (0, 0))
    tab = pl.BlockSpec((tm, LANES), lambda i: (i, 0))
    row = pl.BlockSpec((tm, hw), lambda i: (i, 0))
    col = pl.BlockSpec((hw, tm), lambda i: (0, i))
    return pl.pallas_call(
        body, name="mla_prep", grid=(SEQ // tm,),
        in_specs=[pl.BlockSpec((tm, CQ_PAD), lambda i: (i, 0)),
                  pl.BlockSpec((tm, LANES), lambda i: (i, CQ_PAD // LANES)),
                  pl.BlockSpec((tm, LANES), lambda i: (i, CQ_PAD // LANES + 1)),
                  full((1, CQ_PAD)), full((1, KV_LORA_RANK)),
                  full((CQ_PAD, hw)), full((KV_LORA_RANK, hw)), full((KV_LORA_RANK, hw)), tab, tab, tab],
        out_specs=[row, row, col, row, col],
        out_shape=[jax.ShapeDtypeStruct((SEQ, hw), BF16), jax.ShapeDtypeStruct((SEQ, hw), BF16),
                   jax.ShapeDtypeStruct((hw, SEQ), BF16), jax.ShapeDtypeStruct((SEQ, hw), BF16),
                   jax.ShapeDtypeStruct((hw, SEQ), BF16)],
        compiler_params=pltpu.CompilerParams(dimension_semantics=("arbitrary",), vmem_limit_bytes=VMEM_MID),
    )(h_c, h_c, h_c, gq, gkv, wq, wkn, wvx, c_t, sa_t, sb_t)


ATT_T = 512
ATT_STRIP = 64


def _attn_fwd(q, kt, vx, own):
    t, rs = ATT_T, ATT_STRIP
    nown = len(own)
    nq = SEQ // t
    nsteps = (MLA_HEADS // 2) * nq

    def body(q_ref, kt_ref, vx_ref, *rest):
        own_refs, (o_ref, l_ref), gat_refs = rest[:nown], rest[nown:nown + 2], rest[nown + 2:2 * nown + 2]
        s_scr, p_scr, m_scr, a_scr, acc_scr, send_sems, recv_sems, local_sems = rest[2 * nown + 2:]
        qi = pl.program_id(1)
        _gather_behind(own_refs, gat_refs, send_sems, recv_sems, local_sems, pl.program_id(0) * nq + qi,
                       nsteps - 2, nsteps - 1)
        lane = lax.broadcasted_iota(jnp.int32, (t, LANES), 1)
        m_scr[...] = jnp.full((2, t, LANES), -1e30, F32)
        acc_scr[...] = jnp.zeros((2, t, LANES), F32)

        def block(j, masked):
            off = pl.multiple_of(j * t, t)
            for a in range(2):
                sl = slice(HEAD_PAD * a, HEAD_PAD * (a + 1))
                s_scr[a] = _dot(q_ref[:, sl], kt_ref[sl, pl.ds(off, t)], _NN)
                for r in range(t // rs):
                    rows = slice(rs * r, rs * (r + 1))
                    s = s_scr[a, rows, :]
                    if masked:
                        rowi = lax.broadcasted_iota(jnp.int32, (rs, t), 0) + rs * r
                        coli = lax.broadcasted_iota(jnp.int32, (rs, t), 1)
                        s = jnp.where(coli <= rowi, s, -1e30)
                    m_old = m_scr[a, rows, :]
                    m_new = jnp.maximum(m_old, jnp.max(s, axis=1, keepdims=True))
                    p_scr[a, rows, :] = jnp.exp(s - m_new[:, :1]).astype(BF16)
                    a_scr[a, rows, :] = jnp.exp(m_old - m_new)
                    m_scr[a, rows, :] = m_new
                acc_scr[a] = acc_scr[a] * a_scr[a] + _dot(p_scr[a], vx_ref[pl.ds(off, t), sl], _NN)

        def step(j, carry):
            block(j, False)
            return carry
        lax.fori_loop(0, qi, step, 0)
        block(qi, True)
        res = []
        for a in range(2):
            acc = acc_scr[a]
            l = acc[:, V_HEAD_DIM:V_HEAD_DIM + 1]
            res.append((acc / l, m_scr[a] + jnp.log(l)))
        o_ref[...] = jnp.where(lane < V_HEAD_DIM, res[0][0], pltpu.roll(res[1][0], V_HEAD_DIM, 1))
        l_ref[...] = jnp.where(lane < V_HEAD_DIM, res[0][1], res[1][1])

    hbm = pl.BlockSpec(memory_space=pl.ANY)
    res = pl.pallas_call(
        body, name="attn_fwd", grid=(MLA_HEADS // 2, nq),
        in_specs=[pl.BlockSpec((t, 2 * HEAD_PAD), lambda p, i: (i, p)),
                  pl.BlockSpec((2 * HEAD_PAD, SEQ), lambda p, i: (p, 0)),
                  pl.BlockSpec((SEQ, 2 * HEAD_PAD), lambda p, i: (0, p))] + [hbm] * nown,
        out_specs=[pl.BlockSpec((t, LANES), lambda p, i: (i, p)),
                   pl.BlockSpec((t, LANES), lambda p, i: (i, p))] + [hbm] * nown,
        out_shape=[jax.ShapeDtypeStruct((SEQ, MLA_WIDTH), F32), jax.ShapeDtypeStruct((SEQ, MLA_WIDTH), F32)]
        + [jax.ShapeDtypeStruct((N_DEV,) + a.shape, a.dtype) for a in own],
        scratch_shapes=[pltpu.VMEM((2, t, t), F32), pltpu.VMEM((2, t, t), BF16), pltpu.VMEM((2, t, LANES), F32),
                        pltpu.VMEM((2, t, LANES), F32), pltpu.VMEM((2, t, LANES), F32)] + _exchange_sems(nown),
        compiler_params=pltpu.CompilerParams(dimension_semantics=("arbitrary", "arbitrary"), vmem_limit_bytes=VMEM_MID),
    )(q, kt, vx, *own)
    return res[0], res[1], res[2:]


def _exchange_parts(parts, lands, send_sems, recv_sems, local_sems):
    x, y, c = _mesh_pos()
    me = 4 * x + 2 * y + c
    peers = [(x, y, 1 - c), (1 - x, y, c), (x, 1 - y, c), (1 - x, 1 - y, c),
             (1 - x, y, 1 - c), (x, 1 - y, 1 - c), (1 - x, 1 - y, 1 - c)]
    remote, local = [], []
    for a, (part, land) in enumerate(zip(parts, lands)):
        for k, peer in enumerate(peers):
            t = 4 * peer[0] + 2 * peer[1] + peer[2]
            remote.append(_remote(part.at[t], land.at[me], send_sems, recv_sems, 7 * a + k, peer))
        local.append(pltpu.make_async_copy(part.at[me], land.at[me], local_sems.at[a]))
    return remote, local


def _exchange_start(first_step, exchange):
    remote, local = exchange

    @pl.when(first_step)
    def _():
        for cp in remote + local:
            cp.start()


def _exchange_finish(last_step, exchange):
    remote, local = exchange

    @pl.when(last_step)
    def _():
        for cp in remote:
            cp.wait_recv()
        for cp in remote:
            cp.wait_send()
        for cp in local:
            cp.wait()


def _exchange_sems(npart):
    return [pltpu.SemaphoreType.DMA((7 * npart,)), pltpu.SemaphoreType.DMA((7 * npart,)),
            pltpu.SemaphoreType.DMA((npart,))]


def _attn_bwd(q, kt, k, vxt, d_o, o, lse, parts):
    t, rs = ATT_T, ATT_STRIP
    nq = SEQ // t
    npart = len(parts)
    nsteps = MLA_HEADS // 2

    def body(q_ref, kt_ref, k_ref, vxt_ref, do_ref, o_ref, l_ref, *rest):
        part_refs, rest = rest[:npart], rest[npart:]
        dq_ref, dk_ref, dv_ref = rest[:3]
        land_refs, rest = rest[3:3 + npart], rest[3 + npart:]
        s_scr, dp_scr, p_scr, ds_scr, st_scr, send_sems, recv_sems, local_sems = rest
        exchange = _exchange_parts(part_refs, land_refs, send_sems, recv_sems, local_sems)
        _exchange_start(pl.program_id(0) == 0, exchange)
        dk_ref[...] = jnp.zeros_like(dk_ref)
        dv_ref[...] = jnp.zeros_like(dv_ref)
        lane = lax.broadcasted_iota(jnp.int32, (t, LANES), 1)

        def qtile(i, carry):
            ioff = pl.multiple_of(i * t, t)
            do_i = do_ref[pl.ds(ioff, t), :]
            o_i = o_ref[pl.ds(ioff, t), :]
            l_i = l_ref[pl.ds(ioff, t), :]
            for a in range(2):
                sl = slice(HEAD_PAD * a, HEAD_PAD * (a + 1))
                sel = (lane < V_HEAD_DIM) if a == 0 else (lane >= V_HEAD_DIM)
                doa = jnp.where(sel, do_i, 0.0)
                oa = o_i
                if a == 1:
                    doa = pltpu.roll(doa, V_HEAD_DIM, 1)
                    oa = pltpu.roll(o_i, V_HEAD_DIM, 1)
                st_scr[0] = jnp.broadcast_to(jnp.sum(doa * oa, axis=1, keepdims=True), (t, LANES))
                st_scr[1] = jnp.broadcast_to(l_i[:, V_HEAD_DIM * a:V_HEAD_DIM * a + 1], (t, LANES))
                doa_bf = doa.astype(BF16)
                qa = q_ref[pl.ds(ioff, t), sl]

                def block(j, masked, dq_acc, sl=sl, qa=qa, doa_bf=doa_bf):
                    joff = pl.multiple_of(j * t, t)
                    s_scr[...] = _dot(qa, kt_ref[sl, pl.ds(joff, t)], _NN)
                    dp_scr[...] = _dot(doa_bf, vxt_ref[sl, pl.ds(joff, t)], _NN)
                    for r in range(t // rs):
                        rows = slice(rs * r, rs * (r + 1))
                        p = jnp.exp(s_scr[rows, :] - st_scr[1, rows, :1])
                        if masked:
                            rowi = lax.broadcasted_iota(jnp.int32, (rs, t), 0) + rs * r
                            coli = lax.broadcasted_iota(jnp.int32, (rs, t), 1)
                            p = jnp.where(coli <= rowi, p, 0.0)
                        p_scr[rows, :] = p.astype(BF16)
                        ds_scr[rows, :] = (p * (dp_scr[rows, :] - st_scr[0, rows, :1])).astype(BF16)
                    dk_ref[pl.ds(joff, t), sl] += _dot(ds_scr[...], qa, _TN)
                    dv_ref[pl.ds(joff, t), sl] += _dot(p_scr[...], doa_bf, _TN)
                    return dq_acc + _dot(ds_scr[...], k_ref[pl.ds(joff, t), sl], _NN)

                dq_acc = lax.fori_loop(0, i, lambda j, acc: block(j, False, acc), jnp.zeros((t, HEAD_PAD), F32))
                dq_ref[pl.ds(ioff, t), sl] = block(i, True, dq_acc)
            return carry

        lax.fori_loop(0, nq, qtile, 0)
        _exchange_finish(pl.program_id(0) == nsteps - 1, exchange)

    hw = MLA_HEADS * HEAD_PAD
    wide = pl.BlockSpec((SEQ, 2 * HEAD_PAD), lambda p: (0, p))
    wide_t = pl.BlockSpec((2 * HEAD_PAD, SEQ), lambda p: (p, 0))
    narrow = pl.BlockSpec((SEQ, LANES), lambda p: (0, p))
    hbm = pl.BlockSpec(memory_space=pl.ANY)
    res = pl.pallas_call(
        body, name="attn_bwd", grid=(nsteps,),
        in_specs=[wide, wide_t, wide, wide_t, narrow, narrow, narrow] + [hbm] * npart,
        out_specs=[wide, wide, wide] + [hbm] * npart,
        out_shape=[jax.ShapeDtypeStruct((SEQ, hw), F32)] * 3 + [jax.ShapeDtypeStruct(p.shape, p.dtype) for p in parts],
        scratch_shapes=[pltpu.VMEM((t, t), F32), pltpu.VMEM((t, t), F32), pltpu.VMEM((t, t), BF16),
                        pltpu.VMEM((t, t), BF16), pltpu.VMEM((2, t, LANES), F32)] + _exchange_sems(npart),
        compiler_params=pltpu.CompilerParams(dimension_semantics=("arbitrary",), vmem_limit_bytes=VMEM_BIG),
    )(q, kt, k, vxt, d_o, o, lse, *parts)
    return res[0], res[1], res[2], res[3:]


def _sgu_math(u, v, zb, lg, lb, ws_ref, bias):
    ug, dug = _gelu_and_grad(u)
    vg, dvg = _gelu_and_grad(v)
    mu = jnp.mean(vg, axis=1, keepdims=True)
    xc = vg - mu
    rstd = lax.rsqrt(jnp.mean(xc * xc, axis=1, keepdims=True) + LN_EPS)
    xh = xc * rstd
    vn_bf = (xh * lg + lb).astype(BF16)
    grp = lax.broadcasted_iota(jnp.int32, (CHUNK, SGU_WIDTH), 1) // SGU_GROUP_DIM
    r_i = lax.broadcasted_iota(jnp.int32, (CHUNK, CHUNK), 0)
    c_i = lax.broadcasted_iota(jnp.int32, (CHUNK, CHUNK), 1)
    tri, tri_t = r_i >= c_i, r_i <= c_i
    mixed = bias
    for g in range(SGU_GROUPS):
        wt = jnp.where(tri, ws_ref[g], 0.0).astype(BF16)
        mixed = mixed + jnp.where(grp == g, _dot(wt, vn_bf, _NN), 0.0)
    sb = _sigmoid(zb)
    return ug, dug, dvg, rstd, xh, vn_bf, grp, tri, tri_t, mixed, sb


def _sgu_fwd(h_b, lg, lb, w_s, bias_full):
    def body(u_ref, v_ref, zb_ref, lg_ref, lb_ref, ws_ref, bias_ref, yb_ref):
        zb = zb_ref[...]
        ug, _, _, _, _, _, _, _, _, mixed, sb = _sgu_math(u_ref[...], v_ref[...], zb, lg_ref[...], lb_ref[...],
                                                       ws_ref, bias_ref[...])
        yb_ref[...] = (ug * mixed) * (zb * sb)

    blk = lambda c: pl.BlockSpec((CHUNK, SGU_WIDTH), lambda i, c=c: (i, c))
    full2 = lambda shape: pl.BlockSpec(shape, lambda i: (0, 0))
    return pl.pallas_call(
        body, name="sgu_fwd", grid=(SEQ // CHUNK,),
        in_specs=[blk(0), blk(1), blk(2), full2((1, SGU_WIDTH)), full2((1, SGU_WIDTH)),
                  pl.BlockSpec((SGU_GROUPS, CHUNK, CHUNK), lambda i: (0, 0, 0)), full2((CHUNK, SGU_WIDTH))],
        out_specs=pl.BlockSpec((CHUNK, SGU_WIDTH), lambda i: (i, 0)),
        out_shape=jax.ShapeDtypeStruct((SEQ, SGU_WIDTH), F32),
        compiler_params=pltpu.CompilerParams(dimension_semantics=("arbitrary",)),
    )(h_b, h_b, h_b, lg, lb, w_s, bias_full)


def _sgu_bwd(h_b, d_yb, lg, lb, w_s, w_st, bias_full, parts):
    nsteps = SEQ // CHUNK
    npart = len(parts)

    def body(u_ref, v_ref, zb_ref, dyb_ref, lg_ref, lb_ref, ws_ref, wst_ref, bias_ref, *rest):
        part_refs, rest = rest[:npart], rest[npart:]
        dhb_ref, dws_ref, dbs_ref, dlg_ref, dlb_ref, dbb_ref = rest[:6]
        land_refs, (dbias_acc, send_sems, recv_sems, local_sems) = rest[6:6 + npart], rest[6 + npart:]
        step = pl.program_id(0)
        exchange = _exchange_parts(part_refs, land_refs, send_sems, recv_sems, local_sems)
        _exchange_start(step == 0, exchange)

        @pl.when(step == 0)
        def _():
            dbb_ref[...] = jnp.zeros_like(dbb_ref)
            dws_ref[...] = jnp.zeros_like(dws_ref)
            dlg_ref[...] = jnp.zeros_like(dlg_ref)
            dlb_ref[...] = jnp.zeros_like(dlb_ref)
            dbias_acc[...] = jnp.zeros_like(dbias_acc)

        zb = zb_ref[...]
        lg = lg_ref[...]
        ug, dug, dvg, rstd, xh, vn_bf, grp, tri, tri_t, mixed, sb = _sgu_math(
            u_ref[...], v_ref[...], zb, lg, lb_ref[...], ws_ref, bias_ref[...])
        dyb = dyb_ref[...]
        dsgu = dyb * (zb * sb)
        dzb = dyb * (ug * mixed) * (sb * (1.0 + zb * (1.0 - sb)))
        du = dsgu * mixed * dug
        dmixed = dsgu * ug
        dbias_acc[...] += dmixed
        dvn = jnp.zeros((CHUNK, SGU_WIDTH), F32)
        for g in range(SGU_GROUPS):
            dm_g = jnp.where(grp == g, dmixed, 0.0).astype(BF16)
            wtt = jnp.where(tri_t, wst_ref[g], 0.0).astype(BF16)
            dvn = dvn + _dot(wtt, dm_g, _NN)
            dws_ref[g] += jnp.where(tri, _dot(dm_g, vn_bf, _NT), 0.0)
        dlg_ref[...] += jnp.sum(dvn * xh, axis=0, keepdims=True)
        dlb_ref[...] += jnp.sum(dvn, axis=0, keepdims=True)
        dxh = dvn * lg
        dvgel = rstd * (dxh - jnp.mean(dxh, axis=1, keepdims=True) - xh * jnp.mean(dxh * xh, axis=1, keepdims=True))
        _store_grad(dhb_ref, dbb_ref, 0, du)
        _store_grad(dhb_ref, dbb_ref, SGU_WIDTH, dvgel * dvg)
        _store_grad(dhb_ref, dbb_ref, 2 * SGU_WIDTH, dzb)

        @pl.when(step == nsteps - 1)
        def _():
            acc = dbias_acc[...]
            lane = lax.broadcasted_iota(jnp.int32, (CHUNK, LANES), 1)
            out = jnp.zeros((CHUNK, LANES), F32)
            for g in range(SGU_GROUPS):
                sg = jnp.sum(jnp.where(grp == g, acc, 0.0), axis=1, keepdims=True)
                out = jnp.where(lane == g, sg, out)
            dbs_ref[...] = out

        _exchange_finish(step == nsteps - 1, exchange)

    blk = lambda c: pl.BlockSpec((CHUNK, SGU_WIDTH), lambda i, c=c: (i, c))
    full2 = lambda shape: pl.BlockSpec(shape, lambda i: (0, 0))
    full3 = pl.BlockSpec((SGU_GROUPS, CHUNK, CHUNK), lambda i: (0, 0, 0))
    hbm = pl.BlockSpec(memory_space=pl.ANY)
    res = pl.pallas_call(
        body, name="sgu_bwd", grid=(nsteps,),
        in_specs=[blk(0), blk(1), blk(2), pl.BlockSpec((CHUNK, SGU_WIDTH), lambda i: (i, 0)),
                  full2((1, SGU_WIDTH)), full2((1, SGU_WIDTH)), full3, full3, full2((CHUNK, SGU_WIDTH))] + [hbm] * npart,
        out_specs=[pl.BlockSpec((CHUNK, SEG_B), lambda i: (i, 0)), full3, full2((CHUNK, LANES)),
                   full2((1, SGU_WIDTH)), full2((1, SGU_WIDTH)), full2((1, SEG_B))] + [hbm] * npart,
        out_shape=[jax.ShapeDtypeStruct((SEQ, SEG_B), BF16),
                   jax.ShapeDtypeStruct((SGU_GROUPS, CHUNK, CHUNK), F32),
                   jax.ShapeDtypeStruct((CHUNK, LANES), F32),
                   jax.ShapeDtypeStruct((1, SGU_WIDTH), F32), jax.ShapeDtypeStruct((1, SGU_WIDTH), F32),
                   jax.ShapeDtypeStruct((1, SEG_B), F32)] + [jax.ShapeDtypeStruct(p.shape, p.dtype) for p in parts],
        scratch_shapes=[pltpu.VMEM((CHUNK, SGU_WIDTH), F32)] + _exchange_sems(npart),
        compiler_params=pltpu.CompilerParams(dimension_semantics=("arbitrary",)),
    )(h_b, h_b, h_b, d_yb, lg, lb, w_s, w_st, bias_full, *parts)
    return res[:6], res[6:]


def _merge(x, o, h_a, y_b, target, w_oa, w_ob, w_out, ln_g, ln_b):
    tm = 256
    nsteps = SEQ // tm

    def body(x_ref, o_ref, ga_ref, gb_ref, za_ref, yb_ref, tgt_ref, woa_ref, wob_ref, wout_ref, lng_ref, lnb_ref,
             loss_ref, dxr_ref, dha_ref, do_ref, dyb_ref, poa_ref, pob_ref, pout_ref, dlng_ref, dlnb_ref, dba_ref,
             dwoa_ref, dwob_ref, dwout_ref):
        step = pl.program_id(0)

        @pl.when(step == 0)
        def _():
            for r in (loss_ref, dwoa_ref, dwob_ref, dwout_ref, dlng_ref, dlnb_ref, dba_ref):
                r[...] = jnp.zeros_like(r)

        o = o_ref[...]
        za = za_ref[...]
        sa = _sigmoid(za)
        ya_bf = (o * (za * sa)).astype(BF16)
        yb_bf = yb_ref[...].astype(BF16)
        woa, wob, wout = woa_ref[...], wob_ref[...], wout_ref[...]
        pa = _dot(ya_bf, woa, _NN)
        pb = _dot(yb_bf, wob, _NN)
        sga = _sigmoid(ga_ref[...])
        sgb = _sigmoid(gb_ref[...])
        merged_bf = (sga * pa + sgb * pb).astype(BF16)
        r = DN_ALPHA * x_ref[...] + _dot(merged_bf, wout, _NN)
        mu = jnp.mean(r, axis=1, keepdims=True)
        rc = r - mu
        rstd = lax.rsqrt(jnp.mean(rc * rc, axis=1, keepdims=True) + LN_EPS)
        xh = rc * rstd
        lng = lng_ref[...]
        y = xh * lng + lnb_ref[...]
        e = y - tgt_ref[...]
        loss_ref[...] += 0.5 * jnp.sum(jnp.sum(e * e, axis=1, keepdims=True) * (1.0 / D_MODEL), axis=0, keepdims=True)

        dy = e * (1.0 / D_MODEL)
        dlng_ref[...] += jnp.sum(dy * xh, axis=0, keepdims=True)
        dlnb_ref[...] += jnp.sum(dy, axis=0, keepdims=True)
        dxh = dy * lng
        dr = rstd * (dxh - jnp.mean(dxh, axis=1, keepdims=True) - xh * jnp.mean(dxh * xh, axis=1, keepdims=True))
        dxr_ref[...] = DN_ALPHA * dr
        dr_bf = dr.astype(BF16)
        dwout_ref[...] += _dot(merged_bf, dr_bf, _TN)
        dmerged = _dot(dr_bf, wout, _NT)
        dpa_bf = (dmerged * sga).astype(BF16)
        dpb_bf = (dmerged * sgb).astype(BF16)
        _store_grad(dha_ref, dba_ref, 0, dmerged * pa * (sga * (1.0 - sga)))
        _store_grad(dha_ref, dba_ref, D_MODEL, dmerged * pb * (sgb * (1.0 - sgb)))
        dwoa_ref[...] += _dot(ya_bf, dpa_bf, _TN)
        dwob_ref[...] += _dot(yb_bf, dpb_bf, _TN)
        dya = _dot(dpa_bf, woa, _NT)
        dyb_ref[...] = _dot(dpb_bf, wob, _NT)
        do_ref[...] = dya * (za * sa)
        _store_grad(dha_ref, dba_ref, 2 * D_MODEL, dya * o * (sa * (1.0 + za * (1.0 - sa))))

        @pl.when(step == nsteps - 1)
        def _():
            cols = D_MODEL // N_DEV
            for j in range(N_DEV):
                poa_ref[j] = dwoa_ref[:, cols * j:cols * (j + 1)].astype(BF16)
                pob_ref[j] = dwob_ref[:, cols * j:cols * (j + 1)].astype(BF16)
                pout_ref[j] = dwout_ref[cols * j:cols * (j + 1), :].astype(BF16)

    row = lambda w, c=0: pl.BlockSpec((tm, w), lambda i, c=c: (i, c))
    full = lambda shape: pl.BlockSpec(shape, lambda i: (0, 0))
    full3 = lambda shape: pl.BlockSpec(shape, lambda i: (0, 0, 0))
    return pl.pallas_call(
        body, name="merge", grid=(nsteps,),
        in_specs=[row(D_MODEL), row(MLA_WIDTH), row(D_MODEL, 0), row(D_MODEL, 1), row(MLA_WIDTH, 4), row(SGU_WIDTH),
                  row(D_MODEL), full((MLA_WIDTH, D_MODEL)), full((SGU_WIDTH, D_MODEL)), full((D_MODEL, D_MODEL)),
                  full((1, D_MODEL)), full((1, D_MODEL))],
        out_specs=[full((1, LANES)), row(D_MODEL), row(SEG_A), row(MLA_WIDTH), row(SGU_WIDTH),
                   full3((N_DEV, MLA_WIDTH, D_MODEL // N_DEV)), full3((N_DEV, SGU_WIDTH, D_MODEL // N_DEV)),
                   full3((N_DEV, D_MODEL // N_DEV, D_MODEL)), full((1, D_MODEL)), full((1, D_MODEL)), full((1, SEG_A))],
        out_shape=[jax.ShapeDtypeStruct((1, LANES), F32),
                   jax.ShapeDtypeStruct((SEQ, D_MODEL), F32), jax.ShapeDtypeStruct((SEQ, SEG_A), BF16),
                   jax.ShapeDtypeStruct((SEQ, MLA_WIDTH), F32), jax.ShapeDtypeStruct((SEQ, SGU_WIDTH), F32),
                   jax.ShapeDtypeStruct((N_DEV, MLA_WIDTH, D_MODEL // N_DEV), BF16),
                   jax.ShapeDtypeStruct((N_DEV, SGU_WIDTH, D_MODEL // N_DEV), BF16),
                   jax.ShapeDtypeStruct((N_DEV, D_MODEL // N_DEV, D_MODEL), BF16),
                   jax.ShapeDtypeStruct((1, D_MODEL), F32), jax.ShapeDtypeStruct((1, D_MODEL), F32),
                   jax.ShapeDtypeStruct((1, SEG_A), F32)],
        scratch_shapes=[pltpu.VMEM((MLA_WIDTH, D_MODEL), F32), pltpu.VMEM((SGU_WIDTH, D_MODEL), F32),
                        pltpu.VMEM((D_MODEL, D_MODEL), F32)],
        compiler_params=pltpu.CompilerParams(dimension_semantics=("arbitrary",), vmem_limit_bytes=VMEM_BIG),
    )(x, o, h_a, h_a, h_a, y_b, target, w_oa, w_ob, w_out, ln_g, ln_b)


def _mla_bwd(dq, dk, dv, h_c, gq, gkv, wq, wkn, wv, c_t, sa_t, sb_t):
    tm = 256
    hw = MLA_HEADS * HEAD_PAD

    def body(dq_ref, dk_ref, dv_ref, cq_ref, ckv_ref, gq_ref, gkv_ref, wq_ref, wkn_ref, wv_ref, c_ref, sa_ref, sb_ref,
             dhc_ref, puq_ref, dwkn_ref, dwv_ref, dgq_ref, dgkv_ref, dbc_ref, pre_ref, dwq_ref):
        @pl.when(pl.program_id(0) == 0)
        def _():
            for r in (dwq_ref, dwkn_ref, dwv_ref, dgq_ref, dgkv_ref, dbc_ref):
                r[...] = jnp.zeros_like(r)

        c, sa, sb = c_ref[...], sa_ref[...], sb_ref[...]
        lane = lax.broadcasted_iota(jnp.int32, (tm, LANES), 1)
        rope_lanes = jnp.logical_and(lane >= ROPE_LO, lane < ROPE_HI)

        cq = cq_ref[...]
        gq = gq_ref[...]
        rq = lax.rsqrt(jnp.sum(cq * cq, axis=1, keepdims=True) * (1.0 / Q_LORA_RANK) + RMS_EPS)
        nq = cq * rq
        cqn_bf = (nq * gq).astype(BF16)
        for h in range(MLA_HEADS):
            sl = slice(HEAD_PAD * h, HEAD_PAD * (h + 1))
            pre_ref[:, sl] = _rope_t(dq_ref[:, sl] * ATTN_SCALE, c, sa, sb).astype(BF16)
        dqpre_bf = pre_ref[...]
        dcqn = _dot(dqpre_bf, wq_ref[...], _NT)
        dwq_ref[...] += _dot(cqn_bf, dqpre_bf, _TN)
        dgq_ref[...] += jnp.sum(dcqn * nq, axis=0, keepdims=True)
        dnq = dcqn * gq
        _store_grad(dhc_ref, dbc_ref, 0,
                    rq * (dnq - nq * (jnp.sum(dnq * nq, axis=1, keepdims=True) * (1.0 / Q_LORA_RANK))))

        ckv = ckv_ref[...]
        gkv = gkv_ref[...]
        rkv = lax.rsqrt(jnp.sum(ckv * ckv, axis=1, keepdims=True) * (1.0 / KV_LORA_RANK) + RMS_EPS)
        nkv = ckv * rkv
        ckvn_bf = (nkv * gkv).astype(BF16)
        dk = dk_ref[...]
        dk_bf = dk.astype(BF16)
        dv_bf = dv_ref[...].astype(BF16)
        dckvn = _dot(dk_bf, wkn_ref[...], _NT) + _dot(dv_bf, wv_ref[...], _NT)
        dwkn_ref[...] += _dot(ckvn_bf, dk_bf, _TN)
        dwv_ref[...] += _dot(ckvn_bf, dv_bf, _TN)
        dgkv_ref[...] += jnp.sum(dckvn * nkv, axis=0, keepdims=True)
        dnkv = dckvn * gkv
        _store_grad(dhc_ref, dbc_ref, CQ_PAD, rkv * (
            dnkv - nkv * (jnp.sum(dnkv * nkv, axis=1, keepdims=True) * (1.0 / KV_LORA_RANK))))
        dkpe = jnp.zeros((tm, LANES), F32)
        for h in range(MLA_HEADS):
            dkpe = dkpe + dk[:, HEAD_PAD * h:HEAD_PAD * (h + 1)]
        _store_grad(dhc_ref, dbc_ref, CQ_PAD + LANES, _rope_t(jnp.where(rope_lanes, dkpe, 0.0), c, sa, sb))

        @pl.when(pl.program_id(0) == SEQ // tm - 1)
        def _():
            rows = Q_LORA_RANK // N_DEV
            for j in range(N_DEV):
                for h in range(MLA_HEADS):
                    puq_ref[j, :, QK_HEAD_DIM * h:QK_HEAD_DIM * (h + 1)] = dwq_ref[
                        rows * j:rows * (j + 1), HEAD_PAD * h:HEAD_PAD * h + QK_HEAD_DIM].astype(BF16)

    full = lambda shape: pl.BlockSpec(shape, lambda i: (0, 0))
    row = lambda w, c=0: pl.BlockSpec((tm, w), lambda i, c=c: (i, c))
    return pl.pallas_call(
        body, name="mla_bwd", grid=(SEQ // tm,),
        in_specs=[row(hw), row(hw), row(hw), row(CQ_PAD, 0), row(LANES, CQ_PAD // LANES),
                  full((1, CQ_PAD)), full((1, KV_LORA_RANK)), full((CQ_PAD, hw)), full((KV_LORA_RANK, hw)),
                  full((KV_LORA_RANK, hw)), row(LANES), row(LANES), row(LANES)],
        out_specs=[row(SEG_C), pl.BlockSpec((N_DEV, Q_LORA_RANK // N_DEV, MLA_HEADS * QK_HEAD_DIM), lambda i: (0, 0, 0)),
                   full((KV_LORA_RANK, hw)), full((KV_LORA_RANK, hw)),
                   full((1, CQ_PAD)), full((1, KV_LORA_RANK)), full((1, SEG_C))],
        out_shape=[jax.ShapeDtypeStruct((SEQ, SEG_C), BF16),
                   jax.ShapeDtypeStruct((N_DEV, Q_LORA_RANK // N_DEV, MLA_HEADS * QK_HEAD_DIM), BF16),
                   jax.ShapeDtypeStruct((KV_LORA_RANK, hw), F32), jax.ShapeDtypeStruct((KV_LORA_RANK, hw), F32),
                   jax.ShapeDtypeStruct((1, CQ_PAD), F32), jax.ShapeDtypeStruct((1, KV_LORA_RANK), F32),
                   jax.ShapeDtypeStruct((1, SEG_C), F32)],
        scratch_shapes=[pltpu.VMEM((tm, hw), BF16), pltpu.VMEM((CQ_PAD, hw), F32)],
        compiler_params=pltpu.CompilerParams(dimension_semantics=("arbitrary",), vmem_limit_bytes=VMEM_MID),
    )(dq, dk, dv, h_c, h_c, gq, gkv, wq, wkn, wv, c_t, sa_t, sb_t)


def _adamw_all(ws, gs, ms, vs):
    n = len(ws)
    c1 = 1.0 / (1.0 - ADAM_B1 ** ADAM_STEP)
    c2 = 1.0 / (1.0 - ADAM_B2 ** ADAM_STEP)

    def body(*refs):
        for idx in range(n):
            w, g, m, v = (refs[idx][...], refs[n + idx][...], refs[2 * n + idx][...], refs[3 * n + idx][...])
            m_new = ADAM_B1 * m + (1.0 - ADAM_B1) * g
            v_new = ADAM_B2 * v + (1.0 - ADAM_B2) * (g * g)
            delta = -ADAM_LR * ((m_new * c1) / (jnp.sqrt(v_new * c2) + ADAM_EPS) + ADAM_WD * w)
            refs[4 * n + idx][...] = delta
            refs[5 * n + idx][...] = m_new
            refs[6 * n + idx][...] = v_new

    shapes = [jax.ShapeDtypeStruct(w.shape, F32) for w in ws]
    outs = pl.pallas_call(
        body, name="adamw", out_shape=shapes * 3,
        compiler_params=pltpu.CompilerParams(vmem_limit_bytes=VMEM_BIG),
    )(*ws, *gs, *ms, *vs)
    return outs[:n], outs[n:2 * n], outs[2 * n:]


SHARD_W = IN_WIDTH // N_DEV
W_IN_LO = 128

_PIECES = [(0, 384, 2, 0), (384, 512, 2, CQ_PAD), (512, 544, 2, CQ_PAD + LANES + ROPE_LO),
           (544, 1056, 0, 2 * D_MODEL), (1056, 1568, 1, 0), (1568, 2080, 1, SGU_WIDTH),
           (2080, 2592, 1, 2 * SGU_WIDTH), (2592, 3616, 0, 0), (3616, 4640, 0, D_MODEL)]


def _column_runs():
    runs = []
    for n0, n1, seg, d0 in _PIECES:
        for j in range(N_DEV):
            lo, hi = max(n0, j * SHARD_W), min(n1, (j + 1) * SHARD_W)
            if lo < hi:
                runs.append((j, lo - j * SHARD_W, hi - j * SHARD_W, seg, d0 + lo - n0))
    return runs


def _mesh_pos():
    return lax.axis_index("x"), lax.axis_index("y"), lax.axis_index("c")


def _remote(src, dst, send_sems, recv_sems, k, to):
    return pltpu.make_async_remote_copy(src_ref=src, dst_ref=dst, send_sem=send_sems.at[k], recv_sem=recv_sems.at[k],
                                        device_id=to, device_id_type=pl.DeviceIdType.MESH)


def _gather_exchange(gats, send_sems, recv_sems, meanwhile=None):
    x, y, c = _mesh_pos()
    me, sibling = (x, y, c), (x, y, 1 - c)
    chips = [(1 - x, y), (x, 1 - y), (1 - x, 1 - y)]

    def copy(a, k, blk, to):
        slab = gats[a].at[4 * blk[0] + 2 * blk[1] + blk[2]]
        return _remote(slab, slab, send_sems, recv_sems, 7 * a + k, to)

    arrays = range(len(gats))
    first = [copy(a, 1 + j, me, (*chip, c)) for j, chip in enumerate(chips) for a in arrays]
    first += [copy(a, 0, me, sibling) for a in arrays]
    for cp in first:
        cp.start()
    if meanwhile is not None:
        meanwhile()
    passed = []
    for j, chip in enumerate(chips):
        for a in arrays:
            copy(a, 1 + j, (*chip, c), me).wait_recv()
            fwd = copy(a, 4 + j, (*chip, c), sibling)
            fwd.start()
            passed.append(fwd)
    for a in arrays:
        copy(a, 0, sibling, me).wait_recv()
    for j, chip in enumerate(chips):
        for a in arrays:
            copy(a, 4 + j, (*chip, 1 - c), me).wait_recv()
    for cp in first + passed:
        cp.wait_send()


def _gather_behind(own, gats, send_sems, recv_sems, local_sems, step, mid, last):
    x, y, c = _mesh_pos()
    me, sibling = (x, y, c), (x, y, 1 - c)
    chips = [(1 - x, y), (x, 1 - y), (1 - x, 1 - y)]
    arrays = range(len(gats))

    def copy(a, k, blk, to, src=None):
        slab = gats[a].at[4 * blk[0] + 2 * blk[1] + blk[2]]
        return _remote(slab if src is None else src, slab, send_sems, recv_sems, 7 * a + k, to)

    first = [copy(a, 1 + j, me, (*chip, c), src=own[a]) for j, chip in enumerate(chips) for a in arrays]
    first += [copy(a, 0, me, sibling, src=own[a]) for a in arrays]
    local = [pltpu.make_async_copy(own[a], gats[a].at[4 * x + 2 * y + c], local_sems.at[a]) for a in arrays]
    passed = [copy(a, 4 + j, (*chip, c), sibling) for j, chip in enumerate(chips) for a in arrays]

    @pl.when(step == 0)
    def _():
        for cp in first + local:
            cp.start()

    @pl.when(step == mid)
    def _():
        for j, chip in enumerate(chips):
            for a in arrays:
                copy(a, 1 + j, (*chip, c), me).wait_recv()
        for cp in passed:
            cp.start()

    @pl.when(step == last)
    def _():
        for a in arrays:
            copy(a, 0, sibling, me).wait_recv()
        for j, chip in enumerate(chips):
            for a in arrays:
                copy(a, 4 + j, (*chip, 1 - c), me).wait_recv()
        for cp in first + passed:
            cp.wait_send()
        for cp in local:
            cp.wait()


def _gather_first(w_in, w_uq2, w_oa, w_ob, w_out, x2, pos_col, invf_lane):
    hw = MLA_HEADS * HEAD_PAD
    uq_rows = Q_LORA_RANK // N_DEV
    rows = 256

    def body(win_ref, wuq_ref, woa_ref, wob_ref, wout_ref, x_ref, pos_ref, invf_ref,
             wc_ref, wq_ref, winlo_ref, winhi_ref, oab_ref, obb_ref, outb_ref, xb_ref, xt_ref, c_ref, sa_ref, sb_ref,
             g_uq, blk0, send_sems, recv_sems):
        def local_work():
            for i in range(SEQ // rows):
                xi = x_ref[rows * i:rows * (i + 1), :]
                xb_ref[rows * i:rows * (i + 1), :] = xi.astype(BF16)
                xt_ref[:, rows * i:rows * (i + 1)] = xi.T.astype(BF16)
            ang = pos_ref[...].astype(F32) * invf_ref[...]
            cs, sn = jnp.cos(ang), jnp.sin(ang)
            lane = lax.broadcasted_iota(jnp.int32, ang.shape, 1)
            c_ref[...] = jnp.where(lane < ROPE_LO, 1.0, jnp.where(lane < ROPE_HI, cs, 0.0))
            sa_ref[...] = jnp.where(jnp.logical_and(lane >= ROPE_LO, lane < ROPE_MID), -sn, 0.0)
            sb_ref[...] = jnp.where(jnp.logical_and(lane >= ROPE_MID, lane < ROPE_HI), sn, 0.0)

        x, y, c = _mesh_pos()
        me = (x, y, c)
        winlo_ref[...] = win_ref[0, 0:W_IN_LO, :].astype(BF16)
        winhi_ref[...] = win_ref[0, W_IN_LO:D_MODEL, :].astype(BF16)
        oab_ref[...] = woa_ref[0].astype(BF16)
        obb_ref[...] = wob_ref[0].astype(BF16)
        outb_ref[...] = wout_ref[0].astype(BF16)
        g_uq[4 * x + 2 * y + c] = wuq_ref[...].astype(BF16)

        chip0 = jnp.logical_and(x == 0, y == 0)
        south = c == 0
        half = D_MODEL // 2
        halves = [blk0.at[pl.ds(0, half)], blk0.at[pl.ds(half, half)]]

        def bcopy(k, to, part=None):
            ref = blk0 if part is None else halves[part]
            return _remote(ref, ref, send_sems, recv_sems, 7 + k, to)

        sends0 = [(0, (0, 0, 1), None), (1, (1, 0, 0), 0), (2, (0, 1, 0), 1), (3, (1, 0, 0), 1), (4, (0, 1, 0), 0)]

        @pl.when(jnp.logical_and(chip0, south))
        def _():
            blk0[0:W_IN_LO, :] = winlo_ref[...]
            blk0[W_IN_LO:D_MODEL, :] = winhi_ref[...]
            for k, to, part in sends0:
                bcopy(k, to, part).start()

        _gather_exchange([g_uq], send_sems, recv_sems, meanwhile=local_work)

        for (cx, cy), first_k, first_half, second_k in (((1, 0), 1, 0, 3), ((0, 1), 2, 1, 4)):
            @pl.when(jnp.logical_and(jnp.logical_and(x == cx, y == cy), south))
            def _(cx=cx, cy=cy, first_k=first_k, first_half=first_half, second_k=second_k):
                bcopy(first_k, me, first_half).wait_recv()
                onward = bcopy(5 + first_half, (1, 1, 0), first_half)
                onward.start()
                bcopy(second_k, me, 1 - first_half).wait_recv()
                north = bcopy(7, (cx, cy, 1))
                north.start()
                onward.wait_send()
                north.wait_send()

        @pl.when(jnp.logical_and(jnp.logical_and(x == 1, y == 1), south))
        def _():
            bcopy(5, me, 0).wait_recv()
            bcopy(6, me, 1).wait_recv()
            north = bcopy(7, (1, 1, 1))
            north.start()
            north.wait_send()

        @pl.when(jnp.logical_and(chip0, c == 1))
        def _():
            bcopy(0, me).wait_recv()

        @pl.when(jnp.logical_and(jnp.logical_not(chip0), c == 1))
        def _():
            bcopy(7, me).wait_recv()

        @pl.when(jnp.logical_and(chip0, south))
        def _():
            for k, to, part in sends0:
                bcopy(k, to, part).wait_send()

        for j, s0, s1, seg, d0 in _column_runs():
            if seg == 2:
                wc_ref[:, d0:d0 + (s1 - s0)] = blk0[:, s0:s1]
        zeros = lambda r, w: jnp.zeros((r, w), BF16)
        wc_ref[:, Q_LORA_RANK:CQ_PAD] = zeros(D_MODEL, CQ_PAD - Q_LORA_RANK)
        wc_ref[:, CQ_PAD + LANES:CQ_PAD + LANES + ROPE_LO] = zeros(D_MODEL, ROPE_LO)
        wc_ref[:, CQ_PAD + LANES + ROPE_HI:SEG_C] = zeros(D_MODEL, LANES - ROPE_HI)
        wq_ref[Q_LORA_RANK:CQ_PAD, :] = zeros(CQ_PAD - Q_LORA_RANK, hw)
        for h in range(MLA_HEADS):
            wq_ref[0:Q_LORA_RANK, HEAD_PAD * h + QK_HEAD_DIM:HEAD_PAD * (h + 1)] = zeros(Q_LORA_RANK, HEAD_PAD - QK_HEAD_DIM)
        for j in range(N_DEV):
            for h in range(MLA_HEADS):
                wq_ref[uq_rows * j:uq_rows * (j + 1), HEAD_PAD * h:HEAD_PAD * h + QK_HEAD_DIM] = g_uq[
                    j, :, QK_HEAD_DIM * h:QK_HEAD_DIM * (h + 1)]

    vmem = pl.BlockSpec(memory_space=pltpu.VMEM)
    return pl.pallas_call(
        body, name="gather_first",
        out_shape=[jax.ShapeDtypeStruct((D_MODEL, SEG_C), BF16), jax.ShapeDtypeStruct((CQ_PAD, hw), BF16),
                   jax.ShapeDtypeStruct((W_IN_LO, SHARD_W), BF16), jax.ShapeDtypeStruct((D_MODEL - W_IN_LO, SHARD_W), BF16),
                   jax.ShapeDtypeStruct(w_oa.shape[1:], BF16),
                   jax.ShapeDtypeStruct(w_ob.shape[1:], BF16), jax.ShapeDtypeStruct(w_out.shape[1:], BF16),
                   jax.ShapeDtypeStruct((SEQ, D_MODEL), BF16), jax.ShapeDtypeStruct((D_MODEL, SEQ), BF16)]
        + [jax.ShapeDtypeStruct((SEQ, LANES), F32)] * 3,
        in_specs=[vmem] * 8, out_specs=[vmem] * 12,
        scratch_shapes=[pltpu.VMEM((N_DEV, uq_rows, MLA_HEADS * QK_HEAD_DIM), BF16), pltpu.VMEM((D_MODEL, SHARD_W), BF16),
                        pltpu.SemaphoreType.DMA((15,)), pltpu.SemaphoreType.DMA((15,))],
        compiler_params=pltpu.CompilerParams(vmem_limit_bytes=VMEM_BIG),
    )(w_in, w_uq2, w_oa, w_ob, w_out, x2, pos_col, invf_lane)


def _assemble_in(g_lo, g_hi):
    def body(glo_ref, ghi_ref, wa_ref, wb_ref):
        segs = [wa_ref, wb_ref]
        for j, s0, s1, seg, d0 in _column_runs():
            if seg < 2:
                segs[seg][0:W_IN_LO, d0:d0 + (s1 - s0)] = glo_ref[j, :, s0:s1]
                segs[seg][W_IN_LO:D_MODEL, d0:d0 + (s1 - s0)] = ghi_ref[j, :, s0:s1]

    return pl.pallas_call(
        body, name="assemble_in",
        out_shape=[jax.ShapeDtypeStruct((D_MODEL, SEG_A), BF16), jax.ShapeDtypeStruct((D_MODEL, SEG_B), BF16)],
        compiler_params=pltpu.CompilerParams(vmem_limit_bytes=VMEM_MID),
    )(g_lo, g_hi)


def _assemble_out(g_oa, g_ob, g_out):
    cols = D_MODEL // N_DEV

    def body(goa_ref, gob_ref, gout_ref, oa_ref, ob_ref, out_ref):
        for j in range(N_DEV):
            oa_ref[:, cols * j:cols * (j + 1)] = goa_ref[j]
            ob_ref[:, cols * j:cols * (j + 1)] = gob_ref[j]
            out_ref[cols * j:cols * (j + 1), :] = gout_ref[j]

    return pl.pallas_call(
        body, name="assemble_out",
        out_shape=[jax.ShapeDtypeStruct((MLA_WIDTH, D_MODEL), BF16), jax.ShapeDtypeStruct((SGU_WIDTH, D_MODEL), BF16),
                   jax.ShapeDtypeStruct((D_MODEL, D_MODEL), BF16)],
    )(g_oa, g_ob, g_out)


C_NAT = 544


def _to_parts(dwa, dwb):
    def body(dwa_ref, dwb_ref, pin_ref):
        pin_ref[0, :, 0:C_NAT] = jnp.zeros((D_MODEL, C_NAT), BF16)
        segs = [dwa_ref, dwb_ref]
        for j, s0, s1, seg, d0 in _column_runs():
            if seg < 2:
                pin_ref[j, :, s0:s1] = segs[seg][:, d0:d0 + (s1 - s0)]

    return pl.pallas_call(body, name="to_parts", out_shape=jax.ShapeDtypeStruct((N_DEV, D_MODEL, SHARD_W), BF16),
                          compiler_params=pltpu.CompilerParams(vmem_limit_bytes=VMEM_MID))(dwa, dwb)


def _dx_tail(dhs, ws, dx_res, dwc, p_uq, p_rep):
    tm = SEQ // 4
    rep_rows = p_rep.shape[1]
    c_rows = D_MODEL // N_DEV
    spec = [((c_rows, C_NAT), BF16), (p_uq.shape[1:], BF16), ((rep_rows, LANES), F32)]
    n = len(spec)

    nseg = len(dhs)

    def body(*refs):
        dh_refs, w_refs = refs[:nseg], refs[nseg:2 * nseg]
        dxr_ref, dwc_ref, puq_ref, prep_ref, dx_ref, call_ref, guq_ref, repall_ref, pc_ref, c_all, rep_all = refs[
            2 * nseg:2 * nseg + 11]
        rest = refs[2 * nseg + 11:]
        ras, tbs, rbs = rest[0:n], rest[n:2 * n], rest[2 * n:3 * n]
        send_sems, recv_sems, gsend, grecv = rest[3 * n:]
        step = pl.program_id(0)
        x, y, c = _mesh_pos()
        me_idx = 4 * x + 2 * y + c
        me, sibling = (x, y, c), (x, y, 1 - c)
        others = [(1 - x, y), (x, 1 - y), (1 - x, 1 - y)]
        parts = [pc_ref, puq_ref, prep_ref]
        gats = [rep_all, c_all]

        def stage1(chip, a):
            return _remote(parts[a].at[2 * chip + (1 - c)], ras[a].at[chip], send_sems, recv_sems, 7 * a + chip, sibling)

        def stage2(k, a):
            cx, cy = others[k]
            return _remote(tbs[a].at[k], rbs[a].at[k], send_sems, recv_sems, 7 * a + 4 + k, (cx, cy, c))

        def gcopy(a, k, blk, to):
            slab = gats[a].at[4 * blk[0] + 2 * blk[1] + blk[2]]
            return _remote(slab, slab, gsend, grecv, 7 * a + k, to)

        def chip_sum(a, chip):
            return parts[a][2 * chip + c].astype(F32) + ras[a][chip].astype(F32)

        @pl.when(step == 0)
        def _():
            for j, s0, s1, seg, d0 in _column_runs():
                if seg == 2:
                    for r in range(N_DEV):
                        pc_ref[r, :, s0:s1] = dwc_ref[c_rows * r:c_rows * (r + 1), d0:d0 + (s1 - s0)]
            for chip in range(4):
                for a in range(n):
                    stage1(chip, a).start()

        @pl.when(step == 1)
        def _():
            for chip in range(4):
                for a in range(n):
                    stage1(chip, a).wait_recv()
            for k, (cx, cy) in enumerate(others):
                for a in range(n):
                    tbs[a][k] = chip_sum(a, 2 * cx + cy).astype(spec[a][1])
                    stage2(k, a).start()

        @pl.when(step == 2)
        def _():
            for k in range(3):
                for a in range(n):
                    stage2(k, a).wait_recv()
            sums = []
            for a in range(n):
                acc = chip_sum(a, 2 * x + y)
                for k in range(3):
                    acc = acc + rbs[a][k].astype(F32)
                sums.append(acc)
            c_all[me_idx] = sums[0].astype(BF16)
            guq_ref[...] = sums[1]
            rep_all[me_idx] = sums[2]
            for a in range(2):
                for j, chip in enumerate(others):
                    gcopy(a, 1 + j, me, (*chip, c)).start()
                gcopy(a, 0, me, sibling).start()

        @pl.when(step == 3)
        def _():
            for j, chip in enumerate(others):
                for a in range(2):
                    gcopy(a, 1 + j, (*chip, c), me).wait_recv()
                    gcopy(a, 4 + j, (*chip, c), sibling).start()
            for a in range(2):
                gcopy(a, 0, sibling, me).wait_recv()
                for j, chip in enumerate(others):
                    gcopy(a, 4 + j, (*chip, 1 - c), me).wait_recv()
            for a in range(2):
                gcopy(a, 0, me, sibling).wait_send()
                for j, chip in enumerate(others):
                    gcopy(a, 1 + j, me, (*chip, c)).wait_send()
                    gcopy(a, 4 + j, (*chip, c), sibling).wait_send()
            for a in range(n):
                for chip in range(4):
                    stage1(chip, a).wait_send()
                for k in range(3):
                    stage2(k, a).wait_send()
            call_ref[...] = c_all[...]
            repall_ref[...] = rep_all[...]

        acc = dxr_ref[...]
        for dh_ref, w_ref in zip(dh_refs, w_refs):
            acc = acc + _dot(dh_ref[...], w_ref[...], _NT)
        dx_ref[...] = acc

    row = lambda w: pl.BlockSpec((tm, w), lambda i: (i, 0))
    full = lambda shape: pl.BlockSpec(shape, lambda i: (0,) * len(shape))
    scratch = [pltpu.VMEM((N_DEV, c_rows, C_NAT), BF16), pltpu.VMEM((N_DEV, c_rows, C_NAT), BF16),
               pltpu.VMEM((N_DEV, rep_rows, LANES), F32)]
    for lead in (4, 3, 3):
        scratch += [pltpu.VMEM((lead,) + tuple(shape), dt) for shape, dt in spec]
    scratch += [pltpu.SemaphoreType.DMA((7 * n,)), pltpu.SemaphoreType.DMA((7 * n,)),
                pltpu.SemaphoreType.DMA((14,)), pltpu.SemaphoreType.DMA((14,))]
    return pl.pallas_call(
        body, name="dx_tail", grid=(SEQ // tm,),
        in_specs=[row(dh.shape[1]) for dh in dhs] + [full(w.shape) for w in ws]
        + [row(D_MODEL), full(dwc.shape), full(p_uq.shape), full(p_rep.shape)],
        out_specs=[row(D_MODEL), full((N_DEV, c_rows, C_NAT)), full(p_uq.shape[1:]), full((N_DEV, rep_rows, LANES))],
        out_shape=[jax.ShapeDtypeStruct((SEQ, D_MODEL), F32), jax.ShapeDtypeStruct((N_DEV, c_rows, C_NAT), BF16),
                   jax.ShapeDtypeStruct(p_uq.shape[1:], F32), jax.ShapeDtypeStruct((N_DEV, rep_rows, LANES), F32)],
        scratch_shapes=scratch,
        compiler_params=pltpu.CompilerParams(dimension_semantics=("arbitrary",), vmem_limit_bytes=VMEM_BIG),
    )(*dhs, *ws, dx_res, dwc, p_uq, p_rep)


def _sum_landed(landed, c_all):
    c_rows = D_MODEL // N_DEV

    def body(rin_ref, roa_ref, rob_ref, rout_ref, call_ref, gin_ref, goa_ref, gob_ref, gout_ref):
        def total(ref, sl):
            acc = ref[0, sl, :].astype(F32)
            for s in range(1, N_DEV):
                acc = acc + ref[s, sl, :].astype(F32)
            return acc

        x, y, c = _mesh_pos()
        dev0 = jnp.where(4 * x + 2 * y + c == 0, 1.0, 0.0)
        for j in range(N_DEV):
            sl = slice(c_rows * j, c_rows * (j + 1))
            tot = total(rin_ref, sl)
            gin_ref[0, sl, C_NAT:SHARD_W] = tot[:, C_NAT:SHARD_W]
            gin_ref[0, sl, 0:C_NAT] = tot[:, 0:C_NAT] + dev0 * call_ref[j].astype(F32)
        goa_ref[0] = total(roa_ref, slice(None))
        gob_ref[0] = total(rob_ref, slice(None))
        gout_ref[0] = total(rout_ref, slice(None))

    return pl.pallas_call(
        body, name="sum_landed",
        out_shape=[jax.ShapeDtypeStruct((1,) + r.shape[1:], F32) for r in landed],
        compiler_params=pltpu.CompilerParams(vmem_limit_bytes=VMEM_MID),
    )(*landed, c_all)


_O_CQ, _O_CKV, _O_KPE, _O_ZA, _O_U, _O_V, _O_ZB, _O_GA, _O_GB = 0, 384, 512, 544, 1056, 1568, 2080, 2592, 3616


def _to_segments(w):
    z = lambda n: jnp.zeros(w.shape[:-1] + (n,), w.dtype)
    seg_a = jnp.concatenate([w[..., _O_GA:_O_GB], w[..., _O_GB:IN_WIDTH], w[..., _O_ZA:_O_U]], axis=-1)
    seg_b = jnp.concatenate([w[..., _O_U:_O_V], w[..., _O_V:_O_ZB], w[..., _O_ZB:_O_GA]], axis=-1)
    seg_c = jnp.concatenate([w[..., _O_CQ:_O_CKV], z(CQ_PAD - Q_LORA_RANK), w[..., _O_CKV:_O_KPE],
                             z(ROPE_LO), w[..., _O_KPE:_O_ZA], z(LANES - ROPE_HI)], axis=-1)
    return seg_a, seg_b, seg_c


def _from_segments(seg_a, seg_b, seg_c):
    kpe0 = CQ_PAD + LANES + ROPE_LO
    return jnp.concatenate([
        seg_c[..., 0:Q_LORA_RANK], seg_c[..., CQ_PAD:CQ_PAD + LANES], seg_c[..., kpe0:kpe0 + QK_ROPE_DIM],
        seg_a[..., 2 * D_MODEL:SEG_A], seg_b, seg_a[..., 0:2 * D_MODEL]], axis=-1)


def kernel(x, positions, w_in, b_in, g_q, w_uq, g_kv, w_ukv, w_oa, sgu_ln_g, sgu_ln_b, w_s, b_s, w_ob, w_out, ln_g, ln_b, loss_target, m_w_in, m_b_in, m_g_q, m_w_uq, m_g_kv, m_w_ukv, m_w_oa, m_sgu_ln_g, m_sgu_ln_b, m_w_s, m_b_s, m_w_ob, m_w_out, m_ln_g, m_ln_b, v_w_in, v_b_in, v_g_q, v_w_uq, v_g_kv, v_w_ukv, v_w_oa, v_sgu_ln_g, v_sgu_ln_b, v_w_s, v_b_s, v_w_ob, v_w_out, v_ln_g, v_ln_b):
    w_uq2 = w_uq[0].reshape(Q_LORA_RANK // N_DEV, MLA_HEADS * QK_HEAD_DIM)
    inv_freq = ROPE_THETA ** (-jnp.arange(0, QK_ROPE_DIM, 2, dtype=F32) / QK_ROPE_DIM)
    invf_lane = jnp.concatenate([jnp.zeros((ROPE_LO,), F32), inv_freq, inv_freq,
                                 jnp.zeros((LANES - ROPE_HI,), F32)]).reshape(1, LANES)
    first = _gather_first(w_in, w_uq2, w_oa, w_ob, w_out, x[0], positions.reshape(SEQ, 1), invf_lane)
    partials = _local_step(x[0], loss_target[0], first, b_in, g_q, g_kv, w_ukv, sgu_ln_g, sgu_ln_b, w_s, b_s, ln_g, ln_b)
    weights = dict(w_in=w_in, b_in=b_in, g_q=g_q, w_uq=w_uq, g_kv=g_kv, w_ukv=w_ukv, w_oa=w_oa, sgu_ln_g=sgu_ln_g,
                   sgu_ln_b=sgu_ln_b, w_s=w_s, b_s=b_s, w_ob=w_ob, w_out=w_out, ln_g=ln_g, ln_b=ln_b)
    moms = dict(w_in=m_w_in, b_in=m_b_in, g_q=m_g_q, w_uq=m_w_uq, g_kv=m_g_kv, w_ukv=m_w_ukv, w_oa=m_w_oa,
                sgu_ln_g=m_sgu_ln_g, sgu_ln_b=m_sgu_ln_b, w_s=m_w_s, b_s=m_b_s, w_ob=m_w_ob, w_out=m_w_out,
                ln_g=m_ln_g, ln_b=m_ln_b)
    vars_ = dict(w_in=v_w_in, b_in=v_b_in, g_q=v_g_q, w_uq=v_w_uq, g_kv=v_g_kv, w_ukv=v_w_ukv, w_oa=v_w_oa,
                 sgu_ln_g=v_sgu_ln_g, sgu_ln_b=v_sgu_ln_b, w_s=v_w_s, b_s=v_b_s, w_ob=v_w_ob, w_out=v_w_out,
                 ln_g=v_ln_g, ln_b=v_ln_b)
    return _reduce_and_update(partials, weights, moms, vars_)


def _local_step(x2, tgt, first, b_in, g_q, g_kv, w_ukv, sgu_ln_g, sgu_ln_b, w_s, b_s, ln_g, ln_b):
    wc, wq, win_lo, win_hi, oa_b, ob_b, out_b, x_bf, xt_bf, c_t, sa_t, sb_t = first
    ba, bb, bc = _to_segments(b_in)
    w_ukv_bf = w_ukv[0].astype(BF16)
    wkn = jnp.pad(w_ukv_bf[:, :, :QK_NOPE_DIM], ((0, 0), (0, 0), (0, HEAD_PAD - QK_NOPE_DIM))).reshape(KV_LORA_RANK, -1)
    wv = jnp.pad(w_ukv_bf[:, :, QK_NOPE_DIM:], ((0, 0), (0, 0), (0, HEAD_PAD - V_HEAD_DIM))).reshape(KV_LORA_RANK, -1)
    gq = jnp.pad(g_q, ((0, 0), (0, CQ_PAD - Q_LORA_RANK)))
    bias_full = jnp.repeat(b_s[0].T, SGU_GROUP_DIM, axis=1)
    w_s3 = w_s[0]
    w_st3 = jnp.swapaxes(w_s3, 1, 2)

    h_c = _mm(x_bf, wc, bias=bc, tm=512, tn=SEG_C, name="in_proj_c")
    q, k, kt, vx, vxt = _mla_prep(h_c, gq, g_kv, wq, wkn, wv, c_t, sa_t, sb_t)
    o, lse, (g_lo, g_hi, g_oa, g_ob, g_out) = _attn_fwd(q, kt, vx, (win_lo, win_hi, oa_b, ob_b, out_b))
    wa, wb = _assemble_in(g_lo, g_hi)
    h_a = _mm(x_bf, wa, bias=ba, tm=512, tn=SEG_A // 2, name="in_proj_a")
    h_b = _mm(x_bf, wb, bias=bb, tm=512, tn=SEG_B // 2, name="in_proj_b")
    y_b = _sgu_fwd(h_b, sgu_ln_g, sgu_ln_b, w_s3, bias_full)
    w_oa_f, w_ob_f, w_out_f = _assemble_out(g_oa, g_ob, g_out)

    (loss_row, dx_res, dh_a, d_o, d_yb, p_oa, p_ob, p_out, d_lng, d_lnb, d_ba) = _merge(
        x2, o, h_a, y_b, tgt, w_oa_f, w_ob_f, w_out_f, ln_g, ln_b)
    (dh_b, d_ws, d_bs_t, d_slg, d_slb, d_bb), (r_out,) = _sgu_bwd(h_b, d_yb, sgu_ln_g, sgu_ln_b, w_s3, w_st3, bias_full,
                                                                 (p_out,))
    d_wa, (r_oa,) = _mm(xt_bf, dh_a, out_dtype=BF16, parts=(p_oa,), tm=512, tn=512, name="dw_in_a")
    d_wb, (r_ob,) = _mm(xt_bf, dh_b, out_dtype=BF16, parts=(p_ob,), tm=512, tn=512, name="dw_in_b")
    dq, dk, dv, landed_in = _attn_bwd(q, kt, k, vxt, d_o, o, lse, (_to_parts(d_wa, d_wb),))
    landed = (*landed_in, r_oa, r_ob, r_out)
    dh_c, p_uq, d_wkn, d_wv, d_gq, d_gkv, d_bc = _mla_bwd(dq, dk, dv, h_c, gq, g_kv, wq, wkn, wv, c_t, sa_t, sb_t)
    d_wc = _mm(xt_bf, dh_c, out_dtype=BF16, tm=512, tn=SEG_C, name="dw_in_c")


    p_b_in = _from_segments(d_ba, d_bb, d_bc)
    p_w_ukv = jnp.concatenate([d_wkn.reshape(KV_LORA_RANK, MLA_HEADS, HEAD_PAD)[:, :, :QK_NOPE_DIM],
                               d_wv.reshape(KV_LORA_RANK, MLA_HEADS, HEAD_PAD)[:, :, :V_HEAD_DIM]], axis=-1)
    p_g_q = d_gq[:, :Q_LORA_RANK]
    p_b_s = d_bs_t[:, :SGU_GROUPS].T
    replicated = [p_b_in, p_g_q, d_gkv, p_w_ukv, d_slg, d_slb, d_ws, p_b_s, d_lng, d_lnb]
    return loss_row, ((dh_a, dh_b, dh_c), (wa, wb, wc), dx_res), landed, d_wc, p_uq, replicated


_NAMES = ["w_in", "b_in", "g_q", "w_uq", "g_kv", "w_ukv", "w_oa", "sgu_ln_g", "sgu_ln_b", "w_s", "b_s", "w_ob",
          "w_out", "ln_g", "ln_b"]
_REPLICATED = ["b_in", "g_q", "g_kv", "w_ukv", "sgu_ln_g", "sgu_ln_b", "w_s", "b_s", "ln_g", "ln_b"]


def _reduce_and_update(partials, weights, moms, vars_):
    loss_row, (dhs, ws, dx_res), landed, d_wc, p_uq, replicated = partials
    rep_flat = jnp.concatenate([a.reshape(-1) for a in replicated] + [loss_row[0, :1]])
    rep_flat = jnp.pad(rep_flat, (0, N_DEV * PACK_R_ROWS * LANES - rep_flat.size))
    dx_ab, c_all, g_uq, rep_all = _dx_tail(dhs[:2], ws[:2], dx_res, d_wc, p_uq,
                                           rep_flat.reshape(N_DEV, PACK_R_ROWS, LANES))
    dx = _mm(dhs[2], ws[2], tb=True, add=dx_ab, tm=512, tn=D_MODEL, name="dx_c")
    g_in, g_oa, g_ob, g_out = _sum_landed(landed, c_all)
    rep_sum = rep_all.reshape(-1)
    grads, pos = dict(w_in=g_in, w_uq=g_uq, w_oa=g_oa, w_ob=g_ob, w_out=g_out), 0
    for nm in _REPLICATED:
        grads[nm] = rep_sum[pos:pos + weights[nm].size]
        pos += weights[nm].size
    loss = rep_sum[pos]
    grads = {nm: grads[nm].reshape(weights[nm].shape) for nm in _NAMES}
    deltas, new_m, new_v = _adamw_all([weights[nm] for nm in _NAMES], [grads[nm] for nm in _NAMES],
                                      [moms[nm] for nm in _NAMES], [vars_[nm] for nm in _NAMES])
    return (loss, dx.reshape(1, SEQ, D_MODEL), *[grads[nm] for nm in _NAMES], *deltas, *new_m, *new_v)
```

```python
import math

import jax
import jax.numpy as jnp
from jax import lax
from jax.experimental import pallas as pl
from jax.experimental.pallas import tpu as pltpu

F32 = jnp.float32
BF16 = jnp.bfloat16

D_MODEL = 1024
SEQ = 2048
N_DEV = 8
MLA_HEADS = 8
Q_LORA_RANK = 384
KV_LORA_RANK = 128
QK_NOPE_DIM = 64
QK_ROPE_DIM = 32
V_HEAD_DIM = 64
QK_HEAD_DIM = QK_NOPE_DIM + QK_ROPE_DIM
MLA_WIDTH = MLA_HEADS * V_HEAD_DIM
ROPE_THETA = 10000.0
SGU_GROUPS = 8
SGU_GROUP_DIM = 64
SGU_WIDTH = SGU_GROUPS * SGU_GROUP_DIM
CHUNK = 128
RMS_EPS = 1e-6
LN_EPS = 1e-5
DN_ALPHA = 2.0 ** 0.25
IN_WIDTH = 4640
ATTN_SCALE = QK_HEAD_DIM ** -0.5

ADAM_LR = 0.001
ADAM_B1 = 0.9
ADAM_B2 = 0.999
ADAM_EPS = 1e-08
ADAM_WD = 0.01
ADAM_STEP = 10

LANES = 128
HEAD_PAD = 128
ROPE_LO = QK_NOPE_DIM
ROPE_MID = ROPE_LO + QK_ROPE_DIM // 2
ROPE_HI = ROPE_LO + QK_ROPE_DIM
CQ_PAD = 512

SEG_A = 2560
SEG_B = 1536
SEG_C = 768

PACK_R_ROWS = 272
VMEM_BIG = 56 * 1024 * 1024
VMEM_MID = 40 * 1024 * 1024


def _sigmoid(x):
    return 1.0 / (1.0 + jnp.exp(-x))


def _gelu_and_grad(x):
    c0 = math.sqrt(2.0 / math.pi)
    x2 = x * x
    t = jnp.tanh(c0 * (x + 0.044715 * x * x2))
    g = 0.5 * x * (1.0 + t)
    dg = 0.5 * (1.0 + t) + 0.5 * x * (1.0 - t * t) * (c0 * (1.0 + 3.0 * 0.044715 * x2))
    return g, dg


def _dot(a, b, dims):
    return lax.dot_general(a, b, (dims, ((), ())), preferred_element_type=F32)


_NN = ((1,), (0,))
_NT = ((1,), (1,))
_TN = ((0,), (0,))


def _store_grad(dh_ref, db_ref, col, val):
    cols = slice(col, col + val.shape[1])
    dh_ref[:, cols] = val.astype(BF16)
    db_ref[:, cols] += jnp.sum(val, axis=0, keepdims=True)


def _mm(a, b, *, tb=False, bias=None, add=None, out_dtype=F32, own=(), parts=(), tm, tn, name):
    m, k = a.shape
    n = b.shape[0] if tb else b.shape[1]
    assert m % tm == 0 and n % tn == 0 and not (own and parts)
    dims = _NT if tb else _NN
    nown = len(own) + len(parts)
    nm = m // tm
    nsteps = (n // tn) * nm

    def body(*refs):
        a_ref, b_ref = refs[0], refs[1]
        pos = 2
        r = _dot(a_ref[...], b_ref[...], dims)
        if bias is not None:
            r = r + refs[pos][...]; pos += 1
        if add is not None:
            r = r + refs[pos][...]; pos += 1
        own_refs = refs[pos:pos + nown]; pos += nown
        refs[pos][...] = r.astype(out_dtype)
        if nown:
            gat_refs = refs[pos + 1:pos + 1 + nown]
            send_sems, recv_sems, local_sems = refs[pos + 1 + nown:]
            step = pl.program_id(0) * nm + pl.program_id(1)
            if own:
                _gather_behind(own_refs, gat_refs, send_sems, recv_sems, local_sems, step, nsteps - 2, nsteps - 1)
            else:
                exchange = _exchange_parts(own_refs, gat_refs, send_sems, recv_sems, local_sems)
                _exchange_start(step == 0, exchange)
                _exchange_finish(step == nsteps - 1, exchange)

    b_spec = pl.BlockSpec((tn, k), lambda j, i: (j, 0)) if tb else pl.BlockSpec((k, tn), lambda j, i: (0, j))
    in_specs, args = [pl.BlockSpec((tm, k), lambda j, i: (i, 0)), b_spec], [a, b]
    if bias is not None:
        in_specs.append(pl.BlockSpec((1, tn), lambda j, i: (0, j))); args.append(bias)
    if add is not None:
        in_specs.append(pl.BlockSpec((tm, tn), lambda j, i: (i, j))); args.append(add)
    hbm = pl.BlockSpec(memory_space=pl.ANY)
    res = pl.pallas_call(
        body, name=name, grid=(n // tn, nm), in_specs=in_specs + [hbm] * nown,
        out_specs=[pl.BlockSpec((tm, tn), lambda j, i: (i, j))] + [hbm] * nown,
        out_shape=[jax.ShapeDtypeStruct((m, n), out_dtype)]
        + [jax.ShapeDtypeStruct((N_DEV,) + o.shape, o.dtype) for o in own]
        + [jax.ShapeDtypeStruct(p.shape, p.dtype) for p in parts],
        scratch_shapes=_exchange_sems(nown) if nown else [],
        compiler_params=pltpu.CompilerParams(dimension_semantics=("arbitrary", "arbitrary"), vmem_limit_bytes=VMEM_BIG),
    )(*args, *own, *parts)
    return (res[0], res[1:]) if nown else res[0]


def _rope(x, c, sa, sb):
    return x * c + pltpu.roll(x, LANES - 16, 1) * sa + pltpu.roll(x, 16, 1) * sb


def _rope_t(dy, c, sa, sb):
    return dy * c + pltpu.roll(dy * sa, 16, 1) + pltpu.roll(dy * sb, LANES - 16, 1)


def _mla_prep(h_c, gq, gkv, wq, wkn, wvx, c_t, sa_t, sb_t):
    tm = 256
    hw = MLA_HEADS * HEAD_PAD

    def body(cq_ref, ckv_ref, kpe_ref, gq_ref, gkv_ref, wq_ref, wkn_ref, wvx_ref, c_ref, sa_ref, sb_ref,
             q_ref, k_ref, kt_ref, vx_ref, vxt_ref):
        c, sa, sb = c_ref[...], sa_ref[...], sb_ref[...]
        cq = cq_ref[...]
        rq = lax.rsqrt(jnp.sum(cq * cq, axis=1, keepdims=True) * (1.0 / Q_LORA_RANK) + RMS_EPS)
        cqn = ((cq * rq) * gq_ref[...]).astype(BF16)
        qall = _dot(cqn, wq_ref[...], _NN)
        for h in range(MLA_HEADS):
            sl = slice(HEAD_PAD * h, HEAD_PAD * (h + 1))
            q_ref[:, sl] = (_rope(qall[:, sl], c, sa, sb) * ATTN_SCALE).astype(BF16)
        ckv = ckv_ref[...]
        rkv = lax.rsqrt(jnp.sum(ckv * ckv, axis=1, keepdims=True) * (1.0 / KV_LORA_RANK) + RMS_EPS)
        ckvn = ((ckv * rkv) * gkv_ref[...]).astype(BF16)
        knall = _dot(ckvn, wkn_ref[...], _NN)
        vall = _dot(ckvn, wvx_ref[...], _NN)
        kper = _rope(kpe_ref[...], c, sa, sb)
        ones_half = (lax.broadcasted_iota(jnp.int32, (tm, HEAD_PAD), 1) >= V_HEAD_DIM).astype(F32)
        for h in range(MLA_HEADS):
            sl = slice(HEAD_PAD * h, HEAD_PAD * (h + 1))
            kh = knall[:, sl] + kper
            vh = vall[:, sl] + ones_half
            k_ref[:, sl] = kh.astype(BF16)
            kt_ref[sl, :] = kh.T.astype(BF16)
            vx_ref[:, sl] = vh.astype(BF16)
            vxt_ref[sl, :] = vh.T.astype(BF16)

    full = lambda shape: pl.BlockSpec(shape, lambda i: (0, 0))
    tab = pl.BlockSpec((tm, LANES), lambda i: (i, 0))
    row = pl.BlockSpec((tm, hw), lambda i: (i, 0))
    col = pl.BlockSpec((hw, tm), lambda i: (0, i))
    return pl.pallas_call(
        body, name="mla_prep", grid=(SEQ // tm,),
        in_specs=[pl.BlockSpec((tm, CQ_PAD), lambda i: (i, 0)),
                  pl.BlockSpec((tm, LANES), lambda i: (i, CQ_PAD // LANES)),
                  pl.BlockSpec((tm, LANES), lambda i: (i, CQ_PAD // LANES + 1)),
                  full((1, CQ_PAD)), full((1, KV_LORA_RANK)),
                  full((CQ_PAD, hw)), full((KV_LORA_RANK, hw)), full((KV_LORA_RANK, hw)), tab, tab, tab],
        out_specs=[row, row, col, row, col],
        out_shape=[jax.ShapeDtypeStruct((SEQ, hw), BF16), jax.ShapeDtypeStruct((SEQ, hw), BF16),
                   jax.ShapeDtypeStruct((hw, SEQ), BF16), jax.ShapeDtypeStruct((SEQ, hw), BF16),
                   jax.ShapeDtypeStruct((hw, SEQ), BF16)],
        compiler_params=pltpu.CompilerParams(dimension_semantics=("arbitrary",), vmem_limit_bytes=VMEM_MID),
    )(h_c, h_c, h_c, gq, gkv, wq, wkn, wvx, c_t, sa_t, sb_t)


ATT_T = 512
ATT_STRIP = 64


def _causal_width(r, rs, t):
    return min(t, LANES * (-(-(rs * (r + 1)) // LANES)))


def _attn_fwd(q, kt, vx, own):
    t, rs = ATT_T, ATT_STRIP
    nown = len(own)
    nq = SEQ // t
    nsteps = (MLA_HEADS // 2) * nq

    def body(q_ref, kt_ref, vx_ref, *rest):
        own_refs, (o_ref, l_ref), gat_refs = rest[:nown], rest[nown:nown + 2], rest[nown + 2:2 * nown + 2]
        s_scr, p_scr, m_scr, a_scr, acc_scr, send_sems, recv_sems, local_sems = rest[2 * nown + 2:]
        qi = pl.program_id(1)
        _gather_behind(own_refs, gat_refs, send_sems, recv_sems, local_sems, pl.program_id(0) * nq + qi,
                       nsteps - 2, nsteps - 1)
        lane = lax.broadcasted_iota(jnp.int32, (t, LANES), 1)
        m_scr[...] = jnp.full((2, t, LANES), -1e30, F32)
        acc_scr[...] = jnp.zeros((2, t, LANES), F32)

        def block(j, masked):
            off = pl.multiple_of(j * t, t)
            for a in range(2):
                sl = slice(HEAD_PAD * a, HEAD_PAD * (a + 1))
                s_scr[a] = _dot(q_ref[:, sl], kt_ref[sl, pl.ds(off, t)], _NN)
                for r in range(t // rs):
                    rows = slice(rs * r, rs * (r + 1))
                    w = _causal_width(r, rs, t) if masked else t
                    s = s_scr[a, rows, 0:w]
                    if masked:
                        rowi = lax.broadcasted_iota(jnp.int32, (rs, w), 0) + rs * r
                        coli = lax.broadcasted_iota(jnp.int32, (rs, w), 1)
                        s = jnp.where(coli <= rowi, s, -1e30)
                        if w < t:
                            p_scr[a, rows, w:t] = jnp.zeros((rs, t - w), BF16)
                    m_old = m_scr[a, rows, :]
                    m_new = jnp.maximum(m_old, jnp.max(s, axis=1, keepdims=True))
                    p_scr[a, rows, 0:w] = jnp.exp(s - m_new[:, :1]).astype(BF16)
                    a_scr[a, rows, :] = jnp.exp(m_old - m_new)
                    m_scr[a, rows, :] = m_new
                acc_scr[a] = acc_scr[a] * a_scr[a] + _dot(p_scr[a], vx_ref[pl.ds(off, t), sl], _NN)

        def step(j, carry):
            block(j, False)
            return carry
        lax.fori_loop(0, qi, step, 0)
        block(qi, True)
        res = []
        for a in range(2):
            acc = acc_scr[a]
            l = acc[:, V_HEAD_DIM:V_HEAD_DIM + 1]
            res.append((acc / l, m_scr[a] + jnp.log(l)))
        o_ref[...] = jnp.where(lane < V_HEAD_DIM, res[0][0], pltpu.roll(res[1][0], V_HEAD_DIM, 1))
        l_ref[...] = jnp.where(lane < V_HEAD_DIM, res[0][1], res[1][1])

    hbm = pl.BlockSpec(memory_space=pl.ANY)
    res = pl.pallas_call(
        body, name="attn_fwd", grid=(MLA_HEADS // 2, nq),
        in_specs=[pl.BlockSpec((t, 2 * HEAD_PAD), lambda p, i: (i, p)),
                  pl.BlockSpec((2 * HEAD_PAD, SEQ), lambda p, i: (p, 0)),
                  pl.BlockSpec((SEQ, 2 * HEAD_PAD), lambda p, i: (0, p))] + [hbm] * nown,
        out_specs=[pl.BlockSpec((t, LANES), lambda p, i: (i, p)),
                   pl.BlockSpec((t, LANES), lambda p, i: (i, p))] + [hbm] * nown,
        out_shape=[jax.ShapeDtypeStruct((SEQ, MLA_WIDTH), F32), jax.ShapeDtypeStruct((SEQ, MLA_WIDTH), F32)]
        + [jax.ShapeDtypeStruct((N_DEV,) + a.shape, a.dtype) for a in own],
        scratch_shapes=[pltpu.VMEM((2, t, t), F32), pltpu.VMEM((2, t, t), BF16), pltpu.VMEM((2, t, LANES), F32),
                        pltpu.VMEM((2, t, LANES), F32), pltpu.VMEM((2, t, LANES), F32)] + _exchange_sems(nown),
        compiler_params=pltpu.CompilerParams(dimension_semantics=("arbitrary", "arbitrary"), vmem_limit_bytes=VMEM_MID),
    )(q, kt, vx, *own)
    return res[0], res[1], res[2:]


def _exchange_parts(parts, lands, send_sems, recv_sems, local_sems):
    x, y, c = _mesh_pos()
    me = 4 * x + 2 * y + c
    peers = [(x, y, 1 - c), (1 - x, y, c), (x, 1 - y, c), (1 - x, 1 - y, c),
             (1 - x, y, 1 - c), (x, 1 - y, 1 - c), (1 - x, 1 - y, 1 - c)]
    remote, local = [], []
    for a, (part, land) in enumerate(zip(parts, lands)):
        for k, peer in enumerate(peers):
            t = 4 * peer[0] + 2 * peer[1] + peer[2]
            remote.append(_remote(part.at[t], land.at[me], send_sems, recv_sems, 7 * a + k, peer))
        local.append(pltpu.make_async_copy(part.at[me], land.at[me], local_sems.at[a]))
    return remote, local


def _exchange_start(first_step, exchange):
    remote, local = exchange

    @pl.when(first_step)
    def _():
        for cp in remote + local:
            cp.start()


def _exchange_finish(last_step, exchange):
    remote, local = exchange

    @pl.when(last_step)
    def _():
        for cp in remote:
            cp.wait_recv()
        for cp in remote:
            cp.wait_send()
        for cp in local:
            cp.wait()


def _exchange_sems(npart):
    return [pltpu.SemaphoreType.DMA((7 * npart,)), pltpu.SemaphoreType.DMA((7 * npart,)),
            pltpu.SemaphoreType.DMA((npart,))]


def _attn_bwd(q, kt, k, vxt, d_o, o, lse, parts):
    t, rs = ATT_T, ATT_STRIP
    nq = SEQ // t
    npart = len(parts)
    nsteps = MLA_HEADS // 2

    def body(q_ref, kt_ref, k_ref, vxt_ref, do_ref, o_ref, l_ref, *rest):
        part_refs, rest = rest[:npart], rest[npart:]
        dq_ref, dk_ref, dv_ref = rest[:3]
        land_refs, rest = rest[3:3 + npart], rest[3 + npart:]
        s_scr, dp_scr, p_scr, ds_scr, st_scr, send_sems, recv_sems, local_sems = rest
        exchange = _exchange_parts(part_refs, land_refs, send_sems, recv_sems, local_sems)
        _exchange_start(pl.program_id(0) == 0, exchange)
        dk_ref[...] = jnp.zeros_like(dk_ref)
        dv_ref[...] = jnp.zeros_like(dv_ref)
        lane = lax.broadcasted_iota(jnp.int32, (t, LANES), 1)

        def qtile(i, carry):
            ioff = pl.multiple_of(i * t, t)
            do_i = do_ref[pl.ds(ioff, t), :]
            o_i = o_ref[pl.ds(ioff, t), :]
            l_i = l_ref[pl.ds(ioff, t), :]
            for a in range(2):
                sl = slice(HEAD_PAD * a, HEAD_PAD * (a + 1))
                sel = (lane < V_HEAD_DIM) if a == 0 else (lane >= V_HEAD_DIM)
                doa = jnp.where(sel, do_i, 0.0)
                oa = o_i
                if a == 1:
                    doa = pltpu.roll(doa, V_HEAD_DIM, 1)
                    oa = pltpu.roll(o_i, V_HEAD_DIM, 1)
                st_scr[0] = jnp.broadcast_to(jnp.sum(doa * oa, axis=1, keepdims=True), (t, LANES))
                st_scr[1] = jnp.broadcast_to(l_i[:, V_HEAD_DIM * a:V_HEAD_DIM * a + 1], (t, LANES))
                doa_bf = doa.astype(BF16)
                qa = q_ref[pl.ds(ioff, t), sl]

                def block(j, masked, dq_acc, sl=sl, qa=qa, doa_bf=doa_bf):
                    joff = pl.multiple_of(j * t, t)
                    s_scr[...] = _dot(qa, kt_ref[sl, pl.ds(joff, t)], _NN)
                    dp_scr[...] = _dot(doa_bf, vxt_ref[sl, pl.ds(joff, t)], _NN)
                    for r in range(t // rs):
                        rows = slice(rs * r, rs * (r + 1))
                        w = _causal_width(r, rs, t) if masked else t
                        p = jnp.exp(s_scr[rows, 0:w] - st_scr[1, rows, :1])
                        if masked:
                            rowi = lax.broadcasted_iota(jnp.int32, (rs, w), 0) + rs * r
                            coli = lax.broadcasted_iota(jnp.int32, (rs, w), 1)
                            p = jnp.where(coli <= rowi, p, 0.0)
                            if w < t:
                                p_scr[rows, w:t] = jnp.zeros((rs, t - w), BF16)
                                ds_scr[rows, w:t] = jnp.zeros((rs, t - w), BF16)
                        p_scr[rows, 0:w] = p.astype(BF16)
                        ds_scr[rows, 0:w] = (p * (dp_scr[rows, 0:w] - st_scr[0, rows, :1])).astype(BF16)
                    dk_ref[pl.ds(joff, t), sl] += _dot(ds_scr[...], qa, _TN)
                    dv_ref[pl.ds(joff, t), sl] += _dot(p_scr[...], doa_bf, _TN)
                    return dq_acc + _dot(ds_scr[...], k_ref[pl.ds(joff, t), sl], _NN)

                dq_acc = lax.fori_loop(0, i, lambda j, acc: block(j, False, acc), jnp.zeros((t, HEAD_PAD), F32))
                dq_ref[pl.ds(ioff, t), sl] = block(i, True, dq_acc)
            return carry

        lax.fori_loop(0, nq, qtile, 0)
        _exchange_finish(pl.program_id(0) == nsteps - 1, exchange)

    hw = MLA_HEADS * HEAD_PAD
    wide = pl.BlockSpec((SEQ, 2 * HEAD_PAD), lambda p: (0, p))
    wide_t = pl.BlockSpec((2 * HEAD_PAD, SEQ), lambda p: (p, 0))
    narrow = pl.BlockSpec((SEQ, LANES), lambda p: (0, p))
    hbm = pl.BlockSpec(memory_space=pl.ANY)
    res = pl.pallas_call(
        body, name="attn_bwd", grid=(nsteps,),
        in_specs=[wide, wide_t, wide, wide_t, narrow, narrow, narrow] + [hbm] * npart,
        out_specs=[wide, wide, wide] + [hbm] * npart,
        out_shape=[jax.ShapeDtypeStruct((SEQ, hw), F32)] * 3 + [jax.ShapeDtypeStruct(p.shape, p.dtype) for p in parts],
        scratch_shapes=[pltpu.VMEM((t, t), F32), pltpu.VMEM((t, t), F32), pltpu.VMEM((t, t), BF16),
                        pltpu.VMEM((t, t), BF16), pltpu.VMEM((2, t, LANES), F32)] + _exchange_sems(npart),
        compiler_params=pltpu.CompilerParams(dimension_semantics=("arbitrary",), vmem_limit_bytes=VMEM_BIG),
    )(q, kt, k, vxt, d_o, o, lse, *parts)
    return res[0], res[1], res[2], res[3:]


def _sgu_math(u, v, zb, lg, lb, ws_ref, bias):
    ug, dug = _gelu_and_grad(u)
    vg, dvg = _gelu_and_grad(v)
    mu = jnp.mean(vg, axis=1, keepdims=True)
    xc = vg - mu
    rstd = lax.rsqrt(jnp.mean(xc * xc, axis=1, keepdims=True) + LN_EPS)
    xh = xc * rstd
    vn_bf = (xh * lg + lb).astype(BF16)
    grp = lax.broadcasted_iota(jnp.int32, (CHUNK, SGU_WIDTH), 1) // SGU_GROUP_DIM
    r_i = lax.broadcasted_iota(jnp.int32, (CHUNK, CHUNK), 0)
    c_i = lax.broadcasted_iota(jnp.int32, (CHUNK, CHUNK), 1)
    tri, tri_t = r_i >= c_i, r_i <= c_i
    mixed = bias
    for g in range(SGU_GROUPS):
        wt = jnp.where(tri, ws_ref[g], 0.0).astype(BF16)
        mixed = mixed + jnp.where(grp == g, _dot(wt, vn_bf, _NN), 0.0)
    sb = _sigmoid(zb)
    return ug, dug, dvg, rstd, xh, vn_bf, grp, tri, tri_t, mixed, sb


def _sgu_fwd(h_b, lg, lb, w_s, bias_full):
    def body(u_ref, v_ref, zb_ref, lg_ref, lb_ref, ws_ref, bias_ref, yb_ref):
        zb = zb_ref[...]
        ug, _, _, _, _, _, _, _, _, mixed, sb = _sgu_math(u_ref[...], v_ref[...], zb, lg_ref[...], lb_ref[...],
                                                       ws_ref, bias_ref[...])
        yb_ref[...] = (ug * mixed) * (zb * sb)

    blk = lambda c: pl.BlockSpec((CHUNK, SGU_WIDTH), lambda i, c=c: (i, c))
    full2 = lambda shape: pl.BlockSpec(shape, lambda i: (0, 0))
    return pl.pallas_call(
        body, name="sgu_fwd", grid=(SEQ // CHUNK,),
        in_specs=[blk(0), blk(1), blk(2), full2((1, SGU_WIDTH)), full2((1, SGU_WIDTH)),
                  pl.BlockSpec((SGU_GROUPS, CHUNK, CHUNK), lambda i: (0, 0, 0)), full2((CHUNK, SGU_WIDTH))],
        out_specs=pl.BlockSpec((CHUNK, SGU_WIDTH), lambda i: (i, 0)),
        out_shape=jax.ShapeDtypeStruct((SEQ, SGU_WIDTH), F32),
        compiler_params=pltpu.CompilerParams(dimension_semantics=("arbitrary",)),
    )(h_b, h_b, h_b, lg, lb, w_s, bias_full)


def _sgu_bwd(h_b, d_yb, lg, lb, w_s, w_st, bias_full, parts):
    nsteps = SEQ // CHUNK
    npart = len(parts)

    def body(u_ref, v_ref, zb_ref, dyb_ref, lg_ref, lb_ref, ws_ref, wst_ref, bias_ref, *rest):
        part_refs, rest = rest[:npart], rest[npart:]
        dhb_ref, dws_ref, dbs_ref, dlg_ref, dlb_ref, dbb_ref = rest[:6]
        land_refs, (dbias_acc, send_sems, recv_sems, local_sems) = rest[6:6 + npart], rest[6 + npart:]
        step = pl.program_id(0)
        exchange = _exchange_parts(part_refs, land_refs, send_sems, recv_sems, local_sems)
        _exchange_start(step == 0, exchange)

        @pl.when(step == 0)
        def _():
            dbb_ref[...] = jnp.zeros_like(dbb_ref)
            dws_ref[...] = jnp.zeros_like(dws_ref)
            dlg_ref[...] = jnp.zeros_like(dlg_ref)
            dlb_ref[...] = jnp.zeros_like(dlb_ref)
            dbias_acc[...] = jnp.zeros_like(dbias_acc)

        zb = zb_ref[...]
        lg = lg_ref[...]
        ug, dug, dvg, rstd, xh, vn_bf, grp, tri, tri_t, mixed, sb = _sgu_math(
            u_ref[...], v_ref[...], zb, lg, lb_ref[...], ws_ref, bias_ref[...])
        dyb = dyb_ref[...]
        dsgu = dyb * (zb * sb)
        dzb = dyb * (ug * mixed) * (sb * (1.0 + zb * (1.0 - sb)))
        du = dsgu * mixed * dug
        dmixed = dsgu * ug
        dbias_acc[...] += dmixed
        dvn = jnp.zeros((CHUNK, SGU_WIDTH), F32)
        for g in range(SGU_GROUPS):
            dm_g = jnp.where(grp == g, dmixed, 0.0).astype(BF16)
            wtt = jnp.where(tri_t, wst_ref[g], 0.0).astype(BF16)
            dvn = dvn + _dot(wtt, dm_g, _NN)
            dws_ref[g] += jnp.where(tri, _dot(dm_g, vn_bf, _NT), 0.0)
        dlg_ref[...] += jnp.sum(dvn * xh, axis=0, keepdims=True)
        dlb_ref[...] += jnp.sum(dvn, axis=0, keepdims=True)
        dxh = dvn * lg
        dvgel = rstd * (dxh - jnp.mean(dxh, axis=1, keepdims=True) - xh * jnp.mean(dxh * xh, axis=1, keepdims=True))
        _store_grad(dhb_ref, dbb_ref, 0, du)
        _store_grad(dhb_ref, dbb_ref, SGU_WIDTH, dvgel * dvg)
        _store_grad(dhb_ref, dbb_ref, 2 * SGU_WIDTH, dzb)

        @pl.when(step == nsteps - 1)
        def _():
            acc = dbias_acc[...]
            lane = lax.broadcasted_iota(jnp.int32, (CHUNK, LANES), 1)
            out = jnp.zeros((CHUNK, LANES), F32)
            for g in range(SGU_GROUPS):
                sg = jnp.sum(jnp.where(grp == g, acc, 0.0), axis=1, keepdims=True)
                out = jnp.where(lane == g, sg, out)
            dbs_ref[...] = out

        _exchange_finish(step == nsteps - 1, exchange)

    blk = lambda c: pl.BlockSpec((CHUNK, SGU_WIDTH), lambda i, c=c: (i, c))
    full2 = lambda shape: pl.BlockSpec(shape, lambda i: (0, 0))
    full3 = pl.BlockSpec((SGU_GROUPS, CHUNK, CHUNK), lambda i: (0, 0, 0))
    hbm = pl.BlockSpec(memory_space=pl.ANY)
    res = pl.pallas_call(
        body, name="sgu_bwd", grid=(nsteps,),
        in_specs=[blk(0), blk(1), blk(2), pl.BlockSpec((CHUNK, SGU_WIDTH), lambda i: (i, 0)),
                  full2((1, SGU_WIDTH)), full2((1, SGU_WIDTH)), full3, full3, full2((CHUNK, SGU_WIDTH))] + [hbm] * npart,
        out_specs=[pl.BlockSpec((CHUNK, SEG_B), lambda i: (i, 0)), full3, full2((CHUNK, LANES)),
                   full2((1, SGU_WIDTH)), full2((1, SGU_WIDTH)), full2((1, SEG_B))] + [hbm] * npart,
        out_shape=[jax.ShapeDtypeStruct((SEQ, SEG_B), BF16),
                   jax.ShapeDtypeStruct((SGU_GROUPS, CHUNK, CHUNK), F32),
                   jax.ShapeDtypeStruct((CHUNK, LANES), F32),
                   jax.ShapeDtypeStruct((1, SGU_WIDTH), F32), jax.ShapeDtypeStruct((1, SGU_WIDTH), F32),
                   jax.ShapeDtypeStruct((1, SEG_B), F32)] + [jax.ShapeDtypeStruct(p.shape, p.dtype) for p in parts],
        scratch_shapes=[pltpu.VMEM((CHUNK, SGU_WIDTH), F32)] + _exchange_sems(npart),
        compiler_params=pltpu.CompilerParams(dimension_semantics=("arbitrary",)),
    )(h_b, h_b, h_b, d_yb, lg, lb, w_s, w_st, bias_full, *parts)
    return res[:6], res[6:]


def _merge(x, o, h_a, y_b, target, w_oa, w_ob, w_out, ln_g, ln_b):
    tm = 256
    nsteps = SEQ // tm

    def body(x_ref, o_ref, ga_ref, gb_ref, za_ref, yb_ref, tgt_ref, woa_ref, wob_ref, wout_ref, lng_ref, lnb_ref,
             loss_ref, dxr_ref, dha_ref, do_ref, dyb_ref, poa_ref, pob_ref, pout_ref, dlng_ref, dlnb_ref, dba_ref,
             dwoa_ref, dwob_ref, dwout_ref):
        step = pl.program_id(0)

        @pl.when(step == 0)
        def _():
            for r in (loss_ref, dwoa_ref, dwob_ref, dwout_ref, dlng_ref, dlnb_ref, dba_ref):
                r[...] = jnp.zeros_like(r)

        o = o_ref[...]
        za = za_ref[...]
        sa = _sigmoid(za)
        ya_bf = (o * (za * sa)).astype(BF16)
        yb_bf = yb_ref[...].astype(BF16)
        woa, wob, wout = woa_ref[...], wob_ref[...], wout_ref[...]
        pa = _dot(ya_bf, woa, _NN)
        pb = _dot(yb_bf, wob, _NN)
        sga = _sigmoid(ga_ref[...])
        sgb = _sigmoid(gb_ref[...])
        merged_bf = (sga * pa + sgb * pb).astype(BF16)
        r = DN_ALPHA * x_ref[...] + _dot(merged_bf, wout, _NN)
        mu = jnp.mean(r, axis=1, keepdims=True)
        rc = r - mu
        rstd = lax.rsqrt(jnp.mean(rc * rc, axis=1, keepdims=True) + LN_EPS)
        xh = rc * rstd
        lng = lng_ref[...]
        y = xh * lng + lnb_ref[...]
        e = y - tgt_ref[...]
        loss_ref[...] += 0.5 * jnp.sum(jnp.sum(e * e, axis=1, keepdims=True) * (1.0 / D_MODEL), axis=0, keepdims=True)

        dy = e * (1.0 / D_MODEL)
        dlng_ref[...] += jnp.sum(dy * xh, axis=0, keepdims=True)
        dlnb_ref[...] += jnp.sum(dy, axis=0, keepdims=True)
        dxh = dy * lng
        dr = rstd * (dxh - jnp.mean(dxh, axis=1, keepdims=True) - xh * jnp.mean(dxh * xh, axis=1, keepdims=True))
        dxr_ref[...] = DN_ALPHA * dr
        dr_bf = dr.astype(BF16)
        dwout_ref[...] += _dot(merged_bf, dr_bf, _TN)
        dmerged = _dot(dr_bf, wout, _NT)
        dpa_bf = (dmerged * sga).astype(BF16)
        dpb_bf = (dmerged * sgb).astype(BF16)
        _store_grad(dha_ref, dba_ref, 0, dmerged * pa * (sga * (1.0 - sga)))
        _store_grad(dha_ref, dba_ref, D_MODEL, dmerged * pb * (sgb * (1.0 - sgb)))
        dwoa_ref[...] += _dot(ya_bf, dpa_bf, _TN)
        dwob_ref[...] += _dot(yb_bf, dpb_bf, _TN)
        dya = _dot(dpa_bf, woa, _NT)
        dyb_ref[...] = _dot(dpb_bf, wob, _NT)
        do_ref[...] = dya * (za * sa)
        _store_grad(dha_ref, dba_ref, 2 * D_MODEL, dya * o * (sa * (1.0 + za * (1.0 - sa))))

        @pl.when(step == nsteps - 1)
        def _():
            cols = D_MODEL // N_DEV
            for j in range(N_DEV):
                poa_ref[j] = dwoa_ref[:, cols * j:cols * (j + 1)].astype(BF16)
                pob_ref[j] = dwob_ref[:, cols * j:cols * (j + 1)].astype(BF16)
                pout_ref[j] = dwout_ref[cols * j:cols * (j + 1), :].astype(BF16)

    row = lambda w, c=0: pl.BlockSpec((tm, w), lambda i, c=c: (i, c))
    full = lambda shape: pl.BlockSpec(shape, lambda i: (0, 0))
    full3 = lambda shape: pl.BlockSpec(shape, lambda i: (0, 0, 0))
    return pl.pallas_call(
        body, name="merge", grid=(nsteps,),
        in_specs=[row(D_MODEL), row(MLA_WIDTH), row(D_MODEL, 0), row(D_MODEL, 1), row(MLA_WIDTH, 4), row(SGU_WIDTH),
                  row(D_MODEL), full((MLA_WIDTH, D_MODEL)), full((SGU_WIDTH, D_MODEL)), full((D_MODEL, D_MODEL)),
                  full((1, D_MODEL)), full((1, D_MODEL))],
        out_specs=[full((1, LANES)), row(D_MODEL), row(SEG_A), row(MLA_WIDTH), row(SGU_WIDTH),
                   full3((N_DEV, MLA_WIDTH, D_MODEL // N_DEV)), full3((N_DEV, SGU_WIDTH, D_MODEL // N_DEV)),
                   full3((N_DEV, D_MODEL // N_DEV, D_MODEL)), full((1, D_MODEL)), full((1, D_MODEL)), full((1, SEG_A))],
        out_shape=[jax.ShapeDtypeStruct((1, LANES), F32),
                   jax.ShapeDtypeStruct((SEQ, D_MODEL), F32), jax.ShapeDtypeStruct((SEQ, SEG_A), BF16),
                   jax.ShapeDtypeStruct((SEQ, MLA_WIDTH), F32), jax.ShapeDtypeStruct((SEQ, SGU_WIDTH), F32),
                   jax.ShapeDtypeStruct((N_DEV, MLA_WIDTH, D_MODEL // N_DEV), BF16),
                   jax.ShapeDtypeStruct((N_DEV, SGU_WIDTH, D_MODEL // N_DEV), BF16),
                   jax.ShapeDtypeStruct((N_DEV, D_MODEL // N_DEV, D_MODEL), BF16),
                   jax.ShapeDtypeStruct((1, D_MODEL), F32), jax.ShapeDtypeStruct((1, D_MODEL), F32),
                   jax.ShapeDtypeStruct((1, SEG_A), F32)],
        scratch_shapes=[pltpu.VMEM((MLA_WIDTH, D_MODEL), F32), pltpu.VMEM((SGU_WIDTH, D_MODEL), F32),
                        pltpu.VMEM((D_MODEL, D_MODEL), F32)],
        compiler_params=pltpu.CompilerParams(dimension_semantics=("arbitrary",), vmem_limit_bytes=VMEM_BIG),
    )(x, o, h_a, h_a, h_a, y_b, target, w_oa, w_ob, w_out, ln_g, ln_b)


def _mla_bwd(dq, dk, dv, h_c, gq, gkv, wq, wkn, wv, c_t, sa_t, sb_t):
    tm = 256
    hw = MLA_HEADS * HEAD_PAD

    def body(dq_ref, dk_ref, dv_ref, cq_ref, ckv_ref, gq_ref, gkv_ref, wq_ref, wkn_ref, wv_ref, c_ref, sa_ref, sb_ref,
             dhc_ref, puq_ref, dwkn_ref, dwv_ref, dgq_ref, dgkv_ref, dbc_ref, pre_ref, dwq_ref):
        @pl.when(pl.program_id(0) == 0)
        def _():
            for r in (dwq_ref, dwkn_ref, dwv_ref, dgq_ref, dgkv_ref, dbc_ref):
                r[...] = jnp.zeros_like(r)

        c, sa, sb = c_ref[...], sa_ref[...], sb_ref[...]
        lane = lax.broadcasted_iota(jnp.int32, (tm, LANES), 1)
        rope_lanes = jnp.logical_and(lane >= ROPE_LO, lane < ROPE_HI)

        cq = cq_ref[...]
        gq = gq_ref[...]
        rq = lax.rsqrt(jnp.sum(cq * cq, axis=1, keepdims=True) * (1.0 / Q_LORA_RANK) + RMS_EPS)
        nq = cq * rq
        cqn_bf = (nq * gq).astype(BF16)
        for h in range(MLA_HEADS):
            sl = slice(HEAD_PAD * h, HEAD_PAD * (h + 1))
            pre_ref[:, sl] = _rope_t(dq_ref[:, sl] * ATTN_SCALE, c, sa, sb).astype(BF16)
        dqpre_bf = pre_ref[...]
        dcqn = _dot(dqpre_bf, wq_ref[...], _NT)
        dwq_ref[...] += _dot(cqn_bf, dqpre_bf, _TN)
        dgq_ref[...] += jnp.sum(dcqn * nq, axis=0, keepdims=True)
        dnq = dcqn * gq
        _store_grad(dhc_ref, dbc_ref, 0,
                    rq * (dnq - nq * (jnp.sum(dnq * nq, axis=1, keepdims=True) * (1.0 / Q_LORA_RANK))))

        ckv = ckv_ref[...]
        gkv = gkv_ref[...]
        rkv = lax.rsqrt(jnp.sum(ckv * ckv, axis=1, keepdims=True) * (1.0 / KV_LORA_RANK) + RMS_EPS)
        nkv = ckv * rkv
        ckvn_bf = (nkv * gkv).astype(BF16)
        dk = dk_ref[...]
        dk_bf = dk.astype(BF16)
        dv_bf = dv_ref[...].astype(BF16)
        dckvn = _dot(dk_bf, wkn_ref[...], _NT) + _dot(dv_bf, wv_ref[...], _NT)
        dwkn_ref[...] += _dot(ckvn_bf, dk_bf, _TN)
        dwv_ref[...] += _dot(ckvn_bf, dv_bf, _TN)
        dgkv_ref[...] += jnp.sum(dckvn * nkv, axis=0, keepdims=True)
        dnkv = dckvn * gkv
        _store_grad(dhc_ref, dbc_ref, CQ_PAD, rkv * (
            dnkv - nkv * (jnp.sum(dnkv * nkv, axis=1, keepdims=True) * (1.0 / KV_LORA_RANK))))
        dkpe = jnp.zeros((tm, LANES), F32)
        for h in range(MLA_HEADS):
            dkpe = dkpe + dk[:, HEAD_PAD * h:HEAD_PAD * (h + 1)]
        _store_grad(dhc_ref, dbc_ref, CQ_PAD + LANES, _rope_t(jnp.where(rope_lanes, dkpe, 0.0), c, sa, sb))

        @pl.when(pl.program_id(0) == SEQ // tm - 1)
        def _():
            rows = Q_LORA_RANK // N_DEV
            for j in range(N_DEV):
                for h in range(MLA_HEADS):
                    puq_ref[j, :, QK_HEAD_DIM * h:QK_HEAD_DIM * (h + 1)] = dwq_ref[
                        rows * j:rows * (j + 1), HEAD_PAD * h:HEAD_PAD * h + QK_HEAD_DIM].astype(BF16)

    full = lambda shape: pl.BlockSpec(shape, lambda i: (0, 0))
    row = lambda w, c=0: pl.BlockSpec((tm, w), lambda i, c=c: (i, c))
    return pl.pallas_call(
        body, name="mla_bwd", grid=(SEQ // tm,),
        in_specs=[row(hw), row(hw), row(hw), row(CQ_PAD, 0), row(LANES, CQ_PAD // LANES),
                  full((1, CQ_PAD)), full((1, KV_LORA_RANK)), full((CQ_PAD, hw)), full((KV_LORA_RANK, hw)),
                  full((KV_LORA_RANK, hw)), row(LANES), row(LANES), row(LANES)],
        out_specs=[row(SEG_C), pl.BlockSpec((N_DEV, Q_LORA_RANK // N_DEV, MLA_HEADS * QK_HEAD_DIM), lambda i: (0, 0, 0)),
                   full((KV_LORA_RANK, hw)), full((KV_LORA_RANK, hw)),
                   full((1, CQ_PAD)), full((1, KV_LORA_RANK)), full((1, SEG_C))],
        out_shape=[jax.ShapeDtypeStruct((SEQ, SEG_C), BF16),
                   jax.ShapeDtypeStruct((N_DEV, Q_LORA_RANK // N_DEV, MLA_HEADS * QK_HEAD_DIM), BF16),
                   jax.ShapeDtypeStruct((KV_LORA_RANK, hw), F32), jax.ShapeDtypeStruct((KV_LORA_RANK, hw), F32),
                   jax.ShapeDtypeStruct((1, CQ_PAD), F32), jax.ShapeDtypeStruct((1, KV_LORA_RANK), F32),
                   jax.ShapeDtypeStruct((1, SEG_C), F32)],
        scratch_shapes=[pltpu.VMEM((tm, hw), BF16), pltpu.VMEM((CQ_PAD, hw), F32)],
        compiler_params=pltpu.CompilerParams(dimension_semantics=("arbitrary",), vmem_limit_bytes=VMEM_MID),
    )(dq, dk, dv, h_c, h_c, gq, gkv, wq, wkn, wv, c_t, sa_t, sb_t)


def _adamw_all(ws, gs, ms, vs):
    n = len(ws)
    c1 = 1.0 / (1.0 - ADAM_B1 ** ADAM_STEP)
    c2 = 1.0 / (1.0 - ADAM_B2 ** ADAM_STEP)

    def body(*refs):
        for idx in range(n):
            w, g, m, v = (refs[idx][...], refs[n + idx][...], refs[2 * n + idx][...], refs[3 * n + idx][...])
            m_new = ADAM_B1 * m + (1.0 - ADAM_B1) * g
            v_new = ADAM_B2 * v + (1.0 - ADAM_B2) * (g * g)
            delta = -ADAM_LR * ((m_new * c1) / (jnp.sqrt(v_new * c2) + ADAM_EPS) + ADAM_WD * w)
            refs[4 * n + idx][...] = delta
            refs[5 * n + idx][...] = m_new
            refs[6 * n + idx][...] = v_new

    shapes = [jax.ShapeDtypeStruct(w.shape, F32) for w in ws]
    outs = pl.pallas_call(
        body, name="adamw", out_shape=shapes * 3,
        compiler_params=pltpu.CompilerParams(vmem_limit_bytes=VMEM_BIG),
    )(*ws, *gs, *ms, *vs)
    return outs[:n], outs[n:2 * n], outs[2 * n:]


SHARD_W = IN_WIDTH // N_DEV
W_IN_LO = 128

_PIECES = [(0, 384, 2, 0), (384, 512, 2, CQ_PAD), (512, 544, 2, CQ_PAD + LANES + ROPE_LO),
           (544, 1056, 0, 2 * D_MODEL), (1056, 1568, 1, 0), (1568, 2080, 1, SGU_WIDTH),
           (2080, 2592, 1, 2 * SGU_WIDTH), (2592, 3616, 0, 0), (3616, 4640, 0, D_MODEL)]


def _column_runs():
    runs = []
    for n0, n1, seg, d0 in _PIECES:
        for j in range(N_DEV):
            lo, hi = max(n0, j * SHARD_W), min(n1, (j + 1) * SHARD_W)
            if lo < hi:
                runs.append((j, lo - j * SHARD_W, hi - j * SHARD_W, seg, d0 + lo - n0))
    return runs


def _mesh_pos():
    return lax.axis_index("x"), lax.axis_index("y"), lax.axis_index("c")


def _remote(src, dst, send_sems, recv_sems, k, to):
    return pltpu.make_async_remote_copy(src_ref=src, dst_ref=dst, send_sem=send_sems.at[k], recv_sem=recv_sems.at[k],
                                        device_id=to, device_id_type=pl.DeviceIdType.MESH)


def _gather_exchange(gats, send_sems, recv_sems, meanwhile=None):
    x, y, c = _mesh_pos()
    me, sibling = (x, y, c), (x, y, 1 - c)
    chips = [(1 - x, y), (x, 1 - y), (1 - x, 1 - y)]

    def copy(a, k, blk, to):
        slab = gats[a].at[4 * blk[0] + 2 * blk[1] + blk[2]]
        return _remote(slab, slab, send_sems, recv_sems, 7 * a + k, to)

    arrays = range(len(gats))
    first = [copy(a, 1 + j, me, (*chip, c)) for j, chip in enumerate(chips) for a in arrays]
    first += [copy(a, 0, me, sibling) for a in arrays]
    for cp in first:
        cp.start()
    if meanwhile is not None:
        meanwhile()
    passed = []
    for j, chip in enumerate(chips):
        for a in arrays:
            copy(a, 1 + j, (*chip, c), me).wait_recv()
            fwd = copy(a, 4 + j, (*chip, c), sibling)
            fwd.start()
            passed.append(fwd)
    for a in arrays:
        copy(a, 0, sibling, me).wait_recv()
    for j, chip in enumerate(chips):
        for a in arrays:
            copy(a, 4 + j, (*chip, 1 - c), me).wait_recv()
    for cp in first + passed:
        cp.wait_send()


def _gather_behind(own, gats, send_sems, recv_sems, local_sems, step, mid, last):
    x, y, c = _mesh_pos()
    me, sibling = (x, y, c), (x, y, 1 - c)
    chips = [(1 - x, y), (x, 1 - y), (1 - x, 1 - y)]
    arrays = range(len(gats))

    def copy(a, k, blk, to, src=None):
        slab = gats[a].at[4 * blk[0] + 2 * blk[1] + blk[2]]
        return _remote(slab if src is None else src, slab, send_sems, recv_sems, 7 * a + k, to)

    first = [copy(a, 1 + j, me, (*chip, c), src=own[a]) for j, chip in enumerate(chips) for a in arrays]
    first += [copy(a, 0, me, sibling, src=own[a]) for a in arrays]
    local = [pltpu.make_async_copy(own[a], gats[a].at[4 * x + 2 * y + c], local_sems.at[a]) for a in arrays]
    passed = [copy(a, 4 + j, (*chip, c), sibling) for j, chip in enumerate(chips) for a in arrays]

    @pl.when(step == 0)
    def _():
        for cp in first + local:
            cp.start()

    @pl.when(step == mid)
    def _():
        for j, chip in enumerate(chips):
            for a in arrays:
                copy(a, 1 + j, (*chip, c), me).wait_recv()
        for cp in passed:
            cp.start()

    @pl.when(step == last)
    def _():
        for a in arrays:
            copy(a, 0, sibling, me).wait_recv()
        for j, chip in enumerate(chips):
            for a in arrays:
                copy(a, 4 + j, (*chip, 1 - c), me).wait_recv()
        for cp in first + passed:
            cp.wait_send()
        for cp in local:
            cp.wait()


def _gather_first(w_in, w_uq2, w_oa, w_ob, w_out, x2, pos_col, invf_lane):
    hw = MLA_HEADS * HEAD_PAD
    uq_rows = Q_LORA_RANK // N_DEV
    rows = 256

    def body(win_ref, wuq_ref, woa_ref, wob_ref, wout_ref, x_ref, pos_ref, invf_ref,
             wc_ref, wq_ref, winlo_ref, winhi_ref, oab_ref, obb_ref, outb_ref, xb_ref, xt_ref, c_ref, sa_ref, sb_ref,
             g_uq, blk0, send_sems, recv_sems):
        def local_work():
            for i in range(SEQ // rows):
                xi = x_ref[rows * i:rows * (i + 1), :]
                xb_ref[rows * i:rows * (i + 1), :] = xi.astype(BF16)
                xt_ref[:, rows * i:rows * (i + 1)] = xi.T.astype(BF16)
            ang = pos_ref[...].astype(F32) * invf_ref[...]
            cs, sn = jnp.cos(ang), jnp.sin(ang)
            lane = lax.broadcasted_iota(jnp.int32, ang.shape, 1)
            c_ref[...] = jnp.where(lane < ROPE_LO, 1.0, jnp.where(lane < ROPE_HI, cs, 0.0))
            sa_ref[...] = jnp.where(jnp.logical_and(lane >= ROPE_LO, lane < ROPE_MID), -sn, 0.0)
            sb_ref[...] = jnp.where(jnp.logical_and(lane >= ROPE_MID, lane < ROPE_HI), sn, 0.0)

        x, y, c = _mesh_pos()
        me = (x, y, c)
        winlo_ref[...] = win_ref[0, 0:W_IN_LO, :].astype(BF16)
        winhi_ref[...] = win_ref[0, W_IN_LO:D_MODEL, :].astype(BF16)
        oab_ref[...] = woa_ref[0].astype(BF16)
        obb_ref[...] = wob_ref[0].astype(BF16)
        outb_ref[...] = wout_ref[0].astype(BF16)
        g_uq[4 * x + 2 * y + c] = wuq_ref[...].astype(BF16)

        chip0 = jnp.logical_and(x == 0, y == 0)
        south = c == 0
        half = D_MODEL // 2
        halves = [blk0.at[pl.ds(0, half)], blk0.at[pl.ds(half, half)]]

        def bcopy(k, to, part=None):
            ref = blk0 if part is None else halves[part]
            return _remote(ref, ref, send_sems, recv_sems, 7 + k, to)

        sends0 = [(0, (0, 0, 1), None), (1, (1, 0, 0), 0), (2, (0, 1, 0), 1), (3, (1, 0, 0), 1), (4, (0, 1, 0), 0)]

        @pl.when(jnp.logical_and(chip0, south))
        def _():
            blk0[0:W_IN_LO, :] = winlo_ref[...]
            blk0[W_IN_LO:D_MODEL, :] = winhi_ref[...]
            for k, to, part in sends0:
                bcopy(k, to, part).start()

        _gather_exchange([g_uq], send_sems, recv_sems, meanwhile=local_work)

        for (cx, cy), first_k, first_half, second_k in (((1, 0), 1, 0, 3), ((0, 1), 2, 1, 4)):
            @pl.when(jnp.logical_and(jnp.logical_and(x == cx, y == cy), south))
            def _(cx=cx, cy=cy, first_k=first_k, first_half=first_half, second_k=second_k):
                bcopy(first_k, me, first_half).wait_recv()
                onward = bcopy(5 + first_half, (1, 1, 0), first_half)
                onward.start()
                bcopy(second_k, me, 1 - first_half).wait_recv()
                north = bcopy(7, (cx, cy, 1))
                north.start()
                onward.wait_send()
                north.wait_send()

        @pl.when(jnp.logical_and(jnp.logical_and(x == 1, y == 1), south))
        def _():
            bcopy(5, me, 0).wait_recv()
            bcopy(6, me, 1).wait_recv()
            north = bcopy(7, (1, 1, 1))
            north.start()
            north.wait_send()

        @pl.when(jnp.logical_and(chip0, c == 1))
        def _():
            bcopy(0, me).wait_recv()

        @pl.when(jnp.logical_and(jnp.logical_not(chip0), c == 1))
        def _():
            bcopy(7, me).wait_recv()

        @pl.when(jnp.logical_and(chip0, south))
        def _():
            for k, to, part in sends0:
                bcopy(k, to, part).wait_send()

        for j, s0, s1, seg, d0 in _column_runs():
            if seg == 2:
                wc_ref[:, d0:d0 + (s1 - s0)] = blk0[:, s0:s1]
        zeros = lambda r, w: jnp.zeros((r, w), BF16)
        wc_ref[:, Q_LORA_RANK:CQ_PAD] = zeros(D_MODEL, CQ_PAD - Q_LORA_RANK)
        wc_ref[:, CQ_PAD + LANES:CQ_PAD + LANES + ROPE_LO] = zeros(D_MODEL, ROPE_LO)
        wc_ref[:, CQ_PAD + LANES + ROPE_HI:SEG_C] = zeros(D_MODEL, LANES - ROPE_HI)
        wq_ref[Q_LORA_RANK:CQ_PAD, :] = zeros(CQ_PAD - Q_LORA_RANK, hw)
        for h in range(MLA_HEADS):
            wq_ref[0:Q_LORA_RANK, HEAD_PAD * h + QK_HEAD_DIM:HEAD_PAD * (h + 1)] = zeros(Q_LORA_RANK, HEAD_PAD - QK_HEAD_DIM)
        for j in range(N_DEV):
            for h in range(MLA_HEADS):
                wq_ref[uq_rows * j:uq_rows * (j + 1), HEAD_PAD * h:HEAD_PAD * h + QK_HEAD_DIM] = g_uq[
                    j, :, QK_HEAD_DIM * h:QK_HEAD_DIM * (h + 1)]

    vmem = pl.BlockSpec(memory_space=pltpu.VMEM)
    return pl.pallas_call(
        body, name="gather_first",
        out_shape=[jax.ShapeDtypeStruct((D_MODEL, SEG_C), BF16), jax.ShapeDtypeStruct((CQ_PAD, hw), BF16),
                   jax.ShapeDtypeStruct((W_IN_LO, SHARD_W), BF16), jax.ShapeDtypeStruct((D_MODEL - W_IN_LO, SHARD_W), BF16),
                   jax.ShapeDtypeStruct(w_oa.shape[1:], BF16),
                   jax.ShapeDtypeStruct(w_ob.shape[1:], BF16), jax.ShapeDtypeStruct(w_out.shape[1:], BF16),
                   jax.ShapeDtypeStruct((SEQ, D_MODEL), BF16), jax.ShapeDtypeStruct((D_MODEL, SEQ), BF16)]
        + [jax.ShapeDtypeStruct((SEQ, LANES), F32)] * 3,
        in_specs=[vmem] * 8, out_specs=[vmem] * 12,
        scratch_shapes=[pltpu.VMEM((N_DEV, uq_rows, MLA_HEADS * QK_HEAD_DIM), BF16), pltpu.VMEM((D_MODEL, SHARD_W), BF16),
                        pltpu.SemaphoreType.DMA((15,)), pltpu.SemaphoreType.DMA((15,))],
        compiler_params=pltpu.CompilerParams(vmem_limit_bytes=VMEM_BIG),
    )(w_in, w_uq2, w_oa, w_ob, w_out, x2, pos_col, invf_lane)


def _assemble_in(g_lo, g_hi):
    def body(glo_ref, ghi_ref, wa_ref, wb_ref):
        segs = [wa_ref, wb_ref]
        for j, s0, s1, seg, d0 in _column_runs():
            if seg < 2:
                segs[seg][0:W_IN_LO, d0:d0 + (s1 - s0)] = glo_ref[j, :, s0:s1]
                segs[seg][W_IN_LO:D_MODEL, d0:d0 + (s1 - s0)] = ghi_ref[j, :, s0:s1]

    return pl.pallas_call(
        body, name="assemble_in",
        out_shape=[jax.ShapeDtypeStruct((D_MODEL, SEG_A), BF16), jax.ShapeDtypeStruct((D_MODEL, SEG_B), BF16)],
        compiler_params=pltpu.CompilerParams(vmem_limit_bytes=VMEM_MID),
    )(g_lo, g_hi)


def _assemble_out(g_oa, g_ob, g_out):
    cols = D_MODEL // N_DEV

    def body(goa_ref, gob_ref, gout_ref, oa_ref, ob_ref, out_ref):
        for j in range(N_DEV):
            oa_ref[:, cols * j:cols * (j + 1)] = goa_ref[j]
            ob_ref[:, cols * j:cols * (j + 1)] = gob_ref[j]
            out_ref[cols * j:cols * (j + 1), :] = gout_ref[j]

    return pl.pallas_call(
        body, name="assemble_out",
        out_shape=[jax.ShapeDtypeStruct((MLA_WIDTH, D_MODEL), BF16), jax.ShapeDtypeStruct((SGU_WIDTH, D_MODEL), BF16),
                   jax.ShapeDtypeStruct((D_MODEL, D_MODEL), BF16)],
    )(g_oa, g_ob, g_out)


C_NAT = 544


def _to_parts(dwa, dwb):
    def body(dwa_ref, dwb_ref, pin_ref):
        pin_ref[0, :, 0:C_NAT] = jnp.zeros((D_MODEL, C_NAT), BF16)
        segs = [dwa_ref, dwb_ref]
        for j, s0, s1, seg, d0 in _column_runs():
            if seg < 2:
                pin_ref[j, :, s0:s1] = segs[seg][:, d0:d0 + (s1 - s0)]

    return pl.pallas_call(body, name="to_parts", out_shape=jax.ShapeDtypeStruct((N_DEV, D_MODEL, SHARD_W), BF16),
                          compiler_params=pltpu.CompilerParams(vmem_limit_bytes=VMEM_MID))(dwa, dwb)


def _dx_tail(dhs, ws, dx_res, dwc, p_uq, p_rep):
    tm = SEQ // 4
    rep_rows = p_rep.shape[1]
    c_rows = D_MODEL // N_DEV
    spec = [((c_rows, C_NAT), BF16), (p_uq.shape[1:], BF16), ((rep_rows, LANES), F32)]
    n = len(spec)

    nseg = len(dhs)

    def body(*refs):
        dh_refs, w_refs = refs[:nseg], refs[nseg:2 * nseg]
        dxr_ref, dwc_ref, puq_ref, prep_ref, dx_ref, call_ref, guq_ref, repall_ref, pc_ref, c_all, rep_all = refs[
            2 * nseg:2 * nseg + 11]
        rest = refs[2 * nseg + 11:]
        ras, tbs, rbs = rest[0:n], rest[n:2 * n], rest[2 * n:3 * n]
        send_sems, recv_sems, gsend, grecv = rest[3 * n:]
        step = pl.program_id(0)
        x, y, c = _mesh_pos()
        me_idx = 4 * x + 2 * y + c
        me, sibling = (x, y, c), (x, y, 1 - c)
        others = [(1 - x, y), (x, 1 - y), (1 - x, 1 - y)]
        parts = [pc_ref, puq_ref, prep_ref]
        gats = [rep_all, c_all]

        def stage1(chip, a):
            return _remote(parts[a].at[2 * chip + (1 - c)], ras[a].at[chip], send_sems, recv_sems, 7 * a + chip, sibling)

        def stage2(k, a):
            cx, cy = others[k]
            return _remote(tbs[a].at[k], rbs[a].at[k], send_sems, recv_sems, 7 * a + 4 + k, (cx, cy, c))

        def gcopy(a, k, blk, to):
            slab = gats[a].at[4 * blk[0] + 2 * blk[1] + blk[2]]
            return _remote(slab, slab, gsend, grecv, 7 * a + k, to)

        def chip_sum(a, chip):
            return parts[a][2 * chip + c].astype(F32) + ras[a][chip].astype(F32)

        @pl.when(step == 0)
        def _():
            for j, s0, s1, seg, d0 in _column_runs():
                if seg == 2:
                    for r in range(N_DEV):
                        pc_ref[r, :, s0:s1] = dwc_ref[c_rows * r:c_rows * (r + 1), d0:d0 + (s1 - s0)]
            for chip in range(4):
                for a in range(n):
                    stage1(chip, a).start()

        @pl.when(step == 1)
        def _():
            for chip in range(4):
                for a in range(n):
                    stage1(chip, a).wait_recv()
            for k, (cx, cy) in enumerate(others):
                for a in range(n):
                    tbs[a][k] = chip_sum(a, 2 * cx + cy).astype(spec[a][1])
                    stage2(k, a).start()

        @pl.when(step == 2)
        def _():
            for k in range(3):
                for a in range(n):
                    stage2(k, a).wait_recv()
            sums = []
            for a in range(n):
                acc = chip_sum(a, 2 * x + y)
                for k in range(3):
                    acc = acc + rbs[a][k].astype(F32)
                sums.append(acc)
            c_all[me_idx] = sums[0].astype(BF16)
            guq_ref[...] = sums[1]
            rep_all[me_idx] = sums[2]
            for a in range(2):
                for j, chip in enumerate(others):
                    gcopy(a, 1 + j, me, (*chip, c)).start()
                gcopy(a, 0, me, sibling).start()

        @pl.when(step == 3)
        def _():
            for j, chip in enumerate(others):
                for a in range(2):
                    gcopy(a, 1 + j, (*chip, c), me).wait_recv()
                    gcopy(a, 4 + j, (*chip, c), sibling).start()
            for a in range(2):
                gcopy(a, 0, sibling, me).wait_recv()
                for j, chip in enumerate(others):
                    gcopy(a, 4 + j, (*chip, 1 - c), me).wait_recv()
            for a in range(2):
                gcopy(a, 0, me, sibling).wait_send()
                for j, chip in enumerate(others):
                    gcopy(a, 1 + j, me, (*chip, c)).wait_send()
                    gcopy(a, 4 + j, (*chip, c), sibling).wait_send()
            for a in range(n):
                for chip in range(4):
                    stage1(chip, a).wait_send()
                for k in range(3):
                    stage2(k, a).wait_send()
            call_ref[...] = c_all[...]
            repall_ref[...] = rep_all[...]

        acc = dxr_ref[...]
        for dh_ref, w_ref in zip(dh_refs, w_refs):
            acc = acc + _dot(dh_ref[...], w_ref[...], _NT)
        dx_ref[...] = acc

    row = lambda w: pl.BlockSpec((tm, w), lambda i: (i, 0))
    full = lambda shape: pl.BlockSpec(shape, lambda i: (0,) * len(shape))
    scratch = [pltpu.VMEM((N_DEV, c_rows, C_NAT), BF16), pltpu.VMEM((N_DEV, c_rows, C_NAT), BF16),
               pltpu.VMEM((N_DEV, rep_rows, LANES), F32)]
    for lead in (4, 3, 3):
        scratch += [pltpu.VMEM((lead,) + tuple(shape), dt) for shape, dt in spec]
    scratch += [pltpu.SemaphoreType.DMA((7 * n,)), pltpu.SemaphoreType.DMA((7 * n,)),
                pltpu.SemaphoreType.DMA((14,)), pltpu.SemaphoreType.DMA((14,))]
    return pl.pallas_call(
        body, name="dx_tail", grid=(SEQ // tm,),
        in_specs=[row(dh.shape[1]) for dh in dhs] + [full(w.shape) for w in ws]
        + [row(D_MODEL), full(dwc.shape), full(p_uq.shape), full(p_rep.shape)],
        out_specs=[row(D_MODEL), full((N_DEV, c_rows, C_NAT)), full(p_uq.shape[1:]), full((N_DEV, rep_rows, LANES))],
        out_shape=[jax.ShapeDtypeStruct((SEQ, D_MODEL), F32), jax.ShapeDtypeStruct((N_DEV, c_rows, C_NAT), BF16),
                   jax.ShapeDtypeStruct(p_uq.shape[1:], F32), jax.ShapeDtypeStruct((N_DEV, rep_rows, LANES), F32)],
        scratch_shapes=scratch,
        compiler_params=pltpu.CompilerParams(dimension_semantics=("arbitrary",), vmem_limit_bytes=VMEM_BIG),
    )(*dhs, *ws, dx_res, dwc, p_uq, p_rep)


def _sum_landed(landed, c_all):
    c_rows = D_MODEL // N_DEV

    def body(rin_ref, roa_ref, rob_ref, rout_ref, call_ref, gin_ref, goa_ref, gob_ref, gout_ref):
        def total(ref, sl):
            acc = ref[0, sl, :].astype(F32)
            for s in range(1, N_DEV):
                acc = acc + ref[s, sl, :].astype(F32)
            return acc

        x, y, c = _mesh_pos()
        dev0 = jnp.where(4 * x + 2 * y + c == 0, 1.0, 0.0)
        for j in range(N_DEV):
            sl = slice(c_rows * j, c_rows * (j + 1))
            tot = total(rin_ref, sl)
            gin_ref[0, sl, C_NAT:SHARD_W] = tot[:, C_NAT:SHARD_W]
            gin_ref[0, sl, 0:C_NAT] = tot[:, 0:C_NAT] + dev0 * call_ref[j].astype(F32)
        goa_ref[0] = total(roa_ref, slice(None))
        gob_ref[0] = total(rob_ref, slice(None))
        gout_ref[0] = total(rout_ref, slice(None))

    return pl.pallas_call(
        body, name="sum_landed",
        out_shape=[jax.ShapeDtypeStruct((1,) + r.shape[1:], F32) for r in landed],
        compiler_params=pltpu.CompilerParams(vmem_limit_bytes=VMEM_MID),
    )(*landed, c_all)


_O_CQ, _O_CKV, _O_KPE, _O_ZA, _O_U, _O_V, _O_ZB, _O_GA, _O_GB = 0, 384, 512, 544, 1056, 1568, 2080, 2592, 3616


def _to_segments(w):
    z = lambda n: jnp.zeros(w.shape[:-1] + (n,), w.dtype)
    seg_a = jnp.concatenate([w[..., _O_GA:_O_GB], w[..., _O_GB:IN_WIDTH], w[..., _O_ZA:_O_U]], axis=-1)
    seg_b = jnp.concatenate([w[..., _O_U:_O_V], w[..., _O_V:_O_ZB], w[..., _O_ZB:_O_GA]], axis=-1)
    seg_c = jnp.concatenate([w[..., _O_CQ:_O_CKV], z(CQ_PAD - Q_LORA_RANK), w[..., _O_CKV:_O_KPE],
                             z(ROPE_LO), w[..., _O_KPE:_O_ZA], z(LANES - ROPE_HI)], axis=-1)
    return seg_a, seg_b, seg_c


def _from_segments(seg_a, seg_b, seg_c):
    kpe0 = CQ_PAD + LANES + ROPE_LO
    return jnp.concatenate([
        seg_c[..., 0:Q_LORA_RANK], seg_c[..., CQ_PAD:CQ_PAD + LANES], seg_c[..., kpe0:kpe0 + QK_ROPE_DIM],
        seg_a[..., 2 * D_MODEL:SEG_A], seg_b, seg_a[..., 0:2 * D_MODEL]], axis=-1)


def kernel(x, positions, w_in, b_in, g_q, w_uq, g_kv, w_ukv, w_oa, sgu_ln_g, sgu_ln_b, w_s, b_s, w_ob, w_out, ln_g, ln_b, loss_target, m_w_in, m_b_in, m_g_q, m_w_uq, m_g_kv, m_w_ukv, m_w_oa, m_sgu_ln_g, m_sgu_ln_b, m_w_s, m_b_s, m_w_ob, m_w_out, m_ln_g, m_ln_b, v_w_in, v_b_in, v_g_q, v_w_uq, v_g_kv, v_w_ukv, v_w_oa, v_sgu_ln_g, v_sgu_ln_b, v_w_s, v_b_s, v_w_ob, v_w_out, v_ln_g, v_ln_b):
    w_uq2 = w_uq[0].reshape(Q_LORA_RANK // N_DEV, MLA_HEADS * QK_HEAD_DIM)
    inv_freq = ROPE_THETA ** (-jnp.arange(0, QK_ROPE_DIM, 2, dtype=F32) / QK_ROPE_DIM)
    invf_lane = jnp.concatenate([jnp.zeros((ROPE_LO,), F32), inv_freq, inv_freq,
                                 jnp.zeros((LANES - ROPE_HI,), F32)]).reshape(1, LANES)
    first = _gather_first(w_in, w_uq2, w_oa, w_ob, w_out, x[0], positions.reshape(SEQ, 1), invf_lane)
    partials = _local_step(x[0], loss_target[0], first, b_in, g_q, g_kv, w_ukv, sgu_ln_g, sgu_ln_b, w_s, b_s, ln_g, ln_b)
    weights = dict(w_in=w_in, b_in=b_in, g_q=g_q, w_uq=w_uq, g_kv=g_kv, w_ukv=w_ukv, w_oa=w_oa, sgu_ln_g=sgu_ln_g,
                   sgu_ln_b=sgu_ln_b, w_s=w_s, b_s=b_s, w_ob=w_ob, w_out=w_out, ln_g=ln_g, ln_b=ln_b)
    moms = dict(w_in=m_w_in, b_in=m_b_in, g_q=m_g_q, w_uq=m_w_uq, g_kv=m_g_kv, w_ukv=m_w_ukv, w_oa=m_w_oa,
                sgu_ln_g=m_sgu_ln_g, sgu_ln_b=m_sgu_ln_b, w_s=m_w_s, b_s=m_b_s, w_ob=m_w_ob, w_out=m_w_out,
                ln_g=m_ln_g, ln_b=m_ln_b)
    vars_ = dict(w_in=v_w_in, b_in=v_b_in, g_q=v_g_q, w_uq=v_w_uq, g_kv=v_g_kv, w_ukv=v_w_ukv, w_oa=v_w_oa,
                 sgu_ln_g=v_sgu_ln_g, sgu_ln_b=v_sgu_ln_b, w_s=v_w_s, b_s=v_b_s, w_ob=v_w_ob, w_out=v_w_out,
                 ln_g=v_ln_g, ln_b=v_ln_b)
    return _reduce_and_update(partials, weights, moms, vars_)


def _local_step(x2, tgt, first, b_in, g_q, g_kv, w_ukv, sgu_ln_g, sgu_ln_b, w_s, b_s, ln_g, ln_b):
    wc, wq, win_lo, win_hi, oa_b, ob_b, out_b, x_bf, xt_bf, c_t, sa_t, sb_t = first
    ba, bb, bc = _to_segments(b_in)
    w_ukv_bf = w_ukv[0].astype(BF16)
    wkn = jnp.pad(w_ukv_bf[:, :, :QK_NOPE_DIM], ((0, 0), (0, 0), (0, HEAD_PAD - QK_NOPE_DIM))).reshape(KV_LORA_RANK, -1)
    wv = jnp.pad(w_ukv_bf[:, :, QK_NOPE_DIM:], ((0, 0), (0, 0), (0, HEAD_PAD - V_HEAD_DIM))).reshape(KV_LORA_RANK, -1)
    gq = jnp.pad(g_q, ((0, 0), (0, CQ_PAD - Q_LORA_RANK)))
    bias_full = jnp.repeat(b_s[0].T, SGU_GROUP_DIM, axis=1)
    w_s3 = w_s[0]
    w_st3 = jnp.swapaxes(w_s3, 1, 2)

    h_c = _mm(x_bf, wc, bias=bc, tm=512, tn=SEG_C, name="in_proj_c")
    q, k, kt, vx, vxt = _mla_prep(h_c, gq, g_kv, wq, wkn, wv, c_t, sa_t, sb_t)
    o, lse, (g_lo, g_hi) = _attn_fwd(q, kt, vx, (win_lo, win_hi))
    wa, wb = _assemble_in(g_lo, g_hi)
    h_a, (g_out,) = _mm(x_bf, wa, bias=ba, own=(out_b,), tm=512, tn=SEG_A // 2, name="in_proj_a")
    h_b, (g_oa, g_ob) = _mm(x_bf, wb, bias=bb, own=(oa_b, ob_b), tm=512, tn=SEG_B // 2, name="in_proj_b")
    y_b = _sgu_fwd(h_b, sgu_ln_g, sgu_ln_b, w_s3, bias_full)
    w_oa_f, w_ob_f, w_out_f = _assemble_out(g_oa, g_ob, g_out)

    (loss_row, dx_res, dh_a, d_o, d_yb, p_oa, p_ob, p_out, d_lng, d_lnb, d_ba) = _merge(
        x2, o, h_a, y_b, tgt, w_oa_f, w_ob_f, w_out_f, ln_g, ln_b)
    (dh_b, d_ws, d_bs_t, d_slg, d_slb, d_bb), (r_out,) = _sgu_bwd(h_b, d_yb, sgu_ln_g, sgu_ln_b, w_s3, w_st3, bias_full,
                                                                 (p_out,))
    d_wa, (r_oa,) = _mm(xt_bf, dh_a, out_dtype=BF16, parts=(p_oa,), tm=512, tn=512, name="dw_in_a")
    d_wb, (r_ob,) = _mm(xt_bf, dh_b, out_dtype=BF16, parts=(p_ob,), tm=512, tn=512, name="dw_in_b")
    dq, dk, dv, landed_in = _attn_bwd(q, kt, k, vxt, d_o, o, lse, (_to_parts(d_wa, d_wb),))
    landed = (*landed_in, r_oa, r_ob, r_out)
    dh_c, p_uq, d_wkn, d_wv, d_gq, d_gkv, d_bc = _mla_bwd(dq, dk, dv, h_c, gq, g_kv, wq, wkn, wv, c_t, sa_t, sb_t)
    d_wc = _mm(xt_bf, dh_c, out_dtype=BF16, tm=512, tn=SEG_C, name="dw_in_c")


    p_b_in = _from_segments(d_ba, d_bb, d_bc)
    p_w_ukv = jnp.concatenate([d_wkn.reshape(KV_LORA_RANK, MLA_HEADS, HEAD_PAD)[:, :, :QK_NOPE_DIM],
                               d_wv.reshape(KV_LORA_RANK, MLA_HEADS, HEAD_PAD)[:, :, :V_HEAD_DIM]], axis=-1)
    p_g_q = d_gq[:, :Q_LORA_RANK]
    p_b_s = d_bs_t[:, :SGU_GROUPS].T
    replicated = [p_b_in, p_g_q, d_gkv, p_w_ukv, d_slg, d_slb, d_ws, p_b_s, d_lng, d_lnb]
    return loss_row, ((dh_a, dh_b, dh_c), (wa, wb, wc), dx_res), landed, d_wc, p_uq, replicated


_NAMES = ["w_in", "b_in", "g_q", "w_uq", "g_kv", "w_ukv", "w_oa", "sgu_ln_g", "sgu_ln_b", "w_s", "b_s", "w_ob",
          "w_out", "ln_g", "ln_b"]
_REPLICATED = ["b_in", "g_q", "g_kv", "w_ukv", "sgu_ln_g", "sgu_ln_b", "w_s", "b_s", "ln_g", "ln_b"]


def _reduce_and_update(partials, weights, moms, vars_):
    loss_row, (dhs, ws, dx_res), landed, d_wc, p_uq, replicated = partials
    rep_flat = jnp.concatenate([a.reshape(-1) for a in replicated] + [loss_row[0, :1]])
    rep_flat = jnp.pad(rep_flat, (0, N_DEV * PACK_R_ROWS * LANES - rep_flat.size))
    dx_ab, c_all, g_uq, rep_all = _dx_tail(dhs[:2], ws[:2], dx_res, d_wc, p_uq,
                                           rep_flat.reshape(N_DEV, PACK_R_ROWS, LANES))
    dx = _mm(dhs[2], ws[2], tb=True, add=dx_ab, tm=512, tn=D_MODEL, name="dx_c")
    g_in, g_oa, g_ob, g_out = _sum_landed(landed, c_all)
    rep_sum = rep_all.reshape(-1)
    grads, pos = dict(w_in=g_in, w_uq=g_uq, w_oa=g_oa, w_ob=g_ob, w_out=g_out), 0
    for nm in _REPLICATED:
        grads[nm] = rep_sum[pos:pos + weights[nm].size]
        pos += weights[nm].size
    loss = rep_sum[pos]
    grads = {nm: grads[nm].reshape(weights[nm].shape) for nm in _NAMES}
    deltas, new_m, new_v = _adamw_all([weights[nm] for nm in _NAMES], [grads[nm] for nm in _NAMES],
                                      [moms[nm] for nm in _NAMES], [vars_[nm] for nm in _NAMES])
    return (loss, dx.reshape(1, SEQ, D_MODEL), *[grads[nm] for nm in _NAMES], *deltas, *new_m, *new_v)
```

```python
import math

import jax
import jax.numpy as jnp
from jax import lax
from jax.experimental import pallas as pl
from jax.experimental.pallas import tpu as pltpu

F32 = jnp.float32
BF16 = jnp.bfloat16

D_MODEL = 1024
SEQ = 2048
N_DEV = 8
MLA_HEADS = 8
Q_LORA_RANK = 384
KV_LORA_RANK = 128
QK_NOPE_DIM = 64
QK_ROPE_DIM = 32
V_HEAD_DIM = 64
QK_HEAD_DIM = QK_NOPE_DIM + QK_ROPE_DIM
MLA_WIDTH = MLA_HEADS * V_HEAD_DIM
ROPE_THETA = 10000.0
SGU_GROUPS = 8
SGU_GROUP_DIM = 64
SGU_WIDTH = SGU_GROUPS * SGU_GROUP_DIM
CHUNK = 128
RMS_EPS = 1e-6
LN_EPS = 1e-5
DN_ALPHA = 2.0 ** 0.25
IN_WIDTH = 4640
ATTN_SCALE = QK_HEAD_DIM ** -0.5

ADAM_LR = 0.001
ADAM_B1 = 0.9
ADAM_B2 = 0.999
ADAM_EPS = 1e-08
ADAM_WD = 0.01
ADAM_STEP = 10

LANES = 128
HEAD_PAD = 128
ROPE_LO = QK_NOPE_DIM
ROPE_MID = ROPE_LO + QK_ROPE_DIM // 2
ROPE_HI = ROPE_LO + QK_ROPE_DIM
CQ_PAD = 512

SEG_A = 2560
SEG_B = 1536
SEG_C = 768

PACK_R_ROWS = 272
VMEM_BIG = 56 * 1024 * 1024
VMEM_MID = 40 * 1024 * 1024


def _sigmoid(x):
    return 1.0 / (1.0 + jnp.exp(-x))


def _gelu_and_grad(x):
    c0 = math.sqrt(2.0 / math.pi)
    x2 = x * x
    t = jnp.tanh(c0 * (x + 0.044715 * x * x2))
    g = 0.5 * x * (1.0 + t)
    dg = 0.5 * (1.0 + t) + 0.5 * x * (1.0 - t * t) * (c0 * (1.0 + 3.0 * 0.044715 * x2))
    return g, dg


def _dot(a, b, dims):
    return lax.dot_general(a, b, (dims, ((), ())), preferred_element_type=F32)


_NN = ((1,), (0,))
_NT = ((1,), (1,))
_TN = ((0,), (0,))


def _store_grad(dh_ref, db_ref, col, val):
    cols = slice(col, col + val.shape[1])
    dh_ref[:, cols] = val.astype(BF16)
    db_ref[:, cols] += jnp.sum(val, axis=0, keepdims=True)


def _mm(a, b, *, tb=False, bias=None, add=None, out_dtype=F32, own=(), parts=(), tm, tn, name):
    m, k = a.shape
    n = b.shape[0] if tb else b.shape[1]
    assert m % tm == 0 and n % tn == 0 and not (own and parts)
    dims = _NT if tb else _NN
    nown = len(own) + len(parts)
    nm = m // tm
    nsteps = (n // tn) * nm

    def body(*refs):
        a_ref, b_ref = refs[0], refs[1]
        pos = 2
        r = _dot(a_ref[...], b_ref[...], dims)
        if bias is not None:
            r = r + refs[pos][...]; pos += 1
        if add is not None:
            r = r + refs[pos][...]; pos += 1
        own_refs = refs[pos:pos + nown]; pos += nown
        refs[pos][...] = r.astype(out_dtype)
        if nown:
            gat_refs = refs[pos + 1:pos + 1 + nown]
            send_sems, recv_sems, local_sems = refs[pos + 1 + nown:]
            step = pl.program_id(0) * nm + pl.program_id(1)
            if own:
                _gather_behind(own_refs, gat_refs, send_sems, recv_sems, local_sems, step, nsteps - 4, nsteps - 2,
                               nsteps - 1)
            else:
                exchange = _exchange_parts(own_refs, gat_refs, send_sems, recv_sems, local_sems)
                _exchange_start(step == 0, exchange)
                _exchange_finish(step == nsteps - 1, exchange)

    b_spec = pl.BlockSpec((tn, k), lambda j, i: (j, 0)) if tb else pl.BlockSpec((k, tn), lambda j, i: (0, j))
    in_specs, args = [pl.BlockSpec((tm, k), lambda j, i: (i, 0)), b_spec], [a, b]
    if bias is not None:
        in_specs.append(pl.BlockSpec((1, tn), lambda j, i: (0, j))); args.append(bias)
    if add is not None:
        in_specs.append(pl.BlockSpec((tm, tn), lambda j, i: (i, j))); args.append(add)
    hbm = pl.BlockSpec(memory_space=pl.ANY)
    res = pl.pallas_call(
        body, name=name, grid=(n // tn, nm), in_specs=in_specs + [hbm] * nown,
        out_specs=[pl.BlockSpec((tm, tn), lambda j, i: (i, j))] + [hbm] * nown,
        out_shape=[jax.ShapeDtypeStruct((m, n), out_dtype)]
        + [jax.ShapeDtypeStruct((N_DEV,) + o.shape, o.dtype) for o in own]
        + [jax.ShapeDtypeStruct(p.shape, p.dtype) for p in parts],
        scratch_shapes=_exchange_sems(nown) if nown else [],
        compiler_params=pltpu.CompilerParams(dimension_semantics=("arbitrary", "arbitrary"), vmem_limit_bytes=VMEM_BIG),
    )(*args, *own, *parts)
    return (res[0], res[1:]) if nown else res[0]


def _rope(x, c, sa, sb):
    return x * c + pltpu.roll(x, LANES - 16, 1) * sa + pltpu.roll(x, 16, 1) * sb


def _rope_t(dy, c, sa, sb):
    return dy * c + pltpu.roll(dy * sa, 16, 1) + pltpu.roll(dy * sb, LANES - 16, 1)


def _mla_prep(h_c, gq, gkv, wq, wkn, wvx, c_t, sa_t, sb_t):
    tm = 256
    hw = MLA_HEADS * HEAD_PAD

    def body(cq_ref, ckv_ref, kpe_ref, gq_ref, gkv_ref, wq_ref, wkn_ref, wvx_ref, c_ref, sa_ref, sb_ref,
             q_ref, k_ref, kt_ref, vx_ref, vxt_ref):
        c, sa, sb = c_ref[...], sa_ref[...], sb_ref[...]
        cq = cq_ref[...]
        rq = lax.rsqrt(jnp.sum(cq * cq, axis=1, keepdims=True) * (1.0 / Q_LORA_RANK) + RMS_EPS)
        cqn = ((cq * rq) * gq_ref[...]).astype(BF16)
        qall = _dot(cqn, wq_ref[...], _NN)
        for h in range(MLA_HEADS):
            sl = slice(HEAD_PAD * h, HEAD_PAD * (h + 1))
            q_ref[:, sl] = (_rope(qall[:, sl], c, sa, sb) * ATTN_SCALE).astype(BF16)
        ckv = ckv_ref[...]
        rkv = lax.rsqrt(jnp.sum(ckv * ckv, axis=1, keepdims=True) * (1.0 / KV_LORA_RANK) + RMS_EPS)
        ckvn = ((ckv * rkv) * gkv_ref[...]).astype(BF16)
        knall = _dot(ckvn, wkn_ref[...], _NN)
        vall = _dot(ckvn, wvx_ref[...], _NN)
        kper = _rope(kpe_ref[...], c, sa, sb)
        ones_half = (lax.broadcasted_iota(jnp.int32, (tm, HEAD_PAD), 1) >= V_HEAD_DIM).astype(F32)
        for h in range(MLA_HEADS):
            sl = slice(HEAD_PAD * h, HEAD_PAD * (h + 1))
            kh = knall[:, sl] + kper
            vh = vall[:, sl] + ones_half
            k_ref[:, sl] = kh.astype(BF16)
            kt_ref[sl, :] = kh.T.astype(BF16)
            vx_ref[:, sl] = vh.astype(BF16)
            vxt_ref[sl, :] = vh.T.astype(BF16)

    full = lambda shape: pl.BlockSpec(shape, lambda i: (0, 0))
    tab = pl.BlockSpec((tm, LANES), lambda i: (i, 0))
    row = pl.BlockSpec((tm, hw), lambda i: (i, 0))
    col = pl.BlockSpec((hw, tm), lambda i: (0, i))
    return pl.pallas_call(
        body, name="mla_prep", grid=(SEQ // tm,),
        in_specs=[pl.BlockSpec((tm, CQ_PAD), lambda i: (i, 0)),
                  pl.BlockSpec((tm, LANES), lambda i: (i, CQ_PAD // LANES)),
                  pl.BlockSpec((tm, LANES), lambda i: (i, CQ_PAD // LANES + 1)),
                  full((1, CQ_PAD)), full((1, KV_LORA_RANK)),
                  full((CQ_PAD, hw)), full((KV_LORA_RANK, hw)), full((KV_LORA_RANK, hw)), tab, tab, tab],
        out_specs=[row, row, col, row, col],
        out_shape=[jax.ShapeDtypeStruct((SEQ, hw), BF16), jax.ShapeDtypeStruct((SEQ, hw), BF16),
                   jax.ShapeDtypeStruct((hw, SEQ), BF16), jax.ShapeDtypeStruct((SEQ, hw), BF16),
                   jax.ShapeDtypeStruct((hw, SEQ), BF16)],
        compiler_params=pltpu.CompilerParams(dimension_semantics=("arbitrary",), vmem_limit_bytes=VMEM_MID),
    )(h_c, h_c, h_c, gq, gkv, wq, wkn, wvx, c_t, sa_t, sb_t)


ATT_T = 512
ATT_STRIP = 64


def _causal_width(r, rs, t):
    return min(t, LANES * (-(-(rs * (r + 1)) // LANES)))


def _attn_fwd(q, kt, vx, own):
    t, rs = ATT_T, ATT_STRIP
    nown = len(own)
    nq = SEQ // t
    nsteps = (MLA_HEADS // 2) * nq

    def body(q_ref, kt_ref, vx_ref, *rest):
        own_refs, (o_ref, l_ref), gat_refs = rest[:nown], rest[nown:nown + 2], rest[nown + 2:2 * nown + 2]
        s_scr, p_scr, m_scr, a_scr, acc_scr, send_sems, recv_sems, local_sems = rest[2 * nown + 2:]
        qi = pl.program_id(1)
        _gather_behind(own_refs, gat_refs, send_sems, recv_sems, local_sems, pl.program_id(0) * nq + qi,
                       nsteps - 6, nsteps - 2, nsteps - 1)
        lane = lax.broadcasted_iota(jnp.int32, (t, LANES), 1)
        m_scr[...] = jnp.full((2, t, LANES), -1e30, F32)
        acc_scr[...] = jnp.zeros((2, t, LANES), F32)

        def block(j, masked):
            off = pl.multiple_of(j * t, t)
            for a in range(2):
                sl = slice(HEAD_PAD * a, HEAD_PAD * (a + 1))
                s_scr[a] = _dot(q_ref[:, sl], kt_ref[sl, pl.ds(off, t)], _NN)
                for r in range(t // rs):
                    rows = slice(rs * r, rs * (r + 1))
                    w = _causal_width(r, rs, t) if masked else t
                    s = s_scr[a, rows, 0:w]
                    if masked:
                        rowi = lax.broadcasted_iota(jnp.int32, (rs, w), 0) + rs * r
                        coli = lax.broadcasted_iota(jnp.int32, (rs, w), 1)
                        s = jnp.where(coli <= rowi, s, -1e30)
                        if w < t:
                            p_scr[a, rows, w:t] = jnp.zeros((rs, t - w), BF16)
                    m_old = m_scr[a, rows, :]
                    m_new = jnp.maximum(m_old, jnp.max(s, axis=1, keepdims=True))
                    p_scr[a, rows, 0:w] = jnp.exp(s - m_new[:, :1]).astype(BF16)
                    a_scr[a, rows, :] = jnp.exp(m_old - m_new)
                    m_scr[a, rows, :] = m_new
                acc_scr[a] = acc_scr[a] * a_scr[a] + _dot(p_scr[a], vx_ref[pl.ds(off, t), sl], _NN)

        def step(j, carry):
            block(j, False)
            return carry
        lax.fori_loop(0, qi, step, 0)
        block(qi, True)
        res = []
        for a in range(2):
            acc = acc_scr[a]
            l = acc[:, V_HEAD_DIM:V_HEAD_DIM + 1]
            res.append((acc / l, m_scr[a] + jnp.log(l)))
        o_ref[...] = jnp.where(lane < V_HEAD_DIM, res[0][0], pltpu.roll(res[1][0], V_HEAD_DIM, 1))
        l_ref[...] = jnp.where(lane < V_HEAD_DIM, res[0][1], res[1][1])

    hbm = pl.BlockSpec(memory_space=pl.ANY)
    res = pl.pallas_call(
        body, name="attn_fwd", grid=(MLA_HEADS // 2, nq),
        in_specs=[pl.BlockSpec((t, 2 * HEAD_PAD), lambda p, i: (i, p)),
                  pl.BlockSpec((2 * HEAD_PAD, SEQ), lambda p, i: (p, 0)),
                  pl.BlockSpec((SEQ, 2 * HEAD_PAD), lambda p, i: (0, p))] + [hbm] * nown,
        out_specs=[pl.BlockSpec((t, LANES), lambda p, i: (i, p)),
                   pl.BlockSpec((t, LANES), lambda p, i: (i, p))] + [hbm] * nown,
        out_shape=[jax.ShapeDtypeStruct((SEQ, MLA_WIDTH), F32), jax.ShapeDtypeStruct((SEQ, MLA_WIDTH), F32)]
        + [jax.ShapeDtypeStruct((N_DEV,) + a.shape, a.dtype) for a in own],
        scratch_shapes=[pltpu.VMEM((2, t, t), F32), pltpu.VMEM((2, t, t), BF16), pltpu.VMEM((2, t, LANES), F32),
                        pltpu.VMEM((2, t, LANES), F32), pltpu.VMEM((2, t, LANES), F32)] + _exchange_sems(nown),
        compiler_params=pltpu.CompilerParams(dimension_semantics=("arbitrary", "arbitrary"), vmem_limit_bytes=VMEM_MID),
    )(q, kt, vx, *own)
    return res[0], res[1], res[2:]


def _exchange_parts(parts, lands, send_sems, recv_sems, local_sems):
    x, y, c = _mesh_pos()
    me = 4 * x + 2 * y + c
    peers = [(x, y, 1 - c), (1 - x, y, c), (x, 1 - y, c), (1 - x, 1 - y, c),
             (1 - x, y, 1 - c), (x, 1 - y, 1 - c), (1 - x, 1 - y, 1 - c)]
    remote, local = [], []
    for a, (part, land) in enumerate(zip(parts, lands)):
        for k, peer in enumerate(peers):
            t = 4 * peer[0] + 2 * peer[1] + peer[2]
            remote.append(_remote(part.at[t], land.at[me], send_sems, recv_sems, 7 * a + k, peer))
        local.append(pltpu.make_async_copy(part.at[me], land.at[me], local_sems.at[a]))
    return remote, local


def _exchange_start(first_step, exchange):
    remote, local = exchange

    @pl.when(first_step)
    def _():
        for cp in remote + local:
            cp.start()


def _exchange_finish(last_step, exchange):
    remote, local = exchange

    @pl.when(last_step)
    def _():
        for cp in remote:
            cp.wait_recv()
        for cp in remote:
            cp.wait_send()
        for cp in local:
            cp.wait()


def _exchange_sems(npart):
    return [pltpu.SemaphoreType.DMA((7 * npart,)), pltpu.SemaphoreType.DMA((7 * npart,)),
            pltpu.SemaphoreType.DMA((npart,))]


def _attn_bwd(q, kt, k, vxt, d_o, o, lse, parts):
    t, rs = ATT_T, ATT_STRIP
    nq = SEQ // t
    npart = len(parts)
    nsteps = MLA_HEADS // 2

    def body(q_ref, kt_ref, k_ref, vxt_ref, do_ref, o_ref, l_ref, *rest):
        part_refs, rest = rest[:npart], rest[npart:]
        dq_ref, dk_ref, dv_ref = rest[:3]
        land_refs, rest = rest[3:3 + npart], rest[3 + npart:]
        s_scr, dp_scr, p_scr, ds_scr, st_scr, send_sems, recv_sems, local_sems = rest
        exchange = _exchange_parts(part_refs, land_refs, send_sems, recv_sems, local_sems)
        _exchange_start(pl.program_id(0) == 0, exchange)
        dk_ref[...] = jnp.zeros_like(dk_ref)
        dv_ref[...] = jnp.zeros_like(dv_ref)
        lane = lax.broadcasted_iota(jnp.int32, (t, LANES), 1)

        def qtile(i, carry):
            ioff = pl.multiple_of(i * t, t)
            do_i = do_ref[pl.ds(ioff, t), :]
            o_i = o_ref[pl.ds(ioff, t), :]
            l_i = l_ref[pl.ds(ioff, t), :]
            for a in range(2):
                sl = slice(HEAD_PAD * a, HEAD_PAD * (a + 1))
                sel = (lane < V_HEAD_DIM) if a == 0 else (lane >= V_HEAD_DIM)
                doa = jnp.where(sel, do_i, 0.0)
                oa = o_i
                if a == 1:
                    doa = pltpu.roll(doa, V_HEAD_DIM, 1)
                    oa = pltpu.roll(o_i, V_HEAD_DIM, 1)
                st_scr[0] = jnp.broadcast_to(jnp.sum(doa * oa, axis=1, keepdims=True), (t, LANES))
                st_scr[1] = jnp.broadcast_to(l_i[:, V_HEAD_DIM * a:V_HEAD_DIM * a + 1], (t, LANES))
                doa_bf = doa.astype(BF16)
                qa = q_ref[pl.ds(ioff, t), sl]

                def block(j, masked, dq_acc, sl=sl, qa=qa, doa_bf=doa_bf):
                    joff = pl.multiple_of(j * t, t)
                    s_scr[...] = _dot(qa, kt_ref[sl, pl.ds(joff, t)], _NN)
                    dp_scr[...] = _dot(doa_bf, vxt_ref[sl, pl.ds(joff, t)], _NN)
                    for r in range(t // rs):
                        rows = slice(rs * r, rs * (r + 1))
                        w = _causal_width(r, rs, t) if masked else t
                        p = jnp.exp(s_scr[rows, 0:w] - st_scr[1, rows, :1])
                        if masked:
                            rowi = lax.broadcasted_iota(jnp.int32, (rs, w), 0) + rs * r
                            coli = lax.broadcasted_iota(jnp.int32, (rs, w), 1)
                            p = jnp.where(coli <= rowi, p, 0.0)
                            if w < t:
                                p_scr[rows, w:t] = jnp.zeros((rs, t - w), BF16)
                                ds_scr[rows, w:t] = jnp.zeros((rs, t - w), BF16)
                        p_scr[rows, 0:w] = p.astype(BF16)
                        ds_scr[rows, 0:w] = (p * (dp_scr[rows, 0:w] - st_scr[0, rows, :1])).astype(BF16)
                    dk_ref[pl.ds(joff, t), sl] += _dot(ds_scr[...], qa, _TN)
                    dv_ref[pl.ds(joff, t), sl] += _dot(p_scr[...], doa_bf, _TN)
                    return dq_acc + _dot(ds_scr[...], k_ref[pl.ds(joff, t), sl], _NN)

                dq_acc = lax.fori_loop(0, i, lambda j, acc: block(j, False, acc), jnp.zeros((t, HEAD_PAD), F32))
                dq_ref[pl.ds(ioff, t), sl] = block(i, True, dq_acc)
            return carry

        lax.fori_loop(0, nq, qtile, 0)
        _exchange_finish(pl.program_id(0) == nsteps - 1, exchange)

    hw = MLA_HEADS * HEAD_PAD
    wide = pl.BlockSpec((SEQ, 2 * HEAD_PAD), lambda p: (0, p))
    wide_t = pl.BlockSpec((2 * HEAD_PAD, SEQ), lambda p: (p, 0))
    narrow = pl.BlockSpec((SEQ, LANES), lambda p: (0, p))
    hbm = pl.BlockSpec(memory_space=pl.ANY)
    res = pl.pallas_call(
        body, name="attn_bwd", grid=(nsteps,),
        in_specs=[wide, wide_t, wide, wide_t, narrow, narrow, narrow] + [hbm] * npart,
        out_specs=[wide, wide, wide] + [hbm] * npart,
        out_shape=[jax.ShapeDtypeStruct((SEQ, hw), F32)] * 3 + [jax.ShapeDtypeStruct(p.shape, p.dtype) for p in parts],
        scratch_shapes=[pltpu.VMEM((t, t), F32), pltpu.VMEM((t, t), F32), pltpu.VMEM((t, t), BF16),
                        pltpu.VMEM((t, t), BF16), pltpu.VMEM((2, t, LANES), F32)] + _exchange_sems(npart),
        compiler_params=pltpu.CompilerParams(dimension_semantics=("arbitrary",), vmem_limit_bytes=VMEM_BIG),
    )(q, kt, k, vxt, d_o, o, lse, *parts)
    return res[0], res[1], res[2], res[3:]


def _sgu_math(u, v, zb, lg, lb, ws_ref, bias):
    ug, dug = _gelu_and_grad(u)
    vg, dvg = _gelu_and_grad(v)
    mu = jnp.mean(vg, axis=1, keepdims=True)
    xc = vg - mu
    rstd = lax.rsqrt(jnp.mean(xc * xc, axis=1, keepdims=True) + LN_EPS)
    xh = xc * rstd
    vn_bf = (xh * lg + lb).astype(BF16)
    grp = lax.broadcasted_iota(jnp.int32, (CHUNK, SGU_WIDTH), 1) // SGU_GROUP_DIM
    r_i = lax.broadcasted_iota(jnp.int32, (CHUNK, CHUNK), 0)
    c_i = lax.broadcasted_iota(jnp.int32, (CHUNK, CHUNK), 1)
    tri, tri_t = r_i >= c_i, r_i <= c_i
    mixed = bias
    for g in range(SGU_GROUPS):
        wt = jnp.where(tri, ws_ref[g], 0.0).astype(BF16)
        mixed = mixed + jnp.where(grp == g, _dot(wt, vn_bf, _NN), 0.0)
    sb = _sigmoid(zb)
    return ug, dug, dvg, rstd, xh, vn_bf, grp, tri, tri_t, mixed, sb


def _sgu_fwd(h_b, lg, lb, w_s, bias_full):
    def body(u_ref, v_ref, zb_ref, lg_ref, lb_ref, ws_ref, bias_ref, yb_ref):
        zb = zb_ref[...]
        ug, _, _, _, _, _, _, _, _, mixed, sb = _sgu_math(u_ref[...], v_ref[...], zb, lg_ref[...], lb_ref[...],
                                                       ws_ref, bias_ref[...])
        yb_ref[...] = (ug * mixed) * (zb * sb)

    blk = lambda c: pl.BlockSpec((CHUNK, SGU_WIDTH), lambda i, c=c: (i, c))
    full2 = lambda shape: pl.BlockSpec(shape, lambda i: (0, 0))
    return pl.pallas_call(
        body, name="sgu_fwd", grid=(SEQ // CHUNK,),
        in_specs=[blk(0), blk(1), blk(2), full2((1, SGU_WIDTH)), full2((1, SGU_WIDTH)),
                  pl.BlockSpec((SGU_GROUPS, CHUNK, CHUNK), lambda i: (0, 0, 0)), full2((CHUNK, SGU_WIDTH))],
        out_specs=pl.BlockSpec((CHUNK, SGU_WIDTH), lambda i: (i, 0)),
        out_shape=jax.ShapeDtypeStruct((SEQ, SGU_WIDTH), F32),
        compiler_params=pltpu.CompilerParams(dimension_semantics=("arbitrary",)),
    )(h_b, h_b, h_b, lg, lb, w_s, bias_full)


def _sgu_bwd(h_b, d_yb, lg, lb, w_s, w_st, bias_full, parts):
    nsteps = SEQ // CHUNK
    npart = len(parts)

    def body(u_ref, v_ref, zb_ref, dyb_ref, lg_ref, lb_ref, ws_ref, wst_ref, bias_ref, *rest):
        part_refs, rest = rest[:npart], rest[npart:]
        dhb_ref, dws_ref, dbs_ref, dlg_ref, dlb_ref, dbb_ref = rest[:6]
        land_refs, (dbias_acc, send_sems, recv_sems, local_sems) = rest[6:6 + npart], rest[6 + npart:]
        step = pl.program_id(0)
        exchange = _exchange_parts(part_refs, land_refs, send_sems, recv_sems, local_sems)
        _exchange_start(step == 0, exchange)

        @pl.when(step == 0)
        def _():
            dbb_ref[...] = jnp.zeros_like(dbb_ref)
            dws_ref[...] = jnp.zeros_like(dws_ref)
            dlg_ref[...] = jnp.zeros_like(dlg_ref)
            dlb_ref[...] = jnp.zeros_like(dlb_ref)
            dbias_acc[...] = jnp.zeros_like(dbias_acc)

        zb = zb_ref[...]
        lg = lg_ref[...]
        ug, dug, dvg, rstd, xh, vn_bf, grp, tri, tri_t, mixed, sb = _sgu_math(
            u_ref[...], v_ref[...], zb, lg, lb_ref[...], ws_ref, bias_ref[...])
        dyb = dyb_ref[...]
        dsgu = dyb * (zb * sb)
        dzb = dyb * (ug * mixed) * (sb * (1.0 + zb * (1.0 - sb)))
        du = dsgu * mixed * dug
        dmixed = dsgu * ug
        dbias_acc[...] += dmixed
        dvn = jnp.zeros((CHUNK, SGU_WIDTH), F32)
        for g in range(SGU_GROUPS):
            dm_g = jnp.where(grp == g, dmixed, 0.0).astype(BF16)
            wtt = jnp.where(tri_t, wst_ref[g], 0.0).astype(BF16)
            dvn = dvn + _dot(wtt, dm_g, _NN)
            dws_ref[g] += jnp.where(tri, _dot(dm_g, vn_bf, _NT), 0.0)
        dlg_ref[...] += jnp.sum(dvn * xh, axis=0, keepdims=True)
        dlb_ref[...] += jnp.sum(dvn, axis=0, keepdims=True)
        dxh = dvn * lg
        dvgel = rstd * (dxh - jnp.mean(dxh, axis=1, keepdims=True) - xh * jnp.mean(dxh * xh, axis=1, keepdims=True))
        _store_grad(dhb_ref, dbb_ref, 0, du)
        _store_grad(dhb_ref, dbb_ref, SGU_WIDTH, dvgel * dvg)
        _store_grad(dhb_ref, dbb_ref, 2 * SGU_WIDTH, dzb)

        @pl.when(step == nsteps - 1)
        def _():
            acc = dbias_acc[...]
            lane = lax.broadcasted_iota(jnp.int32, (CHUNK, LANES), 1)
            out = jnp.zeros((CHUNK, LANES), F32)
            for g in range(SGU_GROUPS):
                sg = jnp.sum(jnp.where(grp == g, acc, 0.0), axis=1, keepdims=True)
                out = jnp.where(lane == g, sg, out)
            dbs_ref[...] = out

        _exchange_finish(step == nsteps - 1, exchange)

    blk = lambda c: pl.BlockSpec((CHUNK, SGU_WIDTH), lambda i, c=c: (i, c))
    full2 = lambda shape: pl.BlockSpec(shape, lambda i: (0, 0))
    full3 = pl.BlockSpec((SGU_GROUPS, CHUNK, CHUNK), lambda i: (0, 0, 0))
    hbm = pl.BlockSpec(memory_space=pl.ANY)
    res = pl.pallas_call(
        body, name="sgu_bwd", grid=(nsteps,),
        in_specs=[blk(0), blk(1), blk(2), pl.BlockSpec((CHUNK, SGU_WIDTH), lambda i: (i, 0)),
                  full2((1, SGU_WIDTH)), full2((1, SGU_WIDTH)), full3, full3, full2((CHUNK, SGU_WIDTH))] + [hbm] * npart,
        out_specs=[pl.BlockSpec((CHUNK, SEG_B), lambda i: (i, 0)), full3, full2((CHUNK, LANES)),
                   full2((1, SGU_WIDTH)), full2((1, SGU_WIDTH)), full2((1, SEG_B))] + [hbm] * npart,
        out_shape=[jax.ShapeDtypeStruct((SEQ, SEG_B), BF16),
                   jax.ShapeDtypeStruct((SGU_GROUPS, CHUNK, CHUNK), F32),
                   jax.ShapeDtypeStruct((CHUNK, LANES), F32),
                   jax.ShapeDtypeStruct((1, SGU_WIDTH), F32), jax.ShapeDtypeStruct((1, SGU_WIDTH), F32),
                   jax.ShapeDtypeStruct((1, SEG_B), F32)] + [jax.ShapeDtypeStruct(p.shape, p.dtype) for p in parts],
        scratch_shapes=[pltpu.VMEM((CHUNK, SGU_WIDTH), F32)] + _exchange_sems(npart),
        compiler_params=pltpu.CompilerParams(dimension_semantics=("arbitrary",)),
    )(h_b, h_b, h_b, d_yb, lg, lb, w_s, w_st, bias_full, *parts)
    return res[:6], res[6:]


def _merge(x, o, h_a, y_b, target, w_oa, w_ob, w_out, ln_g, ln_b):
    tm = 256
    nsteps = SEQ // tm

    def body(x_ref, o_ref, ga_ref, gb_ref, za_ref, yb_ref, tgt_ref, woa_ref, wob_ref, wout_ref, lng_ref, lnb_ref,
             loss_ref, dxr_ref, dha_ref, do_ref, dyb_ref, poa_ref, pob_ref, pout_ref, dlng_ref, dlnb_ref, dba_ref,
             dwoa_ref, dwob_ref, dwout_ref):
        step = pl.program_id(0)

        @pl.when(step == 0)
        def _():
            for r in (loss_ref, dwoa_ref, dwob_ref, dwout_ref, dlng_ref, dlnb_ref, dba_ref):
                r[...] = jnp.zeros_like(r)

        o = o_ref[...]
        za = za_ref[...]
        sa = _sigmoid(za)
        ya_bf = (o * (za * sa)).astype(BF16)
        yb_bf = yb_ref[...].astype(BF16)
        woa, wob, wout = woa_ref[...], wob_ref[...], wout_ref[...]
        pa = _dot(ya_bf, woa, _NN)
        pb = _dot(yb_bf, wob, _NN)
        sga = _sigmoid(ga_ref[...])
        sgb = _sigmoid(gb_ref[...])
        merged_bf = (sga * pa + sgb * pb).astype(BF16)
        r = DN_ALPHA * x_ref[...] + _dot(merged_bf, wout, _NN)
        mu = jnp.mean(r, axis=1, keepdims=True)
        rc = r - mu
        rstd = lax.rsqrt(jnp.mean(rc * rc, axis=1, keepdims=True) + LN_EPS)
        xh = rc * rstd
        lng = lng_ref[...]
        y = xh * lng + lnb_ref[...]
        e = y - tgt_ref[...]
        loss_ref[...] += 0.5 * jnp.sum(jnp.sum(e * e, axis=1, keepdims=True) * (1.0 / D_MODEL), axis=0, keepdims=True)

        dy = e * (1.0 / D_MODEL)
        dlng_ref[...] += jnp.sum(dy * xh, axis=0, keepdims=True)
        dlnb_ref[...] += jnp.sum(dy, axis=0, keepdims=True)
        dxh = dy * lng
        dr = rstd * (dxh - jnp.mean(dxh, axis=1, keepdims=True) - xh * jnp.mean(dxh * xh, axis=1, keepdims=True))
        dxr_ref[...] = DN_ALPHA * dr
        dr_bf = dr.astype(BF16)
        dwout_ref[...] += _dot(merged_bf, dr_bf, _TN)
        dmerged = _dot(dr_bf, wout, _NT)
        dpa_bf = (dmerged * sga).astype(BF16)
        dpb_bf = (dmerged * sgb).astype(BF16)
        _store_grad(dha_ref, dba_ref, 0, dmerged * pa * (sga * (1.0 - sga)))
        _store_grad(dha_ref, dba_ref, D_MODEL, dmerged * pb * (sgb * (1.0 - sgb)))
        dwoa_ref[...] += _dot(ya_bf, dpa_bf, _TN)
        dwob_ref[...] += _dot(yb_bf, dpb_bf, _TN)
        dya = _dot(dpa_bf, woa, _NT)
        dyb_ref[...] = _dot(dpb_bf, wob, _NT)
        do_ref[...] = dya * (za * sa)
        _store_grad(dha_ref, dba_ref, 2 * D_MODEL, dya * o * (sa * (1.0 + za * (1.0 - sa))))

        @pl.when(step == nsteps - 1)
        def _():
            cols = D_MODEL // N_DEV
            for j in range(N_DEV):
                poa_ref[j] = dwoa_ref[:, cols * j:cols * (j + 1)].astype(BF16)
                pob_ref[j] = dwob_ref[:, cols * j:cols * (j + 1)].astype(BF16)
                pout_ref[j] = dwout_ref[cols * j:cols * (j + 1), :].astype(BF16)

    row = lambda w, c=0: pl.BlockSpec((tm, w), lambda i, c=c: (i, c))
    full = lambda shape: pl.BlockSpec(shape, lambda i: (0, 0))
    full3 = lambda shape: pl.BlockSpec(shape, lambda i: (0, 0, 0))
    return pl.pallas_call(
        body, name="merge", grid=(nsteps,),
        in_specs=[row(D_MODEL), row(MLA_WIDTH), row(D_MODEL, 0), row(D_MODEL, 1), row(MLA_WIDTH, 4), row(SGU_WIDTH),
                  row(D_MODEL), full((MLA_WIDTH, D_MODEL)), full((SGU_WIDTH, D_MODEL)), full((D_MODEL, D_MODEL)),
                  full((1, D_MODEL)), full((1, D_MODEL))],
        out_specs=[full((1, LANES)), row(D_MODEL), row(SEG_A), row(MLA_WIDTH), row(SGU_WIDTH),
                   full3((N_DEV, MLA_WIDTH, D_MODEL // N_DEV)), full3((N_DEV, SGU_WIDTH, D_MODEL // N_DEV)),
                   full3((N_DEV, D_MODEL // N_DEV, D_MODEL)), full((1, D_MODEL)), full((1, D_MODEL)), full((1, SEG_A))],
        out_shape=[jax.ShapeDtypeStruct((1, LANES), F32),
                   jax.ShapeDtypeStruct((SEQ, D_MODEL), F32), jax.ShapeDtypeStruct((SEQ, SEG_A), BF16),
                   jax.ShapeDtypeStruct((SEQ, MLA_WIDTH), F32), jax.ShapeDtypeStruct((SEQ, SGU_WIDTH), F32),
                   jax.ShapeDtypeStruct((N_DEV, MLA_WIDTH, D_MODEL // N_DEV), BF16),
                   jax.ShapeDtypeStruct((N_DEV, SGU_WIDTH, D_MODEL // N_DEV), BF16),
                   jax.ShapeDtypeStruct((N_DEV, D_MODEL // N_DEV, D_MODEL), BF16),
                   jax.ShapeDtypeStruct((1, D_MODEL), F32), jax.ShapeDtypeStruct((1, D_MODEL), F32),
                   jax.ShapeDtypeStruct((1, SEG_A), F32)],
        scratch_shapes=[pltpu.VMEM((MLA_WIDTH, D_MODEL), F32), pltpu.VMEM((SGU_WIDTH, D_MODEL), F32),
                        pltpu.VMEM((D_MODEL, D_MODEL), F32)],
        compiler_params=pltpu.CompilerParams(dimension_semantics=("arbitrary",), vmem_limit_bytes=VMEM_BIG),
    )(x, o, h_a, h_a, h_a, y_b, target, w_oa, w_ob, w_out, ln_g, ln_b)


def _mla_bwd(dq, dk, dv, h_c, gq, gkv, wq, wkn, wv, c_t, sa_t, sb_t):
    tm = 256
    hw = MLA_HEADS * HEAD_PAD

    def body(dq_ref, dk_ref, dv_ref, cq_ref, ckv_ref, gq_ref, gkv_ref, wq_ref, wkn_ref, wv_ref, c_ref, sa_ref, sb_ref,
             dhc_ref, puq_ref, dwkn_ref, dwv_ref, dgq_ref, dgkv_ref, dbc_ref, pre_ref, dwq_ref):
        @pl.when(pl.program_id(0) == 0)
        def _():
            for r in (dwq_ref, dwkn_ref, dwv_ref, dgq_ref, dgkv_ref, dbc_ref):
                r[...] = jnp.zeros_like(r)

        c, sa, sb = c_ref[...], sa_ref[...], sb_ref[...]
        lane = lax.broadcasted_iota(jnp.int32, (tm, LANES), 1)
        rope_lanes = jnp.logical_and(lane >= ROPE_LO, lane < ROPE_HI)

        cq = cq_ref[...]
        gq = gq_ref[...]
        rq = lax.rsqrt(jnp.sum(cq * cq, axis=1, keepdims=True) * (1.0 / Q_LORA_RANK) + RMS_EPS)
        nq = cq * rq
        cqn_bf = (nq * gq).astype(BF16)
        for h in range(MLA_HEADS):
            sl = slice(HEAD_PAD * h, HEAD_PAD * (h + 1))
            pre_ref[:, sl] = _rope_t(dq_ref[:, sl] * ATTN_SCALE, c, sa, sb).astype(BF16)
        dqpre_bf = pre_ref[...]
        dcqn = _dot(dqpre_bf, wq_ref[...], _NT)
        dwq_ref[...] += _dot(cqn_bf, dqpre_bf, _TN)
        dgq_ref[...] += jnp.sum(dcqn * nq, axis=0, keepdims=True)
        dnq = dcqn * gq
        _store_grad(dhc_ref, dbc_ref, 0,
                    rq * (dnq - nq * (jnp.sum(dnq * nq, axis=1, keepdims=True) * (1.0 / Q_LORA_RANK))))

        ckv = ckv_ref[...]
        gkv = gkv_ref[...]
        rkv = lax.rsqrt(jnp.sum(ckv * ckv, axis=1, keepdims=True) * (1.0 / KV_LORA_RANK) + RMS_EPS)
        nkv = ckv * rkv
        ckvn_bf = (nkv * gkv).astype(BF16)
        dk = dk_ref[...]
        dk_bf = dk.astype(BF16)
        dv_bf = dv_ref[...].astype(BF16)
        dckvn = _dot(dk_bf, wkn_ref[...], _NT) + _dot(dv_bf, wv_ref[...], _NT)
        dwkn_ref[...] += _dot(ckvn_bf, dk_bf, _TN)
        dwv_ref[...] += _dot(ckvn_bf, dv_bf, _TN)
        dgkv_ref[...] += jnp.sum(dckvn * nkv, axis=0, keepdims=True)
        dnkv = dckvn * gkv
        _store_grad(dhc_ref, dbc_ref, CQ_PAD, rkv * (
            dnkv - nkv * (jnp.sum(dnkv * nkv, axis=1, keepdims=True) * (1.0 / KV_LORA_RANK))))
        dkpe = jnp.zeros((tm, LANES), F32)
        for h in range(MLA_HEADS):
            dkpe = dkpe + dk[:, HEAD_PAD * h:HEAD_PAD * (h + 1)]
        _store_grad(dhc_ref, dbc_ref, CQ_PAD + LANES, _rope_t(jnp.where(rope_lanes, dkpe, 0.0), c, sa, sb))

        @pl.when(pl.program_id(0) == SEQ // tm - 1)
        def _():
            rows = Q_LORA_RANK // N_DEV
            for j in range(N_DEV):
                for h in range(MLA_HEADS):
                    puq_ref[j, :, QK_HEAD_DIM * h:QK_HEAD_DIM * (h + 1)] = dwq_ref[
                        rows * j:rows * (j + 1), HEAD_PAD * h:HEAD_PAD * h + QK_HEAD_DIM].astype(BF16)

    full = lambda shape: pl.BlockSpec(shape, lambda i: (0, 0))
    row = lambda w, c=0: pl.BlockSpec((tm, w), lambda i, c=c: (i, c))
    return pl.pallas_call(
        body, name="mla_bwd", grid=(SEQ // tm,),
        in_specs=[row(hw), row(hw), row(hw), row(CQ_PAD, 0), row(LANES, CQ_PAD // LANES),
                  full((1, CQ_PAD)), full((1, KV_LORA_RANK)), full((CQ_PAD, hw)), full((KV_LORA_RANK, hw)),
                  full((KV_LORA_RANK, hw)), row(LANES), row(LANES), row(LANES)],
        out_specs=[row(SEG_C), pl.BlockSpec((N_DEV, Q_LORA_RANK // N_DEV, MLA_HEADS * QK_HEAD_DIM), lambda i: (0, 0, 0)),
                   full((KV_LORA_RANK, hw)), full((KV_LORA_RANK, hw)),
                   full((1, CQ_PAD)), full((1, KV_LORA_RANK)), full((1, SEG_C))],
        out_shape=[jax.ShapeDtypeStruct((SEQ, SEG_C), BF16),
                   jax.ShapeDtypeStruct((N_DEV, Q_LORA_RANK // N_DEV, MLA_HEADS * QK_HEAD_DIM), BF16),
                   jax.ShapeDtypeStruct((KV_LORA_RANK, hw), F32), jax.ShapeDtypeStruct((KV_LORA_RANK, hw), F32),
                   jax.ShapeDtypeStruct((1, CQ_PAD), F32), jax.ShapeDtypeStruct((1, KV_LORA_RANK), F32),
                   jax.ShapeDtypeStruct((1, SEG_C), F32)],
        scratch_shapes=[pltpu.VMEM((tm, hw), BF16), pltpu.VMEM((CQ_PAD, hw), F32)],
        compiler_params=pltpu.CompilerParams(dimension_semantics=("arbitrary",), vmem_limit_bytes=VMEM_MID),
    )(dq, dk, dv, h_c, h_c, gq, gkv, wq, wkn, wv, c_t, sa_t, sb_t)


def _adamw_all(ws, gs, ms, vs):
    n = len(ws)
    c1 = 1.0 / (1.0 - ADAM_B1 ** ADAM_STEP)
    c2 = 1.0 / (1.0 - ADAM_B2 ** ADAM_STEP)

    def body(*refs):
        for idx in range(n):
            w, g, m, v = (refs[idx][...], refs[n + idx][...], refs[2 * n + idx][...], refs[3 * n + idx][...])
            m_new = ADAM_B1 * m + (1.0 - ADAM_B1) * g
            v_new = ADAM_B2 * v + (1.0 - ADAM_B2) * (g * g)
            delta = -ADAM_LR * ((m_new * c1) / (jnp.sqrt(v_new * c2) + ADAM_EPS) + ADAM_WD * w)
            refs[4 * n + idx][...] = delta
            refs[5 * n + idx][...] = m_new
            refs[6 * n + idx][...] = v_new

    shapes = [jax.ShapeDtypeStruct(w.shape, F32) for w in ws]
    outs = pl.pallas_call(
        body, name="adamw", out_shape=shapes * 3,
        compiler_params=pltpu.CompilerParams(vmem_limit_bytes=VMEM_BIG),
    )(*ws, *gs, *ms, *vs)
    return outs[:n], outs[n:2 * n], outs[2 * n:]


SHARD_W = IN_WIDTH // N_DEV
W_IN_LO = 128

_PIECES = [(0, 384, 2, 0), (384, 512, 2, CQ_PAD), (512, 544, 2, CQ_PAD + LANES + ROPE_LO),
           (544, 1056, 0, 2 * D_MODEL), (1056, 1568, 1, 0), (1568, 2080, 1, SGU_WIDTH),
           (2080, 2592, 1, 2 * SGU_WIDTH), (2592, 3616, 0, 0), (3616, 4640, 0, D_MODEL)]


def _column_runs():
    runs = []
    for n0, n1, seg, d0 in _PIECES:
        for j in range(N_DEV):
            lo, hi = max(n0, j * SHARD_W), min(n1, (j + 1) * SHARD_W)
            if lo < hi:
                runs.append((j, lo - j * SHARD_W, hi - j * SHARD_W, seg, d0 + lo - n0))
    return runs


def _mesh_pos():
    return lax.axis_index("x"), lax.axis_index("y"), lax.axis_index("c")


def _remote(src, dst, send_sems, recv_sems, k, to):
    return pltpu.make_async_remote_copy(src_ref=src, dst_ref=dst, send_sem=send_sems.at[k], recv_sem=recv_sems.at[k],
                                        device_id=to, device_id_type=pl.DeviceIdType.MESH)


def _gather_exchange(gats, send_sems, recv_sems, meanwhile=None):
    x, y, c = _mesh_pos()
    me, sibling = (x, y, c), (x, y, 1 - c)
    chips = [(1 - x, y), (x, 1 - y), (1 - x, 1 - y)]

    def copy(a, k, blk, to):
        slab = gats[a].at[4 * blk[0] + 2 * blk[1] + blk[2]]
        return _remote(slab, slab, send_sems, recv_sems, 7 * a + k, to)

    arrays = range(len(gats))
    first = [copy(a, 1 + j, me, (*chip, c)) for j, chip in enumerate(chips) for a in arrays]
    first += [copy(a, 0, me, sibling) for a in arrays]
    for cp in first:
        cp.start()
    if meanwhile is not None:
        meanwhile()
    passed = []
    for j, chip in enumerate(chips):
        for a in arrays:
            copy(a, 1 + j, (*chip, c), me).wait_recv()
            fwd = copy(a, 4 + j, (*chip, c), sibling)
            fwd.start()
            passed.append(fwd)
    for a in arrays:
        copy(a, 0, sibling, me).wait_recv()
    for j, chip in enumerate(chips):
        for a in arrays:
            copy(a, 4 + j, (*chip, 1 - c), me).wait_recv()
    for cp in first + passed:
        cp.wait_send()


def _gather_behind(own, gats, send_sems, recv_sems, local_sems, step, mid1, mid2, last):
    x, y, c = _mesh_pos()
    me, sibling = (x, y, c), (x, y, 1 - c)
    xn, yn, dg = (1 - x, y), (x, 1 - y), (1 - x, 1 - y)
    arrays = range(len(gats))

    def copy(a, k, blk, to, src=None):
        slab = gats[a].at[4 * blk[0] + 2 * blk[1] + blk[2]]
        return _remote(slab if src is None else src, slab, send_sems, recv_sems, 7 * a + k, to)

    first = [copy(a, 1 + j, me, (*chip, c), src=own[a]) for j, chip in enumerate((xn, yn)) for a in arrays]
    first += [copy(a, 0, me, sibling, src=own[a]) for a in arrays]
    local = [pltpu.make_async_copy(own[a], gats[a].at[4 * x + 2 * y + c], local_sems.at[a]) for a in arrays]
    passed = [copy(a, 4 + j, (*chip, c), sibling) for j, chip in enumerate((xn, yn)) for a in arrays]
    relays = [(c == 0, [copy(a, 3, (*xn, c), (*yn, c)) for a in arrays]),
              (c == 1, [copy(a, 3, (*yn, c), (*xn, c)) for a in arrays])]
    last_pass = [copy(a, 6, (*dg, c), sibling) for a in arrays]

    @pl.when(step == 0)
    def _():
        for cp in first + local:
            cp.start()

    @pl.when(step == mid1)
    def _():
        for j, chip in enumerate((xn, yn)):
            for a in arrays:
                copy(a, 1 + j, (*chip, c), me).wait_recv()
        for mine, cps in relays:
            @pl.when(mine)
            def _(cps=cps):
                for cp in cps:
                    cp.start()
        for cp in passed:
            cp.start()

    @pl.when(step == mid2)
    def _():
        for a in arrays:
            copy(a, 3, (*dg, c), me).wait_recv()
        for cp in last_pass:
            cp.start()

    @pl.when(step == last)
    def _():
        for a in arrays:
            copy(a, 0, sibling, me).wait_recv()
            for j, chip in enumerate((xn, yn, dg)):
                copy(a, 4 + j, (*chip, 1 - c), me).wait_recv()
        for cp in first + passed + last_pass:
            cp.wait_send()
        for mine, cps in relays:
            @pl.when(mine)
            def _(cps=cps):
                for cp in cps:
                    cp.wait_send()
        for cp in local:
            cp.wait()


def _gather_first(w_in, w_uq2, w_oa, w_ob, w_out, x2, pos_col, invf_lane):
    hw = MLA_HEADS * HEAD_PAD
    uq_rows = Q_LORA_RANK // N_DEV
    rows = 256

    def body(win_ref, wuq_ref, woa_ref, wob_ref, wout_ref, x_ref, pos_ref, invf_ref,
             wc_ref, wq_ref, winlo_ref, winhi_ref, oab_ref, obb_ref, outb_ref, xb_ref, xt_ref, c_ref, sa_ref, sb_ref,
             g_uq, blk0, send_sems, recv_sems):
        def local_work():
            for i in range(SEQ // rows):
                xi = x_ref[rows * i:rows * (i + 1), :]
                xb_ref[rows * i:rows * (i + 1), :] = xi.astype(BF16)
                xt_ref[:, rows * i:rows * (i + 1)] = xi.T.astype(BF16)
            ang = pos_ref[...].astype(F32) * invf_ref[...]
            cs, sn = jnp.cos(ang), jnp.sin(ang)
            lane = lax.broadcasted_iota(jnp.int32, ang.shape, 1)
            c_ref[...] = jnp.where(lane < ROPE_LO, 1.0, jnp.where(lane < ROPE_HI, cs, 0.0))
            sa_ref[...] = jnp.where(jnp.logical_and(lane >= ROPE_LO, lane < ROPE_MID), -sn, 0.0)
            sb_ref[...] = jnp.where(jnp.logical_and(lane >= ROPE_MID, lane < ROPE_HI), sn, 0.0)

        x, y, c = _mesh_pos()
        me = (x, y, c)
        winlo_ref[...] = win_ref[0, 0:W_IN_LO, :].astype(BF16)
        winhi_ref[...] = win_ref[0, W_IN_LO:D_MODEL, :].astype(BF16)
        oab_ref[...] = woa_ref[0].astype(BF16)
        obb_ref[...] = wob_ref[0].astype(BF16)
        outb_ref[...] = wout_ref[0].astype(BF16)
        g_uq[4 * x + 2 * y + c] = wuq_ref[...].astype(BF16)

        chip0 = jnp.logical_and(x == 0, y == 0)
        south = c == 0
        half = D_MODEL // 2
        halves = [blk0.at[pl.ds(0, half)], blk0.at[pl.ds(half, half)]]

        def bcopy(k, to, part=None):
            ref = blk0 if part is None else halves[part]
            return _remote(ref, ref, send_sems, recv_sems, 7 + k, to)

        sends0 = [(0, (0, 0, 1), None), (1, (1, 0, 0), 0), (2, (0, 1, 0), 1), (3, (1, 0, 0), 1), (4, (0, 1, 0), 0)]

        @pl.when(jnp.logical_and(chip0, south))
        def _():
            blk0[0:W_IN_LO, :] = winlo_ref[...]
            blk0[W_IN_LO:D_MODEL, :] = winhi_ref[...]
            for k, to, part in sends0:
                bcopy(k, to, part).start()

        _gather_exchange([g_uq], send_sems, recv_sems, meanwhile=local_work)

        for (cx, cy), first_k, first_half, second_k in (((1, 0), 1, 0, 3), ((0, 1), 2, 1, 4)):
            @pl.when(jnp.logical_and(jnp.logical_and(x == cx, y == cy), south))
            def _(cx=cx, cy=cy, first_k=first_k, first_half=first_half, second_k=second_k):
                bcopy(first_k, me, first_half).wait_recv()
                onward = bcopy(5 + first_half, (1, 1, 0), first_half)
                onward.start()
                bcopy(second_k, me, 1 - first_half).wait_recv()
                north = bcopy(7, (cx, cy, 1))
                north.start()
                onward.wait_send()
                north.wait_send()

        @pl.when(jnp.logical_and(jnp.logical_and(x == 1, y == 1), south))
        def _():
            bcopy(5, me, 0).wait_recv()
            bcopy(6, me, 1).wait_recv()
            north = bcopy(7, (1, 1, 1))
            north.start()
            north.wait_send()

        @pl.when(jnp.logical_and(chip0, c == 1))
        def _():
            bcopy(0, me).wait_recv()

        @pl.when(jnp.logical_and(jnp.logical_not(chip0), c == 1))
        def _():
            bcopy(7, me).wait_recv()

        @pl.when(jnp.logical_and(chip0, south))
        def _():
            for k, to, part in sends0:
                bcopy(k, to, part).wait_send()

        for j, s0, s1, seg, d0 in _column_runs():
            if seg == 2:
                wc_ref[:, d0:d0 + (s1 - s0)] = blk0[:, s0:s1]
        zeros = lambda r, w: jnp.zeros((r, w), BF16)
        wc_ref[:, Q_LORA_RANK:CQ_PAD] = zeros(D_MODEL, CQ_PAD - Q_LORA_RANK)
        wc_ref[:, CQ_PAD + LANES:CQ_PAD + LANES + ROPE_LO] = zeros(D_MODEL, ROPE_LO)
        wc_ref[:, CQ_PAD + LANES + ROPE_HI:SEG_C] = zeros(D_MODEL, LANES - ROPE_HI)
        wq_ref[Q_LORA_RANK:CQ_PAD, :] = zeros(CQ_PAD - Q_LORA_RANK, hw)
        for h in range(MLA_HEADS):
            wq_ref[0:Q_LORA_RANK, HEAD_PAD * h + QK_HEAD_DIM:HEAD_PAD * (h + 1)] = zeros(Q_LORA_RANK, HEAD_PAD - QK_HEAD_DIM)
        for j in range(N_DEV):
            for h in range(MLA_HEADS):
                wq_ref[uq_rows * j:uq_rows * (j + 1), HEAD_PAD * h:HEAD_PAD * h + QK_HEAD_DIM] = g_uq[
                    j, :, QK_HEAD_DIM * h:QK_HEAD_DIM * (h + 1)]

    vmem = pl.BlockSpec(memory_space=pltpu.VMEM)
    return pl.pallas_call(
        body, name="gather_first",
        out_shape=[jax.ShapeDtypeStruct((D_MODEL, SEG_C), BF16), jax.ShapeDtypeStruct((CQ_PAD, hw), BF16),
                   jax.ShapeDtypeStruct((W_IN_LO, SHARD_W), BF16), jax.ShapeDtypeStruct((D_MODEL - W_IN_LO, SHARD_W), BF16),
                   jax.ShapeDtypeStruct(w_oa.shape[1:], BF16),
                   jax.ShapeDtypeStruct(w_ob.shape[1:], BF16), jax.ShapeDtypeStruct(w_out.shape[1:], BF16),
                   jax.ShapeDtypeStruct((SEQ, D_MODEL), BF16), jax.ShapeDtypeStruct((D_MODEL, SEQ), BF16)]
        + [jax.ShapeDtypeStruct((SEQ, LANES), F32)] * 3,
        in_specs=[vmem] * 8, out_specs=[vmem] * 12,
        scratch_shapes=[pltpu.VMEM((N_DEV, uq_rows, MLA_HEADS * QK_HEAD_DIM), BF16), pltpu.VMEM((D_MODEL, SHARD_W), BF16),
                        pltpu.SemaphoreType.DMA((15,)), pltpu.SemaphoreType.DMA((15,))],
        compiler_params=pltpu.CompilerParams(vmem_limit_bytes=VMEM_BIG),
    )(w_in, w_uq2, w_oa, w_ob, w_out, x2, pos_col, invf_lane)


def _assemble_in(g_lo, g_hi):
    def body(glo_ref, ghi_ref, wa_ref, wb_ref):
        segs = [wa_ref, wb_ref]
        for j, s0, s1, seg, d0 in _column_runs():
            if seg < 2:
                segs[seg][0:W_IN_LO, d0:d0 + (s1 - s0)] = glo_ref[j, :, s0:s1]
                segs[seg][W_IN_LO:D_MODEL, d0:d0 + (s1 - s0)] = ghi_ref[j, :, s0:s1]

    return pl.pallas_call(
        body, name="assemble_in",
        out_shape=[jax.ShapeDtypeStruct((D_MODEL, SEG_A), BF16), jax.ShapeDtypeStruct((D_MODEL, SEG_B), BF16)],
        compiler_params=pltpu.CompilerParams(vmem_limit_bytes=VMEM_MID),
    )(g_lo, g_hi)


def _assemble_out(g_oa, g_ob, g_out):
    cols = D_MODEL // N_DEV

    def body(goa_ref, gob_ref, gout_ref, oa_ref, ob_ref, out_ref):
        for j in range(N_DEV):
            oa_ref[:, cols * j:cols * (j + 1)] = goa_ref[j]
            ob_ref[:, cols * j:cols * (j + 1)] = gob_ref[j]
            out_ref[cols * j:cols * (j + 1), :] = gout_ref[j]

    return pl.pallas_call(
        body, name="assemble_out",
        out_shape=[jax.ShapeDtypeStruct((MLA_WIDTH, D_MODEL), BF16), jax.ShapeDtypeStruct((SGU_WIDTH, D_MODEL), BF16),
                   jax.ShapeDtypeStruct((D_MODEL, D_MODEL), BF16)],
    )(g_oa, g_ob, g_out)


C_NAT = 544


def _to_parts(dwa, dwb):
    def body(dwa_ref, dwb_ref, pin_ref):
        pin_ref[0, :, 0:C_NAT] = jnp.zeros((D_MODEL, C_NAT), BF16)
        segs = [dwa_ref, dwb_ref]
        for j, s0, s1, seg, d0 in _column_runs():
            if seg < 2:
                pin_ref[j, :, s0:s1] = segs[seg][:, d0:d0 + (s1 - s0)]

    return pl.pallas_call(body, name="to_parts", out_shape=jax.ShapeDtypeStruct((N_DEV, D_MODEL, SHARD_W), BF16),
                          compiler_params=pltpu.CompilerParams(vmem_limit_bytes=VMEM_MID))(dwa, dwb)


def _dx_tail(dhs, ws, dx_res, dwc, p_uq, p_rep):
    tm = SEQ // 4
    rep_rows = p_rep.shape[1]
    c_rows = D_MODEL // N_DEV
    spec = [((c_rows, C_NAT), BF16), (p_uq.shape[1:], BF16), ((rep_rows, LANES), F32)]
    n = len(spec)

    nseg = len(dhs)

    def body(*refs):
        dh_refs, w_refs = refs[:nseg], refs[nseg:2 * nseg]
        dxr_ref, dwc_ref, puq_ref, prep_ref, dx_ref, call_ref, guq_ref, repall_ref, pc_ref, c_all, rep_all = refs[
            2 * nseg:2 * nseg + 11]
        rest = refs[2 * nseg + 11:]
        ras, tbs, rbs = rest[0:n], rest[n:2 * n], rest[2 * n:3 * n]
        send_sems, recv_sems, gsend, grecv = rest[3 * n:]
        step = pl.program_id(0)
        x, y, c = _mesh_pos()
        me_idx = 4 * x + 2 * y + c
        me, sibling = (x, y, c), (x, y, 1 - c)
        others = [(1 - x, y), (x, 1 - y), (1 - x, 1 - y)]
        parts = [pc_ref, puq_ref, prep_ref]
        gats = [rep_all, c_all]

        def stage1(chip, a):
            return _remote(parts[a].at[2 * chip + (1 - c)], ras[a].at[chip], send_sems, recv_sems, 7 * a + chip, sibling)

        def stage2(k, a):
            cx, cy = others[k]
            return _remote(tbs[a].at[k], rbs[a].at[k], send_sems, recv_sems, 7 * a + 4 + k, (cx, cy, c))

        def gcopy(a, k, blk, to):
            slab = gats[a].at[4 * blk[0] + 2 * blk[1] + blk[2]]
            return _remote(slab, slab, gsend, grecv, 7 * a + k, to)

        def chip_sum(a, chip):
            return parts[a][2 * chip + c].astype(F32) + ras[a][chip].astype(F32)

        @pl.when(step == 0)
        def _():
            for j, s0, s1, seg, d0 in _column_runs():
                if seg == 2:
                    for r in range(N_DEV):
                        pc_ref[r, :, s0:s1] = dwc_ref[c_rows * r:c_rows * (r + 1), d0:d0 + (s1 - s0)]
            for chip in range(4):
                for a in range(n):
                    stage1(chip, a).start()

        @pl.when(step == 1)
        def _():
            for chip in range(4):
                for a in range(n):
                    stage1(chip, a).wait_recv()
            for k, (cx, cy) in enumerate(others):
                for a in range(n):
                    tbs[a][k] = chip_sum(a, 2 * cx + cy).astype(spec[a][1])
                    stage2(k, a).start()

        @pl.when(step == 2)
        def _():
            for k in range(3):
                for a in range(n):
                    stage2(k, a).wait_recv()
            sums = []
            for a in range(n):
                acc = chip_sum(a, 2 * x + y)
                for k in range(3):
                    acc = acc + rbs[a][k].astype(F32)
                sums.append(acc)
            c_all[me_idx] = sums[0].astype(BF16)
            guq_ref[...] = sums[1]
            rep_all[me_idx] = sums[2]
            for a in range(2):
                for j, chip in enumerate(others):
                    gcopy(a, 1 + j, me, (*chip, c)).start()
                gcopy(a, 0, me, sibling).start()

        @pl.when(step == 3)
        def _():
            for j, chip in enumerate(others):
                for a in range(2):
                    gcopy(a, 1 + j, (*chip, c), me).wait_recv()
                    gcopy(a, 4 + j, (*chip, c), sibling).start()
            for a in range(2):
                gcopy(a, 0, sibling, me).wait_recv()
                for j, chip in enumerate(others):
                    gcopy(a, 4 + j, (*chip, 1 - c), me).wait_recv()
            for a in range(2):
                gcopy(a, 0, me, sibling).wait_send()
                for j, chip in enumerate(others):
                    gcopy(a, 1 + j, me, (*chip, c)).wait_send()
                    gcopy(a, 4 + j, (*chip, c), sibling).wait_send()
            for a in range(n):
                for chip in range(4):
                    stage1(chip, a).wait_send()
                for k in range(3):
                    stage2(k, a).wait_send()
            call_ref[...] = c_all[...]
            repall_ref[...] = rep_all[...]

        acc = dxr_ref[...]
        for dh_ref, w_ref in zip(dh_refs, w_refs):
            acc = acc + _dot(dh_ref[...], w_ref[...], _NT)
        dx_ref[...] = acc

    row = lambda w: pl.BlockSpec((tm, w), lambda i: (i, 0))
    full = lambda shape: pl.BlockSpec(shape, lambda i: (0,) * len(shape))
    scratch = [pltpu.VMEM((N_DEV, c_rows, C_NAT), BF16), pltpu.VMEM((N_DEV, c_rows, C_NAT), BF16),
               pltpu.VMEM((N_DEV, rep_rows, LANES), F32)]
    for lead in (4, 3, 3):
        scratch += [pltpu.VMEM((lead,) + tuple(shape), dt) for shape, dt in spec]
    scratch += [pltpu.SemaphoreType.DMA((7 * n,)), pltpu.SemaphoreType.DMA((7 * n,)),
                pltpu.SemaphoreType.DMA((14,)), pltpu.SemaphoreType.DMA((14,))]
    return pl.pallas_call(
        body, name="dx_tail", grid=(SEQ // tm,),
        in_specs=[row(dh.shape[1]) for dh in dhs] + [full(w.shape) for w in ws]
        + [row(D_MODEL), full(dwc.shape), full(p_uq.shape), full(p_rep.shape)],
        out_specs=[row(D_MODEL), full((N_DEV, c_rows, C_NAT)), full(p_uq.shape[1:]), full((N_DEV, rep_rows, LANES))],
        out_shape=[jax.ShapeDtypeStruct((SEQ, D_MODEL), F32), jax.ShapeDtypeStruct((N_DEV, c_rows, C_NAT), BF16),
                   jax.ShapeDtypeStruct(p_uq.shape[1:], F32), jax.ShapeDtypeStruct((N_DEV, rep_rows, LANES), F32)],
        scratch_shapes=scratch,
        compiler_params=pltpu.CompilerParams(dimension_semantics=("arbitrary",), vmem_limit_bytes=VMEM_BIG),
    )(*dhs, *ws, dx_res, dwc, p_uq, p_rep)


def _sum_landed(landed, c_all):
    c_rows = D_MODEL // N_DEV

    def body(rin_ref, roa_ref, rob_ref, rout_ref, call_ref, gin_ref, goa_ref, gob_ref, gout_ref):
        def total(ref, sl):
            acc = ref[0, sl, :].astype(F32)
            for s in range(1, N_DEV):
                acc = acc + ref[s, sl, :].astype(F32)
            return acc

        x, y, c = _mesh_pos()
        dev0 = jnp.where(4 * x + 2 * y + c == 0, 1.0, 0.0)
        for j in range(N_DEV):
            sl = slice(c_rows * j, c_rows * (j + 1))
            tot = total(rin_ref, sl)
            gin_ref[0, sl, C_NAT:SHARD_W] = tot[:, C_NAT:SHARD_W]
            gin_ref[0, sl, 0:C_NAT] = tot[:, 0:C_NAT] + dev0 * call_ref[j].astype(F32)
        goa_ref[0] = total(roa_ref, slice(None))
        gob_ref[0] = total(rob_ref, slice(None))
        gout_ref[0] = total(rout_ref, slice(None))

    return pl.pallas_call(
        body, name="sum_landed",
        out_shape=[jax.ShapeDtypeStruct((1,) + r.shape[1:], F32) for r in landed],
        compiler_params=pltpu.CompilerParams(vmem_limit_bytes=VMEM_MID),
    )(*landed, c_all)


_O_CQ, _O_CKV, _O_KPE, _O_ZA, _O_U, _O_V, _O_ZB, _O_GA, _O_GB = 0, 384, 512, 544, 1056, 1568, 2080, 2592, 3616


def _to_segments(w):
    z = lambda n: jnp.zeros(w.shape[:-1] + (n,), w.dtype)
    seg_a = jnp.concatenate([w[..., _O_GA:_O_GB], w[..., _O_GB:IN_WIDTH], w[..., _O_ZA:_O_U]], axis=-1)
    seg_b = jnp.concatenate([w[..., _O_U:_O_V], w[..., _O_V:_O_ZB], w[..., _O_ZB:_O_GA]], axis=-1)
    seg_c = jnp.concatenate([w[..., _O_CQ:_O_CKV], z(CQ_PAD - Q_LORA_RANK), w[..., _O_CKV:_O_KPE],
                             z(ROPE_LO), w[..., _O_KPE:_O_ZA], z(LANES - ROPE_HI)], axis=-1)
    return seg_a, seg_b, seg_c


def _from_segments(seg_a, seg_b, seg_c):
    kpe0 = CQ_PAD + LANES + ROPE_LO
    return jnp.concatenate([
        seg_c[..., 0:Q_LORA_RANK], seg_c[..., CQ_PAD:CQ_PAD + LANES], seg_c[..., kpe0:kpe0 + QK_ROPE_DIM],
        seg_a[..., 2 * D_MODEL:SEG_A], seg_b, seg_a[..., 0:2 * D_MODEL]], axis=-1)


def kernel(x, positions, w_in, b_in, g_q, w_uq, g_kv, w_ukv, w_oa, sgu_ln_g, sgu_ln_b, w_s, b_s, w_ob, w_out, ln_g, ln_b, loss_target, m_w_in, m_b_in, m_g_q, m_w_uq, m_g_kv, m_w_ukv, m_w_oa, m_sgu_ln_g, m_sgu_ln_b, m_w_s, m_b_s, m_w_ob, m_w_out, m_ln_g, m_ln_b, v_w_in, v_b_in, v_g_q, v_w_uq, v_g_kv, v_w_ukv, v_w_oa, v_sgu_ln_g, v_sgu_ln_b, v_w_s, v_b_s, v_w_ob, v_w_out, v_ln_g, v_ln_b):
    w_uq2 = w_uq[0].reshape(Q_LORA_RANK // N_DEV, MLA_HEADS * QK_HEAD_DIM)
    inv_freq = ROPE_THETA ** (-jnp.arange(0, QK_ROPE_DIM, 2, dtype=F32) / QK_ROPE_DIM)
    invf_lane = jnp.concatenate([jnp.zeros((ROPE_LO,), F32), inv_freq, inv_freq,
                                 jnp.zeros((LANES - ROPE_HI,), F32)]).reshape(1, LANES)
    first = _gather_first(w_in, w_uq2, w_oa, w_ob, w_out, x[0], positions.reshape(SEQ, 1), invf_lane)
    partials = _local_step(x[0], loss_target[0], first, b_in, g_q, g_kv, w_ukv, sgu_ln_g, sgu_ln_b, w_s, b_s, ln_g, ln_b)
    weights = dict(w_in=w_in, b_in=b_in, g_q=g_q, w_uq=w_uq, g_kv=g_kv, w_ukv=w_ukv, w_oa=w_oa, sgu_ln_g=sgu_ln_g,
                   sgu_ln_b=sgu_ln_b, w_s=w_s, b_s=b_s, w_ob=w_ob, w_out=w_out, ln_g=ln_g, ln_b=ln_b)
    moms = dict(w_in=m_w_in, b_in=m_b_in, g_q=m_g_q, w_uq=m_w_uq, g_kv=m_g_kv, w_ukv=m_w_ukv, w_oa=m_w_oa,
                sgu_ln_g=m_sgu_ln_g, sgu_ln_b=m_sgu_ln_b, w_s=m_w_s, b_s=m_b_s, w_ob=m_w_ob, w_out=m_w_out,
                ln_g=m_ln_g, ln_b=m_ln_b)
    vars_ = dict(w_in=v_w_in, b_in=v_b_in, g_q=v_g_q, w_uq=v_w_uq, g_kv=v_g_kv, w_ukv=v_w_ukv, w_oa=v_w_oa,
                 sgu_ln_g=v_sgu_ln_g, sgu_ln_b=v_sgu_ln_b, w_s=v_w_s, b_s=v_b_s, w_ob=v_w_ob, w_out=v_w_out,
                 ln_g=v_ln_g, ln_b=v_ln_b)
    return _reduce_and_update(partials, weights, moms, vars_)


def _local_step(x2, tgt, first, b_in, g_q, g_kv, w_ukv, sgu_ln_g, sgu_ln_b, w_s, b_s, ln_g, ln_b):
    wc, wq, win_lo, win_hi, oa_b, ob_b, out_b, x_bf, xt_bf, c_t, sa_t, sb_t = first
    ba, bb, bc = _to_segments(b_in)
    w_ukv_bf = w_ukv[0].astype(BF16)
    wkn = jnp.pad(w_ukv_bf[:, :, :QK_NOPE_DIM], ((0, 0), (0, 0), (0, HEAD_PAD - QK_NOPE_DIM))).reshape(KV_LORA_RANK, -1)
    wv = jnp.pad(w_ukv_bf[:, :, QK_NOPE_DIM:], ((0, 0), (0, 0), (0, HEAD_PAD - V_HEAD_DIM))).reshape(KV_LORA_RANK, -1)
    gq = jnp.pad(g_q, ((0, 0), (0, CQ_PAD - Q_LORA_RANK)))
    bias_full = jnp.repeat(b_s[0].T, SGU_GROUP_DIM, axis=1)
    w_s3 = w_s[0]
    w_st3 = jnp.swapaxes(w_s3, 1, 2)

    h_c = _mm(x_bf, wc, bias=bc, tm=512, tn=SEG_C, name="in_proj_c")
    q, k, kt, vx, vxt = _mla_prep(h_c, gq, g_kv, wq, wkn, wv, c_t, sa_t, sb_t)
    o, lse, (g_lo, g_hi) = _attn_fwd(q, kt, vx, (win_lo, win_hi))
    wa, wb = _assemble_in(g_lo, g_hi)
    h_a, (g_out,) = _mm(x_bf, wa, bias=ba, own=(out_b,), tm=512, tn=SEG_A // 2, name="in_proj_a")
    h_b, (g_oa, g_ob) = _mm(x_bf, wb, bias=bb, own=(oa_b, ob_b), tm=512, tn=SEG_B // 2, name="in_proj_b")
    y_b = _sgu_fwd(h_b, sgu_ln_g, sgu_ln_b, w_s3, bias_full)
    w_oa_f, w_ob_f, w_out_f = _assemble_out(g_oa, g_ob, g_out)

    (loss_row, dx_res, dh_a, d_o, d_yb, p_oa, p_ob, p_out, d_lng, d_lnb, d_ba) = _merge(
        x2, o, h_a, y_b, tgt, w_oa_f, w_ob_f, w_out_f, ln_g, ln_b)
    (dh_b, d_ws, d_bs_t, d_slg, d_slb, d_bb), (r_out,) = _sgu_bwd(h_b, d_yb, sgu_ln_g, sgu_ln_b, w_s3, w_st3, bias_full,
                                                                 (p_out,))
    d_wa, (r_oa,) = _mm(xt_bf, dh_a, out_dtype=BF16, parts=(p_oa,), tm=512, tn=512, name="dw_in_a")
    d_wb, (r_ob,) = _mm(xt_bf, dh_b, out_dtype=BF16, parts=(p_ob,), tm=512, tn=512, name="dw_in_b")
    dq, dk, dv, landed_in = _attn_bwd(q, kt, k, vxt, d_o, o, lse, (_to_parts(d_wa, d_wb),))
    landed = (*landed_in, r_oa, r_ob, r_out)
    dh_c, p_uq, d_wkn, d_wv, d_gq, d_gkv, d_bc = _mla_bwd(dq, dk, dv, h_c, gq, g_kv, wq, wkn, wv, c_t, sa_t, sb_t)
    d_wc = _mm(xt_bf, dh_c, out_dtype=BF16, tm=512, tn=SEG_C, name="dw_in_c")


    p_b_in = _from_segments(d_ba, d_bb, d_bc)
    p_w_ukv = jnp.concatenate([d_wkn.reshape(KV_LORA_RANK, MLA_HEADS, HEAD_PAD)[:, :, :QK_NOPE_DIM],
                               d_wv.reshape(KV_LORA_RANK, MLA_HEADS, HEAD_PAD)[:, :, :V_HEAD_DIM]], axis=-1)
    p_g_q = d_gq[:, :Q_LORA_RANK]
    p_b_s = d_bs_t[:, :SGU_GROUPS].T
    replicated = [p_b_in, p_g_q, d_gkv, p_w_ukv, d_slg, d_slb, d_ws, p_b_s, d_lng, d_lnb]
    return loss_row, ((dh_a, dh_b, dh_c), (wa, wb, wc), dx_res), landed, d_wc, p_uq, replicated


_NAMES = ["w_in", "b_in", "g_q", "w_uq", "g_kv", "w_ukv", "w_oa", "sgu_ln_g", "sgu_ln_b", "w_s", "b_s", "w_ob",
          "w_out", "ln_g", "ln_b"]
_REPLICATED = ["b_in", "g_q", "g_kv", "w_ukv", "sgu_ln_g", "sgu_ln_b", "w_s", "b_s", "ln_g", "ln_b"]


def _reduce_and_update(partials, weights, moms, vars_):
    loss_row, (dhs, ws, dx_res), landed, d_wc, p_uq, replicated = partials
    rep_flat = jnp.concatenate([a.reshape(-1) for a in replicated] + [loss_row[0, :1]])
    rep_flat = jnp.pad(rep_flat, (0, N_DEV * PACK_R_ROWS * LANES - rep_flat.size))
    dx_ab, c_all, g_uq, rep_all = _dx_tail(dhs[:2], ws[:2], dx_res, d_wc, p_uq,
                                           rep_flat.reshape(N_DEV, PACK_R_ROWS, LANES))
    dx = _mm(dhs[2], ws[2], tb=True, add=dx_ab, tm=512, tn=D_MODEL, name="dx_c")
    g_in, g_oa, g_ob, g_out = _sum_landed(landed, c_all)
    rep_sum = rep_all.reshape(-1)
    grads, pos = dict(w_in=g_in, w_uq=g_uq, w_oa=g_oa, w_ob=g_ob, w_out=g_out), 0
    for nm in _REPLICATED:
        grads[nm] = rep_sum[pos:pos + weights[nm].size]
        pos += weights[nm].size
    loss = rep_sum[pos]
    grads = {nm: grads[nm].reshape(weights[nm].shape) for nm in _NAMES}
    deltas, new_m, new_v = _adamw_all([weights[nm] for nm in _NAMES], [grads[nm] for nm in _NAMES],
                                      [moms[nm] for nm in _NAMES], [vars_[nm] for nm in _NAMES])
    return (loss, dx.reshape(1, SEQ, D_MODEL), *[grads[nm] for nm in _NAMES], *deltas, *new_m, *new_v)
```

```python
import math

import jax
import jax.numpy as jnp
from jax import lax
from jax.experimental import pallas as pl
from jax.experimental.pallas import tpu as pltpu

F32 = jnp.float32
BF16 = jnp.bfloat16

D_MODEL = 1024
SEQ = 2048
N_DEV = 8
MLA_HEADS = 8
Q_LORA_RANK = 384
KV_LORA_RANK = 128
QK_NOPE_DIM = 64
QK_ROPE_DIM = 32
V_HEAD_DIM = 64
QK_HEAD_DIM = QK_NOPE_DIM + QK_ROPE_DIM
MLA_WIDTH = MLA_HEADS * V_HEAD_DIM
ROPE_THETA = 10000.0
SGU_GROUPS = 8
SGU_GROUP_DIM = 64
SGU_WIDTH = SGU_GROUPS * SGU_GROUP_DIM
CHUNK = 128
RMS_EPS = 1e-6
LN_EPS = 1e-5
DN_ALPHA = 2.0 ** 0.25
IN_WIDTH = 4640
ATTN_SCALE = QK_HEAD_DIM ** -0.5

ADAM_LR = 0.001
ADAM_B1 = 0.9
ADAM_B2 = 0.999
ADAM_EPS = 1e-08
ADAM_WD = 0.01
ADAM_STEP = 10

LANES = 128
HEAD_PAD = 128
ROPE_LO = QK_NOPE_DIM
ROPE_MID = ROPE_LO + QK_ROPE_DIM // 2
ROPE_HI = ROPE_LO + QK_ROPE_DIM
CQ_PAD = 512

SEG_A = 2560
SEG_B = 1536
SEG_C = 768

PACK_R_ROWS = 272
VMEM_BIG = 56 * 1024 * 1024
VMEM_MID = 40 * 1024 * 1024


def _sigmoid(x):
    return 1.0 / (1.0 + jnp.exp(-x))


def _gelu_and_grad(x):
    c0 = math.sqrt(2.0 / math.pi)
    x2 = x * x
    t = jnp.tanh(c0 * (x + 0.044715 * x * x2))
    g = 0.5 * x * (1.0 + t)
    dg = 0.5 * (1.0 + t) + 0.5 * x * (1.0 - t * t) * (c0 * (1.0 + 3.0 * 0.044715 * x2))
    return g, dg


def _dot(a, b, dims):
    return lax.dot_general(a, b, (dims, ((), ())), preferred_element_type=F32)


_NN = ((1,), (0,))
_NT = ((1,), (1,))
_TN = ((0,), (0,))


def _store_grad(dh_ref, db_ref, col, val):
    cols = slice(col, col + val.shape[1])
    dh_ref[:, cols] = val.astype(BF16)
    db_ref[:, cols] += jnp.sum(val, axis=0, keepdims=True)


def _mm(a, b, *, tb=False, bias=None, add=None, out_dtype=F32, own=(), parts=(), tm, tn, name):
    m, k = a.shape
    n = b.shape[0] if tb else b.shape[1]
    assert m % tm == 0 and n % tn == 0 and not (own and parts)
    dims = _NT if tb else _NN
    nown = len(own) + len(parts)
    nm = m // tm
    nsteps = (n // tn) * nm

    def body(*refs):
        a_ref, b_ref = refs[0], refs[1]
        pos = 2
        r = _dot(a_ref[...], b_ref[...], dims)
        if bias is not None:
            r = r + refs[pos][...]; pos += 1
        if add is not None:
            r = r + refs[pos][...]; pos += 1
        own_refs = refs[pos:pos + nown]; pos += nown
        refs[pos][...] = r.astype(out_dtype)
        if nown:
            gat_refs = refs[pos + 1:pos + 1 + nown]
            send_sems, recv_sems, local_sems = refs[pos + 1 + nown:]
            step = pl.program_id(0) * nm + pl.program_id(1)
            if own:
                _gather_behind(own_refs, gat_refs, send_sems, recv_sems, local_sems, step, nsteps - 4, nsteps - 2,
                               nsteps - 1)
            else:
                exchange = _exchange_parts(own_refs, gat_refs, send_sems, recv_sems, local_sems)
                _exchange_start(step == 0, exchange)
                _exchange_finish(step == nsteps - 1, exchange)

    b_spec = pl.BlockSpec((tn, k), lambda j, i: (j, 0)) if tb else pl.BlockSpec((k, tn), lambda j, i: (0, j))
    in_specs, args = [pl.BlockSpec((tm, k), lambda j, i: (i, 0)), b_spec], [a, b]
    if bias is not None:
        in_specs.append(pl.BlockSpec((1, tn), lambda j, i: (0, j))); args.append(bias)
    if add is not None:
        in_specs.append(pl.BlockSpec((tm, tn), lambda j, i: (i, j))); args.append(add)
    hbm = pl.BlockSpec(memory_space=pl.ANY)
    res = pl.pallas_call(
        body, name=name, grid=(n // tn, nm), in_specs=in_specs + [hbm] * nown,
        out_specs=[pl.BlockSpec((tm, tn), lambda j, i: (i, j))] + [hbm] * nown,
        out_shape=[jax.ShapeDtypeStruct((m, n), out_dtype)]
        + [jax.ShapeDtypeStruct((N_DEV,) + o.shape, o.dtype) for o in own]
        + [jax.ShapeDtypeStruct(p.shape, p.dtype) for p in parts],
        scratch_shapes=_exchange_sems(nown) if nown else [],
        compiler_params=pltpu.CompilerParams(dimension_semantics=("arbitrary", "arbitrary"), vmem_limit_bytes=VMEM_BIG),
    )(*args, *own, *parts)
    return (res[0], res[1:]) if nown else res[0]


def _rope(x, c, sa, sb):
    return x * c + pltpu.roll(x, LANES - 16, 1) * sa + pltpu.roll(x, 16, 1) * sb


def _rope_t(dy, c, sa, sb):
    return dy * c + pltpu.roll(dy * sa, 16, 1) + pltpu.roll(dy * sb, LANES - 16, 1)


def _mla_prep(h_c, gq, gkv, wq, wkn, wvx, c_t, sa_t, sb_t):
    tm = 256
    hw = MLA_HEADS * HEAD_PAD

    def body(cq_ref, ckv_ref, kpe_ref, gq_ref, gkv_ref, wq_ref, wkn_ref, wvx_ref, c_ref, sa_ref, sb_ref,
             q_ref, k_ref, kt_ref, vx_ref, vxt_ref):
        c, sa, sb = c_ref[...], sa_ref[...], sb_ref[...]
        cq = cq_ref[...]
        rq = lax.rsqrt(jnp.sum(cq * cq, axis=1, keepdims=True) * (1.0 / Q_LORA_RANK) + RMS_EPS)
        cqn = ((cq * rq) * gq_ref[...]).astype(BF16)
        qall = _dot(cqn, wq_ref[...], _NN)
        for h in range(MLA_HEADS):
            sl = slice(HEAD_PAD * h, HEAD_PAD * (h + 1))
            q_ref[:, sl] = (_rope(qall[:, sl], c, sa, sb) * ATTN_SCALE).astype(BF16)
        ckv = ckv_ref[...]
        rkv = lax.rsqrt(jnp.sum(ckv * ckv, axis=1, keepdims=True) * (1.0 / KV_LORA_RANK) + RMS_EPS)
        ckvn = ((ckv * rkv) * gkv_ref[...]).astype(BF16)
        knall = _dot(ckvn, wkn_ref[...], _NN)
        vall = _dot(ckvn, wvx_ref[...], _NN)
        kper = _rope(kpe_ref[...], c, sa, sb)
        ones_half = (lax.broadcasted_iota(jnp.int32, (tm, HEAD_PAD), 1) >= V_HEAD_DIM).astype(F32)
        for h in range(MLA_HEADS):
            sl = slice(HEAD_PAD * h, HEAD_PAD * (h + 1))
            kh = knall[:, sl] + kper
            vh = vall[:, sl] + ones_half
            k_ref[:, sl] = kh.astype(BF16)
            kt_ref[sl, :] = kh.T.astype(BF16)
            vx_ref[:, sl] = vh.astype(BF16)
            vxt_ref[sl, :] = vh.T.astype(BF16)

    full = lambda shape: pl.BlockSpec(shape, lambda i: (0, 0))
    tab = pl.BlockSpec((tm, LANES), lambda i: (i, 0))
    row = pl.BlockSpec((tm, hw), lambda i: (i, 0))
    col = pl.BlockSpec((hw, tm), lambda i: (0, i))
    return pl.pallas_call(
        body, name="mla_prep", grid=(SEQ // tm,),
        in_specs=[pl.BlockSpec((tm, CQ_PAD), lambda i: (i, 0)),
                  pl.BlockSpec((tm, LANES), lambda i: (i, CQ_PAD // LANES)),
                  pl.BlockSpec((tm, LANES), lambda i: (i, CQ_PAD // LANES + 1)),
                  full((1, CQ_PAD)), full((1, KV_LORA_RANK)),
                  full((CQ_PAD, hw)), full((KV_LORA_RANK, hw)), full((KV_LORA_RANK, hw)), tab, tab, tab],
        out_specs=[row, row, col, row, col],
        out_shape=[jax.ShapeDtypeStruct((SEQ, hw), BF16), jax.ShapeDtypeStruct((SEQ, hw), BF16),
                   jax.ShapeDtypeStruct((hw, SEQ), BF16), jax.ShapeDtypeStruct((SEQ, hw), BF16),
                   jax.ShapeDtypeStruct((hw, SEQ), BF16)],
        compiler_params=pltpu.CompilerParams(dimension_semantics=("arbitrary",), vmem_limit_bytes=VMEM_MID),
    )(h_c, h_c, h_c, gq, gkv, wq, wkn, wvx, c_t, sa_t, sb_t)


ATT_T = 512
ATT_STRIP = 64


def _causal_width(r, rs, t):
    return min(t, LANES * (-(-(rs * (r + 1)) // LANES)))


def _attn_fwd(q, kt, vx, own):
    t, rs = ATT_T, ATT_STRIP
    nown = len(own)
    nq = SEQ // t
    nsteps = (MLA_HEADS // 2) * nq

    def body(q_ref, kt_ref, vx_ref, *rest):
        own_refs, (o_ref, l_ref), gat_refs = rest[:nown], rest[nown:nown + 2], rest[nown + 2:2 * nown + 2]
        s_scr, p_scr, m_scr, a_scr, acc_scr, send_sems, recv_sems, local_sems = rest[2 * nown + 2:]
        qi = pl.program_id(1)
        _gather_behind(own_refs, gat_refs, send_sems, recv_sems, local_sems, pl.program_id(0) * nq + qi,
                       nsteps - 7, nsteps - 3, nsteps - 1)
        lane = lax.broadcasted_iota(jnp.int32, (t, LANES), 1)
        m_scr[...] = jnp.full((2, t, LANES), -1e30, F32)
        acc_scr[...] = jnp.zeros((2, t, LANES), F32)

        def block(j, masked):
            off = pl.multiple_of(j * t, t)
            for a in range(2):
                sl = slice(HEAD_PAD * a, HEAD_PAD * (a + 1))
                s_scr[a] = _dot(q_ref[:, sl], kt_ref[sl, pl.ds(off, t)], _NN)
                for r in range(t // rs):
                    rows = slice(rs * r, rs * (r + 1))
                    w = _causal_width(r, rs, t) if masked else t
                    s = s_scr[a, rows, 0:w]
                    if masked:
                        rowi = lax.broadcasted_iota(jnp.int32, (rs, w), 0) + rs * r
                        coli = lax.broadcasted_iota(jnp.int32, (rs, w), 1)
                        s = jnp.where(coli <= rowi, s, -1e30)
                        if w < t:
                            p_scr[a, rows, w:t] = jnp.zeros((rs, t - w), BF16)
                    m_old = m_scr[a, rows, :]
                    m_new = jnp.maximum(m_old, jnp.max(s, axis=1, keepdims=True))
                    p_scr[a, rows, 0:w] = jnp.exp(s - m_new[:, :1]).astype(BF16)
                    a_scr[a, rows, :] = jnp.exp(m_old - m_new)
                    m_scr[a, rows, :] = m_new
                acc_scr[a] = acc_scr[a] * a_scr[a] + _dot(p_scr[a], vx_ref[pl.ds(off, t), sl], _NN)

        def step(j, carry):
            block(j, False)
            return carry
        lax.fori_loop(0, qi, step, 0)
        block(qi, True)
        res = []
        for a in range(2):
            acc = acc_scr[a]
            l = acc[:, V_HEAD_DIM:V_HEAD_DIM + 1]
            res.append((acc / l, m_scr[a] + jnp.log(l)))
        o_ref[...] = jnp.where(lane < V_HEAD_DIM, res[0][0], pltpu.roll(res[1][0], V_HEAD_DIM, 1))
        l_ref[...] = jnp.where(lane < V_HEAD_DIM, res[0][1], res[1][1])

    hbm = pl.BlockSpec(memory_space=pl.ANY)
    res = pl.pallas_call(
        body, name="attn_fwd", grid=(MLA_HEADS // 2, nq),
        in_specs=[pl.BlockSpec((t, 2 * HEAD_PAD), lambda p, i: (i, p)),
                  pl.BlockSpec((2 * HEAD_PAD, SEQ), lambda p, i: (p, 0)),
                  pl.BlockSpec((SEQ, 2 * HEAD_PAD), lambda p, i: (0, p))] + [hbm] * nown,
        out_specs=[pl.BlockSpec((t, LANES), lambda p, i: (i, p)),
                   pl.BlockSpec((t, LANES), lambda p, i: (i, p))] + [hbm] * nown,
        out_shape=[jax.ShapeDtypeStruct((SEQ, MLA_WIDTH), F32), jax.ShapeDtypeStruct((SEQ, MLA_WIDTH), F32)]
        + [jax.ShapeDtypeStruct((N_DEV,) + a.shape, a.dtype) for a in own],
        scratch_shapes=[pltpu.VMEM((2, t, t), F32), pltpu.VMEM((2, t, t), BF16), pltpu.VMEM((2, t, LANES), F32),
                        pltpu.VMEM((2, t, LANES), F32), pltpu.VMEM((2, t, LANES), F32)] + _exchange_sems(nown),
        compiler_params=pltpu.CompilerParams(dimension_semantics=("arbitrary", "arbitrary"), vmem_limit_bytes=VMEM_MID),
    )(q, kt, vx, *own)
    return res[0], res[1], res[2:]


def _exchange_parts(parts, lands, send_sems, recv_sems, local_sems):
    x, y, c = _mesh_pos()
    me = 4 * x + 2 * y + c
    peers = [(x, y, 1 - c), (1 - x, y, c), (x, 1 - y, c), (1 - x, 1 - y, c),
             (1 - x, y, 1 - c), (x, 1 - y, 1 - c), (1 - x, 1 - y, 1 - c)]
    remote, local = [], []
    for a, (part, land) in enumerate(zip(parts, lands)):
        for k, peer in enumerate(peers):
            t = 4 * peer[0] + 2 * peer[1] + peer[2]
            remote.append(_remote(part.at[t], land.at[me], send_sems, recv_sems, 7 * a + k, peer))
        local.append(pltpu.make_async_copy(part.at[me], land.at[me], local_sems.at[a]))
    return remote, local


def _exchange_start(first_step, exchange):
    remote, local = exchange

    @pl.when(first_step)
    def _():
        for cp in remote + local:
            cp.start()


def _exchange_finish(last_step, exchange):
    remote, local = exchange

    @pl.when(last_step)
    def _():
        for cp in remote:
            cp.wait_recv()
        for cp in remote:
            cp.wait_send()
        for cp in local:
            cp.wait()


def _exchange_sems(npart):
    return [pltpu.SemaphoreType.DMA((7 * npart,)), pltpu.SemaphoreType.DMA((7 * npart,)),
            pltpu.SemaphoreType.DMA((npart,))]


def _attn_bwd(q, kt, k, vxt, d_o, o, lse, parts):
    t, rs = ATT_T, ATT_STRIP
    nq = SEQ // t
    npart = len(parts)
    nsteps = MLA_HEADS // 2

    def body(q_ref, kt_ref, k_ref, vxt_ref, do_ref, o_ref, l_ref, *rest):
        part_refs, rest = rest[:npart], rest[npart:]
        dq_ref, dk_ref, dv_ref = rest[:3]
        land_refs, rest = rest[3:3 + npart], rest[3 + npart:]
        s_scr, dp_scr, p_scr, ds_scr, st_scr, send_sems, recv_sems, local_sems = rest
        exchange = _exchange_parts(part_refs, land_refs, send_sems, recv_sems, local_sems)
        _exchange_start(pl.program_id(0) == 0, exchange)
        dk_ref[...] = jnp.zeros_like(dk_ref)
        dv_ref[...] = jnp.zeros_like(dv_ref)
        lane = lax.broadcasted_iota(jnp.int32, (t, LANES), 1)

        def qtile(i, carry):
            ioff = pl.multiple_of(i * t, t)
            do_i = do_ref[pl.ds(ioff, t), :]
            o_i = o_ref[pl.ds(ioff, t), :]
            l_i = l_ref[pl.ds(ioff, t), :]
            for a in range(2):
                sl = slice(HEAD_PAD * a, HEAD_PAD * (a + 1))
                sel = (lane < V_HEAD_DIM) if a == 0 else (lane >= V_HEAD_DIM)
                doa = jnp.where(sel, do_i, 0.0)
                oa = o_i
                if a == 1:
                    doa = pltpu.roll(doa, V_HEAD_DIM, 1)
                    oa = pltpu.roll(o_i, V_HEAD_DIM, 1)
                st_scr[0] = jnp.broadcast_to(jnp.sum(doa * oa, axis=1, keepdims=True), (t, LANES))
                st_scr[1] = jnp.broadcast_to(l_i[:, V_HEAD_DIM * a:V_HEAD_DIM * a + 1], (t, LANES))
                doa_bf = doa.astype(BF16)
                qa = q_ref[pl.ds(ioff, t), sl]

                def block(j, masked, dq_acc, sl=sl, qa=qa, doa_bf=doa_bf):
                    joff = pl.multiple_of(j * t, t)
                    s_scr[...] = _dot(qa, kt_ref[sl, pl.ds(joff, t)], _NN)
                    dp_scr[...] = _dot(doa_bf, vxt_ref[sl, pl.ds(joff, t)], _NN)
                    for r in range(t // rs):
                        rows = slice(rs * r, rs * (r + 1))
                        w = _causal_width(r, rs, t) if masked else t
                        p = jnp.exp(s_scr[rows, 0:w] - st_scr[1, rows, :1])
                        if masked:
                            rowi = lax.broadcasted_iota(jnp.int32, (rs, w), 0) + rs * r
                            coli = lax.broadcasted_iota(jnp.int32, (rs, w), 1)
                            p = jnp.where(coli <= rowi, p, 0.0)
                            if w < t:
                                p_scr[rows, w:t] = jnp.zeros((rs, t - w), BF16)
                                ds_scr[rows, w:t] = jnp.zeros((rs, t - w), BF16)
                        p_scr[rows, 0:w] = p.astype(BF16)
                        ds_scr[rows, 0:w] = (p * (dp_scr[rows, 0:w] - st_scr[0, rows, :1])).astype(BF16)
                    dk_ref[pl.ds(joff, t), sl] += _dot(ds_scr[...], qa, _TN)
                    dv_ref[pl.ds(joff, t), sl] += _dot(p_scr[...], doa_bf, _TN)
                    return dq_acc + _dot(ds_scr[...], k_ref[pl.ds(joff, t), sl], _NN)

                dq_acc = lax.fori_loop(0, i, lambda j, acc: block(j, False, acc), jnp.zeros((t, HEAD_PAD), F32))
                dq_ref[pl.ds(ioff, t), sl] = block(i, True, dq_acc)
            return carry

        lax.fori_loop(0, nq, qtile, 0)
        _exchange_finish(pl.program_id(0) == nsteps - 1, exchange)

    hw = MLA_HEADS * HEAD_PAD
    wide = pl.BlockSpec((SEQ, 2 * HEAD_PAD), lambda p: (0, p))
    wide_t = pl.BlockSpec((2 * HEAD_PAD, SEQ), lambda p: (p, 0))
    narrow = pl.BlockSpec((SEQ, LANES), lambda p: (0, p))
    hbm = pl.BlockSpec(memory_space=pl.ANY)
    res = pl.pallas_call(
        body, name="attn_bwd", grid=(nsteps,),
        in_specs=[wide, wide_t, wide, wide_t, narrow, narrow, narrow] + [hbm] * npart,
        out_specs=[wide, wide, wide] + [hbm] * npart,
        out_shape=[jax.ShapeDtypeStruct((SEQ, hw), F32)] * 3 + [jax.ShapeDtypeStruct(p.shape, p.dtype) for p in parts],
        scratch_shapes=[pltpu.VMEM((t, t), F32), pltpu.VMEM((t, t), F32), pltpu.VMEM((t, t), BF16),
                        pltpu.VMEM((t, t), BF16), pltpu.VMEM((2, t, LANES), F32)] + _exchange_sems(npart),
        compiler_params=pltpu.CompilerParams(dimension_semantics=("arbitrary",), vmem_limit_bytes=VMEM_BIG),
    )(q, kt, k, vxt, d_o, o, lse, *parts)
    return res[0], res[1], res[2], res[3:]


def _sgu_math(u, v, zb, lg, lb, ws_ref, bias):
    ug, dug = _gelu_and_grad(u)
    vg, dvg = _gelu_and_grad(v)
    mu = jnp.mean(vg, axis=1, keepdims=True)
    xc = vg - mu
    rstd = lax.rsqrt(jnp.mean(xc * xc, axis=1, keepdims=True) + LN_EPS)
    xh = xc * rstd
    vn_bf = (xh * lg + lb).astype(BF16)
    grp = lax.broadcasted_iota(jnp.int32, (CHUNK, SGU_WIDTH), 1) // SGU_GROUP_DIM
    r_i = lax.broadcasted_iota(jnp.int32, (CHUNK, CHUNK), 0)
    c_i = lax.broadcasted_iota(jnp.int32, (CHUNK, CHUNK), 1)
    tri, tri_t = r_i >= c_i, r_i <= c_i
    mixed = bias
    for g in range(SGU_GROUPS):
        wt = jnp.where(tri, ws_ref[g], 0.0).astype(BF16)
        mixed = mixed + jnp.where(grp == g, _dot(wt, vn_bf, _NN), 0.0)
    sb = _sigmoid(zb)
    return ug, dug, dvg, rstd, xh, vn_bf, grp, tri, tri_t, mixed, sb


def _sgu_fwd(h_b, lg, lb, w_s, bias_full):
    def body(u_ref, v_ref, zb_ref, lg_ref, lb_ref, ws_ref, bias_ref, yb_ref):
        zb = zb_ref[...]
        ug, _, _, _, _, _, _, _, _, mixed, sb = _sgu_math(u_ref[...], v_ref[...], zb, lg_ref[...], lb_ref[...],
                                                       ws_ref, bias_ref[...])
        yb_ref[...] = (ug * mixed) * (zb * sb)

    blk = lambda c: pl.BlockSpec((CHUNK, SGU_WIDTH), lambda i, c=c: (i, c))
    full2 = lambda shape: pl.BlockSpec(shape, lambda i: (0, 0))
    return pl.pallas_call(
        body, name="sgu_fwd", grid=(SEQ // CHUNK,),
        in_specs=[blk(0), blk(1), blk(2), full2((1, SGU_WIDTH)), full2((1, SGU_WIDTH)),
                  pl.BlockSpec((SGU_GROUPS, CHUNK, CHUNK), lambda i: (0, 0, 0)), full2((CHUNK, SGU_WIDTH))],
        out_specs=pl.BlockSpec((CHUNK, SGU_WIDTH), lambda i: (i, 0)),
        out_shape=jax.ShapeDtypeStruct((SEQ, SGU_WIDTH), F32),
        compiler_params=pltpu.CompilerParams(dimension_semantics=("arbitrary",)),
    )(h_b, h_b, h_b, lg, lb, w_s, bias_full)


def _sgu_bwd(h_b, d_yb, lg, lb, w_s, w_st, bias_full, parts):
    nsteps = SEQ // CHUNK
    npart = len(parts)

    def body(u_ref, v_ref, zb_ref, dyb_ref, lg_ref, lb_ref, ws_ref, wst_ref, bias_ref, *rest):
        part_refs, rest = rest[:npart], rest[npart:]
        dhb_ref, dws_ref, dbs_ref, dlg_ref, dlb_ref, dbb_ref = rest[:6]
        land_refs, (dbias_acc, send_sems, recv_sems, local_sems) = rest[6:6 + npart], rest[6 + npart:]
        step = pl.program_id(0)
        exchange = _exchange_parts(part_refs, land_refs, send_sems, recv_sems, local_sems)
        _exchange_start(step == 0, exchange)

        @pl.when(step == 0)
        def _():
            dbb_ref[...] = jnp.zeros_like(dbb_ref)
            dws_ref[...] = jnp.zeros_like(dws_ref)
            dlg_ref[...] = jnp.zeros_like(dlg_ref)
            dlb_ref[...] = jnp.zeros_like(dlb_ref)
            dbias_acc[...] = jnp.zeros_like(dbias_acc)

        zb = zb_ref[...]
        lg = lg_ref[...]
        ug, dug, dvg, rstd, xh, vn_bf, grp, tri, tri_t, mixed, sb = _sgu_math(
            u_ref[...], v_ref[...], zb, lg, lb_ref[...], ws_ref, bias_ref[...])
        dyb = dyb_ref[...]
        dsgu = dyb * (zb * sb)
        dzb = dyb * (ug * mixed) * (sb * (1.0 + zb * (1.0 - sb)))
        du = dsgu * mixed * dug
        dmixed = dsgu * ug
        dbias_acc[...] += dmixed
        dvn = jnp.zeros((CHUNK, SGU_WIDTH), F32)
        for g in range(SGU_GROUPS):
            dm_g = jnp.where(grp == g, dmixed, 0.0).astype(BF16)
            wtt = jnp.where(tri_t, wst_ref[g], 0.0).astype(BF16)
            dvn = dvn + _dot(wtt, dm_g, _NN)
            dws_ref[g] += jnp.where(tri, _dot(dm_g, vn_bf, _NT), 0.0)
        dlg_ref[...] += jnp.sum(dvn * xh, axis=0, keepdims=True)
        dlb_ref[...] += jnp.sum(dvn, axis=0, keepdims=True)
        dxh = dvn * lg
        dvgel = rstd * (dxh - jnp.mean(dxh, axis=1, keepdims=True) - xh * jnp.mean(dxh * xh, axis=1, keepdims=True))
        _store_grad(dhb_ref, dbb_ref, 0, du)
        _store_grad(dhb_ref, dbb_ref, SGU_WIDTH, dvgel * dvg)
        _store_grad(dhb_ref, dbb_ref, 2 * SGU_WIDTH, dzb)

        @pl.when(step == nsteps - 1)
        def _():
            acc = dbias_acc[...]
            lane = lax.broadcasted_iota(jnp.int32, (CHUNK, LANES), 1)
            out = jnp.zeros((CHUNK, LANES), F32)
            for g in range(SGU_GROUPS):
                sg = jnp.sum(jnp.where(grp == g, acc, 0.0), axis=1, keepdims=True)
                out = jnp.where(lane == g, sg, out)
            dbs_ref[...] = out

        _exchange_finish(step == nsteps - 1, exchange)

    blk = lambda c: pl.BlockSpec((CHUNK, SGU_WIDTH), lambda i, c=c: (i, c))
    full2 = lambda shape: pl.BlockSpec(shape, lambda i: (0, 0))
    full3 = pl.BlockSpec((SGU_GROUPS, CHUNK, CHUNK), lambda i: (0, 0, 0))
    hbm = pl.BlockSpec(memory_space=pl.ANY)
    res = pl.pallas_call(
        body, name="sgu_bwd", grid=(nsteps,),
        in_specs=[blk(0), blk(1), blk(2), pl.BlockSpec((CHUNK, SGU_WIDTH), lambda i: (i, 0)),
                  full2((1, SGU_WIDTH)), full2((1, SGU_WIDTH)), full3, full3, full2((CHUNK, SGU_WIDTH))] + [hbm] * npart,
        out_specs=[pl.BlockSpec((CHUNK, SEG_B), lambda i: (i, 0)), full3, full2((CHUNK, LANES)),
                   full2((1, SGU_WIDTH)), full2((1, SGU_WIDTH)), full2((1, SEG_B))] + [hbm] * npart,
        out_shape=[jax.ShapeDtypeStruct((SEQ, SEG_B), BF16),
                   jax.ShapeDtypeStruct((SGU_GROUPS, CHUNK, CHUNK), F32),
                   jax.ShapeDtypeStruct((CHUNK, LANES), F32),
                   jax.ShapeDtypeStruct((1, SGU_WIDTH), F32), jax.ShapeDtypeStruct((1, SGU_WIDTH), F32),
                   jax.ShapeDtypeStruct((1, SEG_B), F32)] + [jax.ShapeDtypeStruct(p.shape, p.dtype) for p in parts],
        scratch_shapes=[pltpu.VMEM((CHUNK, SGU_WIDTH), F32)] + _exchange_sems(npart),
        compiler_params=pltpu.CompilerParams(dimension_semantics=("arbitrary",)),
    )(h_b, h_b, h_b, d_yb, lg, lb, w_s, w_st, bias_full, *parts)
    return res[:6], res[6:]


def _merge(x, o, h_a, y_b, target, w_oa, w_ob, w_out, ln_g, ln_b):
    tm = 256
    nsteps = SEQ // tm

    def body(x_ref, o_ref, ga_ref, gb_ref, za_ref, yb_ref, tgt_ref, woa_ref, wob_ref, wout_ref, lng_ref, lnb_ref,
             loss_ref, dxr_ref, dha_ref, do_ref, dyb_ref, poa_ref, pob_ref, pout_ref, dlng_ref, dlnb_ref, dba_ref,
             dwoa_ref, dwob_ref, dwout_ref):
        step = pl.program_id(0)

        @pl.when(step == 0)
        def _():
            for r in (loss_ref, dwoa_ref, dwob_ref, dwout_ref, dlng_ref, dlnb_ref, dba_ref):
                r[...] = jnp.zeros_like(r)

        o = o_ref[...]
        za = za_ref[...]
        sa = _sigmoid(za)
        ya_bf = (o * (za * sa)).astype(BF16)
        yb_bf = yb_ref[...].astype(BF16)
        woa, wob, wout = woa_ref[...], wob_ref[...], wout_ref[...]
        pa = _dot(ya_bf, woa, _NN)
        pb = _dot(yb_bf, wob, _NN)
        sga = _sigmoid(ga_ref[...])
        sgb = _sigmoid(gb_ref[...])
        merged_bf = (sga * pa + sgb * pb).astype(BF16)
        r = DN_ALPHA * x_ref[...] + _dot(merged_bf, wout, _NN)
        mu = jnp.mean(r, axis=1, keepdims=True)
        rc = r - mu
        rstd = lax.rsqrt(jnp.mean(rc * rc, axis=1, keepdims=True) + LN_EPS)
        xh = rc * rstd
        lng = lng_ref[...]
        y = xh * lng + lnb_ref[...]
        e = y - tgt_ref[...]
        loss_ref[...] += 0.5 * jnp.sum(jnp.sum(e * e, axis=1, keepdims=True) * (1.0 / D_MODEL), axis=0, keepdims=True)

        dy = e * (1.0 / D_MODEL)
        dlng_ref[...] += jnp.sum(dy * xh, axis=0, keepdims=True)
        dlnb_ref[...] += jnp.sum(dy, axis=0, keepdims=True)
        dxh = dy * lng
        dr = rstd * (dxh - jnp.mean(dxh, axis=1, keepdims=True) - xh * jnp.mean(dxh * xh, axis=1, keepdims=True))
        dxr_ref[...] = DN_ALPHA * dr
        dr_bf = dr.astype(BF16)
        dwout_ref[...] += _dot(merged_bf, dr_bf, _TN)
        dmerged = _dot(dr_bf, wout, _NT)
        dpa_bf = (dmerged * sga).astype(BF16)
        dpb_bf = (dmerged * sgb).astype(BF16)
        _store_grad(dha_ref, dba_ref, 0, dmerged * pa * (sga * (1.0 - sga)))
        _store_grad(dha_ref, dba_ref, D_MODEL, dmerged * pb * (sgb * (1.0 - sgb)))
        dwoa_ref[...] += _dot(ya_bf, dpa_bf, _TN)
        dwob_ref[...] += _dot(yb_bf, dpb_bf, _TN)
        dya = _dot(dpa_bf, woa, _NT)
        dyb_ref[...] = _dot(dpb_bf, wob, _NT)
        do_ref[...] = dya * (za * sa)
        _store_grad(dha_ref, dba_ref, 2 * D_MODEL, dya * o * (sa * (1.0 + za * (1.0 - sa))))

        @pl.when(step == nsteps - 1)
        def _():
            cols = D_MODEL // N_DEV
            for j in range(N_DEV):
                poa_ref[j] = dwoa_ref[:, cols * j:cols * (j + 1)].astype(BF16)
                pob_ref[j] = dwob_ref[:, cols * j:cols * (j + 1)].astype(BF16)
                pout_ref[j] = dwout_ref[cols * j:cols * (j + 1), :].astype(BF16)

    row = lambda w, c=0: pl.BlockSpec((tm, w), lambda i, c=c: (i, c))
    full = lambda shape: pl.BlockSpec(shape, lambda i: (0, 0))
    full3 = lambda shape: pl.BlockSpec(shape, lambda i: (0, 0, 0))
    return pl.pallas_call(
        body, name="merge", grid=(nsteps,),
        in_specs=[row(D_MODEL), row(MLA_WIDTH), row(D_MODEL, 0), row(D_MODEL, 1), row(MLA_WIDTH, 4), row(SGU_WIDTH),
                  row(D_MODEL), full((MLA_WIDTH, D_MODEL)), full((SGU_WIDTH, D_MODEL)), full((D_MODEL, D_MODEL)),
                  full((1, D_MODEL)), full((1, D_MODEL))],
        out_specs=[full((1, LANES)), row(D_MODEL), row(SEG_A), row(MLA_WIDTH), row(SGU_WIDTH),
                   full3((N_DEV, MLA_WIDTH, D_MODEL // N_DEV)), full3((N_DEV, SGU_WIDTH, D_MODEL // N_DEV)),
                   full3((N_DEV, D_MODEL // N_DEV, D_MODEL)), full((1, D_MODEL)), full((1, D_MODEL)), full((1, SEG_A))],
        out_shape=[jax.ShapeDtypeStruct((1, LANES), F32),
                   jax.ShapeDtypeStruct((SEQ, D_MODEL), F32), jax.ShapeDtypeStruct((SEQ, SEG_A), BF16),
                   jax.ShapeDtypeStruct((SEQ, MLA_WIDTH), F32), jax.ShapeDtypeStruct((SEQ, SGU_WIDTH), F32),
                   jax.ShapeDtypeStruct((N_DEV, MLA_WIDTH, D_MODEL // N_DEV), BF16),
                   jax.ShapeDtypeStruct((N_DEV, SGU_WIDTH, D_MODEL // N_DEV), BF16),
                   jax.ShapeDtypeStruct((N_DEV, D_MODEL // N_DEV, D_MODEL), BF16),
                   jax.ShapeDtypeStruct((1, D_MODEL), F32), jax.ShapeDtypeStruct((1, D_MODEL), F32),
                   jax.ShapeDtypeStruct((1, SEG_A), F32)],
        scratch_shapes=[pltpu.VMEM((MLA_WIDTH, D_MODEL), F32), pltpu.VMEM((SGU_WIDTH, D_MODEL), F32),
                        pltpu.VMEM((D_MODEL, D_MODEL), F32)],
        compiler_params=pltpu.CompilerParams(dimension_semantics=("arbitrary",), vmem_limit_bytes=VMEM_BIG),
    )(x, o, h_a, h_a, h_a, y_b, target, w_oa, w_ob, w_out, ln_g, ln_b)


def _mla_bwd(dq, dk, dv, h_c, gq, gkv, wq, wkn, wv, c_t, sa_t, sb_t):
    tm = 256
    hw = MLA_HEADS * HEAD_PAD

    def body(dq_ref, dk_ref, dv_ref, cq_ref, ckv_ref, gq_ref, gkv_ref, wq_ref, wkn_ref, wv_ref, c_ref, sa_ref, sb_ref,
             dhc_ref, puq_ref, dwkn_ref, dwv_ref, dgq_ref, dgkv_ref, dbc_ref, pre_ref, dwq_ref):
        @pl.when(pl.program_id(0) == 0)
        def _():
            for r in (dwq_ref, dwkn_ref, dwv_ref, dgq_ref, dgkv_ref, dbc_ref):
                r[...] = jnp.zeros_like(r)

        c, sa, sb = c_ref[...], sa_ref[...], sb_ref[...]
        lane = lax.broadcasted_iota(jnp.int32, (tm, LANES), 1)
        rope_lanes = jnp.logical_and(lane >= ROPE_LO, lane < ROPE_HI)

        cq = cq_ref[...]
        gq = gq_ref[...]
        rq = lax.rsqrt(jnp.sum(cq * cq, axis=1, keepdims=True) * (1.0 / Q_LORA_RANK) + RMS_EPS)
        nq = cq * rq
        cqn_bf = (nq * gq).astype(BF16)
        for h in range(MLA_HEADS):
            sl = slice(HEAD_PAD * h, HEAD_PAD * (h + 1))
            pre_ref[:, sl] = _rope_t(dq_ref[:, sl] * ATTN_SCALE, c, sa, sb).astype(BF16)
        dqpre_bf = pre_ref[...]
        dcqn = _dot(dqpre_bf, wq_ref[...], _NT)
        dwq_ref[...] += _dot(cqn_bf, dqpre_bf, _TN)
        dgq_ref[...] += jnp.sum(dcqn * nq, axis=0, keepdims=True)
        dnq = dcqn * gq
        _store_grad(dhc_ref, dbc_ref, 0,
                    rq * (dnq - nq * (jnp.sum(dnq * nq, axis=1, keepdims=True) * (1.0 / Q_LORA_RANK))))

        ckv = ckv_ref[...]
        gkv = gkv_ref[...]
        rkv = lax.rsqrt(jnp.sum(ckv * ckv, axis=1, keepdims=True) * (1.0 / KV_LORA_RANK) + RMS_EPS)
        nkv = ckv * rkv
        ckvn_bf = (nkv * gkv).astype(BF16)
        dk = dk_ref[...]
        dk_bf = dk.astype(BF16)
        dv_bf = dv_ref[...].astype(BF16)
        dckvn = _dot(dk_bf, wkn_ref[...], _NT) + _dot(dv_bf, wv_ref[...], _NT)
        dwkn_ref[...] += _dot(ckvn_bf, dk_bf, _TN)
        dwv_ref[...] += _dot(ckvn_bf, dv_bf, _TN)
        dgkv_ref[...] += jnp.sum(dckvn * nkv, axis=0, keepdims=True)
        dnkv = dckvn * gkv
        _store_grad(dhc_ref, dbc_ref, CQ_PAD, rkv * (
            dnkv - nkv * (jnp.sum(dnkv * nkv, axis=1, keepdims=True) * (1.0 / KV_LORA_RANK))))
        dkpe = jnp.zeros((tm, LANES), F32)
        for h in range(MLA_HEADS):
            dkpe = dkpe + dk[:, HEAD_PAD * h:HEAD_PAD * (h + 1)]
        _store_grad(dhc_ref, dbc_ref, CQ_PAD + LANES, _rope_t(jnp.where(rope_lanes, dkpe, 0.0), c, sa, sb))

        @pl.when(pl.program_id(0) == SEQ // tm - 1)
        def _():
            rows = Q_LORA_RANK // N_DEV
            for j in range(N_DEV):
                for h in range(MLA_HEADS):
                    puq_ref[j, :, QK_HEAD_DIM * h:QK_HEAD_DIM * (h + 1)] = dwq_ref[
                        rows * j:rows * (j + 1), HEAD_PAD * h:HEAD_PAD * h + QK_HEAD_DIM].astype(BF16)

    full = lambda shape: pl.BlockSpec(shape, lambda i: (0, 0))
    row = lambda w, c=0: pl.BlockSpec((tm, w), lambda i, c=c: (i, c))
    return pl.pallas_call(
        body, name="mla_bwd", grid=(SEQ // tm,),
        in_specs=[row(hw), row(hw), row(hw), row(CQ_PAD, 0), row(LANES, CQ_PAD // LANES),
                  full((1, CQ_PAD)), full((1, KV_LORA_RANK)), full((CQ_PAD, hw)), full((KV_LORA_RANK, hw)),
                  full((KV_LORA_RANK, hw)), row(LANES), row(LANES), row(LANES)],
        out_specs=[row(SEG_C), pl.BlockSpec((N_DEV, Q_LORA_RANK // N_DEV, MLA_HEADS * QK_HEAD_DIM), lambda i: (0, 0, 0)),
                   full((KV_LORA_RANK, hw)), full((KV_LORA_RANK, hw)),
                   full((1, CQ_PAD)), full((1, KV_LORA_RANK)), full((1, SEG_C))],
        out_shape=[jax.ShapeDtypeStruct((SEQ, SEG_C), BF16),
                   jax.ShapeDtypeStruct((N_DEV, Q_LORA_RANK // N_DEV, MLA_HEADS * QK_HEAD_DIM), BF16),
                   jax.ShapeDtypeStruct((KV_LORA_RANK, hw), F32), jax.ShapeDtypeStruct((KV_LORA_RANK, hw), F32),
                   jax.ShapeDtypeStruct((1, CQ_PAD), F32), jax.ShapeDtypeStruct((1, KV_LORA_RANK), F32),
                   jax.ShapeDtypeStruct((1, SEG_C), F32)],
        scratch_shapes=[pltpu.VMEM((tm, hw), BF16), pltpu.VMEM((CQ_PAD, hw), F32)],
        compiler_params=pltpu.CompilerParams(dimension_semantics=("arbitrary",), vmem_limit_bytes=VMEM_MID),
    )(dq, dk, dv, h_c, h_c, gq, gkv, wq, wkn, wv, c_t, sa_t, sb_t)


def _adamw_all(ws, gs, ms, vs):
    n = len(ws)
    c1 = 1.0 / (1.0 - ADAM_B1 ** ADAM_STEP)
    c2 = 1.0 / (1.0 - ADAM_B2 ** ADAM_STEP)

    def body(*refs):
        for idx in range(n):
            w, g, m, v = (refs[idx][...], refs[n + idx][...], refs[2 * n + idx][...], refs[3 * n + idx][...])
            m_new = ADAM_B1 * m + (1.0 - ADAM_B1) * g
            v_new = ADAM_B2 * v + (1.0 - ADAM_B2) * (g * g)
            delta = -ADAM_LR * ((m_new * c1) / (jnp.sqrt(v_new * c2) + ADAM_EPS) + ADAM_WD * w)
            refs[4 * n + idx][...] = delta
            refs[5 * n + idx][...] = m_new
            refs[6 * n + idx][...] = v_new

    shapes = [jax.ShapeDtypeStruct(w.shape, F32) for w in ws]
    outs = pl.pallas_call(
        body, name="adamw", out_shape=shapes * 3,
        compiler_params=pltpu.CompilerParams(vmem_limit_bytes=VMEM_BIG),
    )(*ws, *gs, *ms, *vs)
    return outs[:n], outs[n:2 * n], outs[2 * n:]


SHARD_W = IN_WIDTH // N_DEV
W_IN_LO = 128

_PIECES = [(0, 384, 2, 0), (384, 512, 2, CQ_PAD), (512, 544, 2, CQ_PAD + LANES + ROPE_LO),
           (544, 1056, 0, 2 * D_MODEL), (1056, 1568, 1, 0), (1568, 2080, 1, SGU_WIDTH),
           (2080, 2592, 1, 2 * SGU_WIDTH), (2592, 3616, 0, 0), (3616, 4640, 0, D_MODEL)]


def _column_runs():
    runs = []
    for n0, n1, seg, d0 in _PIECES:
        for j in range(N_DEV):
            lo, hi = max(n0, j * SHARD_W), min(n1, (j + 1) * SHARD_W)
            if lo < hi:
                runs.append((j, lo - j * SHARD_W, hi - j * SHARD_W, seg, d0 + lo - n0))
    return runs


def _mesh_pos():
    return lax.axis_index("x"), lax.axis_index("y"), lax.axis_index("c")


def _remote(src, dst, send_sems, recv_sems, k, to):
    return pltpu.make_async_remote_copy(src_ref=src, dst_ref=dst, send_sem=send_sems.at[k], recv_sem=recv_sems.at[k],
                                        device_id=to, device_id_type=pl.DeviceIdType.MESH)


def _gather_exchange(gats, send_sems, recv_sems, meanwhile=None):
    x, y, c = _mesh_pos()
    me, sibling = (x, y, c), (x, y, 1 - c)
    chips = [(1 - x, y), (x, 1 - y), (1 - x, 1 - y)]

    def copy(a, k, blk, to):
        slab = gats[a].at[4 * blk[0] + 2 * blk[1] + blk[2]]
        return _remote(slab, slab, send_sems, recv_sems, 7 * a + k, to)

    arrays = range(len(gats))
    first = [copy(a, 1 + j, me, (*chip, c)) for j, chip in enumerate(chips) for a in arrays]
    first += [copy(a, 0, me, sibling) for a in arrays]
    for cp in first:
        cp.start()
    if meanwhile is not None:
        meanwhile()
    passed = []
    for j, chip in enumerate(chips):
        for a in arrays:
            copy(a, 1 + j, (*chip, c), me).wait_recv()
            fwd = copy(a, 4 + j, (*chip, c), sibling)
            fwd.start()
            passed.append(fwd)
    for a in arrays:
        copy(a, 0, sibling, me).wait_recv()
    for j, chip in enumerate(chips):
        for a in arrays:
            copy(a, 4 + j, (*chip, 1 - c), me).wait_recv()
    for cp in first + passed:
        cp.wait_send()


def _gather_behind(own, gats, send_sems, recv_sems, local_sems, step, mid1, mid2, last):
    x, y, c = _mesh_pos()
    me, sibling = (x, y, c), (x, y, 1 - c)
    xn, yn, dg = (1 - x, y), (x, 1 - y), (1 - x, 1 - y)
    arrays = range(len(gats))

    def copy(a, k, blk, to, src=None):
        slab = gats[a].at[4 * blk[0] + 2 * blk[1] + blk[2]]
        return _remote(slab if src is None else src, slab, send_sems, recv_sems, 7 * a + k, to)

    first = [copy(a, 1 + j, me, (*chip, c), src=own[a]) for j, chip in enumerate((xn, yn)) for a in arrays]
    first += [copy(a, 0, me, sibling, src=own[a]) for a in arrays]
    local = [pltpu.make_async_copy(own[a], gats[a].at[4 * x + 2 * y + c], local_sems.at[a]) for a in arrays]
    passed = [copy(a, 4 + j, (*chip, c), sibling) for j, chip in enumerate((xn, yn)) for a in arrays]
    relays = [(c == 0, [copy(a, 3, (*xn, c), (*yn, c)) for a in arrays]),
              (c == 1, [copy(a, 3, (*yn, c), (*xn, c)) for a in arrays])]
    last_pass = [copy(a, 6, (*dg, c), sibling) for a in arrays]

    @pl.when(step == 0)
    def _():
        for cp in first + local:
            cp.start()

    @pl.when(step == mid1)
    def _():
        for j, chip in enumerate((xn, yn)):
            for a in arrays:
                copy(a, 1 + j, (*chip, c), me).wait_recv()
        for mine, cps in relays:
            @pl.when(mine)
            def _(cps=cps):
                for cp in cps:
                    cp.start()
        for cp in passed:
            cp.start()

    @pl.when(step == mid2)
    def _():
        for a in arrays:
            copy(a, 3, (*dg, c), me).wait_recv()
        for cp in last_pass:
            cp.start()

    @pl.when(step == last)
    def _():
        for a in arrays:
            copy(a, 0, sibling, me).wait_recv()
            for j, chip in enumerate((xn, yn, dg)):
                copy(a, 4 + j, (*chip, 1 - c), me).wait_recv()
        for cp in first + passed + last_pass:
            cp.wait_send()
        for mine, cps in relays:
            @pl.when(mine)
            def _(cps=cps):
                for cp in cps:
                    cp.wait_send()
        for cp in local:
            cp.wait()


def _gather_first(w_in, w_uq2, w_oa, w_ob, w_out, x2, pos_col, invf_lane):
    hw = MLA_HEADS * HEAD_PAD
    uq_rows = Q_LORA_RANK // N_DEV
    rows = 256

    def body(win_ref, wuq_ref, woa_ref, wob_ref, wout_ref, x_ref, pos_ref, invf_ref,
             wc_ref, wq_ref, winlo_ref, winhi_ref, oab_ref, obb_ref, outb_ref, xb_ref, xt_ref, c_ref, sa_ref, sb_ref,
             g_uq, blk0, send_sems, recv_sems):
        def local_work():
            for i in range(SEQ // rows):
                xi = x_ref[rows * i:rows * (i + 1), :]
                xb_ref[rows * i:rows * (i + 1), :] = xi.astype(BF16)
                xt_ref[:, rows * i:rows * (i + 1)] = xi.T.astype(BF16)
            ang = pos_ref[...].astype(F32) * invf_ref[...]
            cs, sn = jnp.cos(ang), jnp.sin(ang)
            lane = lax.broadcasted_iota(jnp.int32, ang.shape, 1)
            c_ref[...] = jnp.where(lane < ROPE_LO, 1.0, jnp.where(lane < ROPE_HI, cs, 0.0))
            sa_ref[...] = jnp.where(jnp.logical_and(lane >= ROPE_LO, lane < ROPE_MID), -sn, 0.0)
            sb_ref[...] = jnp.where(jnp.logical_and(lane >= ROPE_MID, lane < ROPE_HI), sn, 0.0)

        x, y, c = _mesh_pos()
        me = (x, y, c)
        winlo_ref[...] = win_ref[0, 0:W_IN_LO, :].astype(BF16)
        winhi_ref[...] = win_ref[0, W_IN_LO:D_MODEL, :].astype(BF16)
        oab_ref[...] = woa_ref[0].astype(BF16)
        obb_ref[...] = wob_ref[0].astype(BF16)
        outb_ref[...] = wout_ref[0].astype(BF16)
        g_uq[4 * x + 2 * y + c] = wuq_ref[...].astype(BF16)

        chip0 = jnp.logical_and(x == 0, y == 0)
        south = c == 0
        half = D_MODEL // 2
        halves = [blk0.at[pl.ds(0, half)], blk0.at[pl.ds(half, half)]]

        def bcopy(k, to, part=None):
            ref = blk0 if part is None else halves[part]
            return _remote(ref, ref, send_sems, recv_sems, 7 + k, to)

        sends0 = [(0, (0, 0, 1), None), (1, (1, 0, 0), 0), (2, (0, 1, 0), 1), (3, (1, 0, 0), 1), (4, (0, 1, 0), 0)]

        @pl.when(jnp.logical_and(chip0, south))
        def _():
            blk0[0:W_IN_LO, :] = winlo_ref[...]
            blk0[W_IN_LO:D_MODEL, :] = winhi_ref[...]
            for k, to, part in sends0:
                bcopy(k, to, part).start()

        _gather_exchange([g_uq], send_sems, recv_sems, meanwhile=local_work)

        for (cx, cy), first_k, first_half, second_k in (((1, 0), 1, 0, 3), ((0, 1), 2, 1, 4)):
            @pl.when(jnp.logical_and(jnp.logical_and(x == cx, y == cy), south))
            def _(cx=cx, cy=cy, first_k=first_k, first_half=first_half, second_k=second_k):
                bcopy(first_k, me, first_half).wait_recv()
                onward = bcopy(5 + first_half, (1, 1, 0), first_half)
                onward.start()
                bcopy(second_k, me, 1 - first_half).wait_recv()
                north = bcopy(7, (cx, cy, 1))
                north.start()
                onward.wait_send()
                north.wait_send()

        @pl.when(jnp.logical_and(jnp.logical_and(x == 1, y == 1), south))
        def _():
            bcopy(5, me, 0).wait_recv()
            bcopy(6, me, 1).wait_recv()
            north = bcopy(7, (1, 1, 1))
            north.start()
            north.wait_send()

        @pl.when(jnp.logical_and(chip0, c == 1))
        def _():
            bcopy(0, me).wait_recv()

        @pl.when(jnp.logical_and(jnp.logical_not(chip0), c == 1))
        def _():
            bcopy(7, me).wait_recv()

        @pl.when(jnp.logical_and(chip0, south))
        def _():
            for k, to, part in sends0:
                bcopy(k, to, part).wait_send()

        for j, s0, s1, seg, d0 in _column_runs():
            if seg == 2:
                wc_ref[:, d0:d0 + (s1 - s0)] = blk0[:, s0:s1]
        zeros = lambda r, w: jnp.zeros((r, w), BF16)
        wc_ref[:, Q_LORA_RANK:CQ_PAD] = zeros(D_MODEL, CQ_PAD - Q_LORA_RANK)
        wc_ref[:, CQ_PAD + LANES:CQ_PAD + LANES + ROPE_LO] = zeros(D_MODEL, ROPE_LO)
        wc_ref[:, CQ_PAD + LANES + ROPE_HI:SEG_C] = zeros(D_MODEL, LANES - ROPE_HI)
        wq_ref[Q_LORA_RANK:CQ_PAD, :] = zeros(CQ_PAD - Q_LORA_RANK, hw)
        for h in range(MLA_HEADS):
            wq_ref[0:Q_LORA_RANK, HEAD_PAD * h + QK_HEAD_DIM:HEAD_PAD * (h + 1)] = zeros(Q_LORA_RANK, HEAD_PAD - QK_HEAD_DIM)
        for j in range(N_DEV):
            for h in range(MLA_HEADS):
                wq_ref[uq_rows * j:uq_rows * (j + 1), HEAD_PAD * h:HEAD_PAD * h + QK_HEAD_DIM] = g_uq[
                    j, :, QK_HEAD_DIM * h:QK_HEAD_DIM * (h + 1)]

    vmem = pl.BlockSpec(memory_space=pltpu.VMEM)
    return pl.pallas_call(
        body, name="gather_first",
        out_shape=[jax.ShapeDtypeStruct((D_MODEL, SEG_C), BF16), jax.ShapeDtypeStruct((CQ_PAD, hw), BF16),
                   jax.ShapeDtypeStruct((W_IN_LO, SHARD_W), BF16), jax.ShapeDtypeStruct((D_MODEL - W_IN_LO, SHARD_W), BF16),
                   jax.ShapeDtypeStruct(w_oa.shape[1:], BF16),
                   jax.ShapeDtypeStruct(w_ob.shape[1:], BF16), jax.ShapeDtypeStruct(w_out.shape[1:], BF16),
                   jax.ShapeDtypeStruct((SEQ, D_MODEL), BF16), jax.ShapeDtypeStruct((D_MODEL, SEQ), BF16)]
        + [jax.ShapeDtypeStruct((SEQ, LANES), F32)] * 3,
        in_specs=[vmem] * 8, out_specs=[vmem] * 12,
        scratch_shapes=[pltpu.VMEM((N_DEV, uq_rows, MLA_HEADS * QK_HEAD_DIM), BF16), pltpu.VMEM((D_MODEL, SHARD_W), BF16),
                        pltpu.SemaphoreType.DMA((15,)), pltpu.SemaphoreType.DMA((15,))],
        compiler_params=pltpu.CompilerParams(vmem_limit_bytes=VMEM_BIG),
    )(w_in, w_uq2, w_oa, w_ob, w_out, x2, pos_col, invf_lane)


def _assemble_in(g_lo, g_hi):
    def body(glo_ref, ghi_ref, wa_ref, wb_ref):
        segs = [wa_ref, wb_ref]
        for j, s0, s1, seg, d0 in _column_runs():
            if seg < 2:
                segs[seg][0:W_IN_LO, d0:d0 + (s1 - s0)] = glo_ref[j, :, s0:s1]
                segs[seg][W_IN_LO:D_MODEL, d0:d0 + (s1 - s0)] = ghi_ref[j, :, s0:s1]

    return pl.pallas_call(
        body, name="assemble_in",
        out_shape=[jax.ShapeDtypeStruct((D_MODEL, SEG_A), BF16), jax.ShapeDtypeStruct((D_MODEL, SEG_B), BF16)],
        compiler_params=pltpu.CompilerParams(vmem_limit_bytes=VMEM_MID),
    )(g_lo, g_hi)


def _assemble_out(g_oa, g_ob, g_out):
    cols = D_MODEL // N_DEV

    def body(goa_ref, gob_ref, gout_ref, oa_ref, ob_ref, out_ref):
        for j in range(N_DEV):
            oa_ref[:, cols * j:cols * (j + 1)] = goa_ref[j]
            ob_ref[:, cols * j:cols * (j + 1)] = gob_ref[j]
            out_ref[cols * j:cols * (j + 1), :] = gout_ref[j]

    return pl.pallas_call(
        body, name="assemble_out",
        out_shape=[jax.ShapeDtypeStruct((MLA_WIDTH, D_MODEL), BF16), jax.ShapeDtypeStruct((SGU_WIDTH, D_MODEL), BF16),
                   jax.ShapeDtypeStruct((D_MODEL, D_MODEL), BF16)],
    )(g_oa, g_ob, g_out)


C_NAT = 544


def _to_parts(dwa, dwb):
    def body(dwa_ref, dwb_ref, pin_ref):
        pin_ref[0, :, 0:C_NAT] = jnp.zeros((D_MODEL, C_NAT), BF16)
        segs = [dwa_ref, dwb_ref]
        for j, s0, s1, seg, d0 in _column_runs():
            if seg < 2:
                pin_ref[j, :, s0:s1] = segs[seg][:, d0:d0 + (s1 - s0)]

    return pl.pallas_call(body, name="to_parts", out_shape=jax.ShapeDtypeStruct((N_DEV, D_MODEL, SHARD_W), BF16),
                          compiler_params=pltpu.CompilerParams(vmem_limit_bytes=VMEM_MID))(dwa, dwb)


def _dx_tail(dhs, ws, dx_res, dwc, p_uq, p_rep):
    tm = SEQ // 4
    rep_rows = p_rep.shape[1]
    c_rows = D_MODEL // N_DEV
    spec = [((c_rows, C_NAT), BF16), (p_uq.shape[1:], BF16), ((rep_rows, LANES), F32)]
    n = len(spec)

    nseg = len(dhs)

    def body(*refs):
        dh_refs, w_refs = refs[:nseg], refs[nseg:2 * nseg]
        dxr_ref, dwc_ref, puq_ref, prep_ref, dx_ref, call_ref, guq_ref, repall_ref, pc_ref, c_all, rep_all = refs[
            2 * nseg:2 * nseg + 11]
        rest = refs[2 * nseg + 11:]
        ras, tbs, rbs = rest[0:n], rest[n:2 * n], rest[2 * n:3 * n]
        send_sems, recv_sems, gsend, grecv = rest[3 * n:]
        step = pl.program_id(0)
        x, y, c = _mesh_pos()
        me_idx = 4 * x + 2 * y + c
        me, sibling = (x, y, c), (x, y, 1 - c)
        others = [(1 - x, y), (x, 1 - y), (1 - x, 1 - y)]
        parts = [pc_ref, puq_ref, prep_ref]
        gats = [rep_all, c_all]

        def stage1(chip, a):
            return _remote(parts[a].at[2 * chip + (1 - c)], ras[a].at[chip], send_sems, recv_sems, 7 * a + chip, sibling)

        def stage2(k, a):
            cx, cy = others[k]
            return _remote(tbs[a].at[k], rbs[a].at[k], send_sems, recv_sems, 7 * a + 4 + k, (cx, cy, c))

        def gcopy(a, k, blk, to):
            slab = gats[a].at[4 * blk[0] + 2 * blk[1] + blk[2]]
            return _remote(slab, slab, gsend, grecv, 7 * a + k, to)

        def chip_sum(a, chip):
            return parts[a][2 * chip + c].astype(F32) + ras[a][chip].astype(F32)

        @pl.when(step == 0)
        def _():
            for j, s0, s1, seg, d0 in _column_runs():
                if seg == 2:
                    for r in range(N_DEV):
                        pc_ref[r, :, s0:s1] = dwc_ref[c_rows * r:c_rows * (r + 1), d0:d0 + (s1 - s0)]
            for chip in range(4):
                for a in range(n):
                    stage1(chip, a).start()

        @pl.when(step == 1)
        def _():
            for chip in range(4):
                for a in range(n):
                    stage1(chip, a).wait_recv()
            for k, (cx, cy) in enumerate(others):
                for a in range(n):
                    tbs[a][k] = chip_sum(a, 2 * cx + cy).astype(spec[a][1])
                    stage2(k, a).start()

        @pl.when(step == 2)
        def _():
            for k in range(3):
                for a in range(n):
                    stage2(k, a).wait_recv()
            sums = []
            for a in range(n):
                acc = chip_sum(a, 2 * x + y)
                for k in range(3):
                    acc = acc + rbs[a][k].astype(F32)
                sums.append(acc)
            c_all[me_idx] = sums[0].astype(BF16)
            guq_ref[...] = sums[1]
            rep_all[me_idx] = sums[2]
            for a in range(2):
                for j, chip in enumerate(others):
                    gcopy(a, 1 + j, me, (*chip, c)).start()
                gcopy(a, 0, me, sibling).start()

        @pl.when(step == 3)
        def _():
            for j, chip in enumerate(others):
                for a in range(2):
                    gcopy(a, 1 + j, (*chip, c), me).wait_recv()
                    gcopy(a, 4 + j, (*chip, c), sibling).start()
            for a in range(2):
                gcopy(a, 0, sibling, me).wait_recv()
                for j, chip in enumerate(others):
                    gcopy(a, 4 + j, (*chip, 1 - c), me).wait_recv()
            for a in range(2):
                gcopy(a, 0, me, sibling).wait_send()
                for j, chip in enumerate(others):
                    gcopy(a, 1 + j, me, (*chip, c)).wait_send()
                    gcopy(a, 4 + j, (*chip, c), sibling).wait_send()
            for a in range(n):
                for chip in range(4):
                    stage1(chip, a).wait_send()
                for k in range(3):
                    stage2(k, a).wait_send()
            call_ref[...] = c_all[...]
            repall_ref[...] = rep_all[...]

        acc = dxr_ref[...]
        for dh_ref, w_ref in zip(dh_refs, w_refs):
            acc = acc + _dot(dh_ref[...], w_ref[...], _NT)
        dx_ref[...] = acc

    row = lambda w: pl.BlockSpec((tm, w), lambda i: (i, 0))
    full = lambda shape: pl.BlockSpec(shape, lambda i: (0,) * len(shape))
    scratch = [pltpu.VMEM((N_DEV, c_rows, C_NAT), BF16), pltpu.VMEM((N_DEV, c_rows, C_NAT), BF16),
               pltpu.VMEM((N_DEV, rep_rows, LANES), F32)]
    for lead in (4, 3, 3):
        scratch += [pltpu.VMEM((lead,) + tuple(shape), dt) for shape, dt in spec]
    scratch += [pltpu.SemaphoreType.DMA((7 * n,)), pltpu.SemaphoreType.DMA((7 * n,)),
                pltpu.SemaphoreType.DMA((14,)), pltpu.SemaphoreType.DMA((14,))]
    return pl.pallas_call(
        body, name="dx_tail", grid=(SEQ // tm,),
        in_specs=[row(dh.shape[1]) for dh in dhs] + [full(w.shape) for w in ws]
        + [row(D_MODEL), full(dwc.shape), full(p_uq.shape), full(p_rep.shape)],
        out_specs=[row(D_MODEL), full((N_DEV, c_rows, C_NAT)), full(p_uq.shape[1:]), full((N_DEV, rep_rows, LANES))],
        out_shape=[jax.ShapeDtypeStruct((SEQ, D_MODEL), F32), jax.ShapeDtypeStruct((N_DEV, c_rows, C_NAT), BF16),
                   jax.ShapeDtypeStruct(p_uq.shape[1:], F32), jax.ShapeDtypeStruct((N_DEV, rep_rows, LANES), F32)],
        scratch_shapes=scratch,
        compiler_params=pltpu.CompilerParams(dimension_semantics=("arbitrary",), vmem_limit_bytes=VMEM_BIG),
    )(*dhs, *ws, dx_res, dwc, p_uq, p_rep)


def _sum_landed(landed, c_all):
    c_rows = D_MODEL // N_DEV

    def body(rin_ref, roa_ref, rob_ref, rout_ref, call_ref, gin_ref, goa_ref, gob_ref, gout_ref):
        def total(ref, sl):
            acc = ref[0, sl, :].astype(F32)
            for s in range(1, N_DEV):
                acc = acc + ref[s, sl, :].astype(F32)
            return acc

        x, y, c = _mesh_pos()
        dev0 = jnp.where(4 * x + 2 * y + c == 0, 1.0, 0.0)
        for j in range(N_DEV):
            sl = slice(c_rows * j, c_rows * (j + 1))
            tot = total(rin_ref, sl)
            gin_ref[0, sl, C_NAT:SHARD_W] = tot[:, C_NAT:SHARD_W]
            gin_ref[0, sl, 0:C_NAT] = tot[:, 0:C_NAT] + dev0 * call_ref[j].astype(F32)
        goa_ref[0] = total(roa_ref, slice(None))
        gob_ref[0] = total(rob_ref, slice(None))
        gout_ref[0] = total(rout_ref, slice(None))

    return pl.pallas_call(
        body, name="sum_landed",
        out_shape=[jax.ShapeDtypeStruct((1,) + r.shape[1:], F32) for r in landed],
        compiler_params=pltpu.CompilerParams(vmem_limit_bytes=VMEM_MID),
    )(*landed, c_all)


_O_CQ, _O_CKV, _O_KPE, _O_ZA, _O_U, _O_V, _O_ZB, _O_GA, _O_GB = 0, 384, 512, 544, 1056, 1568, 2080, 2592, 3616


def _to_segments(w):
    z = lambda n: jnp.zeros(w.shape[:-1] + (n,), w.dtype)
    seg_a = jnp.concatenate([w[..., _O_GA:_O_GB], w[..., _O_GB:IN_WIDTH], w[..., _O_ZA:_O_U]], axis=-1)
    seg_b = jnp.concatenate([w[..., _O_U:_O_V], w[..., _O_V:_O_ZB], w[..., _O_ZB:_O_GA]], axis=-1)
    seg_c = jnp.concatenate([w[..., _O_CQ:_O_CKV], z(CQ_PAD - Q_LORA_RANK), w[..., _O_CKV:_O_KPE],
                             z(ROPE_LO), w[..., _O_KPE:_O_ZA], z(LANES - ROPE_HI)], axis=-1)
    return seg_a, seg_b, seg_c


def _from_segments(seg_a, seg_b, seg_c):
    kpe0 = CQ_PAD + LANES + ROPE_LO
    return jnp.concatenate([
        seg_c[..., 0:Q_LORA_RANK], seg_c[..., CQ_PAD:CQ_PAD + LANES], seg_c[..., kpe0:kpe0 + QK_ROPE_DIM],
        seg_a[..., 2 * D_MODEL:SEG_A], seg_b, seg_a[..., 0:2 * D_MODEL]], axis=-1)


def kernel(x, positions, w_in, b_in, g_q, w_uq, g_kv, w_ukv, w_oa, sgu_ln_g, sgu_ln_b, w_s, b_s, w_ob, w_out, ln_g, ln_b, loss_target, m_w_in, m_b_in, m_g_q, m_w_uq, m_g_kv, m_w_ukv, m_w_oa, m_sgu_ln_g, m_sgu_ln_b, m_w_s, m_b_s, m_w_ob, m_w_out, m_ln_g, m_ln_b, v_w_in, v_b_in, v_g_q, v_w_uq, v_g_kv, v_w_ukv, v_w_oa, v_sgu_ln_g, v_sgu_ln_b, v_w_s, v_b_s, v_w_ob, v_w_out, v_ln_g, v_ln_b):
    w_uq2 = w_uq[0].reshape(Q_LORA_RANK // N_DEV, MLA_HEADS * QK_HEAD_DIM)
    inv_freq = ROPE_THETA ** (-jnp.arange(0, QK_ROPE_DIM, 2, dtype=F32) / QK_ROPE_DIM)
    invf_lane = jnp.concatenate([jnp.zeros((ROPE_LO,), F32), inv_freq, inv_freq,
                                 jnp.zeros((LANES - ROPE_HI,), F32)]).reshape(1, LANES)
    first = _gather_first(w_in, w_uq2, w_oa, w_ob, w_out, x[0], positions.reshape(SEQ, 1), invf_lane)
    partials = _local_step(x[0], loss_target[0], first, b_in, g_q, g_kv, w_ukv, sgu_ln_g, sgu_ln_b, w_s, b_s, ln_g, ln_b)
    weights = dict(w_in=w_in, b_in=b_in, g_q=g_q, w_uq=w_uq, g_kv=g_kv, w_ukv=w_ukv, w_oa=w_oa, sgu_ln_g=sgu_ln_g,
                   sgu_ln_b=sgu_ln_b, w_s=w_s, b_s=b_s, w_ob=w_ob, w_out=w_out, ln_g=ln_g, ln_b=ln_b)
    moms = dict(w_in=m_w_in, b_in=m_b_in, g_q=m_g_q, w_uq=m_w_uq, g_kv=m_g_kv, w_ukv=m_w_ukv, w_oa=m_w_oa,
                sgu_ln_g=m_sgu_ln_g, sgu_ln_b=m_sgu_ln_b, w_s=m_w_s, b_s=m_b_s, w_ob=m_w_ob, w_out=m_w_out,
                ln_g=m_ln_g, ln_b=m_ln_b)
    vars_ = dict(w_in=v_w_in, b_in=v_b_in, g_q=v_g_q, w_uq=v_w_uq, g_kv=v_g_kv, w_ukv=v_w_ukv, w_oa=v_w_oa,
                 sgu_ln_g=v_sgu_ln_g, sgu_ln_b=v_sgu_ln_b, w_s=v_w_s, b_s=v_b_s, w_ob=v_w_ob, w_out=v_w_out,
                 ln_g=v_ln_g, ln_b=v_ln_b)
    return _reduce_and_update(partials, weights, moms, vars_)


def _local_step(x2, tgt, first, b_in, g_q, g_kv, w_ukv, sgu_ln_g, sgu_ln_b, w_s, b_s, ln_g, ln_b):
    wc, wq, win_lo, win_hi, oa_b, ob_b, out_b, x_bf, xt_bf, c_t, sa_t, sb_t = first
    ba, bb, bc = _to_segments(b_in)
    w_ukv_bf = w_ukv[0].astype(BF16)
    wkn = jnp.pad(w_ukv_bf[:, :, :QK_NOPE_DIM], ((0, 0), (0, 0), (0, HEAD_PAD - QK_NOPE_DIM))).reshape(KV_LORA_RANK, -1)
    wv = jnp.pad(w_ukv_bf[:, :, QK_NOPE_DIM:], ((0, 0), (0, 0), (0, HEAD_PAD - V_HEAD_DIM))).reshape(KV_LORA_RANK, -1)
    gq = jnp.pad(g_q, ((0, 0), (0, CQ_PAD - Q_LORA_RANK)))
    bias_full = jnp.repeat(b_s[0].T, SGU_GROUP_DIM, axis=1)
    w_s3 = w_s[0]
    w_st3 = jnp.swapaxes(w_s3, 1, 2)

    h_c = _mm(x_bf, wc, bias=bc, tm=512, tn=SEG_C, name="in_proj_c")
    q, k, kt, vx, vxt = _mla_prep(h_c, gq, g_kv, wq, wkn, wv, c_t, sa_t, sb_t)
    o, lse, (g_lo, g_hi) = _attn_fwd(q, kt, vx, (win_lo, win_hi))
    wa, wb = _assemble_in(g_lo, g_hi)
    h_a, (g_out,) = _mm(x_bf, wa, bias=ba, own=(out_b,), tm=512, tn=SEG_A // 2, name="in_proj_a")
    h_b, (g_oa, g_ob) = _mm(x_bf, wb, bias=bb, own=(oa_b, ob_b), tm=512, tn=SEG_B // 2, name="in_proj_b")
    y_b = _sgu_fwd(h_b, sgu_ln_g, sgu_ln_b, w_s3, bias_full)
    w_oa_f, w_ob_f, w_out_f = _assemble_out(g_oa, g_ob, g_out)

    (loss_row, dx_res, dh_a, d_o, d_yb, p_oa, p_ob, p_out, d_lng, d_lnb, d_ba) = _merge(
        x2, o, h_a, y_b, tgt, w_oa_f, w_ob_f, w_out_f, ln_g, ln_b)
    (dh_b, d_ws, d_bs_t, d_slg, d_slb, d_bb), (r_out,) = _sgu_bwd(h_b, d_yb, sgu_ln_g, sgu_ln_b, w_s3, w_st3, bias_full,
                                                                 (p_out,))
    d_wa, (r_oa,) = _mm(xt_bf, dh_a, out_dtype=BF16, parts=(p_oa,), tm=512, tn=512, name="dw_in_a")
    d_wb, (r_ob,) = _mm(xt_bf, dh_b, out_dtype=BF16, parts=(p_ob,), tm=512, tn=512, name="dw_in_b")
    dq, dk, dv, landed_in = _attn_bwd(q, kt, k, vxt, d_o, o, lse, (_to_parts(d_wa, d_wb),))
    landed = (*landed_in, r_oa, r_ob, r_out)
    dh_c, p_uq, d_wkn, d_wv, d_gq, d_gkv, d_bc = _mla_bwd(dq, dk, dv, h_c, gq, g_kv, wq, wkn, wv, c_t, sa_t, sb_t)
    d_wc = _mm(xt_bf, dh_c, out_dtype=BF16, tm=512, tn=SEG_C, name="dw_in_c")


    p_b_in = _from_segments(d_ba, d_bb, d_bc)
    p_w_ukv = jnp.concatenate([d_wkn.reshape(KV_LORA_RANK, MLA_HEADS, HEAD_PAD)[:, :, :QK_NOPE_DIM],
                               d_wv.reshape(KV_LORA_RANK, MLA_HEADS, HEAD_PAD)[:, :, :V_HEAD_DIM]], axis=-1)
    p_g_q = d_gq[:, :Q_LORA_RANK]
    p_b_s = d_bs_t[:, :SGU_GROUPS].T
    replicated = [p_b_in, p_g_q, d_gkv, p_w_ukv, d_slg, d_slb, d_ws, p_b_s, d_lng, d_lnb]
    return loss_row, ((dh_a, dh_b, dh_c), (wa, wb, wc), dx_res), landed, d_wc, p_uq, replicated


_NAMES = ["w_in", "b_in", "g_q", "w_uq", "g_kv", "w_ukv", "w_oa", "sgu_ln_g", "sgu_ln_b", "w_s", "b_s", "w_ob",
          "w_out", "ln_g", "ln_b"]
_REPLICATED = ["b_in", "g_q", "g_kv", "w_ukv", "sgu_ln_g", "sgu_ln_b", "w_s", "b_s", "ln_g", "ln_b"]


def _reduce_and_update(partials, weights, moms, vars_):
    loss_row, (dhs, ws, dx_res), landed, d_wc, p_uq, replicated = partials
    rep_flat = jnp.concatenate([a.reshape(-1) for a in replicated] + [loss_row[0, :1]])
    rep_flat = jnp.pad(rep_flat, (0, N_DEV * PACK_R_ROWS * LANES - rep_flat.size))
    dx_ab, c_all, g_uq, rep_all = _dx_tail(dhs[:2], ws[:2], dx_res, d_wc, p_uq,
                                           rep_flat.reshape(N_DEV, PACK_R_ROWS, LANES))
    dx = _mm(dhs[2], ws[2], tb=True, add=dx_ab, tm=512, tn=D_MODEL, name="dx_c")
    g_in, g_oa, g_ob, g_out = _sum_landed(landed, c_all)
    rep_sum = rep_all.reshape(-1)
    grads, pos = dict(w_in=g_in, w_uq=g_uq, w_oa=g_oa, w_ob=g_ob, w_out=g_out), 0
    for nm in _REPLICATED:
        grads[nm] = rep_sum[pos:pos + weights[nm].size]
        pos += weights[nm].size
    loss = rep_sum[pos]
    grads = {nm: grads[nm].reshape(weights[nm].shape) for nm in _NAMES}
    deltas, new_m, new_v = _adamw_all([weights[nm] for nm in _NAMES], [grads[nm] for nm in _NAMES],
                                      [moms[nm] for nm in _NAMES], [vars_[nm] for nm in _NAMES])
    return (loss, dx.reshape(1, SEQ, D_MODEL), *[grads[nm] for nm in _NAMES], *deltas, *new_m, *new_v)
```

```python
import math

import jax
import jax.numpy as jnp
from jax import lax
from jax.experimental import pallas as pl
from jax.experimental.pallas import tpu as pltpu

F32 = jnp.float32
BF16 = jnp.bfloat16

D_MODEL = 1024
SEQ = 2048
N_DEV = 8
MLA_HEADS = 8
Q_LORA_RANK = 384
KV_LORA_RANK = 128
QK_NOPE_DIM = 64
QK_ROPE_DIM = 32
V_HEAD_DIM = 64
QK_HEAD_DIM = QK_NOPE_DIM + QK_ROPE_DIM
MLA_WIDTH = MLA_HEADS * V_HEAD_DIM
ROPE_THETA = 10000.0
SGU_GROUPS = 8
SGU_GROUP_DIM = 64
SGU_WIDTH = SGU_GROUPS * SGU_GROUP_DIM
CHUNK = 128
RMS_EPS = 1e-6
LN_EPS = 1e-5
DN_ALPHA = 2.0 ** 0.25
IN_WIDTH = 4640
ATTN_SCALE = QK_HEAD_DIM ** -0.5

ADAM_LR = 0.001
ADAM_B1 = 0.9
ADAM_B2 = 0.999
ADAM_EPS = 1e-08
ADAM_WD = 0.01
ADAM_STEP = 10

LANES = 128
HEAD_PAD = 128
ROPE_LO = QK_NOPE_DIM
ROPE_MID = ROPE_LO + QK_ROPE_DIM // 2
ROPE_HI = ROPE_LO + QK_ROPE_DIM
CQ_PAD = 512

SEG_A = 2560
SEG_B = 1536
SEG_C = 768

PACK_R_ROWS = 272
VMEM_MAX = 58 * 1024 * 1024
VMEM_BIG = 56 * 1024 * 1024
VMEM_MID = 40 * 1024 * 1024


def _sigmoid(x):
    return 1.0 / (1.0 + jnp.exp(-x))


def _gelu_and_grad(x):
    c0 = math.sqrt(2.0 / math.pi)
    x2 = x * x
    t = jnp.tanh(c0 * (x + 0.044715 * x * x2))
    g = 0.5 * x * (1.0 + t)
    dg = 0.5 * (1.0 + t) + 0.5 * x * (1.0 - t * t) * (c0 * (1.0 + 3.0 * 0.044715 * x2))
    return g, dg


def _dot(a, b, dims):
    return lax.dot_general(a, b, (dims, ((), ())), preferred_element_type=F32)


_NN = ((1,), (0,))
_NT = ((1,), (1,))
_TN = ((0,), (0,))


def _store_grad(dh_ref, db_ref, col, val):
    cols = slice(col, col + val.shape[1])
    dh_ref[:, cols] = val.astype(BF16)
    db_ref[:, cols] += jnp.sum(val, axis=0, keepdims=True)


def _mm(a, b, *, tb=False, bias=None, add=None, out_dtype=F32, own=(), parts=(), tm, tn, name):
    m, k = a.shape
    n = b.shape[0] if tb else b.shape[1]
    assert m % tm == 0 and n % tn == 0 and not (own and parts)
    dims = _NT if tb else _NN
    nown = len(own) + len(parts)
    nm = m // tm
    nsteps = (n // tn) * nm

    def body(*refs):
        a_ref, b_ref = refs[0], refs[1]
        pos = 2
        r = _dot(a_ref[...], b_ref[...], dims)
        if bias is not None:
            r = r + refs[pos][...]; pos += 1
        if add is not None:
            r = r + refs[pos][...]; pos += 1
        own_refs = refs[pos:pos + nown]; pos += nown
        refs[pos][...] = r.astype(out_dtype)
        if nown:
            gat_refs = refs[pos + 1:pos + 1 + nown]
            send_sems, recv_sems, local_sems = refs[pos + 1 + nown:]
            step = pl.program_id(0) * nm + pl.program_id(1)
            if own:
                _gather_behind(own_refs, gat_refs, send_sems, recv_sems, local_sems, step, nsteps - 2, nsteps - 1)
            else:
                exchange = _exchange_parts(own_refs, gat_refs, send_sems, recv_sems, local_sems)
                _exchange_start(step == 0, exchange)
                _exchange_finish(step == nsteps - 1, exchange)

    b_spec = pl.BlockSpec((tn, k), lambda j, i: (j, 0)) if tb else pl.BlockSpec((k, tn), lambda j, i: (0, j))
    in_specs, args = [pl.BlockSpec((tm, k), lambda j, i: (i, 0)), b_spec], [a, b]
    if bias is not None:
        in_specs.append(pl.BlockSpec((1, tn), lambda j, i: (0, j))); args.append(bias)
    if add is not None:
        in_specs.append(pl.BlockSpec((tm, tn), lambda j, i: (i, j))); args.append(add)
    hbm = pl.BlockSpec(memory_space=pl.ANY)
    res = pl.pallas_call(
        body, name=name, grid=(n // tn, nm), in_specs=in_specs + [hbm] * nown,
        out_specs=[pl.BlockSpec((tm, tn), lambda j, i: (i, j))] + [hbm] * nown,
        out_shape=[jax.ShapeDtypeStruct((m, n), out_dtype)]
        + [jax.ShapeDtypeStruct((N_DEV,) + o.shape, o.dtype) for o in own]
        + [jax.ShapeDtypeStruct(p.shape, p.dtype) for p in parts],
        scratch_shapes=_exchange_sems(nown) if nown else [],
        compiler_params=pltpu.CompilerParams(dimension_semantics=("arbitrary", "arbitrary"), vmem_limit_bytes=VMEM_BIG),
    )(*args, *own, *parts)
    return (res[0], res[1:]) if nown else res[0]


def _rope(x, c, sa, sb):
    return x * c + pltpu.roll(x, LANES - 16, 1) * sa + pltpu.roll(x, 16, 1) * sb


def _rope_t(dy, c, sa, sb):
    return dy * c + pltpu.roll(dy * sa, 16, 1) + pltpu.roll(dy * sb, LANES - 16, 1)


def _mla_prep(h_c, gq, gkv, wq, wkn, wvx, c_t, sa_t, sb_t):
    tm = 256
    hw = MLA_HEADS * HEAD_PAD

    def body(cq_ref, ckv_ref, kpe_ref, gq_ref, gkv_ref, wq_ref, wkn_ref, wvx_ref, c_ref, sa_ref, sb_ref,
             q_ref, k_ref, kt_ref, vx_ref, vxt_ref):
        c, sa, sb = c_ref[...], sa_ref[...], sb_ref[...]
        cq = cq_ref[...]
        rq = lax.rsqrt(jnp.sum(cq * cq, axis=1, keepdims=True) * (1.0 / Q_LORA_RANK) + RMS_EPS)
        cqn = ((cq * rq) * gq_ref[...]).astype(BF16)
        qall = _dot(cqn, wq_ref[...], _NN)
        for h in range(MLA_HEADS):
            sl = slice(HEAD_PAD * h, HEAD_PAD * (h + 1))
            q_ref[:, sl] = (_rope(qall[:, sl], c, sa, sb) * ATTN_SCALE).astype(BF16)
        ckv = ckv_ref[...]
        rkv = lax.rsqrt(jnp.sum(ckv * ckv, axis=1, keepdims=True) * (1.0 / KV_LORA_RANK) + RMS_EPS)
        ckvn = ((ckv * rkv) * gkv_ref[...]).astype(BF16)
        knall = _dot(ckvn, wkn_ref[...], _NN)
        vall = _dot(ckvn, wvx_ref[...], _NN)
        kper = _rope(kpe_ref[...], c, sa, sb)
        ones_half = (lax.broadcasted_iota(jnp.int32, (tm, HEAD_PAD), 1) >= V_HEAD_DIM).astype(F32)
        for h in range(MLA_HEADS):
            sl = slice(HEAD_PAD * h, HEAD_PAD * (h + 1))
            kh = knall[:, sl] + kper
            vh = vall[:, sl] + ones_half
            k_ref[:, sl] = kh.astype(BF16)
            kt_ref[sl, :] = kh.T.astype(BF16)
            vx_ref[:, sl] = vh.astype(BF16)
            vxt_ref[sl, :] = vh.T.astype(BF16)

    full = lambda shape: pl.BlockSpec(shape, lambda i: (0, 0))
    tab = pl.BlockSpec((tm, LANES), lambda i: (i, 0))
    row = pl.BlockSpec((tm, hw), lambda i: (i, 0))
    col = pl.BlockSpec((hw, tm), lambda i: (0, i))
    return pl.pallas_call(
        body, name="mla_prep", grid=(SEQ // tm,),
        in_specs=[pl.BlockSpec((tm, CQ_PAD), lambda i: (i, 0)),
                  pl.BlockSpec((tm, LANES), lambda i: (i, CQ_PAD // LANES)),
                  pl.BlockSpec((tm, LANES), lambda i: (i, CQ_PAD // LANES + 1)),
                  full((1, CQ_PAD)), full((1, KV_LORA_RANK)),
                  full((CQ_PAD, hw)), full((KV_LORA_RANK, hw)), full((KV_LORA_RANK, hw)), tab, tab, tab],
        out_specs=[row, row, col, row, col],
        out_shape=[jax.ShapeDtypeStruct((SEQ, hw), BF16), jax.ShapeDtypeStruct((SEQ, hw), BF16),
                   jax.ShapeDtypeStruct((hw, SEQ), BF16), jax.ShapeDtypeStruct((SEQ, hw), BF16),
                   jax.ShapeDtypeStruct((hw, SEQ), BF16)],
        compiler_params=pltpu.CompilerParams(dimension_semantics=("arbitrary",), vmem_limit_bytes=VMEM_MID),
    )(h_c, h_c, h_c, gq, gkv, wq, wkn, wvx, c_t, sa_t, sb_t)


ATT_T = 512
ATT_STRIP = 64


def _attn_fwd(q, kt, vx, own):
    t, rs = ATT_T, ATT_STRIP
    nown = len(own)
    nq = SEQ // t
    nsteps = (MLA_HEADS // 2) * nq

    def body(q_ref, kt_ref, vx_ref, *rest):
        own_refs, (o_ref, l_ref), gat_refs = rest[:nown], rest[nown:nown + 2], rest[nown + 2:2 * nown + 2]
        s_scr, p_scr, m_scr, a_scr, acc_scr, send_sems, recv_sems, local_sems = rest[2 * nown + 2:]
        qi = pl.program_id(1)
        _gather_behind(own_refs, gat_refs, send_sems, recv_sems, local_sems, pl.program_id(0) * nq + qi,
                       nsteps - 2, nsteps - 1)
        lane = lax.broadcasted_iota(jnp.int32, (t, LANES), 1)
        m_scr[...] = jnp.full((2, t, LANES), -1e30, F32)
        acc_scr[...] = jnp.zeros((2, t, LANES), F32)

        def block(j, masked):
            off = pl.multiple_of(j * t, t)
            for a in range(2):
                sl = slice(HEAD_PAD * a, HEAD_PAD * (a + 1))
                s_scr[a] = _dot(q_ref[:, sl], kt_ref[sl, pl.ds(off, t)], _NN)
                for r in range(t // rs):
                    rows = slice(rs * r, rs * (r + 1))
                    s = s_scr[a, rows, :]
                    if masked:
                        rowi = lax.broadcasted_iota(jnp.int32, (rs, t), 0) + rs * r
                        coli = lax.broadcasted_iota(jnp.int32, (rs, t), 1)
                        s = jnp.where(coli <= rowi, s, -1e30)
                    m_old = m_scr[a, rows, :]
                    m_new = jnp.maximum(m_old, jnp.max(s, axis=1, keepdims=True))
                    p_scr[a, rows, :] = jnp.exp(s - m_new[:, :1]).astype(BF16)
                    a_scr[a, rows, :] = jnp.exp(m_old - m_new)
                    m_scr[a, rows, :] = m_new
                acc_scr[a] = acc_scr[a] * a_scr[a] + _dot(p_scr[a], vx_ref[pl.ds(off, t), sl], _NN)

        def step(j, carry):
            block(j, False)
            return carry
        lax.fori_loop(0, qi, step, 0)
        block(qi, True)
        res = []
        for a in range(2):
            acc = acc_scr[a]
            l = acc[:, V_HEAD_DIM:V_HEAD_DIM + 1]
            res.append((acc / l, m_scr[a] + jnp.log(l)))
        o_ref[...] = jnp.where(lane < V_HEAD_DIM, res[0][0], pltpu.roll(res[1][0], V_HEAD_DIM, 1))
        l_ref[...] = jnp.where(lane < V_HEAD_DIM, res[0][1], res[1][1])

    hbm = pl.BlockSpec(memory_space=pl.ANY)
    res = pl.pallas_call(
        body, name="attn_fwd", grid=(MLA_HEADS // 2, nq),
        in_specs=[pl.BlockSpec((t, 2 * HEAD_PAD), lambda p, i: (i, p)),
                  pl.BlockSpec((2 * HEAD_PAD, SEQ), lambda p, i: (p, 0)),
                  pl.BlockSpec((SEQ, 2 * HEAD_PAD), lambda p, i: (0, p))] + [hbm] * nown,
        out_specs=[pl.BlockSpec((t, LANES), lambda p, i: (i, p)),
                   pl.BlockSpec((t, LANES), lambda p, i: (i, p))] + [hbm] * nown,
        out_shape=[jax.ShapeDtypeStruct((SEQ, MLA_WIDTH), F32), jax.ShapeDtypeStruct((SEQ, MLA_WIDTH), F32)]
        + [jax.ShapeDtypeStruct((N_DEV,) + a.shape, a.dtype) for a in own],
        scratch_shapes=[pltpu.VMEM((2, t, t), F32), pltpu.VMEM((2, t, t), BF16), pltpu.VMEM((2, t, LANES), F32),
                        pltpu.VMEM((2, t, LANES), F32), pltpu.VMEM((2, t, LANES), F32)] + _exchange_sems(nown),
        compiler_params=pltpu.CompilerParams(dimension_semantics=("arbitrary", "arbitrary"), vmem_limit_bytes=VMEM_MID),
    )(q, kt, vx, *own)
    return res[0], res[1], res[2:]


def _exchange_parts(parts, lands, send_sems, recv_sems, local_sems):
    x, y, c = _mesh_pos()
    me = 4 * x + 2 * y + c
    peers = [(x, y, 1 - c), (1 - x, y, c), (x, 1 - y, c), (1 - x, 1 - y, c),
             (1 - x, y, 1 - c), (x, 1 - y, 1 - c), (1 - x, 1 - y, 1 - c)]
    remote, local = [], []
    for a, (part, land) in enumerate(zip(parts, lands)):
        for k, peer in enumerate(peers):
            t = 4 * peer[0] + 2 * peer[1] + peer[2]
            remote.append(_remote(part.at[t], land.at[me], send_sems, recv_sems, 7 * a + k, peer))
        local.append(pltpu.make_async_copy(part.at[me], land.at[me], local_sems.at[a]))
    return remote, local


def _exchange_start(first_step, exchange):
    remote, local = exchange

    @pl.when(first_step)
    def _():
        for cp in remote + local:
            cp.start()


def _exchange_finish(last_step, exchange):
    remote, local = exchange

    @pl.when(last_step)
    def _():
        for cp in remote:
            cp.wait_recv()
        for cp in remote:
            cp.wait_send()
        for cp in local:
            cp.wait()


def _exchange_sems(npart):
    return [pltpu.SemaphoreType.DMA((7 * npart,)), pltpu.SemaphoreType.DMA((7 * npart,)),
            pltpu.SemaphoreType.DMA((npart,))]


def _attn_bwd(q, kt, k, vxt, d_o, o, lse, parts):
    t, rs = ATT_T, ATT_STRIP
    nq = SEQ // t
    npart = len(parts)
    nsteps = MLA_HEADS // 2

    def body(q_ref, kt_ref, k_ref, vxt_ref, do_ref, o_ref, l_ref, *rest):
        part_refs, rest = rest[:npart], rest[npart:]
        dq_ref, dk_ref, dv_ref = rest[:3]
        land_refs, rest = rest[3:3 + npart], rest[3 + npart:]
        s_scr, dp_scr, p_scr, ds_scr, st_scr, send_sems, recv_sems, local_sems = rest
        exchange = _exchange_parts(part_refs, land_refs, send_sems, recv_sems, local_sems)
        _exchange_start(pl.program_id(0) == 0, exchange)
        dk_ref[...] = jnp.zeros_like(dk_ref)
        dv_ref[...] = jnp.zeros_like(dv_ref)
        lane = lax.broadcasted_iota(jnp.int32, (t, LANES), 1)

        def qtile(i, carry):
            ioff = pl.multiple_of(i * t, t)
            do_i = do_ref[pl.ds(ioff, t), :]
            o_i = o_ref[pl.ds(ioff, t), :]
            l_i = l_ref[pl.ds(ioff, t), :]
            for a in range(2):
                sl = slice(HEAD_PAD * a, HEAD_PAD * (a + 1))
                sel = (lane < V_HEAD_DIM) if a == 0 else (lane >= V_HEAD_DIM)
                doa = jnp.where(sel, do_i, 0.0)
                oa = o_i
                if a == 1:
                    doa = pltpu.roll(doa, V_HEAD_DIM, 1)
                    oa = pltpu.roll(o_i, V_HEAD_DIM, 1)
                st_scr[0] = jnp.broadcast_to(jnp.sum(doa * oa, axis=1, keepdims=True), (t, LANES))
                st_scr[1] = jnp.broadcast_to(l_i[:, V_HEAD_DIM * a:V_HEAD_DIM * a + 1], (t, LANES))
                doa_bf = doa.astype(BF16)
                qa = q_ref[pl.ds(ioff, t), sl]

                def block(j, masked, dq_acc, sl=sl, qa=qa, doa_bf=doa_bf):
                    joff = pl.multiple_of(j * t, t)
                    s_scr[...] = _dot(qa, kt_ref[sl, pl.ds(joff, t)], _NN)
                    dp_scr[...] = _dot(doa_bf, vxt_ref[sl, pl.ds(joff, t)], _NN)
                    for r in range(t // rs):
                        rows = slice(rs * r, rs * (r + 1))
                        p = jnp.exp(s_scr[rows, :] - st_scr[1, rows, :1])
                        if masked:
                            rowi = lax.broadcasted_iota(jnp.int32, (rs, t), 0) + rs * r
                            coli = lax.broadcasted_iota(jnp.int32, (rs, t), 1)
                            p = jnp.where(coli <= rowi, p, 0.0)
                        p_scr[rows, :] = p.astype(BF16)
                        ds_scr[rows, :] = (p * (dp_scr[rows, :] - st_scr[0, rows, :1])).astype(BF16)
                    dk_ref[pl.ds(joff, t), sl] += _dot(ds_scr[...], qa, _TN)
                    dv_ref[pl.ds(joff, t), sl] += _dot(p_scr[...], doa_bf, _TN)
                    return dq_acc + _dot(ds_scr[...], k_ref[pl.ds(joff, t), sl], _NN)

                dq_acc = lax.fori_loop(0, i, lambda j, acc: block(j, False, acc), jnp.zeros((t, HEAD_PAD), F32))
                dq_ref[pl.ds(ioff, t), sl] = block(i, True, dq_acc)
            return carry

        lax.fori_loop(0, nq, qtile, 0)
        _exchange_finish(pl.program_id(0) == nsteps - 1, exchange)

    hw = MLA_HEADS * HEAD_PAD
    wide = pl.BlockSpec((SEQ, 2 * HEAD_PAD), lambda p: (0, p))
    wide_t = pl.BlockSpec((2 * HEAD_PAD, SEQ), lambda p: (p, 0))
    narrow = pl.BlockSpec((SEQ, LANES), lambda p: (0, p))
    hbm = pl.BlockSpec(memory_space=pl.ANY)
    res = pl.pallas_call(
        body, name="attn_bwd", grid=(nsteps,),
        in_specs=[wide, wide_t, wide, wide_t, narrow, narrow, narrow] + [hbm] * npart,
        out_specs=[wide, wide, wide] + [hbm] * npart,
        out_shape=[jax.ShapeDtypeStruct((SEQ, hw), F32)] * 3 + [jax.ShapeDtypeStruct(p.shape, p.dtype) for p in parts],
        scratch_shapes=[pltpu.VMEM((t, t), F32), pltpu.VMEM((t, t), F32), pltpu.VMEM((t, t), BF16),
                        pltpu.VMEM((t, t), BF16), pltpu.VMEM((2, t, LANES), F32)] + _exchange_sems(npart),
        compiler_params=pltpu.CompilerParams(dimension_semantics=("arbitrary",), vmem_limit_bytes=VMEM_BIG),
    )(q, kt, k, vxt, d_o, o, lse, *parts)
    return res[0], res[1], res[2], res[3:]


def _sgu_math(u, v, zb, lg, lb, ws_ref, bias):
    ug, dug = _gelu_and_grad(u)
    vg, dvg = _gelu_and_grad(v)
    mu = jnp.mean(vg, axis=1, keepdims=True)
    xc = vg - mu
    rstd = lax.rsqrt(jnp.mean(xc * xc, axis=1, keepdims=True) + LN_EPS)
    xh = xc * rstd
    vn_bf = (xh * lg + lb).astype(BF16)
    grp = lax.broadcasted_iota(jnp.int32, (CHUNK, SGU_WIDTH), 1) // SGU_GROUP_DIM
    r_i = lax.broadcasted_iota(jnp.int32, (CHUNK, CHUNK), 0)
    c_i = lax.broadcasted_iota(jnp.int32, (CHUNK, CHUNK), 1)
    tri, tri_t = r_i >= c_i, r_i <= c_i
    mixed = bias
    for g in range(SGU_GROUPS):
        wt = jnp.where(tri, ws_ref[g], 0.0).astype(BF16)
        mixed = mixed + jnp.where(grp == g, _dot(wt, vn_bf, _NN), 0.0)
    sb = _sigmoid(zb)
    return ug, dug, dvg, rstd, xh, vn_bf, grp, tri, tri_t, mixed, sb


def _sgu_fwd(h_b, lg, lb, w_s, bias_full):
    def body(u_ref, v_ref, zb_ref, lg_ref, lb_ref, ws_ref, bias_ref, yb_ref):
        zb = zb_ref[...]
        ug, _, _, _, _, _, _, _, _, mixed, sb = _sgu_math(u_ref[...], v_ref[...], zb, lg_ref[...], lb_ref[...],
                                                       ws_ref, bias_ref[...])
        yb_ref[...] = (ug * mixed) * (zb * sb)

    blk = lambda c: pl.BlockSpec((CHUNK, SGU_WIDTH), lambda i, c=c: (i, c))
    full2 = lambda shape: pl.BlockSpec(shape, lambda i: (0, 0))
    return pl.pallas_call(
        body, name="sgu_fwd", grid=(SEQ // CHUNK,),
        in_specs=[blk(0), blk(1), blk(2), full2((1, SGU_WIDTH)), full2((1, SGU_WIDTH)),
                  pl.BlockSpec((SGU_GROUPS, CHUNK, CHUNK), lambda i: (0, 0, 0)), full2((CHUNK, SGU_WIDTH))],
        out_specs=pl.BlockSpec((CHUNK, SGU_WIDTH), lambda i: (i, 0)),
        out_shape=jax.ShapeDtypeStruct((SEQ, SGU_WIDTH), F32),
        compiler_params=pltpu.CompilerParams(dimension_semantics=("arbitrary",)),
    )(h_b, h_b, h_b, lg, lb, w_s, bias_full)


def _sgu_bwd(h_b, d_yb, lg, lb, w_s, w_st, bias_full, parts):
    nsteps = SEQ // CHUNK
    npart = len(parts)

    def body(u_ref, v_ref, zb_ref, dyb_ref, lg_ref, lb_ref, ws_ref, wst_ref, bias_ref, *rest):
        part_refs, rest = rest[:npart], rest[npart:]
        dhb_ref, dws_ref, dbs_ref, dlg_ref, dlb_ref, dbb_ref = rest[:6]
        land_refs, (dbias_acc, send_sems, recv_sems, local_sems) = rest[6:6 + npart], rest[6 + npart:]
        step = pl.program_id(0)
        exchange = _exchange_parts(part_refs, land_refs, send_sems, recv_sems, local_sems)
        _exchange_start(step == 0, exchange)

        @pl.when(step == 0)
        def _():
            dbb_ref[...] = jnp.zeros_like(dbb_ref)
            dws_ref[...] = jnp.zeros_like(dws_ref)
            dlg_ref[...] = jnp.zeros_like(dlg_ref)
            dlb_ref[...] = jnp.zeros_like(dlb_ref)
            dbias_acc[...] = jnp.zeros_like(dbias_acc)

        zb = zb_ref[...]
        lg = lg_ref[...]
        ug, dug, dvg, rstd, xh, vn_bf, grp, tri, tri_t, mixed, sb = _sgu_math(
            u_ref[...], v_ref[...], zb, lg, lb_ref[...], ws_ref, bias_ref[...])
        dyb = dyb_ref[...]
        dsgu = dyb * (zb * sb)
        dzb = dyb * (ug * mixed) * (sb * (1.0 + zb * (1.0 - sb)))
        du = dsgu * mixed * dug
        dmixed = dsgu * ug
        dbias_acc[...] += dmixed
        dvn = jnp.zeros((CHUNK, SGU_WIDTH), F32)
        for g in range(SGU_GROUPS):
            dm_g = jnp.where(grp == g, dmixed, 0.0).astype(BF16)
            wtt = jnp.where(tri_t, wst_ref[g], 0.0).astype(BF16)
            dvn = dvn + _dot(wtt, dm_g, _NN)
            dws_ref[g] += jnp.where(tri, _dot(dm_g, vn_bf, _NT), 0.0)
        dlg_ref[...] += jnp.sum(dvn * xh, axis=0, keepdims=True)
        dlb_ref[...] += jnp.sum(dvn, axis=0, keepdims=True)
        dxh = dvn * lg
        dvgel = rstd * (dxh - jnp.mean(dxh, axis=1, keepdims=True) - xh * jnp.mean(dxh * xh, axis=1, keepdims=True))
        _store_grad(dhb_ref, dbb_ref, 0, du)
        _store_grad(dhb_ref, dbb_ref, SGU_WIDTH, dvgel * dvg)
        _store_grad(dhb_ref, dbb_ref, 2 * SGU_WIDTH, dzb)

        @pl.when(step == nsteps - 1)
        def _():
            acc = dbias_acc[...]
            lane = lax.broadcasted_iota(jnp.int32, (CHUNK, LANES), 1)
            out = jnp.zeros((CHUNK, LANES), F32)
            for g in range(SGU_GROUPS):
                sg = jnp.sum(jnp.where(grp == g, acc, 0.0), axis=1, keepdims=True)
                out = jnp.where(lane == g, sg, out)
            dbs_ref[...] = out

        _exchange_finish(step == nsteps - 1, exchange)

    blk = lambda c: pl.BlockSpec((CHUNK, SGU_WIDTH), lambda i, c=c: (i, c))
    full2 = lambda shape: pl.BlockSpec(shape, lambda i: (0, 0))
    full3 = pl.BlockSpec((SGU_GROUPS, CHUNK, CHUNK), lambda i: (0, 0, 0))
    hbm = pl.BlockSpec(memory_space=pl.ANY)
    res = pl.pallas_call(
        body, name="sgu_bwd", grid=(nsteps,),
        in_specs=[blk(0), blk(1), blk(2), pl.BlockSpec((CHUNK, SGU_WIDTH), lambda i: (i, 0)),
                  full2((1, SGU_WIDTH)), full2((1, SGU_WIDTH)), full3, full3, full2((CHUNK, SGU_WIDTH))] + [hbm] * npart,
        out_specs=[pl.BlockSpec((CHUNK, SEG_B), lambda i: (i, 0)), full3, full2((CHUNK, LANES)),
                   full2((1, SGU_WIDTH)), full2((1, SGU_WIDTH)), full2((1, SEG_B))] + [hbm] * npart,
        out_shape=[jax.ShapeDtypeStruct((SEQ, SEG_B), BF16),
                   jax.ShapeDtypeStruct((SGU_GROUPS, CHUNK, CHUNK), F32),
                   jax.ShapeDtypeStruct((CHUNK, LANES), F32),
                   jax.ShapeDtypeStruct((1, SGU_WIDTH), F32), jax.ShapeDtypeStruct((1, SGU_WIDTH), F32),
                   jax.ShapeDtypeStruct((1, SEG_B), F32)] + [jax.ShapeDtypeStruct(p.shape, p.dtype) for p in parts],
        scratch_shapes=[pltpu.VMEM((CHUNK, SGU_WIDTH), F32)] + _exchange_sems(npart),
        compiler_params=pltpu.CompilerParams(dimension_semantics=("arbitrary",)),
    )(h_b, h_b, h_b, d_yb, lg, lb, w_s, w_st, bias_full, *parts)
    return res[:6], res[6:]


def _merge(x, o, h_a, y_b, target, w_oa, w_ob, w_out, ln_g, ln_b):
    tm = 256
    nsteps = SEQ // tm

    def body(x_ref, o_ref, ga_ref, gb_ref, za_ref, yb_ref, tgt_ref, woa_ref, wob_ref, wout_ref, lng_ref, lnb_ref,
             loss_ref, dxr_ref, dha_ref, do_ref, dyb_ref, poa_ref, pob_ref, pout_ref, dlng_ref, dlnb_ref, dba_ref,
             dwoa_ref, dwob_ref, dwout_ref):
        step = pl.program_id(0)

        @pl.when(step == 0)
        def _():
            for r in (loss_ref, dwoa_ref, dwob_ref, dwout_ref, dlng_ref, dlnb_ref, dba_ref):
                r[...] = jnp.zeros_like(r)

        o = o_ref[...]
        za = za_ref[...]
        sa = _sigmoid(za)
        ya_bf = (o * (za * sa)).astype(BF16)
        yb_bf = yb_ref[...].astype(BF16)
        woa, wob, wout = woa_ref[...], wob_ref[...], wout_ref[...]
        pa = _dot(ya_bf, woa, _NN)
        pb = _dot(yb_bf, wob, _NN)
        sga = _sigmoid(ga_ref[...])
        sgb = _sigmoid(gb_ref[...])
        merged_bf = (sga * pa + sgb * pb).astype(BF16)
        r = DN_ALPHA * x_ref[...] + _dot(merged_bf, wout, _NN)
        mu = jnp.mean(r, axis=1, keepdims=True)
        rc = r - mu
        rstd = lax.rsqrt(jnp.mean(rc * rc, axis=1, keepdims=True) + LN_EPS)
        xh = rc * rstd
        lng = lng_ref[...]
        y = xh * lng + lnb_ref[...]
        e = y - tgt_ref[...]
        loss_ref[...] += 0.5 * jnp.sum(jnp.sum(e * e, axis=1, keepdims=True) * (1.0 / D_MODEL), axis=0, keepdims=True)

        dy = e * (1.0 / D_MODEL)
        dlng_ref[...] += jnp.sum(dy * xh, axis=0, keepdims=True)
        dlnb_ref[...] += jnp.sum(dy, axis=0, keepdims=True)
        dxh = dy * lng
        dr = rstd * (dxh - jnp.mean(dxh, axis=1, keepdims=True) - xh * jnp.mean(dxh * xh, axis=1, keepdims=True))
        dxr_ref[...] = DN_ALPHA * dr
        dr_bf = dr.astype(BF16)
        dwout_ref[...] += _dot(merged_bf, dr_bf, _TN)
        dmerged = _dot(dr_bf, wout, _NT)
        dpa_bf = (dmerged * sga).astype(BF16)
        dpb_bf = (dmerged * sgb).astype(BF16)
        _store_grad(dha_ref, dba_ref, 0, dmerged * pa * (sga * (1.0 - sga)))
        _store_grad(dha_ref, dba_ref, D_MODEL, dmerged * pb * (sgb * (1.0 - sgb)))
        dwoa_ref[...] += _dot(ya_bf, dpa_bf, _TN)
        dwob_ref[...] += _dot(yb_bf, dpb_bf, _TN)
        dya = _dot(dpa_bf, woa, _NT)
        dyb_ref[...] = _dot(dpb_bf, wob, _NT)
        do_ref[...] = dya * (za * sa)
        _store_grad(dha_ref, dba_ref, 2 * D_MODEL, dya * o * (sa * (1.0 + za * (1.0 - sa))))

        @pl.when(step == nsteps - 1)
        def _():
            cols = D_MODEL // N_DEV
            for j in range(N_DEV):
                poa_ref[j] = dwoa_ref[:, cols * j:cols * (j + 1)].astype(BF16)
                pob_ref[j] = dwob_ref[:, cols * j:cols * (j + 1)].astype(BF16)
                pout_ref[j] = dwout_ref[cols * j:cols * (j + 1), :].astype(BF16)

    row = lambda w, c=0: pl.BlockSpec((tm, w), lambda i, c=c: (i, c))
    full = lambda shape: pl.BlockSpec(shape, lambda i: (0, 0))
    full3 = lambda shape: pl.BlockSpec(shape, lambda i: (0, 0, 0))
    return pl.pallas_call(
        body, name="merge", grid=(nsteps,),
        in_specs=[row(D_MODEL), row(MLA_WIDTH), row(D_MODEL, 0), row(D_MODEL, 1), row(MLA_WIDTH, 4), row(SGU_WIDTH),
                  row(D_MODEL), full((MLA_WIDTH, D_MODEL)), full((SGU_WIDTH, D_MODEL)), full((D_MODEL, D_MODEL)),
                  full((1, D_MODEL)), full((1, D_MODEL))],
        out_specs=[full((1, LANES)), row(D_MODEL), row(SEG_A), row(MLA_WIDTH), row(SGU_WIDTH),
                   full3((N_DEV, MLA_WIDTH, D_MODEL // N_DEV)), full3((N_DEV, SGU_WIDTH, D_MODEL // N_DEV)),
                   full3((N_DEV, D_MODEL // N_DEV, D_MODEL)), full((1, D_MODEL)), full((1, D_MODEL)), full((1, SEG_A))],
        out_shape=[jax.ShapeDtypeStruct((1, LANES), F32),
                   jax.ShapeDtypeStruct((SEQ, D_MODEL), F32), jax.ShapeDtypeStruct((SEQ, SEG_A), BF16),
                   jax.ShapeDtypeStruct((SEQ, MLA_WIDTH), F32), jax.ShapeDtypeStruct((SEQ, SGU_WIDTH), F32),
                   jax.ShapeDtypeStruct((N_DEV, MLA_WIDTH, D_MODEL // N_DEV), BF16),
                   jax.ShapeDtypeStruct((N_DEV, SGU_WIDTH, D_MODEL // N_DEV), BF16),
                   jax.ShapeDtypeStruct((N_DEV, D_MODEL // N_DEV, D_MODEL), BF16),
                   jax.ShapeDtypeStruct((1, D_MODEL), F32), jax.ShapeDtypeStruct((1, D_MODEL), F32),
                   jax.ShapeDtypeStruct((1, SEG_A), F32)],
        scratch_shapes=[pltpu.VMEM((MLA_WIDTH, D_MODEL), F32), pltpu.VMEM((SGU_WIDTH, D_MODEL), F32),
                        pltpu.VMEM((D_MODEL, D_MODEL), F32)],
        compiler_params=pltpu.CompilerParams(dimension_semantics=("arbitrary",), vmem_limit_bytes=VMEM_BIG),
    )(x, o, h_a, h_a, h_a, y_b, target, w_oa, w_ob, w_out, ln_g, ln_b)


def _mla_bwd(dq, dk, dv, h_c, gq, gkv, wq, wkn, wv, c_t, sa_t, sb_t):
    tm = 256
    hw = MLA_HEADS * HEAD_PAD

    def body(dq_ref, dk_ref, dv_ref, cq_ref, ckv_ref, gq_ref, gkv_ref, wq_ref, wkn_ref, wv_ref, c_ref, sa_ref, sb_ref,
             dhc_ref, puq_ref, dwkn_ref, dwv_ref, dgq_ref, dgkv_ref, dbc_ref, pre_ref, dwq_ref):
        @pl.when(pl.program_id(0) == 0)
        def _():
            for r in (dwq_ref, dwkn_ref, dwv_ref, dgq_ref, dgkv_ref, dbc_ref):
                r[...] = jnp.zeros_like(r)

        c, sa, sb = c_ref[...], sa_ref[...], sb_ref[...]
        lane = lax.broadcasted_iota(jnp.int32, (tm, LANES), 1)
        rope_lanes = jnp.logical_and(lane >= ROPE_LO, lane < ROPE_HI)

        cq = cq_ref[...]
        gq = gq_ref[...]
        rq = lax.rsqrt(jnp.sum(cq * cq, axis=1, keepdims=True) * (1.0 / Q_LORA_RANK) + RMS_EPS)
        nq = cq * rq
        cqn_bf = (nq * gq).astype(BF16)
        for h in range(MLA_HEADS):
            sl = slice(HEAD_PAD * h, HEAD_PAD * (h + 1))
            pre_ref[:, sl] = _rope_t(dq_ref[:, sl] * ATTN_SCALE, c, sa, sb).astype(BF16)
        dqpre_bf = pre_ref[...]
        dcqn = _dot(dqpre_bf, wq_ref[...], _NT)
        dwq_ref[...] += _dot(cqn_bf, dqpre_bf, _TN)
        dgq_ref[...] += jnp.sum(dcqn * nq, axis=0, keepdims=True)
        dnq = dcqn * gq
        _store_grad(dhc_ref, dbc_ref, 0,
                    rq * (dnq - nq * (jnp.sum(dnq * nq, axis=1, keepdims=True) * (1.0 / Q_LORA_RANK))))

        ckv = ckv_ref[...]
        gkv = gkv_ref[...]
        rkv = lax.rsqrt(jnp.sum(ckv * ckv, axis=1, keepdims=True) * (1.0 / KV_LORA_RANK) + RMS_EPS)
        nkv = ckv * rkv
        ckvn_bf = (nkv * gkv).astype(BF16)
        dk = dk_ref[...]
        dk_bf = dk.astype(BF16)
        dv_bf = dv_ref[...].astype(BF16)
        dckvn = _dot(dk_bf, wkn_ref[...], _NT) + _dot(dv_bf, wv_ref[...], _NT)
        dwkn_ref[...] += _dot(ckvn_bf, dk_bf, _TN)
        dwv_ref[...] += _dot(ckvn_bf, dv_bf, _TN)
        dgkv_ref[...] += jnp.sum(dckvn * nkv, axis=0, keepdims=True)
        dnkv = dckvn * gkv
        _store_grad(dhc_ref, dbc_ref, CQ_PAD, rkv * (
            dnkv - nkv * (jnp.sum(dnkv * nkv, axis=1, keepdims=True) * (1.0 / KV_LORA_RANK))))
        dkpe = jnp.zeros((tm, LANES), F32)
        for h in range(MLA_HEADS):
            dkpe = dkpe + dk[:, HEAD_PAD * h:HEAD_PAD * (h + 1)]
        _store_grad(dhc_ref, dbc_ref, CQ_PAD + LANES, _rope_t(jnp.where(rope_lanes, dkpe, 0.0), c, sa, sb))

        @pl.when(pl.program_id(0) == SEQ // tm - 1)
        def _():
            rows = Q_LORA_RANK // N_DEV
            for j in range(N_DEV):
                for h in range(MLA_HEADS):
                    puq_ref[j, :, QK_HEAD_DIM * h:QK_HEAD_DIM * (h + 1)] = dwq_ref[
                        rows * j:rows * (j + 1), HEAD_PAD * h:HEAD_PAD * h + QK_HEAD_DIM].astype(BF16)

    full = lambda shape: pl.BlockSpec(shape, lambda i: (0, 0))
    row = lambda w, c=0: pl.BlockSpec((tm, w), lambda i, c=c: (i, c))
    return pl.pallas_call(
        body, name="mla_bwd", grid=(SEQ // tm,),
        in_specs=[row(hw), row(hw), row(hw), row(CQ_PAD, 0), row(LANES, CQ_PAD // LANES),
                  full((1, CQ_PAD)), full((1, KV_LORA_RANK)), full((CQ_PAD, hw)), full((KV_LORA_RANK, hw)),
                  full((KV_LORA_RANK, hw)), row(LANES), row(LANES), row(LANES)],
        out_specs=[row(SEG_C), pl.BlockSpec((N_DEV, Q_LORA_RANK // N_DEV, MLA_HEADS * QK_HEAD_DIM), lambda i: (0, 0, 0)),
                   full((KV_LORA_RANK, hw)), full((KV_LORA_RANK, hw)),
                   full((1, CQ_PAD)), full((1, KV_LORA_RANK)), full((1, SEG_C))],
        out_shape=[jax.ShapeDtypeStruct((SEQ, SEG_C), BF16),
                   jax.ShapeDtypeStruct((N_DEV, Q_LORA_RANK // N_DEV, MLA_HEADS * QK_HEAD_DIM), BF16),
                   jax.ShapeDtypeStruct((KV_LORA_RANK, hw), F32), jax.ShapeDtypeStruct((KV_LORA_RANK, hw), F32),
                   jax.ShapeDtypeStruct((1, CQ_PAD), F32), jax.ShapeDtypeStruct((1, KV_LORA_RANK), F32),
                   jax.ShapeDtypeStruct((1, SEG_C), F32)],
        scratch_shapes=[pltpu.VMEM((tm, hw), BF16), pltpu.VMEM((CQ_PAD, hw), F32)],
        compiler_params=pltpu.CompilerParams(dimension_semantics=("arbitrary",), vmem_limit_bytes=VMEM_MID),
    )(dq, dk, dv, h_c, h_c, gq, gkv, wq, wkn, wv, c_t, sa_t, sb_t)


def _adamw_all(ws, gs, ms, vs, landed, c_all):
    n = len(ws)
    given = [i for i, g in enumerate(gs) if g is not None]
    summed = [i for i, g in enumerate(gs) if g is None]
    nl = len(landed)
    assert len(summed) == nl
    c1 = 1.0 / (1.0 - ADAM_B1 ** ADAM_STEP)
    c2 = 1.0 / (1.0 - ADAM_B2 ** ADAM_STEP)
    c_rows = D_MODEL // N_DEV

    def body(*refs):
        w_refs, refs = refs[:n], refs[n:]
        g_given, refs = dict(zip(given, refs[:len(given)])), refs[len(given):]
        m_refs, v_refs, refs = refs[:n], refs[n:2 * n], refs[2 * n:]
        land_refs, call_ref, refs = refs[:nl], refs[nl], refs[nl + 1:]
        d_refs, mo_refs, vo_refs, gsum_refs = refs[:n], refs[n:2 * n], refs[2 * n:3 * n], refs[3 * n:]

        def total(ref, sl):
            acc = ref[0, sl, :].astype(F32)
            for s in range(1, N_DEV):
                acc = acc + ref[s, sl, :].astype(F32)
            return acc

        x, y, c = _mesh_pos()
        dev0 = jnp.where(4 * x + 2 * y + c == 0, 1.0, 0.0)
        for j in range(N_DEV):
            sl = slice(c_rows * j, c_rows * (j + 1))
            tot = total(land_refs[0], sl)
            gsum_refs[0][0, sl, C_NAT:SHARD_W] = tot[:, C_NAT:SHARD_W]
            gsum_refs[0][0, sl, 0:C_NAT] = tot[:, 0:C_NAT] + dev0 * call_ref[j].astype(F32)
        for k in range(1, nl):
            gsum_refs[k][0] = total(land_refs[k], slice(None))
        g_refs = {**g_given, **dict(zip(summed, gsum_refs))}

        for idx in range(n):
            w, g, m, v = w_refs[idx][...], g_refs[idx][...], m_refs[idx][...], v_refs[idx][...]
            m_new = ADAM_B1 * m + (1.0 - ADAM_B1) * g
            v_new = ADAM_B2 * v + (1.0 - ADAM_B2) * (g * g)
            d_refs[idx][...] = -ADAM_LR * ((m_new * c1) / (jnp.sqrt(v_new * c2) + ADAM_EPS) + ADAM_WD * w)
            mo_refs[idx][...] = m_new
            vo_refs[idx][...] = v_new

    shapes = [jax.ShapeDtypeStruct(w.shape, F32) for w in ws]
    outs = pl.pallas_call(
        body, name="adamw", out_shape=shapes * 3 + [shapes[i] for i in summed],
        compiler_params=pltpu.CompilerParams(vmem_limit_bytes=VMEM_MAX),
    )(*ws, *[gs[i] for i in given], *ms, *vs, *landed, c_all)
    return outs[:n], outs[n:2 * n], outs[2 * n:3 * n], outs[3 * n:]


SHARD_W = IN_WIDTH // N_DEV

_PIECES = [(0, 384, 2, 0), (384, 512, 2, CQ_PAD), (512, 544, 2, CQ_PAD + LANES + ROPE_LO),
           (544, 1056, 0, 2 * D_MODEL), (1056, 1568, 1, 0), (1568, 2080, 1, SGU_WIDTH),
           (2080, 2592, 1, 2 * SGU_WIDTH), (2592, 3616, 0, 0), (3616, 4640, 0, D_MODEL)]


def _column_runs():
    runs = []
    for n0, n1, seg, d0 in _PIECES:
        for j in range(N_DEV):
            lo, hi = max(n0, j * SHARD_W), min(n1, (j + 1) * SHARD_W)
            if lo < hi:
                runs.append((j, lo - j * SHARD_W, hi - j * SHARD_W, seg, d0 + lo - n0))
    return runs


def _mesh_pos():
    return lax.axis_index("x"), lax.axis_index("y"), lax.axis_index("c")


def _remote(src, dst, send_sems, recv_sems, k, to):
    return pltpu.make_async_remote_copy(src_ref=src, dst_ref=dst, send_sem=send_sems.at[k], recv_sem=recv_sems.at[k],
                                        device_id=to, device_id_type=pl.DeviceIdType.MESH)


def _gather_exchange(gats, send_sems, recv_sems, meanwhile=None):
    x, y, c = _mesh_pos()
    me, sibling = (x, y, c), (x, y, 1 - c)
    chips = [(1 - x, y), (x, 1 - y), (1 - x, 1 - y)]

    def copy(a, k, blk, to):
        slab = gats[a].at[4 * blk[0] + 2 * blk[1] + blk[2]]
        return _remote(slab, slab, send_sems, recv_sems, 7 * a + k, to)

    arrays = range(len(gats))
    first = [copy(a, 1 + j, me, (*chip, c)) for j, chip in enumerate(chips) for a in arrays]
    first += [copy(a, 0, me, sibling) for a in arrays]
    for cp in first:
        cp.start()
    if meanwhile is not None:
        meanwhile()
    passed = []
    for j, chip in enumerate(chips):
        for a in arrays:
            copy(a, 1 + j, (*chip, c), me).wait_recv()
            fwd = copy(a, 4 + j, (*chip, c), sibling)
            fwd.start()
            passed.append(fwd)
    for a in arrays:
        copy(a, 0, sibling, me).wait_recv()
    for j, chip in enumerate(chips):
        for a in arrays:
            copy(a, 4 + j, (*chip, 1 - c), me).wait_recv()
    for cp in first + passed:
        cp.wait_send()


def _gather_behind(own, gats, send_sems, recv_sems, local_sems, step, mid, last):
    x, y, c = _mesh_pos()
    me, sibling = (x, y, c), (x, y, 1 - c)
    chips = [(1 - x, y), (x, 1 - y), (1 - x, 1 - y)]
    arrays = range(len(gats))

    def copy(a, k, blk, to, src=None):
        slab = gats[a].at[4 * blk[0] + 2 * blk[1] + blk[2]]
        return _remote(slab if src is None else src, slab, send_sems, recv_sems, 7 * a + k, to)

    first = [copy(a, 1 + j, me, (*chip, c), src=own[a]) for j, chip in enumerate(chips) for a in arrays]
    first += [copy(a, 0, me, sibling, src=own[a]) for a in arrays]
    local = [pltpu.make_async_copy(own[a], gats[a].at[4 * x + 2 * y + c], local_sems.at[a]) for a in arrays]
    passed = [copy(a, 4 + j, (*chip, c), sibling) for j, chip in enumerate(chips) for a in arrays]

    @pl.when(step == 0)
    def _():
        for cp in first + local:
            cp.start()

    @pl.when(step == mid)
    def _():
        for j, chip in enumerate(chips):
            for a in arrays:
                copy(a, 1 + j, (*chip, c), me).wait_recv()
        for cp in passed:
            cp.start()

    @pl.when(step == last)
    def _():
        for a in arrays:
            copy(a, 0, sibling, me).wait_recv()
        for j, chip in enumerate(chips):
            for a in arrays:
                copy(a, 4 + j, (*chip, 1 - c), me).wait_recv()
        for cp in first + passed:
            cp.wait_send()
        for cp in local:
            cp.wait()


def _gather_first(w_in, w_uq2, w_oa, w_ob, w_out, x2, pos_col, invf_lane):
    hw = MLA_HEADS * HEAD_PAD
    uq_rows = Q_LORA_RANK // N_DEV
    rows = 256

    def body(win_ref, wuq_ref, woa_ref, wob_ref, wout_ref, x_ref, pos_ref, invf_ref,
             wc_ref, wq_ref, winb_ref, oab_ref, obb_ref, outb_ref, xb_ref, xt_ref, c_ref, sa_ref, sb_ref,
             g_uq, blk0, send_sems, recv_sems):
        def local_work():
            for i in range(SEQ // rows):
                xi = x_ref[rows * i:rows * (i + 1), :]
                xb_ref[rows * i:rows * (i + 1), :] = xi.astype(BF16)
                xt_ref[:, rows * i:rows * (i + 1)] = xi.T.astype(BF16)
            ang = pos_ref[...].astype(F32) * invf_ref[...]
            cs, sn = jnp.cos(ang), jnp.sin(ang)
            lane = lax.broadcasted_iota(jnp.int32, ang.shape, 1)
            c_ref[...] = jnp.where(lane < ROPE_LO, 1.0, jnp.where(lane < ROPE_HI, cs, 0.0))
            sa_ref[...] = jnp.where(jnp.logical_and(lane >= ROPE_LO, lane < ROPE_MID), -sn, 0.0)
            sb_ref[...] = jnp.where(jnp.logical_and(lane >= ROPE_MID, lane < ROPE_HI), sn, 0.0)

        x, y, c = _mesh_pos()
        me = (x, y, c)
        winb_ref[...] = win_ref[0].astype(BF16)
        oab_ref[...] = woa_ref[0].astype(BF16)
        obb_ref[...] = wob_ref[0].astype(BF16)
        outb_ref[...] = wout_ref[0].astype(BF16)
        g_uq[4 * x + 2 * y + c] = wuq_ref[...].astype(BF16)

        chip0 = jnp.logical_and(x == 0, y == 0)
        south = c == 0
        half = D_MODEL // 2
        halves = [blk0.at[pl.ds(0, half)], blk0.at[pl.ds(half, half)]]

        def bcopy(k, to, part=None):
            ref = blk0 if part is None else halves[part]
            return _remote(ref, ref, send_sems, recv_sems, 7 + k, to)

        sends0 = [(0, (0, 0, 1), None), (1, (1, 0, 0), 0), (2, (0, 1, 0), 1), (3, (1, 0, 0), 1), (4, (0, 1, 0), 0)]

        @pl.when(jnp.logical_and(chip0, south))
        def _():
            blk0[...] = winb_ref[...]
            for k, to, part in sends0:
                bcopy(k, to, part).start()

        _gather_exchange([g_uq], send_sems, recv_sems, meanwhile=local_work)

        for (cx, cy), first_k, first_half, second_k in (((1, 0), 1, 0, 3), ((0, 1), 2, 1, 4)):
            @pl.when(jnp.logical_and(jnp.logical_and(x == cx, y == cy), south))
            def _(cx=cx, cy=cy, first_k=first_k, first_half=first_half, second_k=second_k):
                bcopy(first_k, me, first_half).wait_recv()
                onward = bcopy(5 + first_half, (1, 1, 0), first_half)
                onward.start()
                bcopy(second_k, me, 1 - first_half).wait_recv()
                north = bcopy(7, (cx, cy, 1))
                north.start()
                onward.wait_send()
                north.wait_send()

        @pl.when(jnp.logical_and(jnp.logical_and(x == 1, y == 1), south))
        def _():
            bcopy(5, me, 0).wait_recv()
            bcopy(6, me, 1).wait_recv()
            north = bcopy(7, (1, 1, 1))
            north.start()
            north.wait_send()

        @pl.when(jnp.logical_and(chip0, c == 1))
        def _():
            bcopy(0, me).wait_recv()

        @pl.when(jnp.logical_and(jnp.logical_not(chip0), c == 1))
        def _():
            bcopy(7, me).wait_recv()

        @pl.when(jnp.logical_and(chip0, south))
        def _():
            for k, to, part in sends0:
                bcopy(k, to, part).wait_send()

        for j, s0, s1, seg, d0 in _column_runs():
            if seg == 2:
                wc_ref[:, d0:d0 + (s1 - s0)] = blk0[:, s0:s1]
        zeros = lambda r, w: jnp.zeros((r, w), BF16)
        wc_ref[:, Q_LORA_RANK:CQ_PAD] = zeros(D_MODEL, CQ_PAD - Q_LORA_RANK)
        wc_ref[:, CQ_PAD + LANES:CQ_PAD + LANES + ROPE_LO] = zeros(D_MODEL, ROPE_LO)
        wc_ref[:, CQ_PAD + LANES + ROPE_HI:SEG_C] = zeros(D_MODEL, LANES - ROPE_HI)
        wq_ref[Q_LORA_RANK:CQ_PAD, :] = zeros(CQ_PAD - Q_LORA_RANK, hw)
        for h in range(MLA_HEADS):
            wq_ref[0:Q_LORA_RANK, HEAD_PAD * h + QK_HEAD_DIM:HEAD_PAD * (h + 1)] = zeros(Q_LORA_RANK, HEAD_PAD - QK_HEAD_DIM)
        for j in range(N_DEV):
            for h in range(MLA_HEADS):
                wq_ref[uq_rows * j:uq_rows * (j + 1), HEAD_PAD * h:HEAD_PAD * h + QK_HEAD_DIM] = g_uq[
                    j, :, QK_HEAD_DIM * h:QK_HEAD_DIM * (h + 1)]

    vmem = pl.BlockSpec(memory_space=pltpu.VMEM)
    return pl.pallas_call(
        body, name="gather_first",
        out_shape=[jax.ShapeDtypeStruct((D_MODEL, SEG_C), BF16), jax.ShapeDtypeStruct((CQ_PAD, hw), BF16),
                   jax.ShapeDtypeStruct(w_in.shape[1:], BF16), jax.ShapeDtypeStruct(w_oa.shape[1:], BF16),
                   jax.ShapeDtypeStruct(w_ob.shape[1:], BF16), jax.ShapeDtypeStruct(w_out.shape[1:], BF16),
                   jax.ShapeDtypeStruct((SEQ, D_MODEL), BF16), jax.ShapeDtypeStruct((D_MODEL, SEQ), BF16)]
        + [jax.ShapeDtypeStruct((SEQ, LANES), F32)] * 3,
        in_specs=[vmem] * 8, out_specs=[vmem] * 11,
        scratch_shapes=[pltpu.VMEM((N_DEV, uq_rows, MLA_HEADS * QK_HEAD_DIM), BF16), pltpu.VMEM((D_MODEL, SHARD_W), BF16),
                        pltpu.SemaphoreType.DMA((15,)), pltpu.SemaphoreType.DMA((15,))],
        compiler_params=pltpu.CompilerParams(vmem_limit_bytes=VMEM_BIG),
    )(w_in, w_uq2, w_oa, w_ob, w_out, x2, pos_col, invf_lane)


def _assemble_in(g_in):
    def body(g_ref, wa_ref, wb_ref):
        segs = [wa_ref, wb_ref]
        for j, s0, s1, seg, d0 in _column_runs():
            if seg < 2:
                segs[seg][:, d0:d0 + (s1 - s0)] = g_ref[j, :, s0:s1]

    return pl.pallas_call(
        body, name="assemble_in",
        out_shape=[jax.ShapeDtypeStruct((D_MODEL, SEG_A), BF16), jax.ShapeDtypeStruct((D_MODEL, SEG_B), BF16)],
        compiler_params=pltpu.CompilerParams(vmem_limit_bytes=VMEM_MID),
    )(g_in)


def _assemble_out(g_oa, g_ob, g_out):
    cols = D_MODEL // N_DEV

    def body(goa_ref, gob_ref, gout_ref, oa_ref, ob_ref, out_ref):
        for j in range(N_DEV):
            oa_ref[:, cols * j:cols * (j + 1)] = goa_ref[j]
            ob_ref[:, cols * j:cols * (j + 1)] = gob_ref[j]
            out_ref[cols * j:cols * (j + 1), :] = gout_ref[j]

    return pl.pallas_call(
        body, name="assemble_out",
        out_shape=[jax.ShapeDtypeStruct((MLA_WIDTH, D_MODEL), BF16), jax.ShapeDtypeStruct((SGU_WIDTH, D_MODEL), BF16),
                   jax.ShapeDtypeStruct((D_MODEL, D_MODEL), BF16)],
    )(g_oa, g_ob, g_out)


C_NAT = 544


def _to_parts(dwa, dwb):
    def body(dwa_ref, dwb_ref, pin_ref):
        pin_ref[0, :, 0:C_NAT] = jnp.zeros((D_MODEL, C_NAT), BF16)
        segs = [dwa_ref, dwb_ref]
        for j, s0, s1, seg, d0 in _column_runs():
            if seg < 2:
                pin_ref[j, :, s0:s1] = segs[seg][:, d0:d0 + (s1 - s0)]

    return pl.pallas_call(body, name="to_parts", out_shape=jax.ShapeDtypeStruct((N_DEV, D_MODEL, SHARD_W), BF16),
                          compiler_params=pltpu.CompilerParams(vmem_limit_bytes=VMEM_MID))(dwa, dwb)


def _dx_tail(dhs, ws, dx_res, dwc, p_uq, p_rep):
    tm = SEQ // 4
    rep_rows = p_rep.shape[1]
    c_rows = D_MODEL // N_DEV
    spec = [((c_rows, C_NAT), BF16), (p_uq.shape[1:], BF16), ((rep_rows, LANES), F32)]
    n = len(spec)

    nseg = len(dhs)

    def body(*refs):
        dh_refs, w_refs = refs[:nseg], refs[nseg:2 * nseg]
        dxr_ref, dwc_ref, puq_ref, prep_ref, dx_ref, call_ref, guq_ref, repall_ref, pc_ref, c_all, rep_all = refs[
            2 * nseg:2 * nseg + 11]
        rest = refs[2 * nseg + 11:]
        ras, tbs, rbs = rest[0:n], rest[n:2 * n], rest[2 * n:3 * n]
        send_sems, recv_sems, gsend, grecv = rest[3 * n:]
        step = pl.program_id(0)
        x, y, c = _mesh_pos()
        me_idx = 4 * x + 2 * y + c
        me, sibling = (x, y, c), (x, y, 1 - c)
        others = [(1 - x, y), (x, 1 - y), (1 - x, 1 - y)]
        parts = [pc_ref, puq_ref, prep_ref]
        gats = [rep_all, c_all]

        def stage1(chip, a):
            return _remote(parts[a].at[2 * chip + (1 - c)], ras[a].at[chip], send_sems, recv_sems, 7 * a + chip, sibling)

        def stage2(k, a):
            cx, cy = others[k]
            return _remote(tbs[a].at[k], rbs[a].at[k], send_sems, recv_sems, 7 * a + 4 + k, (cx, cy, c))

        def gcopy(a, k, blk, to):
            slab = gats[a].at[4 * blk[0] + 2 * blk[1] + blk[2]]
            return _remote(slab, slab, gsend, grecv, 7 * a + k, to)

        def chip_sum(a, chip):
            return parts[a][2 * chip + c].astype(F32) + ras[a][chip].astype(F32)

        @pl.when(step == 0)
        def _():
            for j, s0, s1, seg, d0 in _column_runs():
                if seg == 2:
                    for r in range(N_DEV):
                        pc_ref[r, :, s0:s1] = dwc_ref[c_rows * r:c_rows * (r + 1), d0:d0 + (s1 - s0)]
            for chip in range(4):
                for a in range(n):
                    stage1(chip, a).start()

        @pl.when(step == 1)
        def _():
            for chip in range(4):
                for a in range(n):
                    stage1(chip, a).wait_recv()
            for k, (cx, cy) in enumerate(others):
                for a in range(n):
                    tbs[a][k] = chip_sum(a, 2 * cx + cy).astype(spec[a][1])
                    stage2(k, a).start()

        @pl.when(step == 2)
        def _():
            for k in range(3):
                for a in range(n):
                    stage2(k, a).wait_recv()
            sums = []
            for a in range(n):
                acc = chip_sum(a, 2 * x + y)
                for k in range(3):
                    acc = acc + rbs[a][k].astype(F32)
                sums.append(acc)
            c_all[me_idx] = sums[0].astype(BF16)
            guq_ref[...] = sums[1]
            rep_all[me_idx] = sums[2]
            for a in range(2):
                for j, chip in enumerate(others):
                    gcopy(a, 1 + j, me, (*chip, c)).start()
                gcopy(a, 0, me, sibling).start()

        @pl.when(step == 3)
        def _():
            for j, chip in enumerate(others):
                for a in range(2):
                    gcopy(a, 1 + j, (*chip, c), me).wait_recv()
                    gcopy(a, 4 + j, (*chip, c), sibling).start()
            for a in range(2):
                gcopy(a, 0, sibling, me).wait_recv()
                for j, chip in enumerate(others):
                    gcopy(a, 4 + j, (*chip, 1 - c), me).wait_recv()
            for a in range(2):
                gcopy(a, 0, me, sibling).wait_send()
                for j, chip in enumerate(others):
                    gcopy(a, 1 + j, me, (*chip, c)).wait_send()
                    gcopy(a, 4 + j, (*chip, c), sibling).wait_send()
            for a in range(n):
                for chip in range(4):
                    stage1(chip, a).wait_send()
                for k in range(3):
                    stage2(k, a).wait_send()
            call_ref[...] = c_all[...]
            repall_ref[...] = rep_all[...]

        acc = dxr_ref[...]
        for dh_ref, w_ref in zip(dh_refs, w_refs):
            acc = acc + _dot(dh_ref[...], w_ref[...], _NT)
        dx_ref[...] = acc

    row = lambda w: pl.BlockSpec((tm, w), lambda i: (i, 0))
    full = lambda shape: pl.BlockSpec(shape, lambda i: (0,) * len(shape))
    scratch = [pltpu.VMEM((N_DEV, c_rows, C_NAT), BF16), pltpu.VMEM((N_DEV, c_rows, C_NAT), BF16),
               pltpu.VMEM((N_DEV, rep_rows, LANES), F32)]
    for lead in (4, 3, 3):
        scratch += [pltpu.VMEM((lead,) + tuple(shape), dt) for shape, dt in spec]
    scratch += [pltpu.SemaphoreType.DMA((7 * n,)), pltpu.SemaphoreType.DMA((7 * n,)),
                pltpu.SemaphoreType.DMA((14,)), pltpu.SemaphoreType.DMA((14,))]
    return pl.pallas_call(
        body, name="dx_tail", grid=(SEQ // tm,),
        in_specs=[row(dh.shape[1]) for dh in dhs] + [full(w.shape) for w in ws]
        + [row(D_MODEL), full(dwc.shape), full(p_uq.shape), full(p_rep.shape)],
        out_specs=[row(D_MODEL), full((N_DEV, c_rows, C_NAT)), full(p_uq.shape[1:]), full((N_DEV, rep_rows, LANES))],
        out_shape=[jax.ShapeDtypeStruct((SEQ, D_MODEL), F32), jax.ShapeDtypeStruct((N_DEV, c_rows, C_NAT), BF16),
                   jax.ShapeDtypeStruct(p_uq.shape[1:], F32), jax.ShapeDtypeStruct((N_DEV, rep_rows, LANES), F32)],
        scratch_shapes=scratch,
        compiler_params=pltpu.CompilerParams(dimension_semantics=("arbitrary",), vmem_limit_bytes=VMEM_BIG),
    )(*dhs, *ws, dx_res, dwc, p_uq, p_rep)


_O_CQ, _O_CKV, _O_KPE, _O_ZA, _O_U, _O_V, _O_ZB, _O_GA, _O_GB = 0, 384, 512, 544, 1056, 1568, 2080, 2592, 3616


def _to_segments(w):
    z = lambda n: jnp.zeros(w.shape[:-1] + (n,), w.dtype)
    seg_a = jnp.concatenate([w[..., _O_GA:_O_GB], w[..., _O_GB:IN_WIDTH], w[..., _O_ZA:_O_U]], axis=-1)
    seg_b = jnp.concatenate([w[..., _O_U:_O_V], w[..., _O_V:_O_ZB], w[..., _O_ZB:_O_GA]], axis=-1)
    seg_c = jnp.concatenate([w[..., _O_CQ:_O_CKV], z(CQ_PAD - Q_LORA_RANK), w[..., _O_CKV:_O_KPE],
                             z(ROPE_LO), w[..., _O_KPE:_O_ZA], z(LANES - ROPE_HI)], axis=-1)
    return seg_a, seg_b, seg_c


def _from_segments(seg_a, seg_b, seg_c):
    kpe0 = CQ_PAD + LANES + ROPE_LO
    return jnp.concatenate([
        seg_c[..., 0:Q_LORA_RANK], seg_c[..., CQ_PAD:CQ_PAD + LANES], seg_c[..., kpe0:kpe0 + QK_ROPE_DIM],
        seg_a[..., 2 * D_MODEL:SEG_A], seg_b, seg_a[..., 0:2 * D_MODEL]], axis=-1)


def kernel(x, positions, w_in, b_in, g_q, w_uq, g_kv, w_ukv, w_oa, sgu_ln_g, sgu_ln_b, w_s, b_s, w_ob, w_out, ln_g, ln_b, loss_target, m_w_in, m_b_in, m_g_q, m_w_uq, m_g_kv, m_w_ukv, m_w_oa, m_sgu_ln_g, m_sgu_ln_b, m_w_s, m_b_s, m_w_ob, m_w_out, m_ln_g, m_ln_b, v_w_in, v_b_in, v_g_q, v_w_uq, v_g_kv, v_w_ukv, v_w_oa, v_sgu_ln_g, v_sgu_ln_b, v_w_s, v_b_s, v_w_ob, v_w_out, v_ln_g, v_ln_b):
    w_uq2 = w_uq[0].reshape(Q_LORA_RANK // N_DEV, MLA_HEADS * QK_HEAD_DIM)
    inv_freq = ROPE_THETA ** (-jnp.arange(0, QK_ROPE_DIM, 2, dtype=F32) / QK_ROPE_DIM)
    invf_lane = jnp.concatenate([jnp.zeros((ROPE_LO,), F32), inv_freq, inv_freq,
                                 jnp.zeros((LANES - ROPE_HI,), F32)]).reshape(1, LANES)
    first = _gather_first(w_in, w_uq2, w_oa, w_ob, w_out, x[0], positions.reshape(SEQ, 1), invf_lane)
    partials = _local_step(x[0], loss_target[0], first, b_in, g_q, g_kv, w_ukv, sgu_ln_g, sgu_ln_b, w_s, b_s, ln_g, ln_b)
    weights = dict(w_in=w_in, b_in=b_in, g_q=g_q, w_uq=w_uq, g_kv=g_kv, w_ukv=w_ukv, w_oa=w_oa, sgu_ln_g=sgu_ln_g,
                   sgu_ln_b=sgu_ln_b, w_s=w_s, b_s=b_s, w_ob=w_ob, w_out=w_out, ln_g=ln_g, ln_b=ln_b)
    moms = dict(w_in=m_w_in, b_in=m_b_in, g_q=m_g_q, w_uq=m_w_uq, g_kv=m_g_kv, w_ukv=m_w_ukv, w_oa=m_w_oa,
                sgu_ln_g=m_sgu_ln_g, sgu_ln_b=m_sgu_ln_b, w_s=m_w_s, b_s=m_b_s, w_ob=m_w_ob, w_out=m_w_out,
                ln_g=m_ln_g, ln_b=m_ln_b)
    vars_ = dict(w_in=v_w_in, b_in=v_b_in, g_q=v_g_q, w_uq=v_w_uq, g_kv=v_g_kv, w_ukv=v_w_ukv, w_oa=v_w_oa,
                 sgu_ln_g=v_sgu_ln_g, sgu_ln_b=v_sgu_ln_b, w_s=v_w_s, b_s=v_b_s, w_ob=v_w_ob, w_out=v_w_out,
                 ln_g=v_ln_g, ln_b=v_ln_b)
    return _reduce_and_update(partials, weights, moms, vars_)


def _local_step(x2, tgt, first, b_in, g_q, g_kv, w_ukv, sgu_ln_g, sgu_ln_b, w_s, b_s, ln_g, ln_b):
    wc, wq, win_b, oa_b, ob_b, out_b, x_bf, xt_bf, c_t, sa_t, sb_t = first
    ba, bb, bc = _to_segments(b_in)
    w_ukv_bf = w_ukv[0].astype(BF16)
    wkn = jnp.pad(w_ukv_bf[:, :, :QK_NOPE_DIM], ((0, 0), (0, 0), (0, HEAD_PAD - QK_NOPE_DIM))).reshape(KV_LORA_RANK, -1)
    wv = jnp.pad(w_ukv_bf[:, :, QK_NOPE_DIM:], ((0, 0), (0, 0), (0, HEAD_PAD - V_HEAD_DIM))).reshape(KV_LORA_RANK, -1)
    gq = jnp.pad(g_q, ((0, 0), (0, CQ_PAD - Q_LORA_RANK)))
    bias_full = jnp.repeat(b_s[0].T, SGU_GROUP_DIM, axis=1)
    w_s3 = w_s[0]
    w_st3 = jnp.swapaxes(w_s3, 1, 2)

    h_c = _mm(x_bf, wc, bias=bc, tm=512, tn=SEG_C, name="in_proj_c")
    q, k, kt, vx, vxt = _mla_prep(h_c, gq, g_kv, wq, wkn, wv, c_t, sa_t, sb_t)
    o, lse, (g_in,) = _attn_fwd(q, kt, vx, (win_b,))
    wa, wb = _assemble_in(g_in)
    h_a, (g_out,) = _mm(x_bf, wa, bias=ba, own=(out_b,), tm=512, tn=SEG_A // 2, name="in_proj_a")
    h_b, (g_oa, g_ob) = _mm(x_bf, wb, bias=bb, own=(oa_b, ob_b), tm=512, tn=SEG_B // 2, name="in_proj_b")
    y_b = _sgu_fwd(h_b, sgu_ln_g, sgu_ln_b, w_s3, bias_full)
    w_oa_f, w_ob_f, w_out_f = _assemble_out(g_oa, g_ob, g_out)

    (loss_row, dx_res, dh_a, d_o, d_yb, p_oa, p_ob, p_out, d_lng, d_lnb, d_ba) = _merge(
        x2, o, h_a, y_b, tgt, w_oa_f, w_ob_f, w_out_f, ln_g, ln_b)
    (dh_b, d_ws, d_bs_t, d_slg, d_slb, d_bb), (r_out,) = _sgu_bwd(h_b, d_yb, sgu_ln_g, sgu_ln_b, w_s3, w_st3, bias_full,
                                                                 (p_out,))
    d_wa, (r_oa,) = _mm(xt_bf, dh_a, out_dtype=BF16, parts=(p_oa,), tm=512, tn=512, name="dw_in_a")
    d_wb, (r_ob,) = _mm(xt_bf, dh_b, out_dtype=BF16, parts=(p_ob,), tm=512, tn=512, name="dw_in_b")
    dq, dk, dv, landed_in = _attn_bwd(q, kt, k, vxt, d_o, o, lse, (_to_parts(d_wa, d_wb),))
    landed = (*landed_in, r_oa, r_ob, r_out)
    dh_c, p_uq, d_wkn, d_wv, d_gq, d_gkv, d_bc = _mla_bwd(dq, dk, dv, h_c, gq, g_kv, wq, wkn, wv, c_t, sa_t, sb_t)
    d_wc = _mm(xt_bf, dh_c, out_dtype=BF16, tm=512, tn=SEG_C, name="dw_in_c")


    p_b_in = _from_segments(d_ba, d_bb, d_bc)
    p_w_ukv = jnp.concatenate([d_wkn.reshape(KV_LORA_RANK, MLA_HEADS, HEAD_PAD)[:, :, :QK_NOPE_DIM],
                               d_wv.reshape(KV_LORA_RANK, MLA_HEADS, HEAD_PAD)[:, :, :V_HEAD_DIM]], axis=-1)
    p_g_q = d_gq[:, :Q_LORA_RANK]
    p_b_s = d_bs_t[:, :SGU_GROUPS].T
    replicated = [p_b_in, p_g_q, d_gkv, p_w_ukv, d_slg, d_slb, d_ws, p_b_s, d_lng, d_lnb]
    return loss_row, ((dh_a, dh_b, dh_c), (wa, wb, wc), dx_res), landed, d_wc, p_uq, replicated


_NAMES = ["w_in", "b_in", "g_q", "w_uq", "g_kv", "w_ukv", "w_oa", "sgu_ln_g", "sgu_ln_b", "w_s", "b_s", "w_ob",
          "w_out", "ln_g", "ln_b"]
_REPLICATED = ["b_in", "g_q", "g_kv", "w_ukv", "sgu_ln_g", "sgu_ln_b", "w_s", "b_s", "ln_g", "ln_b"]


def _reduce_and_update(partials, weights, moms, vars_):
    loss_row, (dhs, ws, dx_res), landed, d_wc, p_uq, replicated = partials
    rep_flat = jnp.concatenate([a.reshape(-1) for a in replicated] + [loss_row[0, :1]])
    rep_flat = jnp.pad(rep_flat, (0, N_DEV * PACK_R_ROWS * LANES - rep_flat.size))
    dx_ab, c_all, g_uq, rep_all = _dx_tail(dhs[:2], ws[:2], dx_res, d_wc, p_uq,
                                           rep_flat.reshape(N_DEV, PACK_R_ROWS, LANES))
    dx = _mm(dhs[2], ws[2], tb=True, add=dx_ab, tm=512, tn=D_MODEL, name="dx_c")
    rep_sum = rep_all.reshape(-1)
    grads, pos = dict(w_uq=g_uq.reshape(weights["w_uq"].shape)), 0
    for nm in _REPLICATED:
        grads[nm] = rep_sum[pos:pos + weights[nm].size].reshape(weights[nm].shape)
        pos += weights[nm].size
    loss = rep_sum[pos]
    in_parts = ["w_in", "w_oa", "w_ob", "w_out"]
    deltas, new_m, new_v, summed = _adamw_all([weights[nm] for nm in _NAMES], [grads.get(nm) for nm in _NAMES],
                                              [moms[nm] for nm in _NAMES], [vars_[nm] for nm in _NAMES], landed, c_all)
    grads.update(zip(in_parts, summed))
    return (loss, dx.reshape(1, SEQ, D_MODEL), *[grads[nm] for nm in _NAMES], *deltas, *new_m, *new_v)
```

```python
import math

import jax
import jax.numpy as jnp
from jax import lax
from jax.experimental import pallas as pl
from jax.experimental.pallas import tpu as pltpu

F32 = jnp.float32
BF16 = jnp.bfloat16

D_MODEL = 1024
SEQ = 2048
N_DEV = 8
MLA_HEADS = 8
Q_LORA_RANK = 384
KV_LORA_RANK = 128
QK_NOPE_DIM = 64
QK_ROPE_DIM = 32
V_HEAD_DIM = 64
QK_HEAD_DIM = QK_NOPE_DIM + QK_ROPE_DIM
MLA_WIDTH = MLA_HEADS * V_HEAD_DIM
ROPE_THETA = 10000.0
SGU_GROUPS = 8
SGU_GROUP_DIM = 64
SGU_WIDTH = SGU_GROUPS * SGU_GROUP_DIM
CHUNK = 128
RMS_EPS = 1e-6
LN_EPS = 1e-5
DN_ALPHA = 2.0 ** 0.25
IN_WIDTH = 4640
ATTN_SCALE = QK_HEAD_DIM ** -0.5

ADAM_LR = 0.001
ADAM_B1 = 0.9
ADAM_B2 = 0.999
ADAM_EPS = 1e-08
ADAM_WD = 0.01
ADAM_STEP = 10

LANES = 128
HEAD_PAD = 128
ROPE_LO = QK_NOPE_DIM
ROPE_MID = ROPE_LO + QK_ROPE_DIM // 2
ROPE_HI = ROPE_LO + QK_ROPE_DIM
CQ_PAD = 512

SEG_A = 2560
SEG_B = 1536
SEG_C = 768

PACK_R_ROWS = 272
VMEM_BIG = 56 * 1024 * 1024
VMEM_MID = 40 * 1024 * 1024


def _sigmoid(x):
    return 1.0 / (1.0 + jnp.exp(-x))


def _gelu_and_grad(x):
    c0 = math.sqrt(2.0 / math.pi)
    x2 = x * x
    t = jnp.tanh(c0 * (x + 0.044715 * x * x2))
    g = 0.5 * x * (1.0 + t)
    dg = 0.5 * (1.0 + t) + 0.5 * x * (1.0 - t * t) * (c0 * (1.0 + 3.0 * 0.044715 * x2))
    return g, dg


def _dot(a, b, dims):
    return lax.dot_general(a, b, (dims, ((), ())), preferred_element_type=F32)


_NN = ((1,), (0,))
_NT = ((1,), (1,))
_TN = ((0,), (0,))


def _store_grad(dh_ref, db_ref, col, val):
    cols = slice(col, col + val.shape[1])
    dh_ref[:, cols] = val.astype(BF16)
    db_ref[:, cols] += jnp.sum(val, axis=0, keepdims=True)


def _mm(a, b, *, tb=False, bias=None, add=None, out_dtype=F32, own=(), parts=(), tm, tn, name):
    m, k = a.shape
    n = b.shape[0] if tb else b.shape[1]
    assert m % tm == 0 and n % tn == 0 and not (own and parts)
    dims = _NT if tb else _NN
    nown = len(own) + len(parts)
    nm = m // tm
    nsteps = (n // tn) * nm

    def body(*refs):
        a_ref, b_ref = refs[0], refs[1]
        pos = 2
        r = _dot(a_ref[...], b_ref[...], dims)
        if bias is not None:
            r = r + refs[pos][...]; pos += 1
        if add is not None:
            r = r + refs[pos][...]; pos += 1
        own_refs = refs[pos:pos + nown]; pos += nown
        refs[pos][...] = r.astype(out_dtype)
        if nown:
            gat_refs = refs[pos + 1:pos + 1 + nown]
            send_sems, recv_sems, local_sems = refs[pos + 1 + nown:]
            step = pl.program_id(0) * nm + pl.program_id(1)
            if own:
                _gather_behind(own_refs, gat_refs, send_sems, recv_sems, local_sems, step, nsteps - 2, nsteps - 1)
            else:
                exchange = _exchange_parts(own_refs, gat_refs, send_sems, recv_sems, local_sems)
                _exchange_start(step == 0, exchange)
                _exchange_finish(step == nsteps - 1, exchange)

    b_spec = pl.BlockSpec((tn, k), lambda j, i: (j, 0)) if tb else pl.BlockSpec((k, tn), lambda j, i: (0, j))
    in_specs, args = [pl.BlockSpec((tm, k), lambda j, i: (i, 0)), b_spec], [a, b]
    if bias is not None:
        in_specs.append(pl.BlockSpec((1, tn), lambda j, i: (0, j))); args.append(bias)
    if add is not None:
        in_specs.append(pl.BlockSpec((tm, tn), lambda j, i: (i, j))); args.append(add)
    hbm = pl.BlockSpec(memory_space=pl.ANY)
    res = pl.pallas_call(
        body, name=name, grid=(n // tn, nm), in_specs=in_specs + [hbm] * nown,
        out_specs=[pl.BlockSpec((tm, tn), lambda j, i: (i, j))] + [hbm] * nown,
        out_shape=[jax.ShapeDtypeStruct((m, n), out_dtype)]
        + [jax.ShapeDtypeStruct((N_DEV,) + o.shape, o.dtype) for o in own]
        + [jax.ShapeDtypeStruct(p.shape, p.dtype) for p in parts],
        scratch_shapes=_exchange_sems(nown) if nown else [],
        compiler_params=pltpu.CompilerParams(dimension_semantics=("arbitrary", "arbitrary"), vmem_limit_bytes=VMEM_BIG),
    )(*args, *own, *parts)
    return (res[0], res[1:]) if nown else res[0]


def _rope(x, c, sa, sb):
    return x * c + pltpu.roll(x, LANES - 16, 1) * sa + pltpu.roll(x, 16, 1) * sb


def _rope_t(dy, c, sa, sb):
    return dy * c + pltpu.roll(dy * sa, 16, 1) + pltpu.roll(dy * sb, LANES - 16, 1)


def _mla_prep(h_c, gq, gkv, wq, wkn, wvx, c_t, sa_t, sb_t):
    tm = 256
    hw = MLA_HEADS * HEAD_PAD

    def body(cq_ref, ckv_ref, kpe_ref, gq_ref, gkv_ref, wq_ref, wkn_ref, wvx_ref, c_ref, sa_ref, sb_ref,
             q_ref, k_ref, kt_ref, vx_ref, vxt_ref):
        c, sa, sb = c_ref[...], sa_ref[...], sb_ref[...]
        cq = cq_ref[...]
        rq = lax.rsqrt(jnp.sum(cq * cq, axis=1, keepdims=True) * (1.0 / Q_LORA_RANK) + RMS_EPS)
        cqn = ((cq * rq) * gq_ref[...]).astype(BF16)
        qall = _dot(cqn, wq_ref[...], _NN)
        for h in range(MLA_HEADS):
            sl = slice(HEAD_PAD * h, HEAD_PAD * (h + 1))
            q_ref[:, sl] = (_rope(qall[:, sl], c, sa, sb) * ATTN_SCALE).astype(BF16)
        ckv = ckv_ref[...]
        rkv = lax.rsqrt(jnp.sum(ckv * ckv, axis=1, keepdims=True) * (1.0 / KV_LORA_RANK) + RMS_EPS)
        ckvn = ((ckv * rkv) * gkv_ref[...]).astype(BF16)
        knall = _dot(ckvn, wkn_ref[...], _NN)
        vall = _dot(ckvn, wvx_ref[...], _NN)
        kper = _rope(kpe_ref[...], c, sa, sb)
        ones_half = (lax.broadcasted_iota(jnp.int32, (tm, HEAD_PAD), 1) >= V_HEAD_DIM).astype(F32)
        for h in range(MLA_HEADS):
            sl = slice(HEAD_PAD * h, HEAD_PAD * (h + 1))
            kh = knall[:, sl] + kper
            vh = vall[:, sl] + ones_half
            k_ref[:, sl] = kh.astype(BF16)
            kt_ref[sl, :] = kh.T.astype(BF16)
            vx_ref[:, sl] = vh.astype(BF16)
            vxt_ref[sl, :] = vh.T.astype(BF16)

    full = lambda shape: pl.BlockSpec(shape, lambda i: (0, 0))
    tab = pl.BlockSpec((tm, LANES), lambda i: (i, 0))
    row = pl.BlockSpec((tm, hw), lambda i: (i, 0))
    col = pl.BlockSpec((hw, tm), lambda i: (0, i))
    return pl.pallas_call(
        body, name="mla_prep", grid=(SEQ // tm,),
        in_specs=[pl.BlockSpec((tm, CQ_PAD), lambda i: (i, 0)),
                  pl.BlockSpec((tm, LANES), lambda i: (i, CQ_PAD // LANES)),
                  pl.BlockSpec((tm, LANES), lambda i: (i, CQ_PAD // LANES + 1)),
                  full((1, CQ_PAD)), full((1, KV_LORA_RANK)),
                  full((CQ_PAD, hw)), full((KV_LORA_RANK, hw)), full((KV_LORA_RANK, hw)), tab, tab, tab],
        out_specs=[row, row, col, row, col],
        out_shape=[jax.ShapeDtypeStruct((SEQ, hw), BF16), jax.ShapeDtypeStruct((SEQ, hw), BF16),
                   jax.ShapeDtypeStruct((hw, SEQ), BF16), jax.ShapeDtypeStruct((SEQ, hw), BF16),
                   jax.ShapeDtypeStruct((hw, SEQ), BF16)],
        compiler_params=pltpu.CompilerParams(dimension_semantics=("arbitrary",), vmem_limit_bytes=VMEM_MID),
    )(h_c, h_c, h_c, gq, gkv, wq, wkn, wvx, c_t, sa_t, sb_t)


ATT_T = 512
ATT_STRIP = 64


def _attn_fwd(q, kt, vx, own):
    t, rs = ATT_T, ATT_STRIP
    nown = len(own)
    nq = SEQ // t
    nsteps = (MLA_HEADS // 2) * nq

    def body(q_ref, kt_ref, vx_ref, *rest):
        own_refs, (o_ref, l_ref), gat_refs = rest[:nown], rest[nown:nown + 2], rest[nown + 2:2 * nown + 2]
        s_scr, p_scr, m_scr, a_scr, acc_scr, send_sems, recv_sems, local_sems = rest[2 * nown + 2:]
        qi = pl.program_id(1)
        _gather_behind(own_refs, gat_refs, send_sems, recv_sems, local_sems, pl.program_id(0) * nq + qi,
                       nsteps - 2, nsteps - 1)
        lane = lax.broadcasted_iota(jnp.int32, (t, LANES), 1)
        m_scr[...] = jnp.full((2, t, LANES), -1e30, F32)
        acc_scr[...] = jnp.zeros((2, t, LANES), F32)

        def block(j, masked):
            off = pl.multiple_of(j * t, t)
            for a in range(2):
                sl = slice(HEAD_PAD * a, HEAD_PAD * (a + 1))
                s_scr[a] = _dot(q_ref[:, sl], kt_ref[sl, pl.ds(off, t)], _NN)
                for r in range(t // rs):
                    rows = slice(rs * r, rs * (r + 1))
                    s = s_scr[a, rows, :]
                    if masked:
                        rowi = lax.broadcasted_iota(jnp.int32, (rs, t), 0) + rs * r
                        coli = lax.broadcasted_iota(jnp.int32, (rs, t), 1)
                        s = jnp.where(coli <= rowi, s, -1e30)
                    m_old = m_scr[a, rows, :]
                    m_new = jnp.maximum(m_old, jnp.max(s, axis=1, keepdims=True))
                    p_scr[a, rows, :] = jnp.exp(s - m_new[:, :1]).astype(BF16)
                    a_scr[a, rows, :] = jnp.exp(m_old - m_new)
                    m_scr[a, rows, :] = m_new
                acc_scr[a] = acc_scr[a] * a_scr[a] + _dot(p_scr[a], vx_ref[pl.ds(off, t), sl], _NN)

        def step(j, carry):
            block(j, False)
            return carry
        lax.fori_loop(0, qi, step, 0)
        block(qi, True)
        res = []
        for a in range(2):
            acc = acc_scr[a]
            l = acc[:, V_HEAD_DIM:V_HEAD_DIM + 1]
            res.append((acc / l, m_scr[a] + jnp.log(l)))
        o_ref[...] = jnp.where(lane < V_HEAD_DIM, res[0][0], pltpu.roll(res[1][0], V_HEAD_DIM, 1))
        l_ref[...] = jnp.where(lane < V_HEAD_DIM, res[0][1], res[1][1])

    hbm = pl.BlockSpec(memory_space=pl.ANY)
    res = pl.pallas_call(
        body, name="attn_fwd", grid=(MLA_HEADS // 2, nq),
        in_specs=[pl.BlockSpec((t, 2 * HEAD_PAD), lambda p, i: (i, p)),
                  pl.BlockSpec((2 * HEAD_PAD, SEQ), lambda p, i: (p, 0)),
                  pl.BlockSpec((SEQ, 2 * HEAD_PAD), lambda p, i: (0, p))] + [hbm] * nown,
        out_specs=[pl.BlockSpec((t, LANES), lambda p, i: (i, p)),
                   pl.BlockSpec((t, LANES), lambda p, i: (i, p))] + [hbm] * nown,
        out_shape=[jax.ShapeDtypeStruct((SEQ, MLA_WIDTH), F32), jax.ShapeDtypeStruct((SEQ, MLA_WIDTH), F32)]
        + [jax.ShapeDtypeStruct((N_DEV,) + a.shape, a.dtype) for a in own],
        scratch_shapes=[pltpu.VMEM((2, t, t), F32), pltpu.VMEM((2, t, t), BF16), pltpu.VMEM((2, t, LANES), F32),
                        pltpu.VMEM((2, t, LANES), F32), pltpu.VMEM((2, t, LANES), F32)] + _exchange_sems(nown),
        compiler_params=pltpu.CompilerParams(dimension_semantics=("arbitrary", "arbitrary"), vmem_limit_bytes=VMEM_MID),
    )(q, kt, vx, *own)
    return res[0], res[1], res[2:]


def _exchange_parts(parts, lands, send_sems, recv_sems, local_sems):
    x, y, c = _mesh_pos()
    me = 4 * x + 2 * y + c
    peers = [(x, y, 1 - c), (1 - x, y, c), (x, 1 - y, c), (1 - x, 1 - y, c),
             (1 - x, y, 1 - c), (x, 1 - y, 1 - c), (1 - x, 1 - y, 1 - c)]
    remote, local = [], []
    for a, (part, land) in enumerate(zip(parts, lands)):
        for k, peer in enumerate(peers):
            t = 4 * peer[0] + 2 * peer[1] + peer[2]
            remote.append(_remote(part.at[t], land.at[me], send_sems, recv_sems, 7 * a + k, peer))
        local.append(pltpu.make_async_copy(part.at[me], land.at[me], local_sems.at[a]))
    return remote, local


def _exchange_start(first_step, exchange):
    remote, local = exchange

    @pl.when(first_step)
    def _():
        for cp in remote + local:
            cp.start()


def _exchange_finish(last_step, exchange):
    remote, local = exchange

    @pl.when(last_step)
    def _():
        for cp in remote:
            cp.wait_recv()
        for cp in remote:
            cp.wait_send()
        for cp in local:
            cp.wait()


def _exchange_sems(npart):
    return [pltpu.SemaphoreType.DMA((7 * npart,)), pltpu.SemaphoreType.DMA((7 * npart,)),
            pltpu.SemaphoreType.DMA((npart,))]


def _attn_bwd(q, kt, k, vxt, d_o, o, lse, parts):
    t, rs = ATT_T, ATT_STRIP
    nq = SEQ // t
    npart = len(parts)
    nsteps = MLA_HEADS // 2

    def body(q_ref, kt_ref, k_ref, vxt_ref, do_ref, o_ref, l_ref, *rest):
        part_refs, rest = rest[:npart], rest[npart:]
        dq_ref, dk_ref, dv_ref = rest[:3]
        land_refs, rest = rest[3:3 + npart], rest[3 + npart:]
        s_scr, dp_scr, p_scr, ds_scr, st_scr, send_sems, recv_sems, local_sems = rest
        exchange = _exchange_parts(part_refs, land_refs, send_sems, recv_sems, local_sems)
        _exchange_start(pl.program_id(0) == 0, exchange)
        dk_ref[...] = jnp.zeros_like(dk_ref)
        dv_ref[...] = jnp.zeros_like(dv_ref)
        lane = lax.broadcasted_iota(jnp.int32, (t, LANES), 1)

        def qtile(i, carry):
            ioff = pl.multiple_of(i * t, t)
            do_i = do_ref[pl.ds(ioff, t), :]
            o_i = o_ref[pl.ds(ioff, t), :]
            l_i = l_ref[pl.ds(ioff, t), :]
            for a in range(2):
                sl = slice(HEAD_PAD * a, HEAD_PAD * (a + 1))
                sel = (lane < V_HEAD_DIM) if a == 0 else (lane >= V_HEAD_DIM)
                doa = jnp.where(sel, do_i, 0.0)
                oa = o_i
                if a == 1:
                    doa = pltpu.roll(doa, V_HEAD_DIM, 1)
                    oa = pltpu.roll(o_i, V_HEAD_DIM, 1)
                st_scr[0] = jnp.broadcast_to(jnp.sum(doa * oa, axis=1, keepdims=True), (t, LANES))
                st_scr[1] = jnp.broadcast_to(l_i[:, V_HEAD_DIM * a:V_HEAD_DIM * a + 1], (t, LANES))
                doa_bf = doa.astype(BF16)
                qa = q_ref[pl.ds(ioff, t), sl]

                def block(j, masked, dq_acc, sl=sl, qa=qa, doa_bf=doa_bf):
                    joff = pl.multiple_of(j * t, t)
                    s_scr[...] = _dot(qa, kt_ref[sl, pl.ds(joff, t)], _NN)
                    dp_scr[...] = _dot(doa_bf, vxt_ref[sl, pl.ds(joff, t)], _NN)
                    for r in range(t // rs):
                        rows = slice(rs * r, rs * (r + 1))
                        p = jnp.exp(s_scr[rows, :] - st_scr[1, rows, :1])
                        if masked:
                            rowi = lax.broadcasted_iota(jnp.int32, (rs, t), 0) + rs * r
                            coli = lax.broadcasted_iota(jnp.int32, (rs, t), 1)
                            p = jnp.where(coli <= rowi, p, 0.0)
                        p_scr[rows, :] = p.astype(BF16)
                        ds_scr[rows, :] = (p * (dp_scr[rows, :] - st_scr[0, rows, :1])).astype(BF16)
                    dk_ref[pl.ds(joff, t), sl] += _dot(ds_scr[...], qa, _TN)
                    dv_ref[pl.ds(joff, t), sl] += _dot(p_scr[...], doa_bf, _TN)
                    return dq_acc + _dot(ds_scr[...], k_ref[pl.ds(joff, t), sl], _NN)

                dq_acc = lax.fori_loop(0, i, lambda j, acc: block(j, False, acc), jnp.zeros((t, HEAD_PAD), F32))
                dq_ref[pl.ds(ioff, t), sl] = block(i, True, dq_acc)
            return carry

        lax.fori_loop(0, nq, qtile, 0)
        _exchange_finish(pl.program_id(0) == nsteps - 1, exchange)

    hw = MLA_HEADS * HEAD_PAD
    wide = pl.BlockSpec((SEQ, 2 * HEAD_PAD), lambda p: (0, p))
    wide_t = pl.BlockSpec((2 * HEAD_PAD, SEQ), lambda p: (p, 0))
    narrow = pl.BlockSpec((SEQ, LANES), lambda p: (0, p))
    hbm = pl.BlockSpec(memory_space=pl.ANY)
    res = pl.pallas_call(
        body, name="attn_bwd", grid=(nsteps,),
        in_specs=[wide, wide_t, wide, wide_t, narrow, narrow, narrow] + [hbm] * npart,
        out_specs=[wide, wide, wide] + [hbm] * npart,
        out_shape=[jax.ShapeDtypeStruct((SEQ, hw), F32)] * 3 + [jax.ShapeDtypeStruct(p.shape, p.dtype) for p in parts],
        scratch_shapes=[pltpu.VMEM((t, t), F32), pltpu.VMEM((t, t), F32), pltpu.VMEM((t, t), BF16),
                        pltpu.VMEM((t, t), BF16), pltpu.VMEM((2, t, LANES), F32)] + _exchange_sems(npart),
        compiler_params=pltpu.CompilerParams(dimension_semantics=("arbitrary",), vmem_limit_bytes=VMEM_BIG),
    )(q, kt, k, vxt, d_o, o, lse, *parts)
    return res[0], res[1], res[2], res[3:]


def _sgu_math(u, v, zb, lg, lb, ws_ref, bias):
    ug, dug = _gelu_and_grad(u)
    vg, dvg = _gelu_and_grad(v)
    mu = jnp.mean(vg, axis=1, keepdims=True)
    xc = vg - mu
    rstd = lax.rsqrt(jnp.mean(xc * xc, axis=1, keepdims=True) + LN_EPS)
    xh = xc * rstd
    vn_bf = (xh * lg + lb).astype(BF16)
    grp = lax.broadcasted_iota(jnp.int32, (CHUNK, SGU_WIDTH), 1) // SGU_GROUP_DIM
    r_i = lax.broadcasted_iota(jnp.int32, (CHUNK, CHUNK), 0)
    c_i = lax.broadcasted_iota(jnp.int32, (CHUNK, CHUNK), 1)
    tri, tri_t = r_i >= c_i, r_i <= c_i
    mixed = bias
    for g in range(SGU_GROUPS):
        wt = jnp.where(tri, ws_ref[g], 0.0).astype(BF16)
        mixed = mixed + jnp.where(grp == g, _dot(wt, vn_bf, _NN), 0.0)
    sb = _sigmoid(zb)
    return ug, dug, dvg, rstd, xh, vn_bf, grp, tri, tri_t, mixed, sb


def _sgu_fwd(h_b, lg, lb, w_s, bias_full):
    def body(u_ref, v_ref, zb_ref, lg_ref, lb_ref, ws_ref, bias_ref, yb_ref):
        zb = zb_ref[...]
        ug, _, _, _, _, _, _, _, _, mixed, sb = _sgu_math(u_ref[...], v_ref[...], zb, lg_ref[...], lb_ref[...],
                                                       ws_ref, bias_ref[...])
        yb_ref[...] = (ug * mixed) * (zb * sb)

    blk = lambda c: pl.BlockSpec((CHUNK, SGU_WIDTH), lambda i, c=c: (i, c))
    full2 = lambda shape: pl.BlockSpec(shape, lambda i: (0, 0))
    return pl.pallas_call(
        body, name="sgu_fwd", grid=(SEQ // CHUNK,),
        in_specs=[blk(0), blk(1), blk(2), full2((1, SGU_WIDTH)), full2((1, SGU_WIDTH)),
                  pl.BlockSpec((SGU_GROUPS, CHUNK, CHUNK), lambda i: (0, 0, 0)), full2((CHUNK, SGU_WIDTH))],
        out_specs=pl.BlockSpec((CHUNK, SGU_WIDTH), lambda i: (i, 0)),
        out_shape=jax.ShapeDtypeStruct((SEQ, SGU_WIDTH), F32),
        compiler_params=pltpu.CompilerParams(dimension_semantics=("arbitrary",)),
    )(h_b, h_b, h_b, lg, lb, w_s, bias_full)


def _sgu_bwd(h_b, d_yb, lg, lb, w_s, w_st, bias_full, parts):
    nsteps = SEQ // CHUNK
    npart = len(parts)

    def body(u_ref, v_ref, zb_ref, dyb_ref, lg_ref, lb_ref, ws_ref, wst_ref, bias_ref, *rest):
        part_refs, rest = rest[:npart], rest[npart:]
        dhb_ref, dws_ref, dbs_ref, dlg_ref, dlb_ref, dbb_ref = rest[:6]
        land_refs, (dbias_acc, send_sems, recv_sems, local_sems) = rest[6:6 + npart], rest[6 + npart:]
        step = pl.program_id(0)
        exchange = _exchange_parts(part_refs, land_refs, send_sems, recv_sems, local_sems)
        _exchange_start(step == 0, exchange)

        @pl.when(step == 0)
        def _():
            dbb_ref[...] = jnp.zeros_like(dbb_ref)
            dws_ref[...] = jnp.zeros_like(dws_ref)
            dlg_ref[...] = jnp.zeros_like(dlg_ref)
            dlb_ref[...] = jnp.zeros_like(dlb_ref)
            dbias_acc[...] = jnp.zeros_like(dbias_acc)

        zb = zb_ref[...]
        lg = lg_ref[...]
        ug, dug, dvg, rstd, xh, vn_bf, grp, tri, tri_t, mixed, sb = _sgu_math(
            u_ref[...], v_ref[...], zb, lg, lb_ref[...], ws_ref, bias_ref[...])
        dyb = dyb_ref[...]
        dsgu = dyb * (zb * sb)
        dzb = dyb * (ug * mixed) * (sb * (1.0 + zb * (1.0 - sb)))
        du = dsgu * mixed * dug
        dmixed = dsgu * ug
        dbias_acc[...] += dmixed
        dvn = jnp.zeros((CHUNK, SGU_WIDTH), F32)
        for g in range(SGU_GROUPS):
            dm_g = jnp.where(grp == g, dmixed, 0.0).astype(BF16)
            wtt = jnp.where(tri_t, wst_ref[g], 0.0).astype(BF16)
            dvn = dvn + _dot(wtt, dm_g, _NN)
            dws_ref[g] += jnp.where(tri, _dot(dm_g, vn_bf, _NT), 0.0)
        dlg_ref[...] += jnp.sum(dvn * xh, axis=0, keepdims=True)
        dlb_ref[...] += jnp.sum(dvn, axis=0, keepdims=True)
        dxh = dvn * lg
        dvgel = rstd * (dxh - jnp.mean(dxh, axis=1, keepdims=True) - xh * jnp.mean(dxh * xh, axis=1, keepdims=True))
        _store_grad(dhb_ref, dbb_ref, 0, du)
        _store_grad(dhb_ref, dbb_ref, SGU_WIDTH, dvgel * dvg)
        _store_grad(dhb_ref, dbb_ref, 2 * SGU_WIDTH, dzb)

        @pl.when(step == nsteps - 1)
        def _():
            acc = dbias_acc[...]
            lane = lax.broadcasted_iota(jnp.int32, (CHUNK, LANES), 1)
            out = jnp.zeros((CHUNK, LANES), F32)
            for g in range(SGU_GROUPS):
                sg = jnp.sum(jnp.where(grp == g, acc, 0.0), axis=1, keepdims=True)
                out = jnp.where(lane == g, sg, out)
            dbs_ref[...] = out

        _exchange_finish(step == nsteps - 1, exchange)

    blk = lambda c: pl.BlockSpec((CHUNK, SGU_WIDTH), lambda i, c=c: (i, c))
    full2 = lambda shape: pl.BlockSpec(shape, lambda i: (0, 0))
    full3 = pl.BlockSpec((SGU_GROUPS, CHUNK, CHUNK), lambda i: (0, 0, 0))
    hbm = pl.BlockSpec(memory_space=pl.ANY)
    res = pl.pallas_call(
        body, name="sgu_bwd", grid=(nsteps,),
        in_specs=[blk(0), blk(1), blk(2), pl.BlockSpec((CHUNK, SGU_WIDTH), lambda i: (i, 0)),
                  full2((1, SGU_WIDTH)), full2((1, SGU_WIDTH)), full3, full3, full2((CHUNK, SGU_WIDTH))] + [hbm] * npart,
        out_specs=[pl.BlockSpec((CHUNK, SEG_B), lambda i: (i, 0)), full3, full2((CHUNK, LANES)),
                   full2((1, SGU_WIDTH)), full2((1, SGU_WIDTH)), full2((1, SEG_B))] + [hbm] * npart,
        out_shape=[jax.ShapeDtypeStruct((SEQ, SEG_B), BF16),
                   jax.ShapeDtypeStruct((SGU_GROUPS, CHUNK, CHUNK), F32),
                   jax.ShapeDtypeStruct((CHUNK, LANES), F32),
                   jax.ShapeDtypeStruct((1, SGU_WIDTH), F32), jax.ShapeDtypeStruct((1, SGU_WIDTH), F32),
                   jax.ShapeDtypeStruct((1, SEG_B), F32)] + [jax.ShapeDtypeStruct(p.shape, p.dtype) for p in parts],
        scratch_shapes=[pltpu.VMEM((CHUNK, SGU_WIDTH), F32)] + _exchange_sems(npart),
        compiler_params=pltpu.CompilerParams(dimension_semantics=("arbitrary",)),
    )(h_b, h_b, h_b, d_yb, lg, lb, w_s, w_st, bias_full, *parts)
    return res[:6], res[6:]


def _merge(x, o, h_a, y_b, target, w_oa, w_ob, w_out, ln_g, ln_b):
    tm = 256
    nsteps = SEQ // tm

    def body(x_ref, o_ref, ga_ref, gb_ref, za_ref, yb_ref, tgt_ref, woa_ref, wob_ref, wout_ref, lng_ref, lnb_ref,
             loss_ref, dxr_ref, dha_ref, do_ref, dyb_ref, poa_ref, pob_ref, pout_ref, dlng_ref, dlnb_ref, dba_ref,
             dwoa_ref, dwob_ref, dwout_ref):
        step = pl.program_id(0)

        @pl.when(step == 0)
        def _():
            for r in (loss_ref, dwoa_ref, dwob_ref, dwout_ref, dlng_ref, dlnb_ref, dba_ref):
                r[...] = jnp.zeros_like(r)

        o = o_ref[...]
        za = za_ref[...]
        sa = _sigmoid(za)
        ya_bf = (o * (za * sa)).astype(BF16)
        yb_bf = yb_ref[...].astype(BF16)
        woa, wob, wout = woa_ref[...], wob_ref[...], wout_ref[...]
        pa = _dot(ya_bf, woa, _NN)
        pb = _dot(yb_bf, wob, _NN)
        sga = _sigmoid(ga_ref[...])
        sgb = _sigmoid(gb_ref[...])
        merged_bf = (sga * pa + sgb * pb).astype(BF16)
        r = DN_ALPHA * x_ref[...] + _dot(merged_bf, wout, _NN)
        mu = jnp.mean(r, axis=1, keepdims=True)
        rc = r - mu
        rstd = lax.rsqrt(jnp.mean(rc * rc, axis=1, keepdims=True) + LN_EPS)
        xh = rc * rstd
        lng = lng_ref[...]
        y = xh * lng + lnb_ref[...]
        e = y - tgt_ref[...]
        loss_ref[...] += 0.5 * jnp.sum(jnp.sum(e * e, axis=1, keepdims=True) * (1.0 / D_MODEL), axis=0, keepdims=True)

        dy = e * (1.0 / D_MODEL)
        dlng_ref[...] += jnp.sum(dy * xh, axis=0, keepdims=True)
        dlnb_ref[...] += jnp.sum(dy, axis=0, keepdims=True)
        dxh = dy * lng
        dr = rstd * (dxh - jnp.mean(dxh, axis=1, keepdims=True) - xh * jnp.mean(dxh * xh, axis=1, keepdims=True))
        dxr_ref[...] = DN_ALPHA * dr
        dr_bf = dr.astype(BF16)
        dwout_ref[...] += _dot(merged_bf, dr_bf, _TN)
        dmerged = _dot(dr_bf, wout, _NT)
        dpa_bf = (dmerged * sga).astype(BF16)
        dpb_bf = (dmerged * sgb).astype(BF16)
        _store_grad(dha_ref, dba_ref, 0, dmerged * pa * (sga * (1.0 - sga)))
        _store_grad(dha_ref, dba_ref, D_MODEL, dmerged * pb * (sgb * (1.0 - sgb)))
        dwoa_ref[...] += _dot(ya_bf, dpa_bf, _TN)
        dwob_ref[...] += _dot(yb_bf, dpb_bf, _TN)
        dya = _dot(dpa_bf, woa, _NT)
        dyb_ref[...] = _dot(dpb_bf, wob, _NT)
        do_ref[...] = dya * (za * sa)
        _store_grad(dha_ref, dba_ref, 2 * D_MODEL, dya * o * (sa * (1.0 + za * (1.0 - sa))))

        @pl.when(step == nsteps - 1)
        def _():
            cols = D_MODEL // N_DEV
            for j in range(N_DEV):
                poa_ref[j] = dwoa_ref[:, cols * j:cols * (j + 1)].astype(BF16)
                pob_ref[j] = dwob_ref[:, cols * j:cols * (j + 1)].astype(BF16)
                pout_ref[j] = dwout_ref[cols * j:cols * (j + 1), :].astype(BF16)

    row = lambda w, c=0: pl.BlockSpec((tm, w), lambda i, c=c: (i, c))
    full = lambda shape: pl.BlockSpec(shape, lambda i: (0, 0))
    full3 = lambda shape: pl.BlockSpec(shape, lambda i: (0, 0, 0))
    return pl.pallas_call(
        body, name="merge", grid=(nsteps,),
        in_specs=[row(D_MODEL), row(MLA_WIDTH), row(D_MODEL, 0), row(D_MODEL, 1), row(MLA_WIDTH, 4), row(SGU_WIDTH),
                  row(D_MODEL), full((MLA_WIDTH, D_MODEL)), full((SGU_WIDTH, D_MODEL)), full((D_MODEL, D_MODEL)),
                  full((1, D_MODEL)), full((1, D_MODEL))],
        out_specs=[full((1, LANES)), row(D_MODEL), row(SEG_A), row(MLA_WIDTH), row(SGU_WIDTH),
                   full3((N_DEV, MLA_WIDTH, D_MODEL // N_DEV)), full3((N_DEV, SGU_WIDTH, D_MODEL // N_DEV)),
                   full3((N_DEV, D_MODEL // N_DEV, D_MODEL)), full((1, D_MODEL)), full((1, D_MODEL)), full((1, SEG_A))],
        out_shape=[jax.ShapeDtypeStruct((1, LANES), F32),
                   jax.ShapeDtypeStruct((SEQ, D_MODEL), F32), jax.ShapeDtypeStruct((SEQ, SEG_A), BF16),
                   jax.ShapeDtypeStruct((SEQ, MLA_WIDTH), F32), jax.ShapeDtypeStruct((SEQ, SGU_WIDTH), F32),
                   jax.ShapeDtypeStruct((N_DEV, MLA_WIDTH, D_MODEL // N_DEV), BF16),
                   jax.ShapeDtypeStruct((N_DEV, SGU_WIDTH, D_MODEL // N_DEV), BF16),
                   jax.ShapeDtypeStruct((N_DEV, D_MODEL // N_DEV, D_MODEL), BF16),
                   jax.ShapeDtypeStruct((1, D_MODEL), F32), jax.ShapeDtypeStruct((1, D_MODEL), F32),
                   jax.ShapeDtypeStruct((1, SEG_A), F32)],
        scratch_shapes=[pltpu.VMEM((MLA_WIDTH, D_MODEL), F32), pltpu.VMEM((SGU_WIDTH, D_MODEL), F32),
                        pltpu.VMEM((D_MODEL, D_MODEL), F32)],
        compiler_params=pltpu.CompilerParams(dimension_semantics=("arbitrary",), vmem_limit_bytes=VMEM_BIG),
    )(x, o, h_a, h_a, h_a, y_b, target, w_oa, w_ob, w_out, ln_g, ln_b)


def _mla_bwd(dq, dk, dv, h_c, gq, gkv, wq, wkn, wv, c_t, sa_t, sb_t, parts):
    tm = 256
    hw = MLA_HEADS * HEAD_PAD
    npart = len(parts)
    nsteps = SEQ // tm

    def body(dq_ref, dk_ref, dv_ref, cq_ref, ckv_ref, gq_ref, gkv_ref, wq_ref, wkn_ref, wv_ref, c_ref, sa_ref, sb_ref,
             *rest):
        part_refs, rest = rest[:npart], rest[npart:]
        dhc_ref, puq_ref, dwkn_ref, dwv_ref, dgq_ref, dgkv_ref, dbc_ref = rest[:7]
        land_refs, (pre_ref, dwq_ref, send_sems, recv_sems, local_sems) = rest[7:7 + npart], rest[7 + npart:]
        exchange = _exchange_parts(part_refs, land_refs, send_sems, recv_sems, local_sems)
        _exchange_start(pl.program_id(0) == 0, exchange)
        _exchange_finish(pl.program_id(0) == nsteps - 1, exchange)

        @pl.when(pl.program_id(0) == 0)
        def _():
            for r in (dwq_ref, dwkn_ref, dwv_ref, dgq_ref, dgkv_ref, dbc_ref):
                r[...] = jnp.zeros_like(r)

        c, sa, sb = c_ref[...], sa_ref[...], sb_ref[...]
        lane = lax.broadcasted_iota(jnp.int32, (tm, LANES), 1)
        rope_lanes = jnp.logical_and(lane >= ROPE_LO, lane < ROPE_HI)

        cq = cq_ref[...]
        gq = gq_ref[...]
        rq = lax.rsqrt(jnp.sum(cq * cq, axis=1, keepdims=True) * (1.0 / Q_LORA_RANK) + RMS_EPS)
        nq = cq * rq
        cqn_bf = (nq * gq).astype(BF16)
        for h in range(MLA_HEADS):
            sl = slice(HEAD_PAD * h, HEAD_PAD * (h + 1))
            pre_ref[:, sl] = _rope_t(dq_ref[:, sl] * ATTN_SCALE, c, sa, sb).astype(BF16)
        dqpre_bf = pre_ref[...]
        dcqn = _dot(dqpre_bf, wq_ref[...], _NT)
        dwq_ref[...] += _dot(cqn_bf, dqpre_bf, _TN)
        dgq_ref[...] += jnp.sum(dcqn * nq, axis=0, keepdims=True)
        dnq = dcqn * gq
        _store_grad(dhc_ref, dbc_ref, 0,
                    rq * (dnq - nq * (jnp.sum(dnq * nq, axis=1, keepdims=True) * (1.0 / Q_LORA_RANK))))

        ckv = ckv_ref[...]
        gkv = gkv_ref[...]
        rkv = lax.rsqrt(jnp.sum(ckv * ckv, axis=1, keepdims=True) * (1.0 / KV_LORA_RANK) + RMS_EPS)
        nkv = ckv * rkv
        ckvn_bf = (nkv * gkv).astype(BF16)
        dk = dk_ref[...]
        dk_bf = dk.astype(BF16)
        dv_bf = dv_ref[...].astype(BF16)
        dckvn = _dot(dk_bf, wkn_ref[...], _NT) + _dot(dv_bf, wv_ref[...], _NT)
        dwkn_ref[...] += _dot(ckvn_bf, dk_bf, _TN)
        dwv_ref[...] += _dot(ckvn_bf, dv_bf, _TN)
        dgkv_ref[...] += jnp.sum(dckvn * nkv, axis=0, keepdims=True)
        dnkv = dckvn * gkv
        _store_grad(dhc_ref, dbc_ref, CQ_PAD, rkv * (
            dnkv - nkv * (jnp.sum(dnkv * nkv, axis=1, keepdims=True) * (1.0 / KV_LORA_RANK))))
        dkpe = jnp.zeros((tm, LANES), F32)
        for h in range(MLA_HEADS):
            dkpe = dkpe + dk[:, HEAD_PAD * h:HEAD_PAD * (h + 1)]
        _store_grad(dhc_ref, dbc_ref, CQ_PAD + LANES, _rope_t(jnp.where(rope_lanes, dkpe, 0.0), c, sa, sb))

        @pl.when(pl.program_id(0) == SEQ // tm - 1)
        def _():
            rows = Q_LORA_RANK // N_DEV
            for j in range(N_DEV):
                for h in range(MLA_HEADS):
                    puq_ref[j, :, QK_HEAD_DIM * h:QK_HEAD_DIM * (h + 1)] = dwq_ref[
                        rows * j:rows * (j + 1), HEAD_PAD * h:HEAD_PAD * h + QK_HEAD_DIM].astype(BF16)

    full = lambda shape: pl.BlockSpec(shape, lambda i: (0, 0))
    row = lambda w, c=0: pl.BlockSpec((tm, w), lambda i, c=c: (i, c))
    hbm = pl.BlockSpec(memory_space=pl.ANY)
    res = pl.pallas_call(
        body, name="mla_bwd", grid=(nsteps,),
        in_specs=[row(hw), row(hw), row(hw), row(CQ_PAD, 0), row(LANES, CQ_PAD // LANES),
                  full((1, CQ_PAD)), full((1, KV_LORA_RANK)), full((CQ_PAD, hw)), full((KV_LORA_RANK, hw)),
                  full((KV_LORA_RANK, hw)), row(LANES), row(LANES), row(LANES)] + [hbm] * npart,
        out_specs=[row(SEG_C), pl.BlockSpec((N_DEV, Q_LORA_RANK // N_DEV, MLA_HEADS * QK_HEAD_DIM), lambda i: (0, 0, 0)),
                   full((KV_LORA_RANK, hw)), full((KV_LORA_RANK, hw)),
                   full((1, CQ_PAD)), full((1, KV_LORA_RANK)), full((1, SEG_C))] + [hbm] * npart,
        out_shape=[jax.ShapeDtypeStruct((SEQ, SEG_C), BF16),
                   jax.ShapeDtypeStruct((N_DEV, Q_LORA_RANK // N_DEV, MLA_HEADS * QK_HEAD_DIM), BF16),
                   jax.ShapeDtypeStruct((KV_LORA_RANK, hw), F32), jax.ShapeDtypeStruct((KV_LORA_RANK, hw), F32),
                   jax.ShapeDtypeStruct((1, CQ_PAD), F32), jax.ShapeDtypeStruct((1, KV_LORA_RANK), F32),
                   jax.ShapeDtypeStruct((1, SEG_C), F32)] + [jax.ShapeDtypeStruct(p.shape, p.dtype) for p in parts],
        scratch_shapes=[pltpu.VMEM((tm, hw), BF16), pltpu.VMEM((CQ_PAD, hw), F32)] + _exchange_sems(npart),
        compiler_params=pltpu.CompilerParams(dimension_semantics=("arbitrary",), vmem_limit_bytes=VMEM_MID),
    )(dq, dk, dv, h_c, h_c, gq, gkv, wq, wkn, wv, c_t, sa_t, sb_t, *parts)
    return res[:7], res[7:]


def _adamw_all(ws, gs, ms, vs):
    n = len(ws)
    c1 = 1.0 / (1.0 - ADAM_B1 ** ADAM_STEP)
    c2 = 1.0 / (1.0 - ADAM_B2 ** ADAM_STEP)

    def body(*refs):
        for idx in range(n):
            w, g, m, v = (refs[idx][...], refs[n + idx][...], refs[2 * n + idx][...], refs[3 * n + idx][...])
            m_new = ADAM_B1 * m + (1.0 - ADAM_B1) * g
            v_new = ADAM_B2 * v + (1.0 - ADAM_B2) * (g * g)
            delta = -ADAM_LR * ((m_new * c1) / (jnp.sqrt(v_new * c2) + ADAM_EPS) + ADAM_WD * w)
            refs[4 * n + idx][...] = delta
            refs[5 * n + idx][...] = m_new
            refs[6 * n + idx][...] = v_new

    shapes = [jax.ShapeDtypeStruct(w.shape, F32) for w in ws]
    outs = pl.pallas_call(
        body, name="adamw", out_shape=shapes * 3,
        compiler_params=pltpu.CompilerParams(vmem_limit_bytes=VMEM_BIG),
    )(*ws, *gs, *ms, *vs)
    return outs[:n], outs[n:2 * n], outs[2 * n:]


SHARD_W = IN_WIDTH // N_DEV

_PIECES = [(0, 384, 2, 0), (384, 512, 2, CQ_PAD), (512, 544, 2, CQ_PAD + LANES + ROPE_LO),
           (544, 1056, 0, 2 * D_MODEL), (1056, 1568, 1, 0), (1568, 2080, 1, SGU_WIDTH),
           (2080, 2592, 1, 2 * SGU_WIDTH), (2592, 3616, 0, 0), (3616, 4640, 0, D_MODEL)]


def _column_runs():
    runs = []
    for n0, n1, seg, d0 in _PIECES:
        for j in range(N_DEV):
            lo, hi = max(n0, j * SHARD_W), min(n1, (j + 1) * SHARD_W)
            if lo < hi:
                runs.append((j, lo - j * SHARD_W, hi - j * SHARD_W, seg, d0 + lo - n0))
    return runs


def _mesh_pos():
    return lax.axis_index("x"), lax.axis_index("y"), lax.axis_index("c")


def _remote(src, dst, send_sems, recv_sems, k, to):
    return pltpu.make_async_remote_copy(src_ref=src, dst_ref=dst, send_sem=send_sems.at[k], recv_sem=recv_sems.at[k],
                                        device_id=to, device_id_type=pl.DeviceIdType.MESH)


def _gather_exchange(gats, send_sems, recv_sems, meanwhile=None):
    x, y, c = _mesh_pos()
    me, sibling = (x, y, c), (x, y, 1 - c)
    chips = [(1 - x, y), (x, 1 - y), (1 - x, 1 - y)]

    def copy(a, k, blk, to):
        slab = gats[a].at[4 * blk[0] + 2 * blk[1] + blk[2]]
        return _remote(slab, slab, send_sems, recv_sems, 7 * a + k, to)

    arrays = range(len(gats))
    first = [copy(a, 1 + j, me, (*chip, c)) for j, chip in enumerate(chips) for a in arrays]
    first += [copy(a, 0, me, sibling) for a in arrays]
    for cp in first:
        cp.start()
    if meanwhile is not None:
        meanwhile()
    passed = []
    for j, chip in enumerate(chips):
        for a in arrays:
            copy(a, 1 + j, (*chip, c), me).wait_recv()
            fwd = copy(a, 4 + j, (*chip, c), sibling)
            fwd.start()
            passed.append(fwd)
    for a in arrays:
        copy(a, 0, sibling, me).wait_recv()
    for j, chip in enumerate(chips):
        for a in arrays:
            copy(a, 4 + j, (*chip, 1 - c), me).wait_recv()
    for cp in first + passed:
        cp.wait_send()


def _gather_behind(own, gats, send_sems, recv_sems, local_sems, step, mid, last):
    x, y, c = _mesh_pos()
    me, sibling = (x, y, c), (x, y, 1 - c)
    chips = [(1 - x, y), (x, 1 - y), (1 - x, 1 - y)]
    arrays = range(len(gats))

    def copy(a, k, blk, to, src=None):
        slab = gats[a].at[4 * blk[0] + 2 * blk[1] + blk[2]]
        return _remote(slab if src is None else src, slab, send_sems, recv_sems, 7 * a + k, to)

    first = [copy(a, 1 + j, me, (*chip, c), src=own[a]) for j, chip in enumerate(chips) for a in arrays]
    first += [copy(a, 0, me, sibling, src=own[a]) for a in arrays]
    local = [pltpu.make_async_copy(own[a], gats[a].at[4 * x + 2 * y + c], local_sems.at[a]) for a in arrays]
    passed = [copy(a, 4 + j, (*chip, c), sibling) for j, chip in enumerate(chips) for a in arrays]

    @pl.when(step == 0)
    def _():
        for cp in first + local:
            cp.start()

    @pl.when(step == mid)
    def _():
        for j, chip in enumerate(chips):
            for a in arrays:
                copy(a, 1 + j, (*chip, c), me).wait_recv()
        for cp in passed:
            cp.start()

    @pl.when(step == last)
    def _():
        for a in arrays:
            copy(a, 0, sibling, me).wait_recv()
        for j, chip in enumerate(chips):
            for a in arrays:
                copy(a, 4 + j, (*chip, 1 - c), me).wait_recv()
        for cp in first + passed:
            cp.wait_send()
        for cp in local:
            cp.wait()


def _gather_first(w_in, w_uq2, w_oa, w_ob, w_out, x2, pos_col, invf_lane):
    hw = MLA_HEADS * HEAD_PAD
    uq_rows = Q_LORA_RANK // N_DEV
    rows = 256

    def body(win_ref, wuq_ref, woa_ref, wob_ref, wout_ref, x_ref, pos_ref, invf_ref,
             wc_ref, wq_ref, winb_ref, oab_ref, obb_ref, outb_ref, xb_ref, xt_ref, c_ref, sa_ref, sb_ref,
             g_uq, blk0, send_sems, recv_sems):
        def local_work():
            for i in range(SEQ // rows):
                xi = x_ref[rows * i:rows * (i + 1), :]
                xb_ref[rows * i:rows * (i + 1), :] = xi.astype(BF16)
                xt_ref[:, rows * i:rows * (i + 1)] = xi.T.astype(BF16)
            ang = pos_ref[...].astype(F32) * invf_ref[...]
            cs, sn = jnp.cos(ang), jnp.sin(ang)
            lane = lax.broadcasted_iota(jnp.int32, ang.shape, 1)
            c_ref[...] = jnp.where(lane < ROPE_LO, 1.0, jnp.where(lane < ROPE_HI, cs, 0.0))
            sa_ref[...] = jnp.where(jnp.logical_and(lane >= ROPE_LO, lane < ROPE_MID), -sn, 0.0)
            sb_ref[...] = jnp.where(jnp.logical_and(lane >= ROPE_MID, lane < ROPE_HI), sn, 0.0)

        x, y, c = _mesh_pos()
        me = (x, y, c)
        winb_ref[...] = win_ref[0].astype(BF16)
        oab_ref[...] = woa_ref[0].astype(BF16)
        obb_ref[...] = wob_ref[0].astype(BF16)
        outb_ref[...] = wout_ref[0].astype(BF16)
        g_uq[4 * x + 2 * y + c] = wuq_ref[...].astype(BF16)

        chip0 = jnp.logical_and(x == 0, y == 0)
        south = c == 0
        half = D_MODEL // 2
        halves = [blk0.at[pl.ds(0, half)], blk0.at[pl.ds(half, half)]]

        def bcopy(k, to, part=None):
            ref = blk0 if part is None else halves[part]
            return _remote(ref, ref, send_sems, recv_sems, 7 + k, to)

        sends0 = [(0, (0, 0, 1), None), (1, (1, 0, 0), 0), (2, (0, 1, 0), 1), (3, (1, 0, 0), 1), (4, (0, 1, 0), 0)]

        @pl.when(jnp.logical_and(chip0, south))
        def _():
            blk0[...] = winb_ref[...]
            for k, to, part in sends0:
                bcopy(k, to, part).start()

        _gather_exchange([g_uq], send_sems, recv_sems, meanwhile=local_work)

        for (cx, cy), first_k, first_half, second_k in (((1, 0), 1, 0, 3), ((0, 1), 2, 1, 4)):
            @pl.when(jnp.logical_and(jnp.logical_and(x == cx, y == cy), south))
            def _(cx=cx, cy=cy, first_k=first_k, first_half=first_half, second_k=second_k):
                bcopy(first_k, me, first_half).wait_recv()
                onward = bcopy(5 + first_half, (1, 1, 0), first_half)
                onward.start()
                bcopy(second_k, me, 1 - first_half).wait_recv()
                north = bcopy(7, (cx, cy, 1))
                north.start()
                onward.wait_send()
                north.wait_send()

        @pl.when(jnp.logical_and(jnp.logical_and(x == 1, y == 1), south))
        def _():
            bcopy(5, me, 0).wait_recv()
            bcopy(6, me, 1).wait_recv()
            north = bcopy(7, (1, 1, 1))
            north.start()
            north.wait_send()

        @pl.when(jnp.logical_and(chip0, c == 1))
        def _():
            bcopy(0, me).wait_recv()

        @pl.when(jnp.logical_and(jnp.logical_not(chip0), c == 1))
        def _():
            bcopy(7, me).wait_recv()

        @pl.when(jnp.logical_and(chip0, south))
        def _():
            for k, to, part in sends0:
                bcopy(k, to, part).wait_send()

        for j, s0, s1, seg, d0 in _column_runs():
            if seg == 2:
                wc_ref[:, d0:d0 + (s1 - s0)] = blk0[:, s0:s1]
        zeros = lambda r, w: jnp.zeros((r, w), BF16)
        wc_ref[:, Q_LORA_RANK:CQ_PAD] = zeros(D_MODEL, CQ_PAD - Q_LORA_RANK)
        wc_ref[:, CQ_PAD + LANES:CQ_PAD + LANES + ROPE_LO] = zeros(D_MODEL, ROPE_LO)
        wc_ref[:, CQ_PAD + LANES + ROPE_HI:SEG_C] = zeros(D_MODEL, LANES - ROPE_HI)
        wq_ref[Q_LORA_RANK:CQ_PAD, :] = zeros(CQ_PAD - Q_LORA_RANK, hw)
        for h in range(MLA_HEADS):
            wq_ref[0:Q_LORA_RANK, HEAD_PAD * h + QK_HEAD_DIM:HEAD_PAD * (h + 1)] = zeros(Q_LORA_RANK, HEAD_PAD - QK_HEAD_DIM)
        for j in range(N_DEV):
            for h in range(MLA_HEADS):
                wq_ref[uq_rows * j:uq_rows * (j + 1), HEAD_PAD * h:HEAD_PAD * h + QK_HEAD_DIM] = g_uq[
                    j, :, QK_HEAD_DIM * h:QK_HEAD_DIM * (h + 1)]

    vmem = pl.BlockSpec(memory_space=pltpu.VMEM)
    return pl.pallas_call(
        body, name="gather_first",
        out_shape=[jax.ShapeDtypeStruct((D_MODEL, SEG_C), BF16), jax.ShapeDtypeStruct((CQ_PAD, hw), BF16),
                   jax.ShapeDtypeStruct(w_in.shape[1:], BF16), jax.ShapeDtypeStruct(w_oa.shape[1:], BF16),
                   jax.ShapeDtypeStruct(w_ob.shape[1:], BF16), jax.ShapeDtypeStruct(w_out.shape[1:], BF16),
                   jax.ShapeDtypeStruct((SEQ, D_MODEL), BF16), jax.ShapeDtypeStruct((D_MODEL, SEQ), BF16)]
        + [jax.ShapeDtypeStruct((SEQ, LANES), F32)] * 3,
        in_specs=[vmem] * 8, out_specs=[vmem] * 11,
        scratch_shapes=[pltpu.VMEM((N_DEV, uq_rows, MLA_HEADS * QK_HEAD_DIM), BF16), pltpu.VMEM((D_MODEL, SHARD_W), BF16),
                        pltpu.SemaphoreType.DMA((15,)), pltpu.SemaphoreType.DMA((15,))],
        compiler_params=pltpu.CompilerParams(vmem_limit_bytes=VMEM_BIG),
    )(w_in, w_uq2, w_oa, w_ob, w_out, x2, pos_col, invf_lane)


def _assemble_in(g_in):
    def body(g_ref, wa_ref, wb_ref):
        segs = [wa_ref, wb_ref]
        for j, s0, s1, seg, d0 in _column_runs():
            if seg < 2:
                segs[seg][:, d0:d0 + (s1 - s0)] = g_ref[j, :, s0:s1]

    return pl.pallas_call(
        body, name="assemble_in",
        out_shape=[jax.ShapeDtypeStruct((D_MODEL, SEG_A), BF16), jax.ShapeDtypeStruct((D_MODEL, SEG_B), BF16)],
        compiler_params=pltpu.CompilerParams(vmem_limit_bytes=VMEM_MID),
    )(g_in)


def _assemble_out(g_oa, g_ob, g_out):
    cols = D_MODEL // N_DEV

    def body(goa_ref, gob_ref, gout_ref, oa_ref, ob_ref, out_ref):
        for j in range(N_DEV):
            oa_ref[:, cols * j:cols * (j + 1)] = goa_ref[j]
            ob_ref[:, cols * j:cols * (j + 1)] = gob_ref[j]
            out_ref[cols * j:cols * (j + 1), :] = gout_ref[j]

    return pl.pallas_call(
        body, name="assemble_out",
        out_shape=[jax.ShapeDtypeStruct((MLA_WIDTH, D_MODEL), BF16), jax.ShapeDtypeStruct((SGU_WIDTH, D_MODEL), BF16),
                   jax.ShapeDtypeStruct((D_MODEL, D_MODEL), BF16)],
    )(g_oa, g_ob, g_out)


C_NAT = 544


P_IN_SPLIT = 896


def _to_parts(dwa, dwb):
    def body(dwa_ref, dwb_ref, phi_ref, plo_ref):
        phi_ref[0, :, 0:C_NAT] = jnp.zeros((P_IN_SPLIT, C_NAT), BF16)
        plo_ref[0, :, 0:C_NAT] = jnp.zeros((D_MODEL - P_IN_SPLIT, C_NAT), BF16)
        segs = [dwa_ref, dwb_ref]
        for j, s0, s1, seg, d0 in _column_runs():
            if seg < 2:
                phi_ref[j, :, s0:s1] = segs[seg][0:P_IN_SPLIT, d0:d0 + (s1 - s0)]
                plo_ref[j, :, s0:s1] = segs[seg][P_IN_SPLIT:D_MODEL, d0:d0 + (s1 - s0)]

    return pl.pallas_call(
        body, name="to_parts",
        out_shape=[jax.ShapeDtypeStruct((N_DEV, P_IN_SPLIT, SHARD_W), BF16),
                   jax.ShapeDtypeStruct((N_DEV, D_MODEL - P_IN_SPLIT, SHARD_W), BF16)],
        compiler_params=pltpu.CompilerParams(vmem_limit_bytes=VMEM_MID))(dwa, dwb)


def _dx_tail(dhs, ws, dx_res, dwc, p_uq, p_rep):
    tm = SEQ // 4
    rep_rows = p_rep.shape[1]
    c_rows = D_MODEL // N_DEV
    spec = [((c_rows, C_NAT), BF16), (p_uq.shape[1:], BF16), ((rep_rows, LANES), F32)]
    n = len(spec)

    nseg = len(dhs)

    def body(*refs):
        dh_refs, w_refs = refs[:nseg], refs[nseg:2 * nseg]
        dxr_ref, dwc_ref, puq_ref, prep_ref, dx_ref, call_ref, guq_ref, repall_ref, pc_ref, c_all, rep_all = refs[
            2 * nseg:2 * nseg + 11]
        rest = refs[2 * nseg + 11:]
        ras, tbs, rbs = rest[0:n], rest[n:2 * n], rest[2 * n:3 * n]
        send_sems, recv_sems, gsend, grecv = rest[3 * n:]
        step = pl.program_id(0)
        x, y, c = _mesh_pos()
        me_idx = 4 * x + 2 * y + c
        me, sibling = (x, y, c), (x, y, 1 - c)
        others = [(1 - x, y), (x, 1 - y), (1 - x, 1 - y)]
        parts = [pc_ref, puq_ref, prep_ref]
        gats = [rep_all, c_all]

        def stage1(chip, a):
            return _remote(parts[a].at[2 * chip + (1 - c)], ras[a].at[chip], send_sems, recv_sems, 7 * a + chip, sibling)

        def stage2(k, a):
            cx, cy = others[k]
            return _remote(tbs[a].at[k], rbs[a].at[k], send_sems, recv_sems, 7 * a + 4 + k, (cx, cy, c))

        def gcopy(a, k, blk, to):
            slab = gats[a].at[4 * blk[0] + 2 * blk[1] + blk[2]]
            return _remote(slab, slab, gsend, grecv, 7 * a + k, to)

        def chip_sum(a, chip):
            return parts[a][2 * chip + c].astype(F32) + ras[a][chip].astype(F32)

        @pl.when(step == 0)
        def _():
            for j, s0, s1, seg, d0 in _column_runs():
                if seg == 2:
                    for r in range(N_DEV):
                        pc_ref[r, :, s0:s1] = dwc_ref[c_rows * r:c_rows * (r + 1), d0:d0 + (s1 - s0)]
            for chip in range(4):
                for a in range(n):
                    stage1(chip, a).start()

        @pl.when(step == 1)
        def _():
            for chip in range(4):
                for a in range(n):
                    stage1(chip, a).wait_recv()
            for k, (cx, cy) in enumerate(others):
                for a in range(n):
                    tbs[a][k] = chip_sum(a, 2 * cx + cy).astype(spec[a][1])
                    stage2(k, a).start()

        @pl.when(step == 2)
        def _():
            for k in range(3):
                for a in range(n):
                    stage2(k, a).wait_recv()
            sums = []
            for a in range(n):
                acc = chip_sum(a, 2 * x + y)
                for k in range(3):
                    acc = acc + rbs[a][k].astype(F32)
                sums.append(acc)
            c_all[me_idx] = sums[0].astype(BF16)
            guq_ref[...] = sums[1]
            rep_all[me_idx] = sums[2]
            for a in range(2):
                for j, chip in enumerate(others):
                    gcopy(a, 1 + j, me, (*chip, c)).start()
                gcopy(a, 0, me, sibling).start()

        @pl.when(step == 3)
        def _():
            for j, chip in enumerate(others):
                for a in range(2):
                    gcopy(a, 1 + j, (*chip, c), me).wait_recv()
                    gcopy(a, 4 + j, (*chip, c), sibling).start()
            for a in range(2):
                gcopy(a, 0, sibling, me).wait_recv()
                for j, chip in enumerate(others):
                    gcopy(a, 4 + j, (*chip, 1 - c), me).wait_recv()
            for a in range(2):
                gcopy(a, 0, me, sibling).wait_send()
                for j, chip in enumerate(others):
                    gcopy(a, 1 + j, me, (*chip, c)).wait_send()
                    gcopy(a, 4 + j, (*chip, c), sibling).wait_send()
            for a in range(n):
                for chip in range(4):
                    stage1(chip, a).wait_send()
                for k in range(3):
                    stage2(k, a).wait_send()
            call_ref[...] = c_all[...]
            repall_ref[...] = rep_all[...]

        acc = dxr_ref[...]
        for dh_ref, w_ref in zip(dh_refs, w_refs):
            acc = acc + _dot(dh_ref[...], w_ref[...], _NT)
        dx_ref[...] = acc

    row = lambda w: pl.BlockSpec((tm, w), lambda i: (i, 0))
    full = lambda shape: pl.BlockSpec(shape, lambda i: (0,) * len(shape))
    scratch = [pltpu.VMEM((N_DEV, c_rows, C_NAT), BF16), pltpu.VMEM((N_DEV, c_rows, C_NAT), BF16),
               pltpu.VMEM((N_DEV, rep_rows, LANES), F32)]
    for lead in (4, 3, 3):
        scratch += [pltpu.VMEM((lead,) + tuple(shape), dt) for shape, dt in spec]
    scratch += [pltpu.SemaphoreType.DMA((7 * n,)), pltpu.SemaphoreType.DMA((7 * n,)),
                pltpu.SemaphoreType.DMA((14,)), pltpu.SemaphoreType.DMA((14,))]
    return pl.pallas_call(
        body, name="dx_tail", grid=(SEQ // tm,),
        in_specs=[row(dh.shape[1]) for dh in dhs] + [full(w.shape) for w in ws]
        + [row(D_MODEL), full(dwc.shape), full(p_uq.shape), full(p_rep.shape)],
        out_specs=[row(D_MODEL), full((N_DEV, c_rows, C_NAT)), full(p_uq.shape[1:]), full((N_DEV, rep_rows, LANES))],
        out_shape=[jax.ShapeDtypeStruct((SEQ, D_MODEL), F32), jax.ShapeDtypeStruct((N_DEV, c_rows, C_NAT), BF16),
                   jax.ShapeDtypeStruct(p_uq.shape[1:], F32), jax.ShapeDtypeStruct((N_DEV, rep_rows, LANES), F32)],
        scratch_shapes=scratch,
        compiler_params=pltpu.CompilerParams(dimension_semantics=("arbitrary",), vmem_limit_bytes=VMEM_BIG),
    )(*dhs, *ws, dx_res, dwc, p_uq, p_rep)


def _sum_landed(landed, c_all):
    c_rows = D_MODEL // N_DEV

    def body(rhi_ref, rlo_ref, roa_ref, rob_ref, rout_ref, call_ref, gin_ref, goa_ref, gob_ref, gout_ref):
        def total(ref, sl):
            acc = ref[0, sl, :].astype(F32)
            for s in range(1, N_DEV):
                acc = acc + ref[s, sl, :].astype(F32)
            return acc

        x, y, c = _mesh_pos()
        dev0 = jnp.where(4 * x + 2 * y + c == 0, 1.0, 0.0)
        for j in range(N_DEV):
            sl = slice(c_rows * j, c_rows * (j + 1))
            below = c_rows * j < P_IN_SPLIT
            tot = total(rhi_ref, sl) if below else total(rlo_ref, slice(c_rows * j - P_IN_SPLIT, c_rows * (j + 1) - P_IN_SPLIT))
            gin_ref[0, sl, C_NAT:SHARD_W] = tot[:, C_NAT:SHARD_W]
            gin_ref[0, sl, 0:C_NAT] = tot[:, 0:C_NAT] + dev0 * call_ref[j].astype(F32)
        goa_ref[0] = total(roa_ref, slice(None))
        gob_ref[0] = total(rob_ref, slice(None))
        gout_ref[0] = total(rout_ref, slice(None))

    return pl.pallas_call(
        body, name="sum_landed",
        out_shape=[jax.ShapeDtypeStruct((1, D_MODEL, SHARD_W), F32)]
        + [jax.ShapeDtypeStruct((1,) + r.shape[1:], F32) for r in landed[2:]],
        compiler_params=pltpu.CompilerParams(vmem_limit_bytes=VMEM_MID),
    )(*landed, c_all)


_O_CQ, _O_CKV, _O_KPE, _O_ZA, _O_U, _O_V, _O_ZB, _O_GA, _O_GB = 0, 384, 512, 544, 1056, 1568, 2080, 2592, 3616


def _to_segments(w):
    z = lambda n: jnp.zeros(w.shape[:-1] + (n,), w.dtype)
    seg_a = jnp.concatenate([w[..., _O_GA:_O_GB], w[..., _O_GB:IN_WIDTH], w[..., _O_ZA:_O_U]], axis=-1)
    seg_b = jnp.concatenate([w[..., _O_U:_O_V], w[..., _O_V:_O_ZB], w[..., _O_ZB:_O_GA]], axis=-1)
    seg_c = jnp.concatenate([w[..., _O_CQ:_O_CKV], z(CQ_PAD - Q_LORA_RANK), w[..., _O_CKV:_O_KPE],
                             z(ROPE_LO), w[..., _O_KPE:_O_ZA], z(LANES - ROPE_HI)], axis=-1)
    return seg_a, seg_b, seg_c


def _from_segments(seg_a, seg_b, seg_c):
    kpe0 = CQ_PAD + LANES + ROPE_LO
    return jnp.concatenate([
        seg_c[..., 0:Q_LORA_RANK], seg_c[..., CQ_PAD:CQ_PAD + LANES], seg_c[..., kpe0:kpe0 + QK_ROPE_DIM],
        seg_a[..., 2 * D_MODEL:SEG_A], seg_b, seg_a[..., 0:2 * D_MODEL]], axis=-1)


def kernel(x, positions, w_in, b_in, g_q, w_uq, g_kv, w_ukv, w_oa, sgu_ln_g, sgu_ln_b, w_s, b_s, w_ob, w_out, ln_g, ln_b, loss_target, m_w_in, m_b_in, m_g_q, m_w_uq, m_g_kv, m_w_ukv, m_w_oa, m_sgu_ln_g, m_sgu_ln_b, m_w_s, m_b_s, m_w_ob, m_w_out, m_ln_g, m_ln_b, v_w_in, v_b_in, v_g_q, v_w_uq, v_g_kv, v_w_ukv, v_w_oa, v_sgu_ln_g, v_sgu_ln_b, v_w_s, v_b_s, v_w_ob, v_w_out, v_ln_g, v_ln_b):
    w_uq2 = w_uq[0].reshape(Q_LORA_RANK // N_DEV, MLA_HEADS * QK_HEAD_DIM)
    inv_freq = ROPE_THETA ** (-jnp.arange(0, QK_ROPE_DIM, 2, dtype=F32) / QK_ROPE_DIM)
    invf_lane = jnp.concatenate([jnp.zeros((ROPE_LO,), F32), inv_freq, inv_freq,
                                 jnp.zeros((LANES - ROPE_HI,), F32)]).reshape(1, LANES)
    first = _gather_first(w_in, w_uq2, w_oa, w_ob, w_out, x[0], positions.reshape(SEQ, 1), invf_lane)
    partials = _local_step(x[0], loss_target[0], first, b_in, g_q, g_kv, w_ukv, sgu_ln_g, sgu_ln_b, w_s, b_s, ln_g, ln_b)
    weights = dict(w_in=w_in, b_in=b_in, g_q=g_q, w_uq=w_uq, g_kv=g_kv, w_ukv=w_ukv, w_oa=w_oa, sgu_ln_g=sgu_ln_g,
                   sgu_ln_b=sgu_ln_b, w_s=w_s, b_s=b_s, w_ob=w_ob, w_out=w_out, ln_g=ln_g, ln_b=ln_b)
    moms = dict(w_in=m_w_in, b_in=m_b_in, g_q=m_g_q, w_uq=m_w_uq, g_kv=m_g_kv, w_ukv=m_w_ukv, w_oa=m_w_oa,
                sgu_ln_g=m_sgu_ln_g, sgu_ln_b=m_sgu_ln_b, w_s=m_w_s, b_s=m_b_s, w_ob=m_w_ob, w_out=m_w_out,
                ln_g=m_ln_g, ln_b=m_ln_b)
    vars_ = dict(w_in=v_w_in, b_in=v_b_in, g_q=v_g_q, w_uq=v_w_uq, g_kv=v_g_kv, w_ukv=v_w_ukv, w_oa=v_w_oa,
                 sgu_ln_g=v_sgu_ln_g, sgu_ln_b=v_sgu_ln_b, w_s=v_w_s, b_s=v_b_s, w_ob=v_w_ob, w_out=v_w_out,
                 ln_g=v_ln_g, ln_b=v_ln_b)
    return _reduce_and_update(partials, weights, moms, vars_)


def _local_step(x2, tgt, first, b_in, g_q, g_kv, w_ukv, sgu_ln_g, sgu_ln_b, w_s, b_s, ln_g, ln_b):
    wc, wq, win_b, oa_b, ob_b, out_b, x_bf, xt_bf, c_t, sa_t, sb_t = first
    ba, bb, bc = _to_segments(b_in)
    w_ukv_bf = w_ukv[0].astype(BF16)
    wkn = jnp.pad(w_ukv_bf[:, :, :QK_NOPE_DIM], ((0, 0), (0, 0), (0, HEAD_PAD - QK_NOPE_DIM))).reshape(KV_LORA_RANK, -1)
    wv = jnp.pad(w_ukv_bf[:, :, QK_NOPE_DIM:], ((0, 0), (0, 0), (0, HEAD_PAD - V_HEAD_DIM))).reshape(KV_LORA_RANK, -1)
    gq = jnp.pad(g_q, ((0, 0), (0, CQ_PAD - Q_LORA_RANK)))
    bias_full = jnp.repeat(b_s[0].T, SGU_GROUP_DIM, axis=1)
    w_s3 = w_s[0]
    w_st3 = jnp.swapaxes(w_s3, 1, 2)

    h_c = _mm(x_bf, wc, bias=bc, tm=512, tn=SEG_C, name="in_proj_c")
    q, k, kt, vx, vxt = _mla_prep(h_c, gq, g_kv, wq, wkn, wv, c_t, sa_t, sb_t)
    o, lse, (g_in,) = _attn_fwd(q, kt, vx, (win_b,))
    wa, wb = _assemble_in(g_in)
    h_a, (g_out,) = _mm(x_bf, wa, bias=ba, own=(out_b,), tm=512, tn=SEG_A // 2, name="in_proj_a")
    h_b, (g_oa, g_ob) = _mm(x_bf, wb, bias=bb, own=(oa_b, ob_b), tm=512, tn=SEG_B // 2, name="in_proj_b")
    y_b = _sgu_fwd(h_b, sgu_ln_g, sgu_ln_b, w_s3, bias_full)
    w_oa_f, w_ob_f, w_out_f = _assemble_out(g_oa, g_ob, g_out)

    (loss_row, dx_res, dh_a, d_o, d_yb, p_oa, p_ob, p_out, d_lng, d_lnb, d_ba) = _merge(
        x2, o, h_a, y_b, tgt, w_oa_f, w_ob_f, w_out_f, ln_g, ln_b)
    (dh_b, d_ws, d_bs_t, d_slg, d_slb, d_bb), (r_out,) = _sgu_bwd(h_b, d_yb, sgu_ln_g, sgu_ln_b, w_s3, w_st3, bias_full,
                                                                 (p_out,))
    d_wa, (r_oa,) = _mm(xt_bf, dh_a, out_dtype=BF16, parts=(p_oa,), tm=512, tn=512, name="dw_in_a")
    d_wb = _mm(xt_bf, dh_b, out_dtype=BF16, tm=512, tn=512, name="dw_in_b")
    p_hi, p_lo = _to_parts(d_wa, d_wb)
    dq, dk, dv, (r_hi,) = _attn_bwd(q, kt, k, vxt, d_o, o, lse, (p_hi,))
    (dh_c, p_uq, d_wkn, d_wv, d_gq, d_gkv, d_bc), (r_lo, r_ob) = _mla_bwd(
        dq, dk, dv, h_c, gq, g_kv, wq, wkn, wv, c_t, sa_t, sb_t, (p_lo, p_ob))
    landed = (r_hi, r_lo, r_oa, r_ob, r_out)
    d_wc = _mm(xt_bf, dh_c, out_dtype=BF16, tm=512, tn=SEG_C, name="dw_in_c")


    p_b_in = _from_segments(d_ba, d_bb, d_bc)
    p_w_ukv = jnp.concatenate([d_wkn.reshape(KV_LORA_RANK, MLA_HEADS, HEAD_PAD)[:, :, :QK_NOPE_DIM],
                               d_wv.reshape(KV_LORA_RANK, MLA_HEADS, HEAD_PAD)[:, :, :V_HEAD_DIM]], axis=-1)
    p_g_q = d_gq[:, :Q_LORA_RANK]
    p_b_s = d_bs_t[:, :SGU_GROUPS].T
    replicated = [p_b_in, p_g_q, d_gkv, p_w_ukv, d_slg, d_slb, d_ws, p_b_s, d_lng, d_lnb]
    return loss_row, ((dh_a, dh_b, dh_c), (wa, wb, wc), dx_res), landed, d_wc, p_uq, replicated


_NAMES = ["w_in", "b_in", "g_q", "w_uq", "g_kv", "w_ukv", "w_oa", "sgu_ln_g", "sgu_ln_b", "w_s", "b_s", "w_ob",
          "w_out", "ln_g", "ln_b"]
_REPLICATED = ["b_in", "g_q", "g_kv", "w_ukv", "sgu_ln_g", "sgu_ln_b", "w_s", "b_s", "ln_g", "ln_b"]


def _reduce_and_update(partials, weights, moms, vars_):
    loss_row, (dhs, ws, dx_res), landed, d_wc, p_uq, replicated = partials
    rep_flat = jnp.concatenate([a.reshape(-1) for a in replicated] + [loss_row[0, :1]])
    rep_flat = jnp.pad(rep_flat, (0, N_DEV * PACK_R_ROWS * LANES - rep_flat.size))
    dx_ab, c_all, g_uq, rep_all = _dx_tail(dhs[:2], ws[:2], dx_res, d_wc, p_uq,
                                           rep_flat.reshape(N_DEV, PACK_R_ROWS, LANES))
    dx = _mm(dhs[2], ws[2], tb=True, add=dx_ab, tm=512, tn=D_MODEL, name="dx_c")
    g_in, g_oa, g_ob, g_out = _sum_landed(landed, c_all)
    rep_sum = rep_all.reshape(-1)
    grads, pos = dict(w_in=g_in, w_uq=g_uq, w_oa=g_oa, w_ob=g_ob, w_out=g_out), 0
    for nm in _REPLICATED:
        grads[nm] = rep_sum[pos:pos + weights[nm].size]
        pos += weights[nm].size
    loss = rep_sum[pos]
    grads = {nm: grads[nm].reshape(weights[nm].shape) for nm in _NAMES}
    deltas, new_m, new_v = _adamw_all([weights[nm] for nm in _NAMES], [grads[nm] for nm in _NAMES],
                                      [moms[nm] for nm in _NAMES], [vars_[nm] for nm in _NAMES])
    return (loss, dx.reshape(1, SEQ, D_MODEL), *[grads[nm] for nm in _NAMES], *deltas, *new_m, *new_v)
```

```python
import math

import jax
import jax.numpy as jnp
from jax import lax
from jax.experimental import pallas as pl
from jax.experimental.pallas import tpu as pltpu

F32 = jnp.float32
BF16 = jnp.bfloat16

D_MODEL = 1024
SEQ = 2048
N_DEV = 8
MLA_HEADS = 8
Q_LORA_RANK = 384
KV_LORA_RANK = 128
QK_NOPE_DIM = 64
QK_ROPE_DIM = 32
V_HEAD_DIM = 64
QK_HEAD_DIM = QK_NOPE_DIM + QK_ROPE_DIM
MLA_WIDTH = MLA_HEADS * V_HEAD_DIM
ROPE_THETA = 10000.0
SGU_GROUPS = 8
SGU_GROUP_DIM = 64
SGU_WIDTH = SGU_GROUPS * SGU_GROUP_DIM
CHUNK = 128
RMS_EPS = 1e-6
LN_EPS = 1e-5
DN_ALPHA = 2.0 ** 0.25
IN_WIDTH = 4640
ATTN_SCALE = QK_HEAD_DIM ** -0.5

ADAM_LR = 0.001
ADAM_B1 = 0.9
ADAM_B2 = 0.999
ADAM_EPS = 1e-08
ADAM_WD = 0.01
ADAM_STEP = 10

LANES = 128
HEAD_PAD = 128
ROPE_LO = QK_NOPE_DIM
ROPE_MID = ROPE_LO + QK_ROPE_DIM // 2
ROPE_HI = ROPE_LO + QK_ROPE_DIM
CQ_PAD = 512

SEG_A = 2560
SEG_B = 1536
SEG_C = 768

PACK_R_ROWS = 272
VMEM_BIG = 56 * 1024 * 1024
VMEM_MID = 40 * 1024 * 1024


def _sigmoid(x):
    return 1.0 / (1.0 + jnp.exp(-x))


def _gelu_and_grad(x):
    c0 = math.sqrt(2.0 / math.pi)
    x2 = x * x
    t = jnp.tanh(c0 * (x + 0.044715 * x * x2))
    g = 0.5 * x * (1.0 + t)
    dg = 0.5 * (1.0 + t) + 0.5 * x * (1.0 - t * t) * (c0 * (1.0 + 3.0 * 0.044715 * x2))
    return g, dg


def _dot(a, b, dims):
    return lax.dot_general(a, b, (dims, ((), ())), preferred_element_type=F32)


_NN = ((1,), (0,))
_NT = ((1,), (1,))
_TN = ((0,), (0,))


def _store_grad(dh_ref, db_ref, col, val):
    cols = slice(col, col + val.shape[1])
    dh_ref[:, cols] = val.astype(BF16)
    db_ref[:, cols] += jnp.sum(val, axis=0, keepdims=True)


def _mm(a, b, *, tb=False, bias=None, add=None, out_dtype=F32, own=(), parts=(), tm, tn, name):
    m, k = a.shape
    n = b.shape[0] if tb else b.shape[1]
    assert m % tm == 0 and n % tn == 0 and not (own and parts)
    dims = _NT if tb else _NN
    nown = len(own) + len(parts)
    nm = m // tm
    nsteps = (n // tn) * nm

    def body(*refs):
        a_ref, b_ref = refs[0], refs[1]
        pos = 2
        r = _dot(a_ref[...], b_ref[...], dims)
        if bias is not None:
            r = r + refs[pos][...]; pos += 1
        if add is not None:
            r = r + refs[pos][...]; pos += 1
        own_refs = refs[pos:pos + nown]; pos += nown
        refs[pos][...] = r.astype(out_dtype)
        if nown:
            gat_refs = refs[pos + 1:pos + 1 + nown]
            send_sems, recv_sems, local_sems = refs[pos + 1 + nown:]
            step = pl.program_id(0) * nm + pl.program_id(1)
            if own:
                _gather_behind(own_refs, gat_refs, send_sems, recv_sems, local_sems, step, nsteps - 2, nsteps - 1)
            else:
                exchange = _exchange_parts(own_refs, gat_refs, send_sems, recv_sems, local_sems)
                _exchange_start(step == 0, exchange)
                _exchange_finish(step == nsteps - 1, exchange)

    b_spec = pl.BlockSpec((tn, k), lambda j, i: (j, 0)) if tb else pl.BlockSpec((k, tn), lambda j, i: (0, j))
    in_specs, args = [pl.BlockSpec((tm, k), lambda j, i: (i, 0)), b_spec], [a, b]
    if bias is not None:
        in_specs.append(pl.BlockSpec((1, tn), lambda j, i: (0, j))); args.append(bias)
    if add is not None:
        in_specs.append(pl.BlockSpec((tm, tn), lambda j, i: (i, j))); args.append(add)
    hbm = pl.BlockSpec(memory_space=pl.ANY)
    res = pl.pallas_call(
        body, name=name, grid=(n // tn, nm), in_specs=in_specs + [hbm] * nown,
        out_specs=[pl.BlockSpec((tm, tn), lambda j, i: (i, j))] + [hbm] * nown,
        out_shape=[jax.ShapeDtypeStruct((m, n), out_dtype)]
        + [jax.ShapeDtypeStruct((N_DEV,) + o.shape, o.dtype) for o in own]
        + [jax.ShapeDtypeStruct(p.shape, p.dtype) for p in parts],
        scratch_shapes=_exchange_sems(nown) if nown else [],
        compiler_params=pltpu.CompilerParams(dimension_semantics=("arbitrary", "arbitrary"), vmem_limit_bytes=VMEM_BIG),
    )(*args, *own, *parts)
    return (res[0], res[1:]) if nown else res[0]


def _rope(x, c, sa, sb):
    return x * c + pltpu.roll(x, LANES - 16, 1) * sa + pltpu.roll(x, 16, 1) * sb


def _rope_t(dy, c, sa, sb):
    return dy * c + pltpu.roll(dy * sa, 16, 1) + pltpu.roll(dy * sb, LANES - 16, 1)


def _mla_prep(h_c, gq, gkv, wq, wkn, wvx, c_t, sa_t, sb_t):
    tm = 256
    hw = MLA_HEADS * HEAD_PAD

    def body(cq_ref, ckv_ref, kpe_ref, gq_ref, gkv_ref, wq_ref, wkn_ref, wvx_ref, c_ref, sa_ref, sb_ref,
             q_ref, k_ref, kt_ref, vx_ref, vxt_ref):
        c, sa, sb = c_ref[...], sa_ref[...], sb_ref[...]
        cq = cq_ref[...]
        rq = lax.rsqrt(jnp.sum(cq * cq, axis=1, keepdims=True) * (1.0 / Q_LORA_RANK) + RMS_EPS)
        cqn = ((cq * rq) * gq_ref[...]).astype(BF16)
        qall = _dot(cqn, wq_ref[...], _NN)
        for h in range(MLA_HEADS):
            sl = slice(HEAD_PAD * h, HEAD_PAD * (h + 1))
            q_ref[:, sl] = (_rope(qall[:, sl], c, sa, sb) * ATTN_SCALE).astype(BF16)
        ckv = ckv_ref[...]
        rkv = lax.rsqrt(jnp.sum(ckv * ckv, axis=1, keepdims=True) * (1.0 / KV_LORA_RANK) + RMS_EPS)
        ckvn = ((ckv * rkv) * gkv_ref[...]).astype(BF16)
        knall = _dot(ckvn, wkn_ref[...], _NN)
        vall = _dot(ckvn, wvx_ref[...], _NN)
        kper = _rope(kpe_ref[...], c, sa, sb)
        ones_half = (lax.broadcasted_iota(jnp.int32, (tm, HEAD_PAD), 1) >= V_HEAD_DIM).astype(F32)
        for h in range(MLA_HEADS):
            sl = slice(HEAD_PAD * h, HEAD_PAD * (h + 1))
            kh = knall[:, sl] + kper
            vh = vall[:, sl] + ones_half
            k_ref[:, sl] = kh.astype(BF16)
            kt_ref[sl, :] = kh.T.astype(BF16)
            vx_ref[:, sl] = vh.astype(BF16)
            vxt_ref[sl, :] = vh.T.astype(BF16)

    full = lambda shape: pl.BlockSpec(shape, lambda i: (0, 0))
    tab = pl.BlockSpec((tm, LANES), lambda i: (i, 0))
    row = pl.BlockSpec((tm, hw), lambda i: (i, 0))
    col = pl.BlockSpec((hw, tm), lambda i: (0, i))
    return pl.pallas_call(
        body, name="mla_prep", grid=(SEQ // tm,),
        in_specs=[pl.BlockSpec((tm, CQ_PAD), lambda i: (i, 0)),
                  pl.BlockSpec((tm, LANES), lambda i: (i, CQ_PAD // LANES)),
                  pl.BlockSpec((tm, LANES), lambda i: (i, CQ_PAD // LANES + 1)),
                  full((1, CQ_PAD)), full((1, KV_LORA_RANK)),
                  full((CQ_PAD, hw)), full((KV_LORA_RANK, hw)), full((KV_LORA_RANK, hw)), tab, tab, tab],
        out_specs=[row, row, col, row, col],
        out_shape=[jax.ShapeDtypeStruct((SEQ, hw), BF16), jax.ShapeDtypeStruct((SEQ, hw), BF16),
                   jax.ShapeDtypeStruct((hw, SEQ), BF16), jax.ShapeDtypeStruct((SEQ, hw), BF16),
                   jax.ShapeDtypeStruct((hw, SEQ), BF16)],
        compiler_params=pltpu.CompilerParams(dimension_semantics=("arbitrary",), vmem_limit_bytes=VMEM_MID),
    )(h_c, h_c, h_c, gq, gkv, wq, wkn, wvx, c_t, sa_t, sb_t)


ATT_T = 512
ATT_STRIP = 64


def _attn_fwd(q, kt, vx, own):
    t, rs = ATT_T, ATT_STRIP
    nown = len(own)
    nq = SEQ // t
    nsteps = (MLA_HEADS // 2) * nq

    def body(q_ref, kt_ref, vx_ref, *rest):
        own_refs, (o_ref, l_ref), gat_refs = rest[:nown], rest[nown:nown + 2], rest[nown + 2:2 * nown + 2]
        s_scr, p_scr, m_scr, a_scr, acc_scr, send_sems, recv_sems, local_sems = rest[2 * nown + 2:]
        qi = pl.program_id(1)
        _gather_behind(own_refs, gat_refs, send_sems, recv_sems, local_sems, pl.program_id(0) * nq + qi,
                       nsteps - 2, nsteps - 1)
        lane = lax.broadcasted_iota(jnp.int32, (t, LANES), 1)
        m_scr[...] = jnp.full((2, t, LANES), -1e30, F32)
        acc_scr[...] = jnp.zeros((2, t, LANES), F32)

        def block(j, masked):
            off = pl.multiple_of(j * t, t)
            for a in range(2):
                sl = slice(HEAD_PAD * a, HEAD_PAD * (a + 1))
                s_scr[a] = _dot(q_ref[:, sl], kt_ref[sl, pl.ds(off, t)], _NN)
                for r in range(t // rs):
                    rows = slice(rs * r, rs * (r + 1))
                    s = s_scr[a, rows, :]
                    if masked:
                        rowi = lax.broadcasted_iota(jnp.int32, (rs, t), 0) + rs * r
                        coli = lax.broadcasted_iota(jnp.int32, (rs, t), 1)
                        s = jnp.where(coli <= rowi, s, -1e30)
                    m_old = m_scr[a, rows, :]
                    m_new = jnp.maximum(m_old, jnp.max(s, axis=1, keepdims=True))
                    p_scr[a, rows, :] = jnp.exp(s - m_new[:, :1]).astype(BF16)
                    a_scr[a, rows, :] = jnp.exp(m_old - m_new)
                    m_scr[a, rows, :] = m_new
                acc_scr[a] = acc_scr[a] * a_scr[a] + _dot(p_scr[a], vx_ref[pl.ds(off, t), sl], _NN)

        def step(j, carry):
            block(j, False)
            return carry
        lax.fori_loop(0, qi, step, 0)
        block(qi, True)
        res = []
        for a in range(2):
            acc = acc_scr[a]
            l = acc[:, V_HEAD_DIM:V_HEAD_DIM + 1]
            res.append((acc / l, m_scr[a] + jnp.log(l)))
        o_ref[...] = jnp.where(lane < V_HEAD_DIM, res[0][0], pltpu.roll(res[1][0], V_HEAD_DIM, 1))
        l_ref[...] = jnp.where(lane < V_HEAD_DIM, res[0][1], res[1][1])

    hbm = pl.BlockSpec(memory_space=pl.ANY)
    res = pl.pallas_call(
        body, name="attn_fwd", grid=(MLA_HEADS // 2, nq),
        in_specs=[pl.BlockSpec((t, 2 * HEAD_PAD), lambda p, i: (i, p)),
                  pl.BlockSpec((2 * HEAD_PAD, SEQ), lambda p, i: (p, 0)),
                  pl.BlockSpec((SEQ, 2 * HEAD_PAD), lambda p, i: (0, p))] + [hbm] * nown,
        out_specs=[pl.BlockSpec((t, LANES), lambda p, i: (i, p)),
                   pl.BlockSpec((t, LANES), lambda p, i: (i, p))] + [hbm] * nown,
        out_shape=[jax.ShapeDtypeStruct((SEQ, MLA_WIDTH), F32), jax.ShapeDtypeStruct((SEQ, MLA_WIDTH), F32)]
        + [jax.ShapeDtypeStruct((N_DEV,) + a.shape, a.dtype) for a in own],
        scratch_shapes=[pltpu.VMEM((2, t, t), F32), pltpu.VMEM((2, t, t), BF16), pltpu.VMEM((2, t, LANES), F32),
                        pltpu.VMEM((2, t, LANES), F32), pltpu.VMEM((2, t, LANES), F32)] + _exchange_sems(nown),
        compiler_params=pltpu.CompilerParams(dimension_semantics=("arbitrary", "arbitrary"), vmem_limit_bytes=VMEM_MID),
    )(q, kt, vx, *own)
    return res[0], res[1], res[2:]


def _exchange_parts(parts, lands, send_sems, recv_sems, local_sems):
    x, y, c = _mesh_pos()
    me = 4 * x + 2 * y + c
    peers = [(x, y, 1 - c), (1 - x, y, c), (x, 1 - y, c), (1 - x, 1 - y, c),
             (1 - x, y, 1 - c), (x, 1 - y, 1 - c), (1 - x, 1 - y, 1 - c)]
    remote, local = [], []
    for a, (part, land) in enumerate(zip(parts, lands)):
        for k, peer in enumerate(peers):
            t = 4 * peer[0] + 2 * peer[1] + peer[2]
            remote.append(_remote(part.at[t], land.at[me], send_sems, recv_sems, 7 * a + k, peer))
        local.append(pltpu.make_async_copy(part.at[me], land.at[me], local_sems.at[a]))
    return remote, local


def _exchange_start(first_step, exchange):
    remote, local = exchange

    @pl.when(first_step)
    def _():
        for cp in remote + local:
            cp.start()


def _exchange_finish(last_step, exchange):
    remote, local = exchange

    @pl.when(last_step)
    def _():
        for cp in remote:
            cp.wait_recv()
        for cp in remote:
            cp.wait_send()
        for cp in local:
            cp.wait()


def _exchange_sems(npart):
    return [pltpu.SemaphoreType.DMA((7 * npart,)), pltpu.SemaphoreType.DMA((7 * npart,)),
            pltpu.SemaphoreType.DMA((npart,))]


def _attn_bwd(q, kt, k, vxt, d_o, o, lse, parts):
    t, rs = ATT_T, ATT_STRIP
    nq = SEQ // t
    npart = len(parts)
    nsteps = MLA_HEADS // 2

    def body(q_ref, kt_ref, k_ref, vxt_ref, do_ref, o_ref, l_ref, *rest):
        part_refs, rest = rest[:npart], rest[npart:]
        dq_ref, dk_ref, dv_ref = rest[:3]
        land_refs, rest = rest[3:3 + npart], rest[3 + npart:]
        s_scr, dp_scr, p_scr, ds_scr, st_scr, send_sems, recv_sems, local_sems = rest
        exchange = _exchange_parts(part_refs, land_refs, send_sems, recv_sems, local_sems)
        _exchange_start(pl.program_id(0) == 0, exchange)
        dk_ref[...] = jnp.zeros_like(dk_ref)
        dv_ref[...] = jnp.zeros_like(dv_ref)
        lane = lax.broadcasted_iota(jnp.int32, (t, LANES), 1)

        def qtile(i, carry):
            ioff = pl.multiple_of(i * t, t)
            do_i = do_ref[pl.ds(ioff, t), :]
            o_i = o_ref[pl.ds(ioff, t), :]
            l_i = l_ref[pl.ds(ioff, t), :]
            for a in range(2):
                sl = slice(HEAD_PAD * a, HEAD_PAD * (a + 1))
                sel = (lane < V_HEAD_DIM) if a == 0 else (lane >= V_HEAD_DIM)
                doa = jnp.where(sel, do_i, 0.0)
                oa = o_i
                if a == 1:
                    doa = pltpu.roll(doa, V_HEAD_DIM, 1)
                    oa = pltpu.roll(o_i, V_HEAD_DIM, 1)
                st_scr[0] = jnp.broadcast_to(jnp.sum(doa * oa, axis=1, keepdims=True), (t, LANES))
                st_scr[1] = jnp.broadcast_to(l_i[:, V_HEAD_DIM * a:V_HEAD_DIM * a + 1], (t, LANES))
                doa_bf = doa.astype(BF16)
                qa = q_ref[pl.ds(ioff, t), sl]

                def block(j, masked, dq_acc, sl=sl, qa=qa, doa_bf=doa_bf):
                    joff = pl.multiple_of(j * t, t)
                    s_scr[...] = _dot(qa, kt_ref[sl, pl.ds(joff, t)], _NN)
                    dp_scr[...] = _dot(doa_bf, vxt_ref[sl, pl.ds(joff, t)], _NN)
                    for r in range(t // rs):
                        rows = slice(rs * r, rs * (r + 1))
                        p = jnp.exp(s_scr[rows, :] - st_scr[1, rows, :1])
                        if masked:
                            rowi = lax.broadcasted_iota(jnp.int32, (rs, t), 0) + rs * r
                            coli = lax.broadcasted_iota(jnp.int32, (rs, t), 1)
                            p = jnp.where(coli <= rowi, p, 0.0)
                        p_scr[rows, :] = p.astype(BF16)
                        ds_scr[rows, :] = (p * (dp_scr[rows, :] - st_scr[0, rows, :1])).astype(BF16)
                    dk_ref[pl.ds(joff, t), sl] += _dot(ds_scr[...], qa, _TN)
                    dv_ref[pl.ds(joff, t), sl] += _dot(p_scr[...], doa_bf, _TN)
                    return dq_acc + _dot(ds_scr[...], k_ref[pl.ds(joff, t), sl], _NN)

                dq_acc = lax.fori_loop(0, i, lambda j, acc: block(j, False, acc), jnp.zeros((t, HEAD_PAD), F32))
                dq_ref[pl.ds(ioff, t), sl] = block(i, True, dq_acc)
            return carry

        lax.fori_loop(0, nq, qtile, 0)
        _exchange_finish(pl.program_id(0) == nsteps - 1, exchange)

    hw = MLA_HEADS * HEAD_PAD
    wide = pl.BlockSpec((SEQ, 2 * HEAD_PAD), lambda p: (0, p))
    wide_t = pl.BlockSpec((2 * HEAD_PAD, SEQ), lambda p: (p, 0))
    narrow = pl.BlockSpec((SEQ, LANES), lambda p: (0, p))
    hbm = pl.BlockSpec(memory_space=pl.ANY)
    res = pl.pallas_call(
        body, name="attn_bwd", grid=(nsteps,),
        in_specs=[wide, wide_t, wide, wide_t, narrow, narrow, narrow] + [hbm] * npart,
        out_specs=[wide, wide, wide] + [hbm] * npart,
        out_shape=[jax.ShapeDtypeStruct((SEQ, hw), F32)] * 3 + [jax.ShapeDtypeStruct(p.shape, p.dtype) for p in parts],
        scratch_shapes=[pltpu.VMEM((t, t), F32), pltpu.VMEM((t, t), F32), pltpu.VMEM((t, t), BF16),
                        pltpu.VMEM((t, t), BF16), pltpu.VMEM((2, t, LANES), F32)] + _exchange_sems(npart),
        compiler_params=pltpu.CompilerParams(dimension_semantics=("arbitrary",), vmem_limit_bytes=VMEM_BIG),
    )(q, kt, k, vxt, d_o, o, lse, *parts)
    return res[0], res[1], res[2], res[3:]


def _sgu_math(u, v, zb, lg, lb, ws_ref, bias):
    ug, dug = _gelu_and_grad(u)
    vg, dvg = _gelu_and_grad(v)
    mu = jnp.mean(vg, axis=1, keepdims=True)
    xc = vg - mu
    rstd = lax.rsqrt(jnp.mean(xc * xc, axis=1, keepdims=True) + LN_EPS)
    xh = xc * rstd
    vn_bf = (xh * lg + lb).astype(BF16)
    grp = lax.broadcasted_iota(jnp.int32, (CHUNK, SGU_WIDTH), 1) // SGU_GROUP_DIM
    r_i = lax.broadcasted_iota(jnp.int32, (CHUNK, CHUNK), 0)
    c_i = lax.broadcasted_iota(jnp.int32, (CHUNK, CHUNK), 1)
    tri, tri_t = r_i >= c_i, r_i <= c_i
    mixed = bias
    for g in range(SGU_GROUPS):
        wt = jnp.where(tri, ws_ref[g], 0.0).astype(BF16)
        mixed = mixed + jnp.where(grp == g, _dot(wt, vn_bf, _NN), 0.0)
    sb = _sigmoid(zb)
    return ug, dug, dvg, rstd, xh, vn_bf, grp, tri, tri_t, mixed, sb


def _sgu_fwd(h_b, lg, lb, w_s, bias_full):
    def body(u_ref, v_ref, zb_ref, lg_ref, lb_ref, ws_ref, bias_ref, yb_ref):
        zb = zb_ref[...]
        ug, _, _, _, _, _, _, _, _, mixed, sb = _sgu_math(u_ref[...], v_ref[...], zb, lg_ref[...], lb_ref[...],
                                                       ws_ref, bias_ref[...])
        yb_ref[...] = (ug * mixed) * (zb * sb)

    blk = lambda c: pl.BlockSpec((CHUNK, SGU_WIDTH), lambda i, c=c: (i, c))
    full2 = lambda shape: pl.BlockSpec(shape, lambda i: (0, 0))
    return pl.pallas_call(
        body, name="sgu_fwd", grid=(SEQ // CHUNK,),
        in_specs=[blk(0), blk(1), blk(2), full2((1, SGU_WIDTH)), full2((1, SGU_WIDTH)),
                  pl.BlockSpec((SGU_GROUPS, CHUNK, CHUNK), lambda i: (0, 0, 0)), full2((CHUNK, SGU_WIDTH))],
        out_specs=pl.BlockSpec((CHUNK, SGU_WIDTH), lambda i: (i, 0)),
        out_shape=jax.ShapeDtypeStruct((SEQ, SGU_WIDTH), F32),
        compiler_params=pltpu.CompilerParams(dimension_semantics=("arbitrary",)),
    )(h_b, h_b, h_b, lg, lb, w_s, bias_full)


def _sgu_bwd(h_b, d_yb, lg, lb, w_s, w_st, bias_full, parts):
    nsteps = SEQ // CHUNK
    npart = len(parts)

    def body(u_ref, v_ref, zb_ref, dyb_ref, lg_ref, lb_ref, ws_ref, wst_ref, bias_ref, *rest):
        part_refs, rest = rest[:npart], rest[npart:]
        dhb_ref, dws_ref, dbs_ref, dlg_ref, dlb_ref, dbb_ref = rest[:6]
        land_refs, (dbias_acc, send_sems, recv_sems, local_sems) = rest[6:6 + npart], rest[6 + npart:]
        step = pl.program_id(0)
        exchange = _exchange_parts(part_refs, land_refs, send_sems, recv_sems, local_sems)
        _exchange_start(step == 0, exchange)

        @pl.when(step == 0)
        def _():
            dbb_ref[...] = jnp.zeros_like(dbb_ref)
            dws_ref[...] = jnp.zeros_like(dws_ref)
            dlg_ref[...] = jnp.zeros_like(dlg_ref)
            dlb_ref[...] = jnp.zeros_like(dlb_ref)
            dbias_acc[...] = jnp.zeros_like(dbias_acc)

        zb = zb_ref[...]
        lg = lg_ref[...]
        ug, dug, dvg, rstd, xh, vn_bf, grp, tri, tri_t, mixed, sb = _sgu_math(
            u_ref[...], v_ref[...], zb, lg, lb_ref[...], ws_ref, bias_ref[...])
        dyb = dyb_ref[...]
        dsgu = dyb * (zb * sb)
        dzb = dyb * (ug * mixed) * (sb * (1.0 + zb * (1.0 - sb)))
        du = dsgu * mixed * dug
        dmixed = dsgu * ug
        dbias_acc[...] += dmixed
        dvn = jnp.zeros((CHUNK, SGU_WIDTH), F32)
        for g in range(SGU_GROUPS):
            dm_g = jnp.where(grp == g, dmixed, 0.0).astype(BF16)
            wtt = jnp.where(tri_t, wst_ref[g], 0.0).astype(BF16)
            dvn = dvn + _dot(wtt, dm_g, _NN)
            dws_ref[g] += jnp.where(tri, _dot(dm_g, vn_bf, _NT), 0.0)
        dlg_ref[...] += jnp.sum(dvn * xh, axis=0, keepdims=True)
        dlb_ref[...] += jnp.sum(dvn, axis=0, keepdims=True)
        dxh = dvn * lg
        dvgel = rstd * (dxh - jnp.mean(dxh, axis=1, keepdims=True) - xh * jnp.mean(dxh * xh, axis=1, keepdims=True))
        _store_grad(dhb_ref, dbb_ref, 0, du)
        _store_grad(dhb_ref, dbb_ref, SGU_WIDTH, dvgel * dvg)
        _store_grad(dhb_ref, dbb_ref, 2 * SGU_WIDTH, dzb)

        @pl.when(step == nsteps - 1)
        def _():
            acc = dbias_acc[...]
            lane = lax.broadcasted_iota(jnp.int32, (CHUNK, LANES), 1)
            out = jnp.zeros((CHUNK, LANES), F32)
            for g in range(SGU_GROUPS):
                sg = jnp.sum(jnp.where(grp == g, acc, 0.0), axis=1, keepdims=True)
                out = jnp.where(lane == g, sg, out)
            dbs_ref[...] = out

        _exchange_finish(step == nsteps - 1, exchange)

    blk = lambda c: pl.BlockSpec((CHUNK, SGU_WIDTH), lambda i, c=c: (i, c))
    full2 = lambda shape: pl.BlockSpec(shape, lambda i: (0, 0))
    full3 = pl.BlockSpec((SGU_GROUPS, CHUNK, CHUNK), lambda i: (0, 0, 0))
    hbm = pl.BlockSpec(memory_space=pl.ANY)
    res = pl.pallas_call(
        body, name="sgu_bwd", grid=(nsteps,),
        in_specs=[blk(0), blk(1), blk(2), pl.BlockSpec((CHUNK, SGU_WIDTH), lambda i: (i, 0)),
                  full2((1, SGU_WIDTH)), full2((1, SGU_WIDTH)), full3, full3, full2((CHUNK, SGU_WIDTH))] + [hbm] * npart,
        out_specs=[pl.BlockSpec((CHUNK, SEG_B), lambda i: (i, 0)), full3, full2((CHUNK, LANES)),
                   full2((1, SGU_WIDTH)), full2((1, SGU_WIDTH)), full2((1, SEG_B))] + [hbm] * npart,
        out_shape=[jax.ShapeDtypeStruct((SEQ, SEG_B), BF16),
                   jax.ShapeDtypeStruct((SGU_GROUPS, CHUNK, CHUNK), F32),
                   jax.ShapeDtypeStruct((CHUNK, LANES), F32),
                   jax.ShapeDtypeStruct((1, SGU_WIDTH), F32), jax.ShapeDtypeStruct((1, SGU_WIDTH), F32),
                   jax.ShapeDtypeStruct((1, SEG_B), F32)] + [jax.ShapeDtypeStruct(p.shape, p.dtype) for p in parts],
        scratch_shapes=[pltpu.VMEM((CHUNK, SGU_WIDTH), F32)] + _exchange_sems(npart),
        compiler_params=pltpu.CompilerParams(dimension_semantics=("arbitrary",)),
    )(h_b, h_b, h_b, d_yb, lg, lb, w_s, w_st, bias_full, *parts)
    return res[:6], res[6:]


def _merge(x, o, h_a, y_b, target, w_oa, w_ob, w_out, ln_g, ln_b):
    tm = 256
    nsteps = SEQ // tm

    def body(x_ref, o_ref, ga_ref, gb_ref, za_ref, yb_ref, tgt_ref, woa_ref, wob_ref, wout_ref, lng_ref, lnb_ref,
             loss_ref, dxr_ref, dha_ref, do_ref, dyb_ref, poa_ref, pob_ref, pout_ref, dlng_ref, dlnb_ref, dba_ref,
             dwoa_ref, dwob_ref, dwout_ref):
        step = pl.program_id(0)

        @pl.when(step == 0)
        def _():
            for r in (loss_ref, dwoa_ref, dwob_ref, dwout_ref, dlng_ref, dlnb_ref, dba_ref):
                r[...] = jnp.zeros_like(r)

        o = o_ref[...]
        za = za_ref[...]
        sa = _sigmoid(za)
        ya_bf = (o * (za * sa)).astype(BF16)
        yb_bf = yb_ref[...].astype(BF16)
        woa, wob, wout = woa_ref[...], wob_ref[...], wout_ref[...]
        pa = _dot(ya_bf, woa, _NN)
        pb = _dot(yb_bf, wob, _NN)
        sga = _sigmoid(ga_ref[...])
        sgb = _sigmoid(gb_ref[...])
        merged_bf = (sga * pa + sgb * pb).astype(BF16)
        r = DN_ALPHA * x_ref[...] + _dot(merged_bf, wout, _NN)
        mu = jnp.mean(r, axis=1, keepdims=True)
        rc = r - mu
        rstd = lax.rsqrt(jnp.mean(rc * rc, axis=1, keepdims=True) + LN_EPS)
        xh = rc * rstd
        lng = lng_ref[...]
        y = xh * lng + lnb_ref[...]
        e = y - tgt_ref[...]
        loss_ref[...] += 0.5 * jnp.sum(jnp.sum(e * e, axis=1, keepdims=True) * (1.0 / D_MODEL), axis=0, keepdims=True)

        dy = e * (1.0 / D_MODEL)
        dlng_ref[...] += jnp.sum(dy * xh, axis=0, keepdims=True)
        dlnb_ref[...] += jnp.sum(dy, axis=0, keepdims=True)
        dxh = dy * lng
        dr = rstd * (dxh - jnp.mean(dxh, axis=1, keepdims=True) - xh * jnp.mean(dxh * xh, axis=1, keepdims=True))
        dxr_ref[...] = DN_ALPHA * dr
        dr_bf = dr.astype(BF16)
        dwout_ref[...] += _dot(merged_bf, dr_bf, _TN)
        dmerged = _dot(dr_bf, wout, _NT)
        dpa_bf = (dmerged * sga).astype(BF16)
        dpb_bf = (dmerged * sgb).astype(BF16)
        _store_grad(dha_ref, dba_ref, 0, dmerged * pa * (sga * (1.0 - sga)))
        _store_grad(dha_ref, dba_ref, D_MODEL, dmerged * pb * (sgb * (1.0 - sgb)))
        dwoa_ref[...] += _dot(ya_bf, dpa_bf, _TN)
        dwob_ref[...] += _dot(yb_bf, dpb_bf, _TN)
        dya = _dot(dpa_bf, woa, _NT)
        dyb_ref[...] = _dot(dpb_bf, wob, _NT)
        do_ref[...] = dya * (za * sa)
        _store_grad(dha_ref, dba_ref, 2 * D_MODEL, dya * o * (sa * (1.0 + za * (1.0 - sa))))

        @pl.when(step == nsteps - 1)
        def _():
            cols = D_MODEL // N_DEV
            for j in range(N_DEV):
                poa_ref[j] = dwoa_ref[:, cols * j:cols * (j + 1)].astype(BF16)
                pob_ref[j] = dwob_ref[:, cols * j:cols * (j + 1)].astype(BF16)
                pout_ref[j] = dwout_ref[cols * j:cols * (j + 1), :].astype(BF16)

    row = lambda w, c=0: pl.BlockSpec((tm, w), lambda i, c=c: (i, c))
    full = lambda shape: pl.BlockSpec(shape, lambda i: (0, 0))
    full3 = lambda shape: pl.BlockSpec(shape, lambda i: (0, 0, 0))
    return pl.pallas_call(
        body, name="merge", grid=(nsteps,),
        in_specs=[row(D_MODEL), row(MLA_WIDTH), row(D_MODEL, 0), row(D_MODEL, 1), row(MLA_WIDTH, 4), row(SGU_WIDTH),
                  row(D_MODEL), full((MLA_WIDTH, D_MODEL)), full((SGU_WIDTH, D_MODEL)), full((D_MODEL, D_MODEL)),
                  full((1, D_MODEL)), full((1, D_MODEL))],
        out_specs=[full((1, LANES)), row(D_MODEL), row(SEG_A), row(MLA_WIDTH), row(SGU_WIDTH),
                   full3((N_DEV, MLA_WIDTH, D_MODEL // N_DEV)), full3((N_DEV, SGU_WIDTH, D_MODEL // N_DEV)),
                   full3((N_DEV, D_MODEL // N_DEV, D_MODEL)), full((1, D_MODEL)), full((1, D_MODEL)), full((1, SEG_A))],
        out_shape=[jax.ShapeDtypeStruct((1, LANES), F32),
                   jax.ShapeDtypeStruct((SEQ, D_MODEL), F32), jax.ShapeDtypeStruct((SEQ, SEG_A), BF16),
                   jax.ShapeDtypeStruct((SEQ, MLA_WIDTH), F32), jax.ShapeDtypeStruct((SEQ, SGU_WIDTH), F32),
                   jax.ShapeDtypeStruct((N_DEV, MLA_WIDTH, D_MODEL // N_DEV), BF16),
                   jax.ShapeDtypeStruct((N_DEV, SGU_WIDTH, D_MODEL // N_DEV), BF16),
                   jax.ShapeDtypeStruct((N_DEV, D_MODEL // N_DEV, D_MODEL), BF16),
                   jax.ShapeDtypeStruct((1, D_MODEL), F32), jax.ShapeDtypeStruct((1, D_MODEL), F32),
                   jax.ShapeDtypeStruct((1, SEG_A), F32)],
        scratch_shapes=[pltpu.VMEM((MLA_WIDTH, D_MODEL), F32), pltpu.VMEM((SGU_WIDTH, D_MODEL), F32),
                        pltpu.VMEM((D_MODEL, D_MODEL), F32)],
        compiler_params=pltpu.CompilerParams(dimension_semantics=("arbitrary",), vmem_limit_bytes=VMEM_BIG),
    )(x, o, h_a, h_a, h_a, y_b, target, w_oa, w_ob, w_out, ln_g, ln_b)


def _mla_bwd(dq, dk, dv, h_c, gq, gkv, wq, wkn, wv, c_t, sa_t, sb_t, parts):
    tm = 256
    hw = MLA_HEADS * HEAD_PAD
    npart = len(parts)
    nsteps = SEQ // tm

    def body(dq_ref, dk_ref, dv_ref, cq_ref, ckv_ref, gq_ref, gkv_ref, wq_ref, wkn_ref, wv_ref, c_ref, sa_ref, sb_ref,
             *rest):
        part_refs, rest = rest[:npart], rest[npart:]
        dhc_ref, puq_ref, dwkn_ref, dwv_ref, dgq_ref, dgkv_ref, dbc_ref = rest[:7]
        land_refs, (pre_ref, dwq_ref, send_sems, recv_sems, local_sems) = rest[7:7 + npart], rest[7 + npart:]
        exchange = _exchange_parts(part_refs, land_refs, send_sems, recv_sems, local_sems)
        _exchange_start(pl.program_id(0) == 0, exchange)
        _exchange_finish(pl.program_id(0) == nsteps - 1, exchange)

        @pl.when(pl.program_id(0) == 0)
        def _():
            for r in (dwq_ref, dwkn_ref, dwv_ref, dgq_ref, dgkv_ref, dbc_ref):
                r[...] = jnp.zeros_like(r)

        c, sa, sb = c_ref[...], sa_ref[...], sb_ref[...]
        lane = lax.broadcasted_iota(jnp.int32, (tm, LANES), 1)
        rope_lanes = jnp.logical_and(lane >= ROPE_LO, lane < ROPE_HI)

        cq = cq_ref[...]
        gq = gq_ref[...]
        rq = lax.rsqrt(jnp.sum(cq * cq, axis=1, keepdims=True) * (1.0 / Q_LORA_RANK) + RMS_EPS)
        nq = cq * rq
        cqn_bf = (nq * gq).astype(BF16)
        for h in range(MLA_HEADS):
            sl = slice(HEAD_PAD * h, HEAD_PAD * (h + 1))
            pre_ref[:, sl] = _rope_t(dq_ref[:, sl] * ATTN_SCALE, c, sa, sb).astype(BF16)
        dqpre_bf = pre_ref[...]
        dcqn = _dot(dqpre_bf, wq_ref[...], _NT)
        dwq_ref[...] += _dot(cqn_bf, dqpre_bf, _TN)
        dgq_ref[...] += jnp.sum(dcqn * nq, axis=0, keepdims=True)
        dnq = dcqn * gq
        _store_grad(dhc_ref, dbc_ref, 0,
                    rq * (dnq - nq * (jnp.sum(dnq * nq, axis=1, keepdims=True) * (1.0 / Q_LORA_RANK))))

        ckv = ckv_ref[...]
        gkv = gkv_ref[...]
        rkv = lax.rsqrt(jnp.sum(ckv * ckv, axis=1, keepdims=True) * (1.0 / KV_LORA_RANK) + RMS_EPS)
        nkv = ckv * rkv
        ckvn_bf = (nkv * gkv).astype(BF16)
        dk = dk_ref[...]
        dk_bf = dk.astype(BF16)
        dv_bf = dv_ref[...].astype(BF16)
        dckvn = _dot(dk_bf, wkn_ref[...], _NT) + _dot(dv_bf, wv_ref[...], _NT)
        dwkn_ref[...] += _dot(ckvn_bf, dk_bf, _TN)
        dwv_ref[...] += _dot(ckvn_bf, dv_bf, _TN)
        dgkv_ref[...] += jnp.sum(dckvn * nkv, axis=0, keepdims=True)
        dnkv = dckvn * gkv
        _store_grad(dhc_ref, dbc_ref, CQ_PAD, rkv * (
            dnkv - nkv * (jnp.sum(dnkv * nkv, axis=1, keepdims=True) * (1.0 / KV_LORA_RANK))))
        dkpe = jnp.zeros((tm, LANES), F32)
        for h in range(MLA_HEADS):
            dkpe = dkpe + dk[:, HEAD_PAD * h:HEAD_PAD * (h + 1)]
        _store_grad(dhc_ref, dbc_ref, CQ_PAD + LANES, _rope_t(jnp.where(rope_lanes, dkpe, 0.0), c, sa, sb))

        @pl.when(pl.program_id(0) == SEQ // tm - 1)
        def _():
            rows = Q_LORA_RANK // N_DEV
            for j in range(N_DEV):
                for h in range(MLA_HEADS):
                    puq_ref[j, :, QK_HEAD_DIM * h:QK_HEAD_DIM * (h + 1)] = dwq_ref[
                        rows * j:rows * (j + 1), HEAD_PAD * h:HEAD_PAD * h + QK_HEAD_DIM].astype(BF16)

    full = lambda shape: pl.BlockSpec(shape, lambda i: (0, 0))
    row = lambda w, c=0: pl.BlockSpec((tm, w), lambda i, c=c: (i, c))
    hbm = pl.BlockSpec(memory_space=pl.ANY)
    res = pl.pallas_call(
        body, name="mla_bwd", grid=(nsteps,),
        in_specs=[row(hw), row(hw), row(hw), row(CQ_PAD, 0), row(LANES, CQ_PAD // LANES),
                  full((1, CQ_PAD)), full((1, KV_LORA_RANK)), full((CQ_PAD, hw)), full((KV_LORA_RANK, hw)),
                  full((KV_LORA_RANK, hw)), row(LANES), row(LANES), row(LANES)] + [hbm] * npart,
        out_specs=[row(SEG_C), pl.BlockSpec((N_DEV, Q_LORA_RANK // N_DEV, MLA_HEADS * QK_HEAD_DIM), lambda i: (0, 0, 0)),
                   full((KV_LORA_RANK, hw)), full((KV_LORA_RANK, hw)),
                   full((1, CQ_PAD)), full((1, KV_LORA_RANK)), full((1, SEG_C))] + [hbm] * npart,
        out_shape=[jax.ShapeDtypeStruct((SEQ, SEG_C), BF16),
                   jax.ShapeDtypeStruct((N_DEV, Q_LORA_RANK // N_DEV, MLA_HEADS * QK_HEAD_DIM), BF16),
                   jax.ShapeDtypeStruct((KV_LORA_RANK, hw), F32), jax.ShapeDtypeStruct((KV_LORA_RANK, hw), F32),
                   jax.ShapeDtypeStruct((1, CQ_PAD), F32), jax.ShapeDtypeStruct((1, KV_LORA_RANK), F32),
                   jax.ShapeDtypeStruct((1, SEG_C), F32)] + [jax.ShapeDtypeStruct(p.shape, p.dtype) for p in parts],
        scratch_shapes=[pltpu.VMEM((tm, hw), BF16), pltpu.VMEM((CQ_PAD, hw), F32)] + _exchange_sems(npart),
        compiler_params=pltpu.CompilerParams(dimension_semantics=("arbitrary",), vmem_limit_bytes=VMEM_MID),
    )(dq, dk, dv, h_c, h_c, gq, gkv, wq, wkn, wv, c_t, sa_t, sb_t, *parts)
    return res[:7], res[7:]


def _adamw_all(ws, gs, ms, vs):
    n = len(ws)
    nsteps = 4
    chunk = ws[0].shape[1] // nsteps
    c1 = 1.0 / (1.0 - ADAM_B1 ** ADAM_STEP)
    c2 = 1.0 / (1.0 - ADAM_B2 ** ADAM_STEP)

    def body(*refs):
        def update(idx):
            w, g, m, v = (refs[idx][...], refs[n + idx][...], refs[2 * n + idx][...], refs[3 * n + idx][...])
            m_new = ADAM_B1 * m + (1.0 - ADAM_B1) * g
            v_new = ADAM_B2 * v + (1.0 - ADAM_B2) * (g * g)
            delta = -ADAM_LR * ((m_new * c1) / (jnp.sqrt(v_new * c2) + ADAM_EPS) + ADAM_WD * w)
            refs[4 * n + idx][...] = delta
            refs[5 * n + idx][...] = m_new
            refs[6 * n + idx][...] = v_new

        update(0)

        @pl.when(pl.program_id(0) == 0)
        def _():
            for idx in range(1, n):
                update(idx)

    def spec(a, first):
        if first:
            return pl.BlockSpec((1, chunk, a.shape[2]), lambda i: (0, i, 0))
        return pl.BlockSpec(a.shape, lambda i, nd=a.ndim: (0,) * nd)

    specs = [spec(w, idx == 0) for idx, w in enumerate(ws)]
    shapes = [jax.ShapeDtypeStruct(w.shape, F32) for w in ws]
    outs = pl.pallas_call(
        body, name="adamw", grid=(nsteps,), in_specs=specs * 4, out_specs=specs * 3, out_shape=shapes * 3,
        compiler_params=pltpu.CompilerParams(dimension_semantics=("arbitrary",), vmem_limit_bytes=VMEM_BIG),
    )(*ws, *gs, *ms, *vs)
    return outs[:n], outs[n:2 * n], outs[2 * n:]


SHARD_W = IN_WIDTH // N_DEV

_PIECES = [(0, 384, 2, 0), (384, 512, 2, CQ_PAD), (512, 544, 2, CQ_PAD + LANES + ROPE_LO),
           (544, 1056, 0, 2 * D_MODEL), (1056, 1568, 1, 0), (1568, 2080, 1, SGU_WIDTH),
           (2080, 2592, 1, 2 * SGU_WIDTH), (2592, 3616, 0, 0), (3616, 4640, 0, D_MODEL)]


def _column_runs():
    runs = []
    for n0, n1, seg, d0 in _PIECES:
        for j in range(N_DEV):
            lo, hi = max(n0, j * SHARD_W), min(n1, (j + 1) * SHARD_W)
            if lo < hi:
                runs.append((j, lo - j * SHARD_W, hi - j * SHARD_W, seg, d0 + lo - n0))
    return runs


def _mesh_pos():
    return lax.axis_index("x"), lax.axis_index("y"), lax.axis_index("c")


def _remote(src, dst, send_sems, recv_sems, k, to):
    return pltpu.make_async_remote_copy(src_ref=src, dst_ref=dst, send_sem=send_sems.at[k], recv_sem=recv_sems.at[k],
                                        device_id=to, device_id_type=pl.DeviceIdType.MESH)


def _gather_exchange(gats, send_sems, recv_sems, meanwhile=None):
    x, y, c = _mesh_pos()
    me, sibling = (x, y, c), (x, y, 1 - c)
    chips = [(1 - x, y), (x, 1 - y), (1 - x, 1 - y)]

    def copy(a, k, blk, to):
        slab = gats[a].at[4 * blk[0] + 2 * blk[1] + blk[2]]
        return _remote(slab, slab, send_sems, recv_sems, 7 * a + k, to)

    arrays = range(len(gats))
    first = [copy(a, 1 + j, me, (*chip, c)) for j, chip in enumerate(chips) for a in arrays]
    first += [copy(a, 0, me, sibling) for a in arrays]
    for cp in first:
        cp.start()
    if meanwhile is not None:
        meanwhile()
    passed = []
    for j, chip in enumerate(chips):
        for a in arrays:
            copy(a, 1 + j, (*chip, c), me).wait_recv()
            fwd = copy(a, 4 + j, (*chip, c), sibling)
            fwd.start()
            passed.append(fwd)
    for a in arrays:
        copy(a, 0, sibling, me).wait_recv()
    for j, chip in enumerate(chips):
        for a in arrays:
            copy(a, 4 + j, (*chip, 1 - c), me).wait_recv()
    for cp in first + passed:
        cp.wait_send()


def _gather_behind(own, gats, send_sems, recv_sems, local_sems, step, mid, last):
    x, y, c = _mesh_pos()
    me, sibling = (x, y, c), (x, y, 1 - c)
    chips = [(1 - x, y), (x, 1 - y), (1 - x, 1 - y)]
    arrays = range(len(gats))

    def copy(a, k, blk, to, src=None):
        slab = gats[a].at[4 * blk[0] + 2 * blk[1] + blk[2]]
        return _remote(slab if src is None else src, slab, send_sems, recv_sems, 7 * a + k, to)

    first = [copy(a, 1 + j, me, (*chip, c), src=own[a]) for j, chip in enumerate(chips) for a in arrays]
    first += [copy(a, 0, me, sibling, src=own[a]) for a in arrays]
    local = [pltpu.make_async_copy(own[a], gats[a].at[4 * x + 2 * y + c], local_sems.at[a]) for a in arrays]
    passed = [copy(a, 4 + j, (*chip, c), sibling) for j, chip in enumerate(chips) for a in arrays]

    @pl.when(step == 0)
    def _():
        for cp in first + local:
            cp.start()

    @pl.when(step == mid)
    def _():
        for j, chip in enumerate(chips):
            for a in arrays:
                copy(a, 1 + j, (*chip, c), me).wait_recv()
        for cp in passed:
            cp.start()

    @pl.when(step == last)
    def _():
        for a in arrays:
            copy(a, 0, sibling, me).wait_recv()
        for j, chip in enumerate(chips):
            for a in arrays:
                copy(a, 4 + j, (*chip, 1 - c), me).wait_recv()
        for cp in first + passed:
            cp.wait_send()
        for cp in local:
            cp.wait()


def _gather_first(w_in, w_uq2, w_oa, w_ob, w_out, x2, pos_col, invf_lane):
    hw = MLA_HEADS * HEAD_PAD
    uq_rows = Q_LORA_RANK // N_DEV
    rows = 256

    def body(win_ref, wuq_ref, woa_ref, wob_ref, wout_ref, x_ref, pos_ref, invf_ref,
             wc_ref, wq_ref, winb_ref, oab_ref, obb_ref, outb_ref, xb_ref, xt_ref, c_ref, sa_ref, sb_ref,
             g_uq, blk0, send_sems, recv_sems):
        def local_work():
            for i in range(SEQ // rows):
                xi = x_ref[rows * i:rows * (i + 1), :]
                xb_ref[rows * i:rows * (i + 1), :] = xi.astype(BF16)
                xt_ref[:, rows * i:rows * (i + 1)] = xi.T.astype(BF16)
            ang = pos_ref[...].astype(F32) * invf_ref[...]
            cs, sn = jnp.cos(ang), jnp.sin(ang)
            lane = lax.broadcasted_iota(jnp.int32, ang.shape, 1)
            c_ref[...] = jnp.where(lane < ROPE_LO, 1.0, jnp.where(lane < ROPE_HI, cs, 0.0))
            sa_ref[...] = jnp.where(jnp.logical_and(lane >= ROPE_LO, lane < ROPE_MID), -sn, 0.0)
            sb_ref[...] = jnp.where(jnp.logical_and(lane >= ROPE_MID, lane < ROPE_HI), sn, 0.0)

        x, y, c = _mesh_pos()
        me = (x, y, c)
        winb_ref[...] = win_ref[0].astype(BF16)
        oab_ref[...] = woa_ref[0].astype(BF16)
        obb_ref[...] = wob_ref[0].astype(BF16)
        outb_ref[...] = wout_ref[0].astype(BF16)
        g_uq[4 * x + 2 * y + c] = wuq_ref[...].astype(BF16)

        chip0 = jnp.logical_and(x == 0, y == 0)
        south = c == 0
        half = D_MODEL // 2
        halves = [blk0.at[pl.ds(0, half)], blk0.at[pl.ds(half, half)]]

        def bcopy(k, to, part=None):
            ref = blk0 if part is None else halves[part]
            return _remote(ref, ref, send_sems, recv_sems, 7 + k, to)

        sends0 = [(0, (0, 0, 1), None), (1, (1, 0, 0), 0), (2, (0, 1, 0), 1), (3, (1, 0, 0), 1), (4, (0, 1, 0), 0)]

        @pl.when(jnp.logical_and(chip0, south))
        def _():
            blk0[...] = winb_ref[...]
            for k, to, part in sends0:
                bcopy(k, to, part).start()

        _gather_exchange([g_uq], send_sems, recv_sems, meanwhile=local_work)

        for (cx, cy), first_k, first_half, second_k in (((1, 0), 1, 0, 3), ((0, 1), 2, 1, 4)):
            @pl.when(jnp.logical_and(jnp.logical_and(x == cx, y == cy), south))
            def _(cx=cx, cy=cy, first_k=first_k, first_half=first_half, second_k=second_k):
                bcopy(first_k, me, first_half).wait_recv()
                onward = bcopy(5 + first_half, (1, 1, 0), first_half)
                onward.start()
                bcopy(second_k, me, 1 - first_half).wait_recv()
                north = bcopy(7, (cx, cy, 1))
                north.start()
                onward.wait_send()
                north.wait_send()

        @pl.when(jnp.logical_and(jnp.logical_and(x == 1, y == 1), south))
        def _():
            bcopy(5, me, 0).wait_recv()
            bcopy(6, me, 1).wait_recv()
            north = bcopy(7, (1, 1, 1))
            north.start()
            north.wait_send()

        @pl.when(jnp.logical_and(chip0, c == 1))
        def _():
            bcopy(0, me).wait_recv()

        @pl.when(jnp.logical_and(jnp.logical_not(chip0), c == 1))
        def _():
            bcopy(7, me).wait_recv()

        @pl.when(jnp.logical_and(chip0, south))
        def _():
            for k, to, part in sends0:
                bcopy(k, to, part).wait_send()

        for j, s0, s1, seg, d0 in _column_runs():
            if seg == 2:
                wc_ref[:, d0:d0 + (s1 - s0)] = blk0[:, s0:s1]
        zeros = lambda r, w: jnp.zeros((r, w), BF16)
        wc_ref[:, Q_LORA_RANK:CQ_PAD] = zeros(D_MODEL, CQ_PAD - Q_LORA_RANK)
        wc_ref[:, CQ_PAD + LANES:CQ_PAD + LANES + ROPE_LO] = zeros(D_MODEL, ROPE_LO)
        wc_ref[:, CQ_PAD + LANES + ROPE_HI:SEG_C] = zeros(D_MODEL, LANES - ROPE_HI)
        wq_ref[Q_LORA_RANK:CQ_PAD, :] = zeros(CQ_PAD - Q_LORA_RANK, hw)
        for h in range(MLA_HEADS):
            wq_ref[0:Q_LORA_RANK, HEAD_PAD * h + QK_HEAD_DIM:HEAD_PAD * (h + 1)] = zeros(Q_LORA_RANK, HEAD_PAD - QK_HEAD_DIM)
        for j in range(N_DEV):
            for h in range(MLA_HEADS):
                wq_ref[uq_rows * j:uq_rows * (j + 1), HEAD_PAD * h:HEAD_PAD * h + QK_HEAD_DIM] = g_uq[
                    j, :, QK_HEAD_DIM * h:QK_HEAD_DIM * (h + 1)]

    vmem = pl.BlockSpec(memory_space=pltpu.VMEM)
    return pl.pallas_call(
        body, name="gather_first",
        out_shape=[jax.ShapeDtypeStruct((D_MODEL, SEG_C), BF16), jax.ShapeDtypeStruct((CQ_PAD, hw), BF16),
                   jax.ShapeDtypeStruct(w_in.shape[1:], BF16), jax.ShapeDtypeStruct(w_oa.shape[1:], BF16),
                   jax.ShapeDtypeStruct(w_ob.shape[1:], BF16), jax.ShapeDtypeStruct(w_out.shape[1:], BF16),
                   jax.ShapeDtypeStruct((SEQ, D_MODEL), BF16), jax.ShapeDtypeStruct((D_MODEL, SEQ), BF16)]
        + [jax.ShapeDtypeStruct((SEQ, LANES), F32)] * 3,
        in_specs=[vmem] * 8, out_specs=[vmem] * 11,
        scratch_shapes=[pltpu.VMEM((N_DEV, uq_rows, MLA_HEADS * QK_HEAD_DIM), BF16), pltpu.VMEM((D_MODEL, SHARD_W), BF16),
                        pltpu.SemaphoreType.DMA((15,)), pltpu.SemaphoreType.DMA((15,))],
        compiler_params=pltpu.CompilerParams(vmem_limit_bytes=VMEM_BIG),
    )(w_in, w_uq2, w_oa, w_ob, w_out, x2, pos_col, invf_lane)


def _assemble_in(g_in):
    def body(g_ref, wa_ref, wb_ref):
        segs = [wa_ref, wb_ref]
        for j, s0, s1, seg, d0 in _column_runs():
            if seg < 2:
                segs[seg][:, d0:d0 + (s1 - s0)] = g_ref[j, :, s0:s1]

    rows = 256
    return pl.pallas_call(
        body, name="assemble_in", grid=(D_MODEL // rows,),
        in_specs=[pl.BlockSpec((N_DEV, rows, SHARD_W), lambda i: (0, i, 0))],
        out_specs=[pl.BlockSpec((rows, SEG_A), lambda i: (i, 0)), pl.BlockSpec((rows, SEG_B), lambda i: (i, 0))],
        out_shape=[jax.ShapeDtypeStruct((D_MODEL, SEG_A), BF16), jax.ShapeDtypeStruct((D_MODEL, SEG_B), BF16)],
        compiler_params=pltpu.CompilerParams(dimension_semantics=("arbitrary",), vmem_limit_bytes=VMEM_MID),
    )(g_in)


def _assemble_out(g_oa, g_ob, g_out):
    cols = D_MODEL // N_DEV

    def body(goa_ref, gob_ref, gout_ref, oa_ref, ob_ref, out_ref):
        for j in range(N_DEV):
            oa_ref[:, cols * j:cols * (j + 1)] = goa_ref[j]
            ob_ref[:, cols * j:cols * (j + 1)] = gob_ref[j]
            out_ref[cols * j:cols * (j + 1), :] = gout_ref[j]

    return pl.pallas_call(
        body, name="assemble_out",
        out_shape=[jax.ShapeDtypeStruct((MLA_WIDTH, D_MODEL), BF16), jax.ShapeDtypeStruct((SGU_WIDTH, D_MODEL), BF16),
                   jax.ShapeDtypeStruct((D_MODEL, D_MODEL), BF16)],
    )(g_oa, g_ob, g_out)


C_NAT = 544


P_IN_SPLIT = 896


def _to_parts(dwa, dwb):
    def body(dwa_ref, dwb_ref, phi_ref, plo_ref):
        phi_ref[0, :, 0:C_NAT] = jnp.zeros((P_IN_SPLIT, C_NAT), BF16)
        plo_ref[0, :, 0:C_NAT] = jnp.zeros((D_MODEL - P_IN_SPLIT, C_NAT), BF16)
        segs = [dwa_ref, dwb_ref]
        for j, s0, s1, seg, d0 in _column_runs():
            if seg < 2:
                phi_ref[j, :, s0:s1] = segs[seg][0:P_IN_SPLIT, d0:d0 + (s1 - s0)]
                plo_ref[j, :, s0:s1] = segs[seg][P_IN_SPLIT:D_MODEL, d0:d0 + (s1 - s0)]

    return pl.pallas_call(
        body, name="to_parts",
        out_shape=[jax.ShapeDtypeStruct((N_DEV, P_IN_SPLIT, SHARD_W), BF16),
                   jax.ShapeDtypeStruct((N_DEV, D_MODEL - P_IN_SPLIT, SHARD_W), BF16)],
        compiler_params=pltpu.CompilerParams(vmem_limit_bytes=VMEM_MID))(dwa, dwb)


def _dx_tail(dhs, ws, dx_res, dwc, p_uq, p_rep):
    tm = SEQ // 4
    rep_rows = p_rep.shape[1]
    c_rows = D_MODEL // N_DEV
    spec = [((c_rows, C_NAT), BF16), (p_uq.shape[1:], BF16), ((rep_rows, LANES), F32)]
    n = len(spec)

    nseg = len(dhs)

    def body(*refs):
        dh_refs, w_refs = refs[:nseg], refs[nseg:2 * nseg]
        dxr_ref, dwc_ref, puq_ref, prep_ref, dx_ref, call_ref, guq_ref, repall_ref, pc_ref, c_all, rep_all = refs[
            2 * nseg:2 * nseg + 11]
        rest = refs[2 * nseg + 11:]
        ras, tbs, rbs = rest[0:n], rest[n:2 * n], rest[2 * n:3 * n]
        send_sems, recv_sems, gsend, grecv = rest[3 * n:]
        step = pl.program_id(0)
        x, y, c = _mesh_pos()
        me_idx = 4 * x + 2 * y + c
        me, sibling = (x, y, c), (x, y, 1 - c)
        others = [(1 - x, y), (x, 1 - y), (1 - x, 1 - y)]
        parts = [pc_ref, puq_ref, prep_ref]
        gats = [rep_all, c_all]

        def stage1(chip, a):
            return _remote(parts[a].at[2 * chip + (1 - c)], ras[a].at[chip], send_sems, recv_sems, 7 * a + chip, sibling)

        def stage2(k, a):
            cx, cy = others[k]
            return _remote(tbs[a].at[k], rbs[a].at[k], send_sems, recv_sems, 7 * a + 4 + k, (cx, cy, c))

        def gcopy(a, k, blk, to):
            slab = gats[a].at[4 * blk[0] + 2 * blk[1] + blk[2]]
            return _remote(slab, slab, gsend, grecv, 7 * a + k, to)

        def chip_sum(a, chip):
            return parts[a][2 * chip + c].astype(F32) + ras[a][chip].astype(F32)

        @pl.when(step == 0)
        def _():
            for j, s0, s1, seg, d0 in _column_runs():
                if seg == 2:
                    for r in range(N_DEV):
                        pc_ref[r, :, s0:s1] = dwc_ref[c_rows * r:c_rows * (r + 1), d0:d0 + (s1 - s0)]
            for chip in range(4):
                for a in range(n):
                    stage1(chip, a).start()

        @pl.when(step == 1)
        def _():
            for chip in range(4):
                for a in range(n):
                    stage1(chip, a).wait_recv()
            for k, (cx, cy) in enumerate(others):
                for a in range(n):
                    tbs[a][k] = chip_sum(a, 2 * cx + cy).astype(spec[a][1])
                    stage2(k, a).start()

        @pl.when(step == 2)
        def _():
            for k in range(3):
                for a in range(n):
                    stage2(k, a).wait_recv()
            sums = []
            for a in range(n):
                acc = chip_sum(a, 2 * x + y)
                for k in range(3):
                    acc = acc + rbs[a][k].astype(F32)
                sums.append(acc)
            c_all[me_idx] = sums[0].astype(BF16)
            guq_ref[...] = sums[1]
            rep_all[me_idx] = sums[2]
            for a in range(2):
                for j, chip in enumerate(others):
                    gcopy(a, 1 + j, me, (*chip, c)).start()
                gcopy(a, 0, me, sibling).start()

        @pl.when(step == 3)
        def _():
            for j, chip in enumerate(others):
                for a in range(2):
                    gcopy(a, 1 + j, (*chip, c), me).wait_recv()
                    gcopy(a, 4 + j, (*chip, c), sibling).start()
            for a in range(2):
                gcopy(a, 0, sibling, me).wait_recv()
                for j, chip in enumerate(others):
                    gcopy(a, 4 + j, (*chip, 1 - c), me).wait_recv()
            for a in range(2):
                gcopy(a, 0, me, sibling).wait_send()
                for j, chip in enumerate(others):
                    gcopy(a, 1 + j, me, (*chip, c)).wait_send()
                    gcopy(a, 4 + j, (*chip, c), sibling).wait_send()
            for a in range(n):
                for chip in range(4):
                    stage1(chip, a).wait_send()
                for k in range(3):
                    stage2(k, a).wait_send()
            call_ref[...] = c_all[...]
            repall_ref[...] = rep_all[...]

        acc = dxr_ref[...]
        for dh_ref, w_ref in zip(dh_refs, w_refs):
            acc = acc + _dot(dh_ref[...], w_ref[...], _NT)
        dx_ref[...] = acc

    row = lambda w: pl.BlockSpec((tm, w), lambda i: (i, 0))
    full = lambda shape: pl.BlockSpec(shape, lambda i: (0,) * len(shape))
    scratch = [pltpu.VMEM((N_DEV, c_rows, C_NAT), BF16), pltpu.VMEM((N_DEV, c_rows, C_NAT), BF16),
               pltpu.VMEM((N_DEV, rep_rows, LANES), F32)]
    for lead in (4, 3, 3):
        scratch += [pltpu.VMEM((lead,) + tuple(shape), dt) for shape, dt in spec]
    scratch += [pltpu.SemaphoreType.DMA((7 * n,)), pltpu.SemaphoreType.DMA((7 * n,)),
                pltpu.SemaphoreType.DMA((14,)), pltpu.SemaphoreType.DMA((14,))]
    return pl.pallas_call(
        body, name="dx_tail", grid=(SEQ // tm,),
        in_specs=[row(dh.shape[1]) for dh in dhs] + [full(w.shape) for w in ws]
        + [row(D_MODEL), full(dwc.shape), full(p_uq.shape), full(p_rep.shape)],
        out_specs=[row(D_MODEL), full((N_DEV, c_rows, C_NAT)), full(p_uq.shape[1:]), full((N_DEV, rep_rows, LANES))],
        out_shape=[jax.ShapeDtypeStruct((SEQ, D_MODEL), F32), jax.ShapeDtypeStruct((N_DEV, c_rows, C_NAT), BF16),
                   jax.ShapeDtypeStruct(p_uq.shape[1:], F32), jax.ShapeDtypeStruct((N_DEV, rep_rows, LANES), F32)],
        scratch_shapes=scratch,
        compiler_params=pltpu.CompilerParams(dimension_semantics=("arbitrary",), vmem_limit_bytes=VMEM_BIG),
    )(*dhs, *ws, dx_res, dwc, p_uq, p_rep)


def _sum_landed(landed, c_all):
    c_rows = D_MODEL // N_DEV

    def body(rhi_ref, rlo_ref, roa_ref, rob_ref, rout_ref, call_ref, gin_ref, goa_ref, gob_ref, gout_ref):
        def total(ref, sl):
            acc = ref[0, sl, :].astype(F32)
            for s in range(1, N_DEV):
                acc = acc + ref[s, sl, :].astype(F32)
            return acc

        x, y, c = _mesh_pos()
        dev0 = jnp.where(4 * x + 2 * y + c == 0, 1.0, 0.0)
        for j in range(N_DEV):
            sl = slice(c_rows * j, c_rows * (j + 1))
            below = c_rows * j < P_IN_SPLIT
            tot = total(rhi_ref, sl) if below else total(rlo_ref, slice(c_rows * j - P_IN_SPLIT, c_rows * (j + 1) - P_IN_SPLIT))
            gin_ref[0, sl, C_NAT:SHARD_W] = tot[:, C_NAT:SHARD_W]
            gin_ref[0, sl, 0:C_NAT] = tot[:, 0:C_NAT] + dev0 * call_ref[j].astype(F32)
        goa_ref[0] = total(roa_ref, slice(None))
        gob_ref[0] = total(rob_ref, slice(None))
        gout_ref[0] = total(rout_ref, slice(None))

    return pl.pallas_call(
        body, name="sum_landed",
        out_shape=[jax.ShapeDtypeStruct((1, D_MODEL, SHARD_W), F32)]
        + [jax.ShapeDtypeStruct((1,) + r.shape[1:], F32) for r in landed[2:]],
        compiler_params=pltpu.CompilerParams(vmem_limit_bytes=VMEM_MID),
    )(*landed, c_all)


_O_CQ, _O_CKV, _O_KPE, _O_ZA, _O_U, _O_V, _O_ZB, _O_GA, _O_GB = 0, 384, 512, 544, 1056, 1568, 2080, 2592, 3616


def _to_segments(w):
    z = lambda n: jnp.zeros(w.shape[:-1] + (n,), w.dtype)
    seg_a = jnp.concatenate([w[..., _O_GA:_O_GB], w[..., _O_GB:IN_WIDTH], w[..., _O_ZA:_O_U]], axis=-1)
    seg_b = jnp.concatenate([w[..., _O_U:_O_V], w[..., _O_V:_O_ZB], w[..., _O_ZB:_O_GA]], axis=-1)
    seg_c = jnp.concatenate([w[..., _O_CQ:_O_CKV], z(CQ_PAD - Q_LORA_RANK), w[..., _O_CKV:_O_KPE],
                             z(ROPE_LO), w[..., _O_KPE:_O_ZA], z(LANES - ROPE_HI)], axis=-1)
    return seg_a, seg_b, seg_c


def _from_segments(seg_a, seg_b, seg_c):
    kpe0 = CQ_PAD + LANES + ROPE_LO
    return jnp.concatenate([
        seg_c[..., 0:Q_LORA_RANK], seg_c[..., CQ_PAD:CQ_PAD + LANES], seg_c[..., kpe0:kpe0 + QK_ROPE_DIM],
        seg_a[..., 2 * D_MODEL:SEG_A], seg_b, seg_a[..., 0:2 * D_MODEL]], axis=-1)


def kernel(x, positions, w_in, b_in, g_q, w_uq, g_kv, w_ukv, w_oa, sgu_ln_g, sgu_ln_b, w_s, b_s, w_ob, w_out, ln_g, ln_b, loss_target, m_w_in, m_b_in, m_g_q, m_w_uq, m_g_kv, m_w_ukv, m_w_oa, m_sgu_ln_g, m_sgu_ln_b, m_w_s, m_b_s, m_w_ob, m_w_out, m_ln_g, m_ln_b, v_w_in, v_b_in, v_g_q, v_w_uq, v_g_kv, v_w_ukv, v_w_oa, v_sgu_ln_g, v_sgu_ln_b, v_w_s, v_b_s, v_w_ob, v_w_out, v_ln_g, v_ln_b):
    w_uq2 = w_uq[0].reshape(Q_LORA_RANK // N_DEV, MLA_HEADS * QK_HEAD_DIM)
    inv_freq = ROPE_THETA ** (-jnp.arange(0, QK_ROPE_DIM, 2, dtype=F32) / QK_ROPE_DIM)
    invf_lane = jnp.concatenate([jnp.zeros((ROPE_LO,), F32), inv_freq, inv_freq,
                                 jnp.zeros((LANES - ROPE_HI,), F32)]).reshape(1, LANES)
    first = _gather_first(w_in, w_uq2, w_oa, w_ob, w_out, x[0], positions.reshape(SEQ, 1), invf_lane)
    partials = _local_step(x[0], loss_target[0], first, b_in, g_q, g_kv, w_ukv, sgu_ln_g, sgu_ln_b, w_s, b_s, ln_g, ln_b)
    weights = dict(w_in=w_in, b_in=b_in, g_q=g_q, w_uq=w_uq, g_kv=g_kv, w_ukv=w_ukv, w_oa=w_oa, sgu_ln_g=sgu_ln_g,
                   sgu_ln_b=sgu_ln_b, w_s=w_s, b_s=b_s, w_ob=w_ob, w_out=w_out, ln_g=ln_g, ln_b=ln_b)
    moms = dict(w_in=m_w_in, b_in=m_b_in, g_q=m_g_q, w_uq=m_w_uq, g_kv=m_g_kv, w_ukv=m_w_ukv, w_oa=m_w_oa,
                sgu_ln_g=m_sgu_ln_g, sgu_ln_b=m_sgu_ln_b, w_s=m_w_s, b_s=m_b_s, w_ob=m_w_ob, w_out=m_w_out,
                ln_g=m_ln_g, ln_b=m_ln_b)
    vars_ = dict(w_in=v_w_in, b_in=v_b_in, g_q=v_g_q, w_uq=v_w_uq, g_kv=v_g_kv, w_ukv=v_w_ukv, w_oa=v_w_oa,
                 sgu_ln_g=v_sgu_ln_g, sgu_ln_b=v_sgu_ln_b, w_s=v_w_s, b_s=v_b_s, w_ob=v_w_ob, w_out=v_w_out,
                 ln_g=v_ln_g, ln_b=v_ln_b)
    return _reduce_and_update(partials, weights, moms, vars_)


def _local_step(x2, tgt, first, b_in, g_q, g_kv, w_ukv, sgu_ln_g, sgu_ln_b, w_s, b_s, ln_g, ln_b):
    wc, wq, win_b, oa_b, ob_b, out_b, x_bf, xt_bf, c_t, sa_t, sb_t = first
    ba, bb, bc = _to_segments(b_in)
    w_ukv_bf = w_ukv[0].astype(BF16)
    wkn = jnp.pad(w_ukv_bf[:, :, :QK_NOPE_DIM], ((0, 0), (0, 0), (0, HEAD_PAD - QK_NOPE_DIM))).reshape(KV_LORA_RANK, -1)
    wv = jnp.pad(w_ukv_bf[:, :, QK_NOPE_DIM:], ((0, 0), (0, 0), (0, HEAD_PAD - V_HEAD_DIM))).reshape(KV_LORA_RANK, -1)
    gq = jnp.pad(g_q, ((0, 0), (0, CQ_PAD - Q_LORA_RANK)))
    bias_full = jnp.repeat(b_s[0].T, SGU_GROUP_DIM, axis=1)
    w_s3 = w_s[0]
    w_st3 = jnp.swapaxes(w_s3, 1, 2)

    h_c = _mm(x_bf, wc, bias=bc, tm=512, tn=SEG_C, name="in_proj_c")
    q, k, kt, vx, vxt = _mla_prep(h_c, gq, g_kv, wq, wkn, wv, c_t, sa_t, sb_t)
    o, lse, (g_in,) = _attn_fwd(q, kt, vx, (win_b,))
    wa, wb = _assemble_in(g_in)
    h_a, (g_out,) = _mm(x_bf, wa, bias=ba, own=(out_b,), tm=512, tn=SEG_A // 2, name="in_proj_a")
    h_b, (g_oa, g_ob) = _mm(x_bf, wb, bias=bb, own=(oa_b, ob_b), tm=512, tn=SEG_B // 2, name="in_proj_b")
    y_b = _sgu_fwd(h_b, sgu_ln_g, sgu_ln_b, w_s3, bias_full)
    w_oa_f, w_ob_f, w_out_f = _assemble_out(g_oa, g_ob, g_out)

    (loss_row, dx_res, dh_a, d_o, d_yb, p_oa, p_ob, p_out, d_lng, d_lnb, d_ba) = _merge(
        x2, o, h_a, y_b, tgt, w_oa_f, w_ob_f, w_out_f, ln_g, ln_b)
    (dh_b, d_ws, d_bs_t, d_slg, d_slb, d_bb), (r_out,) = _sgu_bwd(h_b, d_yb, sgu_ln_g, sgu_ln_b, w_s3, w_st3, bias_full,
                                                                 (p_out,))
    d_wa, (r_oa,) = _mm(xt_bf, dh_a, out_dtype=BF16, parts=(p_oa,), tm=512, tn=512, name="dw_in_a")
    d_wb = _mm(xt_bf, dh_b, out_dtype=BF16, tm=512, tn=512, name="dw_in_b")
    p_hi, p_lo = _to_parts(d_wa, d_wb)
    dq, dk, dv, (r_hi,) = _attn_bwd(q, kt, k, vxt, d_o, o, lse, (p_hi,))
    (dh_c, p_uq, d_wkn, d_wv, d_gq, d_gkv, d_bc), (r_lo, r_ob) = _mla_bwd(
        dq, dk, dv, h_c, gq, g_kv, wq, wkn, wv, c_t, sa_t, sb_t, (p_lo, p_ob))
    landed = (r_hi, r_lo, r_oa, r_ob, r_out)
    d_wc = _mm(xt_bf, dh_c, out_dtype=BF16, tm=512, tn=SEG_C, name="dw_in_c")


    p_b_in = _from_segments(d_ba, d_bb, d_bc)
    p_w_ukv = jnp.concatenate([d_wkn.reshape(KV_LORA_RANK, MLA_HEADS, HEAD_PAD)[:, :, :QK_NOPE_DIM],
                               d_wv.reshape(KV_LORA_RANK, MLA_HEADS, HEAD_PAD)[:, :, :V_HEAD_DIM]], axis=-1)
    p_g_q = d_gq[:, :Q_LORA_RANK]
    p_b_s = d_bs_t[:, :SGU_GROUPS].T
    replicated = [p_b_in, p_g_q, d_gkv, p_w_ukv, d_slg, d_slb, d_ws, p_b_s, d_lng, d_lnb]
    return loss_row, ((dh_a, dh_b, dh_c), (wa, wb, wc), dx_res), landed, d_wc, p_uq, replicated


_NAMES = ["w_in", "b_in", "g_q", "w_uq", "g_kv", "w_ukv", "w_oa", "sgu_ln_g", "sgu_ln_b", "w_s", "b_s", "w_ob",
          "w_out", "ln_g", "ln_b"]
_REPLICATED = ["b_in", "g_q", "g_kv", "w_ukv", "sgu_ln_g", "sgu_ln_b", "w_s", "b_s", "ln_g", "ln_b"]


def _reduce_and_update(partials, weights, moms, vars_):
    loss_row, (dhs, ws, dx_res), landed, d_wc, p_uq, replicated = partials
    rep_flat = jnp.concatenate([a.reshape(-1) for a in replicated] + [loss_row[0, :1]])
    rep_flat = jnp.pad(rep_flat, (0, N_DEV * PACK_R_ROWS * LANES - rep_flat.size))
    dx_ab, c_all, g_uq, rep_all = _dx_tail(dhs[:2], ws[:2], dx_res, d_wc, p_uq,
                                           rep_flat.reshape(N_DEV, PACK_R_ROWS, LANES))
    dx = _mm(dhs[2], ws[2], tb=True, add=dx_ab, tm=512, tn=D_MODEL, name="dx_c")
    g_in, g_oa, g_ob, g_out = _sum_landed(landed, c_all)
    rep_sum = rep_all.reshape(-1)
    grads, pos = dict(w_in=g_in, w_uq=g_uq, w_oa=g_oa, w_ob=g_ob, w_out=g_out), 0
    for nm in _REPLICATED:
        grads[nm] = rep_sum[pos:pos + weights[nm].size]
        pos += weights[nm].size
    loss = rep_sum[pos]
    grads = {nm: grads[nm].reshape(weights[nm].shape) for nm in _NAMES}
    deltas, new_m, new_v = _adamw_all([weights[nm] for nm in _NAMES], [grads[nm] for nm in _NAMES],
                                      [moms[nm] for nm in _NAMES], [vars_[nm] for nm in _NAMES])
    return (loss, dx.reshape(1, SEQ, D_MODEL), *[grads[nm] for nm in _NAMES], *deltas, *new_m, *new_v)
```

```python
import math

import jax
import jax.numpy as jnp
from jax import lax
from jax.experimental import pallas as pl
from jax.experimental.pallas import tpu as pltpu

F32 = jnp.float32
BF16 = jnp.bfloat16

D_MODEL = 1024
SEQ = 2048
N_DEV = 8
MLA_HEADS = 8
Q_LORA_RANK = 384
KV_LORA_RANK = 128
QK_NOPE_DIM = 64
QK_ROPE_DIM = 32
V_HEAD_DIM = 64
QK_HEAD_DIM = QK_NOPE_DIM + QK_ROPE_DIM
MLA_WIDTH = MLA_HEADS * V_HEAD_DIM
ROPE_THETA = 10000.0
SGU_GROUPS = 8
SGU_GROUP_DIM = 64
SGU_WIDTH = SGU_GROUPS * SGU_GROUP_DIM
CHUNK = 128
RMS_EPS = 1e-6
LN_EPS = 1e-5
DN_ALPHA = 2.0 ** 0.25
IN_WIDTH = 4640
ATTN_SCALE = QK_HEAD_DIM ** -0.5

ADAM_LR = 0.001
ADAM_B1 = 0.9
ADAM_B2 = 0.999
ADAM_EPS = 1e-08
ADAM_WD = 0.01
ADAM_STEP = 10

LANES = 128
HEAD_PAD = 128
ROPE_LO = QK_NOPE_DIM
ROPE_MID = ROPE_LO + QK_ROPE_DIM // 2
ROPE_HI = ROPE_LO + QK_ROPE_DIM
CQ_PAD = 512

SEG_A = 2560
SEG_B = 1536
SEG_C = 768

PACK_R_ROWS = 272
VMEM_BIG = 56 * 1024 * 1024
VMEM_MID = 40 * 1024 * 1024


def _sigmoid(x):
    return 1.0 / (1.0 + jnp.exp(-x))


def _gelu_and_grad(x):
    c0 = math.sqrt(2.0 / math.pi)
    x2 = x * x
    t = jnp.tanh(c0 * (x + 0.044715 * x * x2))
    g = 0.5 * x * (1.0 + t)
    dg = 0.5 * (1.0 + t) + 0.5 * x * (1.0 - t * t) * (c0 * (1.0 + 3.0 * 0.044715 * x2))
    return g, dg


def _dot(a, b, dims):
    return lax.dot_general(a, b, (dims, ((), ())), preferred_element_type=F32)


_NN = ((1,), (0,))
_NT = ((1,), (1,))
_TN = ((0,), (0,))


def _store_grad(dh_ref, db_ref, col, val):
    cols = slice(col, col + val.shape[1])
    dh_ref[:, cols] = val.astype(BF16)
    db_ref[:, cols] += jnp.sum(val, axis=0, keepdims=True)


def _mm(a, b, *, tb=False, bias=None, add=None, out_dtype=F32, own=(), parts=(), tm, tn, name):
    m, k = a.shape
    n = b.shape[0] if tb else b.shape[1]
    assert m % tm == 0 and n % tn == 0 and not (own and parts)
    dims = _NT if tb else _NN
    nown = len(own) + len(parts)
    nm = m // tm
    nsteps = (n // tn) * nm

    def body(*refs):
        a_ref, b_ref = refs[0], refs[1]
        pos = 2
        r = _dot(a_ref[...], b_ref[...], dims)
        if bias is not None:
            r = r + refs[pos][...]; pos += 1
        if add is not None:
            r = r + refs[pos][...]; pos += 1
        own_refs = refs[pos:pos + nown]; pos += nown
        refs[pos][...] = r.astype(out_dtype)
        if nown:
            gat_refs = refs[pos + 1:pos + 1 + nown]
            send_sems, recv_sems, local_sems = refs[pos + 1 + nown:]
            step = pl.program_id(0) * nm + pl.program_id(1)
            if own:
                _gather_behind(own_refs, gat_refs, send_sems, recv_sems, local_sems, step, nsteps - 2, nsteps - 1)
            else:
                exchange = _exchange_parts(own_refs, gat_refs, send_sems, recv_sems, local_sems)
                _exchange_start(step == 0, exchange)
                _exchange_finish(step == nsteps - 1, exchange)

    b_spec = pl.BlockSpec((tn, k), lambda j, i: (j, 0)) if tb else pl.BlockSpec((k, tn), lambda j, i: (0, j))
    in_specs, args = [pl.BlockSpec((tm, k), lambda j, i: (i, 0)), b_spec], [a, b]
    if bias is not None:
        in_specs.append(pl.BlockSpec((1, tn), lambda j, i: (0, j))); args.append(bias)
    if add is not None:
        in_specs.append(pl.BlockSpec((tm, tn), lambda j, i: (i, j))); args.append(add)
    hbm = pl.BlockSpec(memory_space=pl.ANY)
    res = pl.pallas_call(
        body, name=name, grid=(n // tn, nm), in_specs=in_specs + [hbm] * nown,
        out_specs=[pl.BlockSpec((tm, tn), lambda j, i: (i, j))] + [hbm] * nown,
        out_shape=[jax.ShapeDtypeStruct((m, n), out_dtype)]
        + [jax.ShapeDtypeStruct((N_DEV,) + o.shape, o.dtype) for o in own]
        + [jax.ShapeDtypeStruct(p.shape, p.dtype) for p in parts],
        scratch_shapes=_exchange_sems(nown) if nown else [],
        compiler_params=pltpu.CompilerParams(dimension_semantics=("arbitrary", "arbitrary"), vmem_limit_bytes=VMEM_BIG),
    )(*args, *own, *parts)
    return (res[0], res[1:]) if nown else res[0]


def _rope(x, c, sa, sb):
    return x * c + pltpu.roll(x, LANES - 16, 1) * sa + pltpu.roll(x, 16, 1) * sb


def _rope_t(dy, c, sa, sb):
    return dy * c + pltpu.roll(dy * sa, 16, 1) + pltpu.roll(dy * sb, LANES - 16, 1)


def _mla_prep(h_c, gq, gkv, wq, wkn, wvx, c_t, sa_t, sb_t):
    tm = 256
    hw = MLA_HEADS * HEAD_PAD

    def body(cq_ref, ckv_ref, kpe_ref, gq_ref, gkv_ref, wq_ref, wkn_ref, wvx_ref, c_ref, sa_ref, sb_ref,
             q_ref, k_ref, kt_ref, vx_ref, vxt_ref):
        c, sa, sb = c_ref[...], sa_ref[...], sb_ref[...]
        cq = cq_ref[...]
        rq = lax.rsqrt(jnp.sum(cq * cq, axis=1, keepdims=True) * (1.0 / Q_LORA_RANK) + RMS_EPS)
        cqn = ((cq * rq) * gq_ref[...]).astype(BF16)
        qall = _dot(cqn, wq_ref[...], _NN)
        for h in range(MLA_HEADS):
            sl = slice(HEAD_PAD * h, HEAD_PAD * (h + 1))
            q_ref[:, sl] = (_rope(qall[:, sl], c, sa, sb) * ATTN_SCALE).astype(BF16)
        ckv = ckv_ref[...]
        rkv = lax.rsqrt(jnp.sum(ckv * ckv, axis=1, keepdims=True) * (1.0 / KV_LORA_RANK) + RMS_EPS)
        ckvn = ((ckv * rkv) * gkv_ref[...]).astype(BF16)
        knall = _dot(ckvn, wkn_ref[...], _NN)
        vall = _dot(ckvn, wvx_ref[...], _NN)
        kper = _rope(kpe_ref[...], c, sa, sb)
        ones_half = (lax.broadcasted_iota(jnp.int32, (tm, HEAD_PAD), 1) >= V_HEAD_DIM).astype(F32)
        for h in range(MLA_HEADS):
            sl = slice(HEAD_PAD * h, HEAD_PAD * (h + 1))
            kh = knall[:, sl] + kper
            vh = vall[:, sl] + ones_half
            k_ref[:, sl] = kh.astype(BF16)
            kt_ref[sl, :] = kh.T.astype(BF16)
            vx_ref[:, sl] = vh.astype(BF16)
            vxt_ref[sl, :] = vh.T.astype(BF16)

    full = lambda shape: pl.BlockSpec(shape, lambda i: (0, 0))
    tab = pl.BlockSpec((tm, LANES), lambda i: (i, 0))
    row = pl.BlockSpec((tm, hw), lambda i: (i, 0))
    col = pl.BlockSpec((hw, tm), lambda i: (0, i))
    return pl.pallas_call(
        body, name="mla_prep", grid=(SEQ // tm,),
        in_specs=[pl.BlockSpec((tm, CQ_PAD), lambda i: (i, 0)),
                  pl.BlockSpec((tm, LANES), lambda i: (i, CQ_PAD // LANES)),
                  pl.BlockSpec((tm, LANES), lambda i: (i, CQ_PAD // LANES + 1)),
                  full((1, CQ_PAD)), full((1, KV_LORA_RANK)),
                  full((CQ_PAD, hw)), full((KV_LORA_RANK, hw)), full((KV_LORA_RANK, hw)), tab, tab, tab],
        out_specs=[row, row, col, row, col],
        out_shape=[jax.ShapeDtypeStruct((SEQ, hw), BF16), jax.ShapeDtypeStruct((SEQ, hw), BF16),
                   jax.ShapeDtypeStruct((hw, SEQ), BF16), jax.ShapeDtypeStruct((SEQ, hw), BF16),
                   jax.ShapeDtypeStruct((hw, SEQ), BF16)],
        compiler_params=pltpu.CompilerParams(dimension_semantics=("arbitrary",), vmem_limit_bytes=VMEM_MID),
    )(h_c, h_c, h_c, gq, gkv, wq, wkn, wvx, c_t, sa_t, sb_t)


ATT_T = 512
ATT_STRIP = 64


def _attn_fwd(q, kt, vx, own, skip_device0):
    t, rs = ATT_T, ATT_STRIP
    nown = len(own)
    nq = SEQ // t
    nsteps = (MLA_HEADS // 2) * nq

    def body(q_ref, kt_ref, vx_ref, *rest):
        own_refs, (o_ref, l_ref), gat_refs = rest[:nown], rest[nown:nown + 2], rest[nown + 2:2 * nown + 2]
        s_scr, p_scr, m_scr, a_scr, acc_scr, send_sems, recv_sems, local_sems = rest[2 * nown + 2:]
        qi = pl.program_id(1)
        _gather_behind(own_refs, gat_refs, send_sems, recv_sems, local_sems, pl.program_id(0) * nq + qi,
                       nsteps - 2, nsteps - 1, skip_device0=skip_device0)
        lane = lax.broadcasted_iota(jnp.int32, (t, LANES), 1)
        m_scr[...] = jnp.full((2, t, LANES), -1e30, F32)
        acc_scr[...] = jnp.zeros((2, t, LANES), F32)

        def block(j, masked):
            off = pl.multiple_of(j * t, t)
            for a in range(2):
                sl = slice(HEAD_PAD * a, HEAD_PAD * (a + 1))
                s_scr[a] = _dot(q_ref[:, sl], kt_ref[sl, pl.ds(off, t)], _NN)
                for r in range(t // rs):
                    rows = slice(rs * r, rs * (r + 1))
                    s = s_scr[a, rows, :]
                    if masked:
                        rowi = lax.broadcasted_iota(jnp.int32, (rs, t), 0) + rs * r
                        coli = lax.broadcasted_iota(jnp.int32, (rs, t), 1)
                        s = jnp.where(coli <= rowi, s, -1e30)
                    m_old = m_scr[a, rows, :]
                    m_new = jnp.maximum(m_old, jnp.max(s, axis=1, keepdims=True))
                    p_scr[a, rows, :] = jnp.exp(s - m_new[:, :1]).astype(BF16)
                    a_scr[a, rows, :] = jnp.exp(m_old - m_new)
                    m_scr[a, rows, :] = m_new
                acc_scr[a] = acc_scr[a] * a_scr[a] + _dot(p_scr[a], vx_ref[pl.ds(off, t), sl], _NN)

        def step(j, carry):
            block(j, False)
            return carry
        lax.fori_loop(0, qi, step, 0)
        block(qi, True)
        res = []
        for a in range(2):
            acc = acc_scr[a]
            l = acc[:, V_HEAD_DIM:V_HEAD_DIM + 1]
            res.append((acc / l, m_scr[a] + jnp.log(l)))
        o_ref[...] = jnp.where(lane < V_HEAD_DIM, res[0][0], pltpu.roll(res[1][0], V_HEAD_DIM, 1))
        l_ref[...] = jnp.where(lane < V_HEAD_DIM, res[0][1], res[1][1])

    hbm = pl.BlockSpec(memory_space=pl.ANY)
    res = pl.pallas_call(
        body, name="attn_fwd", grid=(MLA_HEADS // 2, nq),
        in_specs=[pl.BlockSpec((t, 2 * HEAD_PAD), lambda p, i: (i, p)),
                  pl.BlockSpec((2 * HEAD_PAD, SEQ), lambda p, i: (p, 0)),
                  pl.BlockSpec((SEQ, 2 * HEAD_PAD), lambda p, i: (0, p))] + [hbm] * nown,
        out_specs=[pl.BlockSpec((t, LANES), lambda p, i: (i, p)),
                   pl.BlockSpec((t, LANES), lambda p, i: (i, p))] + [hbm] * nown,
        out_shape=[jax.ShapeDtypeStruct((SEQ, MLA_WIDTH), F32), jax.ShapeDtypeStruct((SEQ, MLA_WIDTH), F32)]
        + [jax.ShapeDtypeStruct((N_DEV,) + a.shape, a.dtype) for a in own],
        scratch_shapes=[pltpu.VMEM((2, t, t), F32), pltpu.VMEM((2, t, t), BF16), pltpu.VMEM((2, t, LANES), F32),
                        pltpu.VMEM((2, t, LANES), F32), pltpu.VMEM((2, t, LANES), F32)] + _exchange_sems(nown),
        compiler_params=pltpu.CompilerParams(dimension_semantics=("arbitrary", "arbitrary"), vmem_limit_bytes=VMEM_MID),
    )(q, kt, vx, *own)
    return res[0], res[1], res[2:]


def _exchange_parts(parts, lands, send_sems, recv_sems, local_sems):
    x, y, c = _mesh_pos()
    me = 4 * x + 2 * y + c
    peers = [(x, y, 1 - c), (1 - x, y, c), (x, 1 - y, c), (1 - x, 1 - y, c),
             (1 - x, y, 1 - c), (x, 1 - y, 1 - c), (1 - x, 1 - y, 1 - c)]
    remote, local = [], []
    for a, (part, land) in enumerate(zip(parts, lands)):
        for k, peer in enumerate(peers):
            t = 4 * peer[0] + 2 * peer[1] + peer[2]
            remote.append(_remote(part.at[t], land.at[me], send_sems, recv_sems, 7 * a + k, peer))
        local.append(pltpu.make_async_copy(part.at[me], land.at[me], local_sems.at[a]))
    return remote, local


def _exchange_start(first_step, exchange):
    remote, local = exchange

    @pl.when(first_step)
    def _():
        for cp in remote + local:
            cp.start()


def _exchange_finish(last_step, exchange):
    remote, local = exchange

    @pl.when(last_step)
    def _():
        for cp in remote:
            cp.wait_recv()
        for cp in remote:
            cp.wait_send()
        for cp in local:
            cp.wait()


def _exchange_sems(npart):
    return [pltpu.SemaphoreType.DMA((7 * npart,)), pltpu.SemaphoreType.DMA((7 * npart,)),
            pltpu.SemaphoreType.DMA((npart,))]


def _attn_bwd(q, kt, k, vxt, d_o, o, lse, parts):
    t, rs = ATT_T, ATT_STRIP
    nq = SEQ // t
    npart = len(parts)
    nsteps = MLA_HEADS // 2

    def body(q_ref, kt_ref, k_ref, vxt_ref, do_ref, o_ref, l_ref, *rest):
        part_refs, rest = rest[:npart], rest[npart:]
        dq_ref, dk_ref, dv_ref = rest[:3]
        land_refs, rest = rest[3:3 + npart], rest[3 + npart:]
        s_scr, dp_scr, p_scr, ds_scr, st_scr, send_sems, recv_sems, local_sems = rest
        exchange = _exchange_parts(part_refs, land_refs, send_sems, recv_sems, local_sems)
        _exchange_start(pl.program_id(0) == 0, exchange)
        dk_ref[...] = jnp.zeros_like(dk_ref)
        dv_ref[...] = jnp.zeros_like(dv_ref)
        lane = lax.broadcasted_iota(jnp.int32, (t, LANES), 1)

        def qtile(i, carry):
            ioff = pl.multiple_of(i * t, t)
            do_i = do_ref[pl.ds(ioff, t), :]
            o_i = o_ref[pl.ds(ioff, t), :]
            l_i = l_ref[pl.ds(ioff, t), :]
            for a in range(2):
                sl = slice(HEAD_PAD * a, HEAD_PAD * (a + 1))
                sel = (lane < V_HEAD_DIM) if a == 0 else (lane >= V_HEAD_DIM)
                doa = jnp.where(sel, do_i, 0.0)
                oa = o_i
                if a == 1:
                    doa = pltpu.roll(doa, V_HEAD_DIM, 1)
                    oa = pltpu.roll(o_i, V_HEAD_DIM, 1)
                st_scr[0] = jnp.broadcast_to(jnp.sum(doa * oa, axis=1, keepdims=True), (t, LANES))
                st_scr[1] = jnp.broadcast_to(l_i[:, V_HEAD_DIM * a:V_HEAD_DIM * a + 1], (t, LANES))
                doa_bf = doa.astype(BF16)
                qa = q_ref[pl.ds(ioff, t), sl]

                def block(j, masked, dq_acc, sl=sl, qa=qa, doa_bf=doa_bf):
                    joff = pl.multiple_of(j * t, t)
                    s_scr[...] = _dot(qa, kt_ref[sl, pl.ds(joff, t)], _NN)
                    dp_scr[...] = _dot(doa_bf, vxt_ref[sl, pl.ds(joff, t)], _NN)
                    for r in range(t // rs):
                        rows = slice(rs * r, rs * (r + 1))
                        p = jnp.exp(s_scr[rows, :] - st_scr[1, rows, :1])
                        if masked:
                            rowi = lax.broadcasted_iota(jnp.int32, (rs, t), 0) + rs * r
                            coli = lax.broadcasted_iota(jnp.int32, (rs, t), 1)
                            p = jnp.where(coli <= rowi, p, 0.0)
                        p_scr[rows, :] = p.astype(BF16)
                        ds_scr[rows, :] = (p * (dp_scr[rows, :] - st_scr[0, rows, :1])).astype(BF16)
                    dk_ref[pl.ds(joff, t), sl] += _dot(ds_scr[...], qa, _TN)
                    dv_ref[pl.ds(joff, t), sl] += _dot(p_scr[...], doa_bf, _TN)
                    return dq_acc + _dot(ds_scr[...], k_ref[pl.ds(joff, t), sl], _NN)

                dq_acc = lax.fori_loop(0, i, lambda j, acc: block(j, False, acc), jnp.zeros((t, HEAD_PAD), F32))
                dq_ref[pl.ds(ioff, t), sl] = block(i, True, dq_acc)
            return carry

        lax.fori_loop(0, nq, qtile, 0)
        _exchange_finish(pl.program_id(0) == nsteps - 1, exchange)

    hw = MLA_HEADS * HEAD_PAD
    wide = pl.BlockSpec((SEQ, 2 * HEAD_PAD), lambda p: (0, p))
    wide_t = pl.BlockSpec((2 * HEAD_PAD, SEQ), lambda p: (p, 0))
    narrow = pl.BlockSpec((SEQ, LANES), lambda p: (0, p))
    hbm = pl.BlockSpec(memory_space=pl.ANY)
    res = pl.pallas_call(
        body, name="attn_bwd", grid=(nsteps,),
        in_specs=[wide, wide_t, wide, wide_t, narrow, narrow, narrow] + [hbm] * npart,
        out_specs=[wide, wide, wide] + [hbm] * npart,
        out_shape=[jax.ShapeDtypeStruct((SEQ, hw), F32)] * 3 + [jax.ShapeDtypeStruct(p.shape, p.dtype) for p in parts],
        scratch_shapes=[pltpu.VMEM((t, t), F32), pltpu.VMEM((t, t), F32), pltpu.VMEM((t, t), BF16),
                        pltpu.VMEM((t, t), BF16), pltpu.VMEM((2, t, LANES), F32)] + _exchange_sems(npart),
        compiler_params=pltpu.CompilerParams(dimension_semantics=("arbitrary",), vmem_limit_bytes=VMEM_BIG),
    )(q, kt, k, vxt, d_o, o, lse, *parts)
    return res[0], res[1], res[2], res[3:]


def _sgu_math(u, v, zb, lg, lb, ws_ref, bias):
    ug, dug = _gelu_and_grad(u)
    vg, dvg = _gelu_and_grad(v)
    mu = jnp.mean(vg, axis=1, keepdims=True)
    xc = vg - mu
    rstd = lax.rsqrt(jnp.mean(xc * xc, axis=1, keepdims=True) + LN_EPS)
    xh = xc * rstd
    vn_bf = (xh * lg + lb).astype(BF16)
    grp = lax.broadcasted_iota(jnp.int32, (CHUNK, SGU_WIDTH), 1) // SGU_GROUP_DIM
    r_i = lax.broadcasted_iota(jnp.int32, (CHUNK, CHUNK), 0)
    c_i = lax.broadcasted_iota(jnp.int32, (CHUNK, CHUNK), 1)
    tri, tri_t = r_i >= c_i, r_i <= c_i
    mixed = bias
    for g in range(SGU_GROUPS):
        wt = jnp.where(tri, ws_ref[g], 0.0).astype(BF16)
        mixed = mixed + jnp.where(grp == g, _dot(wt, vn_bf, _NN), 0.0)
    sb = _sigmoid(zb)
    return ug, dug, dvg, rstd, xh, vn_bf, grp, tri, tri_t, mixed, sb


def _sgu_fwd(h_b, lg, lb, w_s, bias_full):
    def body(u_ref, v_ref, zb_ref, lg_ref, lb_ref, ws_ref, bias_ref, yb_ref):
        zb = zb_ref[...]
        ug, _, _, _, _, _, _, _, _, mixed, sb = _sgu_math(u_ref[...], v_ref[...], zb, lg_ref[...], lb_ref[...],
                                                       ws_ref, bias_ref[...])
        yb_ref[...] = (ug * mixed) * (zb * sb)

    blk = lambda c: pl.BlockSpec((CHUNK, SGU_WIDTH), lambda i, c=c: (i, c))
    full2 = lambda shape: pl.BlockSpec(shape, lambda i: (0, 0))
    return pl.pallas_call(
        body, name="sgu_fwd", grid=(SEQ // CHUNK,),
        in_specs=[blk(0), blk(1), blk(2), full2((1, SGU_WIDTH)), full2((1, SGU_WIDTH)),
                  pl.BlockSpec((SGU_GROUPS, CHUNK, CHUNK), lambda i: (0, 0, 0)), full2((CHUNK, SGU_WIDTH))],
        out_specs=pl.BlockSpec((CHUNK, SGU_WIDTH), lambda i: (i, 0)),
        out_shape=jax.ShapeDtypeStruct((SEQ, SGU_WIDTH), F32),
        compiler_params=pltpu.CompilerParams(dimension_semantics=("arbitrary",)),
    )(h_b, h_b, h_b, lg, lb, w_s, bias_full)


def _sgu_bwd(h_b, d_yb, lg, lb, w_s, w_st, bias_full, parts):
    nsteps = SEQ // CHUNK
    npart = len(parts)

    def body(u_ref, v_ref, zb_ref, dyb_ref, lg_ref, lb_ref, ws_ref, wst_ref, bias_ref, *rest):
        part_refs, rest = rest[:npart], rest[npart:]
        dhb_ref, dws_ref, dbs_ref, dlg_ref, dlb_ref, dbb_ref = rest[:6]
        land_refs, (dbias_acc, send_sems, recv_sems, local_sems) = rest[6:6 + npart], rest[6 + npart:]
        step = pl.program_id(0)
        exchange = _exchange_parts(part_refs, land_refs, send_sems, recv_sems, local_sems)
        _exchange_start(step == 0, exchange)

        @pl.when(step == 0)
        def _():
            dbb_ref[...] = jnp.zeros_like(dbb_ref)
            dws_ref[...] = jnp.zeros_like(dws_ref)
            dlg_ref[...] = jnp.zeros_like(dlg_ref)
            dlb_ref[...] = jnp.zeros_like(dlb_ref)
            dbias_acc[...] = jnp.zeros_like(dbias_acc)

        zb = zb_ref[...]
        lg = lg_ref[...]
        ug, dug, dvg, rstd, xh, vn_bf, grp, tri, tri_t, mixed, sb = _sgu_math(
            u_ref[...], v_ref[...], zb, lg, lb_ref[...], ws_ref, bias_ref[...])
        dyb = dyb_ref[...]
        dsgu = dyb * (zb * sb)
        dzb = dyb * (ug * mixed) * (sb * (1.0 + zb * (1.0 - sb)))
        du = dsgu * mixed * dug
        dmixed = dsgu * ug
        dbias_acc[...] += dmixed
        dvn = jnp.zeros((CHUNK, SGU_WIDTH), F32)
        for g in range(SGU_GROUPS):
            dm_g = jnp.where(grp == g, dmixed, 0.0).astype(BF16)
            wtt = jnp.where(tri_t, wst_ref[g], 0.0).astype(BF16)
            dvn = dvn + _dot(wtt, dm_g, _NN)
            dws_ref[g] += jnp.where(tri, _dot(dm_g, vn_bf, _NT), 0.0)
        dlg_ref[...] += jnp.sum(dvn * xh, axis=0, keepdims=True)
        dlb_ref[...] += jnp.sum(dvn, axis=0, keepdims=True)
        dxh = dvn * lg
        dvgel = rstd * (dxh - jnp.mean(dxh, axis=1, keepdims=True) - xh * jnp.mean(dxh * xh, axis=1, keepdims=True))
        _store_grad(dhb_ref, dbb_ref, 0, du)
        _store_grad(dhb_ref, dbb_ref, SGU_WIDTH, dvgel * dvg)
        _store_grad(dhb_ref, dbb_ref, 2 * SGU_WIDTH, dzb)

        @pl.when(step == nsteps - 1)
        def _():
            acc = dbias_acc[...]
            lane = lax.broadcasted_iota(jnp.int32, (CHUNK, LANES), 1)
            out = jnp.zeros((CHUNK, LANES), F32)
            for g in range(SGU_GROUPS):
                sg = jnp.sum(jnp.where(grp == g, acc, 0.0), axis=1, keepdims=True)
                out = jnp.where(lane == g, sg, out)
            dbs_ref[...] = out

        _exchange_finish(step == nsteps - 1, exchange)

    blk = lambda c: pl.BlockSpec((CHUNK, SGU_WIDTH), lambda i, c=c: (i, c))
    full2 = lambda shape: pl.BlockSpec(shape, lambda i: (0, 0))
    full3 = pl.BlockSpec((SGU_GROUPS, CHUNK, CHUNK), lambda i: (0, 0, 0))
    hbm = pl.BlockSpec(memory_space=pl.ANY)
    res = pl.pallas_call(
        body, name="sgu_bwd", grid=(nsteps,),
        in_specs=[blk(0), blk(1), blk(2), pl.BlockSpec((CHUNK, SGU_WIDTH), lambda i: (i, 0)),
                  full2((1, SGU_WIDTH)), full2((1, SGU_WIDTH)), full3, full3, full2((CHUNK, SGU_WIDTH))] + [hbm] * npart,
        out_specs=[pl.BlockSpec((CHUNK, SEG_B), lambda i: (i, 0)), full3, full2((CHUNK, LANES)),
                   full2((1, SGU_WIDTH)), full2((1, SGU_WIDTH)), full2((1, SEG_B))] + [hbm] * npart,
        out_shape=[jax.ShapeDtypeStruct((SEQ, SEG_B), BF16),
                   jax.ShapeDtypeStruct((SGU_GROUPS, CHUNK, CHUNK), F32),
                   jax.ShapeDtypeStruct((CHUNK, LANES), F32),
                   jax.ShapeDtypeStruct((1, SGU_WIDTH), F32), jax.ShapeDtypeStruct((1, SGU_WIDTH), F32),
                   jax.ShapeDtypeStruct((1, SEG_B), F32)] + [jax.ShapeDtypeStruct(p.shape, p.dtype) for p in parts],
        scratch_shapes=[pltpu.VMEM((CHUNK, SGU_WIDTH), F32)] + _exchange_sems(npart),
        compiler_params=pltpu.CompilerParams(dimension_semantics=("arbitrary",)),
    )(h_b, h_b, h_b, d_yb, lg, lb, w_s, w_st, bias_full, *parts)
    return res[:6], res[6:]


def _merge(x, o, h_a, y_b, target, w_oa, w_ob, w_out, ln_g, ln_b):
    tm = 256
    nsteps = SEQ // tm

    def body(x_ref, o_ref, ga_ref, gb_ref, za_ref, yb_ref, tgt_ref, woa_ref, wob_ref, wout_ref, lng_ref, lnb_ref,
             loss_ref, dxr_ref, dha_ref, do_ref, dyb_ref, poa_ref, pob_ref, pout_ref, dlng_ref, dlnb_ref, dba_ref,
             dwoa_ref, dwob_ref, dwout_ref):
        step = pl.program_id(0)

        @pl.when(step == 0)
        def _():
            for r in (loss_ref, dwoa_ref, dwob_ref, dwout_ref, dlng_ref, dlnb_ref, dba_ref):
                r[...] = jnp.zeros_like(r)

        o = o_ref[...]
        za = za_ref[...]
        sa = _sigmoid(za)
        ya_bf = (o * (za * sa)).astype(BF16)
        yb_bf = yb_ref[...].astype(BF16)
        woa, wob, wout = woa_ref[...], wob_ref[...], wout_ref[...]
        pa = _dot(ya_bf, woa, _NN)
        pb = _dot(yb_bf, wob, _NN)
        sga = _sigmoid(ga_ref[...])
        sgb = _sigmoid(gb_ref[...])
        merged_bf = (sga * pa + sgb * pb).astype(BF16)
        r = DN_ALPHA * x_ref[...] + _dot(merged_bf, wout, _NN)
        mu = jnp.mean(r, axis=1, keepdims=True)
        rc = r - mu
        rstd = lax.rsqrt(jnp.mean(rc * rc, axis=1, keepdims=True) + LN_EPS)
        xh = rc * rstd
        lng = lng_ref[...]
        y = xh * lng + lnb_ref[...]
        e = y - tgt_ref[...]
        loss_ref[...] += 0.5 * jnp.sum(jnp.sum(e * e, axis=1, keepdims=True) * (1.0 / D_MODEL), axis=0, keepdims=True)

        dy = e * (1.0 / D_MODEL)
        dlng_ref[...] += jnp.sum(dy * xh, axis=0, keepdims=True)
        dlnb_ref[...] += jnp.sum(dy, axis=0, keepdims=True)
        dxh = dy * lng
        dr = rstd * (dxh - jnp.mean(dxh, axis=1, keepdims=True) - xh * jnp.mean(dxh * xh, axis=1, keepdims=True))
        dxr_ref[...] = DN_ALPHA * dr
        dr_bf = dr.astype(BF16)
        dwout_ref[...] += _dot(merged_bf, dr_bf, _TN)
        dmerged = _dot(dr_bf, wout, _NT)
        dpa_bf = (dmerged * sga).astype(BF16)
        dpb_bf = (dmerged * sgb).astype(BF16)
        _store_grad(dha_ref, dba_ref, 0, dmerged * pa * (sga * (1.0 - sga)))
        _store_grad(dha_ref, dba_ref, D_MODEL, dmerged * pb * (sgb * (1.0 - sgb)))
        dwoa_ref[...] += _dot(ya_bf, dpa_bf, _TN)
        dwob_ref[...] += _dot(yb_bf, dpb_bf, _TN)
        dya = _dot(dpa_bf, woa, _NT)
        dyb_ref[...] = _dot(dpb_bf, wob, _NT)
        do_ref[...] = dya * (za * sa)
        _store_grad(dha_ref, dba_ref, 2 * D_MODEL, dya * o * (sa * (1.0 + za * (1.0 - sa))))

        @pl.when(step == nsteps - 1)
        def _():
            cols = D_MODEL // N_DEV
            for j in range(N_DEV):
                poa_ref[j] = dwoa_ref[:, cols * j:cols * (j + 1)].astype(BF16)
                pob_ref[j] = dwob_ref[:, cols * j:cols * (j + 1)].astype(BF16)
                pout_ref[j] = dwout_ref[cols * j:cols * (j + 1), :].astype(BF16)

    row = lambda w, c=0: pl.BlockSpec((tm, w), lambda i, c=c: (i, c))
    full = lambda shape: pl.BlockSpec(shape, lambda i: (0, 0))
    full3 = lambda shape: pl.BlockSpec(shape, lambda i: (0, 0, 0))
    return pl.pallas_call(
        body, name="merge", grid=(nsteps,),
        in_specs=[row(D_MODEL), row(MLA_WIDTH), row(D_MODEL, 0), row(D_MODEL, 1), row(MLA_WIDTH, 4), row(SGU_WIDTH),
                  row(D_MODEL), full((MLA_WIDTH, D_MODEL)), full((SGU_WIDTH, D_MODEL)), full((D_MODEL, D_MODEL)),
                  full((1, D_MODEL)), full((1, D_MODEL))],
        out_specs=[full((1, LANES)), row(D_MODEL), row(SEG_A), row(MLA_WIDTH), row(SGU_WIDTH),
                   full3((N_DEV, MLA_WIDTH, D_MODEL // N_DEV)), full3((N_DEV, SGU_WIDTH, D_MODEL // N_DEV)),
                   full3((N_DEV, D_MODEL // N_DEV, D_MODEL)), full((1, D_MODEL)), full((1, D_MODEL)), full((1, SEG_A))],
        out_shape=[jax.ShapeDtypeStruct((1, LANES), F32),
                   jax.ShapeDtypeStruct((SEQ, D_MODEL), F32), jax.ShapeDtypeStruct((SEQ, SEG_A), BF16),
                   jax.ShapeDtypeStruct((SEQ, MLA_WIDTH), F32), jax.ShapeDtypeStruct((SEQ, SGU_WIDTH), F32),
                   jax.ShapeDtypeStruct((N_DEV, MLA_WIDTH, D_MODEL // N_DEV), BF16),
                   jax.ShapeDtypeStruct((N_DEV, SGU_WIDTH, D_MODEL // N_DEV), BF16),
                   jax.ShapeDtypeStruct((N_DEV, D_MODEL // N_DEV, D_MODEL), BF16),
                   jax.ShapeDtypeStruct((1, D_MODEL), F32), jax.ShapeDtypeStruct((1, D_MODEL), F32),
                   jax.ShapeDtypeStruct((1, SEG_A), F32)],
        scratch_shapes=[pltpu.VMEM((MLA_WIDTH, D_MODEL), F32), pltpu.VMEM((SGU_WIDTH, D_MODEL), F32),
                        pltpu.VMEM((D_MODEL, D_MODEL), F32)],
        compiler_params=pltpu.CompilerParams(dimension_semantics=("arbitrary",), vmem_limit_bytes=VMEM_BIG),
    )(x, o, h_a, h_a, h_a, y_b, target, w_oa, w_ob, w_out, ln_g, ln_b)


def _mla_bwd(dq, dk, dv, h_c, gq, gkv, wq, wkn, wv, c_t, sa_t, sb_t, parts):
    tm = 256
    hw = MLA_HEADS * HEAD_PAD
    npart = len(parts)
    nsteps = SEQ // tm

    def body(dq_ref, dk_ref, dv_ref, cq_ref, ckv_ref, gq_ref, gkv_ref, wq_ref, wkn_ref, wv_ref, c_ref, sa_ref, sb_ref,
             *rest):
        part_refs, rest = rest[:npart], rest[npart:]
        dhc_ref, puq_ref, dwkn_ref, dwv_ref, dgq_ref, dgkv_ref, dbc_ref = rest[:7]
        land_refs, (pre_ref, dwq_ref, send_sems, recv_sems, local_sems) = rest[7:7 + npart], rest[7 + npart:]
        exchange = _exchange_parts(part_refs, land_refs, send_sems, recv_sems, local_sems)
        _exchange_start(pl.program_id(0) == 0, exchange)
        _exchange_finish(pl.program_id(0) == nsteps - 1, exchange)

        @pl.when(pl.program_id(0) == 0)
        def _():
            for r in (dwq_ref, dwkn_ref, dwv_ref, dgq_ref, dgkv_ref, dbc_ref):
                r[...] = jnp.zeros_like(r)

        c, sa, sb = c_ref[...], sa_ref[...], sb_ref[...]
        lane = lax.broadcasted_iota(jnp.int32, (tm, LANES), 1)
        rope_lanes = jnp.logical_and(lane >= ROPE_LO, lane < ROPE_HI)

        cq = cq_ref[...]
        gq = gq_ref[...]
        rq = lax.rsqrt(jnp.sum(cq * cq, axis=1, keepdims=True) * (1.0 / Q_LORA_RANK) + RMS_EPS)
        nq = cq * rq
        cqn_bf = (nq * gq).astype(BF16)
        for h in range(MLA_HEADS):
            sl = slice(HEAD_PAD * h, HEAD_PAD * (h + 1))
            pre_ref[:, sl] = _rope_t(dq_ref[:, sl] * ATTN_SCALE, c, sa, sb).astype(BF16)
        dqpre_bf = pre_ref[...]
        dcqn = _dot(dqpre_bf, wq_ref[...], _NT)
        dwq_ref[...] += _dot(cqn_bf, dqpre_bf, _TN)
        dgq_ref[...] += jnp.sum(dcqn * nq, axis=0, keepdims=True)
        dnq = dcqn * gq
        _store_grad(dhc_ref, dbc_ref, 0,
                    rq * (dnq - nq * (jnp.sum(dnq * nq, axis=1, keepdims=True) * (1.0 / Q_LORA_RANK))))

        ckv = ckv_ref[...]
        gkv = gkv_ref[...]
        rkv = lax.rsqrt(jnp.sum(ckv * ckv, axis=1, keepdims=True) * (1.0 / KV_LORA_RANK) + RMS_EPS)
        nkv = ckv * rkv
        ckvn_bf = (nkv * gkv).astype(BF16)
        dk = dk_ref[...]
        dk_bf = dk.astype(BF16)
        dv_bf = dv_ref[...].astype(BF16)
        dckvn = _dot(dk_bf, wkn_ref[...], _NT) + _dot(dv_bf, wv_ref[...], _NT)
        dwkn_ref[...] += _dot(ckvn_bf, dk_bf, _TN)
        dwv_ref[...] += _dot(ckvn_bf, dv_bf, _TN)
        dgkv_ref[...] += jnp.sum(dckvn * nkv, axis=0, keepdims=True)
        dnkv = dckvn * gkv
        _store_grad(dhc_ref, dbc_ref, CQ_PAD, rkv * (
            dnkv - nkv * (jnp.sum(dnkv * nkv, axis=1, keepdims=True) * (1.0 / KV_LORA_RANK))))
        dkpe = jnp.zeros((tm, LANES), F32)
        for h in range(MLA_HEADS):
            dkpe = dkpe + dk[:, HEAD_PAD * h:HEAD_PAD * (h + 1)]
        _store_grad(dhc_ref, dbc_ref, CQ_PAD + LANES, _rope_t(jnp.where(rope_lanes, dkpe, 0.0), c, sa, sb))

        @pl.when(pl.program_id(0) == SEQ // tm - 1)
        def _():
            rows = Q_LORA_RANK // N_DEV
            for j in range(N_DEV):
                for h in range(MLA_HEADS):
                    puq_ref[j, :, QK_HEAD_DIM * h:QK_HEAD_DIM * (h + 1)] = dwq_ref[
                        rows * j:rows * (j + 1), HEAD_PAD * h:HEAD_PAD * h + QK_HEAD_DIM].astype(BF16)

    full = lambda shape: pl.BlockSpec(shape, lambda i: (0, 0))
    row = lambda w, c=0: pl.BlockSpec((tm, w), lambda i, c=c: (i, c))
    hbm = pl.BlockSpec(memory_space=pl.ANY)
    res = pl.pallas_call(
        body, name="mla_bwd", grid=(nsteps,),
        in_specs=[row(hw), row(hw), row(hw), row(CQ_PAD, 0), row(LANES, CQ_PAD // LANES),
                  full((1, CQ_PAD)), full((1, KV_LORA_RANK)), full((CQ_PAD, hw)), full((KV_LORA_RANK, hw)),
                  full((KV_LORA_RANK, hw)), row(LANES), row(LANES), row(LANES)] + [hbm] * npart,
        out_specs=[row(SEG_C), pl.BlockSpec((N_DEV, Q_LORA_RANK // N_DEV, MLA_HEADS * QK_HEAD_DIM), lambda i: (0, 0, 0)),
                   full((KV_LORA_RANK, hw)), full((KV_LORA_RANK, hw)),
                   full((1, CQ_PAD)), full((1, KV_LORA_RANK)), full((1, SEG_C))] + [hbm] * npart,
        out_shape=[jax.ShapeDtypeStruct((SEQ, SEG_C), BF16),
                   jax.ShapeDtypeStruct((N_DEV, Q_LORA_RANK // N_DEV, MLA_HEADS * QK_HEAD_DIM), BF16),
                   jax.ShapeDtypeStruct((KV_LORA_RANK, hw), F32), jax.ShapeDtypeStruct((KV_LORA_RANK, hw), F32),
                   jax.ShapeDtypeStruct((1, CQ_PAD), F32), jax.ShapeDtypeStruct((1, KV_LORA_RANK), F32),
                   jax.ShapeDtypeStruct((1, SEG_C), F32)] + [jax.ShapeDtypeStruct(p.shape, p.dtype) for p in parts],
        scratch_shapes=[pltpu.VMEM((tm, hw), BF16), pltpu.VMEM((CQ_PAD, hw), F32)] + _exchange_sems(npart),
        compiler_params=pltpu.CompilerParams(dimension_semantics=("arbitrary",), vmem_limit_bytes=VMEM_MID),
    )(dq, dk, dv, h_c, h_c, gq, gkv, wq, wkn, wv, c_t, sa_t, sb_t, *parts)
    return res[:7], res[7:]


def _adamw_all(ws, gs, ms, vs):
    n = len(ws)
    c1 = 1.0 / (1.0 - ADAM_B1 ** ADAM_STEP)
    c2 = 1.0 / (1.0 - ADAM_B2 ** ADAM_STEP)

    def body(*refs):
        for idx in range(n):
            w, g, m, v = (refs[idx][...], refs[n + idx][...], refs[2 * n + idx][...], refs[3 * n + idx][...])
            m_new = ADAM_B1 * m + (1.0 - ADAM_B1) * g
            v_new = ADAM_B2 * v + (1.0 - ADAM_B2) * (g * g)
            delta = -ADAM_LR * ((m_new * c1) / (jnp.sqrt(v_new * c2) + ADAM_EPS) + ADAM_WD * w)
            refs[4 * n + idx][...] = delta
            refs[5 * n + idx][...] = m_new
            refs[6 * n + idx][...] = v_new

    shapes = [jax.ShapeDtypeStruct(w.shape, F32) for w in ws]
    outs = pl.pallas_call(
        body, name="adamw", out_shape=shapes * 3,
        compiler_params=pltpu.CompilerParams(vmem_limit_bytes=VMEM_BIG),
    )(*ws, *gs, *ms, *vs)
    return outs[:n], outs[n:2 * n], outs[2 * n:]


SHARD_W = IN_WIDTH // N_DEV

_PIECES = [(0, 384, 2, 0), (384, 512, 2, CQ_PAD), (512, 544, 2, CQ_PAD + LANES + ROPE_LO),
           (544, 1056, 0, 2 * D_MODEL), (1056, 1568, 1, 0), (1568, 2080, 1, SGU_WIDTH),
           (2080, 2592, 1, 2 * SGU_WIDTH), (2592, 3616, 0, 0), (3616, 4640, 0, D_MODEL)]


def _column_runs():
    runs = []
    for n0, n1, seg, d0 in _PIECES:
        for j in range(N_DEV):
            lo, hi = max(n0, j * SHARD_W), min(n1, (j + 1) * SHARD_W)
            if lo < hi:
                runs.append((j, lo - j * SHARD_W, hi - j * SHARD_W, seg, d0 + lo - n0))
    return runs


def _mesh_pos():
    return lax.axis_index("x"), lax.axis_index("y"), lax.axis_index("c")


def _remote(src, dst, send_sems, recv_sems, k, to):
    return pltpu.make_async_remote_copy(src_ref=src, dst_ref=dst, send_sem=send_sems.at[k], recv_sem=recv_sems.at[k],
                                        device_id=to, device_id_type=pl.DeviceIdType.MESH)


def _gather_exchange(gats, send_sems, recv_sems, meanwhile=None):
    x, y, c = _mesh_pos()
    me, sibling = (x, y, c), (x, y, 1 - c)
    chips = [(1 - x, y), (x, 1 - y), (1 - x, 1 - y)]

    def copy(a, k, blk, to):
        slab = gats[a].at[4 * blk[0] + 2 * blk[1] + blk[2]]
        return _remote(slab, slab, send_sems, recv_sems, 7 * a + k, to)

    arrays = range(len(gats))
    first = [copy(a, 1 + j, me, (*chip, c)) for j, chip in enumerate(chips) for a in arrays]
    first += [copy(a, 0, me, sibling) for a in arrays]
    for cp in first:
        cp.start()
    if meanwhile is not None:
        meanwhile()
    passed = []
    for j, chip in enumerate(chips):
        for a in arrays:
            copy(a, 1 + j, (*chip, c), me).wait_recv()
            fwd = copy(a, 4 + j, (*chip, c), sibling)
            fwd.start()
            passed.append(fwd)
    for a in arrays:
        copy(a, 0, sibling, me).wait_recv()
    for j, chip in enumerate(chips):
        for a in arrays:
            copy(a, 4 + j, (*chip, 1 - c), me).wait_recv()
    for cp in first + passed:
        cp.wait_send()


def _gather_behind(own, gats, send_sems, recv_sems, local_sems, step, mid, last, skip_device0=False):
    x, y, c = _mesh_pos()
    me, sibling = (x, y, c), (x, y, 1 - c)
    chips = [(1 - x, y), (x, 1 - y), (1 - x, 1 - y)]
    arrays = range(len(gats))

    def copy(a, k, blk, to, src=None):
        slab = gats[a].at[4 * blk[0] + 2 * blk[1] + blk[2]]
        return _remote(slab if src is None else src, slab, send_sems, recv_sems, 7 * a + k, to)

    def moved(blk, fn):
        if skip_device0:
            pl.when(4 * blk[0] + 2 * blk[1] + blk[2] != 0)(fn)
        else:
            fn()

    first = [copy(a, 1 + j, me, (*chip, c), src=own[a]) for j, chip in enumerate(chips) for a in arrays]
    first += [copy(a, 0, me, sibling, src=own[a]) for a in arrays]
    local = [pltpu.make_async_copy(own[a], gats[a].at[4 * x + 2 * y + c], local_sems.at[a]) for a in arrays]

    def passed(j):
        return [copy(a, 4 + j, (*chips[j], c), sibling) for a in arrays]

    def each(method, copies):
        def fn():
            for cp in copies:
                getattr(cp, method)()
        return fn

    @pl.when(step == 0)
    def _():
        moved(me, each("start", first + local))

    @pl.when(step == mid)
    def _():
        for j, chip in enumerate(chips):
            moved((*chip, c), each("wait_recv", [copy(a, 1 + j, (*chip, c), me) for a in arrays]))
        for j, chip in enumerate(chips):
            moved((*chip, c), each("start", passed(j)))

    @pl.when(step == last)
    def _():
        moved(sibling, each("wait_recv", [copy(a, 0, sibling, me) for a in arrays]))
        for j, chip in enumerate(chips):
            moved((*chip, 1 - c), each("wait_recv", [copy(a, 4 + j, (*chip, 1 - c), me) for a in arrays]))
            moved((*chip, c), each("wait_send", passed(j)))
        moved(me, each("wait_send", first))
        moved(me, each("wait", local))


def _gather_first(w_in, w_uq2, w_oa, w_ob, w_out, x2, pos_col, invf_lane):
    hw = MLA_HEADS * HEAD_PAD
    uq_rows = Q_LORA_RANK // N_DEV
    rows = 256

    def body(win_ref, wuq_ref, woa_ref, wob_ref, wout_ref, x_ref, pos_ref, invf_ref,
             wc_ref, wq_ref, winb_ref, oab_ref, obb_ref, outb_ref, xb_ref, xt_ref, c_ref, sa_ref, sb_ref, blk0,
             g_uq, send_sems, recv_sems):
        def local_work():
            for i in range(SEQ // rows):
                xi = x_ref[rows * i:rows * (i + 1), :]
                xb_ref[rows * i:rows * (i + 1), :] = xi.astype(BF16)
                xt_ref[:, rows * i:rows * (i + 1)] = xi.T.astype(BF16)
            ang = pos_ref[...].astype(F32) * invf_ref[...]
            cs, sn = jnp.cos(ang), jnp.sin(ang)
            lane = lax.broadcasted_iota(jnp.int32, ang.shape, 1)
            c_ref[...] = jnp.where(lane < ROPE_LO, 1.0, jnp.where(lane < ROPE_HI, cs, 0.0))
            sa_ref[...] = jnp.where(jnp.logical_and(lane >= ROPE_LO, lane < ROPE_MID), -sn, 0.0)
            sb_ref[...] = jnp.where(jnp.logical_and(lane >= ROPE_MID, lane < ROPE_HI), sn, 0.0)

        x, y, c = _mesh_pos()
        me = (x, y, c)
        winb_ref[...] = win_ref[0].astype(BF16)
        oab_ref[...] = woa_ref[0].astype(BF16)
        obb_ref[...] = wob_ref[0].astype(BF16)
        outb_ref[...] = wout_ref[0].astype(BF16)
        g_uq[4 * x + 2 * y + c] = wuq_ref[...].astype(BF16)

        chip0 = jnp.logical_and(x == 0, y == 0)
        south = c == 0
        half = D_MODEL // 2
        halves = [blk0.at[pl.ds(0, half)], blk0.at[pl.ds(half, half)]]

        def bcopy(k, to, part=None):
            ref = blk0 if part is None else halves[part]
            return _remote(ref, ref, send_sems, recv_sems, 7 + k, to)

        sends0 = [(0, (0, 0, 1), None), (1, (1, 0, 0), 0), (2, (0, 1, 0), 1), (3, (1, 0, 0), 1), (4, (0, 1, 0), 0)]

        @pl.when(jnp.logical_and(chip0, south))
        def _():
            blk0[...] = winb_ref[...]
            for k, to, part in sends0:
                bcopy(k, to, part).start()

        _gather_exchange([g_uq], send_sems, recv_sems, meanwhile=local_work)

        for (cx, cy), first_k, first_half, second_k in (((1, 0), 1, 0, 3), ((0, 1), 2, 1, 4)):
            @pl.when(jnp.logical_and(jnp.logical_and(x == cx, y == cy), south))
            def _(cx=cx, cy=cy, first_k=first_k, first_half=first_half, second_k=second_k):
                bcopy(first_k, me, first_half).wait_recv()
                onward = bcopy(5 + first_half, (1, 1, 0), first_half)
                onward.start()
                bcopy(second_k, me, 1 - first_half).wait_recv()
                north = bcopy(7, (cx, cy, 1))
                north.start()
                onward.wait_send()
                north.wait_send()

        @pl.when(jnp.logical_and(jnp.logical_and(x == 1, y == 1), south))
        def _():
            bcopy(5, me, 0).wait_recv()
            bcopy(6, me, 1).wait_recv()
            north = bcopy(7, (1, 1, 1))
            north.start()
            north.wait_send()

        @pl.when(jnp.logical_and(chip0, c == 1))
        def _():
            bcopy(0, me).wait_recv()

        @pl.when(jnp.logical_and(jnp.logical_not(chip0), c == 1))
        def _():
            bcopy(7, me).wait_recv()

        @pl.when(jnp.logical_and(chip0, south))
        def _():
            for k, to, part in sends0:
                bcopy(k, to, part).wait_send()

        for j, s0, s1, seg, d0 in _column_runs():
            if seg == 2:
                wc_ref[:, d0:d0 + (s1 - s0)] = blk0[:, s0:s1]
        zeros = lambda r, w: jnp.zeros((r, w), BF16)
        wc_ref[:, Q_LORA_RANK:CQ_PAD] = zeros(D_MODEL, CQ_PAD - Q_LORA_RANK)
        wc_ref[:, CQ_PAD + LANES:CQ_PAD + LANES + ROPE_LO] = zeros(D_MODEL, ROPE_LO)
        wc_ref[:, CQ_PAD + LANES + ROPE_HI:SEG_C] = zeros(D_MODEL, LANES - ROPE_HI)
        wq_ref[Q_LORA_RANK:CQ_PAD, :] = zeros(CQ_PAD - Q_LORA_RANK, hw)
        for h in range(MLA_HEADS):
            wq_ref[0:Q_LORA_RANK, HEAD_PAD * h + QK_HEAD_DIM:HEAD_PAD * (h + 1)] = zeros(Q_LORA_RANK, HEAD_PAD - QK_HEAD_DIM)
        for j in range(N_DEV):
            for h in range(MLA_HEADS):
                wq_ref[uq_rows * j:uq_rows * (j + 1), HEAD_PAD * h:HEAD_PAD * h + QK_HEAD_DIM] = g_uq[
                    j, :, QK_HEAD_DIM * h:QK_HEAD_DIM * (h + 1)]

    vmem = pl.BlockSpec(memory_space=pltpu.VMEM)
    return pl.pallas_call(
        body, name="gather_first",
        out_shape=[jax.ShapeDtypeStruct((D_MODEL, SEG_C), BF16), jax.ShapeDtypeStruct((CQ_PAD, hw), BF16),
                   jax.ShapeDtypeStruct(w_in.shape[1:], BF16), jax.ShapeDtypeStruct(w_oa.shape[1:], BF16),
                   jax.ShapeDtypeStruct(w_ob.shape[1:], BF16), jax.ShapeDtypeStruct(w_out.shape[1:], BF16),
                   jax.ShapeDtypeStruct((SEQ, D_MODEL), BF16), jax.ShapeDtypeStruct((D_MODEL, SEQ), BF16)]
        + [jax.ShapeDtypeStruct((SEQ, LANES), F32)] * 3 + [jax.ShapeDtypeStruct((D_MODEL, SHARD_W), BF16)],
        in_specs=[vmem] * 8, out_specs=[vmem] * 12,
        scratch_shapes=[pltpu.VMEM((N_DEV, uq_rows, MLA_HEADS * QK_HEAD_DIM), BF16),
                        pltpu.SemaphoreType.DMA((15,)), pltpu.SemaphoreType.DMA((15,))],
        compiler_params=pltpu.CompilerParams(vmem_limit_bytes=VMEM_BIG),
    )(w_in, w_uq2, w_oa, w_ob, w_out, x2, pos_col, invf_lane)


def _assemble_in(g_in, blk0):
    def body(g_ref, blk0_ref, wa_ref, wb_ref):
        segs = [wa_ref, wb_ref]
        for j, s0, s1, seg, d0 in _column_runs():
            if seg < 2:
                segs[seg][:, d0:d0 + (s1 - s0)] = blk0_ref[:, s0:s1] if j == 0 else g_ref[j, :, s0:s1]

    return pl.pallas_call(
        body, name="assemble_in",
        out_shape=[jax.ShapeDtypeStruct((D_MODEL, SEG_A), BF16), jax.ShapeDtypeStruct((D_MODEL, SEG_B), BF16)],
        compiler_params=pltpu.CompilerParams(vmem_limit_bytes=VMEM_MID),
    )(g_in, blk0)


def _assemble_out(g_oa, g_ob, g_out):
    cols = D_MODEL // N_DEV

    def body(goa_ref, gob_ref, gout_ref, oa_ref, ob_ref, out_ref):
        for j in range(N_DEV):
            oa_ref[:, cols * j:cols * (j + 1)] = goa_ref[j]
            ob_ref[:, cols * j:cols * (j + 1)] = gob_ref[j]
            out_ref[cols * j:cols * (j + 1), :] = gout_ref[j]

    return pl.pallas_call(
        body, name="assemble_out",
        out_shape=[jax.ShapeDtypeStruct((MLA_WIDTH, D_MODEL), BF16), jax.ShapeDtypeStruct((SGU_WIDTH, D_MODEL), BF16),
                   jax.ShapeDtypeStruct((D_MODEL, D_MODEL), BF16)],
    )(g_oa, g_ob, g_out)


C_NAT = 544


P_IN_SPLIT = 896


def _to_parts(dwa, dwb):
    def body(dwa_ref, dwb_ref, phi_ref, plo_ref):
        phi_ref[0, :, 0:C_NAT] = jnp.zeros((P_IN_SPLIT, C_NAT), BF16)
        plo_ref[0, :, 0:C_NAT] = jnp.zeros((D_MODEL - P_IN_SPLIT, C_NAT), BF16)
        segs = [dwa_ref, dwb_ref]
        for j, s0, s1, seg, d0 in _column_runs():
            if seg < 2:
                phi_ref[j, :, s0:s1] = segs[seg][0:P_IN_SPLIT, d0:d0 + (s1 - s0)]
                plo_ref[j, :, s0:s1] = segs[seg][P_IN_SPLIT:D_MODEL, d0:d0 + (s1 - s0)]

    return pl.pallas_call(
        body, name="to_parts",
        out_shape=[jax.ShapeDtypeStruct((N_DEV, P_IN_SPLIT, SHARD_W), BF16),
                   jax.ShapeDtypeStruct((N_DEV, D_MODEL - P_IN_SPLIT, SHARD_W), BF16)],
        compiler_params=pltpu.CompilerParams(vmem_limit_bytes=VMEM_MID))(dwa, dwb)


def _dx_tail(dhs, ws, dx_res, dwc, p_uq, p_rep):
    tm = SEQ // 4
    rep_rows = p_rep.shape[1]
    c_rows = D_MODEL // N_DEV
    spec = [((c_rows, C_NAT), BF16), (p_uq.shape[1:], BF16), ((rep_rows, LANES), F32)]
    n = len(spec)

    nseg = len(dhs)

    def body(*refs):
        dh_refs, w_refs = refs[:nseg], refs[nseg:2 * nseg]
        dxr_ref, dwc_ref, puq_ref, prep_ref, dx_ref, call_ref, guq_ref, repall_ref, pc_ref, c_all, rep_all = refs[
            2 * nseg:2 * nseg + 11]
        rest = refs[2 * nseg + 11:]
        ras, tbs, rbs = rest[0:n], rest[n:2 * n], rest[2 * n:3 * n]
        send_sems, recv_sems, gsend, grecv = rest[3 * n:]
        step = pl.program_id(0)
        x, y, c = _mesh_pos()
        me_idx = 4 * x + 2 * y + c
        me, sibling = (x, y, c), (x, y, 1 - c)
        others = [(1 - x, y), (x, 1 - y), (1 - x, 1 - y)]
        parts = [pc_ref, puq_ref, prep_ref]
        gats = [rep_all, c_all]

        def stage1(chip, a):
            return _remote(parts[a].at[2 * chip + (1 - c)], ras[a].at[chip], send_sems, recv_sems, 7 * a + chip, sibling)

        def stage2(k, a):
            cx, cy = others[k]
            return _remote(tbs[a].at[k], rbs[a].at[k], send_sems, recv_sems, 7 * a + 4 + k, (cx, cy, c))

        def gcopy(a, k, blk, to):
            slab = gats[a].at[4 * blk[0] + 2 * blk[1] + blk[2]]
            return _remote(slab, slab, gsend, grecv, 7 * a + k, to)

        def chip_sum(a, chip):
            return parts[a][2 * chip + c].astype(F32) + ras[a][chip].astype(F32)

        @pl.when(step == 0)
        def _():
            for j, s0, s1, seg, d0 in _column_runs():
                if seg == 2:
                    for r in range(N_DEV):
                        pc_ref[r, :, s0:s1] = dwc_ref[c_rows * r:c_rows * (r + 1), d0:d0 + (s1 - s0)]
            for chip in range(4):
                for a in range(n):
                    stage1(chip, a).start()

        @pl.when(step == 1)
        def _():
            for chip in range(4):
                for a in range(n):
                    stage1(chip, a).wait_recv()
            for k, (cx, cy) in enumerate(others):
                for a in range(n):
                    tbs[a][k] = chip_sum(a, 2 * cx + cy).astype(spec[a][1])
                    stage2(k, a).start()

        @pl.when(step == 2)
        def _():
            for k in range(3):
                for a in range(n):
                    stage2(k, a).wait_recv()
            sums = []
            for a in range(n):
                acc = chip_sum(a, 2 * x + y)
                for k in range(3):
                    acc = acc + rbs[a][k].astype(F32)
                sums.append(acc)
            c_all[me_idx] = sums[0].astype(BF16)
            guq_ref[...] = sums[1]
            rep_all[me_idx] = sums[2]
            for a in range(2):
                for j, chip in enumerate(others):
                    gcopy(a, 1 + j, me, (*chip, c)).start()
                gcopy(a, 0, me, sibling).start()

        @pl.when(step == 3)
        def _():
            for j, chip in enumerate(others):
                for a in range(2):
                    gcopy(a, 1 + j, (*chip, c), me).wait_recv()
                    gcopy(a, 4 + j, (*chip, c), sibling).start()
            for a in range(2):
                gcopy(a, 0, sibling, me).wait_recv()
                for j, chip in enumerate(others):
                    gcopy(a, 4 + j, (*chip, 1 - c), me).wait_recv()
            for a in range(2):
                gcopy(a, 0, me, sibling).wait_send()
                for j, chip in enumerate(others):
                    gcopy(a, 1 + j, me, (*chip, c)).wait_send()
                    gcopy(a, 4 + j, (*chip, c), sibling).wait_send()
            for a in range(n):
                for chip in range(4):
                    stage1(chip, a).wait_send()
                for k in range(3):
                    stage2(k, a).wait_send()
            call_ref[...] = c_all[...]
            repall_ref[...] = rep_all[...]

        acc = dxr_ref[...]
        for dh_ref, w_ref in zip(dh_refs, w_refs):
            acc = acc + _dot(dh_ref[...], w_ref[...], _NT)
        dx_ref[...] = acc

    row = lambda w: pl.BlockSpec((tm, w), lambda i: (i, 0))
    full = lambda shape: pl.BlockSpec(shape, lambda i: (0,) * len(shape))
    scratch = [pltpu.VMEM((N_DEV, c_rows, C_NAT), BF16), pltpu.VMEM((N_DEV, c_rows, C_NAT), BF16),
               pltpu.VMEM((N_DEV, rep_rows, LANES), F32)]
    for lead in (4, 3, 3):
        scratch += [pltpu.VMEM((lead,) + tuple(shape), dt) for shape, dt in spec]
    scratch += [pltpu.SemaphoreType.DMA((7 * n,)), pltpu.SemaphoreType.DMA((7 * n,)),
                pltpu.SemaphoreType.DMA((14,)), pltpu.SemaphoreType.DMA((14,))]
    return pl.pallas_call(
        body, name="dx_tail", grid=(SEQ // tm,),
        in_specs=[row(dh.shape[1]) for dh in dhs] + [full(w.shape) for w in ws]
        + [row(D_MODEL), full(dwc.shape), full(p_uq.shape), full(p_rep.shape)],
        out_specs=[row(D_MODEL), full((N_DEV, c_rows, C_NAT)), full(p_uq.shape[1:]), full((N_DEV, rep_rows, LANES))],
        out_shape=[jax.ShapeDtypeStruct((SEQ, D_MODEL), F32), jax.ShapeDtypeStruct((N_DEV, c_rows, C_NAT), BF16),
                   jax.ShapeDtypeStruct(p_uq.shape[1:], F32), jax.ShapeDtypeStruct((N_DEV, rep_rows, LANES), F32)],
        scratch_shapes=scratch,
        compiler_params=pltpu.CompilerParams(dimension_semantics=("arbitrary",), vmem_limit_bytes=VMEM_BIG),
    )(*dhs, *ws, dx_res, dwc, p_uq, p_rep)


def _sum_landed(landed, c_all):
    c_rows = D_MODEL // N_DEV

    def body(rhi_ref, rlo_ref, roa_ref, rob_ref, rout_ref, call_ref, gin_ref, goa_ref, gob_ref, gout_ref):
        def total(ref, sl):
            acc = ref[0, sl, :].astype(F32)
            for s in range(1, N_DEV):
                acc = acc + ref[s, sl, :].astype(F32)
            return acc

        x, y, c = _mesh_pos()
        dev0 = jnp.where(4 * x + 2 * y + c == 0, 1.0, 0.0)
        for j in range(N_DEV):
            sl = slice(c_rows * j, c_rows * (j + 1))
            below = c_rows * j < P_IN_SPLIT
            tot = total(rhi_ref, sl) if below else total(rlo_ref, slice(c_rows * j - P_IN_SPLIT, c_rows * (j + 1) - P_IN_SPLIT))
            gin_ref[0, sl, C_NAT:SHARD_W] = tot[:, C_NAT:SHARD_W]
            gin_ref[0, sl, 0:C_NAT] = tot[:, 0:C_NAT] + dev0 * call_ref[j].astype(F32)
        goa_ref[0] = total(roa_ref, slice(None))
        gob_ref[0] = total(rob_ref, slice(None))
        gout_ref[0] = total(rout_ref, slice(None))

    return pl.pallas_call(
        body, name="sum_landed",
        out_shape=[jax.ShapeDtypeStruct((1, D_MODEL, SHARD_W), F32)]
        + [jax.ShapeDtypeStruct((1,) + r.shape[1:], F32) for r in landed[2:]],
        compiler_params=pltpu.CompilerParams(vmem_limit_bytes=VMEM_MID),
    )(*landed, c_all)


_O_CQ, _O_CKV, _O_KPE, _O_ZA, _O_U, _O_V, _O_ZB, _O_GA, _O_GB = 0, 384, 512, 544, 1056, 1568, 2080, 2592, 3616


def _to_segments(w):
    z = lambda n: jnp.zeros(w.shape[:-1] + (n,), w.dtype)
    seg_a = jnp.concatenate([w[..., _O_GA:_O_GB], w[..., _O_GB:IN_WIDTH], w[..., _O_ZA:_O_U]], axis=-1)
    seg_b = jnp.concatenate([w[..., _O_U:_O_V], w[..., _O_V:_O_ZB], w[..., _O_ZB:_O_GA]], axis=-1)
    seg_c = jnp.concatenate([w[..., _O_CQ:_O_CKV], z(CQ_PAD - Q_LORA_RANK), w[..., _O_CKV:_O_KPE],
                             z(ROPE_LO), w[..., _O_KPE:_O_ZA], z(LANES - ROPE_HI)], axis=-1)
    return seg_a, seg_b, seg_c


def _from_segments(seg_a, seg_b, seg_c):
    kpe0 = CQ_PAD + LANES + ROPE_LO
    return jnp.concatenate([
        seg_c[..., 0:Q_LORA_RANK], seg_c[..., CQ_PAD:CQ_PAD + LANES], seg_c[..., kpe0:kpe0 + QK_ROPE_DIM],
        seg_a[..., 2 * D_MODEL:SEG_A], seg_b, seg_a[..., 0:2 * D_MODEL]], axis=-1)


def kernel(x, positions, w_in, b_in, g_q, w_uq, g_kv, w_ukv, w_oa, sgu_ln_g, sgu_ln_b, w_s, b_s, w_ob, w_out, ln_g, ln_b, loss_target, m_w_in, m_b_in, m_g_q, m_w_uq, m_g_kv, m_w_ukv, m_w_oa, m_sgu_ln_g, m_sgu_ln_b, m_w_s, m_b_s, m_w_ob, m_w_out, m_ln_g, m_ln_b, v_w_in, v_b_in, v_g_q, v_w_uq, v_g_kv, v_w_ukv, v_w_oa, v_sgu_ln_g, v_sgu_ln_b, v_w_s, v_b_s, v_w_ob, v_w_out, v_ln_g, v_ln_b):
    w_uq2 = w_uq[0].reshape(Q_LORA_RANK // N_DEV, MLA_HEADS * QK_HEAD_DIM)
    inv_freq = ROPE_THETA ** (-jnp.arange(0, QK_ROPE_DIM, 2, dtype=F32) / QK_ROPE_DIM)
    invf_lane = jnp.concatenate([jnp.zeros((ROPE_LO,), F32), inv_freq, inv_freq,
                                 jnp.zeros((LANES - ROPE_HI,), F32)]).reshape(1, LANES)
    first = _gather_first(w_in, w_uq2, w_oa, w_ob, w_out, x[0], positions.reshape(SEQ, 1), invf_lane)
    partials = _local_step(x[0], loss_target[0], first, b_in, g_q, g_kv, w_ukv, sgu_ln_g, sgu_ln_b, w_s, b_s, ln_g, ln_b)
    weights = dict(w_in=w_in, b_in=b_in, g_q=g_q, w_uq=w_uq, g_kv=g_kv, w_ukv=w_ukv, w_oa=w_oa, sgu_ln_g=sgu_ln_g,
                   sgu_ln_b=sgu_ln_b, w_s=w_s, b_s=b_s, w_ob=w_ob, w_out=w_out, ln_g=ln_g, ln_b=ln_b)
    moms = dict(w_in=m_w_in, b_in=m_b_in, g_q=m_g_q, w_uq=m_w_uq, g_kv=m_g_kv, w_ukv=m_w_ukv, w_oa=m_w_oa,
                sgu_ln_g=m_sgu_ln_g, sgu_ln_b=m_sgu_ln_b, w_s=m_w_s, b_s=m_b_s, w_ob=m_w_ob, w_out=m_w_out,
                ln_g=m_ln_g, ln_b=m_ln_b)
    vars_ = dict(w_in=v_w_in, b_in=v_b_in, g_q=v_g_q, w_uq=v_w_uq, g_kv=v_g_kv, w_ukv=v_w_ukv, w_oa=v_w_oa,
                 sgu_ln_g=v_sgu_ln_g, sgu_ln_b=v_sgu_ln_b, w_s=v_w_s, b_s=v_b_s, w_ob=v_w_ob, w_out=v_w_out,
                 ln_g=v_ln_g, ln_b=v_ln_b)
    return _reduce_and_update(partials, weights, moms, vars_)


def _local_step(x2, tgt, first, b_in, g_q, g_kv, w_ukv, sgu_ln_g, sgu_ln_b, w_s, b_s, ln_g, ln_b):
    wc, wq, win_b, oa_b, ob_b, out_b, x_bf, xt_bf, c_t, sa_t, sb_t, blk0 = first
    ba, bb, bc = _to_segments(b_in)
    w_ukv_bf = w_ukv[0].astype(BF16)
    wkn = jnp.pad(w_ukv_bf[:, :, :QK_NOPE_DIM], ((0, 0), (0, 0), (0, HEAD_PAD - QK_NOPE_DIM))).reshape(KV_LORA_RANK, -1)
    wv = jnp.pad(w_ukv_bf[:, :, QK_NOPE_DIM:], ((0, 0), (0, 0), (0, HEAD_PAD - V_HEAD_DIM))).reshape(KV_LORA_RANK, -1)
    gq = jnp.pad(g_q, ((0, 0), (0, CQ_PAD - Q_LORA_RANK)))
    bias_full = jnp.repeat(b_s[0].T, SGU_GROUP_DIM, axis=1)
    w_s3 = w_s[0]
    w_st3 = jnp.swapaxes(w_s3, 1, 2)

    h_c = _mm(x_bf, wc, bias=bc, tm=512, tn=SEG_C, name="in_proj_c")
    q, k, kt, vx, vxt = _mla_prep(h_c, gq, g_kv, wq, wkn, wv, c_t, sa_t, sb_t)
    o, lse, (g_in,) = _attn_fwd(q, kt, vx, (win_b,), skip_device0=True)
    wa, wb = _assemble_in(g_in, blk0)
    h_a, (g_out,) = _mm(x_bf, wa, bias=ba, own=(out_b,), tm=512, tn=SEG_A // 2, name="in_proj_a")
    h_b, (g_oa, g_ob) = _mm(x_bf, wb, bias=bb, own=(oa_b, ob_b), tm=512, tn=SEG_B // 2, name="in_proj_b")
    y_b = _sgu_fwd(h_b, sgu_ln_g, sgu_ln_b, w_s3, bias_full)
    w_oa_f, w_ob_f, w_out_f = _assemble_out(g_oa, g_ob, g_out)

    (loss_row, dx_res, dh_a, d_o, d_yb, p_oa, p_ob, p_out, d_lng, d_lnb, d_ba) = _merge(
        x2, o, h_a, y_b, tgt, w_oa_f, w_ob_f, w_out_f, ln_g, ln_b)
    (dh_b, d_ws, d_bs_t, d_slg, d_slb, d_bb), (r_out,) = _sgu_bwd(h_b, d_yb, sgu_ln_g, sgu_ln_b, w_s3, w_st3, bias_full,
                                                                 (p_out,))
    d_wa, (r_oa,) = _mm(xt_bf, dh_a, out_dtype=BF16, parts=(p_oa,), tm=512, tn=512, name="dw_in_a")
    d_wb = _mm(xt_bf, dh_b, out_dtype=BF16, tm=512, tn=512, name="dw_in_b")
    p_hi, p_lo = _to_parts(d_wa, d_wb)
    dq, dk, dv, (r_hi,) = _attn_bwd(q, kt, k, vxt, d_o, o, lse, (p_hi,))
    (dh_c, p_uq, d_wkn, d_wv, d_gq, d_gkv, d_bc), (r_lo, r_ob) = _mla_bwd(
        dq, dk, dv, h_c, gq, g_kv, wq, wkn, wv, c_t, sa_t, sb_t, (p_lo, p_ob))
    landed = (r_hi, r_lo, r_oa, r_ob, r_out)
    d_wc = _mm(xt_bf, dh_c, out_dtype=BF16, tm=512, tn=SEG_C, name="dw_in_c")


    p_b_in = _from_segments(d_ba, d_bb, d_bc)
    p_w_ukv = jnp.concatenate([d_wkn.reshape(KV_LORA_RANK, MLA_HEADS, HEAD_PAD)[:, :, :QK_NOPE_DIM],
                               d_wv.reshape(KV_LORA_RANK, MLA_HEADS, HEAD_PAD)[:, :, :V_HEAD_DIM]], axis=-1)
    p_g_q = d_gq[:, :Q_LORA_RANK]
    p_b_s = d_bs_t[:, :SGU_GROUPS].T
    replicated = [p_b_in, p_g_q, d_gkv, p_w_ukv, d_slg, d_slb, d_ws, p_b_s, d_lng, d_lnb]
    return loss_row, ((dh_a, dh_b, dh_c), (wa, wb, wc), dx_res), landed, d_wc, p_uq, replicated


_NAMES = ["w_in", "b_in", "g_q", "w_uq", "g_kv", "w_ukv", "w_oa", "sgu_ln_g", "sgu_ln_b", "w_s", "b_s", "w_ob",
          "w_out", "ln_g", "ln_b"]
_REPLICATED = ["b_in", "g_q", "g_kv", "w_ukv", "sgu_ln_g", "sgu_ln_b", "w_s", "b_s", "ln_g", "ln_b"]


def _reduce_and_update(partials, weights, moms, vars_):
    loss_row, (dhs, ws, dx_res), landed, d_wc, p_uq, replicated = partials
    rep_flat = jnp.concatenate([a.reshape(-1) for a in replicated] + [loss_row[0, :1]])
    rep_flat = jnp.pad(rep_flat, (0, N_DEV * PACK_R_ROWS * LANES - rep_flat.size))
    dx_ab, c_all, g_uq, rep_all = _dx_tail(dhs[:2], ws[:2], dx_res, d_wc, p_uq,
                                           rep_flat.reshape(N_DEV, PACK_R_ROWS, LANES))
    dx = _mm(dhs[2], ws[2], tb=True, add=dx_ab, tm=512, tn=D_MODEL, name="dx_c")
    g_in, g_oa, g_ob, g_out = _sum_landed(landed, c_all)
    rep_sum = rep_all.reshape(-1)
    grads, pos = dict(w_in=g_in, w_uq=g_uq, w_oa=g_oa, w_ob=g_ob, w_out=g_out), 0
    for nm in _REPLICATED:
        grads[nm] = rep_sum[pos:pos + weights[nm].size]
        pos += weights[nm].size
    loss = rep_sum[pos]
    grads = {nm: grads[nm].reshape(weights[nm].shape) for nm in _NAMES}
    deltas, new_m, new_v = _adamw_all([weights[nm] for nm in _NAMES], [grads[nm] for nm in _NAMES],
                                      [moms[nm] for nm in _NAMES], [vars_[nm] for nm in _NAMES])
    return (loss, dx.reshape(1, SEQ, D_MODEL), *[grads[nm] for nm in _NAMES], *deltas, *new_m, *new_v)
```

```python
import math

import jax
import jax.numpy as jnp
from jax import lax
from jax.experimental import pallas as pl
from jax.experimental.pallas import tpu as pltpu

F32 = jnp.float32
BF16 = jnp.bfloat16

D_MODEL = 1024
SEQ = 2048
N_DEV = 8
MLA_HEADS = 8
Q_LORA_RANK = 384
KV_LORA_RANK = 128
QK_NOPE_DIM = 64
QK_ROPE_DIM = 32
V_HEAD_DIM = 64
QK_HEAD_DIM = QK_NOPE_DIM + QK_ROPE_DIM
MLA_WIDTH = MLA_HEADS * V_HEAD_DIM
ROPE_THETA = 10000.0
SGU_GROUPS = 8
SGU_GROUP_DIM = 64
SGU_WIDTH = SGU_GROUPS * SGU_GROUP_DIM
CHUNK = 128
RMS_EPS = 1e-6
LN_EPS = 1e-5
DN_ALPHA = 2.0 ** 0.25
IN_WIDTH = 4640
ATTN_SCALE = QK_HEAD_DIM ** -0.5

ADAM_LR = 0.001
ADAM_B1 = 0.9
ADAM_B2 = 0.999
ADAM_EPS = 1e-08
ADAM_WD = 0.01
ADAM_STEP = 10

LANES = 128
HEAD_PAD = 128
ROPE_LO = QK_NOPE_DIM
ROPE_MID = ROPE_LO + QK_ROPE_DIM // 2
ROPE_HI = ROPE_LO + QK_ROPE_DIM
CQ_PAD = 512

SEG_A = 2560
SEG_B = 1536
SEG_C = 768

PACK_R_ROWS = 272
PACK_ALIGN = 8 * LANES
VMEM_BIG = 56 * 1024 * 1024
VMEM_MID = 40 * 1024 * 1024


def _sigmoid(x):
    return 1.0 / (1.0 + jnp.exp(-x))


def _gelu_and_grad(x):
    c0 = math.sqrt(2.0 / math.pi)
    x2 = x * x
    t = jnp.tanh(c0 * (x + 0.044715 * x * x2))
    g = 0.5 * x * (1.0 + t)
    dg = 0.5 * (1.0 + t) + 0.5 * x * (1.0 - t * t) * (c0 * (1.0 + 3.0 * 0.044715 * x2))
    return g, dg


def _dot(a, b, dims):
    return lax.dot_general(a, b, (dims, ((), ())), preferred_element_type=F32)


_NN = ((1,), (0,))
_NT = ((1,), (1,))
_TN = ((0,), (0,))


def _store_grad(dh_ref, db_ref, col, val):
    cols = slice(col, col + val.shape[1])
    dh_ref[:, cols] = val.astype(BF16)
    db_ref[:, cols] += jnp.sum(val, axis=0, keepdims=True)


def _mm(a, b, *, tb=False, bias=None, add=None, out_dtype=F32, own=(), parts=(), tm, tn, name):
    m, k = a.shape
    n = b.shape[0] if tb else b.shape[1]
    assert m % tm == 0 and n % tn == 0 and not (own and parts)
    dims = _NT if tb else _NN
    nown = len(own) + len(parts)
    nm = m // tm
    nsteps = (n // tn) * nm

    def body(*refs):
        a_ref, b_ref = refs[0], refs[1]
        pos = 2
        r = _dot(a_ref[...], b_ref[...], dims)
        if bias is not None:
            r = r + refs[pos][...]; pos += 1
        if add is not None:
            r = r + refs[pos][...]; pos += 1
        own_refs = refs[pos:pos + nown]; pos += nown
        refs[pos][...] = r.astype(out_dtype)
        if nown:
            gat_refs = refs[pos + 1:pos + 1 + nown]
            send_sems, recv_sems, local_sems = refs[pos + 1 + nown:]
            step = pl.program_id(0) * nm + pl.program_id(1)
            if own:
                _gather_behind(own_refs, gat_refs, send_sems, recv_sems, local_sems, step, nsteps - 2, nsteps - 1)
            else:
                exchange = _exchange_parts(own_refs, gat_refs, send_sems, recv_sems, local_sems)
                _exchange_start(step == 0, exchange)
                _exchange_finish(step == nsteps - 1, exchange)

    b_spec = pl.BlockSpec((tn, k), lambda j, i: (j, 0)) if tb else pl.BlockSpec((k, tn), lambda j, i: (0, j))
    in_specs, args = [pl.BlockSpec((tm, k), lambda j, i: (i, 0)), b_spec], [a, b]
    if bias is not None:
        in_specs.append(pl.BlockSpec((1, tn), lambda j, i: (0, j))); args.append(bias)
    if add is not None:
        in_specs.append(pl.BlockSpec((tm, tn), lambda j, i: (i, j))); args.append(add)
    hbm = pl.BlockSpec(memory_space=pl.ANY)
    res = pl.pallas_call(
        body, name=name, grid=(n // tn, nm), in_specs=in_specs + [hbm] * nown,
        out_specs=[pl.BlockSpec((tm, tn), lambda j, i: (i, j))] + [hbm] * nown,
        out_shape=[jax.ShapeDtypeStruct((m, n), out_dtype)]
        + [jax.ShapeDtypeStruct((N_DEV,) + o.shape, o.dtype) for o in own]
        + [jax.ShapeDtypeStruct(p.shape, p.dtype) for p in parts],
        scratch_shapes=_exchange_sems(nown) if nown else [],
        compiler_params=pltpu.CompilerParams(dimension_semantics=("arbitrary", "arbitrary"), vmem_limit_bytes=VMEM_BIG),
    )(*args, *own, *parts)
    return (res[0], res[1:]) if nown else res[0]


def _rope(x, c, sa, sb):
    return x * c + pltpu.roll(x, LANES - 16, 1) * sa + pltpu.roll(x, 16, 1) * sb


def _rope_t(dy, c, sa, sb):
    return dy * c + pltpu.roll(dy * sa, 16, 1) + pltpu.roll(dy * sb, LANES - 16, 1)


def _mla_prep(h_c, gq, gkv, wq, wkn, wvx, c_t, sa_t, sb_t):
    tm = 256
    hw = MLA_HEADS * HEAD_PAD

    def body(cq_ref, ckv_ref, kpe_ref, gq_ref, gkv_ref, wq_ref, wkn_ref, wvx_ref, c_ref, sa_ref, sb_ref,
             q_ref, k_ref, kt_ref, vx_ref, vxt_ref):
        c, sa, sb = c_ref[...], sa_ref[...], sb_ref[...]
        cq = cq_ref[...]
        rq = lax.rsqrt(jnp.sum(cq * cq, axis=1, keepdims=True) * (1.0 / Q_LORA_RANK) + RMS_EPS)
        cqn = ((cq * rq) * gq_ref[...]).astype(BF16)
        qall = _dot(cqn, wq_ref[...], _NN)
        for h in range(MLA_HEADS):
            sl = slice(HEAD_PAD * h, HEAD_PAD * (h + 1))
            q_ref[:, sl] = (_rope(qall[:, sl], c, sa, sb) * ATTN_SCALE).astype(BF16)
        ckv = ckv_ref[...]
        rkv = lax.rsqrt(jnp.sum(ckv * ckv, axis=1, keepdims=True) * (1.0 / KV_LORA_RANK) + RMS_EPS)
        ckvn = ((ckv * rkv) * gkv_ref[...]).astype(BF16)
        knall = _dot(ckvn, wkn_ref[...], _NN)
        vall = _dot(ckvn, wvx_ref[...], _NN)
        kper = _rope(kpe_ref[...], c, sa, sb)
        ones_half = (lax.broadcasted_iota(jnp.int32, (tm, HEAD_PAD), 1) >= V_HEAD_DIM).astype(F32)
        for h in range(MLA_HEADS):
            sl = slice(HEAD_PAD * h, HEAD_PAD * (h + 1))
            kh = knall[:, sl] + kper
            vh = vall[:, sl] + ones_half
            k_ref[:, sl] = kh.astype(BF16)
            kt_ref[sl, :] = kh.T.astype(BF16)
            vx_ref[:, sl] = vh.astype(BF16)
            vxt_ref[sl, :] = vh.T.astype(BF16)

    full = lambda shape: pl.BlockSpec(shape, lambda i: (0, 0))
    tab = pl.BlockSpec((tm, LANES), lambda i: (i, 0))
    row = pl.BlockSpec((tm, hw), lambda i: (i, 0))
    col = pl.BlockSpec((hw, tm), lambda i: (0, i))
    return pl.pallas_call(
        body, name="mla_prep", grid=(SEQ // tm,),
        in_specs=[pl.BlockSpec((tm, CQ_PAD), lambda i: (i, 0)),
                  pl.BlockSpec((tm, LANES), lambda i: (i, CQ_PAD // LANES)),
                  pl.BlockSpec((tm, LANES), lambda i: (i, CQ_PAD // LANES + 1)),
                  full((1, CQ_PAD)), full((1, KV_LORA_RANK)),
                  full((CQ_PAD, hw)), full((KV_LORA_RANK, hw)), full((KV_LORA_RANK, hw)), tab, tab, tab],
        out_specs=[row, row, col, row, col],
        out_shape=[jax.ShapeDtypeStruct((SEQ, hw), BF16), jax.ShapeDtypeStruct((SEQ, hw), BF16),
                   jax.ShapeDtypeStruct((hw, SEQ), BF16), jax.ShapeDtypeStruct((SEQ, hw), BF16),
                   jax.ShapeDtypeStruct((hw, SEQ), BF16)],
        compiler_params=pltpu.CompilerParams(dimension_semantics=("arbitrary",), vmem_limit_bytes=VMEM_MID),
    )(h_c, h_c, h_c, gq, gkv, wq, wkn, wvx, c_t, sa_t, sb_t)


ATT_T = 512
ATT_STRIP = 64


def _attn_fwd(q, kt, vx, own):
    t, rs = ATT_T, ATT_STRIP
    nown = len(own)
    nq = SEQ // t
    nsteps = (MLA_HEADS // 2) * nq

    def body(q_ref, kt_ref, vx_ref, *rest):
        own_refs, (o_ref, l_ref), gat_refs = rest[:nown], rest[nown:nown + 2], rest[nown + 2:2 * nown + 2]
        s_scr, p_scr, m_scr, a_scr, acc_scr, send_sems, recv_sems, local_sems = rest[2 * nown + 2:]
        qi = pl.program_id(1)
        _gather_behind(own_refs, gat_refs, send_sems, recv_sems, local_sems, pl.program_id(0) * nq + qi,
                       nsteps - 2, nsteps - 1)
        lane = lax.broadcasted_iota(jnp.int32, (t, LANES), 1)
        m_scr[...] = jnp.full((2, t, LANES), -1e30, F32)
        acc_scr[...] = jnp.zeros((2, t, LANES), F32)

        def block(j, masked):
            off = pl.multiple_of(j * t, t)
            for a in range(2):
                sl = slice(HEAD_PAD * a, HEAD_PAD * (a + 1))
                s_scr[a] = _dot(q_ref[:, sl], kt_ref[sl, pl.ds(off, t)], _NN)
                for r in range(t // rs):
                    rows = slice(rs * r, rs * (r + 1))
                    s = s_scr[a, rows, :]
                    if masked:
                        rowi = lax.broadcasted_iota(jnp.int32, (rs, t), 0) + rs * r
                        coli = lax.broadcasted_iota(jnp.int32, (rs, t), 1)
                        s = jnp.where(coli <= rowi, s, -1e30)
                    m_old = m_scr[a, rows, :]
                    m_new = jnp.maximum(m_old, jnp.max(s, axis=1, keepdims=True))
                    p_scr[a, rows, :] = jnp.exp(s - m_new[:, :1]).astype(BF16)
                    a_scr[a, rows, :] = jnp.exp(m_old - m_new)
                    m_scr[a, rows, :] = m_new
                acc_scr[a] = acc_scr[a] * a_scr[a] + _dot(p_scr[a], vx_ref[pl.ds(off, t), sl], _NN)

        def step(j, carry):
            block(j, False)
            return carry
        lax.fori_loop(0, qi, step, 0)
        block(qi, True)
        res = []
        for a in range(2):
            acc = acc_scr[a]
            l = acc[:, V_HEAD_DIM:V_HEAD_DIM + 1]
            res.append((acc / l, m_scr[a] + jnp.log(l)))
        o_ref[...] = jnp.where(lane < V_HEAD_DIM, res[0][0], pltpu.roll(res[1][0], V_HEAD_DIM, 1))
        l_ref[...] = jnp.where(lane < V_HEAD_DIM, res[0][1], res[1][1])

    hbm = pl.BlockSpec(memory_space=pl.ANY)
    res = pl.pallas_call(
        body, name="attn_fwd", grid=(MLA_HEADS // 2, nq),
        in_specs=[pl.BlockSpec((t, 2 * HEAD_PAD), lambda p, i: (i, p)),
                  pl.BlockSpec((2 * HEAD_PAD, SEQ), lambda p, i: (p, 0)),
                  pl.BlockSpec((SEQ, 2 * HEAD_PAD), lambda p, i: (0, p))] + [hbm] * nown,
        out_specs=[pl.BlockSpec((t, LANES), lambda p, i: (i, p)),
                   pl.BlockSpec((t, LANES), lambda p, i: (i, p))] + [hbm] * nown,
        out_shape=[jax.ShapeDtypeStruct((SEQ, MLA_WIDTH), F32), jax.ShapeDtypeStruct((SEQ, MLA_WIDTH), F32)]
        + [jax.ShapeDtypeStruct((N_DEV,) + a.shape, a.dtype) for a in own],
        scratch_shapes=[pltpu.VMEM((2, t, t), F32), pltpu.VMEM((2, t, t), BF16), pltpu.VMEM((2, t, LANES), F32),
                        pltpu.VMEM((2, t, LANES), F32), pltpu.VMEM((2, t, LANES), F32)] + _exchange_sems(nown),
        compiler_params=pltpu.CompilerParams(dimension_semantics=("arbitrary", "arbitrary"), vmem_limit_bytes=VMEM_MID),
    )(q, kt, vx, *own)
    return res[0], res[1], res[2:]


def _exchange_parts(parts, lands, send_sems, recv_sems, local_sems):
    x, y, c = _mesh_pos()
    me = 4 * x + 2 * y + c
    peers = [(x, y, 1 - c), (1 - x, y, c), (x, 1 - y, c), (1 - x, 1 - y, c),
             (1 - x, y, 1 - c), (x, 1 - y, 1 - c), (1 - x, 1 - y, 1 - c)]
    remote, local = [], []
    for a, (part, land) in enumerate(zip(parts, lands)):
        for k, peer in enumerate(peers):
            t = 4 * peer[0] + 2 * peer[1] + peer[2]
            remote.append(_remote(part.at[t], land.at[me], send_sems, recv_sems, 7 * a + k, peer))
        local.append(pltpu.make_async_copy(part.at[me], land.at[me], local_sems.at[a]))
    return remote, local


def _exchange_start(first_step, exchange):
    remote, local = exchange

    @pl.when(first_step)
    def _():
        for cp in remote + local:
            cp.start()


def _exchange_finish(last_step, exchange):
    remote, local = exchange

    @pl.when(last_step)
    def _():
        for cp in remote:
            cp.wait_recv()
        for cp in remote:
            cp.wait_send()
        for cp in local:
            cp.wait()


def _exchange_sems(npart):
    return [pltpu.SemaphoreType.DMA((7 * npart,)), pltpu.SemaphoreType.DMA((7 * npart,)),
            pltpu.SemaphoreType.DMA((npart,))]


def _attn_bwd(q, kt, k, vxt, d_o, o, lse, parts):
    t, rs = ATT_T, ATT_STRIP
    nq = SEQ // t
    npart = len(parts)
    nsteps = MLA_HEADS // 2

    def body(q_ref, kt_ref, k_ref, vxt_ref, do_ref, o_ref, l_ref, *rest):
        part_refs, rest = rest[:npart], rest[npart:]
        dq_ref, dk_ref, dv_ref = rest[:3]
        land_refs, rest = rest[3:3 + npart], rest[3 + npart:]
        s_scr, dp_scr, p_scr, ds_scr, st_scr, send_sems, recv_sems, local_sems = rest
        exchange = _exchange_parts(part_refs, land_refs, send_sems, recv_sems, local_sems)
        _exchange_start(pl.program_id(0) == 0, exchange)
        dk_ref[...] = jnp.zeros_like(dk_ref)
        dv_ref[...] = jnp.zeros_like(dv_ref)
        lane = lax.broadcasted_iota(jnp.int32, (t, LANES), 1)

        def qtile(i, carry):
            ioff = pl.multiple_of(i * t, t)
            do_i = do_ref[pl.ds(ioff, t), :]
            o_i = o_ref[pl.ds(ioff, t), :]
            l_i = l_ref[pl.ds(ioff, t), :]
            for a in range(2):
                sl = slice(HEAD_PAD * a, HEAD_PAD * (a + 1))
                sel = (lane < V_HEAD_DIM) if a == 0 else (lane >= V_HEAD_DIM)
                doa = jnp.where(sel, do_i, 0.0)
                oa = o_i
                if a == 1:
                    doa = pltpu.roll(doa, V_HEAD_DIM, 1)
                    oa = pltpu.roll(o_i, V_HEAD_DIM, 1)
                st_scr[0] = jnp.broadcast_to(jnp.sum(doa * oa, axis=1, keepdims=True), (t, LANES))
                st_scr[1] = jnp.broadcast_to(l_i[:, V_HEAD_DIM * a:V_HEAD_DIM * a + 1], (t, LANES))
                doa_bf = doa.astype(BF16)
                qa = q_ref[pl.ds(ioff, t), sl]

                def block(j, masked, dq_acc, sl=sl, qa=qa, doa_bf=doa_bf):
                    joff = pl.multiple_of(j * t, t)
                    s_scr[...] = _dot(qa, kt_ref[sl, pl.ds(joff, t)], _NN)
                    dp_scr[...] = _dot(doa_bf, vxt_ref[sl, pl.ds(joff, t)], _NN)
                    for r in range(t // rs):
                        rows = slice(rs * r, rs * (r + 1))
                        p = jnp.exp(s_scr[rows, :] - st_scr[1, rows, :1])
                        if masked:
                            rowi = lax.broadcasted_iota(jnp.int32, (rs, t), 0) + rs * r
                            coli = lax.broadcasted_iota(jnp.int32, (rs, t), 1)
                            p = jnp.where(coli <= rowi, p, 0.0)
                        p_scr[rows, :] = p.astype(BF16)
                        ds_scr[rows, :] = (p * (dp_scr[rows, :] - st_scr[0, rows, :1])).astype(BF16)
                    dk_ref[pl.ds(joff, t), sl] += _dot(ds_scr[...], qa, _TN)
                    dv_ref[pl.ds(joff, t), sl] += _dot(p_scr[...], doa_bf, _TN)
                    return dq_acc + _dot(ds_scr[...], k_ref[pl.ds(joff, t), sl], _NN)

                dq_acc = lax.fori_loop(0, i, lambda j, acc: block(j, False, acc), jnp.zeros((t, HEAD_PAD), F32))
                dq_ref[pl.ds(ioff, t), sl] = block(i, True, dq_acc)
            return carry

        lax.fori_loop(0, nq, qtile, 0)
        _exchange_finish(pl.program_id(0) == nsteps - 1, exchange)

    hw = MLA_HEADS * HEAD_PAD
    wide = pl.BlockSpec((SEQ, 2 * HEAD_PAD), lambda p: (0, p))
    wide_t = pl.BlockSpec((2 * HEAD_PAD, SEQ), lambda p: (p, 0))
    narrow = pl.BlockSpec((SEQ, LANES), lambda p: (0, p))
    hbm = pl.BlockSpec(memory_space=pl.ANY)
    res = pl.pallas_call(
        body, name="attn_bwd", grid=(nsteps,),
        in_specs=[wide, wide_t, wide, wide_t, narrow, narrow, narrow] + [hbm] * npart,
        out_specs=[wide, wide, wide] + [hbm] * npart,
        out_shape=[jax.ShapeDtypeStruct((SEQ, hw), F32)] * 3 + [jax.ShapeDtypeStruct(p.shape, p.dtype) for p in parts],
        scratch_shapes=[pltpu.VMEM((t, t), F32), pltpu.VMEM((t, t), F32), pltpu.VMEM((t, t), BF16),
                        pltpu.VMEM((t, t), BF16), pltpu.VMEM((2, t, LANES), F32)] + _exchange_sems(npart),
        compiler_params=pltpu.CompilerParams(dimension_semantics=("arbitrary",), vmem_limit_bytes=VMEM_BIG),
    )(q, kt, k, vxt, d_o, o, lse, *parts)
    return res[0], res[1], res[2], res[3:]


def _sgu_math(u, v, zb, lg, lb, ws_ref, bias):
    ug, dug = _gelu_and_grad(u)
    vg, dvg = _gelu_and_grad(v)
    mu = jnp.mean(vg, axis=1, keepdims=True)
    xc = vg - mu
    rstd = lax.rsqrt(jnp.mean(xc * xc, axis=1, keepdims=True) + LN_EPS)
    xh = xc * rstd
    vn_bf = (xh * lg + lb).astype(BF16)
    grp = lax.broadcasted_iota(jnp.int32, (CHUNK, SGU_WIDTH), 1) // SGU_GROUP_DIM
    r_i = lax.broadcasted_iota(jnp.int32, (CHUNK, CHUNK), 0)
    c_i = lax.broadcasted_iota(jnp.int32, (CHUNK, CHUNK), 1)
    tri, tri_t = r_i >= c_i, r_i <= c_i
    mixed = bias
    for g in range(SGU_GROUPS):
        wt = jnp.where(tri, ws_ref[g], 0.0).astype(BF16)
        mixed = mixed + jnp.where(grp == g, _dot(wt, vn_bf, _NN), 0.0)
    sb = _sigmoid(zb)
    return ug, dug, dvg, rstd, xh, vn_bf, grp, tri, tri_t, mixed, sb


def _sgu_fwd(h_b, lg, lb, w_s, bias_full):
    def body(u_ref, v_ref, zb_ref, lg_ref, lb_ref, ws_ref, bias_ref, yb_ref):
        zb = zb_ref[...]
        ug, _, _, _, _, _, _, _, _, mixed, sb = _sgu_math(u_ref[...], v_ref[...], zb, lg_ref[...], lb_ref[...],
                                                       ws_ref, bias_ref[...])
        yb_ref[...] = (ug * mixed) * (zb * sb)

    blk = lambda c: pl.BlockSpec((CHUNK, SGU_WIDTH), lambda i, c=c: (i, c))
    full2 = lambda shape: pl.BlockSpec(shape, lambda i: (0, 0))
    return pl.pallas_call(
        body, name="sgu_fwd", grid=(SEQ // CHUNK,),
        in_specs=[blk(0), blk(1), blk(2), full2((1, SGU_WIDTH)), full2((1, SGU_WIDTH)),
                  pl.BlockSpec((SGU_GROUPS, CHUNK, CHUNK), lambda i: (0, 0, 0)), full2((CHUNK, SGU_WIDTH))],
        out_specs=pl.BlockSpec((CHUNK, SGU_WIDTH), lambda i: (i, 0)),
        out_shape=jax.ShapeDtypeStruct((SEQ, SGU_WIDTH), F32),
        compiler_params=pltpu.CompilerParams(dimension_semantics=("arbitrary",)),
    )(h_b, h_b, h_b, lg, lb, w_s, bias_full)


def _sgu_bwd(h_b, d_yb, lg, lb, w_s, w_st, bias_full, parts):
    nsteps = SEQ // CHUNK
    npart = len(parts)

    def body(u_ref, v_ref, zb_ref, dyb_ref, lg_ref, lb_ref, ws_ref, wst_ref, bias_ref, *rest):
        part_refs, rest = rest[:npart], rest[npart:]
        dhb_ref, dws_ref, dbs_ref, dlg_ref, dlb_ref, dbb_ref = rest[:6]
        land_refs, (dbias_acc, send_sems, recv_sems, local_sems) = rest[6:6 + npart], rest[6 + npart:]
        step = pl.program_id(0)
        exchange = _exchange_parts(part_refs, land_refs, send_sems, recv_sems, local_sems)
        _exchange_start(step == 0, exchange)

        @pl.when(step == 0)
        def _():
            dbb_ref[...] = jnp.zeros_like(dbb_ref)
            dws_ref[...] = jnp.zeros_like(dws_ref)
            dlg_ref[...] = jnp.zeros_like(dlg_ref)
            dlb_ref[...] = jnp.zeros_like(dlb_ref)
            dbias_acc[...] = jnp.zeros_like(dbias_acc)

        zb = zb_ref[...]
        lg = lg_ref[...]
        ug, dug, dvg, rstd, xh, vn_bf, grp, tri, tri_t, mixed, sb = _sgu_math(
            u_ref[...], v_ref[...], zb, lg, lb_ref[...], ws_ref, bias_ref[...])
        dyb = dyb_ref[...]
        dsgu = dyb * (zb * sb)
        dzb = dyb * (ug * mixed) * (sb * (1.0 + zb * (1.0 - sb)))
        du = dsgu * mixed * dug
        dmixed = dsgu * ug
        dbias_acc[...] += dmixed
        dvn = jnp.zeros((CHUNK, SGU_WIDTH), F32)
        for g in range(SGU_GROUPS):
            dm_g = jnp.where(grp == g, dmixed, 0.0).astype(BF16)
            wtt = jnp.where(tri_t, wst_ref[g], 0.0).astype(BF16)
            dvn = dvn + _dot(wtt, dm_g, _NN)
            dws_ref[g] += jnp.where(tri, _dot(dm_g, vn_bf, _NT), 0.0)
        dlg_ref[...] += jnp.sum(dvn * xh, axis=0, keepdims=True)
        dlb_ref[...] += jnp.sum(dvn, axis=0, keepdims=True)
        dxh = dvn * lg
        dvgel = rstd * (dxh - jnp.mean(dxh, axis=1, keepdims=True) - xh * jnp.mean(dxh * xh, axis=1, keepdims=True))
        _store_grad(dhb_ref, dbb_ref, 0, du)
        _store_grad(dhb_ref, dbb_ref, SGU_WIDTH, dvgel * dvg)
        _store_grad(dhb_ref, dbb_ref, 2 * SGU_WIDTH, dzb)

        @pl.when(step == nsteps - 1)
        def _():
            acc = dbias_acc[...]
            lane = lax.broadcasted_iota(jnp.int32, (CHUNK, LANES), 1)
            out = jnp.zeros((CHUNK, LANES), F32)
            for g in range(SGU_GROUPS):
                sg = jnp.sum(jnp.where(grp == g, acc, 0.0), axis=1, keepdims=True)
                out = jnp.where(lane == g, sg, out)
            dbs_ref[...] = out

        _exchange_finish(step == nsteps - 1, exchange)

    blk = lambda c: pl.BlockSpec((CHUNK, SGU_WIDTH), lambda i, c=c: (i, c))
    full2 = lambda shape: pl.BlockSpec(shape, lambda i: (0, 0))
    full3 = pl.BlockSpec((SGU_GROUPS, CHUNK, CHUNK), lambda i: (0, 0, 0))
    hbm = pl.BlockSpec(memory_space=pl.ANY)
    res = pl.pallas_call(
        body, name="sgu_bwd", grid=(nsteps,),
        in_specs=[blk(0), blk(1), blk(2), pl.BlockSpec((CHUNK, SGU_WIDTH), lambda i: (i, 0)),
                  full2((1, SGU_WIDTH)), full2((1, SGU_WIDTH)), full3, full3, full2((CHUNK, SGU_WIDTH))] + [hbm] * npart,
        out_specs=[pl.BlockSpec((CHUNK, SEG_B), lambda i: (i, 0)), full3, full2((CHUNK, LANES)),
                   full2((1, SGU_WIDTH)), full2((1, SGU_WIDTH)), full2((1, SEG_B))] + [hbm] * npart,
        out_shape=[jax.ShapeDtypeStruct((SEQ, SEG_B), BF16),
                   jax.ShapeDtypeStruct((SGU_GROUPS, CHUNK, CHUNK), F32),
                   jax.ShapeDtypeStruct((CHUNK, LANES), F32),
                   jax.ShapeDtypeStruct((1, SGU_WIDTH), F32), jax.ShapeDtypeStruct((1, SGU_WIDTH), F32),
                   jax.ShapeDtypeStruct((1, SEG_B), F32)] + [jax.ShapeDtypeStruct(p.shape, p.dtype) for p in parts],
        scratch_shapes=[pltpu.VMEM((CHUNK, SGU_WIDTH), F32)] + _exchange_sems(npart),
        compiler_params=pltpu.CompilerParams(dimension_semantics=("arbitrary",)),
    )(h_b, h_b, h_b, d_yb, lg, lb, w_s, w_st, bias_full, *parts)
    return res[:6], res[6:]


def _merge(x, o, h_a, y_b, target, w_oa, w_ob, w_out, ln_g, ln_b):
    tm = 256
    nsteps = SEQ // tm

    def body(x_ref, o_ref, ga_ref, gb_ref, za_ref, yb_ref, tgt_ref, woa_ref, wob_ref, wout_ref, lng_ref, lnb_ref,
             loss_ref, dxr_ref, dha_ref, do_ref, dyb_ref, poa_ref, pob_ref, pout_ref, dlng_ref, dlnb_ref, dba_ref,
             dwoa_ref, dwob_ref, dwout_ref):
        step = pl.program_id(0)

        @pl.when(step == 0)
        def _():
            for r in (loss_ref, dwoa_ref, dwob_ref, dwout_ref, dlng_ref, dlnb_ref, dba_ref):
                r[...] = jnp.zeros_like(r)

        o = o_ref[...]
        za = za_ref[...]
        sa = _sigmoid(za)
        ya_bf = (o * (za * sa)).astype(BF16)
        yb_bf = yb_ref[...].astype(BF16)
        woa, wob, wout = woa_ref[...], wob_ref[...], wout_ref[...]
        pa = _dot(ya_bf, woa, _NN)
        pb = _dot(yb_bf, wob, _NN)
        sga = _sigmoid(ga_ref[...])
        sgb = _sigmoid(gb_ref[...])
        merged_bf = (sga * pa + sgb * pb).astype(BF16)
        r = DN_ALPHA * x_ref[...] + _dot(merged_bf, wout, _NN)
        mu = jnp.mean(r, axis=1, keepdims=True)
        rc = r - mu
        rstd = lax.rsqrt(jnp.mean(rc * rc, axis=1, keepdims=True) + LN_EPS)
        xh = rc * rstd
        lng = lng_ref[...]
        y = xh * lng + lnb_ref[...]
        e = y - tgt_ref[...]
        loss_ref[...] += 0.5 * jnp.sum(jnp.sum(e * e, axis=1, keepdims=True) * (1.0 / D_MODEL), axis=0, keepdims=True)

        dy = e * (1.0 / D_MODEL)
        dlng_ref[...] += jnp.sum(dy * xh, axis=0, keepdims=True)
        dlnb_ref[...] += jnp.sum(dy, axis=0, keepdims=True)
        dxh = dy * lng
        dr = rstd * (dxh - jnp.mean(dxh, axis=1, keepdims=True) - xh * jnp.mean(dxh * xh, axis=1, keepdims=True))
        dxr_ref[...] = DN_ALPHA * dr
        dr_bf = dr.astype(BF16)
        dwout_ref[...] += _dot(merged_bf, dr_bf, _TN)
        dmerged = _dot(dr_bf, wout, _NT)
        dpa_bf = (dmerged * sga).astype(BF16)
        dpb_bf = (dmerged * sgb).astype(BF16)
        _store_grad(dha_ref, dba_ref, 0, dmerged * pa * (sga * (1.0 - sga)))
        _store_grad(dha_ref, dba_ref, D_MODEL, dmerged * pb * (sgb * (1.0 - sgb)))
        dwoa_ref[...] += _dot(ya_bf, dpa_bf, _TN)
        dwob_ref[...] += _dot(yb_bf, dpb_bf, _TN)
        dya = _dot(dpa_bf, woa, _NT)
        dyb_ref[...] = _dot(dpb_bf, wob, _NT)
        do_ref[...] = dya * (za * sa)
        _store_grad(dha_ref, dba_ref, 2 * D_MODEL, dya * o * (sa * (1.0 + za * (1.0 - sa))))

        @pl.when(step == nsteps - 1)
        def _():
            cols = D_MODEL // N_DEV
            for j in range(N_DEV):
                poa_ref[j] = dwoa_ref[:, cols * j:cols * (j + 1)].astype(BF16)
                pob_ref[j] = dwob_ref[:, cols * j:cols * (j + 1)].astype(BF16)
                pout_ref[j] = dwout_ref[cols * j:cols * (j + 1), :].astype(BF16)

    row = lambda w, c=0: pl.BlockSpec((tm, w), lambda i, c=c: (i, c))
    full = lambda shape: pl.BlockSpec(shape, lambda i: (0, 0))
    full3 = lambda shape: pl.BlockSpec(shape, lambda i: (0, 0, 0))
    return pl.pallas_call(
        body, name="merge", grid=(nsteps,),
        in_specs=[row(D_MODEL), row(MLA_WIDTH), row(D_MODEL, 0), row(D_MODEL, 1), row(MLA_WIDTH, 4), row(SGU_WIDTH),
                  row(D_MODEL), full((MLA_WIDTH, D_MODEL)), full((SGU_WIDTH, D_MODEL)), full((D_MODEL, D_MODEL)),
                  full((1, D_MODEL)), full((1, D_MODEL))],
        out_specs=[full((1, LANES)), row(D_MODEL), row(SEG_A), row(MLA_WIDTH), row(SGU_WIDTH),
                   full3((N_DEV, MLA_WIDTH, D_MODEL // N_DEV)), full3((N_DEV, SGU_WIDTH, D_MODEL // N_DEV)),
                   full3((N_DEV, D_MODEL // N_DEV, D_MODEL)), full((1, D_MODEL)), full((1, D_MODEL)), full((1, SEG_A))],
        out_shape=[jax.ShapeDtypeStruct((1, LANES), F32),
                   jax.ShapeDtypeStruct((SEQ, D_MODEL), F32), jax.ShapeDtypeStruct((SEQ, SEG_A), BF16),
                   jax.ShapeDtypeStruct((SEQ, MLA_WIDTH), F32), jax.ShapeDtypeStruct((SEQ, SGU_WIDTH), F32),
                   jax.ShapeDtypeStruct((N_DEV, MLA_WIDTH, D_MODEL // N_DEV), BF16),
                   jax.ShapeDtypeStruct((N_DEV, SGU_WIDTH, D_MODEL // N_DEV), BF16),
                   jax.ShapeDtypeStruct((N_DEV, D_MODEL // N_DEV, D_MODEL), BF16),
                   jax.ShapeDtypeStruct((1, D_MODEL), F32), jax.ShapeDtypeStruct((1, D_MODEL), F32),
                   jax.ShapeDtypeStruct((1, SEG_A), F32)],
        scratch_shapes=[pltpu.VMEM((MLA_WIDTH, D_MODEL), F32), pltpu.VMEM((SGU_WIDTH, D_MODEL), F32),
                        pltpu.VMEM((D_MODEL, D_MODEL), F32)],
        compiler_params=pltpu.CompilerParams(dimension_semantics=("arbitrary",), vmem_limit_bytes=VMEM_BIG),
    )(x, o, h_a, h_a, h_a, y_b, target, w_oa, w_ob, w_out, ln_g, ln_b)


def _mla_bwd(dq, dk, dv, h_c, gq, gkv, wq, wkn, wv, c_t, sa_t, sb_t, parts):
    tm = 256
    hw = MLA_HEADS * HEAD_PAD
    npart = len(parts)
    nsteps = SEQ // tm

    def body(dq_ref, dk_ref, dv_ref, cq_ref, ckv_ref, gq_ref, gkv_ref, wq_ref, wkn_ref, wv_ref, c_ref, sa_ref, sb_ref,
             *rest):
        part_refs, rest = rest[:npart], rest[npart:]
        dhc_ref, puq_ref, dwkn_ref, dwv_ref, dgq_ref, dgkv_ref, dbc_ref = rest[:7]
        land_refs, (pre_ref, dwq_ref, send_sems, recv_sems, local_sems) = rest[7:7 + npart], rest[7 + npart:]
        exchange = _exchange_parts(part_refs, land_refs, send_sems, recv_sems, local_sems)
        _exchange_start(pl.program_id(0) == 0, exchange)
        _exchange_finish(pl.program_id(0) == nsteps - 1, exchange)

        @pl.when(pl.program_id(0) == 0)
        def _():
            for r in (dwq_ref, dwkn_ref, dwv_ref, dgq_ref, dgkv_ref, dbc_ref):
                r[...] = jnp.zeros_like(r)

        c, sa, sb = c_ref[...], sa_ref[...], sb_ref[...]
        lane = lax.broadcasted_iota(jnp.int32, (tm, LANES), 1)
        rope_lanes = jnp.logical_and(lane >= ROPE_LO, lane < ROPE_HI)

        cq = cq_ref[...]
        gq = gq_ref[...]
        rq = lax.rsqrt(jnp.sum(cq * cq, axis=1, keepdims=True) * (1.0 / Q_LORA_RANK) + RMS_EPS)
        nq = cq * rq
        cqn_bf = (nq * gq).astype(BF16)
        for h in range(MLA_HEADS):
            sl = slice(HEAD_PAD * h, HEAD_PAD * (h + 1))
            pre_ref[:, sl] = _rope_t(dq_ref[:, sl] * ATTN_SCALE, c, sa, sb).astype(BF16)
        dqpre_bf = pre_ref[...]
        dcqn = _dot(dqpre_bf, wq_ref[...], _NT)
        dwq_ref[...] += _dot(cqn_bf, dqpre_bf, _TN)
        dgq_ref[...] += jnp.sum(dcqn * nq, axis=0, keepdims=True)
        dnq = dcqn * gq
        _store_grad(dhc_ref, dbc_ref, 0,
                    rq * (dnq - nq * (jnp.sum(dnq * nq, axis=1, keepdims=True) * (1.0 / Q_LORA_RANK))))

        ckv = ckv_ref[...]
        gkv = gkv_ref[...]
        rkv = lax.rsqrt(jnp.sum(ckv * ckv, axis=1, keepdims=True) * (1.0 / KV_LORA_RANK) + RMS_EPS)
        nkv = ckv * rkv
        ckvn_bf = (nkv * gkv).astype(BF16)
        dk = dk_ref[...]
        dk_bf = dk.astype(BF16)
        dv_bf = dv_ref[...].astype(BF16)
        dckvn = _dot(dk_bf, wkn_ref[...], _NT) + _dot(dv_bf, wv_ref[...], _NT)
        dwkn_ref[...] += _dot(ckvn_bf, dk_bf, _TN)
        dwv_ref[...] += _dot(ckvn_bf, dv_bf, _TN)
        dgkv_ref[...] += jnp.sum(dckvn * nkv, axis=0, keepdims=True)
        dnkv = dckvn * gkv
        _store_grad(dhc_ref, dbc_ref, CQ_PAD, rkv * (
            dnkv - nkv * (jnp.sum(dnkv * nkv, axis=1, keepdims=True) * (1.0 / KV_LORA_RANK))))
        dkpe = jnp.zeros((tm, LANES), F32)
        for h in range(MLA_HEADS):
            dkpe = dkpe + dk[:, HEAD_PAD * h:HEAD_PAD * (h + 1)]
        _store_grad(dhc_ref, dbc_ref, CQ_PAD + LANES, _rope_t(jnp.where(rope_lanes, dkpe, 0.0), c, sa, sb))

        @pl.when(pl.program_id(0) == SEQ // tm - 1)
        def _():
            rows = Q_LORA_RANK // N_DEV
            for j in range(N_DEV):
                for h in range(MLA_HEADS):
                    puq_ref[j, :, QK_HEAD_DIM * h:QK_HEAD_DIM * (h + 1)] = dwq_ref[
                        rows * j:rows * (j + 1), HEAD_PAD * h:HEAD_PAD * h + QK_HEAD_DIM].astype(BF16)

    full = lambda shape: pl.BlockSpec(shape, lambda i: (0, 0))
    row = lambda w, c=0: pl.BlockSpec((tm, w), lambda i, c=c: (i, c))
    hbm = pl.BlockSpec(memory_space=pl.ANY)
    res = pl.pallas_call(
        body, name="mla_bwd", grid=(nsteps,),
        in_specs=[row(hw), row(hw), row(hw), row(CQ_PAD, 0), row(LANES, CQ_PAD // LANES),
                  full((1, CQ_PAD)), full((1, KV_LORA_RANK)), full((CQ_PAD, hw)), full((KV_LORA_RANK, hw)),
                  full((KV_LORA_RANK, hw)), row(LANES), row(LANES), row(LANES)] + [hbm] * npart,
        out_specs=[row(SEG_C), pl.BlockSpec((N_DEV, Q_LORA_RANK // N_DEV, MLA_HEADS * QK_HEAD_DIM), lambda i: (0, 0, 0)),
                   full((KV_LORA_RANK, hw)), full((KV_LORA_RANK, hw)),
                   full((1, CQ_PAD)), full((1, KV_LORA_RANK)), full((1, SEG_C))] + [hbm] * npart,
        out_shape=[jax.ShapeDtypeStruct((SEQ, SEG_C), BF16),
                   jax.ShapeDtypeStruct((N_DEV, Q_LORA_RANK // N_DEV, MLA_HEADS * QK_HEAD_DIM), BF16),
                   jax.ShapeDtypeStruct((KV_LORA_RANK, hw), F32), jax.ShapeDtypeStruct((KV_LORA_RANK, hw), F32),
                   jax.ShapeDtypeStruct((1, CQ_PAD), F32), jax.ShapeDtypeStruct((1, KV_LORA_RANK), F32),
                   jax.ShapeDtypeStruct((1, SEG_C), F32)] + [jax.ShapeDtypeStruct(p.shape, p.dtype) for p in parts],
        scratch_shapes=[pltpu.VMEM((tm, hw), BF16), pltpu.VMEM((CQ_PAD, hw), F32)] + _exchange_sems(npart),
        compiler_params=pltpu.CompilerParams(dimension_semantics=("arbitrary",), vmem_limit_bytes=VMEM_MID),
    )(dq, dk, dv, h_c, h_c, gq, gkv, wq, wkn, wv, c_t, sa_t, sb_t, *parts)
    return res[:7], res[7:]


def _adamw_all(ws, gs, ms, vs):
    n = len(ws)
    c1 = 1.0 / (1.0 - ADAM_B1 ** ADAM_STEP)
    c2 = 1.0 / (1.0 - ADAM_B2 ** ADAM_STEP)

    def body(*refs):
        for idx in range(n):
            w, g, m, v = (refs[idx][...], refs[n + idx][...], refs[2 * n + idx][...], refs[3 * n + idx][...])
            m_new = ADAM_B1 * m + (1.0 - ADAM_B1) * g
            v_new = ADAM_B2 * v + (1.0 - ADAM_B2) * (g * g)
            delta = -ADAM_LR * ((m_new * c1) / (jnp.sqrt(v_new * c2) + ADAM_EPS) + ADAM_WD * w)
            refs[4 * n + idx][...] = delta
            refs[5 * n + idx][...] = m_new
            refs[6 * n + idx][...] = v_new

    shapes = [jax.ShapeDtypeStruct(w.shape, F32) for w in ws]
    outs = pl.pallas_call(
        body, name="adamw", out_shape=shapes * 3,
        compiler_params=pltpu.CompilerParams(vmem_limit_bytes=VMEM_BIG),
    )(*ws, *gs, *ms, *vs)
    return outs[:n], outs[n:2 * n], outs[2 * n:]


SHARD_W = IN_WIDTH // N_DEV

_PIECES = [(0, 384, 2, 0), (384, 512, 2, CQ_PAD), (512, 544, 2, CQ_PAD + LANES + ROPE_LO),
           (544, 1056, 0, 2 * D_MODEL), (1056, 1568, 1, 0), (1568, 2080, 1, SGU_WIDTH),
           (2080, 2592, 1, 2 * SGU_WIDTH), (2592, 3616, 0, 0), (3616, 4640, 0, D_MODEL)]


def _column_runs():
    runs = []
    for n0, n1, seg, d0 in _PIECES:
        for j in range(N_DEV):
            lo, hi = max(n0, j * SHARD_W), min(n1, (j + 1) * SHARD_W)
            if lo < hi:
                runs.append((j, lo - j * SHARD_W, hi - j * SHARD_W, seg, d0 + lo - n0))
    return runs


def _mesh_pos():
    return lax.axis_index("x"), lax.axis_index("y"), lax.axis_index("c")


def _remote(src, dst, send_sems, recv_sems, k, to):
    return pltpu.make_async_remote_copy(src_ref=src, dst_ref=dst, send_sem=send_sems.at[k], recv_sem=recv_sems.at[k],
                                        device_id=to, device_id_type=pl.DeviceIdType.MESH)


def _gather_exchange(gats, send_sems, recv_sems, meanwhile=None):
    x, y, c = _mesh_pos()
    me, sibling = (x, y, c), (x, y, 1 - c)
    chips = [(1 - x, y), (x, 1 - y), (1 - x, 1 - y)]

    def copy(a, k, blk, to):
        slab = gats[a].at[4 * blk[0] + 2 * blk[1] + blk[2]]
        return _remote(slab, slab, send_sems, recv_sems, 7 * a + k, to)

    arrays = range(len(gats))
    first = [copy(a, 1 + j, me, (*chip, c)) for j, chip in enumerate(chips) for a in arrays]
    first += [copy(a, 0, me, sibling) for a in arrays]
    for cp in first:
        cp.start()
    if meanwhile is not None:
        meanwhile()
    passed = []
    for j, chip in enumerate(chips):
        for a in arrays:
            copy(a, 1 + j, (*chip, c), me).wait_recv()
            fwd = copy(a, 4 + j, (*chip, c), sibling)
            fwd.start()
            passed.append(fwd)
    for a in arrays:
        copy(a, 0, sibling, me).wait_recv()
    for j, chip in enumerate(chips):
        for a in arrays:
            copy(a, 4 + j, (*chip, 1 - c), me).wait_recv()
    for cp in first + passed:
        cp.wait_send()


def _gather_behind(own, gats, send_sems, recv_sems, local_sems, step, mid, last):
    x, y, c = _mesh_pos()
    me, sibling = (x, y, c), (x, y, 1 - c)
    chips = [(1 - x, y), (x, 1 - y), (1 - x, 1 - y)]
    arrays = range(len(gats))

    def copy(a, k, blk, to, src=None):
        slab = gats[a].at[4 * blk[0] + 2 * blk[1] + blk[2]]
        return _remote(slab if src is None else src, slab, send_sems, recv_sems, 7 * a + k, to)

    first = [copy(a, 1 + j, me, (*chip, c), src=own[a]) for j, chip in enumerate(chips) for a in arrays]
    first += [copy(a, 0, me, sibling, src=own[a]) for a in arrays]
    local = [pltpu.make_async_copy(own[a], gats[a].at[4 * x + 2 * y + c], local_sems.at[a]) for a in arrays]
    passed = [copy(a, 4 + j, (*chip, c), sibling) for j, chip in enumerate(chips) for a in arrays]

    @pl.when(step == 0)
    def _():
        for cp in first + local:
            cp.start()

    @pl.when(step == mid)
    def _():
        for j, chip in enumerate(chips):
            for a in arrays:
                copy(a, 1 + j, (*chip, c), me).wait_recv()
        for cp in passed:
            cp.start()

    @pl.when(step == last)
    def _():
        for a in arrays:
            copy(a, 0, sibling, me).wait_recv()
        for j, chip in enumerate(chips):
            for a in arrays:
                copy(a, 4 + j, (*chip, 1 - c), me).wait_recv()
        for cp in first + passed:
            cp.wait_send()
        for cp in local:
            cp.wait()


def _gather_first(w_in, w_uq2, w_oa, w_ob, w_out, x2, pos_col, invf_lane):
    hw = MLA_HEADS * HEAD_PAD
    uq_rows = Q_LORA_RANK // N_DEV
    rows = 256

    def body(win_ref, wuq_ref, woa_ref, wob_ref, wout_ref, x_ref, pos_ref, invf_ref,
             wc_ref, wq_ref, winb_ref, oab_ref, obb_ref, outb_ref, xb_ref, xt_ref, c_ref, sa_ref, sb_ref,
             g_uq, blk0, send_sems, recv_sems):
        def local_work():
            for i in range(SEQ // rows):
                xi = x_ref[rows * i:rows * (i + 1), :]
                xb_ref[rows * i:rows * (i + 1), :] = xi.astype(BF16)
                xt_ref[:, rows * i:rows * (i + 1)] = xi.T.astype(BF16)
            ang = pos_ref[...].astype(F32) * invf_ref[...]
            cs, sn = jnp.cos(ang), jnp.sin(ang)
            lane = lax.broadcasted_iota(jnp.int32, ang.shape, 1)
            c_ref[...] = jnp.where(lane < ROPE_LO, 1.0, jnp.where(lane < ROPE_HI, cs, 0.0))
            sa_ref[...] = jnp.where(jnp.logical_and(lane >= ROPE_LO, lane < ROPE_MID), -sn, 0.0)
            sb_ref[...] = jnp.where(jnp.logical_and(lane >= ROPE_MID, lane < ROPE_HI), sn, 0.0)

        x, y, c = _mesh_pos()
        me = (x, y, c)
        winb_ref[...] = win_ref[0].astype(BF16)
        oab_ref[...] = woa_ref[0].astype(BF16)
        obb_ref[...] = wob_ref[0].astype(BF16)
        outb_ref[...] = wout_ref[0].astype(BF16)
        g_uq[4 * x + 2 * y + c] = wuq_ref[...].astype(BF16)

        chip0 = jnp.logical_and(x == 0, y == 0)
        south = c == 0
        half = D_MODEL // 2
        halves = [blk0.at[pl.ds(0, half)], blk0.at[pl.ds(half, half)]]

        def bcopy(k, to, part=None):
            ref = blk0 if part is None else halves[part]
            return _remote(ref, ref, send_sems, recv_sems, 7 + k, to)

        sends0 = [(0, (0, 0, 1), None), (1, (1, 0, 0), 0), (2, (0, 1, 0), 1), (3, (1, 0, 0), 1), (4, (0, 1, 0), 0)]

        @pl.when(jnp.logical_and(chip0, south))
        def _():
            blk0[...] = winb_ref[...]
            for k, to, part in sends0:
                bcopy(k, to, part).start()

        _gather_exchange([g_uq], send_sems, recv_sems, meanwhile=local_work)

        for (cx, cy), first_k, first_half, second_k in (((1, 0), 1, 0, 3), ((0, 1), 2, 1, 4)):
            @pl.when(jnp.logical_and(jnp.logical_and(x == cx, y == cy), south))
            def _(cx=cx, cy=cy, first_k=first_k, first_half=first_half, second_k=second_k):
                bcopy(first_k, me, first_half).wait_recv()
                onward = bcopy(5 + first_half, (1, 1, 0), first_half)
                onward.start()
                bcopy(second_k, me, 1 - first_half).wait_recv()
                north = bcopy(7, (cx, cy, 1))
                north.start()
                onward.wait_send()
                north.wait_send()

        @pl.when(jnp.logical_and(jnp.logical_and(x == 1, y == 1), south))
        def _():
            bcopy(5, me, 0).wait_recv()
            bcopy(6, me, 1).wait_recv()
            north = bcopy(7, (1, 1, 1))
            north.start()
            north.wait_send()

        @pl.when(jnp.logical_and(chip0, c == 1))
        def _():
            bcopy(0, me).wait_recv()

        @pl.when(jnp.logical_and(jnp.logical_not(chip0), c == 1))
        def _():
            bcopy(7, me).wait_recv()

        @pl.when(jnp.logical_and(chip0, south))
        def _():
            for k, to, part in sends0:
                bcopy(k, to, part).wait_send()

        for j, s0, s1, seg, d0 in _column_runs():
            if seg == 2:
                wc_ref[:, d0:d0 + (s1 - s0)] = blk0[:, s0:s1]
        zeros = lambda r, w: jnp.zeros((r, w), BF16)
        wc_ref[:, Q_LORA_RANK:CQ_PAD] = zeros(D_MODEL, CQ_PAD - Q_LORA_RANK)
        wc_ref[:, CQ_PAD + LANES:CQ_PAD + LANES + ROPE_LO] = zeros(D_MODEL, ROPE_LO)
        wc_ref[:, CQ_PAD + LANES + ROPE_HI:SEG_C] = zeros(D_MODEL, LANES - ROPE_HI)
        wq_ref[Q_LORA_RANK:CQ_PAD, :] = zeros(CQ_PAD - Q_LORA_RANK, hw)
        for h in range(MLA_HEADS):
            wq_ref[0:Q_LORA_RANK, HEAD_PAD * h + QK_HEAD_DIM:HEAD_PAD * (h + 1)] = zeros(Q_LORA_RANK, HEAD_PAD - QK_HEAD_DIM)
        for j in range(N_DEV):
            for h in range(MLA_HEADS):
                wq_ref[uq_rows * j:uq_rows * (j + 1), HEAD_PAD * h:HEAD_PAD * h + QK_HEAD_DIM] = g_uq[
                    j, :, QK_HEAD_DIM * h:QK_HEAD_DIM * (h + 1)]

    vmem = pl.BlockSpec(memory_space=pltpu.VMEM)
    return pl.pallas_call(
        body, name="gather_first",
        out_shape=[jax.ShapeDtypeStruct((D_MODEL, SEG_C), BF16), jax.ShapeDtypeStruct((CQ_PAD, hw), BF16),
                   jax.ShapeDtypeStruct(w_in.shape[1:], BF16), jax.ShapeDtypeStruct(w_oa.shape[1:], BF16),
                   jax.ShapeDtypeStruct(w_ob.shape[1:], BF16), jax.ShapeDtypeStruct(w_out.shape[1:], BF16),
                   jax.ShapeDtypeStruct((SEQ, D_MODEL), BF16), jax.ShapeDtypeStruct((D_MODEL, SEQ), BF16)]
        + [jax.ShapeDtypeStruct((SEQ, LANES), F32)] * 3,
        in_specs=[vmem] * 8, out_specs=[vmem] * 11,
        scratch_shapes=[pltpu.VMEM((N_DEV, uq_rows, MLA_HEADS * QK_HEAD_DIM), BF16), pltpu.VMEM((D_MODEL, SHARD_W), BF16),
                        pltpu.SemaphoreType.DMA((15,)), pltpu.SemaphoreType.DMA((15,))],
        compiler_params=pltpu.CompilerParams(vmem_limit_bytes=VMEM_BIG),
    )(w_in, w_uq2, w_oa, w_ob, w_out, x2, pos_col, invf_lane)


def _assemble_in(g_in):
    def body(g_ref, wa_ref, wb_ref):
        segs = [wa_ref, wb_ref]
        for j, s0, s1, seg, d0 in _column_runs():
            if seg < 2:
                segs[seg][:, d0:d0 + (s1 - s0)] = g_ref[j, :, s0:s1]

    return pl.pallas_call(
        body, name="assemble_in",
        out_shape=[jax.ShapeDtypeStruct((D_MODEL, SEG_A), BF16), jax.ShapeDtypeStruct((D_MODEL, SEG_B), BF16)],
        compiler_params=pltpu.CompilerParams(vmem_limit_bytes=VMEM_MID),
    )(g_in)


def _assemble_out(g_oa, g_ob, g_out):
    cols = D_MODEL // N_DEV

    def body(goa_ref, gob_ref, gout_ref, oa_ref, ob_ref, out_ref):
        for j in range(N_DEV):
            oa_ref[:, cols * j:cols * (j + 1)] = goa_ref[j]
            ob_ref[:, cols * j:cols * (j + 1)] = gob_ref[j]
            out_ref[cols * j:cols * (j + 1), :] = gout_ref[j]

    return pl.pallas_call(
        body, name="assemble_out",
        out_shape=[jax.ShapeDtypeStruct((MLA_WIDTH, D_MODEL), BF16), jax.ShapeDtypeStruct((SGU_WIDTH, D_MODEL), BF16),
                   jax.ShapeDtypeStruct((D_MODEL, D_MODEL), BF16)],
    )(g_oa, g_ob, g_out)


C_NAT = 544


P_IN_SPLIT = 896


def _to_parts(dwa, dwb):
    def body(dwa_ref, dwb_ref, phi_ref, plo_ref):
        phi_ref[0, :, 0:C_NAT] = jnp.zeros((P_IN_SPLIT, C_NAT), BF16)
        plo_ref[0, :, 0:C_NAT] = jnp.zeros((D_MODEL - P_IN_SPLIT, C_NAT), BF16)
        segs = [dwa_ref, dwb_ref]
        for j, s0, s1, seg, d0 in _column_runs():
            if seg < 2:
                phi_ref[j, :, s0:s1] = segs[seg][0:P_IN_SPLIT, d0:d0 + (s1 - s0)]
                plo_ref[j, :, s0:s1] = segs[seg][P_IN_SPLIT:D_MODEL, d0:d0 + (s1 - s0)]

    return pl.pallas_call(
        body, name="to_parts",
        out_shape=[jax.ShapeDtypeStruct((N_DEV, P_IN_SPLIT, SHARD_W), BF16),
                   jax.ShapeDtypeStruct((N_DEV, D_MODEL - P_IN_SPLIT, SHARD_W), BF16)],
        compiler_params=pltpu.CompilerParams(vmem_limit_bytes=VMEM_MID))(dwa, dwb)


def _dx_tail(dhs, ws, dx_res, dwc, p_uq, p_rep):
    tm = SEQ // 4
    rep_rows = p_rep.shape[1]
    c_rows = D_MODEL // N_DEV
    spec = [((c_rows, C_NAT), BF16), (p_uq.shape[1:], BF16), ((rep_rows, LANES), F32)]
    n = len(spec)

    nseg = len(dhs)

    def body(*refs):
        dh_refs, w_refs = refs[:nseg], refs[nseg:2 * nseg]
        dxr_ref, dwc_ref, puq_ref, prep_ref, dx_ref, call_ref, guq_ref, repall_ref, pc_ref, c_all, rep_all = refs[
            2 * nseg:2 * nseg + 11]
        rest = refs[2 * nseg + 11:]
        ras, tbs, rbs = rest[0:n], rest[n:2 * n], rest[2 * n:3 * n]
        send_sems, recv_sems, gsend, grecv = rest[3 * n:]
        step = pl.program_id(0)
        x, y, c = _mesh_pos()
        me_idx = 4 * x + 2 * y + c
        me, sibling = (x, y, c), (x, y, 1 - c)
        others = [(1 - x, y), (x, 1 - y), (1 - x, 1 - y)]
        parts = [pc_ref, puq_ref, prep_ref]
        gats = [rep_all, c_all]

        def stage1(chip, a):
            return _remote(parts[a].at[2 * chip + (1 - c)], ras[a].at[chip], send_sems, recv_sems, 7 * a + chip, sibling)

        def stage2(k, a):
            cx, cy = others[k]
            return _remote(tbs[a].at[k], rbs[a].at[k], send_sems, recv_sems, 7 * a + 4 + k, (cx, cy, c))

        def gcopy(a, k, blk, to):
            slab = gats[a].at[4 * blk[0] + 2 * blk[1] + blk[2]]
            return _remote(slab, slab, gsend, grecv, 7 * a + k, to)

        def chip_sum(a, chip):
            return parts[a][2 * chip + c].astype(F32) + ras[a][chip].astype(F32)

        @pl.when(step == 0)
        def _():
            for j, s0, s1, seg, d0 in _column_runs():
                if seg == 2:
                    for r in range(N_DEV):
                        pc_ref[r, :, s0:s1] = dwc_ref[c_rows * r:c_rows * (r + 1), d0:d0 + (s1 - s0)]
            for chip in range(4):
                for a in range(n):
                    stage1(chip, a).start()

        @pl.when(step == 1)
        def _():
            for chip in range(4):
                for a in range(n):
                    stage1(chip, a).wait_recv()
            for k, (cx, cy) in enumerate(others):
                for a in range(n):
                    tbs[a][k] = chip_sum(a, 2 * cx + cy).astype(spec[a][1])
                    stage2(k, a).start()

        @pl.when(step == 2)
        def _():
            for k in range(3):
                for a in range(n):
                    stage2(k, a).wait_recv()
            sums = []
            for a in range(n):
                acc = chip_sum(a, 2 * x + y)
                for k in range(3):
                    acc = acc + rbs[a][k].astype(F32)
                sums.append(acc)
            c_all[me_idx] = sums[0].astype(BF16)
            guq_ref[...] = sums[1]
            rep_all[me_idx] = sums[2]
            for a in range(2):
                for j, chip in enumerate(others):
                    gcopy(a, 1 + j, me, (*chip, c)).start()
                gcopy(a, 0, me, sibling).start()

        @pl.when(step == 3)
        def _():
            for j, chip in enumerate(others):
                for a in range(2):
                    gcopy(a, 1 + j, (*chip, c), me).wait_recv()
                    gcopy(a, 4 + j, (*chip, c), sibling).start()
            for a in range(2):
                gcopy(a, 0, sibling, me).wait_recv()
                for j, chip in enumerate(others):
                    gcopy(a, 4 + j, (*chip, 1 - c), me).wait_recv()
            for a in range(2):
                gcopy(a, 0, me, sibling).wait_send()
                for j, chip in enumerate(others):
                    gcopy(a, 1 + j, me, (*chip, c)).wait_send()
                    gcopy(a, 4 + j, (*chip, c), sibling).wait_send()
            for a in range(n):
                for chip in range(4):
                    stage1(chip, a).wait_send()
                for k in range(3):
                    stage2(k, a).wait_send()
            call_ref[...] = c_all[...]
            repall_ref[...] = rep_all[...]

        acc = dxr_ref[...]
        for dh_ref, w_ref in zip(dh_refs, w_refs):
            acc = acc + _dot(dh_ref[...], w_ref[...], _NT)
        dx_ref[...] = acc

    row = lambda w: pl.BlockSpec((tm, w), lambda i: (i, 0))
    full = lambda shape: pl.BlockSpec(shape, lambda i: (0,) * len(shape))
    scratch = [pltpu.VMEM((N_DEV, c_rows, C_NAT), BF16), pltpu.VMEM((N_DEV, c_rows, C_NAT), BF16),
               pltpu.VMEM((N_DEV, rep_rows, LANES), F32)]
    for lead in (4, 3, 3):
        scratch += [pltpu.VMEM((lead,) + tuple(shape), dt) for shape, dt in spec]
    scratch += [pltpu.SemaphoreType.DMA((7 * n,)), pltpu.SemaphoreType.DMA((7 * n,)),
                pltpu.SemaphoreType.DMA((14,)), pltpu.SemaphoreType.DMA((14,))]
    return pl.pallas_call(
        body, name="dx_tail", grid=(SEQ // tm,),
        in_specs=[row(dh.shape[1]) for dh in dhs] + [full(w.shape) for w in ws]
        + [row(D_MODEL), full(dwc.shape), full(p_uq.shape), full(p_rep.shape)],
        out_specs=[row(D_MODEL), full((N_DEV, c_rows, C_NAT)), full(p_uq.shape[1:]), full((N_DEV, rep_rows, LANES))],
        out_shape=[jax.ShapeDtypeStruct((SEQ, D_MODEL), F32), jax.ShapeDtypeStruct((N_DEV, c_rows, C_NAT), BF16),
                   jax.ShapeDtypeStruct(p_uq.shape[1:], F32), jax.ShapeDtypeStruct((N_DEV, rep_rows, LANES), F32)],
        scratch_shapes=scratch,
        compiler_params=pltpu.CompilerParams(dimension_semantics=("arbitrary",), vmem_limit_bytes=VMEM_BIG),
    )(*dhs, *ws, dx_res, dwc, p_uq, p_rep)


def _sum_landed(landed, c_all):
    c_rows = D_MODEL // N_DEV

    def body(rhi_ref, rlo_ref, roa_ref, rob_ref, rout_ref, call_ref, gin_ref, goa_ref, gob_ref, gout_ref):
        def total(ref, sl):
            acc = ref[0, sl, :].astype(F32)
            for s in range(1, N_DEV):
                acc = acc + ref[s, sl, :].astype(F32)
            return acc

        x, y, c = _mesh_pos()
        dev0 = jnp.where(4 * x + 2 * y + c == 0, 1.0, 0.0)
        for j in range(N_DEV):
            sl = slice(c_rows * j, c_rows * (j + 1))
            below = c_rows * j < P_IN_SPLIT
            tot = total(rhi_ref, sl) if below else total(rlo_ref, slice(c_rows * j - P_IN_SPLIT, c_rows * (j + 1) - P_IN_SPLIT))
            gin_ref[0, sl, C_NAT:SHARD_W] = tot[:, C_NAT:SHARD_W]
            gin_ref[0, sl, 0:C_NAT] = tot[:, 0:C_NAT] + dev0 * call_ref[j].astype(F32)
        goa_ref[0] = total(roa_ref, slice(None))
        gob_ref[0] = total(rob_ref, slice(None))
        gout_ref[0] = total(rout_ref, slice(None))

    return pl.pallas_call(
        body, name="sum_landed",
        out_shape=[jax.ShapeDtypeStruct((1, D_MODEL, SHARD_W), F32)]
        + [jax.ShapeDtypeStruct((1,) + r.shape[1:], F32) for r in landed[2:]],
        compiler_params=pltpu.CompilerParams(vmem_limit_bytes=VMEM_MID),
    )(*landed, c_all)


_O_CQ, _O_CKV, _O_KPE, _O_ZA, _O_U, _O_V, _O_ZB, _O_GA, _O_GB = 0, 384, 512, 544, 1056, 1568, 2080, 2592, 3616


def _to_segments(w):
    z = lambda n: jnp.zeros(w.shape[:-1] + (n,), w.dtype)
    seg_a = jnp.concatenate([w[..., _O_GA:_O_GB], w[..., _O_GB:IN_WIDTH], w[..., _O_ZA:_O_U]], axis=-1)
    seg_b = jnp.concatenate([w[..., _O_U:_O_V], w[..., _O_V:_O_ZB], w[..., _O_ZB:_O_GA]], axis=-1)
    seg_c = jnp.concatenate([w[..., _O_CQ:_O_CKV], z(CQ_PAD - Q_LORA_RANK), w[..., _O_CKV:_O_KPE],
                             z(ROPE_LO), w[..., _O_KPE:_O_ZA], z(LANES - ROPE_HI)], axis=-1)
    return seg_a, seg_b, seg_c


def _from_segments(seg_a, seg_b, seg_c):
    kpe0 = CQ_PAD + LANES + ROPE_LO
    return jnp.concatenate([
        seg_c[..., 0:Q_LORA_RANK], seg_c[..., CQ_PAD:CQ_PAD + LANES], seg_c[..., kpe0:kpe0 + QK_ROPE_DIM],
        seg_a[..., 2 * D_MODEL:SEG_A], seg_b, seg_a[..., 0:2 * D_MODEL]], axis=-1)


def kernel(x, positions, w_in, b_in, g_q, w_uq, g_kv, w_ukv, w_oa, sgu_ln_g, sgu_ln_b, w_s, b_s, w_ob, w_out, ln_g, ln_b, loss_target, m_w_in, m_b_in, m_g_q, m_w_uq, m_g_kv, m_w_ukv, m_w_oa, m_sgu_ln_g, m_sgu_ln_b, m_w_s, m_b_s, m_w_ob, m_w_out, m_ln_g, m_ln_b, v_w_in, v_b_in, v_g_q, v_w_uq, v_g_kv, v_w_ukv, v_w_oa, v_sgu_ln_g, v_sgu_ln_b, v_w_s, v_b_s, v_w_ob, v_w_out, v_ln_g, v_ln_b):
    w_uq2 = w_uq[0].reshape(Q_LORA_RANK // N_DEV, MLA_HEADS * QK_HEAD_DIM)
    inv_freq = ROPE_THETA ** (-jnp.arange(0, QK_ROPE_DIM, 2, dtype=F32) / QK_ROPE_DIM)
    invf_lane = jnp.concatenate([jnp.zeros((ROPE_LO,), F32), inv_freq, inv_freq,
                                 jnp.zeros((LANES - ROPE_HI,), F32)]).reshape(1, LANES)
    first = _gather_first(w_in, w_uq2, w_oa, w_ob, w_out, x[0], positions.reshape(SEQ, 1), invf_lane)
    partials = _local_step(x[0], loss_target[0], first, b_in, g_q, g_kv, w_ukv, sgu_ln_g, sgu_ln_b, w_s, b_s, ln_g, ln_b)
    weights = dict(w_in=w_in, b_in=b_in, g_q=g_q, w_uq=w_uq, g_kv=g_kv, w_ukv=w_ukv, w_oa=w_oa, sgu_ln_g=sgu_ln_g,
                   sgu_ln_b=sgu_ln_b, w_s=w_s, b_s=b_s, w_ob=w_ob, w_out=w_out, ln_g=ln_g, ln_b=ln_b)
    moms = dict(w_in=m_w_in, b_in=m_b_in, g_q=m_g_q, w_uq=m_w_uq, g_kv=m_g_kv, w_ukv=m_w_ukv, w_oa=m_w_oa,
                sgu_ln_g=m_sgu_ln_g, sgu_ln_b=m_sgu_ln_b, w_s=m_w_s, b_s=m_b_s, w_ob=m_w_ob, w_out=m_w_out,
                ln_g=m_ln_g, ln_b=m_ln_b)
    vars_ = dict(w_in=v_w_in, b_in=v_b_in, g_q=v_g_q, w_uq=v_w_uq, g_kv=v_g_kv, w_ukv=v_w_ukv, w_oa=v_w_oa,
                 sgu_ln_g=v_sgu_ln_g, sgu_ln_b=v_sgu_ln_b, w_s=v_w_s, b_s=v_b_s, w_ob=v_w_ob, w_out=v_w_out,
                 ln_g=v_ln_g, ln_b=v_ln_b)
    return _reduce_and_update(partials, weights, moms, vars_)


def _local_step(x2, tgt, first, b_in, g_q, g_kv, w_ukv, sgu_ln_g, sgu_ln_b, w_s, b_s, ln_g, ln_b):
    wc, wq, win_b, oa_b, ob_b, out_b, x_bf, xt_bf, c_t, sa_t, sb_t = first
    ba, bb, bc = _to_segments(b_in)
    w_ukv_bf = w_ukv[0].astype(BF16)
    wkn = jnp.pad(w_ukv_bf[:, :, :QK_NOPE_DIM], ((0, 0), (0, 0), (0, HEAD_PAD - QK_NOPE_DIM))).reshape(KV_LORA_RANK, -1)
    wv = jnp.pad(w_ukv_bf[:, :, QK_NOPE_DIM:], ((0, 0), (0, 0), (0, HEAD_PAD - V_HEAD_DIM))).reshape(KV_LORA_RANK, -1)
    gq = jnp.pad(g_q, ((0, 0), (0, CQ_PAD - Q_LORA_RANK)))
    bias_full = jnp.repeat(b_s[0].T, SGU_GROUP_DIM, axis=1)
    w_s3 = w_s[0]
    w_st3 = jnp.swapaxes(w_s3, 1, 2)

    h_c = _mm(x_bf, wc, bias=bc, tm=512, tn=SEG_C, name="in_proj_c")
    q, k, kt, vx, vxt = _mla_prep(h_c, gq, g_kv, wq, wkn, wv, c_t, sa_t, sb_t)
    o, lse, (g_in,) = _attn_fwd(q, kt, vx, (win_b,))
    wa, wb = _assemble_in(g_in)
    h_a, (g_out,) = _mm(x_bf, wa, bias=ba, own=(out_b,), tm=512, tn=SEG_A // 2, name="in_proj_a")
    h_b, (g_oa, g_ob) = _mm(x_bf, wb, bias=bb, own=(oa_b, ob_b), tm=512, tn=SEG_B // 2, name="in_proj_b")
    y_b = _sgu_fwd(h_b, sgu_ln_g, sgu_ln_b, w_s3, bias_full)
    w_oa_f, w_ob_f, w_out_f = _assemble_out(g_oa, g_ob, g_out)

    (loss_row, dx_res, dh_a, d_o, d_yb, p_oa, p_ob, p_out, d_lng, d_lnb, d_ba) = _merge(
        x2, o, h_a, y_b, tgt, w_oa_f, w_ob_f, w_out_f, ln_g, ln_b)
    (dh_b, d_ws, d_bs_t, d_slg, d_slb, d_bb), (r_out,) = _sgu_bwd(h_b, d_yb, sgu_ln_g, sgu_ln_b, w_s3, w_st3, bias_full,
                                                                 (p_out,))
    d_wa, (r_oa,) = _mm(xt_bf, dh_a, out_dtype=BF16, parts=(p_oa,), tm=512, tn=512, name="dw_in_a")
    d_wb = _mm(xt_bf, dh_b, out_dtype=BF16, tm=512, tn=512, name="dw_in_b")
    p_hi, p_lo = _to_parts(d_wa, d_wb)
    dq, dk, dv, (r_hi,) = _attn_bwd(q, kt, k, vxt, d_o, o, lse, (p_hi,))
    (dh_c, p_uq, d_wkn, d_wv, d_gq, d_gkv, d_bc), (r_lo, r_ob) = _mla_bwd(
        dq, dk, dv, h_c, gq, g_kv, wq, wkn, wv, c_t, sa_t, sb_t, (p_lo, p_ob))
    landed = (r_hi, r_lo, r_oa, r_ob, r_out)
    d_wc = _mm(xt_bf, dh_c, out_dtype=BF16, tm=512, tn=SEG_C, name="dw_in_c")


    p_b_in = _from_segments(d_ba, d_bb, d_bc)
    p_w_ukv = jnp.concatenate([d_wkn.reshape(KV_LORA_RANK, MLA_HEADS, HEAD_PAD)[:, :, :QK_NOPE_DIM],
                               d_wv.reshape(KV_LORA_RANK, MLA_HEADS, HEAD_PAD)[:, :, :V_HEAD_DIM]], axis=-1)
    p_g_q = d_gq[:, :Q_LORA_RANK]
    p_b_s = d_bs_t[:, :SGU_GROUPS].T
    replicated = [p_b_in, p_g_q, d_gkv, p_w_ukv, d_slg, d_slb, d_ws, p_b_s, d_lng, d_lnb]
    return loss_row, ((dh_a, dh_b, dh_c), (wa, wb, wc), dx_res), landed, d_wc, p_uq, replicated


_NAMES = ["w_in", "b_in", "g_q", "w_uq", "g_kv", "w_ukv", "w_oa", "sgu_ln_g", "sgu_ln_b", "w_s", "b_s", "w_ob",
          "w_out", "ln_g", "ln_b"]
_REPLICATED = ["b_in", "g_q", "g_kv", "w_ukv", "sgu_ln_g", "sgu_ln_b", "w_s", "b_s", "ln_g", "ln_b"]


def _reduce_and_update(partials, weights, moms, vars_):
    loss_row, (dhs, ws, dx_res), landed, d_wc, p_uq, replicated = partials
    def piece(a):
        flat = a.reshape(-1)
        return jnp.pad(flat, (0, -flat.size % PACK_ALIGN))

    rep_flat = jnp.concatenate([piece(a) for a in replicated] + [piece(loss_row[0, :1])])
    rep_flat = jnp.pad(rep_flat, (0, N_DEV * PACK_R_ROWS * LANES - rep_flat.size))
    dx_ab, c_all, g_uq, rep_all = _dx_tail(dhs[:2], ws[:2], dx_res, d_wc, p_uq,
                                           rep_flat.reshape(N_DEV, PACK_R_ROWS, LANES))
    dx = _mm(dhs[2], ws[2], tb=True, add=dx_ab, tm=512, tn=D_MODEL, name="dx_c")
    g_in, g_oa, g_ob, g_out = _sum_landed(landed, c_all)
    rep_sum = rep_all.reshape(-1)
    grads, pos = dict(w_in=g_in, w_uq=g_uq, w_oa=g_oa, w_ob=g_ob, w_out=g_out), 0
    for nm in _REPLICATED:
        grads[nm] = rep_sum[pos:pos + weights[nm].size]
        pos += weights[nm].size + -weights[nm].size % PACK_ALIGN
    loss = rep_sum[pos]
    grads = {nm: grads[nm].reshape(weights[nm].shape) for nm in _NAMES}
    deltas, new_m, new_v = _adamw_all([weights[nm] for nm in _NAMES], [grads[nm] for nm in _NAMES],
                                      [moms[nm] for nm in _NAMES], [vars_[nm] for nm in _NAMES])
    return (loss, dx.reshape(1, SEQ, D_MODEL), *[grads[nm] for nm in _NAMES], *deltas, *new_m, *new_v)
```

```python
import math

import jax
import jax.numpy as jnp
from jax import lax
from jax.experimental import pallas as pl
from jax.experimental.pallas import tpu as pltpu

F32 = jnp.float32
BF16 = jnp.bfloat16

D_MODEL = 1024
SEQ = 2048
N_DEV = 8
MLA_HEADS = 8
Q_LORA_RANK = 384
KV_LORA_RANK = 128
QK_NOPE_DIM = 64
QK_ROPE_DIM = 32
V_HEAD_DIM = 64
QK_HEAD_DIM = QK_NOPE_DIM + QK_ROPE_DIM
MLA_WIDTH = MLA_HEADS * V_HEAD_DIM
ROPE_THETA = 10000.0
SGU_GROUPS = 8
SGU_GROUP_DIM = 64
SGU_WIDTH = SGU_GROUPS * SGU_GROUP_DIM
CHUNK = 128
RMS_EPS = 1e-6
LN_EPS = 1e-5
DN_ALPHA = 2.0 ** 0.25
IN_WIDTH = 4640
ATTN_SCALE = QK_HEAD_DIM ** -0.5

ADAM_LR = 0.001
ADAM_B1 = 0.9
ADAM_B2 = 0.999
ADAM_EPS = 1e-08
ADAM_WD = 0.01
ADAM_STEP = 10

LANES = 128
HEAD_PAD = 128
ROPE_LO = QK_NOPE_DIM
ROPE_MID = ROPE_LO + QK_ROPE_DIM // 2
ROPE_HI = ROPE_LO + QK_ROPE_DIM
CQ_PAD = 512

SEG_A = 2560
SEG_B = 1536
SEG_C = 768

PACK_R_ROWS = 272
PACK_ALIGN = 8 * LANES
VMEM_BIG = 56 * 1024 * 1024
VMEM_MID = 40 * 1024 * 1024


def _sigmoid(x):
    return 1.0 / (1.0 + jnp.exp(-x))


def _gelu_and_grad(x):
    c0 = math.sqrt(2.0 / math.pi)
    x2 = x * x
    t = jnp.tanh(c0 * (x + 0.044715 * x * x2))
    g = 0.5 * x * (1.0 + t)
    dg = 0.5 * (1.0 + t) + 0.5 * x * (1.0 - t * t) * (c0 * (1.0 + 3.0 * 0.044715 * x2))
    return g, dg


def _dot(a, b, dims):
    return lax.dot_general(a, b, (dims, ((), ())), preferred_element_type=F32)


_NN = ((1,), (0,))
_NT = ((1,), (1,))
_TN = ((0,), (0,))


def _store_grad(dh_ref, db_ref, col, val):
    cols = slice(col, col + val.shape[1])
    dh_ref[:, cols] = val.astype(BF16)
    db_ref[:, cols] += jnp.sum(val, axis=0, keepdims=True)


def _mm(a, b, *, tb=False, bias=None, add=None, out_dtype=F32, own=(), parts=(), tm, tn, name):
    m, k = a.shape
    n = b.shape[0] if tb else b.shape[1]
    assert m % tm == 0 and n % tn == 0 and not (own and parts)
    dims = _NT if tb else _NN
    nown = len(own) + len(parts)
    nm = m // tm
    nsteps = (n // tn) * nm

    def body(*refs):
        a_ref, b_ref = refs[0], refs[1]
        pos = 2
        r = _dot(a_ref[...], b_ref[...], dims)
        if bias is not None:
            r = r + refs[pos][...]; pos += 1
        if add is not None:
            r = r + refs[pos][...]; pos += 1
        own_refs = refs[pos:pos + nown]; pos += nown
        refs[pos][...] = r.astype(out_dtype)
        if nown:
            gat_refs = refs[pos + 1:pos + 1 + nown]
            send_sems, recv_sems, local_sems = refs[pos + 1 + nown:]
            step = pl.program_id(0) * nm + pl.program_id(1)
            if own:
                _gather_behind(own_refs, gat_refs, send_sems, recv_sems, local_sems, step, nsteps - 2, nsteps - 1)
            else:
                exchange = _exchange_parts(own_refs, gat_refs, send_sems, recv_sems, local_sems)
                _exchange_start(step == 0, exchange)
                _exchange_finish(step == nsteps - 1, exchange)

    b_spec = pl.BlockSpec((tn, k), lambda j, i: (j, 0)) if tb else pl.BlockSpec((k, tn), lambda j, i: (0, j))
    in_specs, args = [pl.BlockSpec((tm, k), lambda j, i: (i, 0)), b_spec], [a, b]
    if bias is not None:
        in_specs.append(pl.BlockSpec((1, tn), lambda j, i: (0, j))); args.append(bias)
    if add is not None:
        in_specs.append(pl.BlockSpec((tm, tn), lambda j, i: (i, j))); args.append(add)
    hbm = pl.BlockSpec(memory_space=pl.ANY)
    res = pl.pallas_call(
        body, name=name, grid=(n // tn, nm), in_specs=in_specs + [hbm] * nown,
        out_specs=[pl.BlockSpec((tm, tn), lambda j, i: (i, j))] + [hbm] * nown,
        out_shape=[jax.ShapeDtypeStruct((m, n), out_dtype)]
        + [jax.ShapeDtypeStruct((N_DEV,) + o.shape, o.dtype) for o in own]
        + [jax.ShapeDtypeStruct(p.shape, p.dtype) for p in parts],
        scratch_shapes=_exchange_sems(nown) if nown else [],
        compiler_params=pltpu.CompilerParams(dimension_semantics=("arbitrary", "arbitrary"), vmem_limit_bytes=VMEM_BIG),
    )(*args, *own, *parts)
    return (res[0], res[1:]) if nown else res[0]


def _rope(x, c, sa, sb):
    return x * c + pltpu.roll(x, LANES - 16, 1) * sa + pltpu.roll(x, 16, 1) * sb


def _rope_t(dy, c, sa, sb):
    return dy * c + pltpu.roll(dy * sa, 16, 1) + pltpu.roll(dy * sb, LANES - 16, 1)


def _mla_prep(h_c, gq, gkv, wq, wkn, wvx, c_t, sa_t, sb_t):
    tm = 256
    hw = MLA_HEADS * HEAD_PAD

    def body(cq_ref, ckv_ref, kpe_ref, gq_ref, gkv_ref, wq_ref, wkn_ref, wvx_ref, c_ref, sa_ref, sb_ref,
             q_ref, k_ref, kt_ref, vx_ref, vxt_ref):
        c, sa, sb = c_ref[...], sa_ref[...], sb_ref[...]
        cq = cq_ref[...]
        rq = lax.rsqrt(jnp.sum(cq * cq, axis=1, keepdims=True) * (1.0 / Q_LORA_RANK) + RMS_EPS)
        cqn = ((cq * rq) * gq_ref[...]).astype(BF16)
        qall = _dot(cqn, wq_ref[...], _NN)
        for h in range(MLA_HEADS):
            sl = slice(HEAD_PAD * h, HEAD_PAD * (h + 1))
            q_ref[:, sl] = (_rope(qall[:, sl], c, sa, sb) * ATTN_SCALE).astype(BF16)
        ckv = ckv_ref[...]
        rkv = lax.rsqrt(jnp.sum(ckv * ckv, axis=1, keepdims=True) * (1.0 / KV_LORA_RANK) + RMS_EPS)
        ckvn = ((ckv * rkv) * gkv_ref[...]).astype(BF16)
        knall = _dot(ckvn, wkn_ref[...], _NN)
        vall = _dot(ckvn, wvx_ref[...], _NN)
        kper = _rope(kpe_ref[...], c, sa, sb)
        ones_half = (lax.broadcasted_iota(jnp.int32, (tm, HEAD_PAD), 1) >= V_HEAD_DIM).astype(F32)
        for h in range(MLA_HEADS):
            sl = slice(HEAD_PAD * h, HEAD_PAD * (h + 1))
            kh = knall[:, sl] + kper
            vh = vall[:, sl] + ones_half
            k_ref[:, sl] = kh.astype(BF16)
            kt_ref[sl, :] = kh.T.astype(BF16)
            vx_ref[:, sl] = vh.astype(BF16)
            vxt_ref[sl, :] = vh.T.astype(BF16)

    full = lambda shape: pl.BlockSpec(shape, lambda i: (0, 0))
    tab = pl.BlockSpec((tm, LANES), lambda i: (i, 0))
    row = pl.BlockSpec((tm, hw), lambda i: (i, 0))
    col = pl.BlockSpec((hw, tm), lambda i: (0, i))
    return pl.pallas_call(
        body, name="mla_prep", grid=(SEQ // tm,),
        in_specs=[pl.BlockSpec((tm, CQ_PAD), lambda i: (i, 0)),
                  pl.BlockSpec((tm, LANES), lambda i: (i, CQ_PAD // LANES)),
                  pl.BlockSpec((tm, LANES), lambda i: (i, CQ_PAD // LANES + 1)),
                  full((1, CQ_PAD)), full((1, KV_LORA_RANK)),
                  full((CQ_PAD, hw)), full((KV_LORA_RANK, hw)), full((KV_LORA_RANK, hw)), tab, tab, tab],
        out_specs=[row, row, col, row, col],
        out_shape=[jax.ShapeDtypeStruct((SEQ, hw), BF16), jax.ShapeDtypeStruct((SEQ, hw), BF16),
                   jax.ShapeDtypeStruct((hw, SEQ), BF16), jax.ShapeDtypeStruct((SEQ, hw), BF16),
                   jax.ShapeDtypeStruct((hw, SEQ), BF16)],
        compiler_params=pltpu.CompilerParams(dimension_semantics=("arbitrary",), vmem_limit_bytes=VMEM_MID),
    )(h_c, h_c, h_c, gq, gkv, wq, wkn, wvx, c_t, sa_t, sb_t)


ATT_T = 512
ATT_STRIP = 64


def _attn_fwd(q, kt, vx, own):
    t, rs = ATT_T, ATT_STRIP
    nown = len(own)
    nq = SEQ // t
    nsteps = (MLA_HEADS // 2) * nq

    def body(q_ref, kt_ref, vx_ref, *rest):
        own_refs, (o_ref, l_ref), gat_refs = rest[:nown], rest[nown:nown + 2], rest[nown + 2:2 * nown + 2]
        s_scr, p_scr, m_scr, a_scr, acc_scr, send_sems, recv_sems, local_sems = rest[2 * nown + 2:]
        qi = pl.program_id(1)
        _gather_behind(own_refs, gat_refs, send_sems, recv_sems, local_sems, pl.program_id(0) * nq + qi,
                       nsteps - 2, nsteps - 1)
        lane = lax.broadcasted_iota(jnp.int32, (t, LANES), 1)
        m_scr[...] = jnp.full((2, t, LANES), -1e30, F32)
        acc_scr[...] = jnp.zeros((2, t, LANES), F32)

        def block(j, masked):
            off = pl.multiple_of(j * t, t)
            for a in range(2):
                sl = slice(HEAD_PAD * a, HEAD_PAD * (a + 1))
                s_scr[a] = _dot(q_ref[:, sl], kt_ref[sl, pl.ds(off, t)], _NN)
                for r in range(t // rs):
                    rows = slice(rs * r, rs * (r + 1))
                    s = s_scr[a, rows, :]
                    if masked:
                        rowi = lax.broadcasted_iota(jnp.int32, (rs, t), 0) + rs * r
                        coli = lax.broadcasted_iota(jnp.int32, (rs, t), 1)
                        s = jnp.where(coli <= rowi, s, -1e30)
                    m_old = m_scr[a, rows, :]
                    m_new = jnp.maximum(m_old, jnp.max(s, axis=1, keepdims=True))
                    p_scr[a, rows, :] = jnp.exp(s - m_new[:, :1]).astype(BF16)
                    a_scr[a, rows, :] = jnp.exp(m_old - m_new)
                    m_scr[a, rows, :] = m_new
                acc_scr[a] = acc_scr[a] * a_scr[a] + _dot(p_scr[a], vx_ref[pl.ds(off, t), sl], _NN)

        def step(j, carry):
            block(j, False)
            return carry
        lax.fori_loop(0, qi, step, 0)
        block(qi, True)
        res = []
        for a in range(2):
            acc = acc_scr[a]
            l = acc[:, V_HEAD_DIM:V_HEAD_DIM + 1]
            res.append((acc / l, m_scr[a] + jnp.log(l)))
        o_ref[...] = jnp.where(lane < V_HEAD_DIM, res[0][0], pltpu.roll(res[1][0], V_HEAD_DIM, 1))
        l_ref[...] = jnp.where(lane < V_HEAD_DIM, res[0][1], res[1][1])

    hbm = pl.BlockSpec(memory_space=pl.ANY)
    res = pl.pallas_call(
        body, name="attn_fwd", grid=(MLA_HEADS // 2, nq),
        in_specs=[pl.BlockSpec((t, 2 * HEAD_PAD), lambda p, i: (i, p)),
                  pl.BlockSpec((2 * HEAD_PAD, SEQ), lambda p, i: (p, 0)),
                  pl.BlockSpec((SEQ, 2 * HEAD_PAD), lambda p, i: (0, p))] + [hbm] * nown,
        out_specs=[pl.BlockSpec((t, LANES), lambda p, i: (i, p)),
                   pl.BlockSpec((t, LANES), lambda p, i: (i, p))] + [hbm] * nown,
        out_shape=[jax.ShapeDtypeStruct((SEQ, MLA_WIDTH), F32), jax.ShapeDtypeStruct((SEQ, MLA_WIDTH), F32)]
        + [jax.ShapeDtypeStruct((N_DEV,) + a.shape, a.dtype) for a in own],
        scratch_shapes=[pltpu.VMEM((2, t, t), F32), pltpu.VMEM((2, t, t), BF16), pltpu.VMEM((2, t, LANES), F32),
                        pltpu.VMEM((2, t, LANES), F32), pltpu.VMEM((2, t, LANES), F32)] + _exchange_sems(nown),
        compiler_params=pltpu.CompilerParams(dimension_semantics=("arbitrary", "arbitrary"), vmem_limit_bytes=VMEM_MID),
    )(q, kt, vx, *own)
    return res[0], res[1], res[2:]


def _exchange_parts(parts, lands, send_sems, recv_sems, local_sems):
    x, y, c = _mesh_pos()
    me = 4 * x + 2 * y + c
    peers = [(x, y, 1 - c), (1 - x, y, c), (x, 1 - y, c), (1 - x, 1 - y, c),
             (1 - x, y, 1 - c), (x, 1 - y, 1 - c), (1 - x, 1 - y, 1 - c)]
    remote, local = [], []
    for a, (part, land) in enumerate(zip(parts, lands)):
        for k, peer in enumerate(peers):
            t = 4 * peer[0] + 2 * peer[1] + peer[2]
            remote.append(_remote(part.at[t], land.at[me], send_sems, recv_sems, 7 * a + k, peer))
        local.append(pltpu.make_async_copy(part.at[me], land.at[me], local_sems.at[a]))
    return remote, local


def _exchange_start(first_step, exchange):
    remote, local = exchange

    @pl.when(first_step)
    def _():
        for cp in remote + local:
            cp.start()


def _exchange_finish(last_step, exchange):
    remote, local = exchange

    @pl.when(last_step)
    def _():
        for cp in remote:
            cp.wait_recv()
        for cp in remote:
            cp.wait_send()
        for cp in local:
            cp.wait()


def _exchange_sems(npart):
    return [pltpu.SemaphoreType.DMA((7 * npart,)), pltpu.SemaphoreType.DMA((7 * npart,)),
            pltpu.SemaphoreType.DMA((npart,))]


def _attn_bwd(q, kt, k, vxt, d_o, o, lse, parts):
    t, rs = ATT_T, ATT_STRIP
    nq = SEQ // t
    npart = len(parts)
    nsteps = MLA_HEADS // 2

    def body(q_ref, kt_ref, k_ref, vxt_ref, do_ref, o_ref, l_ref, *rest):
        part_refs, rest = rest[:npart], rest[npart:]
        dq_ref, dk_ref, dv_ref = rest[:3]
        land_refs, rest = rest[3:3 + npart], rest[3 + npart:]
        s_scr, dp_scr, p_scr, ds_scr, st_scr, send_sems, recv_sems, local_sems = rest
        exchange = _exchange_parts(part_refs, land_refs, send_sems, recv_sems, local_sems)
        _exchange_start(pl.program_id(0) == 0, exchange)
        dk_ref[...] = jnp.zeros_like(dk_ref)
        dv_ref[...] = jnp.zeros_like(dv_ref)
        lane = lax.broadcasted_iota(jnp.int32, (t, LANES), 1)

        def qtile(i, carry):
            ioff = pl.multiple_of(i * t, t)
            do_i = do_ref[pl.ds(ioff, t), :]
            o_i = o_ref[pl.ds(ioff, t), :]
            l_i = l_ref[pl.ds(ioff, t), :]
            for a in range(2):
                sl = slice(HEAD_PAD * a, HEAD_PAD * (a + 1))
                sel = (lane < V_HEAD_DIM) if a == 0 else (lane >= V_HEAD_DIM)
                doa = jnp.where(sel, do_i, 0.0)
                oa = o_i
                if a == 1:
                    doa = pltpu.roll(doa, V_HEAD_DIM, 1)
                    oa = pltpu.roll(o_i, V_HEAD_DIM, 1)
                st_scr[0] = jnp.broadcast_to(jnp.sum(doa * oa, axis=1, keepdims=True), (t, LANES))
                st_scr[1] = jnp.broadcast_to(l_i[:, V_HEAD_DIM * a:V_HEAD_DIM * a + 1], (t, LANES))
                doa_bf = doa.astype(BF16)
                qa = q_ref[pl.ds(ioff, t), sl]

                def block(j, masked, dq_acc, sl=sl, qa=qa, doa_bf=doa_bf):
                    joff = pl.multiple_of(j * t, t)
                    s_scr[...] = _dot(qa, kt_ref[sl, pl.ds(joff, t)], _NN)
                    dp_scr[...] = _dot(doa_bf, vxt_ref[sl, pl.ds(joff, t)], _NN)
                    for r in range(t // rs):
                        rows = slice(rs * r, rs * (r + 1))
                        p = jnp.exp(s_scr[rows, :] - st_scr[1, rows, :1])
                        if masked:
                            rowi = lax.broadcasted_iota(jnp.int32, (rs, t), 0) + rs * r
                            coli = lax.broadcasted_iota(jnp.int32, (rs, t), 1)
                            p = jnp.where(coli <= rowi, p, 0.0)
                        p_scr[rows, :] = p.astype(BF16)
                        ds_scr[rows, :] = (p * (dp_scr[rows, :] - st_scr[0, rows, :1])).astype(BF16)
                    dk_ref[pl.ds(joff, t), sl] += _dot(ds_scr[...], qa, _TN)
                    dv_ref[pl.ds(joff, t), sl] += _dot(p_scr[...], doa_bf, _TN)
                    return dq_acc + _dot(ds_scr[...], k_ref[pl.ds(joff, t), sl], _NN)

                dq_acc = lax.fori_loop(0, i, lambda j, acc: block(j, False, acc), jnp.zeros((t, HEAD_PAD), F32))
                dq_ref[pl.ds(ioff, t), sl] = block(i, True, dq_acc)
            return carry

        lax.fori_loop(0, nq, qtile, 0)
        _exchange_finish(pl.program_id(0) == nsteps - 1, exchange)

    hw = MLA_HEADS * HEAD_PAD
    wide = pl.BlockSpec((SEQ, 2 * HEAD_PAD), lambda p: (0, p))
    wide_t = pl.BlockSpec((2 * HEAD_PAD, SEQ), lambda p: (p, 0))
    narrow = pl.BlockSpec((SEQ, LANES), lambda p: (0, p))
    hbm = pl.BlockSpec(memory_space=pl.ANY)
    res = pl.pallas_call(
        body, name="attn_bwd", grid=(nsteps,),
        in_specs=[wide, wide_t, wide, wide_t, narrow, narrow, narrow] + [hbm] * npart,
        out_specs=[wide, wide, wide] + [hbm] * npart,
        out_shape=[jax.ShapeDtypeStruct((SEQ, hw), F32)] * 3 + [jax.ShapeDtypeStruct(p.shape, p.dtype) for p in parts],
        scratch_shapes=[pltpu.VMEM((t, t), F32), pltpu.VMEM((t, t), F32), pltpu.VMEM((t, t), BF16),
                        pltpu.VMEM((t, t), BF16), pltpu.VMEM((2, t, LANES), F32)] + _exchange_sems(npart),
        compiler_params=pltpu.CompilerParams(dimension_semantics=("arbitrary",), vmem_limit_bytes=VMEM_BIG),
    )(q, kt, k, vxt, d_o, o, lse, *parts)
    return res[0], res[1], res[2], res[3:]


def _sgu_math(u, v, zb, lg, lb, ws_ref, bias):
    ug, dug = _gelu_and_grad(u)
    vg, dvg = _gelu_and_grad(v)
    mu = jnp.mean(vg, axis=1, keepdims=True)
    xc = vg - mu
    rstd = lax.rsqrt(jnp.mean(xc * xc, axis=1, keepdims=True) + LN_EPS)
    xh = xc * rstd
    vn_bf = (xh * lg + lb).astype(BF16)
    grp = lax.broadcasted_iota(jnp.int32, (CHUNK, SGU_WIDTH), 1) // SGU_GROUP_DIM
    r_i = lax.broadcasted_iota(jnp.int32, (CHUNK, CHUNK), 0)
    c_i = lax.broadcasted_iota(jnp.int32, (CHUNK, CHUNK), 1)
    tri, tri_t = r_i >= c_i, r_i <= c_i
    mixed = bias
    for g in range(SGU_GROUPS):
        wt = jnp.where(tri, ws_ref[g], 0.0).astype(BF16)
        mixed = mixed + jnp.where(grp == g, _dot(wt, vn_bf, _NN), 0.0)
    sb = _sigmoid(zb)
    return ug, dug, dvg, rstd, xh, vn_bf, grp, tri, tri_t, mixed, sb


def _sgu_fwd(h_b, lg, lb, w_s, bias_full):
    def body(u_ref, v_ref, zb_ref, lg_ref, lb_ref, ws_ref, bias_ref, yb_ref):
        zb = zb_ref[...]
        ug, _, _, _, _, _, _, _, _, mixed, sb = _sgu_math(u_ref[...], v_ref[...], zb, lg_ref[...], lb_ref[...],
                                                       ws_ref, bias_ref[...])
        yb_ref[...] = (ug * mixed) * (zb * sb)

    blk = lambda c: pl.BlockSpec((CHUNK, SGU_WIDTH), lambda i, c=c: (i, c))
    full2 = lambda shape: pl.BlockSpec(shape, lambda i: (0, 0))
    return pl.pallas_call(
        body, name="sgu_fwd", grid=(SEQ // CHUNK,),
        in_specs=[blk(0), blk(1), blk(2), full2((1, SGU_WIDTH)), full2((1, SGU_WIDTH)),
                  pl.BlockSpec((SGU_GROUPS, CHUNK, CHUNK), lambda i: (0, 0, 0)), full2((CHUNK, SGU_WIDTH))],
        out_specs=pl.BlockSpec((CHUNK, SGU_WIDTH), lambda i: (i, 0)),
        out_shape=jax.ShapeDtypeStruct((SEQ, SGU_WIDTH), F32),
        compiler_params=pltpu.CompilerParams(dimension_semantics=("arbitrary",)),
    )(h_b, h_b, h_b, lg, lb, w_s, bias_full)


def _sgu_bwd(h_b, d_yb, lg, lb, w_s, w_st, bias_full, parts):
    nsteps = SEQ // CHUNK
    npart = len(parts)

    def body(u_ref, v_ref, zb_ref, dyb_ref, lg_ref, lb_ref, ws_ref, wst_ref, bias_ref, *rest):
        part_refs, rest = rest[:npart], rest[npart:]
        dhb_ref, dws_ref, dbs_ref, dlg_ref, dlb_ref, dbb_ref = rest[:6]
        land_refs, (dbias_acc, send_sems, recv_sems, local_sems) = rest[6:6 + npart], rest[6 + npart:]
        step = pl.program_id(0)
        exchange = _exchange_parts(part_refs, land_refs, send_sems, recv_sems, local_sems)
        _exchange_start(step == 0, exchange)

        @pl.when(step == 0)
        def _():
            dbb_ref[...] = jnp.zeros_like(dbb_ref)
            dws_ref[...] = jnp.zeros_like(dws_ref)
            dlg_ref[...] = jnp.zeros_like(dlg_ref)
            dlb_ref[...] = jnp.zeros_like(dlb_ref)
            dbias_acc[...] = jnp.zeros_like(dbias_acc)

        zb = zb_ref[...]
        lg = lg_ref[...]
        ug, dug, dvg, rstd, xh, vn_bf, grp, tri, tri_t, mixed, sb = _sgu_math(
            u_ref[...], v_ref[...], zb, lg, lb_ref[...], ws_ref, bias_ref[...])
        dyb = dyb_ref[...]
        dsgu = dyb * (zb * sb)
        dzb = dyb * (ug * mixed) * (sb * (1.0 + zb * (1.0 - sb)))
        du = dsgu * mixed * dug
        dmixed = dsgu * ug
        dbias_acc[...] += dmixed
        dvn = jnp.zeros((CHUNK, SGU_WIDTH), F32)
        for g in range(SGU_GROUPS):
            dm_g = jnp.where(grp == g, dmixed, 0.0).astype(BF16)
            wtt = jnp.where(tri_t, wst_ref[g], 0.0).astype(BF16)
            dvn = dvn + _dot(wtt, dm_g, _NN)
            dws_ref[g] += jnp.where(tri, _dot(dm_g, vn_bf, _NT), 0.0)
        dlg_ref[...] += jnp.sum(dvn * xh, axis=0, keepdims=True)
        dlb_ref[...] += jnp.sum(dvn, axis=0, keepdims=True)
        dxh = dvn * lg
        dvgel = rstd * (dxh - jnp.mean(dxh, axis=1, keepdims=True) - xh * jnp.mean(dxh * xh, axis=1, keepdims=True))
        _store_grad(dhb_ref, dbb_ref, 0, du)
        _store_grad(dhb_ref, dbb_ref, SGU_WIDTH, dvgel * dvg)
        _store_grad(dhb_ref, dbb_ref, 2 * SGU_WIDTH, dzb)

        @pl.when(step == nsteps - 1)
        def _():
            acc = dbias_acc[...]
            lane = lax.broadcasted_iota(jnp.int32, (CHUNK, LANES), 1)
            out = jnp.zeros((CHUNK, LANES), F32)
            for g in range(SGU_GROUPS):
                sg = jnp.sum(jnp.where(grp == g, acc, 0.0), axis=1, keepdims=True)
                out = jnp.where(lane == g, sg, out)
            dbs_ref[...] = out

        _exchange_finish(step == nsteps - 1, exchange)

    blk = lambda c: pl.BlockSpec((CHUNK, SGU_WIDTH), lambda i, c=c: (i, c))
    full2 = lambda shape: pl.BlockSpec(shape, lambda i: (0, 0))
    full3 = pl.BlockSpec((SGU_GROUPS, CHUNK, CHUNK), lambda i: (0, 0, 0))
    hbm = pl.BlockSpec(memory_space=pl.ANY)
    res = pl.pallas_call(
        body, name="sgu_bwd", grid=(nsteps,),
        in_specs=[blk(0), blk(1), blk(2), pl.BlockSpec((CHUNK, SGU_WIDTH), lambda i: (i, 0)),
                  full2((1, SGU_WIDTH)), full2((1, SGU_WIDTH)), full3, full3, full2((CHUNK, SGU_WIDTH))] + [hbm] * npart,
        out_specs=[pl.BlockSpec((CHUNK, SEG_B), lambda i: (i, 0)), full3, full2((CHUNK, LANES)),
                   full2((1, SGU_WIDTH)), full2((1, SGU_WIDTH)), full2((1, SEG_B))] + [hbm] * npart,
        out_shape=[jax.ShapeDtypeStruct((SEQ, SEG_B), BF16),
                   jax.ShapeDtypeStruct((SGU_GROUPS, CHUNK, CHUNK), F32),
                   jax.ShapeDtypeStruct((CHUNK, LANES), F32),
                   jax.ShapeDtypeStruct((1, SGU_WIDTH), F32), jax.ShapeDtypeStruct((1, SGU_WIDTH), F32),
                   jax.ShapeDtypeStruct((1, SEG_B), F32)] + [jax.ShapeDtypeStruct(p.shape, p.dtype) for p in parts],
        scratch_shapes=[pltpu.VMEM((CHUNK, SGU_WIDTH), F32)] + _exchange_sems(npart),
        compiler_params=pltpu.CompilerParams(dimension_semantics=("arbitrary",)),
    )(h_b, h_b, h_b, d_yb, lg, lb, w_s, w_st, bias_full, *parts)
    return res[:6], res[6:]


def _merge(x, o, h_a, y_b, target, w_oa, w_ob, w_out, ln_g, ln_b):
    tm = 256
    nsteps = SEQ // tm

    def body(x_ref, o_ref, ga_ref, gb_ref, za_ref, yb_ref, tgt_ref, woa_ref, wob_ref, wout_ref, lng_ref, lnb_ref,
             loss_ref, dxr_ref, dha_ref, do_ref, dyb_ref, poa_ref, pob_ref, pout_ref, dlng_ref, dlnb_ref, dba_ref,
             dwoa_ref, dwob_ref, dwout_ref):
        step = pl.program_id(0)

        @pl.when(step == 0)
        def _():
            for r in (loss_ref, dwoa_ref, dwob_ref, dwout_ref, dlng_ref, dlnb_ref, dba_ref):
                r[...] = jnp.zeros_like(r)

        o = o_ref[...]
        za = za_ref[...]
        sa = _sigmoid(za)
        ya_bf = (o * (za * sa)).astype(BF16)
        yb_bf = yb_ref[...].astype(BF16)
        woa, wob, wout = woa_ref[...], wob_ref[...], wout_ref[...]
        pa = _dot(ya_bf, woa, _NN)
        pb = _dot(yb_bf, wob, _NN)
        sga = _sigmoid(ga_ref[...])
        sgb = _sigmoid(gb_ref[...])
        merged_bf = (sga * pa + sgb * pb).astype(BF16)
        r = DN_ALPHA * x_ref[...] + _dot(merged_bf, wout, _NN)
        mu = jnp.mean(r, axis=1, keepdims=True)
        rc = r - mu
        rstd = lax.rsqrt(jnp.mean(rc * rc, axis=1, keepdims=True) + LN_EPS)
        xh = rc * rstd
        lng = lng_ref[...]
        y = xh * lng + lnb_ref[...]
        e = y - tgt_ref[...]
        loss_ref[...] += 0.5 * jnp.sum(jnp.sum(e * e, axis=1, keepdims=True) * (1.0 / D_MODEL), axis=0, keepdims=True)

        dy = e * (1.0 / D_MODEL)
        dlng_ref[...] += jnp.sum(dy * xh, axis=0, keepdims=True)
        dlnb_ref[...] += jnp.sum(dy, axis=0, keepdims=True)
        dxh = dy * lng
        dr = rstd * (dxh - jnp.mean(dxh, axis=1, keepdims=True) - xh * jnp.mean(dxh * xh, axis=1, keepdims=True))
        dxr_ref[...] = DN_ALPHA * dr
        dr_bf = dr.astype(BF16)
        dwout_ref[...] += _dot(merged_bf, dr_bf, _TN)
        dmerged = _dot(dr_bf, wout, _NT)
        dpa_bf = (dmerged * sga).astype(BF16)
        dpb_bf = (dmerged * sgb).astype(BF16)
        _store_grad(dha_ref, dba_ref, 0, dmerged * pa * (sga * (1.0 - sga)))
        _store_grad(dha_ref, dba_ref, D_MODEL, dmerged * pb * (sgb * (1.0 - sgb)))
        dwoa_ref[...] += _dot(ya_bf, dpa_bf, _TN)
        dwob_ref[...] += _dot(yb_bf, dpb_bf, _TN)
        dya = _dot(dpa_bf, woa, _NT)
        dyb_ref[...] = _dot(dpb_bf, wob, _NT)
        do_ref[...] = dya * (za * sa)
        _store_grad(dha_ref, dba_ref, 2 * D_MODEL, dya * o * (sa * (1.0 + za * (1.0 - sa))))

        @pl.when(step == nsteps - 1)
        def _():
            cols = D_MODEL // N_DEV
            for j in range(N_DEV):
                poa_ref[j] = dwoa_ref[:, cols * j:cols * (j + 1)].astype(BF16)
                pob_ref[j] = dwob_ref[:, cols * j:cols * (j + 1)].astype(BF16)
                pout_ref[j] = dwout_ref[cols * j:cols * (j + 1), :].astype(BF16)

    row = lambda w, c=0: pl.BlockSpec((tm, w), lambda i, c=c: (i, c))
    full = lambda shape: pl.BlockSpec(shape, lambda i: (0, 0))
    full3 = lambda shape: pl.BlockSpec(shape, lambda i: (0, 0, 0))
    return pl.pallas_call(
        body, name="merge", grid=(nsteps,),
        in_specs=[row(D_MODEL), row(MLA_WIDTH), row(D_MODEL, 0), row(D_MODEL, 1), row(MLA_WIDTH, 4), row(SGU_WIDTH),
                  row(D_MODEL), full((MLA_WIDTH, D_MODEL)), full((SGU_WIDTH, D_MODEL)), full((D_MODEL, D_MODEL)),
                  full((1, D_MODEL)), full((1, D_MODEL))],
        out_specs=[full((1, LANES)), row(D_MODEL), row(SEG_A), row(MLA_WIDTH), row(SGU_WIDTH),
                   full3((N_DEV, MLA_WIDTH, D_MODEL // N_DEV)), full3((N_DEV, SGU_WIDTH, D_MODEL // N_DEV)),
                   full3((N_DEV, D_MODEL // N_DEV, D_MODEL)), full((1, D_MODEL)), full((1, D_MODEL)), full((1, SEG_A))],
        out_shape=[jax.ShapeDtypeStruct((1, LANES), F32),
                   jax.ShapeDtypeStruct((SEQ, D_MODEL), F32), jax.ShapeDtypeStruct((SEQ, SEG_A), BF16),
                   jax.ShapeDtypeStruct((SEQ, MLA_WIDTH), F32), jax.ShapeDtypeStruct((SEQ, SGU_WIDTH), F32),
                   jax.ShapeDtypeStruct((N_DEV, MLA_WIDTH, D_MODEL // N_DEV), BF16),
                   jax.ShapeDtypeStruct((N_DEV, SGU_WIDTH, D_MODEL // N_DEV), BF16),
                   jax.ShapeDtypeStruct((N_DEV, D_MODEL // N_DEV, D_MODEL), BF16),
                   jax.ShapeDtypeStruct((1, D_MODEL), F32), jax.ShapeDtypeStruct((1, D_MODEL), F32),
                   jax.ShapeDtypeStruct((1, SEG_A), F32)],
        scratch_shapes=[pltpu.VMEM((MLA_WIDTH, D_MODEL), F32), pltpu.VMEM((SGU_WIDTH, D_MODEL), F32),
                        pltpu.VMEM((D_MODEL, D_MODEL), F32)],
        compiler_params=pltpu.CompilerParams(dimension_semantics=("arbitrary",), vmem_limit_bytes=VMEM_BIG),
    )(x, o, h_a, h_a, h_a, y_b, target, w_oa, w_ob, w_out, ln_g, ln_b)


def _mla_bwd(dq, dk, dv, h_c, xt_bf, gq, gkv, wq, wkn, wv, c_t, sa_t, sb_t, parts):
    tm = 256
    hw = MLA_HEADS * HEAD_PAD
    npart = len(parts)
    nsteps = SEQ // tm

    def body(dq_ref, dk_ref, dv_ref, cq_ref, ckv_ref, xt_ref, gq_ref, gkv_ref, wq_ref, wkn_ref, wv_ref, c_ref, sa_ref,
             sb_ref, *rest):
        part_refs, rest = rest[:npart], rest[npart:]
        dhc_ref, puq_ref, dwkn_ref, dwv_ref, dgq_ref, dgkv_ref, dbc_ref, dwc_ref = rest[:8]
        land_refs, (pre_ref, dwq_ref, dwc_acc, send_sems, recv_sems, local_sems) = rest[8:8 + npart], rest[8 + npart:]
        exchange = _exchange_parts(part_refs, land_refs, send_sems, recv_sems, local_sems)
        _exchange_start(pl.program_id(0) == 0, exchange)
        _exchange_finish(pl.program_id(0) == nsteps - 1, exchange)

        @pl.when(pl.program_id(0) == 0)
        def _():
            for r in (dwq_ref, dwc_acc, dwkn_ref, dwv_ref, dgq_ref, dgkv_ref, dbc_ref):
                r[...] = jnp.zeros_like(r)

        c, sa, sb = c_ref[...], sa_ref[...], sb_ref[...]
        lane = lax.broadcasted_iota(jnp.int32, (tm, LANES), 1)
        rope_lanes = jnp.logical_and(lane >= ROPE_LO, lane < ROPE_HI)

        cq = cq_ref[...]
        gq = gq_ref[...]
        rq = lax.rsqrt(jnp.sum(cq * cq, axis=1, keepdims=True) * (1.0 / Q_LORA_RANK) + RMS_EPS)
        nq = cq * rq
        cqn_bf = (nq * gq).astype(BF16)
        for h in range(MLA_HEADS):
            sl = slice(HEAD_PAD * h, HEAD_PAD * (h + 1))
            pre_ref[:, sl] = _rope_t(dq_ref[:, sl] * ATTN_SCALE, c, sa, sb).astype(BF16)
        dqpre_bf = pre_ref[...]
        dcqn = _dot(dqpre_bf, wq_ref[...], _NT)
        dwq_ref[...] += _dot(cqn_bf, dqpre_bf, _TN)
        dgq_ref[...] += jnp.sum(dcqn * nq, axis=0, keepdims=True)
        dnq = dcqn * gq
        _store_grad(dhc_ref, dbc_ref, 0,
                    rq * (dnq - nq * (jnp.sum(dnq * nq, axis=1, keepdims=True) * (1.0 / Q_LORA_RANK))))

        ckv = ckv_ref[...]
        gkv = gkv_ref[...]
        rkv = lax.rsqrt(jnp.sum(ckv * ckv, axis=1, keepdims=True) * (1.0 / KV_LORA_RANK) + RMS_EPS)
        nkv = ckv * rkv
        ckvn_bf = (nkv * gkv).astype(BF16)
        dk = dk_ref[...]
        dk_bf = dk.astype(BF16)
        dv_bf = dv_ref[...].astype(BF16)
        dckvn = _dot(dk_bf, wkn_ref[...], _NT) + _dot(dv_bf, wv_ref[...], _NT)
        dwkn_ref[...] += _dot(ckvn_bf, dk_bf, _TN)
        dwv_ref[...] += _dot(ckvn_bf, dv_bf, _TN)
        dgkv_ref[...] += jnp.sum(dckvn * nkv, axis=0, keepdims=True)
        dnkv = dckvn * gkv
        _store_grad(dhc_ref, dbc_ref, CQ_PAD, rkv * (
            dnkv - nkv * (jnp.sum(dnkv * nkv, axis=1, keepdims=True) * (1.0 / KV_LORA_RANK))))
        dkpe = jnp.zeros((tm, LANES), F32)
        for h in range(MLA_HEADS):
            dkpe = dkpe + dk[:, HEAD_PAD * h:HEAD_PAD * (h + 1)]
        _store_grad(dhc_ref, dbc_ref, CQ_PAD + LANES, _rope_t(jnp.where(rope_lanes, dkpe, 0.0), c, sa, sb))
        dwc_acc[...] += _dot(xt_ref[...], dhc_ref[...], _NN)

        @pl.when(pl.program_id(0) == SEQ // tm - 1)
        def _():
            dwc_ref[...] = dwc_acc[...].astype(BF16)
            rows = Q_LORA_RANK // N_DEV
            for j in range(N_DEV):
                for h in range(MLA_HEADS):
                    puq_ref[j, :, QK_HEAD_DIM * h:QK_HEAD_DIM * (h + 1)] = dwq_ref[
                        rows * j:rows * (j + 1), HEAD_PAD * h:HEAD_PAD * h + QK_HEAD_DIM].astype(BF16)

    full = lambda shape: pl.BlockSpec(shape, lambda i: (0, 0))
    row = lambda w, c=0: pl.BlockSpec((tm, w), lambda i, c=c: (i, c))
    hbm = pl.BlockSpec(memory_space=pl.ANY)
    res = pl.pallas_call(
        body, name="mla_bwd", grid=(nsteps,),
        in_specs=[row(hw), row(hw), row(hw), row(CQ_PAD, 0), row(LANES, CQ_PAD // LANES),
                  pl.BlockSpec((D_MODEL, tm), lambda i: (0, i)),
                  full((1, CQ_PAD)), full((1, KV_LORA_RANK)), full((CQ_PAD, hw)), full((KV_LORA_RANK, hw)),
                  full((KV_LORA_RANK, hw)), row(LANES), row(LANES), row(LANES)] + [hbm] * npart,
        out_specs=[row(SEG_C), pl.BlockSpec((N_DEV, Q_LORA_RANK // N_DEV, MLA_HEADS * QK_HEAD_DIM), lambda i: (0, 0, 0)),
                   full((KV_LORA_RANK, hw)), full((KV_LORA_RANK, hw)),
                   full((1, CQ_PAD)), full((1, KV_LORA_RANK)), full((1, SEG_C)), full((D_MODEL, SEG_C))] + [hbm] * npart,
        out_shape=[jax.ShapeDtypeStruct((SEQ, SEG_C), BF16),
                   jax.ShapeDtypeStruct((N_DEV, Q_LORA_RANK // N_DEV, MLA_HEADS * QK_HEAD_DIM), BF16),
                   jax.ShapeDtypeStruct((KV_LORA_RANK, hw), F32), jax.ShapeDtypeStruct((KV_LORA_RANK, hw), F32),
                   jax.ShapeDtypeStruct((1, CQ_PAD), F32), jax.ShapeDtypeStruct((1, KV_LORA_RANK), F32),
                   jax.ShapeDtypeStruct((1, SEG_C), F32), jax.ShapeDtypeStruct((D_MODEL, SEG_C), BF16)]
        + [jax.ShapeDtypeStruct(p.shape, p.dtype) for p in parts],
        scratch_shapes=[pltpu.VMEM((tm, hw), BF16), pltpu.VMEM((CQ_PAD, hw), F32), pltpu.VMEM((D_MODEL, SEG_C), F32)]
        + _exchange_sems(npart),
        compiler_params=pltpu.CompilerParams(dimension_semantics=("arbitrary",), vmem_limit_bytes=VMEM_MID),
    )(dq, dk, dv, h_c, h_c, xt_bf, gq, gkv, wq, wkn, wv, c_t, sa_t, sb_t, *parts)
    return res[:8], res[8:]


def _adamw_all(ws, gs, ms, vs):
    n = len(ws)
    c1 = 1.0 / (1.0 - ADAM_B1 ** ADAM_STEP)
    c2 = 1.0 / (1.0 - ADAM_B2 ** ADAM_STEP)

    def body(*refs):
        for idx in range(n):
            w, g, m, v = (refs[idx][...], refs[n + idx][...], refs[2 * n + idx][...], refs[3 * n + idx][...])
            m_new = ADAM_B1 * m + (1.0 - ADAM_B1) * g
            v_new = ADAM_B2 * v + (1.0 - ADAM_B2) * (g * g)
            delta = -ADAM_LR * ((m_new * c1) / (jnp.sqrt(v_new * c2) + ADAM_EPS) + ADAM_WD * w)
            refs[4 * n + idx][...] = delta
            refs[5 * n + idx][...] = m_new
            refs[6 * n + idx][...] = v_new

    shapes = [jax.ShapeDtypeStruct(w.shape, F32) for w in ws]
    outs = pl.pallas_call(
        body, name="adamw", out_shape=shapes * 3,
        compiler_params=pltpu.CompilerParams(vmem_limit_bytes=VMEM_BIG),
    )(*ws, *gs, *ms, *vs)
    return outs[:n], outs[n:2 * n], outs[2 * n:]


SHARD_W = IN_WIDTH // N_DEV

_PIECES = [(0, 384, 2, 0), (384, 512, 2, CQ_PAD), (512, 544, 2, CQ_PAD + LANES + ROPE_LO),
           (544, 1056, 0, 2 * D_MODEL), (1056, 1568, 1, 0), (1568, 2080, 1, SGU_WIDTH),
           (2080, 2592, 1, 2 * SGU_WIDTH), (2592, 3616, 0, 0), (3616, 4640, 0, D_MODEL)]


def _column_runs():
    runs = []
    for n0, n1, seg, d0 in _PIECES:
        for j in range(N_DEV):
            lo, hi = max(n0, j * SHARD_W), min(n1, (j + 1) * SHARD_W)
            if lo < hi:
                runs.append((j, lo - j * SHARD_W, hi - j * SHARD_W, seg, d0 + lo - n0))
    return runs


def _mesh_pos():
    return lax.axis_index("x"), lax.axis_index("y"), lax.axis_index("c")


def _remote(src, dst, send_sems, recv_sems, k, to):
    return pltpu.make_async_remote_copy(src_ref=src, dst_ref=dst, send_sem=send_sems.at[k], recv_sem=recv_sems.at[k],
                                        device_id=to, device_id_type=pl.DeviceIdType.MESH)


def _gather_exchange(gats, send_sems, recv_sems, meanwhile=None):
    x, y, c = _mesh_pos()
    me, sibling = (x, y, c), (x, y, 1 - c)
    chips = [(1 - x, y), (x, 1 - y), (1 - x, 1 - y)]

    def copy(a, k, blk, to):
        slab = gats[a].at[4 * blk[0] + 2 * blk[1] + blk[2]]
        return _remote(slab, slab, send_sems, recv_sems, 7 * a + k, to)

    arrays = range(len(gats))
    first = [copy(a, 1 + j, me, (*chip, c)) for j, chip in enumerate(chips) for a in arrays]
    first += [copy(a, 0, me, sibling) for a in arrays]
    for cp in first:
        cp.start()
    if meanwhile is not None:
        meanwhile()
    passed = []
    for j, chip in enumerate(chips):
        for a in arrays:
            copy(a, 1 + j, (*chip, c), me).wait_recv()
            fwd = copy(a, 4 + j, (*chip, c), sibling)
            fwd.start()
            passed.append(fwd)
    for a in arrays:
        copy(a, 0, sibling, me).wait_recv()
    for j, chip in enumerate(chips):
        for a in arrays:
            copy(a, 4 + j, (*chip, 1 - c), me).wait_recv()
    for cp in first + passed:
        cp.wait_send()


def _gather_behind(own, gats, send_sems, recv_sems, local_sems, step, mid, last):
    x, y, c = _mesh_pos()
    me, sibling = (x, y, c), (x, y, 1 - c)
    chips = [(1 - x, y), (x, 1 - y), (1 - x, 1 - y)]
    arrays = range(len(gats))

    def copy(a, k, blk, to, src=None):
        slab = gats[a].at[4 * blk[0] + 2 * blk[1] + blk[2]]
        return _remote(slab if src is None else src, slab, send_sems, recv_sems, 7 * a + k, to)

    first = [copy(a, 1 + j, me, (*chip, c), src=own[a]) for j, chip in enumerate(chips) for a in arrays]
    first += [copy(a, 0, me, sibling, src=own[a]) for a in arrays]
    local = [pltpu.make_async_copy(own[a], gats[a].at[4 * x + 2 * y + c], local_sems.at[a]) for a in arrays]
    passed = [copy(a, 4 + j, (*chip, c), sibling) for j, chip in enumerate(chips) for a in arrays]

    @pl.when(step == 0)
    def _():
        for cp in first + local:
            cp.start()

    @pl.when(step == mid)
    def _():
        for j, chip in enumerate(chips):
            for a in arrays:
                copy(a, 1 + j, (*chip, c), me).wait_recv()
        for cp in passed:
            cp.start()

    @pl.when(step == last)
    def _():
        for a in arrays:
            copy(a, 0, sibling, me).wait_recv()
        for j, chip in enumerate(chips):
            for a in arrays:
                copy(a, 4 + j, (*chip, 1 - c), me).wait_recv()
        for cp in first + passed:
            cp.wait_send()
        for cp in local:
            cp.wait()


def _gather_first(w_in, w_uq2, w_oa, w_ob, w_out, x2, pos_col, invf_lane):
    hw = MLA_HEADS * HEAD_PAD
    uq_rows = Q_LORA_RANK // N_DEV
    rows = 256

    def body(win_ref, wuq_ref, woa_ref, wob_ref, wout_ref, x_ref, pos_ref, invf_ref,
             wc_ref, wq_ref, winb_ref, oab_ref, obb_ref, outb_ref, xb_ref, xt_ref, c_ref, sa_ref, sb_ref,
             g_uq, blk0, send_sems, recv_sems):
        def local_work():
            for i in range(SEQ // rows):
                xi = x_ref[rows * i:rows * (i + 1), :]
                xb_ref[rows * i:rows * (i + 1), :] = xi.astype(BF16)
                xt_ref[:, rows * i:rows * (i + 1)] = xi.T.astype(BF16)
            ang = pos_ref[...].astype(F32) * invf_ref[...]
            cs, sn = jnp.cos(ang), jnp.sin(ang)
            lane = lax.broadcasted_iota(jnp.int32, ang.shape, 1)
            c_ref[...] = jnp.where(lane < ROPE_LO, 1.0, jnp.where(lane < ROPE_HI, cs, 0.0))
            sa_ref[...] = jnp.where(jnp.logical_and(lane >= ROPE_LO, lane < ROPE_MID), -sn, 0.0)
            sb_ref[...] = jnp.where(jnp.logical_and(lane >= ROPE_MID, lane < ROPE_HI), sn, 0.0)

        x, y, c = _mesh_pos()
        me = (x, y, c)
        winb_ref[...] = win_ref[0].astype(BF16)
        oab_ref[...] = woa_ref[0].astype(BF16)
        obb_ref[...] = wob_ref[0].astype(BF16)
        outb_ref[...] = wout_ref[0].astype(BF16)
        g_uq[4 * x + 2 * y + c] = wuq_ref[...].astype(BF16)

        chip0 = jnp.logical_and(x == 0, y == 0)
        south = c == 0
        half = D_MODEL // 2
        halves = [blk0.at[pl.ds(0, half)], blk0.at[pl.ds(half, half)]]

        def bcopy(k, to, part=None):
            ref = blk0 if part is None else halves[part]
            return _remote(ref, ref, send_sems, recv_sems, 7 + k, to)

        sends0 = [(0, (0, 0, 1), None), (1, (1, 0, 0), 0), (2, (0, 1, 0), 1), (3, (1, 0, 0), 1), (4, (0, 1, 0), 0)]

        @pl.when(jnp.logical_and(chip0, south))
        def _():
            blk0[...] = winb_ref[...]
            for k, to, part in sends0:
                bcopy(k, to, part).start()

        _gather_exchange([g_uq], send_sems, recv_sems, meanwhile=local_work)

        for (cx, cy), first_k, first_half, second_k in (((1, 0), 1, 0, 3), ((0, 1), 2, 1, 4)):
            @pl.when(jnp.logical_and(jnp.logical_and(x == cx, y == cy), south))
            def _(cx=cx, cy=cy, first_k=first_k, first_half=first_half, second_k=second_k):
                bcopy(first_k, me, first_half).wait_recv()
                onward = bcopy(5 + first_half, (1, 1, 0), first_half)
                onward.start()
                bcopy(second_k, me, 1 - first_half).wait_recv()
                north = bcopy(7, (cx, cy, 1))
                north.start()
                onward.wait_send()
                north.wait_send()

        @pl.when(jnp.logical_and(jnp.logical_and(x == 1, y == 1), south))
        def _():
            bcopy(5, me, 0).wait_recv()
            bcopy(6, me, 1).wait_recv()
            north = bcopy(7, (1, 1, 1))
            north.start()
            north.wait_send()

        @pl.when(jnp.logical_and(chip0, c == 1))
        def _():
            bcopy(0, me).wait_recv()

        @pl.when(jnp.logical_and(jnp.logical_not(chip0), c == 1))
        def _():
            bcopy(7, me).wait_recv()

        @pl.when(jnp.logical_and(chip0, south))
        def _():
            for k, to, part in sends0:
                bcopy(k, to, part).wait_send()

        for j, s0, s1, seg, d0 in _column_runs():
            if seg == 2:
                wc_ref[:, d0:d0 + (s1 - s0)] = blk0[:, s0:s1]
        zeros = lambda r, w: jnp.zeros((r, w), BF16)
        wc_ref[:, Q_LORA_RANK:CQ_PAD] = zeros(D_MODEL, CQ_PAD - Q_LORA_RANK)
        wc_ref[:, CQ_PAD + LANES:CQ_PAD + LANES + ROPE_LO] = zeros(D_MODEL, ROPE_LO)
        wc_ref[:, CQ_PAD + LANES + ROPE_HI:SEG_C] = zeros(D_MODEL, LANES - ROPE_HI)
        wq_ref[Q_LORA_RANK:CQ_PAD, :] = zeros(CQ_PAD - Q_LORA_RANK, hw)
        for h in range(MLA_HEADS):
            wq_ref[0:Q_LORA_RANK, HEAD_PAD * h + QK_HEAD_DIM:HEAD_PAD * (h + 1)] = zeros(Q_LORA_RANK, HEAD_PAD - QK_HEAD_DIM)
        for j in range(N_DEV):
            for h in range(MLA_HEADS):
                wq_ref[uq_rows * j:uq_rows * (j + 1), HEAD_PAD * h:HEAD_PAD * h + QK_HEAD_DIM] = g_uq[
                    j, :, QK_HEAD_DIM * h:QK_HEAD_DIM * (h + 1)]

    vmem = pl.BlockSpec(memory_space=pltpu.VMEM)
    return pl.pallas_call(
        body, name="gather_first",
        out_shape=[jax.ShapeDtypeStruct((D_MODEL, SEG_C), BF16), jax.ShapeDtypeStruct((CQ_PAD, hw), BF16),
                   jax.ShapeDtypeStruct(w_in.shape[1:], BF16), jax.ShapeDtypeStruct(w_oa.shape[1:], BF16),
                   jax.ShapeDtypeStruct(w_ob.shape[1:], BF16), jax.ShapeDtypeStruct(w_out.shape[1:], BF16),
                   jax.ShapeDtypeStruct((SEQ, D_MODEL), BF16), jax.ShapeDtypeStruct((D_MODEL, SEQ), BF16)]
        + [jax.ShapeDtypeStruct((SEQ, LANES), F32)] * 3,
        in_specs=[vmem] * 8, out_specs=[vmem] * 11,
        scratch_shapes=[pltpu.VMEM((N_DEV, uq_rows, MLA_HEADS * QK_HEAD_DIM), BF16), pltpu.VMEM((D_MODEL, SHARD_W), BF16),
                        pltpu.SemaphoreType.DMA((15,)), pltpu.SemaphoreType.DMA((15,))],
        compiler_params=pltpu.CompilerParams(vmem_limit_bytes=VMEM_BIG),
    )(w_in, w_uq2, w_oa, w_ob, w_out, x2, pos_col, invf_lane)


def _assemble_in(g_in):
    def body(g_ref, wa_ref, wb_ref):
        segs = [wa_ref, wb_ref]
        for j, s0, s1, seg, d0 in _column_runs():
            if seg < 2:
                segs[seg][:, d0:d0 + (s1 - s0)] = g_ref[j, :, s0:s1]

    return pl.pallas_call(
        body, name="assemble_in",
        out_shape=[jax.ShapeDtypeStruct((D_MODEL, SEG_A), BF16), jax.ShapeDtypeStruct((D_MODEL, SEG_B), BF16)],
        compiler_params=pltpu.CompilerParams(vmem_limit_bytes=VMEM_MID),
    )(g_in)


def _assemble_out(g_oa, g_ob, g_out):
    cols = D_MODEL // N_DEV

    def body(goa_ref, gob_ref, gout_ref, oa_ref, ob_ref, out_ref):
        for j in range(N_DEV):
            oa_ref[:, cols * j:cols * (j + 1)] = goa_ref[j]
            ob_ref[:, cols * j:cols * (j + 1)] = gob_ref[j]
            out_ref[cols * j:cols * (j + 1), :] = gout_ref[j]

    return pl.pallas_call(
        body, name="assemble_out",
        out_shape=[jax.ShapeDtypeStruct((MLA_WIDTH, D_MODEL), BF16), jax.ShapeDtypeStruct((SGU_WIDTH, D_MODEL), BF16),
                   jax.ShapeDtypeStruct((D_MODEL, D_MODEL), BF16)],
    )(g_oa, g_ob, g_out)


C_NAT = 544


P_IN_SPLIT = 896


def _to_parts(dwa, dwb):
    def body(dwa_ref, dwb_ref, phi_ref, plo_ref):
        phi_ref[0, :, 0:C_NAT] = jnp.zeros((P_IN_SPLIT, C_NAT), BF16)
        plo_ref[0, :, 0:C_NAT] = jnp.zeros((D_MODEL - P_IN_SPLIT, C_NAT), BF16)
        segs = [dwa_ref, dwb_ref]
        for j, s0, s1, seg, d0 in _column_runs():
            if seg < 2:
                phi_ref[j, :, s0:s1] = segs[seg][0:P_IN_SPLIT, d0:d0 + (s1 - s0)]
                plo_ref[j, :, s0:s1] = segs[seg][P_IN_SPLIT:D_MODEL, d0:d0 + (s1 - s0)]

    return pl.pallas_call(
        body, name="to_parts",
        out_shape=[jax.ShapeDtypeStruct((N_DEV, P_IN_SPLIT, SHARD_W), BF16),
                   jax.ShapeDtypeStruct((N_DEV, D_MODEL - P_IN_SPLIT, SHARD_W), BF16)],
        compiler_params=pltpu.CompilerParams(vmem_limit_bytes=VMEM_MID))(dwa, dwb)


def _dx_tail(dhs, ws, dx_res, dwc, p_uq, p_rep):
    tm = SEQ // 4
    rep_rows = p_rep.shape[1]
    c_rows = D_MODEL // N_DEV
    spec = [((c_rows, C_NAT), BF16), (p_uq.shape[1:], BF16), ((rep_rows, LANES), F32)]
    n = len(spec)

    nseg = len(dhs)

    def body(*refs):
        dh_refs, w_refs = refs[:nseg], refs[nseg:2 * nseg]
        dxr_ref, dwc_ref, puq_ref, prep_ref, dx_ref, call_ref, guq_ref, repall_ref, pc_ref, c_all, rep_all = refs[
            2 * nseg:2 * nseg + 11]
        rest = refs[2 * nseg + 11:]
        ras, tbs, rbs = rest[0:n], rest[n:2 * n], rest[2 * n:3 * n]
        send_sems, recv_sems, gsend, grecv = rest[3 * n:]
        step = pl.program_id(0)
        x, y, c = _mesh_pos()
        me_idx = 4 * x + 2 * y + c
        me, sibling = (x, y, c), (x, y, 1 - c)
        others = [(1 - x, y), (x, 1 - y), (1 - x, 1 - y)]
        parts = [pc_ref, puq_ref, prep_ref]
        gats = [rep_all, c_all]

        def stage1(chip, a):
            return _remote(parts[a].at[2 * chip + (1 - c)], ras[a].at[chip], send_sems, recv_sems, 7 * a + chip, sibling)

        def stage2(k, a):
            cx, cy = others[k]
            return _remote(tbs[a].at[k], rbs[a].at[k], send_sems, recv_sems, 7 * a + 4 + k, (cx, cy, c))

        def gcopy(a, k, blk, to):
            slab = gats[a].at[4 * blk[0] + 2 * blk[1] + blk[2]]
            return _remote(slab, slab, gsend, grecv, 7 * a + k, to)

        def chip_sum(a, chip):
            return parts[a][2 * chip + c].astype(F32) + ras[a][chip].astype(F32)

        @pl.when(step == 0)
        def _():
            for j, s0, s1, seg, d0 in _column_runs():
                if seg == 2:
                    for r in range(N_DEV):
                        pc_ref[r, :, s0:s1] = dwc_ref[c_rows * r:c_rows * (r + 1), d0:d0 + (s1 - s0)]
            for chip in range(4):
                for a in range(n):
                    stage1(chip, a).start()

        @pl.when(step == 1)
        def _():
            for chip in range(4):
                for a in range(n):
                    stage1(chip, a).wait_recv()
            for k, (cx, cy) in enumerate(others):
                for a in range(n):
                    tbs[a][k] = chip_sum(a, 2 * cx + cy).astype(spec[a][1])
                    stage2(k, a).start()

        @pl.when(step == 2)
        def _():
            for k in range(3):
                for a in range(n):
                    stage2(k, a).wait_recv()
            sums = []
            for a in range(n):
                acc = chip_sum(a, 2 * x + y)
                for k in range(3):
                    acc = acc + rbs[a][k].astype(F32)
                sums.append(acc)
            c_all[me_idx] = sums[0].astype(BF16)
            guq_ref[...] = sums[1]
            rep_all[me_idx] = sums[2]
            for a in range(2):
                for j, chip in enumerate(others):
                    gcopy(a, 1 + j, me, (*chip, c)).start()
                gcopy(a, 0, me, sibling).start()

        @pl.when(step == 3)
        def _():
            for j, chip in enumerate(others):
                for a in range(2):
                    gcopy(a, 1 + j, (*chip, c), me).wait_recv()
                    gcopy(a, 4 + j, (*chip, c), sibling).start()
            for a in range(2):
                gcopy(a, 0, sibling, me).wait_recv()
                for j, chip in enumerate(others):
                    gcopy(a, 4 + j, (*chip, 1 - c), me).wait_recv()
            for a in range(2):
                gcopy(a, 0, me, sibling).wait_send()
                for j, chip in enumerate(others):
                    gcopy(a, 1 + j, me, (*chip, c)).wait_send()
                    gcopy(a, 4 + j, (*chip, c), sibling).wait_send()
            for a in range(n):
                for chip in range(4):
                    stage1(chip, a).wait_send()
                for k in range(3):
                    stage2(k, a).wait_send()
            call_ref[...] = c_all[...]
            repall_ref[...] = rep_all[...]

        acc = dxr_ref[...]
        for dh_ref, w_ref in zip(dh_refs, w_refs):
            acc = acc + _dot(dh_ref[...], w_ref[...], _NT)
        dx_ref[...] = acc

    row = lambda w: pl.BlockSpec((tm, w), lambda i: (i, 0))
    full = lambda shape: pl.BlockSpec(shape, lambda i: (0,) * len(shape))
    scratch = [pltpu.VMEM((N_DEV, c_rows, C_NAT), BF16), pltpu.VMEM((N_DEV, c_rows, C_NAT), BF16),
               pltpu.VMEM((N_DEV, rep_rows, LANES), F32)]
    for lead in (4, 3, 3):
        scratch += [pltpu.VMEM((lead,) + tuple(shape), dt) for shape, dt in spec]
    scratch += [pltpu.SemaphoreType.DMA((7 * n,)), pltpu.SemaphoreType.DMA((7 * n,)),
                pltpu.SemaphoreType.DMA((14,)), pltpu.SemaphoreType.DMA((14,))]
    return pl.pallas_call(
        body, name="dx_tail", grid=(SEQ // tm,),
        in_specs=[row(dh.shape[1]) for dh in dhs] + [full(w.shape) for w in ws]
        + [row(D_MODEL), full(dwc.shape), full(p_uq.shape), full(p_rep.shape)],
        out_specs=[row(D_MODEL), full((N_DEV, c_rows, C_NAT)), full(p_uq.shape[1:]), full((N_DEV, rep_rows, LANES))],
        out_shape=[jax.ShapeDtypeStruct((SEQ, D_MODEL), F32), jax.ShapeDtypeStruct((N_DEV, c_rows, C_NAT), BF16),
                   jax.ShapeDtypeStruct(p_uq.shape[1:], F32), jax.ShapeDtypeStruct((N_DEV, rep_rows, LANES), F32)],
        scratch_shapes=scratch,
        compiler_params=pltpu.CompilerParams(dimension_semantics=("arbitrary",), vmem_limit_bytes=VMEM_BIG),
    )(*dhs, *ws, dx_res, dwc, p_uq, p_rep)


def _sum_landed(landed, c_all):
    c_rows = D_MODEL // N_DEV

    def body(rhi_ref, rlo_ref, roa_ref, rob_ref, rout_ref, call_ref, gin_ref, goa_ref, gob_ref, gout_ref):
        def total(ref, sl):
            acc = ref[0, sl, :].astype(F32)
            for s in range(1, N_DEV):
                acc = acc + ref[s, sl, :].astype(F32)
            return acc

        x, y, c = _mesh_pos()
        dev0 = jnp.where(4 * x + 2 * y + c == 0, 1.0, 0.0)
        for j in range(N_DEV):
            sl = slice(c_rows * j, c_rows * (j + 1))
            below = c_rows * j < P_IN_SPLIT
            tot = total(rhi_ref, sl) if below else total(rlo_ref, slice(c_rows * j - P_IN_SPLIT, c_rows * (j + 1) - P_IN_SPLIT))
            gin_ref[0, sl, C_NAT:SHARD_W] = tot[:, C_NAT:SHARD_W]
            gin_ref[0, sl, 0:C_NAT] = tot[:, 0:C_NAT] + dev0 * call_ref[j].astype(F32)
        goa_ref[0] = total(roa_ref, slice(None))
        gob_ref[0] = total(rob_ref, slice(None))
        gout_ref[0] = total(rout_ref, slice(None))

    return pl.pallas_call(
        body, name="sum_landed",
        out_shape=[jax.ShapeDtypeStruct((1, D_MODEL, SHARD_W), F32)]
        + [jax.ShapeDtypeStruct((1,) + r.shape[1:], F32) for r in landed[2:]],
        compiler_params=pltpu.CompilerParams(vmem_limit_bytes=VMEM_MID),
    )(*landed, c_all)


_O_CQ, _O_CKV, _O_KPE, _O_ZA, _O_U, _O_V, _O_ZB, _O_GA, _O_GB = 0, 384, 512, 544, 1056, 1568, 2080, 2592, 3616


def _to_segments(w):
    z = lambda n: jnp.zeros(w.shape[:-1] + (n,), w.dtype)
    seg_a = jnp.concatenate([w[..., _O_GA:_O_GB], w[..., _O_GB:IN_WIDTH], w[..., _O_ZA:_O_U]], axis=-1)
    seg_b = jnp.concatenate([w[..., _O_U:_O_V], w[..., _O_V:_O_ZB], w[..., _O_ZB:_O_GA]], axis=-1)
    seg_c = jnp.concatenate([w[..., _O_CQ:_O_CKV], z(CQ_PAD - Q_LORA_RANK), w[..., _O_CKV:_O_KPE],
                             z(ROPE_LO), w[..., _O_KPE:_O_ZA], z(LANES - ROPE_HI)], axis=-1)
    return seg_a, seg_b, seg_c


def _from_segments(seg_a, seg_b, seg_c):
    kpe0 = CQ_PAD + LANES + ROPE_LO
    return jnp.concatenate([
        seg_c[..., 0:Q_LORA_RANK], seg_c[..., CQ_PAD:CQ_PAD + LANES], seg_c[..., kpe0:kpe0 + QK_ROPE_DIM],
        seg_a[..., 2 * D_MODEL:SEG_A], seg_b, seg_a[..., 0:2 * D_MODEL]], axis=-1)


def kernel(x, positions, w_in, b_in, g_q, w_uq, g_kv, w_ukv, w_oa, sgu_ln_g, sgu_ln_b, w_s, b_s, w_ob, w_out, ln_g, ln_b, loss_target, m_w_in, m_b_in, m_g_q, m_w_uq, m_g_kv, m_w_ukv, m_w_oa, m_sgu_ln_g, m_sgu_ln_b, m_w_s, m_b_s, m_w_ob, m_w_out, m_ln_g, m_ln_b, v_w_in, v_b_in, v_g_q, v_w_uq, v_g_kv, v_w_ukv, v_w_oa, v_sgu_ln_g, v_sgu_ln_b, v_w_s, v_b_s, v_w_ob, v_w_out, v_ln_g, v_ln_b):
    w_uq2 = w_uq[0].reshape(Q_LORA_RANK // N_DEV, MLA_HEADS * QK_HEAD_DIM)
    inv_freq = ROPE_THETA ** (-jnp.arange(0, QK_ROPE_DIM, 2, dtype=F32) / QK_ROPE_DIM)
    invf_lane = jnp.concatenate([jnp.zeros((ROPE_LO,), F32), inv_freq, inv_freq,
                                 jnp.zeros((LANES - ROPE_HI,), F32)]).reshape(1, LANES)
    first = _gather_first(w_in, w_uq2, w_oa, w_ob, w_out, x[0], positions.reshape(SEQ, 1), invf_lane)
    partials = _local_step(x[0], loss_target[0], first, b_in, g_q, g_kv, w_ukv, sgu_ln_g, sgu_ln_b, w_s, b_s, ln_g, ln_b)
    weights = dict(w_in=w_in, b_in=b_in, g_q=g_q, w_uq=w_uq, g_kv=g_kv, w_ukv=w_ukv, w_oa=w_oa, sgu_ln_g=sgu_ln_g,
                   sgu_ln_b=sgu_ln_b, w_s=w_s, b_s=b_s, w_ob=w_ob, w_out=w_out, ln_g=ln_g, ln_b=ln_b)
    moms = dict(w_in=m_w_in, b_in=m_b_in, g_q=m_g_q, w_uq=m_w_uq, g_kv=m_g_kv, w_ukv=m_w_ukv, w_oa=m_w_oa,
                sgu_ln_g=m_sgu_ln_g, sgu_ln_b=m_sgu_ln_b, w_s=m_w_s, b_s=m_b_s, w_ob=m_w_ob, w_out=m_w_out,
                ln_g=m_ln_g, ln_b=m_ln_b)
    vars_ = dict(w_in=v_w_in, b_in=v_b_in, g_q=v_g_q, w_uq=v_w_uq, g_kv=v_g_kv, w_ukv=v_w_ukv, w_oa=v_w_oa,
                 sgu_ln_g=v_sgu_ln_g, sgu_ln_b=v_sgu_ln_b, w_s=v_w_s, b_s=v_b_s, w_ob=v_w_ob, w_out=v_w_out,
                 ln_g=v_ln_g, ln_b=v_ln_b)
    return _reduce_and_update(partials, weights, moms, vars_)


def _local_step(x2, tgt, first, b_in, g_q, g_kv, w_ukv, sgu_ln_g, sgu_ln_b, w_s, b_s, ln_g, ln_b):
    wc, wq, win_b, oa_b, ob_b, out_b, x_bf, xt_bf, c_t, sa_t, sb_t = first
    ba, bb, bc = _to_segments(b_in)
    w_ukv_bf = w_ukv[0].astype(BF16)
    wkn = jnp.pad(w_ukv_bf[:, :, :QK_NOPE_DIM], ((0, 0), (0, 0), (0, HEAD_PAD - QK_NOPE_DIM))).reshape(KV_LORA_RANK, -1)
    wv = jnp.pad(w_ukv_bf[:, :, QK_NOPE_DIM:], ((0, 0), (0, 0), (0, HEAD_PAD - V_HEAD_DIM))).reshape(KV_LORA_RANK, -1)
    gq = jnp.pad(g_q, ((0, 0), (0, CQ_PAD - Q_LORA_RANK)))
    bias_full = jnp.repeat(b_s[0].T, SGU_GROUP_DIM, axis=1)
    w_s3 = w_s[0]
    w_st3 = jnp.swapaxes(w_s3, 1, 2)

    h_c = _mm(x_bf, wc, bias=bc, tm=512, tn=SEG_C, name="in_proj_c")
    q, k, kt, vx, vxt = _mla_prep(h_c, gq, g_kv, wq, wkn, wv, c_t, sa_t, sb_t)
    o, lse, (g_in,) = _attn_fwd(q, kt, vx, (win_b,))
    wa, wb = _assemble_in(g_in)
    h_a, (g_out,) = _mm(x_bf, wa, bias=ba, own=(out_b,), tm=512, tn=SEG_A // 2, name="in_proj_a")
    h_b, (g_oa, g_ob) = _mm(x_bf, wb, bias=bb, own=(oa_b, ob_b), tm=512, tn=SEG_B // 2, name="in_proj_b")
    y_b = _sgu_fwd(h_b, sgu_ln_g, sgu_ln_b, w_s3, bias_full)
    w_oa_f, w_ob_f, w_out_f = _assemble_out(g_oa, g_ob, g_out)

    (loss_row, dx_res, dh_a, d_o, d_yb, p_oa, p_ob, p_out, d_lng, d_lnb, d_ba) = _merge(
        x2, o, h_a, y_b, tgt, w_oa_f, w_ob_f, w_out_f, ln_g, ln_b)
    (dh_b, d_ws, d_bs_t, d_slg, d_slb, d_bb), (r_out,) = _sgu_bwd(h_b, d_yb, sgu_ln_g, sgu_ln_b, w_s3, w_st3, bias_full,
                                                                 (p_out,))
    d_wa, (r_oa,) = _mm(xt_bf, dh_a, out_dtype=BF16, parts=(p_oa,), tm=512, tn=512, name="dw_in_a")
    d_wb = _mm(xt_bf, dh_b, out_dtype=BF16, tm=512, tn=512, name="dw_in_b")
    p_hi, p_lo = _to_parts(d_wa, d_wb)
    dq, dk, dv, (r_hi,) = _attn_bwd(q, kt, k, vxt, d_o, o, lse, (p_hi,))
    (dh_c, p_uq, d_wkn, d_wv, d_gq, d_gkv, d_bc, d_wc), (r_lo, r_ob) = _mla_bwd(
        dq, dk, dv, h_c, xt_bf, gq, g_kv, wq, wkn, wv, c_t, sa_t, sb_t, (p_lo, p_ob))
    landed = (r_hi, r_lo, r_oa, r_ob, r_out)

    p_b_in = _from_segments(d_ba, d_bb, d_bc)
    p_w_ukv = jnp.concatenate([d_wkn.reshape(KV_LORA_RANK, MLA_HEADS, HEAD_PAD)[:, :, :QK_NOPE_DIM],
                               d_wv.reshape(KV_LORA_RANK, MLA_HEADS, HEAD_PAD)[:, :, :V_HEAD_DIM]], axis=-1)
    p_g_q = d_gq[:, :Q_LORA_RANK]
    p_b_s = d_bs_t[:, :SGU_GROUPS].T
    replicated = [p_b_in, p_g_q, d_gkv, p_w_ukv, d_slg, d_slb, d_ws, p_b_s, d_lng, d_lnb]
    return loss_row, ((dh_a, dh_b, dh_c), (wa, wb, wc), dx_res), landed, d_wc, p_uq, replicated


_NAMES = ["w_in", "b_in", "g_q", "w_uq", "g_kv", "w_ukv", "w_oa", "sgu_ln_g", "sgu_ln_b", "w_s", "b_s", "w_ob",
          "w_out", "ln_g", "ln_b"]
_REPLICATED = ["b_in", "g_q", "g_kv", "w_ukv", "sgu_ln_g", "sgu_ln_b", "w_s", "b_s", "ln_g", "ln_b"]


def _reduce_and_update(partials, weights, moms, vars_):
    loss_row, (dhs, ws, dx_res), landed, d_wc, p_uq, replicated = partials
    def piece(a):
        flat = a.reshape(-1)
        return jnp.pad(flat, (0, -flat.size % PACK_ALIGN))

    rep_flat = jnp.concatenate([piece(a) for a in replicated] + [piece(loss_row[0, :1])])
    rep_flat = jnp.pad(rep_flat, (0, N_DEV * PACK_R_ROWS * LANES - rep_flat.size))
    dx_ab, c_all, g_uq, rep_all = _dx_tail(dhs[:2], ws[:2], dx_res, d_wc, p_uq,
                                           rep_flat.reshape(N_DEV, PACK_R_ROWS, LANES))
    dx = _mm(dhs[2], ws[2], tb=True, add=dx_ab, tm=512, tn=D_MODEL, name="dx_c")
    g_in, g_oa, g_ob, g_out = _sum_landed(landed, c_all)
    rep_sum = rep_all.reshape(-1)
    grads, pos = dict(w_in=g_in, w_uq=g_uq, w_oa=g_oa, w_ob=g_ob, w_out=g_out), 0
    for nm in _REPLICATED:
        grads[nm] = rep_sum[pos:pos + weights[nm].size]
        pos += weights[nm].size + -weights[nm].size % PACK_ALIGN
    loss = rep_sum[pos]
    grads = {nm: grads[nm].reshape(weights[nm].shape) for nm in _NAMES}
    deltas, new_m, new_v = _adamw_all([weights[nm] for nm in _NAMES], [grads[nm] for nm in _NAMES],
                                      [moms[nm] for nm in _NAMES], [vars_[nm] for nm in _NAMES])
    return (loss, dx.reshape(1, SEQ, D_MODEL), *[grads[nm] for nm in _NAMES], *deltas, *new_m, *new_v)
```

```python
import math

import jax
import jax.numpy as jnp
from jax import lax
from jax.experimental import pallas as pl
from jax.experimental.pallas import tpu as pltpu

F32 = jnp.float32
BF16 = jnp.bfloat16

D_MODEL = 1024
SEQ = 2048
N_DEV = 8
MLA_HEADS = 8
Q_LORA_RANK = 384
KV_LORA_RANK = 128
QK_NOPE_DIM = 64
QK_ROPE_DIM = 32
V_HEAD_DIM = 64
QK_HEAD_DIM = QK_NOPE_DIM + QK_ROPE_DIM
MLA_WIDTH = MLA_HEADS * V_HEAD_DIM
ROPE_THETA = 10000.0
SGU_GROUPS = 8
SGU_GROUP_DIM = 64
SGU_WIDTH = SGU_GROUPS * SGU_GROUP_DIM
CHUNK = 128
RMS_EPS = 1e-6
LN_EPS = 1e-5
DN_ALPHA = 2.0 ** 0.25
IN_WIDTH = 4640
ATTN_SCALE = QK_HEAD_DIM ** -0.5

ADAM_LR = 0.001
ADAM_B1 = 0.9
ADAM_B2 = 0.999
ADAM_EPS = 1e-08
ADAM_WD = 0.01
ADAM_STEP = 10

LANES = 128
HEAD_PAD = 128
ROPE_LO = QK_NOPE_DIM
ROPE_MID = ROPE_LO + QK_ROPE_DIM // 2
ROPE_HI = ROPE_LO + QK_ROPE_DIM
CQ_PAD = 512

SEG_A = 2560
SEG_B = 1536
SEG_C = 768

PACK_R_ROWS = 272
PACK_ALIGN = 8 * LANES
VMEM_BIG = 56 * 1024 * 1024
VMEM_MID = 40 * 1024 * 1024


def _sigmoid(x):
    return 1.0 / (1.0 + jnp.exp(-x))


def _gelu_and_grad(x):
    c0 = math.sqrt(2.0 / math.pi)
    x2 = x * x
    t = jnp.tanh(c0 * (x + 0.044715 * x * x2))
    g = 0.5 * x * (1.0 + t)
    dg = 0.5 * (1.0 + t) + 0.5 * x * (1.0 - t * t) * (c0 * (1.0 + 3.0 * 0.044715 * x2))
    return g, dg


def _dot(a, b, dims):
    return lax.dot_general(a, b, (dims, ((), ())), preferred_element_type=F32)


_NN = ((1,), (0,))
_NT = ((1,), (1,))
_TN = ((0,), (0,))


def _store_grad(dh_ref, db_ref, col, val):
    cols = slice(col, col + val.shape[1])
    dh_ref[:, cols] = val.astype(BF16)
    db_ref[:, cols] += jnp.sum(val, axis=0, keepdims=True)


def _mm(a, b, *, tb=False, bias=None, add=None, out_dtype=F32, tm, tn, name):
    m, k = a.shape
    n = b.shape[0] if tb else b.shape[1]
    assert m % tm == 0 and n % tn == 0
    dims = _NT if tb else _NN

    def body(*refs):
        a_ref, b_ref = refs[0], refs[1]
        pos = 2
        r = _dot(a_ref[...], b_ref[...], dims)
        if bias is not None:
            r = r + refs[pos][...]; pos += 1
        if add is not None:
            r = r + refs[pos][...]; pos += 1
        refs[pos][...] = r.astype(out_dtype)

    b_spec = pl.BlockSpec((tn, k), lambda j, i: (j, 0)) if tb else pl.BlockSpec((k, tn), lambda j, i: (0, j))
    in_specs, args = [pl.BlockSpec((tm, k), lambda j, i: (i, 0)), b_spec], [a, b]
    if bias is not None:
        in_specs.append(pl.BlockSpec((1, tn), lambda j, i: (0, j))); args.append(bias)
    if add is not None:
        in_specs.append(pl.BlockSpec((tm, tn), lambda j, i: (i, j))); args.append(add)
    return pl.pallas_call(
        body, name=name, grid=(n // tn, m // tm), in_specs=in_specs,
        out_specs=pl.BlockSpec((tm, tn), lambda j, i: (i, j)),
        out_shape=jax.ShapeDtypeStruct((m, n), out_dtype),
        compiler_params=pltpu.CompilerParams(dimension_semantics=("arbitrary", "arbitrary"), vmem_limit_bytes=VMEM_BIG),
    )(*args)


def _mm_pair(a, bs, *, biases=None, out_dtype=F32, own=(), parts=(), tm, tn, name):
    m, k = a.shape
    widths = [b.shape[1] for b in bs]
    assert m % tm == 0 and all(w % tn == 0 for w in widths) and not (own and parts)
    n0 = widths[0] // tn
    nblocks = n0 + widths[1] // tn
    nm = m // tm
    nsteps = nblocks * nm
    nown = len(own) + len(parts)
    nbias = 0 if biases is None else 2

    def body(*refs):
        a_ref, b_refs, bias_refs = refs[0], refs[1:3], refs[3:3 + nbias]
        own_refs = refs[3 + nbias:3 + nbias + nown]
        out_refs = refs[3 + nbias + nown:5 + nbias + nown]
        j = pl.program_id(0)
        for which, mine in enumerate((j < n0, j >= n0)):
            @pl.when(mine)
            def _(which=which):
                r = _dot(a_ref[...], b_refs[which][...], _NN)
                if nbias:
                    r = r + bias_refs[which][...]
                out_refs[which][...] = r.astype(out_dtype)
        if nown:
            gat_refs = refs[5 + nbias + nown:5 + nbias + 2 * nown]
            send_sems, recv_sems, local_sems = refs[5 + nbias + 2 * nown:]
            step = j * nm + pl.program_id(1)
            if own:
                _gather_behind(own_refs, gat_refs, send_sems, recv_sems, local_sems, step, nsteps - 2, nsteps - 1)
            else:
                exchange = _exchange_parts(own_refs, gat_refs, send_sems, recv_sems, local_sems)
                _exchange_start(step == 0, exchange)
                _exchange_finish(step == nsteps - 1, exchange)

    col = [lambda j: jnp.minimum(j, n0 - 1), lambda j: jnp.maximum(j - n0, 0)]
    row = [lambda j, i: jnp.where(j < n0, i, nm - 1), lambda j, i: jnp.where(j < n0, 0, i)]
    in_specs = [pl.BlockSpec((tm, k), lambda j, i: (i, 0))]
    in_specs += [pl.BlockSpec((k, tn), lambda j, i, c=c: (0, c(j))) for c in col]
    if nbias:
        in_specs += [pl.BlockSpec((1, tn), lambda j, i, c=c: (0, c(j))) for c in col]
    hbm = pl.BlockSpec(memory_space=pl.ANY)
    res = pl.pallas_call(
        body, name=name, grid=(nblocks, nm), in_specs=in_specs + [hbm] * nown,
        out_specs=[pl.BlockSpec((tm, tn), lambda j, i, r=r, c=c: (r(j, i), c(j))) for r, c in zip(row, col)] + [hbm] * nown,
        out_shape=[jax.ShapeDtypeStruct((m, w), out_dtype) for w in widths]
        + [jax.ShapeDtypeStruct((N_DEV,) + o.shape, o.dtype) for o in own]
        + [jax.ShapeDtypeStruct(p.shape, p.dtype) for p in parts],
        scratch_shapes=_exchange_sems(nown) if nown else [],
        compiler_params=pltpu.CompilerParams(dimension_semantics=("arbitrary", "arbitrary"), vmem_limit_bytes=VMEM_BIG),
    )(a, *bs, *(biases or ()), *own, *parts)
    return (res[:2], res[2:]) if nown else res[:2]


def _rope(x, c, sa, sb):
    return x * c + pltpu.roll(x, LANES - 16, 1) * sa + pltpu.roll(x, 16, 1) * sb


def _rope_t(dy, c, sa, sb):
    return dy * c + pltpu.roll(dy * sa, 16, 1) + pltpu.roll(dy * sb, LANES - 16, 1)


def _mla_prep(h_c, gq, gkv, wq, wkn, wvx, c_t, sa_t, sb_t):
    tm = 256
    hw = MLA_HEADS * HEAD_PAD

    def body(cq_ref, ckv_ref, kpe_ref, gq_ref, gkv_ref, wq_ref, wkn_ref, wvx_ref, c_ref, sa_ref, sb_ref,
             q_ref, k_ref, kt_ref, vx_ref, vxt_ref):
        c, sa, sb = c_ref[...], sa_ref[...], sb_ref[...]
        cq = cq_ref[...]
        rq = lax.rsqrt(jnp.sum(cq * cq, axis=1, keepdims=True) * (1.0 / Q_LORA_RANK) + RMS_EPS)
        cqn = ((cq * rq) * gq_ref[...]).astype(BF16)
        qall = _dot(cqn, wq_ref[...], _NN)
        for h in range(MLA_HEADS):
            sl = slice(HEAD_PAD * h, HEAD_PAD * (h + 1))
            q_ref[:, sl] = (_rope(qall[:, sl], c, sa, sb) * ATTN_SCALE).astype(BF16)
        ckv = ckv_ref[...]
        rkv = lax.rsqrt(jnp.sum(ckv * ckv, axis=1, keepdims=True) * (1.0 / KV_LORA_RANK) + RMS_EPS)
        ckvn = ((ckv * rkv) * gkv_ref[...]).astype(BF16)
        knall = _dot(ckvn, wkn_ref[...], _NN)
        vall = _dot(ckvn, wvx_ref[...], _NN)
        kper = _rope(kpe_ref[...], c, sa, sb)
        ones_half = (lax.broadcasted_iota(jnp.int32, (tm, HEAD_PAD), 1) >= V_HEAD_DIM).astype(F32)
        for h in range(MLA_HEADS):
            sl = slice(HEAD_PAD * h, HEAD_PAD * (h + 1))
            kh = knall[:, sl] + kper
            vh = vall[:, sl] + ones_half
            k_ref[:, sl] = kh.astype(BF16)
            kt_ref[sl, :] = kh.T.astype(BF16)
            vx_ref[:, sl] = vh.astype(BF16)
            vxt_ref[sl, :] = vh.T.astype(BF16)

    full = lambda shape: pl.BlockSpec(shape, lambda i: (0, 0))
    tab = pl.BlockSpec((tm, LANES), lambda i: (i, 0))
    row = pl.BlockSpec((tm, hw), lambda i: (i, 0))
    col = pl.BlockSpec((hw, tm), lambda i: (0, i))
    return pl.pallas_call(
        body, name="mla_prep", grid=(SEQ // tm,),
        in_specs=[pl.BlockSpec((tm, CQ_PAD), lambda i: (i, 0)),
                  pl.BlockSpec((tm, LANES), lambda i: (i, CQ_PAD // LANES)),
                  pl.BlockSpec((tm, LANES), lambda i: (i, CQ_PAD // LANES + 1)),
                  full((1, CQ_PAD)), full((1, KV_LORA_RANK)),
                  full((CQ_PAD, hw)), full((KV_LORA_RANK, hw)), full((KV_LORA_RANK, hw)), tab, tab, tab],
        out_specs=[row, row, col, row, col],
        out_shape=[jax.ShapeDtypeStruct((SEQ, hw), BF16), jax.ShapeDtypeStruct((SEQ, hw), BF16),
                   jax.ShapeDtypeStruct((hw, SEQ), BF16), jax.ShapeDtypeStruct((SEQ, hw), BF16),
                   jax.ShapeDtypeStruct((hw, SEQ), BF16)],
        compiler_params=pltpu.CompilerParams(dimension_semantics=("arbitrary",), vmem_limit_bytes=VMEM_MID),
    )(h_c, h_c, h_c, gq, gkv, wq, wkn, wvx, c_t, sa_t, sb_t)


ATT_T = 512
ATT_STRIP = 64


def _attn_fwd(q, kt, vx, own):
    t, rs = ATT_T, ATT_STRIP
    nown = len(own)
    nq = SEQ // t
    nsteps = (MLA_HEADS // 2) * nq

    def body(q_ref, kt_ref, vx_ref, *rest):
        own_refs, (o_ref, l_ref), gat_refs = rest[:nown], rest[nown:nown + 2], rest[nown + 2:2 * nown + 2]
        s_scr, p_scr, m_scr, a_scr, acc_scr, send_sems, recv_sems, local_sems = rest[2 * nown + 2:]
        qi = pl.program_id(1)
        _gather_behind(own_refs, gat_refs, send_sems, recv_sems, local_sems, pl.program_id(0) * nq + qi,
                       nsteps - 2, nsteps - 1)
        lane = lax.broadcasted_iota(jnp.int32, (t, LANES), 1)
        m_scr[...] = jnp.full((2, t, LANES), -1e30, F32)
        acc_scr[...] = jnp.zeros((2, t, LANES), F32)

        def block(j, masked):
            off = pl.multiple_of(j * t, t)
            for a in range(2):
                sl = slice(HEAD_PAD * a, HEAD_PAD * (a + 1))
                s_scr[a] = _dot(q_ref[:, sl], kt_ref[sl, pl.ds(off, t)], _NN)
                for r in range(t // rs):
                    rows = slice(rs * r, rs * (r + 1))
                    s = s_scr[a, rows, :]
                    if masked:
                        rowi = lax.broadcasted_iota(jnp.int32, (rs, t), 0) + rs * r
                        coli = lax.broadcasted_iota(jnp.int32, (rs, t), 1)
                        s = jnp.where(coli <= rowi, s, -1e30)
                    m_old = m_scr[a, rows, :]
                    m_new = jnp.maximum(m_old, jnp.max(s, axis=1, keepdims=True))
                    p_scr[a, rows, :] = jnp.exp(s - m_new[:, :1]).astype(BF16)
                    a_scr[a, rows, :] = jnp.exp(m_old - m_new)
                    m_scr[a, rows, :] = m_new
                acc_scr[a] = acc_scr[a] * a_scr[a] + _dot(p_scr[a], vx_ref[pl.ds(off, t), sl], _NN)

        def step(j, carry):
            block(j, False)
            return carry
        lax.fori_loop(0, qi, step, 0)
        block(qi, True)
        res = []
        for a in range(2):
            acc = acc_scr[a]
            l = acc[:, V_HEAD_DIM:V_HEAD_DIM + 1]
            res.append((acc / l, m_scr[a] + jnp.log(l)))
        o_ref[...] = jnp.where(lane < V_HEAD_DIM, res[0][0], pltpu.roll(res[1][0], V_HEAD_DIM, 1))
        l_ref[...] = jnp.where(lane < V_HEAD_DIM, res[0][1], res[1][1])

    hbm = pl.BlockSpec(memory_space=pl.ANY)
    res = pl.pallas_call(
        body, name="attn_fwd", grid=(MLA_HEADS // 2, nq),
        in_specs=[pl.BlockSpec((t, 2 * HEAD_PAD), lambda p, i: (i, p)),
                  pl.BlockSpec((2 * HEAD_PAD, SEQ), lambda p, i: (p, 0)),
                  pl.BlockSpec((SEQ, 2 * HEAD_PAD), lambda p, i: (0, p))] + [hbm] * nown,
        out_specs=[pl.BlockSpec((t, LANES), lambda p, i: (i, p)),
                   pl.BlockSpec((t, LANES), lambda p, i: (i, p))] + [hbm] * nown,
        out_shape=[jax.ShapeDtypeStruct((SEQ, MLA_WIDTH), F32), jax.ShapeDtypeStruct((SEQ, MLA_WIDTH), F32)]
        + [jax.ShapeDtypeStruct((N_DEV,) + a.shape, a.dtype) for a in own],
        scratch_shapes=[pltpu.VMEM((2, t, t), F32), pltpu.VMEM((2, t, t), BF16), pltpu.VMEM((2, t, LANES), F32),
                        pltpu.VMEM((2, t, LANES), F32), pltpu.VMEM((2, t, LANES), F32)] + _exchange_sems(nown),
        compiler_params=pltpu.CompilerParams(dimension_semantics=("arbitrary", "arbitrary"), vmem_limit_bytes=VMEM_MID),
    )(q, kt, vx, *own)
    return res[0], res[1], res[2:]


def _exchange_parts(parts, lands, send_sems, recv_sems, local_sems):
    x, y, c = _mesh_pos()
    me = 4 * x + 2 * y + c
    peers = [(x, y, 1 - c), (1 - x, y, c), (x, 1 - y, c), (1 - x, 1 - y, c),
             (1 - x, y, 1 - c), (x, 1 - y, 1 - c), (1 - x, 1 - y, 1 - c)]
    remote, local = [], []
    for a, (part, land) in enumerate(zip(parts, lands)):
        for k, peer in enumerate(peers):
            t = 4 * peer[0] + 2 * peer[1] + peer[2]
            remote.append(_remote(part.at[t], land.at[me], send_sems, recv_sems, 7 * a + k, peer))
        local.append(pltpu.make_async_copy(part.at[me], land.at[me], local_sems.at[a]))
    return remote, local


def _exchange_start(first_step, exchange):
    remote, local = exchange

    @pl.when(first_step)
    def _():
        for cp in remote + local:
            cp.start()


def _exchange_finish(last_step, exchange):
    remote, local = exchange

    @pl.when(last_step)
    def _():
        for cp in remote:
            cp.wait_recv()
        for cp in remote:
            cp.wait_send()
        for cp in local:
            cp.wait()


def _exchange_sems(npart):
    return [pltpu.SemaphoreType.DMA((7 * npart,)), pltpu.SemaphoreType.DMA((7 * npart,)),
            pltpu.SemaphoreType.DMA((npart,))]


def _attn_bwd(q, kt, k, vxt, d_o, o, lse, parts):
    t, rs = ATT_T, ATT_STRIP
    nq = SEQ // t
    npart = len(parts)
    nsteps = MLA_HEADS // 2

    def body(q_ref, kt_ref, k_ref, vxt_ref, do_ref, o_ref, l_ref, *rest):
        part_refs, rest = rest[:npart], rest[npart:]
        dq_ref, dk_ref, dv_ref = rest[:3]
        land_refs, rest = rest[3:3 + npart], rest[3 + npart:]
        s_scr, dp_scr, p_scr, ds_scr, st_scr, send_sems, recv_sems, local_sems = rest
        exchange = _exchange_parts(part_refs, land_refs, send_sems, recv_sems, local_sems)
        _exchange_start(pl.program_id(0) == 0, exchange)
        dk_ref[...] = jnp.zeros_like(dk_ref)
        dv_ref[...] = jnp.zeros_like(dv_ref)
        lane = lax.broadcasted_iota(jnp.int32, (t, LANES), 1)

        def qtile(i, carry):
            ioff = pl.multiple_of(i * t, t)
            do_i = do_ref[pl.ds(ioff, t), :]
            o_i = o_ref[pl.ds(ioff, t), :]
            l_i = l_ref[pl.ds(ioff, t), :]
            for a in range(2):
                sl = slice(HEAD_PAD * a, HEAD_PAD * (a + 1))
                sel = (lane < V_HEAD_DIM) if a == 0 else (lane >= V_HEAD_DIM)
                doa = jnp.where(sel, do_i, 0.0)
                oa = o_i
                if a == 1:
                    doa = pltpu.roll(doa, V_HEAD_DIM, 1)
                    oa = pltpu.roll(o_i, V_HEAD_DIM, 1)
                st_scr[0] = jnp.broadcast_to(jnp.sum(doa * oa, axis=1, keepdims=True), (t, LANES))
                st_scr[1] = jnp.broadcast_to(l_i[:, V_HEAD_DIM * a:V_HEAD_DIM * a + 1], (t, LANES))
                doa_bf = doa.astype(BF16)
                qa = q_ref[pl.ds(ioff, t), sl]

                def block(j, masked, dq_acc, sl=sl, qa=qa, doa_bf=doa_bf):
                    joff = pl.multiple_of(j * t, t)
                    s_scr[...] = _dot(qa, kt_ref[sl, pl.ds(joff, t)], _NN)
                    dp_scr[...] = _dot(doa_bf, vxt_ref[sl, pl.ds(joff, t)], _NN)
                    for r in range(t // rs):
                        rows = slice(rs * r, rs * (r + 1))
                        p = jnp.exp(s_scr[rows, :] - st_scr[1, rows, :1])
                        if masked:
                            rowi = lax.broadcasted_iota(jnp.int32, (rs, t), 0) + rs * r
                            coli = lax.broadcasted_iota(jnp.int32, (rs, t), 1)
                            p = jnp.where(coli <= rowi, p, 0.0)
                        p_scr[rows, :] = p.astype(BF16)
                        ds_scr[rows, :] = (p * (dp_scr[rows, :] - st_scr[0, rows, :1])).astype(BF16)
                    dk_ref[pl.ds(joff, t), sl] += _dot(ds_scr[...], qa, _TN)
                    dv_ref[pl.ds(joff, t), sl] += _dot(p_scr[...], doa_bf, _TN)
                    return dq_acc + _dot(ds_scr[...], k_ref[pl.ds(joff, t), sl], _NN)

                dq_acc = lax.fori_loop(0, i, lambda j, acc: block(j, False, acc), jnp.zeros((t, HEAD_PAD), F32))
                dq_ref[pl.ds(ioff, t), sl] = block(i, True, dq_acc)
            return carry

        lax.fori_loop(0, nq, qtile, 0)
        _exchange_finish(pl.program_id(0) == nsteps - 1, exchange)

    hw = MLA_HEADS * HEAD_PAD
    wide = pl.BlockSpec((SEQ, 2 * HEAD_PAD), lambda p: (0, p))
    wide_t = pl.BlockSpec((2 * HEAD_PAD, SEQ), lambda p: (p, 0))
    narrow = pl.BlockSpec((SEQ, LANES), lambda p: (0, p))
    hbm = pl.BlockSpec(memory_space=pl.ANY)
    res = pl.pallas_call(
        body, name="attn_bwd", grid=(nsteps,),
        in_specs=[wide, wide_t, wide, wide_t, narrow, narrow, narrow] + [hbm] * npart,
        out_specs=[wide, wide, wide] + [hbm] * npart,
        out_shape=[jax.ShapeDtypeStruct((SEQ, hw), F32)] * 3 + [jax.ShapeDtypeStruct(p.shape, p.dtype) for p in parts],
        scratch_shapes=[pltpu.VMEM((t, t), F32), pltpu.VMEM((t, t), F32), pltpu.VMEM((t, t), BF16),
                        pltpu.VMEM((t, t), BF16), pltpu.VMEM((2, t, LANES), F32)] + _exchange_sems(npart),
        compiler_params=pltpu.CompilerParams(dimension_semantics=("arbitrary",), vmem_limit_bytes=VMEM_BIG),
    )(q, kt, k, vxt, d_o, o, lse, *parts)
    return res[0], res[1], res[2], res[3:]


def _sgu_math(u, v, zb, lg, lb, ws_ref, bias):
    ug, dug = _gelu_and_grad(u)
    vg, dvg = _gelu_and_grad(v)
    mu = jnp.mean(vg, axis=1, keepdims=True)
    xc = vg - mu
    rstd = lax.rsqrt(jnp.mean(xc * xc, axis=1, keepdims=True) + LN_EPS)
    xh = xc * rstd
    vn_bf = (xh * lg + lb).astype(BF16)
    grp = lax.broadcasted_iota(jnp.int32, (CHUNK, SGU_WIDTH), 1) // SGU_GROUP_DIM
    r_i = lax.broadcasted_iota(jnp.int32, (CHUNK, CHUNK), 0)
    c_i = lax.broadcasted_iota(jnp.int32, (CHUNK, CHUNK), 1)
    tri, tri_t = r_i >= c_i, r_i <= c_i
    mixed = bias
    for g in range(SGU_GROUPS):
        wt = jnp.where(tri, ws_ref[g], 0.0).astype(BF16)
        mixed = mixed + jnp.where(grp == g, _dot(wt, vn_bf, _NN), 0.0)
    sb = _sigmoid(zb)
    return ug, dug, dvg, rstd, xh, vn_bf, grp, tri, tri_t, mixed, sb


def _sgu_fwd(h_b, lg, lb, w_s, bias_full):
    def body(u_ref, v_ref, zb_ref, lg_ref, lb_ref, ws_ref, bias_ref, yb_ref):
        zb = zb_ref[...]
        ug, _, _, _, _, _, _, _, _, mixed, sb = _sgu_math(u_ref[...], v_ref[...], zb, lg_ref[...], lb_ref[...],
                                                       ws_ref, bias_ref[...])
        yb_ref[...] = (ug * mixed) * (zb * sb)

    blk = lambda c: pl.BlockSpec((CHUNK, SGU_WIDTH), lambda i, c=c: (i, c))
    full2 = lambda shape: pl.BlockSpec(shape, lambda i: (0, 0))
    return pl.pallas_call(
        body, name="sgu_fwd", grid=(SEQ // CHUNK,),
        in_specs=[blk(0), blk(1), blk(2), full2((1, SGU_WIDTH)), full2((1, SGU_WIDTH)),
                  pl.BlockSpec((SGU_GROUPS, CHUNK, CHUNK), lambda i: (0, 0, 0)), full2((CHUNK, SGU_WIDTH))],
        out_specs=pl.BlockSpec((CHUNK, SGU_WIDTH), lambda i: (i, 0)),
        out_shape=jax.ShapeDtypeStruct((SEQ, SGU_WIDTH), F32),
        compiler_params=pltpu.CompilerParams(dimension_semantics=("arbitrary",)),
    )(h_b, h_b, h_b, lg, lb, w_s, bias_full)


def _sgu_bwd(h_b, d_yb, lg, lb, w_s, w_st, bias_full, parts):
    nsteps = SEQ // CHUNK
    npart = len(parts)

    def body(u_ref, v_ref, zb_ref, dyb_ref, lg_ref, lb_ref, ws_ref, wst_ref, bias_ref, *rest):
        part_refs, rest = rest[:npart], rest[npart:]
        dhb_ref, dws_ref, dbs_ref, dlg_ref, dlb_ref, dbb_ref = rest[:6]
        land_refs, (dbias_acc, send_sems, recv_sems, local_sems) = rest[6:6 + npart], rest[6 + npart:]
        step = pl.program_id(0)
        exchange = _exchange_parts(part_refs, land_refs, send_sems, recv_sems, local_sems)
        _exchange_start(step == 0, exchange)

        @pl.when(step == 0)
        def _():
            dbb_ref[...] = jnp.zeros_like(dbb_ref)
            dws_ref[...] = jnp.zeros_like(dws_ref)
            dlg_ref[...] = jnp.zeros_like(dlg_ref)
            dlb_ref[...] = jnp.zeros_like(dlb_ref)
            dbias_acc[...] = jnp.zeros_like(dbias_acc)

        zb = zb_ref[...]
        lg = lg_ref[...]
        ug, dug, dvg, rstd, xh, vn_bf, grp, tri, tri_t, mixed, sb = _sgu_math(
            u_ref[...], v_ref[...], zb, lg, lb_ref[...], ws_ref, bias_ref[...])
        dyb = dyb_ref[...]
        dsgu = dyb * (zb * sb)
        dzb = dyb * (ug * mixed) * (sb * (1.0 + zb * (1.0 - sb)))
        du = dsgu * mixed * dug
        dmixed = dsgu * ug
        dbias_acc[...] += dmixed
        dvn = jnp.zeros((CHUNK, SGU_WIDTH), F32)
        for g in range(SGU_GROUPS):
            dm_g = jnp.where(grp == g, dmixed, 0.0).astype(BF16)
            wtt = jnp.where(tri_t, wst_ref[g], 0.0).astype(BF16)
            dvn = dvn + _dot(wtt, dm_g, _NN)
            dws_ref[g] += jnp.where(tri, _dot(dm_g, vn_bf, _NT), 0.0)
        dlg_ref[...] += jnp.sum(dvn * xh, axis=0, keepdims=True)
        dlb_ref[...] += jnp.sum(dvn, axis=0, keepdims=True)
        dxh = dvn * lg
        dvgel = rstd * (dxh - jnp.mean(dxh, axis=1, keepdims=True) - xh * jnp.mean(dxh * xh, axis=1, keepdims=True))
        _store_grad(dhb_ref, dbb_ref, 0, du)
        _store_grad(dhb_ref, dbb_ref, SGU_WIDTH, dvgel * dvg)
        _store_grad(dhb_ref, dbb_ref, 2 * SGU_WIDTH, dzb)

        @pl.when(step == nsteps - 1)
        def _():
            acc = dbias_acc[...]
            lane = lax.broadcasted_iota(jnp.int32, (CHUNK, LANES), 1)
            out = jnp.zeros((CHUNK, LANES), F32)
            for g in range(SGU_GROUPS):
                sg = jnp.sum(jnp.where(grp == g, acc, 0.0), axis=1, keepdims=True)
                out = jnp.where(lane == g, sg, out)
            dbs_ref[...] = out

        _exchange_finish(step == nsteps - 1, exchange)

    blk = lambda c: pl.BlockSpec((CHUNK, SGU_WIDTH), lambda i, c=c: (i, c))
    full2 = lambda shape: pl.BlockSpec(shape, lambda i: (0, 0))
    full3 = pl.BlockSpec((SGU_GROUPS, CHUNK, CHUNK), lambda i: (0, 0, 0))
    hbm = pl.BlockSpec(memory_space=pl.ANY)
    res = pl.pallas_call(
        body, name="sgu_bwd", grid=(nsteps,),
        in_specs=[blk(0), blk(1), blk(2), pl.BlockSpec((CHUNK, SGU_WIDTH), lambda i: (i, 0)),
                  full2((1, SGU_WIDTH)), full2((1, SGU_WIDTH)), full3, full3, full2((CHUNK, SGU_WIDTH))] + [hbm] * npart,
        out_specs=[pl.BlockSpec((CHUNK, SEG_B), lambda i: (i, 0)), full3, full2((CHUNK, LANES)),
                   full2((1, SGU_WIDTH)), full2((1, SGU_WIDTH)), full2((1, SEG_B))] + [hbm] * npart,
        out_shape=[jax.ShapeDtypeStruct((SEQ, SEG_B), BF16),
                   jax.ShapeDtypeStruct((SGU_GROUPS, CHUNK, CHUNK), F32),
                   jax.ShapeDtypeStruct((CHUNK, LANES), F32),
                   jax.ShapeDtypeStruct((1, SGU_WIDTH), F32), jax.ShapeDtypeStruct((1, SGU_WIDTH), F32),
                   jax.ShapeDtypeStruct((1, SEG_B), F32)] + [jax.ShapeDtypeStruct(p.shape, p.dtype) for p in parts],
        scratch_shapes=[pltpu.VMEM((CHUNK, SGU_WIDTH), F32)] + _exchange_sems(npart),
        compiler_params=pltpu.CompilerParams(dimension_semantics=("arbitrary",)),
    )(h_b, h_b, h_b, d_yb, lg, lb, w_s, w_st, bias_full, *parts)
    return res[:6], res[6:]


def _merge(x, o, h_a, y_b, target, w_oa, w_ob, w_out, ln_g, ln_b):
    tm = 256
    nsteps = SEQ // tm

    def body(x_ref, o_ref, ga_ref, gb_ref, za_ref, yb_ref, tgt_ref, woa_ref, wob_ref, wout_ref, lng_ref, lnb_ref,
             loss_ref, dxr_ref, dha_ref, do_ref, dyb_ref, poa_ref, pob_ref, pout_ref, dlng_ref, dlnb_ref, dba_ref,
             dwoa_ref, dwob_ref, dwout_ref):
        step = pl.program_id(0)

        @pl.when(step == 0)
        def _():
            for r in (loss_ref, dwoa_ref, dwob_ref, dwout_ref, dlng_ref, dlnb_ref, dba_ref):
                r[...] = jnp.zeros_like(r)

        o = o_ref[...]
        za = za_ref[...]
        sa = _sigmoid(za)
        ya_bf = (o * (za * sa)).astype(BF16)
        yb_bf = yb_ref[...].astype(BF16)
        woa, wob, wout = woa_ref[...], wob_ref[...], wout_ref[...]
        pa = _dot(ya_bf, woa, _NN)
        pb = _dot(yb_bf, wob, _NN)
        sga = _sigmoid(ga_ref[...])
        sgb = _sigmoid(gb_ref[...])
        merged_bf = (sga * pa + sgb * pb).astype(BF16)
        r = DN_ALPHA * x_ref[...] + _dot(merged_bf, wout, _NN)
        mu = jnp.mean(r, axis=1, keepdims=True)
        rc = r - mu
        rstd = lax.rsqrt(jnp.mean(rc * rc, axis=1, keepdims=True) + LN_EPS)
        xh = rc * rstd
        lng = lng_ref[...]
        y = xh * lng + lnb_ref[...]
        e = y - tgt_ref[...]
        loss_ref[...] += 0.5 * jnp.sum(jnp.sum(e * e, axis=1, keepdims=True) * (1.0 / D_MODEL), axis=0, keepdims=True)

        dy = e * (1.0 / D_MODEL)
        dlng_ref[...] += jnp.sum(dy * xh, axis=0, keepdims=True)
        dlnb_ref[...] += jnp.sum(dy, axis=0, keepdims=True)
        dxh = dy * lng
        dr = rstd * (dxh - jnp.mean(dxh, axis=1, keepdims=True) - xh * jnp.mean(dxh * xh, axis=1, keepdims=True))
        dxr_ref[...] = DN_ALPHA * dr
        dr_bf = dr.astype(BF16)
        dwout_ref[...] += _dot(merged_bf, dr_bf, _TN)
        dmerged = _dot(dr_bf, wout, _NT)
        dpa_bf = (dmerged * sga).astype(BF16)
        dpb_bf = (dmerged * sgb).astype(BF16)
        _store_grad(dha_ref, dba_ref, 0, dmerged * pa * (sga * (1.0 - sga)))
        _store_grad(dha_ref, dba_ref, D_MODEL, dmerged * pb * (sgb * (1.0 - sgb)))
        dwoa_ref[...] += _dot(ya_bf, dpa_bf, _TN)
        dwob_ref[...] += _dot(yb_bf, dpb_bf, _TN)
        dya = _dot(dpa_bf, woa, _NT)
        dyb_ref[...] = _dot(dpb_bf, wob, _NT)
        do_ref[...] = dya * (za * sa)
        _store_grad(dha_ref, dba_ref, 2 * D_MODEL, dya * o * (sa * (1.0 + za * (1.0 - sa))))

        @pl.when(step == nsteps - 1)
        def _():
            cols = D_MODEL // N_DEV
            for j in range(N_DEV):
                poa_ref[j] = dwoa_ref[:, cols * j:cols * (j + 1)].astype(BF16)
                pob_ref[j] = dwob_ref[:, cols * j:cols * (j + 1)].astype(BF16)
                pout_ref[j] = dwout_ref[cols * j:cols * (j + 1), :].astype(BF16)

    row = lambda w, c=0: pl.BlockSpec((tm, w), lambda i, c=c: (i, c))
    full = lambda shape: pl.BlockSpec(shape, lambda i: (0, 0))
    full3 = lambda shape: pl.BlockSpec(shape, lambda i: (0, 0, 0))
    return pl.pallas_call(
        body, name="merge", grid=(nsteps,),
        in_specs=[row(D_MODEL), row(MLA_WIDTH), row(D_MODEL, 0), row(D_MODEL, 1), row(MLA_WIDTH, 4), row(SGU_WIDTH),
                  row(D_MODEL), full((MLA_WIDTH, D_MODEL)), full((SGU_WIDTH, D_MODEL)), full((D_MODEL, D_MODEL)),
                  full((1, D_MODEL)), full((1, D_MODEL))],
        out_specs=[full((1, LANES)), row(D_MODEL), row(SEG_A), row(MLA_WIDTH), row(SGU_WIDTH),
                   full3((N_DEV, MLA_WIDTH, D_MODEL // N_DEV)), full3((N_DEV, SGU_WIDTH, D_MODEL // N_DEV)),
                   full3((N_DEV, D_MODEL // N_DEV, D_MODEL)), full((1, D_MODEL)), full((1, D_MODEL)), full((1, SEG_A))],
        out_shape=[jax.ShapeDtypeStruct((1, LANES), F32),
                   jax.ShapeDtypeStruct((SEQ, D_MODEL), F32), jax.ShapeDtypeStruct((SEQ, SEG_A), BF16),
                   jax.ShapeDtypeStruct((SEQ, MLA_WIDTH), F32), jax.ShapeDtypeStruct((SEQ, SGU_WIDTH), F32),
                   jax.ShapeDtypeStruct((N_DEV, MLA_WIDTH, D_MODEL // N_DEV), BF16),
                   jax.ShapeDtypeStruct((N_DEV, SGU_WIDTH, D_MODEL // N_DEV), BF16),
                   jax.ShapeDtypeStruct((N_DEV, D_MODEL // N_DEV, D_MODEL), BF16),
                   jax.ShapeDtypeStruct((1, D_MODEL), F32), jax.ShapeDtypeStruct((1, D_MODEL), F32),
                   jax.ShapeDtypeStruct((1, SEG_A), F32)],
        scratch_shapes=[pltpu.VMEM((MLA_WIDTH, D_MODEL), F32), pltpu.VMEM((SGU_WIDTH, D_MODEL), F32),
                        pltpu.VMEM((D_MODEL, D_MODEL), F32)],
        compiler_params=pltpu.CompilerParams(dimension_semantics=("arbitrary",), vmem_limit_bytes=VMEM_BIG),
    )(x, o, h_a, h_a, h_a, y_b, target, w_oa, w_ob, w_out, ln_g, ln_b)


def _mla_bwd(dq, dk, dv, h_c, xt_bf, gq, gkv, wq, wkn, wv, c_t, sa_t, sb_t, parts):
    tm = 256
    hw = MLA_HEADS * HEAD_PAD
    npart = len(parts)
    nsteps = SEQ // tm

    def body(dq_ref, dk_ref, dv_ref, cq_ref, ckv_ref, xt_ref, gq_ref, gkv_ref, wq_ref, wkn_ref, wv_ref, c_ref, sa_ref,
             sb_ref, *rest):
        part_refs, rest = rest[:npart], rest[npart:]
        dhc_ref, puq_ref, dwkn_ref, dwv_ref, dgq_ref, dgkv_ref, dbc_ref, dwc_ref = rest[:8]
        land_refs, (pre_ref, dwq_ref, dwc_acc, send_sems, recv_sems, local_sems) = rest[8:8 + npart], rest[8 + npart:]
        exchange = _exchange_parts(part_refs, land_refs, send_sems, recv_sems, local_sems)
        _exchange_start(pl.program_id(0) == 0, exchange)
        _exchange_finish(pl.program_id(0) == nsteps - 1, exchange)

        @pl.when(pl.program_id(0) == 0)
        def _():
            for r in (dwq_ref, dwc_acc, dwkn_ref, dwv_ref, dgq_ref, dgkv_ref, dbc_ref):
                r[...] = jnp.zeros_like(r)

        c, sa, sb = c_ref[...], sa_ref[...], sb_ref[...]
        lane = lax.broadcasted_iota(jnp.int32, (tm, LANES), 1)
        rope_lanes = jnp.logical_and(lane >= ROPE_LO, lane < ROPE_HI)

        cq = cq_ref[...]
        gq = gq_ref[...]
        rq = lax.rsqrt(jnp.sum(cq * cq, axis=1, keepdims=True) * (1.0 / Q_LORA_RANK) + RMS_EPS)
        nq = cq * rq
        cqn_bf = (nq * gq).astype(BF16)
        for h in range(MLA_HEADS):
            sl = slice(HEAD_PAD * h, HEAD_PAD * (h + 1))
            pre_ref[:, sl] = _rope_t(dq_ref[:, sl] * ATTN_SCALE, c, sa, sb).astype(BF16)
        dqpre_bf = pre_ref[...]
        dcqn = _dot(dqpre_bf, wq_ref[...], _NT)
        dwq_ref[...] += _dot(cqn_bf, dqpre_bf, _TN)
        dgq_ref[...] += jnp.sum(dcqn * nq, axis=0, keepdims=True)
        dnq = dcqn * gq
        _store_grad(dhc_ref, dbc_ref, 0,
                    rq * (dnq - nq * (jnp.sum(dnq * nq, axis=1, keepdims=True) * (1.0 / Q_LORA_RANK))))

        ckv = ckv_ref[...]
        gkv = gkv_ref[...]
        rkv = lax.rsqrt(jnp.sum(ckv * ckv, axis=1, keepdims=True) * (1.0 / KV_LORA_RANK) + RMS_EPS)
        nkv = ckv * rkv
        ckvn_bf = (nkv * gkv).astype(BF16)
        dk = dk_ref[...]
        dk_bf = dk.astype(BF16)
        dv_bf = dv_ref[...].astype(BF16)
        dckvn = _dot(dk_bf, wkn_ref[...], _NT) + _dot(dv_bf, wv_ref[...], _NT)
        dwkn_ref[...] += _dot(ckvn_bf, dk_bf, _TN)
        dwv_ref[...] += _dot(ckvn_bf, dv_bf, _TN)
        dgkv_ref[...] += jnp.sum(dckvn * nkv, axis=0, keepdims=True)
        dnkv = dckvn * gkv
        _store_grad(dhc_ref, dbc_ref, CQ_PAD, rkv * (
            dnkv - nkv * (jnp.sum(dnkv * nkv, axis=1, keepdims=True) * (1.0 / KV_LORA_RANK))))
        dkpe = jnp.zeros((tm, LANES), F32)
        for h in range(MLA_HEADS):
            dkpe = dkpe + dk[:, HEAD_PAD * h:HEAD_PAD * (h + 1)]
        _store_grad(dhc_ref, dbc_ref, CQ_PAD + LANES, _rope_t(jnp.where(rope_lanes, dkpe, 0.0), c, sa, sb))
        dwc_acc[...] += _dot(xt_ref[...], dhc_ref[...], _NN)

        @pl.when(pl.program_id(0) == SEQ // tm - 1)
        def _():
            dwc_ref[...] = dwc_acc[...].astype(BF16)
            rows = Q_LORA_RANK // N_DEV
            for j in range(N_DEV):
                for h in range(MLA_HEADS):
                    puq_ref[j, :, QK_HEAD_DIM * h:QK_HEAD_DIM * (h + 1)] = dwq_ref[
                        rows * j:rows * (j + 1), HEAD_PAD * h:HEAD_PAD * h + QK_HEAD_DIM].astype(BF16)

    full = lambda shape: pl.BlockSpec(shape, lambda i: (0, 0))
    row = lambda w, c=0: pl.BlockSpec((tm, w), lambda i, c=c: (i, c))
    hbm = pl.BlockSpec(memory_space=pl.ANY)
    res = pl.pallas_call(
        body, name="mla_bwd", grid=(nsteps,),
        in_specs=[row(hw), row(hw), row(hw), row(CQ_PAD, 0), row(LANES, CQ_PAD // LANES),
                  pl.BlockSpec((D_MODEL, tm), lambda i: (0, i)),
                  full((1, CQ_PAD)), full((1, KV_LORA_RANK)), full((CQ_PAD, hw)), full((KV_LORA_RANK, hw)),
                  full((KV_LORA_RANK, hw)), row(LANES), row(LANES), row(LANES)] + [hbm] * npart,
        out_specs=[row(SEG_C), pl.BlockSpec((N_DEV, Q_LORA_RANK // N_DEV, MLA_HEADS * QK_HEAD_DIM), lambda i: (0, 0, 0)),
                   full((KV_LORA_RANK, hw)), full((KV_LORA_RANK, hw)),
                   full((1, CQ_PAD)), full((1, KV_LORA_RANK)), full((1, SEG_C)), full((D_MODEL, SEG_C))] + [hbm] * npart,
        out_shape=[jax.ShapeDtypeStruct((SEQ, SEG_C), BF16),
                   jax.ShapeDtypeStruct((N_DEV, Q_LORA_RANK // N_DEV, MLA_HEADS * QK_HEAD_DIM), BF16),
                   jax.ShapeDtypeStruct((KV_LORA_RANK, hw), F32), jax.ShapeDtypeStruct((KV_LORA_RANK, hw), F32),
                   jax.ShapeDtypeStruct((1, CQ_PAD), F32), jax.ShapeDtypeStruct((1, KV_LORA_RANK), F32),
                   jax.ShapeDtypeStruct((1, SEG_C), F32), jax.ShapeDtypeStruct((D_MODEL, SEG_C), BF16)]
        + [jax.ShapeDtypeStruct(p.shape, p.dtype) for p in parts],
        scratch_shapes=[pltpu.VMEM((tm, hw), BF16), pltpu.VMEM((CQ_PAD, hw), F32), pltpu.VMEM((D_MODEL, SEG_C), F32)]
        + _exchange_sems(npart),
        compiler_params=pltpu.CompilerParams(dimension_semantics=("arbitrary",), vmem_limit_bytes=VMEM_MID),
    )(dq, dk, dv, h_c, h_c, xt_bf, gq, gkv, wq, wkn, wv, c_t, sa_t, sb_t, *parts)
    return res[:8], res[8:]


def _adamw_all(ws, gs, ms, vs):
    n = len(ws)
    c1 = 1.0 / (1.0 - ADAM_B1 ** ADAM_STEP)
    c2 = 1.0 / (1.0 - ADAM_B2 ** ADAM_STEP)

    def body(*refs):
        for idx in range(n):
            w, g, m, v = (refs[idx][...], refs[n + idx][...], refs[2 * n + idx][...], refs[3 * n + idx][...])
            m_new = ADAM_B1 * m + (1.0 - ADAM_B1) * g
            v_new = ADAM_B2 * v + (1.0 - ADAM_B2) * (g * g)
            delta = -ADAM_LR * ((m_new * c1) / (jnp.sqrt(v_new * c2) + ADAM_EPS) + ADAM_WD * w)
            refs[4 * n + idx][...] = delta
            refs[5 * n + idx][...] = m_new
            refs[6 * n + idx][...] = v_new

    shapes = [jax.ShapeDtypeStruct(w.shape, F32) for w in ws]
    outs = pl.pallas_call(
        body, name="adamw", out_shape=shapes * 3,
        compiler_params=pltpu.CompilerParams(vmem_limit_bytes=VMEM_BIG),
    )(*ws, *gs, *ms, *vs)
    return outs[:n], outs[n:2 * n], outs[2 * n:]


SHARD_W = IN_WIDTH // N_DEV

_PIECES = [(0, 384, 2, 0), (384, 512, 2, CQ_PAD), (512, 544, 2, CQ_PAD + LANES + ROPE_LO),
           (544, 1056, 0, 2 * D_MODEL), (1056, 1568, 1, 0), (1568, 2080, 1, SGU_WIDTH),
           (2080, 2592, 1, 2 * SGU_WIDTH), (2592, 3616, 0, 0), (3616, 4640, 0, D_MODEL)]


def _column_runs():
    runs = []
    for n0, n1, seg, d0 in _PIECES:
        for j in range(N_DEV):
            lo, hi = max(n0, j * SHARD_W), min(n1, (j + 1) * SHARD_W)
            if lo < hi:
                runs.append((j, lo - j * SHARD_W, hi - j * SHARD_W, seg, d0 + lo - n0))
    return runs


def _mesh_pos():
    return lax.axis_index("x"), lax.axis_index("y"), lax.axis_index("c")


def _remote(src, dst, send_sems, recv_sems, k, to):
    return pltpu.make_async_remote_copy(src_ref=src, dst_ref=dst, send_sem=send_sems.at[k], recv_sem=recv_sems.at[k],
                                        device_id=to, device_id_type=pl.DeviceIdType.MESH)


def _gather_exchange(gats, send_sems, recv_sems, meanwhile=None):
    x, y, c = _mesh_pos()
    me, sibling = (x, y, c), (x, y, 1 - c)
    chips = [(1 - x, y), (x, 1 - y), (1 - x, 1 - y)]

    def copy(a, k, blk, to):
        slab = gats[a].at[4 * blk[0] + 2 * blk[1] + blk[2]]
        return _remote(slab, slab, send_sems, recv_sems, 7 * a + k, to)

    arrays = range(len(gats))
    first = [copy(a, 1 + j, me, (*chip, c)) for j, chip in enumerate(chips) for a in arrays]
    first += [copy(a, 0, me, sibling) for a in arrays]
    for cp in first:
        cp.start()
    if meanwhile is not None:
        meanwhile()
    passed = []
    for j, chip in enumerate(chips):
        for a in arrays:
            copy(a, 1 + j, (*chip, c), me).wait_recv()
            fwd = copy(a, 4 + j, (*chip, c), sibling)
            fwd.start()
            passed.append(fwd)
    for a in arrays:
        copy(a, 0, sibling, me).wait_recv()
    for j, chip in enumerate(chips):
        for a in arrays:
            copy(a, 4 + j, (*chip, 1 - c), me).wait_recv()
    for cp in first + passed:
        cp.wait_send()


def _gather_behind(own, gats, send_sems, recv_sems, local_sems, step, mid, last):
    x, y, c = _mesh_pos()
    me, sibling = (x, y, c), (x, y, 1 - c)
    chips = [(1 - x, y), (x, 1 - y), (1 - x, 1 - y)]
    arrays = range(len(gats))

    def copy(a, k, blk, to, src=None):
        slab = gats[a].at[4 * blk[0] + 2 * blk[1] + blk[2]]
        return _remote(slab if src is None else src, slab, send_sems, recv_sems, 7 * a + k, to)

    first = [copy(a, 1 + j, me, (*chip, c), src=own[a]) for j, chip in enumerate(chips) for a in arrays]
    first += [copy(a, 0, me, sibling, src=own[a]) for a in arrays]
    local = [pltpu.make_async_copy(own[a], gats[a].at[4 * x + 2 * y + c], local_sems.at[a]) for a in arrays]
    passed = [copy(a, 4 + j, (*chip, c), sibling) for j, chip in enumerate(chips) for a in arrays]

    @pl.when(step == 0)
    def _():
        for cp in first + local:
            cp.start()

    @pl.when(step == mid)
    def _():
        for j, chip in enumerate(chips):
            for a in arrays:
                copy(a, 1 + j, (*chip, c), me).wait_recv()
        for cp in passed:
            cp.start()

    @pl.when(step == last)
    def _():
        for a in arrays:
            copy(a, 0, sibling, me).wait_recv()
        for j, chip in enumerate(chips):
            for a in arrays:
                copy(a, 4 + j, (*chip, 1 - c), me).wait_recv()
        for cp in first + passed:
            cp.wait_send()
        for cp in local:
            cp.wait()


def _gather_first(w_in, w_uq2, w_oa, w_ob, w_out, x2, pos_col, invf_lane):
    hw = MLA_HEADS * HEAD_PAD
    uq_rows = Q_LORA_RANK // N_DEV
    rows = 256

    def body(win_ref, wuq_ref, woa_ref, wob_ref, wout_ref, x_ref, pos_ref, invf_ref,
             wc_ref, wq_ref, winb_ref, oab_ref, obb_ref, outb_ref, xb_ref, xt_ref, c_ref, sa_ref, sb_ref,
             g_uq, blk0, send_sems, recv_sems):
        def local_work():
            for i in range(SEQ // rows):
                xi = x_ref[rows * i:rows * (i + 1), :]
                xb_ref[rows * i:rows * (i + 1), :] = xi.astype(BF16)
                xt_ref[:, rows * i:rows * (i + 1)] = xi.T.astype(BF16)
            ang = pos_ref[...].astype(F32) * invf_ref[...]
            cs, sn = jnp.cos(ang), jnp.sin(ang)
            lane = lax.broadcasted_iota(jnp.int32, ang.shape, 1)
            c_ref[...] = jnp.where(lane < ROPE_LO, 1.0, jnp.where(lane < ROPE_HI, cs, 0.0))
            sa_ref[...] = jnp.where(jnp.logical_and(lane >= ROPE_LO, lane < ROPE_MID), -sn, 0.0)
            sb_ref[...] = jnp.where(jnp.logical_and(lane >= ROPE_MID, lane < ROPE_HI), sn, 0.0)

        x, y, c = _mesh_pos()
        me = (x, y, c)
        winb_ref[...] = win_ref[0].astype(BF16)
        oab_ref[...] = woa_ref[0].astype(BF16)
        obb_ref[...] = wob_ref[0].astype(BF16)
        outb_ref[...] = wout_ref[0].astype(BF16)
        g_uq[4 * x + 2 * y + c] = wuq_ref[...].astype(BF16)

        chip0 = jnp.logical_and(x == 0, y == 0)
        south = c == 0
        half = D_MODEL // 2
        halves = [blk0.at[pl.ds(0, half)], blk0.at[pl.ds(half, half)]]

        def bcopy(k, to, part=None):
            ref = blk0 if part is None else halves[part]
            return _remote(ref, ref, send_sems, recv_sems, 7 + k, to)

        sends0 = [(0, (0, 0, 1), None), (1, (1, 0, 0), 0), (2, (0, 1, 0), 1), (3, (1, 0, 0), 1), (4, (0, 1, 0), 0)]

        @pl.when(jnp.logical_and(chip0, south))
        def _():
            blk0[...] = winb_ref[...]
            for k, to, part in sends0:
                bcopy(k, to, part).start()

        _gather_exchange([g_uq], send_sems, recv_sems, meanwhile=local_work)

        for (cx, cy), first_k, first_half, second_k in (((1, 0), 1, 0, 3), ((0, 1), 2, 1, 4)):
            @pl.when(jnp.logical_and(jnp.logical_and(x == cx, y == cy), south))
            def _(cx=cx, cy=cy, first_k=first_k, first_half=first_half, second_k=second_k):
                bcopy(first_k, me, first_half).wait_recv()
                onward = bcopy(5 + first_half, (1, 1, 0), first_half)
                onward.start()
                bcopy(second_k, me, 1 - first_half).wait_recv()
                north = bcopy(7, (cx, cy, 1))
                north.start()
                onward.wait_send()
                north.wait_send()

        @pl.when(jnp.logical_and(jnp.logical_and(x == 1, y == 1), south))
        def _():
            bcopy(5, me, 0).wait_recv()
            bcopy(6, me, 1).wait_recv()
            north = bcopy(7, (1, 1, 1))
            north.start()
            north.wait_send()

        @pl.when(jnp.logical_and(chip0, c == 1))
        def _():
            bcopy(0, me).wait_recv()

        @pl.when(jnp.logical_and(jnp.logical_not(chip0), c == 1))
        def _():
            bcopy(7, me).wait_recv()

        @pl.when(jnp.logical_and(chip0, south))
        def _():
            for k, to, part in sends0:
                bcopy(k, to, part).wait_send()

        for j, s0, s1, seg, d0 in _column_runs():
            if seg == 2:
                wc_ref[:, d0:d0 + (s1 - s0)] = blk0[:, s0:s1]
        zeros = lambda r, w: jnp.zeros((r, w), BF16)
        wc_ref[:, Q_LORA_RANK:CQ_PAD] = zeros(D_MODEL, CQ_PAD - Q_LORA_RANK)
        wc_ref[:, CQ_PAD + LANES:CQ_PAD + LANES + ROPE_LO] = zeros(D_MODEL, ROPE_LO)
        wc_ref[:, CQ_PAD + LANES + ROPE_HI:SEG_C] = zeros(D_MODEL, LANES - ROPE_HI)
        wq_ref[Q_LORA_RANK:CQ_PAD, :] = zeros(CQ_PAD - Q_LORA_RANK, hw)
        for h in range(MLA_HEADS):
            wq_ref[0:Q_LORA_RANK, HEAD_PAD * h + QK_HEAD_DIM:HEAD_PAD * (h + 1)] = zeros(Q_LORA_RANK, HEAD_PAD - QK_HEAD_DIM)
        for j in range(N_DEV):
            for h in range(MLA_HEADS):
                wq_ref[uq_rows * j:uq_rows * (j + 1), HEAD_PAD * h:HEAD_PAD * h + QK_HEAD_DIM] = g_uq[
                    j, :, QK_HEAD_DIM * h:QK_HEAD_DIM * (h + 1)]

    vmem = pl.BlockSpec(memory_space=pltpu.VMEM)
    return pl.pallas_call(
        body, name="gather_first",
        out_shape=[jax.ShapeDtypeStruct((D_MODEL, SEG_C), BF16), jax.ShapeDtypeStruct((CQ_PAD, hw), BF16),
                   jax.ShapeDtypeStruct(w_in.shape[1:], BF16), jax.ShapeDtypeStruct(w_oa.shape[1:], BF16),
                   jax.ShapeDtypeStruct(w_ob.shape[1:], BF16), jax.ShapeDtypeStruct(w_out.shape[1:], BF16),
                   jax.ShapeDtypeStruct((SEQ, D_MODEL), BF16), jax.ShapeDtypeStruct((D_MODEL, SEQ), BF16)]
        + [jax.ShapeDtypeStruct((SEQ, LANES), F32)] * 3,
        in_specs=[vmem] * 8, out_specs=[vmem] * 11,
        scratch_shapes=[pltpu.VMEM((N_DEV, uq_rows, MLA_HEADS * QK_HEAD_DIM), BF16), pltpu.VMEM((D_MODEL, SHARD_W), BF16),
                        pltpu.SemaphoreType.DMA((15,)), pltpu.SemaphoreType.DMA((15,))],
        compiler_params=pltpu.CompilerParams(vmem_limit_bytes=VMEM_BIG),
    )(w_in, w_uq2, w_oa, w_ob, w_out, x2, pos_col, invf_lane)


def _assemble_in(g_in):
    def body(g_ref, wa_ref, wb_ref):
        segs = [wa_ref, wb_ref]
        for j, s0, s1, seg, d0 in _column_runs():
            if seg < 2:
                segs[seg][:, d0:d0 + (s1 - s0)] = g_ref[j, :, s0:s1]

    return pl.pallas_call(
        body, name="assemble_in",
        out_shape=[jax.ShapeDtypeStruct((D_MODEL, SEG_A), BF16), jax.ShapeDtypeStruct((D_MODEL, SEG_B), BF16)],
        compiler_params=pltpu.CompilerParams(vmem_limit_bytes=VMEM_MID),
    )(g_in)


def _assemble_out(g_oa, g_ob, g_out):
    cols = D_MODEL // N_DEV

    def body(goa_ref, gob_ref, gout_ref, oa_ref, ob_ref, out_ref):
        for j in range(N_DEV):
            oa_ref[:, cols * j:cols * (j + 1)] = goa_ref[j]
            ob_ref[:, cols * j:cols * (j + 1)] = gob_ref[j]
            out_ref[cols * j:cols * (j + 1), :] = gout_ref[j]

    return pl.pallas_call(
        body, name="assemble_out",
        out_shape=[jax.ShapeDtypeStruct((MLA_WIDTH, D_MODEL), BF16), jax.ShapeDtypeStruct((SGU_WIDTH, D_MODEL), BF16),
                   jax.ShapeDtypeStruct((D_MODEL, D_MODEL), BF16)],
    )(g_oa, g_ob, g_out)


C_NAT = 544


P_IN_SPLIT = 896


def _to_parts(dwa, dwb):
    def body(dwa_ref, dwb_ref, phi_ref, plo_ref):
        phi_ref[0, :, 0:C_NAT] = jnp.zeros((P_IN_SPLIT, C_NAT), BF16)
        plo_ref[0, :, 0:C_NAT] = jnp.zeros((D_MODEL - P_IN_SPLIT, C_NAT), BF16)
        segs = [dwa_ref, dwb_ref]
        for j, s0, s1, seg, d0 in _column_runs():
            if seg < 2:
                phi_ref[j, :, s0:s1] = segs[seg][0:P_IN_SPLIT, d0:d0 + (s1 - s0)]
                plo_ref[j, :, s0:s1] = segs[seg][P_IN_SPLIT:D_MODEL, d0:d0 + (s1 - s0)]

    return pl.pallas_call(
        body, name="to_parts",
        out_shape=[jax.ShapeDtypeStruct((N_DEV, P_IN_SPLIT, SHARD_W), BF16),
                   jax.ShapeDtypeStruct((N_DEV, D_MODEL - P_IN_SPLIT, SHARD_W), BF16)],
        compiler_params=pltpu.CompilerParams(vmem_limit_bytes=VMEM_MID))(dwa, dwb)


def _dx_tail(dhs, ws, dx_res, dwc, p_uq, p_rep):
    tm = SEQ // 4
    rep_rows = p_rep.shape[1]
    c_rows = D_MODEL // N_DEV
    spec = [((c_rows, C_NAT), BF16), (p_uq.shape[1:], BF16), ((rep_rows, LANES), F32)]
    n = len(spec)

    nseg = len(dhs)

    def body(*refs):
        dh_refs, w_refs = refs[:nseg], refs[nseg:2 * nseg]
        dxr_ref, dwc_ref, puq_ref, prep_ref, dx_ref, call_ref, guq_ref, repall_ref, pc_ref, c_all, rep_all = refs[
            2 * nseg:2 * nseg + 11]
        rest = refs[2 * nseg + 11:]
        ras, tbs, rbs = rest[0:n], rest[n:2 * n], rest[2 * n:3 * n]
        send_sems, recv_sems, gsend, grecv = rest[3 * n:]
        step = pl.program_id(0)
        x, y, c = _mesh_pos()
        me_idx = 4 * x + 2 * y + c
        me, sibling = (x, y, c), (x, y, 1 - c)
        others = [(1 - x, y), (x, 1 - y), (1 - x, 1 - y)]
        parts = [pc_ref, puq_ref, prep_ref]
        gats = [rep_all, c_all]

        def stage1(chip, a):
            return _remote(parts[a].at[2 * chip + (1 - c)], ras[a].at[chip], send_sems, recv_sems, 7 * a + chip, sibling)

        def stage2(k, a):
            cx, cy = others[k]
            return _remote(tbs[a].at[k], rbs[a].at[k], send_sems, recv_sems, 7 * a + 4 + k, (cx, cy, c))

        def gcopy(a, k, blk, to):
            slab = gats[a].at[4 * blk[0] + 2 * blk[1] + blk[2]]
            return _remote(slab, slab, gsend, grecv, 7 * a + k, to)

        def chip_sum(a, chip):
            return parts[a][2 * chip + c].astype(F32) + ras[a][chip].astype(F32)

        @pl.when(step == 0)
        def _():
            for j, s0, s1, seg, d0 in _column_runs():
                if seg == 2:
                    for r in range(N_DEV):
                        pc_ref[r, :, s0:s1] = dwc_ref[c_rows * r:c_rows * (r + 1), d0:d0 + (s1 - s0)]
            for chip in range(4):
                for a in range(n):
                    stage1(chip, a).start()

        @pl.when(step == 1)
        def _():
            for chip in range(4):
                for a in range(n):
                    stage1(chip, a).wait_recv()
            for k, (cx, cy) in enumerate(others):
                for a in range(n):
                    tbs[a][k] = chip_sum(a, 2 * cx + cy).astype(spec[a][1])
                    stage2(k, a).start()

        @pl.when(step == 2)
        def _():
            for k in range(3):
                for a in range(n):
                    stage2(k, a).wait_recv()
            sums = []
            for a in range(n):
                acc = chip_sum(a, 2 * x + y)
                for k in range(3):
                    acc = acc + rbs[a][k].astype(F32)
                sums.append(acc)
            c_all[me_idx] = sums[0].astype(BF16)
            guq_ref[...] = sums[1]
            rep_all[me_idx] = sums[2]
            for a in range(2):
                for j, chip in enumerate(others):
                    gcopy(a, 1 + j, me, (*chip, c)).start()
                gcopy(a, 0, me, sibling).start()

        @pl.when(step == 3)
        def _():
            for j, chip in enumerate(others):
                for a in range(2):
                    gcopy(a, 1 + j, (*chip, c), me).wait_recv()
                    gcopy(a, 4 + j, (*chip, c), sibling).start()
            for a in range(2):
                gcopy(a, 0, sibling, me).wait_recv()
                for j, chip in enumerate(others):
                    gcopy(a, 4 + j, (*chip, 1 - c), me).wait_recv()
            for a in range(2):
                gcopy(a, 0, me, sibling).wait_send()
                for j, chip in enumerate(others):
                    gcopy(a, 1 + j, me, (*chip, c)).wait_send()
                    gcopy(a, 4 + j, (*chip, c), sibling).wait_send()
            for a in range(n):
                for chip in range(4):
                    stage1(chip, a).wait_send()
                for k in range(3):
                    stage2(k, a).wait_send()
            call_ref[...] = c_all[...]
            repall_ref[...] = rep_all[...]

        acc = dxr_ref[...]
        for dh_ref, w_ref in zip(dh_refs, w_refs):
            acc = acc + _dot(dh_ref[...], w_ref[...], _NT)
        dx_ref[...] = acc

    row = lambda w: pl.BlockSpec((tm, w), lambda i: (i, 0))
    full = lambda shape: pl.BlockSpec(shape, lambda i: (0,) * len(shape))
    scratch = [pltpu.VMEM((N_DEV, c_rows, C_NAT), BF16), pltpu.VMEM((N_DEV, c_rows, C_NAT), BF16),
               pltpu.VMEM((N_DEV, rep_rows, LANES), F32)]
    for lead in (4, 3, 3):
        scratch += [pltpu.VMEM((lead,) + tuple(shape), dt) for shape, dt in spec]
    scratch += [pltpu.SemaphoreType.DMA((7 * n,)), pltpu.SemaphoreType.DMA((7 * n,)),
                pltpu.SemaphoreType.DMA((14,)), pltpu.SemaphoreType.DMA((14,))]
    return pl.pallas_call(
        body, name="dx_tail", grid=(SEQ // tm,),
        in_specs=[row(dh.shape[1]) for dh in dhs] + [full(w.shape) for w in ws]
        + [row(D_MODEL), full(dwc.shape), full(p_uq.shape), full(p_rep.shape)],
        out_specs=[row(D_MODEL), full((N_DEV, c_rows, C_NAT)), full(p_uq.shape[1:]), full((N_DEV, rep_rows, LANES))],
        out_shape=[jax.ShapeDtypeStruct((SEQ, D_MODEL), F32), jax.ShapeDtypeStruct((N_DEV, c_rows, C_NAT), BF16),
                   jax.ShapeDtypeStruct(p_uq.shape[1:], F32), jax.ShapeDtypeStruct((N_DEV, rep_rows, LANES), F32)],
        scratch_shapes=scratch,
        compiler_params=pltpu.CompilerParams(dimension_semantics=("arbitrary",), vmem_limit_bytes=VMEM_BIG),
    )(*dhs, *ws, dx_res, dwc, p_uq, p_rep)


def _sum_landed(landed, c_all):
    c_rows = D_MODEL // N_DEV

    def body(rhi_ref, rlo_ref, roa_ref, rob_ref, rout_ref, call_ref, gin_ref, goa_ref, gob_ref, gout_ref):
        def total(ref, sl):
            acc = ref[0, sl, :].astype(F32)
            for s in range(1, N_DEV):
                acc = acc + ref[s, sl, :].astype(F32)
            return acc

        x, y, c = _mesh_pos()
        dev0 = jnp.where(4 * x + 2 * y + c == 0, 1.0, 0.0)
        for j in range(N_DEV):
            sl = slice(c_rows * j, c_rows * (j + 1))
            below = c_rows * j < P_IN_SPLIT
            tot = total(rhi_ref, sl) if below else total(rlo_ref, slice(c_rows * j - P_IN_SPLIT, c_rows * (j + 1) - P_IN_SPLIT))
            gin_ref[0, sl, C_NAT:SHARD_W] = tot[:, C_NAT:SHARD_W]
            gin_ref[0, sl, 0:C_NAT] = tot[:, 0:C_NAT] + dev0 * call_ref[j].astype(F32)
        goa_ref[0] = total(roa_ref, slice(None))
        gob_ref[0] = total(rob_ref, slice(None))
        gout_ref[0] = total(rout_ref, slice(None))

    return pl.pallas_call(
        body, name="sum_landed",
        out_shape=[jax.ShapeDtypeStruct((1, D_MODEL, SHARD_W), F32)]
        + [jax.ShapeDtypeStruct((1,) + r.shape[1:], F32) for r in landed[2:]],
        compiler_params=pltpu.CompilerParams(vmem_limit_bytes=VMEM_MID),
    )(*landed, c_all)


_O_CQ, _O_CKV, _O_KPE, _O_ZA, _O_U, _O_V, _O_ZB, _O_GA, _O_GB = 0, 384, 512, 544, 1056, 1568, 2080, 2592, 3616


def _to_segments(w):
    z = lambda n: jnp.zeros(w.shape[:-1] + (n,), w.dtype)
    seg_a = jnp.concatenate([w[..., _O_GA:_O_GB], w[..., _O_GB:IN_WIDTH], w[..., _O_ZA:_O_U]], axis=-1)
    seg_b = jnp.concatenate([w[..., _O_U:_O_V], w[..., _O_V:_O_ZB], w[..., _O_ZB:_O_GA]], axis=-1)
    seg_c = jnp.concatenate([w[..., _O_CQ:_O_CKV], z(CQ_PAD - Q_LORA_RANK), w[..., _O_CKV:_O_KPE],
                             z(ROPE_LO), w[..., _O_KPE:_O_ZA], z(LANES - ROPE_HI)], axis=-1)
    return seg_a, seg_b, seg_c


def _from_segments(seg_a, seg_b, seg_c):
    kpe0 = CQ_PAD + LANES + ROPE_LO
    return jnp.concatenate([
        seg_c[..., 0:Q_LORA_RANK], seg_c[..., CQ_PAD:CQ_PAD + LANES], seg_c[..., kpe0:kpe0 + QK_ROPE_DIM],
        seg_a[..., 2 * D_MODEL:SEG_A], seg_b, seg_a[..., 0:2 * D_MODEL]], axis=-1)


def kernel(x, positions, w_in, b_in, g_q, w_uq, g_kv, w_ukv, w_oa, sgu_ln_g, sgu_ln_b, w_s, b_s, w_ob, w_out, ln_g, ln_b, loss_target, m_w_in, m_b_in, m_g_q, m_w_uq, m_g_kv, m_w_ukv, m_w_oa, m_sgu_ln_g, m_sgu_ln_b, m_w_s, m_b_s, m_w_ob, m_w_out, m_ln_g, m_ln_b, v_w_in, v_b_in, v_g_q, v_w_uq, v_g_kv, v_w_ukv, v_w_oa, v_sgu_ln_g, v_sgu_ln_b, v_w_s, v_b_s, v_w_ob, v_w_out, v_ln_g, v_ln_b):
    w_uq2 = w_uq[0].reshape(Q_LORA_RANK // N_DEV, MLA_HEADS * QK_HEAD_DIM)
    inv_freq = ROPE_THETA ** (-jnp.arange(0, QK_ROPE_DIM, 2, dtype=F32) / QK_ROPE_DIM)
    invf_lane = jnp.concatenate([jnp.zeros((ROPE_LO,), F32), inv_freq, inv_freq,
                                 jnp.zeros((LANES - ROPE_HI,), F32)]).reshape(1, LANES)
    first = _gather_first(w_in, w_uq2, w_oa, w_ob, w_out, x[0], positions.reshape(SEQ, 1), invf_lane)
    partials = _local_step(x[0], loss_target[0], first, b_in, g_q, g_kv, w_ukv, sgu_ln_g, sgu_ln_b, w_s, b_s, ln_g, ln_b)
    weights = dict(w_in=w_in, b_in=b_in, g_q=g_q, w_uq=w_uq, g_kv=g_kv, w_ukv=w_ukv, w_oa=w_oa, sgu_ln_g=sgu_ln_g,
                   sgu_ln_b=sgu_ln_b, w_s=w_s, b_s=b_s, w_ob=w_ob, w_out=w_out, ln_g=ln_g, ln_b=ln_b)
    moms = dict(w_in=m_w_in, b_in=m_b_in, g_q=m_g_q, w_uq=m_w_uq, g_kv=m_g_kv, w_ukv=m_w_ukv, w_oa=m_w_oa,
                sgu_ln_g=m_sgu_ln_g, sgu_ln_b=m_sgu_ln_b, w_s=m_w_s, b_s=m_b_s, w_ob=m_w_ob, w_out=m_w_out,
                ln_g=m_ln_g, ln_b=m_ln_b)
    vars_ = dict(w_in=v_w_in, b_in=v_b_in, g_q=v_g_q, w_uq=v_w_uq, g_kv=v_g_kv, w_ukv=v_w_ukv, w_oa=v_w_oa,
                 sgu_ln_g=v_sgu_ln_g, sgu_ln_b=v_sgu_ln_b, w_s=v_w_s, b_s=v_b_s, w_ob=v_w_ob, w_out=v_w_out,
                 ln_g=v_ln_g, ln_b=v_ln_b)
    return _reduce_and_update(partials, weights, moms, vars_)


def _local_step(x2, tgt, first, b_in, g_q, g_kv, w_ukv, sgu_ln_g, sgu_ln_b, w_s, b_s, ln_g, ln_b):
    wc, wq, win_b, oa_b, ob_b, out_b, x_bf, xt_bf, c_t, sa_t, sb_t = first
    ba, bb, bc = _to_segments(b_in)
    w_ukv_bf = w_ukv[0].astype(BF16)
    wkn = jnp.pad(w_ukv_bf[:, :, :QK_NOPE_DIM], ((0, 0), (0, 0), (0, HEAD_PAD - QK_NOPE_DIM))).reshape(KV_LORA_RANK, -1)
    wv = jnp.pad(w_ukv_bf[:, :, QK_NOPE_DIM:], ((0, 0), (0, 0), (0, HEAD_PAD - V_HEAD_DIM))).reshape(KV_LORA_RANK, -1)
    gq = jnp.pad(g_q, ((0, 0), (0, CQ_PAD - Q_LORA_RANK)))
    bias_full = jnp.repeat(b_s[0].T, SGU_GROUP_DIM, axis=1)
    w_s3 = w_s[0]
    w_st3 = jnp.swapaxes(w_s3, 1, 2)

    h_c = _mm(x_bf, wc, bias=bc, tm=512, tn=SEG_C, name="in_proj_c")
    q, k, kt, vx, vxt = _mla_prep(h_c, gq, g_kv, wq, wkn, wv, c_t, sa_t, sb_t)
    o, lse, (g_in,) = _attn_fwd(q, kt, vx, (win_b,))
    wa, wb = _assemble_in(g_in)
    (h_a, h_b), (g_out, g_oa, g_ob) = _mm_pair(x_bf, (wa, wb), biases=(ba, bb), own=(out_b, oa_b, ob_b),
                                               tm=1024, tn=512, name="in_proj_ab")
    y_b = _sgu_fwd(h_b, sgu_ln_g, sgu_ln_b, w_s3, bias_full)
    w_oa_f, w_ob_f, w_out_f = _assemble_out(g_oa, g_ob, g_out)

    (loss_row, dx_res, dh_a, d_o, d_yb, p_oa, p_ob, p_out, d_lng, d_lnb, d_ba) = _merge(
        x2, o, h_a, y_b, tgt, w_oa_f, w_ob_f, w_out_f, ln_g, ln_b)
    (dh_b, d_ws, d_bs_t, d_slg, d_slb, d_bb), (r_out,) = _sgu_bwd(h_b, d_yb, sgu_ln_g, sgu_ln_b, w_s3, w_st3, bias_full,
                                                                 (p_out,))
    (d_wa, d_wb), (r_oa,) = _mm_pair(xt_bf, (dh_a, dh_b), out_dtype=BF16, parts=(p_oa,), tm=512, tn=512, name="dw_in_ab")
    p_hi, p_lo = _to_parts(d_wa, d_wb)
    dq, dk, dv, (r_hi,) = _attn_bwd(q, kt, k, vxt, d_o, o, lse, (p_hi,))
    (dh_c, p_uq, d_wkn, d_wv, d_gq, d_gkv, d_bc, d_wc), (r_lo, r_ob) = _mla_bwd(
        dq, dk, dv, h_c, xt_bf, gq, g_kv, wq, wkn, wv, c_t, sa_t, sb_t, (p_lo, p_ob))
    landed = (r_hi, r_lo, r_oa, r_ob, r_out)

    p_b_in = _from_segments(d_ba, d_bb, d_bc)
    p_w_ukv = jnp.concatenate([d_wkn.reshape(KV_LORA_RANK, MLA_HEADS, HEAD_PAD)[:, :, :QK_NOPE_DIM],
                               d_wv.reshape(KV_LORA_RANK, MLA_HEADS, HEAD_PAD)[:, :, :V_HEAD_DIM]], axis=-1)
    p_g_q = d_gq[:, :Q_LORA_RANK]
    p_b_s = d_bs_t[:, :SGU_GROUPS].T
    replicated = [p_b_in, p_g_q, d_gkv, p_w_ukv, d_slg, d_slb, d_ws, p_b_s, d_lng, d_lnb]
    return loss_row, ((dh_a, dh_b, dh_c), (wa, wb, wc), dx_res), landed, d_wc, p_uq, replicated


_NAMES = ["w_in", "b_in", "g_q", "w_uq", "g_kv", "w_ukv", "w_oa", "sgu_ln_g", "sgu_ln_b", "w_s", "b_s", "w_ob",
          "w_out", "ln_g", "ln_b"]
_REPLICATED = ["b_in", "g_q", "g_kv", "w_ukv", "sgu_ln_g", "sgu_ln_b", "w_s", "b_s", "ln_g", "ln_b"]


def _reduce_and_update(partials, weights, moms, vars_):
    loss_row, (dhs, ws, dx_res), landed, d_wc, p_uq, replicated = partials
    def piece(a):
        flat = a.reshape(-1)
        return jnp.pad(flat, (0, -flat.size % PACK_ALIGN))

    rep_flat = jnp.concatenate([piece(a) for a in replicated] + [piece(loss_row[0, :1])])
    rep_flat = jnp.pad(rep_flat, (0, N_DEV * PACK_R_ROWS * LANES - rep_flat.size))
    dx_ab, c_all, g_uq, rep_all = _dx_tail(dhs[:2], ws[:2], dx_res, d_wc, p_uq,
                                           rep_flat.reshape(N_DEV, PACK_R_ROWS, LANES))
    dx = _mm(dhs[2], ws[2], tb=True, add=dx_ab, tm=512, tn=D_MODEL, name="dx_c")
    g_in, g_oa, g_ob, g_out = _sum_landed(landed, c_all)
    rep_sum = rep_all.reshape(-1)
    grads, pos = dict(w_in=g_in, w_uq=g_uq, w_oa=g_oa, w_ob=g_ob, w_out=g_out), 0
    for nm in _REPLICATED:
        grads[nm] = rep_sum[pos:pos + weights[nm].size]
        pos += weights[nm].size + -weights[nm].size % PACK_ALIGN
    loss = rep_sum[pos]
    grads = {nm: grads[nm].reshape(weights[nm].shape) for nm in _NAMES}
    deltas, new_m, new_v = _adamw_all([weights[nm] for nm in _NAMES], [grads[nm] for nm in _NAMES],
                                      [moms[nm] for nm in _NAMES], [vars_[nm] for nm in _NAMES])
    return (loss, dx.reshape(1, SEQ, D_MODEL), *[grads[nm] for nm in _NAMES], *deltas, *new_m, *new_v)
```

```python
import math

import jax
import jax.numpy as jnp
from jax import lax
from jax.experimental import pallas as pl
from jax.experimental.pallas import tpu as pltpu

F32 = jnp.float32
BF16 = jnp.bfloat16

D_MODEL = 1024
SEQ = 2048
N_DEV = 8
MLA_HEADS = 8
Q_LORA_RANK = 384
KV_LORA_RANK = 128
QK_NOPE_DIM = 64
QK_ROPE_DIM = 32
V_HEAD_DIM = 64
QK_HEAD_DIM = QK_NOPE_DIM + QK_ROPE_DIM
MLA_WIDTH = MLA_HEADS * V_HEAD_DIM
ROPE_THETA = 10000.0
SGU_GROUPS = 8
SGU_GROUP_DIM = 64
SGU_WIDTH = SGU_GROUPS * SGU_GROUP_DIM
CHUNK = 128
RMS_EPS = 1e-6
LN_EPS = 1e-5
DN_ALPHA = 2.0 ** 0.25
IN_WIDTH = 4640
ATTN_SCALE = QK_HEAD_DIM ** -0.5

ADAM_LR = 0.001
ADAM_B1 = 0.9
ADAM_B2 = 0.999
ADAM_EPS = 1e-08
ADAM_WD = 0.01
ADAM_STEP = 10

LANES = 128
HEAD_PAD = 128
ROPE_LO = QK_NOPE_DIM
ROPE_MID = ROPE_LO + QK_ROPE_DIM // 2
ROPE_HI = ROPE_LO + QK_ROPE_DIM
CQ_PAD = 512

SEG_A = 2560
SEG_B = 1536
SEG_C = 768

PACK_R_ROWS = 272
PACK_ALIGN = 8 * LANES
VMEM_BIG = 56 * 1024 * 1024
VMEM_MID = 40 * 1024 * 1024


def _sigmoid(x):
    return 1.0 / (1.0 + jnp.exp(-x))


def _gelu_and_grad(x):
    c0 = math.sqrt(2.0 / math.pi)
    x2 = x * x
    t = jnp.tanh(c0 * (x + 0.044715 * x * x2))
    g = 0.5 * x * (1.0 + t)
    dg = 0.5 * (1.0 + t) + 0.5 * x * (1.0 - t * t) * (c0 * (1.0 + 3.0 * 0.044715 * x2))
    return g, dg


def _dot(a, b, dims):
    return lax.dot_general(a, b, (dims, ((), ())), preferred_element_type=F32)


_NN = ((1,), (0,))
_NT = ((1,), (1,))
_TN = ((0,), (0,))


def _store_grad(dh_ref, db_ref, col, val):
    cols = slice(col, col + val.shape[1])
    dh_ref[:, cols] = val.astype(BF16)
    db_ref[:, cols] += jnp.sum(val, axis=0, keepdims=True)


def _mm(a, b, *, tb=False, bias=None, add=None, out_dtype=F32, tm, tn, name):
    m, k = a.shape
    n = b.shape[0] if tb else b.shape[1]
    assert m % tm == 0 and n % tn == 0
    dims = _NT if tb else _NN

    def body(*refs):
        a_ref, b_ref = refs[0], refs[1]
        pos = 2
        r = _dot(a_ref[...], b_ref[...], dims)
        if bias is not None:
            r = r + refs[pos][...]; pos += 1
        if add is not None:
            r = r + refs[pos][...]; pos += 1
        refs[pos][...] = r.astype(out_dtype)

    b_spec = pl.BlockSpec((tn, k), lambda j, i: (j, 0)) if tb else pl.BlockSpec((k, tn), lambda j, i: (0, j))
    in_specs, args = [pl.BlockSpec((tm, k), lambda j, i: (i, 0)), b_spec], [a, b]
    if bias is not None:
        in_specs.append(pl.BlockSpec((1, tn), lambda j, i: (0, j))); args.append(bias)
    if add is not None:
        in_specs.append(pl.BlockSpec((tm, tn), lambda j, i: (i, j))); args.append(add)
    return pl.pallas_call(
        body, name=name, grid=(n // tn, m // tm), in_specs=in_specs,
        out_specs=pl.BlockSpec((tm, tn), lambda j, i: (i, j)),
        out_shape=jax.ShapeDtypeStruct((m, n), out_dtype),
        compiler_params=pltpu.CompilerParams(dimension_semantics=("arbitrary", "arbitrary"), vmem_limit_bytes=VMEM_BIG),
    )(*args)


def _mm_comm(a, bs, *, biases=(), out_dtype=F32, own=(), parts=(), tm, tn, name):
    m, k = a.shape
    widths = [b.shape[1] for b in bs]
    assert m % tm == 0 and all(w % tn == 0 for w in widths) and len(biases) in (0, len(bs)) and bool(own) != bool(parts)
    nb, nbias, nown = len(bs), len(biases), len(own) + len(parts)
    n0 = widths[0] // tn
    nblocks = sum(widths) // tn
    nm = m // tm
    nsteps = nblocks * nm

    def body(*refs):
        a_ref, b_refs, bias_refs = refs[0], refs[1:1 + nb], refs[1 + nb:1 + nb + nbias]
        refs = refs[1 + nb + nbias:]
        own_refs, out_refs, gat_refs = refs[:nown], refs[nown:nown + nb], refs[nown + nb:2 * nown + nb]
        send_sems, recv_sems, local_sems = refs[2 * nown + nb:]
        j = pl.program_id(0)
        for which, mine in enumerate((j < n0, j >= n0)[:nb]):
            @pl.when(mine)
            def _(which=which):
                r = _dot(a_ref[...], b_refs[which][...], _NN)
                if nbias:
                    r = r + bias_refs[which][...]
                out_refs[which][...] = r.astype(out_dtype)
        step = j * nm + pl.program_id(1)
        if own:
            _gather_behind(own_refs, gat_refs, send_sems, recv_sems, local_sems, step, nsteps - 2, nsteps - 1)
        else:
            exchange = _exchange_parts(own_refs, gat_refs, send_sems, recv_sems, local_sems)
            _exchange_start(step == 0, exchange)
            _exchange_finish(step == nsteps - 1, exchange)

    col = [lambda j: jnp.minimum(j, n0 - 1), lambda j: jnp.maximum(j - n0, 0)][:nb]
    row = [lambda j, i: jnp.where(j < n0, i, nm - 1), lambda j, i: jnp.where(j < n0, 0, i)][:nb]
    in_specs = [pl.BlockSpec((tm, k), lambda j, i: (i, 0))]
    in_specs += [pl.BlockSpec((k, tn), lambda j, i, c=c: (0, c(j))) for c in col]
    in_specs += [pl.BlockSpec((1, tn), lambda j, i, c=c: (0, c(j))) for c in col[:nbias]]
    hbm = pl.BlockSpec(memory_space=pl.ANY)
    res = pl.pallas_call(
        body, name=name, grid=(nblocks, nm), in_specs=in_specs + [hbm] * nown,
        out_specs=[pl.BlockSpec((tm, tn), lambda j, i, r=r, c=c: (r(j, i), c(j))) for r, c in zip(row, col)] + [hbm] * nown,
        out_shape=[jax.ShapeDtypeStruct((m, w), out_dtype) for w in widths]
        + [jax.ShapeDtypeStruct((N_DEV,) + o.shape, o.dtype) for o in own]
        + [jax.ShapeDtypeStruct(p.shape, p.dtype) for p in parts],
        scratch_shapes=_exchange_sems(nown),
        compiler_params=pltpu.CompilerParams(dimension_semantics=("arbitrary", "arbitrary"), vmem_limit_bytes=VMEM_BIG),
    )(a, *bs, *biases, *own, *parts)
    return res[:nb], res[nb:]


def _rope(x, c, sa, sb):
    return x * c + pltpu.roll(x, LANES - 16, 1) * sa + pltpu.roll(x, 16, 1) * sb


def _rope_t(dy, c, sa, sb):
    return dy * c + pltpu.roll(dy * sa, 16, 1) + pltpu.roll(dy * sb, LANES - 16, 1)


def _mla_prep(h_c, gq, gkv, wq, wkn, wvx, c_t, sa_t, sb_t):
    tm = 256
    hw = MLA_HEADS * HEAD_PAD

    def body(cq_ref, ckv_ref, kpe_ref, gq_ref, gkv_ref, wq_ref, wkn_ref, wvx_ref, c_ref, sa_ref, sb_ref,
             q_ref, k_ref, kt_ref, vx_ref, vxt_ref):
        c, sa, sb = c_ref[...], sa_ref[...], sb_ref[...]
        cq = cq_ref[...]
        rq = lax.rsqrt(jnp.sum(cq * cq, axis=1, keepdims=True) * (1.0 / Q_LORA_RANK) + RMS_EPS)
        cqn = ((cq * rq) * gq_ref[...]).astype(BF16)
        qall = _dot(cqn, wq_ref[...], _NN)
        for h in range(MLA_HEADS):
            sl = slice(HEAD_PAD * h, HEAD_PAD * (h + 1))
            q_ref[:, sl] = (_rope(qall[:, sl], c, sa, sb) * ATTN_SCALE).astype(BF16)
        ckv = ckv_ref[...]
        rkv = lax.rsqrt(jnp.sum(ckv * ckv, axis=1, keepdims=True) * (1.0 / KV_LORA_RANK) + RMS_EPS)
        ckvn = ((ckv * rkv) * gkv_ref[...]).astype(BF16)
        knall = _dot(ckvn, wkn_ref[...], _NN)
        vall = _dot(ckvn, wvx_ref[...], _NN)
        kper = _rope(kpe_ref[...], c, sa, sb)
        ones_half = (lax.broadcasted_iota(jnp.int32, (tm, HEAD_PAD), 1) >= V_HEAD_DIM).astype(F32)
        for h in range(MLA_HEADS):
            sl = slice(HEAD_PAD * h, HEAD_PAD * (h + 1))
            kh = knall[:, sl] + kper
            vh = vall[:, sl] + ones_half
            k_ref[:, sl] = kh.astype(BF16)
            kt_ref[sl, :] = kh.T.astype(BF16)
            vx_ref[:, sl] = vh.astype(BF16)
            vxt_ref[sl, :] = vh.T.astype(BF16)

    full = lambda shape: pl.BlockSpec(shape, lambda i: (0, 0))
    tab = pl.BlockSpec((tm, LANES), lambda i: (i, 0))
    row = pl.BlockSpec((tm, hw), lambda i: (i, 0))
    col = pl.BlockSpec((hw, tm), lambda i: (0, i))
    return pl.pallas_call(
        body, name="mla_prep", grid=(SEQ // tm,),
        in_specs=[pl.BlockSpec((tm, CQ_PAD), lambda i: (i, 0)),
                  pl.BlockSpec((tm, LANES), lambda i: (i, CQ_PAD // LANES)),
                  pl.BlockSpec((tm, LANES), lambda i: (i, CQ_PAD // LANES + 1)),
                  full((1, CQ_PAD)), full((1, KV_LORA_RANK)),
                  full((CQ_PAD, hw)), full((KV_LORA_RANK, hw)), full((KV_LORA_RANK, hw)), tab, tab, tab],
        out_specs=[row, row, col, row, col],
        out_shape=[jax.ShapeDtypeStruct((SEQ, hw), BF16), jax.ShapeDtypeStruct((SEQ, hw), BF16),
                   jax.ShapeDtypeStruct((hw, SEQ), BF16), jax.ShapeDtypeStruct((SEQ, hw), BF16),
                   jax.ShapeDtypeStruct((hw, SEQ), BF16)],
        compiler_params=pltpu.CompilerParams(dimension_semantics=("arbitrary",), vmem_limit_bytes=VMEM_MID),
    )(h_c, h_c, h_c, gq, gkv, wq, wkn, wvx, c_t, sa_t, sb_t)


ATT_T = 512
ATT_STRIP = 64


def _attn_fwd(q, kt, vx, own):
    t, rs = ATT_T, ATT_STRIP
    nown = len(own)
    nq = SEQ // t
    nsteps = (MLA_HEADS // 2) * nq

    def body(q_ref, kt_ref, vx_ref, *rest):
        own_refs, (o_ref, l_ref), gat_refs = rest[:nown], rest[nown:nown + 2], rest[nown + 2:2 * nown + 2]
        s_scr, p_scr, m_scr, a_scr, acc_scr, send_sems, recv_sems, local_sems = rest[2 * nown + 2:]
        qi = pl.program_id(1)
        _gather_behind(own_refs, gat_refs, send_sems, recv_sems, local_sems, pl.program_id(0) * nq + qi,
                       nsteps - 2, nsteps - 1)
        lane = lax.broadcasted_iota(jnp.int32, (t, LANES), 1)
        m_scr[...] = jnp.full((2, t, LANES), -1e30, F32)
        acc_scr[...] = jnp.zeros((2, t, LANES), F32)

        def block(j, masked):
            off = pl.multiple_of(j * t, t)
            for a in range(2):
                sl = slice(HEAD_PAD * a, HEAD_PAD * (a + 1))
                s_scr[a] = _dot(q_ref[:, sl], kt_ref[sl, pl.ds(off, t)], _NN)
                for r in range(t // rs):
                    rows = slice(rs * r, rs * (r + 1))
                    s = s_scr[a, rows, :]
                    if masked:
                        rowi = lax.broadcasted_iota(jnp.int32, (rs, t), 0) + rs * r
                        coli = lax.broadcasted_iota(jnp.int32, (rs, t), 1)
                        s = jnp.where(coli <= rowi, s, -1e30)
                    m_old = m_scr[a, rows, :]
                    m_new = jnp.maximum(m_old, jnp.max(s, axis=1, keepdims=True))
                    p_scr[a, rows, :] = jnp.exp(s - m_new[:, :1]).astype(BF16)
                    a_scr[a, rows, :] = jnp.exp(m_old - m_new)
                    m_scr[a, rows, :] = m_new
                acc_scr[a] = acc_scr[a] * a_scr[a] + _dot(p_scr[a], vx_ref[pl.ds(off, t), sl], _NN)

        def step(j, carry):
            block(j, False)
            return carry
        lax.fori_loop(0, qi, step, 0)
        block(qi, True)
        res = []
        for a in range(2):
            acc = acc_scr[a]
            l = acc[:, V_HEAD_DIM:V_HEAD_DIM + 1]
            res.append((acc / l, m_scr[a] + jnp.log(l)))
        o_ref[...] = jnp.where(lane < V_HEAD_DIM, res[0][0], pltpu.roll(res[1][0], V_HEAD_DIM, 1))
        l_ref[...] = jnp.where(lane < V_HEAD_DIM, res[0][1], res[1][1])

    hbm = pl.BlockSpec(memory_space=pl.ANY)
    res = pl.pallas_call(
        body, name="attn_fwd", grid=(MLA_HEADS // 2, nq),
        in_specs=[pl.BlockSpec((t, 2 * HEAD_PAD), lambda p, i: (i, p)),
                  pl.BlockSpec((2 * HEAD_PAD, SEQ), lambda p, i: (p, 0)),
                  pl.BlockSpec((SEQ, 2 * HEAD_PAD), lambda p, i: (0, p))] + [hbm] * nown,
        out_specs=[pl.BlockSpec((t, LANES), lambda p, i: (i, p)),
                   pl.BlockSpec((t, LANES), lambda p, i: (i, p))] + [hbm] * nown,
        out_shape=[jax.ShapeDtypeStruct((SEQ, MLA_WIDTH), F32), jax.ShapeDtypeStruct((SEQ, MLA_WIDTH), F32)]
        + [jax.ShapeDtypeStruct((N_DEV,) + a.shape, a.dtype) for a in own],
        scratch_shapes=[pltpu.VMEM((2, t, t), F32), pltpu.VMEM((2, t, t), BF16), pltpu.VMEM((2, t, LANES), F32),
                        pltpu.VMEM((2, t, LANES), F32), pltpu.VMEM((2, t, LANES), F32)] + _exchange_sems(nown),
        compiler_params=pltpu.CompilerParams(dimension_semantics=("arbitrary", "arbitrary"), vmem_limit_bytes=VMEM_MID),
    )(q, kt, vx, *own)
    return res[0], res[1], res[2:]


def _exchange_parts(parts, lands, send_sems, recv_sems, local_sems):
    x, y, c = _mesh_pos()
    me = 4 * x + 2 * y + c
    peers = [(x, y, 1 - c), (1 - x, y, c), (x, 1 - y, c), (1 - x, 1 - y, c),
             (1 - x, y, 1 - c), (x, 1 - y, 1 - c), (1 - x, 1 - y, 1 - c)]
    remote, local = [], []
    for a, (part, land) in enumerate(zip(parts, lands)):
        for k, peer in enumerate(peers):
            t = 4 * peer[0] + 2 * peer[1] + peer[2]
            remote.append(_remote(part.at[t], land.at[me], send_sems, recv_sems, 7 * a + k, peer))
        local.append(pltpu.make_async_copy(part.at[me], land.at[me], local_sems.at[a]))
    return remote, local


def _exchange_start(first_step, exchange):
    remote, local = exchange

    @pl.when(first_step)
    def _():
        for cp in remote + local:
            cp.start()


def _exchange_finish(last_step, exchange):
    remote, local = exchange

    @pl.when(last_step)
    def _():
        for cp in remote:
            cp.wait_recv()
        for cp in remote:
            cp.wait_send()
        for cp in local:
            cp.wait()


def _exchange_sems(npart):
    return [pltpu.SemaphoreType.DMA((7 * npart,)), pltpu.SemaphoreType.DMA((7 * npart,)),
            pltpu.SemaphoreType.DMA((npart,))]


def _attn_bwd(q, kt, k, vxt, d_o, o, lse, parts):
    t, rs = ATT_T, ATT_STRIP
    nq = SEQ // t
    npart = len(parts)
    nsteps = MLA_HEADS // 2

    def body(q_ref, kt_ref, k_ref, vxt_ref, do_ref, o_ref, l_ref, *rest):
        part_refs, rest = rest[:npart], rest[npart:]
        dq_ref, dk_ref, dv_ref = rest[:3]
        land_refs, rest = rest[3:3 + npart], rest[3 + npart:]
        s_scr, dp_scr, p_scr, ds_scr, st_scr, send_sems, recv_sems, local_sems = rest
        exchange = _exchange_parts(part_refs, land_refs, send_sems, recv_sems, local_sems)
        _exchange_start(pl.program_id(0) == 0, exchange)
        dk_ref[...] = jnp.zeros_like(dk_ref)
        dv_ref[...] = jnp.zeros_like(dv_ref)
        lane = lax.broadcasted_iota(jnp.int32, (t, LANES), 1)

        def qtile(i, carry):
            ioff = pl.multiple_of(i * t, t)
            do_i = do_ref[pl.ds(ioff, t), :]
            o_i = o_ref[pl.ds(ioff, t), :]
            l_i = l_ref[pl.ds(ioff, t), :]
            for a in range(2):
                sl = slice(HEAD_PAD * a, HEAD_PAD * (a + 1))
                sel = (lane < V_HEAD_DIM) if a == 0 else (lane >= V_HEAD_DIM)
                doa = jnp.where(sel, do_i, 0.0)
                oa = o_i
                if a == 1:
                    doa = pltpu.roll(doa, V_HEAD_DIM, 1)
                    oa = pltpu.roll(o_i, V_HEAD_DIM, 1)
                st_scr[0] = jnp.broadcast_to(jnp.sum(doa * oa, axis=1, keepdims=True), (t, LANES))
                st_scr[1] = jnp.broadcast_to(l_i[:, V_HEAD_DIM * a:V_HEAD_DIM * a + 1], (t, LANES))
                doa_bf = doa.astype(BF16)
                qa = q_ref[pl.ds(ioff, t), sl]

                def block(j, masked, dq_acc, sl=sl, qa=qa, doa_bf=doa_bf):
                    joff = pl.multiple_of(j * t, t)
                    s_scr[...] = _dot(qa, kt_ref[sl, pl.ds(joff, t)], _NN)
                    dp_scr[...] = _dot(doa_bf, vxt_ref[sl, pl.ds(joff, t)], _NN)
                    for r in range(t // rs):
                        rows = slice(rs * r, rs * (r + 1))
                        p = jnp.exp(s_scr[rows, :] - st_scr[1, rows, :1])
                        if masked:
                            rowi = lax.broadcasted_iota(jnp.int32, (rs, t), 0) + rs * r
                            coli = lax.broadcasted_iota(jnp.int32, (rs, t), 1)
                            p = jnp.where(coli <= rowi, p, 0.0)
                        p_scr[rows, :] = p.astype(BF16)
                        ds_scr[rows, :] = (p * (dp_scr[rows, :] - st_scr[0, rows, :1])).astype(BF16)
                    dk_ref[pl.ds(joff, t), sl] += _dot(ds_scr[...], qa, _TN)
                    dv_ref[pl.ds(joff, t), sl] += _dot(p_scr[...], doa_bf, _TN)
                    return dq_acc + _dot(ds_scr[...], k_ref[pl.ds(joff, t), sl], _NN)

                dq_acc = lax.fori_loop(0, i, lambda j, acc: block(j, False, acc), jnp.zeros((t, HEAD_PAD), F32))
                dq_ref[pl.ds(ioff, t), sl] = block(i, True, dq_acc)
            return carry

        lax.fori_loop(0, nq, qtile, 0)
        _exchange_finish(pl.program_id(0) == nsteps - 1, exchange)

    hw = MLA_HEADS * HEAD_PAD
    wide = pl.BlockSpec((SEQ, 2 * HEAD_PAD), lambda p: (0, p))
    wide_t = pl.BlockSpec((2 * HEAD_PAD, SEQ), lambda p: (p, 0))
    narrow = pl.BlockSpec((SEQ, LANES), lambda p: (0, p))
    hbm = pl.BlockSpec(memory_space=pl.ANY)
    res = pl.pallas_call(
        body, name="attn_bwd", grid=(nsteps,),
        in_specs=[wide, wide_t, wide, wide_t, narrow, narrow, narrow] + [hbm] * npart,
        out_specs=[wide, wide, wide] + [hbm] * npart,
        out_shape=[jax.ShapeDtypeStruct((SEQ, hw), F32)] * 3 + [jax.ShapeDtypeStruct(p.shape, p.dtype) for p in parts],
        scratch_shapes=[pltpu.VMEM((t, t), F32), pltpu.VMEM((t, t), F32), pltpu.VMEM((t, t), BF16),
                        pltpu.VMEM((t, t), BF16), pltpu.VMEM((2, t, LANES), F32)] + _exchange_sems(npart),
        compiler_params=pltpu.CompilerParams(dimension_semantics=("arbitrary",), vmem_limit_bytes=VMEM_BIG),
    )(q, kt, k, vxt, d_o, o, lse, *parts)
    return res[0], res[1], res[2], res[3:]


def _sgu_math(u, v, zb, lg, lb, ws_ref, bias):
    ug, dug = _gelu_and_grad(u)
    vg, dvg = _gelu_and_grad(v)
    mu = jnp.mean(vg, axis=1, keepdims=True)
    xc = vg - mu
    rstd = lax.rsqrt(jnp.mean(xc * xc, axis=1, keepdims=True) + LN_EPS)
    xh = xc * rstd
    vn_bf = (xh * lg + lb).astype(BF16)
    grp = lax.broadcasted_iota(jnp.int32, (CHUNK, SGU_WIDTH), 1) // SGU_GROUP_DIM
    r_i = lax.broadcasted_iota(jnp.int32, (CHUNK, CHUNK), 0)
    c_i = lax.broadcasted_iota(jnp.int32, (CHUNK, CHUNK), 1)
    tri, tri_t = r_i >= c_i, r_i <= c_i
    mixed = bias
    for g in range(SGU_GROUPS):
        wt = jnp.where(tri, ws_ref[g], 0.0).astype(BF16)
        mixed = mixed + jnp.where(grp == g, _dot(wt, vn_bf, _NN), 0.0)
    sb = _sigmoid(zb)
    return ug, dug, dvg, rstd, xh, vn_bf, grp, tri, tri_t, mixed, sb


def _sgu_fwd(h_b, lg, lb, w_s, bias_full):
    def body(u_ref, v_ref, zb_ref, lg_ref, lb_ref, ws_ref, bias_ref, yb_ref):
        zb = zb_ref[...]
        ug, _, _, _, _, _, _, _, _, mixed, sb = _sgu_math(u_ref[...], v_ref[...], zb, lg_ref[...], lb_ref[...],
                                                       ws_ref, bias_ref[...])
        yb_ref[...] = (ug * mixed) * (zb * sb)

    blk = lambda c: pl.BlockSpec((CHUNK, SGU_WIDTH), lambda i, c=c: (i, c))
    full2 = lambda shape: pl.BlockSpec(shape, lambda i: (0, 0))
    return pl.pallas_call(
        body, name="sgu_fwd", grid=(SEQ // CHUNK,),
        in_specs=[blk(0), blk(1), blk(2), full2((1, SGU_WIDTH)), full2((1, SGU_WIDTH)),
                  pl.BlockSpec((SGU_GROUPS, CHUNK, CHUNK), lambda i: (0, 0, 0)), full2((CHUNK, SGU_WIDTH))],
        out_specs=pl.BlockSpec((CHUNK, SGU_WIDTH), lambda i: (i, 0)),
        out_shape=jax.ShapeDtypeStruct((SEQ, SGU_WIDTH), F32),
        compiler_params=pltpu.CompilerParams(dimension_semantics=("arbitrary",)),
    )(h_b, h_b, h_b, lg, lb, w_s, bias_full)


def _sgu_bwd(h_b, d_yb, lg, lb, w_s, w_st, bias_full, parts):
    nsteps = SEQ // CHUNK
    npart = len(parts)

    def body(u_ref, v_ref, zb_ref, dyb_ref, lg_ref, lb_ref, ws_ref, wst_ref, bias_ref, *rest):
        part_refs, rest = rest[:npart], rest[npart:]
        dhb_ref, dws_ref, dbs_ref, dlg_ref, dlb_ref, dbb_ref = rest[:6]
        land_refs, (dbias_acc, send_sems, recv_sems, local_sems) = rest[6:6 + npart], rest[6 + npart:]
        step = pl.program_id(0)
        exchange = _exchange_parts(part_refs, land_refs, send_sems, recv_sems, local_sems)
        _exchange_start(step == 0, exchange)

        @pl.when(step == 0)
        def _():
            dbb_ref[...] = jnp.zeros_like(dbb_ref)
            dws_ref[...] = jnp.zeros_like(dws_ref)
            dlg_ref[...] = jnp.zeros_like(dlg_ref)
            dlb_ref[...] = jnp.zeros_like(dlb_ref)
            dbias_acc[...] = jnp.zeros_like(dbias_acc)

        zb = zb_ref[...]
        lg = lg_ref[...]
        ug, dug, dvg, rstd, xh, vn_bf, grp, tri, tri_t, mixed, sb = _sgu_math(
            u_ref[...], v_ref[...], zb, lg, lb_ref[...], ws_ref, bias_ref[...])
        dyb = dyb_ref[...]
        dsgu = dyb * (zb * sb)
        dzb = dyb * (ug * mixed) * (sb * (1.0 + zb * (1.0 - sb)))
        du = dsgu * mixed * dug
        dmixed = dsgu * ug
        dbias_acc[...] += dmixed
        dvn = jnp.zeros((CHUNK, SGU_WIDTH), F32)
        for g in range(SGU_GROUPS):
            dm_g = jnp.where(grp == g, dmixed, 0.0).astype(BF16)
            wtt = jnp.where(tri_t, wst_ref[g], 0.0).astype(BF16)
            dvn = dvn + _dot(wtt, dm_g, _NN)
            dws_ref[g] += jnp.where(tri, _dot(dm_g, vn_bf, _NT), 0.0)
        dlg_ref[...] += jnp.sum(dvn * xh, axis=0, keepdims=True)
        dlb_ref[...] += jnp.sum(dvn, axis=0, keepdims=True)
        dxh = dvn * lg
        dvgel = rstd * (dxh - jnp.mean(dxh, axis=1, keepdims=True) - xh * jnp.mean(dxh * xh, axis=1, keepdims=True))
        _store_grad(dhb_ref, dbb_ref, 0, du)
        _store_grad(dhb_ref, dbb_ref, SGU_WIDTH, dvgel * dvg)
        _store_grad(dhb_ref, dbb_ref, 2 * SGU_WIDTH, dzb)

        @pl.when(step == nsteps - 1)
        def _():
            acc = dbias_acc[...]
            lane = lax.broadcasted_iota(jnp.int32, (CHUNK, LANES), 1)
            out = jnp.zeros((CHUNK, LANES), F32)
            for g in range(SGU_GROUPS):
                sg = jnp.sum(jnp.where(grp == g, acc, 0.0), axis=1, keepdims=True)
                out = jnp.where(lane == g, sg, out)
            dbs_ref[...] = out

        _exchange_finish(step == nsteps - 1, exchange)

    blk = lambda c: pl.BlockSpec((CHUNK, SGU_WIDTH), lambda i, c=c: (i, c))
    full2 = lambda shape: pl.BlockSpec(shape, lambda i: (0, 0))
    full3 = pl.BlockSpec((SGU_GROUPS, CHUNK, CHUNK), lambda i: (0, 0, 0))
    hbm = pl.BlockSpec(memory_space=pl.ANY)
    res = pl.pallas_call(
        body, name="sgu_bwd", grid=(nsteps,),
        in_specs=[blk(0), blk(1), blk(2), pl.BlockSpec((CHUNK, SGU_WIDTH), lambda i: (i, 0)),
                  full2((1, SGU_WIDTH)), full2((1, SGU_WIDTH)), full3, full3, full2((CHUNK, SGU_WIDTH))] + [hbm] * npart,
        out_specs=[pl.BlockSpec((CHUNK, SEG_B), lambda i: (i, 0)), full3, full2((CHUNK, LANES)),
                   full2((1, SGU_WIDTH)), full2((1, SGU_WIDTH)), full2((1, SEG_B))] + [hbm] * npart,
        out_shape=[jax.ShapeDtypeStruct((SEQ, SEG_B), BF16),
                   jax.ShapeDtypeStruct((SGU_GROUPS, CHUNK, CHUNK), F32),
                   jax.ShapeDtypeStruct((CHUNK, LANES), F32),
                   jax.ShapeDtypeStruct((1, SGU_WIDTH), F32), jax.ShapeDtypeStruct((1, SGU_WIDTH), F32),
                   jax.ShapeDtypeStruct((1, SEG_B), F32)] + [jax.ShapeDtypeStruct(p.shape, p.dtype) for p in parts],
        scratch_shapes=[pltpu.VMEM((CHUNK, SGU_WIDTH), F32)] + _exchange_sems(npart),
        compiler_params=pltpu.CompilerParams(dimension_semantics=("arbitrary",)),
    )(h_b, h_b, h_b, d_yb, lg, lb, w_s, w_st, bias_full, *parts)
    return res[:6], res[6:]


def _merge(x, o, h_a, y_b, target, w_oa, w_ob, w_out, ln_g, ln_b):
    tm = 256
    nsteps = SEQ // tm

    def body(x_ref, o_ref, ga_ref, gb_ref, za_ref, yb_ref, tgt_ref, woa_ref, wob_ref, wout_ref, lng_ref, lnb_ref,
             loss_ref, dxr_ref, dha_ref, do_ref, dyb_ref, poa_ref, pob_ref, pout_ref, dlng_ref, dlnb_ref, dba_ref,
             dwoa_ref, dwob_ref, dwout_ref):
        step = pl.program_id(0)

        @pl.when(step == 0)
        def _():
            for r in (loss_ref, dwoa_ref, dwob_ref, dwout_ref, dlng_ref, dlnb_ref, dba_ref):
                r[...] = jnp.zeros_like(r)

        o = o_ref[...]
        za = za_ref[...]
        sa = _sigmoid(za)
        ya_bf = (o * (za * sa)).astype(BF16)
        yb_bf = yb_ref[...].astype(BF16)
        woa, wob, wout = woa_ref[...], wob_ref[...], wout_ref[...]
        pa = _dot(ya_bf, woa, _NN)
        pb = _dot(yb_bf, wob, _NN)
        sga = _sigmoid(ga_ref[...])
        sgb = _sigmoid(gb_ref[...])
        merged_bf = (sga * pa + sgb * pb).astype(BF16)
        r = DN_ALPHA * x_ref[...] + _dot(merged_bf, wout, _NN)
        mu = jnp.mean(r, axis=1, keepdims=True)
        rc = r - mu
        rstd = lax.rsqrt(jnp.mean(rc * rc, axis=1, keepdims=True) + LN_EPS)
        xh = rc * rstd
        lng = lng_ref[...]
        y = xh * lng + lnb_ref[...]
        e = y - tgt_ref[...]
        loss_ref[...] += 0.5 * jnp.sum(jnp.sum(e * e, axis=1, keepdims=True) * (1.0 / D_MODEL), axis=0, keepdims=True)

        dy = e * (1.0 / D_MODEL)
        dlng_ref[...] += jnp.sum(dy * xh, axis=0, keepdims=True)
        dlnb_ref[...] += jnp.sum(dy, axis=0, keepdims=True)
        dxh = dy * lng
        dr = rstd * (dxh - jnp.mean(dxh, axis=1, keepdims=True) - xh * jnp.mean(dxh * xh, axis=1, keepdims=True))
        dxr_ref[...] = DN_ALPHA * dr
        dr_bf = dr.astype(BF16)
        dwout_ref[...] += _dot(merged_bf, dr_bf, _TN)
        dmerged = _dot(dr_bf, wout, _NT)
        dpa_bf = (dmerged * sga).astype(BF16)
        dpb_bf = (dmerged * sgb).astype(BF16)
        _store_grad(dha_ref, dba_ref, 0, dmerged * pa * (sga * (1.0 - sga)))
        _store_grad(dha_ref, dba_ref, D_MODEL, dmerged * pb * (sgb * (1.0 - sgb)))
        dwoa_ref[...] += _dot(ya_bf, dpa_bf, _TN)
        dwob_ref[...] += _dot(yb_bf, dpb_bf, _TN)
        dya = _dot(dpa_bf, woa, _NT)
        dyb_ref[...] = _dot(dpb_bf, wob, _NT)
        do_ref[...] = dya * (za * sa)
        _store_grad(dha_ref, dba_ref, 2 * D_MODEL, dya * o * (sa * (1.0 + za * (1.0 - sa))))

        @pl.when(step == nsteps - 1)
        def _():
            cols = D_MODEL // N_DEV
            for j in range(N_DEV):
                poa_ref[j] = dwoa_ref[:, cols * j:cols * (j + 1)].astype(BF16)
                pob_ref[j] = dwob_ref[:, cols * j:cols * (j + 1)].astype(BF16)
                pout_ref[j] = dwout_ref[cols * j:cols * (j + 1), :].astype(BF16)

    row = lambda w, c=0: pl.BlockSpec((tm, w), lambda i, c=c: (i, c))
    full = lambda shape: pl.BlockSpec(shape, lambda i: (0, 0))
    full3 = lambda shape: pl.BlockSpec(shape, lambda i: (0, 0, 0))
    return pl.pallas_call(
        body, name="merge", grid=(nsteps,),
        in_specs=[row(D_MODEL), row(MLA_WIDTH), row(D_MODEL, 0), row(D_MODEL, 1), row(MLA_WIDTH, 4), row(SGU_WIDTH),
                  row(D_MODEL), full((MLA_WIDTH, D_MODEL)), full((SGU_WIDTH, D_MODEL)), full((D_MODEL, D_MODEL)),
                  full((1, D_MODEL)), full((1, D_MODEL))],
        out_specs=[full((1, LANES)), row(D_MODEL), row(SEG_A), row(MLA_WIDTH), row(SGU_WIDTH),
                   full3((N_DEV, MLA_WIDTH, D_MODEL // N_DEV)), full3((N_DEV, SGU_WIDTH, D_MODEL // N_DEV)),
                   full3((N_DEV, D_MODEL // N_DEV, D_MODEL)), full((1, D_MODEL)), full((1, D_MODEL)), full((1, SEG_A))],
        out_shape=[jax.ShapeDtypeStruct((1, LANES), F32),
                   jax.ShapeDtypeStruct((SEQ, D_MODEL), F32), jax.ShapeDtypeStruct((SEQ, SEG_A), BF16),
                   jax.ShapeDtypeStruct((SEQ, MLA_WIDTH), F32), jax.ShapeDtypeStruct((SEQ, SGU_WIDTH), F32),
                   jax.ShapeDtypeStruct((N_DEV, MLA_WIDTH, D_MODEL // N_DEV), BF16),
                   jax.ShapeDtypeStruct((N_DEV, SGU_WIDTH, D_MODEL // N_DEV), BF16),
                   jax.ShapeDtypeStruct((N_DEV, D_MODEL // N_DEV, D_MODEL), BF16),
                   jax.ShapeDtypeStruct((1, D_MODEL), F32), jax.ShapeDtypeStruct((1, D_MODEL), F32),
                   jax.ShapeDtypeStruct((1, SEG_A), F32)],
        scratch_shapes=[pltpu.VMEM((MLA_WIDTH, D_MODEL), F32), pltpu.VMEM((SGU_WIDTH, D_MODEL), F32),
                        pltpu.VMEM((D_MODEL, D_MODEL), F32)],
        compiler_params=pltpu.CompilerParams(dimension_semantics=("arbitrary",), vmem_limit_bytes=VMEM_BIG),
    )(x, o, h_a, h_a, h_a, y_b, target, w_oa, w_ob, w_out, ln_g, ln_b)


def _mla_bwd(dq, dk, dv, h_c, xt_bf, gq, gkv, wq, wkn, wv, c_t, sa_t, sb_t, parts):
    tm = 256
    hw = MLA_HEADS * HEAD_PAD
    npart = len(parts)
    nsteps = SEQ // tm

    def body(dq_ref, dk_ref, dv_ref, cq_ref, ckv_ref, xt_ref, gq_ref, gkv_ref, wq_ref, wkn_ref, wv_ref, c_ref, sa_ref,
             sb_ref, *rest):
        part_refs, rest = rest[:npart], rest[npart:]
        dhc_ref, puq_ref, dwkn_ref, dwv_ref, dgq_ref, dgkv_ref, dbc_ref, dwc_ref = rest[:8]
        land_refs, (pre_ref, dwq_ref, dwc_acc, send_sems, recv_sems, local_sems) = rest[8:8 + npart], rest[8 + npart:]
        exchange = _exchange_parts(part_refs, land_refs, send_sems, recv_sems, local_sems)
        _exchange_start(pl.program_id(0) == 0, exchange)
        _exchange_finish(pl.program_id(0) == nsteps - 1, exchange)

        @pl.when(pl.program_id(0) == 0)
        def _():
            for r in (dwq_ref, dwc_acc, dwkn_ref, dwv_ref, dgq_ref, dgkv_ref, dbc_ref):
                r[...] = jnp.zeros_like(r)

        c, sa, sb = c_ref[...], sa_ref[...], sb_ref[...]
        lane = lax.broadcasted_iota(jnp.int32, (tm, LANES), 1)
        rope_lanes = jnp.logical_and(lane >= ROPE_LO, lane < ROPE_HI)

        cq = cq_ref[...]
        gq = gq_ref[...]
        rq = lax.rsqrt(jnp.sum(cq * cq, axis=1, keepdims=True) * (1.0 / Q_LORA_RANK) + RMS_EPS)
        nq = cq * rq
        cqn_bf = (nq * gq).astype(BF16)
        for h in range(MLA_HEADS):
            sl = slice(HEAD_PAD * h, HEAD_PAD * (h + 1))
            pre_ref[:, sl] = _rope_t(dq_ref[:, sl] * ATTN_SCALE, c, sa, sb).astype(BF16)
        dqpre_bf = pre_ref[...]
        dcqn = _dot(dqpre_bf, wq_ref[...], _NT)
        dwq_ref[...] += _dot(cqn_bf, dqpre_bf, _TN)
        dgq_ref[...] += jnp.sum(dcqn * nq, axis=0, keepdims=True)
        dnq = dcqn * gq
        _store_grad(dhc_ref, dbc_ref, 0,
                    rq * (dnq - nq * (jnp.sum(dnq * nq, axis=1, keepdims=True) * (1.0 / Q_LORA_RANK))))

        ckv = ckv_ref[...]
        gkv = gkv_ref[...]
        rkv = lax.rsqrt(jnp.sum(ckv * ckv, axis=1, keepdims=True) * (1.0 / KV_LORA_RANK) + RMS_EPS)
        nkv = ckv * rkv
        ckvn_bf = (nkv * gkv).astype(BF16)
        dk = dk_ref[...]
        dk_bf = dk.astype(BF16)
        dv_bf = dv_ref[...].astype(BF16)
        dckvn = _dot(dk_bf, wkn_ref[...], _NT) + _dot(dv_bf, wv_ref[...], _NT)
        dwkn_ref[...] += _dot(ckvn_bf, dk_bf, _TN)
        dwv_ref[...] += _dot(ckvn_bf, dv_bf, _TN)
        dgkv_ref[...] += jnp.sum(dckvn * nkv, axis=0, keepdims=True)
        dnkv = dckvn * gkv
        _store_grad(dhc_ref, dbc_ref, CQ_PAD, rkv * (
            dnkv - nkv * (jnp.sum(dnkv * nkv, axis=1, keepdims=True) * (1.0 / KV_LORA_RANK))))
        dkpe = jnp.zeros((tm, LANES), F32)
        for h in range(MLA_HEADS):
            dkpe = dkpe + dk[:, HEAD_PAD * h:HEAD_PAD * (h + 1)]
        _store_grad(dhc_ref, dbc_ref, CQ_PAD + LANES, _rope_t(jnp.where(rope_lanes, dkpe, 0.0), c, sa, sb))
        dwc_acc[...] += _dot(xt_ref[...], dhc_ref[...], _NN)

        @pl.when(pl.program_id(0) == SEQ // tm - 1)
        def _():
            dwc_ref[...] = dwc_acc[...].astype(BF16)
            rows = Q_LORA_RANK // N_DEV
            for j in range(N_DEV):
                for h in range(MLA_HEADS):
                    puq_ref[j, :, QK_HEAD_DIM * h:QK_HEAD_DIM * (h + 1)] = dwq_ref[
                        rows * j:rows * (j + 1), HEAD_PAD * h:HEAD_PAD * h + QK_HEAD_DIM].astype(BF16)

    full = lambda shape: pl.BlockSpec(shape, lambda i: (0, 0))
    row = lambda w, c=0: pl.BlockSpec((tm, w), lambda i, c=c: (i, c))
    hbm = pl.BlockSpec(memory_space=pl.ANY)
    res = pl.pallas_call(
        body, name="mla_bwd", grid=(nsteps,),
        in_specs=[row(hw), row(hw), row(hw), row(CQ_PAD, 0), row(LANES, CQ_PAD // LANES),
                  pl.BlockSpec((D_MODEL, tm), lambda i: (0, i)),
                  full((1, CQ_PAD)), full((1, KV_LORA_RANK)), full((CQ_PAD, hw)), full((KV_LORA_RANK, hw)),
                  full((KV_LORA_RANK, hw)), row(LANES), row(LANES), row(LANES)] + [hbm] * npart,
        out_specs=[row(SEG_C), pl.BlockSpec((N_DEV, Q_LORA_RANK // N_DEV, MLA_HEADS * QK_HEAD_DIM), lambda i: (0, 0, 0)),
                   full((KV_LORA_RANK, hw)), full((KV_LORA_RANK, hw)),
                   full((1, CQ_PAD)), full((1, KV_LORA_RANK)), full((1, SEG_C)), full((D_MODEL, SEG_C))] + [hbm] * npart,
        out_shape=[jax.ShapeDtypeStruct((SEQ, SEG_C), BF16),
                   jax.ShapeDtypeStruct((N_DEV, Q_LORA_RANK // N_DEV, MLA_HEADS * QK_HEAD_DIM), BF16),
                   jax.ShapeDtypeStruct((KV_LORA_RANK, hw), F32), jax.ShapeDtypeStruct((KV_LORA_RANK, hw), F32),
                   jax.ShapeDtypeStruct((1, CQ_PAD), F32), jax.ShapeDtypeStruct((1, KV_LORA_RANK), F32),
                   jax.ShapeDtypeStruct((1, SEG_C), F32), jax.ShapeDtypeStruct((D_MODEL, SEG_C), BF16)]
        + [jax.ShapeDtypeStruct(p.shape, p.dtype) for p in parts],
        scratch_shapes=[pltpu.VMEM((tm, hw), BF16), pltpu.VMEM((CQ_PAD, hw), F32), pltpu.VMEM((D_MODEL, SEG_C), F32)]
        + _exchange_sems(npart),
        compiler_params=pltpu.CompilerParams(dimension_semantics=("arbitrary",), vmem_limit_bytes=VMEM_MID),
    )(dq, dk, dv, h_c, h_c, xt_bf, gq, gkv, wq, wkn, wv, c_t, sa_t, sb_t, *parts)
    return res[:8], res[8:]


def _adamw_all(ws, gs, ms, vs):
    n = len(ws)
    c1 = 1.0 / (1.0 - ADAM_B1 ** ADAM_STEP)
    c2 = 1.0 / (1.0 - ADAM_B2 ** ADAM_STEP)

    def body(*refs):
        for idx in range(n):
            w, g, m, v = (refs[idx][...], refs[n + idx][...], refs[2 * n + idx][...], refs[3 * n + idx][...])
            m_new = ADAM_B1 * m + (1.0 - ADAM_B1) * g
            v_new = ADAM_B2 * v + (1.0 - ADAM_B2) * (g * g)
            delta = -ADAM_LR * ((m_new * c1) / (jnp.sqrt(v_new * c2) + ADAM_EPS) + ADAM_WD * w)
            refs[4 * n + idx][...] = delta
            refs[5 * n + idx][...] = m_new
            refs[6 * n + idx][...] = v_new

    shapes = [jax.ShapeDtypeStruct(w.shape, F32) for w in ws]
    outs = pl.pallas_call(
        body, name="adamw", out_shape=shapes * 3,
        compiler_params=pltpu.CompilerParams(vmem_limit_bytes=VMEM_BIG),
    )(*ws, *gs, *ms, *vs)
    return outs[:n], outs[n:2 * n], outs[2 * n:]


SHARD_W = IN_WIDTH // N_DEV

_PIECES = [(0, 384, 2, 0), (384, 512, 2, CQ_PAD), (512, 544, 2, CQ_PAD + LANES + ROPE_LO),
           (544, 1056, 0, 2 * D_MODEL), (1056, 1568, 1, 0), (1568, 2080, 1, SGU_WIDTH),
           (2080, 2592, 1, 2 * SGU_WIDTH), (2592, 3616, 0, 0), (3616, 4640, 0, D_MODEL)]


def _column_runs():
    runs = []
    for n0, n1, seg, d0 in _PIECES:
        for j in range(N_DEV):
            lo, hi = max(n0, j * SHARD_W), min(n1, (j + 1) * SHARD_W)
            if lo < hi:
                runs.append((j, lo - j * SHARD_W, hi - j * SHARD_W, seg, d0 + lo - n0))
    return runs


def _mesh_pos():
    return lax.axis_index("x"), lax.axis_index("y"), lax.axis_index("c")


def _remote(src, dst, send_sems, recv_sems, k, to):
    return pltpu.make_async_remote_copy(src_ref=src, dst_ref=dst, send_sem=send_sems.at[k], recv_sem=recv_sems.at[k],
                                        device_id=to, device_id_type=pl.DeviceIdType.MESH)


def _gather_exchange(gats, send_sems, recv_sems, meanwhile=None):
    x, y, c = _mesh_pos()
    me, sibling = (x, y, c), (x, y, 1 - c)
    chips = [(1 - x, y), (x, 1 - y), (1 - x, 1 - y)]

    def copy(a, k, blk, to):
        slab = gats[a].at[4 * blk[0] + 2 * blk[1] + blk[2]]
        return _remote(slab, slab, send_sems, recv_sems, 7 * a + k, to)

    arrays = range(len(gats))
    first = [copy(a, 1 + j, me, (*chip, c)) for j, chip in enumerate(chips) for a in arrays]
    first += [copy(a, 0, me, sibling) for a in arrays]
    for cp in first:
        cp.start()
    if meanwhile is not None:
        meanwhile()
    passed = []
    for j, chip in enumerate(chips):
        for a in arrays:
            copy(a, 1 + j, (*chip, c), me).wait_recv()
            fwd = copy(a, 4 + j, (*chip, c), sibling)
            fwd.start()
            passed.append(fwd)
    for a in arrays:
        copy(a, 0, sibling, me).wait_recv()
    for j, chip in enumerate(chips):
        for a in arrays:
            copy(a, 4 + j, (*chip, 1 - c), me).wait_recv()
    for cp in first + passed:
        cp.wait_send()


def _gather_behind(own, gats, send_sems, recv_sems, local_sems, step, mid, last):
    x, y, c = _mesh_pos()
    me, sibling = (x, y, c), (x, y, 1 - c)
    chips = [(1 - x, y), (x, 1 - y), (1 - x, 1 - y)]
    arrays = range(len(gats))

    def copy(a, k, blk, to, src=None):
        slab = gats[a].at[4 * blk[0] + 2 * blk[1] + blk[2]]
        return _remote(slab if src is None else src, slab, send_sems, recv_sems, 7 * a + k, to)

    first = [copy(a, 1 + j, me, (*chip, c), src=own[a]) for j, chip in enumerate(chips) for a in arrays]
    first += [copy(a, 0, me, sibling, src=own[a]) for a in arrays]
    local = [pltpu.make_async_copy(own[a], gats[a].at[4 * x + 2 * y + c], local_sems.at[a]) for a in arrays]
    passed = [copy(a, 4 + j, (*chip, c), sibling) for j, chip in enumerate(chips) for a in arrays]

    @pl.when(step == 0)
    def _():
        for cp in first + local:
            cp.start()

    @pl.when(step == mid)
    def _():
        for j, chip in enumerate(chips):
            for a in arrays:
                copy(a, 1 + j, (*chip, c), me).wait_recv()
        for cp in passed:
            cp.start()

    @pl.when(step == last)
    def _():
        for a in arrays:
            copy(a, 0, sibling, me).wait_recv()
        for j, chip in enumerate(chips):
            for a in arrays:
                copy(a, 4 + j, (*chip, 1 - c), me).wait_recv()
        for cp in first + passed:
            cp.wait_send()
        for cp in local:
            cp.wait()


def _gather_first(w_in, w_uq2, w_oa, w_ob, w_out, x2, pos_col, invf_lane):
    hw = MLA_HEADS * HEAD_PAD
    uq_rows = Q_LORA_RANK // N_DEV
    rows = 256

    def body(win_ref, wuq_ref, woa_ref, wob_ref, wout_ref, x_ref, pos_ref, invf_ref,
             wc_ref, wq_ref, winb_ref, oab_ref, obb_ref, outb_ref, xb_ref, xt_ref, c_ref, sa_ref, sb_ref,
             g_uq, blk0, send_sems, recv_sems):
        def local_work():
            for i in range(SEQ // rows):
                xi = x_ref[rows * i:rows * (i + 1), :]
                xb_ref[rows * i:rows * (i + 1), :] = xi.astype(BF16)
                xt_ref[:, rows * i:rows * (i + 1)] = xi.T.astype(BF16)
            ang = pos_ref[...].astype(F32) * invf_ref[...]
            cs, sn = jnp.cos(ang), jnp.sin(ang)
            lane = lax.broadcasted_iota(jnp.int32, ang.shape, 1)
            c_ref[...] = jnp.where(lane < ROPE_LO, 1.0, jnp.where(lane < ROPE_HI, cs, 0.0))
            sa_ref[...] = jnp.where(jnp.logical_and(lane >= ROPE_LO, lane < ROPE_MID), -sn, 0.0)
            sb_ref[...] = jnp.where(jnp.logical_and(lane >= ROPE_MID, lane < ROPE_HI), sn, 0.0)

        x, y, c = _mesh_pos()
        me = (x, y, c)
        winb_ref[...] = win_ref[0].astype(BF16)
        oab_ref[...] = woa_ref[0].astype(BF16)
        obb_ref[...] = wob_ref[0].astype(BF16)
        outb_ref[...] = wout_ref[0].astype(BF16)
        g_uq[4 * x + 2 * y + c] = wuq_ref[...].astype(BF16)

        chip0 = jnp.logical_and(x == 0, y == 0)
        south = c == 0
        half = D_MODEL // 2
        halves = [blk0.at[pl.ds(0, half)], blk0.at[pl.ds(half, half)]]

        def bcopy(k, to, part=None):
            ref = blk0 if part is None else halves[part]
            return _remote(ref, ref, send_sems, recv_sems, 7 + k, to)

        sends0 = [(0, (0, 0, 1), None), (1, (1, 0, 0), 0), (2, (0, 1, 0), 1), (3, (1, 0, 0), 1), (4, (0, 1, 0), 0)]

        @pl.when(jnp.logical_and(chip0, south))
        def _():
            blk0[...] = winb_ref[...]
            for k, to, part in sends0:
                bcopy(k, to, part).start()

        _gather_exchange([g_uq], send_sems, recv_sems, meanwhile=local_work)

        for (cx, cy), first_k, first_half, second_k in (((1, 0), 1, 0, 3), ((0, 1), 2, 1, 4)):
            @pl.when(jnp.logical_and(jnp.logical_and(x == cx, y == cy), south))
            def _(cx=cx, cy=cy, first_k=first_k, first_half=first_half, second_k=second_k):
                bcopy(first_k, me, first_half).wait_recv()
                onward = bcopy(5 + first_half, (1, 1, 0), first_half)
                onward.start()
                bcopy(second_k, me, 1 - first_half).wait_recv()
                north = bcopy(7, (cx, cy, 1))
                north.start()
                onward.wait_send()
                north.wait_send()

        @pl.when(jnp.logical_and(jnp.logical_and(x == 1, y == 1), south))
        def _():
            bcopy(5, me, 0).wait_recv()
            bcopy(6, me, 1).wait_recv()
            north = bcopy(7, (1, 1, 1))
            north.start()
            north.wait_send()

        @pl.when(jnp.logical_and(chip0, c == 1))
        def _():
            bcopy(0, me).wait_recv()

        @pl.when(jnp.logical_and(jnp.logical_not(chip0), c == 1))
        def _():
            bcopy(7, me).wait_recv()

        @pl.when(jnp.logical_and(chip0, south))
        def _():
            for k, to, part in sends0:
                bcopy(k, to, part).wait_send()

        for j, s0, s1, seg, d0 in _column_runs():
            if seg == 2:
                wc_ref[:, d0:d0 + (s1 - s0)] = blk0[:, s0:s1]
        zeros = lambda r, w: jnp.zeros((r, w), BF16)
        wc_ref[:, Q_LORA_RANK:CQ_PAD] = zeros(D_MODEL, CQ_PAD - Q_LORA_RANK)
        wc_ref[:, CQ_PAD + LANES:CQ_PAD + LANES + ROPE_LO] = zeros(D_MODEL, ROPE_LO)
        wc_ref[:, CQ_PAD + LANES + ROPE_HI:SEG_C] = zeros(D_MODEL, LANES - ROPE_HI)
        wq_ref[Q_LORA_RANK:CQ_PAD, :] = zeros(CQ_PAD - Q_LORA_RANK, hw)
        for h in range(MLA_HEADS):
            wq_ref[0:Q_LORA_RANK, HEAD_PAD * h + QK_HEAD_DIM:HEAD_PAD * (h + 1)] = zeros(Q_LORA_RANK, HEAD_PAD - QK_HEAD_DIM)
        for j in range(N_DEV):
            for h in range(MLA_HEADS):
                wq_ref[uq_rows * j:uq_rows * (j + 1), HEAD_PAD * h:HEAD_PAD * h + QK_HEAD_DIM] = g_uq[
                    j, :, QK_HEAD_DIM * h:QK_HEAD_DIM * (h + 1)]

    vmem = pl.BlockSpec(memory_space=pltpu.VMEM)
    return pl.pallas_call(
        body, name="gather_first",
        out_shape=[jax.ShapeDtypeStruct((D_MODEL, SEG_C), BF16), jax.ShapeDtypeStruct((CQ_PAD, hw), BF16),
                   jax.ShapeDtypeStruct(w_in.shape[1:], BF16), jax.ShapeDtypeStruct(w_oa.shape[1:], BF16),
                   jax.ShapeDtypeStruct(w_ob.shape[1:], BF16), jax.ShapeDtypeStruct(w_out.shape[1:], BF16),
                   jax.ShapeDtypeStruct((SEQ, D_MODEL), BF16), jax.ShapeDtypeStruct((D_MODEL, SEQ), BF16)]
        + [jax.ShapeDtypeStruct((SEQ, LANES), F32)] * 3,
        in_specs=[vmem] * 8, out_specs=[vmem] * 11,
        scratch_shapes=[pltpu.VMEM((N_DEV, uq_rows, MLA_HEADS * QK_HEAD_DIM), BF16), pltpu.VMEM((D_MODEL, SHARD_W), BF16),
                        pltpu.SemaphoreType.DMA((15,)), pltpu.SemaphoreType.DMA((15,))],
        compiler_params=pltpu.CompilerParams(vmem_limit_bytes=VMEM_BIG),
    )(w_in, w_uq2, w_oa, w_ob, w_out, x2, pos_col, invf_lane)


def _assemble_in(g_in):
    def body(g_ref, wa_ref, wb_ref):
        segs = [wa_ref, wb_ref]
        for j, s0, s1, seg, d0 in _column_runs():
            if seg < 2:
                segs[seg][:, d0:d0 + (s1 - s0)] = g_ref[j, :, s0:s1]

    return pl.pallas_call(
        body, name="assemble_in",
        out_shape=[jax.ShapeDtypeStruct((D_MODEL, SEG_A), BF16), jax.ShapeDtypeStruct((D_MODEL, SEG_B), BF16)],
        compiler_params=pltpu.CompilerParams(vmem_limit_bytes=VMEM_MID),
    )(g_in)


def _assemble_out(g_oa, g_ob, g_out):
    cols = D_MODEL // N_DEV

    def body(goa_ref, gob_ref, gout_ref, oa_ref, ob_ref, out_ref):
        for j in range(N_DEV):
            oa_ref[:, cols * j:cols * (j + 1)] = goa_ref[j]
            ob_ref[:, cols * j:cols * (j + 1)] = gob_ref[j]
            out_ref[cols * j:cols * (j + 1), :] = gout_ref[j]

    return pl.pallas_call(
        body, name="assemble_out",
        out_shape=[jax.ShapeDtypeStruct((MLA_WIDTH, D_MODEL), BF16), jax.ShapeDtypeStruct((SGU_WIDTH, D_MODEL), BF16),
                   jax.ShapeDtypeStruct((D_MODEL, D_MODEL), BF16)],
    )(g_oa, g_ob, g_out)


C_NAT = 544


P_IN_SPLIT = 896


def _to_parts(dwa, dwb):
    def body(dwa_ref, dwb_ref, phi_ref, plo_ref):
        phi_ref[0, :, 0:C_NAT] = jnp.zeros((P_IN_SPLIT, C_NAT), BF16)
        plo_ref[0, :, 0:C_NAT] = jnp.zeros((D_MODEL - P_IN_SPLIT, C_NAT), BF16)
        segs = [dwa_ref, dwb_ref]
        for j, s0, s1, seg, d0 in _column_runs():
            if seg < 2:
                phi_ref[j, :, s0:s1] = segs[seg][0:P_IN_SPLIT, d0:d0 + (s1 - s0)]
                plo_ref[j, :, s0:s1] = segs[seg][P_IN_SPLIT:D_MODEL, d0:d0 + (s1 - s0)]

    return pl.pallas_call(
        body, name="to_parts",
        out_shape=[jax.ShapeDtypeStruct((N_DEV, P_IN_SPLIT, SHARD_W), BF16),
                   jax.ShapeDtypeStruct((N_DEV, D_MODEL - P_IN_SPLIT, SHARD_W), BF16)],
        compiler_params=pltpu.CompilerParams(vmem_limit_bytes=VMEM_MID))(dwa, dwb)


def _dx_tail(dhs, ws, dx_res, dwc, p_uq, p_rep):
    tm = SEQ // 4
    rep_rows = p_rep.shape[1]
    c_rows = D_MODEL // N_DEV
    spec = [((c_rows, C_NAT), BF16), (p_uq.shape[1:], BF16), ((rep_rows, LANES), F32)]
    n = len(spec)

    nseg = len(dhs)

    def body(*refs):
        dh_refs, w_refs = refs[:nseg], refs[nseg:2 * nseg]
        dxr_ref, dwc_ref, puq_ref, prep_ref, dx_ref, call_ref, guq_ref, repall_ref, pc_ref, c_all, rep_all = refs[
            2 * nseg:2 * nseg + 11]
        rest = refs[2 * nseg + 11:]
        ras, tbs, rbs = rest[0:n], rest[n:2 * n], rest[2 * n:3 * n]
        send_sems, recv_sems, gsend, grecv = rest[3 * n:]
        step = pl.program_id(0)
        x, y, c = _mesh_pos()
        me_idx = 4 * x + 2 * y + c
        me, sibling = (x, y, c), (x, y, 1 - c)
        others = [(1 - x, y), (x, 1 - y), (1 - x, 1 - y)]
        parts = [pc_ref, puq_ref, prep_ref]
        gats = [rep_all, c_all]

        def stage1(chip, a):
            return _remote(parts[a].at[2 * chip + (1 - c)], ras[a].at[chip], send_sems, recv_sems, 7 * a + chip, sibling)

        def stage2(k, a):
            cx, cy = others[k]
            return _remote(tbs[a].at[k], rbs[a].at[k], send_sems, recv_sems, 7 * a + 4 + k, (cx, cy, c))

        def gcopy(a, k, blk, to):
            slab = gats[a].at[4 * blk[0] + 2 * blk[1] + blk[2]]
            return _remote(slab, slab, gsend, grecv, 7 * a + k, to)

        def chip_sum(a, chip):
            return parts[a][2 * chip + c].astype(F32) + ras[a][chip].astype(F32)

        @pl.when(step == 0)
        def _():
            for j, s0, s1, seg, d0 in _column_runs():
                if seg == 2:
                    for r in range(N_DEV):
                        pc_ref[r, :, s0:s1] = dwc_ref[c_rows * r:c_rows * (r + 1), d0:d0 + (s1 - s0)]
            for chip in range(4):
                for a in range(n):
                    stage1(chip, a).start()

        @pl.when(step == 1)
        def _():
            for chip in range(4):
                for a in range(n):
                    stage1(chip, a).wait_recv()
            for k, (cx, cy) in enumerate(others):
                for a in range(n):
                    tbs[a][k] = chip_sum(a, 2 * cx + cy).astype(spec[a][1])
                    stage2(k, a).start()

        @pl.when(step == 2)
        def _():
            for k in range(3):
                for a in range(n):
                    stage2(k, a).wait_recv()
            sums = []
            for a in range(n):
                acc = chip_sum(a, 2 * x + y)
                for k in range(3):
                    acc = acc + rbs[a][k].astype(F32)
                sums.append(acc)
            c_all[me_idx] = sums[0].astype(BF16)
            guq_ref[...] = sums[1]
            rep_all[me_idx] = sums[2]
            for a in range(2):
                for j, chip in enumerate(others):
                    gcopy(a, 1 + j, me, (*chip, c)).start()
                gcopy(a, 0, me, sibling).start()

        @pl.when(step == 3)
        def _():
            for j, chip in enumerate(others):
                for a in range(2):
                    gcopy(a, 1 + j, (*chip, c), me).wait_recv()
                    gcopy(a, 4 + j, (*chip, c), sibling).start()
            for a in range(2):
                gcopy(a, 0, sibling, me).wait_recv()
                for j, chip in enumerate(others):
                    gcopy(a, 4 + j, (*chip, 1 - c), me).wait_recv()
            for a in range(2):
                gcopy(a, 0, me, sibling).wait_send()
                for j, chip in enumerate(others):
                    gcopy(a, 1 + j, me, (*chip, c)).wait_send()
                    gcopy(a, 4 + j, (*chip, c), sibling).wait_send()
            for a in range(n):
                for chip in range(4):
                    stage1(chip, a).wait_send()
                for k in range(3):
                    stage2(k, a).wait_send()
            call_ref[...] = c_all[...]
            repall_ref[...] = rep_all[...]

        acc = dxr_ref[...]
        for dh_ref, w_ref in zip(dh_refs, w_refs):
            acc = acc + _dot(dh_ref[...], w_ref[...], _NT)
        dx_ref[...] = acc

    row = lambda w: pl.BlockSpec((tm, w), lambda i: (i, 0))
    full = lambda shape: pl.BlockSpec(shape, lambda i: (0,) * len(shape))
    scratch = [pltpu.VMEM((N_DEV, c_rows, C_NAT), BF16), pltpu.VMEM((N_DEV, c_rows, C_NAT), BF16),
               pltpu.VMEM((N_DEV, rep_rows, LANES), F32)]
    for lead in (4, 3, 3):
        scratch += [pltpu.VMEM((lead,) + tuple(shape), dt) for shape, dt in spec]
    scratch += [pltpu.SemaphoreType.DMA((7 * n,)), pltpu.SemaphoreType.DMA((7 * n,)),
                pltpu.SemaphoreType.DMA((14,)), pltpu.SemaphoreType.DMA((14,))]
    return pl.pallas_call(
        body, name="dx_tail", grid=(SEQ // tm,),
        in_specs=[row(dh.shape[1]) for dh in dhs] + [full(w.shape) for w in ws]
        + [row(D_MODEL), full(dwc.shape), full(p_uq.shape), full(p_rep.shape)],
        out_specs=[row(D_MODEL), full((N_DEV, c_rows, C_NAT)), full(p_uq.shape[1:]), full((N_DEV, rep_rows, LANES))],
        out_shape=[jax.ShapeDtypeStruct((SEQ, D_MODEL), F32), jax.ShapeDtypeStruct((N_DEV, c_rows, C_NAT), BF16),
                   jax.ShapeDtypeStruct(p_uq.shape[1:], F32), jax.ShapeDtypeStruct((N_DEV, rep_rows, LANES), F32)],
        scratch_shapes=scratch,
        compiler_params=pltpu.CompilerParams(dimension_semantics=("arbitrary",), vmem_limit_bytes=VMEM_BIG),
    )(*dhs, *ws, dx_res, dwc, p_uq, p_rep)


def _sum_landed(landed, c_all):
    c_rows = D_MODEL // N_DEV

    def body(rhi_ref, rlo_ref, roa_ref, rob_ref, rout_ref, call_ref, gin_ref, goa_ref, gob_ref, gout_ref):
        def total(ref, sl):
            acc = ref[0, sl, :].astype(F32)
            for s in range(1, N_DEV):
                acc = acc + ref[s, sl, :].astype(F32)
            return acc

        x, y, c = _mesh_pos()
        dev0 = jnp.where(4 * x + 2 * y + c == 0, 1.0, 0.0)
        for j in range(N_DEV):
            sl = slice(c_rows * j, c_rows * (j + 1))
            below = c_rows * j < P_IN_SPLIT
            tot = total(rhi_ref, sl) if below else total(rlo_ref, slice(c_rows * j - P_IN_SPLIT, c_rows * (j + 1) - P_IN_SPLIT))
            gin_ref[0, sl, C_NAT:SHARD_W] = tot[:, C_NAT:SHARD_W]
            gin_ref[0, sl, 0:C_NAT] = tot[:, 0:C_NAT] + dev0 * call_ref[j].astype(F32)
        goa_ref[0] = total(roa_ref, slice(None))
        gob_ref[0] = total(rob_ref, slice(None))
        gout_ref[0] = total(rout_ref, slice(None))

    return pl.pallas_call(
        body, name="sum_landed",
        out_shape=[jax.ShapeDtypeStruct((1, D_MODEL, SHARD_W), F32)]
        + [jax.ShapeDtypeStruct((1,) + r.shape[1:], F32) for r in landed[2:]],
        compiler_params=pltpu.CompilerParams(vmem_limit_bytes=VMEM_MID),
    )(*landed, c_all)


_O_CQ, _O_CKV, _O_KPE, _O_ZA, _O_U, _O_V, _O_ZB, _O_GA, _O_GB = 0, 384, 512, 544, 1056, 1568, 2080, 2592, 3616


def _to_segments(w):
    z = lambda n: jnp.zeros(w.shape[:-1] + (n,), w.dtype)
    seg_a = jnp.concatenate([w[..., _O_GA:_O_GB], w[..., _O_GB:IN_WIDTH], w[..., _O_ZA:_O_U]], axis=-1)
    seg_b = jnp.concatenate([w[..., _O_U:_O_V], w[..., _O_V:_O_ZB], w[..., _O_ZB:_O_GA]], axis=-1)
    seg_c = jnp.concatenate([w[..., _O_CQ:_O_CKV], z(CQ_PAD - Q_LORA_RANK), w[..., _O_CKV:_O_KPE],
                             z(ROPE_LO), w[..., _O_KPE:_O_ZA], z(LANES - ROPE_HI)], axis=-1)
    return seg_a, seg_b, seg_c


def _from_segments(seg_a, seg_b, seg_c):
    kpe0 = CQ_PAD + LANES + ROPE_LO
    return jnp.concatenate([
        seg_c[..., 0:Q_LORA_RANK], seg_c[..., CQ_PAD:CQ_PAD + LANES], seg_c[..., kpe0:kpe0 + QK_ROPE_DIM],
        seg_a[..., 2 * D_MODEL:SEG_A], seg_b, seg_a[..., 0:2 * D_MODEL]], axis=-1)


def kernel(x, positions, w_in, b_in, g_q, w_uq, g_kv, w_ukv, w_oa, sgu_ln_g, sgu_ln_b, w_s, b_s, w_ob, w_out, ln_g, ln_b, loss_target, m_w_in, m_b_in, m_g_q, m_w_uq, m_g_kv, m_w_ukv, m_w_oa, m_sgu_ln_g, m_sgu_ln_b, m_w_s, m_b_s, m_w_ob, m_w_out, m_ln_g, m_ln_b, v_w_in, v_b_in, v_g_q, v_w_uq, v_g_kv, v_w_ukv, v_w_oa, v_sgu_ln_g, v_sgu_ln_b, v_w_s, v_b_s, v_w_ob, v_w_out, v_ln_g, v_ln_b):
    w_uq2 = w_uq[0].reshape(Q_LORA_RANK // N_DEV, MLA_HEADS * QK_HEAD_DIM)
    inv_freq = ROPE_THETA ** (-jnp.arange(0, QK_ROPE_DIM, 2, dtype=F32) / QK_ROPE_DIM)
    invf_lane = jnp.concatenate([jnp.zeros((ROPE_LO,), F32), inv_freq, inv_freq,
                                 jnp.zeros((LANES - ROPE_HI,), F32)]).reshape(1, LANES)
    first = _gather_first(w_in, w_uq2, w_oa, w_ob, w_out, x[0], positions.reshape(SEQ, 1), invf_lane)
    partials = _local_step(x[0], loss_target[0], first, b_in, g_q, g_kv, w_ukv, sgu_ln_g, sgu_ln_b, w_s, b_s, ln_g, ln_b)
    weights = dict(w_in=w_in, b_in=b_in, g_q=g_q, w_uq=w_uq, g_kv=g_kv, w_ukv=w_ukv, w_oa=w_oa, sgu_ln_g=sgu_ln_g,
                   sgu_ln_b=sgu_ln_b, w_s=w_s, b_s=b_s, w_ob=w_ob, w_out=w_out, ln_g=ln_g, ln_b=ln_b)
    moms = dict(w_in=m_w_in, b_in=m_b_in, g_q=m_g_q, w_uq=m_w_uq, g_kv=m_g_kv, w_ukv=m_w_ukv, w_oa=m_w_oa,
                sgu_ln_g=m_sgu_ln_g, sgu_ln_b=m_sgu_ln_b, w_s=m_w_s, b_s=m_b_s, w_ob=m_w_ob, w_out=m_w_out,
                ln_g=m_ln_g, ln_b=m_ln_b)
    vars_ = dict(w_in=v_w_in, b_in=v_b_in, g_q=v_g_q, w_uq=v_w_uq, g_kv=v_g_kv, w_ukv=v_w_ukv, w_oa=v_w_oa,
                 sgu_ln_g=v_sgu_ln_g, sgu_ln_b=v_sgu_ln_b, w_s=v_w_s, b_s=v_b_s, w_ob=v_w_ob, w_out=v_w_out,
                 ln_g=v_ln_g, ln_b=v_ln_b)
    return _reduce_and_update(partials, weights, moms, vars_)


def _local_step(x2, tgt, first, b_in, g_q, g_kv, w_ukv, sgu_ln_g, sgu_ln_b, w_s, b_s, ln_g, ln_b):
    wc, wq, win_b, oa_b, ob_b, out_b, x_bf, xt_bf, c_t, sa_t, sb_t = first
    ba, bb, bc = _to_segments(b_in)
    w_ukv_bf = w_ukv[0].astype(BF16)
    wkn = jnp.pad(w_ukv_bf[:, :, :QK_NOPE_DIM], ((0, 0), (0, 0), (0, HEAD_PAD - QK_NOPE_DIM))).reshape(KV_LORA_RANK, -1)
    wv = jnp.pad(w_ukv_bf[:, :, QK_NOPE_DIM:], ((0, 0), (0, 0), (0, HEAD_PAD - V_HEAD_DIM))).reshape(KV_LORA_RANK, -1)
    gq = jnp.pad(g_q, ((0, 0), (0, CQ_PAD - Q_LORA_RANK)))
    bias_full = jnp.repeat(b_s[0].T, SGU_GROUP_DIM, axis=1)
    w_s3 = w_s[0]
    w_st3 = jnp.swapaxes(w_s3, 1, 2)

    h_c = _mm(x_bf, wc, bias=bc, tm=512, tn=SEG_C, name="in_proj_c")
    q, k, kt, vx, vxt = _mla_prep(h_c, gq, g_kv, wq, wkn, wv, c_t, sa_t, sb_t)
    o, lse, (g_in,) = _attn_fwd(q, kt, vx, (win_b,))
    wa, wb = _assemble_in(g_in)
    (h_a, h_b), (g_out, g_oa, g_ob) = _mm_comm(x_bf, (wa, wb), biases=(ba, bb), own=(out_b, oa_b, ob_b),
                                               tm=SEQ, tn=512, name="in_proj_ab")
    y_b = _sgu_fwd(h_b, sgu_ln_g, sgu_ln_b, w_s3, bias_full)
    w_oa_f, w_ob_f, w_out_f = _assemble_out(g_oa, g_ob, g_out)

    (loss_row, dx_res, dh_a, d_o, d_yb, p_oa, p_ob, p_out, d_lng, d_lnb, d_ba) = _merge(
        x2, o, h_a, y_b, tgt, w_oa_f, w_ob_f, w_out_f, ln_g, ln_b)
    (dh_b, d_ws, d_bs_t, d_slg, d_slb, d_bb), (r_out,) = _sgu_bwd(h_b, d_yb, sgu_ln_g, sgu_ln_b, w_s3, w_st3, bias_full,
                                                                 (p_out,))
    (d_wa,), (r_oa,) = _mm_comm(xt_bf, (dh_a,), out_dtype=BF16, parts=(p_oa,), tm=512, tn=512, name="dw_in_a")
    d_wb = _mm(xt_bf, dh_b, out_dtype=BF16, tm=512, tn=512, name="dw_in_b")
    p_hi, p_lo = _to_parts(d_wa, d_wb)
    dq, dk, dv, (r_hi,) = _attn_bwd(q, kt, k, vxt, d_o, o, lse, (p_hi,))
    (dh_c, p_uq, d_wkn, d_wv, d_gq, d_gkv, d_bc, d_wc), (r_lo, r_ob) = _mla_bwd(
        dq, dk, dv, h_c, xt_bf, gq, g_kv, wq, wkn, wv, c_t, sa_t, sb_t, (p_lo, p_ob))
    landed = (r_hi, r_lo, r_oa, r_ob, r_out)

    p_b_in = _from_segments(d_ba, d_bb, d_bc)
    p_w_ukv = jnp.concatenate([d_wkn.reshape(KV_LORA_RANK, MLA_HEADS, HEAD_PAD)[:, :, :QK_NOPE_DIM],
                               d_wv.reshape(KV_LORA_RANK, MLA_HEADS, HEAD_PAD)[:, :, :V_HEAD_DIM]], axis=-1)
    p_g_q = d_gq[:, :Q_LORA_RANK]
    p_b_s = d_bs_t[:, :SGU_GROUPS].T
    replicated = [p_b_in, p_g_q, d_gkv, p_w_ukv, d_slg, d_slb, d_ws, p_b_s, d_lng, d_lnb]
    return loss_row, ((dh_a, dh_b, dh_c), (wa, wb, wc), dx_res), landed, d_wc, p_uq, replicated


_NAMES = ["w_in", "b_in", "g_q", "w_uq", "g_kv", "w_ukv", "w_oa", "sgu_ln_g", "sgu_ln_b", "w_s", "b_s", "w_ob",
          "w_out", "ln_g", "ln_b"]
_REPLICATED = ["b_in", "g_q", "g_kv", "w_ukv", "sgu_ln_g", "sgu_ln_b", "w_s", "b_s", "ln_g", "ln_b"]


def _reduce_and_update(partials, weights, moms, vars_):
    loss_row, (dhs, ws, dx_res), landed, d_wc, p_uq, replicated = partials
    def piece(a):
        flat = a.reshape(-1)
        return jnp.pad(flat, (0, -flat.size % PACK_ALIGN))

    rep_flat = jnp.concatenate([piece(a) for a in replicated] + [piece(loss_row[0, :1])])
    rep_flat = jnp.pad(rep_flat, (0, N_DEV * PACK_R_ROWS * LANES - rep_flat.size))
    dx_ab, c_all, g_uq, rep_all = _dx_tail(dhs[:2], ws[:2], dx_res, d_wc, p_uq,
                                           rep_flat.reshape(N_DEV, PACK_R_ROWS, LANES))
    dx = _mm(dhs[2], ws[2], tb=True, add=dx_ab, tm=512, tn=D_MODEL, name="dx_c")
    g_in, g_oa, g_ob, g_out = _sum_landed(landed, c_all)
    rep_sum = rep_all.reshape(-1)
    grads, pos = dict(w_in=g_in, w_uq=g_uq, w_oa=g_oa, w_ob=g_ob, w_out=g_out), 0
    for nm in _REPLICATED:
        grads[nm] = rep_sum[pos:pos + weights[nm].size]
        pos += weights[nm].size + -weights[nm].size % PACK_ALIGN
    loss = rep_sum[pos]
    grads = {nm: grads[nm].reshape(weights[nm].shape) for nm in _NAMES}
    deltas, new_m, new_v = _adamw_all([weights[nm] for nm in _NAMES], [grads[nm] for nm in _NAMES],
                                      [moms[nm] for nm in _NAMES], [vars_[nm] for nm in _NAMES])
    return (loss, dx.reshape(1, SEQ, D_MODEL), *[grads[nm] for nm in _NAMES], *deltas, *new_m, *new_v)
```

```python
import math

import jax
import jax.numpy as jnp
from jax import lax
from jax.experimental import pallas as pl
from jax.experimental.pallas import tpu as pltpu

F32 = jnp.float32
BF16 = jnp.bfloat16

D_MODEL = 1024
SEQ = 2048
N_DEV = 8
MLA_HEADS = 8
Q_LORA_RANK = 384
KV_LORA_RANK = 128
QK_NOPE_DIM = 64
QK_ROPE_DIM = 32
V_HEAD_DIM = 64
QK_HEAD_DIM = QK_NOPE_DIM + QK_ROPE_DIM
MLA_WIDTH = MLA_HEADS * V_HEAD_DIM
ROPE_THETA = 10000.0
SGU_GROUPS = 8
SGU_GROUP_DIM = 64
SGU_WIDTH = SGU_GROUPS * SGU_GROUP_DIM
CHUNK = 128
RMS_EPS = 1e-6
LN_EPS = 1e-5
DN_ALPHA = 2.0 ** 0.25
IN_WIDTH = 4640
ATTN_SCALE = QK_HEAD_DIM ** -0.5

ADAM_LR = 0.001
ADAM_B1 = 0.9
ADAM_B2 = 0.999
ADAM_EPS = 1e-08
ADAM_WD = 0.01
ADAM_STEP = 10

LANES = 128
HEAD_PAD = 128
ROPE_LO = QK_NOPE_DIM
ROPE_MID = ROPE_LO + QK_ROPE_DIM // 2
ROPE_HI = ROPE_LO + QK_ROPE_DIM
CQ_PAD = 512

SEG_A = 2560
SEG_B = 1536
SEG_C = 768

PACK_R_ROWS = 272
PACK_ALIGN = 8 * LANES
VMEM_BIG = 56 * 1024 * 1024
VMEM_MID = 40 * 1024 * 1024


def _sigmoid(x):
    return 1.0 / (1.0 + jnp.exp(-x))


def _gelu_and_grad(x):
    c0 = math.sqrt(2.0 / math.pi)
    x2 = x * x
    t = jnp.tanh(c0 * (x + 0.044715 * x * x2))
    g = 0.5 * x * (1.0 + t)
    dg = 0.5 * (1.0 + t) + 0.5 * x * (1.0 - t * t) * (c0 * (1.0 + 3.0 * 0.044715 * x2))
    return g, dg


def _dot(a, b, dims):
    return lax.dot_general(a, b, (dims, ((), ())), preferred_element_type=F32)


_NN = ((1,), (0,))
_NT = ((1,), (1,))
_TN = ((0,), (0,))


def _store_grad(dh_ref, db_ref, col, val):
    cols = slice(col, col + val.shape[1])
    dh_ref[:, cols] = val.astype(BF16)
    db_ref[:, cols] += jnp.sum(val, axis=0, keepdims=True)


def _mm(a, b, *, tb=False, bias=None, add=None, out_dtype=F32, own=(), parts=(), tm, tn, name):
    m, k = a.shape
    n = b.shape[0] if tb else b.shape[1]
    assert m % tm == 0 and n % tn == 0 and not (own and parts)
    dims = _NT if tb else _NN
    nown = len(own) + len(parts)
    nm = m // tm
    nsteps = (n // tn) * nm

    def body(*refs):
        a_ref, b_ref = refs[0], refs[1]
        pos = 2
        r = _dot(a_ref[...], b_ref[...], dims)
        if bias is not None:
            r = r + refs[pos][...]; pos += 1
        if add is not None:
            r = r + refs[pos][...]; pos += 1
        own_refs = refs[pos:pos + nown]; pos += nown
        refs[pos][...] = r.astype(out_dtype)
        if nown:
            gat_refs = refs[pos + 1:pos + 1 + nown]
            send_sems, recv_sems, local_sems = refs[pos + 1 + nown:]
            step = pl.program_id(0) * nm + pl.program_id(1)
            if own:
                _gather_behind(own_refs, gat_refs, send_sems, recv_sems, local_sems, step, nsteps - 2, nsteps - 1)
            else:
                exchange = _exchange_parts(own_refs, gat_refs, send_sems, recv_sems, local_sems)
                _exchange_start(step == 0, exchange)
                _exchange_finish(step == nsteps - 1, exchange)

    b_spec = pl.BlockSpec((tn, k), lambda j, i: (j, 0)) if tb else pl.BlockSpec((k, tn), lambda j, i: (0, j))
    in_specs, args = [pl.BlockSpec((tm, k), lambda j, i: (i, 0)), b_spec], [a, b]
    if bias is not None:
        in_specs.append(pl.BlockSpec((1, tn), lambda j, i: (0, j))); args.append(bias)
    if add is not None:
        in_specs.append(pl.BlockSpec((tm, tn), lambda j, i: (i, j))); args.append(add)
    hbm = pl.BlockSpec(memory_space=pl.ANY)
    res = pl.pallas_call(
        body, name=name, grid=(n // tn, nm), in_specs=in_specs + [hbm] * nown,
        out_specs=[pl.BlockSpec((tm, tn), lambda j, i: (i, j))] + [hbm] * nown,
        out_shape=[jax.ShapeDtypeStruct((m, n), out_dtype)]
        + [jax.ShapeDtypeStruct((N_DEV,) + o.shape, o.dtype) for o in own]
        + [jax.ShapeDtypeStruct(p.shape, p.dtype) for p in parts],
        scratch_shapes=_exchange_sems(nown) if nown else [],
        compiler_params=pltpu.CompilerParams(dimension_semantics=("arbitrary", "arbitrary"), vmem_limit_bytes=VMEM_BIG),
    )(*args, *own, *parts)
    return (res[0], res[1:]) if nown else res[0]


def _rope(x, c, sa, sb):
    return x * c + pltpu.roll(x, LANES - 16, 1) * sa + pltpu.roll(x, 16, 1) * sb


def _rope_t(dy, c, sa, sb):
    return dy * c + pltpu.roll(dy * sa, 16, 1) + pltpu.roll(dy * sb, LANES - 16, 1)


def _mla_prep(h_c, gq, gkv, wq, wkn, wvx, c_t, sa_t, sb_t):
    tm = 256
    hw = MLA_HEADS * HEAD_PAD

    def body(cq_ref, ckv_ref, kpe_ref, gq_ref, gkv_ref, wq_ref, wkn_ref, wvx_ref, c_ref, sa_ref, sb_ref,
             q_ref, k_ref, kt_ref, vx_ref, vxt_ref):
        c, sa, sb = c_ref[...], sa_ref[...], sb_ref[...]
        cq = cq_ref[...]
        rq = lax.rsqrt(jnp.sum(cq * cq, axis=1, keepdims=True) * (1.0 / Q_LORA_RANK) + RMS_EPS)
        cqn = ((cq * rq) * gq_ref[...]).astype(BF16)
        qall = _dot(cqn, wq_ref[...], _NN)
        for h in range(MLA_HEADS):
            sl = slice(HEAD_PAD * h, HEAD_PAD * (h + 1))
            q_ref[:, sl] = (_rope(qall[:, sl], c, sa, sb) * ATTN_SCALE).astype(BF16)
        ckv = ckv_ref[...]
        rkv = lax.rsqrt(jnp.sum(ckv * ckv, axis=1, keepdims=True) * (1.0 / KV_LORA_RANK) + RMS_EPS)
        ckvn = ((ckv * rkv) * gkv_ref[...]).astype(BF16)
        knall = _dot(ckvn, wkn_ref[...], _NN)
        vall = _dot(ckvn, wvx_ref[...], _NN)
        kper = _rope(kpe_ref[...], c, sa, sb)
        ones_half = (lax.broadcasted_iota(jnp.int32, (tm, HEAD_PAD), 1) >= V_HEAD_DIM).astype(F32)
        for h in range(MLA_HEADS):
            sl = slice(HEAD_PAD * h, HEAD_PAD * (h + 1))
            kh = knall[:, sl] + kper
            vh = vall[:, sl] + ones_half
            k_ref[:, sl] = kh.astype(BF16)
            kt_ref[sl, :] = kh.T.astype(BF16)
            vx_ref[:, sl] = vh.astype(BF16)
            vxt_ref[sl, :] = vh.T.astype(BF16)

    full = lambda shape: pl.BlockSpec(shape, lambda i: (0, 0))
    tab = pl.BlockSpec((tm, LANES), lambda i: (i, 0))
    row = pl.BlockSpec((tm, hw), lambda i: (i, 0))
    col = pl.BlockSpec((hw, tm), lambda i: (0, i))
    return pl.pallas_call(
        body, name="mla_prep", grid=(SEQ // tm,),
        in_specs=[pl.BlockSpec((tm, CQ_PAD), lambda i: (i, 0)),
                  pl.BlockSpec((tm, LANES), lambda i: (i, CQ_PAD // LANES)),
                  pl.BlockSpec((tm, LANES), lambda i: (i, CQ_PAD // LANES + 1)),
                  full((1, CQ_PAD)), full((1, KV_LORA_RANK)),
                  full((CQ_PAD, hw)), full((KV_LORA_RANK, hw)), full((KV_LORA_RANK, hw)), tab, tab, tab],
        out_specs=[row, row, col, row, col],
        out_shape=[jax.ShapeDtypeStruct((SEQ, hw), BF16), jax.ShapeDtypeStruct((SEQ, hw), BF16),
                   jax.ShapeDtypeStruct((hw, SEQ), BF16), jax.ShapeDtypeStruct((SEQ, hw), BF16),
                   jax.ShapeDtypeStruct((hw, SEQ), BF16)],
        compiler_params=pltpu.CompilerParams(dimension_semantics=("arbitrary",), vmem_limit_bytes=VMEM_MID),
    )(h_c, h_c, h_c, gq, gkv, wq, wkn, wvx, c_t, sa_t, sb_t)


ATT_T = 512
ATT_STRIP = 64


def _attn_fwd(q, kt, vx, own):
    t, rs = ATT_T, ATT_STRIP
    nown = len(own)
    nq = SEQ // t
    nsteps = (MLA_HEADS // 2) * nq

    def body(q_ref, kt_ref, vx_ref, *rest):
        own_refs, (o_ref, l_ref), gat_refs = rest[:nown], rest[nown:nown + 2], rest[nown + 2:2 * nown + 2]
        s_scr, p_scr, m_scr, a_scr, acc_scr, send_sems, recv_sems, local_sems = rest[2 * nown + 2:]
        qi = pl.program_id(1)
        _gather_behind(own_refs, gat_refs, send_sems, recv_sems, local_sems, pl.program_id(0) * nq + qi,
                       nsteps - 2, nsteps - 1)
        lane = lax.broadcasted_iota(jnp.int32, (t, LANES), 1)
        m_scr[...] = jnp.full((2, t, LANES), -1e30, F32)
        acc_scr[...] = jnp.zeros((2, t, LANES), F32)

        def block(j, masked):
            off = pl.multiple_of(j * t, t)
            for a in range(2):
                sl = slice(HEAD_PAD * a, HEAD_PAD * (a + 1))
                s_scr[a] = _dot(q_ref[:, sl], kt_ref[sl, pl.ds(off, t)], _NN)
                for r in range(t // rs):
                    rows = slice(rs * r, rs * (r + 1))
                    s = s_scr[a, rows, :]
                    if masked:
                        rowi = lax.broadcasted_iota(jnp.int32, (rs, t), 0) + rs * r
                        coli = lax.broadcasted_iota(jnp.int32, (rs, t), 1)
                        s = jnp.where(coli <= rowi, s, -1e30)
                    m_old = m_scr[a, rows, :]
                    m_new = jnp.maximum(m_old, jnp.max(s, axis=1, keepdims=True))
                    p_scr[a, rows, :] = jnp.exp(s - m_new[:, :1]).astype(BF16)
                    a_scr[a, rows, :] = jnp.exp(m_old - m_new)
                    m_scr[a, rows, :] = m_new
                acc_scr[a] = acc_scr[a] * a_scr[a] + _dot(p_scr[a], vx_ref[pl.ds(off, t), sl], _NN)

        def step(j, carry):
            block(j, False)
            return carry
        lax.fori_loop(0, qi, step, 0)
        block(qi, True)
        res = []
        for a in range(2):
            acc = acc_scr[a]
            l = acc[:, V_HEAD_DIM:V_HEAD_DIM + 1]
            res.append((acc / l, m_scr[a] + jnp.log(l)))
        o_ref[...] = jnp.where(lane < V_HEAD_DIM, res[0][0], pltpu.roll(res[1][0], V_HEAD_DIM, 1))
        l_ref[...] = jnp.where(lane < V_HEAD_DIM, res[0][1], res[1][1])

    hbm = pl.BlockSpec(memory_space=pl.ANY)
    res = pl.pallas_call(
        body, name="attn_fwd", grid=(MLA_HEADS // 2, nq),
        in_specs=[pl.BlockSpec((t, 2 * HEAD_PAD), lambda p, i: (i, p)),
                  pl.BlockSpec((2 * HEAD_PAD, SEQ), lambda p, i: (p, 0)),
                  pl.BlockSpec((SEQ, 2 * HEAD_PAD), lambda p, i: (0, p))] + [hbm] * nown,
        out_specs=[pl.BlockSpec((t, LANES), lambda p, i: (i, p)),
                   pl.BlockSpec((t, LANES), lambda p, i: (i, p))] + [hbm] * nown,
        out_shape=[jax.ShapeDtypeStruct((SEQ, MLA_WIDTH), F32), jax.ShapeDtypeStruct((SEQ, MLA_WIDTH), F32)]
        + [jax.ShapeDtypeStruct((N_DEV,) + a.shape, a.dtype) for a in own],
        scratch_shapes=[pltpu.VMEM((2, t, t), F32), pltpu.VMEM((2, t, t), BF16), pltpu.VMEM((2, t, LANES), F32),
                        pltpu.VMEM((2, t, LANES), F32), pltpu.VMEM((2, t, LANES), F32)] + _exchange_sems(nown),
        compiler_params=pltpu.CompilerParams(dimension_semantics=("arbitrary", "arbitrary"), vmem_limit_bytes=VMEM_MID),
    )(q, kt, vx, *own)
    return res[0], res[1], res[2:]


def _exchange_parts(parts, lands, send_sems, recv_sems, local_sems):
    x, y, c = _mesh_pos()
    me = 4 * x + 2 * y + c
    peers = [(x, y, 1 - c), (1 - x, y, c), (x, 1 - y, c), (1 - x, 1 - y, c),
             (1 - x, y, 1 - c), (x, 1 - y, 1 - c), (1 - x, 1 - y, 1 - c)]
    remote, local = [], []
    for a, (part, land) in enumerate(zip(parts, lands)):
        for k, peer in enumerate(peers):
            t = 4 * peer[0] + 2 * peer[1] + peer[2]
            remote.append(_remote(part.at[t], land.at[me], send_sems, recv_sems, 7 * a + k, peer))
        local.append(pltpu.make_async_copy(part.at[me], land.at[me], local_sems.at[a]))
    return remote, local


def _exchange_start(first_step, exchange):
    remote, local = exchange

    @pl.when(first_step)
    def _():
        for cp in remote + local:
            cp.start()


def _exchange_finish(last_step, exchange):
    remote, local = exchange

    @pl.when(last_step)
    def _():
        for cp in remote:
            cp.wait_recv()
        for cp in remote:
            cp.wait_send()
        for cp in local:
            cp.wait()


def _exchange_sems(npart):
    return [pltpu.SemaphoreType.DMA((7 * npart,)), pltpu.SemaphoreType.DMA((7 * npart,)),
            pltpu.SemaphoreType.DMA((npart,))]


def _attn_bwd(q, kt, k, vxt, d_o, o, lse, parts):
    t, rs = ATT_T, ATT_STRIP
    nq = SEQ // t
    npart = len(parts)
    nsteps = MLA_HEADS // 2

    def body(q_ref, kt_ref, k_ref, vxt_ref, do_ref, o_ref, l_ref, *rest):
        part_refs, rest = rest[:npart], rest[npart:]
        dq_ref, dk_ref, dv_ref = rest[:3]
        land_refs, rest = rest[3:3 + npart], rest[3 + npart:]
        s_scr, dp_scr, p_scr, ds_scr, st_scr, send_sems, recv_sems, local_sems = rest
        exchange = _exchange_parts(part_refs, land_refs, send_sems, recv_sems, local_sems)
        _exchange_start(pl.program_id(0) == 0, exchange)
        dk_ref[...] = jnp.zeros_like(dk_ref)
        dv_ref[...] = jnp.zeros_like(dv_ref)
        lane = lax.broadcasted_iota(jnp.int32, (t, LANES), 1)

        def qtile(i, carry):
            ioff = pl.multiple_of(i * t, t)
            do_i = do_ref[pl.ds(ioff, t), :]
            o_i = o_ref[pl.ds(ioff, t), :]
            l_i = l_ref[pl.ds(ioff, t), :]
            for a in range(2):
                sl = slice(HEAD_PAD * a, HEAD_PAD * (a + 1))
                sel = (lane < V_HEAD_DIM) if a == 0 else (lane >= V_HEAD_DIM)
                doa = jnp.where(sel, do_i, 0.0)
                oa = o_i
                if a == 1:
                    doa = pltpu.roll(doa, V_HEAD_DIM, 1)
                    oa = pltpu.roll(o_i, V_HEAD_DIM, 1)
                st_scr[0] = jnp.broadcast_to(jnp.sum(doa * oa, axis=1, keepdims=True), (t, LANES))
                st_scr[1] = jnp.broadcast_to(l_i[:, V_HEAD_DIM * a:V_HEAD_DIM * a + 1], (t, LANES))
                doa_bf = doa.astype(BF16)
                qa = q_ref[pl.ds(ioff, t), sl]

                def block(j, masked, dq_acc, sl=sl, qa=qa, doa_bf=doa_bf):
                    joff = pl.multiple_of(j * t, t)
                    s_scr[...] = _dot(qa, kt_ref[sl, pl.ds(joff, t)], _NN)
                    dp_scr[...] = _dot(doa_bf, vxt_ref[sl, pl.ds(joff, t)], _NN)
                    for r in range(t // rs):
                        rows = slice(rs * r, rs * (r + 1))
                        p = jnp.exp(s_scr[rows, :] - st_scr[1, rows, :1])
                        if masked:
                            rowi = lax.broadcasted_iota(jnp.int32, (rs, t), 0) + rs * r
                            coli = lax.broadcasted_iota(jnp.int32, (rs, t), 1)
                            p = jnp.where(coli <= rowi, p, 0.0)
                        p_scr[rows, :] = p.astype(BF16)
                        ds_scr[rows, :] = (p * (dp_scr[rows, :] - st_scr[0, rows, :1])).astype(BF16)
                    dk_ref[pl.ds(joff, t), sl] += _dot(ds_scr[...], qa, _TN)
                    dv_ref[pl.ds(joff, t), sl] += _dot(p_scr[...], doa_bf, _TN)
                    return dq_acc + _dot(ds_scr[...], k_ref[pl.ds(joff, t), sl], _NN)

                dq_acc = lax.fori_loop(0, i, lambda j, acc: block(j, False, acc), jnp.zeros((t, HEAD_PAD), F32))
                dq_ref[pl.ds(ioff, t), sl] = block(i, True, dq_acc)
            return carry

        lax.fori_loop(0, nq, qtile, 0)
        _exchange_finish(pl.program_id(0) == nsteps - 1, exchange)

    hw = MLA_HEADS * HEAD_PAD
    wide = pl.BlockSpec((SEQ, 2 * HEAD_PAD), lambda p: (0, p))
    wide_t = pl.BlockSpec((2 * HEAD_PAD, SEQ), lambda p: (p, 0))
    narrow = pl.BlockSpec((SEQ, LANES), lambda p: (0, p))
    hbm = pl.BlockSpec(memory_space=pl.ANY)
    res = pl.pallas_call(
        body, name="attn_bwd", grid=(nsteps,),
        in_specs=[wide, wide_t, wide, wide_t, narrow, narrow, narrow] + [hbm] * npart,
        out_specs=[wide, wide, wide] + [hbm] * npart,
        out_shape=[jax.ShapeDtypeStruct((SEQ, hw), F32)] * 3 + [jax.ShapeDtypeStruct(p.shape, p.dtype) for p in parts],
        scratch_shapes=[pltpu.VMEM((t, t), F32), pltpu.VMEM((t, t), F32), pltpu.VMEM((t, t), BF16),
                        pltpu.VMEM((t, t), BF16), pltpu.VMEM((2, t, LANES), F32)] + _exchange_sems(npart),
        compiler_params=pltpu.CompilerParams(dimension_semantics=("arbitrary",), vmem_limit_bytes=VMEM_BIG),
    )(q, kt, k, vxt, d_o, o, lse, *parts)
    return res[0], res[1], res[2], res[3:]


def _sgu_math(u, v, zb, lg, lb, ws_ref, bias):
    ug, dug = _gelu_and_grad(u)
    vg, dvg = _gelu_and_grad(v)
    mu = jnp.mean(vg, axis=1, keepdims=True)
    xc = vg - mu
    rstd = lax.rsqrt(jnp.mean(xc * xc, axis=1, keepdims=True) + LN_EPS)
    xh = xc * rstd
    vn_bf = (xh * lg + lb).astype(BF16)
    grp = lax.broadcasted_iota(jnp.int32, (CHUNK, SGU_WIDTH), 1) // SGU_GROUP_DIM
    r_i = lax.broadcasted_iota(jnp.int32, (CHUNK, CHUNK), 0)
    c_i = lax.broadcasted_iota(jnp.int32, (CHUNK, CHUNK), 1)
    tri, tri_t = r_i >= c_i, r_i <= c_i
    mixed = bias
    for g in range(SGU_GROUPS):
        wt = jnp.where(tri, ws_ref[g], 0.0).astype(BF16)
        mixed = mixed + jnp.where(grp == g, _dot(wt, vn_bf, _NN), 0.0)
    sb = _sigmoid(zb)
    return ug, dug, dvg, rstd, xh, vn_bf, grp, tri, tri_t, mixed, sb


def _sgu_fwd(h_b, lg, lb, w_s, bias_full):
    def body(u_ref, v_ref, zb_ref, lg_ref, lb_ref, ws_ref, bias_ref, yb_ref):
        zb = zb_ref[...]
        ug, _, _, _, _, _, _, _, _, mixed, sb = _sgu_math(u_ref[...], v_ref[...], zb, lg_ref[...], lb_ref[...],
                                                       ws_ref, bias_ref[...])
        yb_ref[...] = (ug * mixed) * (zb * sb)

    blk = lambda c: pl.BlockSpec((CHUNK, SGU_WIDTH), lambda i, c=c: (i, c))
    full2 = lambda shape: pl.BlockSpec(shape, lambda i: (0, 0))
    return pl.pallas_call(
        body, name="sgu_fwd", grid=(SEQ // CHUNK,),
        in_specs=[blk(0), blk(1), blk(2), full2((1, SGU_WIDTH)), full2((1, SGU_WIDTH)),
                  pl.BlockSpec((SGU_GROUPS, CHUNK, CHUNK), lambda i: (0, 0, 0)), full2((CHUNK, SGU_WIDTH))],
        out_specs=pl.BlockSpec((CHUNK, SGU_WIDTH), lambda i: (i, 0)),
        out_shape=jax.ShapeDtypeStruct((SEQ, SGU_WIDTH), F32),
        compiler_params=pltpu.CompilerParams(dimension_semantics=("arbitrary",)),
    )(h_b, h_b, h_b, lg, lb, w_s, bias_full)


def _sgu_bwd(h_b, d_yb, lg, lb, w_s, w_st, bias_full, parts):
    nsteps = SEQ // CHUNK
    npart = len(parts)

    def body(u_ref, v_ref, zb_ref, dyb_ref, lg_ref, lb_ref, ws_ref, wst_ref, bias_ref, *rest):
        part_refs, rest = rest[:npart], rest[npart:]
        dhb_ref, dws_ref, dbs_ref, dlg_ref, dlb_ref, dbb_ref = rest[:6]
        land_refs, (dbias_acc, send_sems, recv_sems, local_sems) = rest[6:6 + npart], rest[6 + npart:]
        step = pl.program_id(0)
        exchange = _exchange_parts(part_refs, land_refs, send_sems, recv_sems, local_sems)
        _exchange_start(step == 0, exchange)

        @pl.when(step == 0)
        def _():
            dbb_ref[...] = jnp.zeros_like(dbb_ref)
            dws_ref[...] = jnp.zeros_like(dws_ref)
            dlg_ref[...] = jnp.zeros_like(dlg_ref)
            dlb_ref[...] = jnp.zeros_like(dlb_ref)
            dbias_acc[...] = jnp.zeros_like(dbias_acc)

        zb = zb_ref[...]
        lg = lg_ref[...]
        ug, dug, dvg, rstd, xh, vn_bf, grp, tri, tri_t, mixed, sb = _sgu_math(
            u_ref[...], v_ref[...], zb, lg, lb_ref[...], ws_ref, bias_ref[...])
        dyb = dyb_ref[...]
        dsgu = dyb * (zb * sb)
        dzb = dyb * (ug * mixed) * (sb * (1.0 + zb * (1.0 - sb)))
        du = dsgu * mixed * dug
        dmixed = dsgu * ug
        dbias_acc[...] += dmixed
        dvn = jnp.zeros((CHUNK, SGU_WIDTH), F32)
        for g in range(SGU_GROUPS):
            dm_g = jnp.where(grp == g, dmixed, 0.0).astype(BF16)
            wtt = jnp.where(tri_t, wst_ref[g], 0.0).astype(BF16)
            dvn = dvn + _dot(wtt, dm_g, _NN)
            dws_ref[g] += jnp.where(tri, _dot(dm_g, vn_bf, _NT), 0.0)
        dlg_ref[...] += jnp.sum(dvn * xh, axis=0, keepdims=True)
        dlb_ref[...] += jnp.sum(dvn, axis=0, keepdims=True)
        dxh = dvn * lg
        dvgel = rstd * (dxh - jnp.mean(dxh, axis=1, keepdims=True) - xh * jnp.mean(dxh * xh, axis=1, keepdims=True))
        _store_grad(dhb_ref, dbb_ref, 0, du)
        _store_grad(dhb_ref, dbb_ref, SGU_WIDTH, dvgel * dvg)
        _store_grad(dhb_ref, dbb_ref, 2 * SGU_WIDTH, dzb)

        @pl.when(step == nsteps - 1)
        def _():
            acc = dbias_acc[...]
            lane = lax.broadcasted_iota(jnp.int32, (CHUNK, LANES), 1)
            out = jnp.zeros((CHUNK, LANES), F32)
            for g in range(SGU_GROUPS):
                sg = jnp.sum(jnp.where(grp == g, acc, 0.0), axis=1, keepdims=True)
                out = jnp.where(lane == g, sg, out)
            dbs_ref[...] = out

        _exchange_finish(step == nsteps - 1, exchange)

    blk = lambda c: pl.BlockSpec((CHUNK, SGU_WIDTH), lambda i, c=c: (i, c))
    full2 = lambda shape: pl.BlockSpec(shape, lambda i: (0, 0))
    full3 = pl.BlockSpec((SGU_GROUPS, CHUNK, CHUNK), lambda i: (0, 0, 0))
    hbm = pl.BlockSpec(memory_space=pl.ANY)
    res = pl.pallas_call(
        body, name="sgu_bwd", grid=(nsteps,),
        in_specs=[blk(0), blk(1), blk(2), pl.BlockSpec((CHUNK, SGU_WIDTH), lambda i: (i, 0)),
                  full2((1, SGU_WIDTH)), full2((1, SGU_WIDTH)), full3, full3, full2((CHUNK, SGU_WIDTH))] + [hbm] * npart,
        out_specs=[pl.BlockSpec((CHUNK, SEG_B), lambda i: (i, 0)), full3, full2((CHUNK, LANES)),
                   full2((1, SGU_WIDTH)), full2((1, SGU_WIDTH)), full2((1, SEG_B))] + [hbm] * npart,
        out_shape=[jax.ShapeDtypeStruct((SEQ, SEG_B), BF16),
                   jax.ShapeDtypeStruct((SGU_GROUPS, CHUNK, CHUNK), F32),
                   jax.ShapeDtypeStruct((CHUNK, LANES), F32),
                   jax.ShapeDtypeStruct((1, SGU_WIDTH), F32), jax.ShapeDtypeStruct((1, SGU_WIDTH), F32),
                   jax.ShapeDtypeStruct((1, SEG_B), F32)] + [jax.ShapeDtypeStruct(p.shape, p.dtype) for p in parts],
        scratch_shapes=[pltpu.VMEM((CHUNK, SGU_WIDTH), F32)] + _exchange_sems(npart),
        compiler_params=pltpu.CompilerParams(dimension_semantics=("arbitrary",)),
    )(h_b, h_b, h_b, d_yb, lg, lb, w_s, w_st, bias_full, *parts)
    return res[:6], res[6:]


def _merge(x, o, h_a, y_b, target, w_oa, w_ob, w_out, ln_g, ln_b):
    tm = 256
    nsteps = SEQ // tm

    def body(x_ref, o_ref, ga_ref, gb_ref, za_ref, yb_ref, tgt_ref, woa_ref, wob_ref, wout_ref, lng_ref, lnb_ref,
             loss_ref, dxr_ref, dha_ref, do_ref, dyb_ref, poa_ref, pob_ref, pout_ref, dlng_ref, dlnb_ref, dba_ref,
             dwoa_ref, dwob_ref, dwout_ref):
        step = pl.program_id(0)

        @pl.when(step == 0)
        def _():
            for r in (loss_ref, dwoa_ref, dwob_ref, dwout_ref, dlng_ref, dlnb_ref, dba_ref):
                r[...] = jnp.zeros_like(r)

        o = o_ref[...]
        za = za_ref[...]
        sa = _sigmoid(za)
        ya_bf = (o * (za * sa)).astype(BF16)
        yb_bf = yb_ref[...].astype(BF16)
        woa, wob, wout = woa_ref[...], wob_ref[...], wout_ref[...]
        pa = _dot(ya_bf, woa, _NN)
        pb = _dot(yb_bf, wob, _NN)
        sga = _sigmoid(ga_ref[...])
        sgb = _sigmoid(gb_ref[...])
        merged_bf = (sga * pa + sgb * pb).astype(BF16)
        r = DN_ALPHA * x_ref[...] + _dot(merged_bf, wout, _NN)
        mu = jnp.mean(r, axis=1, keepdims=True)
        rc = r - mu
        rstd = lax.rsqrt(jnp.mean(rc * rc, axis=1, keepdims=True) + LN_EPS)
        xh = rc * rstd
        lng = lng_ref[...]
        y = xh * lng + lnb_ref[...]
        e = y - tgt_ref[...]
        loss_ref[...] += 0.5 * jnp.sum(jnp.sum(e * e, axis=1, keepdims=True) * (1.0 / D_MODEL), axis=0, keepdims=True)

        dy = e * (1.0 / D_MODEL)
        dlng_ref[...] += jnp.sum(dy * xh, axis=0, keepdims=True)
        dlnb_ref[...] += jnp.sum(dy, axis=0, keepdims=True)
        dxh = dy * lng
        dr = rstd * (dxh - jnp.mean(dxh, axis=1, keepdims=True) - xh * jnp.mean(dxh * xh, axis=1, keepdims=True))
        dxr_ref[...] = DN_ALPHA * dr
        dr_bf = dr.astype(BF16)
        dwout_ref[...] += _dot(merged_bf, dr_bf, _TN)
        dmerged = _dot(dr_bf, wout, _NT)
        dpa_bf = (dmerged * sga).astype(BF16)
        dpb_bf = (dmerged * sgb).astype(BF16)
        _store_grad(dha_ref, dba_ref, 0, dmerged * pa * (sga * (1.0 - sga)))
        _store_grad(dha_ref, dba_ref, D_MODEL, dmerged * pb * (sgb * (1.0 - sgb)))
        dwoa_ref[...] += _dot(ya_bf, dpa_bf, _TN)
        dwob_ref[...] += _dot(yb_bf, dpb_bf, _TN)
        dya = _dot(dpa_bf, woa, _NT)
        dyb_ref[...] = _dot(dpb_bf, wob, _NT)
        do_ref[...] = dya * (za * sa)
        _store_grad(dha_ref, dba_ref, 2 * D_MODEL, dya * o * (sa * (1.0 + za * (1.0 - sa))))

        @pl.when(step == nsteps - 1)
        def _():
            cols = D_MODEL // N_DEV
            for j in range(N_DEV):
                poa_ref[j] = dwoa_ref[:, cols * j:cols * (j + 1)].astype(BF16)
                pob_ref[j] = dwob_ref[:, cols * j:cols * (j + 1)].astype(BF16)
                pout_ref[j] = dwout_ref[cols * j:cols * (j + 1), :].astype(BF16)

    row = lambda w, c=0: pl.BlockSpec((tm, w), lambda i, c=c: (i, c))
    full = lambda shape: pl.BlockSpec(shape, lambda i: (0, 0))
    full3 = lambda shape: pl.BlockSpec(shape, lambda i: (0, 0, 0))
    return pl.pallas_call(
        body, name="merge", grid=(nsteps,),
        in_specs=[row(D_MODEL), row(MLA_WIDTH), row(D_MODEL, 0), row(D_MODEL, 1), row(MLA_WIDTH, 4), row(SGU_WIDTH),
                  row(D_MODEL), full((MLA_WIDTH, D_MODEL)), full((SGU_WIDTH, D_MODEL)), full((D_MODEL, D_MODEL)),
                  full((1, D_MODEL)), full((1, D_MODEL))],
        out_specs=[full((1, LANES)), row(D_MODEL), row(SEG_A), row(MLA_WIDTH), row(SGU_WIDTH),
                   full3((N_DEV, MLA_WIDTH, D_MODEL // N_DEV)), full3((N_DEV, SGU_WIDTH, D_MODEL // N_DEV)),
                   full3((N_DEV, D_MODEL // N_DEV, D_MODEL)), full((1, D_MODEL)), full((1, D_MODEL)), full((1, SEG_A))],
        out_shape=[jax.ShapeDtypeStruct((1, LANES), F32),
                   jax.ShapeDtypeStruct((SEQ, D_MODEL), F32), jax.ShapeDtypeStruct((SEQ, SEG_A), BF16),
                   jax.ShapeDtypeStruct((SEQ, MLA_WIDTH), F32), jax.ShapeDtypeStruct((SEQ, SGU_WIDTH), F32),
                   jax.ShapeDtypeStruct((N_DEV, MLA_WIDTH, D_MODEL // N_DEV), BF16),
                   jax.ShapeDtypeStruct((N_DEV, SGU_WIDTH, D_MODEL // N_DEV), BF16),
                   jax.ShapeDtypeStruct((N_DEV, D_MODEL // N_DEV, D_MODEL), BF16),
                   jax.ShapeDtypeStruct((1, D_MODEL), F32), jax.ShapeDtypeStruct((1, D_MODEL), F32),
                   jax.ShapeDtypeStruct((1, SEG_A), F32)],
        scratch_shapes=[pltpu.VMEM((MLA_WIDTH, D_MODEL), F32), pltpu.VMEM((SGU_WIDTH, D_MODEL), F32),
                        pltpu.VMEM((D_MODEL, D_MODEL), F32)],
        compiler_params=pltpu.CompilerParams(dimension_semantics=("arbitrary",), vmem_limit_bytes=VMEM_BIG),
    )(x, o, h_a, h_a, h_a, y_b, target, w_oa, w_ob, w_out, ln_g, ln_b)


def _mla_bwd(dq, dk, dv, h_c, xt_bf, gq, gkv, wq, wkn, wv, c_t, sa_t, sb_t, parts):
    tm = 256
    hw = MLA_HEADS * HEAD_PAD
    npart = len(parts)
    nsteps = SEQ // tm

    def body(dq_ref, dk_ref, dv_ref, cq_ref, ckv_ref, xt_ref, gq_ref, gkv_ref, wq_ref, wkn_ref, wv_ref, c_ref, sa_ref,
             sb_ref, *rest):
        part_refs, rest = rest[:npart], rest[npart:]
        dhc_ref, puq_ref, dwkn_ref, dwv_ref, dgq_ref, dgkv_ref, dbc_ref, dwc_ref = rest[:8]
        land_refs, (pre_ref, dwq_ref, dwc_acc, send_sems, recv_sems, local_sems) = rest[8:8 + npart], rest[8 + npart:]
        exchange = _exchange_parts(part_refs, land_refs, send_sems, recv_sems, local_sems)
        _exchange_start(pl.program_id(0) == 0, exchange)
        _exchange_finish(pl.program_id(0) == nsteps - 1, exchange)

        @pl.when(pl.program_id(0) == 0)
        def _():
            for r in (dwq_ref, dwc_acc, dwkn_ref, dwv_ref, dgq_ref, dgkv_ref, dbc_ref):
                r[...] = jnp.zeros_like(r)

        c, sa, sb = c_ref[...], sa_ref[...], sb_ref[...]
        lane = lax.broadcasted_iota(jnp.int32, (tm, LANES), 1)
        rope_lanes = jnp.logical_and(lane >= ROPE_LO, lane < ROPE_HI)

        cq = cq_ref[...]
        gq = gq_ref[...]
        rq = lax.rsqrt(jnp.sum(cq * cq, axis=1, keepdims=True) * (1.0 / Q_LORA_RANK) + RMS_EPS)
        nq = cq * rq
        cqn_bf = (nq * gq).astype(BF16)
        for h in range(MLA_HEADS):
            sl = slice(HEAD_PAD * h, HEAD_PAD * (h + 1))
            pre_ref[:, sl] = _rope_t(dq_ref[:, sl] * ATTN_SCALE, c, sa, sb).astype(BF16)
        dqpre_bf = pre_ref[...]
        dcqn = _dot(dqpre_bf, wq_ref[...], _NT)
        dwq_ref[...] += _dot(cqn_bf, dqpre_bf, _TN)
        dgq_ref[...] += jnp.sum(dcqn * nq, axis=0, keepdims=True)
        dnq = dcqn * gq
        _store_grad(dhc_ref, dbc_ref, 0,
                    rq * (dnq - nq * (jnp.sum(dnq * nq, axis=1, keepdims=True) * (1.0 / Q_LORA_RANK))))

        ckv = ckv_ref[...]
        gkv = gkv_ref[...]
        rkv = lax.rsqrt(jnp.sum(ckv * ckv, axis=1, keepdims=True) * (1.0 / KV_LORA_RANK) + RMS_EPS)
        nkv = ckv * rkv
        ckvn_bf = (nkv * gkv).astype(BF16)
        dk = dk_ref[...]
        dk_bf = dk.astype(BF16)
        dv_bf = dv_ref[...].astype(BF16)
        dckvn = _dot(dk_bf, wkn_ref[...], _NT) + _dot(dv_bf, wv_ref[...], _NT)
        dwkn_ref[...] += _dot(ckvn_bf, dk_bf, _TN)
        dwv_ref[...] += _dot(ckvn_bf, dv_bf, _TN)
        dgkv_ref[...] += jnp.sum(dckvn * nkv, axis=0, keepdims=True)
        dnkv = dckvn * gkv
        _store_grad(dhc_ref, dbc_ref, CQ_PAD, rkv * (
            dnkv - nkv * (jnp.sum(dnkv * nkv, axis=1, keepdims=True) * (1.0 / KV_LORA_RANK))))
        dkpe = jnp.zeros((tm, LANES), F32)
        for h in range(MLA_HEADS):
            dkpe = dkpe + dk[:, HEAD_PAD * h:HEAD_PAD * (h + 1)]
        _store_grad(dhc_ref, dbc_ref, CQ_PAD + LANES, _rope_t(jnp.where(rope_lanes, dkpe, 0.0), c, sa, sb))
        dwc_acc[...] += _dot(xt_ref[...], dhc_ref[...], _NN)

        @pl.when(pl.program_id(0) == SEQ // tm - 1)
        def _():
            dwc_ref[...] = dwc_acc[...].astype(BF16)
            rows = Q_LORA_RANK // N_DEV
            for j in range(N_DEV):
                for h in range(MLA_HEADS):
                    puq_ref[j, :, QK_HEAD_DIM * h:QK_HEAD_DIM * (h + 1)] = dwq_ref[
                        rows * j:rows * (j + 1), HEAD_PAD * h:HEAD_PAD * h + QK_HEAD_DIM].astype(BF16)

    full = lambda shape: pl.BlockSpec(shape, lambda i: (0, 0))
    row = lambda w, c=0: pl.BlockSpec((tm, w), lambda i, c=c: (i, c))
    hbm = pl.BlockSpec(memory_space=pl.ANY)
    res = pl.pallas_call(
        body, name="mla_bwd", grid=(nsteps,),
        in_specs=[row(hw), row(hw), row(hw), row(CQ_PAD, 0), row(LANES, CQ_PAD // LANES),
                  pl.BlockSpec((D_MODEL, tm), lambda i: (0, i)),
                  full((1, CQ_PAD)), full((1, KV_LORA_RANK)), full((CQ_PAD, hw)), full((KV_LORA_RANK, hw)),
                  full((KV_LORA_RANK, hw)), row(LANES), row(LANES), row(LANES)] + [hbm] * npart,
        out_specs=[row(SEG_C), pl.BlockSpec((N_DEV, Q_LORA_RANK // N_DEV, MLA_HEADS * QK_HEAD_DIM), lambda i: (0, 0, 0)),
                   full((KV_LORA_RANK, hw)), full((KV_LORA_RANK, hw)),
                   full((1, CQ_PAD)), full((1, KV_LORA_RANK)), full((1, SEG_C)), full((D_MODEL, SEG_C))] + [hbm] * npart,
        out_shape=[jax.ShapeDtypeStruct((SEQ, SEG_C), BF16),
                   jax.ShapeDtypeStruct((N_DEV, Q_LORA_RANK // N_DEV, MLA_HEADS * QK_HEAD_DIM), BF16),
                   jax.ShapeDtypeStruct((KV_LORA_RANK, hw), F32), jax.ShapeDtypeStruct((KV_LORA_RANK, hw), F32),
                   jax.ShapeDtypeStruct((1, CQ_PAD), F32), jax.ShapeDtypeStruct((1, KV_LORA_RANK), F32),
                   jax.ShapeDtypeStruct((1, SEG_C), F32), jax.ShapeDtypeStruct((D_MODEL, SEG_C), BF16)]
        + [jax.ShapeDtypeStruct(p.shape, p.dtype) for p in parts],
        scratch_shapes=[pltpu.VMEM((tm, hw), BF16), pltpu.VMEM((CQ_PAD, hw), F32), pltpu.VMEM((D_MODEL, SEG_C), F32)]
        + _exchange_sems(npart),
        compiler_params=pltpu.CompilerParams(dimension_semantics=("arbitrary",), vmem_limit_bytes=VMEM_MID),
    )(dq, dk, dv, h_c, h_c, xt_bf, gq, gkv, wq, wkn, wv, c_t, sa_t, sb_t, *parts)
    return res[:8], res[8:]


def _adamw_all(ws, gs, ms, vs):
    n = len(ws)
    c1 = 1.0 / (1.0 - ADAM_B1 ** ADAM_STEP)
    c2 = 1.0 / (1.0 - ADAM_B2 ** ADAM_STEP)

    def body(*refs):
        for idx in range(n):
            w, g, m, v = (refs[idx][...], refs[n + idx][...], refs[2 * n + idx][...], refs[3 * n + idx][...])
            m_new = ADAM_B1 * m + (1.0 - ADAM_B1) * g
            v_new = ADAM_B2 * v + (1.0 - ADAM_B2) * (g * g)
            delta = -ADAM_LR * ((m_new * c1) / (jnp.sqrt(v_new * c2) + ADAM_EPS) + ADAM_WD * w)
            refs[4 * n + idx][...] = delta
            refs[5 * n + idx][...] = m_new
            refs[6 * n + idx][...] = v_new

    shapes = [jax.ShapeDtypeStruct(w.shape, F32) for w in ws]
    outs = pl.pallas_call(
        body, name="adamw", out_shape=shapes * 3,
        compiler_params=pltpu.CompilerParams(vmem_limit_bytes=VMEM_BIG),
    )(*ws, *gs, *ms, *vs)
    return outs[:n], outs[n:2 * n], outs[2 * n:]


SHARD_W = IN_WIDTH // N_DEV

_PIECES = [(0, 384, 2, 0), (384, 512, 2, CQ_PAD), (512, 544, 2, CQ_PAD + LANES + ROPE_LO),
           (544, 1056, 0, 2 * D_MODEL), (1056, 1568, 1, 0), (1568, 2080, 1, SGU_WIDTH),
           (2080, 2592, 1, 2 * SGU_WIDTH), (2592, 3616, 0, 0), (3616, 4640, 0, D_MODEL)]


def _column_runs():
    runs = []
    for n0, n1, seg, d0 in _PIECES:
        for j in range(N_DEV):
            lo, hi = max(n0, j * SHARD_W), min(n1, (j + 1) * SHARD_W)
            if lo < hi:
                runs.append((j, lo - j * SHARD_W, hi - j * SHARD_W, seg, d0 + lo - n0))
    return runs


def _mesh_pos():
    return lax.axis_index("x"), lax.axis_index("y"), lax.axis_index("c")


def _remote(src, dst, send_sems, recv_sems, k, to):
    return pltpu.make_async_remote_copy(src_ref=src, dst_ref=dst, send_sem=send_sems.at[k], recv_sem=recv_sems.at[k],
                                        device_id=to, device_id_type=pl.DeviceIdType.MESH)


def _gather_exchange(gats, send_sems, recv_sems, meanwhile=None):
    x, y, c = _mesh_pos()
    me, sibling = (x, y, c), (x, y, 1 - c)
    chips = [(1 - x, y), (x, 1 - y), (1 - x, 1 - y)]

    def copy(a, k, blk, to):
        slab = gats[a].at[4 * blk[0] + 2 * blk[1] + blk[2]]
        return _remote(slab, slab, send_sems, recv_sems, 7 * a + k, to)

    arrays = range(len(gats))
    first = [copy(a, 1 + j, me, (*chip, c)) for j, chip in enumerate(chips) for a in arrays]
    first += [copy(a, 0, me, sibling) for a in arrays]
    for cp in first:
        cp.start()
    if meanwhile is not None:
        meanwhile()
    passed = []
    for j, chip in enumerate(chips):
        for a in arrays:
            copy(a, 1 + j, (*chip, c), me).wait_recv()
            fwd = copy(a, 4 + j, (*chip, c), sibling)
            fwd.start()
            passed.append(fwd)
    for a in arrays:
        copy(a, 0, sibling, me).wait_recv()
    for j, chip in enumerate(chips):
        for a in arrays:
            copy(a, 4 + j, (*chip, 1 - c), me).wait_recv()
    for cp in first + passed:
        cp.wait_send()


def _gather_behind(own, gats, send_sems, recv_sems, local_sems, step, mid, last):
    x, y, c = _mesh_pos()
    me, sibling = (x, y, c), (x, y, 1 - c)
    chips = [(1 - x, y), (x, 1 - y), (1 - x, 1 - y)]
    arrays = range(len(gats))

    def copy(a, k, blk, to, src=None):
        slab = gats[a].at[4 * blk[0] + 2 * blk[1] + blk[2]]
        return _remote(slab if src is None else src, slab, send_sems, recv_sems, 7 * a + k, to)

    first = [copy(a, 1 + j, me, (*chip, c), src=own[a]) for j, chip in enumerate(chips) for a in arrays]
    first += [copy(a, 0, me, sibling, src=own[a]) for a in arrays]
    local = [pltpu.make_async_copy(own[a], gats[a].at[4 * x + 2 * y + c], local_sems.at[a]) for a in arrays]
    passed = [copy(a, 4 + j, (*chip, c), sibling) for j, chip in enumerate(chips) for a in arrays]

    @pl.when(step == 0)
    def _():
        for cp in first + local:
            cp.start()

    @pl.when(step == mid)
    def _():
        for j, chip in enumerate(chips):
            for a in arrays:
                copy(a, 1 + j, (*chip, c), me).wait_recv()
        for cp in passed:
            cp.start()

    @pl.when(step == last)
    def _():
        for a in arrays:
            copy(a, 0, sibling, me).wait_recv()
        for j, chip in enumerate(chips):
            for a in arrays:
                copy(a, 4 + j, (*chip, 1 - c), me).wait_recv()
        for cp in first + passed:
            cp.wait_send()
        for cp in local:
            cp.wait()


def _gather_first(w_in, w_uq2, w_oa, w_ob, w_out, x2, pos_col, invf_lane):
    hw = MLA_HEADS * HEAD_PAD
    uq_rows = Q_LORA_RANK // N_DEV
    rows = 256

    def body(win_ref, wuq_ref, woa_ref, wob_ref, wout_ref, x_ref, pos_ref, invf_ref,
             wc_ref, wq_ref, winb_ref, oab_ref, obb_ref, outb_ref, xb_ref, xt_ref, c_ref, sa_ref, sb_ref,
             g_uq, blk0, send_sems, recv_sems):
        def local_work():
            for i in range(SEQ // rows):
                xi = x_ref[rows * i:rows * (i + 1), :]
                xb_ref[rows * i:rows * (i + 1), :] = xi.astype(BF16)
                xt_ref[:, rows * i:rows * (i + 1)] = xi.T.astype(BF16)
            ang = pos_ref[...].astype(F32) * invf_ref[...]
            cs, sn = jnp.cos(ang), jnp.sin(ang)
            lane = lax.broadcasted_iota(jnp.int32, ang.shape, 1)
            c_ref[...] = jnp.where(lane < ROPE_LO, 1.0, jnp.where(lane < ROPE_HI, cs, 0.0))
            sa_ref[...] = jnp.where(jnp.logical_and(lane >= ROPE_LO, lane < ROPE_MID), -sn, 0.0)
            sb_ref[...] = jnp.where(jnp.logical_and(lane >= ROPE_MID, lane < ROPE_HI), sn, 0.0)

        x, y, c = _mesh_pos()
        me = (x, y, c)
        winb_ref[...] = win_ref[0].astype(BF16)
        oab_ref[...] = woa_ref[0].astype(BF16)
        obb_ref[...] = wob_ref[0].astype(BF16)
        outb_ref[...] = wout_ref[0].astype(BF16)
        g_uq[4 * x + 2 * y + c] = wuq_ref[...].astype(BF16)

        chip0 = jnp.logical_and(x == 0, y == 0)
        south = c == 0
        half = D_MODEL // 2
        halves = [blk0.at[pl.ds(0, half)], blk0.at[pl.ds(half, half)]]

        def bcopy(k, to, part=None):
            ref = blk0 if part is None else halves[part]
            return _remote(ref, ref, send_sems, recv_sems, 7 + k, to)

        sends0 = [(0, (0, 0, 1), None), (1, (1, 0, 0), 0), (2, (0, 1, 0), 1), (3, (1, 0, 0), 1), (4, (0, 1, 0), 0)]

        @pl.when(jnp.logical_and(chip0, south))
        def _():
            blk0[...] = winb_ref[...]
            for k, to, part in sends0:
                bcopy(k, to, part).start()

        _gather_exchange([g_uq], send_sems, recv_sems, meanwhile=local_work)

        for (cx, cy), first_k, first_half, second_k in (((1, 0), 1, 0, 3), ((0, 1), 2, 1, 4)):
            @pl.when(jnp.logical_and(jnp.logical_and(x == cx, y == cy), south))
            def _(cx=cx, cy=cy, first_k=first_k, first_half=first_half, second_k=second_k):
                bcopy(first_k, me, first_half).wait_recv()
                onward = bcopy(5 + first_half, (1, 1, 0), first_half)
                onward.start()
                bcopy(second_k, me, 1 - first_half).wait_recv()
                north = bcopy(7, (cx, cy, 1))
                north.start()
                onward.wait_send()
                north.wait_send()

        @pl.when(jnp.logical_and(jnp.logical_and(x == 1, y == 1), south))
        def _():
            bcopy(5, me, 0).wait_recv()
            bcopy(6, me, 1).wait_recv()
            north = bcopy(7, (1, 1, 1))
            north.start()
            north.wait_send()

        @pl.when(jnp.logical_and(chip0, c == 1))
        def _():
            bcopy(0, me).wait_recv()

        @pl.when(jnp.logical_and(jnp.logical_not(chip0), c == 1))
        def _():
            bcopy(7, me).wait_recv()

        @pl.when(jnp.logical_and(chip0, south))
        def _():
            for k, to, part in sends0:
                bcopy(k, to, part).wait_send()

        for j, s0, s1, seg, d0 in _column_runs():
            if seg == 2:
                wc_ref[:, d0:d0 + (s1 - s0)] = blk0[:, s0:s1]
        zeros = lambda r, w: jnp.zeros((r, w), BF16)
        wc_ref[:, Q_LORA_RANK:CQ_PAD] = zeros(D_MODEL, CQ_PAD - Q_LORA_RANK)
        wc_ref[:, CQ_PAD + LANES:CQ_PAD + LANES + ROPE_LO] = zeros(D_MODEL, ROPE_LO)
        wc_ref[:, CQ_PAD + LANES + ROPE_HI:SEG_C] = zeros(D_MODEL, LANES - ROPE_HI)
        wq_ref[Q_LORA_RANK:CQ_PAD, :] = zeros(CQ_PAD - Q_LORA_RANK, hw)
        for h in range(MLA_HEADS):
            wq_ref[0:Q_LORA_RANK, HEAD_PAD * h + QK_HEAD_DIM:HEAD_PAD * (h + 1)] = zeros(Q_LORA_RANK, HEAD_PAD - QK_HEAD_DIM)
        for j in range(N_DEV):
            for h in range(MLA_HEADS):
                wq_ref[uq_rows * j:uq_rows * (j + 1), HEAD_PAD * h:HEAD_PAD * h + QK_HEAD_DIM] = g_uq[
                    j, :, QK_HEAD_DIM * h:QK_HEAD_DIM * (h + 1)]

    vmem = pl.BlockSpec(memory_space=pltpu.VMEM)
    return pl.pallas_call(
        body, name="gather_first",
        out_shape=[jax.ShapeDtypeStruct((D_MODEL, SEG_C), BF16), jax.ShapeDtypeStruct((CQ_PAD, hw), BF16),
                   jax.ShapeDtypeStruct(w_in.shape[1:], BF16), jax.ShapeDtypeStruct(w_oa.shape[1:], BF16),
                   jax.ShapeDtypeStruct(w_ob.shape[1:], BF16), jax.ShapeDtypeStruct(w_out.shape[1:], BF16),
                   jax.ShapeDtypeStruct((SEQ, D_MODEL), BF16), jax.ShapeDtypeStruct((D_MODEL, SEQ), BF16)]
        + [jax.ShapeDtypeStruct((SEQ, LANES), F32)] * 3,
        in_specs=[vmem] * 8, out_specs=[vmem] * 11,
        scratch_shapes=[pltpu.VMEM((N_DEV, uq_rows, MLA_HEADS * QK_HEAD_DIM), BF16), pltpu.VMEM((D_MODEL, SHARD_W), BF16),
                        pltpu.SemaphoreType.DMA((15,)), pltpu.SemaphoreType.DMA((15,))],
        compiler_params=pltpu.CompilerParams(vmem_limit_bytes=VMEM_BIG),
    )(w_in, w_uq2, w_oa, w_ob, w_out, x2, pos_col, invf_lane)


def _assemble_in(g_in):
    def body(g_ref, wa_ref, wb_ref):
        segs = [wa_ref, wb_ref]
        for j, s0, s1, seg, d0 in _column_runs():
            if seg < 2:
                segs[seg][:, d0:d0 + (s1 - s0)] = g_ref[j, :, s0:s1]

    return pl.pallas_call(
        body, name="assemble_in",
        out_shape=[jax.ShapeDtypeStruct((D_MODEL, SEG_A), BF16), jax.ShapeDtypeStruct((D_MODEL, SEG_B), BF16)],
        compiler_params=pltpu.CompilerParams(vmem_limit_bytes=VMEM_MID),
    )(g_in)


def _assemble_out(g_oa, g_ob, g_out):
    cols = D_MODEL // N_DEV

    def body(goa_ref, gob_ref, gout_ref, oa_ref, ob_ref, out_ref):
        for j in range(N_DEV):
            oa_ref[:, cols * j:cols * (j + 1)] = goa_ref[j]
            ob_ref[:, cols * j:cols * (j + 1)] = gob_ref[j]
            out_ref[cols * j:cols * (j + 1), :] = gout_ref[j]

    return pl.pallas_call(
        body, name="assemble_out",
        out_shape=[jax.ShapeDtypeStruct((MLA_WIDTH, D_MODEL), BF16), jax.ShapeDtypeStruct((SGU_WIDTH, D_MODEL), BF16),
                   jax.ShapeDtypeStruct((D_MODEL, D_MODEL), BF16)],
    )(g_oa, g_ob, g_out)


C_NAT = 544


P_IN_SPLIT = 896


def _to_parts(dwa, dwb):
    def body(dwa_ref, dwb_ref, phi_ref, plo_ref):
        phi_ref[0, :, 0:C_NAT] = jnp.zeros((P_IN_SPLIT, C_NAT), BF16)
        plo_ref[0, :, 0:C_NAT] = jnp.zeros((D_MODEL - P_IN_SPLIT, C_NAT), BF16)
        segs = [dwa_ref, dwb_ref]
        for j, s0, s1, seg, d0 in _column_runs():
            if seg < 2:
                phi_ref[j, :, s0:s1] = segs[seg][0:P_IN_SPLIT, d0:d0 + (s1 - s0)]
                plo_ref[j, :, s0:s1] = segs[seg][P_IN_SPLIT:D_MODEL, d0:d0 + (s1 - s0)]

    return pl.pallas_call(
        body, name="to_parts",
        out_shape=[jax.ShapeDtypeStruct((N_DEV, P_IN_SPLIT, SHARD_W), BF16),
                   jax.ShapeDtypeStruct((N_DEV, D_MODEL - P_IN_SPLIT, SHARD_W), BF16)],
        compiler_params=pltpu.CompilerParams(vmem_limit_bytes=VMEM_MID))(dwa, dwb)


def _dx_tail(dhs, ws, dx_res, dwc, p_uq, p_rep):
    ntile, sums_at = 8, 5
    tm = SEQ // ntile
    rep_rows = p_rep.shape[1]
    c_rows = D_MODEL // N_DEV
    spec = [((c_rows, C_NAT), BF16), (p_uq.shape[1:], BF16), ((rep_rows, LANES), F32)]
    n = len(spec)

    nseg = len(dhs)

    def body(*refs):
        dh_refs, w_refs = refs[:nseg], refs[nseg:2 * nseg]
        dxr_ref, dwc_ref, puq_ref, prep_ref, dx_ref, call_ref, guq_ref, repall_ref, pc_ref, c_all, rep_all = refs[
            2 * nseg:2 * nseg + 11]
        rest = refs[2 * nseg + 11:]
        ras, tbs, rbs = rest[0:n], rest[n:2 * n], rest[2 * n:3 * n]
        send_sems, recv_sems, gsend, grecv = rest[3 * n:]
        step = pl.program_id(0)
        x, y, c = _mesh_pos()
        me_idx = 4 * x + 2 * y + c
        me, sibling = (x, y, c), (x, y, 1 - c)
        others = [(1 - x, y), (x, 1 - y), (1 - x, 1 - y)]
        parts = [pc_ref, puq_ref, prep_ref]
        gats = [rep_all, c_all]

        def stage1(chip, a):
            return _remote(parts[a].at[2 * chip + (1 - c)], ras[a].at[chip], send_sems, recv_sems, 7 * a + chip, sibling)

        def stage2(k, a):
            cx, cy = others[k]
            return _remote(tbs[a].at[k], rbs[a].at[k], send_sems, recv_sems, 7 * a + 4 + k, (cx, cy, c))

        def gcopy(a, k, blk, to):
            slab = gats[a].at[4 * blk[0] + 2 * blk[1] + blk[2]]
            return _remote(slab, slab, gsend, grecv, 7 * a + k, to)

        def chip_sum(a, chip):
            return parts[a][2 * chip + c].astype(F32) + ras[a][chip].astype(F32)

        @pl.when(step == 0)
        def _():
            for j, s0, s1, seg, d0 in _column_runs():
                if seg == 2:
                    for r in range(N_DEV):
                        pc_ref[r, :, s0:s1] = dwc_ref[c_rows * r:c_rows * (r + 1), d0:d0 + (s1 - s0)]
            for chip in range(4):
                for a in range(n):
                    stage1(chip, a).start()

        @pl.when(step == 1)
        def _():
            for chip in range(4):
                for a in range(n):
                    stage1(chip, a).wait_recv()
            for k, (cx, cy) in enumerate(others):
                for a in range(n):
                    tbs[a][k] = chip_sum(a, 2 * cx + cy).astype(spec[a][1])
                    stage2(k, a).start()

        @pl.when(step == sums_at)
        def _():
            for k in range(3):
                for a in range(n):
                    stage2(k, a).wait_recv()
            sums = []
            for a in range(n):
                acc = chip_sum(a, 2 * x + y)
                for k in range(3):
                    acc = acc + rbs[a][k].astype(F32)
                sums.append(acc)
            c_all[me_idx] = sums[0].astype(BF16)
            guq_ref[...] = sums[1]
            rep_all[me_idx] = sums[2]
            for a in range(2):
                for j, chip in enumerate(others):
                    gcopy(a, 1 + j, me, (*chip, c)).start()
                gcopy(a, 0, me, sibling).start()

        acc = dxr_ref[...]
        for dh_ref, w_ref in zip(dh_refs, w_refs):
            acc = acc + _dot(dh_ref[...], w_ref[...], _NT)
        dx_ref[...] = acc

        @pl.when(step == ntile - 1)
        def _():
            for j, chip in enumerate(others):
                for a in range(2):
                    gcopy(a, 1 + j, (*chip, c), me).wait_recv()
                    gcopy(a, 4 + j, (*chip, c), sibling).start()
            for a in range(2):
                gcopy(a, 0, sibling, me).wait_recv()
                for j, chip in enumerate(others):
                    gcopy(a, 4 + j, (*chip, 1 - c), me).wait_recv()
            for a in range(2):
                gcopy(a, 0, me, sibling).wait_send()
                for j, chip in enumerate(others):
                    gcopy(a, 1 + j, me, (*chip, c)).wait_send()
                    gcopy(a, 4 + j, (*chip, c), sibling).wait_send()
            for a in range(n):
                for chip in range(4):
                    stage1(chip, a).wait_send()
                for k in range(3):
                    stage2(k, a).wait_send()
            call_ref[...] = c_all[...]
            repall_ref[...] = rep_all[...]

    row = lambda w: pl.BlockSpec((tm, w), lambda i: (i, 0))
    full = lambda shape: pl.BlockSpec(shape, lambda i: (0,) * len(shape))
    scratch = [pltpu.VMEM((N_DEV, c_rows, C_NAT), BF16), pltpu.VMEM((N_DEV, c_rows, C_NAT), BF16),
               pltpu.VMEM((N_DEV, rep_rows, LANES), F32)]
    for lead in (4, 3, 3):
        scratch += [pltpu.VMEM((lead,) + tuple(shape), dt) for shape, dt in spec]
    scratch += [pltpu.SemaphoreType.DMA((7 * n,)), pltpu.SemaphoreType.DMA((7 * n,)),
                pltpu.SemaphoreType.DMA((14,)), pltpu.SemaphoreType.DMA((14,))]
    return pl.pallas_call(
        body, name="dx_tail", grid=(SEQ // tm,),
        in_specs=[row(dh.shape[1]) for dh in dhs] + [full(w.shape) for w in ws]
        + [row(D_MODEL), full(dwc.shape), full(p_uq.shape), full(p_rep.shape)],
        out_specs=[row(D_MODEL), full((N_DEV, c_rows, C_NAT)), full(p_uq.shape[1:]), full((N_DEV, rep_rows, LANES))],
        out_shape=[jax.ShapeDtypeStruct((SEQ, D_MODEL), F32), jax.ShapeDtypeStruct((N_DEV, c_rows, C_NAT), BF16),
                   jax.ShapeDtypeStruct(p_uq.shape[1:], F32), jax.ShapeDtypeStruct((N_DEV, rep_rows, LANES), F32)],
        scratch_shapes=scratch,
        compiler_params=pltpu.CompilerParams(dimension_semantics=("arbitrary",), vmem_limit_bytes=VMEM_BIG),
    )(*dhs, *ws, dx_res, dwc, p_uq, p_rep)


def _sum_landed(landed, c_all):
    c_rows = D_MODEL // N_DEV

    def body(rhi_ref, rlo_ref, roa_ref, rob_ref, rout_ref, call_ref, gin_ref, goa_ref, gob_ref, gout_ref):
        def total(ref, sl):
            acc = ref[0, sl, :].astype(F32)
            for s in range(1, N_DEV):
                acc = acc + ref[s, sl, :].astype(F32)
            return acc

        x, y, c = _mesh_pos()
        dev0 = jnp.where(4 * x + 2 * y + c == 0, 1.0, 0.0)
        for j in range(N_DEV):
            sl = slice(c_rows * j, c_rows * (j + 1))
            below = c_rows * j < P_IN_SPLIT
            tot = total(rhi_ref, sl) if below else total(rlo_ref, slice(c_rows * j - P_IN_SPLIT, c_rows * (j + 1) - P_IN_SPLIT))
            gin_ref[0, sl, C_NAT:SHARD_W] = tot[:, C_NAT:SHARD_W]
            gin_ref[0, sl, 0:C_NAT] = tot[:, 0:C_NAT] + dev0 * call_ref[j].astype(F32)
        goa_ref[0] = total(roa_ref, slice(None))
        gob_ref[0] = total(rob_ref, slice(None))
        gout_ref[0] = total(rout_ref, slice(None))

    return pl.pallas_call(
        body, name="sum_landed",
        out_shape=[jax.ShapeDtypeStruct((1, D_MODEL, SHARD_W), F32)]
        + [jax.ShapeDtypeStruct((1,) + r.shape[1:], F32) for r in landed[2:]],
        compiler_params=pltpu.CompilerParams(vmem_limit_bytes=VMEM_MID),
    )(*landed, c_all)


_O_CQ, _O_CKV, _O_KPE, _O_ZA, _O_U, _O_V, _O_ZB, _O_GA, _O_GB = 0, 384, 512, 544, 1056, 1568, 2080, 2592, 3616


def _to_segments(w):
    z = lambda n: jnp.zeros(w.shape[:-1] + (n,), w.dtype)
    seg_a = jnp.concatenate([w[..., _O_GA:_O_GB], w[..., _O_GB:IN_WIDTH], w[..., _O_ZA:_O_U]], axis=-1)
    seg_b = jnp.concatenate([w[..., _O_U:_O_V], w[..., _O_V:_O_ZB], w[..., _O_ZB:_O_GA]], axis=-1)
    seg_c = jnp.concatenate([w[..., _O_CQ:_O_CKV], z(CQ_PAD - Q_LORA_RANK), w[..., _O_CKV:_O_KPE],
                             z(ROPE_LO), w[..., _O_KPE:_O_ZA], z(LANES - ROPE_HI)], axis=-1)
    return seg_a, seg_b, seg_c


def _from_segments(seg_a, seg_b, seg_c):
    kpe0 = CQ_PAD + LANES + ROPE_LO
    return jnp.concatenate([
        seg_c[..., 0:Q_LORA_RANK], seg_c[..., CQ_PAD:CQ_PAD + LANES], seg_c[..., kpe0:kpe0 + QK_ROPE_DIM],
        seg_a[..., 2 * D_MODEL:SEG_A], seg_b, seg_a[..., 0:2 * D_MODEL]], axis=-1)


def kernel(x, positions, w_in, b_in, g_q, w_uq, g_kv, w_ukv, w_oa, sgu_ln_g, sgu_ln_b, w_s, b_s, w_ob, w_out, ln_g, ln_b, loss_target, m_w_in, m_b_in, m_g_q, m_w_uq, m_g_kv, m_w_ukv, m_w_oa, m_sgu_ln_g, m_sgu_ln_b, m_w_s, m_b_s, m_w_ob, m_w_out, m_ln_g, m_ln_b, v_w_in, v_b_in, v_g_q, v_w_uq, v_g_kv, v_w_ukv, v_w_oa, v_sgu_ln_g, v_sgu_ln_b, v_w_s, v_b_s, v_w_ob, v_w_out, v_ln_g, v_ln_b):
    w_uq2 = w_uq[0].reshape(Q_LORA_RANK // N_DEV, MLA_HEADS * QK_HEAD_DIM)
    inv_freq = ROPE_THETA ** (-jnp.arange(0, QK_ROPE_DIM, 2, dtype=F32) / QK_ROPE_DIM)
    invf_lane = jnp.concatenate([jnp.zeros((ROPE_LO,), F32), inv_freq, inv_freq,
                                 jnp.zeros((LANES - ROPE_HI,), F32)]).reshape(1, LANES)
    first = _gather_first(w_in, w_uq2, w_oa, w_ob, w_out, x[0], positions.reshape(SEQ, 1), invf_lane)
    partials = _local_step(x[0], loss_target[0], first, b_in, g_q, g_kv, w_ukv, sgu_ln_g, sgu_ln_b, w_s, b_s, ln_g, ln_b)
    weights = dict(w_in=w_in, b_in=b_in, g_q=g_q, w_uq=w_uq, g_kv=g_kv, w_ukv=w_ukv, w_oa=w_oa, sgu_ln_g=sgu_ln_g,
                   sgu_ln_b=sgu_ln_b, w_s=w_s, b_s=b_s, w_ob=w_ob, w_out=w_out, ln_g=ln_g, ln_b=ln_b)
    moms = dict(w_in=m_w_in, b_in=m_b_in, g_q=m_g_q, w_uq=m_w_uq, g_kv=m_g_kv, w_ukv=m_w_ukv, w_oa=m_w_oa,
                sgu_ln_g=m_sgu_ln_g, sgu_ln_b=m_sgu_ln_b, w_s=m_w_s, b_s=m_b_s, w_ob=m_w_ob, w_out=m_w_out,
                ln_g=m_ln_g, ln_b=m_ln_b)
    vars_ = dict(w_in=v_w_in, b_in=v_b_in, g_q=v_g_q, w_uq=v_w_uq, g_kv=v_g_kv, w_ukv=v_w_ukv, w_oa=v_w_oa,
                 sgu_ln_g=v_sgu_ln_g, sgu_ln_b=v_sgu_ln_b, w_s=v_w_s, b_s=v_b_s, w_ob=v_w_ob, w_out=v_w_out,
                 ln_g=v_ln_g, ln_b=v_ln_b)
    return _reduce_and_update(partials, weights, moms, vars_)


def _local_step(x2, tgt, first, b_in, g_q, g_kv, w_ukv, sgu_ln_g, sgu_ln_b, w_s, b_s, ln_g, ln_b):
    wc, wq, win_b, oa_b, ob_b, out_b, x_bf, xt_bf, c_t, sa_t, sb_t = first
    ba, bb, bc = _to_segments(b_in)
    w_ukv_bf = w_ukv[0].astype(BF16)
    wkn = jnp.pad(w_ukv_bf[:, :, :QK_NOPE_DIM], ((0, 0), (0, 0), (0, HEAD_PAD - QK_NOPE_DIM))).reshape(KV_LORA_RANK, -1)
    wv = jnp.pad(w_ukv_bf[:, :, QK_NOPE_DIM:], ((0, 0), (0, 0), (0, HEAD_PAD - V_HEAD_DIM))).reshape(KV_LORA_RANK, -1)
    gq = jnp.pad(g_q, ((0, 0), (0, CQ_PAD - Q_LORA_RANK)))
    bias_full = jnp.repeat(b_s[0].T, SGU_GROUP_DIM, axis=1)
    w_s3 = w_s[0]
    w_st3 = jnp.swapaxes(w_s3, 1, 2)

    h_c = _mm(x_bf, wc, bias=bc, tm=512, tn=SEG_C, name="in_proj_c")
    q, k, kt, vx, vxt = _mla_prep(h_c, gq, g_kv, wq, wkn, wv, c_t, sa_t, sb_t)
    o, lse, (g_in,) = _attn_fwd(q, kt, vx, (win_b,))
    wa, wb = _assemble_in(g_in)
    h_a, (g_out,) = _mm(x_bf, wa, bias=ba, own=(out_b,), tm=512, tn=SEG_A // 2, name="in_proj_a")
    h_b, (g_oa, g_ob) = _mm(x_bf, wb, bias=bb, own=(oa_b, ob_b), tm=512, tn=SEG_B // 2, name="in_proj_b")
    y_b = _sgu_fwd(h_b, sgu_ln_g, sgu_ln_b, w_s3, bias_full)
    w_oa_f, w_ob_f, w_out_f = _assemble_out(g_oa, g_ob, g_out)

    (loss_row, dx_res, dh_a, d_o, d_yb, p_oa, p_ob, p_out, d_lng, d_lnb, d_ba) = _merge(
        x2, o, h_a, y_b, tgt, w_oa_f, w_ob_f, w_out_f, ln_g, ln_b)
    (dh_b, d_ws, d_bs_t, d_slg, d_slb, d_bb), (r_out,) = _sgu_bwd(h_b, d_yb, sgu_ln_g, sgu_ln_b, w_s3, w_st3, bias_full,
                                                                 (p_out,))
    d_wa, (r_oa,) = _mm(xt_bf, dh_a, out_dtype=BF16, parts=(p_oa,), tm=512, tn=512, name="dw_in_a")
    d_wb = _mm(xt_bf, dh_b, out_dtype=BF16, tm=512, tn=512, name="dw_in_b")
    p_hi, p_lo = _to_parts(d_wa, d_wb)
    dq, dk, dv, (r_hi,) = _attn_bwd(q, kt, k, vxt, d_o, o, lse, (p_hi,))
    (dh_c, p_uq, d_wkn, d_wv, d_gq, d_gkv, d_bc, d_wc), (r_lo, r_ob) = _mla_bwd(
        dq, dk, dv, h_c, xt_bf, gq, g_kv, wq, wkn, wv, c_t, sa_t, sb_t, (p_lo, p_ob))
    landed = (r_hi, r_lo, r_oa, r_ob, r_out)

    p_b_in = _from_segments(d_ba, d_bb, d_bc)
    p_w_ukv = jnp.concatenate([d_wkn.reshape(KV_LORA_RANK, MLA_HEADS, HEAD_PAD)[:, :, :QK_NOPE_DIM],
                               d_wv.reshape(KV_LORA_RANK, MLA_HEADS, HEAD_PAD)[:, :, :V_HEAD_DIM]], axis=-1)
    p_g_q = d_gq[:, :Q_LORA_RANK]
    p_b_s = d_bs_t[:, :SGU_GROUPS].T
    replicated = [p_b_in, p_g_q, d_gkv, p_w_ukv, d_slg, d_slb, d_ws, p_b_s, d_lng, d_lnb]
    return loss_row, ((dh_a, dh_b, dh_c), (wa, wb, wc), dx_res), landed, d_wc, p_uq, replicated


_NAMES = ["w_in", "b_in", "g_q", "w_uq", "g_kv", "w_ukv", "w_oa", "sgu_ln_g", "sgu_ln_b", "w_s", "b_s", "w_ob",
          "w_out", "ln_g", "ln_b"]
_REPLICATED = ["b_in", "g_q", "g_kv", "w_ukv", "sgu_ln_g", "sgu_ln_b", "w_s", "b_s", "ln_g", "ln_b"]


def _reduce_and_update(partials, weights, moms, vars_):
    loss_row, (dhs, ws, dx_res), landed, d_wc, p_uq, replicated = partials
    def piece(a):
        flat = a.reshape(-1)
        return jnp.pad(flat, (0, -flat.size % PACK_ALIGN))

    rep_flat = jnp.concatenate([piece(a) for a in replicated] + [piece(loss_row[0, :1])])
    rep_flat = jnp.pad(rep_flat, (0, N_DEV * PACK_R_ROWS * LANES - rep_flat.size))
    dx_ab, c_all, g_uq, rep_all = _dx_tail(dhs[:2], ws[:2], dx_res, d_wc, p_uq,
                                           rep_flat.reshape(N_DEV, PACK_R_ROWS, LANES))
    dx = _mm(dhs[2], ws[2], tb=True, add=dx_ab, tm=512, tn=D_MODEL, name="dx_c")
    g_in, g_oa, g_ob, g_out = _sum_landed(landed, c_all)
    rep_sum = rep_all.reshape(-1)
    grads, pos = dict(w_in=g_in, w_uq=g_uq, w_oa=g_oa, w_ob=g_ob, w_out=g_out), 0
    for nm in _REPLICATED:
        grads[nm] = rep_sum[pos:pos + weights[nm].size]
        pos += weights[nm].size + -weights[nm].size % PACK_ALIGN
    loss = rep_sum[pos]
    grads = {nm: grads[nm].reshape(weights[nm].shape) for nm in _NAMES}
    deltas, new_m, new_v = _adamw_all([weights[nm] for nm in _NAMES], [grads[nm] for nm in _NAMES],
                                      [moms[nm] for nm in _NAMES], [vars_[nm] for nm in _NAMES])
    return (loss, dx.reshape(1, SEQ, D_MODEL), *[grads[nm] for nm in _NAMES], *deltas, *new_m, *new_v)
```

```python
import math

import jax
import jax.numpy as jnp
from jax import lax
from jax.experimental import pallas as pl
from jax.experimental.pallas import tpu as pltpu

F32 = jnp.float32
BF16 = jnp.bfloat16

D_MODEL = 1024
SEQ = 2048
N_DEV = 8
MLA_HEADS = 8
Q_LORA_RANK = 384
KV_LORA_RANK = 128
QK_NOPE_DIM = 64
QK_ROPE_DIM = 32
V_HEAD_DIM = 64
QK_HEAD_DIM = QK_NOPE_DIM + QK_ROPE_DIM
MLA_WIDTH = MLA_HEADS * V_HEAD_DIM
ROPE_THETA = 10000.0
SGU_GROUPS = 8
SGU_GROUP_DIM = 64
SGU_WIDTH = SGU_GROUPS * SGU_GROUP_DIM
CHUNK = 128
RMS_EPS = 1e-6
LN_EPS = 1e-5
DN_ALPHA = 2.0 ** 0.25
IN_WIDTH = 4640
ATTN_SCALE = QK_HEAD_DIM ** -0.5

ADAM_LR = 0.001
ADAM_B1 = 0.9
ADAM_B2 = 0.999
ADAM_EPS = 1e-08
ADAM_WD = 0.01
ADAM_STEP = 10

LANES = 128
HEAD_PAD = 128
ROPE_LO = QK_NOPE_DIM
ROPE_MID = ROPE_LO + QK_ROPE_DIM // 2
ROPE_HI = ROPE_LO + QK_ROPE_DIM
CQ_PAD = 512

SEG_A = 2560
SEG_B = 1536
SEG_C = 768

PACK_R_ROWS = 272
PACK_ALIGN = 8 * LANES
VMEM_BIG = 56 * 1024 * 1024
VMEM_MID = 40 * 1024 * 1024


def _sigmoid(x):
    return 1.0 / (1.0 + jnp.exp(-x))


def _gelu_and_grad(x):
    c0 = math.sqrt(2.0 / math.pi)
    x2 = x * x
    t = jnp.tanh(c0 * (x + 0.044715 * x * x2))
    g = 0.5 * x * (1.0 + t)
    dg = 0.5 * (1.0 + t) + 0.5 * x * (1.0 - t * t) * (c0 * (1.0 + 3.0 * 0.044715 * x2))
    return g, dg


def _dot(a, b, dims):
    return lax.dot_general(a, b, (dims, ((), ())), preferred_element_type=F32)


_NN = ((1,), (0,))
_NT = ((1,), (1,))
_TN = ((0,), (0,))


def _store_grad(dh_ref, db_ref, col, val):
    cols = slice(col, col + val.shape[1])
    dh_ref[:, cols] = val.astype(BF16)
    db_ref[:, cols] += jnp.sum(val, axis=0, keepdims=True)


def _mm(a, b, *, tb=False, bias=None, add=None, out_dtype=F32, own=(), parts=(), tm, tn, name):
    m, k = a.shape
    n = b.shape[0] if tb else b.shape[1]
    assert m % tm == 0 and n % tn == 0 and not (own and parts)
    dims = _NT if tb else _NN
    nown = len(own) + len(parts)
    nm = m // tm
    nsteps = (n // tn) * nm

    def body(*refs):
        a_ref, b_ref = refs[0], refs[1]
        pos = 2
        r = _dot(a_ref[...], b_ref[...], dims)
        if bias is not None:
            r = r + refs[pos][...]; pos += 1
        if add is not None:
            r = r + refs[pos][...]; pos += 1
        own_refs = refs[pos:pos + nown]; pos += nown
        refs[pos][...] = r.astype(out_dtype)
        if nown:
            gat_refs = refs[pos + 1:pos + 1 + nown]
            send_sems, recv_sems, local_sems = refs[pos + 1 + nown:]
            step = pl.program_id(0) * nm + pl.program_id(1)
            if own:
                _gather_behind(own_refs, gat_refs, send_sems, recv_sems, local_sems, step, nsteps - 2, nsteps - 1)
            else:
                exchange = _exchange_parts(own_refs, gat_refs, send_sems, recv_sems, local_sems)
                _exchange_start(step == 0, exchange)
                _exchange_finish(step == nsteps - 1, exchange)

    b_spec = pl.BlockSpec((tn, k), lambda j, i: (j, 0)) if tb else pl.BlockSpec((k, tn), lambda j, i: (0, j))
    in_specs, args = [pl.BlockSpec((tm, k), lambda j, i: (i, 0)), b_spec], [a, b]
    if bias is not None:
        in_specs.append(pl.BlockSpec((1, tn), lambda j, i: (0, j))); args.append(bias)
    if add is not None:
        in_specs.append(pl.BlockSpec((tm, tn), lambda j, i: (i, j))); args.append(add)
    hbm = pl.BlockSpec(memory_space=pl.ANY)
    res = pl.pallas_call(
        body, name=name, grid=(n // tn, nm), in_specs=in_specs + [hbm] * nown,
        out_specs=[pl.BlockSpec((tm, tn), lambda j, i: (i, j))] + [hbm] * nown,
        out_shape=[jax.ShapeDtypeStruct((m, n), out_dtype)]
        + [jax.ShapeDtypeStruct((N_DEV,) + o.shape, o.dtype) for o in own]
        + [jax.ShapeDtypeStruct(p.shape, p.dtype) for p in parts],
        scratch_shapes=_exchange_sems(nown) if nown else [],
        compiler_params=pltpu.CompilerParams(dimension_semantics=("arbitrary", "arbitrary"), vmem_limit_bytes=VMEM_BIG),
    )(*args, *own, *parts)
    return (res[0], res[1:]) if nown else res[0]


def _rope(x, c, sa, sb):
    return x * c + pltpu.roll(x, LANES - 16, 1) * sa + pltpu.roll(x, 16, 1) * sb


def _rope_t(dy, c, sa, sb):
    return dy * c + pltpu.roll(dy * sa, 16, 1) + pltpu.roll(dy * sb, LANES - 16, 1)


def _mla_prep(h_c, gq, gkv, wq, wkn, wvx, c_t, sa_t, sb_t):
    tm = 256
    hw = MLA_HEADS * HEAD_PAD

    def body(cq_ref, ckv_ref, kpe_ref, gq_ref, gkv_ref, wq_ref, wkn_ref, wvx_ref, c_ref, sa_ref, sb_ref,
             q_ref, k_ref, kt_ref, vx_ref, vxt_ref):
        c, sa, sb = c_ref[...], sa_ref[...], sb_ref[...]
        cq = cq_ref[...]
        rq = lax.rsqrt(jnp.sum(cq * cq, axis=1, keepdims=True) * (1.0 / Q_LORA_RANK) + RMS_EPS)
        cqn = ((cq * rq) * gq_ref[...]).astype(BF16)
        qall = _dot(cqn, wq_ref[...], _NN)
        for h in range(MLA_HEADS):
            sl = slice(HEAD_PAD * h, HEAD_PAD * (h + 1))
            q_ref[:, sl] = (_rope(qall[:, sl], c, sa, sb) * ATTN_SCALE).astype(BF16)
        ckv = ckv_ref[...]
        rkv = lax.rsqrt(jnp.sum(ckv * ckv, axis=1, keepdims=True) * (1.0 / KV_LORA_RANK) + RMS_EPS)
        ckvn = ((ckv * rkv) * gkv_ref[...]).astype(BF16)
        knall = _dot(ckvn, wkn_ref[...], _NN)
        vall = _dot(ckvn, wvx_ref[...], _NN)
        kper = _rope(kpe_ref[...], c, sa, sb)
        ones_half = (lax.broadcasted_iota(jnp.int32, (tm, HEAD_PAD), 1) >= V_HEAD_DIM).astype(F32)
        for h in range(MLA_HEADS):
            sl = slice(HEAD_PAD * h, HEAD_PAD * (h + 1))
            kh = knall[:, sl] + kper
            vh = vall[:, sl] + ones_half
            k_ref[:, sl] = kh.astype(BF16)
            kt_ref[sl, :] = kh.T.astype(BF16)
            vx_ref[:, sl] = vh.astype(BF16)
            vxt_ref[sl, :] = vh.T.astype(BF16)

    full = lambda shape: pl.BlockSpec(shape, lambda i: (0, 0))
    tab = pl.BlockSpec((tm, LANES), lambda i: (i, 0))
    row = pl.BlockSpec((tm, hw), lambda i: (i, 0))
    col = pl.BlockSpec((hw, tm), lambda i: (0, i))
    return pl.pallas_call(
        body, name="mla_prep", grid=(SEQ // tm,),
        in_specs=[pl.BlockSpec((tm, CQ_PAD), lambda i: (i, 0)),
                  pl.BlockSpec((tm, LANES), lambda i: (i, CQ_PAD // LANES)),
                  pl.BlockSpec((tm, LANES), lambda i: (i, CQ_PAD // LANES + 1)),
                  full((1, CQ_PAD)), full((1, KV_LORA_RANK)),
                  full((CQ_PAD, hw)), full((KV_LORA_RANK, hw)), full((KV_LORA_RANK, hw)), tab, tab, tab],
        out_specs=[row, row, col, row, col],
        out_shape=[jax.ShapeDtypeStruct((SEQ, hw), BF16), jax.ShapeDtypeStruct((SEQ, hw), BF16),
                   jax.ShapeDtypeStruct((hw, SEQ), BF16), jax.ShapeDtypeStruct((SEQ, hw), BF16),
                   jax.ShapeDtypeStruct((hw, SEQ), BF16)],
        compiler_params=pltpu.CompilerParams(dimension_semantics=("arbitrary",), vmem_limit_bytes=VMEM_MID),
    )(h_c, h_c, h_c, gq, gkv, wq, wkn, wvx, c_t, sa_t, sb_t)


ATT_T = 512
ATT_STRIP = 64


def _attn_fwd(q, kt, vx, own):
    t, rs = ATT_T, ATT_STRIP
    nown = len(own)
    nq = SEQ // t
    nsteps = (MLA_HEADS // 2) * nq

    def body(q_ref, kt_ref, vx_ref, *rest):
        own_refs, (o_ref, l_ref), gat_refs = rest[:nown], rest[nown:nown + 2], rest[nown + 2:2 * nown + 2]
        s_scr, p_scr, m_scr, a_scr, acc_scr, send_sems, recv_sems, local_sems = rest[2 * nown + 2:]
        qi = pl.program_id(1)
        _gather_behind(own_refs, gat_refs, send_sems, recv_sems, local_sems, pl.program_id(0) * nq + qi,
                       nsteps - 2, nsteps - 1)
        lane = lax.broadcasted_iota(jnp.int32, (t, LANES), 1)
        m_scr[...] = jnp.full((2, t, LANES), -1e30, F32)
        acc_scr[...] = jnp.zeros((2, t, LANES), F32)

        def block(j, masked):
            off = pl.multiple_of(j * t, t)
            for a in range(2):
                sl = slice(HEAD_PAD * a, HEAD_PAD * (a + 1))
                s_scr[a] = _dot(q_ref[:, sl], kt_ref[sl, pl.ds(off, t)], _NN)
                for r in range(t // rs):
                    rows = slice(rs * r, rs * (r + 1))
                    s = s_scr[a, rows, :]
                    if masked:
                        rowi = lax.broadcasted_iota(jnp.int32, (rs, t), 0) + rs * r
                        coli = lax.broadcasted_iota(jnp.int32, (rs, t), 1)
                        s = jnp.where(coli <= rowi, s, -1e30)
                    m_old = m_scr[a, rows, :]
                    m_new = jnp.maximum(m_old, jnp.max(s, axis=1, keepdims=True))
                    p_scr[a, rows, :] = jnp.exp(s - m_new[:, :1]).astype(BF16)
                    a_scr[a, rows, :] = jnp.exp(m_old - m_new)
                    m_scr[a, rows, :] = m_new
                acc_scr[a] = acc_scr[a] * a_scr[a] + _dot(p_scr[a], vx_ref[pl.ds(off, t), sl], _NN)

        def step(j, carry):
            block(j, False)
            return carry
        lax.fori_loop(0, qi, step, 0)
        block(qi, True)
        res = []
        for a in range(2):
            acc = acc_scr[a]
            l = acc[:, V_HEAD_DIM:V_HEAD_DIM + 1]
            res.append((acc / l, m_scr[a] + jnp.log(l)))
        o_ref[...] = jnp.where(lane < V_HEAD_DIM, res[0][0], pltpu.roll(res[1][0], V_HEAD_DIM, 1))
        l_ref[...] = jnp.where(lane < V_HEAD_DIM, res[0][1], res[1][1])

    hbm = pl.BlockSpec(memory_space=pl.ANY)
    res = pl.pallas_call(
        body, name="attn_fwd", grid=(MLA_HEADS // 2, nq),
        in_specs=[pl.BlockSpec((t, 2 * HEAD_PAD), lambda p, i: (i, p)),
                  pl.BlockSpec((2 * HEAD_PAD, SEQ), lambda p, i: (p, 0)),
                  pl.BlockSpec((SEQ, 2 * HEAD_PAD), lambda p, i: (0, p))] + [hbm] * nown,
        out_specs=[pl.BlockSpec((t, LANES), lambda p, i: (i, p)),
                   pl.BlockSpec((t, LANES), lambda p, i: (i, p))] + [hbm] * nown,
        out_shape=[jax.ShapeDtypeStruct((SEQ, MLA_WIDTH), F32), jax.ShapeDtypeStruct((SEQ, MLA_WIDTH), F32)]
        + [jax.ShapeDtypeStruct((N_DEV,) + a.shape, a.dtype) for a in own],
        scratch_shapes=[pltpu.VMEM((2, t, t), F32), pltpu.VMEM((2, t, t), BF16), pltpu.VMEM((2, t, LANES), F32),
                        pltpu.VMEM((2, t, LANES), F32), pltpu.VMEM((2, t, LANES), F32)] + _exchange_sems(nown),
        compiler_params=pltpu.CompilerParams(dimension_semantics=("arbitrary", "arbitrary"), vmem_limit_bytes=VMEM_MID),
    )(q, kt, vx, *own)
    return res[0], res[1], res[2:]


def _exchange_parts(parts, lands, send_sems, recv_sems, local_sems):
    x, y, c = _mesh_pos()
    me = 4 * x + 2 * y + c
    peers = [(x, y, 1 - c), (1 - x, y, c), (x, 1 - y, c), (1 - x, 1 - y, c),
             (1 - x, y, 1 - c), (x, 1 - y, 1 - c), (1 - x, 1 - y, 1 - c)]
    remote, local = [], []
    for a, (part, land) in enumerate(zip(parts, lands)):
        for k, peer in enumerate(peers):
            t = 4 * peer[0] + 2 * peer[1] + peer[2]
            remote.append(_remote(part.at[t], land.at[me], send_sems, recv_sems, 7 * a + k, peer))
        local.append(pltpu.make_async_copy(part.at[me], land.at[me], local_sems.at[a]))
    return remote, local


def _exchange_start(first_step, exchange):
    remote, local = exchange

    @pl.when(first_step)
    def _():
        for cp in remote + local:
            cp.start()


def _exchange_finish(last_step, exchange):
    remote, local = exchange

    @pl.when(last_step)
    def _():
        for cp in remote:
            cp.wait_recv()
        for cp in remote:
            cp.wait_send()
        for cp in local:
            cp.wait()


def _exchange_sems(npart):
    return [pltpu.SemaphoreType.DMA((7 * npart,)), pltpu.SemaphoreType.DMA((7 * npart,)),
            pltpu.SemaphoreType.DMA((npart,))]


def _attn_bwd(q, kt, k, vxt, d_o, o, lse, parts):
    t, rs = ATT_T, ATT_STRIP
    nq = SEQ // t
    npart = len(parts)
    nsteps = MLA_HEADS // 2

    def body(q_ref, kt_ref, k_ref, vxt_ref, do_ref, o_ref, l_ref, *rest):
        part_refs, rest = rest[:npart], rest[npart:]
        dq_ref, dk_ref, dv_ref = rest[:3]
        land_refs, rest = rest[3:3 + npart], rest[3 + npart:]
        s_scr, dp_scr, p_scr, ds_scr, st_scr, send_sems, recv_sems, local_sems = rest
        exchange = _exchange_parts(part_refs, land_refs, send_sems, recv_sems, local_sems)
        _exchange_start(pl.program_id(0) == 0, exchange)
        dk_ref[...] = jnp.zeros_like(dk_ref)
        dv_ref[...] = jnp.zeros_like(dv_ref)
        lane = lax.broadcasted_iota(jnp.int32, (t, LANES), 1)

        def qtile(i, carry):
            ioff = pl.multiple_of(i * t, t)
            do_i = do_ref[pl.ds(ioff, t), :]
            o_i = o_ref[pl.ds(ioff, t), :]
            l_i = l_ref[pl.ds(ioff, t), :]
            for a in range(2):
                sl = slice(HEAD_PAD * a, HEAD_PAD * (a + 1))
                sel = (lane < V_HEAD_DIM) if a == 0 else (lane >= V_HEAD_DIM)
                doa = jnp.where(sel, do_i, 0.0)
                oa = o_i
                if a == 1:
                    doa = pltpu.roll(doa, V_HEAD_DIM, 1)
                    oa = pltpu.roll(o_i, V_HEAD_DIM, 1)
                st_scr[0] = jnp.broadcast_to(jnp.sum(doa * oa, axis=1, keepdims=True), (t, LANES))
                st_scr[1] = jnp.broadcast_to(l_i[:, V_HEAD_DIM * a:V_HEAD_DIM * a + 1], (t, LANES))
                doa_bf = doa.astype(BF16)
                qa = q_ref[pl.ds(ioff, t), sl]

                def block(j, masked, dq_acc, sl=sl, qa=qa, doa_bf=doa_bf):
                    joff = pl.multiple_of(j * t, t)
                    s_scr[...] = _dot(qa, kt_ref[sl, pl.ds(joff, t)], _NN)
                    dp_scr[...] = _dot(doa_bf, vxt_ref[sl, pl.ds(joff, t)], _NN)
                    for r in range(t // rs):
                        rows = slice(rs * r, rs * (r + 1))
                        p = jnp.exp(s_scr[rows, :] - st_scr[1, rows, :1])
                        if masked:
                            rowi = lax.broadcasted_iota(jnp.int32, (rs, t), 0) + rs * r
                            coli = lax.broadcasted_iota(jnp.int32, (rs, t), 1)
                            p = jnp.where(coli <= rowi, p, 0.0)
                        p_scr[rows, :] = p.astype(BF16)
                        ds_scr[rows, :] = (p * (dp_scr[rows, :] - st_scr[0, rows, :1])).astype(BF16)
                    dk_ref[pl.ds(joff, t), sl] += _dot(ds_scr[...], qa, _TN)
                    dv_ref[pl.ds(joff, t), sl] += _dot(p_scr[...], doa_bf, _TN)
                    return dq_acc + _dot(ds_scr[...], k_ref[pl.ds(joff, t), sl], _NN)

                dq_acc = lax.fori_loop(0, i, lambda j, acc: block(j, False, acc), jnp.zeros((t, HEAD_PAD), F32))
                dq_ref[pl.ds(ioff, t), sl] = block(i, True, dq_acc)
            return carry

        lax.fori_loop(0, nq, qtile, 0)
        _exchange_finish(pl.program_id(0) == nsteps - 1, exchange)

    hw = MLA_HEADS * HEAD_PAD
    wide = pl.BlockSpec((SEQ, 2 * HEAD_PAD), lambda p: (0, p))
    wide_t = pl.BlockSpec((2 * HEAD_PAD, SEQ), lambda p: (p, 0))
    narrow = pl.BlockSpec((SEQ, LANES), lambda p: (0, p))
    hbm = pl.BlockSpec(memory_space=pl.ANY)
    res = pl.pallas_call(
        body, name="attn_bwd", grid=(nsteps,),
        in_specs=[wide, wide_t, wide, wide_t, narrow, narrow, narrow] + [hbm] * npart,
        out_specs=[wide, wide, wide] + [hbm] * npart,
        out_shape=[jax.ShapeDtypeStruct((SEQ, hw), F32)] * 3 + [jax.ShapeDtypeStruct(p.shape, p.dtype) for p in parts],
        scratch_shapes=[pltpu.VMEM((t, t), F32), pltpu.VMEM((t, t), F32), pltpu.VMEM((t, t), BF16),
                        pltpu.VMEM((t, t), BF16), pltpu.VMEM((2, t, LANES), F32)] + _exchange_sems(npart),
        compiler_params=pltpu.CompilerParams(dimension_semantics=("arbitrary",), vmem_limit_bytes=VMEM_BIG),
    )(q, kt, k, vxt, d_o, o, lse, *parts)
    return res[0], res[1], res[2], res[3:]


def _sgu_math(u, v, zb, lg, lb, ws_ref, bias):
    ug, dug = _gelu_and_grad(u)
    vg, dvg = _gelu_and_grad(v)
    mu = jnp.mean(vg, axis=1, keepdims=True)
    xc = vg - mu
    rstd = lax.rsqrt(jnp.mean(xc * xc, axis=1, keepdims=True) + LN_EPS)
    xh = xc * rstd
    vn_bf = (xh * lg + lb).astype(BF16)
    grp = lax.broadcasted_iota(jnp.int32, (CHUNK, SGU_WIDTH), 1) // SGU_GROUP_DIM
    r_i = lax.broadcasted_iota(jnp.int32, (CHUNK, CHUNK), 0)
    c_i = lax.broadcasted_iota(jnp.int32, (CHUNK, CHUNK), 1)
    tri, tri_t = r_i >= c_i, r_i <= c_i
    mixed = bias
    for g in range(SGU_GROUPS):
        wt = jnp.where(tri, ws_ref[g], 0.0).astype(BF16)
        mixed = mixed + jnp.where(grp == g, _dot(wt, vn_bf, _NN), 0.0)
    sb = _sigmoid(zb)
    return ug, dug, dvg, rstd, xh, vn_bf, grp, tri, tri_t, mixed, sb


def _sgu_fwd(h_b, lg, lb, w_s, bias_full):
    def body(u_ref, v_ref, zb_ref, lg_ref, lb_ref, ws_ref, bias_ref, yb_ref):
        zb = zb_ref[...]
        ug, _, _, _, _, _, _, _, _, mixed, sb = _sgu_math(u_ref[...], v_ref[...], zb, lg_ref[...], lb_ref[...],
                                                       ws_ref, bias_ref[...])
        yb_ref[...] = (ug * mixed) * (zb * sb)

    blk = lambda c: pl.BlockSpec((CHUNK, SGU_WIDTH), lambda i, c=c: (i, c))
    full2 = lambda shape: pl.BlockSpec(shape, lambda i: (0, 0))
    return pl.pallas_call(
        body, name="sgu_fwd", grid=(SEQ // CHUNK,),
        in_specs=[blk(0), blk(1), blk(2), full2((1, SGU_WIDTH)), full2((1, SGU_WIDTH)),
                  pl.BlockSpec((SGU_GROUPS, CHUNK, CHUNK), lambda i: (0, 0, 0)), full2((CHUNK, SGU_WIDTH))],
        out_specs=pl.BlockSpec((CHUNK, SGU_WIDTH), lambda i: (i, 0)),
        out_shape=jax.ShapeDtypeStruct((SEQ, SGU_WIDTH), F32),
        compiler_params=pltpu.CompilerParams(dimension_semantics=("arbitrary",)),
    )(h_b, h_b, h_b, lg, lb, w_s, bias_full)


def _sgu_bwd(h_b, d_yb, lg, lb, w_s, w_st, bias_full, parts):
    nsteps = SEQ // CHUNK
    npart = len(parts)

    def body(u_ref, v_ref, zb_ref, dyb_ref, lg_ref, lb_ref, ws_ref, wst_ref, bias_ref, *rest):
        part_refs, rest = rest[:npart], rest[npart:]
        dhb_ref, dws_ref, dbs_ref, dlg_ref, dlb_ref, dbb_ref = rest[:6]
        land_refs, (dbias_acc, send_sems, recv_sems, local_sems) = rest[6:6 + npart], rest[6 + npart:]
        step = pl.program_id(0)
        exchange = _exchange_parts(part_refs, land_refs, send_sems, recv_sems, local_sems)
        _exchange_start(step == 0, exchange)

        @pl.when(step == 0)
        def _():
            dbb_ref[...] = jnp.zeros_like(dbb_ref)
            dws_ref[...] = jnp.zeros_like(dws_ref)
            dlg_ref[...] = jnp.zeros_like(dlg_ref)
            dlb_ref[...] = jnp.zeros_like(dlb_ref)
            dbias_acc[...] = jnp.zeros_like(dbias_acc)

        zb = zb_ref[...]
        lg = lg_ref[...]
        ug, dug, dvg, rstd, xh, vn_bf, grp, tri, tri_t, mixed, sb = _sgu_math(
            u_ref[...], v_ref[...], zb, lg, lb_ref[...], ws_ref, bias_ref[...])
        dyb = dyb_ref[...]
        dsgu = dyb * (zb * sb)
        dzb = dyb * (ug * mixed) * (sb * (1.0 + zb * (1.0 - sb)))
        du = dsgu * mixed * dug
        dmixed = dsgu * ug
        dbias_acc[...] += dmixed
        dvn = jnp.zeros((CHUNK, SGU_WIDTH), F32)
        for g in range(SGU_GROUPS):
            dm_g = jnp.where(grp == g, dmixed, 0.0).astype(BF16)
            wtt = jnp.where(tri_t, wst_ref[g], 0.0).astype(BF16)
            dvn = dvn + _dot(wtt, dm_g, _NN)
            dws_ref[g] += jnp.where(tri, _dot(dm_g, vn_bf, _NT), 0.0)
        dlg_ref[...] += jnp.sum(dvn * xh, axis=0, keepdims=True)
        dlb_ref[...] += jnp.sum(dvn, axis=0, keepdims=True)
        dxh = dvn * lg
        dvgel = rstd * (dxh - jnp.mean(dxh, axis=1, keepdims=True) - xh * jnp.mean(dxh * xh, axis=1, keepdims=True))
        _store_grad(dhb_ref, dbb_ref, 0, du)
        _store_grad(dhb_ref, dbb_ref, SGU_WIDTH, dvgel * dvg)
        _store_grad(dhb_ref, dbb_ref, 2 * SGU_WIDTH, dzb)

        @pl.when(step == nsteps - 1)
        def _():
            acc = dbias_acc[...]
            lane = lax.broadcasted_iota(jnp.int32, (CHUNK, LANES), 1)
            out = jnp.zeros((CHUNK, LANES), F32)
            for g in range(SGU_GROUPS):
                sg = jnp.sum(jnp.where(grp == g, acc, 0.0), axis=1, keepdims=True)
                out = jnp.where(lane == g, sg, out)
            dbs_ref[...] = out

        _exchange_finish(step == nsteps - 1, exchange)

    blk = lambda c: pl.BlockSpec((CHUNK, SGU_WIDTH), lambda i, c=c: (i, c))
    full2 = lambda shape: pl.BlockSpec(shape, lambda i: (0, 0))
    full3 = pl.BlockSpec((SGU_GROUPS, CHUNK, CHUNK), lambda i: (0, 0, 0))
    hbm = pl.BlockSpec(memory_space=pl.ANY)
    res = pl.pallas_call(
        body, name="sgu_bwd", grid=(nsteps,),
        in_specs=[blk(0), blk(1), blk(2), pl.BlockSpec((CHUNK, SGU_WIDTH), lambda i: (i, 0)),
                  full2((1, SGU_WIDTH)), full2((1, SGU_WIDTH)), full3, full3, full2((CHUNK, SGU_WIDTH))] + [hbm] * npart,
        out_specs=[pl.BlockSpec((CHUNK, SEG_B), lambda i: (i, 0)), full3, full2((CHUNK, LANES)),
                   full2((1, SGU_WIDTH)), full2((1, SGU_WIDTH)), full2((1, SEG_B))] + [hbm] * npart,
        out_shape=[jax.ShapeDtypeStruct((SEQ, SEG_B), BF16),
                   jax.ShapeDtypeStruct((SGU_GROUPS, CHUNK, CHUNK), F32),
                   jax.ShapeDtypeStruct((CHUNK, LANES), F32),
                   jax.ShapeDtypeStruct((1, SGU_WIDTH), F32), jax.ShapeDtypeStruct((1, SGU_WIDTH), F32),
                   jax.ShapeDtypeStruct((1, SEG_B), F32)] + [jax.ShapeDtypeStruct(p.shape, p.dtype) for p in parts],
        scratch_shapes=[pltpu.VMEM((CHUNK, SGU_WIDTH), F32)] + _exchange_sems(npart),
        compiler_params=pltpu.CompilerParams(dimension_semantics=("arbitrary",)),
    )(h_b, h_b, h_b, d_yb, lg, lb, w_s, w_st, bias_full, *parts)
    return res[:6], res[6:]


def _merge(x, o, h_a, y_b, target, w_oa, w_ob, w_out, ln_g, ln_b):
    tm = 256
    nsteps = SEQ // tm

    def body(x_ref, o_ref, ga_ref, gb_ref, za_ref, yb_ref, tgt_ref, woa_ref, wob_ref, wout_ref, lng_ref, lnb_ref,
             loss_ref, dxr_ref, dha_ref, do_ref, dyb_ref, poa_ref, pob_ref, pout_ref, dlng_ref, dlnb_ref, dba_ref,
             dwoa_ref, dwob_ref, dwout_ref):
        step = pl.program_id(0)

        @pl.when(step == 0)
        def _():
            for r in (loss_ref, dwoa_ref, dwob_ref, dwout_ref, dlng_ref, dlnb_ref, dba_ref):
                r[...] = jnp.zeros_like(r)

        o = o_ref[...]
        za = za_ref[...]
        sa = _sigmoid(za)
        ya_bf = (o * (za * sa)).astype(BF16)
        yb_bf = yb_ref[...].astype(BF16)
        woa, wob, wout = woa_ref[...], wob_ref[...], wout_ref[...]
        pa = _dot(ya_bf, woa, _NN)
        pb = _dot(yb_bf, wob, _NN)
        sga = _sigmoid(ga_ref[...])
        sgb = _sigmoid(gb_ref[...])
        merged_bf = (sga * pa + sgb * pb).astype(BF16)
        r = DN_ALPHA * x_ref[...] + _dot(merged_bf, wout, _NN)
        mu = jnp.mean(r, axis=1, keepdims=True)
        rc = r - mu
        rstd = lax.rsqrt(jnp.mean(rc * rc, axis=1, keepdims=True) + LN_EPS)
        xh = rc * rstd
        lng = lng_ref[...]
        y = xh * lng + lnb_ref[...]
        e = y - tgt_ref[...]
        loss_ref[...] += 0.5 * jnp.sum(jnp.sum(e * e, axis=1, keepdims=True) * (1.0 / D_MODEL), axis=0, keepdims=True)

        dy = e * (1.0 / D_MODEL)
        dlng_ref[...] += jnp.sum(dy * xh, axis=0, keepdims=True)
        dlnb_ref[...] += jnp.sum(dy, axis=0, keepdims=True)
        dxh = dy * lng
        dr = rstd * (dxh - jnp.mean(dxh, axis=1, keepdims=True) - xh * jnp.mean(dxh * xh, axis=1, keepdims=True))
        dxr_ref[...] = DN_ALPHA * dr
        dr_bf = dr.astype(BF16)
        dwout_ref[...] += _dot(merged_bf, dr_bf, _TN)
        dmerged = _dot(dr_bf, wout, _NT)
        dpa_bf = (dmerged * sga).astype(BF16)
        dpb_bf = (dmerged * sgb).astype(BF16)
        _store_grad(dha_ref, dba_ref, 0, dmerged * pa * (sga * (1.0 - sga)))
        _store_grad(dha_ref, dba_ref, D_MODEL, dmerged * pb * (sgb * (1.0 - sgb)))
        dwoa_ref[...] += _dot(ya_bf, dpa_bf, _TN)
        dwob_ref[...] += _dot(yb_bf, dpb_bf, _TN)
        dya = _dot(dpa_bf, woa, _NT)
        dyb_ref[...] = _dot(dpb_bf, wob, _NT)
        do_ref[...] = dya * (za * sa)
        _store_grad(dha_ref, dba_ref, 2 * D_MODEL, dya * o * (sa * (1.0 + za * (1.0 - sa))))

        @pl.when(step == nsteps - 1)
        def _():
            cols = D_MODEL // N_DEV
            for j in range(N_DEV):
                poa_ref[j] = dwoa_ref[:, cols * j:cols * (j + 1)].astype(BF16)
                pob_ref[j] = dwob_ref[:, cols * j:cols * (j + 1)].astype(BF16)
                pout_ref[j] = dwout_ref[cols * j:cols * (j + 1), :].astype(BF16)

    row = lambda w, c=0: pl.BlockSpec((tm, w), lambda i, c=c: (i, c))
    full = lambda shape: pl.BlockSpec(shape, lambda i: (0, 0))
    full3 = lambda shape: pl.BlockSpec(shape, lambda i: (0, 0, 0))
    return pl.pallas_call(
        body, name="merge", grid=(nsteps,),
        in_specs=[row(D_MODEL), row(MLA_WIDTH), row(D_MODEL, 0), row(D_MODEL, 1), row(MLA_WIDTH, 4), row(SGU_WIDTH),
                  row(D_MODEL), full((MLA_WIDTH, D_MODEL)), full((SGU_WIDTH, D_MODEL)), full((D_MODEL, D_MODEL)),
                  full((1, D_MODEL)), full((1, D_MODEL))],
        out_specs=[full((1, LANES)), row(D_MODEL), row(SEG_A), row(MLA_WIDTH), row(SGU_WIDTH),
                   full3((N_DEV, MLA_WIDTH, D_MODEL // N_DEV)), full3((N_DEV, SGU_WIDTH, D_MODEL // N_DEV)),
                   full3((N_DEV, D_MODEL // N_DEV, D_MODEL)), full((1, D_MODEL)), full((1, D_MODEL)), full((1, SEG_A))],
        out_shape=[jax.ShapeDtypeStruct((1, LANES), F32),
                   jax.ShapeDtypeStruct((SEQ, D_MODEL), F32), jax.ShapeDtypeStruct((SEQ, SEG_A), BF16),
                   jax.ShapeDtypeStruct((SEQ, MLA_WIDTH), F32), jax.ShapeDtypeStruct((SEQ, SGU_WIDTH), F32),
                   jax.ShapeDtypeStruct((N_DEV, MLA_WIDTH, D_MODEL // N_DEV), BF16),
                   jax.ShapeDtypeStruct((N_DEV, SGU_WIDTH, D_MODEL // N_DEV), BF16),
                   jax.ShapeDtypeStruct((N_DEV, D_MODEL // N_DEV, D_MODEL), BF16),
                   jax.ShapeDtypeStruct((1, D_MODEL), F32), jax.ShapeDtypeStruct((1, D_MODEL), F32),
                   jax.ShapeDtypeStruct((1, SEG_A), F32)],
        scratch_shapes=[pltpu.VMEM((MLA_WIDTH, D_MODEL), F32), pltpu.VMEM((SGU_WIDTH, D_MODEL), F32),
                        pltpu.VMEM((D_MODEL, D_MODEL), F32)],
        compiler_params=pltpu.CompilerParams(dimension_semantics=("arbitrary",), vmem_limit_bytes=VMEM_BIG),
    )(x, o, h_a, h_a, h_a, y_b, target, w_oa, w_ob, w_out, ln_g, ln_b)


def _mla_bwd(dq, dk, dv, h_c, xt_bf, gq, gkv, wq, wkn, wv, c_t, sa_t, sb_t, parts):
    tm = 256
    hw = MLA_HEADS * HEAD_PAD
    npart = len(parts)
    nsteps = SEQ // tm

    def body(dq_ref, dk_ref, dv_ref, cq_ref, ckv_ref, xt_ref, gq_ref, gkv_ref, wq_ref, wkn_ref, wv_ref, c_ref, sa_ref,
             sb_ref, *rest):
        part_refs, rest = rest[:npart], rest[npart:]
        dhc_ref, puq_ref, dwkn_ref, dwv_ref, dgq_ref, dgkv_ref, dbc_ref, dwc_ref = rest[:8]
        land_refs, (pre_ref, dwq_ref, dwc_acc, send_sems, recv_sems, local_sems) = rest[8:8 + npart], rest[8 + npart:]
        exchange = _exchange_parts(part_refs, land_refs, send_sems, recv_sems, local_sems)
        _exchange_start(pl.program_id(0) == 0, exchange)
        _exchange_finish(pl.program_id(0) == nsteps - 1, exchange)

        @pl.when(pl.program_id(0) == 0)
        def _():
            for r in (dwq_ref, dwc_acc, dwkn_ref, dwv_ref, dgq_ref, dgkv_ref, dbc_ref):
                r[...] = jnp.zeros_like(r)

        c, sa, sb = c_ref[...], sa_ref[...], sb_ref[...]
        lane = lax.broadcasted_iota(jnp.int32, (tm, LANES), 1)
        rope_lanes = jnp.logical_and(lane >= ROPE_LO, lane < ROPE_HI)

        cq = cq_ref[...]
        gq = gq_ref[...]
        rq = lax.rsqrt(jnp.sum(cq * cq, axis=1, keepdims=True) * (1.0 / Q_LORA_RANK) + RMS_EPS)
        nq = cq * rq
        cqn_bf = (nq * gq).astype(BF16)
        for h in range(MLA_HEADS):
            sl = slice(HEAD_PAD * h, HEAD_PAD * (h + 1))
            pre_ref[:, sl] = _rope_t(dq_ref[:, sl] * ATTN_SCALE, c, sa, sb).astype(BF16)
        dqpre_bf = pre_ref[...]
        dcqn = _dot(dqpre_bf, wq_ref[...], _NT)
        dwq_ref[...] += _dot(cqn_bf, dqpre_bf, _TN)
        dgq_ref[...] += jnp.sum(dcqn * nq, axis=0, keepdims=True)
        dnq = dcqn * gq
        _store_grad(dhc_ref, dbc_ref, 0,
                    rq * (dnq - nq * (jnp.sum(dnq * nq, axis=1, keepdims=True) * (1.0 / Q_LORA_RANK))))

        ckv = ckv_ref[...]
        gkv = gkv_ref[...]
        rkv = lax.rsqrt(jnp.sum(ckv * ckv, axis=1, keepdims=True) * (1.0 / KV_LORA_RANK) + RMS_EPS)
        nkv = ckv * rkv
        ckvn_bf = (nkv * gkv).astype(BF16)
        dk = dk_ref[...]
        dk_bf = dk.astype(BF16)
        dv_bf = dv_ref[...].astype(BF16)
        dckvn = _dot(dk_bf, wkn_ref[...], _NT) + _dot(dv_bf, wv_ref[...], _NT)
        dwkn_ref[...] += _dot(ckvn_bf, dk_bf, _TN)
        dwv_ref[...] += _dot(ckvn_bf, dv_bf, _TN)
        dgkv_ref[...] += jnp.sum(dckvn * nkv, axis=0, keepdims=True)
        dnkv = dckvn * gkv
        _store_grad(dhc_ref, dbc_ref, CQ_PAD, rkv * (
            dnkv - nkv * (jnp.sum(dnkv * nkv, axis=1, keepdims=True) * (1.0 / KV_LORA_RANK))))
        dkpe = jnp.zeros((tm, LANES), F32)
        for h in range(MLA_HEADS):
            dkpe = dkpe + dk[:, HEAD_PAD * h:HEAD_PAD * (h + 1)]
        _store_grad(dhc_ref, dbc_ref, CQ_PAD + LANES, _rope_t(jnp.where(rope_lanes, dkpe, 0.0), c, sa, sb))
        dwc_acc[...] += _dot(xt_ref[...], dhc_ref[...], _NN)

        @pl.when(pl.program_id(0) == SEQ // tm - 1)
        def _():
            dwc_ref[...] = dwc_acc[...].astype(BF16)
            rows = Q_LORA_RANK // N_DEV
            for j in range(N_DEV):
                for h in range(MLA_HEADS):
                    puq_ref[j, :, QK_HEAD_DIM * h:QK_HEAD_DIM * (h + 1)] = dwq_ref[
                        rows * j:rows * (j + 1), HEAD_PAD * h:HEAD_PAD * h + QK_HEAD_DIM].astype(BF16)

    full = lambda shape: pl.BlockSpec(shape, lambda i: (0, 0))
    row = lambda w, c=0: pl.BlockSpec((tm, w), lambda i, c=c: (i, c))
    hbm = pl.BlockSpec(memory_space=pl.ANY)
    res = pl.pallas_call(
        body, name="mla_bwd", grid=(nsteps,),
        in_specs=[row(hw), row(hw), row(hw), row(CQ_PAD, 0), row(LANES, CQ_PAD // LANES),
                  pl.BlockSpec((D_MODEL, tm), lambda i: (0, i)),
                  full((1, CQ_PAD)), full((1, KV_LORA_RANK)), full((CQ_PAD, hw)), full((KV_LORA_RANK, hw)),
                  full((KV_LORA_RANK, hw)), row(LANES), row(LANES), row(LANES)] + [hbm] * npart,
        out_specs=[row(SEG_C), pl.BlockSpec((N_DEV, Q_LORA_RANK // N_DEV, MLA_HEADS * QK_HEAD_DIM), lambda i: (0, 0, 0)),
                   full((KV_LORA_RANK, hw)), full((KV_LORA_RANK, hw)),
                   full((1, CQ_PAD)), full((1, KV_LORA_RANK)), full((1, SEG_C)), full((D_MODEL, SEG_C))] + [hbm] * npart,
        out_shape=[jax.ShapeDtypeStruct((SEQ, SEG_C), BF16),
                   jax.ShapeDtypeStruct((N_DEV, Q_LORA_RANK // N_DEV, MLA_HEADS * QK_HEAD_DIM), BF16),
                   jax.ShapeDtypeStruct((KV_LORA_RANK, hw), F32), jax.ShapeDtypeStruct((KV_LORA_RANK, hw), F32),
                   jax.ShapeDtypeStruct((1, CQ_PAD), F32), jax.ShapeDtypeStruct((1, KV_LORA_RANK), F32),
                   jax.ShapeDtypeStruct((1, SEG_C), F32), jax.ShapeDtypeStruct((D_MODEL, SEG_C), BF16)]
        + [jax.ShapeDtypeStruct(p.shape, p.dtype) for p in parts],
        scratch_shapes=[pltpu.VMEM((tm, hw), BF16), pltpu.VMEM((CQ_PAD, hw), F32), pltpu.VMEM((D_MODEL, SEG_C), F32)]
        + _exchange_sems(npart),
        compiler_params=pltpu.CompilerParams(dimension_semantics=("arbitrary",), vmem_limit_bytes=VMEM_MID),
    )(dq, dk, dv, h_c, h_c, xt_bf, gq, gkv, wq, wkn, wv, c_t, sa_t, sb_t, *parts)
    return res[:8], res[8:]


def _adamw_all(ws, gs, ms, vs):
    n = len(ws)
    c1 = 1.0 / (1.0 - ADAM_B1 ** ADAM_STEP)
    c2 = 1.0 / (1.0 - ADAM_B2 ** ADAM_STEP)

    def body(*refs):
        for idx in range(n):
            w, g, m, v = (refs[idx][...], refs[n + idx][...], refs[2 * n + idx][...], refs[3 * n + idx][...])
            m_new = ADAM_B1 * m + (1.0 - ADAM_B1) * g
            v_new = ADAM_B2 * v + (1.0 - ADAM_B2) * (g * g)
            delta = -ADAM_LR * ((m_new * c1) / (jnp.sqrt(v_new * c2) + ADAM_EPS) + ADAM_WD * w)
            refs[4 * n + idx][...] = delta
            refs[5 * n + idx][...] = m_new
            refs[6 * n + idx][...] = v_new

    shapes = [jax.ShapeDtypeStruct(w.shape, F32) for w in ws]
    outs = pl.pallas_call(
        body, name="adamw", out_shape=shapes * 3,
        compiler_params=pltpu.CompilerParams(vmem_limit_bytes=VMEM_BIG),
    )(*ws, *gs, *ms, *vs)
    return outs[:n], outs[n:2 * n], outs[2 * n:]


SHARD_W = IN_WIDTH // N_DEV

_PIECES = [(0, 384, 2, 0), (384, 512, 2, CQ_PAD), (512, 544, 2, CQ_PAD + LANES + ROPE_LO),
           (544, 1056, 0, 2 * D_MODEL), (1056, 1568, 1, 0), (1568, 2080, 1, SGU_WIDTH),
           (2080, 2592, 1, 2 * SGU_WIDTH), (2592, 3616, 0, 0), (3616, 4640, 0, D_MODEL)]


def _column_runs():
    runs = []
    for n0, n1, seg, d0 in _PIECES:
        for j in range(N_DEV):
            lo, hi = max(n0, j * SHARD_W), min(n1, (j + 1) * SHARD_W)
            if lo < hi:
                runs.append((j, lo - j * SHARD_W, hi - j * SHARD_W, seg, d0 + lo - n0))
    return runs


def _mesh_pos():
    return lax.axis_index("x"), lax.axis_index("y"), lax.axis_index("c")


def _remote(src, dst, send_sems, recv_sems, k, to):
    return pltpu.make_async_remote_copy(src_ref=src, dst_ref=dst, send_sem=send_sems.at[k], recv_sem=recv_sems.at[k],
                                        device_id=to, device_id_type=pl.DeviceIdType.MESH)


def _gather_exchange(gats, send_sems, recv_sems, meanwhile=None):
    x, y, c = _mesh_pos()
    me, sibling = (x, y, c), (x, y, 1 - c)
    chips = [(1 - x, y), (x, 1 - y), (1 - x, 1 - y)]

    def copy(a, k, blk, to):
        slab = gats[a].at[4 * blk[0] + 2 * blk[1] + blk[2]]
        return _remote(slab, slab, send_sems, recv_sems, 7 * a + k, to)

    arrays = range(len(gats))
    first = [copy(a, 1 + j, me, (*chip, c)) for j, chip in enumerate(chips) for a in arrays]
    first += [copy(a, 0, me, sibling) for a in arrays]
    for cp in first:
        cp.start()
    if meanwhile is not None:
        meanwhile()
    passed = []
    for j, chip in enumerate(chips):
        for a in arrays:
            copy(a, 1 + j, (*chip, c), me).wait_recv()
            fwd = copy(a, 4 + j, (*chip, c), sibling)
            fwd.start()
            passed.append(fwd)
    for a in arrays:
        copy(a, 0, sibling, me).wait_recv()
    for j, chip in enumerate(chips):
        for a in arrays:
            copy(a, 4 + j, (*chip, 1 - c), me).wait_recv()
    for cp in first + passed:
        cp.wait_send()


def _gather_behind(own, gats, send_sems, recv_sems, local_sems, step, mid, last):
    x, y, c = _mesh_pos()
    me, sibling = (x, y, c), (x, y, 1 - c)
    chips = [(1 - x, y), (x, 1 - y), (1 - x, 1 - y)]
    arrays = range(len(gats))

    def copy(a, k, blk, to, src=None):
        slab = gats[a].at[4 * blk[0] + 2 * blk[1] + blk[2]]
        return _remote(slab if src is None else src, slab, send_sems, recv_sems, 7 * a + k, to)

    first = [copy(a, 1 + j, me, (*chip, c), src=own[a]) for j, chip in enumerate(chips) for a in arrays]
    first += [copy(a, 0, me, sibling, src=own[a]) for a in arrays]
    local = [pltpu.make_async_copy(own[a], gats[a].at[4 * x + 2 * y + c], local_sems.at[a]) for a in arrays]
    passed = [copy(a, 4 + j, (*chip, c), sibling) for j, chip in enumerate(chips) for a in arrays]

    @pl.when(step == 0)
    def _():
        for cp in first + local:
            cp.start()

    @pl.when(step == mid)
    def _():
        for j, chip in enumerate(chips):
            for a in arrays:
                copy(a, 1 + j, (*chip, c), me).wait_recv()
        for cp in passed:
            cp.start()

    @pl.when(step == last)
    def _():
        for a in arrays:
            copy(a, 0, sibling, me).wait_recv()
        for j, chip in enumerate(chips):
            for a in arrays:
                copy(a, 4 + j, (*chip, 1 - c), me).wait_recv()
        for cp in first + passed:
            cp.wait_send()
        for cp in local:
            cp.wait()


def _gather_first(w_in, w_uq2, w_oa, w_ob, w_out, x2, pos_col, invf_lane):
    hw = MLA_HEADS * HEAD_PAD
    uq_rows = Q_LORA_RANK // N_DEV
    rows = 256

    def body(win_ref, wuq_ref, woa_ref, wob_ref, wout_ref, x_ref, pos_ref, invf_ref,
             wc_ref, wq_ref, winb_ref, oab_ref, obb_ref, outb_ref, xb_ref, xt_ref, c_ref, sa_ref, sb_ref,
             g_uq, blk0, send_sems, recv_sems):
        def local_work():
            for i in range(SEQ // rows):
                xi = x_ref[rows * i:rows * (i + 1), :]
                xb_ref[rows * i:rows * (i + 1), :] = xi.astype(BF16)
                xt_ref[:, rows * i:rows * (i + 1)] = xi.T.astype(BF16)
            ang = pos_ref[...].astype(F32) * invf_ref[...]
            cs, sn = jnp.cos(ang), jnp.sin(ang)
            lane = lax.broadcasted_iota(jnp.int32, ang.shape, 1)
            c_ref[...] = jnp.where(lane < ROPE_LO, 1.0, jnp.where(lane < ROPE_HI, cs, 0.0))
            sa_ref[...] = jnp.where(jnp.logical_and(lane >= ROPE_LO, lane < ROPE_MID), -sn, 0.0)
            sb_ref[...] = jnp.where(jnp.logical_and(lane >= ROPE_MID, lane < ROPE_HI), sn, 0.0)

        x, y, c = _mesh_pos()
        me = (x, y, c)
        winb_ref[...] = win_ref[0].astype(BF16)
        oab_ref[...] = woa_ref[0].astype(BF16)
        obb_ref[...] = wob_ref[0].astype(BF16)
        outb_ref[...] = wout_ref[0].astype(BF16)
        g_uq[4 * x + 2 * y + c] = wuq_ref[...].astype(BF16)

        chip0 = jnp.logical_and(x == 0, y == 0)
        south = c == 0
        half = D_MODEL // 2
        halves = [blk0.at[pl.ds(0, half)], blk0.at[pl.ds(half, half)]]

        def bcopy(k, to, part=None):
            ref = blk0 if part is None else halves[part]
            return _remote(ref, ref, send_sems, recv_sems, 7 + k, to)

        sends0 = [(0, (0, 0, 1), None), (1, (1, 0, 0), 0), (2, (0, 1, 0), 1), (3, (1, 0, 0), 1), (4, (0, 1, 0), 0)]

        @pl.when(jnp.logical_and(chip0, south))
        def _():
            blk0[...] = winb_ref[...]
            for k, to, part in sends0:
                bcopy(k, to, part).start()

        _gather_exchange([g_uq], send_sems, recv_sems, meanwhile=local_work)

        for (cx, cy), first_k, first_half, second_k in (((1, 0), 1, 0, 3), ((0, 1), 2, 1, 4)):
            @pl.when(jnp.logical_and(jnp.logical_and(x == cx, y == cy), south))
            def _(cx=cx, cy=cy, first_k=first_k, first_half=first_half, second_k=second_k):
                bcopy(first_k, me, first_half).wait_recv()
                onward = bcopy(5 + first_half, (1, 1, 0), first_half)
                onward.start()
                bcopy(second_k, me, 1 - first_half).wait_recv()
                north = bcopy(7, (cx, cy, 1))
                north.start()
                onward.wait_send()
                north.wait_send()

        @pl.when(jnp.logical_and(jnp.logical_and(x == 1, y == 1), south))
        def _():
            bcopy(5, me, 0).wait_recv()
            bcopy(6, me, 1).wait_recv()
            north = bcopy(7, (1, 1, 1))
            north.start()
            north.wait_send()

        @pl.when(jnp.logical_and(chip0, c == 1))
        def _():
            bcopy(0, me).wait_recv()

        @pl.when(jnp.logical_and(jnp.logical_not(chip0), c == 1))
        def _():
            bcopy(7, me).wait_recv()

        @pl.when(jnp.logical_and(chip0, south))
        def _():
            for k, to, part in sends0:
                bcopy(k, to, part).wait_send()

        for j, s0, s1, seg, d0 in _column_runs():
            if seg == 2:
                wc_ref[:, d0:d0 + (s1 - s0)] = blk0[:, s0:s1]
        zeros = lambda r, w: jnp.zeros((r, w), BF16)
        wc_ref[:, Q_LORA_RANK:CQ_PAD] = zeros(D_MODEL, CQ_PAD - Q_LORA_RANK)
        wc_ref[:, CQ_PAD + LANES:CQ_PAD + LANES + ROPE_LO] = zeros(D_MODEL, ROPE_LO)
        wc_ref[:, CQ_PAD + LANES + ROPE_HI:SEG_C] = zeros(D_MODEL, LANES - ROPE_HI)
        wq_ref[Q_LORA_RANK:CQ_PAD, :] = zeros(CQ_PAD - Q_LORA_RANK, hw)
        for h in range(MLA_HEADS):
            wq_ref[0:Q_LORA_RANK, HEAD_PAD * h + QK_HEAD_DIM:HEAD_PAD * (h + 1)] = zeros(Q_LORA_RANK, HEAD_PAD - QK_HEAD_DIM)
        for j in range(N_DEV):
            for h in range(MLA_HEADS):
                wq_ref[uq_rows * j:uq_rows * (j + 1), HEAD_PAD * h:HEAD_PAD * h + QK_HEAD_DIM] = g_uq[
                    j, :, QK_HEAD_DIM * h:QK_HEAD_DIM * (h + 1)]

    vmem = pl.BlockSpec(memory_space=pltpu.VMEM)
    return pl.pallas_call(
        body, name="gather_first",
        out_shape=[jax.ShapeDtypeStruct((D_MODEL, SEG_C), BF16), jax.ShapeDtypeStruct((CQ_PAD, hw), BF16),
                   jax.ShapeDtypeStruct(w_in.shape[1:], BF16), jax.ShapeDtypeStruct(w_oa.shape[1:], BF16),
                   jax.ShapeDtypeStruct(w_ob.shape[1:], BF16), jax.ShapeDtypeStruct(w_out.shape[1:], BF16),
                   jax.ShapeDtypeStruct((SEQ, D_MODEL), BF16), jax.ShapeDtypeStruct((D_MODEL, SEQ), BF16)]
        + [jax.ShapeDtypeStruct((SEQ, LANES), F32)] * 3,
        in_specs=[vmem] * 8, out_specs=[vmem] * 11,
        scratch_shapes=[pltpu.VMEM((N_DEV, uq_rows, MLA_HEADS * QK_HEAD_DIM), BF16), pltpu.VMEM((D_MODEL, SHARD_W), BF16),
                        pltpu.SemaphoreType.DMA((15,)), pltpu.SemaphoreType.DMA((15,))],
        compiler_params=pltpu.CompilerParams(vmem_limit_bytes=VMEM_BIG),
    )(w_in, w_uq2, w_oa, w_ob, w_out, x2, pos_col, invf_lane)


def _assemble_in(g_in):
    def body(g_ref, wa_ref, wb_ref):
        segs = [wa_ref, wb_ref]
        for j, s0, s1, seg, d0 in _column_runs():
            if seg < 2:
                segs[seg][:, d0:d0 + (s1 - s0)] = g_ref[j, :, s0:s1]

    return pl.pallas_call(
        body, name="assemble_in",
        out_shape=[jax.ShapeDtypeStruct((D_MODEL, SEG_A), BF16), jax.ShapeDtypeStruct((D_MODEL, SEG_B), BF16)],
        compiler_params=pltpu.CompilerParams(vmem_limit_bytes=VMEM_MID),
    )(g_in)


def _assemble_out(g_oa, g_ob, g_out):
    cols = D_MODEL // N_DEV

    def body(goa_ref, gob_ref, gout_ref, oa_ref, ob_ref, out_ref):
        for j in range(N_DEV):
            oa_ref[:, cols * j:cols * (j + 1)] = goa_ref[j]
            ob_ref[:, cols * j:cols * (j + 1)] = gob_ref[j]
            out_ref[cols * j:cols * (j + 1), :] = gout_ref[j]

    return pl.pallas_call(
        body, name="assemble_out",
        out_shape=[jax.ShapeDtypeStruct((MLA_WIDTH, D_MODEL), BF16), jax.ShapeDtypeStruct((SGU_WIDTH, D_MODEL), BF16),
                   jax.ShapeDtypeStruct((D_MODEL, D_MODEL), BF16)],
    )(g_oa, g_ob, g_out)


C_NAT = 544


P_IN_SPLIT = 896


def _to_parts(dwa, dwb):
    def body(dwa_ref, dwb_ref, phi_ref, plo_ref):
        phi_ref[0, :, 0:C_NAT] = jnp.zeros((P_IN_SPLIT, C_NAT), BF16)
        plo_ref[0, :, 0:C_NAT] = jnp.zeros((D_MODEL - P_IN_SPLIT, C_NAT), BF16)
        segs = [dwa_ref, dwb_ref]
        for j, s0, s1, seg, d0 in _column_runs():
            if seg < 2:
                phi_ref[j, :, s0:s1] = segs[seg][0:P_IN_SPLIT, d0:d0 + (s1 - s0)]
                plo_ref[j, :, s0:s1] = segs[seg][P_IN_SPLIT:D_MODEL, d0:d0 + (s1 - s0)]

    return pl.pallas_call(
        body, name="to_parts",
        out_shape=[jax.ShapeDtypeStruct((N_DEV, P_IN_SPLIT, SHARD_W), BF16),
                   jax.ShapeDtypeStruct((N_DEV, D_MODEL - P_IN_SPLIT, SHARD_W), BF16)],
        compiler_params=pltpu.CompilerParams(vmem_limit_bytes=VMEM_MID))(dwa, dwb)


def _dx_tail(dhs, ws, dx_res, dwc, p_uq, p_rep):
    ntile, sums_at = 8, 5
    tm = SEQ // ntile
    rep_rows = p_rep.shape[1]
    c_rows = D_MODEL // N_DEV
    spec = [((c_rows, C_NAT), BF16), (p_uq.shape[1:], BF16), ((rep_rows, LANES), F32)]
    n = len(spec)

    nseg = len(dhs)

    def body(*refs):
        dh_refs, w_refs = refs[:nseg], refs[nseg:2 * nseg]
        dxr_ref, dwc_ref, puq_ref, prep_ref, dx_ref, call_ref, guq_ref, repall_ref, pc_ref, c_all, rep_all = refs[
            2 * nseg:2 * nseg + 11]
        rest = refs[2 * nseg + 11:]
        ras, tbs, rbs = rest[0:n], rest[n:2 * n], rest[2 * n:3 * n]
        send_sems, recv_sems, gsend, grecv = rest[3 * n:]
        step = pl.program_id(0)
        x, y, c = _mesh_pos()
        me_idx = 4 * x + 2 * y + c
        me, sibling = (x, y, c), (x, y, 1 - c)
        others = [(1 - x, y), (x, 1 - y), (1 - x, 1 - y)]
        parts = [pc_ref, puq_ref, prep_ref]
        gats = [rep_all, c_all]

        def stage1(chip, a):
            return _remote(parts[a].at[2 * chip + (1 - c)], ras[a].at[chip], send_sems, recv_sems, 7 * a + chip, sibling)

        def stage2(k, a):
            cx, cy = others[k]
            return _remote(tbs[a].at[k], rbs[a].at[k], send_sems, recv_sems, 7 * a + 4 + k, (cx, cy, c))

        def gcopy(a, k, blk, to):
            slab = gats[a].at[4 * blk[0] + 2 * blk[1] + blk[2]]
            return _remote(slab, slab, gsend, grecv, 7 * a + k, to)

        def chip_sum(a, chip):
            return parts[a][2 * chip + c].astype(F32) + ras[a][chip].astype(F32)

        @pl.when(step == 0)
        def _():
            for j, s0, s1, seg, d0 in _column_runs():
                if seg == 2:
                    for r in range(N_DEV):
                        pc_ref[r, :, s0:s1] = dwc_ref[c_rows * r:c_rows * (r + 1), d0:d0 + (s1 - s0)]
            for chip in range(4):
                for a in range(n):
                    stage1(chip, a).start()

        @pl.when(step == 1)
        def _():
            for chip in range(4):
                for a in range(n):
                    stage1(chip, a).wait_recv()
            for k, (cx, cy) in enumerate(others):
                for a in range(n):
                    tbs[a][k] = chip_sum(a, 2 * cx + cy).astype(spec[a][1])
                    stage2(k, a).start()

        @pl.when(step == sums_at)
        def _():
            for k in range(3):
                for a in range(n):
                    stage2(k, a).wait_recv()
            sums = []
            for a in range(n):
                acc = chip_sum(a, 2 * x + y)
                for k in range(3):
                    acc = acc + rbs[a][k].astype(F32)
                sums.append(acc)
            c_all[me_idx] = sums[0].astype(BF16)
            guq_ref[...] = sums[1]
            rep_all[me_idx] = sums[2]
            for a in range(2):
                for j, chip in enumerate(others):
                    gcopy(a, 1 + j, me, (*chip, c)).start()
                gcopy(a, 0, me, sibling).start()

        acc = dxr_ref[...]
        for dh_ref, w_ref in zip(dh_refs, w_refs):
            acc = acc + _dot(dh_ref[...], w_ref[...], _NT)
        dx_ref[...] = acc

        @pl.when(step == ntile - 1)
        def _():
            for j, chip in enumerate(others):
                for a in range(2):
                    gcopy(a, 1 + j, (*chip, c), me).wait_recv()
                    gcopy(a, 4 + j, (*chip, c), sibling).start()
            for a in range(2):
                gcopy(a, 0, sibling, me).wait_recv()
                for j, chip in enumerate(others):
                    gcopy(a, 4 + j, (*chip, 1 - c), me).wait_recv()
            for a in range(2):
                gcopy(a, 0, me, sibling).wait_send()
                for j, chip in enumerate(others):
                    gcopy(a, 1 + j, me, (*chip, c)).wait_send()
                    gcopy(a, 4 + j, (*chip, c), sibling).wait_send()
            for a in range(n):
                for chip in range(4):
                    stage1(chip, a).wait_send()
                for k in range(3):
                    stage2(k, a).wait_send()
            call_ref[...] = c_all[...]
            repall_ref[...] = rep_all[...]

    row = lambda w: pl.BlockSpec((tm, w), lambda i: (i, 0))
    full = lambda shape: pl.BlockSpec(shape, lambda i: (0,) * len(shape))
    scratch = [pltpu.VMEM((N_DEV, c_rows, C_NAT), BF16), pltpu.VMEM((N_DEV, c_rows, C_NAT), BF16),
               pltpu.VMEM((N_DEV, rep_rows, LANES), F32)]
    for lead in (4, 3, 3):
        scratch += [pltpu.VMEM((lead,) + tuple(shape), dt) for shape, dt in spec]
    scratch += [pltpu.SemaphoreType.DMA((7 * n,)), pltpu.SemaphoreType.DMA((7 * n,)),
                pltpu.SemaphoreType.DMA((14,)), pltpu.SemaphoreType.DMA((14,))]
    return pl.pallas_call(
        body, name="dx_tail", grid=(SEQ // tm,),
        in_specs=[row(dh.shape[1]) for dh in dhs] + [full(w.shape) for w in ws]
        + [row(D_MODEL), full(dwc.shape), full(p_uq.shape), full(p_rep.shape)],
        out_specs=[row(D_MODEL), full((N_DEV, c_rows, C_NAT)), full(p_uq.shape[1:]), full((N_DEV, rep_rows, LANES))],
        out_shape=[jax.ShapeDtypeStruct((SEQ, D_MODEL), F32), jax.ShapeDtypeStruct((N_DEV, c_rows, C_NAT), BF16),
                   jax.ShapeDtypeStruct(p_uq.shape[1:], F32), jax.ShapeDtypeStruct((N_DEV, rep_rows, LANES), F32)],
        scratch_shapes=scratch,
        compiler_params=pltpu.CompilerParams(dimension_semantics=("arbitrary",), vmem_limit_bytes=VMEM_BIG),
    )(*dhs, *ws, dx_res, dwc, p_uq, p_rep)


def _sum_landed(landed, c_all):
    c_rows = D_MODEL // N_DEV

    def body(rhi_ref, rlo_ref, roa_ref, rob_ref, rout_ref, call_ref, gin_ref, goa_ref, gob_ref, gout_ref):
        def total(ref, sl):
            acc = ref[0, sl, :].astype(F32)
            for s in range(1, N_DEV):
                acc = acc + ref[s, sl, :].astype(F32)
            return acc

        x, y, c = _mesh_pos()
        dev0 = jnp.where(4 * x + 2 * y + c == 0, 1.0, 0.0)
        for j in range(N_DEV):
            sl = slice(c_rows * j, c_rows * (j + 1))
            below = c_rows * j < P_IN_SPLIT
            tot = total(rhi_ref, sl) if below else total(rlo_ref, slice(c_rows * j - P_IN_SPLIT, c_rows * (j + 1) - P_IN_SPLIT))
            gin_ref[0, sl, C_NAT:SHARD_W] = tot[:, C_NAT:SHARD_W]
            gin_ref[0, sl, 0:C_NAT] = tot[:, 0:C_NAT] + dev0 * call_ref[j].astype(F32)
        goa_ref[0] = total(roa_ref, slice(None))
        gob_ref[0] = total(rob_ref, slice(None))
        gout_ref[0] = total(rout_ref, slice(None))

    return pl.pallas_call(
        body, name="sum_landed",
        out_shape=[jax.ShapeDtypeStruct((1, D_MODEL, SHARD_W), F32)]
        + [jax.ShapeDtypeStruct((1,) + r.shape[1:], F32) for r in landed[2:]],
        compiler_params=pltpu.CompilerParams(vmem_limit_bytes=VMEM_MID),
    )(*landed, c_all)


_O_CQ, _O_CKV, _O_KPE, _O_ZA, _O_U, _O_V, _O_ZB, _O_GA, _O_GB = 0, 384, 512, 544, 1056, 1568, 2080, 2592, 3616


def _to_segments(w):
    z = lambda n: jnp.zeros(w.shape[:-1] + (n,), w.dtype)
    seg_a = jnp.concatenate([w[..., _O_GA:_O_GB], w[..., _O_GB:IN_WIDTH], w[..., _O_ZA:_O_U]], axis=-1)
    seg_b = jnp.concatenate([w[..., _O_U:_O_V], w[..., _O_V:_O_ZB], w[..., _O_ZB:_O_GA]], axis=-1)
    seg_c = jnp.concatenate([w[..., _O_CQ:_O_CKV], z(CQ_PAD - Q_LORA_RANK), w[..., _O_CKV:_O_KPE],
                             z(ROPE_LO), w[..., _O_KPE:_O_ZA], z(LANES - ROPE_HI)], axis=-1)
    return seg_a, seg_b, seg_c


def _from_segments(seg_a, seg_b, seg_c):
    kpe0 = CQ_PAD + LANES + ROPE_LO
    return jnp.concatenate([
        seg_c[..., 0:Q_LORA_RANK], seg_c[..., CQ_PAD:CQ_PAD + LANES], seg_c[..., kpe0:kpe0 + QK_ROPE_DIM],
        seg_a[..., 2 * D_MODEL:SEG_A], seg_b, seg_a[..., 0:2 * D_MODEL]], axis=-1)


def kernel(x, positions, w_in, b_in, g_q, w_uq, g_kv, w_ukv, w_oa, sgu_ln_g, sgu_ln_b, w_s, b_s, w_ob, w_out, ln_g, ln_b, loss_target, m_w_in, m_b_in, m_g_q, m_w_uq, m_g_kv, m_w_ukv, m_w_oa, m_sgu_ln_g, m_sgu_ln_b, m_w_s, m_b_s, m_w_ob, m_w_out, m_ln_g, m_ln_b, v_w_in, v_b_in, v_g_q, v_w_uq, v_g_kv, v_w_ukv, v_w_oa, v_sgu_ln_g, v_sgu_ln_b, v_w_s, v_b_s, v_w_ob, v_w_out, v_ln_g, v_ln_b):
    w_uq2 = w_uq[0].reshape(Q_LORA_RANK // N_DEV, MLA_HEADS * QK_HEAD_DIM)
    inv_freq = ROPE_THETA ** (-jnp.arange(0, QK_ROPE_DIM, 2, dtype=F32) / QK_ROPE_DIM)
    invf_lane = jnp.concatenate([jnp.zeros((ROPE_LO,), F32), inv_freq, inv_freq,
                                 jnp.zeros((LANES - ROPE_HI,), F32)]).reshape(1, LANES)
    first = _gather_first(w_in, w_uq2, w_oa, w_ob, w_out, x[0], positions.reshape(SEQ, 1), invf_lane)
    partials = _local_step(x[0], loss_target[0], first, b_in, g_q, g_kv, w_ukv, sgu_ln_g, sgu_ln_b, w_s, b_s, ln_g, ln_b)
    weights = dict(w_in=w_in, b_in=b_in, g_q=g_q, w_uq=w_uq, g_kv=g_kv, w_ukv=w_ukv, w_oa=w_oa, sgu_ln_g=sgu_ln_g,
                   sgu_ln_b=sgu_ln_b, w_s=w_s, b_s=b_s, w_ob=w_ob, w_out=w_out, ln_g=ln_g, ln_b=ln_b)
    moms = dict(w_in=m_w_in, b_in=m_b_in, g_q=m_g_q, w_uq=m_w_uq, g_kv=m_g_kv, w_ukv=m_w_ukv, w_oa=m_w_oa,
                sgu_ln_g=m_sgu_ln_g, sgu_ln_b=m_sgu_ln_b, w_s=m_w_s, b_s=m_b_s, w_ob=m_w_ob, w_out=m_w_out,
                ln_g=m_ln_g, ln_b=m_ln_b)
    vars_ = dict(w_in=v_w_in, b_in=v_b_in, g_q=v_g_q, w_uq=v_w_uq, g_kv=v_g_kv, w_ukv=v_w_ukv, w_oa=v_w_oa,
                 sgu_ln_g=v_sgu_ln_g, sgu_ln_b=v_sgu_ln_b, w_s=v_w_s, b_s=v_b_s, w_ob=v_w_ob, w_out=v_w_out,
                 ln_g=v_ln_g, ln_b=v_ln_b)
    return _reduce_and_update(partials, weights, moms, vars_)


def _local_step(x2, tgt, first, b_in, g_q, g_kv, w_ukv, sgu_ln_g, sgu_ln_b, w_s, b_s, ln_g, ln_b):
    wc, wq, win_b, oa_b, ob_b, out_b, x_bf, xt_bf, c_t, sa_t, sb_t = first
    ba, bb, bc = _to_segments(b_in)
    w_ukv_bf = w_ukv[0].astype(BF16)
    wkn = jnp.pad(w_ukv_bf[:, :, :QK_NOPE_DIM], ((0, 0), (0, 0), (0, HEAD_PAD - QK_NOPE_DIM))).reshape(KV_LORA_RANK, -1)
    wv = jnp.pad(w_ukv_bf[:, :, QK_NOPE_DIM:], ((0, 0), (0, 0), (0, HEAD_PAD - V_HEAD_DIM))).reshape(KV_LORA_RANK, -1)
    gq = jnp.pad(g_q, ((0, 0), (0, CQ_PAD - Q_LORA_RANK)))
    bias_full = jnp.repeat(b_s[0].T, SGU_GROUP_DIM, axis=1)
    w_s3 = w_s[0]
    w_st3 = jnp.swapaxes(w_s3, 1, 2)

    h_c = _mm(x_bf, wc, bias=bc, tm=512, tn=SEG_C, name="in_proj_c")
    q, k, kt, vx, vxt = _mla_prep(h_c, gq, g_kv, wq, wkn, wv, c_t, sa_t, sb_t)
    o, lse, (g_in,) = _attn_fwd(q, kt, vx, (win_b,))
    wa, wb = _assemble_in(g_in)
    h_a, (g_out,) = _mm(x_bf, wa, bias=ba, own=(out_b,), tm=512, tn=SEG_A // 2, name="in_proj_a")
    h_b, (g_oa, g_ob) = _mm(x_bf, wb, bias=bb, own=(oa_b, ob_b), tm=512, tn=SEG_B // 2, name="in_proj_b")
    y_b = _sgu_fwd(h_b, sgu_ln_g, sgu_ln_b, w_s3, bias_full)
    w_oa_f, w_ob_f, w_out_f = _assemble_out(g_oa, g_ob, g_out)

    (loss_row, dx_res, dh_a, d_o, d_yb, p_oa, p_ob, p_out, d_lng, d_lnb, d_ba) = _merge(
        x2, o, h_a, y_b, tgt, w_oa_f, w_ob_f, w_out_f, ln_g, ln_b)
    (dh_b, d_ws, d_bs_t, d_slg, d_slb, d_bb), (r_out,) = _sgu_bwd(h_b, d_yb, sgu_ln_g, sgu_ln_b, w_s3, w_st3, bias_full,
                                                                 (p_out,))
    d_wa, (r_oa,) = _mm(xt_bf, dh_a, out_dtype=BF16, parts=(p_oa,), tm=512, tn=512, name="dw_in_a")
    d_wb = _mm(xt_bf, dh_b, out_dtype=BF16, tm=512, tn=512, name="dw_in_b")
    p_hi, p_lo = _to_parts(d_wa, d_wb)
    dq, dk, dv, (r_hi,) = _attn_bwd(q, kt, k, vxt, d_o, o, lse, (p_hi,))
    (dh_c, p_uq, d_wkn, d_wv, d_gq, d_gkv, d_bc, d_wc), (r_lo, r_ob) = _mla_bwd(
        dq, dk, dv, h_c, xt_bf, gq, g_kv, wq, wkn, wv, c_t, sa_t, sb_t, (p_lo, p_ob))
    landed = (r_hi, r_lo, r_oa, r_ob, r_out)

    p_b_in = _from_segments(d_ba, d_bb, d_bc)
    p_w_ukv = jnp.concatenate([d_wkn.reshape(KV_LORA_RANK, MLA_HEADS, HEAD_PAD)[:, :, :QK_NOPE_DIM],
                               d_wv.reshape(KV_LORA_RANK, MLA_HEADS, HEAD_PAD)[:, :, :V_HEAD_DIM]], axis=-1)
    p_g_q = d_gq[:, :Q_LORA_RANK]
    p_b_s = d_bs_t[:, :SGU_GROUPS].T
    replicated = [p_b_in, p_g_q, d_gkv, p_w_ukv, d_slg, d_slb, d_ws, p_b_s, d_lng, d_lnb]
    return loss_row, ((dh_a, dh_b, dh_c), (wa, wb, wc), dx_res), landed, d_wc, p_uq, replicated


_NAMES = ["w_in", "b_in", "g_q", "w_uq", "g_kv", "w_ukv", "w_oa", "sgu_ln_g", "sgu_ln_b", "w_s", "b_s", "w_ob",
          "w_out", "ln_g", "ln_b"]
_REPLICATED = ["b_in", "g_q", "g_kv", "w_ukv", "sgu_ln_g", "sgu_ln_b", "w_s", "b_s", "ln_g", "ln_b"]


def _reduce_and_update(partials, weights, moms, vars_):
    loss_row, (dhs, ws, dx_res), landed, d_wc, p_uq, replicated = partials
    def piece(a):
        flat = a.reshape(-1)
        return jnp.pad(flat, (0, -flat.size % PACK_ALIGN))

    rep_flat = jnp.concatenate([piece(a) for a in replicated] + [piece(loss_row[0, :1])])
    rep_flat = jnp.pad(rep_flat, (0, N_DEV * PACK_R_ROWS * LANES - rep_flat.size))
    dx, c_all, g_uq, rep_all = _dx_tail(dhs, ws, dx_res, d_wc, p_uq, rep_flat.reshape(N_DEV, PACK_R_ROWS, LANES))
    g_in, g_oa, g_ob, g_out = _sum_landed(landed, c_all)
    rep_sum = rep_all.reshape(-1)
    grads, pos = dict(w_in=g_in, w_uq=g_uq, w_oa=g_oa, w_ob=g_ob, w_out=g_out), 0
    for nm in _REPLICATED:
        grads[nm] = rep_sum[pos:pos + weights[nm].size]
        pos += weights[nm].size + -weights[nm].size % PACK_ALIGN
    loss = rep_sum[pos]
    grads = {nm: grads[nm].reshape(weights[nm].shape) for nm in _NAMES}
    deltas, new_m, new_v = _adamw_all([weights[nm] for nm in _NAMES], [grads[nm] for nm in _NAMES],
                                      [moms[nm] for nm in _NAMES], [vars_[nm] for nm in _NAMES])
    return (loss, dx.reshape(1, SEQ, D_MODEL), *[grads[nm] for nm in _NAMES], *deltas, *new_m, *new_v)
```

```python
import math

import jax
import jax.numpy as jnp
from jax import lax
from jax.experimental import pallas as pl
from jax.experimental.pallas import tpu as pltpu

F32 = jnp.float32
BF16 = jnp.bfloat16

D_MODEL = 1024
SEQ = 2048
N_DEV = 8
MLA_HEADS = 8
Q_LORA_RANK = 384
KV_LORA_RANK = 128
QK_NOPE_DIM = 64
QK_ROPE_DIM = 32
V_HEAD_DIM = 64
QK_HEAD_DIM = QK_NOPE_DIM + QK_ROPE_DIM
MLA_WIDTH = MLA_HEADS * V_HEAD_DIM
ROPE_THETA = 10000.0
SGU_GROUPS = 8
SGU_GROUP_DIM = 64
SGU_WIDTH = SGU_GROUPS * SGU_GROUP_DIM
CHUNK = 128
RMS_EPS = 1e-6
LN_EPS = 1e-5
DN_ALPHA = 2.0 ** 0.25
IN_WIDTH = 4640
ATTN_SCALE = QK_HEAD_DIM ** -0.5

ADAM_LR = 0.001
ADAM_B1 = 0.9
ADAM_B2 = 0.999
ADAM_EPS = 1e-08
ADAM_WD = 0.01
ADAM_STEP = 10

LANES = 128
HEAD_PAD = 128
ROPE_LO = QK_NOPE_DIM
ROPE_MID = ROPE_LO + QK_ROPE_DIM // 2
ROPE_HI = ROPE_LO + QK_ROPE_DIM
CQ_PAD = 512

SEG_A = 2560
SEG_B = 1536
SEG_C = 768

PACK_R_ROWS = 272
PACK_ALIGN = 8 * LANES
VMEM_BIG = 56 * 1024 * 1024
VMEM_MID = 40 * 1024 * 1024


def _sigmoid(x):
    return 1.0 / (1.0 + jnp.exp(-x))


def _gelu_and_grad(x):
    c0 = math.sqrt(2.0 / math.pi)
    x2 = x * x
    t = jnp.tanh(c0 * (x + 0.044715 * x * x2))
    g = 0.5 * x * (1.0 + t)
    dg = 0.5 * (1.0 + t) + 0.5 * x * (1.0 - t * t) * (c0 * (1.0 + 3.0 * 0.044715 * x2))
    return g, dg


def _dot(a, b, dims):
    return lax.dot_general(a, b, (dims, ((), ())), preferred_element_type=F32)


_NN = ((1,), (0,))
_NT = ((1,), (1,))
_TN = ((0,), (0,))


def _store_grad(dh_ref, db_ref, col, val):
    cols = slice(col, col + val.shape[1])
    dh_ref[:, cols] = val.astype(BF16)
    db_ref[:, cols] += jnp.sum(val, axis=0, keepdims=True)


def _mm(a, b, *, tb=False, bias=None, add=None, out_dtype=F32, own=(), parts=(), tm, tn, name):
    m, k = a.shape
    n = b.shape[0] if tb else b.shape[1]
    assert m % tm == 0 and n % tn == 0 and not (own and parts)
    dims = _NT if tb else _NN
    nown = len(own) + len(parts)
    nm = m // tm
    nsteps = (n // tn) * nm

    def body(*refs):
        a_ref, b_ref = refs[0], refs[1]
        pos = 2
        r = _dot(a_ref[...], b_ref[...], dims)
        if bias is not None:
            r = r + refs[pos][...]; pos += 1
        if add is not None:
            r = r + refs[pos][...]; pos += 1
        own_refs = refs[pos:pos + nown]; pos += nown
        refs[pos][...] = r.astype(out_dtype)
        if nown:
            gat_refs = refs[pos + 1:pos + 1 + nown]
            send_sems, recv_sems, local_sems = refs[pos + 1 + nown:]
            step = pl.program_id(0) * nm + pl.program_id(1)
            if own:
                _gather_behind(own_refs, gat_refs, send_sems, recv_sems, local_sems, step, nsteps - 2, nsteps - 1)
            else:
                exchange = _exchange_parts(own_refs, gat_refs, send_sems, recv_sems, local_sems)
                _exchange_start(step == 0, exchange)
                _exchange_finish(step == nsteps - 1, exchange)

    b_spec = pl.BlockSpec((tn, k), lambda j, i: (j, 0)) if tb else pl.BlockSpec((k, tn), lambda j, i: (0, j))
    in_specs, args = [pl.BlockSpec((tm, k), lambda j, i: (i, 0)), b_spec], [a, b]
    if bias is not None:
        in_specs.append(pl.BlockSpec((1, tn), lambda j, i: (0, j))); args.append(bias)
    if add is not None:
        in_specs.append(pl.BlockSpec((tm, tn), lambda j, i: (i, j))); args.append(add)
    hbm = pl.BlockSpec(memory_space=pl.ANY)
    res = pl.pallas_call(
        body, name=name, grid=(n // tn, nm), in_specs=in_specs + [hbm] * nown,
        out_specs=[pl.BlockSpec((tm, tn), lambda j, i: (i, j))] + [hbm] * nown,
        out_shape=[jax.ShapeDtypeStruct((m, n), out_dtype)]
        + [jax.ShapeDtypeStruct((N_DEV,) + o.shape, o.dtype) for o in own]
        + [jax.ShapeDtypeStruct(p.shape, p.dtype) for p in parts],
        scratch_shapes=_exchange_sems(nown) if nown else [],
        compiler_params=pltpu.CompilerParams(dimension_semantics=("arbitrary", "arbitrary"), vmem_limit_bytes=VMEM_BIG),
    )(*args, *own, *parts)
    return (res[0], res[1:]) if nown else res[0]


def _rope(x, c, sa, sb):
    return x * c + pltpu.roll(x, LANES - 16, 1) * sa + pltpu.roll(x, 16, 1) * sb


def _rope_t(dy, c, sa, sb):
    return dy * c + pltpu.roll(dy * sa, 16, 1) + pltpu.roll(dy * sb, LANES - 16, 1)


def _mla_prep(h_c, gq, gkv, wq, wkn, wvx, c_t, sa_t, sb_t):
    tm = 256
    hw = MLA_HEADS * HEAD_PAD

    def body(cq_ref, ckv_ref, kpe_ref, gq_ref, gkv_ref, wq_ref, wkn_ref, wvx_ref, c_ref, sa_ref, sb_ref,
             q_ref, k_ref, kt_ref, vx_ref, vxt_ref):
        c, sa, sb = c_ref[...], sa_ref[...], sb_ref[...]
        cq = cq_ref[...]
        rq = lax.rsqrt(jnp.sum(cq * cq, axis=1, keepdims=True) * (1.0 / Q_LORA_RANK) + RMS_EPS)
        cqn = ((cq * rq) * gq_ref[...]).astype(BF16)
        qall = _dot(cqn, wq_ref[...], _NN)
        for h in range(MLA_HEADS):
            sl = slice(HEAD_PAD * h, HEAD_PAD * (h + 1))
            q_ref[:, sl] = (_rope(qall[:, sl], c, sa, sb) * ATTN_SCALE).astype(BF16)
        ckv = ckv_ref[...]
        rkv = lax.rsqrt(jnp.sum(ckv * ckv, axis=1, keepdims=True) * (1.0 / KV_LORA_RANK) + RMS_EPS)
        ckvn = ((ckv * rkv) * gkv_ref[...]).astype(BF16)
        knall = _dot(ckvn, wkn_ref[...], _NN)
        vall = _dot(ckvn, wvx_ref[...], _NN)
        kper = _rope(kpe_ref[...], c, sa, sb)
        ones_half = (lax.broadcasted_iota(jnp.int32, (tm, HEAD_PAD), 1) >= V_HEAD_DIM).astype(F32)
        for h in range(MLA_HEADS):
            sl = slice(HEAD_PAD * h, HEAD_PAD * (h + 1))
            kh = knall[:, sl] + kper
            vh = vall[:, sl] + ones_half
            k_ref[:, sl] = kh.astype(BF16)
            kt_ref[sl, :] = kh.T.astype(BF16)
            vx_ref[:, sl] = vh.astype(BF16)
            vxt_ref[sl, :] = vh.T.astype(BF16)

    full = lambda shape: pl.BlockSpec(shape, lambda i: (0, 0))
    tab = pl.BlockSpec((tm, LANES), lambda i: (i, 0))
    row = pl.BlockSpec((tm, hw), lambda i: (i, 0))
    col = pl.BlockSpec((hw, tm), lambda i: (0, i))
    return pl.pallas_call(
        body, name="mla_prep", grid=(SEQ // tm,),
        in_specs=[pl.BlockSpec((tm, CQ_PAD), lambda i: (i, 0)),
                  pl.BlockSpec((tm, LANES), lambda i: (i, CQ_PAD // LANES)),
                  pl.BlockSpec((tm, LANES), lambda i: (i, CQ_PAD // LANES + 1)),
                  full((1, CQ_PAD)), full((1, KV_LORA_RANK)),
                  full((CQ_PAD, hw)), full((KV_LORA_RANK, hw)), full((KV_LORA_RANK, hw)), tab, tab, tab],
        out_specs=[row, row, col, row, col],
        out_shape=[jax.ShapeDtypeStruct((SEQ, hw), BF16), jax.ShapeDtypeStruct((SEQ, hw), BF16),
                   jax.ShapeDtypeStruct((hw, SEQ), BF16), jax.ShapeDtypeStruct((SEQ, hw), BF16),
                   jax.ShapeDtypeStruct((hw, SEQ), BF16)],
        compiler_params=pltpu.CompilerParams(dimension_semantics=("arbitrary",), vmem_limit_bytes=VMEM_MID),
    )(h_c, h_c, h_c, gq, gkv, wq, wkn, wvx, c_t, sa_t, sb_t)


ATT_T = 512
ATT_STRIP = 64


def _attn_fwd(q, kt, vx, own):
    t, rs = ATT_T, ATT_STRIP
    nown = len(own)
    nq = SEQ // t
    nsteps = (MLA_HEADS // 2) * nq

    def body(q_ref, kt_ref, vx_ref, *rest):
        own_refs, (o_ref, l_ref), gat_refs = rest[:nown], rest[nown:nown + 2], rest[nown + 2:2 * nown + 2]
        s_scr, p_scr, m_scr, a_scr, acc_scr, send_sems, recv_sems, local_sems = rest[2 * nown + 2:]
        qi = pl.program_id(1)
        _gather_behind(own_refs, gat_refs, send_sems, recv_sems, local_sems, pl.program_id(0) * nq + qi,
                       nsteps - 2, nsteps - 1)
        lane = lax.broadcasted_iota(jnp.int32, (t, LANES), 1)
        m_scr[...] = jnp.full((2, t, LANES), -1e30, F32)
        acc_scr[...] = jnp.zeros((2, t, LANES), F32)

        def block(j, masked):
            off = pl.multiple_of(j * t, t)
            for a in range(2):
                sl = slice(HEAD_PAD * a, HEAD_PAD * (a + 1))
                s_scr[a] = _dot(q_ref[:, sl], kt_ref[sl, pl.ds(off, t)], _NN)
                for r in range(t // rs):
                    rows = slice(rs * r, rs * (r + 1))
                    s = s_scr[a, rows, :]
                    if masked:
                        rowi = lax.broadcasted_iota(jnp.int32, (rs, t), 0) + rs * r
                        coli = lax.broadcasted_iota(jnp.int32, (rs, t), 1)
                        s = jnp.where(coli <= rowi, s, -1e30)
                    m_old = m_scr[a, rows, :]
                    m_new = jnp.maximum(m_old, jnp.max(s, axis=1, keepdims=True))
                    p_scr[a, rows, :] = jnp.exp(s - m_new[:, :1]).astype(BF16)
                    a_scr[a, rows, :] = jnp.exp(m_old - m_new)
                    m_scr[a, rows, :] = m_new
                acc_scr[a] = acc_scr[a] * a_scr[a] + _dot(p_scr[a], vx_ref[pl.ds(off, t), sl], _NN)

        def step(j, carry):
            block(j, False)
            return carry
        lax.fori_loop(0, qi, step, 0)
        block(qi, True)
        res = []
        for a in range(2):
            acc = acc_scr[a]
            l = acc[:, V_HEAD_DIM:V_HEAD_DIM + 1]
            res.append((acc / l, m_scr[a] + jnp.log(l)))
        o_ref[...] = jnp.where(lane < V_HEAD_DIM, res[0][0], pltpu.roll(res[1][0], V_HEAD_DIM, 1))
        l_ref[...] = jnp.where(lane < V_HEAD_DIM, res[0][1], res[1][1])

    hbm = pl.BlockSpec(memory_space=pl.ANY)
    res = pl.pallas_call(
        body, name="attn_fwd", grid=(MLA_HEADS // 2, nq),
        in_specs=[pl.BlockSpec((t, 2 * HEAD_PAD), lambda p, i: (i, p)),
                  pl.BlockSpec((2 * HEAD_PAD, SEQ), lambda p, i: (p, 0)),
                  pl.BlockSpec((SEQ, 2 * HEAD_PAD), lambda p, i: (0, p))] + [hbm] * nown,
        out_specs=[pl.BlockSpec((t, LANES), lambda p, i: (i, p)),
                   pl.BlockSpec((t, LANES), lambda p, i: (i, p))] + [hbm] * nown,
        out_shape=[jax.ShapeDtypeStruct((SEQ, MLA_WIDTH), F32), jax.ShapeDtypeStruct((SEQ, MLA_WIDTH), F32)]
        + [jax.ShapeDtypeStruct((N_DEV,) + a.shape, a.dtype) for a in own],
        scratch_shapes=[pltpu.VMEM((2, t, t), F32), pltpu.VMEM((2, t, t), BF16), pltpu.VMEM((2, t, LANES), F32),
                        pltpu.VMEM((2, t, LANES), F32), pltpu.VMEM((2, t, LANES), F32)] + _exchange_sems(nown),
        compiler_params=pltpu.CompilerParams(dimension_semantics=("arbitrary", "arbitrary"), vmem_limit_bytes=VMEM_MID),
    )(q, kt, vx, *own)
    return res[0], res[1], res[2:]


def _exchange_parts(parts, lands, send_sems, recv_sems, local_sems):
    x, y, c = _mesh_pos()
    me = 4 * x + 2 * y + c
    peers = [(x, y, 1 - c), (1 - x, y, c), (x, 1 - y, c), (1 - x, 1 - y, c),
             (1 - x, y, 1 - c), (x, 1 - y, 1 - c), (1 - x, 1 - y, 1 - c)]
    remote, local = [], []
    for a, (part, land) in enumerate(zip(parts, lands)):
        for k, peer in enumerate(peers):
            t = 4 * peer[0] + 2 * peer[1] + peer[2]
            remote.append(_remote(part.at[t], land.at[me], send_sems, recv_sems, 7 * a + k, peer))
        local.append(pltpu.make_async_copy(part.at[me], land.at[me], local_sems.at[a]))
    return remote, local


def _exchange_start(first_step, exchange):
    remote, local = exchange

    @pl.when(first_step)
    def _():
        for cp in remote + local:
            cp.start()


def _exchange_finish(last_step, exchange):
    remote, local = exchange

    @pl.when(last_step)
    def _():
        for cp in remote:
            cp.wait_recv()
        for cp in remote:
            cp.wait_send()
        for cp in local:
            cp.wait()


def _exchange_sems(npart):
    return [pltpu.SemaphoreType.DMA((7 * npart,)), pltpu.SemaphoreType.DMA((7 * npart,)),
            pltpu.SemaphoreType.DMA((npart,))]


def _attn_bwd(q, kt, k, vxt, d_o, o, lse, parts):
    t, rs = ATT_T, ATT_STRIP
    nq = SEQ // t
    npart = len(parts)
    nsteps = MLA_HEADS // 2

    def body(q_ref, kt_ref, k_ref, vxt_ref, do_ref, o_ref, l_ref, *rest):
        part_refs, rest = rest[:npart], rest[npart:]
        dq_ref, dk_ref, dv_ref = rest[:3]
        land_refs, rest = rest[3:3 + npart], rest[3 + npart:]
        s_scr, dp_scr, p_scr, ds_scr, st_scr, send_sems, recv_sems, local_sems = rest
        exchange = _exchange_parts(part_refs, land_refs, send_sems, recv_sems, local_sems)
        _exchange_start(pl.program_id(0) == 0, exchange)
        dk_ref[...] = jnp.zeros_like(dk_ref)
        dv_ref[...] = jnp.zeros_like(dv_ref)
        lane = lax.broadcasted_iota(jnp.int32, (t, LANES), 1)

        def qtile(i, carry):
            ioff = pl.multiple_of(i * t, t)
            do_i = do_ref[pl.ds(ioff, t), :]
            o_i = o_ref[pl.ds(ioff, t), :]
            l_i = l_ref[pl.ds(ioff, t), :]
            for a in range(2):
                sl = slice(HEAD_PAD * a, HEAD_PAD * (a + 1))
                sel = (lane < V_HEAD_DIM) if a == 0 else (lane >= V_HEAD_DIM)
                doa = jnp.where(sel, do_i, 0.0)
                oa = o_i
                if a == 1:
                    doa = pltpu.roll(doa, V_HEAD_DIM, 1)
                    oa = pltpu.roll(o_i, V_HEAD_DIM, 1)
                st_scr[0] = jnp.broadcast_to(jnp.sum(doa * oa, axis=1, keepdims=True), (t, LANES))
                st_scr[1] = jnp.broadcast_to(l_i[:, V_HEAD_DIM * a:V_HEAD_DIM * a + 1], (t, LANES))
                doa_bf = doa.astype(BF16)
                qa = q_ref[pl.ds(ioff, t), sl]

                def block(j, masked, dq_acc, sl=sl, qa=qa, doa_bf=doa_bf):
                    joff = pl.multiple_of(j * t, t)
                    s_scr[...] = _dot(qa, kt_ref[sl, pl.ds(joff, t)], _NN)
                    dp_scr[...] = _dot(doa_bf, vxt_ref[sl, pl.ds(joff, t)], _NN)
                    for r in range(t // rs):
                        rows = slice(rs * r, rs * (r + 1))
                        p = jnp.exp(s_scr[rows, :] - st_scr[1, rows, :1])
                        if masked:
                            rowi = lax.broadcasted_iota(jnp.int32, (rs, t), 0) + rs * r
                            coli = lax.broadcasted_iota(jnp.int32, (rs, t), 1)
                            p = jnp.where(coli <= rowi, p, 0.0)
                        p_scr[rows, :] = p.astype(BF16)
                        ds_scr[rows, :] = (p * (dp_scr[rows, :] - st_scr[0, rows, :1])).astype(BF16)
                    dk_ref[pl.ds(joff, t), sl] += _dot(ds_scr[...], qa, _TN)
                    dv_ref[pl.ds(joff, t), sl] += _dot(p_scr[...], doa_bf, _TN)
                    return dq_acc + _dot(ds_scr[...], k_ref[pl.ds(joff, t), sl], _NN)

                dq_acc = lax.fori_loop(0, i, lambda j, acc: block(j, False, acc), jnp.zeros((t, HEAD_PAD), F32))
                dq_ref[pl.ds(ioff, t), sl] = block(i, True, dq_acc)
            return carry

        lax.fori_loop(0, nq, qtile, 0)
        _exchange_finish(pl.program_id(0) == nsteps - 1, exchange)

    hw = MLA_HEADS * HEAD_PAD
    wide = pl.BlockSpec((SEQ, 2 * HEAD_PAD), lambda p: (0, p))
    wide_t = pl.BlockSpec((2 * HEAD_PAD, SEQ), lambda p: (p, 0))
    narrow = pl.BlockSpec((SEQ, LANES), lambda p: (0, p))
    hbm = pl.BlockSpec(memory_space=pl.ANY)
    res = pl.pallas_call(
        body, name="attn_bwd", grid=(nsteps,),
        in_specs=[wide, wide_t, wide, wide_t, narrow, narrow, narrow] + [hbm] * npart,
        out_specs=[wide, wide, wide] + [hbm] * npart,
        out_shape=[jax.ShapeDtypeStruct((SEQ, hw), F32)] * 3 + [jax.ShapeDtypeStruct(p.shape, p.dtype) for p in parts],
        scratch_shapes=[pltpu.VMEM((t, t), F32), pltpu.VMEM((t, t), F32), pltpu.VMEM((t, t), BF16),
                        pltpu.VMEM((t, t), BF16), pltpu.VMEM((2, t, LANES), F32)] + _exchange_sems(npart),
        compiler_params=pltpu.CompilerParams(dimension_semantics=("arbitrary",), vmem_limit_bytes=VMEM_BIG),
    )(q, kt, k, vxt, d_o, o, lse, *parts)
    return res[0], res[1], res[2], res[3:]


def _sgu_math(u, v, zb, lg, lb, ws_ref, bias):
    ug, dug = _gelu_and_grad(u)
    vg, dvg = _gelu_and_grad(v)
    mu = jnp.mean(vg, axis=1, keepdims=True)
    xc = vg - mu
    rstd = lax.rsqrt(jnp.mean(xc * xc, axis=1, keepdims=True) + LN_EPS)
    xh = xc * rstd
    vn_bf = (xh * lg + lb).astype(BF16)
    grp = lax.broadcasted_iota(jnp.int32, (CHUNK, SGU_WIDTH), 1) // SGU_GROUP_DIM
    r_i = lax.broadcasted_iota(jnp.int32, (CHUNK, CHUNK), 0)
    c_i = lax.broadcasted_iota(jnp.int32, (CHUNK, CHUNK), 1)
    tri, tri_t = r_i >= c_i, r_i <= c_i
    mixed = bias
    for g in range(SGU_GROUPS):
        wt = jnp.where(tri, ws_ref[g], 0.0).astype(BF16)
        mixed = mixed + jnp.where(grp == g, _dot(wt, vn_bf, _NN), 0.0)
    sb = _sigmoid(zb)
    return ug, dug, dvg, rstd, xh, vn_bf, grp, tri, tri_t, mixed, sb


def _sgu_fwd(h_b, lg, lb, w_s, bias_full):
    def body(u_ref, v_ref, zb_ref, lg_ref, lb_ref, ws_ref, bias_ref, yb_ref):
        zb = zb_ref[...]
        ug, _, _, _, _, _, _, _, _, mixed, sb = _sgu_math(u_ref[...], v_ref[...], zb, lg_ref[...], lb_ref[...],
                                                       ws_ref, bias_ref[...])
        yb_ref[...] = (ug * mixed) * (zb * sb)

    blk = lambda c: pl.BlockSpec((CHUNK, SGU_WIDTH), lambda i, c=c: (i, c))
    full2 = lambda shape: pl.BlockSpec(shape, lambda i: (0, 0))
    return pl.pallas_call(
        body, name="sgu_fwd", grid=(SEQ // CHUNK,),
        in_specs=[blk(0), blk(1), blk(2), full2((1, SGU_WIDTH)), full2((1, SGU_WIDTH)),
                  pl.BlockSpec((SGU_GROUPS, CHUNK, CHUNK), lambda i: (0, 0, 0)), full2((CHUNK, SGU_WIDTH))],
        out_specs=pl.BlockSpec((CHUNK, SGU_WIDTH), lambda i: (i, 0)),
        out_shape=jax.ShapeDtypeStruct((SEQ, SGU_WIDTH), F32),
        compiler_params=pltpu.CompilerParams(dimension_semantics=("arbitrary",)),
    )(h_b, h_b, h_b, lg, lb, w_s, bias_full)


def _sgu_bwd(h_b, d_yb, lg, lb, w_s, w_st, bias_full, parts):
    nsteps = SEQ // CHUNK
    npart = len(parts)

    def body(u_ref, v_ref, zb_ref, dyb_ref, lg_ref, lb_ref, ws_ref, wst_ref, bias_ref, *rest):
        part_refs, rest = rest[:npart], rest[npart:]
        dhb_ref, dws_ref, dbs_ref, dlg_ref, dlb_ref, dbb_ref = rest[:6]
        land_refs, (dbias_acc, send_sems, recv_sems, local_sems) = rest[6:6 + npart], rest[6 + npart:]
        step = pl.program_id(0)
        exchange = _exchange_parts(part_refs, land_refs, send_sems, recv_sems, local_sems)
        _exchange_start(step == 0, exchange)

        @pl.when(step == 0)
        def _():
            dbb_ref[...] = jnp.zeros_like(dbb_ref)
            dws_ref[...] = jnp.zeros_like(dws_ref)
            dlg_ref[...] = jnp.zeros_like(dlg_ref)
            dlb_ref[...] = jnp.zeros_like(dlb_ref)
            dbias_acc[...] = jnp.zeros_like(dbias_acc)

        zb = zb_ref[...]
        lg = lg_ref[...]
        ug, dug, dvg, rstd, xh, vn_bf, grp, tri, tri_t, mixed, sb = _sgu_math(
            u_ref[...], v_ref[...], zb, lg, lb_ref[...], ws_ref, bias_ref[...])
        dyb = dyb_ref[...]
        dsgu = dyb * (zb * sb)
        dzb = dyb * (ug * mixed) * (sb * (1.0 + zb * (1.0 - sb)))
        du = dsgu * mixed * dug
        dmixed = dsgu * ug
        dbias_acc[...] += dmixed
        dvn = jnp.zeros((CHUNK, SGU_WIDTH), F32)
        for g in range(SGU_GROUPS):
            dm_g = jnp.where(grp == g, dmixed, 0.0).astype(BF16)
            wtt = jnp.where(tri_t, wst_ref[g], 0.0).astype(BF16)
            dvn = dvn + _dot(wtt, dm_g, _NN)
            dws_ref[g] += jnp.where(tri, _dot(dm_g, vn_bf, _NT), 0.0)
        dlg_ref[...] += jnp.sum(dvn * xh, axis=0, keepdims=True)
        dlb_ref[...] += jnp.sum(dvn, axis=0, keepdims=True)
        dxh = dvn * lg
        dvgel = rstd * (dxh - jnp.mean(dxh, axis=1, keepdims=True) - xh * jnp.mean(dxh * xh, axis=1, keepdims=True))
        _store_grad(dhb_ref, dbb_ref, 0, du)
        _store_grad(dhb_ref, dbb_ref, SGU_WIDTH, dvgel * dvg)
        _store_grad(dhb_ref, dbb_ref, 2 * SGU_WIDTH, dzb)

        @pl.when(step == nsteps - 1)
        def _():
            acc = dbias_acc[...]
            lane = lax.broadcasted_iota(jnp.int32, (CHUNK, LANES), 1)
            out = jnp.zeros((CHUNK, LANES), F32)
            for g in range(SGU_GROUPS):
                sg = jnp.sum(jnp.where(grp == g, acc, 0.0), axis=1, keepdims=True)
                out = jnp.where(lane == g, sg, out)
            dbs_ref[...] = out

        _exchange_finish(step == nsteps - 1, exchange)

    blk = lambda c: pl.BlockSpec((CHUNK, SGU_WIDTH), lambda i, c=c: (i, c))
    full2 = lambda shape: pl.BlockSpec(shape, lambda i: (0, 0))
    full3 = pl.BlockSpec((SGU_GROUPS, CHUNK, CHUNK), lambda i: (0, 0, 0))
    hbm = pl.BlockSpec(memory_space=pl.ANY)
    res = pl.pallas_call(
        body, name="sgu_bwd", grid=(nsteps,),
        in_specs=[blk(0), blk(1), blk(2), pl.BlockSpec((CHUNK, SGU_WIDTH), lambda i: (i, 0)),
                  full2((1, SGU_WIDTH)), full2((1, SGU_WIDTH)), full3, full3, full2((CHUNK, SGU_WIDTH))] + [hbm] * npart,
        out_specs=[pl.BlockSpec((CHUNK, SEG_B), lambda i: (i, 0)), full3, full2((CHUNK, LANES)),
                   full2((1, SGU_WIDTH)), full2((1, SGU_WIDTH)), full2((1, SEG_B))] + [hbm] * npart,
        out_shape=[jax.ShapeDtypeStruct((SEQ, SEG_B), BF16),
                   jax.ShapeDtypeStruct((SGU_GROUPS, CHUNK, CHUNK), F32),
                   jax.ShapeDtypeStruct((CHUNK, LANES), F32),
                   jax.ShapeDtypeStruct((1, SGU_WIDTH), F32), jax.ShapeDtypeStruct((1, SGU_WIDTH), F32),
                   jax.ShapeDtypeStruct((1, SEG_B), F32)] + [jax.ShapeDtypeStruct(p.shape, p.dtype) for p in parts],
        scratch_shapes=[pltpu.VMEM((CHUNK, SGU_WIDTH), F32)] + _exchange_sems(npart),
        compiler_params=pltpu.CompilerParams(dimension_semantics=("arbitrary",)),
    )(h_b, h_b, h_b, d_yb, lg, lb, w_s, w_st, bias_full, *parts)
    return res[:6], res[6:]


def _merge(x, o, h_a, y_b, target, w_oa, w_ob, w_out, ln_g, ln_b):
    tm = 256
    nsteps = SEQ // tm

    def body(x_ref, o_ref, ga_ref, gb_ref, za_ref, yb_ref, tgt_ref, woa_ref, wob_ref, wout_ref, lng_ref, lnb_ref,
             loss_ref, dxr_ref, dha_ref, do_ref, dyb_ref, poa_ref, pob_ref, pout_ref, dlng_ref, dlnb_ref, dba_ref,
             dwoa_ref, dwob_ref, dwout_ref):
        step = pl.program_id(0)

        @pl.when(step == 0)
        def _():
            for r in (loss_ref, dwoa_ref, dwob_ref, dwout_ref, dlng_ref, dlnb_ref, dba_ref):
                r[...] = jnp.zeros_like(r)

        o = o_ref[...]
        za = za_ref[...]
        sa = _sigmoid(za)
        ya_bf = (o * (za * sa)).astype(BF16)
        yb_bf = yb_ref[...].astype(BF16)
        woa, wob, wout = woa_ref[...], wob_ref[...], wout_ref[...]
        pa = _dot(ya_bf, woa, _NN)
        pb = _dot(yb_bf, wob, _NN)
        sga = _sigmoid(ga_ref[...])
        sgb = _sigmoid(gb_ref[...])
        merged_bf = (sga * pa + sgb * pb).astype(BF16)
        r = DN_ALPHA * x_ref[...] + _dot(merged_bf, wout, _NN)
        mu = jnp.mean(r, axis=1, keepdims=True)
        rc = r - mu
        rstd = lax.rsqrt(jnp.mean(rc * rc, axis=1, keepdims=True) + LN_EPS)
        xh = rc * rstd
        lng = lng_ref[...]
        y = xh * lng + lnb_ref[...]
        e = y - tgt_ref[...]
        loss_ref[...] += 0.5 * jnp.sum(jnp.sum(e * e, axis=1, keepdims=True) * (1.0 / D_MODEL), axis=0, keepdims=True)

        dy = e * (1.0 / D_MODEL)
        dlng_ref[...] += jnp.sum(dy * xh, axis=0, keepdims=True)
        dlnb_ref[...] += jnp.sum(dy, axis=0, keepdims=True)
        dxh = dy * lng
        dr = rstd * (dxh - jnp.mean(dxh, axis=1, keepdims=True) - xh * jnp.mean(dxh * xh, axis=1, keepdims=True))
        dxr_ref[...] = DN_ALPHA * dr
        dr_bf = dr.astype(BF16)
        dwout_ref[...] += _dot(merged_bf, dr_bf, _TN)
        dmerged = _dot(dr_bf, wout, _NT)
        dpa_bf = (dmerged * sga).astype(BF16)
        dpb_bf = (dmerged * sgb).astype(BF16)
        _store_grad(dha_ref, dba_ref, 0, dmerged * pa * (sga * (1.0 - sga)))
        _store_grad(dha_ref, dba_ref, D_MODEL, dmerged * pb * (sgb * (1.0 - sgb)))
        dwoa_ref[...] += _dot(ya_bf, dpa_bf, _TN)
        dwob_ref[...] += _dot(yb_bf, dpb_bf, _TN)
        dya = _dot(dpa_bf, woa, _NT)
        dyb_ref[...] = _dot(dpb_bf, wob, _NT)
        do_ref[...] = dya * (za * sa)
        _store_grad(dha_ref, dba_ref, 2 * D_MODEL, dya * o * (sa * (1.0 + za * (1.0 - sa))))

        @pl.when(step == nsteps - 1)
        def _():
            cols = D_MODEL // N_DEV
            for j in range(N_DEV):
                poa_ref[j] = dwoa_ref[:, cols * j:cols * (j + 1)].astype(BF16)
                pob_ref[j] = dwob_ref[:, cols * j:cols * (j + 1)].astype(BF16)
                pout_ref[j] = dwout_ref[cols * j:cols * (j + 1), :].astype(BF16)

    row = lambda w, c=0: pl.BlockSpec((tm, w), lambda i, c=c: (i, c))
    full = lambda shape: pl.BlockSpec(shape, lambda i: (0, 0))
    full3 = lambda shape: pl.BlockSpec(shape, lambda i: (0, 0, 0))
    return pl.pallas_call(
        body, name="merge", grid=(nsteps,),
        in_specs=[row(D_MODEL), row(MLA_WIDTH), row(D_MODEL, 0), row(D_MODEL, 1), row(MLA_WIDTH, 4), row(SGU_WIDTH),
                  row(D_MODEL), full((MLA_WIDTH, D_MODEL)), full((SGU_WIDTH, D_MODEL)), full((D_MODEL, D_MODEL)),
                  full((1, D_MODEL)), full((1, D_MODEL))],
        out_specs=[full((1, LANES)), row(D_MODEL), row(SEG_A), row(MLA_WIDTH), row(SGU_WIDTH),
                   full3((N_DEV, MLA_WIDTH, D_MODEL // N_DEV)), full3((N_DEV, SGU_WIDTH, D_MODEL // N_DEV)),
                   full3((N_DEV, D_MODEL // N_DEV, D_MODEL)), full((1, D_MODEL)), full((1, D_MODEL)), full((1, SEG_A))],
        out_shape=[jax.ShapeDtypeStruct((1, LANES), F32),
                   jax.ShapeDtypeStruct((SEQ, D_MODEL), F32), jax.ShapeDtypeStruct((SEQ, SEG_A), BF16),
                   jax.ShapeDtypeStruct((SEQ, MLA_WIDTH), F32), jax.ShapeDtypeStruct((SEQ, SGU_WIDTH), F32),
                   jax.ShapeDtypeStruct((N_DEV, MLA_WIDTH, D_MODEL // N_DEV), BF16),
                   jax.ShapeDtypeStruct((N_DEV, SGU_WIDTH, D_MODEL // N_DEV), BF16),
                   jax.ShapeDtypeStruct((N_DEV, D_MODEL // N_DEV, D_MODEL), BF16),
                   jax.ShapeDtypeStruct((1, D_MODEL), F32), jax.ShapeDtypeStruct((1, D_MODEL), F32),
                   jax.ShapeDtypeStruct((1, SEG_A), F32)],
        scratch_shapes=[pltpu.VMEM((MLA_WIDTH, D_MODEL), F32), pltpu.VMEM((SGU_WIDTH, D_MODEL), F32),
                        pltpu.VMEM((D_MODEL, D_MODEL), F32)],
        compiler_params=pltpu.CompilerParams(dimension_semantics=("arbitrary",), vmem_limit_bytes=VMEM_BIG),
    )(x, o, h_a, h_a, h_a, y_b, target, w_oa, w_ob, w_out, ln_g, ln_b)


def _mla_bwd(dq, dk, dv, h_c, xt_bf, landed, gq, gkv, wq, wkn, wv, c_t, sa_t, sb_t, parts):
    tm = 256
    hw = MLA_HEADS * HEAD_PAD
    npart = len(parts)
    nsteps = SEQ // tm
    assert landed.shape[1] % nsteps == 0
    land_rows = landed.shape[1] // nsteps

    def body(dq_ref, dk_ref, dv_ref, cq_ref, ckv_ref, xt_ref, landed_ref, gq_ref, gkv_ref, wq_ref, wkn_ref, wv_ref, c_ref,
             sa_ref, sb_ref, *rest):
        part_refs, rest = rest[:npart], rest[npart:]
        dhc_ref, puq_ref, dwkn_ref, dwv_ref, dgq_ref, dgkv_ref, dbc_ref, dwc_ref, sum_ref = rest[:9]
        land_refs, (pre_ref, dwq_ref, dwc_acc, send_sems, recv_sems, local_sems) = rest[9:9 + npart], rest[9 + npart:]
        exchange = _exchange_parts(part_refs, land_refs, send_sems, recv_sems, local_sems)
        _exchange_start(pl.program_id(0) == 0, exchange)
        _exchange_finish(pl.program_id(0) == nsteps - 1, exchange)

        total = landed_ref[0].astype(F32)
        for s in range(1, N_DEV):
            total = total + landed_ref[s].astype(F32)
        sum_ref[...] = total

        @pl.when(pl.program_id(0) == 0)
        def _():
            for r in (dwq_ref, dwc_acc, dwkn_ref, dwv_ref, dgq_ref, dgkv_ref, dbc_ref):
                r[...] = jnp.zeros_like(r)

        c, sa, sb = c_ref[...], sa_ref[...], sb_ref[...]
        lane = lax.broadcasted_iota(jnp.int32, (tm, LANES), 1)
        rope_lanes = jnp.logical_and(lane >= ROPE_LO, lane < ROPE_HI)

        cq = cq_ref[...]
        gq = gq_ref[...]
        rq = lax.rsqrt(jnp.sum(cq * cq, axis=1, keepdims=True) * (1.0 / Q_LORA_RANK) + RMS_EPS)
        nq = cq * rq
        cqn_bf = (nq * gq).astype(BF16)
        for h in range(MLA_HEADS):
            sl = slice(HEAD_PAD * h, HEAD_PAD * (h + 1))
            pre_ref[:, sl] = _rope_t(dq_ref[:, sl] * ATTN_SCALE, c, sa, sb).astype(BF16)
        dqpre_bf = pre_ref[...]
        dcqn = _dot(dqpre_bf, wq_ref[...], _NT)
        dwq_ref[...] += _dot(cqn_bf, dqpre_bf, _TN)
        dgq_ref[...] += jnp.sum(dcqn * nq, axis=0, keepdims=True)
        dnq = dcqn * gq
        _store_grad(dhc_ref, dbc_ref, 0,
                    rq * (dnq - nq * (jnp.sum(dnq * nq, axis=1, keepdims=True) * (1.0 / Q_LORA_RANK))))

        ckv = ckv_ref[...]
        gkv = gkv_ref[...]
        rkv = lax.rsqrt(jnp.sum(ckv * ckv, axis=1, keepdims=True) * (1.0 / KV_LORA_RANK) + RMS_EPS)
        nkv = ckv * rkv
        ckvn_bf = (nkv * gkv).astype(BF16)
        dk = dk_ref[...]
        dk_bf = dk.astype(BF16)
        dv_bf = dv_ref[...].astype(BF16)
        dckvn = _dot(dk_bf, wkn_ref[...], _NT) + _dot(dv_bf, wv_ref[...], _NT)
        dwkn_ref[...] += _dot(ckvn_bf, dk_bf, _TN)
        dwv_ref[...] += _dot(ckvn_bf, dv_bf, _TN)
        dgkv_ref[...] += jnp.sum(dckvn * nkv, axis=0, keepdims=True)
        dnkv = dckvn * gkv
        _store_grad(dhc_ref, dbc_ref, CQ_PAD, rkv * (
            dnkv - nkv * (jnp.sum(dnkv * nkv, axis=1, keepdims=True) * (1.0 / KV_LORA_RANK))))
        dkpe = jnp.zeros((tm, LANES), F32)
        for h in range(MLA_HEADS):
            dkpe = dkpe + dk[:, HEAD_PAD * h:HEAD_PAD * (h + 1)]
        _store_grad(dhc_ref, dbc_ref, CQ_PAD + LANES, _rope_t(jnp.where(rope_lanes, dkpe, 0.0), c, sa, sb))
        dwc_acc[...] += _dot(xt_ref[...], dhc_ref[...], _NN)

        @pl.when(pl.program_id(0) == SEQ // tm - 1)
        def _():
            dwc_ref[...] = dwc_acc[...].astype(BF16)
            rows = Q_LORA_RANK // N_DEV
            for j in range(N_DEV):
                for h in range(MLA_HEADS):
                    puq_ref[j, :, QK_HEAD_DIM * h:QK_HEAD_DIM * (h + 1)] = dwq_ref[
                        rows * j:rows * (j + 1), HEAD_PAD * h:HEAD_PAD * h + QK_HEAD_DIM].astype(BF16)

    full = lambda shape: pl.BlockSpec(shape, lambda i: (0, 0))
    row = lambda w, c=0: pl.BlockSpec((tm, w), lambda i, c=c: (i, c))
    hbm = pl.BlockSpec(memory_space=pl.ANY)
    res = pl.pallas_call(
        body, name="mla_bwd", grid=(nsteps,),
        in_specs=[row(hw), row(hw), row(hw), row(CQ_PAD, 0), row(LANES, CQ_PAD // LANES),
                  pl.BlockSpec((D_MODEL, tm), lambda i: (0, i)),
                  pl.BlockSpec((N_DEV, land_rows, landed.shape[2]), lambda i: (0, i, 0)),
                  full((1, CQ_PAD)), full((1, KV_LORA_RANK)), full((CQ_PAD, hw)), full((KV_LORA_RANK, hw)),
                  full((KV_LORA_RANK, hw)), row(LANES), row(LANES), row(LANES)] + [hbm] * npart,
        out_specs=[row(SEG_C), pl.BlockSpec((N_DEV, Q_LORA_RANK // N_DEV, MLA_HEADS * QK_HEAD_DIM), lambda i: (0, 0, 0)),
                   full((KV_LORA_RANK, hw)), full((KV_LORA_RANK, hw)),
                   full((1, CQ_PAD)), full((1, KV_LORA_RANK)), full((1, SEG_C)), full((D_MODEL, SEG_C)),
                   pl.BlockSpec((land_rows, landed.shape[2]), lambda i: (i, 0))] + [hbm] * npart,
        out_shape=[jax.ShapeDtypeStruct((SEQ, SEG_C), BF16),
                   jax.ShapeDtypeStruct((N_DEV, Q_LORA_RANK // N_DEV, MLA_HEADS * QK_HEAD_DIM), BF16),
                   jax.ShapeDtypeStruct((KV_LORA_RANK, hw), F32), jax.ShapeDtypeStruct((KV_LORA_RANK, hw), F32),
                   jax.ShapeDtypeStruct((1, CQ_PAD), F32), jax.ShapeDtypeStruct((1, KV_LORA_RANK), F32),
                   jax.ShapeDtypeStruct((1, SEG_C), F32), jax.ShapeDtypeStruct((D_MODEL, SEG_C), BF16),
                   jax.ShapeDtypeStruct(landed.shape[1:], F32)]
        + [jax.ShapeDtypeStruct(p.shape, p.dtype) for p in parts],
        scratch_shapes=[pltpu.VMEM((tm, hw), BF16), pltpu.VMEM((CQ_PAD, hw), F32), pltpu.VMEM((D_MODEL, SEG_C), F32)]
        + _exchange_sems(npart),
        compiler_params=pltpu.CompilerParams(dimension_semantics=("arbitrary",), vmem_limit_bytes=VMEM_MID),
    )(dq, dk, dv, h_c, h_c, xt_bf, landed, gq, gkv, wq, wkn, wv, c_t, sa_t, sb_t, *parts)
    return res[:9], res[9:]


def _adamw_all(ws, gs, ms, vs):
    n = len(ws)
    c1 = 1.0 / (1.0 - ADAM_B1 ** ADAM_STEP)
    c2 = 1.0 / (1.0 - ADAM_B2 ** ADAM_STEP)

    def body(*refs):
        for idx in range(n):
            w, g, m, v = (refs[idx][...], refs[n + idx][...], refs[2 * n + idx][...], refs[3 * n + idx][...])
            m_new = ADAM_B1 * m + (1.0 - ADAM_B1) * g
            v_new = ADAM_B2 * v + (1.0 - ADAM_B2) * (g * g)
            delta = -ADAM_LR * ((m_new * c1) / (jnp.sqrt(v_new * c2) + ADAM_EPS) + ADAM_WD * w)
            refs[4 * n + idx][...] = delta
            refs[5 * n + idx][...] = m_new
            refs[6 * n + idx][...] = v_new

    shapes = [jax.ShapeDtypeStruct(w.shape, F32) for w in ws]
    outs = pl.pallas_call(
        body, name="adamw", out_shape=shapes * 3,
        compiler_params=pltpu.CompilerParams(vmem_limit_bytes=VMEM_BIG),
    )(*ws, *gs, *ms, *vs)
    return outs[:n], outs[n:2 * n], outs[2 * n:]


SHARD_W = IN_WIDTH // N_DEV

_PIECES = [(0, 384, 2, 0), (384, 512, 2, CQ_PAD), (512, 544, 2, CQ_PAD + LANES + ROPE_LO),
           (544, 1056, 0, 2 * D_MODEL), (1056, 1568, 1, 0), (1568, 2080, 1, SGU_WIDTH),
           (2080, 2592, 1, 2 * SGU_WIDTH), (2592, 3616, 0, 0), (3616, 4640, 0, D_MODEL)]


def _column_runs():
    runs = []
    for n0, n1, seg, d0 in _PIECES:
        for j in range(N_DEV):
            lo, hi = max(n0, j * SHARD_W), min(n1, (j + 1) * SHARD_W)
            if lo < hi:
                runs.append((j, lo - j * SHARD_W, hi - j * SHARD_W, seg, d0 + lo - n0))
    return runs


def _mesh_pos():
    return lax.axis_index("x"), lax.axis_index("y"), lax.axis_index("c")


def _remote(src, dst, send_sems, recv_sems, k, to):
    return pltpu.make_async_remote_copy(src_ref=src, dst_ref=dst, send_sem=send_sems.at[k], recv_sem=recv_sems.at[k],
                                        device_id=to, device_id_type=pl.DeviceIdType.MESH)


def _gather_exchange(gats, send_sems, recv_sems, meanwhile=None):
    x, y, c = _mesh_pos()
    me, sibling = (x, y, c), (x, y, 1 - c)
    chips = [(1 - x, y), (x, 1 - y), (1 - x, 1 - y)]

    def copy(a, k, blk, to):
        slab = gats[a].at[4 * blk[0] + 2 * blk[1] + blk[2]]
        return _remote(slab, slab, send_sems, recv_sems, 7 * a + k, to)

    arrays = range(len(gats))
    first = [copy(a, 1 + j, me, (*chip, c)) for j, chip in enumerate(chips) for a in arrays]
    first += [copy(a, 0, me, sibling) for a in arrays]
    for cp in first:
        cp.start()
    if meanwhile is not None:
        meanwhile()
    passed = []
    for j, chip in enumerate(chips):
        for a in arrays:
            copy(a, 1 + j, (*chip, c), me).wait_recv()
            fwd = copy(a, 4 + j, (*chip, c), sibling)
            fwd.start()
            passed.append(fwd)
    for a in arrays:
        copy(a, 0, sibling, me).wait_recv()
    for j, chip in enumerate(chips):
        for a in arrays:
            copy(a, 4 + j, (*chip, 1 - c), me).wait_recv()
    for cp in first + passed:
        cp.wait_send()


def _gather_behind(own, gats, send_sems, recv_sems, local_sems, step, mid, last):
    x, y, c = _mesh_pos()
    me, sibling = (x, y, c), (x, y, 1 - c)
    chips = [(1 - x, y), (x, 1 - y), (1 - x, 1 - y)]
    arrays = range(len(gats))

    def copy(a, k, blk, to, src=None):
        slab = gats[a].at[4 * blk[0] + 2 * blk[1] + blk[2]]
        return _remote(slab if src is None else src, slab, send_sems, recv_sems, 7 * a + k, to)

    first = [copy(a, 1 + j, me, (*chip, c), src=own[a]) for j, chip in enumerate(chips) for a in arrays]
    first += [copy(a, 0, me, sibling, src=own[a]) for a in arrays]
    local = [pltpu.make_async_copy(own[a], gats[a].at[4 * x + 2 * y + c], local_sems.at[a]) for a in arrays]
    passed = [copy(a, 4 + j, (*chip, c), sibling) for j, chip in enumerate(chips) for a in arrays]

    @pl.when(step == 0)
    def _():
        for cp in first + local:
            cp.start()

    @pl.when(step == mid)
    def _():
        for j, chip in enumerate(chips):
            for a in arrays:
                copy(a, 1 + j, (*chip, c), me).wait_recv()
        for cp in passed:
            cp.start()

    @pl.when(step == last)
    def _():
        for a in arrays:
            copy(a, 0, sibling, me).wait_recv()
        for j, chip in enumerate(chips):
            for a in arrays:
                copy(a, 4 + j, (*chip, 1 - c), me).wait_recv()
        for cp in first + passed:
            cp.wait_send()
        for cp in local:
            cp.wait()


def _gather_first(w_in, w_uq2, w_oa, w_ob, w_out, x2, pos_col, invf_lane):
    hw = MLA_HEADS * HEAD_PAD
    uq_rows = Q_LORA_RANK // N_DEV
    rows = 256

    def body(win_ref, wuq_ref, woa_ref, wob_ref, wout_ref, x_ref, pos_ref, invf_ref,
             wc_ref, wq_ref, winb_ref, oab_ref, obb_ref, outb_ref, xb_ref, xt_ref, c_ref, sa_ref, sb_ref,
             g_uq, blk0, send_sems, recv_sems):
        def local_work():
            for i in range(SEQ // rows):
                xi = x_ref[rows * i:rows * (i + 1), :]
                xb_ref[rows * i:rows * (i + 1), :] = xi.astype(BF16)
                xt_ref[:, rows * i:rows * (i + 1)] = xi.T.astype(BF16)
            ang = pos_ref[...].astype(F32) * invf_ref[...]
            cs, sn = jnp.cos(ang), jnp.sin(ang)
            lane = lax.broadcasted_iota(jnp.int32, ang.shape, 1)
            c_ref[...] = jnp.where(lane < ROPE_LO, 1.0, jnp.where(lane < ROPE_HI, cs, 0.0))
            sa_ref[...] = jnp.where(jnp.logical_and(lane >= ROPE_LO, lane < ROPE_MID), -sn, 0.0)
            sb_ref[...] = jnp.where(jnp.logical_and(lane >= ROPE_MID, lane < ROPE_HI), sn, 0.0)

        x, y, c = _mesh_pos()
        me = (x, y, c)
        winb_ref[...] = win_ref[0].astype(BF16)
        oab_ref[...] = woa_ref[0].astype(BF16)
        obb_ref[...] = wob_ref[0].astype(BF16)
        outb_ref[...] = wout_ref[0].astype(BF16)
        g_uq[4 * x + 2 * y + c] = wuq_ref[...].astype(BF16)

        chip0 = jnp.logical_and(x == 0, y == 0)
        south = c == 0
        half = D_MODEL // 2
        halves = [blk0.at[pl.ds(0, half)], blk0.at[pl.ds(half, half)]]

        def bcopy(k, to, part=None):
            ref = blk0 if part is None else halves[part]
            return _remote(ref, ref, send_sems, recv_sems, 7 + k, to)

        sends0 = [(0, (0, 0, 1), None), (1, (1, 0, 0), 0), (2, (0, 1, 0), 1), (3, (1, 0, 0), 1), (4, (0, 1, 0), 0)]

        @pl.when(jnp.logical_and(chip0, south))
        def _():
            blk0[...] = winb_ref[...]
            for k, to, part in sends0:
                bcopy(k, to, part).start()

        _gather_exchange([g_uq], send_sems, recv_sems, meanwhile=local_work)

        for (cx, cy), first_k, first_half, second_k in (((1, 0), 1, 0, 3), ((0, 1), 2, 1, 4)):
            @pl.when(jnp.logical_and(jnp.logical_and(x == cx, y == cy), south))
            def _(cx=cx, cy=cy, first_k=first_k, first_half=first_half, second_k=second_k):
                bcopy(first_k, me, first_half).wait_recv()
                onward = bcopy(5 + first_half, (1, 1, 0), first_half)
                onward.start()
                bcopy(second_k, me, 1 - first_half).wait_recv()
                north = bcopy(7, (cx, cy, 1))
                north.start()
                onward.wait_send()
                north.wait_send()

        @pl.when(jnp.logical_and(jnp.logical_and(x == 1, y == 1), south))
        def _():
            bcopy(5, me, 0).wait_recv()
            bcopy(6, me, 1).wait_recv()
            north = bcopy(7, (1, 1, 1))
            north.start()
            north.wait_send()

        @pl.when(jnp.logical_and(chip0, c == 1))
        def _():
            bcopy(0, me).wait_recv()

        @pl.when(jnp.logical_and(jnp.logical_not(chip0), c == 1))
        def _():
            bcopy(7, me).wait_recv()

        @pl.when(jnp.logical_and(chip0, south))
        def _():
            for k, to, part in sends0:
                bcopy(k, to, part).wait_send()

        for j, s0, s1, seg, d0 in _column_runs():
            if seg == 2:
                wc_ref[:, d0:d0 + (s1 - s0)] = blk0[:, s0:s1]
        zeros = lambda r, w: jnp.zeros((r, w), BF16)
        wc_ref[:, Q_LORA_RANK:CQ_PAD] = zeros(D_MODEL, CQ_PAD - Q_LORA_RANK)
        wc_ref[:, CQ_PAD + LANES:CQ_PAD + LANES + ROPE_LO] = zeros(D_MODEL, ROPE_LO)
        wc_ref[:, CQ_PAD + LANES + ROPE_HI:SEG_C] = zeros(D_MODEL, LANES - ROPE_HI)
        wq_ref[Q_LORA_RANK:CQ_PAD, :] = zeros(CQ_PAD - Q_LORA_RANK, hw)
        for h in range(MLA_HEADS):
            wq_ref[0:Q_LORA_RANK, HEAD_PAD * h + QK_HEAD_DIM:HEAD_PAD * (h + 1)] = zeros(Q_LORA_RANK, HEAD_PAD - QK_HEAD_DIM)
        for j in range(N_DEV):
            for h in range(MLA_HEADS):
                wq_ref[uq_rows * j:uq_rows * (j + 1), HEAD_PAD * h:HEAD_PAD * h + QK_HEAD_DIM] = g_uq[
                    j, :, QK_HEAD_DIM * h:QK_HEAD_DIM * (h + 1)]

    vmem = pl.BlockSpec(memory_space=pltpu.VMEM)
    return pl.pallas_call(
        body, name="gather_first",
        out_shape=[jax.ShapeDtypeStruct((D_MODEL, SEG_C), BF16), jax.ShapeDtypeStruct((CQ_PAD, hw), BF16),
                   jax.ShapeDtypeStruct(w_in.shape[1:], BF16), jax.ShapeDtypeStruct(w_oa.shape[1:], BF16),
                   jax.ShapeDtypeStruct(w_ob.shape[1:], BF16), jax.ShapeDtypeStruct(w_out.shape[1:], BF16),
                   jax.ShapeDtypeStruct((SEQ, D_MODEL), BF16), jax.ShapeDtypeStruct((D_MODEL, SEQ), BF16)]
        + [jax.ShapeDtypeStruct((SEQ, LANES), F32)] * 3,
        in_specs=[vmem] * 8, out_specs=[vmem] * 11,
        scratch_shapes=[pltpu.VMEM((N_DEV, uq_rows, MLA_HEADS * QK_HEAD_DIM), BF16), pltpu.VMEM((D_MODEL, SHARD_W), BF16),
                        pltpu.SemaphoreType.DMA((15,)), pltpu.SemaphoreType.DMA((15,))],
        compiler_params=pltpu.CompilerParams(vmem_limit_bytes=VMEM_BIG),
    )(w_in, w_uq2, w_oa, w_ob, w_out, x2, pos_col, invf_lane)


def _assemble_in(g_in):
    def body(g_ref, wa_ref, wb_ref):
        segs = [wa_ref, wb_ref]
        for j, s0, s1, seg, d0 in _column_runs():
            if seg < 2:
                segs[seg][:, d0:d0 + (s1 - s0)] = g_ref[j, :, s0:s1]

    return pl.pallas_call(
        body, name="assemble_in",
        out_shape=[jax.ShapeDtypeStruct((D_MODEL, SEG_A), BF16), jax.ShapeDtypeStruct((D_MODEL, SEG_B), BF16)],
        compiler_params=pltpu.CompilerParams(vmem_limit_bytes=VMEM_MID),
    )(g_in)


def _assemble_out(g_oa, g_ob, g_out):
    cols = D_MODEL // N_DEV

    def body(goa_ref, gob_ref, gout_ref, oa_ref, ob_ref, out_ref):
        for j in range(N_DEV):
            oa_ref[:, cols * j:cols * (j + 1)] = goa_ref[j]
            ob_ref[:, cols * j:cols * (j + 1)] = gob_ref[j]
            out_ref[cols * j:cols * (j + 1), :] = gout_ref[j]

    return pl.pallas_call(
        body, name="assemble_out",
        out_shape=[jax.ShapeDtypeStruct((MLA_WIDTH, D_MODEL), BF16), jax.ShapeDtypeStruct((SGU_WIDTH, D_MODEL), BF16),
                   jax.ShapeDtypeStruct((D_MODEL, D_MODEL), BF16)],
    )(g_oa, g_ob, g_out)


C_NAT = 544


P_IN_SPLIT = 896


def _to_parts(dwa, dwb):
    def body(dwa_ref, dwb_ref, phi_ref, plo_ref):
        phi_ref[0, :, 0:C_NAT] = jnp.zeros((P_IN_SPLIT, C_NAT), BF16)
        plo_ref[0, :, 0:C_NAT] = jnp.zeros((D_MODEL - P_IN_SPLIT, C_NAT), BF16)
        segs = [dwa_ref, dwb_ref]
        for j, s0, s1, seg, d0 in _column_runs():
            if seg < 2:
                phi_ref[j, :, s0:s1] = segs[seg][0:P_IN_SPLIT, d0:d0 + (s1 - s0)]
                plo_ref[j, :, s0:s1] = segs[seg][P_IN_SPLIT:D_MODEL, d0:d0 + (s1 - s0)]

    return pl.pallas_call(
        body, name="to_parts",
        out_shape=[jax.ShapeDtypeStruct((N_DEV, P_IN_SPLIT, SHARD_W), BF16),
                   jax.ShapeDtypeStruct((N_DEV, D_MODEL - P_IN_SPLIT, SHARD_W), BF16)],
        compiler_params=pltpu.CompilerParams(vmem_limit_bytes=VMEM_MID))(dwa, dwb)


def _dx_tail(dhs, ws, dx_res, dwc, p_uq, p_rep):
    ntile, sums_at = 8, 5
    tm = SEQ // ntile
    rep_rows = p_rep.shape[1]
    c_rows = D_MODEL // N_DEV
    spec = [((c_rows, C_NAT), BF16), (p_uq.shape[1:], BF16), ((rep_rows, LANES), F32)]
    n = len(spec)

    nseg = len(dhs)

    def body(*refs):
        dh_refs, w_refs = refs[:nseg], refs[nseg:2 * nseg]
        dxr_ref, dwc_ref, puq_ref, prep_ref, dx_ref, call_ref, guq_ref, repall_ref, pc_ref, c_all, rep_all = refs[
            2 * nseg:2 * nseg + 11]
        rest = refs[2 * nseg + 11:]
        ras, tbs, rbs = rest[0:n], rest[n:2 * n], rest[2 * n:3 * n]
        send_sems, recv_sems, gsend, grecv = rest[3 * n:]
        step = pl.program_id(0)
        x, y, c = _mesh_pos()
        me_idx = 4 * x + 2 * y + c
        me, sibling = (x, y, c), (x, y, 1 - c)
        others = [(1 - x, y), (x, 1 - y), (1 - x, 1 - y)]
        parts = [pc_ref, puq_ref, prep_ref]
        gats = [rep_all, c_all]

        def stage1(chip, a):
            return _remote(parts[a].at[2 * chip + (1 - c)], ras[a].at[chip], send_sems, recv_sems, 7 * a + chip, sibling)

        def stage2(k, a):
            cx, cy = others[k]
            return _remote(tbs[a].at[k], rbs[a].at[k], send_sems, recv_sems, 7 * a + 4 + k, (cx, cy, c))

        def gcopy(a, k, blk, to):
            slab = gats[a].at[4 * blk[0] + 2 * blk[1] + blk[2]]
            return _remote(slab, slab, gsend, grecv, 7 * a + k, to)

        def chip_sum(a, chip):
            return parts[a][2 * chip + c].astype(F32) + ras[a][chip].astype(F32)

        @pl.when(step == 0)
        def _():
            for j, s0, s1, seg, d0 in _column_runs():
                if seg == 2:
                    for r in range(N_DEV):
                        pc_ref[r, :, s0:s1] = dwc_ref[c_rows * r:c_rows * (r + 1), d0:d0 + (s1 - s0)]
            for chip in range(4):
                for a in range(n):
                    stage1(chip, a).start()

        @pl.when(step == 1)
        def _():
            for chip in range(4):
                for a in range(n):
                    stage1(chip, a).wait_recv()
            for k, (cx, cy) in enumerate(others):
                for a in range(n):
                    tbs[a][k] = chip_sum(a, 2 * cx + cy).astype(spec[a][1])
                    stage2(k, a).start()

        @pl.when(step == sums_at)
        def _():
            for k in range(3):
                for a in range(n):
                    stage2(k, a).wait_recv()
            sums = []
            for a in range(n):
                acc = chip_sum(a, 2 * x + y)
                for k in range(3):
                    acc = acc + rbs[a][k].astype(F32)
                sums.append(acc)
            c_all[me_idx] = sums[0].astype(BF16)
            guq_ref[...] = sums[1]
            rep_all[me_idx] = sums[2]
            for a in range(2):
                for j, chip in enumerate(others):
                    gcopy(a, 1 + j, me, (*chip, c)).start()
                gcopy(a, 0, me, sibling).start()

        acc = dxr_ref[...]
        for dh_ref, w_ref in zip(dh_refs, w_refs):
            acc = acc + _dot(dh_ref[...], w_ref[...], _NT)
        dx_ref[...] = acc

        @pl.when(step == ntile - 1)
        def _():
            for j, chip in enumerate(others):
                for a in range(2):
                    gcopy(a, 1 + j, (*chip, c), me).wait_recv()
                    gcopy(a, 4 + j, (*chip, c), sibling).start()
            for a in range(2):
                gcopy(a, 0, sibling, me).wait_recv()
                for j, chip in enumerate(others):
                    gcopy(a, 4 + j, (*chip, 1 - c), me).wait_recv()
            for a in range(2):
                gcopy(a, 0, me, sibling).wait_send()
                for j, chip in enumerate(others):
                    gcopy(a, 1 + j, me, (*chip, c)).wait_send()
                    gcopy(a, 4 + j, (*chip, c), sibling).wait_send()
            for a in range(n):
                for chip in range(4):
                    stage1(chip, a).wait_send()
                for k in range(3):
                    stage2(k, a).wait_send()
            call_ref[...] = c_all[...]
            repall_ref[...] = rep_all[...]

    row = lambda w: pl.BlockSpec((tm, w), lambda i: (i, 0))
    full = lambda shape: pl.BlockSpec(shape, lambda i: (0,) * len(shape))
    scratch = [pltpu.VMEM((N_DEV, c_rows, C_NAT), BF16), pltpu.VMEM((N_DEV, c_rows, C_NAT), BF16),
               pltpu.VMEM((N_DEV, rep_rows, LANES), F32)]
    for lead in (4, 3, 3):
        scratch += [pltpu.VMEM((lead,) + tuple(shape), dt) for shape, dt in spec]
    scratch += [pltpu.SemaphoreType.DMA((7 * n,)), pltpu.SemaphoreType.DMA((7 * n,)),
                pltpu.SemaphoreType.DMA((14,)), pltpu.SemaphoreType.DMA((14,))]
    return pl.pallas_call(
        body, name="dx_tail", grid=(SEQ // tm,),
        in_specs=[row(dh.shape[1]) for dh in dhs] + [full(w.shape) for w in ws]
        + [row(D_MODEL), full(dwc.shape), full(p_uq.shape), full(p_rep.shape)],
        out_specs=[row(D_MODEL), full((N_DEV, c_rows, C_NAT)), full(p_uq.shape[1:]), full((N_DEV, rep_rows, LANES))],
        out_shape=[jax.ShapeDtypeStruct((SEQ, D_MODEL), F32), jax.ShapeDtypeStruct((N_DEV, c_rows, C_NAT), BF16),
                   jax.ShapeDtypeStruct(p_uq.shape[1:], F32), jax.ShapeDtypeStruct((N_DEV, rep_rows, LANES), F32)],
        scratch_shapes=scratch,
        compiler_params=pltpu.CompilerParams(dimension_semantics=("arbitrary",), vmem_limit_bytes=VMEM_BIG),
    )(*dhs, *ws, dx_res, dwc, p_uq, p_rep)


def _sum_landed(landed, c_all):
    c_rows = D_MODEL // N_DEV

    def body(shi_ref, rlo_ref, roa_ref, rob_ref, rout_ref, call_ref, gin_ref, goa_ref, gob_ref, gout_ref):
        def total(ref, sl):
            acc = ref[0, sl, :].astype(F32)
            for s in range(1, N_DEV):
                acc = acc + ref[s, sl, :].astype(F32)
            return acc

        x, y, c = _mesh_pos()
        dev0 = jnp.where(4 * x + 2 * y + c == 0, 1.0, 0.0)
        for j in range(N_DEV):
            sl = slice(c_rows * j, c_rows * (j + 1))
            below = c_rows * j < P_IN_SPLIT
            tot = shi_ref[sl, :] if below else total(rlo_ref, slice(c_rows * j - P_IN_SPLIT, c_rows * (j + 1) - P_IN_SPLIT))
            gin_ref[0, sl, C_NAT:SHARD_W] = tot[:, C_NAT:SHARD_W]
            gin_ref[0, sl, 0:C_NAT] = tot[:, 0:C_NAT] + dev0 * call_ref[j].astype(F32)
        goa_ref[0] = total(roa_ref, slice(None))
        gob_ref[0] = total(rob_ref, slice(None))
        gout_ref[0] = total(rout_ref, slice(None))

    return pl.pallas_call(
        body, name="sum_landed",
        out_shape=[jax.ShapeDtypeStruct((1, D_MODEL, SHARD_W), F32)]
        + [jax.ShapeDtypeStruct((1,) + r.shape[1:], F32) for r in landed[2:]],
        compiler_params=pltpu.CompilerParams(vmem_limit_bytes=VMEM_MID),
    )(*landed, c_all)


_O_CQ, _O_CKV, _O_KPE, _O_ZA, _O_U, _O_V, _O_ZB, _O_GA, _O_GB = 0, 384, 512, 544, 1056, 1568, 2080, 2592, 3616


def _to_segments(w):
    z = lambda n: jnp.zeros(w.shape[:-1] + (n,), w.dtype)
    seg_a = jnp.concatenate([w[..., _O_GA:_O_GB], w[..., _O_GB:IN_WIDTH], w[..., _O_ZA:_O_U]], axis=-1)
    seg_b = jnp.concatenate([w[..., _O_U:_O_V], w[..., _O_V:_O_ZB], w[..., _O_ZB:_O_GA]], axis=-1)
    seg_c = jnp.concatenate([w[..., _O_CQ:_O_CKV], z(CQ_PAD - Q_LORA_RANK), w[..., _O_CKV:_O_KPE],
                             z(ROPE_LO), w[..., _O_KPE:_O_ZA], z(LANES - ROPE_HI)], axis=-1)
    return seg_a, seg_b, seg_c


def _from_segments(seg_a, seg_b, seg_c):
    kpe0 = CQ_PAD + LANES + ROPE_LO
    return jnp.concatenate([
        seg_c[..., 0:Q_LORA_RANK], seg_c[..., CQ_PAD:CQ_PAD + LANES], seg_c[..., kpe0:kpe0 + QK_ROPE_DIM],
        seg_a[..., 2 * D_MODEL:SEG_A], seg_b, seg_a[..., 0:2 * D_MODEL]], axis=-1)


def kernel(x, positions, w_in, b_in, g_q, w_uq, g_kv, w_ukv, w_oa, sgu_ln_g, sgu_ln_b, w_s, b_s, w_ob, w_out, ln_g, ln_b, loss_target, m_w_in, m_b_in, m_g_q, m_w_uq, m_g_kv, m_w_ukv, m_w_oa, m_sgu_ln_g, m_sgu_ln_b, m_w_s, m_b_s, m_w_ob, m_w_out, m_ln_g, m_ln_b, v_w_in, v_b_in, v_g_q, v_w_uq, v_g_kv, v_w_ukv, v_w_oa, v_sgu_ln_g, v_sgu_ln_b, v_w_s, v_b_s, v_w_ob, v_w_out, v_ln_g, v_ln_b):
    w_uq2 = w_uq[0].reshape(Q_LORA_RANK // N_DEV, MLA_HEADS * QK_HEAD_DIM)
    inv_freq = ROPE_THETA ** (-jnp.arange(0, QK_ROPE_DIM, 2, dtype=F32) / QK_ROPE_DIM)
    invf_lane = jnp.concatenate([jnp.zeros((ROPE_LO,), F32), inv_freq, inv_freq,
                                 jnp.zeros((LANES - ROPE_HI,), F32)]).reshape(1, LANES)
    first = _gather_first(w_in, w_uq2, w_oa, w_ob, w_out, x[0], positions.reshape(SEQ, 1), invf_lane)
    partials = _local_step(x[0], loss_target[0], first, b_in, g_q, g_kv, w_ukv, sgu_ln_g, sgu_ln_b, w_s, b_s, ln_g, ln_b)
    weights = dict(w_in=w_in, b_in=b_in, g_q=g_q, w_uq=w_uq, g_kv=g_kv, w_ukv=w_ukv, w_oa=w_oa, sgu_ln_g=sgu_ln_g,
                   sgu_ln_b=sgu_ln_b, w_s=w_s, b_s=b_s, w_ob=w_ob, w_out=w_out, ln_g=ln_g, ln_b=ln_b)
    moms = dict(w_in=m_w_in, b_in=m_b_in, g_q=m_g_q, w_uq=m_w_uq, g_kv=m_g_kv, w_ukv=m_w_ukv, w_oa=m_w_oa,
                sgu_ln_g=m_sgu_ln_g, sgu_ln_b=m_sgu_ln_b, w_s=m_w_s, b_s=m_b_s, w_ob=m_w_ob, w_out=m_w_out,
                ln_g=m_ln_g, ln_b=m_ln_b)
    vars_ = dict(w_in=v_w_in, b_in=v_b_in, g_q=v_g_q, w_uq=v_w_uq, g_kv=v_g_kv, w_ukv=v_w_ukv, w_oa=v_w_oa,
                 sgu_ln_g=v_sgu_ln_g, sgu_ln_b=v_sgu_ln_b, w_s=v_w_s, b_s=v_b_s, w_ob=v_w_ob, w_out=v_w_out,
                 ln_g=v_ln_g, ln_b=v_ln_b)
    return _reduce_and_update(partials, weights, moms, vars_)


def _local_step(x2, tgt, first, b_in, g_q, g_kv, w_ukv, sgu_ln_g, sgu_ln_b, w_s, b_s, ln_g, ln_b):
    wc, wq, win_b, oa_b, ob_b, out_b, x_bf, xt_bf, c_t, sa_t, sb_t = first
    ba, bb, bc = _to_segments(b_in)
    w_ukv_bf = w_ukv[0].astype(BF16)
    wkn = jnp.pad(w_ukv_bf[:, :, :QK_NOPE_DIM], ((0, 0), (0, 0), (0, HEAD_PAD - QK_NOPE_DIM))).reshape(KV_LORA_RANK, -1)
    wv = jnp.pad(w_ukv_bf[:, :, QK_NOPE_DIM:], ((0, 0), (0, 0), (0, HEAD_PAD - V_HEAD_DIM))).reshape(KV_LORA_RANK, -1)
    gq = jnp.pad(g_q, ((0, 0), (0, CQ_PAD - Q_LORA_RANK)))
    bias_full = jnp.repeat(b_s[0].T, SGU_GROUP_DIM, axis=1)
    w_s3 = w_s[0]
    w_st3 = jnp.swapaxes(w_s3, 1, 2)

    h_c = _mm(x_bf, wc, bias=bc, tm=512, tn=SEG_C, name="in_proj_c")
    q, k, kt, vx, vxt = _mla_prep(h_c, gq, g_kv, wq, wkn, wv, c_t, sa_t, sb_t)
    o, lse, (g_in,) = _attn_fwd(q, kt, vx, (win_b,))
    wa, wb = _assemble_in(g_in)
    h_a, (g_out,) = _mm(x_bf, wa, bias=ba, own=(out_b,), tm=512, tn=SEG_A // 2, name="in_proj_a")
    h_b, (g_oa, g_ob) = _mm(x_bf, wb, bias=bb, own=(oa_b, ob_b), tm=512, tn=SEG_B // 2, name="in_proj_b")
    y_b = _sgu_fwd(h_b, sgu_ln_g, sgu_ln_b, w_s3, bias_full)
    w_oa_f, w_ob_f, w_out_f = _assemble_out(g_oa, g_ob, g_out)

    (loss_row, dx_res, dh_a, d_o, d_yb, p_oa, p_ob, p_out, d_lng, d_lnb, d_ba) = _merge(
        x2, o, h_a, y_b, tgt, w_oa_f, w_ob_f, w_out_f, ln_g, ln_b)
    (dh_b, d_ws, d_bs_t, d_slg, d_slb, d_bb), (r_out,) = _sgu_bwd(h_b, d_yb, sgu_ln_g, sgu_ln_b, w_s3, w_st3, bias_full,
                                                                 (p_out,))
    d_wa, (r_oa,) = _mm(xt_bf, dh_a, out_dtype=BF16, parts=(p_oa,), tm=512, tn=512, name="dw_in_a")
    d_wb = _mm(xt_bf, dh_b, out_dtype=BF16, tm=512, tn=512, name="dw_in_b")
    p_hi, p_lo = _to_parts(d_wa, d_wb)
    dq, dk, dv, (r_hi,) = _attn_bwd(q, kt, k, vxt, d_o, o, lse, (p_hi,))
    (dh_c, p_uq, d_wkn, d_wv, d_gq, d_gkv, d_bc, d_wc, s_hi), (r_lo, r_ob) = _mla_bwd(
        dq, dk, dv, h_c, xt_bf, r_hi, gq, g_kv, wq, wkn, wv, c_t, sa_t, sb_t, (p_lo, p_ob))
    landed = (s_hi, r_lo, r_oa, r_ob, r_out)

    p_b_in = _from_segments(d_ba, d_bb, d_bc)
    p_w_ukv = jnp.concatenate([d_wkn.reshape(KV_LORA_RANK, MLA_HEADS, HEAD_PAD)[:, :, :QK_NOPE_DIM],
                               d_wv.reshape(KV_LORA_RANK, MLA_HEADS, HEAD_PAD)[:, :, :V_HEAD_DIM]], axis=-1)
    p_g_q = d_gq[:, :Q_LORA_RANK]
    p_b_s = d_bs_t[:, :SGU_GROUPS].T
    replicated = [p_b_in, p_g_q, d_gkv, p_w_ukv, d_slg, d_slb, d_ws, p_b_s, d_lng, d_lnb]
    return loss_row, ((dh_a, dh_b, dh_c), (wa, wb, wc), dx_res), landed, d_wc, p_uq, replicated


_NAMES = ["w_in", "b_in", "g_q", "w_uq", "g_kv", "w_ukv", "w_oa", "sgu_ln_g", "sgu_ln_b", "w_s", "b_s", "w_ob",
          "w_out", "ln_g", "ln_b"]
_REPLICATED = ["b_in", "g_q", "g_kv", "w_ukv", "sgu_ln_g", "sgu_ln_b", "w_s", "b_s", "ln_g", "ln_b"]


def _reduce_and_update(partials, weights, moms, vars_):
    loss_row, (dhs, ws, dx_res), landed, d_wc, p_uq, replicated = partials
    def piece(a):
        flat = a.reshape(-1)
        return jnp.pad(flat, (0, -flat.size % PACK_ALIGN))

    rep_flat = jnp.concatenate([piece(a) for a in replicated] + [piece(loss_row[0, :1])])
    rep_flat = jnp.pad(rep_flat, (0, N_DEV * PACK_R_ROWS * LANES - rep_flat.size))
    dx, c_all, g_uq, rep_all = _dx_tail(dhs, ws, dx_res, d_wc, p_uq, rep_flat.reshape(N_DEV, PACK_R_ROWS, LANES))
    g_in, g_oa, g_ob, g_out = _sum_landed(landed, c_all)
    rep_sum = rep_all.reshape(-1)
    grads, pos = dict(w_in=g_in, w_uq=g_uq, w_oa=g_oa, w_ob=g_ob, w_out=g_out), 0
    for nm in _REPLICATED:
        grads[nm] = rep_sum[pos:pos + weights[nm].size]
        pos += weights[nm].size + -weights[nm].size % PACK_ALIGN
    loss = rep_sum[pos]
    grads = {nm: grads[nm].reshape(weights[nm].shape) for nm in _NAMES}
    deltas, new_m, new_v = _adamw_all([weights[nm] for nm in _NAMES], [grads[nm] for nm in _NAMES],
                                      [moms[nm] for nm in _NAMES], [vars_[nm] for nm in _NAMES])
    return (loss, dx.reshape(1, SEQ, D_MODEL), *[grads[nm] for nm in _NAMES], *deltas, *new_m, *new_v)
```

```python
import math

import jax
import jax.numpy as jnp
from jax import lax
from jax.experimental import pallas as pl
from jax.experimental.pallas import tpu as pltpu

F32 = jnp.float32
BF16 = jnp.bfloat16

D_MODEL = 1024
SEQ = 2048
N_DEV = 8
MLA_HEADS = 8
Q_LORA_RANK = 384
KV_LORA_RANK = 128
QK_NOPE_DIM = 64
QK_ROPE_DIM = 32
V_HEAD_DIM = 64
QK_HEAD_DIM = QK_NOPE_DIM + QK_ROPE_DIM
MLA_WIDTH = MLA_HEADS * V_HEAD_DIM
ROPE_THETA = 10000.0
SGU_GROUPS = 8
SGU_GROUP_DIM = 64
SGU_WIDTH = SGU_GROUPS * SGU_GROUP_DIM
CHUNK = 128
RMS_EPS = 1e-6
LN_EPS = 1e-5
DN_ALPHA = 2.0 ** 0.25
IN_WIDTH = 4640
ATTN_SCALE = QK_HEAD_DIM ** -0.5

ADAM_LR = 0.001
ADAM_B1 = 0.9
ADAM_B2 = 0.999
ADAM_EPS = 1e-08
ADAM_WD = 0.01
ADAM_STEP = 10

LANES = 128
HEAD_PAD = 128
ROPE_LO = QK_NOPE_DIM
ROPE_MID = ROPE_LO + QK_ROPE_DIM // 2
ROPE_HI = ROPE_LO + QK_ROPE_DIM
CQ_PAD = 512

SEG_A = 2560
SEG_B = 1536
SEG_C = 768

PACK_R_ROWS = 272
PACK_ALIGN = 8 * LANES
VMEM_BIG = 56 * 1024 * 1024
VMEM_MID = 40 * 1024 * 1024


def _sigmoid(x):
    return 1.0 / (1.0 + jnp.exp(-x))


def _gelu_and_grad(x):
    c0 = math.sqrt(2.0 / math.pi)
    x2 = x * x
    t = jnp.tanh(c0 * (x + 0.044715 * x * x2))
    g = 0.5 * x * (1.0 + t)
    dg = 0.5 * (1.0 + t) + 0.5 * x * (1.0 - t * t) * (c0 * (1.0 + 3.0 * 0.044715 * x2))
    return g, dg


def _dot(a, b, dims):
    return lax.dot_general(a, b, (dims, ((), ())), preferred_element_type=F32)


_NN = ((1,), (0,))
_NT = ((1,), (1,))
_TN = ((0,), (0,))


def _store_grad(dh_ref, db_ref, col, val):
    cols = slice(col, col + val.shape[1])
    dh_ref[:, cols] = val.astype(BF16)
    db_ref[:, cols] += jnp.sum(val, axis=0, keepdims=True)


def _mm(a, b, *, tb=False, bias=None, add=None, out_dtype=F32, own=(), parts=(), tm, tn, name):
    m, k = a.shape
    n = b.shape[0] if tb else b.shape[1]
    assert m % tm == 0 and n % tn == 0 and not (own and parts)
    dims = _NT if tb else _NN
    nown = len(own) + len(parts)
    nm = m // tm
    nsteps = (n // tn) * nm

    def body(*refs):
        a_ref, b_ref = refs[0], refs[1]
        pos = 2
        r = _dot(a_ref[...], b_ref[...], dims)
        if bias is not None:
            r = r + refs[pos][...]; pos += 1
        if add is not None:
            r = r + refs[pos][...]; pos += 1
        own_refs = refs[pos:pos + nown]; pos += nown
        refs[pos][...] = r.astype(out_dtype)
        if nown:
            gat_refs = refs[pos + 1:pos + 1 + nown]
            send_sems, recv_sems, local_sems = refs[pos + 1 + nown:]
            step = pl.program_id(0) * nm + pl.program_id(1)
            if own:
                _gather_behind(own_refs, gat_refs, send_sems, recv_sems, local_sems, step, nsteps - 2, nsteps - 1)
            else:
                exchange = _exchange_parts(own_refs, gat_refs, send_sems, recv_sems, local_sems)
                _exchange_start(step == 0, exchange)
                _exchange_finish(step == nsteps - 1, exchange)

    b_spec = pl.BlockSpec((tn, k), lambda j, i: (j, 0)) if tb else pl.BlockSpec((k, tn), lambda j, i: (0, j))
    in_specs, args = [pl.BlockSpec((tm, k), lambda j, i: (i, 0)), b_spec], [a, b]
    if bias is not None:
        in_specs.append(pl.BlockSpec((1, tn), lambda j, i: (0, j))); args.append(bias)
    if add is not None:
        in_specs.append(pl.BlockSpec((tm, tn), lambda j, i: (i, j))); args.append(add)
    hbm = pl.BlockSpec(memory_space=pl.ANY)
    res = pl.pallas_call(
        body, name=name, grid=(n // tn, nm), in_specs=in_specs + [hbm] * nown,
        out_specs=[pl.BlockSpec((tm, tn), lambda j, i: (i, j))] + [hbm] * nown,
        out_shape=[jax.ShapeDtypeStruct((m, n), out_dtype)]
        + [jax.ShapeDtypeStruct((N_DEV,) + o.shape, o.dtype) for o in own]
        + [jax.ShapeDtypeStruct(p.shape, p.dtype) for p in parts],
        scratch_shapes=_exchange_sems(nown) if nown else [],
        compiler_params=pltpu.CompilerParams(dimension_semantics=("arbitrary", "arbitrary"), vmem_limit_bytes=VMEM_BIG),
    )(*args, *own, *parts)
    return (res[0], res[1:]) if nown else res[0]


def _rope(x, c, sa, sb):
    return x * c + pltpu.roll(x, LANES - 16, 1) * sa + pltpu.roll(x, 16, 1) * sb


def _rope_t(dy, c, sa, sb):
    return dy * c + pltpu.roll(dy * sa, 16, 1) + pltpu.roll(dy * sb, LANES - 16, 1)


def _mla_prep(h_c, gq, gkv, wq, wkn, wvx, c_t, sa_t, sb_t):
    tm = 256
    hw = MLA_HEADS * HEAD_PAD

    def body(cq_ref, ckv_ref, kpe_ref, gq_ref, gkv_ref, wq_ref, wkn_ref, wvx_ref, c_ref, sa_ref, sb_ref,
             q_ref, k_ref, kt_ref, vx_ref, vxt_ref):
        c, sa, sb = c_ref[...], sa_ref[...], sb_ref[...]
        cq = cq_ref[...]
        rq = lax.rsqrt(jnp.sum(cq * cq, axis=1, keepdims=True) * (1.0 / Q_LORA_RANK) + RMS_EPS)
        cqn = ((cq * rq) * gq_ref[...]).astype(BF16)
        qall = _dot(cqn, wq_ref[...], _NN)
        for h in range(MLA_HEADS):
            sl = slice(HEAD_PAD * h, HEAD_PAD * (h + 1))
            q_ref[:, sl] = (_rope(qall[:, sl], c, sa, sb) * ATTN_SCALE).astype(BF16)
        ckv = ckv_ref[...]
        rkv = lax.rsqrt(jnp.sum(ckv * ckv, axis=1, keepdims=True) * (1.0 / KV_LORA_RANK) + RMS_EPS)
        ckvn = ((ckv * rkv) * gkv_ref[...]).astype(BF16)
        knall = _dot(ckvn, wkn_ref[...], _NN)
        vall = _dot(ckvn, wvx_ref[...], _NN)
        kper = _rope(kpe_ref[...], c, sa, sb)
        ones_half = (lax.broadcasted_iota(jnp.int32, (tm, HEAD_PAD), 1) >= V_HEAD_DIM).astype(F32)
        for h in range(MLA_HEADS):
            sl = slice(HEAD_PAD * h, HEAD_PAD * (h + 1))
            kh = knall[:, sl] + kper
            vh = vall[:, sl] + ones_half
            k_ref[:, sl] = kh.astype(BF16)
            kt_ref[sl, :] = kh.T.astype(BF16)
            vx_ref[:, sl] = vh.astype(BF16)
            vxt_ref[sl, :] = vh.T.astype(BF16)

    full = lambda shape: pl.BlockSpec(shape, lambda i: (0, 0))
    tab = pl.BlockSpec((tm, LANES), lambda i: (i, 0))
    row = pl.BlockSpec((tm, hw), lambda i: (i, 0))
    col = pl.BlockSpec((hw, tm), lambda i: (0, i))
    return pl.pallas_call(
        body, name="mla_prep", grid=(SEQ // tm,),
        in_specs=[pl.BlockSpec((tm, CQ_PAD), lambda i: (i, 0)),
                  pl.BlockSpec((tm, LANES), lambda i: (i, CQ_PAD // LANES)),
                  pl.BlockSpec((tm, LANES), lambda i: (i, CQ_PAD // LANES + 1)),
                  full((1, CQ_PAD)), full((1, KV_LORA_RANK)),
                  full((CQ_PAD, hw)), full((KV_LORA_RANK, hw)), full((KV_LORA_RANK, hw)), tab, tab, tab],
        out_specs=[row, row, col, row, col],
        out_shape=[jax.ShapeDtypeStruct((SEQ, hw), BF16), jax.ShapeDtypeStruct((SEQ, hw), BF16),
                   jax.ShapeDtypeStruct((hw, SEQ), BF16), jax.ShapeDtypeStruct((SEQ, hw), BF16),
                   jax.ShapeDtypeStruct((hw, SEQ), BF16)],
        compiler_params=pltpu.CompilerParams(dimension_semantics=("arbitrary",), vmem_limit_bytes=VMEM_MID),
    )(h_c, h_c, h_c, gq, gkv, wq, wkn, wvx, c_t, sa_t, sb_t)


ATT_T = 512
ATT_STRIP = 64


def _attn_fwd(q, kt, vx, own):
    t, rs = ATT_T, ATT_STRIP
    nown = len(own)
    nq = SEQ // t
    nsteps = (MLA_HEADS // 2) * nq

    def body(q_ref, kt_ref, vx_ref, *rest):
        own_refs, (o_ref, l_ref), gat_refs = rest[:nown], rest[nown:nown + 2], rest[nown + 2:2 * nown + 2]
        s_scr, p_scr, m_scr, a_scr, acc_scr, send_sems, recv_sems, local_sems = rest[2 * nown + 2:]
        qi = pl.program_id(1)
        lane = lax.broadcasted_iota(jnp.int32, (t, LANES), 1)
        m_scr[...] = jnp.full((2, t, LANES), -1e30, F32)
        acc_scr[...] = jnp.zeros((2, t, LANES), F32)

        def block(j, masked):
            off = pl.multiple_of(j * t, t)
            for a in range(2):
                sl = slice(HEAD_PAD * a, HEAD_PAD * (a + 1))
                s_scr[a] = _dot(q_ref[:, sl], kt_ref[sl, pl.ds(off, t)], _NN)
                for r in range(t // rs):
                    rows = slice(rs * r, rs * (r + 1))
                    s = s_scr[a, rows, :]
                    if masked:
                        rowi = lax.broadcasted_iota(jnp.int32, (rs, t), 0) + rs * r
                        coli = lax.broadcasted_iota(jnp.int32, (rs, t), 1)
                        s = jnp.where(coli <= rowi, s, -1e30)
                    m_old = m_scr[a, rows, :]
                    m_new = jnp.maximum(m_old, jnp.max(s, axis=1, keepdims=True))
                    p_scr[a, rows, :] = jnp.exp(s - m_new[:, :1]).astype(BF16)
                    a_scr[a, rows, :] = jnp.exp(m_old - m_new)
                    m_scr[a, rows, :] = m_new
                acc_scr[a] = acc_scr[a] * a_scr[a] + _dot(p_scr[a], vx_ref[pl.ds(off, t), sl], _NN)

        def step(j, carry):
            block(j, False)
            return carry
        lax.fori_loop(0, qi, step, 0)
        block(qi, True)
        res = []
        for a in range(2):
            acc = acc_scr[a]
            l = acc[:, V_HEAD_DIM:V_HEAD_DIM + 1]
            res.append((acc / l, m_scr[a] + jnp.log(l)))
        o_ref[...] = jnp.where(lane < V_HEAD_DIM, res[0][0], pltpu.roll(res[1][0], V_HEAD_DIM, 1))
        l_ref[...] = jnp.where(lane < V_HEAD_DIM, res[0][1], res[1][1])
        _gather_behind(own_refs, gat_refs, send_sems, recv_sems, local_sems, pl.program_id(0) * nq + qi,
                       nsteps - 2, nsteps - 1)

    hbm = pl.BlockSpec(memory_space=pl.ANY)
    res = pl.pallas_call(
        body, name="attn_fwd", grid=(MLA_HEADS // 2, nq),
        in_specs=[pl.BlockSpec((t, 2 * HEAD_PAD), lambda p, i: (i, p)),
                  pl.BlockSpec((2 * HEAD_PAD, SEQ), lambda p, i: (p, 0)),
                  pl.BlockSpec((SEQ, 2 * HEAD_PAD), lambda p, i: (0, p))] + [hbm] * nown,
        out_specs=[pl.BlockSpec((t, LANES), lambda p, i: (i, p)),
                   pl.BlockSpec((t, LANES), lambda p, i: (i, p))] + [hbm] * nown,
        out_shape=[jax.ShapeDtypeStruct((SEQ, MLA_WIDTH), F32), jax.ShapeDtypeStruct((SEQ, MLA_WIDTH), F32)]
        + [jax.ShapeDtypeStruct((N_DEV,) + a.shape, a.dtype) for a in own],
        scratch_shapes=[pltpu.VMEM((2, t, t), F32), pltpu.VMEM((2, t, t), BF16), pltpu.VMEM((2, t, LANES), F32),
                        pltpu.VMEM((2, t, LANES), F32), pltpu.VMEM((2, t, LANES), F32)] + _exchange_sems(nown),
        compiler_params=pltpu.CompilerParams(dimension_semantics=("arbitrary", "arbitrary"), vmem_limit_bytes=VMEM_MID),
    )(q, kt, vx, *own)
    return res[0], res[1], res[2:]


def _exchange_parts(parts, lands, send_sems, recv_sems, local_sems):
    x, y, c = _mesh_pos()
    me = 4 * x + 2 * y + c
    peers = [(x, y, 1 - c), (1 - x, y, c), (x, 1 - y, c), (1 - x, 1 - y, c),
             (1 - x, y, 1 - c), (x, 1 - y, 1 - c), (1 - x, 1 - y, 1 - c)]
    remote, local = [], []
    for a, (part, land) in enumerate(zip(parts, lands)):
        for k, peer in enumerate(peers):
            t = 4 * peer[0] + 2 * peer[1] + peer[2]
            remote.append(_remote(part.at[t], land.at[me], send_sems, recv_sems, 7 * a + k, peer))
        local.append(pltpu.make_async_copy(part.at[me], land.at[me], local_sems.at[a]))
    return remote, local


def _exchange_start(first_step, exchange):
    remote, local = exchange

    @pl.when(first_step)
    def _():
        for cp in remote + local:
            cp.start()


def _exchange_finish(last_step, exchange):
    remote, local = exchange

    @pl.when(last_step)
    def _():
        for cp in remote:
            cp.wait_recv()
        for cp in remote:
            cp.wait_send()
        for cp in local:
            cp.wait()


def _exchange_sems(npart):
    return [pltpu.SemaphoreType.DMA((7 * npart,)), pltpu.SemaphoreType.DMA((7 * npart,)),
            pltpu.SemaphoreType.DMA((npart,))]


def _attn_bwd(q, kt, k, vxt, d_o, o, lse, parts):
    t, rs = ATT_T, ATT_STRIP
    nq = SEQ // t
    npart = len(parts)
    nsteps = MLA_HEADS // 2

    def body(q_ref, kt_ref, k_ref, vxt_ref, do_ref, o_ref, l_ref, *rest):
        part_refs, rest = rest[:npart], rest[npart:]
        dq_ref, dk_ref, dv_ref = rest[:3]
        land_refs, rest = rest[3:3 + npart], rest[3 + npart:]
        s_scr, dp_scr, p_scr, ds_scr, st_scr, send_sems, recv_sems, local_sems = rest
        exchange = _exchange_parts(part_refs, land_refs, send_sems, recv_sems, local_sems)
        _exchange_start(pl.program_id(0) == 0, exchange)
        dk_ref[...] = jnp.zeros_like(dk_ref)
        dv_ref[...] = jnp.zeros_like(dv_ref)
        lane = lax.broadcasted_iota(jnp.int32, (t, LANES), 1)

        def qtile(i, carry):
            ioff = pl.multiple_of(i * t, t)
            do_i = do_ref[pl.ds(ioff, t), :]
            o_i = o_ref[pl.ds(ioff, t), :]
            l_i = l_ref[pl.ds(ioff, t), :]
            for a in range(2):
                sl = slice(HEAD_PAD * a, HEAD_PAD * (a + 1))
                sel = (lane < V_HEAD_DIM) if a == 0 else (lane >= V_HEAD_DIM)
                doa = jnp.where(sel, do_i, 0.0)
                oa = o_i
                if a == 1:
                    doa = pltpu.roll(doa, V_HEAD_DIM, 1)
                    oa = pltpu.roll(o_i, V_HEAD_DIM, 1)
                st_scr[0] = jnp.broadcast_to(jnp.sum(doa * oa, axis=1, keepdims=True), (t, LANES))
                st_scr[1] = jnp.broadcast_to(l_i[:, V_HEAD_DIM * a:V_HEAD_DIM * a + 1], (t, LANES))
                doa_bf = doa.astype(BF16)
                qa = q_ref[pl.ds(ioff, t), sl]

                def block(j, masked, dq_acc, sl=sl, qa=qa, doa_bf=doa_bf):
                    joff = pl.multiple_of(j * t, t)
                    s_scr[...] = _dot(qa, kt_ref[sl, pl.ds(joff, t)], _NN)
                    dp_scr[...] = _dot(doa_bf, vxt_ref[sl, pl.ds(joff, t)], _NN)
                    for r in range(t // rs):
                        rows = slice(rs * r, rs * (r + 1))
                        p = jnp.exp(s_scr[rows, :] - st_scr[1, rows, :1])
                        if masked:
                            rowi = lax.broadcasted_iota(jnp.int32, (rs, t), 0) + rs * r
                            coli = lax.broadcasted_iota(jnp.int32, (rs, t), 1)
                            p = jnp.where(coli <= rowi, p, 0.0)
                        p_scr[rows, :] = p.astype(BF16)
                        ds_scr[rows, :] = (p * (dp_scr[rows, :] - st_scr[0, rows, :1])).astype(BF16)
                    dk_ref[pl.ds(joff, t), sl] += _dot(ds_scr[...], qa, _TN)
                    dv_ref[pl.ds(joff, t), sl] += _dot(p_scr[...], doa_bf, _TN)
                    return dq_acc + _dot(ds_scr[...], k_ref[pl.ds(joff, t), sl], _NN)

                dq_acc = lax.fori_loop(0, i, lambda j, acc: block(j, False, acc), jnp.zeros((t, HEAD_PAD), F32))
                dq_ref[pl.ds(ioff, t), sl] = block(i, True, dq_acc)
            return carry

        lax.fori_loop(0, nq, qtile, 0)
        _exchange_finish(pl.program_id(0) == nsteps - 1, exchange)

    hw = MLA_HEADS * HEAD_PAD
    wide = pl.BlockSpec((SEQ, 2 * HEAD_PAD), lambda p: (0, p))
    wide_t = pl.BlockSpec((2 * HEAD_PAD, SEQ), lambda p: (p, 0))
    narrow = pl.BlockSpec((SEQ, LANES), lambda p: (0, p))
    hbm = pl.BlockSpec(memory_space=pl.ANY)
    res = pl.pallas_call(
        body, name="attn_bwd", grid=(nsteps,),
        in_specs=[wide, wide_t, wide, wide_t, narrow, narrow, narrow] + [hbm] * npart,
        out_specs=[wide, wide, wide] + [hbm] * npart,
        out_shape=[jax.ShapeDtypeStruct((SEQ, hw), F32)] * 3 + [jax.ShapeDtypeStruct(p.shape, p.dtype) for p in parts],
        scratch_shapes=[pltpu.VMEM((t, t), F32), pltpu.VMEM((t, t), F32), pltpu.VMEM((t, t), BF16),
                        pltpu.VMEM((t, t), BF16), pltpu.VMEM((2, t, LANES), F32)] + _exchange_sems(npart),
        compiler_params=pltpu.CompilerParams(dimension_semantics=("arbitrary",), vmem_limit_bytes=VMEM_BIG),
    )(q, kt, k, vxt, d_o, o, lse, *parts)
    return res[0], res[1], res[2], res[3:]


def _sgu_math(u, v, zb, lg, lb, ws_ref, bias):
    ug, dug = _gelu_and_grad(u)
    vg, dvg = _gelu_and_grad(v)
    mu = jnp.mean(vg, axis=1, keepdims=True)
    xc = vg - mu
    rstd = lax.rsqrt(jnp.mean(xc * xc, axis=1, keepdims=True) + LN_EPS)
    xh = xc * rstd
    vn_bf = (xh * lg + lb).astype(BF16)
    grp = lax.broadcasted_iota(jnp.int32, (CHUNK, SGU_WIDTH), 1) // SGU_GROUP_DIM
    r_i = lax.broadcasted_iota(jnp.int32, (CHUNK, CHUNK), 0)
    c_i = lax.broadcasted_iota(jnp.int32, (CHUNK, CHUNK), 1)
    tri, tri_t = r_i >= c_i, r_i <= c_i
    mixed = bias
    for g in range(SGU_GROUPS):
        wt = jnp.where(tri, ws_ref[g], 0.0).astype(BF16)
        mixed = mixed + jnp.where(grp == g, _dot(wt, vn_bf, _NN), 0.0)
    sb = _sigmoid(zb)
    return ug, dug, dvg, rstd, xh, vn_bf, grp, tri, tri_t, mixed, sb


def _sgu_fwd(h_b, lg, lb, w_s, bias_full):
    def body(u_ref, v_ref, zb_ref, lg_ref, lb_ref, ws_ref, bias_ref, yb_ref):
        zb = zb_ref[...]
        ug, _, _, _, _, _, _, _, _, mixed, sb = _sgu_math(u_ref[...], v_ref[...], zb, lg_ref[...], lb_ref[...],
                                                       ws_ref, bias_ref[...])
        yb_ref[...] = (ug * mixed) * (zb * sb)

    blk = lambda c: pl.BlockSpec((CHUNK, SGU_WIDTH), lambda i, c=c: (i, c))
    full2 = lambda shape: pl.BlockSpec(shape, lambda i: (0, 0))
    return pl.pallas_call(
        body, name="sgu_fwd", grid=(SEQ // CHUNK,),
        in_specs=[blk(0), blk(1), blk(2), full2((1, SGU_WIDTH)), full2((1, SGU_WIDTH)),
                  pl.BlockSpec((SGU_GROUPS, CHUNK, CHUNK), lambda i: (0, 0, 0)), full2((CHUNK, SGU_WIDTH))],
        out_specs=pl.BlockSpec((CHUNK, SGU_WIDTH), lambda i: (i, 0)),
        out_shape=jax.ShapeDtypeStruct((SEQ, SGU_WIDTH), F32),
        compiler_params=pltpu.CompilerParams(dimension_semantics=("arbitrary",)),
    )(h_b, h_b, h_b, lg, lb, w_s, bias_full)


def _sgu_bwd(h_b, d_yb, lg, lb, w_s, w_st, bias_full, parts):
    nsteps = SEQ // CHUNK
    npart = len(parts)

    def body(u_ref, v_ref, zb_ref, dyb_ref, lg_ref, lb_ref, ws_ref, wst_ref, bias_ref, *rest):
        part_refs, rest = rest[:npart], rest[npart:]
        dhb_ref, dws_ref, dbs_ref, dlg_ref, dlb_ref, dbb_ref = rest[:6]
        land_refs, (dbias_acc, send_sems, recv_sems, local_sems) = rest[6:6 + npart], rest[6 + npart:]
        step = pl.program_id(0)
        exchange = _exchange_parts(part_refs, land_refs, send_sems, recv_sems, local_sems)
        _exchange_start(step == 0, exchange)

        @pl.when(step == 0)
        def _():
            dbb_ref[...] = jnp.zeros_like(dbb_ref)
            dws_ref[...] = jnp.zeros_like(dws_ref)
            dlg_ref[...] = jnp.zeros_like(dlg_ref)
            dlb_ref[...] = jnp.zeros_like(dlb_ref)
            dbias_acc[...] = jnp.zeros_like(dbias_acc)

        zb = zb_ref[...]
        lg = lg_ref[...]
        ug, dug, dvg, rstd, xh, vn_bf, grp, tri, tri_t, mixed, sb = _sgu_math(
            u_ref[...], v_ref[...], zb, lg, lb_ref[...], ws_ref, bias_ref[...])
        dyb = dyb_ref[...]
        dsgu = dyb * (zb * sb)
        dzb = dyb * (ug * mixed) * (sb * (1.0 + zb * (1.0 - sb)))
        du = dsgu * mixed * dug
        dmixed = dsgu * ug
        dbias_acc[...] += dmixed
        dvn = jnp.zeros((CHUNK, SGU_WIDTH), F32)
        for g in range(SGU_GROUPS):
            dm_g = jnp.where(grp == g, dmixed, 0.0).astype(BF16)
            wtt = jnp.where(tri_t, wst_ref[g], 0.0).astype(BF16)
            dvn = dvn + _dot(wtt, dm_g, _NN)
            dws_ref[g] += jnp.where(tri, _dot(dm_g, vn_bf, _NT), 0.0)
        dlg_ref[...] += jnp.sum(dvn * xh, axis=0, keepdims=True)
        dlb_ref[...] += jnp.sum(dvn, axis=0, keepdims=True)
        dxh = dvn * lg
        dvgel = rstd * (dxh - jnp.mean(dxh, axis=1, keepdims=True) - xh * jnp.mean(dxh * xh, axis=1, keepdims=True))
        _store_grad(dhb_ref, dbb_ref, 0, du)
        _store_grad(dhb_ref, dbb_ref, SGU_WIDTH, dvgel * dvg)
        _store_grad(dhb_ref, dbb_ref, 2 * SGU_WIDTH, dzb)

        @pl.when(step == nsteps - 1)
        def _():
            acc = dbias_acc[...]
            lane = lax.broadcasted_iota(jnp.int32, (CHUNK, LANES), 1)
            out = jnp.zeros((CHUNK, LANES), F32)
            for g in range(SGU_GROUPS):
                sg = jnp.sum(jnp.where(grp == g, acc, 0.0), axis=1, keepdims=True)
                out = jnp.where(lane == g, sg, out)
            dbs_ref[...] = out

        _exchange_finish(step == nsteps - 1, exchange)

    blk = lambda c: pl.BlockSpec((CHUNK, SGU_WIDTH), lambda i, c=c: (i, c))
    full2 = lambda shape: pl.BlockSpec(shape, lambda i: (0, 0))
    full3 = pl.BlockSpec((SGU_GROUPS, CHUNK, CHUNK), lambda i: (0, 0, 0))
    hbm = pl.BlockSpec(memory_space=pl.ANY)
    res = pl.pallas_call(
        body, name="sgu_bwd", grid=(nsteps,),
        in_specs=[blk(0), blk(1), blk(2), pl.BlockSpec((CHUNK, SGU_WIDTH), lambda i: (i, 0)),
                  full2((1, SGU_WIDTH)), full2((1, SGU_WIDTH)), full3, full3, full2((CHUNK, SGU_WIDTH))] + [hbm] * npart,
        out_specs=[pl.BlockSpec((CHUNK, SEG_B), lambda i: (i, 0)), full3, full2((CHUNK, LANES)),
                   full2((1, SGU_WIDTH)), full2((1, SGU_WIDTH)), full2((1, SEG_B))] + [hbm] * npart,
        out_shape=[jax.ShapeDtypeStruct((SEQ, SEG_B), BF16),
                   jax.ShapeDtypeStruct((SGU_GROUPS, CHUNK, CHUNK), F32),
                   jax.ShapeDtypeStruct((CHUNK, LANES), F32),
                   jax.ShapeDtypeStruct((1, SGU_WIDTH), F32), jax.ShapeDtypeStruct((1, SGU_WIDTH), F32),
                   jax.ShapeDtypeStruct((1, SEG_B), F32)] + [jax.ShapeDtypeStruct(p.shape, p.dtype) for p in parts],
        scratch_shapes=[pltpu.VMEM((CHUNK, SGU_WIDTH), F32)] + _exchange_sems(npart),
        compiler_params=pltpu.CompilerParams(dimension_semantics=("arbitrary",)),
    )(h_b, h_b, h_b, d_yb, lg, lb, w_s, w_st, bias_full, *parts)
    return res[:6], res[6:]


def _merge(x, o, h_a, y_b, target, w_oa, w_ob, w_out, ln_g, ln_b):
    tm = 256
    nsteps = SEQ // tm

    def body(x_ref, o_ref, ga_ref, gb_ref, za_ref, yb_ref, tgt_ref, woa_ref, wob_ref, wout_ref, lng_ref, lnb_ref,
             loss_ref, dxr_ref, dha_ref, do_ref, dyb_ref, poa_ref, pob_ref, pout_ref, dlng_ref, dlnb_ref, dba_ref,
             dwoa_ref, dwob_ref, dwout_ref):
        step = pl.program_id(0)

        @pl.when(step == 0)
        def _():
            for r in (loss_ref, dwoa_ref, dwob_ref, dwout_ref, dlng_ref, dlnb_ref, dba_ref):
                r[...] = jnp.zeros_like(r)

        o = o_ref[...]
        za = za_ref[...]
        sa = _sigmoid(za)
        ya_bf = (o * (za * sa)).astype(BF16)
        yb_bf = yb_ref[...].astype(BF16)
        woa, wob, wout = woa_ref[...], wob_ref[...], wout_ref[...]
        pa = _dot(ya_bf, woa, _NN)
        pb = _dot(yb_bf, wob, _NN)
        sga = _sigmoid(ga_ref[...])
        sgb = _sigmoid(gb_ref[...])
        merged_bf = (sga * pa + sgb * pb).astype(BF16)
        r = DN_ALPHA * x_ref[...] + _dot(merged_bf, wout, _NN)
        mu = jnp.mean(r, axis=1, keepdims=True)
        rc = r - mu
        rstd = lax.rsqrt(jnp.mean(rc * rc, axis=1, keepdims=True) + LN_EPS)
        xh = rc * rstd
        lng = lng_ref[...]
        y = xh * lng + lnb_ref[...]
        e = y - tgt_ref[...]
        loss_ref[...] += 0.5 * jnp.sum(jnp.sum(e * e, axis=1, keepdims=True) * (1.0 / D_MODEL), axis=0, keepdims=True)

        dy = e * (1.0 / D_MODEL)
        dlng_ref[...] += jnp.sum(dy * xh, axis=0, keepdims=True)
        dlnb_ref[...] += jnp.sum(dy, axis=0, keepdims=True)
        dxh = dy * lng
        dr = rstd * (dxh - jnp.mean(dxh, axis=1, keepdims=True) - xh * jnp.mean(dxh * xh, axis=1, keepdims=True))
        dxr_ref[...] = DN_ALPHA * dr
        dr_bf = dr.astype(BF16)
        dwout_ref[...] += _dot(merged_bf, dr_bf, _TN)
        dmerged = _dot(dr_bf, wout, _NT)
        dpa_bf = (dmerged * sga).astype(BF16)
        dpb_bf = (dmerged * sgb).astype(BF16)
        _store_grad(dha_ref, dba_ref, 0, dmerged * pa * (sga * (1.0 - sga)))
        _store_grad(dha_ref, dba_ref, D_MODEL, dmerged * pb * (sgb * (1.0 - sgb)))
        dwoa_ref[...] += _dot(ya_bf, dpa_bf, _TN)
        dwob_ref[...] += _dot(yb_bf, dpb_bf, _TN)
        dya = _dot(dpa_bf, woa, _NT)
        dyb_ref[...] = _dot(dpb_bf, wob, _NT)
        do_ref[...] = dya * (za * sa)
        _store_grad(dha_ref, dba_ref, 2 * D_MODEL, dya * o * (sa * (1.0 + za * (1.0 - sa))))

        @pl.when(step == nsteps - 1)
        def _():
            cols = D_MODEL // N_DEV
            for j in range(N_DEV):
                poa_ref[j] = dwoa_ref[:, cols * j:cols * (j + 1)].astype(BF16)
                pob_ref[j] = dwob_ref[:, cols * j:cols * (j + 1)].astype(BF16)
                pout_ref[j] = dwout_ref[cols * j:cols * (j + 1), :].astype(BF16)

    row = lambda w, c=0: pl.BlockSpec((tm, w), lambda i, c=c: (i, c))
    full = lambda shape: pl.BlockSpec(shape, lambda i: (0, 0))
    full3 = lambda shape: pl.BlockSpec(shape, lambda i: (0, 0, 0))
    return pl.pallas_call(
        body, name="merge", grid=(nsteps,),
        in_specs=[row(D_MODEL), row(MLA_WIDTH), row(D_MODEL, 0), row(D_MODEL, 1), row(MLA_WIDTH, 4), row(SGU_WIDTH),
                  row(D_MODEL), full((MLA_WIDTH, D_MODEL)), full((SGU_WIDTH, D_MODEL)), full((D_MODEL, D_MODEL)),
                  full((1, D_MODEL)), full((1, D_MODEL))],
        out_specs=[full((1, LANES)), row(D_MODEL), row(SEG_A), row(MLA_WIDTH), row(SGU_WIDTH),
                   full3((N_DEV, MLA_WIDTH, D_MODEL // N_DEV)), full3((N_DEV, SGU_WIDTH, D_MODEL // N_DEV)),
                   full3((N_DEV, D_MODEL // N_DEV, D_MODEL)), full((1, D_MODEL)), full((1, D_MODEL)), full((1, SEG_A))],
        out_shape=[jax.ShapeDtypeStruct((1, LANES), F32),
                   jax.ShapeDtypeStruct((SEQ, D_MODEL), F32), jax.ShapeDtypeStruct((SEQ, SEG_A), BF16),
                   jax.ShapeDtypeStruct((SEQ, MLA_WIDTH), F32), jax.ShapeDtypeStruct((SEQ, SGU_WIDTH), F32),
                   jax.ShapeDtypeStruct((N_DEV, MLA_WIDTH, D_MODEL // N_DEV), BF16),
                   jax.ShapeDtypeStruct((N_DEV, SGU_WIDTH, D_MODEL // N_DEV), BF16),
                   jax.ShapeDtypeStruct((N_DEV, D_MODEL // N_DEV, D_MODEL), BF16),
                   jax.ShapeDtypeStruct((1, D_MODEL), F32), jax.ShapeDtypeStruct((1, D_MODEL), F32),
                   jax.ShapeDtypeStruct((1, SEG_A), F32)],
        scratch_shapes=[pltpu.VMEM((MLA_WIDTH, D_MODEL), F32), pltpu.VMEM((SGU_WIDTH, D_MODEL), F32),
                        pltpu.VMEM((D_MODEL, D_MODEL), F32)],
        compiler_params=pltpu.CompilerParams(dimension_semantics=("arbitrary",), vmem_limit_bytes=VMEM_BIG),
    )(x, o, h_a, h_a, h_a, y_b, target, w_oa, w_ob, w_out, ln_g, ln_b)


def _mla_bwd(dq, dk, dv, h_c, xt_bf, gq, gkv, wq, wkn, wv, c_t, sa_t, sb_t, parts):
    tm = 256
    hw = MLA_HEADS * HEAD_PAD
    npart = len(parts)
    nsteps = SEQ // tm

    def body(dq_ref, dk_ref, dv_ref, cq_ref, ckv_ref, xt_ref, gq_ref, gkv_ref, wq_ref, wkn_ref, wv_ref, c_ref, sa_ref,
             sb_ref, *rest):
        part_refs, rest = rest[:npart], rest[npart:]
        dhc_ref, puq_ref, dwkn_ref, dwv_ref, dgq_ref, dgkv_ref, dbc_ref, dwc_ref = rest[:8]
        land_refs, (pre_ref, dwq_ref, dwc_acc, send_sems, recv_sems, local_sems) = rest[8:8 + npart], rest[8 + npart:]
        exchange = _exchange_parts(part_refs, land_refs, send_sems, recv_sems, local_sems)
        _exchange_start(pl.program_id(0) == 0, exchange)

        @pl.when(pl.program_id(0) == 0)
        def _():
            for r in (dwq_ref, dwc_acc, dwkn_ref, dwv_ref, dgq_ref, dgkv_ref, dbc_ref):
                r[...] = jnp.zeros_like(r)

        c, sa, sb = c_ref[...], sa_ref[...], sb_ref[...]
        lane = lax.broadcasted_iota(jnp.int32, (tm, LANES), 1)
        rope_lanes = jnp.logical_and(lane >= ROPE_LO, lane < ROPE_HI)

        cq = cq_ref[...]
        gq = gq_ref[...]
        rq = lax.rsqrt(jnp.sum(cq * cq, axis=1, keepdims=True) * (1.0 / Q_LORA_RANK) + RMS_EPS)
        nq = cq * rq
        cqn_bf = (nq * gq).astype(BF16)
        for h in range(MLA_HEADS):
            sl = slice(HEAD_PAD * h, HEAD_PAD * (h + 1))
            pre_ref[:, sl] = _rope_t(dq_ref[:, sl] * ATTN_SCALE, c, sa, sb).astype(BF16)
        dqpre_bf = pre_ref[...]
        dcqn = _dot(dqpre_bf, wq_ref[...], _NT)
        dwq_ref[...] += _dot(cqn_bf, dqpre_bf, _TN)
        dgq_ref[...] += jnp.sum(dcqn * nq, axis=0, keepdims=True)
        dnq = dcqn * gq
        _store_grad(dhc_ref, dbc_ref, 0,
                    rq * (dnq - nq * (jnp.sum(dnq * nq, axis=1, keepdims=True) * (1.0 / Q_LORA_RANK))))

        ckv = ckv_ref[...]
        gkv = gkv_ref[...]
        rkv = lax.rsqrt(jnp.sum(ckv * ckv, axis=1, keepdims=True) * (1.0 / KV_LORA_RANK) + RMS_EPS)
        nkv = ckv * rkv
        ckvn_bf = (nkv * gkv).astype(BF16)
        dk = dk_ref[...]
        dk_bf = dk.astype(BF16)
        dv_bf = dv_ref[...].astype(BF16)
        dckvn = _dot(dk_bf, wkn_ref[...], _NT) + _dot(dv_bf, wv_ref[...], _NT)
        dwkn_ref[...] += _dot(ckvn_bf, dk_bf, _TN)
        dwv_ref[...] += _dot(ckvn_bf, dv_bf, _TN)
        dgkv_ref[...] += jnp.sum(dckvn * nkv, axis=0, keepdims=True)
        dnkv = dckvn * gkv
        _store_grad(dhc_ref, dbc_ref, CQ_PAD, rkv * (
            dnkv - nkv * (jnp.sum(dnkv * nkv, axis=1, keepdims=True) * (1.0 / KV_LORA_RANK))))
        dkpe = jnp.zeros((tm, LANES), F32)
        for h in range(MLA_HEADS):
            dkpe = dkpe + dk[:, HEAD_PAD * h:HEAD_PAD * (h + 1)]
        _store_grad(dhc_ref, dbc_ref, CQ_PAD + LANES, _rope_t(jnp.where(rope_lanes, dkpe, 0.0), c, sa, sb))
        dwc_acc[...] += _dot(xt_ref[...], dhc_ref[...], _NN)

        @pl.when(pl.program_id(0) == SEQ // tm - 1)
        def _():
            dwc_ref[...] = dwc_acc[...].astype(BF16)
            rows = Q_LORA_RANK // N_DEV
            for j in range(N_DEV):
                for h in range(MLA_HEADS):
                    puq_ref[j, :, QK_HEAD_DIM * h:QK_HEAD_DIM * (h + 1)] = dwq_ref[
                        rows * j:rows * (j + 1), HEAD_PAD * h:HEAD_PAD * h + QK_HEAD_DIM].astype(BF16)

        _exchange_finish(pl.program_id(0) == nsteps - 1, exchange)

    full = lambda shape: pl.BlockSpec(shape, lambda i: (0, 0))
    row = lambda w, c=0: pl.BlockSpec((tm, w), lambda i, c=c: (i, c))
    hbm = pl.BlockSpec(memory_space=pl.ANY)
    res = pl.pallas_call(
        body, name="mla_bwd", grid=(nsteps,),
        in_specs=[row(hw), row(hw), row(hw), row(CQ_PAD, 0), row(LANES, CQ_PAD // LANES),
                  pl.BlockSpec((D_MODEL, tm), lambda i: (0, i)),
                  full((1, CQ_PAD)), full((1, KV_LORA_RANK)), full((CQ_PAD, hw)), full((KV_LORA_RANK, hw)),
                  full((KV_LORA_RANK, hw)), row(LANES), row(LANES), row(LANES)] + [hbm] * npart,
        out_specs=[row(SEG_C), pl.BlockSpec((N_DEV, Q_LORA_RANK // N_DEV, MLA_HEADS * QK_HEAD_DIM), lambda i: (0, 0, 0)),
                   full((KV_LORA_RANK, hw)), full((KV_LORA_RANK, hw)),
                   full((1, CQ_PAD)), full((1, KV_LORA_RANK)), full((1, SEG_C)), full((D_MODEL, SEG_C))] + [hbm] * npart,
        out_shape=[jax.ShapeDtypeStruct((SEQ, SEG_C), BF16),
                   jax.ShapeDtypeStruct((N_DEV, Q_LORA_RANK // N_DEV, MLA_HEADS * QK_HEAD_DIM), BF16),
                   jax.ShapeDtypeStruct((KV_LORA_RANK, hw), F32), jax.ShapeDtypeStruct((KV_LORA_RANK, hw), F32),
                   jax.ShapeDtypeStruct((1, CQ_PAD), F32), jax.ShapeDtypeStruct((1, KV_LORA_RANK), F32),
                   jax.ShapeDtypeStruct((1, SEG_C), F32), jax.ShapeDtypeStruct((D_MODEL, SEG_C), BF16)]
        + [jax.ShapeDtypeStruct(p.shape, p.dtype) for p in parts],
        scratch_shapes=[pltpu.VMEM((tm, hw), BF16), pltpu.VMEM((CQ_PAD, hw), F32), pltpu.VMEM((D_MODEL, SEG_C), F32)]
        + _exchange_sems(npart),
        compiler_params=pltpu.CompilerParams(dimension_semantics=("arbitrary",), vmem_limit_bytes=VMEM_MID),
    )(dq, dk, dv, h_c, h_c, xt_bf, gq, gkv, wq, wkn, wv, c_t, sa_t, sb_t, *parts)
    return res[:8], res[8:]


def _adamw_all(ws, gs, ms, vs):
    n = len(ws)
    c1 = 1.0 / (1.0 - ADAM_B1 ** ADAM_STEP)
    c2 = 1.0 / (1.0 - ADAM_B2 ** ADAM_STEP)

    def body(*refs):
        for idx in range(n):
            w, g, m, v = (refs[idx][...], refs[n + idx][...], refs[2 * n + idx][...], refs[3 * n + idx][...])
            m_new = ADAM_B1 * m + (1.0 - ADAM_B1) * g
            v_new = ADAM_B2 * v + (1.0 - ADAM_B2) * (g * g)
            delta = -ADAM_LR * ((m_new * c1) / (jnp.sqrt(v_new * c2) + ADAM_EPS) + ADAM_WD * w)
            refs[4 * n + idx][...] = delta
            refs[5 * n + idx][...] = m_new
            refs[6 * n + idx][...] = v_new

    shapes = [jax.ShapeDtypeStruct(w.shape, F32) for w in ws]
    outs = pl.pallas_call(
        body, name="adamw", out_shape=shapes * 3,
        compiler_params=pltpu.CompilerParams(vmem_limit_bytes=VMEM_BIG),
    )(*ws, *gs, *ms, *vs)
    return outs[:n], outs[n:2 * n], outs[2 * n:]


SHARD_W = IN_WIDTH // N_DEV

_PIECES = [(0, 384, 2, 0), (384, 512, 2, CQ_PAD), (512, 544, 2, CQ_PAD + LANES + ROPE_LO),
           (544, 1056, 0, 2 * D_MODEL), (1056, 1568, 1, 0), (1568, 2080, 1, SGU_WIDTH),
           (2080, 2592, 1, 2 * SGU_WIDTH), (2592, 3616, 0, 0), (3616, 4640, 0, D_MODEL)]


def _column_runs():
    runs = []
    for n0, n1, seg, d0 in _PIECES:
        for j in range(N_DEV):
            lo, hi = max(n0, j * SHARD_W), min(n1, (j + 1) * SHARD_W)
            if lo < hi:
                runs.append((j, lo - j * SHARD_W, hi - j * SHARD_W, seg, d0 + lo - n0))
    return runs


def _mesh_pos():
    return lax.axis_index("x"), lax.axis_index("y"), lax.axis_index("c")


def _remote(src, dst, send_sems, recv_sems, k, to):
    return pltpu.make_async_remote_copy(src_ref=src, dst_ref=dst, send_sem=send_sems.at[k], recv_sem=recv_sems.at[k],
                                        device_id=to, device_id_type=pl.DeviceIdType.MESH)


def _gather_exchange(gats, send_sems, recv_sems, meanwhile=None):
    x, y, c = _mesh_pos()
    me, sibling = (x, y, c), (x, y, 1 - c)
    chips = [(1 - x, y), (x, 1 - y), (1 - x, 1 - y)]

    def copy(a, k, blk, to):
        slab = gats[a].at[4 * blk[0] + 2 * blk[1] + blk[2]]
        return _remote(slab, slab, send_sems, recv_sems, 7 * a + k, to)

    arrays = range(len(gats))
    first = [copy(a, 1 + j, me, (*chip, c)) for j, chip in enumerate(chips) for a in arrays]
    first += [copy(a, 0, me, sibling) for a in arrays]
    for cp in first:
        cp.start()
    if meanwhile is not None:
        meanwhile()
    passed = []
    for j, chip in enumerate(chips):
        for a in arrays:
            copy(a, 1 + j, (*chip, c), me).wait_recv()
            fwd = copy(a, 4 + j, (*chip, c), sibling)
            fwd.start()
            passed.append(fwd)
    for a in arrays:
        copy(a, 0, sibling, me).wait_recv()
    for j, chip in enumerate(chips):
        for a in arrays:
            copy(a, 4 + j, (*chip, 1 - c), me).wait_recv()
    for cp in first + passed:
        cp.wait_send()


def _gather_behind(own, gats, send_sems, recv_sems, local_sems, step, mid, last):
    x, y, c = _mesh_pos()
    me, sibling = (x, y, c), (x, y, 1 - c)
    chips = [(1 - x, y), (x, 1 - y), (1 - x, 1 - y)]
    arrays = range(len(gats))

    def copy(a, k, blk, to, src=None):
        slab = gats[a].at[4 * blk[0] + 2 * blk[1] + blk[2]]
        return _remote(slab if src is None else src, slab, send_sems, recv_sems, 7 * a + k, to)

    first = [copy(a, 1 + j, me, (*chip, c), src=own[a]) for j, chip in enumerate(chips) for a in arrays]
    first += [copy(a, 0, me, sibling, src=own[a]) for a in arrays]
    local = [pltpu.make_async_copy(own[a], gats[a].at[4 * x + 2 * y + c], local_sems.at[a]) for a in arrays]
    passed = [copy(a, 4 + j, (*chip, c), sibling) for j, chip in enumerate(chips) for a in arrays]

    @pl.when(step == 0)
    def _():
        for cp in first + local:
            cp.start()

    @pl.when(step == mid)
    def _():
        for j, chip in enumerate(chips):
            for a in arrays:
                copy(a, 1 + j, (*chip, c), me).wait_recv()
        for cp in passed:
            cp.start()

    @pl.when(step == last)
    def _():
        for a in arrays:
            copy(a, 0, sibling, me).wait_recv()
        for j, chip in enumerate(chips):
            for a in arrays:
                copy(a, 4 + j, (*chip, 1 - c), me).wait_recv()
        for cp in first + passed:
            cp.wait_send()
        for cp in local:
            cp.wait()


def _gather_first(w_in, w_uq2, w_oa, w_ob, w_out, x2, pos_col, invf_lane):
    hw = MLA_HEADS * HEAD_PAD
    uq_rows = Q_LORA_RANK // N_DEV
    rows = 256

    def body(win_ref, wuq_ref, woa_ref, wob_ref, wout_ref, x_ref, pos_ref, invf_ref,
             wc_ref, wq_ref, winb_ref, oab_ref, obb_ref, outb_ref, xb_ref, xt_ref, c_ref, sa_ref, sb_ref,
             g_uq, blk0, send_sems, recv_sems):
        def local_work():
            for i in range(SEQ // rows):
                xi = x_ref[rows * i:rows * (i + 1), :]
                xb_ref[rows * i:rows * (i + 1), :] = xi.astype(BF16)
                xt_ref[:, rows * i:rows * (i + 1)] = xi.T.astype(BF16)
            ang = pos_ref[...].astype(F32) * invf_ref[...]
            cs, sn = jnp.cos(ang), jnp.sin(ang)
            lane = lax.broadcasted_iota(jnp.int32, ang.shape, 1)
            c_ref[...] = jnp.where(lane < ROPE_LO, 1.0, jnp.where(lane < ROPE_HI, cs, 0.0))
            sa_ref[...] = jnp.where(jnp.logical_and(lane >= ROPE_LO, lane < ROPE_MID), -sn, 0.0)
            sb_ref[...] = jnp.where(jnp.logical_and(lane >= ROPE_MID, lane < ROPE_HI), sn, 0.0)

        x, y, c = _mesh_pos()
        me = (x, y, c)
        winb_ref[...] = win_ref[0].astype(BF16)
        oab_ref[...] = woa_ref[0].astype(BF16)
        obb_ref[...] = wob_ref[0].astype(BF16)
        outb_ref[...] = wout_ref[0].astype(BF16)
        g_uq[4 * x + 2 * y + c] = wuq_ref[...].astype(BF16)

        chip0 = jnp.logical_and(x == 0, y == 0)
        south = c == 0
        half = D_MODEL // 2
        halves = [blk0.at[pl.ds(0, half)], blk0.at[pl.ds(half, half)]]

        def bcopy(k, to, part=None):
            ref = blk0 if part is None else halves[part]
            return _remote(ref, ref, send_sems, recv_sems, 7 + k, to)

        sends0 = [(0, (0, 0, 1), None), (1, (1, 0, 0), 0), (2, (0, 1, 0), 1), (3, (1, 0, 0), 1), (4, (0, 1, 0), 0)]

        @pl.when(jnp.logical_and(chip0, south))
        def _():
            blk0[...] = winb_ref[...]
            for k, to, part in sends0:
                bcopy(k, to, part).start()

        _gather_exchange([g_uq], send_sems, recv_sems, meanwhile=local_work)

        for (cx, cy), first_k, first_half, second_k in (((1, 0), 1, 0, 3), ((0, 1), 2, 1, 4)):
            @pl.when(jnp.logical_and(jnp.logical_and(x == cx, y == cy), south))
            def _(cx=cx, cy=cy, first_k=first_k, first_half=first_half, second_k=second_k):
                bcopy(first_k, me, first_half).wait_recv()
                onward = bcopy(5 + first_half, (1, 1, 0), first_half)
                onward.start()
                bcopy(second_k, me, 1 - first_half).wait_recv()
                north = bcopy(7, (cx, cy, 1))
                north.start()
                onward.wait_send()
                north.wait_send()

        @pl.when(jnp.logical_and(jnp.logical_and(x == 1, y == 1), south))
        def _():
            bcopy(5, me, 0).wait_recv()
            bcopy(6, me, 1).wait_recv()
            north = bcopy(7, (1, 1, 1))
            north.start()
            north.wait_send()

        @pl.when(jnp.logical_and(chip0, c == 1))
        def _():
            bcopy(0, me).wait_recv()

        @pl.when(jnp.logical_and(jnp.logical_not(chip0), c == 1))
        def _():
            bcopy(7, me).wait_recv()

        @pl.when(jnp.logical_and(chip0, south))
        def _():
            for k, to, part in sends0:
                bcopy(k, to, part).wait_send()

        for j, s0, s1, seg, d0 in _column_runs():
            if seg == 2:
                wc_ref[:, d0:d0 + (s1 - s0)] = blk0[:, s0:s1]
        zeros = lambda r, w: jnp.zeros((r, w), BF16)
        wc_ref[:, Q_LORA_RANK:CQ_PAD] = zeros(D_MODEL, CQ_PAD - Q_LORA_RANK)
        wc_ref[:, CQ_PAD + LANES:CQ_PAD + LANES + ROPE_LO] = zeros(D_MODEL, ROPE_LO)
        wc_ref[:, CQ_PAD + LANES + ROPE_HI:SEG_C] = zeros(D_MODEL, LANES - ROPE_HI)
        wq_ref[Q_LORA_RANK:CQ_PAD, :] = zeros(CQ_PAD - Q_LORA_RANK, hw)
        for h in range(MLA_HEADS):
            wq_ref[0:Q_LORA_RANK, HEAD_PAD * h + QK_HEAD_DIM:HEAD_PAD * (h + 1)] = zeros(Q_LORA_RANK, HEAD_PAD - QK_HEAD_DIM)
        for j in range(N_DEV):
            for h in range(MLA_HEADS):
                wq_ref[uq_rows * j:uq_rows * (j + 1), HEAD_PAD * h:HEAD_PAD * h + QK_HEAD_DIM] = g_uq[
                    j, :, QK_HEAD_DIM * h:QK_HEAD_DIM * (h + 1)]

    vmem = pl.BlockSpec(memory_space=pltpu.VMEM)
    return pl.pallas_call(
        body, name="gather_first",
        out_shape=[jax.ShapeDtypeStruct((D_MODEL, SEG_C), BF16), jax.ShapeDtypeStruct((CQ_PAD, hw), BF16),
                   jax.ShapeDtypeStruct(w_in.shape[1:], BF16), jax.ShapeDtypeStruct(w_oa.shape[1:], BF16),
                   jax.ShapeDtypeStruct(w_ob.shape[1:], BF16), jax.ShapeDtypeStruct(w_out.shape[1:], BF16),
                   jax.ShapeDtypeStruct((SEQ, D_MODEL), BF16), jax.ShapeDtypeStruct((D_MODEL, SEQ), BF16)]
        + [jax.ShapeDtypeStruct((SEQ, LANES), F32)] * 3,
        in_specs=[vmem] * 8, out_specs=[vmem] * 11,
        scratch_shapes=[pltpu.VMEM((N_DEV, uq_rows, MLA_HEADS * QK_HEAD_DIM), BF16), pltpu.VMEM((D_MODEL, SHARD_W), BF16),
                        pltpu.SemaphoreType.DMA((15,)), pltpu.SemaphoreType.DMA((15,))],
        compiler_params=pltpu.CompilerParams(vmem_limit_bytes=VMEM_BIG),
    )(w_in, w_uq2, w_oa, w_ob, w_out, x2, pos_col, invf_lane)


def _assemble_in(g_in):
    def body(g_ref, wa_ref, wb_ref):
        segs = [wa_ref, wb_ref]
        for j, s0, s1, seg, d0 in _column_runs():
            if seg < 2:
                segs[seg][:, d0:d0 + (s1 - s0)] = g_ref[j, :, s0:s1]

    return pl.pallas_call(
        body, name="assemble_in",
        out_shape=[jax.ShapeDtypeStruct((D_MODEL, SEG_A), BF16), jax.ShapeDtypeStruct((D_MODEL, SEG_B), BF16)],
        compiler_params=pltpu.CompilerParams(vmem_limit_bytes=VMEM_MID),
    )(g_in)


def _assemble_out(g_oa, g_ob, g_out):
    cols = D_MODEL // N_DEV

    def body(goa_ref, gob_ref, gout_ref, oa_ref, ob_ref, out_ref):
        for j in range(N_DEV):
            oa_ref[:, cols * j:cols * (j + 1)] = goa_ref[j]
            ob_ref[:, cols * j:cols * (j + 1)] = gob_ref[j]
            out_ref[cols * j:cols * (j + 1), :] = gout_ref[j]

    return pl.pallas_call(
        body, name="assemble_out",
        out_shape=[jax.ShapeDtypeStruct((MLA_WIDTH, D_MODEL), BF16), jax.ShapeDtypeStruct((SGU_WIDTH, D_MODEL), BF16),
                   jax.ShapeDtypeStruct((D_MODEL, D_MODEL), BF16)],
    )(g_oa, g_ob, g_out)


C_NAT = 544


P_IN_SPLIT = 896


def _to_parts(dwa, dwb):
    def body(dwa_ref, dwb_ref, phi_ref, plo_ref):
        phi_ref[0, :, 0:C_NAT] = jnp.zeros((P_IN_SPLIT, C_NAT), BF16)
        plo_ref[0, :, 0:C_NAT] = jnp.zeros((D_MODEL - P_IN_SPLIT, C_NAT), BF16)
        segs = [dwa_ref, dwb_ref]
        for j, s0, s1, seg, d0 in _column_runs():
            if seg < 2:
                phi_ref[j, :, s0:s1] = segs[seg][0:P_IN_SPLIT, d0:d0 + (s1 - s0)]
                plo_ref[j, :, s0:s1] = segs[seg][P_IN_SPLIT:D_MODEL, d0:d0 + (s1 - s0)]

    return pl.pallas_call(
        body, name="to_parts",
        out_shape=[jax.ShapeDtypeStruct((N_DEV, P_IN_SPLIT, SHARD_W), BF16),
                   jax.ShapeDtypeStruct((N_DEV, D_MODEL - P_IN_SPLIT, SHARD_W), BF16)],
        compiler_params=pltpu.CompilerParams(vmem_limit_bytes=VMEM_MID))(dwa, dwb)


def _dx_tail(dhs, ws, dx_res, dwc, p_uq, p_rep):
    ntile, sums_at = 8, 5
    tm = SEQ // ntile
    rep_rows = p_rep.shape[1]
    c_rows = D_MODEL // N_DEV
    spec = [((c_rows, C_NAT), BF16), (p_uq.shape[1:], BF16), ((rep_rows, LANES), F32)]
    n = len(spec)

    nseg = len(dhs)

    def body(*refs):
        dh_refs, w_refs = refs[:nseg], refs[nseg:2 * nseg]
        dxr_ref, dwc_ref, puq_ref, prep_ref, dx_ref, call_ref, guq_ref, repall_ref, pc_ref, c_all, rep_all = refs[
            2 * nseg:2 * nseg + 11]
        rest = refs[2 * nseg + 11:]
        ras, tbs, rbs = rest[0:n], rest[n:2 * n], rest[2 * n:3 * n]
        send_sems, recv_sems, gsend, grecv = rest[3 * n:]
        step = pl.program_id(0)
        x, y, c = _mesh_pos()
        me_idx = 4 * x + 2 * y + c
        me, sibling = (x, y, c), (x, y, 1 - c)
        others = [(1 - x, y), (x, 1 - y), (1 - x, 1 - y)]
        parts = [pc_ref, puq_ref, prep_ref]
        gats = [rep_all, c_all]

        def stage1(chip, a):
            return _remote(parts[a].at[2 * chip + (1 - c)], ras[a].at[chip], send_sems, recv_sems, 7 * a + chip, sibling)

        def stage2(k, a):
            cx, cy = others[k]
            return _remote(tbs[a].at[k], rbs[a].at[k], send_sems, recv_sems, 7 * a + 4 + k, (cx, cy, c))

        def gcopy(a, k, blk, to):
            slab = gats[a].at[4 * blk[0] + 2 * blk[1] + blk[2]]
            return _remote(slab, slab, gsend, grecv, 7 * a + k, to)

        def chip_sum(a, chip):
            return parts[a][2 * chip + c].astype(F32) + ras[a][chip].astype(F32)

        @pl.when(step == 0)
        def _():
            for j, s0, s1, seg, d0 in _column_runs():
                if seg == 2:
                    for r in range(N_DEV):
                        pc_ref[r, :, s0:s1] = dwc_ref[c_rows * r:c_rows * (r + 1), d0:d0 + (s1 - s0)]
            for chip in range(4):
                for a in range(n):
                    stage1(chip, a).start()

        @pl.when(step == 1)
        def _():
            for chip in range(4):
                for a in range(n):
                    stage1(chip, a).wait_recv()
            for k, (cx, cy) in enumerate(others):
                for a in range(n):
                    tbs[a][k] = chip_sum(a, 2 * cx + cy).astype(spec[a][1])
                    stage2(k, a).start()

        @pl.when(step == sums_at)
        def _():
            for k in range(3):
                for a in range(n):
                    stage2(k, a).wait_recv()
            sums = []
            for a in range(n):
                acc = chip_sum(a, 2 * x + y)
                for k in range(3):
                    acc = acc + rbs[a][k].astype(F32)
                sums.append(acc)
            c_all[me_idx] = sums[0].astype(BF16)
            guq_ref[...] = sums[1]
            rep_all[me_idx] = sums[2]
            for a in range(2):
                for j, chip in enumerate(others):
                    gcopy(a, 1 + j, me, (*chip, c)).start()
                gcopy(a, 0, me, sibling).start()

        acc = dxr_ref[...]
        for dh_ref, w_ref in zip(dh_refs, w_refs):
            acc = acc + _dot(dh_ref[...], w_ref[...], _NT)
        dx_ref[...] = acc

        @pl.when(step == ntile - 1)
        def _():
            for j, chip in enumerate(others):
                for a in range(2):
                    gcopy(a, 1 + j, (*chip, c), me).wait_recv()
                    gcopy(a, 4 + j, (*chip, c), sibling).start()
            for a in range(2):
                gcopy(a, 0, sibling, me).wait_recv()
                for j, chip in enumerate(others):
                    gcopy(a, 4 + j, (*chip, 1 - c), me).wait_recv()
            for a in range(2):
                gcopy(a, 0, me, sibling).wait_send()
                for j, chip in enumerate(others):
                    gcopy(a, 1 + j, me, (*chip, c)).wait_send()
                    gcopy(a, 4 + j, (*chip, c), sibling).wait_send()
            for a in range(n):
                for chip in range(4):
                    stage1(chip, a).wait_send()
                for k in range(3):
                    stage2(k, a).wait_send()
            call_ref[...] = c_all[...]
            repall_ref[...] = rep_all[...]

    row = lambda w: pl.BlockSpec((tm, w), lambda i: (i, 0))
    full = lambda shape: pl.BlockSpec(shape, lambda i: (0,) * len(shape))
    scratch = [pltpu.VMEM((N_DEV, c_rows, C_NAT), BF16), pltpu.VMEM((N_DEV, c_rows, C_NAT), BF16),
               pltpu.VMEM((N_DEV, rep_rows, LANES), F32)]
    for lead in (4, 3, 3):
        scratch += [pltpu.VMEM((lead,) + tuple(shape), dt) for shape, dt in spec]
    scratch += [pltpu.SemaphoreType.DMA((7 * n,)), pltpu.SemaphoreType.DMA((7 * n,)),
                pltpu.SemaphoreType.DMA((14,)), pltpu.SemaphoreType.DMA((14,))]
    return pl.pallas_call(
        body, name="dx_tail", grid=(SEQ // tm,),
        in_specs=[row(dh.shape[1]) for dh in dhs] + [full(w.shape) for w in ws]
        + [row(D_MODEL), full(dwc.shape), full(p_uq.shape), full(p_rep.shape)],
        out_specs=[row(D_MODEL), full((N_DEV, c_rows, C_NAT)), full(p_uq.shape[1:]), full((N_DEV, rep_rows, LANES))],
        out_shape=[jax.ShapeDtypeStruct((SEQ, D_MODEL), F32), jax.ShapeDtypeStruct((N_DEV, c_rows, C_NAT), BF16),
                   jax.ShapeDtypeStruct(p_uq.shape[1:], F32), jax.ShapeDtypeStruct((N_DEV, rep_rows, LANES), F32)],
        scratch_shapes=scratch,
        compiler_params=pltpu.CompilerParams(dimension_semantics=("arbitrary",), vmem_limit_bytes=VMEM_BIG),
    )(*dhs, *ws, dx_res, dwc, p_uq, p_rep)


def _sum_landed(landed, c_all):
    c_rows = D_MODEL // N_DEV

    def body(rhi_ref, rlo_ref, roa_ref, rob_ref, rout_ref, call_ref, gin_ref, goa_ref, gob_ref, gout_ref):
        def total(ref, sl):
            acc = ref[0, sl, :].astype(F32)
            for s in range(1, N_DEV):
                acc = acc + ref[s, sl, :].astype(F32)
            return acc

        x, y, c = _mesh_pos()
        dev0 = jnp.where(4 * x + 2 * y + c == 0, 1.0, 0.0)
        for j in range(N_DEV):
            sl = slice(c_rows * j, c_rows * (j + 1))
            below = c_rows * j < P_IN_SPLIT
            tot = total(rhi_ref, sl) if below else total(rlo_ref, slice(c_rows * j - P_IN_SPLIT, c_rows * (j + 1) - P_IN_SPLIT))
            gin_ref[0, sl, C_NAT:SHARD_W] = tot[:, C_NAT:SHARD_W]
            gin_ref[0, sl, 0:C_NAT] = tot[:, 0:C_NAT] + dev0 * call_ref[j].astype(F32)
        goa_ref[0] = total(roa_ref, slice(None))
        gob_ref[0] = total(rob_ref, slice(None))
        gout_ref[0] = total(rout_ref, slice(None))

    return pl.pallas_call(
        body, name="sum_landed",
        out_shape=[jax.ShapeDtypeStruct((1, D_MODEL, SHARD_W), F32)]
        + [jax.ShapeDtypeStruct((1,) + r.shape[1:], F32) for r in landed[2:]],
        compiler_params=pltpu.CompilerParams(vmem_limit_bytes=VMEM_MID),
    )(*landed, c_all)


_O_CQ, _O_CKV, _O_KPE, _O_ZA, _O_U, _O_V, _O_ZB, _O_GA, _O_GB = 0, 384, 512, 544, 1056, 1568, 2080, 2592, 3616


def _to_segments(w):
    z = lambda n: jnp.zeros(w.shape[:-1] + (n,), w.dtype)
    seg_a = jnp.concatenate([w[..., _O_GA:_O_GB], w[..., _O_GB:IN_WIDTH], w[..., _O_ZA:_O_U]], axis=-1)
    seg_b = jnp.concatenate([w[..., _O_U:_O_V], w[..., _O_V:_O_ZB], w[..., _O_ZB:_O_GA]], axis=-1)
    seg_c = jnp.concatenate([w[..., _O_CQ:_O_CKV], z(CQ_PAD - Q_LORA_RANK), w[..., _O_CKV:_O_KPE],
                             z(ROPE_LO), w[..., _O_KPE:_O_ZA], z(LANES - ROPE_HI)], axis=-1)
    return seg_a, seg_b, seg_c


def _from_segments(seg_a, seg_b, seg_c):
    kpe0 = CQ_PAD + LANES + ROPE_LO
    return jnp.concatenate([
        seg_c[..., 0:Q_LORA_RANK], seg_c[..., CQ_PAD:CQ_PAD + LANES], seg_c[..., kpe0:kpe0 + QK_ROPE_DIM],
        seg_a[..., 2 * D_MODEL:SEG_A], seg_b, seg_a[..., 0:2 * D_MODEL]], axis=-1)


def kernel(x, positions, w_in, b_in, g_q, w_uq, g_kv, w_ukv, w_oa, sgu_ln_g, sgu_ln_b, w_s, b_s, w_ob, w_out, ln_g, ln_b, loss_target, m_w_in, m_b_in, m_g_q, m_w_uq, m_g_kv, m_w_ukv, m_w_oa, m_sgu_ln_g, m_sgu_ln_b, m_w_s, m_b_s, m_w_ob, m_w_out, m_ln_g, m_ln_b, v_w_in, v_b_in, v_g_q, v_w_uq, v_g_kv, v_w_ukv, v_w_oa, v_sgu_ln_g, v_sgu_ln_b, v_w_s, v_b_s, v_w_ob, v_w_out, v_ln_g, v_ln_b):
    w_uq2 = w_uq[0].reshape(Q_LORA_RANK // N_DEV, MLA_HEADS * QK_HEAD_DIM)
    inv_freq = ROPE_THETA ** (-jnp.arange(0, QK_ROPE_DIM, 2, dtype=F32) / QK_ROPE_DIM)
    invf_lane = jnp.concatenate([jnp.zeros((ROPE_LO,), F32), inv_freq, inv_freq,
                                 jnp.zeros((LANES - ROPE_HI,), F32)]).reshape(1, LANES)
    first = _gather_first(w_in, w_uq2, w_oa, w_ob, w_out, x[0], positions.reshape(SEQ, 1), invf_lane)
    partials = _local_step(x[0], loss_target[0], first, b_in, g_q, g_kv, w_ukv, sgu_ln_g, sgu_ln_b, w_s, b_s, ln_g, ln_b)
    weights = dict(w_in=w_in, b_in=b_in, g_q=g_q, w_uq=w_uq, g_kv=g_kv, w_ukv=w_ukv, w_oa=w_oa, sgu_ln_g=sgu_ln_g,
                   sgu_ln_b=sgu_ln_b, w_s=w_s, b_s=b_s, w_ob=w_ob, w_out=w_out, ln_g=ln_g, ln_b=ln_b)
    moms = dict(w_in=m_w_in, b_in=m_b_in, g_q=m_g_q, w_uq=m_w_uq, g_kv=m_g_kv, w_ukv=m_w_ukv, w_oa=m_w_oa,
                sgu_ln_g=m_sgu_ln_g, sgu_ln_b=m_sgu_ln_b, w_s=m_w_s, b_s=m_b_s, w_ob=m_w_ob, w_out=m_w_out,
                ln_g=m_ln_g, ln_b=m_ln_b)
    vars_ = dict(w_in=v_w_in, b_in=v_b_in, g_q=v_g_q, w_uq=v_w_uq, g_kv=v_g_kv, w_ukv=v_w_ukv, w_oa=v_w_oa,
                 sgu_ln_g=v_sgu_ln_g, sgu_ln_b=v_sgu_ln_b, w_s=v_w_s, b_s=v_b_s, w_ob=v_w_ob, w_out=v_w_out,
                 ln_g=v_ln_g, ln_b=v_ln_b)
    return _reduce_and_update(partials, weights, moms, vars_)


def _local_step(x2, tgt, first, b_in, g_q, g_kv, w_ukv, sgu_ln_g, sgu_ln_b, w_s, b_s, ln_g, ln_b):
    wc, wq, win_b, oa_b, ob_b, out_b, x_bf, xt_bf, c_t, sa_t, sb_t = first
    ba, bb, bc = _to_segments(b_in)
    w_ukv_bf = w_ukv[0].astype(BF16)
    wkn = jnp.pad(w_ukv_bf[:, :, :QK_NOPE_DIM], ((0, 0), (0, 0), (0, HEAD_PAD - QK_NOPE_DIM))).reshape(KV_LORA_RANK, -1)
    wv = jnp.pad(w_ukv_bf[:, :, QK_NOPE_DIM:], ((0, 0), (0, 0), (0, HEAD_PAD - V_HEAD_DIM))).reshape(KV_LORA_RANK, -1)
    gq = jnp.pad(g_q, ((0, 0), (0, CQ_PAD - Q_LORA_RANK)))
    bias_full = jnp.repeat(b_s[0].T, SGU_GROUP_DIM, axis=1)
    w_s3 = w_s[0]
    w_st3 = jnp.swapaxes(w_s3, 1, 2)

    h_c = _mm(x_bf, wc, bias=bc, tm=512, tn=SEG_C, name="in_proj_c")
    q, k, kt, vx, vxt = _mla_prep(h_c, gq, g_kv, wq, wkn, wv, c_t, sa_t, sb_t)
    o, lse, (g_in,) = _attn_fwd(q, kt, vx, (win_b,))
    wa, wb = _assemble_in(g_in)
    h_a, (g_out,) = _mm(x_bf, wa, bias=ba, own=(out_b,), tm=512, tn=SEG_A // 2, name="in_proj_a")
    h_b, (g_oa, g_ob) = _mm(x_bf, wb, bias=bb, own=(oa_b, ob_b), tm=512, tn=SEG_B // 2, name="in_proj_b")
    y_b = _sgu_fwd(h_b, sgu_ln_g, sgu_ln_b, w_s3, bias_full)
    w_oa_f, w_ob_f, w_out_f = _assemble_out(g_oa, g_ob, g_out)

    (loss_row, dx_res, dh_a, d_o, d_yb, p_oa, p_ob, p_out, d_lng, d_lnb, d_ba) = _merge(
        x2, o, h_a, y_b, tgt, w_oa_f, w_ob_f, w_out_f, ln_g, ln_b)
    (dh_b, d_ws, d_bs_t, d_slg, d_slb, d_bb), (r_out,) = _sgu_bwd(h_b, d_yb, sgu_ln_g, sgu_ln_b, w_s3, w_st3, bias_full,
                                                                 (p_out,))
    d_wa, (r_oa,) = _mm(xt_bf, dh_a, out_dtype=BF16, parts=(p_oa,), tm=512, tn=512, name="dw_in_a")
    d_wb = _mm(xt_bf, dh_b, out_dtype=BF16, tm=512, tn=512, name="dw_in_b")
    p_hi, p_lo = _to_parts(d_wa, d_wb)
    dq, dk, dv, (r_hi,) = _attn_bwd(q, kt, k, vxt, d_o, o, lse, (p_hi,))
    (dh_c, p_uq, d_wkn, d_wv, d_gq, d_gkv, d_bc, d_wc), (r_lo, r_ob) = _mla_bwd(
        dq, dk, dv, h_c, xt_bf, gq, g_kv, wq, wkn, wv, c_t, sa_t, sb_t, (p_lo, p_ob))
    landed = (r_hi, r_lo, r_oa, r_ob, r_out)

    p_b_in = _from_segments(d_ba, d_bb, d_bc)
    p_w_ukv = jnp.concatenate([d_wkn.reshape(KV_LORA_RANK, MLA_HEADS, HEAD_PAD)[:, :, :QK_NOPE_DIM],
                               d_wv.reshape(KV_LORA_RANK, MLA_HEADS, HEAD_PAD)[:, :, :V_HEAD_DIM]], axis=-1)
    p_g_q = d_gq[:, :Q_LORA_RANK]
    p_b_s = d_bs_t[:, :SGU_GROUPS].T
    replicated = [p_b_in, p_g_q, d_gkv, p_w_ukv, d_slg, d_slb, d_ws, p_b_s, d_lng, d_lnb]
    return loss_row, ((dh_a, dh_b, dh_c), (wa, wb, wc), dx_res), landed, d_wc, p_uq, replicated


_NAMES = ["w_in", "b_in", "g_q", "w_uq", "g_kv", "w_ukv", "w_oa", "sgu_ln_g", "sgu_ln_b", "w_s", "b_s", "w_ob",
          "w_out", "ln_g", "ln_b"]
_REPLICATED = ["b_in", "g_q", "g_kv", "w_ukv", "sgu_ln_g", "sgu_ln_b", "w_s", "b_s", "ln_g", "ln_b"]


def _reduce_and_update(partials, weights, moms, vars_):
    loss_row, (dhs, ws, dx_res), landed, d_wc, p_uq, replicated = partials
    def piece(a):
        flat = a.reshape(-1)
        return jnp.pad(flat, (0, -flat.size % PACK_ALIGN))

    rep_flat = jnp.concatenate([piece(a) for a in replicated] + [piece(loss_row[0, :1])])
    rep_flat = jnp.pad(rep_flat, (0, N_DEV * PACK_R_ROWS * LANES - rep_flat.size))
    dx, c_all, g_uq, rep_all = _dx_tail(dhs, ws, dx_res, d_wc, p_uq, rep_flat.reshape(N_DEV, PACK_R_ROWS, LANES))
    g_in, g_oa, g_ob, g_out = _sum_landed(landed, c_all)
    rep_sum = rep_all.reshape(-1)
    grads, pos = dict(w_in=g_in, w_uq=g_uq, w_oa=g_oa, w_ob=g_ob, w_out=g_out), 0
    for nm in _REPLICATED:
        grads[nm] = rep_sum[pos:pos + weights[nm].size]
        pos += weights[nm].size + -weights[nm].size % PACK_ALIGN
    loss = rep_sum[pos]
    grads = {nm: grads[nm].reshape(weights[nm].shape) for nm in _NAMES}
    deltas, new_m, new_v = _adamw_all([weights[nm] for nm in _NAMES], [grads[nm] for nm in _NAMES],
                                      [moms[nm] for nm in _NAMES], [vars_[nm] for nm in _NAMES])
    return (loss, dx.reshape(1, SEQ, D_MODEL), *[grads[nm] for nm in _NAMES], *deltas, *new_m, *new_v)
```

```python
import math

import jax
import jax.numpy as jnp
from jax import lax
from jax.experimental import pallas as pl
from jax.experimental.pallas import tpu as pltpu

F32 = jnp.float32
BF16 = jnp.bfloat16

D_MODEL = 1024
SEQ = 2048
N_DEV = 8
MLA_HEADS = 8
Q_LORA_RANK = 384
KV_LORA_RANK = 128
QK_NOPE_DIM = 64
QK_ROPE_DIM = 32
V_HEAD_DIM = 64
QK_HEAD_DIM = QK_NOPE_DIM + QK_ROPE_DIM
MLA_WIDTH = MLA_HEADS * V_HEAD_DIM
ROPE_THETA = 10000.0
SGU_GROUPS = 8
SGU_GROUP_DIM = 64
SGU_WIDTH = SGU_GROUPS * SGU_GROUP_DIM
CHUNK = 128
RMS_EPS = 1e-6
LN_EPS = 1e-5
DN_ALPHA = 2.0 ** 0.25
IN_WIDTH = 4640
ATTN_SCALE = QK_HEAD_DIM ** -0.5

ADAM_LR = 0.001
ADAM_B1 = 0.9
ADAM_B2 = 0.999
ADAM_EPS = 1e-08
ADAM_WD = 0.01
ADAM_STEP = 10

LANES = 128
HEAD_PAD = 128
ROPE_LO = QK_NOPE_DIM
ROPE_MID = ROPE_LO + QK_ROPE_DIM // 2
ROPE_HI = ROPE_LO + QK_ROPE_DIM
CQ_PAD = 512

SEG_A = 2560
SEG_B = 1536
SEG_C = 768

PACK_R_ROWS = 272
PACK_ALIGN = 8 * LANES
VMEM_BIG = 56 * 1024 * 1024
VMEM_MID = 40 * 1024 * 1024


def _sigmoid(x):
    return 1.0 / (1.0 + jnp.exp(-x))


def _gelu_and_grad(x):
    c0 = math.sqrt(2.0 / math.pi)
    x2 = x * x
    t = jnp.tanh(c0 * (x + 0.044715 * x * x2))
    g = 0.5 * x * (1.0 + t)
    dg = 0.5 * (1.0 + t) + 0.5 * x * (1.0 - t * t) * (c0 * (1.0 + 3.0 * 0.044715 * x2))
    return g, dg


def _dot(a, b, dims):
    return lax.dot_general(a, b, (dims, ((), ())), preferred_element_type=F32)


_NN = ((1,), (0,))
_NT = ((1,), (1,))
_TN = ((0,), (0,))


def _store_grad(dh_ref, db_ref, col, val):
    cols = slice(col, col + val.shape[1])
    dh_ref[:, cols] = val.astype(BF16)
    db_ref[:, cols] += jnp.sum(val, axis=0, keepdims=True)


def _mm(a, b, *, tb=False, bias=None, add=None, out_dtype=F32, own=(), parts=(), tm, tn, name):
    m, k = a.shape
    n = b.shape[0] if tb else b.shape[1]
    assert m % tm == 0 and n % tn == 0 and not (own and parts)
    dims = _NT if tb else _NN
    nown = len(own) + len(parts)
    nm = m // tm
    nsteps = (n // tn) * nm

    def body(*refs):
        a_ref, b_ref = refs[0], refs[1]
        pos = 2
        r = _dot(a_ref[...], b_ref[...], dims)
        if bias is not None:
            r = r + refs[pos][...]; pos += 1
        if add is not None:
            r = r + refs[pos][...]; pos += 1
        own_refs = refs[pos:pos + nown]; pos += nown
        refs[pos][...] = r.astype(out_dtype)
        if nown:
            gat_refs = refs[pos + 1:pos + 1 + nown]
            send_sems, recv_sems, local_sems = refs[pos + 1 + nown:]
            step = pl.program_id(0) * nm + pl.program_id(1)
            if own:
                _gather_behind(own_refs, gat_refs, send_sems, recv_sems, local_sems, step, nsteps - 2, nsteps - 1)
            else:
                exchange = _exchange_parts(own_refs, gat_refs, send_sems, recv_sems, local_sems)
                _exchange_start(step == 0, exchange)
                _exchange_finish(step == nsteps - 1, exchange)

    b_spec = pl.BlockSpec((tn, k), lambda j, i: (j, 0)) if tb else pl.BlockSpec((k, tn), lambda j, i: (0, j))
    in_specs, args = [pl.BlockSpec((tm, k), lambda j, i: (i, 0)), b_spec], [a, b]
    if bias is not None:
        in_specs.append(pl.BlockSpec((1, tn), lambda j, i: (0, j))); args.append(bias)
    if add is not None:
        in_specs.append(pl.BlockSpec((tm, tn), lambda j, i: (i, j))); args.append(add)
    hbm = pl.BlockSpec(memory_space=pl.ANY)
    res = pl.pallas_call(
        body, name=name, grid=(n // tn, nm), in_specs=in_specs + [hbm] * nown,
        out_specs=[pl.BlockSpec((tm, tn), lambda j, i: (i, j))] + [hbm] * nown,
        out_shape=[jax.ShapeDtypeStruct((m, n), out_dtype)]
        + [jax.ShapeDtypeStruct((N_DEV,) + o.shape, o.dtype) for o in own]
        + [jax.ShapeDtypeStruct(p.shape, p.dtype) for p in parts],
        scratch_shapes=_exchange_sems(nown) if nown else [],
        compiler_params=pltpu.CompilerParams(dimension_semantics=("arbitrary", "arbitrary"), vmem_limit_bytes=VMEM_BIG),
    )(*args, *own, *parts)
    return (res[0], res[1:]) if nown else res[0]


def _rope(x, c, sa, sb):
    return x * c + pltpu.roll(x, LANES - 16, 1) * sa + pltpu.roll(x, 16, 1) * sb


def _rope_t(dy, c, sa, sb):
    return dy * c + pltpu.roll(dy * sa, 16, 1) + pltpu.roll(dy * sb, LANES - 16, 1)


def _mla_prep(h_c, gq, gkv, wq, wkn, wvx, c_t, sa_t, sb_t):
    tm = 256
    hw = MLA_HEADS * HEAD_PAD

    def body(cq_ref, ckv_ref, kpe_ref, gq_ref, gkv_ref, wq_ref, wkn_ref, wvx_ref, c_ref, sa_ref, sb_ref,
             q_ref, k_ref, kt_ref, vx_ref, vxt_ref):
        c, sa, sb = c_ref[...], sa_ref[...], sb_ref[...]
        cq = cq_ref[...]
        rq = lax.rsqrt(jnp.sum(cq * cq, axis=1, keepdims=True) * (1.0 / Q_LORA_RANK) + RMS_EPS)
        cqn = ((cq * rq) * gq_ref[...]).astype(BF16)
        qall = _dot(cqn, wq_ref[...], _NN)
        for h in range(MLA_HEADS):
            sl = slice(HEAD_PAD * h, HEAD_PAD * (h + 1))
            q_ref[:, sl] = (_rope(qall[:, sl], c, sa, sb) * ATTN_SCALE).astype(BF16)
        ckv = ckv_ref[...]
        rkv = lax.rsqrt(jnp.sum(ckv * ckv, axis=1, keepdims=True) * (1.0 / KV_LORA_RANK) + RMS_EPS)
        ckvn = ((ckv * rkv) * gkv_ref[...]).astype(BF16)
        knall = _dot(ckvn, wkn_ref[...], _NN)
        vall = _dot(ckvn, wvx_ref[...], _NN)
        kper = _rope(kpe_ref[...], c, sa, sb)
        ones_half = (lax.broadcasted_iota(jnp.int32, (tm, HEAD_PAD), 1) >= V_HEAD_DIM).astype(F32)
        for h in range(MLA_HEADS):
            sl = slice(HEAD_PAD * h, HEAD_PAD * (h + 1))
            kh = knall[:, sl] + kper
            vh = vall[:, sl] + ones_half
            k_ref[:, sl] = kh.astype(BF16)
            kt_ref[sl, :] = kh.T.astype(BF16)
            vx_ref[:, sl] = vh.astype(BF16)
            vxt_ref[sl, :] = vh.T.astype(BF16)

    full = lambda shape: pl.BlockSpec(shape, lambda i: (0, 0))
    tab = pl.BlockSpec((tm, LANES), lambda i: (i, 0))
    row = pl.BlockSpec((tm, hw), lambda i: (i, 0))
    col = pl.BlockSpec((hw, tm), lambda i: (0, i))
    return pl.pallas_call(
        body, name="mla_prep", grid=(SEQ // tm,),
        in_specs=[pl.BlockSpec((tm, CQ_PAD), lambda i: (i, 0)),
                  pl.BlockSpec((tm, LANES), lambda i: (i, CQ_PAD // LANES)),
                  pl.BlockSpec((tm, LANES), lambda i: (i, CQ_PAD // LANES + 1)),
                  full((1, CQ_PAD)), full((1, KV_LORA_RANK)),
                  full((CQ_PAD, hw)), full((KV_LORA_RANK, hw)), full((KV_LORA_RANK, hw)), tab, tab, tab],
        out_specs=[row, row, col, row, col],
        out_shape=[jax.ShapeDtypeStruct((SEQ, hw), BF16), jax.ShapeDtypeStruct((SEQ, hw), BF16),
                   jax.ShapeDtypeStruct((hw, SEQ), BF16), jax.ShapeDtypeStruct((SEQ, hw), BF16),
                   jax.ShapeDtypeStruct((hw, SEQ), BF16)],
        compiler_params=pltpu.CompilerParams(dimension_semantics=("arbitrary",), vmem_limit_bytes=VMEM_MID),
    )(h_c, h_c, h_c, gq, gkv, wq, wkn, wvx, c_t, sa_t, sb_t)


ATT_T = 512
ATT_STRIP = 64


def _attn_fwd(q, kt, vx, own):
    t, rs = ATT_T, ATT_STRIP
    nown = len(own)
    nq = SEQ // t
    nsteps = (MLA_HEADS // 2) * nq

    def body(q_ref, kt_ref, vx_ref, *rest):
        own_refs, (o_ref, l_ref), gat_refs = rest[:nown], rest[nown:nown + 2], rest[nown + 2:2 * nown + 2]
        s_scr, p_scr, m_scr, a_scr, acc_scr, send_sems, recv_sems, local_sems = rest[2 * nown + 2:]
        qi = pl.program_id(1)
        lane = lax.broadcasted_iota(jnp.int32, (t, LANES), 1)
        m_scr[...] = jnp.full((2, t, LANES), -1e30, F32)
        acc_scr[...] = jnp.zeros((2, t, LANES), F32)

        def block(j, masked):
            off = pl.multiple_of(j * t, t)
            for a in range(2):
                sl = slice(HEAD_PAD * a, HEAD_PAD * (a + 1))
                s_scr[a] = _dot(q_ref[:, sl], kt_ref[sl, pl.ds(off, t)], _NN)
                for r in range(t // rs):
                    rows = slice(rs * r, rs * (r + 1))
                    s = s_scr[a, rows, :]
                    if masked:
                        rowi = lax.broadcasted_iota(jnp.int32, (rs, t), 0) + rs * r
                        coli = lax.broadcasted_iota(jnp.int32, (rs, t), 1)
                        s = jnp.where(coli <= rowi, s, -1e30)
                    m_old = m_scr[a, rows, :]
                    m_new = jnp.maximum(m_old, jnp.max(s, axis=1, keepdims=True))
                    p_scr[a, rows, :] = jnp.exp(s - m_new[:, :1]).astype(BF16)
                    a_scr[a, rows, :] = jnp.exp(m_old - m_new)
                    m_scr[a, rows, :] = m_new
                acc_scr[a] = acc_scr[a] * a_scr[a] + _dot(p_scr[a], vx_ref[pl.ds(off, t), sl], _NN)

        def step(j, carry):
            block(j, False)
            return carry
        lax.fori_loop(0, qi, step, 0)
        block(qi, True)
        res = []
        for a in range(2):
            acc = acc_scr[a]
            l = acc[:, V_HEAD_DIM:V_HEAD_DIM + 1]
            res.append((acc / l, m_scr[a] + jnp.log(l)))
        o_ref[...] = jnp.where(lane < V_HEAD_DIM, res[0][0], pltpu.roll(res[1][0], V_HEAD_DIM, 1))
        l_ref[...] = jnp.where(lane < V_HEAD_DIM, res[0][1], res[1][1])
        _gather_behind(own_refs, gat_refs, send_sems, recv_sems, local_sems, pl.program_id(0) * nq + qi,
                       nsteps - 1, nsteps - 1)

    hbm = pl.BlockSpec(memory_space=pl.ANY)
    res = pl.pallas_call(
        body, name="attn_fwd", grid=(MLA_HEADS // 2, nq),
        in_specs=[pl.BlockSpec((t, 2 * HEAD_PAD), lambda p, i: (i, p)),
                  pl.BlockSpec((2 * HEAD_PAD, SEQ), lambda p, i: (p, 0)),
                  pl.BlockSpec((SEQ, 2 * HEAD_PAD), lambda p, i: (0, p))] + [hbm] * nown,
        out_specs=[pl.BlockSpec((t, LANES), lambda p, i: (i, p)),
                   pl.BlockSpec((t, LANES), lambda p, i: (i, p))] + [hbm] * nown,
        out_shape=[jax.ShapeDtypeStruct((SEQ, MLA_WIDTH), F32), jax.ShapeDtypeStruct((SEQ, MLA_WIDTH), F32)]
        + [jax.ShapeDtypeStruct((N_DEV,) + a.shape, a.dtype) for a in own],
        scratch_shapes=[pltpu.VMEM((2, t, t), F32), pltpu.VMEM((2, t, t), BF16), pltpu.VMEM((2, t, LANES), F32),
                        pltpu.VMEM((2, t, LANES), F32), pltpu.VMEM((2, t, LANES), F32)] + _exchange_sems(nown),
        compiler_params=pltpu.CompilerParams(dimension_semantics=("arbitrary", "arbitrary"), vmem_limit_bytes=VMEM_MID),
    )(q, kt, vx, *own)
    return res[0], res[1], res[2:]


def _exchange_parts(parts, lands, send_sems, recv_sems, local_sems):
    x, y, c = _mesh_pos()
    me = 4 * x + 2 * y + c
    peers = [(x, y, 1 - c), (1 - x, y, c), (x, 1 - y, c), (1 - x, 1 - y, c),
             (1 - x, y, 1 - c), (x, 1 - y, 1 - c), (1 - x, 1 - y, 1 - c)]
    remote, local = [], []
    for a, (part, land) in enumerate(zip(parts, lands)):
        for k, peer in enumerate(peers):
            t = 4 * peer[0] + 2 * peer[1] + peer[2]
            remote.append(_remote(part.at[t], land.at[me], send_sems, recv_sems, 7 * a + k, peer))
        local.append(pltpu.make_async_copy(part.at[me], land.at[me], local_sems.at[a]))
    return remote, local


def _exchange_start(first_step, exchange):
    remote, local = exchange

    @pl.when(first_step)
    def _():
        for cp in remote + local:
            cp.start()


def _exchange_finish(last_step, exchange):
    remote, local = exchange

    @pl.when(last_step)
    def _():
        for cp in remote:
            cp.wait_recv()
        for cp in remote:
            cp.wait_send()
        for cp in local:
            cp.wait()


def _exchange_sems(npart):
    return [pltpu.SemaphoreType.DMA((7 * npart,)), pltpu.SemaphoreType.DMA((7 * npart,)),
            pltpu.SemaphoreType.DMA((npart,))]


def _attn_bwd(q, kt, k, vxt, d_o, o, lse, parts):
    t, rs = ATT_T, ATT_STRIP
    nq = SEQ // t
    npart = len(parts)
    nsteps = MLA_HEADS // 2

    def body(q_ref, kt_ref, k_ref, vxt_ref, do_ref, o_ref, l_ref, *rest):
        part_refs, rest = rest[:npart], rest[npart:]
        dq_ref, dk_ref, dv_ref = rest[:3]
        land_refs, rest = rest[3:3 + npart], rest[3 + npart:]
        s_scr, dp_scr, p_scr, ds_scr, st_scr, send_sems, recv_sems, local_sems = rest
        exchange = _exchange_parts(part_refs, land_refs, send_sems, recv_sems, local_sems)
        _exchange_start(pl.program_id(0) == 0, exchange)
        dk_ref[...] = jnp.zeros_like(dk_ref)
        dv_ref[...] = jnp.zeros_like(dv_ref)
        lane = lax.broadcasted_iota(jnp.int32, (t, LANES), 1)

        def qtile(i, carry):
            ioff = pl.multiple_of(i * t, t)
            do_i = do_ref[pl.ds(ioff, t), :]
            o_i = o_ref[pl.ds(ioff, t), :]
            l_i = l_ref[pl.ds(ioff, t), :]
            for a in range(2):
                sl = slice(HEAD_PAD * a, HEAD_PAD * (a + 1))
                sel = (lane < V_HEAD_DIM) if a == 0 else (lane >= V_HEAD_DIM)
                doa = jnp.where(sel, do_i, 0.0)
                oa = o_i
                if a == 1:
                    doa = pltpu.roll(doa, V_HEAD_DIM, 1)
                    oa = pltpu.roll(o_i, V_HEAD_DIM, 1)
                st_scr[0] = jnp.broadcast_to(jnp.sum(doa * oa, axis=1, keepdims=True), (t, LANES))
                st_scr[1] = jnp.broadcast_to(l_i[:, V_HEAD_DIM * a:V_HEAD_DIM * a + 1], (t, LANES))
                doa_bf = doa.astype(BF16)
                qa = q_ref[pl.ds(ioff, t), sl]

                def block(j, masked, dq_acc, sl=sl, qa=qa, doa_bf=doa_bf):
                    joff = pl.multiple_of(j * t, t)
                    s_scr[...] = _dot(qa, kt_ref[sl, pl.ds(joff, t)], _NN)
                    dp_scr[...] = _dot(doa_bf, vxt_ref[sl, pl.ds(joff, t)], _NN)
                    for r in range(t // rs):
                        rows = slice(rs * r, rs * (r + 1))
                        p = jnp.exp(s_scr[rows, :] - st_scr[1, rows, :1])
                        if masked:
                            rowi = lax.broadcasted_iota(jnp.int32, (rs, t), 0) + rs * r
                            coli = lax.broadcasted_iota(jnp.int32, (rs, t), 1)
                            p = jnp.where(coli <= rowi, p, 0.0)
                        p_scr[rows, :] = p.astype(BF16)
                        ds_scr[rows, :] = (p * (dp_scr[rows, :] - st_scr[0, rows, :1])).astype(BF16)
                    dk_ref[pl.ds(joff, t), sl] += _dot(ds_scr[...], qa, _TN)
                    dv_ref[pl.ds(joff, t), sl] += _dot(p_scr[...], doa_bf, _TN)
                    return dq_acc + _dot(ds_scr[...], k_ref[pl.ds(joff, t), sl], _NN)

                dq_acc = lax.fori_loop(0, i, lambda j, acc: block(j, False, acc), jnp.zeros((t, HEAD_PAD), F32))
                dq_ref[pl.ds(ioff, t), sl] = block(i, True, dq_acc)
            return carry

        lax.fori_loop(0, nq, qtile, 0)
        _exchange_finish(pl.program_id(0) == nsteps - 1, exchange)

    hw = MLA_HEADS * HEAD_PAD
    wide = pl.BlockSpec((SEQ, 2 * HEAD_PAD), lambda p: (0, p))
    wide_t = pl.BlockSpec((2 * HEAD_PAD, SEQ), lambda p: (p, 0))
    narrow = pl.BlockSpec((SEQ, LANES), lambda p: (0, p))
    hbm = pl.BlockSpec(memory_space=pl.ANY)
    res = pl.pallas_call(
        body, name="attn_bwd", grid=(nsteps,),
        in_specs=[wide, wide_t, wide, wide_t, narrow, narrow, narrow] + [hbm] * npart,
        out_specs=[wide, wide, wide] + [hbm] * npart,
        out_shape=[jax.ShapeDtypeStruct((SEQ, hw), F32)] * 3 + [jax.ShapeDtypeStruct(p.shape, p.dtype) for p in parts],
        scratch_shapes=[pltpu.VMEM((t, t), F32), pltpu.VMEM((t, t), F32), pltpu.VMEM((t, t), BF16),
                        pltpu.VMEM((t, t), BF16), pltpu.VMEM((2, t, LANES), F32)] + _exchange_sems(npart),
        compiler_params=pltpu.CompilerParams(dimension_semantics=("arbitrary",), vmem_limit_bytes=VMEM_BIG),
    )(q, kt, k, vxt, d_o, o, lse, *parts)
    return res[0], res[1], res[2], res[3:]


def _sgu_math(u, v, zb, lg, lb, ws_ref, bias):
    ug, dug = _gelu_and_grad(u)
    vg, dvg = _gelu_and_grad(v)
    mu = jnp.mean(vg, axis=1, keepdims=True)
    xc = vg - mu
    rstd = lax.rsqrt(jnp.mean(xc * xc, axis=1, keepdims=True) + LN_EPS)
    xh = xc * rstd
    vn_bf = (xh * lg + lb).astype(BF16)
    grp = lax.broadcasted_iota(jnp.int32, (CHUNK, SGU_WIDTH), 1) // SGU_GROUP_DIM
    r_i = lax.broadcasted_iota(jnp.int32, (CHUNK, CHUNK), 0)
    c_i = lax.broadcasted_iota(jnp.int32, (CHUNK, CHUNK), 1)
    tri, tri_t = r_i >= c_i, r_i <= c_i
    mixed = bias
    for g in range(SGU_GROUPS):
        wt = jnp.where(tri, ws_ref[g], 0.0).astype(BF16)
        mixed = mixed + jnp.where(grp == g, _dot(wt, vn_bf, _NN), 0.0)
    sb = _sigmoid(zb)
    return ug, dug, dvg, rstd, xh, vn_bf, grp, tri, tri_t, mixed, sb


def _sgu_fwd(h_b, lg, lb, w_s, bias_full):
    def body(u_ref, v_ref, zb_ref, lg_ref, lb_ref, ws_ref, bias_ref, yb_ref):
        zb = zb_ref[...]
        ug, _, _, _, _, _, _, _, _, mixed, sb = _sgu_math(u_ref[...], v_ref[...], zb, lg_ref[...], lb_ref[...],
                                                       ws_ref, bias_ref[...])
        yb_ref[...] = (ug * mixed) * (zb * sb)

    blk = lambda c: pl.BlockSpec((CHUNK, SGU_WIDTH), lambda i, c=c: (i, c))
    full2 = lambda shape: pl.BlockSpec(shape, lambda i: (0, 0))
    return pl.pallas_call(
        body, name="sgu_fwd", grid=(SEQ // CHUNK,),
        in_specs=[blk(0), blk(1), blk(2), full2((1, SGU_WIDTH)), full2((1, SGU_WIDTH)),
                  pl.BlockSpec((SGU_GROUPS, CHUNK, CHUNK), lambda i: (0, 0, 0)), full2((CHUNK, SGU_WIDTH))],
        out_specs=pl.BlockSpec((CHUNK, SGU_WIDTH), lambda i: (i, 0)),
        out_shape=jax.ShapeDtypeStruct((SEQ, SGU_WIDTH), F32),
        compiler_params=pltpu.CompilerParams(dimension_semantics=("arbitrary",)),
    )(h_b, h_b, h_b, lg, lb, w_s, bias_full)


def _sgu_bwd(h_b, d_yb, lg, lb, w_s, w_st, bias_full, parts):
    nsteps = SEQ // CHUNK
    npart = len(parts)

    def body(u_ref, v_ref, zb_ref, dyb_ref, lg_ref, lb_ref, ws_ref, wst_ref, bias_ref, *rest):
        part_refs, rest = rest[:npart], rest[npart:]
        dhb_ref, dws_ref, dbs_ref, dlg_ref, dlb_ref, dbb_ref = rest[:6]
        land_refs, (dbias_acc, send_sems, recv_sems, local_sems) = rest[6:6 + npart], rest[6 + npart:]
        step = pl.program_id(0)
        exchange = _exchange_parts(part_refs, land_refs, send_sems, recv_sems, local_sems)
        _exchange_start(step == 0, exchange)

        @pl.when(step == 0)
        def _():
            dbb_ref[...] = jnp.zeros_like(dbb_ref)
            dws_ref[...] = jnp.zeros_like(dws_ref)
            dlg_ref[...] = jnp.zeros_like(dlg_ref)
            dlb_ref[...] = jnp.zeros_like(dlb_ref)
            dbias_acc[...] = jnp.zeros_like(dbias_acc)

        zb = zb_ref[...]
        lg = lg_ref[...]
        ug, dug, dvg, rstd, xh, vn_bf, grp, tri, tri_t, mixed, sb = _sgu_math(
            u_ref[...], v_ref[...], zb, lg, lb_ref[...], ws_ref, bias_ref[...])
        dyb = dyb_ref[...]
        dsgu = dyb * (zb * sb)
        dzb = dyb * (ug * mixed) * (sb * (1.0 + zb * (1.0 - sb)))
        du = dsgu * mixed * dug
        dmixed = dsgu * ug
        dbias_acc[...] += dmixed
        dvn = jnp.zeros((CHUNK, SGU_WIDTH), F32)
        for g in range(SGU_GROUPS):
            dm_g = jnp.where(grp == g, dmixed, 0.0).astype(BF16)
            wtt = jnp.where(tri_t, wst_ref[g], 0.0).astype(BF16)
            dvn = dvn + _dot(wtt, dm_g, _NN)
            dws_ref[g] += jnp.where(tri, _dot(dm_g, vn_bf, _NT), 0.0)
        dlg_ref[...] += jnp.sum(dvn * xh, axis=0, keepdims=True)
        dlb_ref[...] += jnp.sum(dvn, axis=0, keepdims=True)
        dxh = dvn * lg
        dvgel = rstd * (dxh - jnp.mean(dxh, axis=1, keepdims=True) - xh * jnp.mean(dxh * xh, axis=1, keepdims=True))
        _store_grad(dhb_ref, dbb_ref, 0, du)
        _store_grad(dhb_ref, dbb_ref, SGU_WIDTH, dvgel * dvg)
        _store_grad(dhb_ref, dbb_ref, 2 * SGU_WIDTH, dzb)

        @pl.when(step == nsteps - 1)
        def _():
            acc = dbias_acc[...]
            lane = lax.broadcasted_iota(jnp.int32, (CHUNK, LANES), 1)
            out = jnp.zeros((CHUNK, LANES), F32)
            for g in range(SGU_GROUPS):
                sg = jnp.sum(jnp.where(grp == g, acc, 0.0), axis=1, keepdims=True)
                out = jnp.where(lane == g, sg, out)
            dbs_ref[...] = out

        _exchange_finish(step == nsteps - 1, exchange)

    blk = lambda c: pl.BlockSpec((CHUNK, SGU_WIDTH), lambda i, c=c: (i, c))
    full2 = lambda shape: pl.BlockSpec(shape, lambda i: (0, 0))
    full3 = pl.BlockSpec((SGU_GROUPS, CHUNK, CHUNK), lambda i: (0, 0, 0))
    hbm = pl.BlockSpec(memory_space=pl.ANY)
    res = pl.pallas_call(
        body, name="sgu_bwd", grid=(nsteps,),
        in_specs=[blk(0), blk(1), blk(2), pl.BlockSpec((CHUNK, SGU_WIDTH), lambda i: (i, 0)),
                  full2((1, SGU_WIDTH)), full2((1, SGU_WIDTH)), full3, full3, full2((CHUNK, SGU_WIDTH))] + [hbm] * npart,
        out_specs=[pl.BlockSpec((CHUNK, SEG_B), lambda i: (i, 0)), full3, full2((CHUNK, LANES)),
                   full2((1, SGU_WIDTH)), full2((1, SGU_WIDTH)), full2((1, SEG_B))] + [hbm] * npart,
        out_shape=[jax.ShapeDtypeStruct((SEQ, SEG_B), BF16),
                   jax.ShapeDtypeStruct((SGU_GROUPS, CHUNK, CHUNK), F32),
                   jax.ShapeDtypeStruct((CHUNK, LANES), F32),
                   jax.ShapeDtypeStruct((1, SGU_WIDTH), F32), jax.ShapeDtypeStruct((1, SGU_WIDTH), F32),
                   jax.ShapeDtypeStruct((1, SEG_B), F32)] + [jax.ShapeDtypeStruct(p.shape, p.dtype) for p in parts],
        scratch_shapes=[pltpu.VMEM((CHUNK, SGU_WIDTH), F32)] + _exchange_sems(npart),
        compiler_params=pltpu.CompilerParams(dimension_semantics=("arbitrary",)),
    )(h_b, h_b, h_b, d_yb, lg, lb, w_s, w_st, bias_full, *parts)
    return res[:6], res[6:]


def _merge(x, o, h_a, y_b, target, w_oa, w_ob, w_out, ln_g, ln_b):
    tm = 256
    nsteps = SEQ // tm

    def body(x_ref, o_ref, ga_ref, gb_ref, za_ref, yb_ref, tgt_ref, woa_ref, wob_ref, wout_ref, lng_ref, lnb_ref,
             loss_ref, dxr_ref, dha_ref, do_ref, dyb_ref, poa_ref, pob_ref, pout_ref, dlng_ref, dlnb_ref, dba_ref,
             dwoa_ref, dwob_ref, dwout_ref):
        step = pl.program_id(0)

        @pl.when(step == 0)
        def _():
            for r in (loss_ref, dwoa_ref, dwob_ref, dwout_ref, dlng_ref, dlnb_ref, dba_ref):
                r[...] = jnp.zeros_like(r)

        o = o_ref[...]
        za = za_ref[...]
        sa = _sigmoid(za)
        ya_bf = (o * (za * sa)).astype(BF16)
        yb_bf = yb_ref[...].astype(BF16)
        woa, wob, wout = woa_ref[...], wob_ref[...], wout_ref[...]
        pa = _dot(ya_bf, woa, _NN)
        pb = _dot(yb_bf, wob, _NN)
        sga = _sigmoid(ga_ref[...])
        sgb = _sigmoid(gb_ref[...])
        merged_bf = (sga * pa + sgb * pb).astype(BF16)
        r = DN_ALPHA * x_ref[...] + _dot(merged_bf, wout, _NN)
        mu = jnp.mean(r, axis=1, keepdims=True)
        rc = r - mu
        rstd = lax.rsqrt(jnp.mean(rc * rc, axis=1, keepdims=True) + LN_EPS)
        xh = rc * rstd
        lng = lng_ref[...]
        y = xh * lng + lnb_ref[...]
        e = y - tgt_ref[...]
        loss_ref[...] += 0.5 * jnp.sum(jnp.sum(e * e, axis=1, keepdims=True) * (1.0 / D_MODEL), axis=0, keepdims=True)

        dy = e * (1.0 / D_MODEL)
        dlng_ref[...] += jnp.sum(dy * xh, axis=0, keepdims=True)
        dlnb_ref[...] += jnp.sum(dy, axis=0, keepdims=True)
        dxh = dy * lng
        dr = rstd * (dxh - jnp.mean(dxh, axis=1, keepdims=True) - xh * jnp.mean(dxh * xh, axis=1, keepdims=True))
        dxr_ref[...] = DN_ALPHA * dr
        dr_bf = dr.astype(BF16)
        dwout_ref[...] += _dot(merged_bf, dr_bf, _TN)
        dmerged = _dot(dr_bf, wout, _NT)
        dpa_bf = (dmerged * sga).astype(BF16)
        dpb_bf = (dmerged * sgb).astype(BF16)
        _store_grad(dha_ref, dba_ref, 0, dmerged * pa * (sga * (1.0 - sga)))
        _store_grad(dha_ref, dba_ref, D_MODEL, dmerged * pb * (sgb * (1.0 - sgb)))
        dwoa_ref[...] += _dot(ya_bf, dpa_bf, _TN)
        dwob_ref[...] += _dot(yb_bf, dpb_bf, _TN)
        dya = _dot(dpa_bf, woa, _NT)
        dyb_ref[...] = _dot(dpb_bf, wob, _NT)
        do_ref[...] = dya * (za * sa)
        _store_grad(dha_ref, dba_ref, 2 * D_MODEL, dya * o * (sa * (1.0 + za * (1.0 - sa))))

        @pl.when(step == nsteps - 1)
        def _():
            cols = D_MODEL // N_DEV
            for j in range(N_DEV):
                poa_ref[j] = dwoa_ref[:, cols * j:cols * (j + 1)].astype(BF16)
                pob_ref[j] = dwob_ref[:, cols * j:cols * (j + 1)].astype(BF16)
                pout_ref[j] = dwout_ref[cols * j:cols * (j + 1), :].astype(BF16)

    row = lambda w, c=0: pl.BlockSpec((tm, w), lambda i, c=c: (i, c))
    full = lambda shape: pl.BlockSpec(shape, lambda i: (0, 0))
    full3 = lambda shape: pl.BlockSpec(shape, lambda i: (0, 0, 0))
    return pl.pallas_call(
        body, name="merge", grid=(nsteps,),
        in_specs=[row(D_MODEL), row(MLA_WIDTH), row(D_MODEL, 0), row(D_MODEL, 1), row(MLA_WIDTH, 4), row(SGU_WIDTH),
                  row(D_MODEL), full((MLA_WIDTH, D_MODEL)), full((SGU_WIDTH, D_MODEL)), full((D_MODEL, D_MODEL)),
                  full((1, D_MODEL)), full((1, D_MODEL))],
        out_specs=[full((1, LANES)), row(D_MODEL), row(SEG_A), row(MLA_WIDTH), row(SGU_WIDTH),
                   full3((N_DEV, MLA_WIDTH, D_MODEL // N_DEV)), full3((N_DEV, SGU_WIDTH, D_MODEL // N_DEV)),
                   full3((N_DEV, D_MODEL // N_DEV, D_MODEL)), full((1, D_MODEL)), full((1, D_MODEL)), full((1, SEG_A))],
        out_shape=[jax.ShapeDtypeStruct((1, LANES), F32),
                   jax.ShapeDtypeStruct((SEQ, D_MODEL), F32), jax.ShapeDtypeStruct((SEQ, SEG_A), BF16),
                   jax.ShapeDtypeStruct((SEQ, MLA_WIDTH), F32), jax.ShapeDtypeStruct((SEQ, SGU_WIDTH), F32),
                   jax.ShapeDtypeStruct((N_DEV, MLA_WIDTH, D_MODEL // N_DEV), BF16),
                   jax.ShapeDtypeStruct((N_DEV, SGU_WIDTH, D_MODEL // N_DEV), BF16),
                   jax.ShapeDtypeStruct((N_DEV, D_MODEL // N_DEV, D_MODEL), BF16),
                   jax.ShapeDtypeStruct((1, D_MODEL), F32), jax.ShapeDtypeStruct((1, D_MODEL), F32),
                   jax.ShapeDtypeStruct((1, SEG_A), F32)],
        scratch_shapes=[pltpu.VMEM((MLA_WIDTH, D_MODEL), F32), pltpu.VMEM((SGU_WIDTH, D_MODEL), F32),
                        pltpu.VMEM((D_MODEL, D_MODEL), F32)],
        compiler_params=pltpu.CompilerParams(dimension_semantics=("arbitrary",), vmem_limit_bytes=VMEM_BIG),
    )(x, o, h_a, h_a, h_a, y_b, target, w_oa, w_ob, w_out, ln_g, ln_b)


def _mla_bwd(dq, dk, dv, h_c, xt_bf, gq, gkv, wq, wkn, wv, c_t, sa_t, sb_t, parts):
    tm = 256
    hw = MLA_HEADS * HEAD_PAD
    npart = len(parts)
    nsteps = SEQ // tm

    def body(dq_ref, dk_ref, dv_ref, cq_ref, ckv_ref, xt_ref, gq_ref, gkv_ref, wq_ref, wkn_ref, wv_ref, c_ref, sa_ref,
             sb_ref, *rest):
        part_refs, rest = rest[:npart], rest[npart:]
        dhc_ref, puq_ref, dwkn_ref, dwv_ref, dgq_ref, dgkv_ref, dbc_ref, dwc_ref = rest[:8]
        land_refs, (pre_ref, dwq_ref, dwc_acc, send_sems, recv_sems, local_sems) = rest[8:8 + npart], rest[8 + npart:]
        exchange = _exchange_parts(part_refs, land_refs, send_sems, recv_sems, local_sems)
        _exchange_start(pl.program_id(0) == 0, exchange)

        @pl.when(pl.program_id(0) == 0)
        def _():
            for r in (dwq_ref, dwc_acc, dwkn_ref, dwv_ref, dgq_ref, dgkv_ref, dbc_ref):
                r[...] = jnp.zeros_like(r)

        c, sa, sb = c_ref[...], sa_ref[...], sb_ref[...]
        lane = lax.broadcasted_iota(jnp.int32, (tm, LANES), 1)
        rope_lanes = jnp.logical_and(lane >= ROPE_LO, lane < ROPE_HI)

        cq = cq_ref[...]
        gq = gq_ref[...]
        rq = lax.rsqrt(jnp.sum(cq * cq, axis=1, keepdims=True) * (1.0 / Q_LORA_RANK) + RMS_EPS)
        nq = cq * rq
        cqn_bf = (nq * gq).astype(BF16)
        for h in range(MLA_HEADS):
            sl = slice(HEAD_PAD * h, HEAD_PAD * (h + 1))
            pre_ref[:, sl] = _rope_t(dq_ref[:, sl] * ATTN_SCALE, c, sa, sb).astype(BF16)
        dqpre_bf = pre_ref[...]
        dcqn = _dot(dqpre_bf, wq_ref[...], _NT)
        dwq_ref[...] += _dot(cqn_bf, dqpre_bf, _TN)
        dgq_ref[...] += jnp.sum(dcqn * nq, axis=0, keepdims=True)
        dnq = dcqn * gq
        _store_grad(dhc_ref, dbc_ref, 0,
                    rq * (dnq - nq * (jnp.sum(dnq * nq, axis=1, keepdims=True) * (1.0 / Q_LORA_RANK))))

        ckv = ckv_ref[...]
        gkv = gkv_ref[...]
        rkv = lax.rsqrt(jnp.sum(ckv * ckv, axis=1, keepdims=True) * (1.0 / KV_LORA_RANK) + RMS_EPS)
        nkv = ckv * rkv
        ckvn_bf = (nkv * gkv).astype(BF16)
        dk = dk_ref[...]
        dk_bf = dk.astype(BF16)
        dv_bf = dv_ref[...].astype(BF16)
        dckvn = _dot(dk_bf, wkn_ref[...], _NT) + _dot(dv_bf, wv_ref[...], _NT)
        dwkn_ref[...] += _dot(ckvn_bf, dk_bf, _TN)
        dwv_ref[...] += _dot(ckvn_bf, dv_bf, _TN)
        dgkv_ref[...] += jnp.sum(dckvn * nkv, axis=0, keepdims=True)
        dnkv = dckvn * gkv
        _store_grad(dhc_ref, dbc_ref, CQ_PAD, rkv * (
            dnkv - nkv * (jnp.sum(dnkv * nkv, axis=1, keepdims=True) * (1.0 / KV_LORA_RANK))))
        dkpe = jnp.zeros((tm, LANES), F32)
        for h in range(MLA_HEADS):
            dkpe = dkpe + dk[:, HEAD_PAD * h:HEAD_PAD * (h + 1)]
        _store_grad(dhc_ref, dbc_ref, CQ_PAD + LANES, _rope_t(jnp.where(rope_lanes, dkpe, 0.0), c, sa, sb))
        dwc_acc[...] += _dot(xt_ref[...], dhc_ref[...], _NN)

        @pl.when(pl.program_id(0) == SEQ // tm - 1)
        def _():
            dwc_ref[...] = dwc_acc[...].astype(BF16)
            rows = Q_LORA_RANK // N_DEV
            for j in range(N_DEV):
                for h in range(MLA_HEADS):
                    puq_ref[j, :, QK_HEAD_DIM * h:QK_HEAD_DIM * (h + 1)] = dwq_ref[
                        rows * j:rows * (j + 1), HEAD_PAD * h:HEAD_PAD * h + QK_HEAD_DIM].astype(BF16)

        _exchange_finish(pl.program_id(0) == nsteps - 1, exchange)

    full = lambda shape: pl.BlockSpec(shape, lambda i: (0, 0))
    row = lambda w, c=0: pl.BlockSpec((tm, w), lambda i, c=c: (i, c))
    hbm = pl.BlockSpec(memory_space=pl.ANY)
    res = pl.pallas_call(
        body, name="mla_bwd", grid=(nsteps,),
        in_specs=[row(hw), row(hw), row(hw), row(CQ_PAD, 0), row(LANES, CQ_PAD // LANES),
                  pl.BlockSpec((D_MODEL, tm), lambda i: (0, i)),
                  full((1, CQ_PAD)), full((1, KV_LORA_RANK)), full((CQ_PAD, hw)), full((KV_LORA_RANK, hw)),
                  full((KV_LORA_RANK, hw)), row(LANES), row(LANES), row(LANES)] + [hbm] * npart,
        out_specs=[row(SEG_C), pl.BlockSpec((N_DEV, Q_LORA_RANK // N_DEV, MLA_HEADS * QK_HEAD_DIM), lambda i: (0, 0, 0)),
                   full((KV_LORA_RANK, hw)), full((KV_LORA_RANK, hw)),
                   full((1, CQ_PAD)), full((1, KV_LORA_RANK)), full((1, SEG_C)), full((D_MODEL, SEG_C))] + [hbm] * npart,
        out_shape=[jax.ShapeDtypeStruct((SEQ, SEG_C), BF16),
                   jax.ShapeDtypeStruct((N_DEV, Q_LORA_RANK // N_DEV, MLA_HEADS * QK_HEAD_DIM), BF16),
                   jax.ShapeDtypeStruct((KV_LORA_RANK, hw), F32), jax.ShapeDtypeStruct((KV_LORA_RANK, hw), F32),
                   jax.ShapeDtypeStruct((1, CQ_PAD), F32), jax.ShapeDtypeStruct((1, KV_LORA_RANK), F32),
                   jax.ShapeDtypeStruct((1, SEG_C), F32), jax.ShapeDtypeStruct((D_MODEL, SEG_C), BF16)]
        + [jax.ShapeDtypeStruct(p.shape, p.dtype) for p in parts],
        scratch_shapes=[pltpu.VMEM((tm, hw), BF16), pltpu.VMEM((CQ_PAD, hw), F32), pltpu.VMEM((D_MODEL, SEG_C), F32)]
        + _exchange_sems(npart),
        compiler_params=pltpu.CompilerParams(dimension_semantics=("arbitrary",), vmem_limit_bytes=VMEM_MID),
    )(dq, dk, dv, h_c, h_c, xt_bf, gq, gkv, wq, wkn, wv, c_t, sa_t, sb_t, *parts)
    return res[:8], res[8:]


def _adamw_all(ws, gs, ms, vs):
    n = len(ws)
    c1 = 1.0 / (1.0 - ADAM_B1 ** ADAM_STEP)
    c2 = 1.0 / (1.0 - ADAM_B2 ** ADAM_STEP)

    def body(*refs):
        for idx in range(n):
            w, g, m, v = (refs[idx][...], refs[n + idx][...], refs[2 * n + idx][...], refs[3 * n + idx][...])
            m_new = ADAM_B1 * m + (1.0 - ADAM_B1) * g
            v_new = ADAM_B2 * v + (1.0 - ADAM_B2) * (g * g)
            delta = -ADAM_LR * ((m_new * c1) / (jnp.sqrt(v_new * c2) + ADAM_EPS) + ADAM_WD * w)
            refs[4 * n + idx][...] = delta
            refs[5 * n + idx][...] = m_new
            refs[6 * n + idx][...] = v_new

    shapes = [jax.ShapeDtypeStruct(w.shape, F32) for w in ws]
    outs = pl.pallas_call(
        body, name="adamw", out_shape=shapes * 3,
        compiler_params=pltpu.CompilerParams(vmem_limit_bytes=VMEM_BIG),
    )(*ws, *gs, *ms, *vs)
    return outs[:n], outs[n:2 * n], outs[2 * n:]


SHARD_W = IN_WIDTH // N_DEV

_PIECES = [(0, 384, 2, 0), (384, 512, 2, CQ_PAD), (512, 544, 2, CQ_PAD + LANES + ROPE_LO),
           (544, 1056, 0, 2 * D_MODEL), (1056, 1568, 1, 0), (1568, 2080, 1, SGU_WIDTH),
           (2080, 2592, 1, 2 * SGU_WIDTH), (2592, 3616, 0, 0), (3616, 4640, 0, D_MODEL)]


def _column_runs():
    runs = []
    for n0, n1, seg, d0 in _PIECES:
        for j in range(N_DEV):
            lo, hi = max(n0, j * SHARD_W), min(n1, (j + 1) * SHARD_W)
            if lo < hi:
                runs.append((j, lo - j * SHARD_W, hi - j * SHARD_W, seg, d0 + lo - n0))
    return runs


def _mesh_pos():
    return lax.axis_index("x"), lax.axis_index("y"), lax.axis_index("c")


def _remote(src, dst, send_sems, recv_sems, k, to):
    return pltpu.make_async_remote_copy(src_ref=src, dst_ref=dst, send_sem=send_sems.at[k], recv_sem=recv_sems.at[k],
                                        device_id=to, device_id_type=pl.DeviceIdType.MESH)


def _gather_exchange(gats, send_sems, recv_sems, meanwhile=None):
    x, y, c = _mesh_pos()
    me, sibling = (x, y, c), (x, y, 1 - c)
    chips = [(1 - x, y), (x, 1 - y), (1 - x, 1 - y)]

    def copy(a, k, blk, to):
        slab = gats[a].at[4 * blk[0] + 2 * blk[1] + blk[2]]
        return _remote(slab, slab, send_sems, recv_sems, 7 * a + k, to)

    arrays = range(len(gats))
    first = [copy(a, 1 + j, me, (*chip, c)) for j, chip in enumerate(chips) for a in arrays]
    first += [copy(a, 0, me, sibling) for a in arrays]
    for cp in first:
        cp.start()
    if meanwhile is not None:
        meanwhile()
    passed = []
    for j, chip in enumerate(chips):
        for a in arrays:
            copy(a, 1 + j, (*chip, c), me).wait_recv()
            fwd = copy(a, 4 + j, (*chip, c), sibling)
            fwd.start()
            passed.append(fwd)
    for a in arrays:
        copy(a, 0, sibling, me).wait_recv()
    for j, chip in enumerate(chips):
        for a in arrays:
            copy(a, 4 + j, (*chip, 1 - c), me).wait_recv()
    for cp in first + passed:
        cp.wait_send()


def _gather_behind(own, gats, send_sems, recv_sems, local_sems, step, mid, last):
    x, y, c = _mesh_pos()
    me, sibling = (x, y, c), (x, y, 1 - c)
    chips = [(1 - x, y), (x, 1 - y), (1 - x, 1 - y)]
    arrays = range(len(gats))

    def copy(a, k, blk, to, src=None):
        slab = gats[a].at[4 * blk[0] + 2 * blk[1] + blk[2]]
        return _remote(slab if src is None else src, slab, send_sems, recv_sems, 7 * a + k, to)

    first = [copy(a, 1 + j, me, (*chip, c), src=own[a]) for j, chip in enumerate(chips) for a in arrays]
    first += [copy(a, 0, me, sibling, src=own[a]) for a in arrays]
    local = [pltpu.make_async_copy(own[a], gats[a].at[4 * x + 2 * y + c], local_sems.at[a]) for a in arrays]
    passed = [copy(a, 4 + j, (*chip, c), sibling) for j, chip in enumerate(chips) for a in arrays]

    @pl.when(step == 0)
    def _():
        for cp in first + local:
            cp.start()

    @pl.when(step == mid)
    def _():
        for j, chip in enumerate(chips):
            for a in arrays:
                copy(a, 1 + j, (*chip, c), me).wait_recv()
            for a in arrays:
                copy(a, 4 + j, (*chip, c), sibling).start()

    @pl.when(step == last)
    def _():
        for a in arrays:
            copy(a, 0, sibling, me).wait_recv()
        for j, chip in enumerate(chips):
            for a in arrays:
                copy(a, 4 + j, (*chip, 1 - c), me).wait_recv()
        for cp in first + passed:
            cp.wait_send()
        for cp in local:
            cp.wait()


def _gather_first(w_in, w_uq2, w_oa, w_ob, w_out, x2, pos_col, invf_lane):
    hw = MLA_HEADS * HEAD_PAD
    uq_rows = Q_LORA_RANK // N_DEV
    rows = 256

    def body(win_ref, wuq_ref, woa_ref, wob_ref, wout_ref, x_ref, pos_ref, invf_ref,
             wc_ref, wq_ref, winb_ref, oab_ref, obb_ref, outb_ref, xb_ref, xt_ref, c_ref, sa_ref, sb_ref,
             g_uq, blk0, send_sems, recv_sems):
        def local_work():
            for i in range(SEQ // rows):
                xi = x_ref[rows * i:rows * (i + 1), :]
                xb_ref[rows * i:rows * (i + 1), :] = xi.astype(BF16)
                xt_ref[:, rows * i:rows * (i + 1)] = xi.T.astype(BF16)
            ang = pos_ref[...].astype(F32) * invf_ref[...]
            cs, sn = jnp.cos(ang), jnp.sin(ang)
            lane = lax.broadcasted_iota(jnp.int32, ang.shape, 1)
            c_ref[...] = jnp.where(lane < ROPE_LO, 1.0, jnp.where(lane < ROPE_HI, cs, 0.0))
            sa_ref[...] = jnp.where(jnp.logical_and(lane >= ROPE_LO, lane < ROPE_MID), -sn, 0.0)
            sb_ref[...] = jnp.where(jnp.logical_and(lane >= ROPE_MID, lane < ROPE_HI), sn, 0.0)

        x, y, c = _mesh_pos()
        me = (x, y, c)
        winb_ref[...] = win_ref[0].astype(BF16)
        oab_ref[...] = woa_ref[0].astype(BF16)
        obb_ref[...] = wob_ref[0].astype(BF16)
        outb_ref[...] = wout_ref[0].astype(BF16)
        g_uq[4 * x + 2 * y + c] = wuq_ref[...].astype(BF16)

        chip0 = jnp.logical_and(x == 0, y == 0)
        south = c == 0
        half = D_MODEL // 2
        halves = [blk0.at[pl.ds(0, half)], blk0.at[pl.ds(half, half)]]

        def bcopy(k, to, part=None):
            ref = blk0 if part is None else halves[part]
            return _remote(ref, ref, send_sems, recv_sems, 7 + k, to)

        sends0 = [(0, (0, 0, 1), None), (1, (1, 0, 0), 0), (2, (0, 1, 0), 1), (3, (1, 0, 0), 1), (4, (0, 1, 0), 0)]

        @pl.when(jnp.logical_and(chip0, south))
        def _():
            blk0[...] = winb_ref[...]
            for k, to, part in sends0:
                bcopy(k, to, part).start()

        _gather_exchange([g_uq], send_sems, recv_sems, meanwhile=local_work)

        for (cx, cy), first_k, first_half, second_k in (((1, 0), 1, 0, 3), ((0, 1), 2, 1, 4)):
            @pl.when(jnp.logical_and(jnp.logical_and(x == cx, y == cy), south))
            def _(cx=cx, cy=cy, first_k=first_k, first_half=first_half, second_k=second_k):
                bcopy(first_k, me, first_half).wait_recv()
                onward = bcopy(5 + first_half, (1, 1, 0), first_half)
                onward.start()
                bcopy(second_k, me, 1 - first_half).wait_recv()
                north = bcopy(7, (cx, cy, 1))
                north.start()
                onward.wait_send()
                north.wait_send()

        @pl.when(jnp.logical_and(jnp.logical_and(x == 1, y == 1), south))
        def _():
            bcopy(5, me, 0).wait_recv()
            bcopy(6, me, 1).wait_recv()
            north = bcopy(7, (1, 1, 1))
            north.start()
            north.wait_send()

        @pl.when(jnp.logical_and(chip0, c == 1))
        def _():
            bcopy(0, me).wait_recv()

        @pl.when(jnp.logical_and(jnp.logical_not(chip0), c == 1))
        def _():
            bcopy(7, me).wait_recv()

        @pl.when(jnp.logical_and(chip0, south))
        def _():
            for k, to, part in sends0:
                bcopy(k, to, part).wait_send()

        for j, s0, s1, seg, d0 in _column_runs():
            if seg == 2:
                wc_ref[:, d0:d0 + (s1 - s0)] = blk0[:, s0:s1]
        zeros = lambda r, w: jnp.zeros((r, w), BF16)
        wc_ref[:, Q_LORA_RANK:CQ_PAD] = zeros(D_MODEL, CQ_PAD - Q_LORA_RANK)
        wc_ref[:, CQ_PAD + LANES:CQ_PAD + LANES + ROPE_LO] = zeros(D_MODEL, ROPE_LO)
        wc_ref[:, CQ_PAD + LANES + ROPE_HI:SEG_C] = zeros(D_MODEL, LANES - ROPE_HI)
        wq_ref[Q_LORA_RANK:CQ_PAD, :] = zeros(CQ_PAD - Q_LORA_RANK, hw)
        for h in range(MLA_HEADS):
            wq_ref[0:Q_LORA_RANK, HEAD_PAD * h + QK_HEAD_DIM:HEAD_PAD * (h + 1)] = zeros(Q_LORA_RANK, HEAD_PAD - QK_HEAD_DIM)
        for j in range(N_DEV):
            for h in range(MLA_HEADS):
                wq_ref[uq_rows * j:uq_rows * (j + 1), HEAD_PAD * h:HEAD_PAD * h + QK_HEAD_DIM] = g_uq[
                    j, :, QK_HEAD_DIM * h:QK_HEAD_DIM * (h + 1)]

    vmem = pl.BlockSpec(memory_space=pltpu.VMEM)
    return pl.pallas_call(
        body, name="gather_first",
        out_shape=[jax.ShapeDtypeStruct((D_MODEL, SEG_C), BF16), jax.ShapeDtypeStruct((CQ_PAD, hw), BF16),
                   jax.ShapeDtypeStruct(w_in.shape[1:], BF16), jax.ShapeDtypeStruct(w_oa.shape[1:], BF16),
                   jax.ShapeDtypeStruct(w_ob.shape[1:], BF16), jax.ShapeDtypeStruct(w_out.shape[1:], BF16),
                   jax.ShapeDtypeStruct((SEQ, D_MODEL), BF16), jax.ShapeDtypeStruct((D_MODEL, SEQ), BF16)]
        + [jax.ShapeDtypeStruct((SEQ, LANES), F32)] * 3,
        in_specs=[vmem] * 8, out_specs=[vmem] * 11,
        scratch_shapes=[pltpu.VMEM((N_DEV, uq_rows, MLA_HEADS * QK_HEAD_DIM), BF16), pltpu.VMEM((D_MODEL, SHARD_W), BF16),
                        pltpu.SemaphoreType.DMA((15,)), pltpu.SemaphoreType.DMA((15,))],
        compiler_params=pltpu.CompilerParams(vmem_limit_bytes=VMEM_BIG),
    )(w_in, w_uq2, w_oa, w_ob, w_out, x2, pos_col, invf_lane)


def _assemble_in(g_in):
    def body(g_ref, wa_ref, wb_ref):
        segs = [wa_ref, wb_ref]
        for j, s0, s1, seg, d0 in _column_runs():
            if seg < 2:
                segs[seg][:, d0:d0 + (s1 - s0)] = g_ref[j, :, s0:s1]

    return pl.pallas_call(
        body, name="assemble_in",
        out_shape=[jax.ShapeDtypeStruct((D_MODEL, SEG_A), BF16), jax.ShapeDtypeStruct((D_MODEL, SEG_B), BF16)],
        compiler_params=pltpu.CompilerParams(vmem_limit_bytes=VMEM_MID),
    )(g_in)


def _assemble_out(g_oa, g_ob, g_out):
    cols = D_MODEL // N_DEV

    def body(goa_ref, gob_ref, gout_ref, oa_ref, ob_ref, out_ref):
        for j in range(N_DEV):
            oa_ref[:, cols * j:cols * (j + 1)] = goa_ref[j]
            ob_ref[:, cols * j:cols * (j + 1)] = gob_ref[j]
            out_ref[cols * j:cols * (j + 1), :] = gout_ref[j]

    return pl.pallas_call(
        body, name="assemble_out",
        out_shape=[jax.ShapeDtypeStruct((MLA_WIDTH, D_MODEL), BF16), jax.ShapeDtypeStruct((SGU_WIDTH, D_MODEL), BF16),
                   jax.ShapeDtypeStruct((D_MODEL, D_MODEL), BF16)],
    )(g_oa, g_ob, g_out)


C_NAT = 544


P_IN_SPLIT = 896


def _to_parts(dwa, dwb):
    def body(dwa_ref, dwb_ref, phi_ref, plo_ref):
        phi_ref[0, :, 0:C_NAT] = jnp.zeros((P_IN_SPLIT, C_NAT), BF16)
        plo_ref[0, :, 0:C_NAT] = jnp.zeros((D_MODEL - P_IN_SPLIT, C_NAT), BF16)
        segs = [dwa_ref, dwb_ref]
        for j, s0, s1, seg, d0 in _column_runs():
            if seg < 2:
                phi_ref[j, :, s0:s1] = segs[seg][0:P_IN_SPLIT, d0:d0 + (s1 - s0)]
                plo_ref[j, :, s0:s1] = segs[seg][P_IN_SPLIT:D_MODEL, d0:d0 + (s1 - s0)]

    return pl.pallas_call(
        body, name="to_parts",
        out_shape=[jax.ShapeDtypeStruct((N_DEV, P_IN_SPLIT, SHARD_W), BF16),
                   jax.ShapeDtypeStruct((N_DEV, D_MODEL - P_IN_SPLIT, SHARD_W), BF16)],
        compiler_params=pltpu.CompilerParams(vmem_limit_bytes=VMEM_MID))(dwa, dwb)


def _dx_tail(dhs, ws, dx_res, dwc, p_uq, p_rep):
    ntile, sums_at = 8, 5
    tm = SEQ // ntile
    rep_rows = p_rep.shape[1]
    c_rows = D_MODEL // N_DEV
    spec = [((c_rows, C_NAT), BF16), (p_uq.shape[1:], BF16), ((rep_rows, LANES), F32)]
    n = len(spec)

    nseg = len(dhs)

    def body(*refs):
        dh_refs, w_refs = refs[:nseg], refs[nseg:2 * nseg]
        dxr_ref, dwc_ref, puq_ref, prep_ref, dx_ref, call_ref, guq_ref, repall_ref, pc_ref, c_all, rep_all = refs[
            2 * nseg:2 * nseg + 11]
        rest = refs[2 * nseg + 11:]
        ras, tbs, rbs = rest[0:n], rest[n:2 * n], rest[2 * n:3 * n]
        send_sems, recv_sems, gsend, grecv = rest[3 * n:]
        step = pl.program_id(0)
        x, y, c = _mesh_pos()
        me_idx = 4 * x + 2 * y + c
        me, sibling = (x, y, c), (x, y, 1 - c)
        others = [(1 - x, y), (x, 1 - y), (1 - x, 1 - y)]
        parts = [pc_ref, puq_ref, prep_ref]
        gats = [rep_all, c_all]

        def stage1(chip, a):
            return _remote(parts[a].at[2 * chip + (1 - c)], ras[a].at[chip], send_sems, recv_sems, 7 * a + chip, sibling)

        def stage2(k, a):
            cx, cy = others[k]
            return _remote(tbs[a].at[k], rbs[a].at[k], send_sems, recv_sems, 7 * a + 4 + k, (cx, cy, c))

        def gcopy(a, k, blk, to):
            slab = gats[a].at[4 * blk[0] + 2 * blk[1] + blk[2]]
            return _remote(slab, slab, gsend, grecv, 7 * a + k, to)

        def chip_sum(a, chip):
            return parts[a][2 * chip + c].astype(F32) + ras[a][chip].astype(F32)

        @pl.when(step == 0)
        def _():
            for j, s0, s1, seg, d0 in _column_runs():
                if seg == 2:
                    for r in range(N_DEV):
                        pc_ref[r, :, s0:s1] = dwc_ref[c_rows * r:c_rows * (r + 1), d0:d0 + (s1 - s0)]
            for chip in range(4):
                for a in range(n):
                    stage1(chip, a).start()

        @pl.when(step == 1)
        def _():
            for chip in range(4):
                for a in range(n):
                    stage1(chip, a).wait_recv()
            for k, (cx, cy) in enumerate(others):
                for a in range(n):
                    tbs[a][k] = chip_sum(a, 2 * cx + cy).astype(spec[a][1])
                    stage2(k, a).start()

        @pl.when(step == sums_at)
        def _():
            for k in range(3):
                for a in range(n):
                    stage2(k, a).wait_recv()
            sums = []
            for a in range(n):
                acc = chip_sum(a, 2 * x + y)
                for k in range(3):
                    acc = acc + rbs[a][k].astype(F32)
                sums.append(acc)
            c_all[me_idx] = sums[0].astype(BF16)
            guq_ref[...] = sums[1]
            rep_all[me_idx] = sums[2]
            for a in range(2):
                for j, chip in enumerate(others):
                    gcopy(a, 1 + j, me, (*chip, c)).start()
                gcopy(a, 0, me, sibling).start()

        acc = dxr_ref[...]
        for dh_ref, w_ref in zip(dh_refs, w_refs):
            acc = acc + _dot(dh_ref[...], w_ref[...], _NT)
        dx_ref[...] = acc

        @pl.when(step == ntile - 1)
        def _():
            for j, chip in enumerate(others):
                for a in range(2):
                    gcopy(a, 1 + j, (*chip, c), me).wait_recv()
                    gcopy(a, 4 + j, (*chip, c), sibling).start()
            for a in range(2):
                gcopy(a, 0, sibling, me).wait_recv()
                for j, chip in enumerate(others):
                    gcopy(a, 4 + j, (*chip, 1 - c), me).wait_recv()
            for a in range(2):
                gcopy(a, 0, me, sibling).wait_send()
                for j, chip in enumerate(others):
                    gcopy(a, 1 + j, me, (*chip, c)).wait_send()
                    gcopy(a, 4 + j, (*chip, c), sibling).wait_send()
            for a in range(n):
                for chip in range(4):
                    stage1(chip, a).wait_send()
                for k in range(3):
                    stage2(k, a).wait_send()
            call_ref[...] = c_all[...]
            repall_ref[...] = rep_all[...]

    row = lambda w: pl.BlockSpec((tm, w), lambda i: (i, 0))
    full = lambda shape: pl.BlockSpec(shape, lambda i: (0,) * len(shape))
    scratch = [pltpu.VMEM((N_DEV, c_rows, C_NAT), BF16), pltpu.VMEM((N_DEV, c_rows, C_NAT), BF16),
               pltpu.VMEM((N_DEV, rep_rows, LANES), F32)]
    for lead in (4, 3, 3):
        scratch += [pltpu.VMEM((lead,) + tuple(shape), dt) for shape, dt in spec]
    scratch += [pltpu.SemaphoreType.DMA((7 * n,)), pltpu.SemaphoreType.DMA((7 * n,)),
                pltpu.SemaphoreType.DMA((14,)), pltpu.SemaphoreType.DMA((14,))]
    return pl.pallas_call(
        body, name="dx_tail", grid=(SEQ // tm,),
        in_specs=[row(dh.shape[1]) for dh in dhs] + [full(w.shape) for w in ws]
        + [row(D_MODEL), full(dwc.shape), full(p_uq.shape), full(p_rep.shape)],
        out_specs=[row(D_MODEL), full((N_DEV, c_rows, C_NAT)), full(p_uq.shape[1:]), full((N_DEV, rep_rows, LANES))],
        out_shape=[jax.ShapeDtypeStruct((SEQ, D_MODEL), F32), jax.ShapeDtypeStruct((N_DEV, c_rows, C_NAT), BF16),
                   jax.ShapeDtypeStruct(p_uq.shape[1:], F32), jax.ShapeDtypeStruct((N_DEV, rep_rows, LANES), F32)],
        scratch_shapes=scratch,
        compiler_params=pltpu.CompilerParams(dimension_semantics=("arbitrary",), vmem_limit_bytes=VMEM_BIG),
    )(*dhs, *ws, dx_res, dwc, p_uq, p_rep)


def _sum_landed(landed, c_all):
    c_rows = D_MODEL // N_DEV

    def body(rhi_ref, rlo_ref, roa_ref, rob_ref, rout_ref, call_ref, gin_ref, goa_ref, gob_ref, gout_ref):
        def total(ref, sl):
            acc = ref[0, sl, :].astype(F32)
            for s in range(1, N_DEV):
                acc = acc + ref[s, sl, :].astype(F32)
            return acc

        x, y, c = _mesh_pos()
        dev0 = jnp.where(4 * x + 2 * y + c == 0, 1.0, 0.0)
        for j in range(N_DEV):
            sl = slice(c_rows * j, c_rows * (j + 1))
            below = c_rows * j < P_IN_SPLIT
            tot = total(rhi_ref, sl) if below else total(rlo_ref, slice(c_rows * j - P_IN_SPLIT, c_rows * (j + 1) - P_IN_SPLIT))
            gin_ref[0, sl, C_NAT:SHARD_W] = tot[:, C_NAT:SHARD_W]
            gin_ref[0, sl, 0:C_NAT] = tot[:, 0:C_NAT] + dev0 * call_ref[j].astype(F32)
        goa_ref[0] = total(roa_ref, slice(None))
        gob_ref[0] = total(rob_ref, slice(None))
        gout_ref[0] = total(rout_ref, slice(None))

    return pl.pallas_call(
        body, name="sum_landed",
        out_shape=[jax.ShapeDtypeStruct((1, D_MODEL, SHARD_W), F32)]
        + [jax.ShapeDtypeStruct((1,) + r.shape[1:], F32) for r in landed[2:]],
        compiler_params=pltpu.CompilerParams(vmem_limit_bytes=VMEM_MID),
    )(*landed, c_all)


_O_CQ, _O_CKV, _O_KPE, _O_ZA, _O_U, _O_V, _O_ZB, _O_GA, _O_GB = 0, 384, 512, 544, 1056, 1568, 2080, 2592, 3616


def _to_segments(w):
    z = lambda n: jnp.zeros(w.shape[:-1] + (n,), w.dtype)
    seg_a = jnp.concatenate([w[..., _O_GA:_O_GB], w[..., _O_GB:IN_WIDTH], w[..., _O_ZA:_O_U]], axis=-1)
    seg_b = jnp.concatenate([w[..., _O_U:_O_V], w[..., _O_V:_O_ZB], w[..., _O_ZB:_O_GA]], axis=-1)
    seg_c = jnp.concatenate([w[..., _O_CQ:_O_CKV], z(CQ_PAD - Q_LORA_RANK), w[..., _O_CKV:_O_KPE],
                             z(ROPE_LO), w[..., _O_KPE:_O_ZA], z(LANES - ROPE_HI)], axis=-1)
    return seg_a, seg_b, seg_c


def _from_segments(seg_a, seg_b, seg_c):
    kpe0 = CQ_PAD + LANES + ROPE_LO
    return jnp.concatenate([
        seg_c[..., 0:Q_LORA_RANK], seg_c[..., CQ_PAD:CQ_PAD + LANES], seg_c[..., kpe0:kpe0 + QK_ROPE_DIM],
        seg_a[..., 2 * D_MODEL:SEG_A], seg_b, seg_a[..., 0:2 * D_MODEL]], axis=-1)


def kernel(x, positions, w_in, b_in, g_q, w_uq, g_kv, w_ukv, w_oa, sgu_ln_g, sgu_ln_b, w_s, b_s, w_ob, w_out, ln_g, ln_b, loss_target, m_w_in, m_b_in, m_g_q, m_w_uq, m_g_kv, m_w_ukv, m_w_oa, m_sgu_ln_g, m_sgu_ln_b, m_w_s, m_b_s, m_w_ob, m_w_out, m_ln_g, m_ln_b, v_w_in, v_b_in, v_g_q, v_w_uq, v_g_kv, v_w_ukv, v_w_oa, v_sgu_ln_g, v_sgu_ln_b, v_w_s, v_b_s, v_w_ob, v_w_out, v_ln_g, v_ln_b):
    w_uq2 = w_uq[0].reshape(Q_LORA_RANK // N_DEV, MLA_HEADS * QK_HEAD_DIM)
    inv_freq = ROPE_THETA ** (-jnp.arange(0, QK_ROPE_DIM, 2, dtype=F32) / QK_ROPE_DIM)
    invf_lane = jnp.concatenate([jnp.zeros((ROPE_LO,), F32), inv_freq, inv_freq,
                                 jnp.zeros((LANES - ROPE_HI,), F32)]).reshape(1, LANES)
    first = _gather_first(w_in, w_uq2, w_oa, w_ob, w_out, x[0], positions.reshape(SEQ, 1), invf_lane)
    partials = _local_step(x[0], loss_target[0], first, b_in, g_q, g_kv, w_ukv, sgu_ln_g, sgu_ln_b, w_s, b_s, ln_g, ln_b)
    weights = dict(w_in=w_in, b_in=b_in, g_q=g_q, w_uq=w_uq, g_kv=g_kv, w_ukv=w_ukv, w_oa=w_oa, sgu_ln_g=sgu_ln_g,
                   sgu_ln_b=sgu_ln_b, w_s=w_s, b_s=b_s, w_ob=w_ob, w_out=w_out, ln_g=ln_g, ln_b=ln_b)
    moms = dict(w_in=m_w_in, b_in=m_b_in, g_q=m_g_q, w_uq=m_w_uq, g_kv=m_g_kv, w_ukv=m_w_ukv, w_oa=m_w_oa,
                sgu_ln_g=m_sgu_ln_g, sgu_ln_b=m_sgu_ln_b, w_s=m_w_s, b_s=m_b_s, w_ob=m_w_ob, w_out=m_w_out,
                ln_g=m_ln_g, ln_b=m_ln_b)
    vars_ = dict(w_in=v_w_in, b_in=v_b_in, g_q=v_g_q, w_uq=v_w_uq, g_kv=v_g_kv, w_ukv=v_w_ukv, w_oa=v_w_oa,
                 sgu_ln_g=v_sgu_ln_g, sgu_ln_b=v_sgu_ln_b, w_s=v_w_s, b_s=v_b_s, w_ob=v_w_ob, w_out=v_w_out,
                 ln_g=v_ln_g, ln_b=v_ln_b)
    return _reduce_and_update(partials, weights, moms, vars_)


def _local_step(x2, tgt, first, b_in, g_q, g_kv, w_ukv, sgu_ln_g, sgu_ln_b, w_s, b_s, ln_g, ln_b):
    wc, wq, win_b, oa_b, ob_b, out_b, x_bf, xt_bf, c_t, sa_t, sb_t = first
    ba, bb, bc = _to_segments(b_in)
    w_ukv_bf = w_ukv[0].astype(BF16)
    wkn = jnp.pad(w_ukv_bf[:, :, :QK_NOPE_DIM], ((0, 0), (0, 0), (0, HEAD_PAD - QK_NOPE_DIM))).reshape(KV_LORA_RANK, -1)
    wv = jnp.pad(w_ukv_bf[:, :, QK_NOPE_DIM:], ((0, 0), (0, 0), (0, HEAD_PAD - V_HEAD_DIM))).reshape(KV_LORA_RANK, -1)
    gq = jnp.pad(g_q, ((0, 0), (0, CQ_PAD - Q_LORA_RANK)))
    bias_full = jnp.repeat(b_s[0].T, SGU_GROUP_DIM, axis=1)
    w_s3 = w_s[0]
    w_st3 = jnp.swapaxes(w_s3, 1, 2)

    h_c = _mm(x_bf, wc, bias=bc, tm=512, tn=SEG_C, name="in_proj_c")
    q, k, kt, vx, vxt = _mla_prep(h_c, gq, g_kv, wq, wkn, wv, c_t, sa_t, sb_t)
    o, lse, (g_in,) = _attn_fwd(q, kt, vx, (win_b,))
    wa, wb = _assemble_in(g_in)
    h_a, (g_out,) = _mm(x_bf, wa, bias=ba, own=(out_b,), tm=512, tn=SEG_A // 2, name="in_proj_a")
    h_b, (g_oa, g_ob) = _mm(x_bf, wb, bias=bb, own=(oa_b, ob_b), tm=512, tn=SEG_B // 2, name="in_proj_b")
    y_b = _sgu_fwd(h_b, sgu_ln_g, sgu_ln_b, w_s3, bias_full)
    w_oa_f, w_ob_f, w_out_f = _assemble_out(g_oa, g_ob, g_out)

    (loss_row, dx_res, dh_a, d_o, d_yb, p_oa, p_ob, p_out, d_lng, d_lnb, d_ba) = _merge(
        x2, o, h_a, y_b, tgt, w_oa_f, w_ob_f, w_out_f, ln_g, ln_b)
    (dh_b, d_ws, d_bs_t, d_slg, d_slb, d_bb), (r_out,) = _sgu_bwd(h_b, d_yb, sgu_ln_g, sgu_ln_b, w_s3, w_st3, bias_full,
                                                                 (p_out,))
    d_wa, (r_oa,) = _mm(xt_bf, dh_a, out_dtype=BF16, parts=(p_oa,), tm=512, tn=512, name="dw_in_a")
    d_wb = _mm(xt_bf, dh_b, out_dtype=BF16, tm=512, tn=512, name="dw_in_b")
    p_hi, p_lo = _to_parts(d_wa, d_wb)
    dq, dk, dv, (r_hi,) = _attn_bwd(q, kt, k, vxt, d_o, o, lse, (p_hi,))
    (dh_c, p_uq, d_wkn, d_wv, d_gq, d_gkv, d_bc, d_wc), (r_lo, r_ob) = _mla_bwd(
        dq, dk, dv, h_c, xt_bf, gq, g_kv, wq, wkn, wv, c_t, sa_t, sb_t, (p_lo, p_ob))
    landed = (r_hi, r_lo, r_oa, r_ob, r_out)

    p_b_in = _from_segments(d_ba, d_bb, d_bc)
    p_w_ukv = jnp.concatenate([d_wkn.reshape(KV_LORA_RANK, MLA_HEADS, HEAD_PAD)[:, :, :QK_NOPE_DIM],
                               d_wv.reshape(KV_LORA_RANK, MLA_HEADS, HEAD_PAD)[:, :, :V_HEAD_DIM]], axis=-1)
    p_g_q = d_gq[:, :Q_LORA_RANK]
    p_b_s = d_bs_t[:, :SGU_GROUPS].T
    replicated = [p_b_in, p_g_q, d_gkv, p_w_ukv, d_slg, d_slb, d_ws, p_b_s, d_lng, d_lnb]
    return loss_row, ((dh_a, dh_b, dh_c), (wa, wb, wc), dx_res), landed, d_wc, p_uq, replicated


_NAMES = ["w_in", "b_in", "g_q", "w_uq", "g_kv", "w_ukv", "w_oa", "sgu_ln_g", "sgu_ln_b", "w_s", "b_s", "w_ob",
          "w_out", "ln_g", "ln_b"]
_REPLICATED = ["b_in", "g_q", "g_kv", "w_ukv", "sgu_ln_g", "sgu_ln_b", "w_s", "b_s", "ln_g", "ln_b"]


def _reduce_and_update(partials, weights, moms, vars_):
    loss_row, (dhs, ws, dx_res), landed, d_wc, p_uq, replicated = partials
    def piece(a):
        flat = a.reshape(-1)
        return jnp.pad(flat, (0, -flat.size % PACK_ALIGN))

    rep_flat = jnp.concatenate([piece(a) for a in replicated] + [piece(loss_row[0, :1])])
    rep_flat = jnp.pad(rep_flat, (0, N_DEV * PACK_R_ROWS * LANES - rep_flat.size))
    dx, c_all, g_uq, rep_all = _dx_tail(dhs, ws, dx_res, d_wc, p_uq, rep_flat.reshape(N_DEV, PACK_R_ROWS, LANES))
    g_in, g_oa, g_ob, g_out = _sum_landed(landed, c_all)
    rep_sum = rep_all.reshape(-1)
    grads, pos = dict(w_in=g_in, w_uq=g_uq, w_oa=g_oa, w_ob=g_ob, w_out=g_out), 0
    for nm in _REPLICATED:
        grads[nm] = rep_sum[pos:pos + weights[nm].size]
        pos += weights[nm].size + -weights[nm].size % PACK_ALIGN
    loss = rep_sum[pos]
    grads = {nm: grads[nm].reshape(weights[nm].shape) for nm in _NAMES}
    deltas, new_m, new_v = _adamw_all([weights[nm] for nm in _NAMES], [grads[nm] for nm in _NAMES],
                                      [moms[nm] for nm in _NAMES], [vars_[nm] for nm in _NAMES])
    return (loss, dx.reshape(1, SEQ, D_MODEL), *[grads[nm] for nm in _NAMES], *deltas, *new_m, *new_v)
```

```python
import math

import jax
import jax.numpy as jnp
from jax import lax
from jax.experimental import pallas as pl
from jax.experimental.pallas import tpu as pltpu

F32 = jnp.float32
BF16 = jnp.bfloat16

D_MODEL = 1024
SEQ = 2048
N_DEV = 8
MLA_HEADS = 8
Q_LORA_RANK = 384
KV_LORA_RANK = 128
QK_NOPE_DIM = 64
QK_ROPE_DIM = 32
V_HEAD_DIM = 64
QK_HEAD_DIM = QK_NOPE_DIM + QK_ROPE_DIM
MLA_WIDTH = MLA_HEADS * V_HEAD_DIM
ROPE_THETA = 10000.0
SGU_GROUPS = 8
SGU_GROUP_DIM = 64
SGU_WIDTH = SGU_GROUPS * SGU_GROUP_DIM
CHUNK = 128
RMS_EPS = 1e-6
LN_EPS = 1e-5
DN_ALPHA = 2.0 ** 0.25
IN_WIDTH = 4640
ATTN_SCALE = QK_HEAD_DIM ** -0.5

ADAM_LR = 0.001
ADAM_B1 = 0.9
ADAM_B2 = 0.999
ADAM_EPS = 1e-08
ADAM_WD = 0.01
ADAM_STEP = 10

LANES = 128
HEAD_PAD = 128
ROPE_LO = QK_NOPE_DIM
ROPE_MID = ROPE_LO + QK_ROPE_DIM // 2
ROPE_HI = ROPE_LO + QK_ROPE_DIM
CQ_PAD = 512

SEG_A = 2560
SEG_B = 1536
SEG_C = 768

PACK_R_ROWS = 272
PACK_ALIGN = 8 * LANES
VMEM_BIG = 56 * 1024 * 1024
VMEM_MID = 40 * 1024 * 1024


def _sigmoid(x):
    return 1.0 / (1.0 + jnp.exp(-x))


def _gelu_and_grad(x):
    c0 = math.sqrt(2.0 / math.pi)
    x2 = x * x
    t = jnp.tanh(c0 * (x + 0.044715 * x * x2))
    g = 0.5 * x * (1.0 + t)
    dg = 0.5 * (1.0 + t) + 0.5 * x * (1.0 - t * t) * (c0 * (1.0 + 3.0 * 0.044715 * x2))
    return g, dg


def _dot(a, b, dims):
    return lax.dot_general(a, b, (dims, ((), ())), preferred_element_type=F32)


_NN = ((1,), (0,))
_NT = ((1,), (1,))
_TN = ((0,), (0,))


def _store_grad(dh_ref, db_ref, col, val):
    cols = slice(col, col + val.shape[1])
    dh_ref[:, cols] = val.astype(BF16)
    db_ref[:, cols] += jnp.sum(val, axis=0, keepdims=True)


def _mm(a, b, *, tb=False, bias=None, add=None, out_dtype=F32, own=(), parts=(), tm, tn, name):
    m, k = a.shape
    n = b.shape[0] if tb else b.shape[1]
    assert m % tm == 0 and n % tn == 0 and not (own and parts)
    dims = _NT if tb else _NN
    nown = len(own) + len(parts)
    nm = m // tm
    nsteps = (n // tn) * nm

    def body(*refs):
        a_ref, b_ref = refs[0], refs[1]
        pos = 2
        r = _dot(a_ref[...], b_ref[...], dims)
        if bias is not None:
            r = r + refs[pos][...]; pos += 1
        if add is not None:
            r = r + refs[pos][...]; pos += 1
        own_refs = refs[pos:pos + nown]; pos += nown
        refs[pos][...] = r.astype(out_dtype)
        if nown:
            gat_refs = refs[pos + 1:pos + 1 + nown]
            send_sems, recv_sems, local_sems = refs[pos + 1 + nown:]
            step = pl.program_id(0) * nm + pl.program_id(1)
            if own:
                _gather_behind(own_refs, gat_refs, send_sems, recv_sems, local_sems, step, nsteps // 2, nsteps - 1)
            else:
                exchange = _exchange_parts(own_refs, gat_refs, send_sems, recv_sems, local_sems)
                _exchange_start(step == 0, exchange)
                _exchange_finish(step == nsteps - 1, exchange)

    b_spec = pl.BlockSpec((tn, k), lambda j, i: (j, 0)) if tb else pl.BlockSpec((k, tn), lambda j, i: (0, j))
    in_specs, args = [pl.BlockSpec((tm, k), lambda j, i: (i, 0)), b_spec], [a, b]
    if bias is not None:
        in_specs.append(pl.BlockSpec((1, tn), lambda j, i: (0, j))); args.append(bias)
    if add is not None:
        in_specs.append(pl.BlockSpec((tm, tn), lambda j, i: (i, j))); args.append(add)
    hbm = pl.BlockSpec(memory_space=pl.ANY)
    res = pl.pallas_call(
        body, name=name, grid=(n // tn, nm), in_specs=in_specs + [hbm] * nown,
        out_specs=[pl.BlockSpec((tm, tn), lambda j, i: (i, j))] + [hbm] * nown,
        out_shape=[jax.ShapeDtypeStruct((m, n), out_dtype)]
        + [jax.ShapeDtypeStruct((N_DEV,) + o.shape, o.dtype) for o in own]
        + [jax.ShapeDtypeStruct(p.shape, p.dtype) for p in parts],
        scratch_shapes=_exchange_sems(nown) if nown else [],
        compiler_params=pltpu.CompilerParams(dimension_semantics=("arbitrary", "arbitrary"), vmem_limit_bytes=VMEM_BIG),
    )(*args, *own, *parts)
    return (res[0], res[1:]) if nown else res[0]


def _rope(x, c, sa, sb):
    return x * c + pltpu.roll(x, LANES - 16, 1) * sa + pltpu.roll(x, 16, 1) * sb


def _rope_t(dy, c, sa, sb):
    return dy * c + pltpu.roll(dy * sa, 16, 1) + pltpu.roll(dy * sb, LANES - 16, 1)


def _mla_prep(h_c, gq, gkv, wq, wkn, wvx, c_t, sa_t, sb_t):
    tm = 256
    hw = MLA_HEADS * HEAD_PAD

    def body(cq_ref, ckv_ref, kpe_ref, gq_ref, gkv_ref, wq_ref, wkn_ref, wvx_ref, c_ref, sa_ref, sb_ref,
             q_ref, k_ref, kt_ref, vx_ref, vxt_ref):
        c, sa, sb = c_ref[...], sa_ref[...], sb_ref[...]
        cq = cq_ref[...]
        rq = lax.rsqrt(jnp.sum(cq * cq, axis=1, keepdims=True) * (1.0 / Q_LORA_RANK) + RMS_EPS)
        cqn = ((cq * rq) * gq_ref[...]).astype(BF16)
        qall = _dot(cqn, wq_ref[...], _NN)
        for h in range(MLA_HEADS):
            sl = slice(HEAD_PAD * h, HEAD_PAD * (h + 1))
            q_ref[:, sl] = (_rope(qall[:, sl], c, sa, sb) * ATTN_SCALE).astype(BF16)
        ckv = ckv_ref[...]
        rkv = lax.rsqrt(jnp.sum(ckv * ckv, axis=1, keepdims=True) * (1.0 / KV_LORA_RANK) + RMS_EPS)
        ckvn = ((ckv * rkv) * gkv_ref[...]).astype(BF16)
        knall = _dot(ckvn, wkn_ref[...], _NN)
        vall = _dot(ckvn, wvx_ref[...], _NN)
        kper = _rope(kpe_ref[...], c, sa, sb)
        ones_half = (lax.broadcasted_iota(jnp.int32, (tm, HEAD_PAD), 1) >= V_HEAD_DIM).astype(F32)
        for h in range(MLA_HEADS):
            sl = slice(HEAD_PAD * h, HEAD_PAD * (h + 1))
            kh = knall[:, sl] + kper
            vh = vall[:, sl] + ones_half
            k_ref[:, sl] = kh.astype(BF16)
            kt_ref[sl, :] = kh.T.astype(BF16)
            vx_ref[:, sl] = vh.astype(BF16)
            vxt_ref[sl, :] = vh.T.astype(BF16)

    full = lambda shape: pl.BlockSpec(shape, lambda i: (0, 0))
    tab = pl.BlockSpec((tm, LANES), lambda i: (i, 0))
    row = pl.BlockSpec((tm, hw), lambda i: (i, 0))
    col = pl.BlockSpec((hw, tm), lambda i: (0, i))
    return pl.pallas_call(
        body, name="mla_prep", grid=(SEQ // tm,),
        in_specs=[pl.BlockSpec((tm, CQ_PAD), lambda i: (i, 0)),
                  pl.BlockSpec((tm, LANES), lambda i: (i, CQ_PAD // LANES)),
                  pl.BlockSpec((tm, LANES), lambda i: (i, CQ_PAD // LANES + 1)),
                  full((1, CQ_PAD)), full((1, KV_LORA_RANK)),
                  full((CQ_PAD, hw)), full((KV_LORA_RANK, hw)), full((KV_LORA_RANK, hw)), tab, tab, tab],
        out_specs=[row, row, col, row, col],
        out_shape=[jax.ShapeDtypeStruct((SEQ, hw), BF16), jax.ShapeDtypeStruct((SEQ, hw), BF16),
                   jax.ShapeDtypeStruct((hw, SEQ), BF16), jax.ShapeDtypeStruct((SEQ, hw), BF16),
                   jax.ShapeDtypeStruct((hw, SEQ), BF16)],
        compiler_params=pltpu.CompilerParams(dimension_semantics=("arbitrary",), vmem_limit_bytes=VMEM_MID),
    )(h_c, h_c, h_c, gq, gkv, wq, wkn, wvx, c_t, sa_t, sb_t)


ATT_T = 512
ATT_STRIP = 64


def _attn_fwd(q, kt, vx, own):
    t, rs = ATT_T, ATT_STRIP
    nown = len(own)
    nq = SEQ // t
    nsteps = (MLA_HEADS // 2) * nq

    def body(q_ref, kt_ref, vx_ref, *rest):
        own_refs, (o_ref, l_ref), gat_refs = rest[:nown], rest[nown:nown + 2], rest[nown + 2:2 * nown + 2]
        s_scr, p_scr, m_scr, a_scr, acc_scr, send_sems, recv_sems, local_sems = rest[2 * nown + 2:]
        qi = pl.program_id(1)
        lane = lax.broadcasted_iota(jnp.int32, (t, LANES), 1)
        m_scr[...] = jnp.full((2, t, LANES), -1e30, F32)
        acc_scr[...] = jnp.zeros((2, t, LANES), F32)

        def block(j, masked):
            off = pl.multiple_of(j * t, t)
            for a in range(2):
                sl = slice(HEAD_PAD * a, HEAD_PAD * (a + 1))
                s_scr[a] = _dot(q_ref[:, sl], kt_ref[sl, pl.ds(off, t)], _NN)
                for r in range(t // rs):
                    rows = slice(rs * r, rs * (r + 1))
                    s = s_scr[a, rows, :]
                    if masked:
                        rowi = lax.broadcasted_iota(jnp.int32, (rs, t), 0) + rs * r
                        coli = lax.broadcasted_iota(jnp.int32, (rs, t), 1)
                        s = jnp.where(coli <= rowi, s, -1e30)
                    m_old = m_scr[a, rows, :]
                    m_new = jnp.maximum(m_old, jnp.max(s, axis=1, keepdims=True))
                    p_scr[a, rows, :] = jnp.exp(s - m_new[:, :1]).astype(BF16)
                    a_scr[a, rows, :] = jnp.exp(m_old - m_new)
                    m_scr[a, rows, :] = m_new
                acc_scr[a] = acc_scr[a] * a_scr[a] + _dot(p_scr[a], vx_ref[pl.ds(off, t), sl], _NN)

        def step(j, carry):
            block(j, False)
            return carry
        lax.fori_loop(0, qi, step, 0)
        block(qi, True)
        res = []
        for a in range(2):
            acc = acc_scr[a]
            l = acc[:, V_HEAD_DIM:V_HEAD_DIM + 1]
            res.append((acc / l, m_scr[a] + jnp.log(l)))
        o_ref[...] = jnp.where(lane < V_HEAD_DIM, res[0][0], pltpu.roll(res[1][0], V_HEAD_DIM, 1))
        l_ref[...] = jnp.where(lane < V_HEAD_DIM, res[0][1], res[1][1])
        _gather_behind(own_refs, gat_refs, send_sems, recv_sems, local_sems, pl.program_id(0) * nq + qi,
                       nsteps - 2, nsteps - 1)

    hbm = pl.BlockSpec(memory_space=pl.ANY)
    res = pl.pallas_call(
        body, name="attn_fwd", grid=(MLA_HEADS // 2, nq),
        in_specs=[pl.BlockSpec((t, 2 * HEAD_PAD), lambda p, i: (i, p)),
                  pl.BlockSpec((2 * HEAD_PAD, SEQ), lambda p, i: (p, 0)),
                  pl.BlockSpec((SEQ, 2 * HEAD_PAD), lambda p, i: (0, p))] + [hbm] * nown,
        out_specs=[pl.BlockSpec((t, LANES), lambda p, i: (i, p)),
                   pl.BlockSpec((t, LANES), lambda p, i: (i, p))] + [hbm] * nown,
        out_shape=[jax.ShapeDtypeStruct((SEQ, MLA_WIDTH), F32), jax.ShapeDtypeStruct((SEQ, MLA_WIDTH), F32)]
        + [jax.ShapeDtypeStruct((N_DEV,) + a.shape, a.dtype) for a in own],
        scratch_shapes=[pltpu.VMEM((2, t, t), F32), pltpu.VMEM((2, t, t), BF16), pltpu.VMEM((2, t, LANES), F32),
                        pltpu.VMEM((2, t, LANES), F32), pltpu.VMEM((2, t, LANES), F32)] + _exchange_sems(nown),
        compiler_params=pltpu.CompilerParams(dimension_semantics=("arbitrary", "arbitrary"), vmem_limit_bytes=VMEM_MID),
    )(q, kt, vx, *own)
    return res[0], res[1], res[2:]


def _exchange_parts(parts, lands, send_sems, recv_sems, local_sems):
    x, y, c = _mesh_pos()
    me = 4 * x + 2 * y + c
    peers = [(x, y, 1 - c), (1 - x, y, c), (x, 1 - y, c), (1 - x, 1 - y, c),
             (1 - x, y, 1 - c), (x, 1 - y, 1 - c), (1 - x, 1 - y, 1 - c)]
    remote, local = [], []
    for a, (part, land) in enumerate(zip(parts, lands)):
        for k, peer in enumerate(peers):
            t = 4 * peer[0] + 2 * peer[1] + peer[2]
            remote.append(_remote(part.at[t], land.at[me], send_sems, recv_sems, 7 * a + k, peer))
        local.append(pltpu.make_async_copy(part.at[me], land.at[me], local_sems.at[a]))
    return remote, local


def _exchange_start(first_step, exchange):
    remote, local = exchange

    @pl.when(first_step)
    def _():
        for cp in remote + local:
            cp.start()


def _exchange_finish(last_step, exchange):
    remote, local = exchange

    @pl.when(last_step)
    def _():
        for cp in remote:
            cp.wait_recv()
        for cp in remote:
            cp.wait_send()
        for cp in local:
            cp.wait()


def _exchange_sems(npart):
    return [pltpu.SemaphoreType.DMA((7 * npart,)), pltpu.SemaphoreType.DMA((7 * npart,)),
            pltpu.SemaphoreType.DMA((npart,))]


def _attn_bwd(q, kt, k, vxt, d_o, o, lse, parts):
    t, rs = ATT_T, ATT_STRIP
    nq = SEQ // t
    npart = len(parts)
    nsteps = MLA_HEADS // 2

    def body(q_ref, kt_ref, k_ref, vxt_ref, do_ref, o_ref, l_ref, *rest):
        part_refs, rest = rest[:npart], rest[npart:]
        dq_ref, dk_ref, dv_ref = rest[:3]
        land_refs, rest = rest[3:3 + npart], rest[3 + npart:]
        s_scr, dp_scr, p_scr, ds_scr, st_scr, send_sems, recv_sems, local_sems = rest
        exchange = _exchange_parts(part_refs, land_refs, send_sems, recv_sems, local_sems)
        _exchange_start(pl.program_id(0) == 0, exchange)
        dk_ref[...] = jnp.zeros_like(dk_ref)
        dv_ref[...] = jnp.zeros_like(dv_ref)
        lane = lax.broadcasted_iota(jnp.int32, (t, LANES), 1)

        def qtile(i, carry):
            ioff = pl.multiple_of(i * t, t)
            do_i = do_ref[pl.ds(ioff, t), :]
            o_i = o_ref[pl.ds(ioff, t), :]
            l_i = l_ref[pl.ds(ioff, t), :]
            for a in range(2):
                sl = slice(HEAD_PAD * a, HEAD_PAD * (a + 1))
                sel = (lane < V_HEAD_DIM) if a == 0 else (lane >= V_HEAD_DIM)
                doa = jnp.where(sel, do_i, 0.0)
                oa = o_i
                if a == 1:
                    doa = pltpu.roll(doa, V_HEAD_DIM, 1)
                    oa = pltpu.roll(o_i, V_HEAD_DIM, 1)
                st_scr[0] = jnp.broadcast_to(jnp.sum(doa * oa, axis=1, keepdims=True), (t, LANES))
                st_scr[1] = jnp.broadcast_to(l_i[:, V_HEAD_DIM * a:V_HEAD_DIM * a + 1], (t, LANES))
                doa_bf = doa.astype(BF16)
                qa = q_ref[pl.ds(ioff, t), sl]

                def block(j, masked, dq_acc, sl=sl, qa=qa, doa_bf=doa_bf):
                    joff = pl.multiple_of(j * t, t)
                    s_scr[...] = _dot(qa, kt_ref[sl, pl.ds(joff, t)], _NN)
                    dp_scr[...] = _dot(doa_bf, vxt_ref[sl, pl.ds(joff, t)], _NN)
                    for r in range(t // rs):
                        rows = slice(rs * r, rs * (r + 1))
                        p = jnp.exp(s_scr[rows, :] - st_scr[1, rows, :1])
                        if masked:
                            rowi = lax.broadcasted_iota(jnp.int32, (rs, t), 0) + rs * r
                            coli = lax.broadcasted_iota(jnp.int32, (rs, t), 1)
                            p = jnp.where(coli <= rowi, p, 0.0)
                        p_scr[rows, :] = p.astype(BF16)
                        ds_scr[rows, :] = (p * (dp_scr[rows, :] - st_scr[0, rows, :1])).astype(BF16)
                    dk_ref[pl.ds(joff, t), sl] += _dot(ds_scr[...], qa, _TN)
                    dv_ref[pl.ds(joff, t), sl] += _dot(p_scr[...], doa_bf, _TN)
                    return dq_acc + _dot(ds_scr[...], k_ref[pl.ds(joff, t), sl], _NN)

                dq_acc = lax.fori_loop(0, i, lambda j, acc: block(j, False, acc), jnp.zeros((t, HEAD_PAD), F32))
                dq_ref[pl.ds(ioff, t), sl] = block(i, True, dq_acc)
            return carry

        lax.fori_loop(0, nq, qtile, 0)
        _exchange_finish(pl.program_id(0) == nsteps - 1, exchange)

    hw = MLA_HEADS * HEAD_PAD
    wide = pl.BlockSpec((SEQ, 2 * HEAD_PAD), lambda p: (0, p))
    wide_t = pl.BlockSpec((2 * HEAD_PAD, SEQ), lambda p: (p, 0))
    narrow = pl.BlockSpec((SEQ, LANES), lambda p: (0, p))
    hbm = pl.BlockSpec(memory_space=pl.ANY)
    res = pl.pallas_call(
        body, name="attn_bwd", grid=(nsteps,),
        in_specs=[wide, wide_t, wide, wide_t, narrow, narrow, narrow] + [hbm] * npart,
        out_specs=[wide, wide, wide] + [hbm] * npart,
        out_shape=[jax.ShapeDtypeStruct((SEQ, hw), F32)] * 3 + [jax.ShapeDtypeStruct(p.shape, p.dtype) for p in parts],
        scratch_shapes=[pltpu.VMEM((t, t), F32), pltpu.VMEM((t, t), F32), pltpu.VMEM((t, t), BF16),
                        pltpu.VMEM((t, t), BF16), pltpu.VMEM((2, t, LANES), F32)] + _exchange_sems(npart),
        compiler_params=pltpu.CompilerParams(dimension_semantics=("arbitrary",), vmem_limit_bytes=VMEM_BIG),
    )(q, kt, k, vxt, d_o, o, lse, *parts)
    return res[0], res[1], res[2], res[3:]


def _sgu_math(u, v, zb, lg, lb, ws_ref, bias):
    ug, dug = _gelu_and_grad(u)
    vg, dvg = _gelu_and_grad(v)
    mu = jnp.mean(vg, axis=1, keepdims=True)
    xc = vg - mu
    rstd = lax.rsqrt(jnp.mean(xc * xc, axis=1, keepdims=True) + LN_EPS)
    xh = xc * rstd
    vn_bf = (xh * lg + lb).astype(BF16)
    grp = lax.broadcasted_iota(jnp.int32, (CHUNK, SGU_WIDTH), 1) // SGU_GROUP_DIM
    r_i = lax.broadcasted_iota(jnp.int32, (CHUNK, CHUNK), 0)
    c_i = lax.broadcasted_iota(jnp.int32, (CHUNK, CHUNK), 1)
    tri, tri_t = r_i >= c_i, r_i <= c_i
    mixed = bias
    for g in range(SGU_GROUPS):
        wt = jnp.where(tri, ws_ref[g], 0.0).astype(BF16)
        mixed = mixed + jnp.where(grp == g, _dot(wt, vn_bf, _NN), 0.0)
    sb = _sigmoid(zb)
    return ug, dug, dvg, rstd, xh, vn_bf, grp, tri, tri_t, mixed, sb


def _sgu_fwd(h_b, lg, lb, w_s, bias_full):
    def body(u_ref, v_ref, zb_ref, lg_ref, lb_ref, ws_ref, bias_ref, yb_ref):
        zb = zb_ref[...]
        ug, _, _, _, _, _, _, _, _, mixed, sb = _sgu_math(u_ref[...], v_ref[...], zb, lg_ref[...], lb_ref[...],
                                                       ws_ref, bias_ref[...])
        yb_ref[...] = (ug * mixed) * (zb * sb)

    blk = lambda c: pl.BlockSpec((CHUNK, SGU_WIDTH), lambda i, c=c: (i, c))
    full2 = lambda shape: pl.BlockSpec(shape, lambda i: (0, 0))
    return pl.pallas_call(
        body, name="sgu_fwd", grid=(SEQ // CHUNK,),
        in_specs=[blk(0), blk(1), blk(2), full2((1, SGU_WIDTH)), full2((1, SGU_WIDTH)),
                  pl.BlockSpec((SGU_GROUPS, CHUNK, CHUNK), lambda i: (0, 0, 0)), full2((CHUNK, SGU_WIDTH))],
        out_specs=pl.BlockSpec((CHUNK, SGU_WIDTH), lambda i: (i, 0)),
        out_shape=jax.ShapeDtypeStruct((SEQ, SGU_WIDTH), F32),
        compiler_params=pltpu.CompilerParams(dimension_semantics=("arbitrary",)),
    )(h_b, h_b, h_b, lg, lb, w_s, bias_full)


def _sgu_bwd(h_b, d_yb, lg, lb, w_s, w_st, bias_full, parts):
    nsteps = SEQ // CHUNK
    npart = len(parts)

    def body(u_ref, v_ref, zb_ref, dyb_ref, lg_ref, lb_ref, ws_ref, wst_ref, bias_ref, *rest):
        part_refs, rest = rest[:npart], rest[npart:]
        dhb_ref, dws_ref, dbs_ref, dlg_ref, dlb_ref, dbb_ref = rest[:6]
        land_refs, (dbias_acc, send_sems, recv_sems, local_sems) = rest[6:6 + npart], rest[6 + npart:]
        step = pl.program_id(0)
        exchange = _exchange_parts(part_refs, land_refs, send_sems, recv_sems, local_sems)
        _exchange_start(step == 0, exchange)

        @pl.when(step == 0)
        def _():
            dbb_ref[...] = jnp.zeros_like(dbb_ref)
            dws_ref[...] = jnp.zeros_like(dws_ref)
            dlg_ref[...] = jnp.zeros_like(dlg_ref)
            dlb_ref[...] = jnp.zeros_like(dlb_ref)
            dbias_acc[...] = jnp.zeros_like(dbias_acc)

        zb = zb_ref[...]
        lg = lg_ref[...]
        ug, dug, dvg, rstd, xh, vn_bf, grp, tri, tri_t, mixed, sb = _sgu_math(
            u_ref[...], v_ref[...], zb, lg, lb_ref[...], ws_ref, bias_ref[...])
        dyb = dyb_ref[...]
        dsgu = dyb * (zb * sb)
        dzb = dyb * (ug * mixed) * (sb * (1.0 + zb * (1.0 - sb)))
        du = dsgu * mixed * dug
        dmixed = dsgu * ug
        dbias_acc[...] += dmixed
        dvn = jnp.zeros((CHUNK, SGU_WIDTH), F32)
        for g in range(SGU_GROUPS):
            dm_g = jnp.where(grp == g, dmixed, 0.0).astype(BF16)
            wtt = jnp.where(tri_t, wst_ref[g], 0.0).astype(BF16)
            dvn = dvn + _dot(wtt, dm_g, _NN)
            dws_ref[g] += jnp.where(tri, _dot(dm_g, vn_bf, _NT), 0.0)
        dlg_ref[...] += jnp.sum(dvn * xh, axis=0, keepdims=True)
        dlb_ref[...] += jnp.sum(dvn, axis=0, keepdims=True)
        dxh = dvn * lg
        dvgel = rstd * (dxh - jnp.mean(dxh, axis=1, keepdims=True) - xh * jnp.mean(dxh * xh, axis=1, keepdims=True))
        _store_grad(dhb_ref, dbb_ref, 0, du)
        _store_grad(dhb_ref, dbb_ref, SGU_WIDTH, dvgel * dvg)
        _store_grad(dhb_ref, dbb_ref, 2 * SGU_WIDTH, dzb)

        @pl.when(step == nsteps - 1)
        def _():
            acc = dbias_acc[...]
            lane = lax.broadcasted_iota(jnp.int32, (CHUNK, LANES), 1)
            out = jnp.zeros((CHUNK, LANES), F32)
            for g in range(SGU_GROUPS):
                sg = jnp.sum(jnp.where(grp == g, acc, 0.0), axis=1, keepdims=True)
                out = jnp.where(lane == g, sg, out)
            dbs_ref[...] = out

        _exchange_finish(step == nsteps - 1, exchange)

    blk = lambda c: pl.BlockSpec((CHUNK, SGU_WIDTH), lambda i, c=c: (i, c))
    full2 = lambda shape: pl.BlockSpec(shape, lambda i: (0, 0))
    full3 = pl.BlockSpec((SGU_GROUPS, CHUNK, CHUNK), lambda i: (0, 0, 0))
    hbm = pl.BlockSpec(memory_space=pl.ANY)
    res = pl.pallas_call(
        body, name="sgu_bwd", grid=(nsteps,),
        in_specs=[blk(0), blk(1), blk(2), pl.BlockSpec((CHUNK, SGU_WIDTH), lambda i: (i, 0)),
                  full2((1, SGU_WIDTH)), full2((1, SGU_WIDTH)), full3, full3, full2((CHUNK, SGU_WIDTH))] + [hbm] * npart,
        out_specs=[pl.BlockSpec((CHUNK, SEG_B), lambda i: (i, 0)), full3, full2((CHUNK, LANES)),
                   full2((1, SGU_WIDTH)), full2((1, SGU_WIDTH)), full2((1, SEG_B))] + [hbm] * npart,
        out_shape=[jax.ShapeDtypeStruct((SEQ, SEG_B), BF16),
                   jax.ShapeDtypeStruct((SGU_GROUPS, CHUNK, CHUNK), F32),
                   jax.ShapeDtypeStruct((CHUNK, LANES), F32),
                   jax.ShapeDtypeStruct((1, SGU_WIDTH), F32), jax.ShapeDtypeStruct((1, SGU_WIDTH), F32),
                   jax.ShapeDtypeStruct((1, SEG_B), F32)] + [jax.ShapeDtypeStruct(p.shape, p.dtype) for p in parts],
        scratch_shapes=[pltpu.VMEM((CHUNK, SGU_WIDTH), F32)] + _exchange_sems(npart),
        compiler_params=pltpu.CompilerParams(dimension_semantics=("arbitrary",)),
    )(h_b, h_b, h_b, d_yb, lg, lb, w_s, w_st, bias_full, *parts)
    return res[:6], res[6:]


def _merge(x, o, h_a, y_b, target, w_oa, w_ob, w_out, ln_g, ln_b):
    tm = 256
    nsteps = SEQ // tm

    def body(x_ref, o_ref, ga_ref, gb_ref, za_ref, yb_ref, tgt_ref, woa_ref, wob_ref, wout_ref, lng_ref, lnb_ref,
             loss_ref, dxr_ref, dha_ref, do_ref, dyb_ref, poa_ref, pob_ref, pout_ref, dlng_ref, dlnb_ref, dba_ref,
             dwoa_ref, dwob_ref, dwout_ref):
        step = pl.program_id(0)

        @pl.when(step == 0)
        def _():
            for r in (loss_ref, dwoa_ref, dwob_ref, dwout_ref, dlng_ref, dlnb_ref, dba_ref):
                r[...] = jnp.zeros_like(r)

        o = o_ref[...]
        za = za_ref[...]
        sa = _sigmoid(za)
        ya_bf = (o * (za * sa)).astype(BF16)
        yb_bf = yb_ref[...].astype(BF16)
        woa, wob, wout = woa_ref[...], wob_ref[...], wout_ref[...]
        pa = _dot(ya_bf, woa, _NN)
        pb = _dot(yb_bf, wob, _NN)
        sga = _sigmoid(ga_ref[...])
        sgb = _sigmoid(gb_ref[...])
        merged_bf = (sga * pa + sgb * pb).astype(BF16)
        r = DN_ALPHA * x_ref[...] + _dot(merged_bf, wout, _NN)
        mu = jnp.mean(r, axis=1, keepdims=True)
        rc = r - mu
        rstd = lax.rsqrt(jnp.mean(rc * rc, axis=1, keepdims=True) + LN_EPS)
        xh = rc * rstd
        lng = lng_ref[...]
        y = xh * lng + lnb_ref[...]
        e = y - tgt_ref[...]
        loss_ref[...] += 0.5 * jnp.sum(jnp.sum(e * e, axis=1, keepdims=True) * (1.0 / D_MODEL), axis=0, keepdims=True)

        dy = e * (1.0 / D_MODEL)
        dlng_ref[...] += jnp.sum(dy * xh, axis=0, keepdims=True)
        dlnb_ref[...] += jnp.sum(dy, axis=0, keepdims=True)
        dxh = dy * lng
        dr = rstd * (dxh - jnp.mean(dxh, axis=1, keepdims=True) - xh * jnp.mean(dxh * xh, axis=1, keepdims=True))
        dxr_ref[...] = DN_ALPHA * dr
        dr_bf = dr.astype(BF16)
        dwout_ref[...] += _dot(merged_bf, dr_bf, _TN)
        dmerged = _dot(dr_bf, wout, _NT)
        dpa_bf = (dmerged * sga).astype(BF16)
        dpb_bf = (dmerged * sgb).astype(BF16)
        _store_grad(dha_ref, dba_ref, 0, dmerged * pa * (sga * (1.0 - sga)))
        _store_grad(dha_ref, dba_ref, D_MODEL, dmerged * pb * (sgb * (1.0 - sgb)))
        dwoa_ref[...] += _dot(ya_bf, dpa_bf, _TN)
        dwob_ref[...] += _dot(yb_bf, dpb_bf, _TN)
        dya = _dot(dpa_bf, woa, _NT)
        dyb_ref[...] = _dot(dpb_bf, wob, _NT)
        do_ref[...] = dya * (za * sa)
        _store_grad(dha_ref, dba_ref, 2 * D_MODEL, dya * o * (sa * (1.0 + za * (1.0 - sa))))

        @pl.when(step == nsteps - 1)
        def _():
            cols = D_MODEL // N_DEV
            for j in range(N_DEV):
                poa_ref[j] = dwoa_ref[:, cols * j:cols * (j + 1)].astype(BF16)
                pob_ref[j] = dwob_ref[:, cols * j:cols * (j + 1)].astype(BF16)
                pout_ref[j] = dwout_ref[cols * j:cols * (j + 1), :].astype(BF16)

    row = lambda w, c=0: pl.BlockSpec((tm, w), lambda i, c=c: (i, c))
    full = lambda shape: pl.BlockSpec(shape, lambda i: (0, 0))
    full3 = lambda shape: pl.BlockSpec(shape, lambda i: (0, 0, 0))
    return pl.pallas_call(
        body, name="merge", grid=(nsteps,),
        in_specs=[row(D_MODEL), row(MLA_WIDTH), row(D_MODEL, 0), row(D_MODEL, 1), row(MLA_WIDTH, 4), row(SGU_WIDTH),
                  row(D_MODEL), full((MLA_WIDTH, D_MODEL)), full((SGU_WIDTH, D_MODEL)), full((D_MODEL, D_MODEL)),
                  full((1, D_MODEL)), full((1, D_MODEL))],
        out_specs=[full((1, LANES)), row(D_MODEL), row(SEG_A), row(MLA_WIDTH), row(SGU_WIDTH),
                   full3((N_DEV, MLA_WIDTH, D_MODEL // N_DEV)), full3((N_DEV, SGU_WIDTH, D_MODEL // N_DEV)),
                   full3((N_DEV, D_MODEL // N_DEV, D_MODEL)), full((1, D_MODEL)), full((1, D_MODEL)), full((1, SEG_A))],
        out_shape=[jax.ShapeDtypeStruct((1, LANES), F32),
                   jax.ShapeDtypeStruct((SEQ, D_MODEL), F32), jax.ShapeDtypeStruct((SEQ, SEG_A), BF16),
                   jax.ShapeDtypeStruct((SEQ, MLA_WIDTH), F32), jax.ShapeDtypeStruct((SEQ, SGU_WIDTH), F32),
                   jax.ShapeDtypeStruct((N_DEV, MLA_WIDTH, D_MODEL // N_DEV), BF16),
                   jax.ShapeDtypeStruct((N_DEV, SGU_WIDTH, D_MODEL // N_DEV), BF16),
                   jax.ShapeDtypeStruct((N_DEV, D_MODEL // N_DEV, D_MODEL), BF16),
                   jax.ShapeDtypeStruct((1, D_MODEL), F32), jax.ShapeDtypeStruct((1, D_MODEL), F32),
                   jax.ShapeDtypeStruct((1, SEG_A), F32)],
        scratch_shapes=[pltpu.VMEM((MLA_WIDTH, D_MODEL), F32), pltpu.VMEM((SGU_WIDTH, D_MODEL), F32),
                        pltpu.VMEM((D_MODEL, D_MODEL), F32)],
        compiler_params=pltpu.CompilerParams(dimension_semantics=("arbitrary",), vmem_limit_bytes=VMEM_BIG),
    )(x, o, h_a, h_a, h_a, y_b, target, w_oa, w_ob, w_out, ln_g, ln_b)


def _mla_bwd(dq, dk, dv, h_c, xt_bf, gq, gkv, wq, wkn, wv, c_t, sa_t, sb_t, parts):
    tm = 256
    hw = MLA_HEADS * HEAD_PAD
    npart = len(parts)
    nsteps = SEQ // tm

    def body(dq_ref, dk_ref, dv_ref, cq_ref, ckv_ref, xt_ref, gq_ref, gkv_ref, wq_ref, wkn_ref, wv_ref, c_ref, sa_ref,
             sb_ref, *rest):
        part_refs, rest = rest[:npart], rest[npart:]
        dhc_ref, puq_ref, dwkn_ref, dwv_ref, dgq_ref, dgkv_ref, dbc_ref, dwc_ref = rest[:8]
        land_refs, (pre_ref, dwq_ref, dwc_acc, send_sems, recv_sems, local_sems) = rest[8:8 + npart], rest[8 + npart:]
        exchange = _exchange_parts(part_refs, land_refs, send_sems, recv_sems, local_sems)
        _exchange_start(pl.program_id(0) == 0, exchange)

        @pl.when(pl.program_id(0) == 0)
        def _():
            for r in (dwq_ref, dwc_acc, dwkn_ref, dwv_ref, dgq_ref, dgkv_ref, dbc_ref):
                r[...] = jnp.zeros_like(r)

        c, sa, sb = c_ref[...], sa_ref[...], sb_ref[...]
        lane = lax.broadcasted_iota(jnp.int32, (tm, LANES), 1)
        rope_lanes = jnp.logical_and(lane >= ROPE_LO, lane < ROPE_HI)

        cq = cq_ref[...]
        gq = gq_ref[...]
        rq = lax.rsqrt(jnp.sum(cq * cq, axis=1, keepdims=True) * (1.0 / Q_LORA_RANK) + RMS_EPS)
        nq = cq * rq
        cqn_bf = (nq * gq).astype(BF16)
        for h in range(MLA_HEADS):
            sl = slice(HEAD_PAD * h, HEAD_PAD * (h + 1))
            pre_ref[:, sl] = _rope_t(dq_ref[:, sl] * ATTN_SCALE, c, sa, sb).astype(BF16)
        dqpre_bf = pre_ref[...]
        dcqn = _dot(dqpre_bf, wq_ref[...], _NT)
        dwq_ref[...] += _dot(cqn_bf, dqpre_bf, _TN)
        dgq_ref[...] += jnp.sum(dcqn * nq, axis=0, keepdims=True)
        dnq = dcqn * gq
        _store_grad(dhc_ref, dbc_ref, 0,
                    rq * (dnq - nq * (jnp.sum(dnq * nq, axis=1, keepdims=True) * (1.0 / Q_LORA_RANK))))

        ckv = ckv_ref[...]
        gkv = gkv_ref[...]
        rkv = lax.rsqrt(jnp.sum(ckv * ckv, axis=1, keepdims=True) * (1.0 / KV_LORA_RANK) + RMS_EPS)
        nkv = ckv * rkv
        ckvn_bf = (nkv * gkv).astype(BF16)
        dk = dk_ref[...]
        dk_bf = dk.astype(BF16)
        dv_bf = dv_ref[...].astype(BF16)
        dckvn = _dot(dk_bf, wkn_ref[...], _NT) + _dot(dv_bf, wv_ref[...], _NT)
        dwkn_ref[...] += _dot(ckvn_bf, dk_bf, _TN)
        dwv_ref[...] += _dot(ckvn_bf, dv_bf, _TN)
        dgkv_ref[...] += jnp.sum(dckvn * nkv, axis=0, keepdims=True)
        dnkv = dckvn * gkv
        _store_grad(dhc_ref, dbc_ref, CQ_PAD, rkv * (
            dnkv - nkv * (jnp.sum(dnkv * nkv, axis=1, keepdims=True) * (1.0 / KV_LORA_RANK))))
        dkpe = jnp.zeros((tm, LANES), F32)
        for h in range(MLA_HEADS):
            dkpe = dkpe + dk[:, HEAD_PAD * h:HEAD_PAD * (h + 1)]
        _store_grad(dhc_ref, dbc_ref, CQ_PAD + LANES, _rope_t(jnp.where(rope_lanes, dkpe, 0.0), c, sa, sb))
        dwc_acc[...] += _dot(xt_ref[...], dhc_ref[...], _NN)

        @pl.when(pl.program_id(0) == SEQ // tm - 1)
        def _():
            dwc_ref[...] = dwc_acc[...].astype(BF16)
            rows = Q_LORA_RANK // N_DEV
            for j in range(N_DEV):
                for h in range(MLA_HEADS):
                    puq_ref[j, :, QK_HEAD_DIM * h:QK_HEAD_DIM * (h + 1)] = dwq_ref[
                        rows * j:rows * (j + 1), HEAD_PAD * h:HEAD_PAD * h + QK_HEAD_DIM].astype(BF16)

        _exchange_finish(pl.program_id(0) == nsteps - 1, exchange)

    full = lambda shape: pl.BlockSpec(shape, lambda i: (0, 0))
    row = lambda w, c=0: pl.BlockSpec((tm, w), lambda i, c=c: (i, c))
    hbm = pl.BlockSpec(memory_space=pl.ANY)
    res = pl.pallas_call(
        body, name="mla_bwd", grid=(nsteps,),
        in_specs=[row(hw), row(hw), row(hw), row(CQ_PAD, 0), row(LANES, CQ_PAD // LANES),
                  pl.BlockSpec((D_MODEL, tm), lambda i: (0, i)),
                  full((1, CQ_PAD)), full((1, KV_LORA_RANK)), full((CQ_PAD, hw)), full((KV_LORA_RANK, hw)),
                  full((KV_LORA_RANK, hw)), row(LANES), row(LANES), row(LANES)] + [hbm] * npart,
        out_specs=[row(SEG_C), pl.BlockSpec((N_DEV, Q_LORA_RANK // N_DEV, MLA_HEADS * QK_HEAD_DIM), lambda i: (0, 0, 0)),
                   full((KV_LORA_RANK, hw)), full((KV_LORA_RANK, hw)),
                   full((1, CQ_PAD)), full((1, KV_LORA_RANK)), full((1, SEG_C)), full((D_MODEL, SEG_C))] + [hbm] * npart,
        out_shape=[jax.ShapeDtypeStruct((SEQ, SEG_C), BF16),
                   jax.ShapeDtypeStruct((N_DEV, Q_LORA_RANK // N_DEV, MLA_HEADS * QK_HEAD_DIM), BF16),
                   jax.ShapeDtypeStruct((KV_LORA_RANK, hw), F32), jax.ShapeDtypeStruct((KV_LORA_RANK, hw), F32),
                   jax.ShapeDtypeStruct((1, CQ_PAD), F32), jax.ShapeDtypeStruct((1, KV_LORA_RANK), F32),
                   jax.ShapeDtypeStruct((1, SEG_C), F32), jax.ShapeDtypeStruct((D_MODEL, SEG_C), BF16)]
        + [jax.ShapeDtypeStruct(p.shape, p.dtype) for p in parts],
        scratch_shapes=[pltpu.VMEM((tm, hw), BF16), pltpu.VMEM((CQ_PAD, hw), F32), pltpu.VMEM((D_MODEL, SEG_C), F32)]
        + _exchange_sems(npart),
        compiler_params=pltpu.CompilerParams(dimension_semantics=("arbitrary",), vmem_limit_bytes=VMEM_MID),
    )(dq, dk, dv, h_c, h_c, xt_bf, gq, gkv, wq, wkn, wv, c_t, sa_t, sb_t, *parts)
    return res[:8], res[8:]


def _adamw_all(ws, gs, ms, vs):
    n = len(ws)
    c1 = 1.0 / (1.0 - ADAM_B1 ** ADAM_STEP)
    c2 = 1.0 / (1.0 - ADAM_B2 ** ADAM_STEP)

    def body(*refs):
        for idx in range(n):
            w, g, m, v = (refs[idx][...], refs[n + idx][...], refs[2 * n + idx][...], refs[3 * n + idx][...])
            m_new = ADAM_B1 * m + (1.0 - ADAM_B1) * g
            v_new = ADAM_B2 * v + (1.0 - ADAM_B2) * (g * g)
            delta = -ADAM_LR * ((m_new * c1) / (jnp.sqrt(v_new * c2) + ADAM_EPS) + ADAM_WD * w)
            refs[4 * n + idx][...] = delta
            refs[5 * n + idx][...] = m_new
            refs[6 * n + idx][...] = v_new

    shapes = [jax.ShapeDtypeStruct(w.shape, F32) for w in ws]
    outs = pl.pallas_call(
        body, name="adamw", out_shape=shapes * 3,
        compiler_params=pltpu.CompilerParams(vmem_limit_bytes=VMEM_BIG),
    )(*ws, *gs, *ms, *vs)
    return outs[:n], outs[n:2 * n], outs[2 * n:]


SHARD_W = IN_WIDTH // N_DEV

_PIECES = [(0, 384, 2, 0), (384, 512, 2, CQ_PAD), (512, 544, 2, CQ_PAD + LANES + ROPE_LO),
           (544, 1056, 0, 2 * D_MODEL), (1056, 1568, 1, 0), (1568, 2080, 1, SGU_WIDTH),
           (2080, 2592, 1, 2 * SGU_WIDTH), (2592, 3616, 0, 0), (3616, 4640, 0, D_MODEL)]


def _column_runs():
    runs = []
    for n0, n1, seg, d0 in _PIECES:
        for j in range(N_DEV):
            lo, hi = max(n0, j * SHARD_W), min(n1, (j + 1) * SHARD_W)
            if lo < hi:
                runs.append((j, lo - j * SHARD_W, hi - j * SHARD_W, seg, d0 + lo - n0))
    return runs


def _mesh_pos():
    return lax.axis_index("x"), lax.axis_index("y"), lax.axis_index("c")


def _remote(src, dst, send_sems, recv_sems, k, to):
    return pltpu.make_async_remote_copy(src_ref=src, dst_ref=dst, send_sem=send_sems.at[k], recv_sem=recv_sems.at[k],
                                        device_id=to, device_id_type=pl.DeviceIdType.MESH)


def _gather_exchange(gats, send_sems, recv_sems, meanwhile=None):
    x, y, c = _mesh_pos()
    me, sibling = (x, y, c), (x, y, 1 - c)
    chips = [(1 - x, y), (x, 1 - y), (1 - x, 1 - y)]

    def copy(a, k, blk, to):
        slab = gats[a].at[4 * blk[0] + 2 * blk[1] + blk[2]]
        return _remote(slab, slab, send_sems, recv_sems, 7 * a + k, to)

    arrays = range(len(gats))
    first = [copy(a, 1 + j, me, (*chip, c)) for j, chip in enumerate(chips) for a in arrays]
    first += [copy(a, 0, me, sibling) for a in arrays]
    for cp in first:
        cp.start()
    if meanwhile is not None:
        meanwhile()
    passed = []
    for j, chip in enumerate(chips):
        for a in arrays:
            copy(a, 1 + j, (*chip, c), me).wait_recv()
            fwd = copy(a, 4 + j, (*chip, c), sibling)
            fwd.start()
            passed.append(fwd)
    for a in arrays:
        copy(a, 0, sibling, me).wait_recv()
    for j, chip in enumerate(chips):
        for a in arrays:
            copy(a, 4 + j, (*chip, 1 - c), me).wait_recv()
    for cp in first + passed:
        cp.wait_send()


def _gather_behind(own, gats, send_sems, recv_sems, local_sems, step, mid, last):
    x, y, c = _mesh_pos()
    me, sibling = (x, y, c), (x, y, 1 - c)
    chips = [(1 - x, y), (x, 1 - y), (1 - x, 1 - y)]
    arrays = range(len(gats))

    def copy(a, k, blk, to, src=None):
        slab = gats[a].at[4 * blk[0] + 2 * blk[1] + blk[2]]
        return _remote(slab if src is None else src, slab, send_sems, recv_sems, 7 * a + k, to)

    first = [copy(a, 1 + j, me, (*chip, c), src=own[a]) for j, chip in enumerate(chips) for a in arrays]
    first += [copy(a, 0, me, sibling, src=own[a]) for a in arrays]
    local = [pltpu.make_async_copy(own[a], gats[a].at[4 * x + 2 * y + c], local_sems.at[a]) for a in arrays]
    passed = [copy(a, 4 + j, (*chip, c), sibling) for j, chip in enumerate(chips) for a in arrays]

    @pl.when(step == 0)
    def _():
        for cp in first + local:
            cp.start()

    @pl.when(step == mid)
    def _():
        for j, chip in enumerate(chips):
            for a in arrays:
                copy(a, 1 + j, (*chip, c), me).wait_recv()
            for a in arrays:
                copy(a, 4 + j, (*chip, c), sibling).start()

    @pl.when(step == last)
    def _():
        for a in arrays:
            copy(a, 0, sibling, me).wait_recv()
        for j, chip in enumerate(chips):
            for a in arrays:
                copy(a, 4 + j, (*chip, 1 - c), me).wait_recv()
        for cp in first + passed:
            cp.wait_send()
        for cp in local:
            cp.wait()


def _gather_first(w_in, w_uq2, w_oa, w_ob, w_out, x2, pos_col, invf_lane):
    hw = MLA_HEADS * HEAD_PAD
    uq_rows = Q_LORA_RANK // N_DEV
    rows = 256

    def body(win_ref, wuq_ref, woa_ref, wob_ref, wout_ref, x_ref, pos_ref, invf_ref,
             wc_ref, wq_ref, winb_ref, oab_ref, obb_ref, outb_ref, xb_ref, xt_ref, c_ref, sa_ref, sb_ref,
             g_uq, blk0, send_sems, recv_sems):
        def local_work():
            for i in range(SEQ // rows):
                xi = x_ref[rows * i:rows * (i + 1), :]
                xb_ref[rows * i:rows * (i + 1), :] = xi.astype(BF16)
                xt_ref[:, rows * i:rows * (i + 1)] = xi.T.astype(BF16)
            ang = pos_ref[...].astype(F32) * invf_ref[...]
            cs, sn = jnp.cos(ang), jnp.sin(ang)
            lane = lax.broadcasted_iota(jnp.int32, ang.shape, 1)
            c_ref[...] = jnp.where(lane < ROPE_LO, 1.0, jnp.where(lane < ROPE_HI, cs, 0.0))
            sa_ref[...] = jnp.where(jnp.logical_and(lane >= ROPE_LO, lane < ROPE_MID), -sn, 0.0)
            sb_ref[...] = jnp.where(jnp.logical_and(lane >= ROPE_MID, lane < ROPE_HI), sn, 0.0)

        x, y, c = _mesh_pos()
        me = (x, y, c)
        winb_ref[...] = win_ref[0].astype(BF16)
        oab_ref[...] = woa_ref[0].astype(BF16)
        obb_ref[...] = wob_ref[0].astype(BF16)
        outb_ref[...] = wout_ref[0].astype(BF16)
        g_uq[4 * x + 2 * y + c] = wuq_ref[...].astype(BF16)

        chip0 = jnp.logical_and(x == 0, y == 0)
        south = c == 0
        half = D_MODEL // 2
        halves = [blk0.at[pl.ds(0, half)], blk0.at[pl.ds(half, half)]]

        def bcopy(k, to, part=None):
            ref = blk0 if part is None else halves[part]
            return _remote(ref, ref, send_sems, recv_sems, 7 + k, to)

        sends0 = [(0, (0, 0, 1), None), (1, (1, 0, 0), 0), (2, (0, 1, 0), 1), (3, (1, 0, 0), 1), (4, (0, 1, 0), 0)]

        @pl.when(jnp.logical_and(chip0, south))
        def _():
            blk0[...] = winb_ref[...]
            for k, to, part in sends0:
                bcopy(k, to, part).start()

        _gather_exchange([g_uq], send_sems, recv_sems, meanwhile=local_work)

        for (cx, cy), first_k, first_half, second_k in (((1, 0), 1, 0, 3), ((0, 1), 2, 1, 4)):
            @pl.when(jnp.logical_and(jnp.logical_and(x == cx, y == cy), south))
            def _(cx=cx, cy=cy, first_k=first_k, first_half=first_half, second_k=second_k):
                bcopy(first_k, me, first_half).wait_recv()
                onward = bcopy(5 + first_half, (1, 1, 0), first_half)
                onward.start()
                bcopy(second_k, me, 1 - first_half).wait_recv()
                north = bcopy(7, (cx, cy, 1))
                north.start()
                onward.wait_send()
                north.wait_send()

        @pl.when(jnp.logical_and(jnp.logical_and(x == 1, y == 1), south))
        def _():
            bcopy(5, me, 0).wait_recv()
            bcopy(6, me, 1).wait_recv()
            north = bcopy(7, (1, 1, 1))
            north.start()
            north.wait_send()

        @pl.when(jnp.logical_and(chip0, c == 1))
        def _():
            bcopy(0, me).wait_recv()

        @pl.when(jnp.logical_and(jnp.logical_not(chip0), c == 1))
        def _():
            bcopy(7, me).wait_recv()

        @pl.when(jnp.logical_and(chip0, south))
        def _():
            for k, to, part in sends0:
                bcopy(k, to, part).wait_send()

        for j, s0, s1, seg, d0 in _column_runs():
            if seg == 2:
                wc_ref[:, d0:d0 + (s1 - s0)] = blk0[:, s0:s1]
        zeros = lambda r, w: jnp.zeros((r, w), BF16)
        wc_ref[:, Q_LORA_RANK:CQ_PAD] = zeros(D_MODEL, CQ_PAD - Q_LORA_RANK)
        wc_ref[:, CQ_PAD + LANES:CQ_PAD + LANES + ROPE_LO] = zeros(D_MODEL, ROPE_LO)
        wc_ref[:, CQ_PAD + LANES + ROPE_HI:SEG_C] = zeros(D_MODEL, LANES - ROPE_HI)
        wq_ref[Q_LORA_RANK:CQ_PAD, :] = zeros(CQ_PAD - Q_LORA_RANK, hw)
        for h in range(MLA_HEADS):
            wq_ref[0:Q_LORA_RANK, HEAD_PAD * h + QK_HEAD_DIM:HEAD_PAD * (h + 1)] = zeros(Q_LORA_RANK, HEAD_PAD - QK_HEAD_DIM)
        for j in range(N_DEV):
            for h in range(MLA_HEADS):
                wq_ref[uq_rows * j:uq_rows * (j + 1), HEAD_PAD * h:HEAD_PAD * h + QK_HEAD_DIM] = g_uq[
                    j, :, QK_HEAD_DIM * h:QK_HEAD_DIM * (h + 1)]

    vmem = pl.BlockSpec(memory_space=pltpu.VMEM)
    return pl.pallas_call(
        body, name="gather_first",
        out_shape=[jax.ShapeDtypeStruct((D_MODEL, SEG_C), BF16), jax.ShapeDtypeStruct((CQ_PAD, hw), BF16),
                   jax.ShapeDtypeStruct(w_in.shape[1:], BF16), jax.ShapeDtypeStruct(w_oa.shape[1:], BF16),
                   jax.ShapeDtypeStruct(w_ob.shape[1:], BF16), jax.ShapeDtypeStruct(w_out.shape[1:], BF16),
                   jax.ShapeDtypeStruct((SEQ, D_MODEL), BF16), jax.ShapeDtypeStruct((D_MODEL, SEQ), BF16)]
        + [jax.ShapeDtypeStruct((SEQ, LANES), F32)] * 3,
        in_specs=[vmem] * 8, out_specs=[vmem] * 11,
        scratch_shapes=[pltpu.VMEM((N_DEV, uq_rows, MLA_HEADS * QK_HEAD_DIM), BF16), pltpu.VMEM((D_MODEL, SHARD_W), BF16),
                        pltpu.SemaphoreType.DMA((15,)), pltpu.SemaphoreType.DMA((15,))],
        compiler_params=pltpu.CompilerParams(vmem_limit_bytes=VMEM_BIG),
    )(w_in, w_uq2, w_oa, w_ob, w_out, x2, pos_col, invf_lane)


def _assemble_in(g_in):
    def body(g_ref, wa_ref, wb_ref):
        segs = [wa_ref, wb_ref]
        for j, s0, s1, seg, d0 in _column_runs():
            if seg < 2:
                segs[seg][:, d0:d0 + (s1 - s0)] = g_ref[j, :, s0:s1]

    return pl.pallas_call(
        body, name="assemble_in",
        out_shape=[jax.ShapeDtypeStruct((D_MODEL, SEG_A), BF16), jax.ShapeDtypeStruct((D_MODEL, SEG_B), BF16)],
        compiler_params=pltpu.CompilerParams(vmem_limit_bytes=VMEM_MID),
    )(g_in)


def _assemble_out(g_oa, g_ob, g_out):
    cols = D_MODEL // N_DEV

    def body(goa_ref, gob_ref, gout_ref, oa_ref, ob_ref, out_ref):
        for j in range(N_DEV):
            oa_ref[:, cols * j:cols * (j + 1)] = goa_ref[j]
            ob_ref[:, cols * j:cols * (j + 1)] = gob_ref[j]
            out_ref[cols * j:cols * (j + 1), :] = gout_ref[j]

    return pl.pallas_call(
        body, name="assemble_out",
        out_shape=[jax.ShapeDtypeStruct((MLA_WIDTH, D_MODEL), BF16), jax.ShapeDtypeStruct((SGU_WIDTH, D_MODEL), BF16),
                   jax.ShapeDtypeStruct((D_MODEL, D_MODEL), BF16)],
    )(g_oa, g_ob, g_out)


C_NAT = 544


P_IN_SPLIT = 896


def _to_parts(dwa, dwb):
    def body(dwa_ref, dwb_ref, phi_ref, plo_ref):
        phi_ref[0, :, 0:C_NAT] = jnp.zeros((P_IN_SPLIT, C_NAT), BF16)
        plo_ref[0, :, 0:C_NAT] = jnp.zeros((D_MODEL - P_IN_SPLIT, C_NAT), BF16)
        segs = [dwa_ref, dwb_ref]
        for j, s0, s1, seg, d0 in _column_runs():
            if seg < 2:
                phi_ref[j, :, s0:s1] = segs[seg][0:P_IN_SPLIT, d0:d0 + (s1 - s0)]
                plo_ref[j, :, s0:s1] = segs[seg][P_IN_SPLIT:D_MODEL, d0:d0 + (s1 - s0)]

    return pl.pallas_call(
        body, name="to_parts",
        out_shape=[jax.ShapeDtypeStruct((N_DEV, P_IN_SPLIT, SHARD_W), BF16),
                   jax.ShapeDtypeStruct((N_DEV, D_MODEL - P_IN_SPLIT, SHARD_W), BF16)],
        compiler_params=pltpu.CompilerParams(vmem_limit_bytes=VMEM_MID))(dwa, dwb)


def _dx_tail(dhs, ws, dx_res, dwc, p_uq, p_rep):
    ntile, sums_at = 8, 5
    tm = SEQ // ntile
    rep_rows = p_rep.shape[1]
    c_rows = D_MODEL // N_DEV
    spec = [((c_rows, C_NAT), BF16), (p_uq.shape[1:], BF16), ((rep_rows, LANES), F32)]
    n = len(spec)

    nseg = len(dhs)

    def body(*refs):
        dh_refs, w_refs = refs[:nseg], refs[nseg:2 * nseg]
        dxr_ref, dwc_ref, puq_ref, prep_ref, dx_ref, call_ref, guq_ref, repall_ref, pc_ref, c_all, rep_all = refs[
            2 * nseg:2 * nseg + 11]
        rest = refs[2 * nseg + 11:]
        ras, tbs, rbs = rest[0:n], rest[n:2 * n], rest[2 * n:3 * n]
        send_sems, recv_sems, gsend, grecv = rest[3 * n:]
        step = pl.program_id(0)
        x, y, c = _mesh_pos()
        me_idx = 4 * x + 2 * y + c
        me, sibling = (x, y, c), (x, y, 1 - c)
        others = [(1 - x, y), (x, 1 - y), (1 - x, 1 - y)]
        parts = [pc_ref, puq_ref, prep_ref]
        gats = [rep_all, c_all]

        def stage1(chip, a):
            return _remote(parts[a].at[2 * chip + (1 - c)], ras[a].at[chip], send_sems, recv_sems, 7 * a + chip, sibling)

        def stage2(k, a):
            cx, cy = others[k]
            return _remote(tbs[a].at[k], rbs[a].at[k], send_sems, recv_sems, 7 * a + 4 + k, (cx, cy, c))

        def gcopy(a, k, blk, to):
            slab = gats[a].at[4 * blk[0] + 2 * blk[1] + blk[2]]
            return _remote(slab, slab, gsend, grecv, 7 * a + k, to)

        def chip_sum(a, chip):
            return parts[a][2 * chip + c].astype(F32) + ras[a][chip].astype(F32)

        @pl.when(step == 0)
        def _():
            for j, s0, s1, seg, d0 in _column_runs():
                if seg == 2:
                    for r in range(N_DEV):
                        pc_ref[r, :, s0:s1] = dwc_ref[c_rows * r:c_rows * (r + 1), d0:d0 + (s1 - s0)]
            for chip in range(4):
                for a in range(n):
                    stage1(chip, a).start()

        @pl.when(step == 1)
        def _():
            for chip in range(4):
                for a in range(n):
                    stage1(chip, a).wait_recv()
            for k, (cx, cy) in enumerate(others):
                for a in range(n):
                    tbs[a][k] = chip_sum(a, 2 * cx + cy).astype(spec[a][1])
                    stage2(k, a).start()

        @pl.when(step == sums_at)
        def _():
            for k in range(3):
                for a in range(n):
                    stage2(k, a).wait_recv()
            sums = []
            for a in range(n):
                acc = chip_sum(a, 2 * x + y)
                for k in range(3):
                    acc = acc + rbs[a][k].astype(F32)
                sums.append(acc)
            c_all[me_idx] = sums[0].astype(BF16)
            guq_ref[...] = sums[1]
            rep_all[me_idx] = sums[2]
            for a in range(2):
                for j, chip in enumerate(others):
                    gcopy(a, 1 + j, me, (*chip, c)).start()
                gcopy(a, 0, me, sibling).start()

        acc = dxr_ref[...]
        for dh_ref, w_ref in zip(dh_refs, w_refs):
            acc = acc + _dot(dh_ref[...], w_ref[...], _NT)
        dx_ref[...] = acc

        @pl.when(step == ntile - 1)
        def _():
            for j, chip in enumerate(others):
                for a in range(2):
                    gcopy(a, 1 + j, (*chip, c), me).wait_recv()
                    gcopy(a, 4 + j, (*chip, c), sibling).start()
            for a in range(2):
                gcopy(a, 0, sibling, me).wait_recv()
                for j, chip in enumerate(others):
                    gcopy(a, 4 + j, (*chip, 1 - c), me).wait_recv()
            for a in range(2):
                gcopy(a, 0, me, sibling).wait_send()
                for j, chip in enumerate(others):
                    gcopy(a, 1 + j, me, (*chip, c)).wait_send()
                    gcopy(a, 4 + j, (*chip, c), sibling).wait_send()
            for a in range(n):
                for chip in range(4):
                    stage1(chip, a).wait_send()
                for k in range(3):
                    stage2(k, a).wait_send()
            call_ref[...] = c_all[...]
            repall_ref[...] = rep_all[...]

    row = lambda w: pl.BlockSpec((tm, w), lambda i: (i, 0))
    full = lambda shape: pl.BlockSpec(shape, lambda i: (0,) * len(shape))
    scratch = [pltpu.VMEM((N_DEV, c_rows, C_NAT), BF16), pltpu.VMEM((N_DEV, c_rows, C_NAT), BF16),
               pltpu.VMEM((N_DEV, rep_rows, LANES), F32)]
    for lead in (4, 3, 3):
        scratch += [pltpu.VMEM((lead,) + tuple(shape), dt) for shape, dt in spec]
    scratch += [pltpu.SemaphoreType.DMA((7 * n,)), pltpu.SemaphoreType.DMA((7 * n,)),
                pltpu.SemaphoreType.DMA((14,)), pltpu.SemaphoreType.DMA((14,))]
    return pl.pallas_call(
        body, name="dx_tail", grid=(SEQ // tm,),
        in_specs=[row(dh.shape[1]) for dh in dhs] + [full(w.shape) for w in ws]
        + [row(D_MODEL), full(dwc.shape), full(p_uq.shape), full(p_rep.shape)],
        out_specs=[row(D_MODEL), full((N_DEV, c_rows, C_NAT)), full(p_uq.shape[1:]), full((N_DEV, rep_rows, LANES))],
        out_shape=[jax.ShapeDtypeStruct((SEQ, D_MODEL), F32), jax.ShapeDtypeStruct((N_DEV, c_rows, C_NAT), BF16),
                   jax.ShapeDtypeStruct(p_uq.shape[1:], F32), jax.ShapeDtypeStruct((N_DEV, rep_rows, LANES), F32)],
        scratch_shapes=scratch,
        compiler_params=pltpu.CompilerParams(dimension_semantics=("arbitrary",), vmem_limit_bytes=VMEM_BIG),
    )(*dhs, *ws, dx_res, dwc, p_uq, p_rep)


def _sum_landed(landed, c_all):
    c_rows = D_MODEL // N_DEV

    def body(rhi_ref, rlo_ref, roa_ref, rob_ref, rout_ref, call_ref, gin_ref, goa_ref, gob_ref, gout_ref):
        def total(ref, sl):
            acc = ref[0, sl, :].astype(F32)
            for s in range(1, N_DEV):
                acc = acc + ref[s, sl, :].astype(F32)
            return acc

        x, y, c = _mesh_pos()
        dev0 = jnp.where(4 * x + 2 * y + c == 0, 1.0, 0.0)
        for j in range(N_DEV):
            sl = slice(c_rows * j, c_rows * (j + 1))
            below = c_rows * j < P_IN_SPLIT
            tot = total(rhi_ref, sl) if below else total(rlo_ref, slice(c_rows * j - P_IN_SPLIT, c_rows * (j + 1) - P_IN_SPLIT))
            gin_ref[0, sl, C_NAT:SHARD_W] = tot[:, C_NAT:SHARD_W]
            gin_ref[0, sl, 0:C_NAT] = tot[:, 0:C_NAT] + dev0 * call_ref[j].astype(F32)
        goa_ref[0] = total(roa_ref, slice(None))
        gob_ref[0] = total(rob_ref, slice(None))
        gout_ref[0] = total(rout_ref, slice(None))

    return pl.pallas_call(
        body, name="sum_landed",
        out_shape=[jax.ShapeDtypeStruct((1, D_MODEL, SHARD_W), F32)]
        + [jax.ShapeDtypeStruct((1,) + r.shape[1:], F32) for r in landed[2:]],
        compiler_params=pltpu.CompilerParams(vmem_limit_bytes=VMEM_MID),
    )(*landed, c_all)


_O_CQ, _O_CKV, _O_KPE, _O_ZA, _O_U, _O_V, _O_ZB, _O_GA, _O_GB = 0, 384, 512, 544, 1056, 1568, 2080, 2592, 3616


def _to_segments(w):
    z = lambda n: jnp.zeros(w.shape[:-1] + (n,), w.dtype)
    seg_a = jnp.concatenate([w[..., _O_GA:_O_GB], w[..., _O_GB:IN_WIDTH], w[..., _O_ZA:_O_U]], axis=-1)
    seg_b = jnp.concatenate([w[..., _O_U:_O_V], w[..., _O_V:_O_ZB], w[..., _O_ZB:_O_GA]], axis=-1)
    seg_c = jnp.concatenate([w[..., _O_CQ:_O_CKV], z(CQ_PAD - Q_LORA_RANK), w[..., _O_CKV:_O_KPE],
                             z(ROPE_LO), w[..., _O_KPE:_O_ZA], z(LANES - ROPE_HI)], axis=-1)
    return seg_a, seg_b, seg_c


def _from_segments(seg_a, seg_b, seg_c):
    kpe0 = CQ_PAD + LANES + ROPE_LO
    return jnp.concatenate([
        seg_c[..., 0:Q_LORA_RANK], seg_c[..., CQ_PAD:CQ_PAD + LANES], seg_c[..., kpe0:kpe0 + QK_ROPE_DIM],
        seg_a[..., 2 * D_MODEL:SEG_A], seg_b, seg_a[..., 0:2 * D_MODEL]], axis=-1)


def kernel(x, positions, w_in, b_in, g_q, w_uq, g_kv, w_ukv, w_oa, sgu_ln_g, sgu_ln_b, w_s, b_s, w_ob, w_out, ln_g, ln_b, loss_target, m_w_in, m_b_in, m_g_q, m_w_uq, m_g_kv, m_w_ukv, m_w_oa, m_sgu_ln_g, m_sgu_ln_b, m_w_s, m_b_s, m_w_ob, m_w_out, m_ln_g, m_ln_b, v_w_in, v_b_in, v_g_q, v_w_uq, v_g_kv, v_w_ukv, v_w_oa, v_sgu_ln_g, v_sgu_ln_b, v_w_s, v_b_s, v_w_ob, v_w_out, v_ln_g, v_ln_b):
    w_uq2 = w_uq[0].reshape(Q_LORA_RANK // N_DEV, MLA_HEADS * QK_HEAD_DIM)
    inv_freq = ROPE_THETA ** (-jnp.arange(0, QK_ROPE_DIM, 2, dtype=F32) / QK_ROPE_DIM)
    invf_lane = jnp.concatenate([jnp.zeros((ROPE_LO,), F32), inv_freq, inv_freq,
                                 jnp.zeros((LANES - ROPE_HI,), F32)]).reshape(1, LANES)
    first = _gather_first(w_in, w_uq2, w_oa, w_ob, w_out, x[0], positions.reshape(SEQ, 1), invf_lane)
    partials = _local_step(x[0], loss_target[0], first, b_in, g_q, g_kv, w_ukv, sgu_ln_g, sgu_ln_b, w_s, b_s, ln_g, ln_b)
    weights = dict(w_in=w_in, b_in=b_in, g_q=g_q, w_uq=w_uq, g_kv=g_kv, w_ukv=w_ukv, w_oa=w_oa, sgu_ln_g=sgu_ln_g,
                   sgu_ln_b=sgu_ln_b, w_s=w_s, b_s=b_s, w_ob=w_ob, w_out=w_out, ln_g=ln_g, ln_b=ln_b)
    moms = dict(w_in=m_w_in, b_in=m_b_in, g_q=m_g_q, w_uq=m_w_uq, g_kv=m_g_kv, w_ukv=m_w_ukv, w_oa=m_w_oa,
                sgu_ln_g=m_sgu_ln_g, sgu_ln_b=m_sgu_ln_b, w_s=m_w_s, b_s=m_b_s, w_ob=m_w_ob, w_out=m_w_out,
                ln_g=m_ln_g, ln_b=m_ln_b)
    vars_ = dict(w_in=v_w_in, b_in=v_b_in, g_q=v_g_q, w_uq=v_w_uq, g_kv=v_g_kv, w_ukv=v_w_ukv, w_oa=v_w_oa,
                 sgu_ln_g=v_sgu_ln_g, sgu_ln_b=v_sgu_ln_b, w_s=v_w_s, b_s=v_b_s, w_ob=v_w_ob, w_out=v_w_out,
                 ln_g=v_ln_g, ln_b=v_ln_b)
    return _reduce_and_update(partials, weights, moms, vars_)


def _local_step(x2, tgt, first, b_in, g_q, g_kv, w_ukv, sgu_ln_g, sgu_ln_b, w_s, b_s, ln_g, ln_b):
    wc, wq, win_b, oa_b, ob_b, out_b, x_bf, xt_bf, c_t, sa_t, sb_t = first
    ba, bb, bc = _to_segments(b_in)
    w_ukv_bf = w_ukv[0].astype(BF16)
    wkn = jnp.pad(w_ukv_bf[:, :, :QK_NOPE_DIM], ((0, 0), (0, 0), (0, HEAD_PAD - QK_NOPE_DIM))).reshape(KV_LORA_RANK, -1)
    wv = jnp.pad(w_ukv_bf[:, :, QK_NOPE_DIM:], ((0, 0), (0, 0), (0, HEAD_PAD - V_HEAD_DIM))).reshape(KV_LORA_RANK, -1)
    gq = jnp.pad(g_q, ((0, 0), (0, CQ_PAD - Q_LORA_RANK)))
    bias_full = jnp.repeat(b_s[0].T, SGU_GROUP_DIM, axis=1)
    w_s3 = w_s[0]
    w_st3 = jnp.swapaxes(w_s3, 1, 2)

    h_c = _mm(x_bf, wc, bias=bc, tm=512, tn=SEG_C, name="in_proj_c")
    q, k, kt, vx, vxt = _mla_prep(h_c, gq, g_kv, wq, wkn, wv, c_t, sa_t, sb_t)
    o, lse, (g_in,) = _attn_fwd(q, kt, vx, (win_b,))
    wa, wb = _assemble_in(g_in)
    h_a, (g_out,) = _mm(x_bf, wa, bias=ba, own=(out_b,), tm=512, tn=SEG_A // 2, name="in_proj_a")
    h_b, (g_oa, g_ob) = _mm(x_bf, wb, bias=bb, own=(oa_b, ob_b), tm=512, tn=SEG_B // 2, name="in_proj_b")
    y_b = _sgu_fwd(h_b, sgu_ln_g, sgu_ln_b, w_s3, bias_full)
    w_oa_f, w_ob_f, w_out_f = _assemble_out(g_oa, g_ob, g_out)

    (loss_row, dx_res, dh_a, d_o, d_yb, p_oa, p_ob, p_out, d_lng, d_lnb, d_ba) = _merge(
        x2, o, h_a, y_b, tgt, w_oa_f, w_ob_f, w_out_f, ln_g, ln_b)
    (dh_b, d_ws, d_bs_t, d_slg, d_slb, d_bb), (r_out,) = _sgu_bwd(h_b, d_yb, sgu_ln_g, sgu_ln_b, w_s3, w_st3, bias_full,
                                                                 (p_out,))
    d_wa, (r_oa,) = _mm(xt_bf, dh_a, out_dtype=BF16, parts=(p_oa,), tm=512, tn=512, name="dw_in_a")
    d_wb = _mm(xt_bf, dh_b, out_dtype=BF16, tm=512, tn=512, name="dw_in_b")
    p_hi, p_lo = _to_parts(d_wa, d_wb)
    dq, dk, dv, (r_hi,) = _attn_bwd(q, kt, k, vxt, d_o, o, lse, (p_hi,))
    (dh_c, p_uq, d_wkn, d_wv, d_gq, d_gkv, d_bc, d_wc), (r_lo, r_ob) = _mla_bwd(
        dq, dk, dv, h_c, xt_bf, gq, g_kv, wq, wkn, wv, c_t, sa_t, sb_t, (p_lo, p_ob))
    landed = (r_hi, r_lo, r_oa, r_ob, r_out)

    p_b_in = _from_segments(d_ba, d_bb, d_bc)
    p_w_ukv = jnp.concatenate([d_wkn.reshape(KV_LORA_RANK, MLA_HEADS, HEAD_PAD)[:, :, :QK_NOPE_DIM],
                               d_wv.reshape(KV_LORA_RANK, MLA_HEADS, HEAD_PAD)[:, :, :V_HEAD_DIM]], axis=-1)
    p_g_q = d_gq[:, :Q_LORA_RANK]
    p_b_s = d_bs_t[:, :SGU_GROUPS].T
    replicated = [p_b_in, p_g_q, d_gkv, p_w_ukv, d_slg, d_slb, d_ws, p_b_s, d_lng, d_lnb]
    return loss_row, ((dh_a, dh_b, dh_c), (wa, wb, wc), dx_res), landed, d_wc, p_uq, replicated


_NAMES = ["w_in", "b_in", "g_q", "w_uq", "g_kv", "w_ukv", "w_oa", "sgu_ln_g", "sgu_ln_b", "w_s", "b_s", "w_ob",
          "w_out", "ln_g", "ln_b"]
_REPLICATED = ["b_in", "g_q", "g_kv", "w_ukv", "sgu_ln_g", "sgu_ln_b", "w_s", "b_s", "ln_g", "ln_b"]


def _reduce_and_update(partials, weights, moms, vars_):
    loss_row, (dhs, ws, dx_res), landed, d_wc, p_uq, replicated = partials
    def piece(a):
        flat = a.reshape(-1)
        return jnp.pad(flat, (0, -flat.size % PACK_ALIGN))

    rep_flat = jnp.concatenate([piece(a) for a in replicated] + [piece(loss_row[0, :1])])
    rep_flat = jnp.pad(rep_flat, (0, N_DEV * PACK_R_ROWS * LANES - rep_flat.size))
    dx, c_all, g_uq, rep_all = _dx_tail(dhs, ws, dx_res, d_wc, p_uq, rep_flat.reshape(N_DEV, PACK_R_ROWS, LANES))
    g_in, g_oa, g_ob, g_out = _sum_landed(landed, c_all)
    rep_sum = rep_all.reshape(-1)
    grads, pos = dict(w_in=g_in, w_uq=g_uq, w_oa=g_oa, w_ob=g_ob, w_out=g_out), 0
    for nm in _REPLICATED:
        grads[nm] = rep_sum[pos:pos + weights[nm].size]
        pos += weights[nm].size + -weights[nm].size % PACK_ALIGN
    loss = rep_sum[pos]
    grads = {nm: grads[nm].reshape(weights[nm].shape) for nm in _NAMES}
    deltas, new_m, new_v = _adamw_all([weights[nm] for nm in _NAMES], [grads[nm] for nm in _NAMES],
                                      [moms[nm] for nm in _NAMES], [vars_[nm] for nm in _NAMES])
    return (loss, dx.reshape(1, SEQ, D_MODEL), *[grads[nm] for nm in _NAMES], *deltas, *new_m, *new_v)
```

```python
import math

import jax
import jax.numpy as jnp
from jax import lax
from jax.experimental import pallas as pl
from jax.experimental.pallas import tpu as pltpu

F32 = jnp.float32
BF16 = jnp.bfloat16

D_MODEL = 1024
SEQ = 2048
N_DEV = 8
MLA_HEADS = 8
Q_LORA_RANK = 384
KV_LORA_RANK = 128
QK_NOPE_DIM = 64
QK_ROPE_DIM = 32
V_HEAD_DIM = 64
QK_HEAD_DIM = QK_NOPE_DIM + QK_ROPE_DIM
MLA_WIDTH = MLA_HEADS * V_HEAD_DIM
ROPE_THETA = 10000.0
SGU_GROUPS = 8
SGU_GROUP_DIM = 64
SGU_WIDTH = SGU_GROUPS * SGU_GROUP_DIM
CHUNK = 128
RMS_EPS = 1e-6
LN_EPS = 1e-5
DN_ALPHA = 2.0 ** 0.25
IN_WIDTH = 4640
ATTN_SCALE = QK_HEAD_DIM ** -0.5

ADAM_LR = 0.001
ADAM_B1 = 0.9
ADAM_B2 = 0.999
ADAM_EPS = 1e-08
ADAM_WD = 0.01
ADAM_STEP = 10

LANES = 128
HEAD_PAD = 128
ROPE_LO = QK_NOPE_DIM
ROPE_MID = ROPE_LO + QK_ROPE_DIM // 2
ROPE_HI = ROPE_LO + QK_ROPE_DIM
CQ_PAD = 512

SEG_A = 2560
SEG_B = 1536
SEG_C = 768

PACK_R_ROWS = 272
PACK_ALIGN = 8 * LANES
VMEM_BIG = 56 * 1024 * 1024
VMEM_MID = 40 * 1024 * 1024


def _sigmoid(x):
    return 1.0 / (1.0 + jnp.exp(-x))


def _gelu_and_grad(x):
    c0 = math.sqrt(2.0 / math.pi)
    x2 = x * x
    t = jnp.tanh(c0 * (x + 0.044715 * x * x2))
    g = 0.5 * x * (1.0 + t)
    dg = 0.5 * (1.0 + t) + 0.5 * x * (1.0 - t * t) * (c0 * (1.0 + 3.0 * 0.044715 * x2))
    return g, dg


def _dot(a, b, dims):
    return lax.dot_general(a, b, (dims, ((), ())), preferred_element_type=F32)


_NN = ((1,), (0,))
_NT = ((1,), (1,))
_TN = ((0,), (0,))


def _store_grad(dh_ref, db_ref, col, val):
    cols = slice(col, col + val.shape[1])
    dh_ref[:, cols] = val.astype(BF16)
    db_ref[:, cols] += jnp.sum(val, axis=0, keepdims=True)


def _mm(a, b, *, tb=False, bias=None, add=None, out_dtype=F32, own=(), parts=(), tm, tn, name):
    m, k = a.shape
    n = b.shape[0] if tb else b.shape[1]
    assert m % tm == 0 and n % tn == 0 and not (own and parts)
    dims = _NT if tb else _NN
    nown = len(own) + len(parts)
    nm = m // tm
    nsteps = (n // tn) * nm

    def body(*refs):
        a_ref, b_ref = refs[0], refs[1]
        pos = 2
        r = _dot(a_ref[...], b_ref[...], dims)
        if bias is not None:
            r = r + refs[pos][...]; pos += 1
        if add is not None:
            r = r + refs[pos][...]; pos += 1
        own_refs = refs[pos:pos + nown]; pos += nown
        refs[pos][...] = r.astype(out_dtype)
        if nown:
            gat_refs = refs[pos + 1:pos + 1 + nown]
            send_sems, recv_sems, local_sems = refs[pos + 1 + nown:]
            step = pl.program_id(0) * nm + pl.program_id(1)
            if own:
                _gather_behind(own_refs, gat_refs, send_sems, recv_sems, local_sems, step, nsteps - 2, nsteps - 1)
            else:
                exchange = _exchange_parts(own_refs, gat_refs, send_sems, recv_sems, local_sems)
                _exchange_start(step == 0, exchange)
                _exchange_finish(step == nsteps - 1, exchange)

    b_spec = pl.BlockSpec((tn, k), lambda j, i: (j, 0)) if tb else pl.BlockSpec((k, tn), lambda j, i: (0, j))
    in_specs, args = [pl.BlockSpec((tm, k), lambda j, i: (i, 0)), b_spec], [a, b]
    if bias is not None:
        in_specs.append(pl.BlockSpec((1, tn), lambda j, i: (0, j))); args.append(bias)
    if add is not None:
        in_specs.append(pl.BlockSpec((tm, tn), lambda j, i: (i, j))); args.append(add)
    hbm = pl.BlockSpec(memory_space=pl.ANY)
    res = pl.pallas_call(
        body, name=name, grid=(n // tn, nm), in_specs=in_specs + [hbm] * nown,
        out_specs=[pl.BlockSpec((tm, tn), lambda j, i: (i, j))] + [hbm] * nown,
        out_shape=[jax.ShapeDtypeStruct((m, n), out_dtype)]
        + [jax.ShapeDtypeStruct((N_DEV,) + o.shape, o.dtype) for o in own]
        + [jax.ShapeDtypeStruct(p.shape, p.dtype) for p in parts],
        scratch_shapes=_exchange_sems(nown) if nown else [],
        compiler_params=pltpu.CompilerParams(dimension_semantics=("arbitrary", "arbitrary"), vmem_limit_bytes=VMEM_BIG),
    )(*args, *own, *parts)
    return (res[0], res[1:]) if nown else res[0]


def _rope(x, c, sa, sb):
    return x * c + pltpu.roll(x, LANES - 16, 1) * sa + pltpu.roll(x, 16, 1) * sb


def _rope_t(dy, c, sa, sb):
    return dy * c + pltpu.roll(dy * sa, 16, 1) + pltpu.roll(dy * sb, LANES - 16, 1)


def _mla_prep(h_c, gq, gkv, wq, wkn, wvx, c_t, sa_t, sb_t):
    tm = 256
    hw = MLA_HEADS * HEAD_PAD

    def body(cq_ref, ckv_ref, kpe_ref, gq_ref, gkv_ref, wq_ref, wkn_ref, wvx_ref, c_ref, sa_ref, sb_ref,
             q_ref, k_ref, kt_ref, vx_ref, vxt_ref):
        c, sa, sb = c_ref[...], sa_ref[...], sb_ref[...]
        cq = cq_ref[...]
        rq = lax.rsqrt(jnp.sum(cq * cq, axis=1, keepdims=True) * (1.0 / Q_LORA_RANK) + RMS_EPS)
        cqn = ((cq * rq) * gq_ref[...]).astype(BF16)
        qall = _dot(cqn, wq_ref[...], _NN)
        for h in range(MLA_HEADS):
            sl = slice(HEAD_PAD * h, HEAD_PAD * (h + 1))
            q_ref[:, sl] = (_rope(qall[:, sl], c, sa, sb) * ATTN_SCALE).astype(BF16)
        ckv = ckv_ref[...]
        rkv = lax.rsqrt(jnp.sum(ckv * ckv, axis=1, keepdims=True) * (1.0 / KV_LORA_RANK) + RMS_EPS)
        ckvn = ((ckv * rkv) * gkv_ref[...]).astype(BF16)
        knall = _dot(ckvn, wkn_ref[...], _NN)
        vall = _dot(ckvn, wvx_ref[...], _NN)
        kper = _rope(kpe_ref[...], c, sa, sb)
        ones_half = (lax.broadcasted_iota(jnp.int32, (tm, HEAD_PAD), 1) >= V_HEAD_DIM).astype(F32)
        for h in range(MLA_HEADS):
            sl = slice(HEAD_PAD * h, HEAD_PAD * (h + 1))
            kh = knall[:, sl] + kper
            vh = vall[:, sl] + ones_half
            k_ref[:, sl] = kh.astype(BF16)
            kt_ref[sl, :] = kh.T.astype(BF16)
            vx_ref[:, sl] = vh.astype(BF16)
            vxt_ref[sl, :] = vh.T.astype(BF16)

    full = lambda shape: pl.BlockSpec(shape, lambda i: (0, 0))
    tab = pl.BlockSpec((tm, LANES), lambda i: (i, 0))
    row = pl.BlockSpec((tm, hw), lambda i: (i, 0))
    col = pl.BlockSpec((hw, tm), lambda i: (0, i))
    return pl.pallas_call(
        body, name="mla_prep", grid=(SEQ // tm,),
        in_specs=[pl.BlockSpec((tm, CQ_PAD), lambda i: (i, 0)),
                  pl.BlockSpec((tm, LANES), lambda i: (i, CQ_PAD // LANES)),
                  pl.BlockSpec((tm, LANES), lambda i: (i, CQ_PAD // LANES + 1)),
                  full((1, CQ_PAD)), full((1, KV_LORA_RANK)),
                  full((CQ_PAD, hw)), full((KV_LORA_RANK, hw)), full((KV_LORA_RANK, hw)), tab, tab, tab],
        out_specs=[row, row, col, row, col],
        out_shape=[jax.ShapeDtypeStruct((SEQ, hw), BF16), jax.ShapeDtypeStruct((SEQ, hw), BF16),
                   jax.ShapeDtypeStruct((hw, SEQ), BF16), jax.ShapeDtypeStruct((SEQ, hw), BF16),
                   jax.ShapeDtypeStruct((hw, SEQ), BF16)],
        compiler_params=pltpu.CompilerParams(dimension_semantics=("arbitrary",), vmem_limit_bytes=VMEM_MID),
    )(h_c, h_c, h_c, gq, gkv, wq, wkn, wvx, c_t, sa_t, sb_t)


ATT_T = 512
ATT_STRIP = 64


def _attn_fwd(q, kt, vx, own):
    t, rs = ATT_T, ATT_STRIP
    nown = len(own)
    nq = SEQ // t
    nsteps = (MLA_HEADS // 2) * nq

    def body(q_ref, kt_ref, vx_ref, *rest):
        own_refs, (o_ref, l_ref), gat_refs = rest[:nown], rest[nown:nown + 2], rest[nown + 2:2 * nown + 2]
        s_scr, p_scr, m_scr, a_scr, acc_scr, send_sems, recv_sems, local_sems = rest[2 * nown + 2:]
        qi = pl.program_id(1)
        lane = lax.broadcasted_iota(jnp.int32, (t, LANES), 1)
        m_scr[...] = jnp.full((2, t, LANES), -1e30, F32)
        acc_scr[...] = jnp.zeros((2, t, LANES), F32)

        def block(j, masked):
            off = pl.multiple_of(j * t, t)
            for a in range(2):
                sl = slice(HEAD_PAD * a, HEAD_PAD * (a + 1))
                s_scr[a] = _dot(q_ref[:, sl], kt_ref[sl, pl.ds(off, t)], _NN)
                for r in range(t // rs):
                    rows = slice(rs * r, rs * (r + 1))
                    s = s_scr[a, rows, :]
                    if masked:
                        rowi = lax.broadcasted_iota(jnp.int32, (rs, t), 0) + rs * r
                        coli = lax.broadcasted_iota(jnp.int32, (rs, t), 1)
                        s = jnp.where(coli <= rowi, s, -1e30)
                    m_old = m_scr[a, rows, :]
                    m_new = jnp.maximum(m_old, jnp.max(s, axis=1, keepdims=True))
                    p_scr[a, rows, :] = jnp.exp(s - m_new[:, :1]).astype(BF16)
                    a_scr[a, rows, :] = jnp.exp(m_old - m_new)
                    m_scr[a, rows, :] = m_new
                acc_scr[a] = acc_scr[a] * a_scr[a] + _dot(p_scr[a], vx_ref[pl.ds(off, t), sl], _NN)

        def step(j, carry):
            block(j, False)
            return carry
        lax.fori_loop(0, qi, step, 0)
        block(qi, True)
        res = []
        for a in range(2):
            acc = acc_scr[a]
            l = acc[:, V_HEAD_DIM:V_HEAD_DIM + 1]
            res.append((acc / l, m_scr[a] + jnp.log(l)))
        o_ref[...] = jnp.where(lane < V_HEAD_DIM, res[0][0], pltpu.roll(res[1][0], V_HEAD_DIM, 1))
        l_ref[...] = jnp.where(lane < V_HEAD_DIM, res[0][1], res[1][1])
        _gather_behind(own_refs, gat_refs, send_sems, recv_sems, local_sems, pl.program_id(0) * nq + qi,
                       nsteps - 3, nsteps - 1)

    hbm = pl.BlockSpec(memory_space=pl.ANY)
    res = pl.pallas_call(
        body, name="attn_fwd", grid=(MLA_HEADS // 2, nq),
        in_specs=[pl.BlockSpec((t, 2 * HEAD_PAD), lambda p, i: (i, p)),
                  pl.BlockSpec((2 * HEAD_PAD, SEQ), lambda p, i: (p, 0)),
                  pl.BlockSpec((SEQ, 2 * HEAD_PAD), lambda p, i: (0, p))] + [hbm] * nown,
        out_specs=[pl.BlockSpec((t, LANES), lambda p, i: (i, p)),
                   pl.BlockSpec((t, LANES), lambda p, i: (i, p))] + [hbm] * nown,
        out_shape=[jax.ShapeDtypeStruct((SEQ, MLA_WIDTH), F32), jax.ShapeDtypeStruct((SEQ, MLA_WIDTH), F32)]
        + [jax.ShapeDtypeStruct((N_DEV,) + a.shape, a.dtype) for a in own],
        scratch_shapes=[pltpu.VMEM((2, t, t), F32), pltpu.VMEM((2, t, t), BF16), pltpu.VMEM((2, t, LANES), F32),
                        pltpu.VMEM((2, t, LANES), F32), pltpu.VMEM((2, t, LANES), F32)] + _exchange_sems(nown),
        compiler_params=pltpu.CompilerParams(dimension_semantics=("arbitrary", "arbitrary"), vmem_limit_bytes=VMEM_MID),
    )(q, kt, vx, *own)
    return res[0], res[1], res[2:]


def _exchange_parts(parts, lands, send_sems, recv_sems, local_sems):
    x, y, c = _mesh_pos()
    me = 4 * x + 2 * y + c
    peers = [(x, y, 1 - c), (1 - x, y, c), (x, 1 - y, c), (1 - x, 1 - y, c),
             (1 - x, y, 1 - c), (x, 1 - y, 1 - c), (1 - x, 1 - y, 1 - c)]
    remote, local = [], []
    for a, (part, land) in enumerate(zip(parts, lands)):
        for k, peer in enumerate(peers):
            t = 4 * peer[0] + 2 * peer[1] + peer[2]
            remote.append(_remote(part.at[t], land.at[me], send_sems, recv_sems, 7 * a + k, peer))
        local.append(pltpu.make_async_copy(part.at[me], land.at[me], local_sems.at[a]))
    return remote, local


def _exchange_start(first_step, exchange):
    remote, local = exchange

    @pl.when(first_step)
    def _():
        for cp in remote + local:
            cp.start()


def _exchange_finish(last_step, exchange):
    remote, local = exchange

    @pl.when(last_step)
    def _():
        for cp in remote:
            cp.wait_recv()
        for cp in remote:
            cp.wait_send()
        for cp in local:
            cp.wait()


def _exchange_sems(npart):
    return [pltpu.SemaphoreType.DMA((7 * npart,)), pltpu.SemaphoreType.DMA((7 * npart,)),
            pltpu.SemaphoreType.DMA((npart,))]


def _attn_bwd(q, kt, k, vxt, d_o, o, lse, parts):
    t, rs = ATT_T, ATT_STRIP
    nq = SEQ // t
    npart = len(parts)
    nsteps = MLA_HEADS // 2

    def body(q_ref, kt_ref, k_ref, vxt_ref, do_ref, o_ref, l_ref, *rest):
        part_refs, rest = rest[:npart], rest[npart:]
        dq_ref, dk_ref, dv_ref = rest[:3]
        land_refs, rest = rest[3:3 + npart], rest[3 + npart:]
        s_scr, dp_scr, p_scr, ds_scr, st_scr, send_sems, recv_sems, local_sems = rest
        exchange = _exchange_parts(part_refs, land_refs, send_sems, recv_sems, local_sems)
        _exchange_start(pl.program_id(0) == 0, exchange)
        dk_ref[...] = jnp.zeros_like(dk_ref)
        dv_ref[...] = jnp.zeros_like(dv_ref)
        lane = lax.broadcasted_iota(jnp.int32, (t, LANES), 1)

        def qtile(i, carry):
            ioff = pl.multiple_of(i * t, t)
            do_i = do_ref[pl.ds(ioff, t), :]
            o_i = o_ref[pl.ds(ioff, t), :]
            l_i = l_ref[pl.ds(ioff, t), :]
            for a in range(2):
                sl = slice(HEAD_PAD * a, HEAD_PAD * (a + 1))
                sel = (lane < V_HEAD_DIM) if a == 0 else (lane >= V_HEAD_DIM)
                doa = jnp.where(sel, do_i, 0.0)
                oa = o_i
                if a == 1:
                    doa = pltpu.roll(doa, V_HEAD_DIM, 1)
                    oa = pltpu.roll(o_i, V_HEAD_DIM, 1)
                st_scr[0] = jnp.broadcast_to(jnp.sum(doa * oa, axis=1, keepdims=True), (t, LANES))
                st_scr[1] = jnp.broadcast_to(l_i[:, V_HEAD_DIM * a:V_HEAD_DIM * a + 1], (t, LANES))
                doa_bf = doa.astype(BF16)
                qa = q_ref[pl.ds(ioff, t), sl]

                def block(j, masked, dq_acc, sl=sl, qa=qa, doa_bf=doa_bf):
                    joff = pl.multiple_of(j * t, t)
                    s_scr[...] = _dot(qa, kt_ref[sl, pl.ds(joff, t)], _NN)
                    dp_scr[...] = _dot(doa_bf, vxt_ref[sl, pl.ds(joff, t)], _NN)
                    for r in range(t // rs):
                        rows = slice(rs * r, rs * (r + 1))
                        p = jnp.exp(s_scr[rows, :] - st_scr[1, rows, :1])
                        if masked:
                            rowi = lax.broadcasted_iota(jnp.int32, (rs, t), 0) + rs * r
                            coli = lax.broadcasted_iota(jnp.int32, (rs, t), 1)
                            p = jnp.where(coli <= rowi, p, 0.0)
                        p_scr[rows, :] = p.astype(BF16)
                        ds_scr[rows, :] = (p * (dp_scr[rows, :] - st_scr[0, rows, :1])).astype(BF16)
                    dk_ref[pl.ds(joff, t), sl] += _dot(ds_scr[...], qa, _TN)
                    dv_ref[pl.ds(joff, t), sl] += _dot(p_scr[...], doa_bf, _TN)
                    return dq_acc + _dot(ds_scr[...], k_ref[pl.ds(joff, t), sl], _NN)

                dq_acc = lax.fori_loop(0, i, lambda j, acc: block(j, False, acc), jnp.zeros((t, HEAD_PAD), F32))
                dq_ref[pl.ds(ioff, t), sl] = block(i, True, dq_acc)
            return carry

        lax.fori_loop(0, nq, qtile, 0)
        _exchange_finish(pl.program_id(0) == nsteps - 1, exchange)

    hw = MLA_HEADS * HEAD_PAD
    wide = pl.BlockSpec((SEQ, 2 * HEAD_PAD), lambda p: (0, p))
    wide_t = pl.BlockSpec((2 * HEAD_PAD, SEQ), lambda p: (p, 0))
    narrow = pl.BlockSpec((SEQ, LANES), lambda p: (0, p))
    hbm = pl.BlockSpec(memory_space=pl.ANY)
    res = pl.pallas_call(
        body, name="attn_bwd", grid=(nsteps,),
        in_specs=[wide, wide_t, wide, wide_t, narrow, narrow, narrow] + [hbm] * npart,
        out_specs=[wide, wide, wide] + [hbm] * npart,
        out_shape=[jax.ShapeDtypeStruct((SEQ, hw), F32)] * 3 + [jax.ShapeDtypeStruct(p.shape, p.dtype) for p in parts],
        scratch_shapes=[pltpu.VMEM((t, t), F32), pltpu.VMEM((t, t), F32), pltpu.VMEM((t, t), BF16),
                        pltpu.VMEM((t, t), BF16), pltpu.VMEM((2, t, LANES), F32)] + _exchange_sems(npart),
        compiler_params=pltpu.CompilerParams(dimension_semantics=("arbitrary",), vmem_limit_bytes=VMEM_BIG),
    )(q, kt, k, vxt, d_o, o, lse, *parts)
    return res[0], res[1], res[2], res[3:]


def _sgu_math(u, v, zb, lg, lb, ws_ref, bias):
    ug, dug = _gelu_and_grad(u)
    vg, dvg = _gelu_and_grad(v)
    mu = jnp.mean(vg, axis=1, keepdims=True)
    xc = vg - mu
    rstd = lax.rsqrt(jnp.mean(xc * xc, axis=1, keepdims=True) + LN_EPS)
    xh = xc * rstd
    vn_bf = (xh * lg + lb).astype(BF16)
    grp = lax.broadcasted_iota(jnp.int32, (CHUNK, SGU_WIDTH), 1) // SGU_GROUP_DIM
    r_i = lax.broadcasted_iota(jnp.int32, (CHUNK, CHUNK), 0)
    c_i = lax.broadcasted_iota(jnp.int32, (CHUNK, CHUNK), 1)
    tri, tri_t = r_i >= c_i, r_i <= c_i
    mixed = bias
    for g in range(SGU_GROUPS):
        wt = jnp.where(tri, ws_ref[g], 0.0).astype(BF16)
        mixed = mixed + jnp.where(grp == g, _dot(wt, vn_bf, _NN), 0.0)
    sb = _sigmoid(zb)
    return ug, dug, dvg, rstd, xh, vn_bf, grp, tri, tri_t, mixed, sb


def _sgu_fwd(h_b, lg, lb, w_s, bias_full):
    def body(u_ref, v_ref, zb_ref, lg_ref, lb_ref, ws_ref, bias_ref, yb_ref):
        zb = zb_ref[...]
        ug, _, _, _, _, _, _, _, _, mixed, sb = _sgu_math(u_ref[...], v_ref[...], zb, lg_ref[...], lb_ref[...],
                                                       ws_ref, bias_ref[...])
        yb_ref[...] = (ug * mixed) * (zb * sb)

    blk = lambda c: pl.BlockSpec((CHUNK, SGU_WIDTH), lambda i, c=c: (i, c))
    full2 = lambda shape: pl.BlockSpec(shape, lambda i: (0, 0))
    return pl.pallas_call(
        body, name="sgu_fwd", grid=(SEQ // CHUNK,),
        in_specs=[blk(0), blk(1), blk(2), full2((1, SGU_WIDTH)), full2((1, SGU_WIDTH)),
                  pl.BlockSpec((SGU_GROUPS, CHUNK, CHUNK), lambda i: (0, 0, 0)), full2((CHUNK, SGU_WIDTH))],
        out_specs=pl.BlockSpec((CHUNK, SGU_WIDTH), lambda i: (i, 0)),
        out_shape=jax.ShapeDtypeStruct((SEQ, SGU_WIDTH), F32),
        compiler_params=pltpu.CompilerParams(dimension_semantics=("arbitrary",)),
    )(h_b, h_b, h_b, lg, lb, w_s, bias_full)


def _sgu_bwd(h_b, d_yb, lg, lb, w_s, w_st, bias_full, parts):
    nsteps = SEQ // CHUNK
    npart = len(parts)

    def body(u_ref, v_ref, zb_ref, dyb_ref, lg_ref, lb_ref, ws_ref, wst_ref, bias_ref, *rest):
        part_refs, rest = rest[:npart], rest[npart:]
        dhb_ref, dws_ref, dbs_ref, dlg_ref, dlb_ref, dbb_ref = rest[:6]
        land_refs, (dbias_acc, send_sems, recv_sems, local_sems) = rest[6:6 + npart], rest[6 + npart:]
        step = pl.program_id(0)
        exchange = _exchange_parts(part_refs, land_refs, send_sems, recv_sems, local_sems)
        _exchange_start(step == 0, exchange)

        @pl.when(step == 0)
        def _():
            dbb_ref[...] = jnp.zeros_like(dbb_ref)
            dws_ref[...] = jnp.zeros_like(dws_ref)
            dlg_ref[...] = jnp.zeros_like(dlg_ref)
            dlb_ref[...] = jnp.zeros_like(dlb_ref)
            dbias_acc[...] = jnp.zeros_like(dbias_acc)

        zb = zb_ref[...]
        lg = lg_ref[...]
        ug, dug, dvg, rstd, xh, vn_bf, grp, tri, tri_t, mixed, sb = _sgu_math(
            u_ref[...], v_ref[...], zb, lg, lb_ref[...], ws_ref, bias_ref[...])
        dyb = dyb_ref[...]
        dsgu = dyb * (zb * sb)
        dzb = dyb * (ug * mixed) * (sb * (1.0 + zb * (1.0 - sb)))
        du = dsgu * mixed * dug
        dmixed = dsgu * ug
        dbias_acc[...] += dmixed
        dvn = jnp.zeros((CHUNK, SGU_WIDTH), F32)
        for g in range(SGU_GROUPS):
            dm_g = jnp.where(grp == g, dmixed, 0.0).astype(BF16)
            wtt = jnp.where(tri_t, wst_ref[g], 0.0).astype(BF16)
            dvn = dvn + _dot(wtt, dm_g, _NN)
            dws_ref[g] += jnp.where(tri, _dot(dm_g, vn_bf, _NT), 0.0)
        dlg_ref[...] += jnp.sum(dvn * xh, axis=0, keepdims=True)
        dlb_ref[...] += jnp.sum(dvn, axis=0, keepdims=True)
        dxh = dvn * lg
        dvgel = rstd * (dxh - jnp.mean(dxh, axis=1, keepdims=True) - xh * jnp.mean(dxh * xh, axis=1, keepdims=True))
        _store_grad(dhb_ref, dbb_ref, 0, du)
        _store_grad(dhb_ref, dbb_ref, SGU_WIDTH, dvgel * dvg)
        _store_grad(dhb_ref, dbb_ref, 2 * SGU_WIDTH, dzb)

        @pl.when(step == nsteps - 1)
        def _():
            acc = dbias_acc[...]
            lane = lax.broadcasted_iota(jnp.int32, (CHUNK, LANES), 1)
            out = jnp.zeros((CHUNK, LANES), F32)
            for g in range(SGU_GROUPS):
                sg = jnp.sum(jnp.where(grp == g, acc, 0.0), axis=1, keepdims=True)
                out = jnp.where(lane == g, sg, out)
            dbs_ref[...] = out

        _exchange_finish(step == nsteps - 1, exchange)

    blk = lambda c: pl.BlockSpec((CHUNK, SGU_WIDTH), lambda i, c=c: (i, c))
    full2 = lambda shape: pl.BlockSpec(shape, lambda i: (0, 0))
    full3 = pl.BlockSpec((SGU_GROUPS, CHUNK, CHUNK), lambda i: (0, 0, 0))
    hbm = pl.BlockSpec(memory_space=pl.ANY)
    res = pl.pallas_call(
        body, name="sgu_bwd", grid=(nsteps,),
        in_specs=[blk(0), blk(1), blk(2), pl.BlockSpec((CHUNK, SGU_WIDTH), lambda i: (i, 0)),
                  full2((1, SGU_WIDTH)), full2((1, SGU_WIDTH)), full3, full3, full2((CHUNK, SGU_WIDTH))] + [hbm] * npart,
        out_specs=[pl.BlockSpec((CHUNK, SEG_B), lambda i: (i, 0)), full3, full2((CHUNK, LANES)),
                   full2((1, SGU_WIDTH)), full2((1, SGU_WIDTH)), full2((1, SEG_B))] + [hbm] * npart,
        out_shape=[jax.ShapeDtypeStruct((SEQ, SEG_B), BF16),
                   jax.ShapeDtypeStruct((SGU_GROUPS, CHUNK, CHUNK), F32),
                   jax.ShapeDtypeStruct((CHUNK, LANES), F32),
                   jax.ShapeDtypeStruct((1, SGU_WIDTH), F32), jax.ShapeDtypeStruct((1, SGU_WIDTH), F32),
                   jax.ShapeDtypeStruct((1, SEG_B), F32)] + [jax.ShapeDtypeStruct(p.shape, p.dtype) for p in parts],
        scratch_shapes=[pltpu.VMEM((CHUNK, SGU_WIDTH), F32)] + _exchange_sems(npart),
        compiler_params=pltpu.CompilerParams(dimension_semantics=("arbitrary",)),
    )(h_b, h_b, h_b, d_yb, lg, lb, w_s, w_st, bias_full, *parts)
    return res[:6], res[6:]


def _merge(x, o, h_a, y_b, target, w_oa, w_ob, w_out, ln_g, ln_b):
    tm = 256
    nsteps = SEQ // tm

    def body(x_ref, o_ref, ga_ref, gb_ref, za_ref, yb_ref, tgt_ref, woa_ref, wob_ref, wout_ref, lng_ref, lnb_ref,
             loss_ref, dxr_ref, dha_ref, do_ref, dyb_ref, poa_ref, pob_ref, pout_ref, dlng_ref, dlnb_ref, dba_ref,
             dwoa_ref, dwob_ref, dwout_ref):
        step = pl.program_id(0)

        @pl.when(step == 0)
        def _():
            for r in (loss_ref, dwoa_ref, dwob_ref, dwout_ref, dlng_ref, dlnb_ref, dba_ref):
                r[...] = jnp.zeros_like(r)

        o = o_ref[...]
        za = za_ref[...]
        sa = _sigmoid(za)
        ya_bf = (o * (za * sa)).astype(BF16)
        yb_bf = yb_ref[...].astype(BF16)
        woa, wob, wout = woa_ref[...], wob_ref[...], wout_ref[...]
        pa = _dot(ya_bf, woa, _NN)
        pb = _dot(yb_bf, wob, _NN)
        sga = _sigmoid(ga_ref[...])
        sgb = _sigmoid(gb_ref[...])
        merged_bf = (sga * pa + sgb * pb).astype(BF16)
        r = DN_ALPHA * x_ref[...] + _dot(merged_bf, wout, _NN)
        mu = jnp.mean(r, axis=1, keepdims=True)
        rc = r - mu
        rstd = lax.rsqrt(jnp.mean(rc * rc, axis=1, keepdims=True) + LN_EPS)
        xh = rc * rstd
        lng = lng_ref[...]
        y = xh * lng + lnb_ref[...]
        e = y - tgt_ref[...]
        loss_ref[...] += 0.5 * jnp.sum(jnp.sum(e * e, axis=1, keepdims=True) * (1.0 / D_MODEL), axis=0, keepdims=True)

        dy = e * (1.0 / D_MODEL)
        dlng_ref[...] += jnp.sum(dy * xh, axis=0, keepdims=True)
        dlnb_ref[...] += jnp.sum(dy, axis=0, keepdims=True)
        dxh = dy * lng
        dr = rstd * (dxh - jnp.mean(dxh, axis=1, keepdims=True) - xh * jnp.mean(dxh * xh, axis=1, keepdims=True))
        dxr_ref[...] = DN_ALPHA * dr
        dr_bf = dr.astype(BF16)
        dwout_ref[...] += _dot(merged_bf, dr_bf, _TN)
        dmerged = _dot(dr_bf, wout, _NT)
        dpa_bf = (dmerged * sga).astype(BF16)
        dpb_bf = (dmerged * sgb).astype(BF16)
        _store_grad(dha_ref, dba_ref, 0, dmerged * pa * (sga * (1.0 - sga)))
        _store_grad(dha_ref, dba_ref, D_MODEL, dmerged * pb * (sgb * (1.0 - sgb)))
        dwoa_ref[...] += _dot(ya_bf, dpa_bf, _TN)
        dwob_ref[...] += _dot(yb_bf, dpb_bf, _TN)
        dya = _dot(dpa_bf, woa, _NT)
        dyb_ref[...] = _dot(dpb_bf, wob, _NT)
        do_ref[...] = dya * (za * sa)
        _store_grad(dha_ref, dba_ref, 2 * D_MODEL, dya * o * (sa * (1.0 + za * (1.0 - sa))))

        @pl.when(step == nsteps - 1)
        def _():
            cols = D_MODEL // N_DEV
            for j in range(N_DEV):
                poa_ref[j] = dwoa_ref[:, cols * j:cols * (j + 1)].astype(BF16)
                pob_ref[j] = dwob_ref[:, cols * j:cols * (j + 1)].astype(BF16)
                pout_ref[j] = dwout_ref[cols * j:cols * (j + 1), :].astype(BF16)

    row = lambda w, c=0: pl.BlockSpec((tm, w), lambda i, c=c: (i, c))
    full = lambda shape: pl.BlockSpec(shape, lambda i: (0, 0))
    full3 = lambda shape: pl.BlockSpec(shape, lambda i: (0, 0, 0))
    return pl.pallas_call(
        body, name="merge", grid=(nsteps,),
        in_specs=[row(D_MODEL), row(MLA_WIDTH), row(D_MODEL, 0), row(D_MODEL, 1), row(MLA_WIDTH, 4), row(SGU_WIDTH),
                  row(D_MODEL), full((MLA_WIDTH, D_MODEL)), full((SGU_WIDTH, D_MODEL)), full((D_MODEL, D_MODEL)),
                  full((1, D_MODEL)), full((1, D_MODEL))],
        out_specs=[full((1, LANES)), row(D_MODEL), row(SEG_A), row(MLA_WIDTH), row(SGU_WIDTH),
                   full3((N_DEV, MLA_WIDTH, D_MODEL // N_DEV)), full3((N_DEV, SGU_WIDTH, D_MODEL // N_DEV)),
                   full3((N_DEV, D_MODEL // N_DEV, D_MODEL)), full((1, D_MODEL)), full((1, D_MODEL)), full((1, SEG_A))],
        out_shape=[jax.ShapeDtypeStruct((1, LANES), F32),
                   jax.ShapeDtypeStruct((SEQ, D_MODEL), F32), jax.ShapeDtypeStruct((SEQ, SEG_A), BF16),
                   jax.ShapeDtypeStruct((SEQ, MLA_WIDTH), F32), jax.ShapeDtypeStruct((SEQ, SGU_WIDTH), F32),
                   jax.ShapeDtypeStruct((N_DEV, MLA_WIDTH, D_MODEL // N_DEV), BF16),
                   jax.ShapeDtypeStruct((N_DEV, SGU_WIDTH, D_MODEL // N_DEV), BF16),
                   jax.ShapeDtypeStruct((N_DEV, D_MODEL // N_DEV, D_MODEL), BF16),
                   jax.ShapeDtypeStruct((1, D_MODEL), F32), jax.ShapeDtypeStruct((1, D_MODEL), F32),
                   jax.ShapeDtypeStruct((1, SEG_A), F32)],
        scratch_shapes=[pltpu.VMEM((MLA_WIDTH, D_MODEL), F32), pltpu.VMEM((SGU_WIDTH, D_MODEL), F32),
                        pltpu.VMEM((D_MODEL, D_MODEL), F32)],
        compiler_params=pltpu.CompilerParams(dimension_semantics=("arbitrary",), vmem_limit_bytes=VMEM_BIG),
    )(x, o, h_a, h_a, h_a, y_b, target, w_oa, w_ob, w_out, ln_g, ln_b)


def _mla_bwd(dq, dk, dv, h_c, xt_bf, gq, gkv, wq, wkn, wv, c_t, sa_t, sb_t, parts):
    tm = 256
    hw = MLA_HEADS * HEAD_PAD
    npart = len(parts)
    nsteps = SEQ // tm

    def body(dq_ref, dk_ref, dv_ref, cq_ref, ckv_ref, xt_ref, gq_ref, gkv_ref, wq_ref, wkn_ref, wv_ref, c_ref, sa_ref,
             sb_ref, *rest):
        part_refs, rest = rest[:npart], rest[npart:]
        dhc_ref, puq_ref, dwkn_ref, dwv_ref, dgq_ref, dgkv_ref, dbc_ref, dwc_ref = rest[:8]
        land_refs, (pre_ref, dwq_ref, dwc_acc, send_sems, recv_sems, local_sems) = rest[8:8 + npart], rest[8 + npart:]
        exchange = _exchange_parts(part_refs, land_refs, send_sems, recv_sems, local_sems)
        _exchange_start(pl.program_id(0) == 0, exchange)

        @pl.when(pl.program_id(0) == 0)
        def _():
            for r in (dwq_ref, dwc_acc, dwkn_ref, dwv_ref, dgq_ref, dgkv_ref, dbc_ref):
                r[...] = jnp.zeros_like(r)

        c, sa, sb = c_ref[...], sa_ref[...], sb_ref[...]
        lane = lax.broadcasted_iota(jnp.int32, (tm, LANES), 1)
        rope_lanes = jnp.logical_and(lane >= ROPE_LO, lane < ROPE_HI)

        cq = cq_ref[...]
        gq = gq_ref[...]
        rq = lax.rsqrt(jnp.sum(cq * cq, axis=1, keepdims=True) * (1.0 / Q_LORA_RANK) + RMS_EPS)
        nq = cq * rq
        cqn_bf = (nq * gq).astype(BF16)
        for h in range(MLA_HEADS):
            sl = slice(HEAD_PAD * h, HEAD_PAD * (h + 1))
            pre_ref[:, sl] = _rope_t(dq_ref[:, sl] * ATTN_SCALE, c, sa, sb).astype(BF16)
        dqpre_bf = pre_ref[...]
        dcqn = _dot(dqpre_bf, wq_ref[...], _NT)
        dwq_ref[...] += _dot(cqn_bf, dqpre_bf, _TN)
        dgq_ref[...] += jnp.sum(dcqn * nq, axis=0, keepdims=True)
        dnq = dcqn * gq
        _store_grad(dhc_ref, dbc_ref, 0,
                    rq * (dnq - nq * (jnp.sum(dnq * nq, axis=1, keepdims=True) * (1.0 / Q_LORA_RANK))))

        ckv = ckv_ref[...]
        gkv = gkv_ref[...]
        rkv = lax.rsqrt(jnp.sum(ckv * ckv, axis=1, keepdims=True) * (1.0 / KV_LORA_RANK) + RMS_EPS)
        nkv = ckv * rkv
        ckvn_bf = (nkv * gkv).astype(BF16)
        dk = dk_ref[...]
        dk_bf = dk.astype(BF16)
        dv_bf = dv_ref[...].astype(BF16)
        dckvn = _dot(dk_bf, wkn_ref[...], _NT) + _dot(dv_bf, wv_ref[...], _NT)
        dwkn_ref[...] += _dot(ckvn_bf, dk_bf, _TN)
        dwv_ref[...] += _dot(ckvn_bf, dv_bf, _TN)
        dgkv_ref[...] += jnp.sum(dckvn * nkv, axis=0, keepdims=True)
        dnkv = dckvn * gkv
        _store_grad(dhc_ref, dbc_ref, CQ_PAD, rkv * (
            dnkv - nkv * (jnp.sum(dnkv * nkv, axis=1, keepdims=True) * (1.0 / KV_LORA_RANK))))
        dkpe = jnp.zeros((tm, LANES), F32)
        for h in range(MLA_HEADS):
            dkpe = dkpe + dk[:, HEAD_PAD * h:HEAD_PAD * (h + 1)]
        _store_grad(dhc_ref, dbc_ref, CQ_PAD + LANES, _rope_t(jnp.where(rope_lanes, dkpe, 0.0), c, sa, sb))
        dwc_acc[...] += _dot(xt_ref[...], dhc_ref[...], _NN)

        @pl.when(pl.program_id(0) == SEQ // tm - 1)
        def _():
            dwc_ref[...] = dwc_acc[...].astype(BF16)
            rows = Q_LORA_RANK // N_DEV
            for j in range(N_DEV):
                for h in range(MLA_HEADS):
                    puq_ref[j, :, QK_HEAD_DIM * h:QK_HEAD_DIM * (h + 1)] = dwq_ref[
                        rows * j:rows * (j + 1), HEAD_PAD * h:HEAD_PAD * h + QK_HEAD_DIM].astype(BF16)

        _exchange_finish(pl.program_id(0) == nsteps - 1, exchange)

    full = lambda shape: pl.BlockSpec(shape, lambda i: (0, 0))
    row = lambda w, c=0: pl.BlockSpec((tm, w), lambda i, c=c: (i, c))
    hbm = pl.BlockSpec(memory_space=pl.ANY)
    res = pl.pallas_call(
        body, name="mla_bwd", grid=(nsteps,),
        in_specs=[row(hw), row(hw), row(hw), row(CQ_PAD, 0), row(LANES, CQ_PAD // LANES),
                  pl.BlockSpec((D_MODEL, tm), lambda i: (0, i)),
                  full((1, CQ_PAD)), full((1, KV_LORA_RANK)), full((CQ_PAD, hw)), full((KV_LORA_RANK, hw)),
                  full((KV_LORA_RANK, hw)), row(LANES), row(LANES), row(LANES)] + [hbm] * npart,
        out_specs=[row(SEG_C), pl.BlockSpec((N_DEV, Q_LORA_RANK // N_DEV, MLA_HEADS * QK_HEAD_DIM), lambda i: (0, 0, 0)),
                   full((KV_LORA_RANK, hw)), full((KV_LORA_RANK, hw)),
                   full((1, CQ_PAD)), full((1, KV_LORA_RANK)), full((1, SEG_C)), full((D_MODEL, SEG_C))] + [hbm] * npart,
        out_shape=[jax.ShapeDtypeStruct((SEQ, SEG_C), BF16),
                   jax.ShapeDtypeStruct((N_DEV, Q_LORA_RANK // N_DEV, MLA_HEADS * QK_HEAD_DIM), BF16),
                   jax.ShapeDtypeStruct((KV_LORA_RANK, hw), F32), jax.ShapeDtypeStruct((KV_LORA_RANK, hw), F32),
                   jax.ShapeDtypeStruct((1, CQ_PAD), F32), jax.ShapeDtypeStruct((1, KV_LORA_RANK), F32),
                   jax.ShapeDtypeStruct((1, SEG_C), F32), jax.ShapeDtypeStruct((D_MODEL, SEG_C), BF16)]
        + [jax.ShapeDtypeStruct(p.shape, p.dtype) for p in parts],
        scratch_shapes=[pltpu.VMEM((tm, hw), BF16), pltpu.VMEM((CQ_PAD, hw), F32), pltpu.VMEM((D_MODEL, SEG_C), F32)]
        + _exchange_sems(npart),
        compiler_params=pltpu.CompilerParams(dimension_semantics=("arbitrary",), vmem_limit_bytes=VMEM_MID),
    )(dq, dk, dv, h_c, h_c, xt_bf, gq, gkv, wq, wkn, wv, c_t, sa_t, sb_t, *parts)
    return res[:8], res[8:]


def _adamw_all(ws, gs, ms, vs):
    n = len(ws)
    c1 = 1.0 / (1.0 - ADAM_B1 ** ADAM_STEP)
    c2 = 1.0 / (1.0 - ADAM_B2 ** ADAM_STEP)

    def body(*refs):
        for idx in range(n):
            w, g, m, v = (refs[idx][...], refs[n + idx][...], refs[2 * n + idx][...], refs[3 * n + idx][...])
            m_new = ADAM_B1 * m + (1.0 - ADAM_B1) * g
            v_new = ADAM_B2 * v + (1.0 - ADAM_B2) * (g * g)
            delta = -ADAM_LR * ((m_new * c1) / (jnp.sqrt(v_new * c2) + ADAM_EPS) + ADAM_WD * w)
            refs[4 * n + idx][...] = delta
            refs[5 * n + idx][...] = m_new
            refs[6 * n + idx][...] = v_new

    shapes = [jax.ShapeDtypeStruct(w.shape, F32) for w in ws]
    outs = pl.pallas_call(
        body, name="adamw", out_shape=shapes * 3,
        compiler_params=pltpu.CompilerParams(vmem_limit_bytes=VMEM_BIG),
    )(*ws, *gs, *ms, *vs)
    return outs[:n], outs[n:2 * n], outs[2 * n:]


SHARD_W = IN_WIDTH // N_DEV

_PIECES = [(0, 384, 2, 0), (384, 512, 2, CQ_PAD), (512, 544, 2, CQ_PAD + LANES + ROPE_LO),
           (544, 1056, 0, 2 * D_MODEL), (1056, 1568, 1, 0), (1568, 2080, 1, SGU_WIDTH),
           (2080, 2592, 1, 2 * SGU_WIDTH), (2592, 3616, 0, 0), (3616, 4640, 0, D_MODEL)]


def _column_runs():
    runs = []
    for n0, n1, seg, d0 in _PIECES:
        for j in range(N_DEV):
            lo, hi = max(n0, j * SHARD_W), min(n1, (j + 1) * SHARD_W)
            if lo < hi:
                runs.append((j, lo - j * SHARD_W, hi - j * SHARD_W, seg, d0 + lo - n0))
    return runs


def _mesh_pos():
    return lax.axis_index("x"), lax.axis_index("y"), lax.axis_index("c")


def _remote(src, dst, send_sems, recv_sems, k, to):
    return pltpu.make_async_remote_copy(src_ref=src, dst_ref=dst, send_sem=send_sems.at[k], recv_sem=recv_sems.at[k],
                                        device_id=to, device_id_type=pl.DeviceIdType.MESH)


def _gather_exchange(gats, send_sems, recv_sems, meanwhile=None):
    x, y, c = _mesh_pos()
    me, sibling = (x, y, c), (x, y, 1 - c)
    chips = [(1 - x, y), (x, 1 - y), (1 - x, 1 - y)]

    def copy(a, k, blk, to):
        slab = gats[a].at[4 * blk[0] + 2 * blk[1] + blk[2]]
        return _remote(slab, slab, send_sems, recv_sems, 7 * a + k, to)

    arrays = range(len(gats))
    first = [copy(a, 1 + j, me, (*chip, c)) for j, chip in enumerate(chips) for a in arrays]
    first += [copy(a, 0, me, sibling) for a in arrays]
    for cp in first:
        cp.start()
    if meanwhile is not None:
        meanwhile()
    passed = []
    for j, chip in enumerate(chips):
        for a in arrays:
            copy(a, 1 + j, (*chip, c), me).wait_recv()
            fwd = copy(a, 4 + j, (*chip, c), sibling)
            fwd.start()
            passed.append(fwd)
    for a in arrays:
        copy(a, 0, sibling, me).wait_recv()
    for j, chip in enumerate(chips):
        for a in arrays:
            copy(a, 4 + j, (*chip, 1 - c), me).wait_recv()
    for cp in first + passed:
        cp.wait_send()


def _gather_behind(own, gats, send_sems, recv_sems, local_sems, step, mid, last):
    x, y, c = _mesh_pos()
    me, sibling = (x, y, c), (x, y, 1 - c)
    chips = [(1 - x, y), (x, 1 - y), (1 - x, 1 - y)]
    arrays = range(len(gats))

    def copy(a, k, blk, to, src=None):
        slab = gats[a].at[4 * blk[0] + 2 * blk[1] + blk[2]]
        return _remote(slab if src is None else src, slab, send_sems, recv_sems, 7 * a + k, to)

    first = [copy(a, 1 + j, me, (*chip, c), src=own[a]) for j, chip in enumerate(chips) for a in arrays]
    first += [copy(a, 0, me, sibling, src=own[a]) for a in arrays]
    local = [pltpu.make_async_copy(own[a], gats[a].at[4 * x + 2 * y + c], local_sems.at[a]) for a in arrays]
    passed = [copy(a, 4 + j, (*chip, c), sibling) for j, chip in enumerate(chips) for a in arrays]

    @pl.when(step == 0)
    def _():
        for cp in first + local:
            cp.start()

    @pl.when(step == mid)
    def _():
        for j, chip in enumerate(chips):
            for a in arrays:
                copy(a, 1 + j, (*chip, c), me).wait_recv()
        for cp in passed:
            cp.start()

    @pl.when(step == last)
    def _():
        for a in arrays:
            copy(a, 0, sibling, me).wait_recv()
        for j, chip in enumerate(chips):
            for a in arrays:
                copy(a, 4 + j, (*chip, 1 - c), me).wait_recv()
        for cp in first + passed:
            cp.wait_send()
        for cp in local:
            cp.wait()


def _gather_first(w_in, w_uq2, w_oa, w_ob, w_out, x2, pos_col, invf_lane):
    hw = MLA_HEADS * HEAD_PAD
    uq_rows = Q_LORA_RANK // N_DEV
    rows = 256

    def body(win_ref, wuq_ref, woa_ref, wob_ref, wout_ref, x_ref, pos_ref, invf_ref,
             wc_ref, wq_ref, winb_ref, oab_ref, obb_ref, outb_ref, xb_ref, xt_ref, c_ref, sa_ref, sb_ref,
             g_uq, blk0, send_sems, recv_sems):
        def local_work():
            for i in range(SEQ // rows):
                xi = x_ref[rows * i:rows * (i + 1), :]
                xb_ref[rows * i:rows * (i + 1), :] = xi.astype(BF16)
                xt_ref[:, rows * i:rows * (i + 1)] = xi.T.astype(BF16)
            ang = pos_ref[...].astype(F32) * invf_ref[...]
            cs, sn = jnp.cos(ang), jnp.sin(ang)
            lane = lax.broadcasted_iota(jnp.int32, ang.shape, 1)
            c_ref[...] = jnp.where(lane < ROPE_LO, 1.0, jnp.where(lane < ROPE_HI, cs, 0.0))
            sa_ref[...] = jnp.where(jnp.logical_and(lane >= ROPE_LO, lane < ROPE_MID), -sn, 0.0)
            sb_ref[...] = jnp.where(jnp.logical_and(lane >= ROPE_MID, lane < ROPE_HI), sn, 0.0)

        x, y, c = _mesh_pos()
        me = (x, y, c)
        winb_ref[...] = win_ref[0].astype(BF16)
        oab_ref[...] = woa_ref[0].astype(BF16)
        obb_ref[...] = wob_ref[0].astype(BF16)
        outb_ref[...] = wout_ref[0].astype(BF16)
        g_uq[4 * x + 2 * y + c] = wuq_ref[...].astype(BF16)

        chip0 = jnp.logical_and(x == 0, y == 0)
        south = c == 0
        half = D_MODEL // 2
        halves = [blk0.at[pl.ds(0, half)], blk0.at[pl.ds(half, half)]]

        def bcopy(k, to, part=None):
            ref = blk0 if part is None else halves[part]
            return _remote(ref, ref, send_sems, recv_sems, 7 + k, to)

        sends0 = [(0, (0, 0, 1), None), (1, (1, 0, 0), 0), (2, (0, 1, 0), 1), (3, (1, 0, 0), 1), (4, (0, 1, 0), 0)]

        @pl.when(jnp.logical_and(chip0, south))
        def _():
            blk0[...] = winb_ref[...]
            for k, to, part in sends0:
                bcopy(k, to, part).start()

        _gather_exchange([g_uq], send_sems, recv_sems, meanwhile=local_work)

        for (cx, cy), first_k, first_half, second_k in (((1, 0), 1, 0, 3), ((0, 1), 2, 1, 4)):
            @pl.when(jnp.logical_and(jnp.logical_and(x == cx, y == cy), south))
            def _(cx=cx, cy=cy, first_k=first_k, first_half=first_half, second_k=second_k):
                bcopy(first_k, me, first_half).wait_recv()
                onward = bcopy(5 + first_half, (1, 1, 0), first_half)
                onward.start()
                bcopy(second_k, me, 1 - first_half).wait_recv()
                north = bcopy(7, (cx, cy, 1))
                north.start()
                onward.wait_send()
                north.wait_send()

        @pl.when(jnp.logical_and(jnp.logical_and(x == 1, y == 1), south))
        def _():
            bcopy(5, me, 0).wait_recv()
            bcopy(6, me, 1).wait_recv()
            north = bcopy(7, (1, 1, 1))
            north.start()
            north.wait_send()

        @pl.when(jnp.logical_and(chip0, c == 1))
        def _():
            bcopy(0, me).wait_recv()

        @pl.when(jnp.logical_and(jnp.logical_not(chip0), c == 1))
        def _():
            bcopy(7, me).wait_recv()

        @pl.when(jnp.logical_and(chip0, south))
        def _():
            for k, to, part in sends0:
                bcopy(k, to, part).wait_send()

        for j, s0, s1, seg, d0 in _column_runs():
            if seg == 2:
                wc_ref[:, d0:d0 + (s1 - s0)] = blk0[:, s0:s1]
        zeros = lambda r, w: jnp.zeros((r, w), BF16)
        wc_ref[:, Q_LORA_RANK:CQ_PAD] = zeros(D_MODEL, CQ_PAD - Q_LORA_RANK)
        wc_ref[:, CQ_PAD + LANES:CQ_PAD + LANES + ROPE_LO] = zeros(D_MODEL, ROPE_LO)
        wc_ref[:, CQ_PAD + LANES + ROPE_HI:SEG_C] = zeros(D_MODEL, LANES - ROPE_HI)
        wq_ref[Q_LORA_RANK:CQ_PAD, :] = zeros(CQ_PAD - Q_LORA_RANK, hw)
        for h in range(MLA_HEADS):
            wq_ref[0:Q_LORA_RANK, HEAD_PAD * h + QK_HEAD_DIM:HEAD_PAD * (h + 1)] = zeros(Q_LORA_RANK, HEAD_PAD - QK_HEAD_DIM)
        for j in range(N_DEV):
            for h in range(MLA_HEADS):
                wq_ref[uq_rows * j:uq_rows * (j + 1), HEAD_PAD * h:HEAD_PAD * h + QK_HEAD_DIM] = g_uq[
                    j, :, QK_HEAD_DIM * h:QK_HEAD_DIM * (h + 1)]

    vmem = pl.BlockSpec(memory_space=pltpu.VMEM)
    return pl.pallas_call(
        body, name="gather_first",
        out_shape=[jax.ShapeDtypeStruct((D_MODEL, SEG_C), BF16), jax.ShapeDtypeStruct((CQ_PAD, hw), BF16),
                   jax.ShapeDtypeStruct(w_in.shape[1:], BF16), jax.ShapeDtypeStruct(w_oa.shape[1:], BF16),
                   jax.ShapeDtypeStruct(w_ob.shape[1:], BF16), jax.ShapeDtypeStruct(w_out.shape[1:], BF16),
                   jax.ShapeDtypeStruct((SEQ, D_MODEL), BF16), jax.ShapeDtypeStruct((D_MODEL, SEQ), BF16)]
        + [jax.ShapeDtypeStruct((SEQ, LANES), F32)] * 3,
        in_specs=[vmem] * 8, out_specs=[vmem] * 11,
        scratch_shapes=[pltpu.VMEM((N_DEV, uq_rows, MLA_HEADS * QK_HEAD_DIM), BF16), pltpu.VMEM((D_MODEL, SHARD_W), BF16),
                        pltpu.SemaphoreType.DMA((15,)), pltpu.SemaphoreType.DMA((15,))],
        compiler_params=pltpu.CompilerParams(vmem_limit_bytes=VMEM_BIG),
    )(w_in, w_uq2, w_oa, w_ob, w_out, x2, pos_col, invf_lane)


def _assemble_in(g_in):
    def body(g_ref, wa_ref, wb_ref):
        segs = [wa_ref, wb_ref]
        for j, s0, s1, seg, d0 in _column_runs():
            if seg < 2:
                segs[seg][:, d0:d0 + (s1 - s0)] = g_ref[j, :, s0:s1]

    return pl.pallas_call(
        body, name="assemble_in",
        out_shape=[jax.ShapeDtypeStruct((D_MODEL, SEG_A), BF16), jax.ShapeDtypeStruct((D_MODEL, SEG_B), BF16)],
        compiler_params=pltpu.CompilerParams(vmem_limit_bytes=VMEM_MID),
    )(g_in)


def _assemble_out(g_oa, g_ob, g_out):
    cols = D_MODEL // N_DEV

    def body(goa_ref, gob_ref, gout_ref, oa_ref, ob_ref, out_ref):
        for j in range(N_DEV):
            oa_ref[:, cols * j:cols * (j + 1)] = goa_ref[j]
            ob_ref[:, cols * j:cols * (j + 1)] = gob_ref[j]
            out_ref[cols * j:cols * (j + 1), :] = gout_ref[j]

    return pl.pallas_call(
        body, name="assemble_out",
        out_shape=[jax.ShapeDtypeStruct((MLA_WIDTH, D_MODEL), BF16), jax.ShapeDtypeStruct((SGU_WIDTH, D_MODEL), BF16),
                   jax.ShapeDtypeStruct((D_MODEL, D_MODEL), BF16)],
    )(g_oa, g_ob, g_out)


C_NAT = 544


P_IN_SPLIT = 896


def _to_parts(dwa, dwb):
    def body(dwa_ref, dwb_ref, phi_ref, plo_ref):
        phi_ref[0, :, 0:C_NAT] = jnp.zeros((P_IN_SPLIT, C_NAT), BF16)
        plo_ref[0, :, 0:C_NAT] = jnp.zeros((D_MODEL - P_IN_SPLIT, C_NAT), BF16)
        segs = [dwa_ref, dwb_ref]
        for j, s0, s1, seg, d0 in _column_runs():
            if seg < 2:
                phi_ref[j, :, s0:s1] = segs[seg][0:P_IN_SPLIT, d0:d0 + (s1 - s0)]
                plo_ref[j, :, s0:s1] = segs[seg][P_IN_SPLIT:D_MODEL, d0:d0 + (s1 - s0)]

    return pl.pallas_call(
        body, name="to_parts",
        out_shape=[jax.ShapeDtypeStruct((N_DEV, P_IN_SPLIT, SHARD_W), BF16),
                   jax.ShapeDtypeStruct((N_DEV, D_MODEL - P_IN_SPLIT, SHARD_W), BF16)],
        compiler_params=pltpu.CompilerParams(vmem_limit_bytes=VMEM_MID))(dwa, dwb)


def _dx_tail(dhs, ws, dx_res, dwc, p_uq, p_rep):
    ntile, sums_at = 8, 5
    tm = SEQ // ntile
    rep_rows = p_rep.shape[1]
    c_rows = D_MODEL // N_DEV
    spec = [((c_rows, C_NAT), BF16), (p_uq.shape[1:], BF16), ((rep_rows, LANES), F32)]
    n = len(spec)

    nseg = len(dhs)

    def body(*refs):
        dh_refs, w_refs = refs[:nseg], refs[nseg:2 * nseg]
        dxr_ref, dwc_ref, puq_ref, prep_ref, dx_ref, call_ref, guq_ref, repall_ref, pc_ref, c_all, rep_all = refs[
            2 * nseg:2 * nseg + 11]
        rest = refs[2 * nseg + 11:]
        ras, tbs, rbs = rest[0:n], rest[n:2 * n], rest[2 * n:3 * n]
        send_sems, recv_sems, gsend, grecv = rest[3 * n:]
        step = pl.program_id(0)
        x, y, c = _mesh_pos()
        me_idx = 4 * x + 2 * y + c
        me, sibling = (x, y, c), (x, y, 1 - c)
        others = [(1 - x, y), (x, 1 - y), (1 - x, 1 - y)]
        parts = [pc_ref, puq_ref, prep_ref]
        gats = [rep_all, c_all]

        def stage1(chip, a):
            return _remote(parts[a].at[2 * chip + (1 - c)], ras[a].at[chip], send_sems, recv_sems, 7 * a + chip, sibling)

        def stage2(k, a):
            cx, cy = others[k]
            return _remote(tbs[a].at[k], rbs[a].at[k], send_sems, recv_sems, 7 * a + 4 + k, (cx, cy, c))

        def gcopy(a, k, blk, to):
            slab = gats[a].at[4 * blk[0] + 2 * blk[1] + blk[2]]
            return _remote(slab, slab, gsend, grecv, 7 * a + k, to)

        def chip_sum(a, chip):
            return parts[a][2 * chip + c].astype(F32) + ras[a][chip].astype(F32)

        @pl.when(step == 0)
        def _():
            for j, s0, s1, seg, d0 in _column_runs():
                if seg == 2:
                    for r in range(N_DEV):
                        pc_ref[r, :, s0:s1] = dwc_ref[c_rows * r:c_rows * (r + 1), d0:d0 + (s1 - s0)]
            for chip in range(4):
                for a in range(n):
                    stage1(chip, a).start()

        @pl.when(step == 1)
        def _():
            for chip in range(4):
                for a in range(n):
                    stage1(chip, a).wait_recv()
            for k, (cx, cy) in enumerate(others):
                for a in range(n):
                    tbs[a][k] = chip_sum(a, 2 * cx + cy).astype(spec[a][1])
                    stage2(k, a).start()

        @pl.when(step == sums_at)
        def _():
            for k in range(3):
                for a in range(n):
                    stage2(k, a).wait_recv()
            sums = []
            for a in range(n):
                acc = chip_sum(a, 2 * x + y)
                for k in range(3):
                    acc = acc + rbs[a][k].astype(F32)
                sums.append(acc)
            c_all[me_idx] = sums[0].astype(BF16)
            guq_ref[...] = sums[1]
            rep_all[me_idx] = sums[2]
            for a in range(2):
                for j, chip in enumerate(others):
                    gcopy(a, 1 + j, me, (*chip, c)).start()
                gcopy(a, 0, me, sibling).start()

        acc = dxr_ref[...]
        for dh_ref, w_ref in zip(dh_refs, w_refs):
            acc = acc + _dot(dh_ref[...], w_ref[...], _NT)
        dx_ref[...] = acc

        @pl.when(step == ntile - 1)
        def _():
            for j, chip in enumerate(others):
                for a in range(2):
                    gcopy(a, 1 + j, (*chip, c), me).wait_recv()
                    gcopy(a, 4 + j, (*chip, c), sibling).start()
            for a in range(2):
                gcopy(a, 0, sibling, me).wait_recv()
                for j, chip in enumerate(others):
                    gcopy(a, 4 + j, (*chip, 1 - c), me).wait_recv()
            for a in range(2):
                gcopy(a, 0, me, sibling).wait_send()
                for j, chip in enumerate(others):
                    gcopy(a, 1 + j, me, (*chip, c)).wait_send()
                    gcopy(a, 4 + j, (*chip, c), sibling).wait_send()
            for a in range(n):
                for chip in range(4):
                    stage1(chip, a).wait_send()
                for k in range(3):
                    stage2(k, a).wait_send()
            call_ref[...] = c_all[...]
            repall_ref[...] = rep_all[...]

    row = lambda w: pl.BlockSpec((tm, w), lambda i: (i, 0))
    full = lambda shape: pl.BlockSpec(shape, lambda i: (0,) * len(shape))
    scratch = [pltpu.VMEM((N_DEV, c_rows, C_NAT), BF16), pltpu.VMEM((N_DEV, c_rows, C_NAT), BF16),
               pltpu.VMEM((N_DEV, rep_rows, LANES), F32)]
    for lead in (4, 3, 3):
        scratch += [pltpu.VMEM((lead,) + tuple(shape), dt) for shape, dt in spec]
    scratch += [pltpu.SemaphoreType.DMA((7 * n,)), pltpu.SemaphoreType.DMA((7 * n,)),
                pltpu.SemaphoreType.DMA((14,)), pltpu.SemaphoreType.DMA((14,))]
    return pl.pallas_call(
        body, name="dx_tail", grid=(SEQ // tm,),
        in_specs=[row(dh.shape[1]) for dh in dhs] + [full(w.shape) for w in ws]
        + [row(D_MODEL), full(dwc.shape), full(p_uq.shape), full(p_rep.shape)],
        out_specs=[row(D_MODEL), full((N_DEV, c_rows, C_NAT)), full(p_uq.shape[1:]), full((N_DEV, rep_rows, LANES))],
        out_shape=[jax.ShapeDtypeStruct((SEQ, D_MODEL), F32), jax.ShapeDtypeStruct((N_DEV, c_rows, C_NAT), BF16),
                   jax.ShapeDtypeStruct(p_uq.shape[1:], F32), jax.ShapeDtypeStruct((N_DEV, rep_rows, LANES), F32)],
        scratch_shapes=scratch,
        compiler_params=pltpu.CompilerParams(dimension_semantics=("arbitrary",), vmem_limit_bytes=VMEM_BIG),
    )(*dhs, *ws, dx_res, dwc, p_uq, p_rep)


def _sum_landed(landed, c_all):
    c_rows = D_MODEL // N_DEV

    def body(rhi_ref, rlo_ref, roa_ref, rob_ref, rout_ref, call_ref, gin_ref, goa_ref, gob_ref, gout_ref):
        def total(ref, sl):
            acc = ref[0, sl, :].astype(F32)
            for s in range(1, N_DEV):
                acc = acc + ref[s, sl, :].astype(F32)
            return acc

        x, y, c = _mesh_pos()
        dev0 = jnp.where(4 * x + 2 * y + c == 0, 1.0, 0.0)
        for j in range(N_DEV):
            sl = slice(c_rows * j, c_rows * (j + 1))
            below = c_rows * j < P_IN_SPLIT
            tot = total(rhi_ref, sl) if below else total(rlo_ref, slice(c_rows * j - P_IN_SPLIT, c_rows * (j + 1) - P_IN_SPLIT))
            gin_ref[0, sl, C_NAT:SHARD_W] = tot[:, C_NAT:SHARD_W]
            gin_ref[0, sl, 0:C_NAT] = tot[:, 0:C_NAT] + dev0 * call_ref[j].astype(F32)
        goa_ref[0] = total(roa_ref, slice(None))
        gob_ref[0] = total(rob_ref, slice(None))
        gout_ref[0] = total(rout_ref, slice(None))

    return pl.pallas_call(
        body, name="sum_landed",
        out_shape=[jax.ShapeDtypeStruct((1, D_MODEL, SHARD_W), F32)]
        + [jax.ShapeDtypeStruct((1,) + r.shape[1:], F32) for r in landed[2:]],
        compiler_params=pltpu.CompilerParams(vmem_limit_bytes=VMEM_MID),
    )(*landed, c_all)


_O_CQ, _O_CKV, _O_KPE, _O_ZA, _O_U, _O_V, _O_ZB, _O_GA, _O_GB = 0, 384, 512, 544, 1056, 1568, 2080, 2592, 3616


def _to_segments(w):
    z = lambda n: jnp.zeros(w.shape[:-1] + (n,), w.dtype)
    seg_a = jnp.concatenate([w[..., _O_GA:_O_GB], w[..., _O_GB:IN_WIDTH], w[..., _O_ZA:_O_U]], axis=-1)
    seg_b = jnp.concatenate([w[..., _O_U:_O_V], w[..., _O_V:_O_ZB], w[..., _O_ZB:_O_GA]], axis=-1)
    seg_c = jnp.concatenate([w[..., _O_CQ:_O_CKV], z(CQ_PAD - Q_LORA_RANK), w[..., _O_CKV:_O_KPE],
                             z(ROPE_LO), w[..., _O_KPE:_O_ZA], z(LANES - ROPE_HI)], axis=-1)
    return seg_a, seg_b, seg_c


def _from_segments(seg_a, seg_b, seg_c):
    kpe0 = CQ_PAD + LANES + ROPE_LO
    return jnp.concatenate([
        seg_c[..., 0:Q_LORA_RANK], seg_c[..., CQ_PAD:CQ_PAD + LANES], seg_c[..., kpe0:kpe0 + QK_ROPE_DIM],
        seg_a[..., 2 * D_MODEL:SEG_A], seg_b, seg_a[..., 0:2 * D_MODEL]], axis=-1)


def kernel(x, positions, w_in, b_in, g_q, w_uq, g_kv, w_ukv, w_oa, sgu_ln_g, sgu_ln_b, w_s, b_s, w_ob, w_out, ln_g, ln_b, loss_target, m_w_in, m_b_in, m_g_q, m_w_uq, m_g_kv, m_w_ukv, m_w_oa, m_sgu_ln_g, m_sgu_ln_b, m_w_s, m_b_s, m_w_ob, m_w_out, m_ln_g, m_ln_b, v_w_in, v_b_in, v_g_q, v_w_uq, v_g_kv, v_w_ukv, v_w_oa, v_sgu_ln_g, v_sgu_ln_b, v_w_s, v_b_s, v_w_ob, v_w_out, v_ln_g, v_ln_b):
    w_uq2 = w_uq[0].reshape(Q_LORA_RANK // N_DEV, MLA_HEADS * QK_HEAD_DIM)
    inv_freq = ROPE_THETA ** (-jnp.arange(0, QK_ROPE_DIM, 2, dtype=F32) / QK_ROPE_DIM)
    invf_lane = jnp.concatenate([jnp.zeros((ROPE_LO,), F32), inv_freq, inv_freq,
                                 jnp.zeros((LANES - ROPE_HI,), F32)]).reshape(1, LANES)
    first = _gather_first(w_in, w_uq2, w_oa, w_ob, w_out, x[0], positions.reshape(SEQ, 1), invf_lane)
    partials = _local_step(x[0], loss_target[0], first, b_in, g_q, g_kv, w_ukv, sgu_ln_g, sgu_ln_b, w_s, b_s, ln_g, ln_b)
    weights = dict(w_in=w_in, b_in=b_in, g_q=g_q, w_uq=w_uq, g_kv=g_kv, w_ukv=w_ukv, w_oa=w_oa, sgu_ln_g=sgu_ln_g,
                   sgu_ln_b=sgu_ln_b, w_s=w_s, b_s=b_s, w_ob=w_ob, w_out=w_out, ln_g=ln_g, ln_b=ln_b)
    moms = dict(w_in=m_w_in, b_in=m_b_in, g_q=m_g_q, w_uq=m_w_uq, g_kv=m_g_kv, w_ukv=m_w_ukv, w_oa=m_w_oa,
                sgu_ln_g=m_sgu_ln_g, sgu_ln_b=m_sgu_ln_b, w_s=m_w_s, b_s=m_b_s, w_ob=m_w_ob, w_out=m_w_out,
                ln_g=m_ln_g, ln_b=m_ln_b)
    vars_ = dict(w_in=v_w_in, b_in=v_b_in, g_q=v_g_q, w_uq=v_w_uq, g_kv=v_g_kv, w_ukv=v_w_ukv, w_oa=v_w_oa,
                 sgu_ln_g=v_sgu_ln_g, sgu_ln_b=v_sgu_ln_b, w_s=v_w_s, b_s=v_b_s, w_ob=v_w_ob, w_out=v_w_out,
                 ln_g=v_ln_g, ln_b=v_ln_b)
    return _reduce_and_update(partials, weights, moms, vars_)


def _local_step(x2, tgt, first, b_in, g_q, g_kv, w_ukv, sgu_ln_g, sgu_ln_b, w_s, b_s, ln_g, ln_b):
    wc, wq, win_b, oa_b, ob_b, out_b, x_bf, xt_bf, c_t, sa_t, sb_t = first
    ba, bb, bc = _to_segments(b_in)
    w_ukv_bf = w_ukv[0].astype(BF16)
    wkn = jnp.pad(w_ukv_bf[:, :, :QK_NOPE_DIM], ((0, 0), (0, 0), (0, HEAD_PAD - QK_NOPE_DIM))).reshape(KV_LORA_RANK, -1)
    wv = jnp.pad(w_ukv_bf[:, :, QK_NOPE_DIM:], ((0, 0), (0, 0), (0, HEAD_PAD - V_HEAD_DIM))).reshape(KV_LORA_RANK, -1)
    gq = jnp.pad(g_q, ((0, 0), (0, CQ_PAD - Q_LORA_RANK)))
    bias_full = jnp.repeat(b_s[0].T, SGU_GROUP_DIM, axis=1)
    w_s3 = w_s[0]
    w_st3 = jnp.swapaxes(w_s3, 1, 2)

    h_c = _mm(x_bf, wc, bias=bc, tm=512, tn=SEG_C, name="in_proj_c")
    q, k, kt, vx, vxt = _mla_prep(h_c, gq, g_kv, wq, wkn, wv, c_t, sa_t, sb_t)
    o, lse, (g_in,) = _attn_fwd(q, kt, vx, (win_b,))
    wa, wb = _assemble_in(g_in)
    h_a, (g_out,) = _mm(x_bf, wa, bias=ba, own=(out_b,), tm=512, tn=SEG_A // 2, name="in_proj_a")
    h_b, (g_oa, g_ob) = _mm(x_bf, wb, bias=bb, own=(oa_b, ob_b), tm=512, tn=SEG_B // 2, name="in_proj_b")
    y_b = _sgu_fwd(h_b, sgu_ln_g, sgu_ln_b, w_s3, bias_full)
    w_oa_f, w_ob_f, w_out_f = _assemble_out(g_oa, g_ob, g_out)

    (loss_row, dx_res, dh_a, d_o, d_yb, p_oa, p_ob, p_out, d_lng, d_lnb, d_ba) = _merge(
        x2, o, h_a, y_b, tgt, w_oa_f, w_ob_f, w_out_f, ln_g, ln_b)
    (dh_b, d_ws, d_bs_t, d_slg, d_slb, d_bb), (r_out,) = _sgu_bwd(h_b, d_yb, sgu_ln_g, sgu_ln_b, w_s3, w_st3, bias_full,
                                                                 (p_out,))
    d_wa, (r_oa,) = _mm(xt_bf, dh_a, out_dtype=BF16, parts=(p_oa,), tm=512, tn=512, name="dw_in_a")
    d_wb = _mm(xt_bf, dh_b, out_dtype=BF16, tm=512, tn=512, name="dw_in_b")
    p_hi, p_lo = _to_parts(d_wa, d_wb)
    dq, dk, dv, (r_hi,) = _attn_bwd(q, kt, k, vxt, d_o, o, lse, (p_hi,))
    (dh_c, p_uq, d_wkn, d_wv, d_gq, d_gkv, d_bc, d_wc), (r_lo, r_ob) = _mla_bwd(
        dq, dk, dv, h_c, xt_bf, gq, g_kv, wq, wkn, wv, c_t, sa_t, sb_t, (p_lo, p_ob))
    landed = (r_hi, r_lo, r_oa, r_ob, r_out)

    p_b_in = _from_segments(d_ba, d_bb, d_bc)
    p_w_ukv = jnp.concatenate([d_wkn.reshape(KV_LORA_RANK, MLA_HEADS, HEAD_PAD)[:, :, :QK_NOPE_DIM],
                               d_wv.reshape(KV_LORA_RANK, MLA_HEADS, HEAD_PAD)[:, :, :V_HEAD_DIM]], axis=-1)
    p_g_q = d_gq[:, :Q_LORA_RANK]
    p_b_s = d_bs_t[:, :SGU_GROUPS].T
    replicated = [p_b_in, p_g_q, d_gkv, p_w_ukv, d_slg, d_slb, d_ws, p_b_s, d_lng, d_lnb]
    return loss_row, ((dh_a, dh_b, dh_c), (wa, wb, wc), dx_res), landed, d_wc, p_uq, replicated


_NAMES = ["w_in", "b_in", "g_q", "w_uq", "g_kv", "w_ukv", "w_oa", "sgu_ln_g", "sgu_ln_b", "w_s", "b_s", "w_ob",
          "w_out", "ln_g", "ln_b"]
_REPLICATED = ["b_in", "g_q", "g_kv", "w_ukv", "sgu_ln_g", "sgu_ln_b", "w_s", "b_s", "ln_g", "ln_b"]


def _reduce_and_update(partials, weights, moms, vars_):
    loss_row, (dhs, ws, dx_res), landed, d_wc, p_uq, replicated = partials
    def piece(a):
        flat = a.reshape(-1)
        return jnp.pad(flat, (0, -flat.size % PACK_ALIGN))

    rep_flat = jnp.concatenate([piece(a) for a in replicated] + [piece(loss_row[0, :1])])
    rep_flat = jnp.pad(rep_flat, (0, N_DEV * PACK_R_ROWS * LANES - rep_flat.size))
    dx, c_all, g_uq, rep_all = _dx_tail(dhs, ws, dx_res, d_wc, p_uq, rep_flat.reshape(N_DEV, PACK_R_ROWS, LANES))
    g_in, g_oa, g_ob, g_out = _sum_landed(landed, c_all)
    rep_sum = rep_all.reshape(-1)
    grads, pos = dict(w_in=g_in, w_uq=g_uq, w_oa=g_oa, w_ob=g_ob, w_out=g_out), 0
    for nm in _REPLICATED:
        grads[nm] = rep_sum[pos:pos + weights[nm].size]
        pos += weights[nm].size + -weights[nm].size % PACK_ALIGN
    loss = rep_sum[pos]
    grads = {nm: grads[nm].reshape(weights[nm].shape) for nm in _NAMES}
    deltas, new_m, new_v = _adamw_all([weights[nm] for nm in _NAMES], [grads[nm] for nm in _NAMES],
                                      [moms[nm] for nm in _NAMES], [vars_[nm] for nm in _NAMES])
    return (loss, dx.reshape(1, SEQ, D_MODEL), *[grads[nm] for nm in _NAMES], *deltas, *new_m, *new_v)
```

```python
import math

import jax
import jax.numpy as jnp
from jax import lax
from jax.experimental import pallas as pl
from jax.experimental.pallas import tpu as pltpu

F32 = jnp.float32
BF16 = jnp.bfloat16

D_MODEL = 1024
SEQ = 2048
N_DEV = 8
MLA_HEADS = 8
Q_LORA_RANK = 384
KV_LORA_RANK = 128
QK_NOPE_DIM = 64
QK_ROPE_DIM = 32
V_HEAD_DIM = 64
QK_HEAD_DIM = QK_NOPE_DIM + QK_ROPE_DIM
MLA_WIDTH = MLA_HEADS * V_HEAD_DIM
ROPE_THETA = 10000.0
SGU_GROUPS = 8
SGU_GROUP_DIM = 64
SGU_WIDTH = SGU_GROUPS * SGU_GROUP_DIM
CHUNK = 128
RMS_EPS = 1e-6
LN_EPS = 1e-5
DN_ALPHA = 2.0 ** 0.25
IN_WIDTH = 4640
ATTN_SCALE = QK_HEAD_DIM ** -0.5

ADAM_LR = 0.001
ADAM_B1 = 0.9
ADAM_B2 = 0.999
ADAM_EPS = 1e-08
ADAM_WD = 0.01
ADAM_STEP = 10

LANES = 128
HEAD_PAD = 128
ROPE_LO = QK_NOPE_DIM
ROPE_MID = ROPE_LO + QK_ROPE_DIM // 2
ROPE_HI = ROPE_LO + QK_ROPE_DIM
CQ_PAD = 512

SEG_A = 2560
SEG_B = 1536
SEG_C = 768

PACK_R_ROWS = 272
PACK_ALIGN = 8 * LANES
VMEM_BIG = 56 * 1024 * 1024
VMEM_MID = 40 * 1024 * 1024


def _sigmoid(x):
    return 1.0 / (1.0 + jnp.exp(-x))


def _gelu_and_grad(x):
    c0 = math.sqrt(2.0 / math.pi)
    x2 = x * x
    t = jnp.tanh(c0 * (x + 0.044715 * x * x2))
    g = 0.5 * x * (1.0 + t)
    dg = 0.5 * (1.0 + t) + 0.5 * x * (1.0 - t * t) * (c0 * (1.0 + 3.0 * 0.044715 * x2))
    return g, dg


def _dot(a, b, dims):
    return lax.dot_general(a, b, (dims, ((), ())), preferred_element_type=F32)


_NN = ((1,), (0,))
_NT = ((1,), (1,))
_TN = ((0,), (0,))


def _store_grad(dh_ref, db_ref, col, val):
    cols = slice(col, col + val.shape[1])
    dh_ref[:, cols] = val.astype(BF16)
    db_ref[:, cols] += jnp.sum(val, axis=0, keepdims=True)


def _mm(a, b, *, tb=False, bias=None, add=None, out_dtype=F32, own=(), parts=(), tm, tn, name):
    m, k = a.shape
    n = b.shape[0] if tb else b.shape[1]
    assert m % tm == 0 and n % tn == 0 and not (own and parts)
    dims = _NT if tb else _NN
    nown = len(own) + len(parts)
    nm = m // tm
    nsteps = (n // tn) * nm

    def body(*refs):
        a_ref, b_ref = refs[0], refs[1]
        pos = 2
        r = _dot(a_ref[...], b_ref[...], dims)
        if bias is not None:
            r = r + refs[pos][...]; pos += 1
        if add is not None:
            r = r + refs[pos][...]; pos += 1
        own_refs = refs[pos:pos + nown]; pos += nown
        refs[pos][...] = r.astype(out_dtype)
        if nown:
            gat_refs = refs[pos + 1:pos + 1 + nown]
            send_sems, recv_sems, local_sems = refs[pos + 1 + nown:]
            step = pl.program_id(0) * nm + pl.program_id(1)
            if own:
                _gather_behind(own_refs, gat_refs, send_sems, recv_sems, local_sems, step, nsteps - 2, nsteps - 1)
            else:
                exchange = _exchange_parts(own_refs, gat_refs, send_sems, recv_sems, local_sems)
                _exchange_start(step == 0, exchange)
                _exchange_finish(step == nsteps - 1, exchange)

    b_spec = pl.BlockSpec((tn, k), lambda j, i: (j, 0)) if tb else pl.BlockSpec((k, tn), lambda j, i: (0, j))
    in_specs, args = [pl.BlockSpec((tm, k), lambda j, i: (i, 0)), b_spec], [a, b]
    if bias is not None:
        in_specs.append(pl.BlockSpec((1, tn), lambda j, i: (0, j))); args.append(bias)
    if add is not None:
        in_specs.append(pl.BlockSpec((tm, tn), lambda j, i: (i, j))); args.append(add)
    hbm = pl.BlockSpec(memory_space=pl.ANY)
    res = pl.pallas_call(
        body, name=name, grid=(n // tn, nm), in_specs=in_specs + [hbm] * nown,
        out_specs=[pl.BlockSpec((tm, tn), lambda j, i: (i, j))] + [hbm] * nown,
        out_shape=[jax.ShapeDtypeStruct((m, n), out_dtype)]
        + [jax.ShapeDtypeStruct((N_DEV,) + o.shape, o.dtype) for o in own]
        + [jax.ShapeDtypeStruct(p.shape, p.dtype) for p in parts],
        scratch_shapes=_exchange_sems(nown) if nown else [],
        compiler_params=pltpu.CompilerParams(dimension_semantics=("arbitrary", "arbitrary"), vmem_limit_bytes=VMEM_BIG),
    )(*args, *own, *parts)
    return (res[0], res[1:]) if nown else res[0]


def _rope(x, c, sa, sb):
    return x * c + pltpu.roll(x, LANES - 16, 1) * sa + pltpu.roll(x, 16, 1) * sb


def _rope_t(dy, c, sa, sb):
    return dy * c + pltpu.roll(dy * sa, 16, 1) + pltpu.roll(dy * sb, LANES - 16, 1)


def _mla_prep(h_c, gq, gkv, wq, wkn, wvx, c_t, sa_t, sb_t):
    tm = 256
    hw = MLA_HEADS * HEAD_PAD

    def body(cq_ref, ckv_ref, kpe_ref, gq_ref, gkv_ref, wq_ref, wkn_ref, wvx_ref, c_ref, sa_ref, sb_ref,
             q_ref, k_ref, kt_ref, vx_ref, vxt_ref):
        c, sa, sb = c_ref[...], sa_ref[...], sb_ref[...]
        cq = cq_ref[...]
        rq = lax.rsqrt(jnp.sum(cq * cq, axis=1, keepdims=True) * (1.0 / Q_LORA_RANK) + RMS_EPS)
        cqn = ((cq * rq) * gq_ref[...]).astype(BF16)
        qall = _dot(cqn, wq_ref[...], _NN)
        for h in range(MLA_HEADS):
            sl = slice(HEAD_PAD * h, HEAD_PAD * (h + 1))
            q_ref[:, sl] = (_rope(qall[:, sl], c, sa, sb) * ATTN_SCALE).astype(BF16)
        ckv = ckv_ref[...]
        rkv = lax.rsqrt(jnp.sum(ckv * ckv, axis=1, keepdims=True) * (1.0 / KV_LORA_RANK) + RMS_EPS)
        ckvn = ((ckv * rkv) * gkv_ref[...]).astype(BF16)
        knall = _dot(ckvn, wkn_ref[...], _NN)
        vall = _dot(ckvn, wvx_ref[...], _NN)
        kper = _rope(kpe_ref[...], c, sa, sb)
        ones_half = (lax.broadcasted_iota(jnp.int32, (tm, HEAD_PAD), 1) >= V_HEAD_DIM).astype(F32)
        for h in range(MLA_HEADS):
            sl = slice(HEAD_PAD * h, HEAD_PAD * (h + 1))
            kh = knall[:, sl] + kper
            vh = vall[:, sl] + ones_half
            k_ref[:, sl] = kh.astype(BF16)
            kt_ref[sl, :] = kh.T.astype(BF16)
            vx_ref[:, sl] = vh.astype(BF16)
            vxt_ref[sl, :] = vh.T.astype(BF16)

    full = lambda shape: pl.BlockSpec(shape, lambda i: (0, 0))
    tab = pl.BlockSpec((tm, LANES), lambda i: (i, 0))
    row = pl.BlockSpec((tm, hw), lambda i: (i, 0))
    col = pl.BlockSpec((hw, tm), lambda i: (0, i))
    return pl.pallas_call(
        body, name="mla_prep", grid=(SEQ // tm,),
        in_specs=[pl.BlockSpec((tm, CQ_PAD), lambda i: (i, 0)),
                  pl.BlockSpec((tm, LANES), lambda i: (i, CQ_PAD // LANES)),
                  pl.BlockSpec((tm, LANES), lambda i: (i, CQ_PAD // LANES + 1)),
                  full((1, CQ_PAD)), full((1, KV_LORA_RANK)),
                  full((CQ_PAD, hw)), full((KV_LORA_RANK, hw)), full((KV_LORA_RANK, hw)), tab, tab, tab],
        out_specs=[row, row, col, row, col],
        out_shape=[jax.ShapeDtypeStruct((SEQ, hw), BF16), jax.ShapeDtypeStruct((SEQ, hw), BF16),
                   jax.ShapeDtypeStruct((hw, SEQ), BF16), jax.ShapeDtypeStruct((SEQ, hw), BF16),
                   jax.ShapeDtypeStruct((hw, SEQ), BF16)],
        compiler_params=pltpu.CompilerParams(dimension_semantics=("arbitrary",), vmem_limit_bytes=VMEM_MID),
    )(h_c, h_c, h_c, gq, gkv, wq, wkn, wvx, c_t, sa_t, sb_t)


ATT_T = 512
ATT_STRIP = 64


def _attn_fwd(q, kt, vx, own):
    t, rs = ATT_T, ATT_STRIP
    nown = len(own)
    nq = SEQ // t
    nsteps = (MLA_HEADS // 2) * nq

    def body(q_ref, kt_ref, vx_ref, *rest):
        own_refs, (o_ref, l_ref), gat_refs = rest[:nown], rest[nown:nown + 2], rest[nown + 2:2 * nown + 2]
        s_scr, p_scr, m_scr, a_scr, acc_scr, send_sems, recv_sems, local_sems = rest[2 * nown + 2:]
        qi = pl.program_id(1)
        lane = lax.broadcasted_iota(jnp.int32, (t, LANES), 1)
        m_scr[...] = jnp.full((2, t, LANES), -1e30, F32)
        acc_scr[...] = jnp.zeros((2, t, LANES), F32)

        def block(j, masked):
            off = pl.multiple_of(j * t, t)
            for a in range(2):
                sl = slice(HEAD_PAD * a, HEAD_PAD * (a + 1))
                s_scr[a] = _dot(q_ref[:, sl], kt_ref[sl, pl.ds(off, t)], _NN)
                for r in range(t // rs):
                    rows = slice(rs * r, rs * (r + 1))
                    s = s_scr[a, rows, :]
                    if masked:
                        rowi = lax.broadcasted_iota(jnp.int32, (rs, t), 0) + rs * r
                        coli = lax.broadcasted_iota(jnp.int32, (rs, t), 1)
                        s = jnp.where(coli <= rowi, s, -1e30)
                    m_old = m_scr[a, rows, :]
                    m_new = jnp.maximum(m_old, jnp.max(s, axis=1, keepdims=True))
                    p_scr[a, rows, :] = jnp.exp(s - m_new[:, :1]).astype(BF16)
                    a_scr[a, rows, :] = jnp.exp(m_old - m_new)
                    m_scr[a, rows, :] = m_new
                acc_scr[a] = acc_scr[a] * a_scr[a] + _dot(p_scr[a], vx_ref[pl.ds(off, t), sl], _NN)

        def step(j, carry):
            block(j, False)
            return carry
        lax.fori_loop(0, qi, step, 0)
        block(qi, True)
        res = []
        for a in range(2):
            acc = acc_scr[a]
            l = acc[:, V_HEAD_DIM:V_HEAD_DIM + 1]
            res.append((acc / l, m_scr[a] + jnp.log(l)))
        o_ref[...] = jnp.where(lane < V_HEAD_DIM, res[0][0], pltpu.roll(res[1][0], V_HEAD_DIM, 1))
        l_ref[...] = jnp.where(lane < V_HEAD_DIM, res[0][1], res[1][1])
        _gather_behind(own_refs, gat_refs, send_sems, recv_sems, local_sems, pl.program_id(0) * nq + qi,
                       nsteps - 2, nsteps - 1)

    hbm = pl.BlockSpec(memory_space=pl.ANY)
    res = pl.pallas_call(
        body, name="attn_fwd", grid=(MLA_HEADS // 2, nq),
        in_specs=[pl.BlockSpec((t, 2 * HEAD_PAD), lambda p, i: (i, p)),
                  pl.BlockSpec((2 * HEAD_PAD, SEQ), lambda p, i: (p, 0)),
                  pl.BlockSpec((SEQ, 2 * HEAD_PAD), lambda p, i: (0, p))] + [hbm] * nown,
        out_specs=[pl.BlockSpec((t, LANES), lambda p, i: (i, p)),
                   pl.BlockSpec((t, LANES), lambda p, i: (i, p))] + [hbm] * nown,
        out_shape=[jax.ShapeDtypeStruct((SEQ, MLA_WIDTH), F32), jax.ShapeDtypeStruct((SEQ, MLA_WIDTH), F32)]
        + [jax.ShapeDtypeStruct((N_DEV,) + a.shape, a.dtype) for a in own],
        scratch_shapes=[pltpu.VMEM((2, t, t), F32), pltpu.VMEM((2, t, t), BF16), pltpu.VMEM((2, t, LANES), F32),
                        pltpu.VMEM((2, t, LANES), F32), pltpu.VMEM((2, t, LANES), F32)] + _exchange_sems(nown),
        compiler_params=pltpu.CompilerParams(dimension_semantics=("arbitrary", "arbitrary"), vmem_limit_bytes=VMEM_MID),
    )(q, kt, vx, *own)
    return res[0], res[1], res[2:]


def _exchange_parts(parts, lands, send_sems, recv_sems, local_sems):
    x, y, c = _mesh_pos()
    me = 4 * x + 2 * y + c
    peers = [(x, y, 1 - c), (1 - x, y, c), (x, 1 - y, c), (1 - x, 1 - y, c),
             (1 - x, y, 1 - c), (x, 1 - y, 1 - c), (1 - x, 1 - y, 1 - c)]
    remote, local = [], []
    for a, (part, land) in enumerate(zip(parts, lands)):
        for k, peer in enumerate(peers):
            t = 4 * peer[0] + 2 * peer[1] + peer[2]
            remote.append(_remote(part.at[t], land.at[me], send_sems, recv_sems, 7 * a + k, peer))
        local.append(pltpu.make_async_copy(part.at[me], land.at[me], local_sems.at[a]))
    return remote, local


def _exchange_start(first_step, exchange):
    remote, local = exchange

    @pl.when(first_step)
    def _():
        for cp in remote + local:
            cp.start()


def _exchange_finish(last_step, exchange):
    remote, local = exchange

    @pl.when(last_step)
    def _():
        for cp in remote:
            cp.wait_recv()
        for cp in remote:
            cp.wait_send()
        for cp in local:
            cp.wait()


def _exchange_sems(npart):
    return [pltpu.SemaphoreType.DMA((7 * npart,)), pltpu.SemaphoreType.DMA((7 * npart,)),
            pltpu.SemaphoreType.DMA((npart,))]


def _attn_bwd(q, kt, k, vxt, d_o, o, lse, parts):
    t, rs = ATT_T, ATT_STRIP
    nq = SEQ // t
    npart = len(parts)
    nsteps = MLA_HEADS // 2

    def body(q_ref, kt_ref, k_ref, vxt_ref, do_ref, o_ref, l_ref, *rest):
        part_refs, rest = rest[:npart], rest[npart:]
        dq_ref, dk_ref, dv_ref = rest[:3]
        land_refs, rest = rest[3:3 + npart], rest[3 + npart:]
        s_scr, dp_scr, p_scr, ds_scr, st_scr, send_sems, recv_sems, local_sems = rest
        exchange = _exchange_parts(part_refs, land_refs, send_sems, recv_sems, local_sems)
        _exchange_start(pl.program_id(0) == 0, exchange)
        dk_ref[...] = jnp.zeros_like(dk_ref)
        dv_ref[...] = jnp.zeros_like(dv_ref)
        lane = lax.broadcasted_iota(jnp.int32, (t, LANES), 1)

        def qtile(i, carry):
            ioff = pl.multiple_of(i * t, t)
            do_i = do_ref[pl.ds(ioff, t), :]
            o_i = o_ref[pl.ds(ioff, t), :]
            l_i = l_ref[pl.ds(ioff, t), :]
            for a in range(2):
                sl = slice(HEAD_PAD * a, HEAD_PAD * (a + 1))
                sel = (lane < V_HEAD_DIM) if a == 0 else (lane >= V_HEAD_DIM)
                doa = jnp.where(sel, do_i, 0.0)
                oa = o_i
                if a == 1:
                    doa = pltpu.roll(doa, V_HEAD_DIM, 1)
                    oa = pltpu.roll(o_i, V_HEAD_DIM, 1)
                st_scr[0] = jnp.broadcast_to(jnp.sum(doa * oa, axis=1, keepdims=True), (t, LANES))
                st_scr[1] = jnp.broadcast_to(l_i[:, V_HEAD_DIM * a:V_HEAD_DIM * a + 1], (t, LANES))
                doa_bf = doa.astype(BF16)
                qa = q_ref[pl.ds(ioff, t), sl]

                def block(j, masked, dq_acc, sl=sl, qa=qa, doa_bf=doa_bf):
                    joff = pl.multiple_of(j * t, t)
                    s_scr[...] = _dot(qa, kt_ref[sl, pl.ds(joff, t)], _NN)
                    dp_scr[...] = _dot(doa_bf, vxt_ref[sl, pl.ds(joff, t)], _NN)
                    for r in range(t // rs):
                        rows = slice(rs * r, rs * (r + 1))
                        p = jnp.exp(s_scr[rows, :] - st_scr[1, rows, :1])
                        if masked:
                            rowi = lax.broadcasted_iota(jnp.int32, (rs, t), 0) + rs * r
                            coli = lax.broadcasted_iota(jnp.int32, (rs, t), 1)
                            p = jnp.where(coli <= rowi, p, 0.0)
                        p_scr[rows, :] = p.astype(BF16)
                        ds_scr[rows, :] = (p * (dp_scr[rows, :] - st_scr[0, rows, :1])).astype(BF16)
                    dk_ref[pl.ds(joff, t), sl] += _dot(ds_scr[...], qa, _TN)
                    dv_ref[pl.ds(joff, t), sl] += _dot(p_scr[...], doa_bf, _TN)
                    return dq_acc + _dot(ds_scr[...], k_ref[pl.ds(joff, t), sl], _NN)

                dq_acc = lax.fori_loop(0, i, lambda j, acc: block(j, False, acc), jnp.zeros((t, HEAD_PAD), F32))
                dq_ref[pl.ds(ioff, t), sl] = block(i, True, dq_acc)
            return carry

        lax.fori_loop(0, nq, qtile, 0)
        _exchange_finish(pl.program_id(0) == nsteps - 1, exchange)

    hw = MLA_HEADS * HEAD_PAD
    wide = pl.BlockSpec((SEQ, 2 * HEAD_PAD), lambda p: (0, p))
    wide_t = pl.BlockSpec((2 * HEAD_PAD, SEQ), lambda p: (p, 0))
    narrow = pl.BlockSpec((SEQ, LANES), lambda p: (0, p))
    hbm = pl.BlockSpec(memory_space=pl.ANY)
    res = pl.pallas_call(
        body, name="attn_bwd", grid=(nsteps,),
        in_specs=[wide, wide_t, wide, wide_t, narrow, narrow, narrow] + [hbm] * npart,
        out_specs=[wide, wide, wide] + [hbm] * npart,
        out_shape=[jax.ShapeDtypeStruct((SEQ, hw), F32)] * 3 + [jax.ShapeDtypeStruct(p.shape, p.dtype) for p in parts],
        scratch_shapes=[pltpu.VMEM((t, t), F32), pltpu.VMEM((t, t), F32), pltpu.VMEM((t, t), BF16),
                        pltpu.VMEM((t, t), BF16), pltpu.VMEM((2, t, LANES), F32)] + _exchange_sems(npart),
        compiler_params=pltpu.CompilerParams(dimension_semantics=("arbitrary",), vmem_limit_bytes=VMEM_BIG),
    )(q, kt, k, vxt, d_o, o, lse, *parts)
    return res[0], res[1], res[2], res[3:]


def _sgu_math(u, v, zb, lg, lb, ws_ref, bias):
    ug, dug = _gelu_and_grad(u)
    vg, dvg = _gelu_and_grad(v)
    mu = jnp.mean(vg, axis=1, keepdims=True)
    xc = vg - mu
    rstd = lax.rsqrt(jnp.mean(xc * xc, axis=1, keepdims=True) + LN_EPS)
    xh = xc * rstd
    vn_bf = (xh * lg + lb).astype(BF16)
    grp = lax.broadcasted_iota(jnp.int32, (CHUNK, SGU_WIDTH), 1) // SGU_GROUP_DIM
    r_i = lax.broadcasted_iota(jnp.int32, (CHUNK, CHUNK), 0)
    c_i = lax.broadcasted_iota(jnp.int32, (CHUNK, CHUNK), 1)
    tri, tri_t = r_i >= c_i, r_i <= c_i
    mixed = bias
    for g in range(SGU_GROUPS):
        wt = jnp.where(tri, ws_ref[g], 0.0).astype(BF16)
        mixed = mixed + jnp.where(grp == g, _dot(wt, vn_bf, _NN), 0.0)
    sb = _sigmoid(zb)
    return ug, dug, dvg, rstd, xh, vn_bf, grp, tri, tri_t, mixed, sb


def _sgu_fwd(h_b, lg, lb, w_s, bias_full):
    def body(u_ref, v_ref, zb_ref, lg_ref, lb_ref, ws_ref, bias_ref, yb_ref):
        zb = zb_ref[...]
        ug, _, _, _, _, _, _, _, _, mixed, sb = _sgu_math(u_ref[...], v_ref[...], zb, lg_ref[...], lb_ref[...],
                                                       ws_ref, bias_ref[...])
        yb_ref[...] = (ug * mixed) * (zb * sb)

    blk = lambda c: pl.BlockSpec((CHUNK, SGU_WIDTH), lambda i, c=c: (i, c))
    full2 = lambda shape: pl.BlockSpec(shape, lambda i: (0, 0))
    return pl.pallas_call(
        body, name="sgu_fwd", grid=(SEQ // CHUNK,),
        in_specs=[blk(0), blk(1), blk(2), full2((1, SGU_WIDTH)), full2((1, SGU_WIDTH)),
                  pl.BlockSpec((SGU_GROUPS, CHUNK, CHUNK), lambda i: (0, 0, 0)), full2((CHUNK, SGU_WIDTH))],
        out_specs=pl.BlockSpec((CHUNK, SGU_WIDTH), lambda i: (i, 0)),
        out_shape=jax.ShapeDtypeStruct((SEQ, SGU_WIDTH), F32),
        compiler_params=pltpu.CompilerParams(dimension_semantics=("arbitrary",)),
    )(h_b, h_b, h_b, lg, lb, w_s, bias_full)


def _sgu_bwd(h_b, d_yb, lg, lb, w_s, w_st, bias_full, parts):
    nsteps = SEQ // CHUNK
    npart = len(parts)

    def body(u_ref, v_ref, zb_ref, dyb_ref, lg_ref, lb_ref, ws_ref, wst_ref, bias_ref, *rest):
        part_refs, rest = rest[:npart], rest[npart:]
        dhb_ref, dws_ref, dbs_ref, dlg_ref, dlb_ref, dbb_ref = rest[:6]
        land_refs, (dbias_acc, send_sems, recv_sems, local_sems) = rest[6:6 + npart], rest[6 + npart:]
        step = pl.program_id(0)
        exchange = _exchange_parts(part_refs, land_refs, send_sems, recv_sems, local_sems)
        _exchange_start(step == 0, exchange)

        @pl.when(step == 0)
        def _():
            dbb_ref[...] = jnp.zeros_like(dbb_ref)
            dws_ref[...] = jnp.zeros_like(dws_ref)
            dlg_ref[...] = jnp.zeros_like(dlg_ref)
            dlb_ref[...] = jnp.zeros_like(dlb_ref)
            dbias_acc[...] = jnp.zeros_like(dbias_acc)

        zb = zb_ref[...]
        lg = lg_ref[...]
        ug, dug, dvg, rstd, xh, vn_bf, grp, tri, tri_t, mixed, sb = _sgu_math(
            u_ref[...], v_ref[...], zb, lg, lb_ref[...], ws_ref, bias_ref[...])
        dyb = dyb_ref[...]
        dsgu = dyb * (zb * sb)
        dzb = dyb * (ug * mixed) * (sb * (1.0 + zb * (1.0 - sb)))
        du = dsgu * mixed * dug
        dmixed = dsgu * ug
        dbias_acc[...] += dmixed
        dvn = jnp.zeros((CHUNK, SGU_WIDTH), F32)
        for g in range(SGU_GROUPS):
            dm_g = jnp.where(grp == g, dmixed, 0.0).astype(BF16)
            wtt = jnp.where(tri_t, wst_ref[g], 0.0).astype(BF16)
            dvn = dvn + _dot(wtt, dm_g, _NN)
            dws_ref[g] += jnp.where(tri, _dot(dm_g, vn_bf, _NT), 0.0)
        dlg_ref[...] += jnp.sum(dvn * xh, axis=0, keepdims=True)
        dlb_ref[...] += jnp.sum(dvn, axis=0, keepdims=True)
        dxh = dvn * lg
        dvgel = rstd * (dxh - jnp.mean(dxh, axis=1, keepdims=True) - xh * jnp.mean(dxh * xh, axis=1, keepdims=True))
        _store_grad(dhb_ref, dbb_ref, 0, du)
        _store_grad(dhb_ref, dbb_ref, SGU_WIDTH, dvgel * dvg)
        _store_grad(dhb_ref, dbb_ref, 2 * SGU_WIDTH, dzb)

        @pl.when(step == nsteps - 1)
        def _():
            acc = dbias_acc[...]
            lane = lax.broadcasted_iota(jnp.int32, (CHUNK, LANES), 1)
            out = jnp.zeros((CHUNK, LANES), F32)
            for g in range(SGU_GROUPS):
                sg = jnp.sum(jnp.where(grp == g, acc, 0.0), axis=1, keepdims=True)
                out = jnp.where(lane == g, sg, out)
            dbs_ref[...] = out

        _exchange_finish(step == nsteps - 1, exchange)

    blk = lambda c: pl.BlockSpec((CHUNK, SGU_WIDTH), lambda i, c=c: (i, c))
    full2 = lambda shape: pl.BlockSpec(shape, lambda i: (0, 0))
    full3 = pl.BlockSpec((SGU_GROUPS, CHUNK, CHUNK), lambda i: (0, 0, 0))
    hbm = pl.BlockSpec(memory_space=pl.ANY)
    res = pl.pallas_call(
        body, name="sgu_bwd", grid=(nsteps,),
        in_specs=[blk(0), blk(1), blk(2), pl.BlockSpec((CHUNK, SGU_WIDTH), lambda i: (i, 0)),
                  full2((1, SGU_WIDTH)), full2((1, SGU_WIDTH)), full3, full3, full2((CHUNK, SGU_WIDTH))] + [hbm] * npart,
        out_specs=[pl.BlockSpec((CHUNK, SEG_B), lambda i: (i, 0)), full3, full2((CHUNK, LANES)),
                   full2((1, SGU_WIDTH)), full2((1, SGU_WIDTH)), full2((1, SEG_B))] + [hbm] * npart,
        out_shape=[jax.ShapeDtypeStruct((SEQ, SEG_B), BF16),
                   jax.ShapeDtypeStruct((SGU_GROUPS, CHUNK, CHUNK), F32),
                   jax.ShapeDtypeStruct((CHUNK, LANES), F32),
                   jax.ShapeDtypeStruct((1, SGU_WIDTH), F32), jax.ShapeDtypeStruct((1, SGU_WIDTH), F32),
                   jax.ShapeDtypeStruct((1, SEG_B), F32)] + [jax.ShapeDtypeStruct(p.shape, p.dtype) for p in parts],
        scratch_shapes=[pltpu.VMEM((CHUNK, SGU_WIDTH), F32)] + _exchange_sems(npart),
        compiler_params=pltpu.CompilerParams(dimension_semantics=("arbitrary",)),
    )(h_b, h_b, h_b, d_yb, lg, lb, w_s, w_st, bias_full, *parts)
    return res[:6], res[6:]


def _merge(x, o, h_a, y_b, target, w_oa, w_ob, w_out, ln_g, ln_b):
    tm = 256
    nsteps = SEQ // tm

    def body(x_ref, o_ref, ga_ref, gb_ref, za_ref, yb_ref, tgt_ref, woa_ref, wob_ref, wout_ref, lng_ref, lnb_ref,
             loss_ref, dxr_ref, dha_ref, do_ref, dyb_ref, poa_ref, pob_ref, pout_ref, dlng_ref, dlnb_ref, dba_ref,
             dwoa_ref, dwob_ref, dwout_ref):
        step = pl.program_id(0)

        @pl.when(step == 0)
        def _():
            for r in (loss_ref, dwoa_ref, dwob_ref, dwout_ref, dlng_ref, dlnb_ref, dba_ref):
                r[...] = jnp.zeros_like(r)

        o = o_ref[...]
        za = za_ref[...]
        sa = _sigmoid(za)
        ya_bf = (o * (za * sa)).astype(BF16)
        yb_bf = yb_ref[...].astype(BF16)
        woa, wob, wout = woa_ref[...], wob_ref[...], wout_ref[...]
        pa = _dot(ya_bf, woa, _NN)
        pb = _dot(yb_bf, wob, _NN)
        sga = _sigmoid(ga_ref[...])
        sgb = _sigmoid(gb_ref[...])
        merged_bf = (sga * pa + sgb * pb).astype(BF16)
        r = DN_ALPHA * x_ref[...] + _dot(merged_bf, wout, _NN)
        mu = jnp.mean(r, axis=1, keepdims=True)
        rc = r - mu
        rstd = lax.rsqrt(jnp.mean(rc * rc, axis=1, keepdims=True) + LN_EPS)
        xh = rc * rstd
        lng = lng_ref[...]
        y = xh * lng + lnb_ref[...]
        e = y - tgt_ref[...]
        loss_ref[...] += 0.5 * jnp.sum(jnp.sum(e * e, axis=1, keepdims=True) * (1.0 / D_MODEL), axis=0, keepdims=True)

        dy = e * (1.0 / D_MODEL)
        dlng_ref[...] += jnp.sum(dy * xh, axis=0, keepdims=True)
        dlnb_ref[...] += jnp.sum(dy, axis=0, keepdims=True)
        dxh = dy * lng
        dr = rstd * (dxh - jnp.mean(dxh, axis=1, keepdims=True) - xh * jnp.mean(dxh * xh, axis=1, keepdims=True))
        dxr_ref[...] = DN_ALPHA * dr
        dr_bf = dr.astype(BF16)
        dwout_ref[...] += _dot(merged_bf, dr_bf, _TN)
        dmerged = _dot(dr_bf, wout, _NT)
        dpa_bf = (dmerged * sga).astype(BF16)
        dpb_bf = (dmerged * sgb).astype(BF16)
        _store_grad(dha_ref, dba_ref, 0, dmerged * pa * (sga * (1.0 - sga)))
        _store_grad(dha_ref, dba_ref, D_MODEL, dmerged * pb * (sgb * (1.0 - sgb)))
        dwoa_ref[...] += _dot(ya_bf, dpa_bf, _TN)
        dwob_ref[...] += _dot(yb_bf, dpb_bf, _TN)
        dya = _dot(dpa_bf, woa, _NT)
        dyb_ref[...] = _dot(dpb_bf, wob, _NT)
        do_ref[...] = dya * (za * sa)
        _store_grad(dha_ref, dba_ref, 2 * D_MODEL, dya * o * (sa * (1.0 + za * (1.0 - sa))))

        @pl.when(step == nsteps - 1)
        def _():
            cols = D_MODEL // N_DEV
            for j in range(N_DEV):
                poa_ref[j] = dwoa_ref[:, cols * j:cols * (j + 1)].astype(BF16)
                pob_ref[j] = dwob_ref[:, cols * j:cols * (j + 1)].astype(BF16)
                pout_ref[j] = dwout_ref[cols * j:cols * (j + 1), :].astype(BF16)

    row = lambda w, c=0: pl.BlockSpec((tm, w), lambda i, c=c: (i, c))
    full = lambda shape: pl.BlockSpec(shape, lambda i: (0, 0))
    full3 = lambda shape: pl.BlockSpec(shape, lambda i: (0, 0, 0))
    return pl.pallas_call(
        body, name="merge", grid=(nsteps,),
        in_specs=[row(D_MODEL), row(MLA_WIDTH), row(D_MODEL, 0), row(D_MODEL, 1), row(MLA_WIDTH, 4), row(SGU_WIDTH),
                  row(D_MODEL), full((MLA_WIDTH, D_MODEL)), full((SGU_WIDTH, D_MODEL)), full((D_MODEL, D_MODEL)),
                  full((1, D_MODEL)), full((1, D_MODEL))],
        out_specs=[full((1, LANES)), row(D_MODEL), row(SEG_A), row(MLA_WIDTH), row(SGU_WIDTH),
                   full3((N_DEV, MLA_WIDTH, D_MODEL // N_DEV)), full3((N_DEV, SGU_WIDTH, D_MODEL // N_DEV)),
                   full3((N_DEV, D_MODEL // N_DEV, D_MODEL)), full((1, D_MODEL)), full((1, D_MODEL)), full((1, SEG_A))],
        out_shape=[jax.ShapeDtypeStruct((1, LANES), F32),
                   jax.ShapeDtypeStruct((SEQ, D_MODEL), F32), jax.ShapeDtypeStruct((SEQ, SEG_A), BF16),
                   jax.ShapeDtypeStruct((SEQ, MLA_WIDTH), F32), jax.ShapeDtypeStruct((SEQ, SGU_WIDTH), F32),
                   jax.ShapeDtypeStruct((N_DEV, MLA_WIDTH, D_MODEL // N_DEV), BF16),
                   jax.ShapeDtypeStruct((N_DEV, SGU_WIDTH, D_MODEL // N_DEV), BF16),
                   jax.ShapeDtypeStruct((N_DEV, D_MODEL // N_DEV, D_MODEL), BF16),
                   jax.ShapeDtypeStruct((1, D_MODEL), F32), jax.ShapeDtypeStruct((1, D_MODEL), F32),
                   jax.ShapeDtypeStruct((1, SEG_A), F32)],
        scratch_shapes=[pltpu.VMEM((MLA_WIDTH, D_MODEL), F32), pltpu.VMEM((SGU_WIDTH, D_MODEL), F32),
                        pltpu.VMEM((D_MODEL, D_MODEL), F32)],
        compiler_params=pltpu.CompilerParams(dimension_semantics=("arbitrary",), vmem_limit_bytes=VMEM_BIG),
    )(x, o, h_a, h_a, h_a, y_b, target, w_oa, w_ob, w_out, ln_g, ln_b)


def _mla_bwd(dq, dk, dv, h_c, xt_bf, gq, gkv, wq, wkn, wv, c_t, sa_t, sb_t, parts):
    tm = 256
    hw = MLA_HEADS * HEAD_PAD
    npart = len(parts)
    nsteps = SEQ // tm

    def body(dq_ref, dk_ref, dv_ref, cq_ref, ckv_ref, xt_ref, gq_ref, gkv_ref, wq_ref, wkn_ref, wv_ref, c_ref, sa_ref,
             sb_ref, *rest):
        part_refs, rest = rest[:npart], rest[npart:]
        dhc_ref, puq_ref, dwkn_ref, dwv_ref, dgq_ref, dgkv_ref, dbc_ref, dwc_ref = rest[:8]
        land_refs, (pre_ref, dwq_ref, dwc_acc, send_sems, recv_sems, local_sems) = rest[8:8 + npart], rest[8 + npart:]
        exchange = _exchange_parts(part_refs, land_refs, send_sems, recv_sems, local_sems)
        _exchange_start(pl.program_id(0) == 0, exchange)

        @pl.when(pl.program_id(0) == 0)
        def _():
            for r in (dwq_ref, dwc_acc, dwkn_ref, dwv_ref, dgq_ref, dgkv_ref, dbc_ref):
                r[...] = jnp.zeros_like(r)

        c, sa, sb = c_ref[...], sa_ref[...], sb_ref[...]
        lane = lax.broadcasted_iota(jnp.int32, (tm, LANES), 1)
        rope_lanes = jnp.logical_and(lane >= ROPE_LO, lane < ROPE_HI)

        cq = cq_ref[...]
        gq = gq_ref[...]
        rq = lax.rsqrt(jnp.sum(cq * cq, axis=1, keepdims=True) * (1.0 / Q_LORA_RANK) + RMS_EPS)
        nq = cq * rq
        cqn_bf = (nq * gq).astype(BF16)
        for h in range(MLA_HEADS):
            sl = slice(HEAD_PAD * h, HEAD_PAD * (h + 1))
            pre_ref[:, sl] = _rope_t(dq_ref[:, sl] * ATTN_SCALE, c, sa, sb).astype(BF16)
        dqpre_bf = pre_ref[...]
        dcqn = _dot(dqpre_bf, wq_ref[...], _NT)
        dwq_ref[...] += _dot(cqn_bf, dqpre_bf, _TN)
        dgq_ref[...] += jnp.sum(dcqn * nq, axis=0, keepdims=True)
        dnq = dcqn * gq
        _store_grad(dhc_ref, dbc_ref, 0,
                    rq * (dnq - nq * (jnp.sum(dnq * nq, axis=1, keepdims=True) * (1.0 / Q_LORA_RANK))))

        ckv = ckv_ref[...]
        gkv = gkv_ref[...]
        rkv = lax.rsqrt(jnp.sum(ckv * ckv, axis=1, keepdims=True) * (1.0 / KV_LORA_RANK) + RMS_EPS)
        nkv = ckv * rkv
        ckvn_bf = (nkv * gkv).astype(BF16)
        dk = dk_ref[...]
        dk_bf = dk.astype(BF16)
        dv_bf = dv_ref[...].astype(BF16)
        dckvn = _dot(dk_bf, wkn_ref[...], _NT) + _dot(dv_bf, wv_ref[...], _NT)
        dwkn_ref[...] += _dot(ckvn_bf, dk_bf, _TN)
        dwv_ref[...] += _dot(ckvn_bf, dv_bf, _TN)
        dgkv_ref[...] += jnp.sum(dckvn * nkv, axis=0, keepdims=True)
        dnkv = dckvn * gkv
        _store_grad(dhc_ref, dbc_ref, CQ_PAD, rkv * (
            dnkv - nkv * (jnp.sum(dnkv * nkv, axis=1, keepdims=True) * (1.0 / KV_LORA_RANK))))
        dkpe = jnp.zeros((tm, LANES), F32)
        for h in range(MLA_HEADS):
            dkpe = dkpe + dk[:, HEAD_PAD * h:HEAD_PAD * (h + 1)]
        _store_grad(dhc_ref, dbc_ref, CQ_PAD + LANES, _rope_t(jnp.where(rope_lanes, dkpe, 0.0), c, sa, sb))
        dwc_acc[...] += _dot(xt_ref[...], dhc_ref[...], _NN)

        @pl.when(pl.program_id(0) == SEQ // tm - 1)
        def _():
            dwc_ref[...] = dwc_acc[...].astype(BF16)
            rows = Q_LORA_RANK // N_DEV
            for j in range(N_DEV):
                for h in range(MLA_HEADS):
                    puq_ref[j, :, QK_HEAD_DIM * h:QK_HEAD_DIM * (h + 1)] = dwq_ref[
                        rows * j:rows * (j + 1), HEAD_PAD * h:HEAD_PAD * h + QK_HEAD_DIM].astype(BF16)

        _exchange_finish(pl.program_id(0) == nsteps - 1, exchange)

    full = lambda shape: pl.BlockSpec(shape, lambda i: (0, 0))
    row = lambda w, c=0: pl.BlockSpec((tm, w), lambda i, c=c: (i, c))
    hbm = pl.BlockSpec(memory_space=pl.ANY)
    res = pl.pallas_call(
        body, name="mla_bwd", grid=(nsteps,),
        in_specs=[row(hw), row(hw), row(hw), row(CQ_PAD, 0), row(LANES, CQ_PAD // LANES),
                  pl.BlockSpec((D_MODEL, tm), lambda i: (0, i)),
                  full((1, CQ_PAD)), full((1, KV_LORA_RANK)), full((CQ_PAD, hw)), full((KV_LORA_RANK, hw)),
                  full((KV_LORA_RANK, hw)), row(LANES), row(LANES), row(LANES)] + [hbm] * npart,
        out_specs=[row(SEG_C), pl.BlockSpec((N_DEV, Q_LORA_RANK // N_DEV, MLA_HEADS * QK_HEAD_DIM), lambda i: (0, 0, 0)),
                   full((KV_LORA_RANK, hw)), full((KV_LORA_RANK, hw)),
                   full((1, CQ_PAD)), full((1, KV_LORA_RANK)), full((1, SEG_C)), full((D_MODEL, SEG_C))] + [hbm] * npart,
        out_shape=[jax.ShapeDtypeStruct((SEQ, SEG_C), BF16),
                   jax.ShapeDtypeStruct((N_DEV, Q_LORA_RANK // N_DEV, MLA_HEADS * QK_HEAD_DIM), BF16),
                   jax.ShapeDtypeStruct((KV_LORA_RANK, hw), F32), jax.ShapeDtypeStruct((KV_LORA_RANK, hw), F32),
                   jax.ShapeDtypeStruct((1, CQ_PAD), F32), jax.ShapeDtypeStruct((1, KV_LORA_RANK), F32),
                   jax.ShapeDtypeStruct((1, SEG_C), F32), jax.ShapeDtypeStruct((D_MODEL, SEG_C), BF16)]
        + [jax.ShapeDtypeStruct(p.shape, p.dtype) for p in parts],
        scratch_shapes=[pltpu.VMEM((tm, hw), BF16), pltpu.VMEM((CQ_PAD, hw), F32), pltpu.VMEM((D_MODEL, SEG_C), F32)]
        + _exchange_sems(npart),
        compiler_params=pltpu.CompilerParams(dimension_semantics=("arbitrary",), vmem_limit_bytes=VMEM_MID),
    )(dq, dk, dv, h_c, h_c, xt_bf, gq, gkv, wq, wkn, wv, c_t, sa_t, sb_t, *parts)
    return res[:8], res[8:]


def _adamw_all(ws, gs, ms, vs):
    n = len(ws)
    c1 = 1.0 / (1.0 - ADAM_B1 ** ADAM_STEP)
    c2 = 1.0 / (1.0 - ADAM_B2 ** ADAM_STEP)

    def body(*refs):
        for idx in range(n):
            w, g, m, v = (refs[idx][...], refs[n + idx][...], refs[2 * n + idx][...], refs[3 * n + idx][...])
            m_new = ADAM_B1 * m + (1.0 - ADAM_B1) * g
            v_new = ADAM_B2 * v + (1.0 - ADAM_B2) * (g * g)
            delta = -ADAM_LR * ((m_new * c1) / (jnp.sqrt(v_new * c2) + ADAM_EPS) + ADAM_WD * w)
            refs[4 * n + idx][...] = delta
            refs[5 * n + idx][...] = m_new
            refs[6 * n + idx][...] = v_new

    shapes = [jax.ShapeDtypeStruct(w.shape, F32) for w in ws]
    outs = pl.pallas_call(
        body, name="adamw", out_shape=shapes * 3,
        compiler_params=pltpu.CompilerParams(vmem_limit_bytes=VMEM_BIG),
    )(*ws, *gs, *ms, *vs)
    return outs[:n], outs[n:2 * n], outs[2 * n:]


SHARD_W = IN_WIDTH // N_DEV

_PIECES = [(0, 384, 2, 0), (384, 512, 2, CQ_PAD), (512, 544, 2, CQ_PAD + LANES + ROPE_LO),
           (544, 1056, 0, 2 * D_MODEL), (1056, 1568, 1, 0), (1568, 2080, 1, SGU_WIDTH),
           (2080, 2592, 1, 2 * SGU_WIDTH), (2592, 3616, 0, 0), (3616, 4640, 0, D_MODEL)]


def _column_runs():
    runs = []
    for n0, n1, seg, d0 in _PIECES:
        for j in range(N_DEV):
            lo, hi = max(n0, j * SHARD_W), min(n1, (j + 1) * SHARD_W)
            if lo < hi:
                runs.append((j, lo - j * SHARD_W, hi - j * SHARD_W, seg, d0 + lo - n0))
    return runs


def _mesh_pos():
    return lax.axis_index("x"), lax.axis_index("y"), lax.axis_index("c")


def _remote(src, dst, send_sems, recv_sems, k, to):
    return pltpu.make_async_remote_copy(src_ref=src, dst_ref=dst, send_sem=send_sems.at[k], recv_sem=recv_sems.at[k],
                                        device_id=to, device_id_type=pl.DeviceIdType.MESH)


def _gather_exchange(gats, send_sems, recv_sems, meanwhile=None):
    x, y, c = _mesh_pos()
    me, sibling = (x, y, c), (x, y, 1 - c)
    chips = [(1 - x, y), (x, 1 - y), (1 - x, 1 - y)]

    def copy(a, k, blk, to):
        slab = gats[a].at[4 * blk[0] + 2 * blk[1] + blk[2]]
        return _remote(slab, slab, send_sems, recv_sems, 7 * a + k, to)

    arrays = range(len(gats))
    first = [copy(a, 1 + j, me, (*chip, c)) for j, chip in enumerate(chips) for a in arrays]
    first += [copy(a, 0, me, sibling) for a in arrays]
    for cp in first:
        cp.start()
    if meanwhile is not None:
        meanwhile()
    passed = []
    for j, chip in enumerate(chips):
        for a in arrays:
            copy(a, 1 + j, (*chip, c), me).wait_recv()
            fwd = copy(a, 4 + j, (*chip, c), sibling)
            fwd.start()
            passed.append(fwd)
    for a in arrays:
        copy(a, 0, sibling, me).wait_recv()
    for j, chip in enumerate(chips):
        for a in arrays:
            copy(a, 4 + j, (*chip, 1 - c), me).wait_recv()
    for cp in first + passed:
        cp.wait_send()


def _gather_behind(own, gats, send_sems, recv_sems, local_sems, step, mid, last):
    x, y, c = _mesh_pos()
    me, sibling = (x, y, c), (x, y, 1 - c)
    chips = [(1 - x, y), (x, 1 - y), (1 - x, 1 - y)]
    arrays = range(len(gats))

    def copy(a, k, blk, to, src=None):
        slab = gats[a].at[4 * blk[0] + 2 * blk[1] + blk[2]]
        return _remote(slab if src is None else src, slab, send_sems, recv_sems, 7 * a + k, to)

    first = [copy(a, 1 + j, me, (*chip, c), src=own[a]) for j, chip in enumerate(chips) for a in arrays]
    first += [copy(a, 0, me, sibling, src=own[a]) for a in arrays]
    local = [pltpu.make_async_copy(own[a], gats[a].at[4 * x + 2 * y + c], local_sems.at[a]) for a in arrays]
    passed = [copy(a, 4 + j, (*chip, c), sibling) for j, chip in enumerate(chips) for a in arrays]

    @pl.when(step == 0)
    def _():
        for cp in first + local:
            cp.start()

    @pl.when(step == mid)
    def _():
        for j, chip in enumerate(chips):
            for a in arrays:
                copy(a, 1 + j, (*chip, c), me).wait_recv()
        for cp in passed:
            cp.start()

    @pl.when(step == last)
    def _():
        for a in arrays:
            copy(a, 0, sibling, me).wait_recv()
        for j, chip in enumerate(chips):
            for a in arrays:
                copy(a, 4 + j, (*chip, 1 - c), me).wait_recv()
        for cp in first + passed:
            cp.wait_send()
        for cp in local:
            cp.wait()


def _gather_first(w_in, w_uq2, w_oa, w_ob, w_out, x2, pos_col, invf_lane):
    hw = MLA_HEADS * HEAD_PAD
    uq_rows = Q_LORA_RANK // N_DEV
    rows = 256

    def body(win_ref, wuq_ref, woa_ref, wob_ref, wout_ref, x_ref, pos_ref, invf_ref,
             wc_ref, wq_ref, winb_ref, oab_ref, obb_ref, outb_ref, xb_ref, xt_ref, c_ref, sa_ref, sb_ref,
             g_uq, blk0, send_sems, recv_sems):
        def local_work():
            for i in range(SEQ // rows):
                xi = x_ref[rows * i:rows * (i + 1), :]
                xb_ref[rows * i:rows * (i + 1), :] = xi.astype(BF16)
                xt_ref[:, rows * i:rows * (i + 1)] = xi.T.astype(BF16)
            ang = pos_ref[...].astype(F32) * invf_ref[...]
            cs, sn = jnp.cos(ang), jnp.sin(ang)
            lane = lax.broadcasted_iota(jnp.int32, ang.shape, 1)
            c_ref[...] = jnp.where(lane < ROPE_LO, 1.0, jnp.where(lane < ROPE_HI, cs, 0.0))
            sa_ref[...] = jnp.where(jnp.logical_and(lane >= ROPE_LO, lane < ROPE_MID), -sn, 0.0)
            sb_ref[...] = jnp.where(jnp.logical_and(lane >= ROPE_MID, lane < ROPE_HI), sn, 0.0)

        x, y, c = _mesh_pos()
        me = (x, y, c)
        winb_ref[...] = win_ref[0].astype(BF16)
        oab_ref[...] = woa_ref[0].astype(BF16)
        obb_ref[...] = wob_ref[0].astype(BF16)
        outb_ref[...] = wout_ref[0].astype(BF16)
        g_uq[4 * x + 2 * y + c] = wuq_ref[...].astype(BF16)

        chip0 = jnp.logical_and(x == 0, y == 0)
        south = c == 0
        half = D_MODEL // 2
        halves = [blk0.at[pl.ds(0, half)], blk0.at[pl.ds(half, half)]]

        def bcopy(k, to, part=None):
            ref = blk0 if part is None else halves[part]
            return _remote(ref, ref, send_sems, recv_sems, 7 + k, to)

        sends0 = [(0, (0, 0, 1), None), (1, (1, 0, 0), 0), (2, (0, 1, 0), 1), (3, (1, 0, 0), 1), (4, (0, 1, 0), 0)]

        @pl.when(jnp.logical_and(chip0, south))
        def _():
            blk0[...] = winb_ref[...]
            for k, to, part in sends0:
                bcopy(k, to, part).start()

        _gather_exchange([g_uq], send_sems, recv_sems, meanwhile=local_work)

        for (cx, cy), first_k, first_half, second_k in (((1, 0), 1, 0, 3), ((0, 1), 2, 1, 4)):
            @pl.when(jnp.logical_and(jnp.logical_and(x == cx, y == cy), south))
            def _(cx=cx, cy=cy, first_k=first_k, first_half=first_half, second_k=second_k):
                bcopy(first_k, me, first_half).wait_recv()
                onward = bcopy(5 + first_half, (1, 1, 0), first_half)
                onward.start()
                bcopy(second_k, me, 1 - first_half).wait_recv()
                north = bcopy(7, (cx, cy, 1))
                north.start()
                onward.wait_send()
                north.wait_send()

        @pl.when(jnp.logical_and(jnp.logical_and(x == 1, y == 1), south))
        def _():
            bcopy(5, me, 0).wait_recv()
            bcopy(6, me, 1).wait_recv()
            north = bcopy(7, (1, 1, 1))
            north.start()
            north.wait_send()

        @pl.when(jnp.logical_and(chip0, c == 1))
        def _():
            bcopy(0, me).wait_recv()

        @pl.when(jnp.logical_and(jnp.logical_not(chip0), c == 1))
        def _():
            bcopy(7, me).wait_recv()

        @pl.when(jnp.logical_and(chip0, south))
        def _():
            for k, to, part in sends0:
                bcopy(k, to, part).wait_send()

        for j, s0, s1, seg, d0 in _column_runs():
            if seg == 2:
                wc_ref[:, d0:d0 + (s1 - s0)] = blk0[:, s0:s1]
        zeros = lambda r, w: jnp.zeros((r, w), BF16)
        wc_ref[:, Q_LORA_RANK:CQ_PAD] = zeros(D_MODEL, CQ_PAD - Q_LORA_RANK)
        wc_ref[:, CQ_PAD + LANES:CQ_PAD + LANES + ROPE_LO] = zeros(D_MODEL, ROPE_LO)
        wc_ref[:, CQ_PAD + LANES + ROPE_HI:SEG_C] = zeros(D_MODEL, LANES - ROPE_HI)
        wq_ref[Q_LORA_RANK:CQ_PAD, :] = zeros(CQ_PAD - Q_LORA_RANK, hw)
        for h in range(MLA_HEADS):
            wq_ref[0:Q_LORA_RANK, HEAD_PAD * h + QK_HEAD_DIM:HEAD_PAD * (h + 1)] = zeros(Q_LORA_RANK, HEAD_PAD - QK_HEAD_DIM)
        for j in range(N_DEV):
            for h in range(MLA_HEADS):
                wq_ref[uq_rows * j:uq_rows * (j + 1), HEAD_PAD * h:HEAD_PAD * h + QK_HEAD_DIM] = g_uq[
                    j, :, QK_HEAD_DIM * h:QK_HEAD_DIM * (h + 1)]

    vmem = pl.BlockSpec(memory_space=pltpu.VMEM)
    return pl.pallas_call(
        body, name="gather_first",
        out_shape=[jax.ShapeDtypeStruct((D_MODEL, SEG_C), BF16), jax.ShapeDtypeStruct((CQ_PAD, hw), BF16),
                   jax.ShapeDtypeStruct(w_in.shape[1:], BF16), jax.ShapeDtypeStruct(w_oa.shape[1:], BF16),
                   jax.ShapeDtypeStruct(w_ob.shape[1:], BF16), jax.ShapeDtypeStruct(w_out.shape[1:], BF16),
                   jax.ShapeDtypeStruct((SEQ, D_MODEL), BF16), jax.ShapeDtypeStruct((D_MODEL, SEQ), BF16)]
        + [jax.ShapeDtypeStruct((SEQ, LANES), F32)] * 3,
        in_specs=[vmem] * 8, out_specs=[vmem] * 11,
        scratch_shapes=[pltpu.VMEM((N_DEV, uq_rows, MLA_HEADS * QK_HEAD_DIM), BF16), pltpu.VMEM((D_MODEL, SHARD_W), BF16),
                        pltpu.SemaphoreType.DMA((15,)), pltpu.SemaphoreType.DMA((15,))],
        compiler_params=pltpu.CompilerParams(vmem_limit_bytes=VMEM_BIG),
    )(w_in, w_uq2, w_oa, w_ob, w_out, x2, pos_col, invf_lane)


def _assemble_in(g_in):
    def body(g_ref, wa_ref, wb_ref):
        segs = [wa_ref, wb_ref]
        for j, s0, s1, seg, d0 in _column_runs():
            if seg < 2:
                segs[seg][:, d0:d0 + (s1 - s0)] = g_ref[j, :, s0:s1]

    return pl.pallas_call(
        body, name="assemble_in",
        out_shape=[jax.ShapeDtypeStruct((D_MODEL, SEG_A), BF16), jax.ShapeDtypeStruct((D_MODEL, SEG_B), BF16)],
        compiler_params=pltpu.CompilerParams(vmem_limit_bytes=VMEM_MID),
    )(g_in)


def _assemble_out(g_oa, g_ob, g_out):
    cols = D_MODEL // N_DEV

    def body(goa_ref, gob_ref, gout_ref, oa_ref, ob_ref, out_ref):
        for j in range(N_DEV):
            oa_ref[:, cols * j:cols * (j + 1)] = goa_ref[j]
            ob_ref[:, cols * j:cols * (j + 1)] = gob_ref[j]
            out_ref[cols * j:cols * (j + 1), :] = gout_ref[j]

    return pl.pallas_call(
        body, name="assemble_out",
        out_shape=[jax.ShapeDtypeStruct((MLA_WIDTH, D_MODEL), BF16), jax.ShapeDtypeStruct((SGU_WIDTH, D_MODEL), BF16),
                   jax.ShapeDtypeStruct((D_MODEL, D_MODEL), BF16)],
    )(g_oa, g_ob, g_out)


C_NAT = 544


P_IN_SPLIT = 896


def _to_parts(dwa, dwb):
    def body(dwa_ref, dwb_ref, phi_ref, plo_ref):
        phi_ref[0, :, 0:C_NAT] = jnp.zeros((P_IN_SPLIT, C_NAT), BF16)
        plo_ref[0, :, 0:C_NAT] = jnp.zeros((D_MODEL - P_IN_SPLIT, C_NAT), BF16)
        segs = [dwa_ref, dwb_ref]
        for j, s0, s1, seg, d0 in _column_runs():
            if seg < 2:
                phi_ref[j, :, s0:s1] = segs[seg][0:P_IN_SPLIT, d0:d0 + (s1 - s0)]
                plo_ref[j, :, s0:s1] = segs[seg][P_IN_SPLIT:D_MODEL, d0:d0 + (s1 - s0)]

    return pl.pallas_call(
        body, name="to_parts",
        out_shape=[jax.ShapeDtypeStruct((N_DEV, P_IN_SPLIT, SHARD_W), BF16),
                   jax.ShapeDtypeStruct((N_DEV, D_MODEL - P_IN_SPLIT, SHARD_W), BF16)],
        compiler_params=pltpu.CompilerParams(vmem_limit_bytes=VMEM_MID))(dwa, dwb)


def _dx_tail(dhs, ws, dx_res, dwc, p_uq, p_rep):
    ntile, sums_at = 8, 5
    tm = SEQ // ntile
    rep_rows = p_rep.shape[1]
    c_rows = D_MODEL // N_DEV
    spec = [((c_rows, C_NAT), BF16), (p_uq.shape[1:], BF16), ((rep_rows, LANES), F32)]
    n = len(spec)

    nseg = len(dhs)

    def body(*refs):
        dh_refs, w_refs = refs[:nseg], refs[nseg:2 * nseg]
        dxr_ref, dwc_ref, puq_ref, prep_ref, dx_ref, call_ref, guq_ref, repall_ref, pc_ref, c_all, rep_all = refs[
            2 * nseg:2 * nseg + 11]
        rest = refs[2 * nseg + 11:]
        ras, tbs, rbs = rest[0:n], rest[n:2 * n], rest[2 * n:3 * n]
        send_sems, recv_sems, gsend, grecv = rest[3 * n:]
        step = pl.program_id(0)
        x, y, c = _mesh_pos()
        me_idx = 4 * x + 2 * y + c
        me, sibling = (x, y, c), (x, y, 1 - c)
        others = [(1 - x, y), (x, 1 - y), (1 - x, 1 - y)]
        parts = [pc_ref, puq_ref, prep_ref]
        gats = [rep_all, c_all]

        def stage1(chip, a):
            return _remote(parts[a].at[2 * chip + (1 - c)], ras[a].at[chip], send_sems, recv_sems, 7 * a + chip, sibling)

        def stage2(k, a):
            cx, cy = others[k]
            return _remote(tbs[a].at[k], rbs[a].at[k], send_sems, recv_sems, 7 * a + 4 + k, (cx, cy, c))

        def gcopy(a, k, blk, to):
            slab = gats[a].at[4 * blk[0] + 2 * blk[1] + blk[2]]
            return _remote(slab, slab, gsend, grecv, 7 * a + k, to)

        def chip_sum(a, chip):
            return parts[a][2 * chip + c].astype(F32) + ras[a][chip].astype(F32)

        @pl.when(step == 0)
        def _():
            for j, s0, s1, seg, d0 in _column_runs():
                if seg == 2:
                    for r in range(N_DEV):
                        pc_ref[r, :, s0:s1] = dwc_ref[c_rows * r:c_rows * (r + 1), d0:d0 + (s1 - s0)]
            for chip in range(4):
                for a in range(n):
                    stage1(chip, a).start()

        @pl.when(step == 1)
        def _():
            for chip in range(4):
                for a in range(n):
                    stage1(chip, a).wait_recv()
            for k, (cx, cy) in enumerate(others):
                for a in range(n):
                    tbs[a][k] = chip_sum(a, 2 * cx + cy).astype(spec[a][1])
                    stage2(k, a).start()

        @pl.when(step == sums_at)
        def _():
            for k in range(3):
                for a in range(n):
                    stage2(k, a).wait_recv()
            sums = []
            for a in range(n):
                acc = chip_sum(a, 2 * x + y)
                for k in range(3):
                    acc = acc + rbs[a][k].astype(F32)
                sums.append(acc)
            c_all[me_idx] = sums[0].astype(BF16)
            guq_ref[...] = sums[1]
            rep_all[me_idx] = sums[2]
            for a in range(2):
                for j, chip in enumerate(others):
                    gcopy(a, 1 + j, me, (*chip, c)).start()
                gcopy(a, 0, me, sibling).start()

        acc = dxr_ref[...]
        for dh_ref, w_ref in zip(dh_refs, w_refs):
            acc = acc + _dot(dh_ref[...], w_ref[...], _NT)
        dx_ref[...] = acc

        @pl.when(step == ntile - 1)
        def _():
            for j, chip in enumerate(others):
                for a in range(2):
                    gcopy(a, 1 + j, (*chip, c), me).wait_recv()
                    gcopy(a, 4 + j, (*chip, c), sibling).start()
            for a in range(2):
                gcopy(a, 0, sibling, me).wait_recv()
                for j, chip in enumerate(others):
                    gcopy(a, 4 + j, (*chip, 1 - c), me).wait_recv()
            for a in range(2):
                gcopy(a, 0, me, sibling).wait_send()
                for j, chip in enumerate(others):
                    gcopy(a, 1 + j, me, (*chip, c)).wait_send()
                    gcopy(a, 4 + j, (*chip, c), sibling).wait_send()
            for a in range(n):
                for chip in range(4):
                    stage1(chip, a).wait_send()
                for k in range(3):
                    stage2(k, a).wait_send()
            call_ref[...] = c_all[...]
            repall_ref[...] = rep_all[...]

    row = lambda w: pl.BlockSpec((tm, w), lambda i: (i, 0))
    full = lambda shape: pl.BlockSpec(shape, lambda i: (0,) * len(shape))
    scratch = [pltpu.VMEM((N_DEV, c_rows, C_NAT), BF16), pltpu.VMEM((N_DEV, c_rows, C_NAT), BF16),
               pltpu.VMEM((N_DEV, rep_rows, LANES), F32)]
    for lead in (4, 3, 3):
        scratch += [pltpu.VMEM((lead,) + tuple(shape), dt) for shape, dt in spec]
    scratch += [pltpu.SemaphoreType.DMA((7 * n,)), pltpu.SemaphoreType.DMA((7 * n,)),
                pltpu.SemaphoreType.DMA((14,)), pltpu.SemaphoreType.DMA((14,))]
    return pl.pallas_call(
        body, name="dx_tail", grid=(SEQ // tm,),
        in_specs=[row(dh.shape[1]) for dh in dhs] + [full(w.shape) for w in ws]
        + [row(D_MODEL), full(dwc.shape), full(p_uq.shape), full(p_rep.shape)],
        out_specs=[row(D_MODEL), full((N_DEV, c_rows, C_NAT)), full(p_uq.shape[1:]), full((N_DEV, rep_rows, LANES))],
        out_shape=[jax.ShapeDtypeStruct((SEQ, D_MODEL), F32), jax.ShapeDtypeStruct((N_DEV, c_rows, C_NAT), BF16),
                   jax.ShapeDtypeStruct(p_uq.shape[1:], F32), jax.ShapeDtypeStruct((N_DEV, rep_rows, LANES), F32)],
        scratch_shapes=scratch,
        compiler_params=pltpu.CompilerParams(dimension_semantics=("arbitrary",), vmem_limit_bytes=VMEM_BIG),
    )(*dhs, *ws, dx_res, dwc, p_uq, p_rep)


def _sum_landed(landed, c_all):
    c_rows = D_MODEL // N_DEV

    def body(rhi_ref, rlo_ref, roa_ref, rob_ref, rout_ref, call_ref, gin_ref, goa_ref, gob_ref, gout_ref):
        def total(ref, sl):
            acc = ref[0, sl, :].astype(F32)
            for s in range(1, N_DEV):
                acc = acc + ref[s, sl, :].astype(F32)
            return acc

        x, y, c = _mesh_pos()
        dev0 = jnp.where(4 * x + 2 * y + c == 0, 1.0, 0.0)
        for j in range(N_DEV):
            sl = slice(c_rows * j, c_rows * (j + 1))
            below = c_rows * j < P_IN_SPLIT
            tot = total(rhi_ref, sl) if below else total(rlo_ref, slice(c_rows * j - P_IN_SPLIT, c_rows * (j + 1) - P_IN_SPLIT))
            gin_ref[0, sl, C_NAT:SHARD_W] = tot[:, C_NAT:SHARD_W]
            gin_ref[0, sl, 0:C_NAT] = tot[:, 0:C_NAT] + dev0 * call_ref[j].astype(F32)
        goa_ref[0] = total(roa_ref, slice(None))
        gob_ref[0] = total(rob_ref, slice(None))
        gout_ref[0] = total(rout_ref, slice(None))

    return pl.pallas_call(
        body, name="sum_landed",
        out_shape=[jax.ShapeDtypeStruct((1, D_MODEL, SHARD_W), F32)]
        + [jax.ShapeDtypeStruct((1,) + r.shape[1:], F32) for r in landed[2:]],
        compiler_params=pltpu.CompilerParams(vmem_limit_bytes=VMEM_MID),
    )(*landed, c_all)


_O_CQ, _O_CKV, _O_KPE, _O_ZA, _O_U, _O_V, _O_ZB, _O_GA, _O_GB = 0, 384, 512, 544, 1056, 1568, 2080, 2592, 3616


def _to_segments(w):
    z = lambda n: jnp.zeros(w.shape[:-1] + (n,), w.dtype)
    seg_a = jnp.concatenate([w[..., _O_GA:_O_GB], w[..., _O_GB:IN_WIDTH], w[..., _O_ZA:_O_U]], axis=-1)
    seg_b = jnp.concatenate([w[..., _O_U:_O_V], w[..., _O_V:_O_ZB], w[..., _O_ZB:_O_GA]], axis=-1)
    seg_c = jnp.concatenate([w[..., _O_CQ:_O_CKV], z(CQ_PAD - Q_LORA_RANK), w[..., _O_CKV:_O_KPE],
                             z(ROPE_LO), w[..., _O_KPE:_O_ZA], z(LANES - ROPE_HI)], axis=-1)
    return seg_a, seg_b, seg_c


def _from_segments(seg_a, seg_b, seg_c):
    kpe0 = CQ_PAD + LANES + ROPE_LO
    return jnp.concatenate([
        seg_c[..., 0:Q_LORA_RANK], seg_c[..., CQ_PAD:CQ_PAD + LANES], seg_c[..., kpe0:kpe0 + QK_ROPE_DIM],
        seg_a[..., 2 * D_MODEL:SEG_A], seg_b, seg_a[..., 0:2 * D_MODEL]], axis=-1)


def kernel(x, positions, w_in, b_in, g_q, w_uq, g_kv, w_ukv, w_oa, sgu_ln_g, sgu_ln_b, w_s, b_s, w_ob, w_out, ln_g, ln_b, loss_target, m_w_in, m_b_in, m_g_q, m_w_uq, m_g_kv, m_w_ukv, m_w_oa, m_sgu_ln_g, m_sgu_ln_b, m_w_s, m_b_s, m_w_ob, m_w_out, m_ln_g, m_ln_b, v_w_in, v_b_in, v_g_q, v_w_uq, v_g_kv, v_w_ukv, v_w_oa, v_sgu_ln_g, v_sgu_ln_b, v_w_s, v_b_s, v_w_ob, v_w_out, v_ln_g, v_ln_b):
    w_uq2 = w_uq[0].reshape(Q_LORA_RANK // N_DEV, MLA_HEADS * QK_HEAD_DIM)
    inv_freq = ROPE_THETA ** (-jnp.arange(0, QK_ROPE_DIM, 2, dtype=F32) / QK_ROPE_DIM)
    invf_lane = jnp.concatenate([jnp.zeros((ROPE_LO,), F32), inv_freq, inv_freq,
                                 jnp.zeros((LANES - ROPE_HI,), F32)]).reshape(1, LANES)
    first = _gather_first(w_in, w_uq2, w_oa, w_ob, w_out, x[0], positions.reshape(SEQ, 1), invf_lane)
    partials = _local_step(x[0], loss_target[0], first, b_in, g_q, g_kv, w_ukv, sgu_ln_g, sgu_ln_b, w_s, b_s, ln_g, ln_b)
    weights = dict(w_in=w_in, b_in=b_in, g_q=g_q, w_uq=w_uq, g_kv=g_kv, w_ukv=w_ukv, w_oa=w_oa, sgu_ln_g=sgu_ln_g,
                   sgu_ln_b=sgu_ln_b, w_s=w_s, b_s=b_s, w_ob=w_ob, w_out=w_out, ln_g=ln_g, ln_b=ln_b)
    moms = dict(w_in=m_w_in, b_in=m_b_in, g_q=m_g_q, w_uq=m_w_uq, g_kv=m_g_kv, w_ukv=m_w_ukv, w_oa=m_w_oa,
                sgu_ln_g=m_sgu_ln_g, sgu_ln_b=m_sgu_ln_b, w_s=m_w_s, b_s=m_b_s, w_ob=m_w_ob, w_out=m_w_out,
                ln_g=m_ln_g, ln_b=m_ln_b)
    vars_ = dict(w_in=v_w_in, b_in=v_b_in, g_q=v_g_q, w_uq=v_w_uq, g_kv=v_g_kv, w_ukv=v_w_ukv, w_oa=v_w_oa,
                 sgu_ln_g=v_sgu_ln_g, sgu_ln_b=v_sgu_ln_b, w_s=v_w_s, b_s=v_b_s, w_ob=v_w_ob, w_out=v_w_out,
                 ln_g=v_ln_g, ln_b=v_ln_b)
    return _reduce_and_update(partials, weights, moms, vars_)


def _local_step(x2, tgt, first, b_in, g_q, g_kv, w_ukv, sgu_ln_g, sgu_ln_b, w_s, b_s, ln_g, ln_b):
    wc, wq, win_b, oa_b, ob_b, out_b, x_bf, xt_bf, c_t, sa_t, sb_t = first
    ba, bb, bc = _to_segments(b_in)
    w_ukv_bf = w_ukv[0].astype(BF16)
    wkn = jnp.pad(w_ukv_bf[:, :, :QK_NOPE_DIM], ((0, 0), (0, 0), (0, HEAD_PAD - QK_NOPE_DIM))).reshape(KV_LORA_RANK, -1)
    wv = jnp.pad(w_ukv_bf[:, :, QK_NOPE_DIM:], ((0, 0), (0, 0), (0, HEAD_PAD - V_HEAD_DIM))).reshape(KV_LORA_RANK, -1)
    gq = jnp.pad(g_q, ((0, 0), (0, CQ_PAD - Q_LORA_RANK)))
    bias_full = jnp.repeat(b_s[0].T, SGU_GROUP_DIM, axis=1)
    w_s3 = w_s[0]
    w_st3 = jnp.swapaxes(w_s3, 1, 2)

    h_c = _mm(x_bf, wc, bias=bc, tm=512, tn=SEG_C, name="in_proj_c")
    q, k, kt, vx, vxt = _mla_prep(h_c, gq, g_kv, wq, wkn, wv, c_t, sa_t, sb_t)
    o, lse, (g_in,) = _attn_fwd(q, kt, vx, (win_b,))
    wa, wb = _assemble_in(g_in)
    h_a, (g_out,) = _mm(x_bf, wa, bias=ba, own=(out_b,), tm=512, tn=SEG_A // 2, name="in_proj_a")
    h_b, (g_oa, g_ob) = _mm(x_bf, wb, bias=bb, own=(oa_b, ob_b), tm=512, tn=SEG_B // 2, name="in_proj_b")
    y_b = _sgu_fwd(h_b, sgu_ln_g, sgu_ln_b, w_s3, bias_full)
    w_oa_f, w_ob_f, w_out_f = _assemble_out(g_oa, g_ob, g_out)

    (loss_row, dx_res, dh_a, d_o, d_yb, p_oa, p_ob, p_out, d_lng, d_lnb, d_ba) = _merge(
        x2, o, h_a, y_b, tgt, w_oa_f, w_ob_f, w_out_f, ln_g, ln_b)
    (dh_b, d_ws, d_bs_t, d_slg, d_slb, d_bb), (r_out,) = _sgu_bwd(h_b, d_yb, sgu_ln_g, sgu_ln_b, w_s3, w_st3, bias_full,
                                                                 (p_out,))
    d_wa, (r_oa, r_ob) = _mm(xt_bf, dh_a, out_dtype=BF16, parts=(p_oa, p_ob), tm=512, tn=512, name="dw_in_a")
    d_wb = _mm(xt_bf, dh_b, out_dtype=BF16, tm=512, tn=512, name="dw_in_b")
    p_hi, p_lo = _to_parts(d_wa, d_wb)
    dq, dk, dv, (r_hi,) = _attn_bwd(q, kt, k, vxt, d_o, o, lse, (p_hi,))
    (dh_c, p_uq, d_wkn, d_wv, d_gq, d_gkv, d_bc, d_wc), (r_lo,) = _mla_bwd(
        dq, dk, dv, h_c, xt_bf, gq, g_kv, wq, wkn, wv, c_t, sa_t, sb_t, (p_lo,))
    landed = (r_hi, r_lo, r_oa, r_ob, r_out)

    p_b_in = _from_segments(d_ba, d_bb, d_bc)
    p_w_ukv = jnp.concatenate([d_wkn.reshape(KV_LORA_RANK, MLA_HEADS, HEAD_PAD)[:, :, :QK_NOPE_DIM],
                               d_wv.reshape(KV_LORA_RANK, MLA_HEADS, HEAD_PAD)[:, :, :V_HEAD_DIM]], axis=-1)
    p_g_q = d_gq[:, :Q_LORA_RANK]
    p_b_s = d_bs_t[:, :SGU_GROUPS].T
    replicated = [p_b_in, p_g_q, d_gkv, p_w_ukv, d_slg, d_slb, d_ws, p_b_s, d_lng, d_lnb]
    return loss_row, ((dh_a, dh_b, dh_c), (wa, wb, wc), dx_res), landed, d_wc, p_uq, replicated


_NAMES = ["w_in", "b_in", "g_q", "w_uq", "g_kv", "w_ukv", "w_oa", "sgu_ln_g", "sgu_ln_b", "w_s", "b_s", "w_ob",
          "w_out", "ln_g", "ln_b"]
_REPLICATED = ["b_in", "g_q", "g_kv", "w_ukv", "sgu_ln_g", "sgu_ln_b", "w_s", "b_s", "ln_g", "ln_b"]


def _reduce_and_update(partials, weights, moms, vars_):
    loss_row, (dhs, ws, dx_res), landed, d_wc, p_uq, replicated = partials
    def piece(a):
        flat = a.reshape(-1)
        return jnp.pad(flat, (0, -flat.size % PACK_ALIGN))

    rep_flat = jnp.concatenate([piece(a) for a in replicated] + [piece(loss_row[0, :1])])
    rep_flat = jnp.pad(rep_flat, (0, N_DEV * PACK_R_ROWS * LANES - rep_flat.size))
    dx, c_all, g_uq, rep_all = _dx_tail(dhs, ws, dx_res, d_wc, p_uq, rep_flat.reshape(N_DEV, PACK_R_ROWS, LANES))
    g_in, g_oa, g_ob, g_out = _sum_landed(landed, c_all)
    rep_sum = rep_all.reshape(-1)
    grads, pos = dict(w_in=g_in, w_uq=g_uq, w_oa=g_oa, w_ob=g_ob, w_out=g_out), 0
    for nm in _REPLICATED:
        grads[nm] = rep_sum[pos:pos + weights[nm].size]
        pos += weights[nm].size + -weights[nm].size % PACK_ALIGN
    loss = rep_sum[pos]
    grads = {nm: grads[nm].reshape(weights[nm].shape) for nm in _NAMES}
    deltas, new_m, new_v = _adamw_all([weights[nm] for nm in _NAMES], [grads[nm] for nm in _NAMES],
                                      [moms[nm] for nm in _NAMES], [vars_[nm] for nm in _NAMES])
    return (loss, dx.reshape(1, SEQ, D_MODEL), *[grads[nm] for nm in _NAMES], *deltas, *new_m, *new_v)
```

```python
import math

import jax
import jax.numpy as jnp
from jax import lax
from jax.experimental import pallas as pl
from jax.experimental.pallas import tpu as pltpu

F32 = jnp.float32
BF16 = jnp.bfloat16

D_MODEL = 1024
SEQ = 2048
N_DEV = 8
MLA_HEADS = 8
Q_LORA_RANK = 384
KV_LORA_RANK = 128
QK_NOPE_DIM = 64
QK_ROPE_DIM = 32
V_HEAD_DIM = 64
QK_HEAD_DIM = QK_NOPE_DIM + QK_ROPE_DIM
MLA_WIDTH = MLA_HEADS * V_HEAD_DIM
ROPE_THETA = 10000.0
SGU_GROUPS = 8
SGU_GROUP_DIM = 64
SGU_WIDTH = SGU_GROUPS * SGU_GROUP_DIM
CHUNK = 128
RMS_EPS = 1e-6
LN_EPS = 1e-5
DN_ALPHA = 2.0 ** 0.25
IN_WIDTH = 4640
ATTN_SCALE = QK_HEAD_DIM ** -0.5

ADAM_LR = 0.001
ADAM_B1 = 0.9
ADAM_B2 = 0.999
ADAM_EPS = 1e-08
ADAM_WD = 0.01
ADAM_STEP = 10

LANES = 128
HEAD_PAD = 128
ROPE_LO = QK_NOPE_DIM
ROPE_MID = ROPE_LO + QK_ROPE_DIM // 2
ROPE_HI = ROPE_LO + QK_ROPE_DIM
CQ_PAD = 512

SEG_A = 2560
SEG_B = 1536
SEG_C = 768

PACK_R_ROWS = 272
PACK_ALIGN = 8 * LANES
VMEM_BIG = 56 * 1024 * 1024
VMEM_MID = 40 * 1024 * 1024


def _sigmoid(x):
    return 1.0 / (1.0 + jnp.exp(-x))


def _gelu_and_grad(x):
    c0 = math.sqrt(2.0 / math.pi)
    x2 = x * x
    t = jnp.tanh(c0 * (x + 0.044715 * x * x2))
    g = 0.5 * x * (1.0 + t)
    dg = 0.5 * (1.0 + t) + 0.5 * x * (1.0 - t * t) * (c0 * (1.0 + 3.0 * 0.044715 * x2))
    return g, dg


def _dot(a, b, dims):
    return lax.dot_general(a, b, (dims, ((), ())), preferred_element_type=F32)


_NN = ((1,), (0,))
_NT = ((1,), (1,))
_TN = ((0,), (0,))


def _store_grad(dh_ref, db_ref, col, val):
    cols = slice(col, col + val.shape[1])
    dh_ref[:, cols] = val.astype(BF16)
    db_ref[:, cols] += jnp.sum(val, axis=0, keepdims=True)


def _mm(a, b, *, tb=False, bias=None, add=None, out_dtype=F32, own=(), parts=(), tm, tn, name):
    m, k = a.shape
    n = b.shape[0] if tb else b.shape[1]
    assert m % tm == 0 and n % tn == 0 and not (own and parts)
    dims = _NT if tb else _NN
    nown = len(own) + len(parts)
    nm = m // tm
    nsteps = (n // tn) * nm

    def body(*refs):
        a_ref, b_ref = refs[0], refs[1]
        pos = 2
        r = _dot(a_ref[...], b_ref[...], dims)
        if bias is not None:
            r = r + refs[pos][...]; pos += 1
        if add is not None:
            r = r + refs[pos][...]; pos += 1
        own_refs = refs[pos:pos + nown]; pos += nown
        refs[pos][...] = r.astype(out_dtype)
        if nown:
            gat_refs = refs[pos + 1:pos + 1 + nown]
            send_sems, recv_sems, local_sems = refs[pos + 1 + nown:]
            step = pl.program_id(0) * nm + pl.program_id(1)
            if own:
                _gather_behind(own_refs, gat_refs, send_sems, recv_sems, local_sems, step, nsteps - 2, nsteps - 1)
            else:
                exchange = _exchange_parts(own_refs, gat_refs, send_sems, recv_sems, local_sems)
                _exchange_start(step == 0, exchange)
                _exchange_finish(step == nsteps - 1, exchange)

    b_spec = pl.BlockSpec((tn, k), lambda j, i: (j, 0)) if tb else pl.BlockSpec((k, tn), lambda j, i: (0, j))
    in_specs, args = [pl.BlockSpec((tm, k), lambda j, i: (i, 0)), b_spec], [a, b]
    if bias is not None:
        in_specs.append(pl.BlockSpec((1, tn), lambda j, i: (0, j))); args.append(bias)
    if add is not None:
        in_specs.append(pl.BlockSpec((tm, tn), lambda j, i: (i, j))); args.append(add)
    hbm = pl.BlockSpec(memory_space=pl.ANY)
    res = pl.pallas_call(
        body, name=name, grid=(n // tn, nm), in_specs=in_specs + [hbm] * nown,
        out_specs=[pl.BlockSpec((tm, tn), lambda j, i: (i, j))] + [hbm] * nown,
        out_shape=[jax.ShapeDtypeStruct((m, n), out_dtype)]
        + [jax.ShapeDtypeStruct((N_DEV,) + o.shape, o.dtype) for o in own]
        + [jax.ShapeDtypeStruct(p.shape, p.dtype) for p in parts],
        scratch_shapes=_exchange_sems(nown) if nown else [],
        compiler_params=pltpu.CompilerParams(dimension_semantics=("arbitrary", "arbitrary"), vmem_limit_bytes=VMEM_BIG),
    )(*args, *own, *parts)
    return (res[0], res[1:]) if nown else res[0]


def _rope(x, c, sa, sb):
    return x * c + pltpu.roll(x, LANES - 16, 1) * sa + pltpu.roll(x, 16, 1) * sb


def _rope_t(dy, c, sa, sb):
    return dy * c + pltpu.roll(dy * sa, 16, 1) + pltpu.roll(dy * sb, LANES - 16, 1)


def _mla_prep(h_c, gq, gkv, wq, wkn, wvx, c_t, sa_t, sb_t):
    tm = 256
    hw = MLA_HEADS * HEAD_PAD

    def body(cq_ref, ckv_ref, kpe_ref, gq_ref, gkv_ref, wq_ref, wkn_ref, wvx_ref, c_ref, sa_ref, sb_ref,
             q_ref, k_ref, kt_ref, vx_ref, vxt_ref):
        c, sa, sb = c_ref[...], sa_ref[...], sb_ref[...]
        cq = cq_ref[...]
        rq = lax.rsqrt(jnp.sum(cq * cq, axis=1, keepdims=True) * (1.0 / Q_LORA_RANK) + RMS_EPS)
        cqn = ((cq * rq) * gq_ref[...]).astype(BF16)
        qall = _dot(cqn, wq_ref[...], _NN)
        for h in range(MLA_HEADS):
            sl = slice(HEAD_PAD * h, HEAD_PAD * (h + 1))
            q_ref[:, sl] = (_rope(qall[:, sl], c, sa, sb) * ATTN_SCALE).astype(BF16)
        ckv = ckv_ref[...]
        rkv = lax.rsqrt(jnp.sum(ckv * ckv, axis=1, keepdims=True) * (1.0 / KV_LORA_RANK) + RMS_EPS)
        ckvn = ((ckv * rkv) * gkv_ref[...]).astype(BF16)
        knall = _dot(ckvn, wkn_ref[...], _NN)
        vall = _dot(ckvn, wvx_ref[...], _NN)
        kper = _rope(kpe_ref[...], c, sa, sb)
        ones_half = (lax.broadcasted_iota(jnp.int32, (tm, HEAD_PAD), 1) >= V_HEAD_DIM).astype(F32)
        for h in range(MLA_HEADS):
            sl = slice(HEAD_PAD * h, HEAD_PAD * (h + 1))
            kh = knall[:, sl] + kper
            vh = vall[:, sl] + ones_half
            k_ref[:, sl] = kh.astype(BF16)
            kt_ref[sl, :] = kh.T.astype(BF16)
            vx_ref[:, sl] = vh.astype(BF16)
            vxt_ref[sl, :] = vh.T.astype(BF16)

    full = lambda shape: pl.BlockSpec(shape, lambda i: (0, 0))
    tab = pl.BlockSpec((tm, LANES), lambda i: (i, 0))
    row = pl.BlockSpec((tm, hw), lambda i: (i, 0))
    col = pl.BlockSpec((hw, tm), lambda i: (0, i))
    return pl.pallas_call(
        body, name="mla_prep", grid=(SEQ // tm,),
        in_specs=[pl.BlockSpec((tm, CQ_PAD), lambda i: (i, 0)),
                  pl.BlockSpec((tm, LANES), lambda i: (i, CQ_PAD // LANES)),
                  pl.BlockSpec((tm, LANES), lambda i: (i, CQ_PAD // LANES + 1)),
                  full((1, CQ_PAD)), full((1, KV_LORA_RANK)),
                  full((CQ_PAD, hw)), full((KV_LORA_RANK, hw)), full((KV_LORA_RANK, hw)), tab, tab, tab],
        out_specs=[row, row, col, row, col],
        out_shape=[jax.ShapeDtypeStruct((SEQ, hw), BF16), jax.ShapeDtypeStruct((SEQ, hw), BF16),
                   jax.ShapeDtypeStruct((hw, SEQ), BF16), jax.ShapeDtypeStruct((SEQ, hw), BF16),
                   jax.ShapeDtypeStruct((hw, SEQ), BF16)],
        compiler_params=pltpu.CompilerParams(dimension_semantics=("arbitrary",), vmem_limit_bytes=VMEM_MID),
    )(h_c, h_c, h_c, gq, gkv, wq, wkn, wvx, c_t, sa_t, sb_t)


ATT_T = 512
ATT_STRIP = 64


def _attn_fwd(q, kt, vx, own):
    t, rs = ATT_T, ATT_STRIP
    nown = len(own)
    nq = SEQ // t
    nsteps = (MLA_HEADS // 2) * nq

    def body(q_ref, kt_ref, vx_ref, *rest):
        own_refs, (o_ref, l_ref), gat_refs = rest[:nown], rest[nown:nown + 2], rest[nown + 2:2 * nown + 2]
        s_scr, p_scr, m_scr, a_scr, acc_scr, send_sems, recv_sems, local_sems = rest[2 * nown + 2:]
        qi = pl.program_id(1)
        lane = lax.broadcasted_iota(jnp.int32, (t, LANES), 1)
        m_scr[...] = jnp.full((2, t, LANES), -1e30, F32)
        acc_scr[...] = jnp.zeros((2, t, LANES), F32)

        def block(j, masked):
            off = pl.multiple_of(j * t, t)
            for a in range(2):
                sl = slice(HEAD_PAD * a, HEAD_PAD * (a + 1))
                s_scr[a] = _dot(q_ref[:, sl], kt_ref[sl, pl.ds(off, t)], _NN)
                for r in range(t // rs):
                    rows = slice(rs * r, rs * (r + 1))
                    s = s_scr[a, rows, :]
                    if masked:
                        rowi = lax.broadcasted_iota(jnp.int32, (rs, t), 0) + rs * r
                        coli = lax.broadcasted_iota(jnp.int32, (rs, t), 1)
                        s = jnp.where(coli <= rowi, s, -1e30)
                    m_old = m_scr[a, rows, :]
                    m_new = jnp.maximum(m_old, jnp.max(s, axis=1, keepdims=True))
                    p_scr[a, rows, :] = jnp.exp(s - m_new[:, :1]).astype(BF16)
                    a_scr[a, rows, :] = jnp.exp(m_old - m_new)
                    m_scr[a, rows, :] = m_new
                acc_scr[a] = acc_scr[a] * a_scr[a] + _dot(p_scr[a], vx_ref[pl.ds(off, t), sl], _NN)

        def step(j, carry):
            block(j, False)
            return carry
        lax.fori_loop(0, qi, step, 0)
        block(qi, True)
        res = []
        for a in range(2):
            acc = acc_scr[a]
            l = acc[:, V_HEAD_DIM:V_HEAD_DIM + 1]
            res.append((acc / l, m_scr[a] + jnp.log(l)))
        o_ref[...] = jnp.where(lane < V_HEAD_DIM, res[0][0], pltpu.roll(res[1][0], V_HEAD_DIM, 1))
        l_ref[...] = jnp.where(lane < V_HEAD_DIM, res[0][1], res[1][1])
        _gather_behind(own_refs, gat_refs, send_sems, recv_sems, local_sems, pl.program_id(0) * nq + qi,
                       nsteps - 2, nsteps - 1)

    hbm = pl.BlockSpec(memory_space=pl.ANY)
    res = pl.pallas_call(
        body, name="attn_fwd", grid=(MLA_HEADS // 2, nq),
        in_specs=[pl.BlockSpec((t, 2 * HEAD_PAD), lambda p, i: (i, p)),
                  pl.BlockSpec((2 * HEAD_PAD, SEQ), lambda p, i: (p, 0)),
                  pl.BlockSpec((SEQ, 2 * HEAD_PAD), lambda p, i: (0, p))] + [hbm] * nown,
        out_specs=[pl.BlockSpec((t, LANES), lambda p, i: (i, p)),
                   pl.BlockSpec((t, LANES), lambda p, i: (i, p))] + [hbm] * nown,
        out_shape=[jax.ShapeDtypeStruct((SEQ, MLA_WIDTH), F32), jax.ShapeDtypeStruct((SEQ, MLA_WIDTH), F32)]
        + [jax.ShapeDtypeStruct((N_DEV,) + a.shape, a.dtype) for a in own],
        scratch_shapes=[pltpu.VMEM((2, t, t), F32), pltpu.VMEM((2, t, t), BF16), pltpu.VMEM((2, t, LANES), F32),
                        pltpu.VMEM((2, t, LANES), F32), pltpu.VMEM((2, t, LANES), F32)] + _exchange_sems(nown),
        compiler_params=pltpu.CompilerParams(dimension_semantics=("arbitrary", "arbitrary"), vmem_limit_bytes=VMEM_MID),
    )(q, kt, vx, *own)
    return res[0], res[1], res[2:]


def _exchange_parts(parts, lands, send_sems, recv_sems, local_sems):
    x, y, c = _mesh_pos()
    me = 4 * x + 2 * y + c
    peers = [(x, y, 1 - c), (1 - x, y, c), (x, 1 - y, c), (1 - x, 1 - y, c),
             (1 - x, y, 1 - c), (x, 1 - y, 1 - c), (1 - x, 1 - y, 1 - c)]
    remote, local = [], []
    for a, (part, land) in enumerate(zip(parts, lands)):
        for k, peer in enumerate(peers):
            t = 4 * peer[0] + 2 * peer[1] + peer[2]
            remote.append(_remote(part.at[t], land.at[me], send_sems, recv_sems, 7 * a + k, peer))
        local.append(pltpu.make_async_copy(part.at[me], land.at[me], local_sems.at[a]))
    return remote, local


def _exchange_start(first_step, exchange):
    remote, local = exchange

    @pl.when(first_step)
    def _():
        for cp in remote + local:
            cp.start()


def _exchange_finish(last_step, exchange):
    remote, local = exchange

    @pl.when(last_step)
    def _():
        for cp in remote:
            cp.wait_recv()
        for cp in remote:
            cp.wait_send()
        for cp in local:
            cp.wait()


def _exchange_sems(npart):
    return [pltpu.SemaphoreType.DMA((7 * npart,)), pltpu.SemaphoreType.DMA((7 * npart,)),
            pltpu.SemaphoreType.DMA((npart,))]


def _attn_bwd(q, kt, k, vxt, d_o, o, lse, parts):
    t, rs = ATT_T, ATT_STRIP
    nq = SEQ // t
    npart = len(parts)
    nsteps = MLA_HEADS // 2

    def body(q_ref, kt_ref, k_ref, vxt_ref, do_ref, o_ref, l_ref, *rest):
        part_refs, rest = rest[:npart], rest[npart:]
        dq_ref, dk_ref, dv_ref = rest[:3]
        land_refs, rest = rest[3:3 + npart], rest[3 + npart:]
        s_scr, dp_scr, p_scr, ds_scr, st_scr, send_sems, recv_sems, local_sems = rest
        exchange = _exchange_parts(part_refs, land_refs, send_sems, recv_sems, local_sems)
        _exchange_start(pl.program_id(0) == 0, exchange)
        dk_ref[...] = jnp.zeros_like(dk_ref)
        dv_ref[...] = jnp.zeros_like(dv_ref)
        lane = lax.broadcasted_iota(jnp.int32, (t, LANES), 1)

        def qtile(i, carry):
            ioff = pl.multiple_of(i * t, t)
            do_i = do_ref[pl.ds(ioff, t), :]
            o_i = o_ref[pl.ds(ioff, t), :]
            l_i = l_ref[pl.ds(ioff, t), :]
            for a in range(2):
                sl = slice(HEAD_PAD * a, HEAD_PAD * (a + 1))
                sel = (lane < V_HEAD_DIM) if a == 0 else (lane >= V_HEAD_DIM)
                doa = jnp.where(sel, do_i, 0.0)
                oa = o_i
                if a == 1:
                    doa = pltpu.roll(doa, V_HEAD_DIM, 1)
                    oa = pltpu.roll(o_i, V_HEAD_DIM, 1)
                st_scr[0] = jnp.broadcast_to(jnp.sum(doa * oa, axis=1, keepdims=True), (t, LANES))
                st_scr[1] = jnp.broadcast_to(l_i[:, V_HEAD_DIM * a:V_HEAD_DIM * a + 1], (t, LANES))
                doa_bf = doa.astype(BF16)
                qa = q_ref[pl.ds(ioff, t), sl]

                def block(j, masked, dq_acc, sl=sl, qa=qa, doa_bf=doa_bf):
                    joff = pl.multiple_of(j * t, t)
                    s_scr[...] = _dot(qa, kt_ref[sl, pl.ds(joff, t)], _NN)
                    dp_scr[...] = _dot(doa_bf, vxt_ref[sl, pl.ds(joff, t)], _NN)
                    for r in range(t // rs):
                        rows = slice(rs * r, rs * (r + 1))
                        p = jnp.exp(s_scr[rows, :] - st_scr[1, rows, :1])
                        if masked:
                            rowi = lax.broadcasted_iota(jnp.int32, (rs, t), 0) + rs * r
                            coli = lax.broadcasted_iota(jnp.int32, (rs, t), 1)
                            p = jnp.where(coli <= rowi, p, 0.0)
                        p_scr[rows, :] = p.astype(BF16)
                        ds_scr[rows, :] = (p * (dp_scr[rows, :] - st_scr[0, rows, :1])).astype(BF16)
                    dk_ref[pl.ds(joff, t), sl] += _dot(ds_scr[...], qa, _TN)
                    dv_ref[pl.ds(joff, t), sl] += _dot(p_scr[...], doa_bf, _TN)
                    return dq_acc + _dot(ds_scr[...], k_ref[pl.ds(joff, t), sl], _NN)

                dq_acc = lax.fori_loop(0, i, lambda j, acc: block(j, False, acc), jnp.zeros((t, HEAD_PAD), F32))
                dq_ref[pl.ds(ioff, t), sl] = block(i, True, dq_acc)
            return carry

        lax.fori_loop(0, nq, qtile, 0)
        _exchange_finish(pl.program_id(0) == nsteps - 1, exchange)

    hw = MLA_HEADS * HEAD_PAD
    wide = pl.BlockSpec((SEQ, 2 * HEAD_PAD), lambda p: (0, p))
    wide_t = pl.BlockSpec((2 * HEAD_PAD, SEQ), lambda p: (p, 0))
    narrow = pl.BlockSpec((SEQ, LANES), lambda p: (0, p))
    hbm = pl.BlockSpec(memory_space=pl.ANY)
    res = pl.pallas_call(
        body, name="attn_bwd", grid=(nsteps,),
        in_specs=[wide, wide_t, wide, wide_t, narrow, narrow, narrow] + [hbm] * npart,
        out_specs=[wide, wide, wide] + [hbm] * npart,
        out_shape=[jax.ShapeDtypeStruct((SEQ, hw), F32)] * 3 + [jax.ShapeDtypeStruct(p.shape, p.dtype) for p in parts],
        scratch_shapes=[pltpu.VMEM((t, t), F32), pltpu.VMEM((t, t), F32), pltpu.VMEM((t, t), BF16),
                        pltpu.VMEM((t, t), BF16), pltpu.VMEM((2, t, LANES), F32)] + _exchange_sems(npart),
        compiler_params=pltpu.CompilerParams(dimension_semantics=("arbitrary",), vmem_limit_bytes=VMEM_BIG),
    )(q, kt, k, vxt, d_o, o, lse, *parts)
    return res[0], res[1], res[2], res[3:]


def _sgu_math(u, v, zb, lg, lb, ws_ref, bias):
    ug, dug = _gelu_and_grad(u)
    vg, dvg = _gelu_and_grad(v)
    mu = jnp.mean(vg, axis=1, keepdims=True)
    xc = vg - mu
    rstd = lax.rsqrt(jnp.mean(xc * xc, axis=1, keepdims=True) + LN_EPS)
    xh = xc * rstd
    vn_bf = (xh * lg + lb).astype(BF16)
    grp = lax.broadcasted_iota(jnp.int32, (CHUNK, SGU_WIDTH), 1) // SGU_GROUP_DIM
    r_i = lax.broadcasted_iota(jnp.int32, (CHUNK, CHUNK), 0)
    c_i = lax.broadcasted_iota(jnp.int32, (CHUNK, CHUNK), 1)
    tri, tri_t = r_i >= c_i, r_i <= c_i
    mixed = bias
    for g in range(SGU_GROUPS):
        wt = jnp.where(tri, ws_ref[g], 0.0).astype(BF16)
        mixed = mixed + jnp.where(grp == g, _dot(wt, vn_bf, _NN), 0.0)
    sb = _sigmoid(zb)
    return ug, dug, dvg, rstd, xh, vn_bf, grp, tri, tri_t, mixed, sb


def _sgu_fwd(h_b, lg, lb, w_s, bias_full):
    def body(u_ref, v_ref, zb_ref, lg_ref, lb_ref, ws_ref, bias_ref, yb_ref):
        zb = zb_ref[...]
        ug, _, _, _, _, _, _, _, _, mixed, sb = _sgu_math(u_ref[...], v_ref[...], zb, lg_ref[...], lb_ref[...],
                                                       ws_ref, bias_ref[...])
        yb_ref[...] = (ug * mixed) * (zb * sb)

    blk = lambda c: pl.BlockSpec((CHUNK, SGU_WIDTH), lambda i, c=c: (i, c))
    full2 = lambda shape: pl.BlockSpec(shape, lambda i: (0, 0))
    return pl.pallas_call(
        body, name="sgu_fwd", grid=(SEQ // CHUNK,),
        in_specs=[blk(0), blk(1), blk(2), full2((1, SGU_WIDTH)), full2((1, SGU_WIDTH)),
                  pl.BlockSpec((SGU_GROUPS, CHUNK, CHUNK), lambda i: (0, 0, 0)), full2((CHUNK, SGU_WIDTH))],
        out_specs=pl.BlockSpec((CHUNK, SGU_WIDTH), lambda i: (i, 0)),
        out_shape=jax.ShapeDtypeStruct((SEQ, SGU_WIDTH), F32),
        compiler_params=pltpu.CompilerParams(dimension_semantics=("arbitrary",)),
    )(h_b, h_b, h_b, lg, lb, w_s, bias_full)


def _sgu_bwd(h_b, d_yb, lg, lb, w_s, w_st, bias_full, parts):
    nsteps = SEQ // CHUNK
    npart = len(parts)

    def body(u_ref, v_ref, zb_ref, dyb_ref, lg_ref, lb_ref, ws_ref, wst_ref, bias_ref, *rest):
        part_refs, rest = rest[:npart], rest[npart:]
        dhb_ref, dws_ref, dbs_ref, dlg_ref, dlb_ref, dbb_ref = rest[:6]
        land_refs, (dbias_acc, send_sems, recv_sems, local_sems) = rest[6:6 + npart], rest[6 + npart:]
        step = pl.program_id(0)
        exchange = _exchange_parts(part_refs, land_refs, send_sems, recv_sems, local_sems)
        _exchange_start(step == 0, exchange)

        @pl.when(step == 0)
        def _():
            dbb_ref[...] = jnp.zeros_like(dbb_ref)
            dws_ref[...] = jnp.zeros_like(dws_ref)
            dlg_ref[...] = jnp.zeros_like(dlg_ref)
            dlb_ref[...] = jnp.zeros_like(dlb_ref)
            dbias_acc[...] = jnp.zeros_like(dbias_acc)

        zb = zb_ref[...]
        lg = lg_ref[...]
        ug, dug, dvg, rstd, xh, vn_bf, grp, tri, tri_t, mixed, sb = _sgu_math(
            u_ref[...], v_ref[...], zb, lg, lb_ref[...], ws_ref, bias_ref[...])
        dyb = dyb_ref[...]
        dsgu = dyb * (zb * sb)
        dzb = dyb * (ug * mixed) * (sb * (1.0 + zb * (1.0 - sb)))
        du = dsgu * mixed * dug
        dmixed = dsgu * ug
        dbias_acc[...] += dmixed
        dvn = jnp.zeros((CHUNK, SGU_WIDTH), F32)
        for g in range(SGU_GROUPS):
            dm_g = jnp.where(grp == g, dmixed, 0.0).astype(BF16)
            wtt = jnp.where(tri_t, wst_ref[g], 0.0).astype(BF16)
            dvn = dvn + _dot(wtt, dm_g, _NN)
            dws_ref[g] += jnp.where(tri, _dot(dm_g, vn_bf, _NT), 0.0)
        dlg_ref[...] += jnp.sum(dvn * xh, axis=0, keepdims=True)
        dlb_ref[...] += jnp.sum(dvn, axis=0, keepdims=True)
        dxh = dvn * lg
        dvgel = rstd * (dxh - jnp.mean(dxh, axis=1, keepdims=True) - xh * jnp.mean(dxh * xh, axis=1, keepdims=True))
        _store_grad(dhb_ref, dbb_ref, 0, du)
        _store_grad(dhb_ref, dbb_ref, SGU_WIDTH, dvgel * dvg)
        _store_grad(dhb_ref, dbb_ref, 2 * SGU_WIDTH, dzb)

        @pl.when(step == nsteps - 1)
        def _():
            acc = dbias_acc[...]
            lane = lax.broadcasted_iota(jnp.int32, (CHUNK, LANES), 1)
            out = jnp.zeros((CHUNK, LANES), F32)
            for g in range(SGU_GROUPS):
                sg = jnp.sum(jnp.where(grp == g, acc, 0.0), axis=1, keepdims=True)
                out = jnp.where(lane == g, sg, out)
            dbs_ref[...] = out

        _exchange_finish(step == nsteps - 1, exchange)

    blk = lambda c: pl.BlockSpec((CHUNK, SGU_WIDTH), lambda i, c=c: (i, c))
    full2 = lambda shape: pl.BlockSpec(shape, lambda i: (0, 0))
    full3 = pl.BlockSpec((SGU_GROUPS, CHUNK, CHUNK), lambda i: (0, 0, 0))
    hbm = pl.BlockSpec(memory_space=pl.ANY)
    res = pl.pallas_call(
        body, name="sgu_bwd", grid=(nsteps,),
        in_specs=[blk(0), blk(1), blk(2), pl.BlockSpec((CHUNK, SGU_WIDTH), lambda i: (i, 0)),
                  full2((1, SGU_WIDTH)), full2((1, SGU_WIDTH)), full3, full3, full2((CHUNK, SGU_WIDTH))] + [hbm] * npart,
        out_specs=[pl.BlockSpec((CHUNK, SEG_B), lambda i: (i, 0)), full3, full2((CHUNK, LANES)),
                   full2((1, SGU_WIDTH)), full2((1, SGU_WIDTH)), full2((1, SEG_B))] + [hbm] * npart,
        out_shape=[jax.ShapeDtypeStruct((SEQ, SEG_B), BF16),
                   jax.ShapeDtypeStruct((SGU_GROUPS, CHUNK, CHUNK), F32),
                   jax.ShapeDtypeStruct((CHUNK, LANES), F32),
                   jax.ShapeDtypeStruct((1, SGU_WIDTH), F32), jax.ShapeDtypeStruct((1, SGU_WIDTH), F32),
                   jax.ShapeDtypeStruct((1, SEG_B), F32)] + [jax.ShapeDtypeStruct(p.shape, p.dtype) for p in parts],
        scratch_shapes=[pltpu.VMEM((CHUNK, SGU_WIDTH), F32)] + _exchange_sems(npart),
        compiler_params=pltpu.CompilerParams(dimension_semantics=("arbitrary",)),
    )(h_b, h_b, h_b, d_yb, lg, lb, w_s, w_st, bias_full, *parts)
    return res[:6], res[6:]


def _merge(x, o, h_a, y_b, target, w_oa, w_ob, w_out, ln_g, ln_b):
    tm = 256
    nsteps = SEQ // tm

    def body(x_ref, o_ref, ga_ref, gb_ref, za_ref, yb_ref, tgt_ref, woa_ref, wob_ref, wout_ref, lng_ref, lnb_ref,
             loss_ref, dxr_ref, dha_ref, do_ref, dyb_ref, poa_ref, pob_ref, pout_ref, dlng_ref, dlnb_ref, dba_ref,
             dwoa_ref, dwob_ref, dwout_ref):
        step = pl.program_id(0)

        @pl.when(step == 0)
        def _():
            for r in (loss_ref, dwoa_ref, dwob_ref, dwout_ref, dlng_ref, dlnb_ref, dba_ref):
                r[...] = jnp.zeros_like(r)

        o = o_ref[...]
        za = za_ref[...]
        sa = _sigmoid(za)
        ya_bf = (o * (za * sa)).astype(BF16)
        yb_bf = yb_ref[...].astype(BF16)
        woa, wob, wout = woa_ref[...], wob_ref[...], wout_ref[...]
        pa = _dot(ya_bf, woa, _NN)
        pb = _dot(yb_bf, wob, _NN)
        sga = _sigmoid(ga_ref[...])
        sgb = _sigmoid(gb_ref[...])
        merged_bf = (sga * pa + sgb * pb).astype(BF16)
        r = DN_ALPHA * x_ref[...] + _dot(merged_bf, wout, _NN)
        mu = jnp.mean(r, axis=1, keepdims=True)
        rc = r - mu
        rstd = lax.rsqrt(jnp.mean(rc * rc, axis=1, keepdims=True) + LN_EPS)
        xh = rc * rstd
        lng = lng_ref[...]
        y = xh * lng + lnb_ref[...]
        e = y - tgt_ref[...]
        loss_ref[...] += 0.5 * jnp.sum(jnp.sum(e * e, axis=1, keepdims=True) * (1.0 / D_MODEL), axis=0, keepdims=True)

        dy = e * (1.0 / D_MODEL)
        dlng_ref[...] += jnp.sum(dy * xh, axis=0, keepdims=True)
        dlnb_ref[...] += jnp.sum(dy, axis=0, keepdims=True)
        dxh = dy * lng
        dr = rstd * (dxh - jnp.mean(dxh, axis=1, keepdims=True) - xh * jnp.mean(dxh * xh, axis=1, keepdims=True))
        dxr_ref[...] = DN_ALPHA * dr
        dr_bf = dr.astype(BF16)
        dwout_ref[...] += _dot(merged_bf, dr_bf, _TN)
        dmerged = _dot(dr_bf, wout, _NT)
        dpa_bf = (dmerged * sga).astype(BF16)
        dpb_bf = (dmerged * sgb).astype(BF16)
        _store_grad(dha_ref, dba_ref, 0, dmerged * pa * (sga * (1.0 - sga)))
        _store_grad(dha_ref, dba_ref, D_MODEL, dmerged * pb * (sgb * (1.0 - sgb)))
        dwoa_ref[...] += _dot(ya_bf, dpa_bf, _TN)
        dwob_ref[...] += _dot(yb_bf, dpb_bf, _TN)
        dya = _dot(dpa_bf, woa, _NT)
        dyb_ref[...] = _dot(dpb_bf, wob, _NT)
        do_ref[...] = dya * (za * sa)
        _store_grad(dha_ref, dba_ref, 2 * D_MODEL, dya * o * (sa * (1.0 + za * (1.0 - sa))))

        @pl.when(step == nsteps - 1)
        def _():
            cols = D_MODEL // N_DEV
            for j in range(N_DEV):
                poa_ref[j] = dwoa_ref[:, cols * j:cols * (j + 1)].astype(BF16)
                pob_ref[j] = dwob_ref[:, cols * j:cols * (j + 1)].astype(BF16)
                pout_ref[j] = dwout_ref[cols * j:cols * (j + 1), :].astype(BF16)

    row = lambda w, c=0: pl.BlockSpec((tm, w), lambda i, c=c: (i, c))
    full = lambda shape: pl.BlockSpec(shape, lambda i: (0, 0))
    full3 = lambda shape: pl.BlockSpec(shape, lambda i: (0, 0, 0))
    return pl.pallas_call(
        body, name="merge", grid=(nsteps,),
        in_specs=[row(D_MODEL), row(MLA_WIDTH), row(D_MODEL, 0), row(D_MODEL, 1), row(MLA_WIDTH, 4), row(SGU_WIDTH),
                  row(D_MODEL), full((MLA_WIDTH, D_MODEL)), full((SGU_WIDTH, D_MODEL)), full((D_MODEL, D_MODEL)),
                  full((1, D_MODEL)), full((1, D_MODEL))],
        out_specs=[full((1, LANES)), row(D_MODEL), row(SEG_A), row(MLA_WIDTH), row(SGU_WIDTH),
                   full3((N_DEV, MLA_WIDTH, D_MODEL // N_DEV)), full3((N_DEV, SGU_WIDTH, D_MODEL // N_DEV)),
                   full3((N_DEV, D_MODEL // N_DEV, D_MODEL)), full((1, D_MODEL)), full((1, D_MODEL)), full((1, SEG_A))],
        out_shape=[jax.ShapeDtypeStruct((1, LANES), F32),
                   jax.ShapeDtypeStruct((SEQ, D_MODEL), F32), jax.ShapeDtypeStruct((SEQ, SEG_A), BF16),
                   jax.ShapeDtypeStruct((SEQ, MLA_WIDTH), F32), jax.ShapeDtypeStruct((SEQ, SGU_WIDTH), F32),
                   jax.ShapeDtypeStruct((N_DEV, MLA_WIDTH, D_MODEL // N_DEV), BF16),
                   jax.ShapeDtypeStruct((N_DEV, SGU_WIDTH, D_MODEL // N_DEV), BF16),
                   jax.ShapeDtypeStruct((N_DEV, D_MODEL // N_DEV, D_MODEL), BF16),
                   jax.ShapeDtypeStruct((1, D_MODEL), F32), jax.ShapeDtypeStruct((1, D_MODEL), F32),
                   jax.ShapeDtypeStruct((1, SEG_A), F32)],
        scratch_shapes=[pltpu.VMEM((MLA_WIDTH, D_MODEL), F32), pltpu.VMEM((SGU_WIDTH, D_MODEL), F32),
                        pltpu.VMEM((D_MODEL, D_MODEL), F32)],
        compiler_params=pltpu.CompilerParams(dimension_semantics=("arbitrary",), vmem_limit_bytes=VMEM_BIG),
    )(x, o, h_a, h_a, h_a, y_b, target, w_oa, w_ob, w_out, ln_g, ln_b)


def _mla_bwd(dq, dk, dv, h_c, xt_bf, gq, gkv, wq, wkn, wv, c_t, sa_t, sb_t, parts):
    tm = 256
    hw = MLA_HEADS * HEAD_PAD
    npart = len(parts)
    nsteps = SEQ // tm

    def body(dq_ref, dk_ref, dv_ref, cq_ref, ckv_ref, xt_ref, gq_ref, gkv_ref, wq_ref, wkn_ref, wv_ref, c_ref, sa_ref,
             sb_ref, *rest):
        part_refs, rest = rest[:npart], rest[npart:]
        dhc_ref, puq_ref, dwkn_ref, dwv_ref, dgq_ref, dgkv_ref, dbc_ref, dwc_ref = rest[:8]
        land_refs, (pre_ref, dwq_ref, dwc_acc, send_sems, recv_sems, local_sems) = rest[8:8 + npart], rest[8 + npart:]
        exchange = _exchange_parts(part_refs, land_refs, send_sems, recv_sems, local_sems)
        _exchange_start(pl.program_id(0) == 0, exchange)

        @pl.when(pl.program_id(0) == 0)
        def _():
            for r in (dwq_ref, dwc_acc, dwkn_ref, dwv_ref, dgq_ref, dgkv_ref, dbc_ref):
                r[...] = jnp.zeros_like(r)

        c, sa, sb = c_ref[...], sa_ref[...], sb_ref[...]
        lane = lax.broadcasted_iota(jnp.int32, (tm, LANES), 1)
        rope_lanes = jnp.logical_and(lane >= ROPE_LO, lane < ROPE_HI)

        cq = cq_ref[...]
        gq = gq_ref[...]
        rq = lax.rsqrt(jnp.sum(cq * cq, axis=1, keepdims=True) * (1.0 / Q_LORA_RANK) + RMS_EPS)
        nq = cq * rq
        cqn_bf = (nq * gq).astype(BF16)
        for h in range(MLA_HEADS):
            sl = slice(HEAD_PAD * h, HEAD_PAD * (h + 1))
            pre_ref[:, sl] = _rope_t(dq_ref[:, sl] * ATTN_SCALE, c, sa, sb).astype(BF16)
        dqpre_bf = pre_ref[...]
        dcqn = _dot(dqpre_bf, wq_ref[...], _NT)
        dwq_ref[...] += _dot(cqn_bf, dqpre_bf, _TN)
        dgq_ref[...] += jnp.sum(dcqn * nq, axis=0, keepdims=True)
        dnq = dcqn * gq
        _store_grad(dhc_ref, dbc_ref, 0,
                    rq * (dnq - nq * (jnp.sum(dnq * nq, axis=1, keepdims=True) * (1.0 / Q_LORA_RANK))))

        ckv = ckv_ref[...]
        gkv = gkv_ref[...]
        rkv = lax.rsqrt(jnp.sum(ckv * ckv, axis=1, keepdims=True) * (1.0 / KV_LORA_RANK) + RMS_EPS)
        nkv = ckv * rkv
        ckvn_bf = (nkv * gkv).astype(BF16)
        dk = dk_ref[...]
        dk_bf = dk.astype(BF16)
        dv_bf = dv_ref[...].astype(BF16)
        dckvn = _dot(dk_bf, wkn_ref[...], _NT) + _dot(dv_bf, wv_ref[...], _NT)
        dwkn_ref[...] += _dot(ckvn_bf, dk_bf, _TN)
        dwv_ref[...] += _dot(ckvn_bf, dv_bf, _TN)
        dgkv_ref[...] += jnp.sum(dckvn * nkv, axis=0, keepdims=True)
        dnkv = dckvn * gkv
        _store_grad(dhc_ref, dbc_ref, CQ_PAD, rkv * (
            dnkv - nkv * (jnp.sum(dnkv * nkv, axis=1, keepdims=True) * (1.0 / KV_LORA_RANK))))
        dkpe = jnp.zeros((tm, LANES), F32)
        for h in range(MLA_HEADS):
            dkpe = dkpe + dk[:, HEAD_PAD * h:HEAD_PAD * (h + 1)]
        _store_grad(dhc_ref, dbc_ref, CQ_PAD + LANES, _rope_t(jnp.where(rope_lanes, dkpe, 0.0), c, sa, sb))
        dwc_acc[...] += _dot(xt_ref[...], dhc_ref[...], _NN)

        @pl.when(pl.program_id(0) == SEQ // tm - 1)
        def _():
            dwc_ref[...] = dwc_acc[...].astype(BF16)
            rows = Q_LORA_RANK // N_DEV
            for j in range(N_DEV):
                for h in range(MLA_HEADS):
                    puq_ref[j, :, QK_HEAD_DIM * h:QK_HEAD_DIM * (h + 1)] = dwq_ref[
                        rows * j:rows * (j + 1), HEAD_PAD * h:HEAD_PAD * h + QK_HEAD_DIM].astype(BF16)

        _exchange_finish(pl.program_id(0) == nsteps - 1, exchange)

    full = lambda shape: pl.BlockSpec(shape, lambda i: (0, 0))
    row = lambda w, c=0: pl.BlockSpec((tm, w), lambda i, c=c: (i, c))
    hbm = pl.BlockSpec(memory_space=pl.ANY)
    res = pl.pallas_call(
        body, name="mla_bwd", grid=(nsteps,),
        in_specs=[row(hw), row(hw), row(hw), row(CQ_PAD, 0), row(LANES, CQ_PAD // LANES),
                  pl.BlockSpec((D_MODEL, tm), lambda i: (0, i)),
                  full((1, CQ_PAD)), full((1, KV_LORA_RANK)), full((CQ_PAD, hw)), full((KV_LORA_RANK, hw)),
                  full((KV_LORA_RANK, hw)), row(LANES), row(LANES), row(LANES)] + [hbm] * npart,
        out_specs=[row(SEG_C), pl.BlockSpec((N_DEV, Q_LORA_RANK // N_DEV, MLA_HEADS * QK_HEAD_DIM), lambda i: (0, 0, 0)),
                   full((KV_LORA_RANK, hw)), full((KV_LORA_RANK, hw)),
                   full((1, CQ_PAD)), full((1, KV_LORA_RANK)), full((1, SEG_C)), full((D_MODEL, SEG_C))] + [hbm] * npart,
        out_shape=[jax.ShapeDtypeStruct((SEQ, SEG_C), BF16),
                   jax.ShapeDtypeStruct((N_DEV, Q_LORA_RANK // N_DEV, MLA_HEADS * QK_HEAD_DIM), BF16),
                   jax.ShapeDtypeStruct((KV_LORA_RANK, hw), F32), jax.ShapeDtypeStruct((KV_LORA_RANK, hw), F32),
                   jax.ShapeDtypeStruct((1, CQ_PAD), F32), jax.ShapeDtypeStruct((1, KV_LORA_RANK), F32),
                   jax.ShapeDtypeStruct((1, SEG_C), F32), jax.ShapeDtypeStruct((D_MODEL, SEG_C), BF16)]
        + [jax.ShapeDtypeStruct(p.shape, p.dtype) for p in parts],
        scratch_shapes=[pltpu.VMEM((tm, hw), BF16), pltpu.VMEM((CQ_PAD, hw), F32), pltpu.VMEM((D_MODEL, SEG_C), F32)]
        + _exchange_sems(npart),
        compiler_params=pltpu.CompilerParams(dimension_semantics=("arbitrary",), vmem_limit_bytes=VMEM_MID),
    )(dq, dk, dv, h_c, h_c, xt_bf, gq, gkv, wq, wkn, wv, c_t, sa_t, sb_t, *parts)
    return res[:8], res[8:]


def _adamw_all(ws, gs, ms, vs):
    n = len(ws)
    c1 = 1.0 / (1.0 - ADAM_B1 ** ADAM_STEP)
    c2 = 1.0 / (1.0 - ADAM_B2 ** ADAM_STEP)

    def body(*refs):
        for idx in range(n):
            w, g, m, v = (refs[idx][...], refs[n + idx][...], refs[2 * n + idx][...], refs[3 * n + idx][...])
            m_new = ADAM_B1 * m + (1.0 - ADAM_B1) * g
            v_new = ADAM_B2 * v + (1.0 - ADAM_B2) * (g * g)
            delta = -ADAM_LR * ((m_new * c1) / (jnp.sqrt(v_new * c2) + ADAM_EPS) + ADAM_WD * w)
            refs[4 * n + idx][...] = delta
            refs[5 * n + idx][...] = m_new
            refs[6 * n + idx][...] = v_new

    shapes = [jax.ShapeDtypeStruct(w.shape, F32) for w in ws]
    outs = pl.pallas_call(
        body, name="adamw", out_shape=shapes * 3,
        compiler_params=pltpu.CompilerParams(vmem_limit_bytes=VMEM_BIG),
    )(*ws, *gs, *ms, *vs)
    return outs[:n], outs[n:2 * n], outs[2 * n:]


SHARD_W = IN_WIDTH // N_DEV

_PIECES = [(0, 384, 2, 0), (384, 512, 2, CQ_PAD), (512, 544, 2, CQ_PAD + LANES + ROPE_LO),
           (544, 1056, 0, 2 * D_MODEL), (1056, 1568, 1, 0), (1568, 2080, 1, SGU_WIDTH),
           (2080, 2592, 1, 2 * SGU_WIDTH), (2592, 3616, 0, 0), (3616, 4640, 0, D_MODEL)]


def _column_runs():
    runs = []
    for n0, n1, seg, d0 in _PIECES:
        for j in range(N_DEV):
            lo, hi = max(n0, j * SHARD_W), min(n1, (j + 1) * SHARD_W)
            if lo < hi:
                runs.append((j, lo - j * SHARD_W, hi - j * SHARD_W, seg, d0 + lo - n0))
    return runs


def _mesh_pos():
    return lax.axis_index("x"), lax.axis_index("y"), lax.axis_index("c")


def _remote(src, dst, send_sems, recv_sems, k, to):
    return pltpu.make_async_remote_copy(src_ref=src, dst_ref=dst, send_sem=send_sems.at[k], recv_sem=recv_sems.at[k],
                                        device_id=to, device_id_type=pl.DeviceIdType.MESH)


def _gather_exchange(gats, send_sems, recv_sems, meanwhile=None):
    x, y, c = _mesh_pos()
    me, sibling = (x, y, c), (x, y, 1 - c)
    chips = [(1 - x, y), (x, 1 - y), (1 - x, 1 - y)]

    def copy(a, k, blk, to):
        slab = gats[a].at[4 * blk[0] + 2 * blk[1] + blk[2]]
        return _remote(slab, slab, send_sems, recv_sems, 7 * a + k, to)

    arrays = range(len(gats))
    first = [copy(a, 1 + j, me, (*chip, c)) for j, chip in enumerate(chips) for a in arrays]
    first += [copy(a, 0, me, sibling) for a in arrays]
    for cp in first:
        cp.start()
    if meanwhile is not None:
        meanwhile()
    passed = []
    for j, chip in enumerate(chips):
        for a in arrays:
            copy(a, 1 + j, (*chip, c), me).wait_recv()
            fwd = copy(a, 4 + j, (*chip, c), sibling)
            fwd.start()
            passed.append(fwd)
    for a in arrays:
        copy(a, 0, sibling, me).wait_recv()
    for j, chip in enumerate(chips):
        for a in arrays:
            copy(a, 4 + j, (*chip, 1 - c), me).wait_recv()
    for cp in first + passed:
        cp.wait_send()


def _gather_behind(own, gats, send_sems, recv_sems, local_sems, step, mid, last):
    x, y, c = _mesh_pos()
    me, sibling = (x, y, c), (x, y, 1 - c)
    chips = [(1 - x, y), (x, 1 - y), (1 - x, 1 - y)]
    arrays = range(len(gats))

    def copy(a, k, blk, to, src=None):
        slab = gats[a].at[4 * blk[0] + 2 * blk[1] + blk[2]]
        return _remote(slab if src is None else src, slab, send_sems, recv_sems, 7 * a + k, to)

    first = [copy(a, 1 + j, me, (*chip, c), src=own[a]) for j, chip in enumerate(chips) for a in arrays]
    first += [copy(a, 0, me, sibling, src=own[a]) for a in arrays]
    local = [pltpu.make_async_copy(own[a], gats[a].at[4 * x + 2 * y + c], local_sems.at[a]) for a in arrays]
    passed = [copy(a, 4 + j, (*chip, c), sibling) for j, chip in enumerate(chips) for a in arrays]

    @pl.when(step == 0)
    def _():
        for cp in first + local:
            cp.start()

    @pl.when(step == mid)
    def _():
        for j, chip in enumerate(chips):
            for a in arrays:
                copy(a, 1 + j, (*chip, c), me).wait_recv()
        for cp in passed:
            cp.start()

    @pl.when(step == last)
    def _():
        for a in arrays:
            copy(a, 0, sibling, me).wait_recv()
        for j, chip in enumerate(chips):
            for a in arrays:
                copy(a, 4 + j, (*chip, 1 - c), me).wait_recv()
        for cp in first + passed:
            cp.wait_send()
        for cp in local:
            cp.wait()


def _gather_first(w_in, w_uq2, w_oa, w_ob, w_out, x2, pos_col, invf_lane):
    hw = MLA_HEADS * HEAD_PAD
    uq_rows = Q_LORA_RANK // N_DEV
    rows = 256

    def body(win_ref, wuq_ref, woa_ref, wob_ref, wout_ref, x_ref, pos_ref, invf_ref,
             wc_ref, wq_ref, winb_ref, oab_ref, obb_ref, outb_ref, xb_ref, xt_ref, c_ref, sa_ref, sb_ref,
             g_uq, blk0, send_sems, recv_sems):
        def local_work():
            for i in range(SEQ // rows):
                xi = x_ref[rows * i:rows * (i + 1), :]
                xb_ref[rows * i:rows * (i + 1), :] = xi.astype(BF16)
                xt_ref[:, rows * i:rows * (i + 1)] = xi.T.astype(BF16)
            ang = pos_ref[...].astype(F32) * invf_ref[...]
            cs, sn = jnp.cos(ang), jnp.sin(ang)
            lane = lax.broadcasted_iota(jnp.int32, ang.shape, 1)
            c_ref[...] = jnp.where(lane < ROPE_LO, 1.0, jnp.where(lane < ROPE_HI, cs, 0.0))
            sa_ref[...] = jnp.where(jnp.logical_and(lane >= ROPE_LO, lane < ROPE_MID), -sn, 0.0)
            sb_ref[...] = jnp.where(jnp.logical_and(lane >= ROPE_MID, lane < ROPE_HI), sn, 0.0)

        x, y, c = _mesh_pos()
        me = (x, y, c)
        winb_ref[...] = win_ref[0].astype(BF16)
        oab_ref[...] = woa_ref[0].astype(BF16)
        obb_ref[...] = wob_ref[0].astype(BF16)
        outb_ref[...] = wout_ref[0].astype(BF16)
        g_uq[4 * x + 2 * y + c] = wuq_ref[...].astype(BF16)

        chip0 = jnp.logical_and(x == 0, y == 0)
        south = c == 0
        half = D_MODEL // 2
        halves = [blk0.at[pl.ds(0, half)], blk0.at[pl.ds(half, half)]]

        def bcopy(k, to, part=None):
            ref = blk0 if part is None else halves[part]
            return _remote(ref, ref, send_sems, recv_sems, 7 + k, to)

        sends0 = [(0, (0, 0, 1), None), (1, (1, 0, 0), 0), (2, (0, 1, 0), 1), (3, (1, 0, 0), 1), (4, (0, 1, 0), 0)]

        @pl.when(jnp.logical_and(chip0, south))
        def _():
            blk0[...] = winb_ref[...]
            for k, to, part in sends0:
                bcopy(k, to, part).start()

        _gather_exchange([g_uq], send_sems, recv_sems, meanwhile=local_work)

        for (cx, cy), first_k, first_half, second_k in (((1, 0), 1, 0, 3), ((0, 1), 2, 1, 4)):
            @pl.when(jnp.logical_and(jnp.logical_and(x == cx, y == cy), south))
            def _(cx=cx, cy=cy, first_k=first_k, first_half=first_half, second_k=second_k):
                bcopy(first_k, me, first_half).wait_recv()
                onward = bcopy(5 + first_half, (1, 1, 0), first_half)
                onward.start()
                bcopy(second_k, me, 1 - first_half).wait_recv()
                north = bcopy(7, (cx, cy, 1))
                north.start()
                onward.wait_send()
                north.wait_send()

        @pl.when(jnp.logical_and(jnp.logical_and(x == 1, y == 1), south))
        def _():
            bcopy(5, me, 0).wait_recv()
            bcopy(6, me, 1).wait_recv()
            north = bcopy(7, (1, 1, 1))
            north.start()
            north.wait_send()

        @pl.when(jnp.logical_and(chip0, c == 1))
        def _():
            bcopy(0, me).wait_recv()

        @pl.when(jnp.logical_and(jnp.logical_not(chip0), c == 1))
        def _():
            bcopy(7, me).wait_recv()

        @pl.when(jnp.logical_and(chip0, south))
        def _():
            for k, to, part in sends0:
                bcopy(k, to, part).wait_send()

        for j, s0, s1, seg, d0 in _column_runs():
            if seg == 2:
                wc_ref[:, d0:d0 + (s1 - s0)] = blk0[:, s0:s1]
        zeros = lambda r, w: jnp.zeros((r, w), BF16)
        wc_ref[:, Q_LORA_RANK:CQ_PAD] = zeros(D_MODEL, CQ_PAD - Q_LORA_RANK)
        wc_ref[:, CQ_PAD + LANES:CQ_PAD + LANES + ROPE_LO] = zeros(D_MODEL, ROPE_LO)
        wc_ref[:, CQ_PAD + LANES + ROPE_HI:SEG_C] = zeros(D_MODEL, LANES - ROPE_HI)
        wq_ref[Q_LORA_RANK:CQ_PAD, :] = zeros(CQ_PAD - Q_LORA_RANK, hw)
        for h in range(MLA_HEADS):
            wq_ref[0:Q_LORA_RANK, HEAD_PAD * h + QK_HEAD_DIM:HEAD_PAD * (h + 1)] = zeros(Q_LORA_RANK, HEAD_PAD - QK_HEAD_DIM)
        for j in range(N_DEV):
            for h in range(MLA_HEADS):
                wq_ref[uq_rows * j:uq_rows * (j + 1), HEAD_PAD * h:HEAD_PAD * h + QK_HEAD_DIM] = g_uq[
                    j, :, QK_HEAD_DIM * h:QK_HEAD_DIM * (h + 1)]

    vmem = pl.BlockSpec(memory_space=pltpu.VMEM)
    return pl.pallas_call(
        body, name="gather_first",
        out_shape=[jax.ShapeDtypeStruct((D_MODEL, SEG_C), BF16), jax.ShapeDtypeStruct((CQ_PAD, hw), BF16),
                   jax.ShapeDtypeStruct(w_in.shape[1:], BF16), jax.ShapeDtypeStruct(w_oa.shape[1:], BF16),
                   jax.ShapeDtypeStruct(w_ob.shape[1:], BF16), jax.ShapeDtypeStruct(w_out.shape[1:], BF16),
                   jax.ShapeDtypeStruct((SEQ, D_MODEL), BF16), jax.ShapeDtypeStruct((D_MODEL, SEQ), BF16)]
        + [jax.ShapeDtypeStruct((SEQ, LANES), F32)] * 3,
        in_specs=[vmem] * 8, out_specs=[vmem] * 11,
        scratch_shapes=[pltpu.VMEM((N_DEV, uq_rows, MLA_HEADS * QK_HEAD_DIM), BF16), pltpu.VMEM((D_MODEL, SHARD_W), BF16),
                        pltpu.SemaphoreType.DMA((15,)), pltpu.SemaphoreType.DMA((15,))],
        compiler_params=pltpu.CompilerParams(vmem_limit_bytes=VMEM_BIG),
    )(w_in, w_uq2, w_oa, w_ob, w_out, x2, pos_col, invf_lane)


def _assemble_in(g_in):
    def body(g_ref, wa_ref, wb_ref):
        segs = [wa_ref, wb_ref]
        for j, s0, s1, seg, d0 in _column_runs():
            if seg < 2:
                segs[seg][:, d0:d0 + (s1 - s0)] = g_ref[j, :, s0:s1]

    return pl.pallas_call(
        body, name="assemble_in",
        out_shape=[jax.ShapeDtypeStruct((D_MODEL, SEG_A), BF16), jax.ShapeDtypeStruct((D_MODEL, SEG_B), BF16)],
        compiler_params=pltpu.CompilerParams(vmem_limit_bytes=VMEM_MID),
    )(g_in)


def _assemble_out(g_oa, g_ob, g_out):
    cols = D_MODEL // N_DEV

    def body(goa_ref, gob_ref, gout_ref, oa_ref, ob_ref, out_ref):
        for j in range(N_DEV):
            oa_ref[:, cols * j:cols * (j + 1)] = goa_ref[j]
            ob_ref[:, cols * j:cols * (j + 1)] = gob_ref[j]
            out_ref[cols * j:cols * (j + 1), :] = gout_ref[j]

    return pl.pallas_call(
        body, name="assemble_out",
        out_shape=[jax.ShapeDtypeStruct((MLA_WIDTH, D_MODEL), BF16), jax.ShapeDtypeStruct((SGU_WIDTH, D_MODEL), BF16),
                   jax.ShapeDtypeStruct((D_MODEL, D_MODEL), BF16)],
    )(g_oa, g_ob, g_out)


C_NAT = 544


P_IN_SPLIT = 896


def _to_parts(dwa, dwb):
    def body(dwa_ref, dwb_ref, phi_ref, plo_ref):
        phi_ref[0, :, 0:C_NAT] = jnp.zeros((P_IN_SPLIT, C_NAT), BF16)
        plo_ref[0, :, 0:C_NAT] = jnp.zeros((D_MODEL - P_IN_SPLIT, C_NAT), BF16)
        segs = [dwa_ref, dwb_ref]
        for j, s0, s1, seg, d0 in _column_runs():
            if seg < 2:
                phi_ref[j, :, s0:s1] = segs[seg][0:P_IN_SPLIT, d0:d0 + (s1 - s0)]
                plo_ref[j, :, s0:s1] = segs[seg][P_IN_SPLIT:D_MODEL, d0:d0 + (s1 - s0)]

    return pl.pallas_call(
        body, name="to_parts",
        out_shape=[jax.ShapeDtypeStruct((N_DEV, P_IN_SPLIT, SHARD_W), BF16),
                   jax.ShapeDtypeStruct((N_DEV, D_MODEL - P_IN_SPLIT, SHARD_W), BF16)],
        compiler_params=pltpu.CompilerParams(vmem_limit_bytes=VMEM_MID))(dwa, dwb)


def _dx_tail(dhs, ws, dx_res, dwc, p_uq, p_rep):
    ntile, sums_at = 8, 4
    tm = SEQ // ntile
    rep_rows = p_rep.shape[1]
    c_rows = D_MODEL // N_DEV
    spec = [((c_rows, C_NAT), BF16), (p_uq.shape[1:], BF16), ((rep_rows, LANES), F32)]
    n = len(spec)

    nseg = len(dhs)

    def body(*refs):
        dh_refs, w_refs = refs[:nseg], refs[nseg:2 * nseg]
        dxr_ref, dwc_ref, puq_ref, prep_ref, dx_ref, call_ref, guq_ref, repall_ref, pc_ref, c_all, rep_all = refs[
            2 * nseg:2 * nseg + 11]
        rest = refs[2 * nseg + 11:]
        ras, tbs, rbs = rest[0:n], rest[n:2 * n], rest[2 * n:3 * n]
        send_sems, recv_sems, gsend, grecv = rest[3 * n:]
        step = pl.program_id(0)
        x, y, c = _mesh_pos()
        me_idx = 4 * x + 2 * y + c
        me, sibling = (x, y, c), (x, y, 1 - c)
        others = [(1 - x, y), (x, 1 - y), (1 - x, 1 - y)]
        parts = [pc_ref, puq_ref, prep_ref]
        gats = [rep_all, c_all]

        def stage1(chip, a):
            return _remote(parts[a].at[2 * chip + (1 - c)], ras[a].at[chip], send_sems, recv_sems, 7 * a + chip, sibling)

        def stage2(k, a):
            cx, cy = others[k]
            return _remote(tbs[a].at[k], rbs[a].at[k], send_sems, recv_sems, 7 * a + 4 + k, (cx, cy, c))

        def gcopy(a, k, blk, to):
            slab = gats[a].at[4 * blk[0] + 2 * blk[1] + blk[2]]
            return _remote(slab, slab, gsend, grecv, 7 * a + k, to)

        def chip_sum(a, chip):
            return parts[a][2 * chip + c].astype(F32) + ras[a][chip].astype(F32)

        @pl.when(step == 0)
        def _():
            for j, s0, s1, seg, d0 in _column_runs():
                if seg == 2:
                    for r in range(N_DEV):
                        pc_ref[r, :, s0:s1] = dwc_ref[c_rows * r:c_rows * (r + 1), d0:d0 + (s1 - s0)]
            for chip in range(4):
                for a in range(n):
                    stage1(chip, a).start()

        @pl.when(step == 1)
        def _():
            for chip in range(4):
                for a in range(n):
                    stage1(chip, a).wait_recv()
            for k, (cx, cy) in enumerate(others):
                for a in range(n):
                    tbs[a][k] = chip_sum(a, 2 * cx + cy).astype(spec[a][1])
                    stage2(k, a).start()

        @pl.when(step == sums_at)
        def _():
            for k in range(3):
                for a in range(n):
                    stage2(k, a).wait_recv()
            sums = []
            for a in range(n):
                acc = chip_sum(a, 2 * x + y)
                for k in range(3):
                    acc = acc + rbs[a][k].astype(F32)
                sums.append(acc)
            c_all[me_idx] = sums[0].astype(BF16)
            guq_ref[...] = sums[1]
            rep_all[me_idx] = sums[2]
            for a in range(2):
                for j, chip in enumerate(others):
                    gcopy(a, 1 + j, me, (*chip, c)).start()
                gcopy(a, 0, me, sibling).start()

        acc = dxr_ref[...]
        for dh_ref, w_ref in zip(dh_refs, w_refs):
            acc = acc + _dot(dh_ref[...], w_ref[...], _NT)
        dx_ref[...] = acc

        @pl.when(step == ntile - 1)
        def _():
            for j, chip in enumerate(others):
                for a in range(2):
                    gcopy(a, 1 + j, (*chip, c), me).wait_recv()
                    gcopy(a, 4 + j, (*chip, c), sibling).start()
            for a in range(2):
                gcopy(a, 0, sibling, me).wait_recv()
                for j, chip in enumerate(others):
                    gcopy(a, 4 + j, (*chip, 1 - c), me).wait_recv()
            for a in range(2):
                gcopy(a, 0, me, sibling).wait_send()
                for j, chip in enumerate(others):
                    gcopy(a, 1 + j, me, (*chip, c)).wait_send()
                    gcopy(a, 4 + j, (*chip, c), sibling).wait_send()
            for a in range(n):
                for chip in range(4):
                    stage1(chip, a).wait_send()
                for k in range(3):
                    stage2(k, a).wait_send()
            call_ref[...] = c_all[...]
            repall_ref[...] = rep_all[...]

    row = lambda w: pl.BlockSpec((tm, w), lambda i: (i, 0))
    full = lambda shape: pl.BlockSpec(shape, lambda i: (0,) * len(shape))
    scratch = [pltpu.VMEM((N_DEV, c_rows, C_NAT), BF16), pltpu.VMEM((N_DEV, c_rows, C_NAT), BF16),
               pltpu.VMEM((N_DEV, rep_rows, LANES), F32)]
    for lead in (4, 3, 3):
        scratch += [pltpu.VMEM((lead,) + tuple(shape), dt) for shape, dt in spec]
    scratch += [pltpu.SemaphoreType.DMA((7 * n,)), pltpu.SemaphoreType.DMA((7 * n,)),
                pltpu.SemaphoreType.DMA((14,)), pltpu.SemaphoreType.DMA((14,))]
    return pl.pallas_call(
        body, name="dx_tail", grid=(SEQ // tm,),
        in_specs=[row(dh.shape[1]) for dh in dhs] + [full(w.shape) for w in ws]
        + [row(D_MODEL), full(dwc.shape), full(p_uq.shape), full(p_rep.shape)],
        out_specs=[row(D_MODEL), full((N_DEV, c_rows, C_NAT)), full(p_uq.shape[1:]), full((N_DEV, rep_rows, LANES))],
        out_shape=[jax.ShapeDtypeStruct((SEQ, D_MODEL), F32), jax.ShapeDtypeStruct((N_DEV, c_rows, C_NAT), BF16),
                   jax.ShapeDtypeStruct(p_uq.shape[1:], F32), jax.ShapeDtypeStruct((N_DEV, rep_rows, LANES), F32)],
        scratch_shapes=scratch,
        compiler_params=pltpu.CompilerParams(dimension_semantics=("arbitrary",), vmem_limit_bytes=VMEM_BIG),
    )(*dhs, *ws, dx_res, dwc, p_uq, p_rep)


def _sum_landed(landed, c_all):
    c_rows = D_MODEL // N_DEV

    def body(rhi_ref, rlo_ref, roa_ref, rob_ref, rout_ref, call_ref, gin_ref, goa_ref, gob_ref, gout_ref):
        def total(ref, sl):
            acc = ref[0, sl, :].astype(F32)
            for s in range(1, N_DEV):
                acc = acc + ref[s, sl, :].astype(F32)
            return acc

        x, y, c = _mesh_pos()
        dev0 = jnp.where(4 * x + 2 * y + c == 0, 1.0, 0.0)
        for j in range(N_DEV):
            sl = slice(c_rows * j, c_rows * (j + 1))
            below = c_rows * j < P_IN_SPLIT
            tot = total(rhi_ref, sl) if below else total(rlo_ref, slice(c_rows * j - P_IN_SPLIT, c_rows * (j + 1) - P_IN_SPLIT))
            gin_ref[0, sl, C_NAT:SHARD_W] = tot[:, C_NAT:SHARD_W]
            gin_ref[0, sl, 0:C_NAT] = tot[:, 0:C_NAT] + dev0 * call_ref[j].astype(F32)
        goa_ref[0] = total(roa_ref, slice(None))
        gob_ref[0] = total(rob_ref, slice(None))
        gout_ref[0] = total(rout_ref, slice(None))

    return pl.pallas_call(
        body, name="sum_landed",
        out_shape=[jax.ShapeDtypeStruct((1, D_MODEL, SHARD_W), F32)]
        + [jax.ShapeDtypeStruct((1,) + r.shape[1:], F32) for r in landed[2:]],
        compiler_params=pltpu.CompilerParams(vmem_limit_bytes=VMEM_MID),
    )(*landed, c_all)


_O_CQ, _O_CKV, _O_KPE, _O_ZA, _O_U, _O_V, _O_ZB, _O_GA, _O_GB = 0, 384, 512, 544, 1056, 1568, 2080, 2592, 3616


def _to_segments(w):
    z = lambda n: jnp.zeros(w.shape[:-1] + (n,), w.dtype)
    seg_a = jnp.concatenate([w[..., _O_GA:_O_GB], w[..., _O_GB:IN_WIDTH], w[..., _O_ZA:_O_U]], axis=-1)
    seg_b = jnp.concatenate([w[..., _O_U:_O_V], w[..., _O_V:_O_ZB], w[..., _O_ZB:_O_GA]], axis=-1)
    seg_c = jnp.concatenate([w[..., _O_CQ:_O_CKV], z(CQ_PAD - Q_LORA_RANK), w[..., _O_CKV:_O_KPE],
                             z(ROPE_LO), w[..., _O_KPE:_O_ZA], z(LANES - ROPE_HI)], axis=-1)
    return seg_a, seg_b, seg_c


def _from_segments(seg_a, seg_b, seg_c):
    kpe0 = CQ_PAD + LANES + ROPE_LO
    return jnp.concatenate([
        seg_c[..., 0:Q_LORA_RANK], seg_c[..., CQ_PAD:CQ_PAD + LANES], seg_c[..., kpe0:kpe0 + QK_ROPE_DIM],
        seg_a[..., 2 * D_MODEL:SEG_A], seg_b, seg_a[..., 0:2 * D_MODEL]], axis=-1)


def kernel(x, positions, w_in, b_in, g_q, w_uq, g_kv, w_ukv, w_oa, sgu_ln_g, sgu_ln_b, w_s, b_s, w_ob, w_out, ln_g, ln_b, loss_target, m_w_in, m_b_in, m_g_q, m_w_uq, m_g_kv, m_w_ukv, m_w_oa, m_sgu_ln_g, m_sgu_ln_b, m_w_s, m_b_s, m_w_ob, m_w_out, m_ln_g, m_ln_b, v_w_in, v_b_in, v_g_q, v_w_uq, v_g_kv, v_w_ukv, v_w_oa, v_sgu_ln_g, v_sgu_ln_b, v_w_s, v_b_s, v_w_ob, v_w_out, v_ln_g, v_ln_b):
    w_uq2 = w_uq[0].reshape(Q_LORA_RANK // N_DEV, MLA_HEADS * QK_HEAD_DIM)
    inv_freq = ROPE_THETA ** (-jnp.arange(0, QK_ROPE_DIM, 2, dtype=F32) / QK_ROPE_DIM)
    invf_lane = jnp.concatenate([jnp.zeros((ROPE_LO,), F32), inv_freq, inv_freq,
                                 jnp.zeros((LANES - ROPE_HI,), F32)]).reshape(1, LANES)
    first = _gather_first(w_in, w_uq2, w_oa, w_ob, w_out, x[0], positions.reshape(SEQ, 1), invf_lane)
    partials = _local_step(x[0], loss_target[0], first, b_in, g_q, g_kv, w_ukv, sgu_ln_g, sgu_ln_b, w_s, b_s, ln_g, ln_b)
    weights = dict(w_in=w_in, b_in=b_in, g_q=g_q, w_uq=w_uq, g_kv=g_kv, w_ukv=w_ukv, w_oa=w_oa, sgu_ln_g=sgu_ln_g,
                   sgu_ln_b=sgu_ln_b, w_s=w_s, b_s=b_s, w_ob=w_ob, w_out=w_out, ln_g=ln_g, ln_b=ln_b)
    moms = dict(w_in=m_w_in, b_in=m_b_in, g_q=m_g_q, w_uq=m_w_uq, g_kv=m_g_kv, w_ukv=m_w_ukv, w_oa=m_w_oa,
                sgu_ln_g=m_sgu_ln_g, sgu_ln_b=m_sgu_ln_b, w_s=m_w_s, b_s=m_b_s, w_ob=m_w_ob, w_out=m_w_out,
                ln_g=m_ln_g, ln_b=m_ln_b)
    vars_ = dict(w_in=v_w_in, b_in=v_b_in, g_q=v_g_q, w_uq=v_w_uq, g_kv=v_g_kv, w_ukv=v_w_ukv, w_oa=v_w_oa,
                 sgu_ln_g=v_sgu_ln_g, sgu_ln_b=v_sgu_ln_b, w_s=v_w_s, b_s=v_b_s, w_ob=v_w_ob, w_out=v_w_out,
                 ln_g=v_ln_g, ln_b=v_ln_b)
    return _reduce_and_update(partials, weights, moms, vars_)


def _local_step(x2, tgt, first, b_in, g_q, g_kv, w_ukv, sgu_ln_g, sgu_ln_b, w_s, b_s, ln_g, ln_b):
    wc, wq, win_b, oa_b, ob_b, out_b, x_bf, xt_bf, c_t, sa_t, sb_t = first
    ba, bb, bc = _to_segments(b_in)
    w_ukv_bf = w_ukv[0].astype(BF16)
    wkn = jnp.pad(w_ukv_bf[:, :, :QK_NOPE_DIM], ((0, 0), (0, 0), (0, HEAD_PAD - QK_NOPE_DIM))).reshape(KV_LORA_RANK, -1)
    wv = jnp.pad(w_ukv_bf[:, :, QK_NOPE_DIM:], ((0, 0), (0, 0), (0, HEAD_PAD - V_HEAD_DIM))).reshape(KV_LORA_RANK, -1)
    gq = jnp.pad(g_q, ((0, 0), (0, CQ_PAD - Q_LORA_RANK)))
    bias_full = jnp.repeat(b_s[0].T, SGU_GROUP_DIM, axis=1)
    w_s3 = w_s[0]
    w_st3 = jnp.swapaxes(w_s3, 1, 2)

    h_c = _mm(x_bf, wc, bias=bc, tm=512, tn=SEG_C, name="in_proj_c")
    q, k, kt, vx, vxt = _mla_prep(h_c, gq, g_kv, wq, wkn, wv, c_t, sa_t, sb_t)
    o, lse, (g_in,) = _attn_fwd(q, kt, vx, (win_b,))
    wa, wb = _assemble_in(g_in)
    h_a, (g_out,) = _mm(x_bf, wa, bias=ba, own=(out_b,), tm=512, tn=SEG_A // 2, name="in_proj_a")
    h_b, (g_oa, g_ob) = _mm(x_bf, wb, bias=bb, own=(oa_b, ob_b), tm=512, tn=SEG_B // 2, name="in_proj_b")
    y_b = _sgu_fwd(h_b, sgu_ln_g, sgu_ln_b, w_s3, bias_full)
    w_oa_f, w_ob_f, w_out_f = _assemble_out(g_oa, g_ob, g_out)

    (loss_row, dx_res, dh_a, d_o, d_yb, p_oa, p_ob, p_out, d_lng, d_lnb, d_ba) = _merge(
        x2, o, h_a, y_b, tgt, w_oa_f, w_ob_f, w_out_f, ln_g, ln_b)
    (dh_b, d_ws, d_bs_t, d_slg, d_slb, d_bb), (r_out,) = _sgu_bwd(h_b, d_yb, sgu_ln_g, sgu_ln_b, w_s3, w_st3, bias_full,
                                                                 (p_out,))
    d_wa, (r_oa,) = _mm(xt_bf, dh_a, out_dtype=BF16, parts=(p_oa,), tm=512, tn=512, name="dw_in_a")
    d_wb = _mm(xt_bf, dh_b, out_dtype=BF16, tm=512, tn=512, name="dw_in_b")
    p_hi, p_lo = _to_parts(d_wa, d_wb)
    dq, dk, dv, (r_hi,) = _attn_bwd(q, kt, k, vxt, d_o, o, lse, (p_hi,))
    (dh_c, p_uq, d_wkn, d_wv, d_gq, d_gkv, d_bc, d_wc), (r_lo, r_ob) = _mla_bwd(
        dq, dk, dv, h_c, xt_bf, gq, g_kv, wq, wkn, wv, c_t, sa_t, sb_t, (p_lo, p_ob))
    landed = (r_hi, r_lo, r_oa, r_ob, r_out)

    p_b_in = _from_segments(d_ba, d_bb, d_bc)
    p_w_ukv = jnp.concatenate([d_wkn.reshape(KV_LORA_RANK, MLA_HEADS, HEAD_PAD)[:, :, :QK_NOPE_DIM],
                               d_wv.reshape(KV_LORA_RANK, MLA_HEADS, HEAD_PAD)[:, :, :V_HEAD_DIM]], axis=-1)
    p_g_q = d_gq[:, :Q_LORA_RANK]
    p_b_s = d_bs_t[:, :SGU_GROUPS].T
    replicated = [p_b_in, p_g_q, d_gkv, p_w_ukv, d_slg, d_slb, d_ws, p_b_s, d_lng, d_lnb]
    return loss_row, ((dh_a, dh_b, dh_c), (wa, wb, wc), dx_res), landed, d_wc, p_uq, replicated


_NAMES = ["w_in", "b_in", "g_q", "w_uq", "g_kv", "w_ukv", "w_oa", "sgu_ln_g", "sgu_ln_b", "w_s", "b_s", "w_ob",
          "w_out", "ln_g", "ln_b"]
_REPLICATED = ["b_in", "g_q", "g_kv", "w_ukv", "sgu_ln_g", "sgu_ln_b", "w_s", "b_s", "ln_g", "ln_b"]


def _reduce_and_update(partials, weights, moms, vars_):
    loss_row, (dhs, ws, dx_res), landed, d_wc, p_uq, replicated = partials
    def piece(a):
        flat = a.reshape(-1)
        return jnp.pad(flat, (0, -flat.size % PACK_ALIGN))

    pieces = [piece(a) for a in replicated] + [piece(loss_row[0, :1])]
    pieces.append(jnp.zeros((N_DEV * PACK_R_ROWS * LANES - sum(p.size for p in pieces),), F32))
    rep_flat = jnp.concatenate(pieces)
    dx, c_all, g_uq, rep_all = _dx_tail(dhs, ws, dx_res, d_wc, p_uq, rep_flat.reshape(N_DEV, PACK_R_ROWS, LANES))
    g_in, g_oa, g_ob, g_out = _sum_landed(landed, c_all)
    rep_sum = rep_all.reshape(-1)
    grads, pos = dict(w_in=g_in, w_uq=g_uq, w_oa=g_oa, w_ob=g_ob, w_out=g_out), 0
    for nm in _REPLICATED:
        grads[nm] = rep_sum[pos:pos + weights[nm].size]
        pos += weights[nm].size + -weights[nm].size % PACK_ALIGN
    loss = rep_all.reshape(-1, LANES)[pos // LANES, 0]
    grads = {nm: grads[nm].reshape(weights[nm].shape) for nm in _NAMES}
    deltas, new_m, new_v = _adamw_all([weights[nm] for nm in _NAMES], [grads[nm] for nm in _NAMES],
                                      [moms[nm] for nm in _NAMES], [vars_[nm] for nm in _NAMES])
    return (loss, dx.reshape(1, SEQ, D_MODEL), *[grads[nm] for nm in _NAMES], *deltas, *new_m, *new_v)
```

```python
import math

import jax
import jax.numpy as jnp
from jax import lax
from jax.experimental import pallas as pl
from jax.experimental.pallas import tpu as pltpu

F32 = jnp.float32
BF16 = jnp.bfloat16

D_MODEL = 1024
SEQ = 2048
N_DEV = 8
MLA_HEADS = 8
Q_LORA_RANK = 384
KV_LORA_RANK = 128
QK_NOPE_DIM = 64
QK_ROPE_DIM = 32
V_HEAD_DIM = 64
QK_HEAD_DIM = QK_NOPE_DIM + QK_ROPE_DIM
MLA_WIDTH = MLA_HEADS * V_HEAD_DIM
ROPE_THETA = 10000.0
SGU_GROUPS = 8
SGU_GROUP_DIM = 64
SGU_WIDTH = SGU_GROUPS * SGU_GROUP_DIM
CHUNK = 128
RMS_EPS = 1e-6
LN_EPS = 1e-5
DN_ALPHA = 2.0 ** 0.25
IN_WIDTH = 4640
ATTN_SCALE = QK_HEAD_DIM ** -0.5

ADAM_LR = 0.001
ADAM_B1 = 0.9
ADAM_B2 = 0.999
ADAM_EPS = 1e-08
ADAM_WD = 0.01
ADAM_STEP = 10

LANES = 128
HEAD_PAD = 128
ROPE_LO = QK_NOPE_DIM
ROPE_MID = ROPE_LO + QK_ROPE_DIM // 2
ROPE_HI = ROPE_LO + QK_ROPE_DIM
CQ_PAD = 512

SEG_A = 2560
SEG_B = 1536
SEG_C = 768

PACK_R_ROWS = 272
PACK_ALIGN = 8 * LANES
VMEM_BIG = 56 * 1024 * 1024
VMEM_MID = 40 * 1024 * 1024


def _sigmoid(x):
    return 1.0 / (1.0 + jnp.exp(-x))


def _gelu_and_grad(x):
    c0 = math.sqrt(2.0 / math.pi)
    x2 = x * x
    t = jnp.tanh(c0 * (x + 0.044715 * x * x2))
    g = 0.5 * x * (1.0 + t)
    dg = 0.5 * (1.0 + t) + 0.5 * x * (1.0 - t * t) * (c0 * (1.0 + 3.0 * 0.044715 * x2))
    return g, dg


def _dot(a, b, dims):
    return lax.dot_general(a, b, (dims, ((), ())), preferred_element_type=F32)


_NN = ((1,), (0,))
_NT = ((1,), (1,))
_TN = ((0,), (0,))


def _store_grad(dh_ref, db_ref, col, val):
    cols = slice(col, col + val.shape[1])
    dh_ref[:, cols] = val.astype(BF16)
    db_ref[:, cols] += jnp.sum(val, axis=0, keepdims=True)


def _mm(a, b, *, tb=False, bias=None, add=None, out_dtype=F32, own=(), parts=(), tm, tn, name):
    m, k = a.shape
    n = b.shape[0] if tb else b.shape[1]
    assert m % tm == 0 and n % tn == 0 and not (own and parts)
    dims = _NT if tb else _NN
    nown = len(own) + len(parts)
    nm = m // tm
    nsteps = (n // tn) * nm

    def body(*refs):
        a_ref, b_ref = refs[0], refs[1]
        pos = 2
        r = _dot(a_ref[...], b_ref[...], dims)
        if bias is not None:
            r = r + refs[pos][...]; pos += 1
        if add is not None:
            r = r + refs[pos][...]; pos += 1
        own_refs = refs[pos:pos + nown]; pos += nown
        refs[pos][...] = r.astype(out_dtype)
        if nown:
            gat_refs = refs[pos + 1:pos + 1 + nown]
            send_sems, recv_sems, local_sems = refs[pos + 1 + nown:]
            step = pl.program_id(0) * nm + pl.program_id(1)
            if own:
                _gather_behind(own_refs, gat_refs, send_sems, recv_sems, local_sems, step, nsteps - 3, nsteps - 1)
            else:
                exchange = _exchange_parts(own_refs, gat_refs, send_sems, recv_sems, local_sems)
                _exchange_start(step == 0, exchange)
                _exchange_finish(step == nsteps - 1, exchange)

    b_spec = pl.BlockSpec((tn, k), lambda j, i: (j, 0)) if tb else pl.BlockSpec((k, tn), lambda j, i: (0, j))
    in_specs, args = [pl.BlockSpec((tm, k), lambda j, i: (i, 0)), b_spec], [a, b]
    if bias is not None:
        in_specs.append(pl.BlockSpec((1, tn), lambda j, i: (0, j))); args.append(bias)
    if add is not None:
        in_specs.append(pl.BlockSpec((tm, tn), lambda j, i: (i, j))); args.append(add)
    hbm = pl.BlockSpec(memory_space=pl.ANY)
    res = pl.pallas_call(
        body, name=name, grid=(n // tn, nm), in_specs=in_specs + [hbm] * nown,
        out_specs=[pl.BlockSpec((tm, tn), lambda j, i: (i, j))] + [hbm] * nown,
        out_shape=[jax.ShapeDtypeStruct((m, n), out_dtype)]
        + [jax.ShapeDtypeStruct((N_DEV,) + o.shape, o.dtype) for o in own]
        + [jax.ShapeDtypeStruct(p.shape, p.dtype) for p in parts],
        scratch_shapes=_exchange_sems(nown) if nown else [],
        compiler_params=pltpu.CompilerParams(dimension_semantics=("arbitrary", "arbitrary"), vmem_limit_bytes=VMEM_BIG),
    )(*args, *own, *parts)
    return (res[0], res[1:]) if nown else res[0]


def _rope(x, c, sa, sb):
    return x * c + pltpu.roll(x, LANES - 16, 1) * sa + pltpu.roll(x, 16, 1) * sb


def _rope_t(dy, c, sa, sb):
    return dy * c + pltpu.roll(dy * sa, 16, 1) + pltpu.roll(dy * sb, LANES - 16, 1)


def _mla_prep(h_c, gq, gkv, wq, wkn, wvx, c_t, sa_t, sb_t):
    tm = 256
    hw = MLA_HEADS * HEAD_PAD

    def body(cq_ref, ckv_ref, kpe_ref, gq_ref, gkv_ref, wq_ref, wkn_ref, wvx_ref, c_ref, sa_ref, sb_ref,
             q_ref, k_ref, kt_ref, vx_ref, vxt_ref):
        c, sa, sb = c_ref[...], sa_ref[...], sb_ref[...]
        cq = cq_ref[...]
        rq = lax.rsqrt(jnp.sum(cq * cq, axis=1, keepdims=True) * (1.0 / Q_LORA_RANK) + RMS_EPS)
        cqn = ((cq * rq) * gq_ref[...]).astype(BF16)
        qall = _dot(cqn, wq_ref[...], _NN)
        for h in range(MLA_HEADS):
            sl = slice(HEAD_PAD * h, HEAD_PAD * (h + 1))
            q_ref[:, sl] = (_rope(qall[:, sl], c, sa, sb) * ATTN_SCALE).astype(BF16)
        ckv = ckv_ref[...]
        rkv = lax.rsqrt(jnp.sum(ckv * ckv, axis=1, keepdims=True) * (1.0 / KV_LORA_RANK) + RMS_EPS)
        ckvn = ((ckv * rkv) * gkv_ref[...]).astype(BF16)
        knall = _dot(ckvn, wkn_ref[...], _NN)
        vall = _dot(ckvn, wvx_ref[...], _NN)
        kper = _rope(kpe_ref[...], c, sa, sb)
        ones_half = (lax.broadcasted_iota(jnp.int32, (tm, HEAD_PAD), 1) >= V_HEAD_DIM).astype(F32)
        for h in range(MLA_HEADS):
            sl = slice(HEAD_PAD * h, HEAD_PAD * (h + 1))
            kh = knall[:, sl] + kper
            vh = vall[:, sl] + ones_half
            k_ref[:, sl] = kh.astype(BF16)
            kt_ref[sl, :] = kh.T.astype(BF16)
            vx_ref[:, sl] = vh.astype(BF16)
            vxt_ref[sl, :] = vh.T.astype(BF16)

    full = lambda shape: pl.BlockSpec(shape, lambda i: (0, 0))
    tab = pl.BlockSpec((tm, LANES), lambda i: (i, 0))
    row = pl.BlockSpec((tm, hw), lambda i: (i, 0))
    col = pl.BlockSpec((hw, tm), lambda i: (0, i))
    return pl.pallas_call(
        body, name="mla_prep", grid=(SEQ // tm,),
        in_specs=[pl.BlockSpec((tm, CQ_PAD), lambda i: (i, 0)),
                  pl.BlockSpec((tm, LANES), lambda i: (i, CQ_PAD // LANES)),
                  pl.BlockSpec((tm, LANES), lambda i: (i, CQ_PAD // LANES + 1)),
                  full((1, CQ_PAD)), full((1, KV_LORA_RANK)),
                  full((CQ_PAD, hw)), full((KV_LORA_RANK, hw)), full((KV_LORA_RANK, hw)), tab, tab, tab],
        out_specs=[row, row, col, row, col],
        out_shape=[jax.ShapeDtypeStruct((SEQ, hw), BF16), jax.ShapeDtypeStruct((SEQ, hw), BF16),
                   jax.ShapeDtypeStruct((hw, SEQ), BF16), jax.ShapeDtypeStruct((SEQ, hw), BF16),
                   jax.ShapeDtypeStruct((hw, SEQ), BF16)],
        compiler_params=pltpu.CompilerParams(dimension_semantics=("arbitrary",), vmem_limit_bytes=VMEM_MID),
    )(h_c, h_c, h_c, gq, gkv, wq, wkn, wvx, c_t, sa_t, sb_t)


ATT_T = 512
ATT_STRIP = 64


def _attn_fwd(q, kt, vx, own):
    t, rs = ATT_T, ATT_STRIP
    nown = len(own)
    nq = SEQ // t
    nsteps = (MLA_HEADS // 2) * nq

    def body(q_ref, kt_ref, vx_ref, *rest):
        own_refs, (o_ref, l_ref), gat_refs = rest[:nown], rest[nown:nown + 2], rest[nown + 2:2 * nown + 2]
        s_scr, p_scr, m_scr, a_scr, acc_scr, send_sems, recv_sems, local_sems = rest[2 * nown + 2:]
        qi = pl.program_id(1)
        lane = lax.broadcasted_iota(jnp.int32, (t, LANES), 1)
        m_scr[...] = jnp.full((2, t, LANES), -1e30, F32)
        acc_scr[...] = jnp.zeros((2, t, LANES), F32)

        def block(j, masked):
            off = pl.multiple_of(j * t, t)
            for a in range(2):
                sl = slice(HEAD_PAD * a, HEAD_PAD * (a + 1))
                s_scr[a] = _dot(q_ref[:, sl], kt_ref[sl, pl.ds(off, t)], _NN)
                for r in range(t // rs):
                    rows = slice(rs * r, rs * (r + 1))
                    s = s_scr[a, rows, :]
                    if masked:
                        rowi = lax.broadcasted_iota(jnp.int32, (rs, t), 0) + rs * r
                        coli = lax.broadcasted_iota(jnp.int32, (rs, t), 1)
                        s = jnp.where(coli <= rowi, s, -1e30)
                    m_old = m_scr[a, rows, :]
                    m_new = jnp.maximum(m_old, jnp.max(s, axis=1, keepdims=True))
                    p_scr[a, rows, :] = jnp.exp(s - m_new[:, :1]).astype(BF16)
                    a_scr[a, rows, :] = jnp.exp(m_old - m_new)
                    m_scr[a, rows, :] = m_new
                acc_scr[a] = acc_scr[a] * a_scr[a] + _dot(p_scr[a], vx_ref[pl.ds(off, t), sl], _NN)

        def step(j, carry):
            block(j, False)
            return carry
        lax.fori_loop(0, qi, step, 0)
        block(qi, True)
        res = []
        for a in range(2):
            acc = acc_scr[a]
            l = acc[:, V_HEAD_DIM:V_HEAD_DIM + 1]
            res.append((acc / l, m_scr[a] + jnp.log(l)))
        o_ref[...] = jnp.where(lane < V_HEAD_DIM, res[0][0], pltpu.roll(res[1][0], V_HEAD_DIM, 1))
        l_ref[...] = jnp.where(lane < V_HEAD_DIM, res[0][1], res[1][1])
        _gather_behind(own_refs, gat_refs, send_sems, recv_sems, local_sems, pl.program_id(0) * nq + qi,
                       nsteps - 2, nsteps - 1)

    hbm = pl.BlockSpec(memory_space=pl.ANY)
    res = pl.pallas_call(
        body, name="attn_fwd", grid=(MLA_HEADS // 2, nq),
        in_specs=[pl.BlockSpec((t, 2 * HEAD_PAD), lambda p, i: (i, p)),
                  pl.BlockSpec((2 * HEAD_PAD, SEQ), lambda p, i: (p, 0)),
                  pl.BlockSpec((SEQ, 2 * HEAD_PAD), lambda p, i: (0, p))] + [hbm] * nown,
        out_specs=[pl.BlockSpec((t, LANES), lambda p, i: (i, p)),
                   pl.BlockSpec((t, LANES), lambda p, i: (i, p))] + [hbm] * nown,
        out_shape=[jax.ShapeDtypeStruct((SEQ, MLA_WIDTH), F32), jax.ShapeDtypeStruct((SEQ, MLA_WIDTH), F32)]
        + [jax.ShapeDtypeStruct((N_DEV,) + a.shape, a.dtype) for a in own],
        scratch_shapes=[pltpu.VMEM((2, t, t), F32), pltpu.VMEM((2, t, t), BF16), pltpu.VMEM((2, t, LANES), F32),
                        pltpu.VMEM((2, t, LANES), F32), pltpu.VMEM((2, t, LANES), F32)] + _exchange_sems(nown),
        compiler_params=pltpu.CompilerParams(dimension_semantics=("arbitrary", "arbitrary"), vmem_limit_bytes=VMEM_MID),
    )(q, kt, vx, *own)
    return res[0], res[1], res[2:]


def _exchange_parts(parts, lands, send_sems, recv_sems, local_sems):
    x, y, c = _mesh_pos()
    me = 4 * x + 2 * y + c
    peers = [(x, y, 1 - c), (1 - x, y, c), (x, 1 - y, c), (1 - x, 1 - y, c),
             (1 - x, y, 1 - c), (x, 1 - y, 1 - c), (1 - x, 1 - y, 1 - c)]
    remote, local = [], []
    for a, (part, land) in enumerate(zip(parts, lands)):
        for k, peer in enumerate(peers):
            t = 4 * peer[0] + 2 * peer[1] + peer[2]
            remote.append(_remote(part.at[t], land.at[me], send_sems, recv_sems, 7 * a + k, peer))
        local.append(pltpu.make_async_copy(part.at[me], land.at[me], local_sems.at[a]))
    return remote, local


def _exchange_start(first_step, exchange):
    remote, local = exchange

    @pl.when(first_step)
    def _():
        for cp in remote + local:
            cp.start()


def _exchange_finish(last_step, exchange):
    remote, local = exchange

    @pl.when(last_step)
    def _():
        for cp in remote:
            cp.wait_recv()
        for cp in remote:
            cp.wait_send()
        for cp in local:
            cp.wait()


def _exchange_sems(npart):
    return [pltpu.SemaphoreType.DMA((7 * npart,)), pltpu.SemaphoreType.DMA((7 * npart,)),
            pltpu.SemaphoreType.DMA((npart,))]


def _attn_bwd(q, kt, k, vxt, d_o, o, lse, parts):
    t, rs = ATT_T, ATT_STRIP
    nq = SEQ // t
    npart = len(parts)
    nsteps = MLA_HEADS // 2

    def body(q_ref, kt_ref, k_ref, vxt_ref, do_ref, o_ref, l_ref, *rest):
        part_refs, rest = rest[:npart], rest[npart:]
        dq_ref, dk_ref, dv_ref = rest[:3]
        land_refs, rest = rest[3:3 + npart], rest[3 + npart:]
        s_scr, dp_scr, p_scr, ds_scr, st_scr, send_sems, recv_sems, local_sems = rest
        exchange = _exchange_parts(part_refs, land_refs, send_sems, recv_sems, local_sems)
        _exchange_start(pl.program_id(0) == 0, exchange)
        dk_ref[...] = jnp.zeros_like(dk_ref)
        dv_ref[...] = jnp.zeros_like(dv_ref)
        lane = lax.broadcasted_iota(jnp.int32, (t, LANES), 1)

        def qtile(i, carry):
            ioff = pl.multiple_of(i * t, t)
            do_i = do_ref[pl.ds(ioff, t), :]
            o_i = o_ref[pl.ds(ioff, t), :]
            l_i = l_ref[pl.ds(ioff, t), :]
            for a in range(2):
                sl = slice(HEAD_PAD * a, HEAD_PAD * (a + 1))
                sel = (lane < V_HEAD_DIM) if a == 0 else (lane >= V_HEAD_DIM)
                doa = jnp.where(sel, do_i, 0.0)
                oa = o_i
                if a == 1:
                    doa = pltpu.roll(doa, V_HEAD_DIM, 1)
                    oa = pltpu.roll(o_i, V_HEAD_DIM, 1)
                st_scr[0] = jnp.broadcast_to(jnp.sum(doa * oa, axis=1, keepdims=True), (t, LANES))
                st_scr[1] = jnp.broadcast_to(l_i[:, V_HEAD_DIM * a:V_HEAD_DIM * a + 1], (t, LANES))
                doa_bf = doa.astype(BF16)
                qa = q_ref[pl.ds(ioff, t), sl]

                def block(j, masked, dq_acc, sl=sl, qa=qa, doa_bf=doa_bf):
                    joff = pl.multiple_of(j * t, t)
                    s_scr[...] = _dot(qa, kt_ref[sl, pl.ds(joff, t)], _NN)
                    dp_scr[...] = _dot(doa_bf, vxt_ref[sl, pl.ds(joff, t)], _NN)
                    for r in range(t // rs):
                        rows = slice(rs * r, rs * (r + 1))
                        p = jnp.exp(s_scr[rows, :] - st_scr[1, rows, :1])
                        if masked:
                            rowi = lax.broadcasted_iota(jnp.int32, (rs, t), 0) + rs * r
                            coli = lax.broadcasted_iota(jnp.int32, (rs, t), 1)
                            p = jnp.where(coli <= rowi, p, 0.0)
                        p_scr[rows, :] = p.astype(BF16)
                        ds_scr[rows, :] = (p * (dp_scr[rows, :] - st_scr[0, rows, :1])).astype(BF16)
                    dk_ref[pl.ds(joff, t), sl] += _dot(ds_scr[...], qa, _TN)
                    dv_ref[pl.ds(joff, t), sl] += _dot(p_scr[...], doa_bf, _TN)
                    return dq_acc + _dot(ds_scr[...], k_ref[pl.ds(joff, t), sl], _NN)

                dq_acc = lax.fori_loop(0, i, lambda j, acc: block(j, False, acc), jnp.zeros((t, HEAD_PAD), F32))
                dq_ref[pl.ds(ioff, t), sl] = block(i, True, dq_acc)
            return carry

        lax.fori_loop(0, nq, qtile, 0)
        _exchange_finish(pl.program_id(0) == nsteps - 1, exchange)

    hw = MLA_HEADS * HEAD_PAD
    wide = pl.BlockSpec((SEQ, 2 * HEAD_PAD), lambda p: (0, p))
    wide_t = pl.BlockSpec((2 * HEAD_PAD, SEQ), lambda p: (p, 0))
    narrow = pl.BlockSpec((SEQ, LANES), lambda p: (0, p))
    hbm = pl.BlockSpec(memory_space=pl.ANY)
    res = pl.pallas_call(
        body, name="attn_bwd", grid=(nsteps,),
        in_specs=[wide, wide_t, wide, wide_t, narrow, narrow, narrow] + [hbm] * npart,
        out_specs=[wide, wide, wide] + [hbm] * npart,
        out_shape=[jax.ShapeDtypeStruct((SEQ, hw), F32)] * 3 + [jax.ShapeDtypeStruct(p.shape, p.dtype) for p in parts],
        scratch_shapes=[pltpu.VMEM((t, t), F32), pltpu.VMEM((t, t), F32), pltpu.VMEM((t, t), BF16),
                        pltpu.VMEM((t, t), BF16), pltpu.VMEM((2, t, LANES), F32)] + _exchange_sems(npart),
        compiler_params=pltpu.CompilerParams(dimension_semantics=("arbitrary",), vmem_limit_bytes=VMEM_BIG),
    )(q, kt, k, vxt, d_o, o, lse, *parts)
    return res[0], res[1], res[2], res[3:]


def _sgu_math(u, v, zb, lg, lb, ws_ref, bias):
    ug, dug = _gelu_and_grad(u)
    vg, dvg = _gelu_and_grad(v)
    mu = jnp.mean(vg, axis=1, keepdims=True)
    xc = vg - mu
    rstd = lax.rsqrt(jnp.mean(xc * xc, axis=1, keepdims=True) + LN_EPS)
    xh = xc * rstd
    vn_bf = (xh * lg + lb).astype(BF16)
    grp = lax.broadcasted_iota(jnp.int32, (CHUNK, SGU_WIDTH), 1) // SGU_GROUP_DIM
    r_i = lax.broadcasted_iota(jnp.int32, (CHUNK, CHUNK), 0)
    c_i = lax.broadcasted_iota(jnp.int32, (CHUNK, CHUNK), 1)
    tri, tri_t = r_i >= c_i, r_i <= c_i
    mixed = bias
    for g in range(SGU_GROUPS):
        wt = jnp.where(tri, ws_ref[g], 0.0).astype(BF16)
        mixed = mixed + jnp.where(grp == g, _dot(wt, vn_bf, _NN), 0.0)
    sb = _sigmoid(zb)
    return ug, dug, dvg, rstd, xh, vn_bf, grp, tri, tri_t, mixed, sb


def _sgu_fwd(h_b, lg, lb, w_s, bias_full):
    def body(u_ref, v_ref, zb_ref, lg_ref, lb_ref, ws_ref, bias_ref, yb_ref):
        zb = zb_ref[...]
        ug, _, _, _, _, _, _, _, _, mixed, sb = _sgu_math(u_ref[...], v_ref[...], zb, lg_ref[...], lb_ref[...],
                                                       ws_ref, bias_ref[...])
        yb_ref[...] = (ug * mixed) * (zb * sb)

    blk = lambda c: pl.BlockSpec((CHUNK, SGU_WIDTH), lambda i, c=c: (i, c))
    full2 = lambda shape: pl.BlockSpec(shape, lambda i: (0, 0))
    return pl.pallas_call(
        body, name="sgu_fwd", grid=(SEQ // CHUNK,),
        in_specs=[blk(0), blk(1), blk(2), full2((1, SGU_WIDTH)), full2((1, SGU_WIDTH)),
                  pl.BlockSpec((SGU_GROUPS, CHUNK, CHUNK), lambda i: (0, 0, 0)), full2((CHUNK, SGU_WIDTH))],
        out_specs=pl.BlockSpec((CHUNK, SGU_WIDTH), lambda i: (i, 0)),
        out_shape=jax.ShapeDtypeStruct((SEQ, SGU_WIDTH), F32),
        compiler_params=pltpu.CompilerParams(dimension_semantics=("arbitrary",)),
    )(h_b, h_b, h_b, lg, lb, w_s, bias_full)


def _sgu_bwd(h_b, d_yb, lg, lb, w_s, w_st, bias_full, parts):
    nsteps = SEQ // CHUNK
    npart = len(parts)

    def body(u_ref, v_ref, zb_ref, dyb_ref, lg_ref, lb_ref, ws_ref, wst_ref, bias_ref, *rest):
        part_refs, rest = rest[:npart], rest[npart:]
        dhb_ref, dws_ref, dbs_ref, dlg_ref, dlb_ref, dbb_ref = rest[:6]
        land_refs, (dbias_acc, send_sems, recv_sems, local_sems) = rest[6:6 + npart], rest[6 + npart:]
        step = pl.program_id(0)
        exchange = _exchange_parts(part_refs, land_refs, send_sems, recv_sems, local_sems)
        _exchange_start(step == 0, exchange)

        @pl.when(step == 0)
        def _():
            dbb_ref[...] = jnp.zeros_like(dbb_ref)
            dws_ref[...] = jnp.zeros_like(dws_ref)
            dlg_ref[...] = jnp.zeros_like(dlg_ref)
            dlb_ref[...] = jnp.zeros_like(dlb_ref)
            dbias_acc[...] = jnp.zeros_like(dbias_acc)

        zb = zb_ref[...]
        lg = lg_ref[...]
        ug, dug, dvg, rstd, xh, vn_bf, grp, tri, tri_t, mixed, sb = _sgu_math(
            u_ref[...], v_ref[...], zb, lg, lb_ref[...], ws_ref, bias_ref[...])
        dyb = dyb_ref[...]
        dsgu = dyb * (zb * sb)
        dzb = dyb * (ug * mixed) * (sb * (1.0 + zb * (1.0 - sb)))
        du = dsgu * mixed * dug
        dmixed = dsgu * ug
        dbias_acc[...] += dmixed
        dvn = jnp.zeros((CHUNK, SGU_WIDTH), F32)
        for g in range(SGU_GROUPS):
            dm_g = jnp.where(grp == g, dmixed, 0.0).astype(BF16)
            wtt = jnp.where(tri_t, wst_ref[g], 0.0).astype(BF16)
            dvn = dvn + _dot(wtt, dm_g, _NN)
            dws_ref[g] += jnp.where(tri, _dot(dm_g, vn_bf, _NT), 0.0)
        dlg_ref[...] += jnp.sum(dvn * xh, axis=0, keepdims=True)
        dlb_ref[...] += jnp.sum(dvn, axis=0, keepdims=True)
        dxh = dvn * lg
        dvgel = rstd * (dxh - jnp.mean(dxh, axis=1, keepdims=True) - xh * jnp.mean(dxh * xh, axis=1, keepdims=True))
        _store_grad(dhb_ref, dbb_ref, 0, du)
        _store_grad(dhb_ref, dbb_ref, SGU_WIDTH, dvgel * dvg)
        _store_grad(dhb_ref, dbb_ref, 2 * SGU_WIDTH, dzb)

        @pl.when(step == nsteps - 1)
        def _():
            acc = dbias_acc[...]
            lane = lax.broadcasted_iota(jnp.int32, (CHUNK, LANES), 1)
            out = jnp.zeros((CHUNK, LANES), F32)
            for g in range(SGU_GROUPS):
                sg = jnp.sum(jnp.where(grp == g, acc, 0.0), axis=1, keepdims=True)
                out = jnp.where(lane == g, sg, out)
            dbs_ref[...] = out

        _exchange_finish(step == nsteps - 1, exchange)

    blk = lambda c: pl.BlockSpec((CHUNK, SGU_WIDTH), lambda i, c=c: (i, c))
    full2 = lambda shape: pl.BlockSpec(shape, lambda i: (0, 0))
    full3 = pl.BlockSpec((SGU_GROUPS, CHUNK, CHUNK), lambda i: (0, 0, 0))
    hbm = pl.BlockSpec(memory_space=pl.ANY)
    res = pl.pallas_call(
        body, name="sgu_bwd", grid=(nsteps,),
        in_specs=[blk(0), blk(1), blk(2), pl.BlockSpec((CHUNK, SGU_WIDTH), lambda i: (i, 0)),
                  full2((1, SGU_WIDTH)), full2((1, SGU_WIDTH)), full3, full3, full2((CHUNK, SGU_WIDTH))] + [hbm] * npart,
        out_specs=[pl.BlockSpec((CHUNK, SEG_B), lambda i: (i, 0)), full3, full2((CHUNK, LANES)),
                   full2((1, SGU_WIDTH)), full2((1, SGU_WIDTH)), full2((1, SEG_B))] + [hbm] * npart,
        out_shape=[jax.ShapeDtypeStruct((SEQ, SEG_B), BF16),
                   jax.ShapeDtypeStruct((SGU_GROUPS, CHUNK, CHUNK), F32),
                   jax.ShapeDtypeStruct((CHUNK, LANES), F32),
                   jax.ShapeDtypeStruct((1, SGU_WIDTH), F32), jax.ShapeDtypeStruct((1, SGU_WIDTH), F32),
                   jax.ShapeDtypeStruct((1, SEG_B), F32)] + [jax.ShapeDtypeStruct(p.shape, p.dtype) for p in parts],
        scratch_shapes=[pltpu.VMEM((CHUNK, SGU_WIDTH), F32)] + _exchange_sems(npart),
        compiler_params=pltpu.CompilerParams(dimension_semantics=("arbitrary",)),
    )(h_b, h_b, h_b, d_yb, lg, lb, w_s, w_st, bias_full, *parts)
    return res[:6], res[6:]


def _merge(x, o, h_a, y_b, target, w_oa, w_ob, w_out, ln_g, ln_b):
    tm = 256
    nsteps = SEQ // tm

    def body(x_ref, o_ref, ga_ref, gb_ref, za_ref, yb_ref, tgt_ref, woa_ref, wob_ref, wout_ref, lng_ref, lnb_ref,
             loss_ref, dxr_ref, dha_ref, do_ref, dyb_ref, poa_ref, pob_ref, pout_ref, dlng_ref, dlnb_ref, dba_ref,
             dwoa_ref, dwob_ref, dwout_ref):
        step = pl.program_id(0)

        @pl.when(step == 0)
        def _():
            for r in (loss_ref, dwoa_ref, dwob_ref, dwout_ref, dlng_ref, dlnb_ref, dba_ref):
                r[...] = jnp.zeros_like(r)

        o = o_ref[...]
        za = za_ref[...]
        sa = _sigmoid(za)
        ya_bf = (o * (za * sa)).astype(BF16)
        yb_bf = yb_ref[...].astype(BF16)
        woa, wob, wout = woa_ref[...], wob_ref[...], wout_ref[...]
        pa = _dot(ya_bf, woa, _NN)
        pb = _dot(yb_bf, wob, _NN)
        sga = _sigmoid(ga_ref[...])
        sgb = _sigmoid(gb_ref[...])
        merged_bf = (sga * pa + sgb * pb).astype(BF16)
        r = DN_ALPHA * x_ref[...] + _dot(merged_bf, wout, _NN)
        mu = jnp.mean(r, axis=1, keepdims=True)
        rc = r - mu
        rstd = lax.rsqrt(jnp.mean(rc * rc, axis=1, keepdims=True) + LN_EPS)
        xh = rc * rstd
        lng = lng_ref[...]
        y = xh * lng + lnb_ref[...]
        e = y - tgt_ref[...]
        loss_ref[...] += 0.5 * jnp.sum(jnp.sum(e * e, axis=1, keepdims=True) * (1.0 / D_MODEL), axis=0, keepdims=True)

        dy = e * (1.0 / D_MODEL)
        dlng_ref[...] += jnp.sum(dy * xh, axis=0, keepdims=True)
        dlnb_ref[...] += jnp.sum(dy, axis=0, keepdims=True)
        dxh = dy * lng
        dr = rstd * (dxh - jnp.mean(dxh, axis=1, keepdims=True) - xh * jnp.mean(dxh * xh, axis=1, keepdims=True))
        dxr_ref[...] = DN_ALPHA * dr
        dr_bf = dr.astype(BF16)
        dwout_ref[...] += _dot(merged_bf, dr_bf, _TN)
        dmerged = _dot(dr_bf, wout, _NT)
        dpa_bf = (dmerged * sga).astype(BF16)
        dpb_bf = (dmerged * sgb).astype(BF16)
        _store_grad(dha_ref, dba_ref, 0, dmerged * pa * (sga * (1.0 - sga)))
        _store_grad(dha_ref, dba_ref, D_MODEL, dmerged * pb * (sgb * (1.0 - sgb)))
        dwoa_ref[...] += _dot(ya_bf, dpa_bf, _TN)
        dwob_ref[...] += _dot(yb_bf, dpb_bf, _TN)
        dya = _dot(dpa_bf, woa, _NT)
        dyb_ref[...] = _dot(dpb_bf, wob, _NT)
        do_ref[...] = dya * (za * sa)
        _store_grad(dha_ref, dba_ref, 2 * D_MODEL, dya * o * (sa * (1.0 + za * (1.0 - sa))))

        @pl.when(step == nsteps - 1)
        def _():
            cols = D_MODEL // N_DEV
            for j in range(N_DEV):
                poa_ref[j] = dwoa_ref[:, cols * j:cols * (j + 1)].astype(BF16)
                pob_ref[j] = dwob_ref[:, cols * j:cols * (j + 1)].astype(BF16)
                pout_ref[j] = dwout_ref[cols * j:cols * (j + 1), :].astype(BF16)

    row = lambda w, c=0: pl.BlockSpec((tm, w), lambda i, c=c: (i, c))
    full = lambda shape: pl.BlockSpec(shape, lambda i: (0, 0))
    full3 = lambda shape: pl.BlockSpec(shape, lambda i: (0, 0, 0))
    return pl.pallas_call(
        body, name="merge", grid=(nsteps,),
        in_specs=[row(D_MODEL), row(MLA_WIDTH), row(D_MODEL, 0), row(D_MODEL, 1), row(MLA_WIDTH, 4), row(SGU_WIDTH),
                  row(D_MODEL), full((MLA_WIDTH, D_MODEL)), full((SGU_WIDTH, D_MODEL)), full((D_MODEL, D_MODEL)),
                  full((1, D_MODEL)), full((1, D_MODEL))],
        out_specs=[full((1, LANES)), row(D_MODEL), row(SEG_A), row(MLA_WIDTH), row(SGU_WIDTH),
                   full3((N_DEV, MLA_WIDTH, D_MODEL // N_DEV)), full3((N_DEV, SGU_WIDTH, D_MODEL // N_DEV)),
                   full3((N_DEV, D_MODEL // N_DEV, D_MODEL)), full((1, D_MODEL)), full((1, D_MODEL)), full((1, SEG_A))],
        out_shape=[jax.ShapeDtypeStruct((1, LANES), F32),
                   jax.ShapeDtypeStruct((SEQ, D_MODEL), F32), jax.ShapeDtypeStruct((SEQ, SEG_A), BF16),
                   jax.ShapeDtypeStruct((SEQ, MLA_WIDTH), F32), jax.ShapeDtypeStruct((SEQ, SGU_WIDTH), F32),
                   jax.ShapeDtypeStruct((N_DEV, MLA_WIDTH, D_MODEL // N_DEV), BF16),
                   jax.ShapeDtypeStruct((N_DEV, SGU_WIDTH, D_MODEL // N_DEV), BF16),
                   jax.ShapeDtypeStruct((N_DEV, D_MODEL // N_DEV, D_MODEL), BF16),
                   jax.ShapeDtypeStruct((1, D_MODEL), F32), jax.ShapeDtypeStruct((1, D_MODEL), F32),
                   jax.ShapeDtypeStruct((1, SEG_A), F32)],
        scratch_shapes=[pltpu.VMEM((MLA_WIDTH, D_MODEL), F32), pltpu.VMEM((SGU_WIDTH, D_MODEL), F32),
                        pltpu.VMEM((D_MODEL, D_MODEL), F32)],
        compiler_params=pltpu.CompilerParams(dimension_semantics=("arbitrary",), vmem_limit_bytes=VMEM_BIG),
    )(x, o, h_a, h_a, h_a, y_b, target, w_oa, w_ob, w_out, ln_g, ln_b)


def _mla_bwd(dq, dk, dv, h_c, xt_bf, gq, gkv, wq, wkn, wv, c_t, sa_t, sb_t, parts):
    tm = 256
    hw = MLA_HEADS * HEAD_PAD
    npart = len(parts)
    nsteps = SEQ // tm

    def body(dq_ref, dk_ref, dv_ref, cq_ref, ckv_ref, xt_ref, gq_ref, gkv_ref, wq_ref, wkn_ref, wv_ref, c_ref, sa_ref,
             sb_ref, *rest):
        part_refs, rest = rest[:npart], rest[npart:]
        dhc_ref, puq_ref, dwkn_ref, dwv_ref, dgq_ref, dgkv_ref, dbc_ref, dwc_ref = rest[:8]
        land_refs, (pre_ref, dwq_ref, dwc_acc, send_sems, recv_sems, local_sems) = rest[8:8 + npart], rest[8 + npart:]
        exchange = _exchange_parts(part_refs, land_refs, send_sems, recv_sems, local_sems)
        _exchange_start(pl.program_id(0) == 0, exchange)

        @pl.when(pl.program_id(0) == 0)
        def _():
            for r in (dwq_ref, dwc_acc, dwkn_ref, dwv_ref, dgq_ref, dgkv_ref, dbc_ref):
                r[...] = jnp.zeros_like(r)

        c, sa, sb = c_ref[...], sa_ref[...], sb_ref[...]
        lane = lax.broadcasted_iota(jnp.int32, (tm, LANES), 1)
        rope_lanes = jnp.logical_and(lane >= ROPE_LO, lane < ROPE_HI)

        cq = cq_ref[...]
        gq = gq_ref[...]
        rq = lax.rsqrt(jnp.sum(cq * cq, axis=1, keepdims=True) * (1.0 / Q_LORA_RANK) + RMS_EPS)
        nq = cq * rq
        cqn_bf = (nq * gq).astype(BF16)
        for h in range(MLA_HEADS):
            sl = slice(HEAD_PAD * h, HEAD_PAD * (h + 1))
            pre_ref[:, sl] = _rope_t(dq_ref[:, sl] * ATTN_SCALE, c, sa, sb).astype(BF16)
        dqpre_bf = pre_ref[...]
        dcqn = _dot(dqpre_bf, wq_ref[...], _NT)
        dwq_ref[...] += _dot(cqn_bf, dqpre_bf, _TN)
        dgq_ref[...] += jnp.sum(dcqn * nq, axis=0, keepdims=True)
        dnq = dcqn * gq
        _store_grad(dhc_ref, dbc_ref, 0,
                    rq * (dnq - nq * (jnp.sum(dnq * nq, axis=1, keepdims=True) * (1.0 / Q_LORA_RANK))))

        ckv = ckv_ref[...]
        gkv = gkv_ref[...]
        rkv = lax.rsqrt(jnp.sum(ckv * ckv, axis=1, keepdims=True) * (1.0 / KV_LORA_RANK) + RMS_EPS)
        nkv = ckv * rkv
        ckvn_bf = (nkv * gkv).astype(BF16)
        dk = dk_ref[...]
        dk_bf = dk.astype(BF16)
        dv_bf = dv_ref[...].astype(BF16)
        dckvn = _dot(dk_bf, wkn_ref[...], _NT) + _dot(dv_bf, wv_ref[...], _NT)
        dwkn_ref[...] += _dot(ckvn_bf, dk_bf, _TN)
        dwv_ref[...] += _dot(ckvn_bf, dv_bf, _TN)
        dgkv_ref[...] += jnp.sum(dckvn * nkv, axis=0, keepdims=True)
        dnkv = dckvn * gkv
        _store_grad(dhc_ref, dbc_ref, CQ_PAD, rkv * (
            dnkv - nkv * (jnp.sum(dnkv * nkv, axis=1, keepdims=True) * (1.0 / KV_LORA_RANK))))
        dkpe = jnp.zeros((tm, LANES), F32)
        for h in range(MLA_HEADS):
            dkpe = dkpe + dk[:, HEAD_PAD * h:HEAD_PAD * (h + 1)]
        _store_grad(dhc_ref, dbc_ref, CQ_PAD + LANES, _rope_t(jnp.where(rope_lanes, dkpe, 0.0), c, sa, sb))
        dwc_acc[...] += _dot(xt_ref[...], dhc_ref[...], _NN)

        @pl.when(pl.program_id(0) == SEQ // tm - 1)
        def _():
            dwc_ref[...] = dwc_acc[...].astype(BF16)
            rows = Q_LORA_RANK // N_DEV
            for j in range(N_DEV):
                for h in range(MLA_HEADS):
                    puq_ref[j, :, QK_HEAD_DIM * h:QK_HEAD_DIM * (h + 1)] = dwq_ref[
                        rows * j:rows * (j + 1), HEAD_PAD * h:HEAD_PAD * h + QK_HEAD_DIM].astype(BF16)

        _exchange_finish(pl.program_id(0) == nsteps - 1, exchange)

    full = lambda shape: pl.BlockSpec(shape, lambda i: (0, 0))
    row = lambda w, c=0: pl.BlockSpec((tm, w), lambda i, c=c: (i, c))
    hbm = pl.BlockSpec(memory_space=pl.ANY)
    res = pl.pallas_call(
        body, name="mla_bwd", grid=(nsteps,),
        in_specs=[row(hw), row(hw), row(hw), row(CQ_PAD, 0), row(LANES, CQ_PAD // LANES),
                  pl.BlockSpec((D_MODEL, tm), lambda i: (0, i)),
                  full((1, CQ_PAD)), full((1, KV_LORA_RANK)), full((CQ_PAD, hw)), full((KV_LORA_RANK, hw)),
                  full((KV_LORA_RANK, hw)), row(LANES), row(LANES), row(LANES)] + [hbm] * npart,
        out_specs=[row(SEG_C), pl.BlockSpec((N_DEV, Q_LORA_RANK // N_DEV, MLA_HEADS * QK_HEAD_DIM), lambda i: (0, 0, 0)),
                   full((KV_LORA_RANK, hw)), full((KV_LORA_RANK, hw)),
                   full((1, CQ_PAD)), full((1, KV_LORA_RANK)), full((1, SEG_C)), full((D_MODEL, SEG_C))] + [hbm] * npart,
        out_shape=[jax.ShapeDtypeStruct((SEQ, SEG_C), BF16),
                   jax.ShapeDtypeStruct((N_DEV, Q_LORA_RANK // N_DEV, MLA_HEADS * QK_HEAD_DIM), BF16),
                   jax.ShapeDtypeStruct((KV_LORA_RANK, hw), F32), jax.ShapeDtypeStruct((KV_LORA_RANK, hw), F32),
                   jax.ShapeDtypeStruct((1, CQ_PAD), F32), jax.ShapeDtypeStruct((1, KV_LORA_RANK), F32),
                   jax.ShapeDtypeStruct((1, SEG_C), F32), jax.ShapeDtypeStruct((D_MODEL, SEG_C), BF16)]
        + [jax.ShapeDtypeStruct(p.shape, p.dtype) for p in parts],
        scratch_shapes=[pltpu.VMEM((tm, hw), BF16), pltpu.VMEM((CQ_PAD, hw), F32), pltpu.VMEM((D_MODEL, SEG_C), F32)]
        + _exchange_sems(npart),
        compiler_params=pltpu.CompilerParams(dimension_semantics=("arbitrary",), vmem_limit_bytes=VMEM_MID),
    )(dq, dk, dv, h_c, h_c, xt_bf, gq, gkv, wq, wkn, wv, c_t, sa_t, sb_t, *parts)
    return res[:8], res[8:]


def _adamw_all(ws, gs, ms, vs):
    n = len(ws)
    c1 = 1.0 / (1.0 - ADAM_B1 ** ADAM_STEP)
    c2 = 1.0 / (1.0 - ADAM_B2 ** ADAM_STEP)

    def body(*refs):
        for idx in range(n):
            w, g, m, v = (refs[idx][...], refs[n + idx][...], refs[2 * n + idx][...], refs[3 * n + idx][...])
            m_new = ADAM_B1 * m + (1.0 - ADAM_B1) * g
            v_new = ADAM_B2 * v + (1.0 - ADAM_B2) * (g * g)
            delta = -ADAM_LR * ((m_new * c1) / (jnp.sqrt(v_new * c2) + ADAM_EPS) + ADAM_WD * w)
            refs[4 * n + idx][...] = delta
            refs[5 * n + idx][...] = m_new
            refs[6 * n + idx][...] = v_new

    shapes = [jax.ShapeDtypeStruct(w.shape, F32) for w in ws]
    outs = pl.pallas_call(
        body, name="adamw", out_shape=shapes * 3,
        compiler_params=pltpu.CompilerParams(vmem_limit_bytes=VMEM_BIG),
    )(*ws, *gs, *ms, *vs)
    return outs[:n], outs[n:2 * n], outs[2 * n:]


SHARD_W = IN_WIDTH // N_DEV

_PIECES = [(0, 384, 2, 0), (384, 512, 2, CQ_PAD), (512, 544, 2, CQ_PAD + LANES + ROPE_LO),
           (544, 1056, 0, 2 * D_MODEL), (1056, 1568, 1, 0), (1568, 2080, 1, SGU_WIDTH),
           (2080, 2592, 1, 2 * SGU_WIDTH), (2592, 3616, 0, 0), (3616, 4640, 0, D_MODEL)]


def _column_runs():
    runs = []
    for n0, n1, seg, d0 in _PIECES:
        for j in range(N_DEV):
            lo, hi = max(n0, j * SHARD_W), min(n1, (j + 1) * SHARD_W)
            if lo < hi:
                runs.append((j, lo - j * SHARD_W, hi - j * SHARD_W, seg, d0 + lo - n0))
    return runs


def _mesh_pos():
    return lax.axis_index("x"), lax.axis_index("y"), lax.axis_index("c")


def _remote(src, dst, send_sems, recv_sems, k, to):
    return pltpu.make_async_remote_copy(src_ref=src, dst_ref=dst, send_sem=send_sems.at[k], recv_sem=recv_sems.at[k],
                                        device_id=to, device_id_type=pl.DeviceIdType.MESH)


def _gather_exchange(gats, send_sems, recv_sems, meanwhile=None):
    x, y, c = _mesh_pos()
    me, sibling = (x, y, c), (x, y, 1 - c)
    chips = [(1 - x, y), (x, 1 - y), (1 - x, 1 - y)]

    def copy(a, k, blk, to):
        slab = gats[a].at[4 * blk[0] + 2 * blk[1] + blk[2]]
        return _remote(slab, slab, send_sems, recv_sems, 7 * a + k, to)

    arrays = range(len(gats))
    first = [copy(a, 1 + j, me, (*chip, c)) for j, chip in enumerate(chips) for a in arrays]
    first += [copy(a, 0, me, sibling) for a in arrays]
    for cp in first:
        cp.start()
    if meanwhile is not None:
        meanwhile()
    passed = []
    for j, chip in enumerate(chips):
        for a in arrays:
            copy(a, 1 + j, (*chip, c), me).wait_recv()
            fwd = copy(a, 4 + j, (*chip, c), sibling)
            fwd.start()
            passed.append(fwd)
    for a in arrays:
        copy(a, 0, sibling, me).wait_recv()
    for j, chip in enumerate(chips):
        for a in arrays:
            copy(a, 4 + j, (*chip, 1 - c), me).wait_recv()
    for cp in first + passed:
        cp.wait_send()


def _gather_behind(own, gats, send_sems, recv_sems, local_sems, step, mid, last):
    x, y, c = _mesh_pos()
    me, sibling = (x, y, c), (x, y, 1 - c)
    chips = [(1 - x, y), (x, 1 - y), (1 - x, 1 - y)]
    arrays = range(len(gats))

    def copy(a, k, blk, to, src=None):
        slab = gats[a].at[4 * blk[0] + 2 * blk[1] + blk[2]]
        return _remote(slab if src is None else src, slab, send_sems, recv_sems, 7 * a + k, to)

    first = [copy(a, 1 + j, me, (*chip, c), src=own[a]) for j, chip in enumerate(chips) for a in arrays]
    first += [copy(a, 0, me, sibling, src=own[a]) for a in arrays]
    local = [pltpu.make_async_copy(own[a], gats[a].at[4 * x + 2 * y + c], local_sems.at[a]) for a in arrays]
    passed = [copy(a, 4 + j, (*chip, c), sibling) for j, chip in enumerate(chips) for a in arrays]

    @pl.when(step == 0)
    def _():
        for cp in first + local:
            cp.start()

    @pl.when(step == mid)
    def _():
        for j, chip in enumerate(chips):
            for a in arrays:
                copy(a, 1 + j, (*chip, c), me).wait_recv()
        for cp in passed:
            cp.start()

    @pl.when(step == last)
    def _():
        for a in arrays:
            copy(a, 0, sibling, me).wait_recv()
        for j, chip in enumerate(chips):
            for a in arrays:
                copy(a, 4 + j, (*chip, 1 - c), me).wait_recv()
        for cp in first + passed:
            cp.wait_send()
        for cp in local:
            cp.wait()


def _gather_first(w_in, w_uq2, w_oa, w_ob, w_out, x2, pos_col, invf_lane):
    hw = MLA_HEADS * HEAD_PAD
    uq_rows = Q_LORA_RANK // N_DEV
    rows = 256

    def body(win_ref, wuq_ref, woa_ref, wob_ref, wout_ref, x_ref, pos_ref, invf_ref,
             wc_ref, wq_ref, winb_ref, oab_ref, obb_ref, outb_ref, xb_ref, xt_ref, c_ref, sa_ref, sb_ref,
             g_uq, blk0, send_sems, recv_sems):
        def local_work():
            for i in range(SEQ // rows):
                xi = x_ref[rows * i:rows * (i + 1), :]
                xb_ref[rows * i:rows * (i + 1), :] = xi.astype(BF16)
                xt_ref[:, rows * i:rows * (i + 1)] = xi.T.astype(BF16)
            ang = pos_ref[...].astype(F32) * invf_ref[...]
            cs, sn = jnp.cos(ang), jnp.sin(ang)
            lane = lax.broadcasted_iota(jnp.int32, ang.shape, 1)
            c_ref[...] = jnp.where(lane < ROPE_LO, 1.0, jnp.where(lane < ROPE_HI, cs, 0.0))
            sa_ref[...] = jnp.where(jnp.logical_and(lane >= ROPE_LO, lane < ROPE_MID), -sn, 0.0)
            sb_ref[...] = jnp.where(jnp.logical_and(lane >= ROPE_MID, lane < ROPE_HI), sn, 0.0)

        x, y, c = _mesh_pos()
        me = (x, y, c)
        winb_ref[...] = win_ref[0].astype(BF16)
        oab_ref[...] = woa_ref[0].astype(BF16)
        obb_ref[...] = wob_ref[0].astype(BF16)
        outb_ref[...] = wout_ref[0].astype(BF16)
        g_uq[4 * x + 2 * y + c] = wuq_ref[...].astype(BF16)

        chip0 = jnp.logical_and(x == 0, y == 0)
        south = c == 0
        half = D_MODEL // 2
        halves = [blk0.at[pl.ds(0, half)], blk0.at[pl.ds(half, half)]]

        def bcopy(k, to, part=None):
            ref = blk0 if part is None else halves[part]
            return _remote(ref, ref, send_sems, recv_sems, 7 + k, to)

        sends0 = [(0, (0, 0, 1), None), (1, (1, 0, 0), 0), (2, (0, 1, 0), 1), (3, (1, 0, 0), 1), (4, (0, 1, 0), 0)]

        @pl.when(jnp.logical_and(chip0, south))
        def _():
            blk0[...] = winb_ref[...]
            for k, to, part in sends0:
                bcopy(k, to, part).start()

        _gather_exchange([g_uq], send_sems, recv_sems, meanwhile=local_work)

        for (cx, cy), first_k, first_half, second_k in (((1, 0), 1, 0, 3), ((0, 1), 2, 1, 4)):
            @pl.when(jnp.logical_and(jnp.logical_and(x == cx, y == cy), south))
            def _(cx=cx, cy=cy, first_k=first_k, first_half=first_half, second_k=second_k):
                bcopy(first_k, me, first_half).wait_recv()
                onward = bcopy(5 + first_half, (1, 1, 0), first_half)
                onward.start()
                bcopy(second_k, me, 1 - first_half).wait_recv()
                north = bcopy(7, (cx, cy, 1))
                north.start()
                onward.wait_send()
                north.wait_send()

        @pl.when(jnp.logical_and(jnp.logical_and(x == 1, y == 1), south))
        def _():
            bcopy(5, me, 0).wait_recv()
            bcopy(6, me, 1).wait_recv()
            north = bcopy(7, (1, 1, 1))
            north.start()
            north.wait_send()

        @pl.when(jnp.logical_and(chip0, c == 1))
        def _():
            bcopy(0, me).wait_recv()

        @pl.when(jnp.logical_and(jnp.logical_not(chip0), c == 1))
        def _():
            bcopy(7, me).wait_recv()

        @pl.when(jnp.logical_and(chip0, south))
        def _():
            for k, to, part in sends0:
                bcopy(k, to, part).wait_send()

        for j, s0, s1, seg, d0 in _column_runs():
            if seg == 2:
                wc_ref[:, d0:d0 + (s1 - s0)] = blk0[:, s0:s1]
        zeros = lambda r, w: jnp.zeros((r, w), BF16)
        wc_ref[:, Q_LORA_RANK:CQ_PAD] = zeros(D_MODEL, CQ_PAD - Q_LORA_RANK)
        wc_ref[:, CQ_PAD + LANES:CQ_PAD + LANES + ROPE_LO] = zeros(D_MODEL, ROPE_LO)
        wc_ref[:, CQ_PAD + LANES + ROPE_HI:SEG_C] = zeros(D_MODEL, LANES - ROPE_HI)
        wq_ref[Q_LORA_RANK:CQ_PAD, :] = zeros(CQ_PAD - Q_LORA_RANK, hw)
        for h in range(MLA_HEADS):
            wq_ref[0:Q_LORA_RANK, HEAD_PAD * h + QK_HEAD_DIM:HEAD_PAD * (h + 1)] = zeros(Q_LORA_RANK, HEAD_PAD - QK_HEAD_DIM)
        for j in range(N_DEV):
            for h in range(MLA_HEADS):
                wq_ref[uq_rows * j:uq_rows * (j + 1), HEAD_PAD * h:HEAD_PAD * h + QK_HEAD_DIM] = g_uq[
                    j, :, QK_HEAD_DIM * h:QK_HEAD_DIM * (h + 1)]

    vmem = pl.BlockSpec(memory_space=pltpu.VMEM)
    return pl.pallas_call(
        body, name="gather_first",
        out_shape=[jax.ShapeDtypeStruct((D_MODEL, SEG_C), BF16), jax.ShapeDtypeStruct((CQ_PAD, hw), BF16),
                   jax.ShapeDtypeStruct(w_in.shape[1:], BF16), jax.ShapeDtypeStruct(w_oa.shape[1:], BF16),
                   jax.ShapeDtypeStruct(w_ob.shape[1:], BF16), jax.ShapeDtypeStruct(w_out.shape[1:], BF16),
                   jax.ShapeDtypeStruct((SEQ, D_MODEL), BF16), jax.ShapeDtypeStruct((D_MODEL, SEQ), BF16)]
        + [jax.ShapeDtypeStruct((SEQ, LANES), F32)] * 3,
        in_specs=[vmem] * 8, out_specs=[vmem] * 11,
        scratch_shapes=[pltpu.VMEM((N_DEV, uq_rows, MLA_HEADS * QK_HEAD_DIM), BF16), pltpu.VMEM((D_MODEL, SHARD_W), BF16),
                        pltpu.SemaphoreType.DMA((15,)), pltpu.SemaphoreType.DMA((15,))],
        compiler_params=pltpu.CompilerParams(vmem_limit_bytes=VMEM_BIG),
    )(w_in, w_uq2, w_oa, w_ob, w_out, x2, pos_col, invf_lane)


def _assemble_in(g_in):
    def body(g_ref, wa_ref, wb_ref):
        segs = [wa_ref, wb_ref]
        for j, s0, s1, seg, d0 in _column_runs():
            if seg < 2:
                segs[seg][:, d0:d0 + (s1 - s0)] = g_ref[j, :, s0:s1]

    return pl.pallas_call(
        body, name="assemble_in",
        out_shape=[jax.ShapeDtypeStruct((D_MODEL, SEG_A), BF16), jax.ShapeDtypeStruct((D_MODEL, SEG_B), BF16)],
        compiler_params=pltpu.CompilerParams(vmem_limit_bytes=VMEM_MID),
    )(g_in)


def _assemble_out(g_oa, g_ob, g_out):
    cols = D_MODEL // N_DEV

    def body(goa_ref, gob_ref, gout_ref, oa_ref, ob_ref, out_ref):
        for j in range(N_DEV):
            oa_ref[:, cols * j:cols * (j + 1)] = goa_ref[j]
            ob_ref[:, cols * j:cols * (j + 1)] = gob_ref[j]
            out_ref[cols * j:cols * (j + 1), :] = gout_ref[j]

    return pl.pallas_call(
        body, name="assemble_out",
        out_shape=[jax.ShapeDtypeStruct((MLA_WIDTH, D_MODEL), BF16), jax.ShapeDtypeStruct((SGU_WIDTH, D_MODEL), BF16),
                   jax.ShapeDtypeStruct((D_MODEL, D_MODEL), BF16)],
    )(g_oa, g_ob, g_out)


C_NAT = 544


P_IN_SPLIT = 896


def _to_parts(dwa, dwb):
    def body(dwa_ref, dwb_ref, phi_ref, plo_ref):
        phi_ref[0, :, 0:C_NAT] = jnp.zeros((P_IN_SPLIT, C_NAT), BF16)
        plo_ref[0, :, 0:C_NAT] = jnp.zeros((D_MODEL - P_IN_SPLIT, C_NAT), BF16)
        segs = [dwa_ref, dwb_ref]
        for j, s0, s1, seg, d0 in _column_runs():
            if seg < 2:
                phi_ref[j, :, s0:s1] = segs[seg][0:P_IN_SPLIT, d0:d0 + (s1 - s0)]
                plo_ref[j, :, s0:s1] = segs[seg][P_IN_SPLIT:D_MODEL, d0:d0 + (s1 - s0)]

    return pl.pallas_call(
        body, name="to_parts",
        out_shape=[jax.ShapeDtypeStruct((N_DEV, P_IN_SPLIT, SHARD_W), BF16),
                   jax.ShapeDtypeStruct((N_DEV, D_MODEL - P_IN_SPLIT, SHARD_W), BF16)],
        compiler_params=pltpu.CompilerParams(vmem_limit_bytes=VMEM_MID))(dwa, dwb)


def _dx_tail(dhs, ws, dx_res, dwc, p_uq, p_rep):
    ntile, sums_at = 8, 5
    tm = SEQ // ntile
    rep_rows = p_rep.shape[1]
    c_rows = D_MODEL // N_DEV
    spec = [((c_rows, C_NAT), BF16), (p_uq.shape[1:], BF16), ((rep_rows, LANES), F32)]
    n = len(spec)

    nseg = len(dhs)

    def body(*refs):
        dh_refs, w_refs = refs[:nseg], refs[nseg:2 * nseg]
        dxr_ref, dwc_ref, puq_ref, prep_ref, dx_ref, call_ref, guq_ref, repall_ref, pc_ref, c_all, rep_all = refs[
            2 * nseg:2 * nseg + 11]
        rest = refs[2 * nseg + 11:]
        ras, tbs, rbs = rest[0:n], rest[n:2 * n], rest[2 * n:3 * n]
        send_sems, recv_sems, gsend, grecv = rest[3 * n:]
        step = pl.program_id(0)
        x, y, c = _mesh_pos()
        me_idx = 4 * x + 2 * y + c
        me, sibling = (x, y, c), (x, y, 1 - c)
        others = [(1 - x, y), (x, 1 - y), (1 - x, 1 - y)]
        parts = [pc_ref, puq_ref, prep_ref]
        gats = [rep_all, c_all]

        def stage1(chip, a):
            return _remote(parts[a].at[2 * chip + (1 - c)], ras[a].at[chip], send_sems, recv_sems, 7 * a + chip, sibling)

        def stage2(k, a):
            cx, cy = others[k]
            return _remote(tbs[a].at[k], rbs[a].at[k], send_sems, recv_sems, 7 * a + 4 + k, (cx, cy, c))

        def gcopy(a, k, blk, to):
            slab = gats[a].at[4 * blk[0] + 2 * blk[1] + blk[2]]
            return _remote(slab, slab, gsend, grecv, 7 * a + k, to)

        def chip_sum(a, chip):
            return parts[a][2 * chip + c].astype(F32) + ras[a][chip].astype(F32)

        @pl.when(step == 0)
        def _():
            for j, s0, s1, seg, d0 in _column_runs():
                if seg == 2:
                    for r in range(N_DEV):
                        pc_ref[r, :, s0:s1] = dwc_ref[c_rows * r:c_rows * (r + 1), d0:d0 + (s1 - s0)]
            for chip in range(4):
                for a in range(n):
                    stage1(chip, a).start()

        @pl.when(step == 1)
        def _():
            for chip in range(4):
                for a in range(n):
                    stage1(chip, a).wait_recv()
            for k, (cx, cy) in enumerate(others):
                for a in range(n):
                    tbs[a][k] = chip_sum(a, 2 * cx + cy).astype(spec[a][1])
                    stage2(k, a).start()

        @pl.when(step == sums_at)
        def _():
            for k in range(3):
                for a in range(n):
                    stage2(k, a).wait_recv()
            sums = []
            for a in range(n):
                acc = chip_sum(a, 2 * x + y)
                for k in range(3):
                    acc = acc + rbs[a][k].astype(F32)
                sums.append(acc)
            c_all[me_idx] = sums[0].astype(BF16)
            guq_ref[...] = sums[1]
            rep_all[me_idx] = sums[2]
            for a in range(2):
                for j, chip in enumerate(others):
                    gcopy(a, 1 + j, me, (*chip, c)).start()
                gcopy(a, 0, me, sibling).start()

        acc = dxr_ref[...]
        for dh_ref, w_ref in zip(dh_refs, w_refs):
            acc = acc + _dot(dh_ref[...], w_ref[...], _NT)
        dx_ref[...] = acc

        @pl.when(step == ntile - 1)
        def _():
            for j, chip in enumerate(others):
                for a in range(2):
                    gcopy(a, 1 + j, (*chip, c), me).wait_recv()
                    gcopy(a, 4 + j, (*chip, c), sibling).start()
            for a in range(2):
                gcopy(a, 0, sibling, me).wait_recv()
                for j, chip in enumerate(others):
                    gcopy(a, 4 + j, (*chip, 1 - c), me).wait_recv()
            for a in range(2):
                gcopy(a, 0, me, sibling).wait_send()
                for j, chip in enumerate(others):
                    gcopy(a, 1 + j, me, (*chip, c)).wait_send()
                    gcopy(a, 4 + j, (*chip, c), sibling).wait_send()
            for a in range(n):
                for chip in range(4):
                    stage1(chip, a).wait_send()
                for k in range(3):
                    stage2(k, a).wait_send()
            call_ref[...] = c_all[...]
            repall_ref[...] = rep_all[...]

    row = lambda w: pl.BlockSpec((tm, w), lambda i: (i, 0))
    full = lambda shape: pl.BlockSpec(shape, lambda i: (0,) * len(shape))
    scratch = [pltpu.VMEM((N_DEV, c_rows, C_NAT), BF16), pltpu.VMEM((N_DEV, c_rows, C_NAT), BF16),
               pltpu.VMEM((N_DEV, rep_rows, LANES), F32)]
    for lead in (4, 3, 3):
        scratch += [pltpu.VMEM((lead,) + tuple(shape), dt) for shape, dt in spec]
    scratch += [pltpu.SemaphoreType.DMA((7 * n,)), pltpu.SemaphoreType.DMA((7 * n,)),
                pltpu.SemaphoreType.DMA((14,)), pltpu.SemaphoreType.DMA((14,))]
    return pl.pallas_call(
        body, name="dx_tail", grid=(SEQ // tm,),
        in_specs=[row(dh.shape[1]) for dh in dhs] + [full(w.shape) for w in ws]
        + [row(D_MODEL), full(dwc.shape), full(p_uq.shape), full(p_rep.shape)],
        out_specs=[row(D_MODEL), full((N_DEV, c_rows, C_NAT)), full(p_uq.shape[1:]), full((N_DEV, rep_rows, LANES))],
        out_shape=[jax.ShapeDtypeStruct((SEQ, D_MODEL), F32), jax.ShapeDtypeStruct((N_DEV, c_rows, C_NAT), BF16),
                   jax.ShapeDtypeStruct(p_uq.shape[1:], F32), jax.ShapeDtypeStruct((N_DEV, rep_rows, LANES), F32)],
        scratch_shapes=scratch,
        compiler_params=pltpu.CompilerParams(dimension_semantics=("arbitrary",), vmem_limit_bytes=VMEM_BIG),
    )(*dhs, *ws, dx_res, dwc, p_uq, p_rep)


def _sum_landed(landed, c_all):
    c_rows = D_MODEL // N_DEV

    def body(rhi_ref, rlo_ref, roa_ref, rob_ref, rout_ref, call_ref, gin_ref, goa_ref, gob_ref, gout_ref):
        def total(ref, sl):
            acc = ref[0, sl, :].astype(F32)
            for s in range(1, N_DEV):
                acc = acc + ref[s, sl, :].astype(F32)
            return acc

        x, y, c = _mesh_pos()
        dev0 = jnp.where(4 * x + 2 * y + c == 0, 1.0, 0.0)
        for j in range(N_DEV):
            sl = slice(c_rows * j, c_rows * (j + 1))
            below = c_rows * j < P_IN_SPLIT
            tot = total(rhi_ref, sl) if below else total(rlo_ref, slice(c_rows * j - P_IN_SPLIT, c_rows * (j + 1) - P_IN_SPLIT))
            gin_ref[0, sl, C_NAT:SHARD_W] = tot[:, C_NAT:SHARD_W]
            gin_ref[0, sl, 0:C_NAT] = tot[:, 0:C_NAT] + dev0 * call_ref[j].astype(F32)
        goa_ref[0] = total(roa_ref, slice(None))
        gob_ref[0] = total(rob_ref, slice(None))
        gout_ref[0] = total(rout_ref, slice(None))

    return pl.pallas_call(
        body, name="sum_landed",
        out_shape=[jax.ShapeDtypeStruct((1, D_MODEL, SHARD_W), F32)]
        + [jax.ShapeDtypeStruct((1,) + r.shape[1:], F32) for r in landed[2:]],
        compiler_params=pltpu.CompilerParams(vmem_limit_bytes=VMEM_MID),
    )(*landed, c_all)


_O_CQ, _O_CKV, _O_KPE, _O_ZA, _O_U, _O_V, _O_ZB, _O_GA, _O_GB = 0, 384, 512, 544, 1056, 1568, 2080, 2592, 3616


def _to_segments(w):
    z = lambda n: jnp.zeros(w.shape[:-1] + (n,), w.dtype)
    seg_a = jnp.concatenate([w[..., _O_GA:_O_GB], w[..., _O_GB:IN_WIDTH], w[..., _O_ZA:_O_U]], axis=-1)
    seg_b = jnp.concatenate([w[..., _O_U:_O_V], w[..., _O_V:_O_ZB], w[..., _O_ZB:_O_GA]], axis=-1)
    seg_c = jnp.concatenate([w[..., _O_CQ:_O_CKV], z(CQ_PAD - Q_LORA_RANK), w[..., _O_CKV:_O_KPE],
                             z(ROPE_LO), w[..., _O_KPE:_O_ZA], z(LANES - ROPE_HI)], axis=-1)
    return seg_a, seg_b, seg_c


def _from_segments(seg_a, seg_b, seg_c):
    kpe0 = CQ_PAD + LANES + ROPE_LO
    return jnp.concatenate([
        seg_c[..., 0:Q_LORA_RANK], seg_c[..., CQ_PAD:CQ_PAD + LANES], seg_c[..., kpe0:kpe0 + QK_ROPE_DIM],
        seg_a[..., 2 * D_MODEL:SEG_A], seg_b, seg_a[..., 0:2 * D_MODEL]], axis=-1)


def kernel(x, positions, w_in, b_in, g_q, w_uq, g_kv, w_ukv, w_oa, sgu_ln_g, sgu_ln_b, w_s, b_s, w_ob, w_out, ln_g, ln_b, loss_target, m_w_in, m_b_in, m_g_q, m_w_uq, m_g_kv, m_w_ukv, m_w_oa, m_sgu_ln_g, m_sgu_ln_b, m_w_s, m_b_s, m_w_ob, m_w_out, m_ln_g, m_ln_b, v_w_in, v_b_in, v_g_q, v_w_uq, v_g_kv, v_w_ukv, v_w_oa, v_sgu_ln_g, v_sgu_ln_b, v_w_s, v_b_s, v_w_ob, v_w_out, v_ln_g, v_ln_b):
    w_uq2 = w_uq[0].reshape(Q_LORA_RANK // N_DEV, MLA_HEADS * QK_HEAD_DIM)
    inv_freq = ROPE_THETA ** (-jnp.arange(0, QK_ROPE_DIM, 2, dtype=F32) / QK_ROPE_DIM)
    invf_lane = jnp.concatenate([jnp.zeros((ROPE_LO,), F32), inv_freq, inv_freq,
                                 jnp.zeros((LANES - ROPE_HI,), F32)]).reshape(1, LANES)
    first = _gather_first(w_in, w_uq2, w_oa, w_ob, w_out, x[0], positions.reshape(SEQ, 1), invf_lane)
    partials = _local_step(x[0], loss_target[0], first, b_in, g_q, g_kv, w_ukv, sgu_ln_g, sgu_ln_b, w_s, b_s, ln_g, ln_b)
    weights = dict(w_in=w_in, b_in=b_in, g_q=g_q, w_uq=w_uq, g_kv=g_kv, w_ukv=w_ukv, w_oa=w_oa, sgu_ln_g=sgu_ln_g,
                   sgu_ln_b=sgu_ln_b, w_s=w_s, b_s=b_s, w_ob=w_ob, w_out=w_out, ln_g=ln_g, ln_b=ln_b)
    moms = dict(w_in=m_w_in, b_in=m_b_in, g_q=m_g_q, w_uq=m_w_uq, g_kv=m_g_kv, w_ukv=m_w_ukv, w_oa=m_w_oa,
                sgu_ln_g=m_sgu_ln_g, sgu_ln_b=m_sgu_ln_b, w_s=m_w_s, b_s=m_b_s, w_ob=m_w_ob, w_out=m_w_out,
                ln_g=m_ln_g, ln_b=m_ln_b)
    vars_ = dict(w_in=v_w_in, b_in=v_b_in, g_q=v_g_q, w_uq=v_w_uq, g_kv=v_g_kv, w_ukv=v_w_ukv, w_oa=v_w_oa,
                 sgu_ln_g=v_sgu_ln_g, sgu_ln_b=v_sgu_ln_b, w_s=v_w_s, b_s=v_b_s, w_ob=v_w_ob, w_out=v_w_out,
                 ln_g=v_ln_g, ln_b=v_ln_b)
    return _reduce_and_update(partials, weights, moms, vars_)


def _local_step(x2, tgt, first, b_in, g_q, g_kv, w_ukv, sgu_ln_g, sgu_ln_b, w_s, b_s, ln_g, ln_b):
    wc, wq, win_b, oa_b, ob_b, out_b, x_bf, xt_bf, c_t, sa_t, sb_t = first
    ba, bb, bc = _to_segments(b_in)
    w_ukv_bf = w_ukv[0].astype(BF16)
    wkn = jnp.pad(w_ukv_bf[:, :, :QK_NOPE_DIM], ((0, 0), (0, 0), (0, HEAD_PAD - QK_NOPE_DIM))).reshape(KV_LORA_RANK, -1)
    wv = jnp.pad(w_ukv_bf[:, :, QK_NOPE_DIM:], ((0, 0), (0, 0), (0, HEAD_PAD - V_HEAD_DIM))).reshape(KV_LORA_RANK, -1)
    gq = jnp.pad(g_q, ((0, 0), (0, CQ_PAD - Q_LORA_RANK)))
    bias_full = jnp.repeat(b_s[0].T, SGU_GROUP_DIM, axis=1)
    w_s3 = w_s[0]
    w_st3 = jnp.swapaxes(w_s3, 1, 2)

    h_c = _mm(x_bf, wc, bias=bc, tm=512, tn=SEG_C, name="in_proj_c")
    q, k, kt, vx, vxt = _mla_prep(h_c, gq, g_kv, wq, wkn, wv, c_t, sa_t, sb_t)
    o, lse, (g_in,) = _attn_fwd(q, kt, vx, (win_b,))
    wa, wb = _assemble_in(g_in)
    h_a, (g_out,) = _mm(x_bf, wa, bias=ba, own=(out_b,), tm=512, tn=SEG_A // 2, name="in_proj_a")
    h_b, (g_oa, g_ob) = _mm(x_bf, wb, bias=bb, own=(oa_b, ob_b), tm=512, tn=SEG_B // 2, name="in_proj_b")
    y_b = _sgu_fwd(h_b, sgu_ln_g, sgu_ln_b, w_s3, bias_full)
    w_oa_f, w_ob_f, w_out_f = _assemble_out(g_oa, g_ob, g_out)

    (loss_row, dx_res, dh_a, d_o, d_yb, p_oa, p_ob, p_out, d_lng, d_lnb, d_ba) = _merge(
        x2, o, h_a, y_b, tgt, w_oa_f, w_ob_f, w_out_f, ln_g, ln_b)
    (dh_b, d_ws, d_bs_t, d_slg, d_slb, d_bb), (r_out,) = _sgu_bwd(h_b, d_yb, sgu_ln_g, sgu_ln_b, w_s3, w_st3, bias_full,
                                                                 (p_out,))
    d_wa, (r_oa,) = _mm(xt_bf, dh_a, out_dtype=BF16, parts=(p_oa,), tm=512, tn=512, name="dw_in_a")
    d_wb = _mm(xt_bf, dh_b, out_dtype=BF16, tm=512, tn=512, name="dw_in_b")
    p_hi, p_lo = _to_parts(d_wa, d_wb)
    dq, dk, dv, (r_hi,) = _attn_bwd(q, kt, k, vxt, d_o, o, lse, (p_hi,))
    (dh_c, p_uq, d_wkn, d_wv, d_gq, d_gkv, d_bc, d_wc), (r_lo, r_ob) = _mla_bwd(
        dq, dk, dv, h_c, xt_bf, gq, g_kv, wq, wkn, wv, c_t, sa_t, sb_t, (p_lo, p_ob))
    landed = (r_hi, r_lo, r_oa, r_ob, r_out)

    p_b_in = _from_segments(d_ba, d_bb, d_bc)
    p_w_ukv = jnp.concatenate([d_wkn.reshape(KV_LORA_RANK, MLA_HEADS, HEAD_PAD)[:, :, :QK_NOPE_DIM],
                               d_wv.reshape(KV_LORA_RANK, MLA_HEADS, HEAD_PAD)[:, :, :V_HEAD_DIM]], axis=-1)
    p_g_q = d_gq[:, :Q_LORA_RANK]
    p_b_s = d_bs_t[:, :SGU_GROUPS].T
    replicated = [p_b_in, p_g_q, d_gkv, p_w_ukv, d_slg, d_slb, d_ws, p_b_s, d_lng, d_lnb]
    return loss_row, ((dh_a, dh_b, dh_c), (wa, wb, wc), dx_res), landed, d_wc, p_uq, replicated


_NAMES = ["w_in", "b_in", "g_q", "w_uq", "g_kv", "w_ukv", "w_oa", "sgu_ln_g", "sgu_ln_b", "w_s", "b_s", "w_ob",
          "w_out", "ln_g", "ln_b"]
_REPLICATED = ["b_in", "g_q", "g_kv", "w_ukv", "sgu_ln_g", "sgu_ln_b", "w_s", "b_s", "ln_g", "ln_b"]


def _reduce_and_update(partials, weights, moms, vars_):
    loss_row, (dhs, ws, dx_res), landed, d_wc, p_uq, replicated = partials
    def piece(a):
        flat = a.reshape(-1)
        return jnp.pad(flat, (0, -flat.size % PACK_ALIGN))

    rep_flat = jnp.concatenate([piece(a) for a in replicated] + [piece(loss_row[0, :1])])
    rep_flat = jnp.pad(rep_flat, (0, N_DEV * PACK_R_ROWS * LANES - rep_flat.size))
    dx, c_all, g_uq, rep_all = _dx_tail(dhs, ws, dx_res, d_wc, p_uq, rep_flat.reshape(N_DEV, PACK_R_ROWS, LANES))
    g_in, g_oa, g_ob, g_out = _sum_landed(landed, c_all)
    rep_sum = rep_all.reshape(-1)
    grads, pos = dict(w_in=g_in, w_uq=g_uq, w_oa=g_oa, w_ob=g_ob, w_out=g_out), 0
    for nm in _REPLICATED:
        grads[nm] = rep_sum[pos:pos + weights[nm].size]
        pos += weights[nm].size + -weights[nm].size % PACK_ALIGN
    loss = rep_sum[pos]
    grads = {nm: grads[nm].reshape(weights[nm].shape) for nm in _NAMES}
    deltas, new_m, new_v = _adamw_all([weights[nm] for nm in _NAMES], [grads[nm] for nm in _NAMES],
                                      [moms[nm] for nm in _NAMES], [vars_[nm] for nm in _NAMES])
    return (loss, dx.reshape(1, SEQ, D_MODEL), *[grads[nm] for nm in _NAMES], *deltas, *new_m, *new_v)
```

```python
import math

import jax
import jax.numpy as jnp
from jax import lax
from jax.experimental import pallas as pl
from jax.experimental.pallas import tpu as pltpu

F32 = jnp.float32
BF16 = jnp.bfloat16

D_MODEL = 1024
SEQ = 2048
N_DEV = 8
MLA_HEADS = 8
Q_LORA_RANK = 384
KV_LORA_RANK = 128
QK_NOPE_DIM = 64
QK_ROPE_DIM = 32
V_HEAD_DIM = 64
QK_HEAD_DIM = QK_NOPE_DIM + QK_ROPE_DIM
MLA_WIDTH = MLA_HEADS * V_HEAD_DIM
ROPE_THETA = 10000.0
SGU_GROUPS = 8
SGU_GROUP_DIM = 64
SGU_WIDTH = SGU_GROUPS * SGU_GROUP_DIM
CHUNK = 128
RMS_EPS = 1e-6
LN_EPS = 1e-5
DN_ALPHA = 2.0 ** 0.25
IN_WIDTH = 4640
ATTN_SCALE = QK_HEAD_DIM ** -0.5

ADAM_LR = 0.001
ADAM_B1 = 0.9
ADAM_B2 = 0.999
ADAM_EPS = 1e-08
ADAM_WD = 0.01
ADAM_STEP = 10

LANES = 128
HEAD_PAD = 128
ROPE_LO = QK_NOPE_DIM
ROPE_MID = ROPE_LO + QK_ROPE_DIM // 2
ROPE_HI = ROPE_LO + QK_ROPE_DIM
CQ_PAD = 512

SEG_A = 2560
SEG_B = 1536
SEG_C = 768

PACK_R_ROWS = 272
PACK_ALIGN = 8 * LANES
VMEM_BIG = 56 * 1024 * 1024
VMEM_MID = 40 * 1024 * 1024


def _sigmoid(x):
    return 1.0 / (1.0 + jnp.exp(-x))


def _gelu_and_grad(x):
    c0 = math.sqrt(2.0 / math.pi)
    x2 = x * x
    t = jnp.tanh(c0 * (x + 0.044715 * x * x2))
    g = 0.5 * x * (1.0 + t)
    dg = 0.5 * (1.0 + t) + 0.5 * x * (1.0 - t * t) * (c0 * (1.0 + 3.0 * 0.044715 * x2))
    return g, dg


def _dot(a, b, dims):
    return lax.dot_general(a, b, (dims, ((), ())), preferred_element_type=F32)


_NN = ((1,), (0,))
_NT = ((1,), (1,))
_TN = ((0,), (0,))


def _store_grad(dh_ref, db_ref, col, val):
    cols = slice(col, col + val.shape[1])
    dh_ref[:, cols] = val.astype(BF16)
    db_ref[:, cols] += jnp.sum(val, axis=0, keepdims=True)


def _mm(a, b, *, tb=False, bias=None, add=None, out_dtype=F32, own=(), parts=(), tm, tn, name):
    m, k = a.shape
    n = b.shape[0] if tb else b.shape[1]
    assert m % tm == 0 and n % tn == 0 and not (own and parts)
    dims = _NT if tb else _NN
    nown = len(own) + len(parts)
    nm = m // tm
    nsteps = (n // tn) * nm

    def body(*refs):
        a_ref, b_ref = refs[0], refs[1]
        pos = 2
        r = _dot(a_ref[...], b_ref[...], dims)
        if bias is not None:
            r = r + refs[pos][...]; pos += 1
        if add is not None:
            r = r + refs[pos][...]; pos += 1
        own_refs = refs[pos:pos + nown]; pos += nown
        refs[pos][...] = r.astype(out_dtype)
        if nown:
            gat_refs = refs[pos + 1:pos + 1 + nown]
            send_sems, recv_sems, local_sems = refs[pos + 1 + nown:]
            step = pl.program_id(0) * nm + pl.program_id(1)
            if own:
                _gather_behind(own_refs, gat_refs, send_sems, recv_sems, local_sems, step, nsteps - 2, nsteps - 1)
            else:
                exchange = _exchange_parts(own_refs, gat_refs, send_sems, recv_sems, local_sems)
                _exchange_start(step == 0, exchange)
                _exchange_finish(step == nsteps - 1, exchange)

    b_spec = pl.BlockSpec((tn, k), lambda j, i: (j, 0)) if tb else pl.BlockSpec((k, tn), lambda j, i: (0, j))
    in_specs, args = [pl.BlockSpec((tm, k), lambda j, i: (i, 0)), b_spec], [a, b]
    if bias is not None:
        in_specs.append(pl.BlockSpec((1, tn), lambda j, i: (0, j))); args.append(bias)
    if add is not None:
        in_specs.append(pl.BlockSpec((tm, tn), lambda j, i: (i, j))); args.append(add)
    hbm = pl.BlockSpec(memory_space=pl.ANY)
    res = pl.pallas_call(
        body, name=name, grid=(n // tn, nm), in_specs=in_specs + [hbm] * nown,
        out_specs=[pl.BlockSpec((tm, tn), lambda j, i: (i, j))] + [hbm] * nown,
        out_shape=[jax.ShapeDtypeStruct((m, n), out_dtype)]
        + [jax.ShapeDtypeStruct((N_DEV,) + o.shape, o.dtype) for o in own]
        + [jax.ShapeDtypeStruct(p.shape, p.dtype) for p in parts],
        scratch_shapes=_exchange_sems(nown) if nown else [],
        compiler_params=pltpu.CompilerParams(dimension_semantics=("arbitrary", "arbitrary"), vmem_limit_bytes=VMEM_BIG),
    )(*args, *own, *parts)
    return (res[0], res[1:]) if nown else res[0]


def _rope(x, c, sa, sb):
    return x * c + pltpu.roll(x, LANES - 16, 1) * sa + pltpu.roll(x, 16, 1) * sb


def _rope_t(dy, c, sa, sb):
    return dy * c + pltpu.roll(dy * sa, 16, 1) + pltpu.roll(dy * sb, LANES - 16, 1)


def _mla_prep(h_c, gq, gkv, wq, wkn, wvx, c_t, sa_t, sb_t):
    tm = 256
    hw = MLA_HEADS * HEAD_PAD

    def body(cq_ref, ckv_ref, kpe_ref, gq_ref, gkv_ref, wq_ref, wkn_ref, wvx_ref, c_ref, sa_ref, sb_ref,
             q_ref, k_ref, kt_ref, vx_ref, vxt_ref):
        c, sa, sb = c_ref[...], sa_ref[...], sb_ref[...]
        cq = cq_ref[...]
        rq = lax.rsqrt(jnp.sum(cq * cq, axis=1, keepdims=True) * (1.0 / Q_LORA_RANK) + RMS_EPS)
        cqn = ((cq * rq) * gq_ref[...]).astype(BF16)
        qall = _dot(cqn, wq_ref[...], _NN)
        for h in range(MLA_HEADS):
            sl = slice(HEAD_PAD * h, HEAD_PAD * (h + 1))
            q_ref[:, sl] = (_rope(qall[:, sl], c, sa, sb) * ATTN_SCALE).astype(BF16)
        ckv = ckv_ref[...]
        rkv = lax.rsqrt(jnp.sum(ckv * ckv, axis=1, keepdims=True) * (1.0 / KV_LORA_RANK) + RMS_EPS)
        ckvn = ((ckv * rkv) * gkv_ref[...]).astype(BF16)
        knall = _dot(ckvn, wkn_ref[...], _NN)
        vall = _dot(ckvn, wvx_ref[...], _NN)
        kper = _rope(kpe_ref[...], c, sa, sb)
        ones_half = (lax.broadcasted_iota(jnp.int32, (tm, HEAD_PAD), 1) >= V_HEAD_DIM).astype(F32)
        for h in range(MLA_HEADS):
            sl = slice(HEAD_PAD * h, HEAD_PAD * (h + 1))
            kh = knall[:, sl] + kper
            vh = vall[:, sl] + ones_half
            k_ref[:, sl] = kh.astype(BF16)
            kt_ref[sl, :] = kh.T.astype(BF16)
            vx_ref[:, sl] = vh.astype(BF16)
            vxt_ref[sl, :] = vh.T.astype(BF16)

    full = lambda shape: pl.BlockSpec(shape, lambda i: (0, 0))
    tab = pl.BlockSpec((tm, LANES), lambda i: (i, 0))
    row = pl.BlockSpec((tm, hw), lambda i: (i, 0))
    col = pl.BlockSpec((hw, tm), lambda i: (0, i))
    return pl.pallas_call(
        body, name="mla_prep", grid=(SEQ // tm,),
        in_specs=[pl.BlockSpec((tm, CQ_PAD), lambda i: (i, 0)),
                  pl.BlockSpec((tm, LANES), lambda i: (i, CQ_PAD // LANES)),
                  pl.BlockSpec((tm, LANES), lambda i: (i, CQ_PAD // LANES + 1)),
                  full((1, CQ_PAD)), full((1, KV_LORA_RANK)),
                  full((CQ_PAD, hw)), full((KV_LORA_RANK, hw)), full((KV_LORA_RANK, hw)), tab, tab, tab],
        out_specs=[row, row, col, row, col],
        out_shape=[jax.ShapeDtypeStruct((SEQ, hw), BF16), jax.ShapeDtypeStruct((SEQ, hw), BF16),
                   jax.ShapeDtypeStruct((hw, SEQ), BF16), jax.ShapeDtypeStruct((SEQ, hw), BF16),
                   jax.ShapeDtypeStruct((hw, SEQ), BF16)],
        compiler_params=pltpu.CompilerParams(dimension_semantics=("arbitrary",), vmem_limit_bytes=VMEM_MID),
    )(h_c, h_c, h_c, gq, gkv, wq, wkn, wvx, c_t, sa_t, sb_t)


ATT_T = 512
ATT_STRIP = 64


def _attn_fwd(q, kt, vx, own):
    t, rs = ATT_T, ATT_STRIP
    nown = len(own)
    nq = SEQ // t
    nsteps = (MLA_HEADS // 2) * nq

    def body(q_ref, kt_ref, vx_ref, *rest):
        own_refs, (o_ref, l_ref), gat_refs = rest[:nown], rest[nown:nown + 2], rest[nown + 2:2 * nown + 2]
        s_scr, p_scr, m_scr, a_scr, acc_scr, send_sems, recv_sems, local_sems = rest[2 * nown + 2:]
        qi = pl.program_id(1)
        lane = lax.broadcasted_iota(jnp.int32, (t, LANES), 1)
        m_scr[...] = jnp.full((2, t, LANES), -1e30, F32)
        acc_scr[...] = jnp.zeros((2, t, LANES), F32)

        def block(j, masked):
            off = pl.multiple_of(j * t, t)
            for a in range(2):
                sl = slice(HEAD_PAD * a, HEAD_PAD * (a + 1))
                s_scr[a] = _dot(q_ref[:, sl], kt_ref[sl, pl.ds(off, t)], _NN)
                for r in range(t // rs):
                    rows = slice(rs * r, rs * (r + 1))
                    s = s_scr[a, rows, :]
                    if masked:
                        rowi = lax.broadcasted_iota(jnp.int32, (rs, t), 0) + rs * r
                        coli = lax.broadcasted_iota(jnp.int32, (rs, t), 1)
                        s = jnp.where(coli <= rowi, s, -1e30)
                    m_old = m_scr[a, rows, :]
                    m_new = jnp.maximum(m_old, jnp.max(s, axis=1, keepdims=True))
                    p_scr[a, rows, :] = jnp.exp(s - m_new[:, :1]).astype(BF16)
                    a_scr[a, rows, :] = jnp.exp(m_old - m_new)
                    m_scr[a, rows, :] = m_new
                acc_scr[a] = acc_scr[a] * a_scr[a] + _dot(p_scr[a], vx_ref[pl.ds(off, t), sl], _NN)

        def step(j, carry):
            block(j, False)
            return carry
        lax.fori_loop(0, qi, step, 0)
        block(qi, True)
        res = []
        for a in range(2):
            acc = acc_scr[a]
            l = acc[:, V_HEAD_DIM:V_HEAD_DIM + 1]
            res.append((acc / l, m_scr[a] + jnp.log(l)))
        o_ref[...] = jnp.where(lane < V_HEAD_DIM, res[0][0], pltpu.roll(res[1][0], V_HEAD_DIM, 1))
        l_ref[...] = jnp.where(lane < V_HEAD_DIM, res[0][1], res[1][1])
        _gather_behind(own_refs, gat_refs, send_sems, recv_sems, local_sems, pl.program_id(0) * nq + qi,
                       nsteps - 2, nsteps - 1)

    hbm = pl.BlockSpec(memory_space=pl.ANY)
    res = pl.pallas_call(
        body, name="attn_fwd", grid=(MLA_HEADS // 2, nq),
        in_specs=[pl.BlockSpec((t, 2 * HEAD_PAD), lambda p, i: (i, p)),
                  pl.BlockSpec((2 * HEAD_PAD, SEQ), lambda p, i: (p, 0)),
                  pl.BlockSpec((SEQ, 2 * HEAD_PAD), lambda p, i: (0, p))] + [hbm] * nown,
        out_specs=[pl.BlockSpec((t, LANES), lambda p, i: (i, p)),
                   pl.BlockSpec((t, LANES), lambda p, i: (i, p))] + [hbm] * nown,
        out_shape=[jax.ShapeDtypeStruct((SEQ, MLA_WIDTH), F32), jax.ShapeDtypeStruct((SEQ, MLA_WIDTH), F32)]
        + [jax.ShapeDtypeStruct((N_DEV,) + a.shape, a.dtype) for a in own],
        scratch_shapes=[pltpu.VMEM((2, t, t), F32), pltpu.VMEM((2, t, t), BF16), pltpu.VMEM((2, t, LANES), F32),
                        pltpu.VMEM((2, t, LANES), F32), pltpu.VMEM((2, t, LANES), F32)] + _exchange_sems(nown),
        compiler_params=pltpu.CompilerParams(dimension_semantics=("arbitrary", "arbitrary"), vmem_limit_bytes=VMEM_MID),
    )(q, kt, vx, *own)
    return res[0], res[1], res[2:]


def _exchange_parts(parts, lands, send_sems, recv_sems, local_sems):
    x, y, c = _mesh_pos()
    me = 4 * x + 2 * y + c
    peers = [(x, y, 1 - c), (1 - x, y, c), (x, 1 - y, c), (1 - x, 1 - y, c),
             (1 - x, y, 1 - c), (x, 1 - y, 1 - c), (1 - x, 1 - y, 1 - c)]
    remote, local = [], []
    for a, (part, land) in enumerate(zip(parts, lands)):
        for k, peer in enumerate(peers):
            t = 4 * peer[0] + 2 * peer[1] + peer[2]
            remote.append(_remote(part.at[t], land.at[me], send_sems, recv_sems, 7 * a + k, peer))
        local.append(pltpu.make_async_copy(part.at[me], land.at[me], local_sems.at[a]))
    return remote, local


def _exchange_start(first_step, exchange):
    remote, local = exchange

    @pl.when(first_step)
    def _():
        for cp in remote + local:
            cp.start()


def _exchange_finish(last_step, exchange):
    remote, local = exchange

    @pl.when(last_step)
    def _():
        for cp in remote:
            cp.wait_recv()
        for cp in remote:
            cp.wait_send()
        for cp in local:
            cp.wait()


def _exchange_sems(npart):
    return [pltpu.SemaphoreType.DMA((7 * npart,)), pltpu.SemaphoreType.DMA((7 * npart,)),
            pltpu.SemaphoreType.DMA((npart,))]


def _attn_bwd(q, kt, k, vxt, d_o, o, lse, parts):
    t, rs = ATT_T, ATT_STRIP
    nq = SEQ // t
    npart = len(parts)
    nsteps = MLA_HEADS // 2

    def body(q_ref, kt_ref, k_ref, vxt_ref, do_ref, o_ref, l_ref, *rest):
        part_refs, rest = rest[:npart], rest[npart:]
        dq_ref, dk_ref, dv_ref = rest[:3]
        land_refs, rest = rest[3:3 + npart], rest[3 + npart:]
        s_scr, dp_scr, p_scr, ds_scr, st_scr, send_sems, recv_sems, local_sems = rest
        exchange = _exchange_parts(part_refs, land_refs, send_sems, recv_sems, local_sems)
        _exchange_start(pl.program_id(0) == 0, exchange)
        dk_ref[...] = jnp.zeros_like(dk_ref)
        dv_ref[...] = jnp.zeros_like(dv_ref)
        lane = lax.broadcasted_iota(jnp.int32, (t, LANES), 1)

        def qtile(i, carry):
            ioff = pl.multiple_of(i * t, t)
            do_i = do_ref[pl.ds(ioff, t), :]
            o_i = o_ref[pl.ds(ioff, t), :]
            l_i = l_ref[pl.ds(ioff, t), :]
            for a in range(2):
                sl = slice(HEAD_PAD * a, HEAD_PAD * (a + 1))
                sel = (lane < V_HEAD_DIM) if a == 0 else (lane >= V_HEAD_DIM)
                doa = jnp.where(sel, do_i, 0.0)
                oa = o_i
                if a == 1:
                    doa = pltpu.roll(doa, V_HEAD_DIM, 1)
                    oa = pltpu.roll(o_i, V_HEAD_DIM, 1)
                st_scr[0] = jnp.broadcast_to(jnp.sum(doa * oa, axis=1, keepdims=True), (t, LANES))
                st_scr[1] = jnp.broadcast_to(l_i[:, V_HEAD_DIM * a:V_HEAD_DIM * a + 1], (t, LANES))
                doa_bf = doa.astype(BF16)
                qa = q_ref[pl.ds(ioff, t), sl]

                def block(j, masked, dq_acc, sl=sl, qa=qa, doa_bf=doa_bf):
                    joff = pl.multiple_of(j * t, t)
                    s_scr[...] = _dot(qa, kt_ref[sl, pl.ds(joff, t)], _NN)
                    dp_scr[...] = _dot(doa_bf, vxt_ref[sl, pl.ds(joff, t)], _NN)
                    for r in range(t // rs):
                        rows = slice(rs * r, rs * (r + 1))
                        p = jnp.exp(s_scr[rows, :] - st_scr[1, rows, :1])
                        if masked:
                            rowi = lax.broadcasted_iota(jnp.int32, (rs, t), 0) + rs * r
                            coli = lax.broadcasted_iota(jnp.int32, (rs, t), 1)
                            p = jnp.where(coli <= rowi, p, 0.0)
                        p_scr[rows, :] = p.astype(BF16)
                        ds_scr[rows, :] = (p * (dp_scr[rows, :] - st_scr[0, rows, :1])).astype(BF16)
                    dk_ref[pl.ds(joff, t), sl] += _dot(ds_scr[...], qa, _TN)
                    dv_ref[pl.ds(joff, t), sl] += _dot(p_scr[...], doa_bf, _TN)
                    return dq_acc + _dot(ds_scr[...], k_ref[pl.ds(joff, t), sl], _NN)

                dq_acc = lax.fori_loop(0, i, lambda j, acc: block(j, False, acc), jnp.zeros((t, HEAD_PAD), F32))
                dq_ref[pl.ds(ioff, t), sl] = block(i, True, dq_acc)
            return carry

        lax.fori_loop(0, nq, qtile, 0)
        _exchange_finish(pl.program_id(0) == nsteps - 1, exchange)

    hw = MLA_HEADS * HEAD_PAD
    wide = pl.BlockSpec((SEQ, 2 * HEAD_PAD), lambda p: (0, p))
    wide_t = pl.BlockSpec((2 * HEAD_PAD, SEQ), lambda p: (p, 0))
    narrow = pl.BlockSpec((SEQ, LANES), lambda p: (0, p))
    hbm = pl.BlockSpec(memory_space=pl.ANY)
    res = pl.pallas_call(
        body, name="attn_bwd", grid=(nsteps,),
        in_specs=[wide, wide_t, wide, wide_t, narrow, narrow, narrow] + [hbm] * npart,
        out_specs=[wide, wide, wide] + [hbm] * npart,
        out_shape=[jax.ShapeDtypeStruct((SEQ, hw), F32)] * 3 + [jax.ShapeDtypeStruct(p.shape, p.dtype) for p in parts],
        scratch_shapes=[pltpu.VMEM((t, t), F32), pltpu.VMEM((t, t), F32), pltpu.VMEM((t, t), BF16),
                        pltpu.VMEM((t, t), BF16), pltpu.VMEM((2, t, LANES), F32)] + _exchange_sems(npart),
        compiler_params=pltpu.CompilerParams(dimension_semantics=("arbitrary",), vmem_limit_bytes=VMEM_BIG),
    )(q, kt, k, vxt, d_o, o, lse, *parts)
    return res[0], res[1], res[2], res[3:]


def _sgu_math(u, v, zb, lg, lb, ws_ref, bias):
    ug, dug = _gelu_and_grad(u)
    vg, dvg = _gelu_and_grad(v)
    mu = jnp.mean(vg, axis=1, keepdims=True)
    xc = vg - mu
    rstd = lax.rsqrt(jnp.mean(xc * xc, axis=1, keepdims=True) + LN_EPS)
    xh = xc * rstd
    vn_bf = (xh * lg + lb).astype(BF16)
    grp = lax.broadcasted_iota(jnp.int32, (CHUNK, SGU_WIDTH), 1) // SGU_GROUP_DIM
    r_i = lax.broadcasted_iota(jnp.int32, (CHUNK, CHUNK), 0)
    c_i = lax.broadcasted_iota(jnp.int32, (CHUNK, CHUNK), 1)
    tri, tri_t = r_i >= c_i, r_i <= c_i
    mixed = bias
    for g in range(SGU_GROUPS):
        wt = jnp.where(tri, ws_ref[g], 0.0).astype(BF16)
        mixed = mixed + jnp.where(grp == g, _dot(wt, vn_bf, _NN), 0.0)
    sb = _sigmoid(zb)
    return ug, dug, dvg, rstd, xh, vn_bf, grp, tri, tri_t, mixed, sb


def _sgu_fwd(h_b, lg, lb, w_s, bias_full):
    def body(u_ref, v_ref, zb_ref, lg_ref, lb_ref, ws_ref, bias_ref, yb_ref):
        zb = zb_ref[...]
        ug, _, _, _, _, _, _, _, _, mixed, sb = _sgu_math(u_ref[...], v_ref[...], zb, lg_ref[...], lb_ref[...],
                                                       ws_ref, bias_ref[...])
        yb_ref[...] = (ug * mixed) * (zb * sb)

    blk = lambda c: pl.BlockSpec((CHUNK, SGU_WIDTH), lambda i, c=c: (i, c))
    full2 = lambda shape: pl.BlockSpec(shape, lambda i: (0, 0))
    return pl.pallas_call(
        body, name="sgu_fwd", grid=(SEQ // CHUNK,),
        in_specs=[blk(0), blk(1), blk(2), full2((1, SGU_WIDTH)), full2((1, SGU_WIDTH)),
                  pl.BlockSpec((SGU_GROUPS, CHUNK, CHUNK), lambda i: (0, 0, 0)), full2((CHUNK, SGU_WIDTH))],
        out_specs=pl.BlockSpec((CHUNK, SGU_WIDTH), lambda i: (i, 0)),
        out_shape=jax.ShapeDtypeStruct((SEQ, SGU_WIDTH), F32),
        compiler_params=pltpu.CompilerParams(dimension_semantics=("arbitrary",)),
    )(h_b, h_b, h_b, lg, lb, w_s, bias_full)


def _sgu_bwd(h_b, d_yb, lg, lb, w_s, w_st, bias_full, parts):
    nsteps = SEQ // CHUNK
    npart = len(parts)

    def body(u_ref, v_ref, zb_ref, dyb_ref, lg_ref, lb_ref, ws_ref, wst_ref, bias_ref, *rest):
        part_refs, rest = rest[:npart], rest[npart:]
        dhb_ref, dws_ref, dbs_ref, dlg_ref, dlb_ref, dbb_ref = rest[:6]
        land_refs, (dbias_acc, send_sems, recv_sems, local_sems) = rest[6:6 + npart], rest[6 + npart:]
        step = pl.program_id(0)
        exchange = _exchange_parts(part_refs, land_refs, send_sems, recv_sems, local_sems)
        _exchange_start(step == 0, exchange)

        @pl.when(step == 0)
        def _():
            dbb_ref[...] = jnp.zeros_like(dbb_ref)
            dws_ref[...] = jnp.zeros_like(dws_ref)
            dlg_ref[...] = jnp.zeros_like(dlg_ref)
            dlb_ref[...] = jnp.zeros_like(dlb_ref)
            dbias_acc[...] = jnp.zeros_like(dbias_acc)

        zb = zb_ref[...]
        lg = lg_ref[...]
        ug, dug, dvg, rstd, xh, vn_bf, grp, tri, tri_t, mixed, sb = _sgu_math(
            u_ref[...], v_ref[...], zb, lg, lb_ref[...], ws_ref, bias_ref[...])
        dyb = dyb_ref[...]
        dsgu = dyb * (zb * sb)
        dzb = dyb * (ug * mixed) * (sb * (1.0 + zb * (1.0 - sb)))
        du = dsgu * mixed * dug
        dmixed = dsgu * ug
        dbias_acc[...] += dmixed
        dvn = jnp.zeros((CHUNK, SGU_WIDTH), F32)
        for g in range(SGU_GROUPS):
            dm_g = jnp.where(grp == g, dmixed, 0.0).astype(BF16)
            wtt = jnp.where(tri_t, wst_ref[g], 0.0).astype(BF16)
            dvn = dvn + _dot(wtt, dm_g, _NN)
            dws_ref[g] += jnp.where(tri, _dot(dm_g, vn_bf, _NT), 0.0)
        dlg_ref[...] += jnp.sum(dvn * xh, axis=0, keepdims=True)
        dlb_ref[...] += jnp.sum(dvn, axis=0, keepdims=True)
        dxh = dvn * lg
        dvgel = rstd * (dxh - jnp.mean(dxh, axis=1, keepdims=True) - xh * jnp.mean(dxh * xh, axis=1, keepdims=True))
        _store_grad(dhb_ref, dbb_ref, 0, du)
        _store_grad(dhb_ref, dbb_ref, SGU_WIDTH, dvgel * dvg)
        _store_grad(dhb_ref, dbb_ref, 2 * SGU_WIDTH, dzb)

        @pl.when(step == nsteps - 1)
        def _():
            acc = dbias_acc[...]
            lane = lax.broadcasted_iota(jnp.int32, (CHUNK, LANES), 1)
            out = jnp.zeros((CHUNK, LANES), F32)
            for g in range(SGU_GROUPS):
                sg = jnp.sum(jnp.where(grp == g, acc, 0.0), axis=1, keepdims=True)
                out = jnp.where(lane == g, sg, out)
            dbs_ref[...] = out

        _exchange_finish(step == nsteps - 1, exchange)

    blk = lambda c: pl.BlockSpec((CHUNK, SGU_WIDTH), lambda i, c=c: (i, c))
    full2 = lambda shape: pl.BlockSpec(shape, lambda i: (0, 0))
    full3 = pl.BlockSpec((SGU_GROUPS, CHUNK, CHUNK), lambda i: (0, 0, 0))
    hbm = pl.BlockSpec(memory_space=pl.ANY)
    res = pl.pallas_call(
        body, name="sgu_bwd", grid=(nsteps,),
        in_specs=[blk(0), blk(1), blk(2), pl.BlockSpec((CHUNK, SGU_WIDTH), lambda i: (i, 0)),
                  full2((1, SGU_WIDTH)), full2((1, SGU_WIDTH)), full3, full3, full2((CHUNK, SGU_WIDTH))] + [hbm] * npart,
        out_specs=[pl.BlockSpec((CHUNK, SEG_B), lambda i: (i, 0)), full3, full2((CHUNK, LANES)),
                   full2((1, SGU_WIDTH)), full2((1, SGU_WIDTH)), full2((1, SEG_B))] + [hbm] * npart,
        out_shape=[jax.ShapeDtypeStruct((SEQ, SEG_B), BF16),
                   jax.ShapeDtypeStruct((SGU_GROUPS, CHUNK, CHUNK), F32),
                   jax.ShapeDtypeStruct((CHUNK, LANES), F32),
                   jax.ShapeDtypeStruct((1, SGU_WIDTH), F32), jax.ShapeDtypeStruct((1, SGU_WIDTH), F32),
                   jax.ShapeDtypeStruct((1, SEG_B), F32)] + [jax.ShapeDtypeStruct(p.shape, p.dtype) for p in parts],
        scratch_shapes=[pltpu.VMEM((CHUNK, SGU_WIDTH), F32)] + _exchange_sems(npart),
        compiler_params=pltpu.CompilerParams(dimension_semantics=("arbitrary",)),
    )(h_b, h_b, h_b, d_yb, lg, lb, w_s, w_st, bias_full, *parts)
    return res[:6], res[6:]


def _merge(x, o, h_a, y_b, target, w_oa, w_ob, w_out, ln_g, ln_b):
    tm = 256
    nsteps = SEQ // tm

    def body(x_ref, o_ref, ga_ref, gb_ref, za_ref, yb_ref, tgt_ref, woa_ref, wob_ref, wout_ref, lng_ref, lnb_ref,
             loss_ref, dxr_ref, dha_ref, do_ref, dyb_ref, poa_ref, pob_ref, pout_ref, dlng_ref, dlnb_ref, dba_ref,
             dwoa_ref, dwob_ref, dwout_ref):
        step = pl.program_id(0)

        @pl.when(step == 0)
        def _():
            for r in (loss_ref, dwoa_ref, dwob_ref, dwout_ref, dlng_ref, dlnb_ref, dba_ref):
                r[...] = jnp.zeros_like(r)

        o = o_ref[...]
        za = za_ref[...]
        sa = _sigmoid(za)
        ya_bf = (o * (za * sa)).astype(BF16)
        yb_bf = yb_ref[...].astype(BF16)
        woa, wob, wout = woa_ref[...], wob_ref[...], wout_ref[...]
        pa = _dot(ya_bf, woa, _NN)
        pb = _dot(yb_bf, wob, _NN)
        sga = _sigmoid(ga_ref[...])
        sgb = _sigmoid(gb_ref[...])
        merged_bf = (sga * pa + sgb * pb).astype(BF16)
        r = DN_ALPHA * x_ref[...] + _dot(merged_bf, wout, _NN)
        mu = jnp.mean(r, axis=1, keepdims=True)
        rc = r - mu
        rstd = lax.rsqrt(jnp.mean(rc * rc, axis=1, keepdims=True) + LN_EPS)
        xh = rc * rstd
        lng = lng_ref[...]
        y = xh * lng + lnb_ref[...]
        e = y - tgt_ref[...]
        loss_ref[...] += 0.5 * jnp.sum(jnp.sum(e * e, axis=1, keepdims=True) * (1.0 / D_MODEL), axis=0, keepdims=True)

        dy = e * (1.0 / D_MODEL)
        dlng_ref[...] += jnp.sum(dy * xh, axis=0, keepdims=True)
        dlnb_ref[...] += jnp.sum(dy, axis=0, keepdims=True)
        dxh = dy * lng
        dr = rstd * (dxh - jnp.mean(dxh, axis=1, keepdims=True) - xh * jnp.mean(dxh * xh, axis=1, keepdims=True))
        dxr_ref[...] = DN_ALPHA * dr
        dr_bf = dr.astype(BF16)
        dwout_ref[...] += _dot(merged_bf, dr_bf, _TN)
        dmerged = _dot(dr_bf, wout, _NT)
        dpa_bf = (dmerged * sga).astype(BF16)
        dpb_bf = (dmerged * sgb).astype(BF16)
        _store_grad(dha_ref, dba_ref, 0, dmerged * pa * (sga * (1.0 - sga)))
        _store_grad(dha_ref, dba_ref, D_MODEL, dmerged * pb * (sgb * (1.0 - sgb)))
        dwoa_ref[...] += _dot(ya_bf, dpa_bf, _TN)
        dwob_ref[...] += _dot(yb_bf, dpb_bf, _TN)
        dya = _dot(dpa_bf, woa, _NT)
        dyb_ref[...] = _dot(dpb_bf, wob, _NT)
        do_ref[...] = dya * (za * sa)
        _store_grad(dha_ref, dba_ref, 2 * D_MODEL, dya * o * (sa * (1.0 + za * (1.0 - sa))))

        @pl.when(step == nsteps - 1)
        def _():
            cols = D_MODEL // N_DEV
            for j in range(N_DEV):
                poa_ref[j] = dwoa_ref[:, cols * j:cols * (j + 1)].astype(BF16)
                pob_ref[j] = dwob_ref[:, cols * j:cols * (j + 1)].astype(BF16)
                pout_ref[j] = dwout_ref[cols * j:cols * (j + 1), :].astype(BF16)

    row = lambda w, c=0: pl.BlockSpec((tm, w), lambda i, c=c: (i, c))
    full = lambda shape: pl.BlockSpec(shape, lambda i: (0, 0))
    full3 = lambda shape: pl.BlockSpec(shape, lambda i: (0, 0, 0))
    return pl.pallas_call(
        body, name="merge", grid=(nsteps,),
        in_specs=[row(D_MODEL), row(MLA_WIDTH), row(D_MODEL, 0), row(D_MODEL, 1), row(MLA_WIDTH, 4), row(SGU_WIDTH),
                  row(D_MODEL), full((MLA_WIDTH, D_MODEL)), full((SGU_WIDTH, D_MODEL)), full((D_MODEL, D_MODEL)),
                  full((1, D_MODEL)), full((1, D_MODEL))],
        out_specs=[full((1, LANES)), row(D_MODEL), row(SEG_A), row(MLA_WIDTH), row(SGU_WIDTH),
                   full3((N_DEV, MLA_WIDTH, D_MODEL // N_DEV)), full3((N_DEV, SGU_WIDTH, D_MODEL // N_DEV)),
                   full3((N_DEV, D_MODEL // N_DEV, D_MODEL)), full((1, D_MODEL)), full((1, D_MODEL)), full((1, SEG_A))],
        out_shape=[jax.ShapeDtypeStruct((1, LANES), F32),
                   jax.ShapeDtypeStruct((SEQ, D_MODEL), F32), jax.ShapeDtypeStruct((SEQ, SEG_A), BF16),
                   jax.ShapeDtypeStruct((SEQ, MLA_WIDTH), F32), jax.ShapeDtypeStruct((SEQ, SGU_WIDTH), F32),
                   jax.ShapeDtypeStruct((N_DEV, MLA_WIDTH, D_MODEL // N_DEV), BF16),
                   jax.ShapeDtypeStruct((N_DEV, SGU_WIDTH, D_MODEL // N_DEV), BF16),
                   jax.ShapeDtypeStruct((N_DEV, D_MODEL // N_DEV, D_MODEL), BF16),
                   jax.ShapeDtypeStruct((1, D_MODEL), F32), jax.ShapeDtypeStruct((1, D_MODEL), F32),
                   jax.ShapeDtypeStruct((1, SEG_A), F32)],
        scratch_shapes=[pltpu.VMEM((MLA_WIDTH, D_MODEL), F32), pltpu.VMEM((SGU_WIDTH, D_MODEL), F32),
                        pltpu.VMEM((D_MODEL, D_MODEL), F32)],
        compiler_params=pltpu.CompilerParams(dimension_semantics=("arbitrary",), vmem_limit_bytes=VMEM_BIG),
    )(x, o, h_a, h_a, h_a, y_b, target, w_oa, w_ob, w_out, ln_g, ln_b)


def _mla_bwd(dq, dk, dv, h_c, xt_bf, gq, gkv, wq, wkn, wv, c_t, sa_t, sb_t, parts):
    tm = 256
    hw = MLA_HEADS * HEAD_PAD
    npart = len(parts)
    nsteps = SEQ // tm

    def body(dq_ref, dk_ref, dv_ref, cq_ref, ckv_ref, xt_ref, gq_ref, gkv_ref, wq_ref, wkn_ref, wv_ref, c_ref, sa_ref,
             sb_ref, *rest):
        part_refs, rest = rest[:npart], rest[npart:]
        dhc_ref, puq_ref, dwkn_ref, dwv_ref, dgq_ref, dgkv_ref, dbc_ref, dwc_ref = rest[:8]
        land_refs, (pre_ref, dwq_ref, dwc_acc, send_sems, recv_sems, local_sems) = rest[8:8 + npart], rest[8 + npart:]
        exchange = _exchange_parts(part_refs, land_refs, send_sems, recv_sems, local_sems)
        _exchange_start(pl.program_id(0) == 0, exchange)

        @pl.when(pl.program_id(0) == 0)
        def _():
            for r in (dwq_ref, dwc_acc, dwkn_ref, dwv_ref, dgq_ref, dgkv_ref, dbc_ref):
                r[...] = jnp.zeros_like(r)

        c, sa, sb = c_ref[...], sa_ref[...], sb_ref[...]
        lane = lax.broadcasted_iota(jnp.int32, (tm, LANES), 1)
        rope_lanes = jnp.logical_and(lane >= ROPE_LO, lane < ROPE_HI)

        cq = cq_ref[...]
        gq = gq_ref[...]
        rq = lax.rsqrt(jnp.sum(cq * cq, axis=1, keepdims=True) * (1.0 / Q_LORA_RANK) + RMS_EPS)
        nq = cq * rq
        cqn_bf = (nq * gq).astype(BF16)
        for h in range(MLA_HEADS):
            sl = slice(HEAD_PAD * h, HEAD_PAD * (h + 1))
            pre_ref[:, sl] = _rope_t(dq_ref[:, sl] * ATTN_SCALE, c, sa, sb).astype(BF16)
        dqpre_bf = pre_ref[...]
        dcqn = _dot(dqpre_bf, wq_ref[...], _NT)
        dwq_ref[...] += _dot(cqn_bf, dqpre_bf, _TN)
        dgq_ref[...] += jnp.sum(dcqn * nq, axis=0, keepdims=True)
        dnq = dcqn * gq
        _store_grad(dhc_ref, dbc_ref, 0,
                    rq * (dnq - nq * (jnp.sum(dnq * nq, axis=1, keepdims=True) * (1.0 / Q_LORA_RANK))))

        ckv = ckv_ref[...]
        gkv = gkv_ref[...]
        rkv = lax.rsqrt(jnp.sum(ckv * ckv, axis=1, keepdims=True) * (1.0 / KV_LORA_RANK) + RMS_EPS)
        nkv = ckv * rkv
        ckvn_bf = (nkv * gkv).astype(BF16)
        dk = dk_ref[...]
        dk_bf = dk.astype(BF16)
        dv_bf = dv_ref[...].astype(BF16)
        dckvn = _dot(dk_bf, wkn_ref[...], _NT) + _dot(dv_bf, wv_ref[...], _NT)
        dwkn_ref[...] += _dot(ckvn_bf, dk_bf, _TN)
        dwv_ref[...] += _dot(ckvn_bf, dv_bf, _TN)
        dgkv_ref[...] += jnp.sum(dckvn * nkv, axis=0, keepdims=True)
        dnkv = dckvn * gkv
        _store_grad(dhc_ref, dbc_ref, CQ_PAD, rkv * (
            dnkv - nkv * (jnp.sum(dnkv * nkv, axis=1, keepdims=True) * (1.0 / KV_LORA_RANK))))
        dkpe = jnp.zeros((tm, LANES), F32)
        for h in range(MLA_HEADS):
            dkpe = dkpe + dk[:, HEAD_PAD * h:HEAD_PAD * (h + 1)]
        _store_grad(dhc_ref, dbc_ref, CQ_PAD + LANES, _rope_t(jnp.where(rope_lanes, dkpe, 0.0), c, sa, sb))
        dwc_acc[...] += _dot(xt_ref[...], dhc_ref[...], _NN)

        @pl.when(pl.program_id(0) == SEQ // tm - 1)
        def _():
            dwc_ref[...] = dwc_acc[...].astype(BF16)
            rows = Q_LORA_RANK // N_DEV
            for j in range(N_DEV):
                for h in range(MLA_HEADS):
                    puq_ref[j, :, QK_HEAD_DIM * h:QK_HEAD_DIM * (h + 1)] = dwq_ref[
                        rows * j:rows * (j + 1), HEAD_PAD * h:HEAD_PAD * h + QK_HEAD_DIM].astype(BF16)

        _exchange_finish(pl.program_id(0) == nsteps - 1, exchange)

    full = lambda shape: pl.BlockSpec(shape, lambda i: (0, 0))
    row = lambda w, c=0: pl.BlockSpec((tm, w), lambda i, c=c: (i, c))
    hbm = pl.BlockSpec(memory_space=pl.ANY)
    res = pl.pallas_call(
        body, name="mla_bwd", grid=(nsteps,),
        in_specs=[row(hw), row(hw), row(hw), row(CQ_PAD, 0), row(LANES, CQ_PAD // LANES),
                  pl.BlockSpec((D_MODEL, tm), lambda i: (0, i)),
                  full((1, CQ_PAD)), full((1, KV_LORA_RANK)), full((CQ_PAD, hw)), full((KV_LORA_RANK, hw)),
                  full((KV_LORA_RANK, hw)), row(LANES), row(LANES), row(LANES)] + [hbm] * npart,
        out_specs=[row(SEG_C), pl.BlockSpec((N_DEV, Q_LORA_RANK // N_DEV, MLA_HEADS * QK_HEAD_DIM), lambda i: (0, 0, 0)),
                   full((KV_LORA_RANK, hw)), full((KV_LORA_RANK, hw)),
                   full((1, CQ_PAD)), full((1, KV_LORA_RANK)), full((1, SEG_C)), full((D_MODEL, SEG_C))] + [hbm] * npart,
        out_shape=[jax.ShapeDtypeStruct((SEQ, SEG_C), BF16),
                   jax.ShapeDtypeStruct((N_DEV, Q_LORA_RANK // N_DEV, MLA_HEADS * QK_HEAD_DIM), BF16),
                   jax.ShapeDtypeStruct((KV_LORA_RANK, hw), F32), jax.ShapeDtypeStruct((KV_LORA_RANK, hw), F32),
                   jax.ShapeDtypeStruct((1, CQ_PAD), F32), jax.ShapeDtypeStruct((1, KV_LORA_RANK), F32),
                   jax.ShapeDtypeStruct((1, SEG_C), F32), jax.ShapeDtypeStruct((D_MODEL, SEG_C), BF16)]
        + [jax.ShapeDtypeStruct(p.shape, p.dtype) for p in parts],
        scratch_shapes=[pltpu.VMEM((tm, hw), BF16), pltpu.VMEM((CQ_PAD, hw), F32), pltpu.VMEM((D_MODEL, SEG_C), F32)]
        + _exchange_sems(npart),
        compiler_params=pltpu.CompilerParams(dimension_semantics=("arbitrary",), vmem_limit_bytes=VMEM_MID),
    )(dq, dk, dv, h_c, h_c, xt_bf, gq, gkv, wq, wkn, wv, c_t, sa_t, sb_t, *parts)
    return res[:8], res[8:]


def _adamw_all(ws, gs, ms, vs):
    n = len(ws)
    c1 = 1.0 / (1.0 - ADAM_B1 ** ADAM_STEP)
    c2 = 1.0 / (1.0 - ADAM_B2 ** ADAM_STEP)

    def body(*refs):
        for idx in range(n):
            w, g, m, v = (refs[idx][...], refs[n + idx][...], refs[2 * n + idx][...], refs[3 * n + idx][...])
            m_new = ADAM_B1 * m + (1.0 - ADAM_B1) * g
            v_new = ADAM_B2 * v + (1.0 - ADAM_B2) * (g * g)
            delta = -ADAM_LR * ((m_new * c1) / (jnp.sqrt(v_new * c2) + ADAM_EPS) + ADAM_WD * w)
            refs[4 * n + idx][...] = delta
            refs[5 * n + idx][...] = m_new
            refs[6 * n + idx][...] = v_new

    shapes = [jax.ShapeDtypeStruct(w.shape, F32) for w in ws]
    outs = pl.pallas_call(
        body, name="adamw", out_shape=shapes * 3,
        compiler_params=pltpu.CompilerParams(vmem_limit_bytes=VMEM_BIG),
    )(*ws, *gs, *ms, *vs)
    return outs[:n], outs[n:2 * n], outs[2 * n:]


SHARD_W = IN_WIDTH // N_DEV

_PIECES = [(0, 384, 2, 0), (384, 512, 2, CQ_PAD), (512, 544, 2, CQ_PAD + LANES + ROPE_LO),
           (544, 1056, 0, 2 * D_MODEL), (1056, 1568, 1, 0), (1568, 2080, 1, SGU_WIDTH),
           (2080, 2592, 1, 2 * SGU_WIDTH), (2592, 3616, 0, 0), (3616, 4640, 0, D_MODEL)]


def _column_runs():
    runs = []
    for n0, n1, seg, d0 in _PIECES:
        for j in range(N_DEV):
            lo, hi = max(n0, j * SHARD_W), min(n1, (j + 1) * SHARD_W)
            if lo < hi:
                runs.append((j, lo - j * SHARD_W, hi - j * SHARD_W, seg, d0 + lo - n0))
    return runs


def _mesh_pos():
    return lax.axis_index("x"), lax.axis_index("y"), lax.axis_index("c")


def _remote(src, dst, send_sems, recv_sems, k, to):
    return pltpu.make_async_remote_copy(src_ref=src, dst_ref=dst, send_sem=send_sems.at[k], recv_sem=recv_sems.at[k],
                                        device_id=to, device_id_type=pl.DeviceIdType.MESH)


def _gather_exchange(gats, send_sems, recv_sems, meanwhile=None):
    x, y, c = _mesh_pos()
    me, sibling = (x, y, c), (x, y, 1 - c)
    chips = [(1 - x, y), (x, 1 - y), (1 - x, 1 - y)]

    def copy(a, k, blk, to):
        slab = gats[a].at[4 * blk[0] + 2 * blk[1] + blk[2]]
        return _remote(slab, slab, send_sems, recv_sems, 7 * a + k, to)

    arrays = range(len(gats))
    first = [copy(a, 1 + j, me, (*chip, c)) for j, chip in enumerate(chips) for a in arrays]
    first += [copy(a, 0, me, sibling) for a in arrays]
    for cp in first:
        cp.start()
    if meanwhile is not None:
        meanwhile()
    passed = []
    for j, chip in enumerate(chips):
        for a in arrays:
            copy(a, 1 + j, (*chip, c), me).wait_recv()
            fwd = copy(a, 4 + j, (*chip, c), sibling)
            fwd.start()
            passed.append(fwd)
    for a in arrays:
        copy(a, 0, sibling, me).wait_recv()
    for j, chip in enumerate(chips):
        for a in arrays:
            copy(a, 4 + j, (*chip, 1 - c), me).wait_recv()
    for cp in first + passed:
        cp.wait_send()


def _gather_behind(own, gats, send_sems, recv_sems, local_sems, step, mid, last):
    x, y, c = _mesh_pos()
    me, sibling = (x, y, c), (x, y, 1 - c)
    chips = [(1 - x, y), (x, 1 - y), (1 - x, 1 - y)]
    arrays = range(len(gats))

    def copy(a, k, blk, to, src=None):
        slab = gats[a].at[4 * blk[0] + 2 * blk[1] + blk[2]]
        return _remote(slab if src is None else src, slab, send_sems, recv_sems, 7 * a + k, to)

    first = [copy(a, 1 + j, me, (*chip, c), src=own[a]) for j, chip in enumerate(chips) for a in arrays]
    first += [copy(a, 0, me, sibling, src=own[a]) for a in arrays]
    local = [pltpu.make_async_copy(own[a], gats[a].at[4 * x + 2 * y + c], local_sems.at[a]) for a in arrays]
    passed = [copy(a, 4 + j, (*chip, c), sibling) for j, chip in enumerate(chips) for a in arrays]

    @pl.when(step == 0)
    def _():
        for cp in first + local:
            cp.start()

    @pl.when(step == mid)
    def _():
        for j, chip in enumerate(chips):
            for a in arrays:
                copy(a, 1 + j, (*chip, c), me).wait_recv()
        for cp in passed:
            cp.start()

    @pl.when(step == last)
    def _():
        for a in arrays:
            copy(a, 0, sibling, me).wait_recv()
        for j, chip in enumerate(chips):
            for a in arrays:
                copy(a, 4 + j, (*chip, 1 - c), me).wait_recv()
        for cp in first + passed:
            cp.wait_send()
        for cp in local:
            cp.wait()


def _gather_first(w_in, w_uq2, w_oa, w_ob, w_out, x2, pos_col, invf_lane):
    hw = MLA_HEADS * HEAD_PAD
    uq_rows = Q_LORA_RANK // N_DEV
    rows = 256

    def body(win_ref, wuq_ref, woa_ref, wob_ref, wout_ref, x_ref, pos_ref, invf_ref,
             wc_ref, wq_ref, winb_ref, oab_ref, obb_ref, outb_ref, xb_ref, xt_ref, c_ref, sa_ref, sb_ref,
             g_uq, blk0, send_sems, recv_sems):
        def local_work():
            for i in range(SEQ // rows):
                xi = x_ref[rows * i:rows * (i + 1), :]
                xb_ref[rows * i:rows * (i + 1), :] = xi.astype(BF16)
                xt_ref[:, rows * i:rows * (i + 1)] = xi.T.astype(BF16)
            ang = pos_ref[...].astype(F32) * invf_ref[...]
            cs, sn = jnp.cos(ang), jnp.sin(ang)
            lane = lax.broadcasted_iota(jnp.int32, ang.shape, 1)
            c_ref[...] = jnp.where(lane < ROPE_LO, 1.0, jnp.where(lane < ROPE_HI, cs, 0.0))
            sa_ref[...] = jnp.where(jnp.logical_and(lane >= ROPE_LO, lane < ROPE_MID), -sn, 0.0)
            sb_ref[...] = jnp.where(jnp.logical_and(lane >= ROPE_MID, lane < ROPE_HI), sn, 0.0)

        x, y, c = _mesh_pos()
        me = (x, y, c)
        winb_ref[...] = win_ref[0].astype(BF16)
        oab_ref[...] = woa_ref[0].astype(BF16)
        obb_ref[...] = wob_ref[0].astype(BF16)
        outb_ref[...] = wout_ref[0].astype(BF16)
        g_uq[4 * x + 2 * y + c] = wuq_ref[...].astype(BF16)

        chip0 = jnp.logical_and(x == 0, y == 0)
        south = c == 0
        half = D_MODEL // 2
        halves = [blk0.at[pl.ds(0, half)], blk0.at[pl.ds(half, half)]]

        def bcopy(k, to, part=None):
            ref = blk0 if part is None else halves[part]
            return _remote(ref, ref, send_sems, recv_sems, 7 + k, to)

        sends0 = [(0, (0, 0, 1), None), (1, (1, 0, 0), 0), (2, (0, 1, 0), 1), (3, (1, 0, 0), 1), (4, (0, 1, 0), 0)]

        @pl.when(jnp.logical_and(chip0, south))
        def _():
            blk0[...] = winb_ref[...]
            for k, to, part in sends0:
                bcopy(k, to, part).start()

        _gather_exchange([g_uq], send_sems, recv_sems, meanwhile=local_work)

        for (cx, cy), first_k, first_half, second_k in (((1, 0), 1, 0, 3), ((0, 1), 2, 1, 4)):
            @pl.when(jnp.logical_and(jnp.logical_and(x == cx, y == cy), south))
            def _(cx=cx, cy=cy, first_k=first_k, first_half=first_half, second_k=second_k):
                bcopy(first_k, me, first_half).wait_recv()
                onward = bcopy(5 + first_half, (1, 1, 0), first_half)
                onward.start()
                bcopy(second_k, me, 1 - first_half).wait_recv()
                north = bcopy(7, (cx, cy, 1))
                north.start()
                onward.wait_send()
                north.wait_send()

        @pl.when(jnp.logical_and(jnp.logical_and(x == 1, y == 1), south))
        def _():
            bcopy(5, me, 0).wait_recv()
            bcopy(6, me, 1).wait_recv()
            north = bcopy(7, (1, 1, 1))
            north.start()
            north.wait_send()

        @pl.when(jnp.logical_and(chip0, c == 1))
        def _():
            bcopy(0, me).wait_recv()

        @pl.when(jnp.logical_and(jnp.logical_not(chip0), c == 1))
        def _():
            bcopy(7, me).wait_recv()

        @pl.when(jnp.logical_and(chip0, south))
        def _():
            for k, to, part in sends0:
                bcopy(k, to, part).wait_send()

        for j, s0, s1, seg, d0 in _column_runs():
            if seg == 2:
                wc_ref[:, d0:d0 + (s1 - s0)] = blk0[:, s0:s1]
        zeros = lambda r, w: jnp.zeros((r, w), BF16)
        wc_ref[:, Q_LORA_RANK:CQ_PAD] = zeros(D_MODEL, CQ_PAD - Q_LORA_RANK)
        wc_ref[:, CQ_PAD + LANES:CQ_PAD + LANES + ROPE_LO] = zeros(D_MODEL, ROPE_LO)
        wc_ref[:, CQ_PAD + LANES + ROPE_HI:SEG_C] = zeros(D_MODEL, LANES - ROPE_HI)
        wq_ref[Q_LORA_RANK:CQ_PAD, :] = zeros(CQ_PAD - Q_LORA_RANK, hw)
        for h in range(MLA_HEADS):
            wq_ref[0:Q_LORA_RANK, HEAD_PAD * h + QK_HEAD_DIM:HEAD_PAD * (h + 1)] = zeros(Q_LORA_RANK, HEAD_PAD - QK_HEAD_DIM)
        for j in range(N_DEV):
            for h in range(MLA_HEADS):
                wq_ref[uq_rows * j:uq_rows * (j + 1), HEAD_PAD * h:HEAD_PAD * h + QK_HEAD_DIM] = g_uq[
                    j, :, QK_HEAD_DIM * h:QK_HEAD_DIM * (h + 1)]

    vmem = pl.BlockSpec(memory_space=pltpu.VMEM)
    return pl.pallas_call(
        body, name="gather_first",
        out_shape=[jax.ShapeDtypeStruct((D_MODEL, SEG_C), BF16), jax.ShapeDtypeStruct((CQ_PAD, hw), BF16),
                   jax.ShapeDtypeStruct(w_in.shape[1:], BF16), jax.ShapeDtypeStruct(w_oa.shape[1:], BF16),
                   jax.ShapeDtypeStruct(w_ob.shape[1:], BF16), jax.ShapeDtypeStruct(w_out.shape[1:], BF16),
                   jax.ShapeDtypeStruct((SEQ, D_MODEL), BF16), jax.ShapeDtypeStruct((D_MODEL, SEQ), BF16)]
        + [jax.ShapeDtypeStruct((SEQ, LANES), F32)] * 3,
        in_specs=[vmem] * 8, out_specs=[vmem] * 11,
        scratch_shapes=[pltpu.VMEM((N_DEV, uq_rows, MLA_HEADS * QK_HEAD_DIM), BF16), pltpu.VMEM((D_MODEL, SHARD_W), BF16),
                        pltpu.SemaphoreType.DMA((15,)), pltpu.SemaphoreType.DMA((15,))],
        compiler_params=pltpu.CompilerParams(vmem_limit_bytes=VMEM_BIG),
    )(w_in, w_uq2, w_oa, w_ob, w_out, x2, pos_col, invf_lane)


def _assemble_in(g_in):
    def body(g_ref, wa_ref, wb_ref):
        segs = [wa_ref, wb_ref]
        for j, s0, s1, seg, d0 in _column_runs():
            if seg < 2:
                segs[seg][:, d0:d0 + (s1 - s0)] = g_ref[j, :, s0:s1]

    return pl.pallas_call(
        body, name="assemble_in",
        out_shape=[jax.ShapeDtypeStruct((D_MODEL, SEG_A), BF16), jax.ShapeDtypeStruct((D_MODEL, SEG_B), BF16)],
        compiler_params=pltpu.CompilerParams(vmem_limit_bytes=VMEM_MID),
    )(g_in)


def _assemble_out(g_oa, g_ob, g_out):
    cols = D_MODEL // N_DEV

    def body(goa_ref, gob_ref, gout_ref, oa_ref, ob_ref, out_ref):
        for j in range(N_DEV):
            oa_ref[:, cols * j:cols * (j + 1)] = goa_ref[j]
            ob_ref[:, cols * j:cols * (j + 1)] = gob_ref[j]
            out_ref[cols * j:cols * (j + 1), :] = gout_ref[j]

    return pl.pallas_call(
        body, name="assemble_out",
        out_shape=[jax.ShapeDtypeStruct((MLA_WIDTH, D_MODEL), BF16), jax.ShapeDtypeStruct((SGU_WIDTH, D_MODEL), BF16),
                   jax.ShapeDtypeStruct((D_MODEL, D_MODEL), BF16)],
    )(g_oa, g_ob, g_out)


C_NAT = 544


P_IN_SPLIT = 896


def _to_parts(dwa, dwb):
    def body(dwa_ref, dwb_ref, phi_ref, plo_ref):
        phi_ref[0, :, 0:C_NAT] = jnp.zeros((P_IN_SPLIT, C_NAT), BF16)
        plo_ref[0, :, 0:C_NAT] = jnp.zeros((D_MODEL - P_IN_SPLIT, C_NAT), BF16)
        segs = [dwa_ref, dwb_ref]
        for j, s0, s1, seg, d0 in _column_runs():
            if seg < 2:
                phi_ref[j, :, s0:s1] = segs[seg][0:P_IN_SPLIT, d0:d0 + (s1 - s0)]
                plo_ref[j, :, s0:s1] = segs[seg][P_IN_SPLIT:D_MODEL, d0:d0 + (s1 - s0)]

    return pl.pallas_call(
        body, name="to_parts",
        out_shape=[jax.ShapeDtypeStruct((N_DEV, P_IN_SPLIT, SHARD_W), BF16),
                   jax.ShapeDtypeStruct((N_DEV, D_MODEL - P_IN_SPLIT, SHARD_W), BF16)],
        compiler_params=pltpu.CompilerParams(vmem_limit_bytes=VMEM_MID))(dwa, dwb)


def _dx_tail(dhs, ws, dx_res, dwc, p_uq, p_rep, parts):
    ntile, sums_at = 8, 5
    npart = len(parts)
    tm = SEQ // ntile
    rep_rows = p_rep.shape[1]
    c_rows = D_MODEL // N_DEV
    spec = [((c_rows, C_NAT), BF16), (p_uq.shape[1:], BF16), ((rep_rows, LANES), F32)]
    n = len(spec)

    nseg = len(dhs)

    def body(*refs):
        dh_refs, w_refs = refs[:nseg], refs[nseg:2 * nseg]
        dxr_ref, dwc_ref, puq_ref, prep_ref = refs[2 * nseg:2 * nseg + 4]
        part_refs, refs = refs[2 * nseg + 4:2 * nseg + 4 + npart], refs[2 * nseg + 4 + npart:]
        dx_ref, call_ref, guq_ref, repall_ref = refs[:4]
        land_refs, refs = refs[4:4 + npart], refs[4 + npart:]
        pc_ref, c_all, rep_all = refs[:3]
        rest = refs[3:]
        ras, tbs, rbs = rest[0:n], rest[n:2 * n], rest[2 * n:3 * n]
        send_sems, recv_sems, gsend, grecv, xsend, xrecv, xlocal = rest[3 * n:]
        step = pl.program_id(0)
        exchange = _exchange_parts(part_refs, land_refs, xsend, xrecv, xlocal)
        _exchange_start(step == 0, exchange)
        x, y, c = _mesh_pos()
        me_idx = 4 * x + 2 * y + c
        me, sibling = (x, y, c), (x, y, 1 - c)
        others = [(1 - x, y), (x, 1 - y), (1 - x, 1 - y)]
        parts = [pc_ref, puq_ref, prep_ref]
        gats = [rep_all, c_all]

        def stage1(chip, a):
            return _remote(parts[a].at[2 * chip + (1 - c)], ras[a].at[chip], send_sems, recv_sems, 7 * a + chip, sibling)

        def stage2(k, a):
            cx, cy = others[k]
            return _remote(tbs[a].at[k], rbs[a].at[k], send_sems, recv_sems, 7 * a + 4 + k, (cx, cy, c))

        def gcopy(a, k, blk, to):
            slab = gats[a].at[4 * blk[0] + 2 * blk[1] + blk[2]]
            return _remote(slab, slab, gsend, grecv, 7 * a + k, to)

        def chip_sum(a, chip):
            return parts[a][2 * chip + c].astype(F32) + ras[a][chip].astype(F32)

        @pl.when(step == 0)
        def _():
            for j, s0, s1, seg, d0 in _column_runs():
                if seg == 2:
                    for r in range(N_DEV):
                        pc_ref[r, :, s0:s1] = dwc_ref[c_rows * r:c_rows * (r + 1), d0:d0 + (s1 - s0)]
            for chip in range(4):
                for a in range(n):
                    stage1(chip, a).start()

        @pl.when(step == 1)
        def _():
            for chip in range(4):
                for a in range(n):
                    stage1(chip, a).wait_recv()
            for k, (cx, cy) in enumerate(others):
                for a in range(n):
                    tbs[a][k] = chip_sum(a, 2 * cx + cy).astype(spec[a][1])
                    stage2(k, a).start()

        @pl.when(step == sums_at)
        def _():
            for k in range(3):
                for a in range(n):
                    stage2(k, a).wait_recv()
            sums = []
            for a in range(n):
                acc = chip_sum(a, 2 * x + y)
                for k in range(3):
                    acc = acc + rbs[a][k].astype(F32)
                sums.append(acc)
            c_all[me_idx] = sums[0].astype(BF16)
            guq_ref[...] = sums[1]
            rep_all[me_idx] = sums[2]
            for a in range(2):
                for j, chip in enumerate(others):
                    gcopy(a, 1 + j, me, (*chip, c)).start()
                gcopy(a, 0, me, sibling).start()

        acc = dxr_ref[...]
        for dh_ref, w_ref in zip(dh_refs, w_refs):
            acc = acc + _dot(dh_ref[...], w_ref[...], _NT)
        dx_ref[...] = acc

        @pl.when(step == ntile - 1)
        def _():
            for j, chip in enumerate(others):
                for a in range(2):
                    gcopy(a, 1 + j, (*chip, c), me).wait_recv()
                    gcopy(a, 4 + j, (*chip, c), sibling).start()
            for a in range(2):
                gcopy(a, 0, sibling, me).wait_recv()
                for j, chip in enumerate(others):
                    gcopy(a, 4 + j, (*chip, 1 - c), me).wait_recv()
            for a in range(2):
                gcopy(a, 0, me, sibling).wait_send()
                for j, chip in enumerate(others):
                    gcopy(a, 1 + j, me, (*chip, c)).wait_send()
                    gcopy(a, 4 + j, (*chip, c), sibling).wait_send()
            for a in range(n):
                for chip in range(4):
                    stage1(chip, a).wait_send()
                for k in range(3):
                    stage2(k, a).wait_send()
            call_ref[...] = c_all[...]
            repall_ref[...] = rep_all[...]

        _exchange_finish(step == ntile - 1, exchange)

    row = lambda w: pl.BlockSpec((tm, w), lambda i: (i, 0))
    full = lambda shape: pl.BlockSpec(shape, lambda i: (0,) * len(shape))
    scratch = [pltpu.VMEM((N_DEV, c_rows, C_NAT), BF16), pltpu.VMEM((N_DEV, c_rows, C_NAT), BF16),
               pltpu.VMEM((N_DEV, rep_rows, LANES), F32)]
    for lead in (4, 3, 3):
        scratch += [pltpu.VMEM((lead,) + tuple(shape), dt) for shape, dt in spec]
    scratch += [pltpu.SemaphoreType.DMA((7 * n,)), pltpu.SemaphoreType.DMA((7 * n,)),
                pltpu.SemaphoreType.DMA((14,)), pltpu.SemaphoreType.DMA((14,))] + _exchange_sems(npart)
    hbm = pl.BlockSpec(memory_space=pl.ANY)
    res = pl.pallas_call(
        body, name="dx_tail", grid=(SEQ // tm,),
        in_specs=[row(dh.shape[1]) for dh in dhs] + [full(w.shape) for w in ws]
        + [row(D_MODEL), full(dwc.shape), full(p_uq.shape), full(p_rep.shape)] + [hbm] * npart,
        out_specs=[row(D_MODEL), full((N_DEV, c_rows, C_NAT)), full(p_uq.shape[1:]), full((N_DEV, rep_rows, LANES))]
        + [hbm] * npart,
        out_shape=[jax.ShapeDtypeStruct((SEQ, D_MODEL), F32), jax.ShapeDtypeStruct((N_DEV, c_rows, C_NAT), BF16),
                   jax.ShapeDtypeStruct(p_uq.shape[1:], F32), jax.ShapeDtypeStruct((N_DEV, rep_rows, LANES), F32)]
        + [jax.ShapeDtypeStruct(p.shape, p.dtype) for p in parts],
        scratch_shapes=scratch,
        compiler_params=pltpu.CompilerParams(dimension_semantics=("arbitrary",), vmem_limit_bytes=VMEM_BIG),
    )(*dhs, *ws, dx_res, dwc, p_uq, p_rep, *parts)
    return res[:4], res[4:]


def _sum_landed(landed, c_all):
    c_rows = D_MODEL // N_DEV

    def body(rhi_ref, rlo_ref, roa_ref, rob_ref, rout_ref, call_ref, gin_ref, goa_ref, gob_ref, gout_ref):
        def total(ref, sl):
            acc = ref[0, sl, :].astype(F32)
            for s in range(1, N_DEV):
                acc = acc + ref[s, sl, :].astype(F32)
            return acc

        x, y, c = _mesh_pos()
        dev0 = jnp.where(4 * x + 2 * y + c == 0, 1.0, 0.0)
        for j in range(N_DEV):
            sl = slice(c_rows * j, c_rows * (j + 1))
            below = c_rows * j < P_IN_SPLIT
            tot = total(rhi_ref, sl) if below else total(rlo_ref, slice(c_rows * j - P_IN_SPLIT, c_rows * (j + 1) - P_IN_SPLIT))
            gin_ref[0, sl, C_NAT:SHARD_W] = tot[:, C_NAT:SHARD_W]
            gin_ref[0, sl, 0:C_NAT] = tot[:, 0:C_NAT] + dev0 * call_ref[j].astype(F32)
        goa_ref[0] = total(roa_ref, slice(None))
        gob_ref[0] = total(rob_ref, slice(None))
        gout_ref[0] = total(rout_ref, slice(None))

    return pl.pallas_call(
        body, name="sum_landed",
        out_shape=[jax.ShapeDtypeStruct((1, D_MODEL, SHARD_W), F32)]
        + [jax.ShapeDtypeStruct((1,) + r.shape[1:], F32) for r in landed[2:]],
        compiler_params=pltpu.CompilerParams(vmem_limit_bytes=VMEM_MID),
    )(*landed, c_all)


_O_CQ, _O_CKV, _O_KPE, _O_ZA, _O_U, _O_V, _O_ZB, _O_GA, _O_GB = 0, 384, 512, 544, 1056, 1568, 2080, 2592, 3616


def _to_segments(w):
    z = lambda n: jnp.zeros(w.shape[:-1] + (n,), w.dtype)
    seg_a = jnp.concatenate([w[..., _O_GA:_O_GB], w[..., _O_GB:IN_WIDTH], w[..., _O_ZA:_O_U]], axis=-1)
    seg_b = jnp.concatenate([w[..., _O_U:_O_V], w[..., _O_V:_O_ZB], w[..., _O_ZB:_O_GA]], axis=-1)
    seg_c = jnp.concatenate([w[..., _O_CQ:_O_CKV], z(CQ_PAD - Q_LORA_RANK), w[..., _O_CKV:_O_KPE],
                             z(ROPE_LO), w[..., _O_KPE:_O_ZA], z(LANES - ROPE_HI)], axis=-1)
    return seg_a, seg_b, seg_c


def _from_segments(seg_a, seg_b, seg_c):
    kpe0 = CQ_PAD + LANES + ROPE_LO
    return jnp.concatenate([
        seg_c[..., 0:Q_LORA_RANK], seg_c[..., CQ_PAD:CQ_PAD + LANES], seg_c[..., kpe0:kpe0 + QK_ROPE_DIM],
        seg_a[..., 2 * D_MODEL:SEG_A], seg_b, seg_a[..., 0:2 * D_MODEL]], axis=-1)


def kernel(x, positions, w_in, b_in, g_q, w_uq, g_kv, w_ukv, w_oa, sgu_ln_g, sgu_ln_b, w_s, b_s, w_ob, w_out, ln_g, ln_b, loss_target, m_w_in, m_b_in, m_g_q, m_w_uq, m_g_kv, m_w_ukv, m_w_oa, m_sgu_ln_g, m_sgu_ln_b, m_w_s, m_b_s, m_w_ob, m_w_out, m_ln_g, m_ln_b, v_w_in, v_b_in, v_g_q, v_w_uq, v_g_kv, v_w_ukv, v_w_oa, v_sgu_ln_g, v_sgu_ln_b, v_w_s, v_b_s, v_w_ob, v_w_out, v_ln_g, v_ln_b):
    w_uq2 = w_uq[0].reshape(Q_LORA_RANK // N_DEV, MLA_HEADS * QK_HEAD_DIM)
    inv_freq = ROPE_THETA ** (-jnp.arange(0, QK_ROPE_DIM, 2, dtype=F32) / QK_ROPE_DIM)
    invf_lane = jnp.concatenate([jnp.zeros((ROPE_LO,), F32), inv_freq, inv_freq,
                                 jnp.zeros((LANES - ROPE_HI,), F32)]).reshape(1, LANES)
    first = _gather_first(w_in, w_uq2, w_oa, w_ob, w_out, x[0], positions.reshape(SEQ, 1), invf_lane)
    partials = _local_step(x[0], loss_target[0], first, b_in, g_q, g_kv, w_ukv, sgu_ln_g, sgu_ln_b, w_s, b_s, ln_g, ln_b)
    weights = dict(w_in=w_in, b_in=b_in, g_q=g_q, w_uq=w_uq, g_kv=g_kv, w_ukv=w_ukv, w_oa=w_oa, sgu_ln_g=sgu_ln_g,
                   sgu_ln_b=sgu_ln_b, w_s=w_s, b_s=b_s, w_ob=w_ob, w_out=w_out, ln_g=ln_g, ln_b=ln_b)
    moms = dict(w_in=m_w_in, b_in=m_b_in, g_q=m_g_q, w_uq=m_w_uq, g_kv=m_g_kv, w_ukv=m_w_ukv, w_oa=m_w_oa,
                sgu_ln_g=m_sgu_ln_g, sgu_ln_b=m_sgu_ln_b, w_s=m_w_s, b_s=m_b_s, w_ob=m_w_ob, w_out=m_w_out,
                ln_g=m_ln_g, ln_b=m_ln_b)
    vars_ = dict(w_in=v_w_in, b_in=v_b_in, g_q=v_g_q, w_uq=v_w_uq, g_kv=v_g_kv, w_ukv=v_w_ukv, w_oa=v_w_oa,
                 sgu_ln_g=v_sgu_ln_g, sgu_ln_b=v_sgu_ln_b, w_s=v_w_s, b_s=v_b_s, w_ob=v_w_ob, w_out=v_w_out,
                 ln_g=v_ln_g, ln_b=v_ln_b)
    return _reduce_and_update(partials, weights, moms, vars_)


def _local_step(x2, tgt, first, b_in, g_q, g_kv, w_ukv, sgu_ln_g, sgu_ln_b, w_s, b_s, ln_g, ln_b):
    wc, wq, win_b, oa_b, ob_b, out_b, x_bf, xt_bf, c_t, sa_t, sb_t = first
    ba, bb, bc = _to_segments(b_in)
    w_ukv_bf = w_ukv[0].astype(BF16)
    wkn = jnp.pad(w_ukv_bf[:, :, :QK_NOPE_DIM], ((0, 0), (0, 0), (0, HEAD_PAD - QK_NOPE_DIM))).reshape(KV_LORA_RANK, -1)
    wv = jnp.pad(w_ukv_bf[:, :, QK_NOPE_DIM:], ((0, 0), (0, 0), (0, HEAD_PAD - V_HEAD_DIM))).reshape(KV_LORA_RANK, -1)
    gq = jnp.pad(g_q, ((0, 0), (0, CQ_PAD - Q_LORA_RANK)))
    bias_full = jnp.repeat(b_s[0].T, SGU_GROUP_DIM, axis=1)
    w_s3 = w_s[0]
    w_st3 = jnp.swapaxes(w_s3, 1, 2)

    h_c = _mm(x_bf, wc, bias=bc, tm=512, tn=SEG_C, name="in_proj_c")
    q, k, kt, vx, vxt = _mla_prep(h_c, gq, g_kv, wq, wkn, wv, c_t, sa_t, sb_t)
    o, lse, (g_in,) = _attn_fwd(q, kt, vx, (win_b,))
    wa, wb = _assemble_in(g_in)
    h_a, (g_out,) = _mm(x_bf, wa, bias=ba, own=(out_b,), tm=512, tn=SEG_A // 2, name="in_proj_a")
    h_b, (g_oa, g_ob) = _mm(x_bf, wb, bias=bb, own=(oa_b, ob_b), tm=512, tn=SEG_B // 2, name="in_proj_b")
    y_b = _sgu_fwd(h_b, sgu_ln_g, sgu_ln_b, w_s3, bias_full)
    w_oa_f, w_ob_f, w_out_f = _assemble_out(g_oa, g_ob, g_out)

    (loss_row, dx_res, dh_a, d_o, d_yb, p_oa, p_ob, p_out, d_lng, d_lnb, d_ba) = _merge(
        x2, o, h_a, y_b, tgt, w_oa_f, w_ob_f, w_out_f, ln_g, ln_b)
    (dh_b, d_ws, d_bs_t, d_slg, d_slb, d_bb), (r_out,) = _sgu_bwd(h_b, d_yb, sgu_ln_g, sgu_ln_b, w_s3, w_st3, bias_full,
                                                                 (p_out,))
    d_wa, (r_oa,) = _mm(xt_bf, dh_a, out_dtype=BF16, parts=(p_oa,), tm=512, tn=512, name="dw_in_a")
    d_wb = _mm(xt_bf, dh_b, out_dtype=BF16, tm=512, tn=512, name="dw_in_b")
    p_hi, p_lo = _to_parts(d_wa, d_wb)
    dq, dk, dv, (r_hi,) = _attn_bwd(q, kt, k, vxt, d_o, o, lse, (p_hi,))
    (dh_c, p_uq, d_wkn, d_wv, d_gq, d_gkv, d_bc, d_wc), (r_lo,) = _mla_bwd(
        dq, dk, dv, h_c, xt_bf, gq, g_kv, wq, wkn, wv, c_t, sa_t, sb_t, (p_lo,))
    landed = (r_hi, r_lo, r_oa, p_ob, r_out)

    p_b_in = _from_segments(d_ba, d_bb, d_bc)
    p_w_ukv = jnp.concatenate([d_wkn.reshape(KV_LORA_RANK, MLA_HEADS, HEAD_PAD)[:, :, :QK_NOPE_DIM],
                               d_wv.reshape(KV_LORA_RANK, MLA_HEADS, HEAD_PAD)[:, :, :V_HEAD_DIM]], axis=-1)
    p_g_q = d_gq[:, :Q_LORA_RANK]
    p_b_s = d_bs_t[:, :SGU_GROUPS].T
    replicated = [p_b_in, p_g_q, d_gkv, p_w_ukv, d_slg, d_slb, d_ws, p_b_s, d_lng, d_lnb]
    return loss_row, ((dh_a, dh_b, dh_c), (wa, wb, wc), dx_res), landed, d_wc, p_uq, replicated


_NAMES = ["w_in", "b_in", "g_q", "w_uq", "g_kv", "w_ukv", "w_oa", "sgu_ln_g", "sgu_ln_b", "w_s", "b_s", "w_ob",
          "w_out", "ln_g", "ln_b"]
_REPLICATED = ["b_in", "g_q", "g_kv", "w_ukv", "sgu_ln_g", "sgu_ln_b", "w_s", "b_s", "ln_g", "ln_b"]


def _reduce_and_update(partials, weights, moms, vars_):
    loss_row, (dhs, ws, dx_res), landed, d_wc, p_uq, replicated = partials
    def piece(a):
        flat = a.reshape(-1)
        return jnp.pad(flat, (0, -flat.size % PACK_ALIGN))

    rep_flat = jnp.concatenate([piece(a) for a in replicated] + [piece(loss_row[0, :1])])
    rep_flat = jnp.pad(rep_flat, (0, N_DEV * PACK_R_ROWS * LANES - rep_flat.size))
    r_hi, r_lo, r_oa, p_ob, r_out = landed
    (dx, c_all, g_uq, rep_all), (r_ob,) = _dx_tail(dhs, ws, dx_res, d_wc, p_uq,
                                                   rep_flat.reshape(N_DEV, PACK_R_ROWS, LANES), (p_ob,))
    g_in, g_oa, g_ob, g_out = _sum_landed((r_hi, r_lo, r_oa, r_ob, r_out), c_all)
    rep_sum = rep_all.reshape(-1)
    grads, pos = dict(w_in=g_in, w_uq=g_uq, w_oa=g_oa, w_ob=g_ob, w_out=g_out), 0
    for nm in _REPLICATED:
        grads[nm] = rep_sum[pos:pos + weights[nm].size]
        pos += weights[nm].size + -weights[nm].size % PACK_ALIGN
    loss = rep_sum[pos]
    grads = {nm: grads[nm].reshape(weights[nm].shape) for nm in _NAMES}
    deltas, new_m, new_v = _adamw_all([weights[nm] for nm in _NAMES], [grads[nm] for nm in _NAMES],
                                      [moms[nm] for nm in _NAMES], [vars_[nm] for nm in _NAMES])
    return (loss, dx.reshape(1, SEQ, D_MODEL), *[grads[nm] for nm in _NAMES], *deltas, *new_m, *new_v)
```

```python
import math

import jax
import jax.numpy as jnp
from jax import lax
from jax.experimental import pallas as pl
from jax.experimental.pallas import tpu as pltpu

F32 = jnp.float32
BF16 = jnp.bfloat16

D_MODEL = 1024
SEQ = 2048
N_DEV = 8
MLA_HEADS = 8
Q_LORA_RANK = 384
KV_LORA_RANK = 128
QK_NOPE_DIM = 64
QK_ROPE_DIM = 32
V_HEAD_DIM = 64
QK_HEAD_DIM = QK_NOPE_DIM + QK_ROPE_DIM
MLA_WIDTH = MLA_HEADS * V_HEAD_DIM
ROPE_THETA = 10000.0
SGU_GROUPS = 8
SGU_GROUP_DIM = 64
SGU_WIDTH = SGU_GROUPS * SGU_GROUP_DIM
CHUNK = 128
RMS_EPS = 1e-6
LN_EPS = 1e-5
DN_ALPHA = 2.0 ** 0.25
IN_WIDTH = 4640
ATTN_SCALE = QK_HEAD_DIM ** -0.5

ADAM_LR = 0.001
ADAM_B1 = 0.9
ADAM_B2 = 0.999
ADAM_EPS = 1e-08
ADAM_WD = 0.01
ADAM_STEP = 10

LANES = 128
HEAD_PAD = 128
ROPE_LO = QK_NOPE_DIM
ROPE_MID = ROPE_LO + QK_ROPE_DIM // 2
ROPE_HI = ROPE_LO + QK_ROPE_DIM
CQ_PAD = 512

SEG_A = 2560
SEG_B = 1536
SEG_C = 768

PACK_R_ROWS = 272
PACK_ALIGN = 8 * LANES
VMEM_BIG = 56 * 1024 * 1024
VMEM_MID = 40 * 1024 * 1024


def _sigmoid(x):
    return 1.0 / (1.0 + jnp.exp(-x))


def _gelu_and_grad(x):
    c0 = math.sqrt(2.0 / math.pi)
    x2 = x * x
    t = jnp.tanh(c0 * (x + 0.044715 * x * x2))
    g = 0.5 * x * (1.0 + t)
    dg = 0.5 * (1.0 + t) + 0.5 * x * (1.0 - t * t) * (c0 * (1.0 + 3.0 * 0.044715 * x2))
    return g, dg


def _dot(a, b, dims):
    return lax.dot_general(a, b, (dims, ((), ())), preferred_element_type=F32)


_NN = ((1,), (0,))
_NT = ((1,), (1,))
_TN = ((0,), (0,))


def _store_grad(dh_ref, db_ref, col, val):
    cols = slice(col, col + val.shape[1])
    dh_ref[:, cols] = val.astype(BF16)
    db_ref[:, cols] += jnp.sum(val, axis=0, keepdims=True)


def _mm(a, b, *, tb=False, bias=None, add=None, out_dtype=F32, own=(), parts=(), tm, tn, name):
    m, k = a.shape
    n = b.shape[0] if tb else b.shape[1]
    assert m % tm == 0 and n % tn == 0 and not (own and parts)
    dims = _NT if tb else _NN
    nown = len(own) + len(parts)
    nm = m // tm
    nsteps = (n // tn) * nm

    def body(*refs):
        a_ref, b_ref = refs[0], refs[1]
        pos = 2
        r = _dot(a_ref[...], b_ref[...], dims)
        if bias is not None:
            r = r + refs[pos][...]; pos += 1
        if add is not None:
            r = r + refs[pos][...]; pos += 1
        own_refs = refs[pos:pos + nown]; pos += nown
        refs[pos][...] = r.astype(out_dtype)
        if nown:
            gat_refs = refs[pos + 1:pos + 1 + nown]
            send_sems, recv_sems, local_sems = refs[pos + 1 + nown:]
            step = pl.program_id(0) * nm + pl.program_id(1)
            if own:
                _gather_behind(own_refs, gat_refs, send_sems, recv_sems, local_sems, step, nsteps - 2, nsteps - 1)
            else:
                exchange = _exchange_parts(own_refs, gat_refs, send_sems, recv_sems, local_sems)
                _exchange_start(step == 0, exchange)
                _exchange_finish(step == nsteps - 1, exchange)

    b_spec = pl.BlockSpec((tn, k), lambda j, i: (j, 0)) if tb else pl.BlockSpec((k, tn), lambda j, i: (0, j))
    in_specs, args = [pl.BlockSpec((tm, k), lambda j, i: (i, 0)), b_spec], [a, b]
    if bias is not None:
        in_specs.append(pl.BlockSpec((1, tn), lambda j, i: (0, j))); args.append(bias)
    if add is not None:
        in_specs.append(pl.BlockSpec((tm, tn), lambda j, i: (i, j))); args.append(add)
    hbm = pl.BlockSpec(memory_space=pl.ANY)
    res = pl.pallas_call(
        body, name=name, grid=(n // tn, nm), in_specs=in_specs + [hbm] * nown,
        out_specs=[pl.BlockSpec((tm, tn), lambda j, i: (i, j))] + [hbm] * nown,
        out_shape=[jax.ShapeDtypeStruct((m, n), out_dtype)]
        + [jax.ShapeDtypeStruct((N_DEV,) + o.shape, o.dtype) for o in own]
        + [jax.ShapeDtypeStruct(p.shape, p.dtype) for p in parts],
        scratch_shapes=_exchange_sems(nown) if nown else [],
        compiler_params=pltpu.CompilerParams(dimension_semantics=("arbitrary", "arbitrary"), vmem_limit_bytes=VMEM_BIG),
    )(*args, *own, *parts)
    return (res[0], res[1:]) if nown else res[0]


def _rope(x, c, sa, sb):
    return x * c + pltpu.roll(x, LANES - 16, 1) * sa + pltpu.roll(x, 16, 1) * sb


def _rope_t(dy, c, sa, sb):
    return dy * c + pltpu.roll(dy * sa, 16, 1) + pltpu.roll(dy * sb, LANES - 16, 1)


def _mla_prep(h_c, gq, gkv, wq, wkn, wvx, c_t, sa_t, sb_t):
    tm = 256
    hw = MLA_HEADS * HEAD_PAD

    def body(cq_ref, ckv_ref, kpe_ref, gq_ref, gkv_ref, wq_ref, wkn_ref, wvx_ref, c_ref, sa_ref, sb_ref,
             q_ref, k_ref, kt_ref, vx_ref, vxt_ref):
        c, sa, sb = c_ref[...], sa_ref[...], sb_ref[...]
        cq = cq_ref[...]
        rq = lax.rsqrt(jnp.sum(cq * cq, axis=1, keepdims=True) * (1.0 / Q_LORA_RANK) + RMS_EPS)
        cqn = ((cq * rq) * gq_ref[...]).astype(BF16)
        qall = _dot(cqn, wq_ref[...], _NN)
        for h in range(MLA_HEADS):
            sl = slice(HEAD_PAD * h, HEAD_PAD * (h + 1))
            q_ref[:, sl] = (_rope(qall[:, sl], c, sa, sb) * ATTN_SCALE).astype(BF16)
        ckv = ckv_ref[...]
        rkv = lax.rsqrt(jnp.sum(ckv * ckv, axis=1, keepdims=True) * (1.0 / KV_LORA_RANK) + RMS_EPS)
        ckvn = ((ckv * rkv) * gkv_ref[...]).astype(BF16)
        knall = _dot(ckvn, wkn_ref[...], _NN)
        vall = _dot(ckvn, wvx_ref[...], _NN)
        kper = _rope(kpe_ref[...], c, sa, sb)
        ones_half = (lax.broadcasted_iota(jnp.int32, (tm, HEAD_PAD), 1) >= V_HEAD_DIM).astype(F32)
        for h in range(MLA_HEADS):
            sl = slice(HEAD_PAD * h, HEAD_PAD * (h + 1))
            kh = knall[:, sl] + kper
            vh = vall[:, sl] + ones_half
            k_ref[:, sl] = kh.astype(BF16)
            kt_ref[sl, :] = kh.T.astype(BF16)
            vx_ref[:, sl] = vh.astype(BF16)
            vxt_ref[sl, :] = vh.T.astype(BF16)

    full = lambda shape: pl.BlockSpec(shape, lambda i: (0, 0))
    tab = pl.BlockSpec((tm, LANES), lambda i: (i, 0))
    row = pl.BlockSpec((tm, hw), lambda i: (i, 0))
    col = pl.BlockSpec((hw, tm), lambda i: (0, i))
    return pl.pallas_call(
        body, name="mla_prep", grid=(SEQ // tm,),
        in_specs=[pl.BlockSpec((tm, CQ_PAD), lambda i: (i, 0)),
                  pl.BlockSpec((tm, LANES), lambda i: (i, CQ_PAD // LANES)),
                  pl.BlockSpec((tm, LANES), lambda i: (i, CQ_PAD // LANES + 1)),
                  full((1, CQ_PAD)), full((1, KV_LORA_RANK)),
                  full((CQ_PAD, hw)), full((KV_LORA_RANK, hw)), full((KV_LORA_RANK, hw)), tab, tab, tab],
        out_specs=[row, row, col, row, col],
        out_shape=[jax.ShapeDtypeStruct((SEQ, hw), BF16), jax.ShapeDtypeStruct((SEQ, hw), BF16),
                   jax.ShapeDtypeStruct((hw, SEQ), BF16), jax.ShapeDtypeStruct((SEQ, hw), BF16),
                   jax.ShapeDtypeStruct((hw, SEQ), BF16)],
        compiler_params=pltpu.CompilerParams(dimension_semantics=("arbitrary",), vmem_limit_bytes=VMEM_MID),
    )(h_c, h_c, h_c, gq, gkv, wq, wkn, wvx, c_t, sa_t, sb_t)


ATT_T = 512
ATT_STRIP = 64


def _attn_fwd(q, kt, vx, own):
    t, rs = ATT_T, ATT_STRIP
    nown = len(own)
    nq = SEQ // t
    nsteps = (MLA_HEADS // 2) * nq

    def body(q_ref, kt_ref, vx_ref, *rest):
        own_refs, (o_ref, l_ref), gat_refs = rest[:nown], rest[nown:nown + 2], rest[nown + 2:2 * nown + 2]
        s_scr, p_scr, m_scr, a_scr, acc_scr, send_sems, recv_sems, local_sems = rest[2 * nown + 2:]
        qi = pl.program_id(1)
        lane = lax.broadcasted_iota(jnp.int32, (t, LANES), 1)
        m_scr[...] = jnp.full((2, t, LANES), -1e30, F32)
        acc_scr[...] = jnp.zeros((2, t, LANES), F32)

        def block(j, masked):
            off = pl.multiple_of(j * t, t)
            for a in range(2):
                sl = slice(HEAD_PAD * a, HEAD_PAD * (a + 1))
                s_scr[a] = _dot(q_ref[:, sl], kt_ref[sl, pl.ds(off, t)], _NN)
                for r in range(t // rs):
                    rows = slice(rs * r, rs * (r + 1))
                    s = s_scr[a, rows, :]
                    if masked:
                        rowi = lax.broadcasted_iota(jnp.int32, (rs, t), 0) + rs * r
                        coli = lax.broadcasted_iota(jnp.int32, (rs, t), 1)
                        s = jnp.where(coli <= rowi, s, -1e30)
                    m_old = m_scr[a, rows, :]
                    m_new = jnp.maximum(m_old, jnp.max(s, axis=1, keepdims=True))
                    p_scr[a, rows, :] = jnp.exp(s - m_new[:, :1]).astype(BF16)
                    a_scr[a, rows, :] = jnp.exp(m_old - m_new)
                    m_scr[a, rows, :] = m_new
                acc_scr[a] = acc_scr[a] * a_scr[a] + _dot(p_scr[a], vx_ref[pl.ds(off, t), sl], _NN)

        def step(j, carry):
            block(j, False)
            return carry
        lax.fori_loop(0, qi, step, 0)
        block(qi, True)
        res = []
        for a in range(2):
            acc = acc_scr[a]
            l = acc[:, V_HEAD_DIM:V_HEAD_DIM + 1]
            res.append((acc / l, m_scr[a] + jnp.log(l)))
        o_ref[...] = jnp.where(lane < V_HEAD_DIM, res[0][0], pltpu.roll(res[1][0], V_HEAD_DIM, 1))
        l_ref[...] = jnp.where(lane < V_HEAD_DIM, res[0][1], res[1][1])
        _gather_behind(own_refs, gat_refs, send_sems, recv_sems, local_sems, pl.program_id(0) * nq + qi,
                       nsteps - 2, nsteps - 1)

    hbm = pl.BlockSpec(memory_space=pl.ANY)
    res = pl.pallas_call(
        body, name="attn_fwd", grid=(MLA_HEADS // 2, nq),
        in_specs=[pl.BlockSpec((t, 2 * HEAD_PAD), lambda p, i: (i, p)),
                  pl.BlockSpec((2 * HEAD_PAD, SEQ), lambda p, i: (p, 0)),
                  pl.BlockSpec((SEQ, 2 * HEAD_PAD), lambda p, i: (0, p))] + [hbm] * nown,
        out_specs=[pl.BlockSpec((t, LANES), lambda p, i: (i, p)),
                   pl.BlockSpec((t, LANES), lambda p, i: (i, p))] + [hbm] * nown,
        out_shape=[jax.ShapeDtypeStruct((SEQ, MLA_WIDTH), F32), jax.ShapeDtypeStruct((SEQ, MLA_WIDTH), F32)]
        + [jax.ShapeDtypeStruct((N_DEV,) + a.shape, a.dtype) for a in own],
        scratch_shapes=[pltpu.VMEM((2, t, t), F32), pltpu.VMEM((2, t, t), BF16), pltpu.VMEM((2, t, LANES), F32),
                        pltpu.VMEM((2, t, LANES), F32), pltpu.VMEM((2, t, LANES), F32)] + _exchange_sems(nown),
        compiler_params=pltpu.CompilerParams(dimension_semantics=("arbitrary", "arbitrary"), vmem_limit_bytes=VMEM_MID),
    )(q, kt, vx, *own)
    return res[0], res[1], res[2:]


def _exchange_parts(parts, lands, send_sems, recv_sems, local_sems):
    x, y, c = _mesh_pos()
    me = 4 * x + 2 * y + c
    peers = [(x, y, 1 - c), (1 - x, y, c), (x, 1 - y, c), (1 - x, 1 - y, c),
             (1 - x, y, 1 - c), (x, 1 - y, 1 - c), (1 - x, 1 - y, 1 - c)]
    remote, local = [], []
    for a, (part, land) in enumerate(zip(parts, lands)):
        for k, peer in enumerate(peers):
            t = 4 * peer[0] + 2 * peer[1] + peer[2]
            remote.append(_remote(part.at[t], land.at[me], send_sems, recv_sems, 7 * a + k, peer))
        local.append(pltpu.make_async_copy(part.at[me], land.at[me], local_sems.at[a]))
    return remote, local


def _exchange_start(first_step, exchange):
    remote, local = exchange

    @pl.when(first_step)
    def _():
        for cp in remote + local:
            cp.start()


def _exchange_finish(last_step, exchange):
    remote, local = exchange

    @pl.when(last_step)
    def _():
        for cp in remote:
            cp.wait_recv()
        for cp in remote:
            cp.wait_send()
        for cp in local:
            cp.wait()


def _exchange_sems(npart):
    return [pltpu.SemaphoreType.DMA((7 * npart,)), pltpu.SemaphoreType.DMA((7 * npart,)),
            pltpu.SemaphoreType.DMA((npart,))]


def _attn_bwd(q, kt, k, vxt, d_o, o, lse, parts):
    t, rs = ATT_T, ATT_STRIP
    nq = SEQ // t
    npart = len(parts)
    nsteps = MLA_HEADS // 2

    def body(q_ref, kt_ref, k_ref, vxt_ref, do_ref, o_ref, l_ref, *rest):
        part_refs, rest = rest[:npart], rest[npart:]
        dq_ref, dk_ref, dv_ref = rest[:3]
        land_refs, rest = rest[3:3 + npart], rest[3 + npart:]
        s_scr, dp_scr, p_scr, ds_scr, st_scr, send_sems, recv_sems, local_sems = rest
        exchange = _exchange_parts(part_refs, land_refs, send_sems, recv_sems, local_sems)
        _exchange_start(pl.program_id(0) == 0, exchange)
        dk_ref[...] = jnp.zeros_like(dk_ref)
        dv_ref[...] = jnp.zeros_like(dv_ref)
        lane = lax.broadcasted_iota(jnp.int32, (t, LANES), 1)

        def qtile(i, carry):
            ioff = pl.multiple_of(i * t, t)
            do_i = do_ref[pl.ds(ioff, t), :]
            o_i = o_ref[pl.ds(ioff, t), :]
            l_i = l_ref[pl.ds(ioff, t), :]
            for a in range(2):
                sl = slice(HEAD_PAD * a, HEAD_PAD * (a + 1))
                sel = (lane < V_HEAD_DIM) if a == 0 else (lane >= V_HEAD_DIM)
                doa = jnp.where(sel, do_i, 0.0)
                oa = o_i
                if a == 1:
                    doa = pltpu.roll(doa, V_HEAD_DIM, 1)
                    oa = pltpu.roll(o_i, V_HEAD_DIM, 1)
                st_scr[0] = jnp.broadcast_to(jnp.sum(doa * oa, axis=1, keepdims=True), (t, LANES))
                st_scr[1] = jnp.broadcast_to(l_i[:, V_HEAD_DIM * a:V_HEAD_DIM * a + 1], (t, LANES))
                doa_bf = doa.astype(BF16)
                qa = q_ref[pl.ds(ioff, t), sl]

                def block(j, masked, dq_acc, sl=sl, qa=qa, doa_bf=doa_bf):
                    joff = pl.multiple_of(j * t, t)
                    s_scr[...] = _dot(qa, kt_ref[sl, pl.ds(joff, t)], _NN)
                    dp_scr[...] = _dot(doa_bf, vxt_ref[sl, pl.ds(joff, t)], _NN)
                    for r in range(t // rs):
                        rows = slice(rs * r, rs * (r + 1))
                        p = jnp.exp(s_scr[rows, :] - st_scr[1, rows, :1])
                        if masked:
                            rowi = lax.broadcasted_iota(jnp.int32, (rs, t), 0) + rs * r
                            coli = lax.broadcasted_iota(jnp.int32, (rs, t), 1)
                            p = jnp.where(coli <= rowi, p, 0.0)
                        p_scr[rows, :] = p.astype(BF16)
                        ds_scr[rows, :] = (p * (dp_scr[rows, :] - st_scr[0, rows, :1])).astype(BF16)
                    dk_ref[pl.ds(joff, t), sl] += _dot(ds_scr[...], qa, _TN)
                    dv_ref[pl.ds(joff, t), sl] += _dot(p_scr[...], doa_bf, _TN)
                    return dq_acc + _dot(ds_scr[...], k_ref[pl.ds(joff, t), sl], _NN)

                dq_acc = lax.fori_loop(0, i, lambda j, acc: block(j, False, acc), jnp.zeros((t, HEAD_PAD), F32))
                dq_ref[pl.ds(ioff, t), sl] = block(i, True, dq_acc)
            return carry

        lax.fori_loop(0, nq, qtile, 0)
        _exchange_finish(pl.program_id(0) == nsteps - 1, exchange)

    hw = MLA_HEADS * HEAD_PAD
    wide = pl.BlockSpec((SEQ, 2 * HEAD_PAD), lambda p: (0, p))
    wide_t = pl.BlockSpec((2 * HEAD_PAD, SEQ), lambda p: (p, 0))
    narrow = pl.BlockSpec((SEQ, LANES), lambda p: (0, p))
    hbm = pl.BlockSpec(memory_space=pl.ANY)
    res = pl.pallas_call(
        body, name="attn_bwd", grid=(nsteps,),
        in_specs=[wide, wide_t, wide, wide_t, narrow, narrow, narrow] + [hbm] * npart,
        out_specs=[wide, wide, wide] + [hbm] * npart,
        out_shape=[jax.ShapeDtypeStruct((SEQ, hw), F32)] * 3 + [jax.ShapeDtypeStruct(p.shape, p.dtype) for p in parts],
        scratch_shapes=[pltpu.VMEM((t, t), F32), pltpu.VMEM((t, t), F32), pltpu.VMEM((t, t), BF16),
                        pltpu.VMEM((t, t), BF16), pltpu.VMEM((2, t, LANES), F32)] + _exchange_sems(npart),
        compiler_params=pltpu.CompilerParams(dimension_semantics=("arbitrary",), vmem_limit_bytes=VMEM_BIG),
    )(q, kt, k, vxt, d_o, o, lse, *parts)
    return res[0], res[1], res[2], res[3:]


def _sgu_math(u, v, zb, lg, lb, ws_ref, bias):
    ug, dug = _gelu_and_grad(u)
    vg, dvg = _gelu_and_grad(v)
    mu = jnp.mean(vg, axis=1, keepdims=True)
    xc = vg - mu
    rstd = lax.rsqrt(jnp.mean(xc * xc, axis=1, keepdims=True) + LN_EPS)
    xh = xc * rstd
    vn_bf = (xh * lg + lb).astype(BF16)
    grp = lax.broadcasted_iota(jnp.int32, (CHUNK, SGU_WIDTH), 1) // SGU_GROUP_DIM
    r_i = lax.broadcasted_iota(jnp.int32, (CHUNK, CHUNK), 0)
    c_i = lax.broadcasted_iota(jnp.int32, (CHUNK, CHUNK), 1)
    tri, tri_t = r_i >= c_i, r_i <= c_i
    mixed = bias
    for g in range(SGU_GROUPS):
        wt = jnp.where(tri, ws_ref[g], 0.0).astype(BF16)
        mixed = mixed + jnp.where(grp == g, _dot(wt, vn_bf, _NN), 0.0)
    sb = _sigmoid(zb)
    return ug, dug, dvg, rstd, xh, vn_bf, grp, tri, tri_t, mixed, sb


def _sgu_fwd(h_b, lg, lb, w_s, bias_full):
    def body(u_ref, v_ref, zb_ref, lg_ref, lb_ref, ws_ref, bias_ref, yb_ref):
        zb = zb_ref[...]
        ug, _, _, _, _, _, _, _, _, mixed, sb = _sgu_math(u_ref[...], v_ref[...], zb, lg_ref[...], lb_ref[...],
                                                       ws_ref, bias_ref[...])
        yb_ref[...] = (ug * mixed) * (zb * sb)

    blk = lambda c: pl.BlockSpec((CHUNK, SGU_WIDTH), lambda i, c=c: (i, c))
    full2 = lambda shape: pl.BlockSpec(shape, lambda i: (0, 0))
    return pl.pallas_call(
        body, name="sgu_fwd", grid=(SEQ // CHUNK,),
        in_specs=[blk(0), blk(1), blk(2), full2((1, SGU_WIDTH)), full2((1, SGU_WIDTH)),
                  pl.BlockSpec((SGU_GROUPS, CHUNK, CHUNK), lambda i: (0, 0, 0)), full2((CHUNK, SGU_WIDTH))],
        out_specs=pl.BlockSpec((CHUNK, SGU_WIDTH), lambda i: (i, 0)),
        out_shape=jax.ShapeDtypeStruct((SEQ, SGU_WIDTH), F32),
        compiler_params=pltpu.CompilerParams(dimension_semantics=("arbitrary",)),
    )(h_b, h_b, h_b, lg, lb, w_s, bias_full)


def _sgu_bwd(h_b, d_yb, lg, lb, w_s, w_st, bias_full, parts):
    nsteps = SEQ // CHUNK
    npart = len(parts)

    def body(u_ref, v_ref, zb_ref, dyb_ref, lg_ref, lb_ref, ws_ref, wst_ref, bias_ref, *rest):
        part_refs, rest = rest[:npart], rest[npart:]
        dhb_ref, dws_ref, dbs_ref, dlg_ref, dlb_ref, dbb_ref = rest[:6]
        land_refs, (dbias_acc, send_sems, recv_sems, local_sems) = rest[6:6 + npart], rest[6 + npart:]
        step = pl.program_id(0)
        exchange = _exchange_parts(part_refs, land_refs, send_sems, recv_sems, local_sems)
        _exchange_start(step == 0, exchange)

        @pl.when(step == 0)
        def _():
            dbb_ref[...] = jnp.zeros_like(dbb_ref)
            dws_ref[...] = jnp.zeros_like(dws_ref)
            dlg_ref[...] = jnp.zeros_like(dlg_ref)
            dlb_ref[...] = jnp.zeros_like(dlb_ref)
            dbias_acc[...] = jnp.zeros_like(dbias_acc)

        zb = zb_ref[...]
        lg = lg_ref[...]
        ug, dug, dvg, rstd, xh, vn_bf, grp, tri, tri_t, mixed, sb = _sgu_math(
            u_ref[...], v_ref[...], zb, lg, lb_ref[...], ws_ref, bias_ref[...])
        dyb = dyb_ref[...]
        dsgu = dyb * (zb * sb)
        dzb = dyb * (ug * mixed) * (sb * (1.0 + zb * (1.0 - sb)))
        du = dsgu * mixed * dug
        dmixed = dsgu * ug
        dbias_acc[...] += dmixed
        dvn = jnp.zeros((CHUNK, SGU_WIDTH), F32)
        for g in range(SGU_GROUPS):
            dm_g = jnp.where(grp == g, dmixed, 0.0).astype(BF16)
            wtt = jnp.where(tri_t, wst_ref[g], 0.0).astype(BF16)
            dvn = dvn + _dot(wtt, dm_g, _NN)
            dws_ref[g] += jnp.where(tri, _dot(dm_g, vn_bf, _NT), 0.0)
        dlg_ref[...] += jnp.sum(dvn * xh, axis=0, keepdims=True)
        dlb_ref[...] += jnp.sum(dvn, axis=0, keepdims=True)
        dxh = dvn * lg
        dvgel = rstd * (dxh - jnp.mean(dxh, axis=1, keepdims=True) - xh * jnp.mean(dxh * xh, axis=1, keepdims=True))
        _store_grad(dhb_ref, dbb_ref, 0, du)
        _store_grad(dhb_ref, dbb_ref, SGU_WIDTH, dvgel * dvg)
        _store_grad(dhb_ref, dbb_ref, 2 * SGU_WIDTH, dzb)

        @pl.when(step == nsteps - 1)
        def _():
            acc = dbias_acc[...]
            lane = lax.broadcasted_iota(jnp.int32, (CHUNK, LANES), 1)
            out = jnp.zeros((CHUNK, LANES), F32)
            for g in range(SGU_GROUPS):
                sg = jnp.sum(jnp.where(grp == g, acc, 0.0), axis=1, keepdims=True)
                out = jnp.where(lane == g, sg, out)
            dbs_ref[...] = out

        _exchange_finish(step == nsteps - 1, exchange)

    blk = lambda c: pl.BlockSpec((CHUNK, SGU_WIDTH), lambda i, c=c: (i, c))
    full2 = lambda shape: pl.BlockSpec(shape, lambda i: (0, 0))
    full3 = pl.BlockSpec((SGU_GROUPS, CHUNK, CHUNK), lambda i: (0, 0, 0))
    hbm = pl.BlockSpec(memory_space=pl.ANY)
    res = pl.pallas_call(
        body, name="sgu_bwd", grid=(nsteps,),
        in_specs=[blk(0), blk(1), blk(2), pl.BlockSpec((CHUNK, SGU_WIDTH), lambda i: (i, 0)),
                  full2((1, SGU_WIDTH)), full2((1, SGU_WIDTH)), full3, full3, full2((CHUNK, SGU_WIDTH))] + [hbm] * npart,
        out_specs=[pl.BlockSpec((CHUNK, SEG_B), lambda i: (i, 0)), full3, full2((CHUNK, LANES)),
                   full2((1, SGU_WIDTH)), full2((1, SGU_WIDTH)), full2((1, SEG_B))] + [hbm] * npart,
        out_shape=[jax.ShapeDtypeStruct((SEQ, SEG_B), BF16),
                   jax.ShapeDtypeStruct((SGU_GROUPS, CHUNK, CHUNK), F32),
                   jax.ShapeDtypeStruct((CHUNK, LANES), F32),
                   jax.ShapeDtypeStruct((1, SGU_WIDTH), F32), jax.ShapeDtypeStruct((1, SGU_WIDTH), F32),
                   jax.ShapeDtypeStruct((1, SEG_B), F32)] + [jax.ShapeDtypeStruct(p.shape, p.dtype) for p in parts],
        scratch_shapes=[pltpu.VMEM((CHUNK, SGU_WIDTH), F32)] + _exchange_sems(npart),
        compiler_params=pltpu.CompilerParams(dimension_semantics=("arbitrary",)),
    )(h_b, h_b, h_b, d_yb, lg, lb, w_s, w_st, bias_full, *parts)
    return res[:6], res[6:]


def _merge(x, o, h_a, y_b, target, w_oa, w_ob, w_out, ln_g, ln_b):
    tm = 256
    nsteps = SEQ // tm

    def body(x_ref, o_ref, ga_ref, gb_ref, za_ref, yb_ref, tgt_ref, woa_ref, wob_ref, wout_ref, lng_ref, lnb_ref,
             loss_ref, dxr_ref, dha_ref, do_ref, dyb_ref, poa_ref, pob_ref, pout_ref, dlng_ref, dlnb_ref, dba_ref,
             dwoa_ref, dwob_ref, dwout_ref):
        step = pl.program_id(0)

        @pl.when(step == 0)
        def _():
            for r in (loss_ref, dwoa_ref, dwob_ref, dwout_ref, dlng_ref, dlnb_ref, dba_ref):
                r[...] = jnp.zeros_like(r)

        o = o_ref[...]
        za = za_ref[...]
        sa = _sigmoid(za)
        ya_bf = (o * (za * sa)).astype(BF16)
        yb_bf = yb_ref[...].astype(BF16)
        woa, wob, wout = woa_ref[...], wob_ref[...], wout_ref[...]
        pa = _dot(ya_bf, woa, _NN)
        pb = _dot(yb_bf, wob, _NN)
        sga = _sigmoid(ga_ref[...])
        sgb = _sigmoid(gb_ref[...])
        merged_bf = (sga * pa + sgb * pb).astype(BF16)
        r = DN_ALPHA * x_ref[...] + _dot(merged_bf, wout, _NN)
        mu = jnp.mean(r, axis=1, keepdims=True)
        rc = r - mu
        rstd = lax.rsqrt(jnp.mean(rc * rc, axis=1, keepdims=True) + LN_EPS)
        xh = rc * rstd
        lng = lng_ref[...]
        y = xh * lng + lnb_ref[...]
        e = y - tgt_ref[...]
        loss_ref[...] += 0.5 * jnp.sum(jnp.sum(e * e, axis=1, keepdims=True) * (1.0 / D_MODEL), axis=0, keepdims=True)

        dy = e * (1.0 / D_MODEL)
        dlng_ref[...] += jnp.sum(dy * xh, axis=0, keepdims=True)
        dlnb_ref[...] += jnp.sum(dy, axis=0, keepdims=True)
        dxh = dy * lng
        dr = rstd * (dxh - jnp.mean(dxh, axis=1, keepdims=True) - xh * jnp.mean(dxh * xh, axis=1, keepdims=True))
        dxr_ref[...] = DN_ALPHA * dr
        dr_bf = dr.astype(BF16)
        dwout_ref[...] += _dot(merged_bf, dr_bf, _TN)
        dmerged = _dot(dr_bf, wout, _NT)
        dpa_bf = (dmerged * sga).astype(BF16)
        dpb_bf = (dmerged * sgb).astype(BF16)
        _store_grad(dha_ref, dba_ref, 0, dmerged * pa * (sga * (1.0 - sga)))
        _store_grad(dha_ref, dba_ref, D_MODEL, dmerged * pb * (sgb * (1.0 - sgb)))
        dwoa_ref[...] += _dot(ya_bf, dpa_bf, _TN)
        dwob_ref[...] += _dot(yb_bf, dpb_bf, _TN)
        dya = _dot(dpa_bf, woa, _NT)
        dyb_ref[...] = _dot(dpb_bf, wob, _NT)
        do_ref[...] = dya * (za * sa)
        _store_grad(dha_ref, dba_ref, 2 * D_MODEL, dya * o * (sa * (1.0 + za * (1.0 - sa))))

        @pl.when(step == nsteps - 1)
        def _():
            cols = D_MODEL // N_DEV
            for j in range(N_DEV):
                poa_ref[j] = dwoa_ref[:, cols * j:cols * (j + 1)].astype(BF16)
                pob_ref[j] = dwob_ref[:, cols * j:cols * (j + 1)].astype(BF16)
                pout_ref[j] = dwout_ref[cols * j:cols * (j + 1), :].astype(BF16)

    row = lambda w, c=0: pl.BlockSpec((tm, w), lambda i, c=c: (i, c))
    full = lambda shape: pl.BlockSpec(shape, lambda i: (0, 0))
    full3 = lambda shape: pl.BlockSpec(shape, lambda i: (0, 0, 0))
    return pl.pallas_call(
        body, name="merge", grid=(nsteps,),
        in_specs=[row(D_MODEL), row(MLA_WIDTH), row(D_MODEL, 0), row(D_MODEL, 1), row(MLA_WIDTH, 4), row(SGU_WIDTH),
                  row(D_MODEL), full((MLA_WIDTH, D_MODEL)), full((SGU_WIDTH, D_MODEL)), full((D_MODEL, D_MODEL)),
                  full((1, D_MODEL)), full((1, D_MODEL))],
        out_specs=[full((1, LANES)), row(D_MODEL), row(SEG_A), row(MLA_WIDTH), row(SGU_WIDTH),
                   full3((N_DEV, MLA_WIDTH, D_MODEL // N_DEV)), full3((N_DEV, SGU_WIDTH, D_MODEL // N_DEV)),
                   full3((N_DEV, D_MODEL // N_DEV, D_MODEL)), full((1, D_MODEL)), full((1, D_MODEL)), full((1, SEG_A))],
        out_shape=[jax.ShapeDtypeStruct((1, LANES), F32),
                   jax.ShapeDtypeStruct((SEQ, D_MODEL), F32), jax.ShapeDtypeStruct((SEQ, SEG_A), BF16),
                   jax.ShapeDtypeStruct((SEQ, MLA_WIDTH), F32), jax.ShapeDtypeStruct((SEQ, SGU_WIDTH), F32),
                   jax.ShapeDtypeStruct((N_DEV, MLA_WIDTH, D_MODEL // N_DEV), BF16),
                   jax.ShapeDtypeStruct((N_DEV, SGU_WIDTH, D_MODEL // N_DEV), BF16),
                   jax.ShapeDtypeStruct((N_DEV, D_MODEL // N_DEV, D_MODEL), BF16),
                   jax.ShapeDtypeStruct((1, D_MODEL), F32), jax.ShapeDtypeStruct((1, D_MODEL), F32),
                   jax.ShapeDtypeStruct((1, SEG_A), F32)],
        scratch_shapes=[pltpu.VMEM((MLA_WIDTH, D_MODEL), F32), pltpu.VMEM((SGU_WIDTH, D_MODEL), F32),
                        pltpu.VMEM((D_MODEL, D_MODEL), F32)],
        compiler_params=pltpu.CompilerParams(dimension_semantics=("arbitrary",), vmem_limit_bytes=VMEM_BIG),
    )(x, o, h_a, h_a, h_a, y_b, target, w_oa, w_ob, w_out, ln_g, ln_b)


def _mla_bwd(dq, dk, dv, h_c, xt_bf, gq, gkv, wq, wkn, wv, c_t, sa_t, sb_t, parts):
    tm = 256
    hw = MLA_HEADS * HEAD_PAD
    npart = len(parts)
    nsteps = SEQ // tm

    def body(dq_ref, dk_ref, dv_ref, cq_ref, ckv_ref, xt_ref, gq_ref, gkv_ref, wq_ref, wkn_ref, wv_ref, c_ref, sa_ref,
             sb_ref, *rest):
        part_refs, rest = rest[:npart], rest[npart:]
        dhc_ref, puq_ref, dwkn_ref, dwv_ref, dgq_ref, dgkv_ref, dbc_ref, dwc_ref = rest[:8]
        land_refs, (pre_ref, dwq_ref, dwc_acc, send_sems, recv_sems, local_sems) = rest[8:8 + npart], rest[8 + npart:]
        exchange = _exchange_parts(part_refs, land_refs, send_sems, recv_sems, local_sems)
        _exchange_start(pl.program_id(0) == 0, exchange)

        @pl.when(pl.program_id(0) == 0)
        def _():
            for r in (dwq_ref, dwc_acc, dwkn_ref, dwv_ref, dgq_ref, dgkv_ref, dbc_ref):
                r[...] = jnp.zeros_like(r)

        c, sa, sb = c_ref[...], sa_ref[...], sb_ref[...]
        lane = lax.broadcasted_iota(jnp.int32, (tm, LANES), 1)
        rope_lanes = jnp.logical_and(lane >= ROPE_LO, lane < ROPE_HI)

        cq = cq_ref[...]
        gq = gq_ref[...]
        rq = lax.rsqrt(jnp.sum(cq * cq, axis=1, keepdims=True) * (1.0 / Q_LORA_RANK) + RMS_EPS)
        nq = cq * rq
        cqn_bf = (nq * gq).astype(BF16)
        for h in range(MLA_HEADS):
            sl = slice(HEAD_PAD * h, HEAD_PAD * (h + 1))
            pre_ref[:, sl] = _rope_t(dq_ref[:, sl] * ATTN_SCALE, c, sa, sb).astype(BF16)
        dqpre_bf = pre_ref[...]
        dcqn = _dot(dqpre_bf, wq_ref[...], _NT)
        dwq_ref[...] += _dot(cqn_bf, dqpre_bf, _TN)
        dgq_ref[...] += jnp.sum(dcqn * nq, axis=0, keepdims=True)
        dnq = dcqn * gq
        _store_grad(dhc_ref, dbc_ref, 0,
                    rq * (dnq - nq * (jnp.sum(dnq * nq, axis=1, keepdims=True) * (1.0 / Q_LORA_RANK))))

        ckv = ckv_ref[...]
        gkv = gkv_ref[...]
        rkv = lax.rsqrt(jnp.sum(ckv * ckv, axis=1, keepdims=True) * (1.0 / KV_LORA_RANK) + RMS_EPS)
        nkv = ckv * rkv
        ckvn_bf = (nkv * gkv).astype(BF16)
        dk = dk_ref[...]
        dk_bf = dk.astype(BF16)
        dv_bf = dv_ref[...].astype(BF16)
        dckvn = _dot(dk_bf, wkn_ref[...], _NT) + _dot(dv_bf, wv_ref[...], _NT)
        dwkn_ref[...] += _dot(ckvn_bf, dk_bf, _TN)
        dwv_ref[...] += _dot(ckvn_bf, dv_bf, _TN)
        dgkv_ref[...] += jnp.sum(dckvn * nkv, axis=0, keepdims=True)
        dnkv = dckvn * gkv
        _store_grad(dhc_ref, dbc_ref, CQ_PAD, rkv * (
            dnkv - nkv * (jnp.sum(dnkv * nkv, axis=1, keepdims=True) * (1.0 / KV_LORA_RANK))))
        dkpe = jnp.zeros((tm, LANES), F32)
        for h in range(MLA_HEADS):
            dkpe = dkpe + dk[:, HEAD_PAD * h:HEAD_PAD * (h + 1)]
        _store_grad(dhc_ref, dbc_ref, CQ_PAD + LANES, _rope_t(jnp.where(rope_lanes, dkpe, 0.0), c, sa, sb))
        dwc_acc[...] += _dot(xt_ref[...], dhc_ref[...], _NN)

        @pl.when(pl.program_id(0) == SEQ // tm - 1)
        def _():
            dwc_ref[...] = dwc_acc[...].astype(BF16)
            rows = Q_LORA_RANK // N_DEV
            for j in range(N_DEV):
                for h in range(MLA_HEADS):
                    puq_ref[j, :, QK_HEAD_DIM * h:QK_HEAD_DIM * (h + 1)] = dwq_ref[
                        rows * j:rows * (j + 1), HEAD_PAD * h:HEAD_PAD * h + QK_HEAD_DIM].astype(BF16)

        _exchange_finish(pl.program_id(0) == nsteps - 1, exchange)

    full = lambda shape: pl.BlockSpec(shape, lambda i: (0, 0))
    row = lambda w, c=0: pl.BlockSpec((tm, w), lambda i, c=c: (i, c))
    hbm = pl.BlockSpec(memory_space=pl.ANY)
    res = pl.pallas_call(
        body, name="mla_bwd", grid=(nsteps,),
        in_specs=[row(hw), row(hw), row(hw), row(CQ_PAD, 0), row(LANES, CQ_PAD // LANES),
                  pl.BlockSpec((D_MODEL, tm), lambda i: (0, i)),
                  full((1, CQ_PAD)), full((1, KV_LORA_RANK)), full((CQ_PAD, hw)), full((KV_LORA_RANK, hw)),
                  full((KV_LORA_RANK, hw)), row(LANES), row(LANES), row(LANES)] + [hbm] * npart,
        out_specs=[row(SEG_C), pl.BlockSpec((N_DEV, Q_LORA_RANK // N_DEV, MLA_HEADS * QK_HEAD_DIM), lambda i: (0, 0, 0)),
                   full((KV_LORA_RANK, hw)), full((KV_LORA_RANK, hw)),
                   full((1, CQ_PAD)), full((1, KV_LORA_RANK)), full((1, SEG_C)), full((D_MODEL, SEG_C))] + [hbm] * npart,
        out_shape=[jax.ShapeDtypeStruct((SEQ, SEG_C), BF16),
                   jax.ShapeDtypeStruct((N_DEV, Q_LORA_RANK // N_DEV, MLA_HEADS * QK_HEAD_DIM), BF16),
                   jax.ShapeDtypeStruct((KV_LORA_RANK, hw), F32), jax.ShapeDtypeStruct((KV_LORA_RANK, hw), F32),
                   jax.ShapeDtypeStruct((1, CQ_PAD), F32), jax.ShapeDtypeStruct((1, KV_LORA_RANK), F32),
                   jax.ShapeDtypeStruct((1, SEG_C), F32), jax.ShapeDtypeStruct((D_MODEL, SEG_C), BF16)]
        + [jax.ShapeDtypeStruct(p.shape, p.dtype) for p in parts],
        scratch_shapes=[pltpu.VMEM((tm, hw), BF16), pltpu.VMEM((CQ_PAD, hw), F32), pltpu.VMEM((D_MODEL, SEG_C), F32)]
        + _exchange_sems(npart),
        compiler_params=pltpu.CompilerParams(dimension_semantics=("arbitrary",), vmem_limit_bytes=VMEM_MID),
    )(dq, dk, dv, h_c, h_c, xt_bf, gq, gkv, wq, wkn, wv, c_t, sa_t, sb_t, *parts)
    return res[:8], res[8:]


def _adamw_all(ws, gs, ms, vs):
    n = len(ws)
    c1 = 1.0 / (1.0 - ADAM_B1 ** ADAM_STEP)
    c2 = 1.0 / (1.0 - ADAM_B2 ** ADAM_STEP)

    def body(*refs):
        for idx in range(n):
            w, g, m, v = (refs[idx][...], refs[n + idx][...], refs[2 * n + idx][...], refs[3 * n + idx][...])
            m_new = ADAM_B1 * m + (1.0 - ADAM_B1) * g
            v_new = ADAM_B2 * v + (1.0 - ADAM_B2) * (g * g)
            delta = -ADAM_LR * ((m_new * c1) / (jnp.sqrt(v_new * c2) + ADAM_EPS) + ADAM_WD * w)
            refs[4 * n + idx][...] = delta
            refs[5 * n + idx][...] = m_new
            refs[6 * n + idx][...] = v_new

    shapes = [jax.ShapeDtypeStruct(w.shape, F32) for w in ws]
    outs = pl.pallas_call(
        body, name="adamw", out_shape=shapes * 3,
        compiler_params=pltpu.CompilerParams(vmem_limit_bytes=VMEM_BIG),
    )(*ws, *gs, *ms, *vs)
    return outs[:n], outs[n:2 * n], outs[2 * n:]


SHARD_W = IN_WIDTH // N_DEV

_PIECES = [(0, 384, 2, 0), (384, 512, 2, CQ_PAD), (512, 544, 2, CQ_PAD + LANES + ROPE_LO),
           (544, 1056, 0, 2 * D_MODEL), (1056, 1568, 1, 0), (1568, 2080, 1, SGU_WIDTH),
           (2080, 2592, 1, 2 * SGU_WIDTH), (2592, 3616, 0, 0), (3616, 4640, 0, D_MODEL)]


def _column_runs():
    runs = []
    for n0, n1, seg, d0 in _PIECES:
        for j in range(N_DEV):
            lo, hi = max(n0, j * SHARD_W), min(n1, (j + 1) * SHARD_W)
            if lo < hi:
                runs.append((j, lo - j * SHARD_W, hi - j * SHARD_W, seg, d0 + lo - n0))
    return runs


def _mesh_pos():
    return lax.axis_index("x"), lax.axis_index("y"), lax.axis_index("c")


def _remote(src, dst, send_sems, recv_sems, k, to):
    return pltpu.make_async_remote_copy(src_ref=src, dst_ref=dst, send_sem=send_sems.at[k], recv_sem=recv_sems.at[k],
                                        device_id=to, device_id_type=pl.DeviceIdType.MESH)


def _gather_exchange(gats, send_sems, recv_sems, meanwhile=None):
    x, y, c = _mesh_pos()
    me, sibling = (x, y, c), (x, y, 1 - c)
    chips = [(1 - x, y), (x, 1 - y), (1 - x, 1 - y)]

    def copy(a, k, blk, to):
        slab = gats[a].at[4 * blk[0] + 2 * blk[1] + blk[2]]
        return _remote(slab, slab, send_sems, recv_sems, 7 * a + k, to)

    arrays = range(len(gats))
    first = [copy(a, 1 + j, me, (*chip, c)) for j, chip in enumerate(chips) for a in arrays]
    first += [copy(a, 0, me, sibling) for a in arrays]
    for cp in first:
        cp.start()
    if meanwhile is not None:
        meanwhile()
    passed = []
    for j, chip in enumerate(chips):
        for a in arrays:
            copy(a, 1 + j, (*chip, c), me).wait_recv()
            fwd = copy(a, 4 + j, (*chip, c), sibling)
            fwd.start()
            passed.append(fwd)
    for a in arrays:
        copy(a, 0, sibling, me).wait_recv()
    for j, chip in enumerate(chips):
        for a in arrays:
            copy(a, 4 + j, (*chip, 1 - c), me).wait_recv()
    for cp in first + passed:
        cp.wait_send()


def _gather_behind(own, gats, send_sems, recv_sems, local_sems, step, mid, last):
    x, y, c = _mesh_pos()
    me, sibling = (x, y, c), (x, y, 1 - c)
    chips = [(1 - x, y), (x, 1 - y), (1 - x, 1 - y)]
    arrays = range(len(gats))

    def copy(a, k, blk, to, src=None):
        slab = gats[a].at[4 * blk[0] + 2 * blk[1] + blk[2]]
        return _remote(slab if src is None else src, slab, send_sems, recv_sems, 7 * a + k, to)

    first = [copy(a, 1 + j, me, (*chip, c), src=own[a]) for j, chip in enumerate(chips) for a in arrays]
    first += [copy(a, 0, me, sibling, src=own[a]) for a in arrays]
    local = [pltpu.make_async_copy(own[a], gats[a].at[4 * x + 2 * y + c], local_sems.at[a]) for a in arrays]
    passed = [copy(a, 4 + j, (*chip, c), sibling) for j, chip in enumerate(chips) for a in arrays]

    @pl.when(step == 0)
    def _():
        for cp in first + local:
            cp.start()

    @pl.when(step == mid)
    def _():
        for j, chip in enumerate(chips):
            for a in arrays:
                copy(a, 1 + j, (*chip, c), me).wait_recv()
        for cp in passed:
            cp.start()

    @pl.when(step == last)
    def _():
        for a in arrays:
            copy(a, 0, sibling, me).wait_recv()
        for j, chip in enumerate(chips):
            for a in arrays:
                copy(a, 4 + j, (*chip, 1 - c), me).wait_recv()
        for cp in first + passed:
            cp.wait_send()
        for cp in local:
            cp.wait()


def _gather_first(w_in, w_uq2, w_oa, w_ob, w_out, x2, pos_col, invf_lane):
    hw = MLA_HEADS * HEAD_PAD
    uq_rows = Q_LORA_RANK // N_DEV
    rows = 256

    def body(win_ref, wuq_ref, woa_ref, wob_ref, wout_ref, x_ref, pos_ref, invf_ref,
             wc_ref, wq_ref, winb_ref, oab_ref, obb_ref, outb_ref, xb_ref, xt_ref, c_ref, sa_ref, sb_ref,
             g_uq, blk0, send_sems, recv_sems):
        def local_work():
            for i in range(SEQ // rows):
                xi = x_ref[rows * i:rows * (i + 1), :]
                xb_ref[rows * i:rows * (i + 1), :] = xi.astype(BF16)
                xt_ref[:, rows * i:rows * (i + 1)] = xi.T.astype(BF16)
            ang = pos_ref[...].astype(F32) * invf_ref[...]
            cs, sn = jnp.cos(ang), jnp.sin(ang)
            lane = lax.broadcasted_iota(jnp.int32, ang.shape, 1)
            c_ref[...] = jnp.where(lane < ROPE_LO, 1.0, jnp.where(lane < ROPE_HI, cs, 0.0))
            sa_ref[...] = jnp.where(jnp.logical_and(lane >= ROPE_LO, lane < ROPE_MID), -sn, 0.0)
            sb_ref[...] = jnp.where(jnp.logical_and(lane >= ROPE_MID, lane < ROPE_HI), sn, 0.0)

        x, y, c = _mesh_pos()
        me = (x, y, c)
        winb_ref[...] = win_ref[0].astype(BF16)
        oab_ref[...] = woa_ref[0].astype(BF16)
        obb_ref[...] = wob_ref[0].astype(BF16)
        outb_ref[...] = wout_ref[0].astype(BF16)
        g_uq[4 * x + 2 * y + c] = wuq_ref[...].astype(BF16)

        chip0 = jnp.logical_and(x == 0, y == 0)
        south = c == 0
        half = D_MODEL // 2
        halves = [blk0.at[pl.ds(0, half)], blk0.at[pl.ds(half, half)]]

        def bcopy(k, to, part=None):
            ref = blk0 if part is None else halves[part]
            return _remote(ref, ref, send_sems, recv_sems, 7 + k, to)

        sends0 = [(0, (0, 0, 1), None), (1, (1, 0, 0), 0), (2, (0, 1, 0), 1), (3, (1, 0, 0), 1), (4, (0, 1, 0), 0)]

        @pl.when(jnp.logical_and(chip0, south))
        def _():
            blk0[...] = winb_ref[...]
            for k, to, part in sends0:
                bcopy(k, to, part).start()

        _gather_exchange([g_uq], send_sems, recv_sems, meanwhile=local_work)

        for (cx, cy), first_k, first_half, second_k in (((1, 0), 1, 0, 3), ((0, 1), 2, 1, 4)):
            @pl.when(jnp.logical_and(jnp.logical_and(x == cx, y == cy), south))
            def _(cx=cx, cy=cy, first_k=first_k, first_half=first_half, second_k=second_k):
                bcopy(first_k, me, first_half).wait_recv()
                onward = bcopy(5 + first_half, (1, 1, 0), first_half)
                onward.start()
                bcopy(second_k, me, 1 - first_half).wait_recv()
                north = bcopy(7, (cx, cy, 1))
                north.start()
                onward.wait_send()
                north.wait_send()

        @pl.when(jnp.logical_and(jnp.logical_and(x == 1, y == 1), south))
        def _():
            bcopy(5, me, 0).wait_recv()
            bcopy(6, me, 1).wait_recv()
            north = bcopy(7, (1, 1, 1))
            north.start()
            north.wait_send()

        @pl.when(jnp.logical_and(chip0, c == 1))
        def _():
            bcopy(0, me).wait_recv()

        @pl.when(jnp.logical_and(jnp.logical_not(chip0), c == 1))
        def _():
            bcopy(7, me).wait_recv()

        @pl.when(jnp.logical_and(chip0, south))
        def _():
            for k, to, part in sends0:
                bcopy(k, to, part).wait_send()

        for j, s0, s1, seg, d0 in _column_runs():
            if seg == 2:
                wc_ref[:, d0:d0 + (s1 - s0)] = blk0[:, s0:s1]
        zeros = lambda r, w: jnp.zeros((r, w), BF16)
        wc_ref[:, Q_LORA_RANK:CQ_PAD] = zeros(D_MODEL, CQ_PAD - Q_LORA_RANK)
        wc_ref[:, CQ_PAD + LANES:CQ_PAD + LANES + ROPE_LO] = zeros(D_MODEL, ROPE_LO)
        wc_ref[:, CQ_PAD + LANES + ROPE_HI:SEG_C] = zeros(D_MODEL, LANES - ROPE_HI)
        wq_ref[Q_LORA_RANK:CQ_PAD, :] = zeros(CQ_PAD - Q_LORA_RANK, hw)
        for h in range(MLA_HEADS):
            wq_ref[0:Q_LORA_RANK, HEAD_PAD * h + QK_HEAD_DIM:HEAD_PAD * (h + 1)] = zeros(Q_LORA_RANK, HEAD_PAD - QK_HEAD_DIM)
        for j in range(N_DEV):
            for h in range(MLA_HEADS):
                wq_ref[uq_rows * j:uq_rows * (j + 1), HEAD_PAD * h:HEAD_PAD * h + QK_HEAD_DIM] = g_uq[
                    j, :, QK_HEAD_DIM * h:QK_HEAD_DIM * (h + 1)]

    vmem = pl.BlockSpec(memory_space=pltpu.VMEM)
    return pl.pallas_call(
        body, name="gather_first",
        out_shape=[jax.ShapeDtypeStruct((D_MODEL, SEG_C), BF16), jax.ShapeDtypeStruct((CQ_PAD, hw), BF16),
                   jax.ShapeDtypeStruct(w_in.shape[1:], BF16), jax.ShapeDtypeStruct(w_oa.shape[1:], BF16),
                   jax.ShapeDtypeStruct(w_ob.shape[1:], BF16), jax.ShapeDtypeStruct(w_out.shape[1:], BF16),
                   jax.ShapeDtypeStruct((SEQ, D_MODEL), BF16), jax.ShapeDtypeStruct((D_MODEL, SEQ), BF16)]
        + [jax.ShapeDtypeStruct((SEQ, LANES), F32)] * 3,
        in_specs=[vmem] * 8, out_specs=[vmem] * 11,
        scratch_shapes=[pltpu.VMEM((N_DEV, uq_rows, MLA_HEADS * QK_HEAD_DIM), BF16), pltpu.VMEM((D_MODEL, SHARD_W), BF16),
                        pltpu.SemaphoreType.DMA((15,)), pltpu.SemaphoreType.DMA((15,))],
        compiler_params=pltpu.CompilerParams(vmem_limit_bytes=VMEM_BIG),
    )(w_in, w_uq2, w_oa, w_ob, w_out, x2, pos_col, invf_lane)


def _assemble_in(g_in):
    def body(g_ref, wa_ref, wb_ref):
        segs = [wa_ref, wb_ref]
        for j, s0, s1, seg, d0 in _column_runs():
            if seg < 2:
                segs[seg][:, d0:d0 + (s1 - s0)] = g_ref[j, :, s0:s1]

    return pl.pallas_call(
        body, name="assemble_in",
        out_shape=[jax.ShapeDtypeStruct((D_MODEL, SEG_A), BF16), jax.ShapeDtypeStruct((D_MODEL, SEG_B), BF16)],
        compiler_params=pltpu.CompilerParams(vmem_limit_bytes=VMEM_MID),
    )(g_in)


def _assemble_out(g_oa, g_ob, g_out):
    cols = D_MODEL // N_DEV

    def body(goa_ref, gob_ref, gout_ref, oa_ref, ob_ref, out_ref):
        for j in range(N_DEV):
            oa_ref[:, cols * j:cols * (j + 1)] = goa_ref[j]
            ob_ref[:, cols * j:cols * (j + 1)] = gob_ref[j]
            out_ref[cols * j:cols * (j + 1), :] = gout_ref[j]

    return pl.pallas_call(
        body, name="assemble_out",
        out_shape=[jax.ShapeDtypeStruct((MLA_WIDTH, D_MODEL), BF16), jax.ShapeDtypeStruct((SGU_WIDTH, D_MODEL), BF16),
                   jax.ShapeDtypeStruct((D_MODEL, D_MODEL), BF16)],
    )(g_oa, g_ob, g_out)


C_NAT = 544


P_IN_SPLIT = 896


def _to_parts(dwa, dwb):
    def body(dwa_hbm, dwb_hbm, phi_hbm, plo_hbm, dwa_ref, dwb_ref, phi_ref, plo_ref, in_sems, out_sems):
        loads = [pltpu.make_async_copy(dwa_hbm, dwa_ref, in_sems.at[0]), pltpu.make_async_copy(dwb_hbm, dwb_ref, in_sems.at[1])]
        for cp in loads:
            cp.start()
        phi_ref[0, :, 0:C_NAT] = jnp.zeros((P_IN_SPLIT, C_NAT), BF16)
        plo_ref[0, :, 0:C_NAT] = jnp.zeros((D_MODEL - P_IN_SPLIT, C_NAT), BF16)
        segs = [dwa_ref, dwb_ref]

        def place(j, s0, s1, seg, d0):
            phi_ref[j, :, s0:s1] = segs[seg][0:P_IN_SPLIT, d0:d0 + (s1 - s0)]
            plo_ref[j, :, s0:s1] = segs[seg][P_IN_SPLIT:D_MODEL, d0:d0 + (s1 - s0)]

        loads[0].wait()
        for run in _column_runs():
            if run[3] == 0:
                place(*run)
        loads[1].wait()
        stores = []
        for part in range(N_DEV):
            for run in _column_runs():
                if run[3] == 1 and run[0] == part:
                    place(*run)
            stores += [pltpu.make_async_copy(phi_ref.at[part], phi_hbm.at[part], out_sems.at[2 * part]),
                       pltpu.make_async_copy(plo_ref.at[part], plo_hbm.at[part], out_sems.at[2 * part + 1])]
            for cp in stores[-2:]:
                cp.start()
        for cp in stores:
            cp.wait()

    hbm = pl.BlockSpec(memory_space=pl.ANY)
    out_shape = [jax.ShapeDtypeStruct((N_DEV, P_IN_SPLIT, SHARD_W), BF16),
                 jax.ShapeDtypeStruct((N_DEV, D_MODEL - P_IN_SPLIT, SHARD_W), BF16)]
    return pl.pallas_call(
        body, name="to_parts", in_specs=[hbm, hbm], out_specs=[hbm, hbm], out_shape=out_shape,
        scratch_shapes=[pltpu.VMEM(dwa.shape, BF16), pltpu.VMEM(dwb.shape, BF16)]
        + [pltpu.VMEM(s.shape, BF16) for s in out_shape]
        + [pltpu.SemaphoreType.DMA((2,)), pltpu.SemaphoreType.DMA((2 * N_DEV,))],
        compiler_params=pltpu.CompilerParams(vmem_limit_bytes=VMEM_MID))(dwa, dwb)


def _dx_tail(dhs, ws, dx_res, dwc, p_uq, p_rep):
    ntile, sums_at = 8, 5
    tm = SEQ // ntile
    rep_rows = p_rep.shape[1]
    c_rows = D_MODEL // N_DEV
    spec = [((c_rows, C_NAT), BF16), (p_uq.shape[1:], BF16), ((rep_rows, LANES), F32)]
    n = len(spec)

    nseg = len(dhs)

    def body(*refs):
        dh_refs, w_refs = refs[:nseg], refs[nseg:2 * nseg]
        dxr_ref, dwc_ref, puq_ref, prep_ref, dx_ref, call_ref, guq_ref, repall_ref, pc_ref, c_all, rep_all = refs[
            2 * nseg:2 * nseg + 11]
        rest = refs[2 * nseg + 11:]
        ras, tbs, rbs = rest[0:n], rest[n:2 * n], rest[2 * n:3 * n]
        send_sems, recv_sems, gsend, grecv = rest[3 * n:]
        step = pl.program_id(0)
        x, y, c = _mesh_pos()
        me_idx = 4 * x + 2 * y + c
        me, sibling = (x, y, c), (x, y, 1 - c)
        others = [(1 - x, y), (x, 1 - y), (1 - x, 1 - y)]
        parts = [pc_ref, puq_ref, prep_ref]
        gats = [rep_all, c_all]

        def stage1(chip, a):
            return _remote(parts[a].at[2 * chip + (1 - c)], ras[a].at[chip], send_sems, recv_sems, 7 * a + chip, sibling)

        def stage2(k, a):
            cx, cy = others[k]
            return _remote(tbs[a].at[k], rbs[a].at[k], send_sems, recv_sems, 7 * a + 4 + k, (cx, cy, c))

        def gcopy(a, k, blk, to):
            slab = gats[a].at[4 * blk[0] + 2 * blk[1] + blk[2]]
            return _remote(slab, slab, gsend, grecv, 7 * a + k, to)

        def chip_sum(a, chip):
            return parts[a][2 * chip + c].astype(F32) + ras[a][chip].astype(F32)

        @pl.when(step == 0)
        def _():
            for j, s0, s1, seg, d0 in _column_runs():
                if seg == 2:
                    for r in range(N_DEV):
                        pc_ref[r, :, s0:s1] = dwc_ref[c_rows * r:c_rows * (r + 1), d0:d0 + (s1 - s0)]
            for chip in range(4):
                for a in range(n):
                    stage1(chip, a).start()

        @pl.when(step == 1)
        def _():
            for chip in range(4):
                for a in range(n):
                    stage1(chip, a).wait_recv()
            for k, (cx, cy) in enumerate(others):
                for a in range(n):
                    tbs[a][k] = chip_sum(a, 2 * cx + cy).astype(spec[a][1])
                    stage2(k, a).start()

        @pl.when(step == sums_at)
        def _():
            for k in range(3):
                for a in range(n):
                    stage2(k, a).wait_recv()
            sums = []
            for a in range(n):
                acc = chip_sum(a, 2 * x + y)
                for k in range(3):
                    acc = acc + rbs[a][k].astype(F32)
                sums.append(acc)
            c_all[me_idx] = sums[0].astype(BF16)
            guq_ref[...] = sums[1]
            rep_all[me_idx] = sums[2]
            for a in range(2):
                for j, chip in enumerate(others):
                    gcopy(a, 1 + j, me, (*chip, c)).start()
                gcopy(a, 0, me, sibling).start()

        acc = dxr_ref[...]
        for dh_ref, w_ref in zip(dh_refs, w_refs):
            acc = acc + _dot(dh_ref[...], w_ref[...], _NT)
        dx_ref[...] = acc

        @pl.when(step == ntile - 1)
        def _():
            for j, chip in enumerate(others):
                for a in range(2):
                    gcopy(a, 1 + j, (*chip, c), me).wait_recv()
                    gcopy(a, 4 + j, (*chip, c), sibling).start()
            for a in range(2):
                gcopy(a, 0, sibling, me).wait_recv()
                for j, chip in enumerate(others):
                    gcopy(a, 4 + j, (*chip, 1 - c), me).wait_recv()
            for a in range(2):
                gcopy(a, 0, me, sibling).wait_send()
                for j, chip in enumerate(others):
                    gcopy(a, 1 + j, me, (*chip, c)).wait_send()
                    gcopy(a, 4 + j, (*chip, c), sibling).wait_send()
            for a in range(n):
                for chip in range(4):
                    stage1(chip, a).wait_send()
                for k in range(3):
                    stage2(k, a).wait_send()
            call_ref[...] = c_all[...]
            repall_ref[...] = rep_all[...]

    row = lambda w: pl.BlockSpec((tm, w), lambda i: (i, 0))
    full = lambda shape: pl.BlockSpec(shape, lambda i: (0,) * len(shape))
    scratch = [pltpu.VMEM((N_DEV, c_rows, C_NAT), BF16), pltpu.VMEM((N_DEV, c_rows, C_NAT), BF16),
               pltpu.VMEM((N_DEV, rep_rows, LANES), F32)]
    for lead in (4, 3, 3):
        scratch += [pltpu.VMEM((lead,) + tuple(shape), dt) for shape, dt in spec]
    scratch += [pltpu.SemaphoreType.DMA((7 * n,)), pltpu.SemaphoreType.DMA((7 * n,)),
                pltpu.SemaphoreType.DMA((14,)), pltpu.SemaphoreType.DMA((14,))]
    return pl.pallas_call(
        body, name="dx_tail", grid=(SEQ // tm,),
        in_specs=[row(dh.shape[1]) for dh in dhs] + [full(w.shape) for w in ws]
        + [row(D_MODEL), full(dwc.shape), full(p_uq.shape), full(p_rep.shape)],
        out_specs=[row(D_MODEL), full((N_DEV, c_rows, C_NAT)), full(p_uq.shape[1:]), full((N_DEV, rep_rows, LANES))],
        out_shape=[jax.ShapeDtypeStruct((SEQ, D_MODEL), F32), jax.ShapeDtypeStruct((N_DEV, c_rows, C_NAT), BF16),
                   jax.ShapeDtypeStruct(p_uq.shape[1:], F32), jax.ShapeDtypeStruct((N_DEV, rep_rows, LANES), F32)],
        scratch_shapes=scratch,
        compiler_params=pltpu.CompilerParams(dimension_semantics=("arbitrary",), vmem_limit_bytes=VMEM_BIG),
    )(*dhs, *ws, dx_res, dwc, p_uq, p_rep)


def _sum_landed(landed, c_all):
    c_rows = D_MODEL // N_DEV

    def body(rhi_ref, rlo_ref, roa_ref, rob_ref, rout_ref, call_ref, gin_ref, goa_ref, gob_ref, gout_ref):
        def total(ref, sl):
            acc = ref[0, sl, :].astype(F32)
            for s in range(1, N_DEV):
                acc = acc + ref[s, sl, :].astype(F32)
            return acc

        x, y, c = _mesh_pos()
        dev0 = jnp.where(4 * x + 2 * y + c == 0, 1.0, 0.0)
        for j in range(N_DEV):
            sl = slice(c_rows * j, c_rows * (j + 1))
            below = c_rows * j < P_IN_SPLIT
            tot = total(rhi_ref, sl) if below else total(rlo_ref, slice(c_rows * j - P_IN_SPLIT, c_rows * (j + 1) - P_IN_SPLIT))
            gin_ref[0, sl, C_NAT:SHARD_W] = tot[:, C_NAT:SHARD_W]
            gin_ref[0, sl, 0:C_NAT] = tot[:, 0:C_NAT] + dev0 * call_ref[j].astype(F32)
        goa_ref[0] = total(roa_ref, slice(None))
        gob_ref[0] = total(rob_ref, slice(None))
        gout_ref[0] = total(rout_ref, slice(None))

    return pl.pallas_call(
        body, name="sum_landed",
        out_shape=[jax.ShapeDtypeStruct((1, D_MODEL, SHARD_W), F32)]
        + [jax.ShapeDtypeStruct((1,) + r.shape[1:], F32) for r in landed[2:]],
        compiler_params=pltpu.CompilerParams(vmem_limit_bytes=VMEM_MID),
    )(*landed, c_all)


_O_CQ, _O_CKV, _O_KPE, _O_ZA, _O_U, _O_V, _O_ZB, _O_GA, _O_GB = 0, 384, 512, 544, 1056, 1568, 2080, 2592, 3616


def _to_segments(w):
    z = lambda n: jnp.zeros(w.shape[:-1] + (n,), w.dtype)
    seg_a = jnp.concatenate([w[..., _O_GA:_O_GB], w[..., _O_GB:IN_WIDTH], w[..., _O_ZA:_O_U]], axis=-1)
    seg_b = jnp.concatenate([w[..., _O_U:_O_V], w[..., _O_V:_O_ZB], w[..., _O_ZB:_O_GA]], axis=-1)
    seg_c = jnp.concatenate([w[..., _O_CQ:_O_CKV], z(CQ_PAD - Q_LORA_RANK), w[..., _O_CKV:_O_KPE],
                             z(ROPE_LO), w[..., _O_KPE:_O_ZA], z(LANES - ROPE_HI)], axis=-1)
    return seg_a, seg_b, seg_c


def _from_segments(seg_a, seg_b, seg_c):
    kpe0 = CQ_PAD + LANES + ROPE_LO
    return jnp.concatenate([
        seg_c[..., 0:Q_LORA_RANK], seg_c[..., CQ_PAD:CQ_PAD + LANES], seg_c[..., kpe0:kpe0 + QK_ROPE_DIM],
        seg_a[..., 2 * D_MODEL:SEG_A], seg_b, seg_a[..., 0:2 * D_MODEL]], axis=-1)


def kernel(x, positions, w_in, b_in, g_q, w_uq, g_kv, w_ukv, w_oa, sgu_ln_g, sgu_ln_b, w_s, b_s, w_ob, w_out, ln_g, ln_b, loss_target, m_w_in, m_b_in, m_g_q, m_w_uq, m_g_kv, m_w_ukv, m_w_oa, m_sgu_ln_g, m_sgu_ln_b, m_w_s, m_b_s, m_w_ob, m_w_out, m_ln_g, m_ln_b, v_w_in, v_b_in, v_g_q, v_w_uq, v_g_kv, v_w_ukv, v_w_oa, v_sgu_ln_g, v_sgu_ln_b, v_w_s, v_b_s, v_w_ob, v_w_out, v_ln_g, v_ln_b):
    w_uq2 = w_uq[0].reshape(Q_LORA_RANK // N_DEV, MLA_HEADS * QK_HEAD_DIM)
    inv_freq = ROPE_THETA ** (-jnp.arange(0, QK_ROPE_DIM, 2, dtype=F32) / QK_ROPE_DIM)
    invf_lane = jnp.concatenate([jnp.zeros((ROPE_LO,), F32), inv_freq, inv_freq,
                                 jnp.zeros((LANES - ROPE_HI,), F32)]).reshape(1, LANES)
    first = _gather_first(w_in, w_uq2, w_oa, w_ob, w_out, x[0], positions.reshape(SEQ, 1), invf_lane)
    partials = _local_step(x[0], loss_target[0], first, b_in, g_q, g_kv, w_ukv, sgu_ln_g, sgu_ln_b, w_s, b_s, ln_g, ln_b)
    weights = dict(w_in=w_in, b_in=b_in, g_q=g_q, w_uq=w_uq, g_kv=g_kv, w_ukv=w_ukv, w_oa=w_oa, sgu_ln_g=sgu_ln_g,
                   sgu_ln_b=sgu_ln_b, w_s=w_s, b_s=b_s, w_ob=w_ob, w_out=w_out, ln_g=ln_g, ln_b=ln_b)
    moms = dict(w_in=m_w_in, b_in=m_b_in, g_q=m_g_q, w_uq=m_w_uq, g_kv=m_g_kv, w_ukv=m_w_ukv, w_oa=m_w_oa,
                sgu_ln_g=m_sgu_ln_g, sgu_ln_b=m_sgu_ln_b, w_s=m_w_s, b_s=m_b_s, w_ob=m_w_ob, w_out=m_w_out,
                ln_g=m_ln_g, ln_b=m_ln_b)
    vars_ = dict(w_in=v_w_in, b_in=v_b_in, g_q=v_g_q, w_uq=v_w_uq, g_kv=v_g_kv, w_ukv=v_w_ukv, w_oa=v_w_oa,
                 sgu_ln_g=v_sgu_ln_g, sgu_ln_b=v_sgu_ln_b, w_s=v_w_s, b_s=v_b_s, w_ob=v_w_ob, w_out=v_w_out,
                 ln_g=v_ln_g, ln_b=v_ln_b)
    return _reduce_and_update(partials, weights, moms, vars_)


def _local_step(x2, tgt, first, b_in, g_q, g_kv, w_ukv, sgu_ln_g, sgu_ln_b, w_s, b_s, ln_g, ln_b):
    wc, wq, win_b, oa_b, ob_b, out_b, x_bf, xt_bf, c_t, sa_t, sb_t = first
    ba, bb, bc = _to_segments(b_in)
    w_ukv_bf = w_ukv[0].astype(BF16)
    wkn = jnp.pad(w_ukv_bf[:, :, :QK_NOPE_DIM], ((0, 0), (0, 0), (0, HEAD_PAD - QK_NOPE_DIM))).reshape(KV_LORA_RANK, -1)
    wv = jnp.pad(w_ukv_bf[:, :, QK_NOPE_DIM:], ((0, 0), (0, 0), (0, HEAD_PAD - V_HEAD_DIM))).reshape(KV_LORA_RANK, -1)
    gq = jnp.pad(g_q, ((0, 0), (0, CQ_PAD - Q_LORA_RANK)))
    bias_full = jnp.repeat(b_s[0].T, SGU_GROUP_DIM, axis=1)
    w_s3 = w_s[0]
    w_st3 = jnp.swapaxes(w_s3, 1, 2)

    h_c = _mm(x_bf, wc, bias=bc, tm=512, tn=SEG_C, name="in_proj_c")
    q, k, kt, vx, vxt = _mla_prep(h_c, gq, g_kv, wq, wkn, wv, c_t, sa_t, sb_t)
    o, lse, (g_in,) = _attn_fwd(q, kt, vx, (win_b,))
    wa, wb = _assemble_in(g_in)
    h_a, (g_out,) = _mm(x_bf, wa, bias=ba, own=(out_b,), tm=512, tn=SEG_A // 2, name="in_proj_a")
    h_b, (g_oa, g_ob) = _mm(x_bf, wb, bias=bb, own=(oa_b, ob_b), tm=512, tn=SEG_B // 2, name="in_proj_b")
    y_b = _sgu_fwd(h_b, sgu_ln_g, sgu_ln_b, w_s3, bias_full)
    w_oa_f, w_ob_f, w_out_f = _assemble_out(g_oa, g_ob, g_out)

    (loss_row, dx_res, dh_a, d_o, d_yb, p_oa, p_ob, p_out, d_lng, d_lnb, d_ba) = _merge(
        x2, o, h_a, y_b, tgt, w_oa_f, w_ob_f, w_out_f, ln_g, ln_b)
    (dh_b, d_ws, d_bs_t, d_slg, d_slb, d_bb), (r_out,) = _sgu_bwd(h_b, d_yb, sgu_ln_g, sgu_ln_b, w_s3, w_st3, bias_full,
                                                                 (p_out,))
    d_wa, (r_oa,) = _mm(xt_bf, dh_a, out_dtype=BF16, parts=(p_oa,), tm=512, tn=512, name="dw_in_a")
    d_wb = _mm(xt_bf, dh_b, out_dtype=BF16, tm=512, tn=512, name="dw_in_b")
    p_hi, p_lo = _to_parts(d_wa, d_wb)
    dq, dk, dv, (r_hi,) = _attn_bwd(q, kt, k, vxt, d_o, o, lse, (p_hi,))
    (dh_c, p_uq, d_wkn, d_wv, d_gq, d_gkv, d_bc, d_wc), (r_lo, r_ob) = _mla_bwd(
        dq, dk, dv, h_c, xt_bf, gq, g_kv, wq, wkn, wv, c_t, sa_t, sb_t, (p_lo, p_ob))
    landed = (r_hi, r_lo, r_oa, r_ob, r_out)

    p_b_in = _from_segments(d_ba, d_bb, d_bc)
    p_w_ukv = jnp.concatenate([d_wkn.reshape(KV_LORA_RANK, MLA_HEADS, HEAD_PAD)[:, :, :QK_NOPE_DIM],
                               d_wv.reshape(KV_LORA_RANK, MLA_HEADS, HEAD_PAD)[:, :, :V_HEAD_DIM]], axis=-1)
    p_g_q = d_gq[:, :Q_LORA_RANK]
    p_b_s = d_bs_t[:, :SGU_GROUPS].T
    replicated = [p_b_in, p_g_q, d_gkv, p_w_ukv, d_slg, d_slb, d_ws, p_b_s, d_lng, d_lnb]
    return loss_row, ((dh_a, dh_b, dh_c), (wa, wb, wc), dx_res), landed, d_wc, p_uq, replicated


_NAMES = ["w_in", "b_in", "g_q", "w_uq", "g_kv", "w_ukv", "w_oa", "sgu_ln_g", "sgu_ln_b", "w_s", "b_s", "w_ob",
          "w_out", "ln_g", "ln_b"]
_REPLICATED = ["b_in", "g_q", "g_kv", "w_ukv", "sgu_ln_g", "sgu_ln_b", "w_s", "b_s", "ln_g", "ln_b"]


def _reduce_and_update(partials, weights, moms, vars_):
    loss_row, (dhs, ws, dx_res), landed, d_wc, p_uq, replicated = partials
    def piece(a):
        flat = a.reshape(-1)
        return jnp.pad(flat, (0, -flat.size % PACK_ALIGN))

    rep_flat = jnp.concatenate([piece(a) for a in replicated] + [piece(loss_row[0, :1])])
    rep_flat = jnp.pad(rep_flat, (0, N_DEV * PACK_R_ROWS * LANES - rep_flat.size))
    dx, c_all, g_uq, rep_all = _dx_tail(dhs, ws, dx_res, d_wc, p_uq, rep_flat.reshape(N_DEV, PACK_R_ROWS, LANES))
    g_in, g_oa, g_ob, g_out = _sum_landed(landed, c_all)
    rep_sum = rep_all.reshape(-1)
    grads, pos = dict(w_in=g_in, w_uq=g_uq, w_oa=g_oa, w_ob=g_ob, w_out=g_out), 0
    for nm in _REPLICATED:
        grads[nm] = rep_sum[pos:pos + weights[nm].size]
        pos += weights[nm].size + -weights[nm].size % PACK_ALIGN
    loss = rep_sum[pos]
    grads = {nm: grads[nm].reshape(weights[nm].shape) for nm in _NAMES}
    deltas, new_m, new_v = _adamw_all([weights[nm] for nm in _NAMES], [grads[nm] for nm in _NAMES],
                                      [moms[nm] for nm in _NAMES], [vars_[nm] for nm in _NAMES])
    return (loss, dx.reshape(1, SEQ, D_MODEL), *[grads[nm] for nm in _NAMES], *deltas, *new_m, *new_v)
```

```python
import math

import jax
import jax.numpy as jnp
from jax import lax
from jax.experimental import pallas as pl
from jax.experimental.pallas import tpu as pltpu

F32 = jnp.float32
BF16 = jnp.bfloat16

D_MODEL = 1024
SEQ = 2048
N_DEV = 8
MLA_HEADS = 8
Q_LORA_RANK = 384
KV_LORA_RANK = 128
QK_NOPE_DIM = 64
QK_ROPE_DIM = 32
V_HEAD_DIM = 64
QK_HEAD_DIM = QK_NOPE_DIM + QK_ROPE_DIM
MLA_WIDTH = MLA_HEADS * V_HEAD_DIM
ROPE_THETA = 10000.0
SGU_GROUPS = 8
SGU_GROUP_DIM = 64
SGU_WIDTH = SGU_GROUPS * SGU_GROUP_DIM
CHUNK = 128
RMS_EPS = 1e-6
LN_EPS = 1e-5
DN_ALPHA = 2.0 ** 0.25
IN_WIDTH = 4640
ATTN_SCALE = QK_HEAD_DIM ** -0.5

ADAM_LR = 0.001
ADAM_B1 = 0.9
ADAM_B2 = 0.999
ADAM_EPS = 1e-08
ADAM_WD = 0.01
ADAM_STEP = 10

LANES = 128
HEAD_PAD = 128
ROPE_LO = QK_NOPE_DIM
ROPE_MID = ROPE_LO + QK_ROPE_DIM // 2
ROPE_HI = ROPE_LO + QK_ROPE_DIM
CQ_PAD = 512

SEG_A = 2560
SEG_B = 1536
SEG_C = 768

PACK_R_ROWS = 272
PACK_ALIGN = 8 * LANES
VMEM_BIG = 56 * 1024 * 1024
VMEM_MID = 40 * 1024 * 1024


def _sigmoid(x):
    return 1.0 / (1.0 + jnp.exp(-x))


def _gelu_and_grad(x):
    c0 = math.sqrt(2.0 / math.pi)
    x2 = x * x
    t = jnp.tanh(c0 * (x + 0.044715 * x * x2))
    g = 0.5 * x * (1.0 + t)
    dg = 0.5 * (1.0 + t) + 0.5 * x * (1.0 - t * t) * (c0 * (1.0 + 3.0 * 0.044715 * x2))
    return g, dg


def _dot(a, b, dims):
    return lax.dot_general(a, b, (dims, ((), ())), preferred_element_type=F32)


_NN = ((1,), (0,))
_NT = ((1,), (1,))
_TN = ((0,), (0,))


def _store_grad(dh_ref, db_ref, col, val):
    cols = slice(col, col + val.shape[1])
    dh_ref[:, cols] = val.astype(BF16)
    db_ref[:, cols] += jnp.sum(val, axis=0, keepdims=True)


def _mm(a, b, *, tb=False, bias=None, add=None, out_dtype=F32, own=(), parts=(), tm, tn, name):
    m, k = a.shape
    n = b.shape[0] if tb else b.shape[1]
    assert m % tm == 0 and n % tn == 0 and not (own and parts)
    dims = _NT if tb else _NN
    nown = len(own) + len(parts)
    nm = m // tm
    nsteps = (n // tn) * nm

    def body(*refs):
        a_ref, b_ref = refs[0], refs[1]
        pos = 2
        r = _dot(a_ref[...], b_ref[...], dims)
        if bias is not None:
            r = r + refs[pos][...]; pos += 1
        if add is not None:
            r = r + refs[pos][...]; pos += 1
        own_refs = refs[pos:pos + nown]; pos += nown
        refs[pos][...] = r.astype(out_dtype)
        if nown:
            gat_refs = refs[pos + 1:pos + 1 + nown]
            send_sems, recv_sems, local_sems = refs[pos + 1 + nown:]
            step = pl.program_id(0) * nm + pl.program_id(1)
            if own:
                _gather_behind(own_refs, gat_refs, send_sems, recv_sems, local_sems, step, nsteps - 2, nsteps - 1)
            else:
                exchange = _exchange_parts(own_refs, gat_refs, send_sems, recv_sems, local_sems)
                _exchange_start(step == 0, exchange)
                _exchange_finish(step == nsteps - 1, exchange)

    b_spec = pl.BlockSpec((tn, k), lambda j, i: (j, 0)) if tb else pl.BlockSpec((k, tn), lambda j, i: (0, j))
    in_specs, args = [pl.BlockSpec((tm, k), lambda j, i: (i, 0)), b_spec], [a, b]
    if bias is not None:
        in_specs.append(pl.BlockSpec((1, tn), lambda j, i: (0, j))); args.append(bias)
    if add is not None:
        in_specs.append(pl.BlockSpec((tm, tn), lambda j, i: (i, j))); args.append(add)
    hbm = pl.BlockSpec(memory_space=pl.ANY)
    res = pl.pallas_call(
        body, name=name, grid=(n // tn, nm), in_specs=in_specs + [hbm] * nown,
        out_specs=[pl.BlockSpec((tm, tn), lambda j, i: (i, j))] + [hbm] * nown,
        out_shape=[jax.ShapeDtypeStruct((m, n), out_dtype)]
        + [jax.ShapeDtypeStruct((N_DEV,) + o.shape, o.dtype) for o in own]
        + [jax.ShapeDtypeStruct(p.shape, p.dtype) for p in parts],
        scratch_shapes=_exchange_sems(nown) if nown else [],
        compiler_params=pltpu.CompilerParams(dimension_semantics=("arbitrary", "arbitrary"), vmem_limit_bytes=VMEM_BIG),
    )(*args, *own, *parts)
    return (res[0], res[1:]) if nown else res[0]


def _rope(x, c, sa, sb):
    return x * c + pltpu.roll(x, LANES - 16, 1) * sa + pltpu.roll(x, 16, 1) * sb


def _rope_t(dy, c, sa, sb):
    return dy * c + pltpu.roll(dy * sa, 16, 1) + pltpu.roll(dy * sb, LANES - 16, 1)


def _mla_prep(h_c, gq, gkv, wq, wkn, wvx, c_t, sa_t, sb_t):
    tm = 256
    hw = MLA_HEADS * HEAD_PAD

    def body(cq_ref, ckv_ref, kpe_ref, gq_ref, gkv_ref, wq_ref, wkn_ref, wvx_ref, c_ref, sa_ref, sb_ref,
             q_ref, k_ref, kt_ref, vx_ref, vxt_ref):
        c, sa, sb = c_ref[...], sa_ref[...], sb_ref[...]
        cq = cq_ref[...]
        rq = lax.rsqrt(jnp.sum(cq * cq, axis=1, keepdims=True) * (1.0 / Q_LORA_RANK) + RMS_EPS)
        cqn = ((cq * rq) * gq_ref[...]).astype(BF16)
        qall = _dot(cqn, wq_ref[...], _NN)
        for h in range(MLA_HEADS):
            sl = slice(HEAD_PAD * h, HEAD_PAD * (h + 1))
            q_ref[:, sl] = (_rope(qall[:, sl], c, sa, sb) * ATTN_SCALE).astype(BF16)
        ckv = ckv_ref[...]
        rkv = lax.rsqrt(jnp.sum(ckv * ckv, axis=1, keepdims=True) * (1.0 / KV_LORA_RANK) + RMS_EPS)
        ckvn = ((ckv * rkv) * gkv_ref[...]).astype(BF16)
        knall = _dot(ckvn, wkn_ref[...], _NN)
        vall = _dot(ckvn, wvx_ref[...], _NN)
        kper = _rope(kpe_ref[...], c, sa, sb)
        ones_half = (lax.broadcasted_iota(jnp.int32, (tm, HEAD_PAD), 1) >= V_HEAD_DIM).astype(F32)
        for h in range(MLA_HEADS):
            sl = slice(HEAD_PAD * h, HEAD_PAD * (h + 1))
            kh = knall[:, sl] + kper
            vh = vall[:, sl] + ones_half
            k_ref[:, sl] = kh.astype(BF16)
            kt_ref[sl, :] = kh.T.astype(BF16)
            vx_ref[:, sl] = vh.astype(BF16)
            vxt_ref[sl, :] = vh.T.astype(BF16)

    full = lambda shape: pl.BlockSpec(shape, lambda i: (0, 0))
    tab = pl.BlockSpec((tm, LANES), lambda i: (i, 0))
    row = pl.BlockSpec((tm, hw), lambda i: (i, 0))
    col = pl.BlockSpec((hw, tm), lambda i: (0, i))
    return pl.pallas_call(
        body, name="mla_prep", grid=(SEQ // tm,),
        in_specs=[pl.BlockSpec((tm, CQ_PAD), lambda i: (i, 0)),
                  pl.BlockSpec((tm, LANES), lambda i: (i, CQ_PAD // LANES)),
                  pl.BlockSpec((tm, LANES), lambda i: (i, CQ_PAD // LANES + 1)),
                  full((1, CQ_PAD)), full((1, KV_LORA_RANK)),
                  full((CQ_PAD, hw)), full((KV_LORA_RANK, hw)), full((KV_LORA_RANK, hw)), tab, tab, tab],
        out_specs=[row, row, col, row, col],
        out_shape=[jax.ShapeDtypeStruct((SEQ, hw), BF16), jax.ShapeDtypeStruct((SEQ, hw), BF16),
                   jax.ShapeDtypeStruct((hw, SEQ), BF16), jax.ShapeDtypeStruct((SEQ, hw), BF16),
                   jax.ShapeDtypeStruct((hw, SEQ), BF16)],
        compiler_params=pltpu.CompilerParams(dimension_semantics=("arbitrary",), vmem_limit_bytes=VMEM_MID),
    )(h_c, h_c, h_c, gq, gkv, wq, wkn, wvx, c_t, sa_t, sb_t)


ATT_T = 512
ATT_STRIP = 64


def _attn_fwd(q, kt, vx, own):
    t, rs = ATT_T, ATT_STRIP
    nown = len(own)
    nq = SEQ // t
    nsteps = (MLA_HEADS // 2) * nq

    def body(q_ref, kt_ref, vx_ref, *rest):
        own_refs, (o_ref, l_ref), gat_refs = rest[:nown], rest[nown:nown + 2], rest[nown + 2:2 * nown + 2]
        s_scr, p_scr, m_scr, a_scr, acc_scr, send_sems, recv_sems, local_sems = rest[2 * nown + 2:]
        qi = pl.program_id(1)
        lane = lax.broadcasted_iota(jnp.int32, (t, LANES), 1)
        m_scr[...] = jnp.full((2, t, LANES), -1e30, F32)
        acc_scr[...] = jnp.zeros((2, t, LANES), F32)

        def block(j, masked):
            off = pl.multiple_of(j * t, t)
            for a in range(2):
                sl = slice(HEAD_PAD * a, HEAD_PAD * (a + 1))
                s_scr[a] = _dot(q_ref[:, sl], kt_ref[sl, pl.ds(off, t)], _NN)
                for r in range(t // rs):
                    rows = slice(rs * r, rs * (r + 1))
                    s = s_scr[a, rows, :]
                    if masked:
                        rowi = lax.broadcasted_iota(jnp.int32, (rs, t), 0) + rs * r
                        coli = lax.broadcasted_iota(jnp.int32, (rs, t), 1)
                        s = jnp.where(coli <= rowi, s, -1e30)
                    m_old = m_scr[a, rows, :]
                    m_new = jnp.maximum(m_old, jnp.max(s, axis=1, keepdims=True))
                    p_scr[a, rows, :] = jnp.exp(s - m_new[:, :1]).astype(BF16)
                    a_scr[a, rows, :] = jnp.exp(m_old - m_new)
                    m_scr[a, rows, :] = m_new
                acc_scr[a] = acc_scr[a] * a_scr[a] + _dot(p_scr[a], vx_ref[pl.ds(off, t), sl], _NN)

        def step(j, carry):
            block(j, False)
            return carry
        lax.fori_loop(0, qi, step, 0)
        block(qi, True)
        res = []
        for a in range(2):
            acc = acc_scr[a]
            l = acc[:, V_HEAD_DIM:V_HEAD_DIM + 1]
            res.append((acc / l, m_scr[a] + jnp.log(l)))
        o_ref[...] = jnp.where(lane < V_HEAD_DIM, res[0][0], pltpu.roll(res[1][0], V_HEAD_DIM, 1))
        l_ref[...] = jnp.where(lane < V_HEAD_DIM, res[0][1], res[1][1])
        _gather_behind(own_refs, gat_refs, send_sems, recv_sems, local_sems, pl.program_id(0) * nq + qi,
                       nsteps - 2, nsteps - 1)

    hbm = pl.BlockSpec(memory_space=pl.ANY)
    res = pl.pallas_call(
        body, name="attn_fwd", grid=(MLA_HEADS // 2, nq),
        in_specs=[pl.BlockSpec((t, 2 * HEAD_PAD), lambda p, i: (i, p)),
                  pl.BlockSpec((2 * HEAD_PAD, SEQ), lambda p, i: (p, 0)),
                  pl.BlockSpec((SEQ, 2 * HEAD_PAD), lambda p, i: (0, p))] + [hbm] * nown,
        out_specs=[pl.BlockSpec((t, LANES), lambda p, i: (i, p)),
                   pl.BlockSpec((t, LANES), lambda p, i: (i, p))] + [hbm] * nown,
        out_shape=[jax.ShapeDtypeStruct((SEQ, MLA_WIDTH), F32), jax.ShapeDtypeStruct((SEQ, MLA_WIDTH), F32)]
        + [jax.ShapeDtypeStruct((N_DEV,) + a.shape, a.dtype) for a in own],
        scratch_shapes=[pltpu.VMEM((2, t, t), F32), pltpu.VMEM((2, t, t), BF16), pltpu.VMEM((2, t, LANES), F32),
                        pltpu.VMEM((2, t, LANES), F32), pltpu.VMEM((2, t, LANES), F32)] + _exchange_sems(nown),
        compiler_params=pltpu.CompilerParams(dimension_semantics=("arbitrary", "arbitrary"), vmem_limit_bytes=VMEM_MID),
    )(q, kt, vx, *own)
    return res[0], res[1], res[2:]


def _exchange_parts(parts, lands, send_sems, recv_sems, local_sems):
    x, y, c = _mesh_pos()
    me = 4 * x + 2 * y + c
    peers = [(x, y, 1 - c), (1 - x, y, c), (x, 1 - y, c), (1 - x, 1 - y, c),
             (1 - x, y, 1 - c), (x, 1 - y, 1 - c), (1 - x, 1 - y, 1 - c)]
    remote, local = [], []
    for a, (part, land) in enumerate(zip(parts, lands)):
        for k, peer in enumerate(peers):
            t = 4 * peer[0] + 2 * peer[1] + peer[2]
            remote.append(_remote(part.at[t], land.at[me], send_sems, recv_sems, 7 * a + k, peer))
        local.append(pltpu.make_async_copy(part.at[me], land.at[me], local_sems.at[a]))
    return remote, local


def _exchange_start(first_step, exchange):
    remote, local = exchange

    @pl.when(first_step)
    def _():
        for cp in remote + local:
            cp.start()


def _exchange_finish(last_step, exchange):
    remote, local = exchange

    @pl.when(last_step)
    def _():
        for cp in remote:
            cp.wait_recv()
        for cp in remote:
            cp.wait_send()
        for cp in local:
            cp.wait()


def _exchange_sems(npart):
    return [pltpu.SemaphoreType.DMA((7 * npart,)), pltpu.SemaphoreType.DMA((7 * npart,)),
            pltpu.SemaphoreType.DMA((npart,))]


def _attn_bwd(q, kt, k, vxt, d_o, o, lse, parts):
    t, rs = ATT_T, ATT_STRIP
    nq = SEQ // t
    npart = len(parts)
    nsteps = MLA_HEADS // 2

    def body(q_ref, kt_ref, k_ref, vxt_ref, do_ref, o_ref, l_ref, *rest):
        part_refs, rest = rest[:npart], rest[npart:]
        dq_ref, dk_ref, dv_ref = rest[:3]
        land_refs, rest = rest[3:3 + npart], rest[3 + npart:]
        s_scr, dp_scr, p_scr, ds_scr, st_scr, send_sems, recv_sems, local_sems = rest
        exchange = _exchange_parts(part_refs, land_refs, send_sems, recv_sems, local_sems)
        _exchange_start(pl.program_id(0) == 0, exchange)
        dk_ref[...] = jnp.zeros_like(dk_ref)
        dv_ref[...] = jnp.zeros_like(dv_ref)
        lane = lax.broadcasted_iota(jnp.int32, (t, LANES), 1)

        def qtile(i, carry):
            ioff = pl.multiple_of(i * t, t)
            do_i = do_ref[pl.ds(ioff, t), :]
            o_i = o_ref[pl.ds(ioff, t), :]
            l_i = l_ref[pl.ds(ioff, t), :]
            for a in range(2):
                sl = slice(HEAD_PAD * a, HEAD_PAD * (a + 1))
                sel = (lane < V_HEAD_DIM) if a == 0 else (lane >= V_HEAD_DIM)
                doa = jnp.where(sel, do_i, 0.0)
                oa = o_i
                if a == 1:
                    doa = pltpu.roll(doa, V_HEAD_DIM, 1)
                    oa = pltpu.roll(o_i, V_HEAD_DIM, 1)
                st_scr[0] = jnp.broadcast_to(jnp.sum(doa * oa, axis=1, keepdims=True), (t, LANES))
                st_scr[1] = jnp.broadcast_to(l_i[:, V_HEAD_DIM * a:V_HEAD_DIM * a + 1], (t, LANES))
                doa_bf = doa.astype(BF16)
                qa = q_ref[pl.ds(ioff, t), sl]

                def block(j, masked, dq_acc, sl=sl, qa=qa, doa_bf=doa_bf):
                    joff = pl.multiple_of(j * t, t)
                    s_scr[...] = _dot(qa, kt_ref[sl, pl.ds(joff, t)], _NN)
                    dp_scr[...] = _dot(doa_bf, vxt_ref[sl, pl.ds(joff, t)], _NN)
                    for r in range(t // rs):
                        rows = slice(rs * r, rs * (r + 1))
                        p = jnp.exp(s_scr[rows, :] - st_scr[1, rows, :1])
                        if masked:
                            rowi = lax.broadcasted_iota(jnp.int32, (rs, t), 0) + rs * r
                            coli = lax.broadcasted_iota(jnp.int32, (rs, t), 1)
                            p = jnp.where(coli <= rowi, p, 0.0)
                        p_scr[rows, :] = p.astype(BF16)
                        ds_scr[rows, :] = (p * (dp_scr[rows, :] - st_scr[0, rows, :1])).astype(BF16)
                    dk_ref[pl.ds(joff, t), sl] += _dot(ds_scr[...], qa, _TN)
                    dv_ref[pl.ds(joff, t), sl] += _dot(p_scr[...], doa_bf, _TN)
                    return dq_acc + _dot(ds_scr[...], k_ref[pl.ds(joff, t), sl], _NN)

                dq_acc = lax.fori_loop(0, i, lambda j, acc: block(j, False, acc), jnp.zeros((t, HEAD_PAD), F32))
                dq_ref[pl.ds(ioff, t), sl] = block(i, True, dq_acc)
            return carry

        lax.fori_loop(0, nq, qtile, 0)
        _exchange_finish(pl.program_id(0) == nsteps - 1, exchange)

    hw = MLA_HEADS * HEAD_PAD
    wide = pl.BlockSpec((SEQ, 2 * HEAD_PAD), lambda p: (0, p))
    wide_t = pl.BlockSpec((2 * HEAD_PAD, SEQ), lambda p: (p, 0))
    narrow = pl.BlockSpec((SEQ, LANES), lambda p: (0, p))
    hbm = pl.BlockSpec(memory_space=pl.ANY)
    res = pl.pallas_call(
        body, name="attn_bwd", grid=(nsteps,),
        in_specs=[wide, wide_t, wide, wide_t, narrow, narrow, narrow] + [hbm] * npart,
        out_specs=[wide, wide, wide] + [hbm] * npart,
        out_shape=[jax.ShapeDtypeStruct((SEQ, hw), F32)] * 3 + [jax.ShapeDtypeStruct(p.shape, p.dtype) for p in parts],
        scratch_shapes=[pltpu.VMEM((t, t), F32), pltpu.VMEM((t, t), F32), pltpu.VMEM((t, t), BF16),
                        pltpu.VMEM((t, t), BF16), pltpu.VMEM((2, t, LANES), F32)] + _exchange_sems(npart),
        compiler_params=pltpu.CompilerParams(dimension_semantics=("arbitrary",), vmem_limit_bytes=VMEM_BIG),
    )(q, kt, k, vxt, d_o, o, lse, *parts)
    return res[0], res[1], res[2], res[3:]


def _sgu_math(u, v, zb, lg, lb, ws_ref, bias):
    ug, dug = _gelu_and_grad(u)
    vg, dvg = _gelu_and_grad(v)
    mu = jnp.mean(vg, axis=1, keepdims=True)
    xc = vg - mu
    rstd = lax.rsqrt(jnp.mean(xc * xc, axis=1, keepdims=True) + LN_EPS)
    xh = xc * rstd
    vn_bf = (xh * lg + lb).astype(BF16)
    grp = lax.broadcasted_iota(jnp.int32, (CHUNK, SGU_WIDTH), 1) // SGU_GROUP_DIM
    r_i = lax.broadcasted_iota(jnp.int32, (CHUNK, CHUNK), 0)
    c_i = lax.broadcasted_iota(jnp.int32, (CHUNK, CHUNK), 1)
    tri, tri_t = r_i >= c_i, r_i <= c_i
    mixed = bias
    for g in range(SGU_GROUPS):
        wt = jnp.where(tri, ws_ref[g], 0.0).astype(BF16)
        mixed = mixed + jnp.where(grp == g, _dot(wt, vn_bf, _NN), 0.0)
    sb = _sigmoid(zb)
    return ug, dug, dvg, rstd, xh, vn_bf, grp, tri, tri_t, mixed, sb


def _sgu_fwd(h_b, lg, lb, w_s, bias_full):
    def body(u_ref, v_ref, zb_ref, lg_ref, lb_ref, ws_ref, bias_ref, yb_ref):
        zb = zb_ref[...]
        ug, _, _, _, _, _, _, _, _, mixed, sb = _sgu_math(u_ref[...], v_ref[...], zb, lg_ref[...], lb_ref[...],
                                                       ws_ref, bias_ref[...])
        yb_ref[...] = (ug * mixed) * (zb * sb)

    blk = lambda c: pl.BlockSpec((CHUNK, SGU_WIDTH), lambda i, c=c: (i, c))
    full2 = lambda shape: pl.BlockSpec(shape, lambda i: (0, 0))
    return pl.pallas_call(
        body, name="sgu_fwd", grid=(SEQ // CHUNK,),
        in_specs=[blk(0), blk(1), blk(2), full2((1, SGU_WIDTH)), full2((1, SGU_WIDTH)),
                  pl.BlockSpec((SGU_GROUPS, CHUNK, CHUNK), lambda i: (0, 0, 0)), full2((CHUNK, SGU_WIDTH))],
        out_specs=pl.BlockSpec((CHUNK, SGU_WIDTH), lambda i: (i, 0)),
        out_shape=jax.ShapeDtypeStruct((SEQ, SGU_WIDTH), F32),
        compiler_params=pltpu.CompilerParams(dimension_semantics=("arbitrary",)),
    )(h_b, h_b, h_b, lg, lb, w_s, bias_full)


def _sgu_bwd(h_b, d_yb, lg, lb, w_s, w_st, bias_full, parts):
    nsteps = SEQ // CHUNK
    npart = len(parts)

    def body(u_ref, v_ref, zb_ref, dyb_ref, lg_ref, lb_ref, ws_ref, wst_ref, bias_ref, *rest):
        part_refs, rest = rest[:npart], rest[npart:]
        dhb_ref, dws_ref, dbs_ref, dlg_ref, dlb_ref, dbb_ref = rest[:6]
        land_refs, (dbias_acc, send_sems, recv_sems, local_sems) = rest[6:6 + npart], rest[6 + npart:]
        step = pl.program_id(0)
        exchange = _exchange_parts(part_refs, land_refs, send_sems, recv_sems, local_sems)
        _exchange_start(step == 0, exchange)

        @pl.when(step == 0)
        def _():
            dbb_ref[...] = jnp.zeros_like(dbb_ref)
            dws_ref[...] = jnp.zeros_like(dws_ref)
            dlg_ref[...] = jnp.zeros_like(dlg_ref)
            dlb_ref[...] = jnp.zeros_like(dlb_ref)
            dbias_acc[...] = jnp.zeros_like(dbias_acc)

        zb = zb_ref[...]
        lg = lg_ref[...]
        ug, dug, dvg, rstd, xh, vn_bf, grp, tri, tri_t, mixed, sb = _sgu_math(
            u_ref[...], v_ref[...], zb, lg, lb_ref[...], ws_ref, bias_ref[...])
        dyb = dyb_ref[...]
        dsgu = dyb * (zb * sb)
        dzb = dyb * (ug * mixed) * (sb * (1.0 + zb * (1.0 - sb)))
        du = dsgu * mixed * dug
        dmixed = dsgu * ug
        dbias_acc[...] += dmixed
        dvn = jnp.zeros((CHUNK, SGU_WIDTH), F32)
        for g in range(SGU_GROUPS):
            dm_g = jnp.where(grp == g, dmixed, 0.0).astype(BF16)
            wtt = jnp.where(tri_t, wst_ref[g], 0.0).astype(BF16)
            dvn = dvn + _dot(wtt, dm_g, _NN)
            dws_ref[g] += jnp.where(tri, _dot(dm_g, vn_bf, _NT), 0.0)
        dlg_ref[...] += jnp.sum(dvn * xh, axis=0, keepdims=True)
        dlb_ref[...] += jnp.sum(dvn, axis=0, keepdims=True)
        dxh = dvn * lg
        dvgel = rstd * (dxh - jnp.mean(dxh, axis=1, keepdims=True) - xh * jnp.mean(dxh * xh, axis=1, keepdims=True))
        _store_grad(dhb_ref, dbb_ref, 0, du)
        _store_grad(dhb_ref, dbb_ref, SGU_WIDTH, dvgel * dvg)
        _store_grad(dhb_ref, dbb_ref, 2 * SGU_WIDTH, dzb)

        @pl.when(step == nsteps - 1)
        def _():
            acc = dbias_acc[...]
            lane = lax.broadcasted_iota(jnp.int32, (CHUNK, LANES), 1)
            out = jnp.zeros((CHUNK, LANES), F32)
            for g in range(SGU_GROUPS):
                sg = jnp.sum(jnp.where(grp == g, acc, 0.0), axis=1, keepdims=True)
                out = jnp.where(lane == g, sg, out)
            dbs_ref[...] = out

        _exchange_finish(step == nsteps - 1, exchange)

    blk = lambda c: pl.BlockSpec((CHUNK, SGU_WIDTH), lambda i, c=c: (i, c))
    full2 = lambda shape: pl.BlockSpec(shape, lambda i: (0, 0))
    full3 = pl.BlockSpec((SGU_GROUPS, CHUNK, CHUNK), lambda i: (0, 0, 0))
    hbm = pl.BlockSpec(memory_space=pl.ANY)
    res = pl.pallas_call(
        body, name="sgu_bwd", grid=(nsteps,),
        in_specs=[blk(0), blk(1), blk(2), pl.BlockSpec((CHUNK, SGU_WIDTH), lambda i: (i, 0)),
                  full2((1, SGU_WIDTH)), full2((1, SGU_WIDTH)), full3, full3, full2((CHUNK, SGU_WIDTH))] + [hbm] * npart,
        out_specs=[pl.BlockSpec((CHUNK, SEG_B), lambda i: (i, 0)), full3, full2((CHUNK, LANES)),
                   full2((1, SGU_WIDTH)), full2((1, SGU_WIDTH)), full2((1, SEG_B))] + [hbm] * npart,
        out_shape=[jax.ShapeDtypeStruct((SEQ, SEG_B), BF16),
                   jax.ShapeDtypeStruct((SGU_GROUPS, CHUNK, CHUNK), F32),
                   jax.ShapeDtypeStruct((CHUNK, LANES), F32),
                   jax.ShapeDtypeStruct((1, SGU_WIDTH), F32), jax.ShapeDtypeStruct((1, SGU_WIDTH), F32),
                   jax.ShapeDtypeStruct((1, SEG_B), F32)] + [jax.ShapeDtypeStruct(p.shape, p.dtype) for p in parts],
        scratch_shapes=[pltpu.VMEM((CHUNK, SGU_WIDTH), F32)] + _exchange_sems(npart),
        compiler_params=pltpu.CompilerParams(dimension_semantics=("arbitrary",)),
    )(h_b, h_b, h_b, d_yb, lg, lb, w_s, w_st, bias_full, *parts)
    return res[:6], res[6:]


def _merge(x, o, h_a, y_b, target, w_oa, w_ob, w_out, ln_g, ln_b):
    tm = 256
    nsteps = SEQ // tm

    def body(x_ref, o_ref, ga_ref, gb_ref, za_ref, yb_ref, tgt_ref, woa_ref, wob_ref, wout_ref, lng_ref, lnb_ref,
             loss_ref, dxr_ref, dha_ref, do_ref, dyb_ref, poa_ref, pob_ref, pout_ref, dlng_ref, dlnb_ref, dba_ref,
             dwoa_ref, dwob_ref, dwout_ref):
        step = pl.program_id(0)

        @pl.when(step == 0)
        def _():
            for r in (loss_ref, dwoa_ref, dwob_ref, dwout_ref, dlng_ref, dlnb_ref, dba_ref):
                r[...] = jnp.zeros_like(r)

        o = o_ref[...]
        za = za_ref[...]
        sa = _sigmoid(za)
        ya_bf = (o * (za * sa)).astype(BF16)
        yb_bf = yb_ref[...].astype(BF16)
        woa, wob, wout = woa_ref[...], wob_ref[...], wout_ref[...]
        pa = _dot(ya_bf, woa, _NN)
        pb = _dot(yb_bf, wob, _NN)
        sga = _sigmoid(ga_ref[...])
        sgb = _sigmoid(gb_ref[...])
        merged_bf = (sga * pa + sgb * pb).astype(BF16)
        r = DN_ALPHA * x_ref[...] + _dot(merged_bf, wout, _NN)
        mu = jnp.mean(r, axis=1, keepdims=True)
        rc = r - mu
        rstd = lax.rsqrt(jnp.mean(rc * rc, axis=1, keepdims=True) + LN_EPS)
        xh = rc * rstd
        lng = lng_ref[...]
        y = xh * lng + lnb_ref[...]
        e = y - tgt_ref[...]
        loss_ref[...] += 0.5 * jnp.sum(jnp.sum(e * e, axis=1, keepdims=True) * (1.0 / D_MODEL), axis=0, keepdims=True)

        dy = e * (1.0 / D_MODEL)
        dlng_ref[...] += jnp.sum(dy * xh, axis=0, keepdims=True)
        dlnb_ref[...] += jnp.sum(dy, axis=0, keepdims=True)
        dxh = dy * lng
        dr = rstd * (dxh - jnp.mean(dxh, axis=1, keepdims=True) - xh * jnp.mean(dxh * xh, axis=1, keepdims=True))
        dxr_ref[...] = DN_ALPHA * dr
        dr_bf = dr.astype(BF16)
        dwout_ref[...] += _dot(merged_bf, dr_bf, _TN)
        dmerged = _dot(dr_bf, wout, _NT)
        dpa_bf = (dmerged * sga).astype(BF16)
        dpb_bf = (dmerged * sgb).astype(BF16)
        _store_grad(dha_ref, dba_ref, 0, dmerged * pa * (sga * (1.0 - sga)))
        _store_grad(dha_ref, dba_ref, D_MODEL, dmerged * pb * (sgb * (1.0 - sgb)))
        dwoa_ref[...] += _dot(ya_bf, dpa_bf, _TN)
        dwob_ref[...] += _dot(yb_bf, dpb_bf, _TN)
        dya = _dot(dpa_bf, woa, _NT)
        dyb_ref[...] = _dot(dpb_bf, wob, _NT)
        do_ref[...] = dya * (za * sa)
        _store_grad(dha_ref, dba_ref, 2 * D_MODEL, dya * o * (sa * (1.0 + za * (1.0 - sa))))

        @pl.when(step == nsteps - 1)
        def _():
            cols = D_MODEL // N_DEV
            for j in range(N_DEV):
                poa_ref[j] = dwoa_ref[:, cols * j:cols * (j + 1)].astype(BF16)
                pob_ref[j] = dwob_ref[:, cols * j:cols * (j + 1)].astype(BF16)
                pout_ref[j] = dwout_ref[cols * j:cols * (j + 1), :].astype(BF16)

    row = lambda w, c=0: pl.BlockSpec((tm, w), lambda i, c=c: (i, c))
    full = lambda shape: pl.BlockSpec(shape, lambda i: (0, 0))
    full3 = lambda shape: pl.BlockSpec(shape, lambda i: (0, 0, 0))
    return pl.pallas_call(
        body, name="merge", grid=(nsteps,),
        in_specs=[row(D_MODEL), row(MLA_WIDTH), row(D_MODEL, 0), row(D_MODEL, 1), row(MLA_WIDTH, 4), row(SGU_WIDTH),
                  row(D_MODEL), full((MLA_WIDTH, D_MODEL)), full((SGU_WIDTH, D_MODEL)), full((D_MODEL, D_MODEL)),
                  full((1, D_MODEL)), full((1, D_MODEL))],
        out_specs=[full((1, LANES)), row(D_MODEL), row(SEG_A), row(MLA_WIDTH), row(SGU_WIDTH),
                   full3((N_DEV, MLA_WIDTH, D_MODEL // N_DEV)), full3((N_DEV, SGU_WIDTH, D_MODEL // N_DEV)),
                   full3((N_DEV, D_MODEL // N_DEV, D_MODEL)), full((1, D_MODEL)), full((1, D_MODEL)), full((1, SEG_A))],
        out_shape=[jax.ShapeDtypeStruct((1, LANES), F32),
                   jax.ShapeDtypeStruct((SEQ, D_MODEL), F32), jax.ShapeDtypeStruct((SEQ, SEG_A), BF16),
                   jax.ShapeDtypeStruct((SEQ, MLA_WIDTH), F32), jax.ShapeDtypeStruct((SEQ, SGU_WIDTH), F32),
                   jax.ShapeDtypeStruct((N_DEV, MLA_WIDTH, D_MODEL // N_DEV), BF16),
                   jax.ShapeDtypeStruct((N_DEV, SGU_WIDTH, D_MODEL // N_DEV), BF16),
                   jax.ShapeDtypeStruct((N_DEV, D_MODEL // N_DEV, D_MODEL), BF16),
                   jax.ShapeDtypeStruct((1, D_MODEL), F32), jax.ShapeDtypeStruct((1, D_MODEL), F32),
                   jax.ShapeDtypeStruct((1, SEG_A), F32)],
        scratch_shapes=[pltpu.VMEM((MLA_WIDTH, D_MODEL), F32), pltpu.VMEM((SGU_WIDTH, D_MODEL), F32),
                        pltpu.VMEM((D_MODEL, D_MODEL), F32)],
        compiler_params=pltpu.CompilerParams(dimension_semantics=("arbitrary",), vmem_limit_bytes=VMEM_BIG),
    )(x, o, h_a, h_a, h_a, y_b, target, w_oa, w_ob, w_out, ln_g, ln_b)


def _mla_bwd(dq, dk, dv, h_c, xt_bf, gq, gkv, wq, wkn, wv, c_t, sa_t, sb_t, parts):
    tm = 256
    hw = MLA_HEADS * HEAD_PAD
    npart = len(parts)
    nsteps = SEQ // tm

    def body(dq_ref, dk_ref, dv_ref, cq_ref, ckv_ref, xt_ref, gq_ref, gkv_ref, wq_ref, wkn_ref, wv_ref, c_ref, sa_ref,
             sb_ref, *rest):
        part_refs, rest = rest[:npart], rest[npart:]
        dhc_ref, puq_ref, dwkn_ref, dwv_ref, dgq_ref, dgkv_ref, dbc_ref, dwc_ref = rest[:8]
        land_refs, (pre_ref, dwq_ref, dwc_acc, send_sems, recv_sems, local_sems) = rest[8:8 + npart], rest[8 + npart:]
        exchange = _exchange_parts(part_refs, land_refs, send_sems, recv_sems, local_sems)
        _exchange_start(pl.program_id(0) == 0, exchange)

        @pl.when(pl.program_id(0) == 0)
        def _():
            for r in (dwq_ref, dwc_acc, dwkn_ref, dwv_ref, dgq_ref, dgkv_ref, dbc_ref):
                r[...] = jnp.zeros_like(r)

        c, sa, sb = c_ref[...], sa_ref[...], sb_ref[...]
        lane = lax.broadcasted_iota(jnp.int32, (tm, LANES), 1)
        rope_lanes = jnp.logical_and(lane >= ROPE_LO, lane < ROPE_HI)

        cq = cq_ref[...]
        gq = gq_ref[...]
        rq = lax.rsqrt(jnp.sum(cq * cq, axis=1, keepdims=True) * (1.0 / Q_LORA_RANK) + RMS_EPS)
        nq = cq * rq
        cqn_bf = (nq * gq).astype(BF16)
        for h in range(MLA_HEADS):
            sl = slice(HEAD_PAD * h, HEAD_PAD * (h + 1))
            pre_ref[:, sl] = _rope_t(dq_ref[:, sl] * ATTN_SCALE, c, sa, sb).astype(BF16)
        dqpre_bf = pre_ref[...]
        dcqn = _dot(dqpre_bf, wq_ref[...], _NT)
        dwq_ref[...] += _dot(cqn_bf, dqpre_bf, _TN)
        dgq_ref[...] += jnp.sum(dcqn * nq, axis=0, keepdims=True)
        dnq = dcqn * gq
        _store_grad(dhc_ref, dbc_ref, 0,
                    rq * (dnq - nq * (jnp.sum(dnq * nq, axis=1, keepdims=True) * (1.0 / Q_LORA_RANK))))

        ckv = ckv_ref[...]
        gkv = gkv_ref[...]
        rkv = lax.rsqrt(jnp.sum(ckv * ckv, axis=1, keepdims=True) * (1.0 / KV_LORA_RANK) + RMS_EPS)
        nkv = ckv * rkv
        ckvn_bf = (nkv * gkv).astype(BF16)
        dk = dk_ref[...]
        dk_bf = dk.astype(BF16)
        dv_bf = dv_ref[...].astype(BF16)
        dckvn = _dot(dk_bf, wkn_ref[...], _NT) + _dot(dv_bf, wv_ref[...], _NT)
        dwkn_ref[...] += _dot(ckvn_bf, dk_bf, _TN)
        dwv_ref[...] += _dot(ckvn_bf, dv_bf, _TN)
        dgkv_ref[...] += jnp.sum(dckvn * nkv, axis=0, keepdims=True)
        dnkv = dckvn * gkv
        _store_grad(dhc_ref, dbc_ref, CQ_PAD, rkv * (
            dnkv - nkv * (jnp.sum(dnkv * nkv, axis=1, keepdims=True) * (1.0 / KV_LORA_RANK))))
        dkpe = jnp.zeros((tm, LANES), F32)
        for h in range(MLA_HEADS):
            dkpe = dkpe + dk[:, HEAD_PAD * h:HEAD_PAD * (h + 1)]
        _store_grad(dhc_ref, dbc_ref, CQ_PAD + LANES, _rope_t(jnp.where(rope_lanes, dkpe, 0.0), c, sa, sb))
        dwc_acc[...] += _dot(xt_ref[...], dhc_ref[...], _NN)

        @pl.when(pl.program_id(0) == SEQ // tm - 1)
        def _():
            dwc_ref[...] = dwc_acc[...].astype(BF16)
            rows = Q_LORA_RANK // N_DEV
            for j in range(N_DEV):
                for h in range(MLA_HEADS):
                    puq_ref[j, :, QK_HEAD_DIM * h:QK_HEAD_DIM * (h + 1)] = dwq_ref[
                        rows * j:rows * (j + 1), HEAD_PAD * h:HEAD_PAD * h + QK_HEAD_DIM].astype(BF16)

        _exchange_finish(pl.program_id(0) == nsteps - 1, exchange)

    full = lambda shape: pl.BlockSpec(shape, lambda i: (0, 0))
    row = lambda w, c=0: pl.BlockSpec((tm, w), lambda i, c=c: (i, c))
    hbm = pl.BlockSpec(memory_space=pl.ANY)
    res = pl.pallas_call(
        body, name="mla_bwd", grid=(nsteps,),
        in_specs=[row(hw), row(hw), row(hw), row(CQ_PAD, 0), row(LANES, CQ_PAD // LANES),
                  pl.BlockSpec((D_MODEL, tm), lambda i: (0, i)),
                  full((1, CQ_PAD)), full((1, KV_LORA_RANK)), full((CQ_PAD, hw)), full((KV_LORA_RANK, hw)),
                  full((KV_LORA_RANK, hw)), row(LANES), row(LANES), row(LANES)] + [hbm] * npart,
        out_specs=[row(SEG_C), pl.BlockSpec((N_DEV, Q_LORA_RANK // N_DEV, MLA_HEADS * QK_HEAD_DIM), lambda i: (0, 0, 0)),
                   full((KV_LORA_RANK, hw)), full((KV_LORA_RANK, hw)),
                   full((1, CQ_PAD)), full((1, KV_LORA_RANK)), full((1, SEG_C)), full((D_MODEL, SEG_C))] + [hbm] * npart,
        out_shape=[jax.ShapeDtypeStruct((SEQ, SEG_C), BF16),
                   jax.ShapeDtypeStruct((N_DEV, Q_LORA_RANK // N_DEV, MLA_HEADS * QK_HEAD_DIM), BF16),
                   jax.ShapeDtypeStruct((KV_LORA_RANK, hw), F32), jax.ShapeDtypeStruct((KV_LORA_RANK, hw), F32),
                   jax.ShapeDtypeStruct((1, CQ_PAD), F32), jax.ShapeDtypeStruct((1, KV_LORA_RANK), F32),
                   jax.ShapeDtypeStruct((1, SEG_C), F32), jax.ShapeDtypeStruct((D_MODEL, SEG_C), BF16)]
        + [jax.ShapeDtypeStruct(p.shape, p.dtype) for p in parts],
        scratch_shapes=[pltpu.VMEM((tm, hw), BF16), pltpu.VMEM((CQ_PAD, hw), F32), pltpu.VMEM((D_MODEL, SEG_C), F32)]
        + _exchange_sems(npart),
        compiler_params=pltpu.CompilerParams(dimension_semantics=("arbitrary",), vmem_limit_bytes=VMEM_MID),
    )(dq, dk, dv, h_c, h_c, xt_bf, gq, gkv, wq, wkn, wv, c_t, sa_t, sb_t, *parts)
    return res[:8], res[8:]


def _adamw_all(ws, gs, ms, vs):
    n = len(ws)
    c1 = 1.0 / (1.0 - ADAM_B1 ** ADAM_STEP)
    c2 = 1.0 / (1.0 - ADAM_B2 ** ADAM_STEP)

    def body(*refs):
        for idx in range(n):
            w, g, m, v = (refs[idx][...], refs[n + idx][...], refs[2 * n + idx][...], refs[3 * n + idx][...])
            m_new = ADAM_B1 * m + (1.0 - ADAM_B1) * g
            v_new = ADAM_B2 * v + (1.0 - ADAM_B2) * (g * g)
            delta = -ADAM_LR * ((m_new * c1) / (jnp.sqrt(v_new * c2) + ADAM_EPS) + ADAM_WD * w)
            refs[4 * n + idx][...] = delta
            refs[5 * n + idx][...] = m_new
            refs[6 * n + idx][...] = v_new

    shapes = [jax.ShapeDtypeStruct(w.shape, F32) for w in ws]
    outs = pl.pallas_call(
        body, name="adamw", out_shape=shapes * 3,
        compiler_params=pltpu.CompilerParams(vmem_limit_bytes=VMEM_BIG),
    )(*ws, *gs, *ms, *vs)
    return outs[:n], outs[n:2 * n], outs[2 * n:]


SHARD_W = IN_WIDTH // N_DEV

_PIECES = [(0, 384, 2, 0), (384, 512, 2, CQ_PAD), (512, 544, 2, CQ_PAD + LANES + ROPE_LO),
           (544, 1056, 0, 2 * D_MODEL), (1056, 1568, 1, 0), (1568, 2080, 1, SGU_WIDTH),
           (2080, 2592, 1, 2 * SGU_WIDTH), (2592, 3616, 0, 0), (3616, 4640, 0, D_MODEL)]


def _column_runs():
    runs = []
    for n0, n1, seg, d0 in _PIECES:
        for j in range(N_DEV):
            lo, hi = max(n0, j * SHARD_W), min(n1, (j + 1) * SHARD_W)
            if lo < hi:
                runs.append((j, lo - j * SHARD_W, hi - j * SHARD_W, seg, d0 + lo - n0))
    return runs


def _mesh_pos():
    return lax.axis_index("x"), lax.axis_index("y"), lax.axis_index("c")


def _remote(src, dst, send_sems, recv_sems, k, to):
    return pltpu.make_async_remote_copy(src_ref=src, dst_ref=dst, send_sem=send_sems.at[k], recv_sem=recv_sems.at[k],
                                        device_id=to, device_id_type=pl.DeviceIdType.MESH)


def _gather_exchange(gats, send_sems, recv_sems, meanwhile=None):
    x, y, c = _mesh_pos()
    me, sibling = (x, y, c), (x, y, 1 - c)
    chips = [(1 - x, y), (x, 1 - y), (1 - x, 1 - y)]

    def copy(a, k, blk, to):
        slab = gats[a].at[4 * blk[0] + 2 * blk[1] + blk[2]]
        return _remote(slab, slab, send_sems, recv_sems, 7 * a + k, to)

    arrays = range(len(gats))
    first = [copy(a, 1 + j, me, (*chip, c)) for j, chip in enumerate(chips) for a in arrays]
    first += [copy(a, 0, me, sibling) for a in arrays]
    for cp in first:
        cp.start()
    if meanwhile is not None:
        meanwhile()
    passed = []
    for j, chip in enumerate(chips):
        for a in arrays:
            copy(a, 1 + j, (*chip, c), me).wait_recv()
            fwd = copy(a, 4 + j, (*chip, c), sibling)
            fwd.start()
            passed.append(fwd)
    for a in arrays:
        copy(a, 0, sibling, me).wait_recv()
    for j, chip in enumerate(chips):
        for a in arrays:
            copy(a, 4 + j, (*chip, 1 - c), me).wait_recv()
    for cp in first + passed:
        cp.wait_send()


def _gather_behind(own, gats, send_sems, recv_sems, local_sems, step, mid, last):
    x, y, c = _mesh_pos()
    me, sibling = (x, y, c), (x, y, 1 - c)
    chips = [(1 - x, y), (x, 1 - y), (1 - x, 1 - y)]
    arrays = range(len(gats))

    def copy(a, k, blk, to, src=None):
        slab = gats[a].at[4 * blk[0] + 2 * blk[1] + blk[2]]
        return _remote(slab if src is None else src, slab, send_sems, recv_sems, 7 * a + k, to)

    first = [copy(a, 1 + j, me, (*chip, c), src=own[a]) for j, chip in enumerate(chips) for a in arrays]
    first += [copy(a, 0, me, sibling, src=own[a]) for a in arrays]
    local = [pltpu.make_async_copy(own[a], gats[a].at[4 * x + 2 * y + c], local_sems.at[a]) for a in arrays]
    passed = [copy(a, 4 + j, (*chip, c), sibling) for j, chip in enumerate(chips) for a in arrays]

    @pl.when(step == 0)
    def _():
        for cp in first + local:
            cp.start()

    @pl.when(step == mid)
    def _():
        for j, chip in enumerate(chips):
            for a in arrays:
                copy(a, 1 + j, (*chip, c), me).wait_recv()
        for cp in passed:
            cp.start()

    @pl.when(step == last)
    def _():
        for a in arrays:
            copy(a, 0, sibling, me).wait_recv()
        for j, chip in enumerate(chips):
            for a in arrays:
                copy(a, 4 + j, (*chip, 1 - c), me).wait_recv()
        for cp in first + passed:
            cp.wait_send()
        for cp in local:
            cp.wait()


def _gather_first(w_in, w_uq2, w_oa, w_ob, w_out, x2, pos_col, invf_lane):
    hw = MLA_HEADS * HEAD_PAD
    uq_rows = Q_LORA_RANK // N_DEV
    rows = 256

    def body(win_ref, wuq_ref, woa_ref, wob_ref, wout_ref, x_ref, pos_ref, invf_ref,
             wc_ref, wq_ref, winb_ref, oab_ref, obb_ref, outb_ref, xb_ref, xt_ref, c_ref, sa_ref, sb_ref,
             g_uq, blk0, send_sems, recv_sems):
        def local_work():
            for i in range(SEQ // rows):
                xi = x_ref[rows * i:rows * (i + 1), :]
                xb_ref[rows * i:rows * (i + 1), :] = xi.astype(BF16)
                xt_ref[:, rows * i:rows * (i + 1)] = xi.T.astype(BF16)
            ang = pos_ref[...].astype(F32) * invf_ref[...]
            cs, sn = jnp.cos(ang), jnp.sin(ang)
            lane = lax.broadcasted_iota(jnp.int32, ang.shape, 1)
            c_ref[...] = jnp.where(lane < ROPE_LO, 1.0, jnp.where(lane < ROPE_HI, cs, 0.0))
            sa_ref[...] = jnp.where(jnp.logical_and(lane >= ROPE_LO, lane < ROPE_MID), -sn, 0.0)
            sb_ref[...] = jnp.where(jnp.logical_and(lane >= ROPE_MID, lane < ROPE_HI), sn, 0.0)

        x, y, c = _mesh_pos()
        me = (x, y, c)
        winb_ref[...] = win_ref[0].astype(BF16)
        oab_ref[...] = woa_ref[0].astype(BF16)
        obb_ref[...] = wob_ref[0].astype(BF16)
        outb_ref[...] = wout_ref[0].astype(BF16)
        g_uq[4 * x + 2 * y + c] = wuq_ref[...].astype(BF16)

        chip0 = jnp.logical_and(x == 0, y == 0)
        south = c == 0
        half = D_MODEL // 2
        halves = [blk0.at[pl.ds(0, half)], blk0.at[pl.ds(half, half)]]

        def bcopy(k, to, part=None):
            ref = blk0 if part is None else halves[part]
            return _remote(ref, ref, send_sems, recv_sems, 7 + k, to)

        sends0 = [(0, (0, 0, 1), None), (1, (1, 0, 0), 0), (2, (0, 1, 0), 1), (3, (1, 0, 0), 1), (4, (0, 1, 0), 0)]

        @pl.when(jnp.logical_and(chip0, south))
        def _():
            blk0[...] = winb_ref[...]
            for k, to, part in sends0:
                bcopy(k, to, part).start()

        _gather_exchange([g_uq], send_sems, recv_sems, meanwhile=local_work)

        for (cx, cy), first_k, first_half, second_k in (((1, 0), 1, 0, 3), ((0, 1), 2, 1, 4)):
            @pl.when(jnp.logical_and(jnp.logical_and(x == cx, y == cy), south))
            def _(cx=cx, cy=cy, first_k=first_k, first_half=first_half, second_k=second_k):
                bcopy(first_k, me, first_half).wait_recv()
                onward = bcopy(5 + first_half, (1, 1, 0), first_half)
                onward.start()
                bcopy(second_k, me, 1 - first_half).wait_recv()
                north = bcopy(7, (cx, cy, 1))
                north.start()
                onward.wait_send()
                north.wait_send()

        @pl.when(jnp.logical_and(jnp.logical_and(x == 1, y == 1), south))
        def _():
            bcopy(5, me, 0).wait_recv()
            bcopy(6, me, 1).wait_recv()
            north = bcopy(7, (1, 1, 1))
            north.start()
            north.wait_send()

        @pl.when(jnp.logical_and(chip0, c == 1))
        def _():
            bcopy(0, me).wait_recv()

        @pl.when(jnp.logical_and(jnp.logical_not(chip0), c == 1))
        def _():
            bcopy(7, me).wait_recv()

        @pl.when(jnp.logical_and(chip0, south))
        def _():
            for k, to, part in sends0:
                bcopy(k, to, part).wait_send()

        for j, s0, s1, seg, d0 in _column_runs():
            if seg == 2:
                wc_ref[:, d0:d0 + (s1 - s0)] = blk0[:, s0:s1]
        zeros = lambda r, w: jnp.zeros((r, w), BF16)
        wc_ref[:, Q_LORA_RANK:CQ_PAD] = zeros(D_MODEL, CQ_PAD - Q_LORA_RANK)
        wc_ref[:, CQ_PAD + LANES:CQ_PAD + LANES + ROPE_LO] = zeros(D_MODEL, ROPE_LO)
        wc_ref[:, CQ_PAD + LANES + ROPE_HI:SEG_C] = zeros(D_MODEL, LANES - ROPE_HI)
        wq_ref[Q_LORA_RANK:CQ_PAD, :] = zeros(CQ_PAD - Q_LORA_RANK, hw)
        for h in range(MLA_HEADS):
            wq_ref[0:Q_LORA_RANK, HEAD_PAD * h + QK_HEAD_DIM:HEAD_PAD * (h + 1)] = zeros(Q_LORA_RANK, HEAD_PAD - QK_HEAD_DIM)
        for j in range(N_DEV):
            for h in range(MLA_HEADS):
                wq_ref[uq_rows * j:uq_rows * (j + 1), HEAD_PAD * h:HEAD_PAD * h + QK_HEAD_DIM] = g_uq[
                    j, :, QK_HEAD_DIM * h:QK_HEAD_DIM * (h + 1)]

    vmem = pl.BlockSpec(memory_space=pltpu.VMEM)
    return pl.pallas_call(
        body, name="gather_first",
        out_shape=[jax.ShapeDtypeStruct((D_MODEL, SEG_C), BF16), jax.ShapeDtypeStruct((CQ_PAD, hw), BF16),
                   jax.ShapeDtypeStruct(w_in.shape[1:], BF16), jax.ShapeDtypeStruct(w_oa.shape[1:], BF16),
                   jax.ShapeDtypeStruct(w_ob.shape[1:], BF16), jax.ShapeDtypeStruct(w_out.shape[1:], BF16),
                   jax.ShapeDtypeStruct((SEQ, D_MODEL), BF16), jax.ShapeDtypeStruct((D_MODEL, SEQ), BF16)]
        + [jax.ShapeDtypeStruct((SEQ, LANES), F32)] * 3,
        in_specs=[vmem] * 8, out_specs=[vmem] * 11,
        scratch_shapes=[pltpu.VMEM((N_DEV, uq_rows, MLA_HEADS * QK_HEAD_DIM), BF16), pltpu.VMEM((D_MODEL, SHARD_W), BF16),
                        pltpu.SemaphoreType.DMA((15,)), pltpu.SemaphoreType.DMA((15,))],
        compiler_params=pltpu.CompilerParams(vmem_limit_bytes=VMEM_BIG),
    )(w_in, w_uq2, w_oa, w_ob, w_out, x2, pos_col, invf_lane)


def _assemble_in(g_in):
    def body(g_ref, wa_ref, wb_ref):
        segs = [wa_ref, wb_ref]
        for j, s0, s1, seg, d0 in _column_runs():
            if seg < 2:
                segs[seg][:, d0:d0 + (s1 - s0)] = g_ref[j, :, s0:s1]

    return pl.pallas_call(
        body, name="assemble_in",
        out_shape=[jax.ShapeDtypeStruct((D_MODEL, SEG_A), BF16), jax.ShapeDtypeStruct((D_MODEL, SEG_B), BF16)],
        compiler_params=pltpu.CompilerParams(vmem_limit_bytes=VMEM_MID),
    )(g_in)


def _assemble_out(g_oa, g_ob, g_out):
    cols = D_MODEL // N_DEV

    def body(goa_ref, gob_ref, gout_ref, oa_ref, ob_ref, out_ref):
        for j in range(N_DEV):
            oa_ref[:, cols * j:cols * (j + 1)] = goa_ref[j]
            ob_ref[:, cols * j:cols * (j + 1)] = gob_ref[j]
            out_ref[cols * j:cols * (j + 1), :] = gout_ref[j]

    return pl.pallas_call(
        body, name="assemble_out",
        out_shape=[jax.ShapeDtypeStruct((MLA_WIDTH, D_MODEL), BF16), jax.ShapeDtypeStruct((SGU_WIDTH, D_MODEL), BF16),
                   jax.ShapeDtypeStruct((D_MODEL, D_MODEL), BF16)],
    )(g_oa, g_ob, g_out)


C_NAT = 544


P_IN_SPLIT = 896


def _to_parts(dwa, dwb):
    def body(dwa_ref, dwb_ref, phi_ref, plo_ref):
        phi_ref[0, :, 0:C_NAT] = jnp.zeros((P_IN_SPLIT, C_NAT), BF16)
        plo_ref[0, :, 0:C_NAT] = jnp.zeros((D_MODEL - P_IN_SPLIT, C_NAT), BF16)
        segs = [dwa_ref, dwb_ref]
        for j, s0, s1, seg, d0 in _column_runs():
            if seg < 2:
                phi_ref[j, :, s0:s1] = segs[seg][0:P_IN_SPLIT, d0:d0 + (s1 - s0)]
                plo_ref[j, :, s0:s1] = segs[seg][P_IN_SPLIT:D_MODEL, d0:d0 + (s1 - s0)]

    return pl.pallas_call(
        body, name="to_parts",
        out_shape=[jax.ShapeDtypeStruct((N_DEV, P_IN_SPLIT, SHARD_W), BF16),
                   jax.ShapeDtypeStruct((N_DEV, D_MODEL - P_IN_SPLIT, SHARD_W), BF16)],
        compiler_params=pltpu.CompilerParams(vmem_limit_bytes=VMEM_MID))(dwa, dwb)


def _dx_tail(dhs, ws, dx_res, dwc, p_uq, p_rep):
    ntile, sums_at = 8, 5
    tm = SEQ // ntile
    rep_rows = p_rep.shape[1]
    c_rows = D_MODEL // N_DEV
    spec = [((c_rows, C_NAT), BF16), (p_uq.shape[1:], BF16), ((rep_rows, LANES), F32)]
    n = len(spec)

    nseg = len(dhs)

    def body(*refs):
        dh_refs, w_refs = refs[:nseg], refs[nseg:2 * nseg]
        dxr_ref, dwc_ref, puq_ref, prep_ref, dx_ref, call_ref, guq_ref, repall_ref, pc_ref, c_all, rep_all = refs[
            2 * nseg:2 * nseg + 11]
        rest = refs[2 * nseg + 11:]
        ras, tbs, rbs = rest[0:n], rest[n:2 * n], rest[2 * n:3 * n]
        send_sems, recv_sems, gsend, grecv = rest[3 * n:]
        step = pl.program_id(0)
        x, y, c = _mesh_pos()
        me_idx = 4 * x + 2 * y + c
        me, sibling = (x, y, c), (x, y, 1 - c)
        others = [(1 - x, y), (x, 1 - y), (1 - x, 1 - y)]
        parts = [pc_ref, puq_ref, prep_ref]
        gats = [rep_all, c_all]

        def stage1(chip, a):
            return _remote(parts[a].at[2 * chip + (1 - c)], ras[a].at[chip], send_sems, recv_sems, 7 * a + chip, sibling)

        def stage2(k, a):
            cx, cy = others[k]
            return _remote(tbs[a].at[k], rbs[a].at[k], send_sems, recv_sems, 7 * a + 4 + k, (cx, cy, c))

        def gcopy(a, k, blk, to):
            slab = gats[a].at[4 * blk[0] + 2 * blk[1] + blk[2]]
            return _remote(slab, slab, gsend, grecv, 7 * a + k, to)

        def chip_sum(a, chip):
            return parts[a][2 * chip + c].astype(F32) + ras[a][chip].astype(F32)

        @pl.when(step == 0)
        def _():
            for j, s0, s1, seg, d0 in _column_runs():
                if seg == 2:
                    for r in range(N_DEV):
                        pc_ref[r, :, s0:s1] = dwc_ref[c_rows * r:c_rows * (r + 1), d0:d0 + (s1 - s0)]
            for chip in range(4):
                for a in range(n):
                    stage1(chip, a).start()

        @pl.when(step == 1)
        def _():
            for chip in range(4):
                for a in range(n):
                    stage1(chip, a).wait_recv()
            for k, (cx, cy) in enumerate(others):
                for a in range(n):
                    tbs[a][k] = chip_sum(a, 2 * cx + cy).astype(spec[a][1])
                    stage2(k, a).start()

        @pl.when(step == sums_at)
        def _():
            for k in range(3):
                for a in range(n):
                    stage2(k, a).wait_recv()
            sums = []
            for a in range(n):
                acc = chip_sum(a, 2 * x + y)
                for k in range(3):
                    acc = acc + rbs[a][k].astype(F32)
                sums.append(acc)
            c_all[me_idx] = sums[0].astype(BF16)
            guq_ref[...] = sums[1]
            rep_all[me_idx] = sums[2]
            for a in range(2):
                for j, chip in enumerate(others):
                    gcopy(a, 1 + j, me, (*chip, c)).start()
                gcopy(a, 0, me, sibling).start()

        acc = dxr_ref[...]
        for dh_ref, w_ref in zip(dh_refs, w_refs):
            acc = acc + _dot(dh_ref[...], w_ref[...], _NT)
        dx_ref[...] = acc

        @pl.when(step == ntile - 1)
        def _():
            for j, chip in enumerate(others):
                for a in range(2):
                    gcopy(a, 1 + j, (*chip, c), me).wait_recv()
                    gcopy(a, 4 + j, (*chip, c), sibling).start()
            for a in range(2):
                gcopy(a, 0, sibling, me).wait_recv()
                for j, chip in enumerate(others):
                    gcopy(a, 4 + j, (*chip, 1 - c), me).wait_recv()
            for a in range(2):
                gcopy(a, 0, me, sibling).wait_send()
                for j, chip in enumerate(others):
                    gcopy(a, 1 + j, me, (*chip, c)).wait_send()
                    gcopy(a, 4 + j, (*chip, c), sibling).wait_send()
            for a in range(n):
                for chip in range(4):
                    stage1(chip, a).wait_send()
                for k in range(3):
                    stage2(k, a).wait_send()
            call_ref[...] = c_all[...]
            repall_ref[...] = rep_all[...]

    row = lambda w: pl.BlockSpec((tm, w), lambda i: (i, 0))
    full = lambda shape: pl.BlockSpec(shape, lambda i: (0,) * len(shape))
    scratch = [pltpu.VMEM((N_DEV, c_rows, C_NAT), BF16), pltpu.VMEM((N_DEV, c_rows, C_NAT), BF16),
               pltpu.VMEM((N_DEV, rep_rows, LANES), F32)]
    for lead in (4, 3, 3):
        scratch += [pltpu.VMEM((lead,) + tuple(shape), dt) for shape, dt in spec]
    scratch += [pltpu.SemaphoreType.DMA((7 * n,)), pltpu.SemaphoreType.DMA((7 * n,)),
                pltpu.SemaphoreType.DMA((14,)), pltpu.SemaphoreType.DMA((14,))]
    return pl.pallas_call(
        body, name="dx_tail", grid=(SEQ // tm,),
        in_specs=[row(dh.shape[1]) for dh in dhs] + [full(w.shape) for w in ws]
        + [row(D_MODEL), full(dwc.shape), full(p_uq.shape), full(p_rep.shape)],
        out_specs=[row(D_MODEL), full((N_DEV, c_rows, C_NAT)), full(p_uq.shape[1:]), full((N_DEV, rep_rows, LANES))],
        out_shape=[jax.ShapeDtypeStruct((SEQ, D_MODEL), F32), jax.ShapeDtypeStruct((N_DEV, c_rows, C_NAT), BF16),
                   jax.ShapeDtypeStruct(p_uq.shape[1:], F32), jax.ShapeDtypeStruct((N_DEV, rep_rows, LANES), F32)],
        scratch_shapes=scratch, input_output_aliases={2 * nseg: 0},
        compiler_params=pltpu.CompilerParams(dimension_semantics=("arbitrary",), vmem_limit_bytes=VMEM_BIG),
    )(*dhs, *ws, dx_res, dwc, p_uq, p_rep)


def _sum_landed(landed, c_all):
    c_rows = D_MODEL // N_DEV

    def body(rhi_ref, rlo_ref, roa_ref, rob_ref, rout_ref, call_ref, gin_ref, goa_ref, gob_ref, gout_ref):
        def total(ref, sl):
            acc = ref[0, sl, :].astype(F32)
            for s in range(1, N_DEV):
                acc = acc + ref[s, sl, :].astype(F32)
            return acc

        x, y, c = _mesh_pos()
        dev0 = jnp.where(4 * x + 2 * y + c == 0, 1.0, 0.0)
        for j in range(N_DEV):
            sl = slice(c_rows * j, c_rows * (j + 1))
            below = c_rows * j < P_IN_SPLIT
            tot = total(rhi_ref, sl) if below else total(rlo_ref, slice(c_rows * j - P_IN_SPLIT, c_rows * (j + 1) - P_IN_SPLIT))
            gin_ref[0, sl, C_NAT:SHARD_W] = tot[:, C_NAT:SHARD_W]
            gin_ref[0, sl, 0:C_NAT] = tot[:, 0:C_NAT] + dev0 * call_ref[j].astype(F32)
        goa_ref[0] = total(roa_ref, slice(None))
        gob_ref[0] = total(rob_ref, slice(None))
        gout_ref[0] = total(rout_ref, slice(None))

    return pl.pallas_call(
        body, name="sum_landed",
        out_shape=[jax.ShapeDtypeStruct((1, D_MODEL, SHARD_W), F32)]
        + [jax.ShapeDtypeStruct((1,) + r.shape[1:], F32) for r in landed[2:]],
        compiler_params=pltpu.CompilerParams(vmem_limit_bytes=VMEM_MID),
    )(*landed, c_all)


_O_CQ, _O_CKV, _O_KPE, _O_ZA, _O_U, _O_V, _O_ZB, _O_GA, _O_GB = 0, 384, 512, 544, 1056, 1568, 2080, 2592, 3616


def _to_segments(w):
    z = lambda n: jnp.zeros(w.shape[:-1] + (n,), w.dtype)
    seg_a = jnp.concatenate([w[..., _O_GA:_O_GB], w[..., _O_GB:IN_WIDTH], w[..., _O_ZA:_O_U]], axis=-1)
    seg_b = jnp.concatenate([w[..., _O_U:_O_V], w[..., _O_V:_O_ZB], w[..., _O_ZB:_O_GA]], axis=-1)
    seg_c = jnp.concatenate([w[..., _O_CQ:_O_CKV], z(CQ_PAD - Q_LORA_RANK), w[..., _O_CKV:_O_KPE],
                             z(ROPE_LO), w[..., _O_KPE:_O_ZA], z(LANES - ROPE_HI)], axis=-1)
    return seg_a, seg_b, seg_c


def _from_segments(seg_a, seg_b, seg_c):
    kpe0 = CQ_PAD + LANES + ROPE_LO
    return jnp.concatenate([
        seg_c[..., 0:Q_LORA_RANK], seg_c[..., CQ_PAD:CQ_PAD + LANES], seg_c[..., kpe0:kpe0 + QK_ROPE_DIM],
        seg_a[..., 2 * D_MODEL:SEG_A], seg_b, seg_a[..., 0:2 * D_MODEL]], axis=-1)


def kernel(x, positions, w_in, b_in, g_q, w_uq, g_kv, w_ukv, w_oa, sgu_ln_g, sgu_ln_b, w_s, b_s, w_ob, w_out, ln_g, ln_b, loss_target, m_w_in, m_b_in, m_g_q, m_w_uq, m_g_kv, m_w_ukv, m_w_oa, m_sgu_ln_g, m_sgu_ln_b, m_w_s, m_b_s, m_w_ob, m_w_out, m_ln_g, m_ln_b, v_w_in, v_b_in, v_g_q, v_w_uq, v_g_kv, v_w_ukv, v_w_oa, v_sgu_ln_g, v_sgu_ln_b, v_w_s, v_b_s, v_w_ob, v_w_out, v_ln_g, v_ln_b):
    w_uq2 = w_uq[0].reshape(Q_LORA_RANK // N_DEV, MLA_HEADS * QK_HEAD_DIM)
    inv_freq = ROPE_THETA ** (-jnp.arange(0, QK_ROPE_DIM, 2, dtype=F32) / QK_ROPE_DIM)
    invf_lane = jnp.concatenate([jnp.zeros((ROPE_LO,), F32), inv_freq, inv_freq,
                                 jnp.zeros((LANES - ROPE_HI,), F32)]).reshape(1, LANES)
    first = _gather_first(w_in, w_uq2, w_oa, w_ob, w_out, x[0], positions.reshape(SEQ, 1), invf_lane)
    partials = _local_step(x[0], loss_target[0], first, b_in, g_q, g_kv, w_ukv, sgu_ln_g, sgu_ln_b, w_s, b_s, ln_g, ln_b)
    weights = dict(w_in=w_in, b_in=b_in, g_q=g_q, w_uq=w_uq, g_kv=g_kv, w_ukv=w_ukv, w_oa=w_oa, sgu_ln_g=sgu_ln_g,
                   sgu_ln_b=sgu_ln_b, w_s=w_s, b_s=b_s, w_ob=w_ob, w_out=w_out, ln_g=ln_g, ln_b=ln_b)
    moms = dict(w_in=m_w_in, b_in=m_b_in, g_q=m_g_q, w_uq=m_w_uq, g_kv=m_g_kv, w_ukv=m_w_ukv, w_oa=m_w_oa,
                sgu_ln_g=m_sgu_ln_g, sgu_ln_b=m_sgu_ln_b, w_s=m_w_s, b_s=m_b_s, w_ob=m_w_ob, w_out=m_w_out,
                ln_g=m_ln_g, ln_b=m_ln_b)
    vars_ = dict(w_in=v_w_in, b_in=v_b_in, g_q=v_g_q, w_uq=v_w_uq, g_kv=v_g_kv, w_ukv=v_w_ukv, w_oa=v_w_oa,
                 sgu_ln_g=v_sgu_ln_g, sgu_ln_b=v_sgu_ln_b, w_s=v_w_s, b_s=v_b_s, w_ob=v_w_ob, w_out=v_w_out,
                 ln_g=v_ln_g, ln_b=v_ln_b)
    return _reduce_and_update(partials, weights, moms, vars_)


def _local_step(x2, tgt, first, b_in, g_q, g_kv, w_ukv, sgu_ln_g, sgu_ln_b, w_s, b_s, ln_g, ln_b):
    wc, wq, win_b, oa_b, ob_b, out_b, x_bf, xt_bf, c_t, sa_t, sb_t = first
    ba, bb, bc = _to_segments(b_in)
    w_ukv_bf = w_ukv[0].astype(BF16)
    wkn = jnp.pad(w_ukv_bf[:, :, :QK_NOPE_DIM], ((0, 0), (0, 0), (0, HEAD_PAD - QK_NOPE_DIM))).reshape(KV_LORA_RANK, -1)
    wv = jnp.pad(w_ukv_bf[:, :, QK_NOPE_DIM:], ((0, 0), (0, 0), (0, HEAD_PAD - V_HEAD_DIM))).reshape(KV_LORA_RANK, -1)
    gq = jnp.pad(g_q, ((0, 0), (0, CQ_PAD - Q_LORA_RANK)))
    bias_full = jnp.repeat(b_s[0].T, SGU_GROUP_DIM, axis=1)
    w_s3 = w_s[0]
    w_st3 = jnp.swapaxes(w_s3, 1, 2)

    h_c = _mm(x_bf, wc, bias=bc, tm=512, tn=SEG_C, name="in_proj_c")
    q, k, kt, vx, vxt = _mla_prep(h_c, gq, g_kv, wq, wkn, wv, c_t, sa_t, sb_t)
    o, lse, (g_in,) = _attn_fwd(q, kt, vx, (win_b,))
    wa, wb = _assemble_in(g_in)
    h_a, (g_out,) = _mm(x_bf, wa, bias=ba, own=(out_b,), tm=512, tn=SEG_A // 2, name="in_proj_a")
    h_b, (g_oa, g_ob) = _mm(x_bf, wb, bias=bb, own=(oa_b, ob_b), tm=512, tn=SEG_B // 2, name="in_proj_b")
    y_b = _sgu_fwd(h_b, sgu_ln_g, sgu_ln_b, w_s3, bias_full)
    w_oa_f, w_ob_f, w_out_f = _assemble_out(g_oa, g_ob, g_out)

    (loss_row, dx_res, dh_a, d_o, d_yb, p_oa, p_ob, p_out, d_lng, d_lnb, d_ba) = _merge(
        x2, o, h_a, y_b, tgt, w_oa_f, w_ob_f, w_out_f, ln_g, ln_b)
    (dh_b, d_ws, d_bs_t, d_slg, d_slb, d_bb), (r_out,) = _sgu_bwd(h_b, d_yb, sgu_ln_g, sgu_ln_b, w_s3, w_st3, bias_full,
                                                                 (p_out,))
    d_wa, (r_oa,) = _mm(xt_bf, dh_a, out_dtype=BF16, parts=(p_oa,), tm=512, tn=512, name="dw_in_a")
    d_wb = _mm(xt_bf, dh_b, out_dtype=BF16, tm=512, tn=512, name="dw_in_b")
    p_hi, p_lo = _to_parts(d_wa, d_wb)
    dq, dk, dv, (r_hi,) = _attn_bwd(q, kt, k, vxt, d_o, o, lse, (p_hi,))
    (dh_c, p_uq, d_wkn, d_wv, d_gq, d_gkv, d_bc, d_wc), (r_lo, r_ob) = _mla_bwd(
        dq, dk, dv, h_c, xt_bf, gq, g_kv, wq, wkn, wv, c_t, sa_t, sb_t, (p_lo, p_ob))
    landed = (r_hi, r_lo, r_oa, r_ob, r_out)

    p_b_in = _from_segments(d_ba, d_bb, d_bc)
    p_w_ukv = jnp.concatenate([d_wkn.reshape(KV_LORA_RANK, MLA_HEADS, HEAD_PAD)[:, :, :QK_NOPE_DIM],
                               d_wv.reshape(KV_LORA_RANK, MLA_HEADS, HEAD_PAD)[:, :, :V_HEAD_DIM]], axis=-1)
    p_g_q = d_gq[:, :Q_LORA_RANK]
    p_b_s = d_bs_t[:, :SGU_GROUPS].T
    replicated = [p_b_in, p_g_q, d_gkv, p_w_ukv, d_slg, d_slb, d_ws, p_b_s, d_lng, d_lnb]
    return loss_row, ((dh_a, dh_b, dh_c), (wa, wb, wc), dx_res), landed, d_wc, p_uq, replicated


_NAMES = ["w_in", "b_in", "g_q", "w_uq", "g_kv", "w_ukv", "w_oa", "sgu_ln_g", "sgu_ln_b", "w_s", "b_s", "w_ob",
          "w_out", "ln_g", "ln_b"]
_REPLICATED = ["b_in", "g_q", "g_kv", "w_ukv", "sgu_ln_g", "sgu_ln_b", "w_s", "b_s", "ln_g", "ln_b"]


def _reduce_and_update(partials, weights, moms, vars_):
    loss_row, (dhs, ws, dx_res), landed, d_wc, p_uq, replicated = partials
    def piece(a):
        flat = a.reshape(-1)
        return jnp.pad(flat, (0, -flat.size % PACK_ALIGN))

    rep_flat = jnp.concatenate([piece(a) for a in replicated] + [piece(loss_row[0, :1])])
    rep_flat = jnp.pad(rep_flat, (0, N_DEV * PACK_R_ROWS * LANES - rep_flat.size))
    dx, c_all, g_uq, rep_all = _dx_tail(dhs, ws, dx_res, d_wc, p_uq, rep_flat.reshape(N_DEV, PACK_R_ROWS, LANES))
    g_in, g_oa, g_ob, g_out = _sum_landed(landed, c_all)
    rep_sum = rep_all.reshape(-1)
    grads, pos = dict(w_in=g_in, w_uq=g_uq, w_oa=g_oa, w_ob=g_ob, w_out=g_out), 0
    for nm in _REPLICATED:
        grads[nm] = rep_sum[pos:pos + weights[nm].size]
        pos += weights[nm].size + -weights[nm].size % PACK_ALIGN
    loss = rep_sum[pos]
    grads = {nm: grads[nm].reshape(weights[nm].shape) for nm in _NAMES}
    deltas, new_m, new_v = _adamw_all([weights[nm] for nm in _NAMES], [grads[nm] for nm in _NAMES],
                                      [moms[nm] for nm in _NAMES], [vars_[nm] for nm in _NAMES])
    return (loss, dx.reshape(1, SEQ, D_MODEL), *[grads[nm] for nm in _NAMES], *deltas, *new_m, *new_v)
```

```python
import math

import jax
import jax.numpy as jnp
from jax import lax
from jax.experimental import pallas as pl
from jax.experimental.pallas import tpu as pltpu

F32 = jnp.float32
BF16 = jnp.bfloat16

D_MODEL = 1024
SEQ = 2048
N_DEV = 8
MLA_HEADS = 8
Q_LORA_RANK = 384
KV_LORA_RANK = 128
QK_NOPE_DIM = 64
QK_ROPE_DIM = 32
V_HEAD_DIM = 64
QK_HEAD_DIM = QK_NOPE_DIM + QK_ROPE_DIM
MLA_WIDTH = MLA_HEADS * V_HEAD_DIM
ROPE_THETA = 10000.0
SGU_GROUPS = 8
SGU_GROUP_DIM = 64
SGU_WIDTH = SGU_GROUPS * SGU_GROUP_DIM
CHUNK = 128
RMS_EPS = 1e-6
LN_EPS = 1e-5
DN_ALPHA = 2.0 ** 0.25
IN_WIDTH = 4640
ATTN_SCALE = QK_HEAD_DIM ** -0.5

ADAM_LR = 0.001
ADAM_B1 = 0.9
ADAM_B2 = 0.999
ADAM_EPS = 1e-08
ADAM_WD = 0.01
ADAM_STEP = 10

LANES = 128
HEAD_PAD = 128
ROPE_LO = QK_NOPE_DIM
ROPE_MID = ROPE_LO + QK_ROPE_DIM // 2
ROPE_HI = ROPE_LO + QK_ROPE_DIM
CQ_PAD = 512

SEG_A = 2560
SEG_B = 1536
SEG_C = 768

PACK_R_ROWS = 272
PACK_ALIGN = 8 * LANES
VMEM_BIG = 56 * 1024 * 1024
VMEM_MID = 40 * 1024 * 1024


def _sigmoid(x):
    return 1.0 / (1.0 + jnp.exp(-x))


def _gelu_and_grad(x):
    c0 = math.sqrt(2.0 / math.pi)
    x2 = x * x
    t = jnp.tanh(c0 * (x + 0.044715 * x * x2))
    g = 0.5 * x * (1.0 + t)
    dg = 0.5 * (1.0 + t) + 0.5 * x * (1.0 - t * t) * (c0 * (1.0 + 3.0 * 0.044715 * x2))
    return g, dg


def _dot(a, b, dims):
    return lax.dot_general(a, b, (dims, ((), ())), preferred_element_type=F32)


_NN = ((1,), (0,))
_NT = ((1,), (1,))
_TN = ((0,), (0,))


def _store_grad(dh_ref, db_ref, col, val):
    cols = slice(col, col + val.shape[1])
    dh_ref[:, cols] = val.astype(BF16)
    db_ref[:, cols] += jnp.sum(val, axis=0, keepdims=True)


def _mm(a, b, *, tb=False, bias=None, add=None, out_dtype=F32, own=(), parts=(), tm, tn, name):
    m, k = a.shape
    n = b.shape[0] if tb else b.shape[1]
    assert m % tm == 0 and n % tn == 0 and not (own and parts)
    dims = _NT if tb else _NN
    nown = len(own) + len(parts)
    nm = m // tm
    nsteps = (n // tn) * nm

    def body(*refs):
        a_ref, b_ref = refs[0], refs[1]
        pos = 2 + (bias is not None) + (add is not None)
        own_refs, out_ref = refs[pos:pos + nown], refs[pos + nown]
        if nown:
            gat_refs = refs[pos + nown + 1:pos + 2 * nown + 1]
            send_sems, recv_sems, local_sems = refs[pos + 2 * nown + 1:]
            step = pl.program_id(0) * nm + pl.program_id(1)
            comm = (own_refs, gat_refs, send_sems, recv_sems, local_sems)
            if own:
                _gather_behind(*comm, step, nsteps - 2, nsteps - 1, phases=("start",))
            else:
                exchange = _exchange_parts(*comm)
                _exchange_start(step == 0, exchange)
        pos = 2
        r = _dot(a_ref[...], b_ref[...], dims)
        if bias is not None:
            r = r + refs[pos][...]; pos += 1
        if add is not None:
            r = r + refs[pos][...]; pos += 1
        out_ref[...] = r.astype(out_dtype)
        if own:
            _gather_behind(*comm, step, nsteps - 2, nsteps - 1, phases=("forward", "finish"))
        elif parts:
            _exchange_finish(step == nsteps - 1, exchange)

    b_spec = pl.BlockSpec((tn, k), lambda j, i: (j, 0)) if tb else pl.BlockSpec((k, tn), lambda j, i: (0, j))
    in_specs, args = [pl.BlockSpec((tm, k), lambda j, i: (i, 0)), b_spec], [a, b]
    if bias is not None:
        in_specs.append(pl.BlockSpec((1, tn), lambda j, i: (0, j))); args.append(bias)
    if add is not None:
        in_specs.append(pl.BlockSpec((tm, tn), lambda j, i: (i, j))); args.append(add)
    hbm = pl.BlockSpec(memory_space=pl.ANY)
    res = pl.pallas_call(
        body, name=name, grid=(n // tn, nm), in_specs=in_specs + [hbm] * nown,
        out_specs=[pl.BlockSpec((tm, tn), lambda j, i: (i, j))] + [hbm] * nown,
        out_shape=[jax.ShapeDtypeStruct((m, n), out_dtype)]
        + [jax.ShapeDtypeStruct((N_DEV,) + o.shape, o.dtype) for o in own]
        + [jax.ShapeDtypeStruct(p.shape, p.dtype) for p in parts],
        scratch_shapes=_exchange_sems(nown) if nown else [],
        compiler_params=pltpu.CompilerParams(dimension_semantics=("arbitrary", "arbitrary"), vmem_limit_bytes=VMEM_BIG),
    )(*args, *own, *parts)
    return (res[0], res[1:]) if nown else res[0]


def _rope(x, c, sa, sb):
    return x * c + pltpu.roll(x, LANES - 16, 1) * sa + pltpu.roll(x, 16, 1) * sb


def _rope_t(dy, c, sa, sb):
    return dy * c + pltpu.roll(dy * sa, 16, 1) + pltpu.roll(dy * sb, LANES - 16, 1)


def _mla_prep(h_c, gq, gkv, wq, wkn, wvx, c_t, sa_t, sb_t):
    tm = 256
    hw = MLA_HEADS * HEAD_PAD

    def body(cq_ref, ckv_ref, kpe_ref, gq_ref, gkv_ref, wq_ref, wkn_ref, wvx_ref, c_ref, sa_ref, sb_ref,
             q_ref, k_ref, kt_ref, vx_ref, vxt_ref):
        c, sa, sb = c_ref[...], sa_ref[...], sb_ref[...]
        cq = cq_ref[...]
        rq = lax.rsqrt(jnp.sum(cq * cq, axis=1, keepdims=True) * (1.0 / Q_LORA_RANK) + RMS_EPS)
        cqn = ((cq * rq) * gq_ref[...]).astype(BF16)
        qall = _dot(cqn, wq_ref[...], _NN)
        for h in range(MLA_HEADS):
            sl = slice(HEAD_PAD * h, HEAD_PAD * (h + 1))
            q_ref[:, sl] = (_rope(qall[:, sl], c, sa, sb) * ATTN_SCALE).astype(BF16)
        ckv = ckv_ref[...]
        rkv = lax.rsqrt(jnp.sum(ckv * ckv, axis=1, keepdims=True) * (1.0 / KV_LORA_RANK) + RMS_EPS)
        ckvn = ((ckv * rkv) * gkv_ref[...]).astype(BF16)
        knall = _dot(ckvn, wkn_ref[...], _NN)
        vall = _dot(ckvn, wvx_ref[...], _NN)
        kper = _rope(kpe_ref[...], c, sa, sb)
        ones_half = (lax.broadcasted_iota(jnp.int32, (tm, HEAD_PAD), 1) >= V_HEAD_DIM).astype(F32)
        for h in range(MLA_HEADS):
            sl = slice(HEAD_PAD * h, HEAD_PAD * (h + 1))
            kh = knall[:, sl] + kper
            vh = vall[:, sl] + ones_half
            k_ref[:, sl] = kh.astype(BF16)
            kt_ref[sl, :] = kh.T.astype(BF16)
            vx_ref[:, sl] = vh.astype(BF16)
            vxt_ref[sl, :] = vh.T.astype(BF16)

    full = lambda shape: pl.BlockSpec(shape, lambda i: (0, 0))
    tab = pl.BlockSpec((tm, LANES), lambda i: (i, 0))
    row = pl.BlockSpec((tm, hw), lambda i: (i, 0))
    col = pl.BlockSpec((hw, tm), lambda i: (0, i))
    return pl.pallas_call(
        body, name="mla_prep", grid=(SEQ // tm,),
        in_specs=[pl.BlockSpec((tm, CQ_PAD), lambda i: (i, 0)),
                  pl.BlockSpec((tm, LANES), lambda i: (i, CQ_PAD // LANES)),
                  pl.BlockSpec((tm, LANES), lambda i: (i, CQ_PAD // LANES + 1)),
                  full((1, CQ_PAD)), full((1, KV_LORA_RANK)),
                  full((CQ_PAD, hw)), full((KV_LORA_RANK, hw)), full((KV_LORA_RANK, hw)), tab, tab, tab],
        out_specs=[row, row, col, row, col],
        out_shape=[jax.ShapeDtypeStruct((SEQ, hw), BF16), jax.ShapeDtypeStruct((SEQ, hw), BF16),
                   jax.ShapeDtypeStruct((hw, SEQ), BF16), jax.ShapeDtypeStruct((SEQ, hw), BF16),
                   jax.ShapeDtypeStruct((hw, SEQ), BF16)],
        compiler_params=pltpu.CompilerParams(dimension_semantics=("arbitrary",), vmem_limit_bytes=VMEM_MID),
    )(h_c, h_c, h_c, gq, gkv, wq, wkn, wvx, c_t, sa_t, sb_t)


ATT_T = 512
ATT_STRIP = 64


def _attn_fwd(q, kt, vx, own):
    t, rs = ATT_T, ATT_STRIP
    nown = len(own)
    nq = SEQ // t
    nsteps = (MLA_HEADS // 2) * nq

    def body(q_ref, kt_ref, vx_ref, *rest):
        own_refs, (o_ref, l_ref), gat_refs = rest[:nown], rest[nown:nown + 2], rest[nown + 2:2 * nown + 2]
        s_scr, p_scr, m_scr, a_scr, acc_scr, send_sems, recv_sems, local_sems = rest[2 * nown + 2:]
        qi = pl.program_id(1)
        lane = lax.broadcasted_iota(jnp.int32, (t, LANES), 1)
        m_scr[...] = jnp.full((2, t, LANES), -1e30, F32)
        acc_scr[...] = jnp.zeros((2, t, LANES), F32)

        def block(j, masked):
            off = pl.multiple_of(j * t, t)
            for a in range(2):
                sl = slice(HEAD_PAD * a, HEAD_PAD * (a + 1))
                s_scr[a] = _dot(q_ref[:, sl], kt_ref[sl, pl.ds(off, t)], _NN)
                for r in range(t // rs):
                    rows = slice(rs * r, rs * (r + 1))
                    s = s_scr[a, rows, :]
                    if masked:
                        rowi = lax.broadcasted_iota(jnp.int32, (rs, t), 0) + rs * r
                        coli = lax.broadcasted_iota(jnp.int32, (rs, t), 1)
                        s = jnp.where(coli <= rowi, s, -1e30)
                    m_old = m_scr[a, rows, :]
                    m_new = jnp.maximum(m_old, jnp.max(s, axis=1, keepdims=True))
                    p_scr[a, rows, :] = jnp.exp(s - m_new[:, :1]).astype(BF16)
                    a_scr[a, rows, :] = jnp.exp(m_old - m_new)
                    m_scr[a, rows, :] = m_new
                acc_scr[a] = acc_scr[a] * a_scr[a] + _dot(p_scr[a], vx_ref[pl.ds(off, t), sl], _NN)

        def step(j, carry):
            block(j, False)
            return carry
        lax.fori_loop(0, qi, step, 0)
        block(qi, True)
        res = []
        for a in range(2):
            acc = acc_scr[a]
            l = acc[:, V_HEAD_DIM:V_HEAD_DIM + 1]
            res.append((acc / l, m_scr[a] + jnp.log(l)))
        o_ref[...] = jnp.where(lane < V_HEAD_DIM, res[0][0], pltpu.roll(res[1][0], V_HEAD_DIM, 1))
        l_ref[...] = jnp.where(lane < V_HEAD_DIM, res[0][1], res[1][1])
        _gather_behind(own_refs, gat_refs, send_sems, recv_sems, local_sems, pl.program_id(0) * nq + qi,
                       nsteps - 2, nsteps - 1)

    hbm = pl.BlockSpec(memory_space=pl.ANY)
    res = pl.pallas_call(
        body, name="attn_fwd", grid=(MLA_HEADS // 2, nq),
        in_specs=[pl.BlockSpec((t, 2 * HEAD_PAD), lambda p, i: (i, p)),
                  pl.BlockSpec((2 * HEAD_PAD, SEQ), lambda p, i: (p, 0)),
                  pl.BlockSpec((SEQ, 2 * HEAD_PAD), lambda p, i: (0, p))] + [hbm] * nown,
        out_specs=[pl.BlockSpec((t, LANES), lambda p, i: (i, p)),
                   pl.BlockSpec((t, LANES), lambda p, i: (i, p))] + [hbm] * nown,
        out_shape=[jax.ShapeDtypeStruct((SEQ, MLA_WIDTH), F32), jax.ShapeDtypeStruct((SEQ, MLA_WIDTH), F32)]
        + [jax.ShapeDtypeStruct((N_DEV,) + a.shape, a.dtype) for a in own],
        scratch_shapes=[pltpu.VMEM((2, t, t), F32), pltpu.VMEM((2, t, t), BF16), pltpu.VMEM((2, t, LANES), F32),
                        pltpu.VMEM((2, t, LANES), F32), pltpu.VMEM((2, t, LANES), F32)] + _exchange_sems(nown),
        compiler_params=pltpu.CompilerParams(dimension_semantics=("arbitrary", "arbitrary"), vmem_limit_bytes=VMEM_MID),
    )(q, kt, vx, *own)
    return res[0], res[1], res[2:]


def _exchange_parts(parts, lands, send_sems, recv_sems, local_sems):
    x, y, c = _mesh_pos()
    me = 4 * x + 2 * y + c
    peers = [(x, y, 1 - c), (1 - x, y, c), (x, 1 - y, c), (1 - x, 1 - y, c),
             (1 - x, y, 1 - c), (x, 1 - y, 1 - c), (1 - x, 1 - y, 1 - c)]
    remote, local = [], []
    for a, (part, land) in enumerate(zip(parts, lands)):
        for k, peer in enumerate(peers):
            t = 4 * peer[0] + 2 * peer[1] + peer[2]
            remote.append(_remote(part.at[t], land.at[me], send_sems, recv_sems, 7 * a + k, peer))
        local.append(pltpu.make_async_copy(part.at[me], land.at[me], local_sems.at[a]))
    return remote, local


def _exchange_start(first_step, exchange):
    remote, local = exchange

    @pl.when(first_step)
    def _():
        for cp in remote + local:
            cp.start()


def _exchange_finish(last_step, exchange):
    remote, local = exchange

    @pl.when(last_step)
    def _():
        for cp in remote:
            cp.wait_recv()
        for cp in remote:
            cp.wait_send()
        for cp in local:
            cp.wait()


def _exchange_sems(npart):
    return [pltpu.SemaphoreType.DMA((7 * npart,)), pltpu.SemaphoreType.DMA((7 * npart,)),
            pltpu.SemaphoreType.DMA((npart,))]


def _attn_bwd(q, kt, k, vxt, d_o, o, lse, parts):
    t, rs = ATT_T, ATT_STRIP
    nq = SEQ // t
    npart = len(parts)
    nsteps = MLA_HEADS // 2

    def body(q_ref, kt_ref, k_ref, vxt_ref, do_ref, o_ref, l_ref, *rest):
        part_refs, rest = rest[:npart], rest[npart:]
        dq_ref, dk_ref, dv_ref = rest[:3]
        land_refs, rest = rest[3:3 + npart], rest[3 + npart:]
        s_scr, dp_scr, p_scr, ds_scr, st_scr, send_sems, recv_sems, local_sems = rest
        exchange = _exchange_parts(part_refs, land_refs, send_sems, recv_sems, local_sems)
        _exchange_start(pl.program_id(0) == 0, exchange)
        dk_ref[...] = jnp.zeros_like(dk_ref)
        dv_ref[...] = jnp.zeros_like(dv_ref)
        lane = lax.broadcasted_iota(jnp.int32, (t, LANES), 1)

        def qtile(i, carry):
            ioff = pl.multiple_of(i * t, t)
            do_i = do_ref[pl.ds(ioff, t), :]
            o_i = o_ref[pl.ds(ioff, t), :]
            l_i = l_ref[pl.ds(ioff, t), :]
            for a in range(2):
                sl = slice(HEAD_PAD * a, HEAD_PAD * (a + 1))
                sel = (lane < V_HEAD_DIM) if a == 0 else (lane >= V_HEAD_DIM)
                doa = jnp.where(sel, do_i, 0.0)
                oa = o_i
                if a == 1:
                    doa = pltpu.roll(doa, V_HEAD_DIM, 1)
                    oa = pltpu.roll(o_i, V_HEAD_DIM, 1)
                st_scr[0] = jnp.broadcast_to(jnp.sum(doa * oa, axis=1, keepdims=True), (t, LANES))
                st_scr[1] = jnp.broadcast_to(l_i[:, V_HEAD_DIM * a:V_HEAD_DIM * a + 1], (t, LANES))
                doa_bf = doa.astype(BF16)
                qa = q_ref[pl.ds(ioff, t), sl]

                def block(j, masked, dq_acc, sl=sl, qa=qa, doa_bf=doa_bf):
                    joff = pl.multiple_of(j * t, t)
                    s_scr[...] = _dot(qa, kt_ref[sl, pl.ds(joff, t)], _NN)
                    dp_scr[...] = _dot(doa_bf, vxt_ref[sl, pl.ds(joff, t)], _NN)
                    for r in range(t // rs):
                        rows = slice(rs * r, rs * (r + 1))
                        p = jnp.exp(s_scr[rows, :] - st_scr[1, rows, :1])
                        if masked:
                            rowi = lax.broadcasted_iota(jnp.int32, (rs, t), 0) + rs * r
                            coli = lax.broadcasted_iota(jnp.int32, (rs, t), 1)
                            p = jnp.where(coli <= rowi, p, 0.0)
                        p_scr[rows, :] = p.astype(BF16)
                        ds_scr[rows, :] = (p * (dp_scr[rows, :] - st_scr[0, rows, :1])).astype(BF16)
                    dk_ref[pl.ds(joff, t), sl] += _dot(ds_scr[...], qa, _TN)
                    dv_ref[pl.ds(joff, t), sl] += _dot(p_scr[...], doa_bf, _TN)
                    return dq_acc + _dot(ds_scr[...], k_ref[pl.ds(joff, t), sl], _NN)

                dq_acc = lax.fori_loop(0, i, lambda j, acc: block(j, False, acc), jnp.zeros((t, HEAD_PAD), F32))
                dq_ref[pl.ds(ioff, t), sl] = block(i, True, dq_acc)
            return carry

        lax.fori_loop(0, nq, qtile, 0)
        _exchange_finish(pl.program_id(0) == nsteps - 1, exchange)

    hw = MLA_HEADS * HEAD_PAD
    wide = pl.BlockSpec((SEQ, 2 * HEAD_PAD), lambda p: (0, p))
    wide_t = pl.BlockSpec((2 * HEAD_PAD, SEQ), lambda p: (p, 0))
    narrow = pl.BlockSpec((SEQ, LANES), lambda p: (0, p))
    hbm = pl.BlockSpec(memory_space=pl.ANY)
    res = pl.pallas_call(
        body, name="attn_bwd", grid=(nsteps,),
        in_specs=[wide, wide_t, wide, wide_t, narrow, narrow, narrow] + [hbm] * npart,
        out_specs=[wide, wide, wide] + [hbm] * npart,
        out_shape=[jax.ShapeDtypeStruct((SEQ, hw), F32)] * 3 + [jax.ShapeDtypeStruct(p.shape, p.dtype) for p in parts],
        scratch_shapes=[pltpu.VMEM((t, t), F32), pltpu.VMEM((t, t), F32), pltpu.VMEM((t, t), BF16),
                        pltpu.VMEM((t, t), BF16), pltpu.VMEM((2, t, LANES), F32)] + _exchange_sems(npart),
        compiler_params=pltpu.CompilerParams(dimension_semantics=("arbitrary",), vmem_limit_bytes=VMEM_BIG),
    )(q, kt, k, vxt, d_o, o, lse, *parts)
    return res[0], res[1], res[2], res[3:]


def _sgu_math(u, v, zb, lg, lb, ws_ref, bias):
    ug, dug = _gelu_and_grad(u)
    vg, dvg = _gelu_and_grad(v)
    mu = jnp.mean(vg, axis=1, keepdims=True)
    xc = vg - mu
    rstd = lax.rsqrt(jnp.mean(xc * xc, axis=1, keepdims=True) + LN_EPS)
    xh = xc * rstd
    vn_bf = (xh * lg + lb).astype(BF16)
    grp = lax.broadcasted_iota(jnp.int32, (CHUNK, SGU_WIDTH), 1) // SGU_GROUP_DIM
    r_i = lax.broadcasted_iota(jnp.int32, (CHUNK, CHUNK), 0)
    c_i = lax.broadcasted_iota(jnp.int32, (CHUNK, CHUNK), 1)
    tri, tri_t = r_i >= c_i, r_i <= c_i
    mixed = bias
    for g in range(SGU_GROUPS):
        wt = jnp.where(tri, ws_ref[g], 0.0).astype(BF16)
        mixed = mixed + jnp.where(grp == g, _dot(wt, vn_bf, _NN), 0.0)
    sb = _sigmoid(zb)
    return ug, dug, dvg, rstd, xh, vn_bf, grp, tri, tri_t, mixed, sb


def _sgu_fwd(h_b, lg, lb, w_s, bias_full):
    def body(u_ref, v_ref, zb_ref, lg_ref, lb_ref, ws_ref, bias_ref, yb_ref):
        zb = zb_ref[...]
        ug, _, _, _, _, _, _, _, _, mixed, sb = _sgu_math(u_ref[...], v_ref[...], zb, lg_ref[...], lb_ref[...],
                                                       ws_ref, bias_ref[...])
        yb_ref[...] = (ug * mixed) * (zb * sb)

    blk = lambda c: pl.BlockSpec((CHUNK, SGU_WIDTH), lambda i, c=c: (i, c))
    full2 = lambda shape: pl.BlockSpec(shape, lambda i: (0, 0))
    return pl.pallas_call(
        body, name="sgu_fwd", grid=(SEQ // CHUNK,),
        in_specs=[blk(0), blk(1), blk(2), full2((1, SGU_WIDTH)), full2((1, SGU_WIDTH)),
                  pl.BlockSpec((SGU_GROUPS, CHUNK, CHUNK), lambda i: (0, 0, 0)), full2((CHUNK, SGU_WIDTH))],
        out_specs=pl.BlockSpec((CHUNK, SGU_WIDTH), lambda i: (i, 0)),
        out_shape=jax.ShapeDtypeStruct((SEQ, SGU_WIDTH), F32),
        compiler_params=pltpu.CompilerParams(dimension_semantics=("arbitrary",)),
    )(h_b, h_b, h_b, lg, lb, w_s, bias_full)


def _sgu_bwd(h_b, d_yb, lg, lb, w_s, w_st, bias_full, parts):
    nsteps = SEQ // CHUNK
    npart = len(parts)

    def body(u_ref, v_ref, zb_ref, dyb_ref, lg_ref, lb_ref, ws_ref, wst_ref, bias_ref, *rest):
        part_refs, rest = rest[:npart], rest[npart:]
        dhb_ref, dws_ref, dbs_ref, dlg_ref, dlb_ref, dbb_ref = rest[:6]
        land_refs, (dbias_acc, send_sems, recv_sems, local_sems) = rest[6:6 + npart], rest[6 + npart:]
        step = pl.program_id(0)
        exchange = _exchange_parts(part_refs, land_refs, send_sems, recv_sems, local_sems)
        _exchange_start(step == 0, exchange)

        @pl.when(step == 0)
        def _():
            dbb_ref[...] = jnp.zeros_like(dbb_ref)
            dws_ref[...] = jnp.zeros_like(dws_ref)
            dlg_ref[...] = jnp.zeros_like(dlg_ref)
            dlb_ref[...] = jnp.zeros_like(dlb_ref)
            dbias_acc[...] = jnp.zeros_like(dbias_acc)

        zb = zb_ref[...]
        lg = lg_ref[...]
        ug, dug, dvg, rstd, xh, vn_bf, grp, tri, tri_t, mixed, sb = _sgu_math(
            u_ref[...], v_ref[...], zb, lg, lb_ref[...], ws_ref, bias_ref[...])
        dyb = dyb_ref[...]
        dsgu = dyb * (zb * sb)
        dzb = dyb * (ug * mixed) * (sb * (1.0 + zb * (1.0 - sb)))
        du = dsgu * mixed * dug
        dmixed = dsgu * ug
        dbias_acc[...] += dmixed
        dvn = jnp.zeros((CHUNK, SGU_WIDTH), F32)
        for g in range(SGU_GROUPS):
            dm_g = jnp.where(grp == g, dmixed, 0.0).astype(BF16)
            wtt = jnp.where(tri_t, wst_ref[g], 0.0).astype(BF16)
            dvn = dvn + _dot(wtt, dm_g, _NN)
            dws_ref[g] += jnp.where(tri, _dot(dm_g, vn_bf, _NT), 0.0)
        dlg_ref[...] += jnp.sum(dvn * xh, axis=0, keepdims=True)
        dlb_ref[...] += jnp.sum(dvn, axis=0, keepdims=True)
        dxh = dvn * lg
        dvgel = rstd * (dxh - jnp.mean(dxh, axis=1, keepdims=True) - xh * jnp.mean(dxh * xh, axis=1, keepdims=True))
        _store_grad(dhb_ref, dbb_ref, 0, du)
        _store_grad(dhb_ref, dbb_ref, SGU_WIDTH, dvgel * dvg)
        _store_grad(dhb_ref, dbb_ref, 2 * SGU_WIDTH, dzb)

        @pl.when(step == nsteps - 1)
        def _():
            acc = dbias_acc[...]
            lane = lax.broadcasted_iota(jnp.int32, (CHUNK, LANES), 1)
            out = jnp.zeros((CHUNK, LANES), F32)
            for g in range(SGU_GROUPS):
                sg = jnp.sum(jnp.where(grp == g, acc, 0.0), axis=1, keepdims=True)
                out = jnp.where(lane == g, sg, out)
            dbs_ref[...] = out

        _exchange_finish(step == nsteps - 1, exchange)

    blk = lambda c: pl.BlockSpec((CHUNK, SGU_WIDTH), lambda i, c=c: (i, c))
    full2 = lambda shape: pl.BlockSpec(shape, lambda i: (0, 0))
    full3 = pl.BlockSpec((SGU_GROUPS, CHUNK, CHUNK), lambda i: (0, 0, 0))
    hbm = pl.BlockSpec(memory_space=pl.ANY)
    res = pl.pallas_call(
        body, name="sgu_bwd", grid=(nsteps,),
        in_specs=[blk(0), blk(1), blk(2), pl.BlockSpec((CHUNK, SGU_WIDTH), lambda i: (i, 0)),
                  full2((1, SGU_WIDTH)), full2((1, SGU_WIDTH)), full3, full3, full2((CHUNK, SGU_WIDTH))] + [hbm] * npart,
        out_specs=[pl.BlockSpec((CHUNK, SEG_B), lambda i: (i, 0)), full3, full2((CHUNK, LANES)),
                   full2((1, SGU_WIDTH)), full2((1, SGU_WIDTH)), full2((1, SEG_B))] + [hbm] * npart,
        out_shape=[jax.ShapeDtypeStruct((SEQ, SEG_B), BF16),
                   jax.ShapeDtypeStruct((SGU_GROUPS, CHUNK, CHUNK), F32),
                   jax.ShapeDtypeStruct((CHUNK, LANES), F32),
                   jax.ShapeDtypeStruct((1, SGU_WIDTH), F32), jax.ShapeDtypeStruct((1, SGU_WIDTH), F32),
                   jax.ShapeDtypeStruct((1, SEG_B), F32)] + [jax.ShapeDtypeStruct(p.shape, p.dtype) for p in parts],
        scratch_shapes=[pltpu.VMEM((CHUNK, SGU_WIDTH), F32)] + _exchange_sems(npart),
        compiler_params=pltpu.CompilerParams(dimension_semantics=("arbitrary",)),
    )(h_b, h_b, h_b, d_yb, lg, lb, w_s, w_st, bias_full, *parts)
    return res[:6], res[6:]


def _merge(x, o, h_a, y_b, target, w_oa, w_ob, w_out, ln_g, ln_b):
    tm = 256
    nsteps = SEQ // tm

    def body(x_ref, o_ref, ga_ref, gb_ref, za_ref, yb_ref, tgt_ref, woa_ref, wob_ref, wout_ref, lng_ref, lnb_ref,
             loss_ref, dxr_ref, dha_ref, do_ref, dyb_ref, poa_ref, pob_ref, pout_ref, dlng_ref, dlnb_ref, dba_ref,
             dwoa_ref, dwob_ref, dwout_ref):
        step = pl.program_id(0)

        @pl.when(step == 0)
        def _():
            for r in (loss_ref, dwoa_ref, dwob_ref, dwout_ref, dlng_ref, dlnb_ref, dba_ref):
                r[...] = jnp.zeros_like(r)

        o = o_ref[...]
        za = za_ref[...]
        sa = _sigmoid(za)
        ya_bf = (o * (za * sa)).astype(BF16)
        yb_bf = yb_ref[...].astype(BF16)
        woa, wob, wout = woa_ref[...], wob_ref[...], wout_ref[...]
        pa = _dot(ya_bf, woa, _NN)
        pb = _dot(yb_bf, wob, _NN)
        sga = _sigmoid(ga_ref[...])
        sgb = _sigmoid(gb_ref[...])
        merged_bf = (sga * pa + sgb * pb).astype(BF16)
        r = DN_ALPHA * x_ref[...] + _dot(merged_bf, wout, _NN)
        mu = jnp.mean(r, axis=1, keepdims=True)
        rc = r - mu
        rstd = lax.rsqrt(jnp.mean(rc * rc, axis=1, keepdims=True) + LN_EPS)
        xh = rc * rstd
        lng = lng_ref[...]
        y = xh * lng + lnb_ref[...]
        e = y - tgt_ref[...]
        loss_ref[...] += 0.5 * jnp.sum(jnp.sum(e * e, axis=1, keepdims=True) * (1.0 / D_MODEL), axis=0, keepdims=True)

        dy = e * (1.0 / D_MODEL)
        dlng_ref[...] += jnp.sum(dy * xh, axis=0, keepdims=True)
        dlnb_ref[...] += jnp.sum(dy, axis=0, keepdims=True)
        dxh = dy * lng
        dr = rstd * (dxh - jnp.mean(dxh, axis=1, keepdims=True) - xh * jnp.mean(dxh * xh, axis=1, keepdims=True))
        dxr_ref[...] = DN_ALPHA * dr
        dr_bf = dr.astype(BF16)
        dwout_ref[...] += _dot(merged_bf, dr_bf, _TN)
        dmerged = _dot(dr_bf, wout, _NT)
        dpa_bf = (dmerged * sga).astype(BF16)
        dpb_bf = (dmerged * sgb).astype(BF16)
        _store_grad(dha_ref, dba_ref, 0, dmerged * pa * (sga * (1.0 - sga)))
        _store_grad(dha_ref, dba_ref, D_MODEL, dmerged * pb * (sgb * (1.0 - sgb)))
        dwoa_ref[...] += _dot(ya_bf, dpa_bf, _TN)
        dwob_ref[...] += _dot(yb_bf, dpb_bf, _TN)
        dya = _dot(dpa_bf, woa, _NT)
        dyb_ref[...] = _dot(dpb_bf, wob, _NT)
        do_ref[...] = dya * (za * sa)
        _store_grad(dha_ref, dba_ref, 2 * D_MODEL, dya * o * (sa * (1.0 + za * (1.0 - sa))))

        @pl.when(step == nsteps - 1)
        def _():
            cols = D_MODEL // N_DEV
            for j in range(N_DEV):
                poa_ref[j] = dwoa_ref[:, cols * j:cols * (j + 1)].astype(BF16)
                pob_ref[j] = dwob_ref[:, cols * j:cols * (j + 1)].astype(BF16)
                pout_ref[j] = dwout_ref[cols * j:cols * (j + 1), :].astype(BF16)

    row = lambda w, c=0: pl.BlockSpec((tm, w), lambda i, c=c: (i, c))
    full = lambda shape: pl.BlockSpec(shape, lambda i: (0, 0))
    full3 = lambda shape: pl.BlockSpec(shape, lambda i: (0, 0, 0))
    return pl.pallas_call(
        body, name="merge", grid=(nsteps,),
        in_specs=[row(D_MODEL), row(MLA_WIDTH), row(D_MODEL, 0), row(D_MODEL, 1), row(MLA_WIDTH, 4), row(SGU_WIDTH),
                  row(D_MODEL), full((MLA_WIDTH, D_MODEL)), full((SGU_WIDTH, D_MODEL)), full((D_MODEL, D_MODEL)),
                  full((1, D_MODEL)), full((1, D_MODEL))],
        out_specs=[full((1, LANES)), row(D_MODEL), row(SEG_A), row(MLA_WIDTH), row(SGU_WIDTH),
                   full3((N_DEV, MLA_WIDTH, D_MODEL // N_DEV)), full3((N_DEV, SGU_WIDTH, D_MODEL // N_DEV)),
                   full3((N_DEV, D_MODEL // N_DEV, D_MODEL)), full((1, D_MODEL)), full((1, D_MODEL)), full((1, SEG_A))],
        out_shape=[jax.ShapeDtypeStruct((1, LANES), F32),
                   jax.ShapeDtypeStruct((SEQ, D_MODEL), F32), jax.ShapeDtypeStruct((SEQ, SEG_A), BF16),
                   jax.ShapeDtypeStruct((SEQ, MLA_WIDTH), F32), jax.ShapeDtypeStruct((SEQ, SGU_WIDTH), F32),
                   jax.ShapeDtypeStruct((N_DEV, MLA_WIDTH, D_MODEL // N_DEV), BF16),
                   jax.ShapeDtypeStruct((N_DEV, SGU_WIDTH, D_MODEL // N_DEV), BF16),
                   jax.ShapeDtypeStruct((N_DEV, D_MODEL // N_DEV, D_MODEL), BF16),
                   jax.ShapeDtypeStruct((1, D_MODEL), F32), jax.ShapeDtypeStruct((1, D_MODEL), F32),
                   jax.ShapeDtypeStruct((1, SEG_A), F32)],
        scratch_shapes=[pltpu.VMEM((MLA_WIDTH, D_MODEL), F32), pltpu.VMEM((SGU_WIDTH, D_MODEL), F32),
                        pltpu.VMEM((D_MODEL, D_MODEL), F32)],
        compiler_params=pltpu.CompilerParams(dimension_semantics=("arbitrary",), vmem_limit_bytes=VMEM_BIG),
    )(x, o, h_a, h_a, h_a, y_b, target, w_oa, w_ob, w_out, ln_g, ln_b)


def _mla_bwd(dq, dk, dv, h_c, xt_bf, gq, gkv, wq, wkn, wv, c_t, sa_t, sb_t, parts):
    tm = 256
    hw = MLA_HEADS * HEAD_PAD
    npart = len(parts)
    nsteps = SEQ // tm

    def body(dq_ref, dk_ref, dv_ref, cq_ref, ckv_ref, xt_ref, gq_ref, gkv_ref, wq_ref, wkn_ref, wv_ref, c_ref, sa_ref,
             sb_ref, *rest):
        part_refs, rest = rest[:npart], rest[npart:]
        dhc_ref, puq_ref, dwkn_ref, dwv_ref, dgq_ref, dgkv_ref, dbc_ref, dwc_ref = rest[:8]
        land_refs, (pre_ref, dwq_ref, dwc_acc, send_sems, recv_sems, local_sems) = rest[8:8 + npart], rest[8 + npart:]
        exchange = _exchange_parts(part_refs, land_refs, send_sems, recv_sems, local_sems)
        _exchange_start(pl.program_id(0) == 0, exchange)

        @pl.when(pl.program_id(0) == 0)
        def _():
            for r in (dwq_ref, dwc_acc, dwkn_ref, dwv_ref, dgq_ref, dgkv_ref, dbc_ref):
                r[...] = jnp.zeros_like(r)

        c, sa, sb = c_ref[...], sa_ref[...], sb_ref[...]
        lane = lax.broadcasted_iota(jnp.int32, (tm, LANES), 1)
        rope_lanes = jnp.logical_and(lane >= ROPE_LO, lane < ROPE_HI)

        cq = cq_ref[...]
        gq = gq_ref[...]
        rq = lax.rsqrt(jnp.sum(cq * cq, axis=1, keepdims=True) * (1.0 / Q_LORA_RANK) + RMS_EPS)
        nq = cq * rq
        cqn_bf = (nq * gq).astype(BF16)
        for h in range(MLA_HEADS):
            sl = slice(HEAD_PAD * h, HEAD_PAD * (h + 1))
            pre_ref[:, sl] = _rope_t(dq_ref[:, sl] * ATTN_SCALE, c, sa, sb).astype(BF16)
        dqpre_bf = pre_ref[...]
        dcqn = _dot(dqpre_bf, wq_ref[...], _NT)
        dwq_ref[...] += _dot(cqn_bf, dqpre_bf, _TN)
        dgq_ref[...] += jnp.sum(dcqn * nq, axis=0, keepdims=True)
        dnq = dcqn * gq
        _store_grad(dhc_ref, dbc_ref, 0,
                    rq * (dnq - nq * (jnp.sum(dnq * nq, axis=1, keepdims=True) * (1.0 / Q_LORA_RANK))))

        ckv = ckv_ref[...]
        gkv = gkv_ref[...]
        rkv = lax.rsqrt(jnp.sum(ckv * ckv, axis=1, keepdims=True) * (1.0 / KV_LORA_RANK) + RMS_EPS)
        nkv = ckv * rkv
        ckvn_bf = (nkv * gkv).astype(BF16)
        dk = dk_ref[...]
        dk_bf = dk.astype(BF16)
        dv_bf = dv_ref[...].astype(BF16)
        dckvn = _dot(dk_bf, wkn_ref[...], _NT) + _dot(dv_bf, wv_ref[...], _NT)
        dwkn_ref[...] += _dot(ckvn_bf, dk_bf, _TN)
        dwv_ref[...] += _dot(ckvn_bf, dv_bf, _TN)
        dgkv_ref[...] += jnp.sum(dckvn * nkv, axis=0, keepdims=True)
        dnkv = dckvn * gkv
        _store_grad(dhc_ref, dbc_ref, CQ_PAD, rkv * (
            dnkv - nkv * (jnp.sum(dnkv * nkv, axis=1, keepdims=True) * (1.0 / KV_LORA_RANK))))
        dkpe = jnp.zeros((tm, LANES), F32)
        for h in range(MLA_HEADS):
            dkpe = dkpe + dk[:, HEAD_PAD * h:HEAD_PAD * (h + 1)]
        _store_grad(dhc_ref, dbc_ref, CQ_PAD + LANES, _rope_t(jnp.where(rope_lanes, dkpe, 0.0), c, sa, sb))
        dwc_acc[...] += _dot(xt_ref[...], dhc_ref[...], _NN)

        @pl.when(pl.program_id(0) == SEQ // tm - 1)
        def _():
            dwc_ref[...] = dwc_acc[...].astype(BF16)
            rows = Q_LORA_RANK // N_DEV
            for j in range(N_DEV):
                for h in range(MLA_HEADS):
                    puq_ref[j, :, QK_HEAD_DIM * h:QK_HEAD_DIM * (h + 1)] = dwq_ref[
                        rows * j:rows * (j + 1), HEAD_PAD * h:HEAD_PAD * h + QK_HEAD_DIM].astype(BF16)

        _exchange_finish(pl.program_id(0) == nsteps - 1, exchange)

    full = lambda shape: pl.BlockSpec(shape, lambda i: (0, 0))
    row = lambda w, c=0: pl.BlockSpec((tm, w), lambda i, c=c: (i, c))
    hbm = pl.BlockSpec(memory_space=pl.ANY)
    res = pl.pallas_call(
        body, name="mla_bwd", grid=(nsteps,),
        in_specs=[row(hw), row(hw), row(hw), row(CQ_PAD, 0), row(LANES, CQ_PAD // LANES),
                  pl.BlockSpec((D_MODEL, tm), lambda i: (0, i)),
                  full((1, CQ_PAD)), full((1, KV_LORA_RANK)), full((CQ_PAD, hw)), full((KV_LORA_RANK, hw)),
                  full((KV_LORA_RANK, hw)), row(LANES), row(LANES), row(LANES)] + [hbm] * npart,
        out_specs=[row(SEG_C), pl.BlockSpec((N_DEV, Q_LORA_RANK // N_DEV, MLA_HEADS * QK_HEAD_DIM), lambda i: (0, 0, 0)),
                   full((KV_LORA_RANK, hw)), full((KV_LORA_RANK, hw)),
                   full((1, CQ_PAD)), full((1, KV_LORA_RANK)), full((1, SEG_C)), full((D_MODEL, SEG_C))] + [hbm] * npart,
        out_shape=[jax.ShapeDtypeStruct((SEQ, SEG_C), BF16),
                   jax.ShapeDtypeStruct((N_DEV, Q_LORA_RANK // N_DEV, MLA_HEADS * QK_HEAD_DIM), BF16),
                   jax.ShapeDtypeStruct((KV_LORA_RANK, hw), F32), jax.ShapeDtypeStruct((KV_LORA_RANK, hw), F32),
                   jax.ShapeDtypeStruct((1, CQ_PAD), F32), jax.ShapeDtypeStruct((1, KV_LORA_RANK), F32),
                   jax.ShapeDtypeStruct((1, SEG_C), F32), jax.ShapeDtypeStruct((D_MODEL, SEG_C), BF16)]
        + [jax.ShapeDtypeStruct(p.shape, p.dtype) for p in parts],
        scratch_shapes=[pltpu.VMEM((tm, hw), BF16), pltpu.VMEM((CQ_PAD, hw), F32), pltpu.VMEM((D_MODEL, SEG_C), F32)]
        + _exchange_sems(npart),
        compiler_params=pltpu.CompilerParams(dimension_semantics=("arbitrary",), vmem_limit_bytes=VMEM_MID),
    )(dq, dk, dv, h_c, h_c, xt_bf, gq, gkv, wq, wkn, wv, c_t, sa_t, sb_t, *parts)
    return res[:8], res[8:]


def _adamw_all(ws, gs, ms, vs):
    n = len(ws)
    c1 = 1.0 / (1.0 - ADAM_B1 ** ADAM_STEP)
    c2 = 1.0 / (1.0 - ADAM_B2 ** ADAM_STEP)

    def body(*refs):
        for idx in range(n):
            w, g, m, v = (refs[idx][...], refs[n + idx][...], refs[2 * n + idx][...], refs[3 * n + idx][...])
            m_new = ADAM_B1 * m + (1.0 - ADAM_B1) * g
            v_new = ADAM_B2 * v + (1.0 - ADAM_B2) * (g * g)
            delta = -ADAM_LR * ((m_new * c1) / (jnp.sqrt(v_new * c2) + ADAM_EPS) + ADAM_WD * w)
            refs[4 * n + idx][...] = delta
            refs[5 * n + idx][...] = m_new
            refs[6 * n + idx][...] = v_new

    shapes = [jax.ShapeDtypeStruct(w.shape, F32) for w in ws]
    outs = pl.pallas_call(
        body, name="adamw", out_shape=shapes * 3,
        compiler_params=pltpu.CompilerParams(vmem_limit_bytes=VMEM_BIG),
    )(*ws, *gs, *ms, *vs)
    return outs[:n], outs[n:2 * n], outs[2 * n:]


SHARD_W = IN_WIDTH // N_DEV

_PIECES = [(0, 384, 2, 0), (384, 512, 2, CQ_PAD), (512, 544, 2, CQ_PAD + LANES + ROPE_LO),
           (544, 1056, 0, 2 * D_MODEL), (1056, 1568, 1, 0), (1568, 2080, 1, SGU_WIDTH),
           (2080, 2592, 1, 2 * SGU_WIDTH), (2592, 3616, 0, 0), (3616, 4640, 0, D_MODEL)]


def _column_runs():
    runs = []
    for n0, n1, seg, d0 in _PIECES:
        for j in range(N_DEV):
            lo, hi = max(n0, j * SHARD_W), min(n1, (j + 1) * SHARD_W)
            if lo < hi:
                runs.append((j, lo - j * SHARD_W, hi - j * SHARD_W, seg, d0 + lo - n0))
    return runs


def _mesh_pos():
    return lax.axis_index("x"), lax.axis_index("y"), lax.axis_index("c")


def _remote(src, dst, send_sems, recv_sems, k, to):
    return pltpu.make_async_remote_copy(src_ref=src, dst_ref=dst, send_sem=send_sems.at[k], recv_sem=recv_sems.at[k],
                                        device_id=to, device_id_type=pl.DeviceIdType.MESH)


def _gather_exchange(gats, send_sems, recv_sems, meanwhile=None):
    x, y, c = _mesh_pos()
    me, sibling = (x, y, c), (x, y, 1 - c)
    chips = [(1 - x, y), (x, 1 - y), (1 - x, 1 - y)]

    def copy(a, k, blk, to):
        slab = gats[a].at[4 * blk[0] + 2 * blk[1] + blk[2]]
        return _remote(slab, slab, send_sems, recv_sems, 7 * a + k, to)

    arrays = range(len(gats))
    first = [copy(a, 1 + j, me, (*chip, c)) for j, chip in enumerate(chips) for a in arrays]
    first += [copy(a, 0, me, sibling) for a in arrays]
    for cp in first:
        cp.start()
    if meanwhile is not None:
        meanwhile()
    passed = []
    for j, chip in enumerate(chips):
        for a in arrays:
            copy(a, 1 + j, (*chip, c), me).wait_recv()
            fwd = copy(a, 4 + j, (*chip, c), sibling)
            fwd.start()
            passed.append(fwd)
    for a in arrays:
        copy(a, 0, sibling, me).wait_recv()
    for j, chip in enumerate(chips):
        for a in arrays:
            copy(a, 4 + j, (*chip, 1 - c), me).wait_recv()
    for cp in first + passed:
        cp.wait_send()


def _gather_behind(own, gats, send_sems, recv_sems, local_sems, step, mid, last, phases=("start", "forward", "finish")):
    x, y, c = _mesh_pos()
    me, sibling = (x, y, c), (x, y, 1 - c)
    chips = [(1 - x, y), (x, 1 - y), (1 - x, 1 - y)]
    arrays = range(len(gats))

    def copy(a, k, blk, to, src=None):
        slab = gats[a].at[4 * blk[0] + 2 * blk[1] + blk[2]]
        return _remote(slab if src is None else src, slab, send_sems, recv_sems, 7 * a + k, to)

    first = [copy(a, 1 + j, me, (*chip, c), src=own[a]) for j, chip in enumerate(chips) for a in arrays]
    first += [copy(a, 0, me, sibling, src=own[a]) for a in arrays]
    local = [pltpu.make_async_copy(own[a], gats[a].at[4 * x + 2 * y + c], local_sems.at[a]) for a in arrays]
    passed = [copy(a, 4 + j, (*chip, c), sibling) for j, chip in enumerate(chips) for a in arrays]

    @pl.when(jnp.logical_and(step == 0, "start" in phases))
    def _():
        for cp in first + local:
            cp.start()

    @pl.when(jnp.logical_and(step == mid, "forward" in phases))
    def _():
        for j, chip in enumerate(chips):
            for a in arrays:
                copy(a, 1 + j, (*chip, c), me).wait_recv()
        for cp in passed:
            cp.start()

    @pl.when(jnp.logical_and(step == last, "finish" in phases))
    def _():
        for a in arrays:
            copy(a, 0, sibling, me).wait_recv()
        for j, chip in enumerate(chips):
            for a in arrays:
                copy(a, 4 + j, (*chip, 1 - c), me).wait_recv()
        for cp in first + passed:
            cp.wait_send()
        for cp in local:
            cp.wait()


def _gather_first(w_in, w_uq2, w_oa, w_ob, w_out, x2, pos_col, invf_lane):
    hw = MLA_HEADS * HEAD_PAD
    uq_rows = Q_LORA_RANK // N_DEV
    rows = 256

    def body(win_ref, wuq_ref, woa_ref, wob_ref, wout_ref, x_ref, pos_ref, invf_ref,
             wc_ref, wq_ref, winb_ref, oab_ref, obb_ref, outb_ref, xb_ref, xt_ref, c_ref, sa_ref, sb_ref,
             g_uq, blk0, send_sems, recv_sems):
        def local_work():
            for i in range(SEQ // rows):
                xi = x_ref[rows * i:rows * (i + 1), :]
                xb_ref[rows * i:rows * (i + 1), :] = xi.astype(BF16)
                xt_ref[:, rows * i:rows * (i + 1)] = xi.T.astype(BF16)
            ang = pos_ref[...].astype(F32) * invf_ref[...]
            cs, sn = jnp.cos(ang), jnp.sin(ang)
            lane = lax.broadcasted_iota(jnp.int32, ang.shape, 1)
            c_ref[...] = jnp.where(lane < ROPE_LO, 1.0, jnp.where(lane < ROPE_HI, cs, 0.0))
            sa_ref[...] = jnp.where(jnp.logical_and(lane >= ROPE_LO, lane < ROPE_MID), -sn, 0.0)
            sb_ref[...] = jnp.where(jnp.logical_and(lane >= ROPE_MID, lane < ROPE_HI), sn, 0.0)

        x, y, c = _mesh_pos()
        me = (x, y, c)
        winb_ref[...] = win_ref[0].astype(BF16)
        oab_ref[...] = woa_ref[0].astype(BF16)
        obb_ref[...] = wob_ref[0].astype(BF16)
        outb_ref[...] = wout_ref[0].astype(BF16)
        g_uq[4 * x + 2 * y + c] = wuq_ref[...].astype(BF16)

        chip0 = jnp.logical_and(x == 0, y == 0)
        south = c == 0
        half = D_MODEL // 2
        halves = [blk0.at[pl.ds(0, half)], blk0.at[pl.ds(half, half)]]

        def bcopy(k, to, part=None):
            ref = blk0 if part is None else halves[part]
            return _remote(ref, ref, send_sems, recv_sems, 7 + k, to)

        sends0 = [(0, (0, 0, 1), None), (1, (1, 0, 0), 0), (2, (0, 1, 0), 1), (3, (1, 0, 0), 1), (4, (0, 1, 0), 0)]

        @pl.when(jnp.logical_and(chip0, south))
        def _():
            blk0[...] = winb_ref[...]
            for k, to, part in sends0:
                bcopy(k, to, part).start()

        _gather_exchange([g_uq], send_sems, recv_sems, meanwhile=local_work)

        for (cx, cy), first_k, first_half, second_k in (((1, 0), 1, 0, 3), ((0, 1), 2, 1, 4)):
            @pl.when(jnp.logical_and(jnp.logical_and(x == cx, y == cy), south))
            def _(cx=cx, cy=cy, first_k=first_k, first_half=first_half, second_k=second_k):
                bcopy(first_k, me, first_half).wait_recv()
                onward = bcopy(5 + first_half, (1, 1, 0), first_half)
                onward.start()
                bcopy(second_k, me, 1 - first_half).wait_recv()
                north = bcopy(7, (cx, cy, 1))
                north.start()
                onward.wait_send()
                north.wait_send()

        @pl.when(jnp.logical_and(jnp.logical_and(x == 1, y == 1), south))
        def _():
            bcopy(5, me, 0).wait_recv()
            bcopy(6, me, 1).wait_recv()
            north = bcopy(7, (1, 1, 1))
            north.start()
            north.wait_send()

        @pl.when(jnp.logical_and(chip0, c == 1))
        def _():
            bcopy(0, me).wait_recv()

        @pl.when(jnp.logical_and(jnp.logical_not(chip0), c == 1))
        def _():
            bcopy(7, me).wait_recv()

        @pl.when(jnp.logical_and(chip0, south))
        def _():
            for k, to, part in sends0:
                bcopy(k, to, part).wait_send()

        for j, s0, s1, seg, d0 in _column_runs():
            if seg == 2:
                wc_ref[:, d0:d0 + (s1 - s0)] = blk0[:, s0:s1]
        zeros = lambda r, w: jnp.zeros((r, w), BF16)
        wc_ref[:, Q_LORA_RANK:CQ_PAD] = zeros(D_MODEL, CQ_PAD - Q_LORA_RANK)
        wc_ref[:, CQ_PAD + LANES:CQ_PAD + LANES + ROPE_LO] = zeros(D_MODEL, ROPE_LO)
        wc_ref[:, CQ_PAD + LANES + ROPE_HI:SEG_C] = zeros(D_MODEL, LANES - ROPE_HI)
        wq_ref[Q_LORA_RANK:CQ_PAD, :] = zeros(CQ_PAD - Q_LORA_RANK, hw)
        for h in range(MLA_HEADS):
            wq_ref[0:Q_LORA_RANK, HEAD_PAD * h + QK_HEAD_DIM:HEAD_PAD * (h + 1)] = zeros(Q_LORA_RANK, HEAD_PAD - QK_HEAD_DIM)
        for j in range(N_DEV):
            for h in range(MLA_HEADS):
                wq_ref[uq_rows * j:uq_rows * (j + 1), HEAD_PAD * h:HEAD_PAD * h + QK_HEAD_DIM] = g_uq[
                    j, :, QK_HEAD_DIM * h:QK_HEAD_DIM * (h + 1)]

    vmem = pl.BlockSpec(memory_space=pltpu.VMEM)
    return pl.pallas_call(
        body, name="gather_first",
        out_shape=[jax.ShapeDtypeStruct((D_MODEL, SEG_C), BF16), jax.ShapeDtypeStruct((CQ_PAD, hw), BF16),
                   jax.ShapeDtypeStruct(w_in.shape[1:], BF16), jax.ShapeDtypeStruct(w_oa.shape[1:], BF16),
                   jax.ShapeDtypeStruct(w_ob.shape[1:], BF16), jax.ShapeDtypeStruct(w_out.shape[1:], BF16),
                   jax.ShapeDtypeStruct((SEQ, D_MODEL), BF16), jax.ShapeDtypeStruct((D_MODEL, SEQ), BF16)]
        + [jax.ShapeDtypeStruct((SEQ, LANES), F32)] * 3,
        in_specs=[vmem] * 8, out_specs=[vmem] * 11,
        scratch_shapes=[pltpu.VMEM((N_DEV, uq_rows, MLA_HEADS * QK_HEAD_DIM), BF16), pltpu.VMEM((D_MODEL, SHARD_W), BF16),
                        pltpu.SemaphoreType.DMA((15,)), pltpu.SemaphoreType.DMA((15,))],
        compiler_params=pltpu.CompilerParams(vmem_limit_bytes=VMEM_BIG),
    )(w_in, w_uq2, w_oa, w_ob, w_out, x2, pos_col, invf_lane)


def _assemble_in(g_in):
    def body(g_ref, wa_ref, wb_ref):
        segs = [wa_ref, wb_ref]
        for j, s0, s1, seg, d0 in _column_runs():
            if seg < 2:
                segs[seg][:, d0:d0 + (s1 - s0)] = g_ref[j, :, s0:s1]

    return pl.pallas_call(
        body, name="assemble_in",
        out_shape=[jax.ShapeDtypeStruct((D_MODEL, SEG_A), BF16), jax.ShapeDtypeStruct((D_MODEL, SEG_B), BF16)],
        compiler_params=pltpu.CompilerParams(vmem_limit_bytes=VMEM_MID),
    )(g_in)


def _assemble_out(g_oa, g_ob, g_out):
    cols = D_MODEL // N_DEV

    def body(goa_ref, gob_ref, gout_ref, oa_ref, ob_ref, out_ref):
        for j in range(N_DEV):
            oa_ref[:, cols * j:cols * (j + 1)] = goa_ref[j]
            ob_ref[:, cols * j:cols * (j + 1)] = gob_ref[j]
            out_ref[cols * j:cols * (j + 1), :] = gout_ref[j]

    return pl.pallas_call(
        body, name="assemble_out",
        out_shape=[jax.ShapeDtypeStruct((MLA_WIDTH, D_MODEL), BF16), jax.ShapeDtypeStruct((SGU_WIDTH, D_MODEL), BF16),
                   jax.ShapeDtypeStruct((D_MODEL, D_MODEL), BF16)],
    )(g_oa, g_ob, g_out)


C_NAT = 544


P_IN_SPLIT = 896


def _to_parts(dwa, dwb):
    def body(dwa_ref, dwb_ref, phi_ref, plo_ref):
        phi_ref[0, :, 0:C_NAT] = jnp.zeros((P_IN_SPLIT, C_NAT), BF16)
        plo_ref[0, :, 0:C_NAT] = jnp.zeros((D_MODEL - P_IN_SPLIT, C_NAT), BF16)
        segs = [dwa_ref, dwb_ref]
        for j, s0, s1, seg, d0 in _column_runs():
            if seg < 2:
                phi_ref[j, :, s0:s1] = segs[seg][0:P_IN_SPLIT, d0:d0 + (s1 - s0)]
                plo_ref[j, :, s0:s1] = segs[seg][P_IN_SPLIT:D_MODEL, d0:d0 + (s1 - s0)]

    return pl.pallas_call(
        body, name="to_parts",
        out_shape=[jax.ShapeDtypeStruct((N_DEV, P_IN_SPLIT, SHARD_W), BF16),
                   jax.ShapeDtypeStruct((N_DEV, D_MODEL - P_IN_SPLIT, SHARD_W), BF16)],
        compiler_params=pltpu.CompilerParams(vmem_limit_bytes=VMEM_MID))(dwa, dwb)


def _dx_tail(dhs, ws, dx_res, dwc, p_uq, p_rep):
    ntile, sums_at = 8, 5
    tm = SEQ // ntile
    rep_rows = p_rep.shape[1]
    c_rows = D_MODEL // N_DEV
    spec = [((c_rows, C_NAT), BF16), (p_uq.shape[1:], BF16), ((rep_rows, LANES), F32)]
    n = len(spec)

    nseg = len(dhs)

    def body(*refs):
        dh_refs, w_refs = refs[:nseg], refs[nseg:2 * nseg]
        dxr_ref, dwc_ref, puq_ref, prep_ref, dx_ref, call_ref, guq_ref, repall_ref, pc_ref, c_all, rep_all = refs[
            2 * nseg:2 * nseg + 11]
        rest = refs[2 * nseg + 11:]
        ras, tbs, rbs = rest[0:n], rest[n:2 * n], rest[2 * n:3 * n]
        send_sems, recv_sems, gsend, grecv = rest[3 * n:]
        step = pl.program_id(0)
        x, y, c = _mesh_pos()
        me_idx = 4 * x + 2 * y + c
        me, sibling = (x, y, c), (x, y, 1 - c)
        others = [(1 - x, y), (x, 1 - y), (1 - x, 1 - y)]
        parts = [pc_ref, puq_ref, prep_ref]
        gats = [rep_all, c_all]

        def stage1(chip, a):
            return _remote(parts[a].at[2 * chip + (1 - c)], ras[a].at[chip], send_sems, recv_sems, 7 * a + chip, sibling)

        def stage2(k, a):
            cx, cy = others[k]
            return _remote(tbs[a].at[k], rbs[a].at[k], send_sems, recv_sems, 7 * a + 4 + k, (cx, cy, c))

        def gcopy(a, k, blk, to):
            slab = gats[a].at[4 * blk[0] + 2 * blk[1] + blk[2]]
            return _remote(slab, slab, gsend, grecv, 7 * a + k, to)

        def chip_sum(a, chip):
            return parts[a][2 * chip + c].astype(F32) + ras[a][chip].astype(F32)

        @pl.when(step == 0)
        def _():
            for j, s0, s1, seg, d0 in _column_runs():
                if seg == 2:
                    for r in range(N_DEV):
                        pc_ref[r, :, s0:s1] = dwc_ref[c_rows * r:c_rows * (r + 1), d0:d0 + (s1 - s0)]
            for chip in range(4):
                for a in range(n):
                    stage1(chip, a).start()

        @pl.when(step == 1)
        def _():
            for chip in range(4):
                for a in range(n):
                    stage1(chip, a).wait_recv()
            for k, (cx, cy) in enumerate(others):
                for a in range(n):
                    tbs[a][k] = chip_sum(a, 2 * cx + cy).astype(spec[a][1])
                    stage2(k, a).start()

        @pl.when(step == sums_at)
        def _():
            for k in range(3):
                for a in range(n):
                    stage2(k, a).wait_recv()
            sums = []
            for a in range(n):
                acc = chip_sum(a, 2 * x + y)
                for k in range(3):
                    acc = acc + rbs[a][k].astype(F32)
                sums.append(acc)
            c_all[me_idx] = sums[0].astype(BF16)
            guq_ref[...] = sums[1]
            rep_all[me_idx] = sums[2]
            for a in range(2):
                for j, chip in enumerate(others):
                    gcopy(a, 1 + j, me, (*chip, c)).start()
                gcopy(a, 0, me, sibling).start()

        acc = dxr_ref[...]
        for dh_ref, w_ref in zip(dh_refs, w_refs):
            acc = acc + _dot(dh_ref[...], w_ref[...], _NT)
        dx_ref[...] = acc

        @pl.when(step == ntile - 1)
        def _():
            for j, chip in enumerate(others):
                for a in range(2):
                    gcopy(a, 1 + j, (*chip, c), me).wait_recv()
                    gcopy(a, 4 + j, (*chip, c), sibling).start()
            for a in range(2):
                gcopy(a, 0, sibling, me).wait_recv()
                for j, chip in enumerate(others):
                    gcopy(a, 4 + j, (*chip, 1 - c), me).wait_recv()
            for a in range(2):
                gcopy(a, 0, me, sibling).wait_send()
                for j, chip in enumerate(others):
                    gcopy(a, 1 + j, me, (*chip, c)).wait_send()
                    gcopy(a, 4 + j, (*chip, c), sibling).wait_send()
            for a in range(n):
                for chip in range(4):
                    stage1(chip, a).wait_send()
                for k in range(3):
                    stage2(k, a).wait_send()
            call_ref[...] = c_all[...]
            repall_ref[...] = rep_all[...]

    row = lambda w: pl.BlockSpec((tm, w), lambda i: (i, 0))
    full = lambda shape: pl.BlockSpec(shape, lambda i: (0,) * len(shape))
    scratch = [pltpu.VMEM((N_DEV, c_rows, C_NAT), BF16), pltpu.VMEM((N_DEV, c_rows, C_NAT), BF16),
               pltpu.VMEM((N_DEV, rep_rows, LANES), F32)]
    for lead in (4, 3, 3):
        scratch += [pltpu.VMEM((lead,) + tuple(shape), dt) for shape, dt in spec]
    scratch += [pltpu.SemaphoreType.DMA((7 * n,)), pltpu.SemaphoreType.DMA((7 * n,)),
                pltpu.SemaphoreType.DMA((14,)), pltpu.SemaphoreType.DMA((14,))]
    return pl.pallas_call(
        body, name="dx_tail", grid=(SEQ // tm,),
        in_specs=[row(dh.shape[1]) for dh in dhs] + [full(w.shape) for w in ws]
        + [row(D_MODEL), full(dwc.shape), full(p_uq.shape), full(p_rep.shape)],
        out_specs=[row(D_MODEL), full((N_DEV, c_rows, C_NAT)), full(p_uq.shape[1:]), full((N_DEV, rep_rows, LANES))],
        out_shape=[jax.ShapeDtypeStruct((SEQ, D_MODEL), F32), jax.ShapeDtypeStruct((N_DEV, c_rows, C_NAT), BF16),
                   jax.ShapeDtypeStruct(p_uq.shape[1:], F32), jax.ShapeDtypeStruct((N_DEV, rep_rows, LANES), F32)],
        scratch_shapes=scratch,
        compiler_params=pltpu.CompilerParams(dimension_semantics=("arbitrary",), vmem_limit_bytes=VMEM_BIG),
    )(*dhs, *ws, dx_res, dwc, p_uq, p_rep)


def _sum_landed(landed, c_all):
    c_rows = D_MODEL // N_DEV

    def body(rhi_ref, rlo_ref, roa_ref, rob_ref, rout_ref, call_ref, gin_ref, goa_ref, gob_ref, gout_ref):
        def total(ref, sl):
            acc = ref[0, sl, :].astype(F32)
            for s in range(1, N_DEV):
                acc = acc + ref[s, sl, :].astype(F32)
            return acc

        x, y, c = _mesh_pos()
        dev0 = jnp.where(4 * x + 2 * y + c == 0, 1.0, 0.0)
        for j in range(N_DEV):
            sl = slice(c_rows * j, c_rows * (j + 1))
            below = c_rows * j < P_IN_SPLIT
            tot = total(rhi_ref, sl) if below else total(rlo_ref, slice(c_rows * j - P_IN_SPLIT, c_rows * (j + 1) - P_IN_SPLIT))
            gin_ref[0, sl, C_NAT:SHARD_W] = tot[:, C_NAT:SHARD_W]
            gin_ref[0, sl, 0:C_NAT] = tot[:, 0:C_NAT] + dev0 * call_ref[j].astype(F32)
        goa_ref[0] = total(roa_ref, slice(None))
        gob_ref[0] = total(rob_ref, slice(None))
        gout_ref[0] = total(rout_ref, slice(None))

    return pl.pallas_call(
        body, name="sum_landed",
        out_shape=[jax.ShapeDtypeStruct((1, D_MODEL, SHARD_W), F32)]
        + [jax.ShapeDtypeStruct((1,) + r.shape[1:], F32) for r in landed[2:]],
        compiler_params=pltpu.CompilerParams(vmem_limit_bytes=VMEM_MID),
    )(*landed, c_all)


_O_CQ, _O_CKV, _O_KPE, _O_ZA, _O_U, _O_V, _O_ZB, _O_GA, _O_GB = 0, 384, 512, 544, 1056, 1568, 2080, 2592, 3616


def _to_segments(w):
    z = lambda n: jnp.zeros(w.shape[:-1] + (n,), w.dtype)
    seg_a = jnp.concatenate([w[..., _O_GA:_O_GB], w[..., _O_GB:IN_WIDTH], w[..., _O_ZA:_O_U]], axis=-1)
    seg_b = jnp.concatenate([w[..., _O_U:_O_V], w[..., _O_V:_O_ZB], w[..., _O_ZB:_O_GA]], axis=-1)
    seg_c = jnp.concatenate([w[..., _O_CQ:_O_CKV], z(CQ_PAD - Q_LORA_RANK), w[..., _O_CKV:_O_KPE],
                             z(ROPE_LO), w[..., _O_KPE:_O_ZA], z(LANES - ROPE_HI)], axis=-1)
    return seg_a, seg_b, seg_c


def _from_segments(seg_a, seg_b, seg_c):
    kpe0 = CQ_PAD + LANES + ROPE_LO
    return jnp.concatenate([
        seg_c[..., 0:Q_LORA_RANK], seg_c[..., CQ_PAD:CQ_PAD + LANES], seg_c[..., kpe0:kpe0 + QK_ROPE_DIM],
        seg_a[..., 2 * D_MODEL:SEG_A], seg_b, seg_a[..., 0:2 * D_MODEL]], axis=-1)


def kernel(x, positions, w_in, b_in, g_q, w_uq, g_kv, w_ukv, w_oa, sgu_ln_g, sgu_ln_b, w_s, b_s, w_ob, w_out, ln_g, ln_b, loss_target, m_w_in, m_b_in, m_g_q, m_w_uq, m_g_kv, m_w_ukv, m_w_oa, m_sgu_ln_g, m_sgu_ln_b, m_w_s, m_b_s, m_w_ob, m_w_out, m_ln_g, m_ln_b, v_w_in, v_b_in, v_g_q, v_w_uq, v_g_kv, v_w_ukv, v_w_oa, v_sgu_ln_g, v_sgu_ln_b, v_w_s, v_b_s, v_w_ob, v_w_out, v_ln_g, v_ln_b):
    w_uq2 = w_uq[0].reshape(Q_LORA_RANK // N_DEV, MLA_HEADS * QK_HEAD_DIM)
    inv_freq = ROPE_THETA ** (-jnp.arange(0, QK_ROPE_DIM, 2, dtype=F32) / QK_ROPE_DIM)
    invf_lane = jnp.concatenate([jnp.zeros((ROPE_LO,), F32), inv_freq, inv_freq,
                                 jnp.zeros((LANES - ROPE_HI,), F32)]).reshape(1, LANES)
    first = _gather_first(w_in, w_uq2, w_oa, w_ob, w_out, x[0], positions.reshape(SEQ, 1), invf_lane)
    partials = _local_step(x[0], loss_target[0], first, b_in, g_q, g_kv, w_ukv, sgu_ln_g, sgu_ln_b, w_s, b_s, ln_g, ln_b)
    weights = dict(w_in=w_in, b_in=b_in, g_q=g_q, w_uq=w_uq, g_kv=g_kv, w_ukv=w_ukv, w_oa=w_oa, sgu_ln_g=sgu_ln_g,
                   sgu_ln_b=sgu_ln_b, w_s=w_s, b_s=b_s, w_ob=w_ob, w_out=w_out, ln_g=ln_g, ln_b=ln_b)
    moms = dict(w_in=m_w_in, b_in=m_b_in, g_q=m_g_q, w_uq=m_w_uq, g_kv=m_g_kv, w_ukv=m_w_ukv, w_oa=m_w_oa,
                sgu_ln_g=m_sgu_ln_g, sgu_ln_b=m_sgu_ln_b, w_s=m_w_s, b_s=m_b_s, w_ob=m_w_ob, w_out=m_w_out,
                ln_g=m_ln_g, ln_b=m_ln_b)
    vars_ = dict(w_in=v_w_in, b_in=v_b_in, g_q=v_g_q, w_uq=v_w_uq, g_kv=v_g_kv, w_ukv=v_w_ukv, w_oa=v_w_oa,
                 sgu_ln_g=v_sgu_ln_g, sgu_ln_b=v_sgu_ln_b, w_s=v_w_s, b_s=v_b_s, w_ob=v_w_ob, w_out=v_w_out,
                 ln_g=v_ln_g, ln_b=v_ln_b)
    return _reduce_and_update(partials, weights, moms, vars_)


def _local_step(x2, tgt, first, b_in, g_q, g_kv, w_ukv, sgu_ln_g, sgu_ln_b, w_s, b_s, ln_g, ln_b):
    wc, wq, win_b, oa_b, ob_b, out_b, x_bf, xt_bf, c_t, sa_t, sb_t = first
    ba, bb, bc = _to_segments(b_in)
    w_ukv_bf = w_ukv[0].astype(BF16)
    wkn = jnp.pad(w_ukv_bf[:, :, :QK_NOPE_DIM], ((0, 0), (0, 0), (0, HEAD_PAD - QK_NOPE_DIM))).reshape(KV_LORA_RANK, -1)
    wv = jnp.pad(w_ukv_bf[:, :, QK_NOPE_DIM:], ((0, 0), (0, 0), (0, HEAD_PAD - V_HEAD_DIM))).reshape(KV_LORA_RANK, -1)
    gq = jnp.pad(g_q, ((0, 0), (0, CQ_PAD - Q_LORA_RANK)))
    bias_full = jnp.repeat(b_s[0].T, SGU_GROUP_DIM, axis=1)
    w_s3 = w_s[0]
    w_st3 = jnp.swapaxes(w_s3, 1, 2)

    h_c = _mm(x_bf, wc, bias=bc, tm=512, tn=SEG_C, name="in_proj_c")
    q, k, kt, vx, vxt = _mla_prep(h_c, gq, g_kv, wq, wkn, wv, c_t, sa_t, sb_t)
    o, lse, (g_in,) = _attn_fwd(q, kt, vx, (win_b,))
    wa, wb = _assemble_in(g_in)
    h_a, (g_out,) = _mm(x_bf, wa, bias=ba, own=(out_b,), tm=512, tn=SEG_A // 2, name="in_proj_a")
    h_b, (g_oa, g_ob) = _mm(x_bf, wb, bias=bb, own=(oa_b, ob_b), tm=512, tn=SEG_B // 2, name="in_proj_b")
    y_b = _sgu_fwd(h_b, sgu_ln_g, sgu_ln_b, w_s3, bias_full)
    w_oa_f, w_ob_f, w_out_f = _assemble_out(g_oa, g_ob, g_out)

    (loss_row, dx_res, dh_a, d_o, d_yb, p_oa, p_ob, p_out, d_lng, d_lnb, d_ba) = _merge(
        x2, o, h_a, y_b, tgt, w_oa_f, w_ob_f, w_out_f, ln_g, ln_b)
    (dh_b, d_ws, d_bs_t, d_slg, d_slb, d_bb), (r_out,) = _sgu_bwd(h_b, d_yb, sgu_ln_g, sgu_ln_b, w_s3, w_st3, bias_full,
                                                                 (p_out,))
    d_wa, (r_oa,) = _mm(xt_bf, dh_a, out_dtype=BF16, parts=(p_oa,), tm=512, tn=512, name="dw_in_a")
    d_wb = _mm(xt_bf, dh_b, out_dtype=BF16, tm=512, tn=512, name="dw_in_b")
    p_hi, p_lo = _to_parts(d_wa, d_wb)
    dq, dk, dv, (r_hi,) = _attn_bwd(q, kt, k, vxt, d_o, o, lse, (p_hi,))
    (dh_c, p_uq, d_wkn, d_wv, d_gq, d_gkv, d_bc, d_wc), (r_lo, r_ob) = _mla_bwd(
        dq, dk, dv, h_c, xt_bf, gq, g_kv, wq, wkn, wv, c_t, sa_t, sb_t, (p_lo, p_ob))
    landed = (r_hi, r_lo, r_oa, r_ob, r_out)

    p_b_in = _from_segments(d_ba, d_bb, d_bc)
    p_w_ukv = jnp.concatenate([d_wkn.reshape(KV_LORA_RANK, MLA_HEADS, HEAD_PAD)[:, :, :QK_NOPE_DIM],
                               d_wv.reshape(KV_LORA_RANK, MLA_HEADS, HEAD_PAD)[:, :, :V_HEAD_DIM]], axis=-1)
    p_g_q = d_gq[:, :Q_LORA_RANK]
    p_b_s = d_bs_t[:, :SGU_GROUPS].T
    replicated = [p_b_in, p_g_q, d_gkv, p_w_ukv, d_slg, d_slb, d_ws, p_b_s, d_lng, d_lnb]
    return loss_row, ((dh_a, dh_b, dh_c), (wa, wb, wc), dx_res), landed, d_wc, p_uq, replicated


_NAMES = ["w_in", "b_in", "g_q", "w_uq", "g_kv", "w_ukv", "w_oa", "sgu_ln_g", "sgu_ln_b", "w_s", "b_s", "w_ob",
          "w_out", "ln_g", "ln_b"]
_REPLICATED = ["b_in", "g_q", "g_kv", "w_ukv", "sgu_ln_g", "sgu_ln_b", "w_s", "b_s", "ln_g", "ln_b"]


def _reduce_and_update(partials, weights, moms, vars_):
    loss_row, (dhs, ws, dx_res), landed, d_wc, p_uq, replicated = partials
    def piece(a):
        flat = a.reshape(-1)
        return jnp.pad(flat, (0, -flat.size % PACK_ALIGN))

    rep_flat = jnp.concatenate([piece(a) for a in replicated] + [piece(loss_row[0, :1])])
    rep_flat = jnp.pad(rep_flat, (0, N_DEV * PACK_R_ROWS * LANES - rep_flat.size))
    dx, c_all, g_uq, rep_all = _dx_tail(dhs, ws, dx_res, d_wc, p_uq, rep_flat.reshape(N_DEV, PACK_R_ROWS, LANES))
    g_in, g_oa, g_ob, g_out = _sum_landed(landed, c_all)
    rep_sum = rep_all.reshape(-1)
    grads, pos = dict(w_in=g_in, w_uq=g_uq, w_oa=g_oa, w_ob=g_ob, w_out=g_out), 0
    for nm in _REPLICATED:
        grads[nm] = rep_sum[pos:pos + weights[nm].size]
        pos += weights[nm].size + -weights[nm].size % PACK_ALIGN
    loss = rep_sum[pos]
    grads = {nm: grads[nm].reshape(weights[nm].shape) for nm in _NAMES}
    deltas, new_m, new_v = _adamw_all([weights[nm] for nm in _NAMES], [grads[nm] for nm in _NAMES],
                                      [moms[nm] for nm in _NAMES], [vars_[nm] for nm in _NAMES])
    return (loss, dx.reshape(1, SEQ, D_MODEL), *[grads[nm] for nm in _NAMES], *deltas, *new_m, *new_v)
```
